```python
import math
import jax, jax.numpy as jnp
from jax import lax
import numpy as np

D_MODEL = 1024
BATCH = 8
SEQ = 4096
DEPTH = 1

CTX_LEN = 256
GRID_W = 64
D_ATTN = 512
N_HEADS_A = 8
HEAD_DIM = D_ATTN // N_HEADS_A
D_CONV = D_MODEL - D_ATTN
CONV_WIDTH = 31
NA_ROWS = 8
NA_COLS = 16
Q_BLOCK_COLS = 16
K_BLOCK_COLS = Q_BLOCK_COLS + NA_COLS
D_FF = 2816
FFN_CONV_WIDTH = 3
EPS = 1e-6
ATTN_SCALE = HEAD_DIM ** -0.5
SPLITS = [D_ATTN, 2 * D_ATTN, 3 * D_ATTN, 3 * D_ATTN + D_CONV]

kernel_name = 'hybrid_natten_conformer_dit_layer'


def rms_norm(x, g):
    xf = x.astype(jnp.float32)
    y = xf * lax.rsqrt(jnp.mean(xf * xf, axis=-1, keepdims=True) + EPS)
    return (y * g.astype(jnp.float32)).astype(x.dtype)


def layer_norm(x, g, b):
    xf = x.astype(jnp.float32)
    mu = jnp.mean(xf, axis=-1, keepdims=True)
    xc = xf - mu
    var = jnp.mean(xc * xc, axis=-1, keepdims=True)
    y = xc * lax.rsqrt(var + EPS) * g.astype(jnp.float32) + b.astype(jnp.float32)
    return y.astype(x.dtype)


def modulate(h, shift, scale):
    return h * (1 + scale) + shift


def depthwise_conv(x, w, b):
    y = lax.conv_general_dilated(x, w[:, None, :], window_strides=(1,), padding='SAME',
                                 dimension_numbers=('NWC', 'WIO', 'NWC'),
                                 feature_group_count=x.shape[-1])
    return y + b


def heads(t):
    return t.reshape(t.shape[0], t.shape[1], N_HEADS_A, HEAD_DIM)


def neighbourhood_tables(rows):
    wr = min(NA_ROWS, rows)
    n_blk = GRID_W // Q_BLOCK_COLS
    r = np.arange(rows)
    key_rows = np.clip(r - wr // 2, 0, rows - wr)[:, None] + np.arange(wr)[None, :]
    blk = np.arange(n_blk)
    key_cols = (np.clip(blk * Q_BLOCK_COLS - NA_COLS // 2, 0, GRID_W - K_BLOCK_COLS)[:, None]
                + np.arange(K_BLOCK_COLS)[None, :])
    q_cols = blk[:, None] * Q_BLOCK_COLS + np.arange(Q_BLOCK_COLS)[None, :]
    col_start = np.clip(q_cols - NA_COLS // 2, 0, GRID_W - NA_COLS)
    kc = key_cols[:, None, :]
    valid = (kc >= col_start[..., None]) & (kc < col_start[..., None] + NA_COLS)
    key_tok = key_rows[:, None, :, None] * GRID_W + key_cols[None, :, None, :]
    row_off = key_rows - r[:, None] + (NA_ROWS - 1)
    col_off = np.clip(kc - q_cols[..., None] + (NA_COLS - 1), 0, 2 * NA_COLS - 2)
    shape5 = (rows, n_blk, Q_BLOCK_COLS, wr, K_BLOCK_COLS)
    n_keys = wr * K_BLOCK_COLS
    shape4 = (rows, n_blk, Q_BLOCK_COLS, n_keys)
    valid = np.broadcast_to(valid[None, :, :, None, :], shape5).reshape(shape4)
    row_idx = np.broadcast_to(row_off[:, None, None, :, None], shape5).reshape(shape4)
    col_idx = np.broadcast_to(col_off[None, :, :, None, :], shape5).reshape(shape4)
    return (jnp.asarray(key_tok.reshape(rows, n_blk, n_keys), jnp.int32),
            jnp.asarray(row_idx, jnp.int32), jnp.asarray(col_idx, jnp.int32), jnp.asarray(valid))


def neighbourhood_bias(rpb, row_idx, col_idx, valid):
    b = rpb[:, row_idx, col_idx].astype(jnp.float32)
    b = jnp.where(valid[None], b, -jnp.inf)
    return jnp.transpose(b, (1, 2, 0, 3, 4))


def neighbourhood_attention(q, k, v, k_ctx, v_ctx, key_tok, bias):
    B, S = q.shape[0], q.shape[1]
    rows, n_blk, n_keys = key_tok.shape
    qb = q.reshape(B, rows, n_blk, Q_BLOCK_COLS, N_HEADS_A, HEAD_DIM)
    kg = k[:, key_tok]
    vg = v[:, key_tok]
    s_loc = jnp.einsum('brnqhd,brnkhd->brnhqk', qb, kg, preferred_element_type=jnp.float32) * ATTN_SCALE + bias
    s_ctx = jnp.einsum('brnqhd,bchd->brnhqc', qb, k_ctx, preferred_element_type=jnp.float32) * ATTN_SCALE
    p = jax.nn.softmax(jnp.concatenate([s_loc, s_ctx], axis=-1), axis=-1).astype(v.dtype)
    out = (jnp.einsum('brnhqk,brnkhd->brnqhd', p[..., :n_keys], vg)
           + jnp.einsum('brnhqc,bchd->brnqhd', p[..., n_keys:], v_ctx))
    return out.reshape(B, S, D_ATTN)


def context_attention(q, k, v):
    s = jnp.einsum('bqhd,bkhd->bhqk', q, k, preferred_element_type=jnp.float32) * ATTN_SCALE
    p = jax.nn.softmax(s, axis=-1).astype(v.dtype)
    out = jnp.einsum('bhqk,bkhd->bqhd', p, v)
    return out.reshape(q.shape[0], q.shape[1], D_ATTN)


def conformer_conv(a, g, conv_w, conv_b, ln_g, ln_b):
    u = a * jax.nn.sigmoid(g)
    u = depthwise_conv(u, conv_w, conv_b)
    u = layer_norm(u, ln_g, ln_b)
    return jax.nn.silu(u)


def conv_ffn(h, w_up, ffn_w, ffn_b, w_down):
    u = depthwise_conv(h @ w_up, ffn_w, ffn_b)
    gate, val = jnp.split(u, 2, axis=-1)
    return (jax.nn.silu(gate) * val) @ w_down


def _fwd_setup_inputs(seed: int = 0) -> dict:
    key = jax.random.key(seed)
    ks = jax.random.split(key, 20)
    D = D_MODEL
    n_in = 3 * D_ATTN + 2 * D_CONV

    def nrm(k, shape, scale):
        return jax.random.normal(k, shape, jnp.float32) * scale

    return {
        'x': nrm(ks[0], (BATCH, SEQ, D), 1.0),
        'c': nrm(ks[1], (BATCH, D), 1.0),
        'ctx': nrm(ks[2], (BATCH, CTX_LEN, D), 1.0),
        'c_ctx': nrm(ks[3], (D,), 1.0),
        'w_mod': nrm(ks[4], (DEPTH, D, 6 * D), D ** -0.5),
        'b_mod': nrm(ks[5], (DEPTH, 6 * D), 0.02),
        'g_norm1': 1.0 + nrm(ks[6], (DEPTH, D), 0.05),
        'w_in': nrm(ks[7], (DEPTH, D, n_in), D ** -0.5),
        'rpb': nrm(ks[8], (DEPTH, N_HEADS_A, 2 * NA_ROWS - 1, 2 * NA_COLS - 1), 0.5),
        'conv_w': nrm(ks[9], (DEPTH, CONV_WIDTH, D_CONV), CONV_WIDTH ** -0.5),
        'conv_b': nrm(ks[10], (DEPTH, D_CONV), 0.02),
        'ln_g': 1.0 + nrm(ks[11], (DEPTH, D_CONV), 0.05),
        'ln_b': nrm(ks[12], (DEPTH, D_CONV), 0.02),
        'w_out': nrm(ks[13], (DEPTH, D, D), D ** -0.5),
        'g_norm2': 1.0 + nrm(ks[14], (DEPTH, D), 0.05),
        'w_up': nrm(ks[15], (DEPTH, D, 2 * D_FF), D ** -0.5),
        'ffn_conv_w': nrm(ks[16], (DEPTH, FFN_CONV_WIDTH, 2 * D_FF), FFN_CONV_WIDTH ** -0.5),
        'ffn_conv_b': nrm(ks[17], (DEPTH, 2 * D_FF), 0.02),
        'w_down': nrm(ks[18], (DEPTH, D_FF, D), D_FF ** -0.5),
        'g_final': 1.0 + nrm(ks[19], (D,), 0.05),
    }


def _fwd_reference(x, c, ctx, c_ctx, w_mod, b_mod, g_norm1, w_in, rpb, conv_w, conv_b, ln_g, ln_b,
              w_out, g_norm2, w_up, ffn_conv_w, ffn_conv_b, w_down, g_final):
    S = x.shape[1]
    rows = S // GRID_W
    key_tok, row_idx, col_idx, valid = neighbourhood_tables(rows)
    c_act = jax.nn.silu(c)
    cctx_act = jax.nn.silu(c_ctx)
    for l in range(DEPTH):
        last = l == DEPTH - 1
        mod = (c_act @ w_mod[l] + b_mod[l])[:, None, :]
        sh1, sc1, gt1, sh2, sc2, gt2 = jnp.split(mod, 6, axis=-1)
        mod_c = cctx_act @ w_mod[l] + b_mod[l]
        csh1, csc1, cgt1, csh2, csc2, cgt2 = jnp.split(mod_c, 6, axis=-1)
        bias = neighbourhood_bias(rpb[l], row_idx, col_idx, valid)

        h = modulate(rms_norm(x, g_norm1[l]), sh1, sc1)
        hc = modulate(rms_norm(ctx, g_norm1[l]), csh1, csc1)
        q, k, v, a, g = jnp.split(h @ w_in[l], SPLITS, axis=-1)
        if last:
            k_c, v_c = jnp.split(hc @ w_in[l][:, D_ATTN:3 * D_ATTN], 2, axis=-1)
        else:
            q_c, k_c, v_c, a_c, g_c = jnp.split(hc @ w_in[l], SPLITS, axis=-1)
        k_c, v_c = heads(k_c), heads(v_c)
        y_na = neighbourhood_attention(heads(q), heads(k), heads(v), k_c, v_c, key_tok, bias)
        y_cv = conformer_conv(a, g, conv_w[l], conv_b[l], ln_g[l], ln_b[l])
        x = x + gt1 * (jnp.concatenate([y_na, y_cv], axis=-1) @ w_out[l])

        if not last:
            yc_na = context_attention(heads(q_c), k_c, v_c)
            yc_cv = conformer_conv(a_c, g_c, conv_w[l], conv_b[l], ln_g[l], ln_b[l])
            ctx = ctx + cgt1 * (jnp.concatenate([yc_na, yc_cv], axis=-1) @ w_out[l])
            hc2 = modulate(rms_norm(ctx, g_norm2[l]), csh2, csc2)
            ctx = ctx + cgt2 * conv_ffn(hc2, w_up[l], ffn_conv_w[l], ffn_conv_b[l], w_down[l])

        h2 = modulate(rms_norm(x, g_norm2[l]), sh2, sc2)
        x = x + gt2 * conv_ffn(h2, w_up[l], ffn_conv_w[l], ffn_conv_b[l], w_down[l])
    return rms_norm(x, g_final)


import jax as _jax
import jax.numpy as _jnp

TWIN_FORMAT = 'train_step'
FWD_PARAMS = ['x', 'c', 'ctx', 'c_ctx', 'w_mod', 'b_mod', 'g_norm1', 'w_in', 'rpb', 'conv_w', 'conv_b', 'ln_g', 'ln_b', 'w_out', 'g_norm2', 'w_up', 'ffn_conv_w', 'ffn_conv_b', 'w_down', 'g_final']
TWIN_WEIGHTS = ['c_ctx', 'w_mod', 'b_mod', 'g_norm1', 'w_in', 'rpb', 'conv_w', 'conv_b', 'ln_g', 'ln_b', 'w_out', 'g_norm2', 'w_up', 'ffn_conv_w', 'ffn_conv_b', 'w_down', 'g_final']
TWIN_DIFF_INPUT = 'x'
TWIN_INPUTS = ['x', 'c', 'ctx', 'c_ctx', 'w_mod', 'b_mod', 'g_norm1', 'w_in', 'rpb', 'conv_w', 'conv_b', 'ln_g', 'ln_b', 'w_out', 'g_norm2', 'w_up', 'ffn_conv_w', 'ffn_conv_b', 'w_down', 'g_final', 'loss_target', 'm_c_ctx', 'm_w_mod', 'm_b_mod', 'm_g_norm1', 'm_w_in', 'm_rpb', 'm_conv_w', 'm_conv_b', 'm_ln_g', 'm_ln_b', 'm_w_out', 'm_g_norm2', 'm_w_up', 'm_ffn_conv_w', 'm_ffn_conv_b', 'm_w_down', 'm_g_final', 'v_c_ctx', 'v_w_mod', 'v_b_mod', 'v_g_norm1', 'v_w_in', 'v_rpb', 'v_conv_w', 'v_conv_b', 'v_ln_g', 'v_ln_b', 'v_w_out', 'v_g_norm2', 'v_w_up', 'v_ffn_conv_w', 'v_ffn_conv_b', 'v_w_down', 'v_g_final']
TWIN_OUTPUTS = ['loss', 'grad_x', 'grad_c_ctx', 'grad_w_mod', 'grad_b_mod', 'grad_g_norm1', 'grad_w_in', 'grad_rpb', 'grad_conv_w', 'grad_conv_b', 'grad_ln_g', 'grad_ln_b', 'grad_w_out', 'grad_g_norm2', 'grad_w_up', 'grad_ffn_conv_w', 'grad_ffn_conv_b', 'grad_w_down', 'grad_g_final', 'delta_c_ctx', 'delta_w_mod', 'delta_b_mod', 'delta_g_norm1', 'delta_w_in', 'delta_rpb', 'delta_conv_w', 'delta_conv_b', 'delta_ln_g', 'delta_ln_b', 'delta_w_out', 'delta_g_norm2', 'delta_w_up', 'delta_ffn_conv_w', 'delta_ffn_conv_b', 'delta_w_down', 'delta_g_final', 'new_m_c_ctx', 'new_m_w_mod', 'new_m_b_mod', 'new_m_g_norm1', 'new_m_w_in', 'new_m_rpb', 'new_m_conv_w', 'new_m_conv_b', 'new_m_ln_g', 'new_m_ln_b', 'new_m_w_out', 'new_m_g_norm2', 'new_m_w_up', 'new_m_ffn_conv_w', 'new_m_ffn_conv_b', 'new_m_w_down', 'new_m_g_final', 'new_v_c_ctx', 'new_v_w_mod', 'new_v_b_mod', 'new_v_g_norm1', 'new_v_w_in', 'new_v_rpb', 'new_v_conv_w', 'new_v_conv_b', 'new_v_ln_g', 'new_v_ln_b', 'new_v_w_out', 'new_v_g_norm2', 'new_v_w_up', 'new_v_ffn_conv_w', 'new_v_ffn_conv_b', 'new_v_w_down', 'new_v_g_final']
TWIN_LEAF_KINDS = {'loss': 'loss', 'grad_x': 'grad_x', 'grad_c_ctx': 'grad_w', 'grad_w_mod': 'grad_w', 'grad_b_mod': 'grad_w', 'grad_g_norm1': 'grad_w', 'grad_w_in': 'grad_w', 'grad_rpb': 'grad_w', 'grad_conv_w': 'grad_w', 'grad_conv_b': 'grad_w', 'grad_ln_g': 'grad_w', 'grad_ln_b': 'grad_w', 'grad_w_out': 'grad_w', 'grad_g_norm2': 'grad_w', 'grad_w_up': 'grad_w', 'grad_ffn_conv_w': 'grad_w', 'grad_ffn_conv_b': 'grad_w', 'grad_w_down': 'grad_w', 'grad_g_final': 'grad_w', 'delta_c_ctx': 'delta_w', 'delta_w_mod': 'delta_w', 'delta_b_mod': 'delta_w', 'delta_g_norm1': 'delta_w', 'delta_w_in': 'delta_w', 'delta_rpb': 'delta_w', 'delta_conv_w': 'delta_w', 'delta_conv_b': 'delta_w', 'delta_ln_g': 'delta_w', 'delta_ln_b': 'delta_w', 'delta_w_out': 'delta_w', 'delta_g_norm2': 'delta_w', 'delta_w_up': 'delta_w', 'delta_ffn_conv_w': 'delta_w', 'delta_ffn_conv_b': 'delta_w', 'delta_w_down': 'delta_w', 'delta_g_final': 'delta_w', 'new_m_c_ctx': 'new_m', 'new_m_w_mod': 'new_m', 'new_m_b_mod': 'new_m', 'new_m_g_norm1': 'new_m', 'new_m_w_in': 'new_m', 'new_m_rpb': 'new_m', 'new_m_conv_w': 'new_m', 'new_m_conv_b': 'new_m', 'new_m_ln_g': 'new_m', 'new_m_ln_b': 'new_m', 'new_m_w_out': 'new_m', 'new_m_g_norm2': 'new_m', 'new_m_w_up': 'new_m', 'new_m_ffn_conv_w': 'new_m', 'new_m_ffn_conv_b': 'new_m', 'new_m_w_down': 'new_m', 'new_m_g_final': 'new_m', 'new_v_c_ctx': 'new_v', 'new_v_w_mod': 'new_v', 'new_v_b_mod': 'new_v', 'new_v_g_norm1': 'new_v', 'new_v_w_in': 'new_v', 'new_v_rpb': 'new_v', 'new_v_conv_w': 'new_v', 'new_v_conv_b': 'new_v', 'new_v_ln_g': 'new_v', 'new_v_ln_b': 'new_v', 'new_v_w_out': 'new_v', 'new_v_g_norm2': 'new_v', 'new_v_w_up': 'new_v', 'new_v_ffn_conv_w': 'new_v', 'new_v_ffn_conv_b': 'new_v', 'new_v_w_down': 'new_v', 'new_v_g_final': 'new_v'}


def _forward(args):
    return _fwd_reference(*[args[k] for k in FWD_PARAMS])


def _output_shape():
    def fwd():
        inp = _fwd_setup_inputs(0)
        return _fwd_reference(*[inp[k] for k in FWD_PARAMS])
    out = _jax.eval_shape(fwd)
    return out.shape, out.dtype

N_MICROBATCH = 1
ADAM_LR = 0.001
ADAM_B1 = 0.9
ADAM_B2 = 0.999
ADAM_EPS = 1e-08
ADAM_WD = 0.01
ADAM_STEP = 10
PER_EXAMPLE_BATCH_AXIS = {'x': 0, 'c': 0, 'ctx': 0, 'loss_target': 0}
SHARED_INPUTS = []
_WEIGHT_DTYPES = {'c_ctx': _jnp.float32, 'w_mod': _jnp.float32, 'b_mod': _jnp.float32, 'g_norm1': _jnp.float32, 'w_in': _jnp.float32, 'rpb': _jnp.float32, 'conv_w': _jnp.float32, 'conv_b': _jnp.float32, 'ln_g': _jnp.float32, 'ln_b': _jnp.float32, 'w_out': _jnp.float32, 'g_norm2': _jnp.float32, 'w_up': _jnp.float32, 'ffn_conv_w': _jnp.float32, 'ffn_conv_b': _jnp.float32, 'w_down': _jnp.float32, 'g_final': _jnp.float32}
MOMENT_SCALE = {'c_ctx': 5.207294e-02, 'w_mod': 1.345634e-01, 'b_mod': 2.843859e-01, 'g_norm1': 6.042269e-02, 'w_in': 6.357515e-02, 'rpb': 5.374026e-03, 'conv_w': 7.939924e-02, 'conv_b': 1.856649e-01, 'ln_g': 1.223973e-01, 'ln_b': 1.378871e-01, 'w_out': 9.495841e-02, 'g_norm2': 1.222245e-01, 'w_up': 6.376893e-02, 'ffn_conv_w': 6.664613e-02, 'ffn_conv_b': 5.826470e-02, 'w_down': 1.080668e-01, 'g_final': 3.241439e+01}


def _to_microbatches(a, axis):
    t = _jnp.moveaxis(a, axis, 0)
    t = t.reshape((N_MICROBATCH, t.shape[0] // N_MICROBATCH) + t.shape[1:])
    return _jnp.moveaxis(t, 1, axis + 1)


def setup_inputs(seed: int = 0) -> dict:
    inp = _fwd_setup_inputs(seed)
    key = _jax.random.fold_in(_jax.random.key(seed), 7919)
    shape, _ = _output_shape()
    out = dict(inp)
    out["loss_target"] = _jax.random.normal(_jax.random.fold_in(key, 0), shape, _jnp.float32)
    for i, name in enumerate(TWIN_WEIGHTS):
        w = inp[name].astype(_jnp.float32)
        if MOMENT_SCALE is None:
            s = _jnp.sqrt(_jnp.mean(_jnp.square(w)) + 1e-30)
        else:
            s = MOMENT_SCALE[name]
        km, kv = _jax.random.split(_jax.random.fold_in(key, i + 1))
        out[name] = w
        out["m_" + name] = s * _jax.random.normal(km, w.shape, _jnp.float32)
        out["v_" + name] = (s * s) * _jax.random.uniform(kv, w.shape, _jnp.float32, 0.5, 1.5)
    if N_MICROBATCH > 1:
        for name, axis in PER_EXAMPLE_BATCH_AXIS.items():
            out[name] = _to_microbatches(out[name], axis)
    return {'x': out['x'], 'c': out['c'], 'ctx': out['ctx'], 'c_ctx': out['c_ctx'], 'w_mod': out['w_mod'], 'b_mod': out['b_mod'], 'g_norm1': out['g_norm1'], 'w_in': out['w_in'], 'rpb': out['rpb'], 'conv_w': out['conv_w'], 'conv_b': out['conv_b'], 'ln_g': out['ln_g'], 'ln_b': out['ln_b'], 'w_out': out['w_out'], 'g_norm2': out['g_norm2'], 'w_up': out['w_up'], 'ffn_conv_w': out['ffn_conv_w'], 'ffn_conv_b': out['ffn_conv_b'], 'w_down': out['w_down'], 'g_final': out['g_final'], 'loss_target': out['loss_target'], 'm_c_ctx': out['m_c_ctx'], 'm_w_mod': out['m_w_mod'], 'm_b_mod': out['m_b_mod'], 'm_g_norm1': out['m_g_norm1'], 'm_w_in': out['m_w_in'], 'm_rpb': out['m_rpb'], 'm_conv_w': out['m_conv_w'], 'm_conv_b': out['m_conv_b'], 'm_ln_g': out['m_ln_g'], 'm_ln_b': out['m_ln_b'], 'm_w_out': out['m_w_out'], 'm_g_norm2': out['m_g_norm2'], 'm_w_up': out['m_w_up'], 'm_ffn_conv_w': out['m_ffn_conv_w'], 'm_ffn_conv_b': out['m_ffn_conv_b'], 'm_w_down': out['m_w_down'], 'm_g_final': out['m_g_final'], 'v_c_ctx': out['v_c_ctx'], 'v_w_mod': out['v_w_mod'], 'v_b_mod': out['v_b_mod'], 'v_g_norm1': out['v_g_norm1'], 'v_w_in': out['v_w_in'], 'v_rpb': out['v_rpb'], 'v_conv_w': out['v_conv_w'], 'v_conv_b': out['v_conv_b'], 'v_ln_g': out['v_ln_g'], 'v_ln_b': out['v_ln_b'], 'v_w_out': out['v_w_out'], 'v_g_norm2': out['v_g_norm2'], 'v_w_up': out['v_w_up'], 'v_ffn_conv_w': out['v_ffn_conv_w'], 'v_ffn_conv_b': out['v_ffn_conv_b'], 'v_w_down': out['v_w_down'], 'v_g_final': out['v_g_final']}


def _loss(weights, diff, rest, loss_target):
    with _jax.named_scope("forward"):
        args = {**rest, TWIN_DIFF_INPUT: diff, **{k: w.astype(_WEIGHT_DTYPES[k]) for k, w in weights.items()}}
        y = _forward(args)
    with _jax.named_scope("loss_head"):
        err = _jnp.square(y.astype(_jnp.float32) - loss_target)
        return 0.5 * _jnp.sum(_jnp.mean(err, axis=-1)) if err.ndim else 0.5 * err


def _adamw(w, g, m, v):
    m = ADAM_B1 * m + (1.0 - ADAM_B1) * g
    v = ADAM_B2 * v + (1.0 - ADAM_B2) * _jnp.square(g)
    m_hat = m / (1.0 - ADAM_B1 ** ADAM_STEP)
    v_hat = v / (1.0 - ADAM_B2 ** ADAM_STEP)
    delta = -ADAM_LR * (m_hat / (_jnp.sqrt(v_hat) + ADAM_EPS) + ADAM_WD * w)
    return delta, m, v


def reference(x, c, ctx, c_ctx, w_mod, b_mod, g_norm1, w_in, rpb, conv_w, conv_b, ln_g, ln_b, w_out, g_norm2, w_up, ffn_conv_w, ffn_conv_b, w_down, g_final, loss_target, m_c_ctx, m_w_mod, m_b_mod, m_g_norm1, m_w_in, m_rpb, m_conv_w, m_conv_b, m_ln_g, m_ln_b, m_w_out, m_g_norm2, m_w_up, m_ffn_conv_w, m_ffn_conv_b, m_w_down, m_g_final, v_c_ctx, v_w_mod, v_b_mod, v_g_norm1, v_w_in, v_rpb, v_conv_w, v_conv_b, v_ln_g, v_ln_b, v_w_out, v_g_norm2, v_w_up, v_ffn_conv_w, v_ffn_conv_b, v_w_down, v_g_final):
    given = dict(x=x, c=c, ctx=ctx, c_ctx=c_ctx, w_mod=w_mod, b_mod=b_mod, g_norm1=g_norm1, w_in=w_in, rpb=rpb, conv_w=conv_w, conv_b=conv_b, ln_g=ln_g, ln_b=ln_b, w_out=w_out, g_norm2=g_norm2, w_up=w_up, ffn_conv_w=ffn_conv_w, ffn_conv_b=ffn_conv_b, w_down=w_down, g_final=g_final, loss_target=loss_target, m_c_ctx=m_c_ctx, m_w_mod=m_w_mod, m_b_mod=m_b_mod, m_g_norm1=m_g_norm1, m_w_in=m_w_in, m_rpb=m_rpb, m_conv_w=m_conv_w, m_conv_b=m_conv_b, m_ln_g=m_ln_g, m_ln_b=m_ln_b, m_w_out=m_w_out, m_g_norm2=m_g_norm2, m_w_up=m_w_up, m_ffn_conv_w=m_ffn_conv_w, m_ffn_conv_b=m_ffn_conv_b, m_w_down=m_w_down, m_g_final=m_g_final, v_c_ctx=v_c_ctx, v_w_mod=v_w_mod, v_b_mod=v_b_mod, v_g_norm1=v_g_norm1, v_w_in=v_w_in, v_rpb=v_rpb, v_conv_w=v_conv_w, v_conv_b=v_conv_b, v_ln_g=v_ln_g, v_ln_b=v_ln_b, v_w_out=v_w_out, v_g_norm2=v_g_norm2, v_w_up=v_w_up, v_ffn_conv_w=v_ffn_conv_w, v_ffn_conv_b=v_ffn_conv_b, v_w_down=v_w_down, v_g_final=v_g_final)
    weights = {n: given[n] for n in TWIN_WEIGHTS}
    shared = {n: given[n] for n in SHARED_INPUTS}
    per_example = {n: given[n] for n in ['x', 'c', 'ctx']}
    grad_fn = _jax.value_and_grad(_loss, argnums=(0, 1))

    def one_microbatch(ex, loss_target):
        ex = dict(ex)
        diff = ex.pop(TWIN_DIFF_INPUT)
        return grad_fn(weights, diff, {**shared, **ex}, loss_target)

    if N_MICROBATCH == 1:
        loss, (grad_w, grad_x) = one_microbatch(per_example, given["loss_target"])
    else:
        def body(carry, xs):
            loss_sum, grad_sum = carry
            l_k, (gw_k, gx_k) = one_microbatch(xs[0], xs[1])
            with _jax.named_scope("update"):
                return (loss_sum + l_k, _jax.tree.map(_jnp.add, grad_sum, gw_k)), gx_k

        init = (_jnp.zeros((), _jnp.float32), _jax.tree.map(_jnp.zeros_like, weights))
        (loss, grad_w), grad_x = _jax.lax.scan(body, init, (per_example, given["loss_target"]))
    with _jax.named_scope("update"):
        delta_w, new_m, new_v = {}, {}, {}
        for n in TWIN_WEIGHTS:
            delta_w[n], new_m[n], new_v[n] = _adamw(weights[n], grad_w[n], given["m_" + n], given["v_" + n])
    return (loss, grad_x, *[grad_w[n] for n in TWIN_WEIGHTS], *[delta_w[n] for n in TWIN_WEIGHTS],
            *[new_m[n] for n in TWIN_WEIGHTS], *[new_v[n] for n in TWIN_WEIGHTS])
```

```python
import functools

import jax
import jax.numpy as jnp
import numpy as np
from jax import lax
from jax.experimental import pallas as pl
from jax.experimental.pallas import tpu as pltpu

F32 = jnp.float32
BF16 = jnp.bfloat16
MXU_DTYPE = jnp.bfloat16

D = 1024
CTX = 256
GW = 64
DA = 512
NH = 8
HD = 64
DC = 512
CW = 31
DFF = 2816
NIN = 3 * DA + 2 * DC
EPS = 1e-6
SCALE = HD ** -0.5
NEG = -1e30
NA_ROWS = 8
PAIR_ROWS = NA_ROWS + 1
TAB_BLOCKS = 17
LANES = 128
VMEM_LIMIT = 56 * 1024 * 1024

ADAM_LR = 0.001
ADAM_B1 = 0.9
ADAM_B2 = 0.999
ADAM_EPS = 1e-08
ADAM_WD = 0.01
ADAM_STEP = 10

MESH = pl.DeviceIdType.MESH


def _pallas(body, *, name, semantics=None, vmem=VMEM_LIMIT, **kw):
    params = dict(vmem_limit_bytes=vmem)
    if semantics is not None:
        params["dimension_semantics"] = semantics
    return pl.pallas_call(body, name=name, compiler_params=pltpu.CompilerParams(**params), **kw)


def _sds(shape, dtype):
    return jax.ShapeDtypeStruct(shape, dtype)


def _vec_spec(n):
    return pl.BlockSpec((1, n), lambda *_: (0, 0))


def _colsum8(x):
    t, n = x.shape
    return jnp.sum(x.reshape(t // 8, 8, n), axis=0)


def _sigmoid(x):
    return 1.0 / (1.0 + jnp.exp(-x))


def _pieces(arrs, tile):
    lo, out = 0, []
    for a in arrs:
        nt = a.shape[1] // tile
        assert nt * tile == a.shape[1], (a.shape, tile)
        out.append((lo, nt))
        lo += nt
    return out


def _mm(a, b, *, mode, m, n, k, tm, tn, tk, out_dtype, name, a_off=(0, 0), b_off=(0, 0)):
    a_list = list(a) if isinstance(a, (list, tuple)) else [a]
    b_list = list(b) if isinstance(b, (list, tuple)) else [b]
    assert m % tm == 0 and n % tn == 0 and k % tk == 0, (name, m, n, k, tm, tn, tk)
    gi, gj, nk = m // tm, n // tn, k // tk
    a_tile = tm if mode == "tn" else tk
    a_pc = _pieces(a_list, a_tile) if len(a_list) > 1 else [(0, 1 << 30)]
    if mode == "nt":
        assert len(b_list) == 1
    b_pc = _pieces(b_list, tn) if len(b_list) > 1 else [(0, 1 << 30)]
    dims = {"nn": (((1,), (0,)), ((), ())), "nt": (((1,), (1,)), ((), ())), "tn": (((0,), (0,)), ((), ()))}[mode]

    def a_spec(lo, cnt):
        def loc(idx):
            return idx + a_off[1] if len(a_list) == 1 else jnp.clip(idx - lo, 0, cnt - 1)
        if mode == "tn":
            return pl.BlockSpec((tk, tm), lambda i, j, kk: (kk + a_off[0], loc(i)))
        return pl.BlockSpec((tm, tk), lambda i, j, kk: (i + a_off[0], loc(kk)))

    def b_spec(lo, cnt):
        def loc(idx):
            return idx + b_off[1] if len(b_list) == 1 else jnp.clip(idx - lo, 0, cnt - 1)
        if mode == "nt":
            return pl.BlockSpec((tn, tk), lambda i, j, kk: (j + b_off[0], kk + b_off[1]))
        return pl.BlockSpec((tk, tn), lambda i, j, kk: (kk + b_off[0], loc(j)))

    na, nb = len(a_list), len(b_list)

    def body(*refs):
        a_refs, b_refs, o_ref = refs[:na], refs[na:na + nb], refs[na + nb]
        acc = refs[na + nb + 1] if nk > 1 else None
        i, j, kk = pl.program_id(0), pl.program_id(1), pl.program_id(2)
        a_idx = i if mode == "tn" else kk

        def step(ar, br):
            p = lax.dot_general(ar[...].astype(MXU_DTYPE), br[...].astype(MXU_DTYPE), dims,
                                preferred_element_type=F32)
            if nk == 1:
                o_ref[...] = p.astype(out_dtype)
                return

            @pl.when(kk == 0)
            def _():
                acc[...] = p

            @pl.when(kk > 0)
            def _():
                acc[...] += p

            @pl.when(kk == nk - 1)
            def _():
                o_ref[...] = acc[...].astype(out_dtype)

        for pa, (alo, acnt) in enumerate(a_pc):
            for pb, (blo, bcnt) in enumerate(b_pc):
                if na == 1 and nb == 1:
                    step(a_refs[0], b_refs[0])
                else:
                    cond = (a_idx >= alo) & (a_idx < alo + acnt) & (j >= blo) & (j < blo + bcnt)
                    pl.when(cond)(functools.partial(step, a_refs[pa], b_refs[pb]))

    return _pallas(
        body, name=name, grid=(gi, gj, nk),
        in_specs=[a_spec(*p) for p in a_pc] + [b_spec(*p) for p in b_pc],
        out_specs=pl.BlockSpec((tm, tn), lambda i, j, kk: (i, j)),
        out_shape=_sds((m, n), out_dtype),
        scratch_shapes=[pltpu.VMEM((tm, tn), F32)] if nk > 1 else [],
        semantics=("parallel", "parallel", "arbitrary"),
    )(*a_list, *b_list)


ROW_TILE = 256


def _rmsmod_fwd(x, ctx, g, sc, sh, csc, csh):
    s = x.shape[0]
    nt = s // ROW_TILE
    assert ctx.shape[0] == ROW_TILE

    def body(x_ref, c_ref, g_ref, sc_ref, sh_ref, csc_ref, csh_ref, o_ref):
        is_ctx = pl.program_id(0) == nt
        xv = jnp.where(is_ctx, c_ref[...], x_ref[...])
        scv = jnp.where(is_ctx, csc_ref[...], sc_ref[...])
        shv = jnp.where(is_ctx, csh_ref[...], sh_ref[...])
        r = lax.rsqrt(jnp.mean(xv * xv, axis=-1, keepdims=True) + EPS)
        y = xv * r * g_ref[...]
        o_ref[...] = (y * (1.0 + scv) + shv).astype(o_ref.dtype)

    return _pallas(
        body, name="rmsmod1_fwd", grid=(nt + 1,),
        in_specs=[pl.BlockSpec((ROW_TILE, D), lambda i: (jnp.minimum(i, nt - 1), 0)),
                  pl.BlockSpec((ROW_TILE, D), lambda i: (0, 0))] + [_vec_spec(D)] * 5,
        out_specs=pl.BlockSpec((ROW_TILE, D), lambda i: (i, 0)),
        out_shape=_sds((s + CTX, D), MXU_DTYPE),
        semantics=("arbitrary",),
    )(x, ctx, g, sc, sh, csc, csh)


def _resid_rmsmod_fwd(x, y, gt, g, sc, sh):
    s = x.shape[0]

    def body(x_ref, y_ref, gt_ref, g_ref, sc_ref, sh_ref, x1_ref, h_ref):
        x1 = x_ref[...] + gt_ref[...] * y_ref[...]
        x1_ref[...] = x1
        r = lax.rsqrt(jnp.mean(x1 * x1, axis=-1, keepdims=True) + EPS)
        h_ref[...] = ((x1 * r * g_ref[...]) * (1.0 + sc_ref[...]) + sh_ref[...]).astype(h_ref.dtype)

    row = pl.BlockSpec((ROW_TILE, D), lambda i: (i, 0))
    return _pallas(
        body, name="resid_rmsmod2_fwd", grid=(s // ROW_TILE,),
        in_specs=[row, row] + [_vec_spec(D)] * 4,
        out_specs=[row, row],
        out_shape=[_sds((s, D), F32), _sds((s, D), MXU_DTYPE)],
        semantics=("parallel",),
    )(x, y, gt, g, sc, sh)


def _final_fwd_bwd(x1, z, gt2, gf, tgt):
    s = x1.shape[0]
    nt = s // ROW_TILE

    def body(x1_ref, z_ref, gt_ref, gf_ref, t_ref, dx2_ref, dz_ref, loss_ref, dgt_ref, dgf_ref, a_loss, a_gt, a_gf):
        i = pl.program_id(0)

        @pl.when(i == 0)
        def _():
            a_loss[...] = jnp.zeros_like(a_loss)
            a_gt[...] = jnp.zeros_like(a_gt)
            a_gf[...] = jnp.zeros_like(a_gf)

        zv = z_ref[...]
        gt = gt_ref[...]
        gf_ = gf_ref[...]
        x2 = x1_ref[...] + gt * zv
        r = lax.rsqrt(jnp.mean(x2 * x2, axis=-1, keepdims=True) + EPS)
        xn = x2 * r
        e = xn * gf_ - t_ref[...]
        a_loss[...] += _colsum8(e * e)
        dyo = e * (1.0 / D)
        a_gf[...] += _colsum8(dyo * xn)
        gdy = gf_ * dyo
        dx2 = r * gdy - xn * (r * r) * jnp.mean(x2 * gdy, axis=-1, keepdims=True)
        dx2_ref[...] = dx2
        dz_ref[...] = (gt * dx2).astype(dz_ref.dtype)
        a_gt[...] += _colsum8(dx2 * zv)

        @pl.when(i == nt - 1)
        def _():
            tot = jnp.sum(jnp.sum(a_loss[...], axis=0, keepdims=True), axis=1, keepdims=True) * (0.5 / D)
            loss_ref[...] = jnp.broadcast_to(tot, loss_ref.shape)
            dgt_ref[...] = jnp.sum(a_gt[...], axis=0, keepdims=True)
            dgf_ref[...] = jnp.sum(a_gf[...], axis=0, keepdims=True)

    row = pl.BlockSpec((ROW_TILE, D), lambda i: (i, 0))
    return _pallas(
        body, name="final_norm_loss", grid=(nt,),
        in_specs=[row, row, _vec_spec(D), _vec_spec(D), row],
        out_specs=[row, row, _vec_spec(LANES), _vec_spec(D), _vec_spec(D)],
        out_shape=[_sds((s, D), F32), _sds((s, D), MXU_DTYPE), _sds((1, LANES), F32), _sds((1, D), F32), _sds((1, D), F32)],
        scratch_shapes=[pltpu.VMEM((8, D), F32)] * 3,
        semantics=("arbitrary",),
    )(x1, z, gt2, gf, tgt)


def _rmsmod_bwd(xin, dh, g, sc, *, name, dh_row_off=0, add=None, resid=None):
    s = xin.shape[0]
    nt = s // ROW_TILE
    want_dx = add is not None
    assert resid is None or want_dx

    def body(*refs):
        it = iter(refs)
        x_ref, dh_ref, g_ref, sc_ref = next(it), next(it), next(it), next(it)
        add_ref = next(it) if want_dx else None
        gt_ref, y_ref = (next(it), next(it)) if resid is not None else (None, None)
        dsh_ref, dsc_ref, dg_ref = next(it), next(it), next(it)
        dx_ref = next(it) if want_dx else None
        dy_ref, dgt_ref = (next(it), next(it)) if resid is not None else (None, None)
        a_sh, a_sc, a_g = next(it), next(it), next(it)
        a_gt = next(it) if resid is not None else None
        i = pl.program_id(0)

        @pl.when(i == 0)
        def _():
            a_sh[...] = jnp.zeros_like(a_sh)
            a_sc[...] = jnp.zeros_like(a_sc)
            a_g[...] = jnp.zeros_like(a_g)
            if a_gt is not None:
                a_gt[...] = jnp.zeros_like(a_gt)

        xv = x_ref[...]
        dhv = dh_ref[...]
        gv = g_ref[...]
        r = lax.rsqrt(jnp.mean(xv * xv, axis=-1, keepdims=True) + EPS)
        xn = xv * r
        a_sh[...] += _colsum8(dhv)
        a_sc[...] += _colsum8(dhv * (xn * gv))
        dn = dhv * (1.0 + sc_ref[...])
        a_g[...] += _colsum8(dn * xn)
        if want_dx:
            gdn = gv * dn
            dx = add_ref[...] + r * gdn - xn * (r * r) * jnp.mean(xv * gdn, axis=-1, keepdims=True)
            dx_ref[...] = dx
            if resid is not None:
                dy_ref[...] = (gt_ref[...] * dx).astype(dy_ref.dtype)
                a_gt[...] += _colsum8(dx * y_ref[...])

        @pl.when(i == nt - 1)
        def _():
            dsh_ref[...] = jnp.sum(a_sh[...], axis=0, keepdims=True)
            dsc_ref[...] = jnp.sum(a_sc[...], axis=0, keepdims=True)
            dg_ref[...] = jnp.sum(a_g[...], axis=0, keepdims=True)
            if a_gt is not None:
                dgt_ref[...] = jnp.sum(a_gt[...], axis=0, keepdims=True)

    row = pl.BlockSpec((ROW_TILE, D), lambda i: (i, 0))
    ins = [xin, dh, g, sc]
    in_specs = [row, pl.BlockSpec((ROW_TILE, D), lambda i: (i + dh_row_off, 0)), _vec_spec(D), _vec_spec(D)]
    out_specs = [_vec_spec(D)] * 3
    out_shape = [_sds((1, D), F32)] * 3
    scratch = [pltpu.VMEM((8, D), F32)] * 3
    if want_dx:
        ins.append(add)
        in_specs.append(row)
        out_specs.append(row)
        out_shape.append(_sds((s, D), F32))
    if resid is not None:
        ins += [resid[0], resid[1]]
        in_specs += [_vec_spec(D), row]
        out_specs += [row, _vec_spec(D)]
        out_shape += [_sds((s, D), MXU_DTYPE), _sds((1, D), F32)]
        scratch.append(pltpu.VMEM((8, D), F32))
    return _pallas(body, name=name, grid=(nt,), in_specs=in_specs, out_specs=out_specs, out_shape=out_shape,
                   scratch_shapes=scratch, semantics=("arbitrary",))(*ins)


FF_TILE = 128
FF_CHUNK = 128
HALO = 8


def _shift3(pad_ref, r0, ch):
    win = pad_ref[pl.ds(r0, ch + 2 * HALO), :]
    prev = pltpu.roll(win, 1, 0)[HALO:HALO + ch]
    cur = win[HALO:HALO + ch]
    nxt = pltpu.roll(win, ch + 2 * HALO - 1, 0)[HALO:HALO + ch]
    return prev, cur, nxt


def _fill_padded(pad_ref, src_ref, s, ch, halo):
    zeros = jnp.zeros((halo, pad_ref.shape[1]), F32)
    pad_ref[0:halo, :] = zeros
    pad_ref[s + halo:s + 2 * halo, :] = zeros

    def cp(c, carry):
        r0 = pl.multiple_of(c * ch, ch)
        pad_ref[pl.ds(r0 + halo, ch), :] = src_ref[pl.ds(r0, ch), :].astype(F32)
        return carry

    lax.fori_loop(0, s // ch, cp, 0)


def _ffn_act_fwd(u, w, b):
    s = u.shape[0]
    nj = DFF // FF_TILE
    ch = FF_CHUNK

    def body(ug_ref, uv_ref, wg_ref, wv_ref, bg_ref, bv_ref, f_ref, gpad, vpad):
        _fill_padded(gpad, ug_ref, s, ch, HALO)
        _fill_padded(vpad, uv_ref, s, ch, HALO)

        def conv(pad, w_ref, b_ref, r0):
            prev, cur, nxt = _shift3(pad, r0, ch)
            return w_ref[0:1, :] * prev + w_ref[1:2, :] * cur + w_ref[2:3, :] * nxt + b_ref[...]

        def step(c, carry):
            r0 = pl.multiple_of(c * ch, ch)
            gc = conv(gpad, wg_ref, bg_ref, r0)
            vc = conv(vpad, wv_ref, bv_ref, r0)
            f_ref[pl.ds(r0, ch), :] = (gc * _sigmoid(gc) * vc).astype(f_ref.dtype)
            return carry

        lax.fori_loop(0, s // ch, step, 0)

    col = lambda off: pl.BlockSpec((s, FF_TILE), lambda j: (0, j + off))
    wsp = lambda off: pl.BlockSpec((3, FF_TILE), lambda j: (0, j + off))
    bsp = lambda off: pl.BlockSpec((1, FF_TILE), lambda j: (0, j + off))
    return _pallas(
        body, name="ffn_act_fwd", grid=(nj,),
        in_specs=[col(0), col(nj), wsp(0), wsp(nj), bsp(0), bsp(nj)],
        out_specs=col(0), out_shape=_sds((s, DFF), MXU_DTYPE),
        scratch_shapes=[pltpu.VMEM((s + 2 * HALO, FF_TILE), F32)] * 2,
        semantics=("parallel",),
    )(u, u, w, w, b, b)


def _ffn_act_bwd(u, df, w, b):
    s = u.shape[0]
    nj = DFF // FF_TILE
    ch = FF_CHUNK

    def body(ug_ref, uv_ref, df_ref, wg_ref, wv_ref, bg_ref, bv_ref,
             dug_ref, duv_ref, dwg_ref, dwv_ref, dbg_ref, dbv_ref, gpad, vpad, dgpad, dvpad, acc):
        _fill_padded(gpad, ug_ref, s, ch, HALO)
        _fill_padded(vpad, uv_ref, s, ch, HALO)
        zeros = jnp.zeros((HALO, FF_TILE), F32)
        for p in (dgpad, dvpad):
            p[0:HALO, :] = zeros
            p[s + HALO:s + 2 * HALO, :] = zeros
        acc[...] = jnp.zeros_like(acc)

        def step(c, carry):
            r0 = pl.multiple_of(c * ch, ch)
            gs = _shift3(gpad, r0, ch)
            vs = _shift3(vpad, r0, ch)
            gc = wg_ref[0:1, :] * gs[0] + wg_ref[1:2, :] * gs[1] + wg_ref[2:3, :] * gs[2] + bg_ref[...]
            vc = wv_ref[0:1, :] * vs[0] + wv_ref[1:2, :] * vs[1] + wv_ref[2:3, :] * vs[2] + bv_ref[...]
            sg = _sigmoid(gc)
            dfv = df_ref[pl.ds(r0, ch), :]
            dgc = dfv * vc * (sg * (1.0 + gc * (1.0 - sg)))
            dvc = dfv * (gc * sg)
            dgpad[pl.ds(r0 + HALO, ch), :] = dgc
            dvpad[pl.ds(r0 + HALO, ch), :] = dvc
            for t in range(3):
                acc[8 * t:8 * t + 8, :] += _colsum8(dgc * gs[t])
                acc[24 + 8 * t:32 + 8 * t, :] += _colsum8(dvc * vs[t])
            acc[48:56, :] += _colsum8(dgc)
            acc[56:64, :] += _colsum8(dvc)
            return carry

        lax.fori_loop(0, s // ch, step, 0)

        def step2(c, carry):
            r0 = pl.multiple_of(c * ch, ch)
            for pad, w_ref, o_ref in ((dgpad, wg_ref, dug_ref), (dvpad, wv_ref, duv_ref)):
                prev, cur, nxt = _shift3(pad, r0, ch)
                o_ref[pl.ds(r0, ch), :] = (w_ref[0:1, :] * nxt + w_ref[1:2, :] * cur + w_ref[2:3, :] * prev).astype(o_ref.dtype)
            return carry

        lax.fori_loop(0, s // ch, step2, 0)
        for t in range(3):
            dwg_ref[t:t + 1, :] = jnp.sum(acc[8 * t:8 * t + 8, :], axis=0, keepdims=True)
            dwv_ref[t:t + 1, :] = jnp.sum(acc[24 + 8 * t:32 + 8 * t, :], axis=0, keepdims=True)
        dbg_ref[...] = jnp.sum(acc[48:56, :], axis=0, keepdims=True)
        dbv_ref[...] = jnp.sum(acc[56:64, :], axis=0, keepdims=True)

    col = lambda off: pl.BlockSpec((s, FF_TILE), lambda j: (0, j + off))
    wsp = lambda off: pl.BlockSpec((3, FF_TILE), lambda j: (0, j + off))
    bsp = lambda off: pl.BlockSpec((1, FF_TILE), lambda j: (0, j + off))
    return _pallas(
        body, name="ffn_act_bwd", grid=(nj,),
        in_specs=[col(0), col(nj), col(0), wsp(0), wsp(nj), bsp(0), bsp(nj)],
        out_specs=[col(0), col(0), wsp(0), wsp(0), bsp(0), bsp(0)],
        out_shape=[_sds((s, DFF), MXU_DTYPE)] * 2 + [_sds((3, DFF), F32)] * 2 + [_sds((1, DFF), F32)] * 2,
        scratch_shapes=[pltpu.VMEM((s + 2 * HALO, FF_TILE), F32)] * 4 + [pltpu.VMEM((64, FF_TILE), F32)],
        semantics=("parallel",),
    )(u, u, df, w, w, b, b)


CONV_CHUNK = 64
CONV_HALO = 16
CONV_WIN = CONV_CHUNK + 2 * CONV_HALO


def _tap(win, k):
    off = CONV_HALO - CW // 2 + k
    return pltpu.roll(win, CONV_WIN - off, 0)[0:CONV_CHUNK]


def _glu_into(pad_ref, a_ref, g_ref, s):
    zeros = jnp.zeros((CONV_HALO, LANES), F32)
    pad_ref[0:CONV_HALO, :] = zeros
    pad_ref[s + CONV_HALO:s + 2 * CONV_HALO, :] = zeros

    def cp(c, carry):
        r0 = pl.multiple_of(c * ROW_TILE, ROW_TILE)
        pad_ref[pl.ds(r0 + CONV_HALO, ROW_TILE), :] = a_ref[pl.ds(r0, ROW_TILE), :] * _sigmoid(g_ref[pl.ds(r0, ROW_TILE), :])
        return carry

    lax.fori_loop(0, s // ROW_TILE, cp, 0)


def _conf_conv_fwd(ag, conv_w, conv_b):
    s = ag.shape[0]
    nc = DC // LANES

    def body(a_ref, g_ref, w_ref, b_ref, o_ref, upad):
        _glu_into(upad, a_ref, g_ref, s)

        def step(c, carry):
            r0 = pl.multiple_of(c * CONV_CHUNK, CONV_CHUNK)
            win = upad[pl.ds(r0, CONV_WIN), :]
            acc = jnp.broadcast_to(b_ref[...], (CONV_CHUNK, LANES))
            for k in range(CW):
                acc = acc + w_ref[k:k + 1, :] * _tap(win, k)
            o_ref[pl.ds(r0, CONV_CHUNK), :] = acc
            return carry

        lax.fori_loop(0, s // CONV_CHUNK, step, 0)

    col = lambda off: pl.BlockSpec((s, LANES), lambda c: (0, c + off))
    return _pallas(
        body, name="conf_conv_fwd", grid=(nc,),
        in_specs=[col(0), col(nc), pl.BlockSpec((CW, LANES), lambda c: (0, c)), pl.BlockSpec((1, LANES), lambda c: (0, c))],
        out_specs=col(0), out_shape=_sds((s, DC), F32),
        scratch_shapes=[pltpu.VMEM((s + 2 * CONV_HALO, LANES), F32)],
        semantics=("parallel",),
    )(ag, ag, conv_w, conv_b)


def _ln_stats(x):
    mu = jnp.mean(x, axis=-1, keepdims=True)
    xc = x - mu
    var = jnp.mean(xc * xc, axis=-1, keepdims=True)
    rstd = lax.rsqrt(var + EPS)
    return xc * rstd, rstd


def _conf_ln_fwd(u1, ln_g, ln_b, ycat):
    s = u1.shape[0]

    def body(u_ref, g_ref, b_ref, ycat_ref, o_ref):
        del ycat_ref
        xhat, _ = _ln_stats(u_ref[...])
        y = xhat * g_ref[...] + b_ref[...]
        o_ref[...] = (y * _sigmoid(y)).astype(o_ref.dtype)

    return _pallas(
        body, name="conf_ln_fwd", grid=(s // ROW_TILE,),
        in_specs=[pl.BlockSpec((ROW_TILE, DC), lambda i: (i, 0)), _vec_spec(DC), _vec_spec(DC),
                  pl.BlockSpec(memory_space=pl.ANY)],
        out_specs=pl.BlockSpec((ROW_TILE, DC), lambda i: (i, 1)),
        out_shape=_sds(ycat.shape, ycat.dtype),
        input_output_aliases={3: 0},
        semantics=("parallel",),
    )(u1, ln_g, ln_b, ycat)


def _conf_ln_bwd(dycat, u1, ln_g, ln_b):
    s = u1.shape[0]
    nt = s // ROW_TILE

    def body(dy_ref, u_ref, g_ref, b_ref, du_ref, dg_ref, db_ref, a_g, a_b):
        i = pl.program_id(0)

        @pl.when(i == 0)
        def _():
            a_g[...] = jnp.zeros_like(a_g)
            a_b[...] = jnp.zeros_like(a_b)

        xhat, rstd = _ln_stats(u_ref[...])
        gv = g_ref[...]
        y = xhat * gv + b_ref[...]
        sg = _sigmoid(y)
        dyl = dy_ref[...] * (sg * (1.0 + y * (1.0 - sg)))
        a_g[...] += _colsum8(dyl * xhat)
        a_b[...] += _colsum8(dyl)
        dxh = dyl * gv
        du_ref[...] = rstd * (dxh - jnp.mean(dxh, axis=-1, keepdims=True)
                              - xhat * jnp.mean(dxh * xhat, axis=-1, keepdims=True))

        @pl.when(i == nt - 1)
        def _():
            dg_ref[...] = jnp.sum(a_g[...], axis=0, keepdims=True)
            db_ref[...] = jnp.sum(a_b[...], axis=0, keepdims=True)

    return _pallas(
        body, name="conf_ln_bwd", grid=(nt,),
        in_specs=[pl.BlockSpec((ROW_TILE, DC), lambda i: (i, 1)), pl.BlockSpec((ROW_TILE, DC), lambda i: (i, 0)),
                  _vec_spec(DC), _vec_spec(DC)],
        out_specs=[pl.BlockSpec((ROW_TILE, DC), lambda i: (i, 0)), _vec_spec(DC), _vec_spec(DC)],
        out_shape=[_sds((s, DC), F32), _sds((1, DC), F32), _sds((1, DC), F32)],
        scratch_shapes=[pltpu.VMEM((8, DC), F32)] * 2,
        semantics=("arbitrary",),
    )(dycat, u1, ln_g, ln_b)


def _conf_conv_bwd(ag, du1, conv_w, rows_out):
    s = ag.shape[0]
    nc = DC // LANES

    def body(a_ref, g_ref, d_ref, w_ref, da_ref, dg_ref, dw_ref, db_ref, upad, dpad, acc):
        _glu_into(upad, a_ref, g_ref, s)
        _fill_padded(dpad, d_ref, s, ROW_TILE, CONV_HALO)
        acc[...] = jnp.zeros_like(acc)

        def step(c, carry):
            r0 = pl.multiple_of(c * CONV_CHUNK, CONV_CHUNK)
            uwin = upad[pl.ds(r0, CONV_WIN), :]
            dwin = dpad[pl.ds(r0, CONV_WIN), :]
            dcur = dwin[CONV_HALO:CONV_HALO + CONV_CHUNK]
            du0 = jnp.zeros((CONV_CHUNK, LANES), F32)
            for k in range(CW):
                du0 = du0 + w_ref[k:k + 1, :] * _tap(dwin, CW - 1 - k)
                acc[8 * k:8 * k + 8, :] += _colsum8(dcur * _tap(uwin, k))
            acc[8 * CW:8 * CW + 8, :] += _colsum8(dcur)
            av = a_ref[pl.ds(r0, CONV_CHUNK), :]
            sg = _sigmoid(g_ref[pl.ds(r0, CONV_CHUNK), :])
            da_ref[pl.ds(r0, CONV_CHUNK), :] = (du0 * sg).astype(da_ref.dtype)
            dg_ref[pl.ds(r0, CONV_CHUNK), :] = (du0 * av * (sg * (1.0 - sg))).astype(dg_ref.dtype)
            return carry

        lax.fori_loop(0, s // CONV_CHUNK, step, 0)
        if rows_out > s:
            zeros = jnp.zeros((rows_out - s, LANES), da_ref.dtype)
            da_ref[s:rows_out, :] = zeros
            dg_ref[s:rows_out, :] = zeros
        for k in range(CW):
            dw_ref[k:k + 1, :] = jnp.sum(acc[8 * k:8 * k + 8, :], axis=0, keepdims=True)
        db_ref[...] = jnp.sum(acc[8 * CW:8 * CW + 8, :], axis=0, keepdims=True)

    col = lambda off: pl.BlockSpec((s, LANES), lambda c: (0, c + off))
    ocol = pl.BlockSpec((rows_out, LANES), lambda c: (0, c))
    return _pallas(
        body, name="conf_conv_bwd", grid=(nc,),
        in_specs=[col(0), col(nc), col(0), pl.BlockSpec((CW, LANES), lambda c: (0, c))],
        out_specs=[ocol, ocol, pl.BlockSpec((CW, LANES), lambda c: (0, c)), pl.BlockSpec((1, LANES), lambda c: (0, c))],
        out_shape=[_sds((rows_out, DC), MXU_DTYPE)] * 2 + [_sds((CW, DC), F32), _sds((1, DC), F32)],
        scratch_shapes=[pltpu.VMEM((s + 2 * CONV_HALO, LANES), F32)] * 2 + [pltpu.VMEM((8 * (CW + 1), LANES), F32)],
        semantics=("parallel",),
    )(ag, ag, du1, conv_w)


Q_TILE = 2 * GW
K_WIN = PAIR_ROWS * GW


def _bias_table(rpb_rev):
    def body(p_ref, t_ref):
        kcol = lax.broadcasted_iota(jnp.int32, (GW, LANES), 0)
        lane = lax.broadcasted_iota(jnp.int32, (GW, LANES), 1)
        qcol = lane % GW
        cs = jnp.clip(qcol - NA_ROWS, 0, GW - 2 * NA_ROWS)
        colvalid = (kcol >= cs) & (kcol < cs + 2 * NA_ROWS)
        neg = jnp.full((GW, LANES), NEG, F32)

        def skew(h, ro, shift):
            if ro < 0 or ro >= 2 * NA_ROWS - 1:
                return neg
            row = jnp.broadcast_to(p_ref[h * 16 + ro:h * 16 + ro + 1, :], (GW, LANES))
            return pltpu.roll(row, shift, 1, stride=1, stride_axis=0)

        for h in range(NH):
            for b in range(TAB_BLOCKS):
                val = jnp.where(lane < GW, skew(h, b - 1, GW + 1), skew(h, b - 2, 1))
                t_ref[h, b * GW:(b + 1) * GW, :] = jnp.where(colvalid, val, neg)

    return _pallas(body, name="attn_bias_table", out_shape=_sds((NH, TAB_BLOCKS * GW, LANES), F32))(rpb_rev)


def _rpb_grad(tt):
    def body(t_ref, o_ref):
        lane = lax.broadcasted_iota(jnp.int32, (GW, LANES), 1)
        si = lax.broadcasted_iota(jnp.int32, (GW, GW), 0)
        ti = lax.broadcasted_iota(jnp.int32, (GW, GW), 1)
        flip = jnp.where(si + ti == GW - 1, 1.0, 0.0).astype(F32)
        o_ref[...] = jnp.zeros_like(o_ref)
        for h in range(NH):
            for ro in range(2 * NA_ROWS - 1):
                lo = t_ref[h, (ro + 1) * GW:(ro + 2) * GW, :]
                hi = t_ref[h, (ro + 2) * GW:(ro + 3) * GW, :]
                g = jnp.where(lane < GW, lo + pltpu.roll(hi, GW, 1), 0.0)
                gf = jnp.dot(flip, g, preferred_element_type=F32, precision=lax.Precision.HIGHEST)
                sk = pltpu.roll(gf, 0, 1, stride=1, stride_axis=0)
                o_ref[h * 16 + ro:h * 16 + ro + 1, :] = jnp.sum(sk, axis=0, keepdims=True)

    return _pallas(body, name="attn_rpb_grad", out_shape=_sds((NH * 16, LANES), F32))(tt)


def _attn_geometry(i, rows):
    wsp = jnp.clip(2 * i - NA_ROWS // 2, 0, rows - PAIR_ROWS)
    k0 = pl.multiple_of(wsp * GW, GW)
    t0 = pl.multiple_of((wsp - 2 * i + NA_ROWS) * GW, GW)
    jr = lax.broadcasted_iota(jnp.int32, (K_WIN, Q_TILE), 0) // GW
    rr = lax.broadcasted_iota(jnp.int32, (K_WIN, Q_TILE), 1) // GW
    kr = wsp + jr
    wsr = jnp.clip(2 * i + rr - NA_ROWS // 2, 0, rows - NA_ROWS)
    rowmask = jnp.where((kr >= wsr) & (kr < wsr + NA_ROWS), 0.0, NEG).astype(F32)
    return k0, t0, rowmask


def _two_heads_on_lanes(xt):
    feat = lax.broadcasted_iota(jnp.int32, xt.shape, 0)
    zero = jnp.zeros_like(xt)
    return jnp.concatenate([jnp.where(feat < HD, xt, zero), jnp.where(feat >= HD, xt, zero)], axis=1)


def _two_heads_on_rows(x):
    lane = lax.broadcasted_iota(jnp.int32, x.shape, 1)
    zero = jnp.zeros_like(x)
    return jnp.concatenate([jnp.where(lane < HD, x, zero), jnp.where(lane >= HD, x, zero)], axis=0)


def _pick_heads(x2):
    n = x2.shape[0] // 2
    lane = lax.broadcasted_iota(jnp.int32, (n, LANES), 1)
    return jnp.where(lane < HD, x2[:n], x2[n:])


_TN = (((0,), (0,)), ((), ()))


def _attn_fwd(qkv, tab, s):
    rows = s // GW
    npair = rows // 2

    def body(q_ref, kv_ref, tab_ref, o_ref, lse_ref):
        i = pl.program_id(0)
        k0, t0, rowmask = _attn_geometry(i, rows)
        for p in range(NH // 2):
            cq = slice(p * LANES, (p + 1) * LANES)
            ck = slice(DA + p * LANES, DA + (p + 1) * LANES)
            cv = slice(2 * DA + p * LANES, 2 * DA + (p + 1) * LANES)
            qm2 = _two_heads_on_lanes(q_ref[:, cq].T)
            s_loc = jnp.dot(kv_ref[pl.ds(k0, K_WIN), ck], qm2, preferred_element_type=F32) * SCALE
            s_ctx = jnp.dot(kv_ref[pl.ds(s, CTX), ck], qm2, preferred_element_type=F32) * SCALE
            p_loc, p_ctx = [], []
            for hh in range(2):
                h = 2 * p + hh
                ch = slice(hh * Q_TILE, (hh + 1) * Q_TILE)
                sl = s_loc[:, ch] + tab_ref[h, pl.ds(t0, K_WIN), :] + rowmask
                sc = s_ctx[:, ch]
                m = jnp.maximum(jnp.max(sl, axis=0, keepdims=True), jnp.max(sc, axis=0, keepdims=True))
                el = jnp.exp(sl - m)
                ec = jnp.exp(sc - m)
                l = jnp.sum(el, axis=0, keepdims=True) + jnp.sum(ec, axis=0, keepdims=True)
                inv = 1.0 / l
                lse_ref[h:h + 1, :] = m + jnp.log(l)
                p_loc.append((el * inv).astype(MXU_DTYPE))
                p_ctx.append((ec * inv).astype(MXU_DTYPE))
            o2 = (lax.dot_general(jnp.concatenate(p_loc, axis=1), kv_ref[pl.ds(k0, K_WIN), cv], _TN, preferred_element_type=F32)
                  + lax.dot_general(jnp.concatenate(p_ctx, axis=1), kv_ref[pl.ds(s, CTX), cv], _TN, preferred_element_type=F32))
            o_ref[:, cq] = _pick_heads(o2).astype(o_ref.dtype)

    return _pallas(
        body, name="attn_fwd", grid=(npair,),
        in_specs=[pl.BlockSpec((Q_TILE, DA), lambda i: (i, 0)), pl.BlockSpec(memory_space=pltpu.VMEM),
                  pl.BlockSpec(memory_space=pltpu.VMEM)],
        out_specs=[pl.BlockSpec((Q_TILE, DA), lambda i: (i, 0)), pl.BlockSpec((NH, Q_TILE), lambda i: (0, i))],
        out_shape=[_sds((s, D), MXU_DTYPE), _sds((NH, s), F32)],
        semantics=("arbitrary",),
    )(qkv, qkv, tab)


def _attn_bwd(qkv, tab, lse, dycat, s):
    rows = s // GW
    npair = rows // 2
    sa = s + CTX
    nzero = CTX // Q_TILE

    def body(q_ref, do_ref, lse_ref, kv_ref, tab_ref, dq_ref, dkv_ref, tt_ref, dk_acc, dv_acc):
        i = pl.program_id(0)

        @pl.when(i == 0)
        def _():
            dk_acc[...] = jnp.zeros_like(dk_acc)
            dv_acc[...] = jnp.zeros_like(dv_acc)
            tt_ref[...] = jnp.zeros_like(tt_ref)

        @pl.when(i >= npair)
        def _():
            dq_ref[...] = jnp.zeros_like(dq_ref)

        @pl.when(i < npair)
        def _():
            k0, t0, rowmask = _attn_geometry(i, rows)
            for p in range(NH // 2):
                cq = slice(p * LANES, (p + 1) * LANES)
                ck = slice(DA + p * LANES, DA + (p + 1) * LANES)
                cv = slice(2 * DA + p * LANES, 2 * DA + (p + 1) * LANES)
                qp = q_ref[:, cq]
                dop = do_ref[:, cq].astype(MXU_DTYPE)
                qm2 = _two_heads_on_lanes(qp.T)
                dom2 = _two_heads_on_lanes(dop.T)
                kw = kv_ref[pl.ds(k0, K_WIN), ck]
                kc = kv_ref[pl.ds(s, CTX), ck]
                vw = kv_ref[pl.ds(k0, K_WIN), cv]
                vc = kv_ref[pl.ds(s, CTX), cv]
                s_loc = jnp.dot(kw, qm2, preferred_element_type=F32) * SCALE
                s_ctx = jnp.dot(kc, qm2, preferred_element_type=F32) * SCALE
                dp_loc = jnp.dot(vw, dom2, preferred_element_type=F32)
                dp_ctx = jnp.dot(vc, dom2, preferred_element_type=F32)
                p_loc, p_ctx, ds_loc, ds_ctx = [], [], [], []
                for hh in range(2):
                    h = 2 * p + hh
                    ch = slice(hh * Q_TILE, (hh + 1) * Q_TILE)
                    lse_h = lse_ref[h:h + 1, :]
                    pl_ = jnp.exp(s_loc[:, ch] + tab_ref[h, pl.ds(t0, K_WIN), :] + rowmask - lse_h)
                    pc_ = jnp.exp(s_ctx[:, ch] - lse_h)
                    dpl = dp_loc[:, ch]
                    dpc = dp_ctx[:, ch]
                    delta = jnp.sum(pl_ * dpl, axis=0, keepdims=True) + jnp.sum(pc_ * dpc, axis=0, keepdims=True)
                    dsl = pl_ * (dpl - delta)
                    dsc = pc_ * (dpc - delta)
                    tt_ref[h, pl.ds(t0, K_WIN), :] += dsl
                    p_loc.append(pl_.astype(MXU_DTYPE))
                    p_ctx.append(pc_.astype(MXU_DTYPE))
                    ds_loc.append((dsl * SCALE).astype(MXU_DTYPE))
                    ds_ctx.append((dsc * SCALE).astype(MXU_DTYPE))
                p_loc, p_ctx = jnp.concatenate(p_loc, axis=1), jnp.concatenate(p_ctx, axis=1)
                ds_loc, ds_ctx = jnp.concatenate(ds_loc, axis=1), jnp.concatenate(ds_ctx, axis=1)
                do_rows = _two_heads_on_rows(dop)
                q_rows = _two_heads_on_rows(qp)
                dv_acc[pl.ds(k0, K_WIN), cq] += jnp.dot(p_loc, do_rows, preferred_element_type=F32)
                dv_acc[pl.ds(s, CTX), cq] += jnp.dot(p_ctx, do_rows, preferred_element_type=F32)
                dk_acc[pl.ds(k0, K_WIN), cq] += jnp.dot(ds_loc, q_rows, preferred_element_type=F32)
                dk_acc[pl.ds(s, CTX), cq] += jnp.dot(ds_ctx, q_rows, preferred_element_type=F32)
                dq2 = (lax.dot_general(ds_loc, kw, _TN, preferred_element_type=F32)
                       + lax.dot_general(ds_ctx, kc, _TN, preferred_element_type=F32))
                dq_ref[:, cq] = _pick_heads(dq2).astype(dq_ref.dtype)

        @pl.when(i == npair - 1)
        def _():
            def cp(c, carry):
                r0 = pl.multiple_of(c * ROW_TILE, ROW_TILE)
                dkv_ref[pl.ds(r0, ROW_TILE), 0:DA] = dk_acc[pl.ds(r0, ROW_TILE), :].astype(dkv_ref.dtype)
                dkv_ref[pl.ds(r0, ROW_TILE), DA:2 * DA] = dv_acc[pl.ds(r0, ROW_TILE), :].astype(dkv_ref.dtype)
                return carry

            lax.fori_loop(0, sa // ROW_TILE, cp, 0)

    qmap = lambda i: (jnp.minimum(i, npair - 1), 0)
    return _pallas(
        body, name="attn_bwd", grid=(npair + nzero,),
        in_specs=[pl.BlockSpec((Q_TILE, DA), qmap), pl.BlockSpec((Q_TILE, DA), qmap),
                  pl.BlockSpec((NH, Q_TILE), lambda i: (0, jnp.minimum(i, npair - 1))),
                  pl.BlockSpec(memory_space=pltpu.VMEM), pl.BlockSpec(memory_space=pltpu.VMEM)],
        out_specs=[pl.BlockSpec((Q_TILE, DA), lambda i: (i, 0)), pl.BlockSpec(memory_space=pltpu.VMEM),
                   pl.BlockSpec(memory_space=pltpu.VMEM)],
        out_shape=[_sds((sa, DA), MXU_DTYPE), _sds((sa, 2 * DA), MXU_DTYPE), _sds((NH, TAB_BLOCKS * GW, LANES), F32)],
        scratch_shapes=[pltpu.VMEM((sa, DA), F32)] * 2,
        semantics=("arbitrary",),
    )(qkv, dycat, lse, qkv, tab)


def _tile(n, prefs):
    for t in prefs:
        if n % t == 0:
            return t
    raise ValueError((n, prefs))


def _local_step(x, ctx, tgt, mod, mod_c, vec, w_in, w_out, w_up, w_down, rpb_rev):
    s = x.shape[0]
    sa = s + CTX
    ts = _tile(s, (1024, 512, 256))
    tsa = _tile(sa, (1088, 640, 256))
    sh1, sc1, gt1, sh2, sc2, gt2 = (mod[i:i + 1] for i in range(6))
    csh1, csc1 = mod_c[0:1], mod_c[1:2]

    tab = _bias_table(rpb_rev)
    h_all = _rmsmod_fwd(x, ctx, vec["g_norm1"], sc1, sh1, csc1, csh1)
    qkv = _mm(h_all, w_in, mode="nn", m=sa, n=3 * DA, k=D, tm=tsa, tn=512, tk=D, out_dtype=MXU_DTYPE, name="mm_qkv")
    ag = _mm(h_all, w_in, mode="nn", m=s, n=2 * DC, k=D, tm=ts, tn=512, tk=D, out_dtype=F32, name="mm_ag", b_off=(0, 3))
    ycat, lse = _attn_fwd(qkv, tab, s)
    u1 = _conf_conv_fwd(ag, vec["conv_w"], vec["conv_b"])
    ycat = _conf_ln_fwd(u1, vec["ln_g"], vec["ln_b"], ycat)
    y = _mm(ycat, w_out, mode="nn", m=s, n=D, k=D, tm=ts, tn=512, tk=D, out_dtype=F32, name="mm_out")
    x1, h2 = _resid_rmsmod_fwd(x, y, gt1, vec["g_norm2"], sc2, sh2)
    u = _mm(h2, w_up, mode="nn", m=s, n=2 * DFF, k=D, tm=ts, tn=512, tk=D, out_dtype=F32, name="mm_up")
    f = _ffn_act_fwd(u, vec["ffn_conv_w"], vec["ffn_conv_b"])
    z = _mm(f, w_down, mode="nn", m=s, n=D, k=DFF, tm=ts, tn=512, tk=DFF // 2, out_dtype=F32, name="mm_down")
    dx2, dz, loss, dgt2, dgf = _final_fwd_bwd(x1, z, gt2, vec["g_final"], tgt)

    df = _mm(dz, w_down, mode="nt", m=s, n=DFF, k=D, tm=ts, tn=DFF // 2, tk=D, out_dtype=F32, name="mm_down_dx")
    d_w_down = _mm(f, dz, mode="tn", m=DFF, n=D, k=s, tm=DFF // 2, tn=D, tk=ts, out_dtype=F32, name="mm_down_dw")
    dug, duv, dfw_g, dfw_v, dfb_g, dfb_v = _ffn_act_bwd(u, df, vec["ffn_conv_w"], vec["ffn_conv_b"])
    dh2 = _mm([dug, duv], w_up, mode="nt", m=s, n=D, k=2 * DFF, tm=ts, tn=D, tk=DFF // 2, out_dtype=F32, name="mm_up_dx")
    d_w_up = _mm(h2, [dug, duv], mode="tn", m=D, n=2 * DFF, k=s, tm=D, tn=DFF // 2, tk=ts, out_dtype=F32, name="mm_up_dw")
    dsh2, dsc2, dg2, dx1, dy, dgt1 = _rmsmod_bwd(x1, dh2, vec["g_norm2"], sc2, name="rmsmod2_bwd", add=dx2, resid=(gt1, y))
    dycat = _mm(dy, w_out, mode="nt", m=s, n=D, k=D, tm=ts, tn=512, tk=D, out_dtype=F32, name="mm_out_dx")
    d_w_out = _mm(ycat, dy, mode="tn", m=D, n=D, k=s, tm=512, tn=D, tk=ts, out_dtype=F32, name="mm_out_dw")
    du1, dln_g, dln_b = _conf_ln_bwd(dycat, u1, vec["ln_g"], vec["ln_b"])
    da, dg, dconv_w, dconv_b = _conf_conv_bwd(ag, du1, vec["conv_w"], sa)
    dq, dkv, tt = _attn_bwd(qkv, tab, lse, dycat, s)
    drpb_rev = _rpb_grad(tt)
    d_pieces = [dq, dkv, da, dg]
    dh = _mm(d_pieces, w_in, mode="nt", m=sa, n=D, k=NIN, tm=tsa, tn=D, tk=512, out_dtype=F32, name="mm_in_dx")
    d_w_in = _mm(h_all, d_pieces, mode="tn", m=D, n=NIN, k=sa, tm=D, tn=512, tk=tsa, out_dtype=F32, name="mm_in_dw")
    dsh1, dsc1, dg1, grad_x = _rmsmod_bwd(x, dh, vec["g_norm1"], sc1, name="rmsmod1_bwd", add=dx1)
    dcsh1, dcsc1, dg1c = _rmsmod_bwd(ctx, dh, vec["g_norm1"], csc1, name="rmsmod1_ctx_bwd", dh_row_off=s // ROW_TILE)

    small = dict(
        dmod=[dsh1, dsc1, dgt1, dsh2, dsc2, dgt2], dmod_c=[dcsh1, dcsc1],
        g_norm1=[dg1, dg1c], g_norm2=dg2, g_final=dgf, conv_b=dconv_b, ln_g=dln_g, ln_b=dln_b, conv_w=dconv_w,
        ffn_conv_w=[dfw_g, dfw_v], ffn_conv_b=[dfb_g, dfb_v], rpb_rev=drpb_rev,
    )
    return loss, grad_x, d_w_in, d_w_out, d_w_up, d_w_down, small


N_CHIPS = 4
HBM = pl.BlockSpec(memory_space=pl.ANY)
BIG = {"w_in": ("col", (D, NIN)), "w_out": ("row", (D, D)), "w_up": ("col", (D, 2 * DFF)), "w_down": ("row", (DFF, D))}
BIG_NAMES = tuple(BIG)


def _shard_shape(name):
    kind, (r, c) = BIG[name]
    return (r, c // N_CHIPS) if kind == "col" else (r // N_CHIPS, c)


def _half_rows(name):
    return _shard_shape(name)[0] // 2


def _place():
    x, y, c = lax.axis_index("x"), lax.axis_index("y"), lax.axis_index("c")
    others = [(1 - x, y), (x, 1 - y), (1 - x, 1 - y)]
    return x, y, c, 2 * x + y, (x, y, 1 - c), others


def _whole_region(ref, name, chip, half):
    kind, _ = BIG[name]
    r, c = _shard_shape(name)
    if kind == "col":
        return ref.at[pl.ds(half * (r // 2), r // 2), pl.ds(chip * c, c)]
    return ref.at[pl.ds(chip * r + half * (r // 2), r // 2), :]


def _remote(src, dst, send_sem, recv_sem, to):
    return pltpu.make_async_remote_copy(src_ref=src, dst_ref=dst, send_sem=send_sem, recv_sem=recv_sem,
                                        device_id=to, device_id_type=MESH)


def _gather_small(v, name):
    m_per, n = v.shape

    def body(x_ref, out_ref, send_sems, recv_sems, local_sem):
        x, y, c, _, sibling, others = _place()
        me = (x, y, c)

        def rows(px, py, pc):
            return out_ref.at[pl.ds((4 * px + 2 * py + pc) * m_per, m_per), :]

        def copy(k, block, to, src=None):
            return _remote(rows(*block) if src is None else src, rows(*block), send_sems.at[k], recv_sems.at[k], to)

        mine = pltpu.make_async_copy(x_ref, rows(*me), local_sem)
        mine.start()
        first = [copy(0, me, sibling, src=x_ref)]
        first += [copy(1 + j, me, (*chip, c), src=x_ref) for j, chip in enumerate(others)]
        for cp in first:
            cp.start()
        passed = [copy(4 + j, (*chip, c), sibling) for j, chip in enumerate(others)]
        for j, chip in enumerate(others):
            copy(1 + j, (*chip, c), me).wait_recv()
            passed[j].start()
        copy(0, sibling, me).wait_recv()
        for j, chip in enumerate(others):
            copy(4 + j, (*chip, 1 - c), me).wait_recv()
        for cp in first + passed:
            cp.wait_send()
        mine.wait()

    return pl.pallas_call(
        body, name=name, out_shape=_sds((8 * m_per, n), v.dtype),
        in_specs=[pl.BlockSpec(memory_space=pltpu.VMEM)], out_specs=pl.BlockSpec(memory_space=pltpu.VMEM),
        scratch_shapes=[pltpu.SemaphoreType.DMA((7,)), pltpu.SemaphoreType.DMA((7,)), pltpu.SemaphoreType.DMA],
    )(v)


def _gather_weights(shards):
    nw = len(BIG_NAMES)

    def body(*refs):
        ins, outs = refs[:nw], refs[nw:2 * nw]
        send_sems, recv_sems, local_sems = refs[2 * nw:]
        x, y, c, chip, sibling, others = _place()
        sends = []
        for w, name in enumerate(BIG_NAMES):
            half = _half_rows(name)
            for hf in range(2):
                pltpu.make_async_copy(ins[w].at[pl.ds(hf * half, half), :], _whole_region(outs[w], name, chip, hf),
                                      local_sems.at[w, hf]).start()
            for t, (ox, oy) in enumerate(others):
                cp = _remote(ins[w].at[pl.ds(c * half, half), :], _whole_region(outs[w], name, chip, c),
                             send_sems.at[w, t], recv_sems.at[w, t], (ox, oy, c))
                cp.start()
                sends.append(cp)
        for w, name in enumerate(BIG_NAMES):
            for t, (ox, oy) in enumerate(others):
                got = _whole_region(outs[w], name, 2 * ox + oy, c)
                _remote(got, got, send_sems.at[w, t], recv_sems.at[w, t], (ox, oy, c)).wait_recv()
                cp = _remote(got, got, send_sems.at[w, 3 + t], recv_sems.at[w, 3 + t], sibling)
                cp.start()
                sends.append(cp)
        for w, name in enumerate(BIG_NAMES):
            for t, (ox, oy) in enumerate(others):
                got = _whole_region(outs[w], name, 2 * ox + oy, 1 - c)
                _remote(got, got, send_sems.at[w, 3 + t], recv_sems.at[w, 3 + t], sibling).wait_recv()
        for cp in sends:
            cp.wait_send()
        for w, name in enumerate(BIG_NAMES):
            half = _half_rows(name)
            for hf in range(2):
                pltpu.make_async_copy(ins[w].at[pl.ds(hf * half, half), :], _whole_region(outs[w], name, chip, hf),
                                      local_sems.at[w, hf]).wait()

    return pl.pallas_call(
        body, name="gather_weights",
        out_shape=[_sds(BIG[n][1], shards[i].dtype) for i, n in enumerate(BIG_NAMES)],
        in_specs=[HBM] * nw, out_specs=[HBM] * nw,
        scratch_shapes=[pltpu.SemaphoreType.DMA((nw, 6)), pltpu.SemaphoreType.DMA((nw, 6)), pltpu.SemaphoreType.DMA((nw, 2))],
    )(*shards)


def _compact_shape(name, dtype):
    kind, (r, c) = BIG[name]
    return _sds((r // 2, c), dtype)


def _swap_halves(grads):
    nw = len(BIG_NAMES)

    def body(*refs):
        ins, outs = refs[:nw], refs[nw:2 * nw]
        send_sems, recv_sems = refs[2 * nw:]
        _, _, c, _, sibling, _ = _place()
        copies = []
        for w, name in enumerate(BIG_NAMES):
            kind, (r, _) = BIG[name]
            half = _half_rows(name)
            if kind == "col":
                parts = [(ins[w].at[pl.ds((1 - c) * half, half), :], outs[w])]
            else:
                parts = [(ins[w].at[pl.ds(jj * 2 * half + (1 - c) * half, half), :], outs[w].at[pl.ds(jj * half, half), :])
                         for jj in range(N_CHIPS)]
            for t, (src, dst) in enumerate(parts):
                cp = _remote(src, dst, send_sems.at[w, t], recv_sems.at[w, t], sibling)
                cp.start()
                copies.append(cp)
        for cp in copies:
            cp.wait()

    return pl.pallas_call(
        body, name="grad_swap_halves",
        out_shape=[_compact_shape(n, F32) for n in BIG_NAMES],
        in_specs=[HBM] * nw, out_specs=[HBM] * nw,
        scratch_shapes=[pltpu.SemaphoreType.DMA((nw, N_CHIPS)), pltpu.SemaphoreType.DMA((nw, N_CHIPS))],
    )(*grads)


def _add_halves(name, grad, got, core):
    kind, (r, c) = BIG[name]
    half = _half_rows(name)
    if kind == "col":
        t = 128
        grid = (half // t,)
        g_spec = pl.BlockSpec((t, c), lambda i, cr: (cr[0] * (half // t) + i, 0))
        o_spec = pl.BlockSpec((t, c), lambda i, cr: (i, 0))
    else:
        t = half
        grid = (N_CHIPS,)
        g_spec = pl.BlockSpec((t, c), lambda i, cr: (2 * i + cr[0], 0))
        o_spec = pl.BlockSpec((t, c), lambda i, cr: (i, 0))

    def body(c_ref, g_ref, b_ref, o_ref):
        del c_ref
        o_ref[...] = (g_ref[...] + b_ref[...]).astype(o_ref.dtype)

    return pl.pallas_call(
        body, name="grad_add_" + name,
        grid_spec=pltpu.PrefetchScalarGridSpec(num_scalar_prefetch=1, grid=grid, in_specs=[g_spec, o_spec], out_specs=o_spec),
        out_shape=_compact_shape(name, BF16),
        compiler_params=pltpu.CompilerParams(dimension_semantics=("parallel",), vmem_limit_bytes=VMEM_LIMIT),
    )(core, grad, got)


def _exchange_shards(parts):
    nw = len(BIG_NAMES)

    def piece(ref, name, chip):
        kind, _ = BIG[name]
        r, c = _shard_shape(name)
        if kind == "col":
            return ref.at[:, pl.ds(chip * c, c)]
        return ref.at[pl.ds(chip * (r // 2), r // 2), :]

    def body(*refs):
        ins, outs = refs[:nw], refs[nw:2 * nw]
        send_sems, recv_sems, local_sems = refs[2 * nw:]
        x, y, c, chip, _, others = _place()
        sends = []
        for w, name in enumerate(BIG_NAMES):
            pltpu.make_async_copy(piece(ins[w], name, chip), outs[w].at[chip], local_sems.at[w]).start()
            for t, (ox, oy) in enumerate(others):
                cp = _remote(piece(ins[w], name, 2 * ox + oy), outs[w].at[chip], send_sems.at[w, t], recv_sems.at[w, t], (ox, oy, c))
                cp.start()
                sends.append(cp)
        for w, name in enumerate(BIG_NAMES):
            for t, (ox, oy) in enumerate(others):
                got = outs[w].at[2 * ox + oy]
                _remote(got, got, send_sems.at[w, t], recv_sems.at[w, t], (ox, oy, c)).wait_recv()
        for cp in sends:
            cp.wait_send()
        for w, name in enumerate(BIG_NAMES):
            pltpu.make_async_copy(piece(ins[w], name, chip), outs[w].at[chip], local_sems.at[w]).wait()

    def out_shape(name):
        r, c = _shard_shape(name)
        return _sds((N_CHIPS, r // 2, c), BF16)

    return pl.pallas_call(
        body, name="grad_exchange_shards",
        out_shape=[out_shape(n) for n in BIG_NAMES],
        in_specs=[HBM] * nw, out_specs=[HBM] * nw,
        scratch_shapes=[pltpu.SemaphoreType.DMA((nw, 3)), pltpu.SemaphoreType.DMA((nw, 3)), pltpu.SemaphoreType.DMA((nw,))],
    )(*parts)


def _sum_chips(name, got):
    _, r, c = got.shape
    t = _tile(r, (128, 352))

    def body(g_ref, o_ref):
        acc = g_ref[0].astype(F32)
        for j in range(1, N_CHIPS):
            acc = acc + g_ref[j].astype(F32)
        o_ref[...] = acc

    return _pallas(
        body, name="grad_sum_" + name, grid=(r // t,),
        in_specs=[pl.BlockSpec((N_CHIPS, t, c), lambda i: (0, i, 0))], out_specs=pl.BlockSpec((t, c), lambda i: (i, 0)),
        out_shape=_sds((r, c), F32), semantics=("parallel",),
    )(got)


def _join_halves(sums):
    nw = len(BIG_NAMES)

    def body(*refs):
        ins, outs = refs[:nw], refs[nw:2 * nw]
        send_sems, recv_sems, local_sems = refs[2 * nw:]
        _, _, c, _, sibling, _ = _place()
        copies = []
        for w, name in enumerate(BIG_NAMES):
            half = _half_rows(name)
            mine = outs[w].at[pl.ds(c * half, half), :]
            pltpu.make_async_copy(ins[w], mine, local_sems.at[w]).start()
            cp = _remote(ins[w], mine, send_sems.at[w], recv_sems.at[w], sibling)
            cp.start()
            copies.append(cp)
        for w, name in enumerate(BIG_NAMES):
            half = _half_rows(name)
            theirs = outs[w].at[pl.ds((1 - c) * half, half), :]
            _remote(theirs, theirs, send_sems.at[w], recv_sems.at[w], sibling).wait_recv()
        for cp in copies:
            cp.wait_send()
        for w, name in enumerate(BIG_NAMES):
            half = _half_rows(name)
            pltpu.make_async_copy(ins[w], outs[w].at[pl.ds(c * half, half), :], local_sems.at[w]).wait()

    return pl.pallas_call(
        body, name="grad_join_halves",
        out_shape=[_sds(_shard_shape(n), F32) for n in BIG_NAMES],
        in_specs=[HBM] * nw, out_specs=[HBM] * nw,
        scratch_shapes=[pltpu.SemaphoreType.DMA((nw,)), pltpu.SemaphoreType.DMA((nw,)), pltpu.SemaphoreType.DMA((nw,))],
    )(*sums)


def _reduce_scatter(grads, core):
    got = _swap_halves(grads)
    parts = [_add_halves(n, grads[i], got[i], core) for i, n in enumerate(BIG_NAMES)]
    gathered = _exchange_shards(parts)
    sums = [_sum_chips(n, gathered[i]) for i, n in enumerate(BIG_NAMES)]
    return _join_halves(sums)


HI = lax.Precision.HIGHEST
MOD_COLS = 6 * D // N_CHIPS
COND_ROWS = 16


def _silu(v):
    return v * _sigmoid(v)


def _mod_shard(cond, w_mod, b_mod):
    def body(c_ref, w_ref, b_ref, o_ref):
        o_ref[...] = jnp.dot(_silu(c_ref[...]), w_ref[...], preferred_element_type=F32, precision=HI) + b_ref[...]

    return _pallas(body, name="mod_fwd", out_shape=_sds((COND_ROWS, MOD_COLS), F32))(cond, w_mod, b_mod)


def _mod_weight_grad(cond, dmod):
    def body(c_ref, d_ref, o_ref):
        o_ref[...] = lax.dot_general(_silu(c_ref[...]), d_ref[...], _TN, preferred_element_type=F32, precision=HI)

    return _pallas(body, name="mod_weight_grad", out_shape=_sds((D, MOD_COLS), F32))(cond, dmod)


def _cond_grad_partial(dmod_c, w_mod):
    def body(d_ref, w_ref, o_ref):
        o_ref[...] = lax.dot_general(d_ref[...], w_ref[...], (((1,), (1,)), ((), ())), preferred_element_type=F32, precision=HI)

    return _pallas(body, name="cond_grad_partial", out_shape=_sds((8, D), F32))(dmod_c, w_mod)


def _cond_grad(parts, c_ctx):
    def body(p_ref, c_ref, o_ref):
        tot = ((p_ref[0:1, :] + p_ref[1:2, :]) + p_ref[2:3, :]) + p_ref[3:4, :]
        cv = c_ref[...]
        sg = _sigmoid(cv)
        o_ref[...] = tot * (sg * (1.0 + cv * (1.0 - sg)))

    return _pallas(body, name="cond_grad", out_shape=_sds((1, D), F32))(parts, c_ctx)


def _sum_devices(packs, widths):
    n = packs.shape[1]

    def body(p_ref, o_ref, *extra):
        acc = p_ref[0:1, :]
        for d in range(1, 8):
            acc = acc + p_ref[d:d + 1, :]
        o_ref[...] = acc
        for (a, ma, b, mb), e_ref in zip(widths, extra):
            e_ref[...] = acc[:, a:a + ma]
            e_ref[:, 0:mb] = acc[:, a:a + mb] + acc[:, b:b + mb]

    return _pallas(body, name="sum_devices", out_shape=[_sds((1, n), F32)] + [_sds((1, w[1]), F32) for w in widths])(packs)


def _adamw(w, g, m, v, name):
    r, c = w.shape
    t = r
    for cand in (128, 352, 256):
        if r % cand == 0 and r > cand:
            t = cand
            break
    c1 = 1.0 - ADAM_B1 ** ADAM_STEP
    c2 = 1.0 - ADAM_B2 ** ADAM_STEP

    def body(w_ref, g_ref, m_ref, v_ref, d_ref, nm_ref, nv_ref):
        gv = g_ref[...]
        nm = ADAM_B1 * m_ref[...] + (1.0 - ADAM_B1) * gv
        nv = ADAM_B2 * v_ref[...] + (1.0 - ADAM_B2) * (gv * gv)
        nm_ref[...] = nm
        nv_ref[...] = nv
        d_ref[...] = -ADAM_LR * ((nm / c1) / (jnp.sqrt(nv / c2) + ADAM_EPS) + ADAM_WD * w_ref[...])

    blk = pl.BlockSpec((t, c), lambda i: (i, 0))
    return _pallas(body, name=name, grid=(r // t,), in_specs=[blk] * 4, out_specs=[blk] * 3,
                   out_shape=[_sds((r, c), F32)] * 3, semantics=("parallel",))(w, g, m, v)


WEIGHTS = ("c_ctx", "w_mod", "b_mod", "g_norm1", "w_in", "rpb", "conv_w", "conv_b", "ln_g", "ln_b", "w_out", "g_norm2",
           "w_up", "ffn_conv_w", "ffn_conv_b", "w_down", "g_final")
RPB_N = NH * (2 * NA_ROWS - 1) * (4 * NA_ROWS - 1)
RPB_PAD = -(-RPB_N // LANES) * LANES
SMALL = (("c_ctx", D), ("b_mod", 6 * D), ("g_norm1", D), ("rpb", RPB_PAD), ("conv_b", DC), ("ln_g", DC), ("ln_b", DC),
         ("g_norm2", D), ("ffn_conv_b", 2 * DFF), ("g_final", D))
PACK = (("dmod", 6 * D), ("dmod_c", 2 * D), ("g_norm1", D), ("g_norm1_ctx", D), ("g_norm2", D), ("g_final", D),
        ("conv_b", DC), ("ln_g", DC), ("ln_b", DC), ("ffn_conv_b", 2 * DFF), ("ffn_conv_w", 3 * 2 * DFF),
        ("conv_w", CW * DC), ("rpb_rev", NH * 16 * LANES), ("loss", LANES))
PACK_OFF = {}
_o = 0
for _n, _w in PACK:
    PACK_OFF[_n] = (_o, _w)
    _o += _w
PACK_N = -(-_o // (8 * LANES)) * (8 * LANES)
GATHER_ROWS = 48
FFW_COLS = 2 * DFF // N_CHIPS


def _seg(tot, name):
    off, w = PACK_OFF[name]
    return tot[:, off:off + w]


def kernel(x, c, ctx, c_ctx, w_mod, b_mod, g_norm1, w_in, rpb, conv_w, conv_b, ln_g, ln_b, w_out, g_norm2, w_up, ffn_conv_w, ffn_conv_b, w_down, g_final, loss_target, m_c_ctx, m_w_mod, m_b_mod, m_g_norm1, m_w_in, m_rpb, m_conv_w, m_conv_b, m_ln_g, m_ln_b, m_w_out, m_g_norm2, m_w_up, m_ffn_conv_w, m_ffn_conv_b, m_w_down, m_g_final, v_c_ctx, v_w_mod, v_b_mod, v_g_norm1, v_w_in, v_rpb, v_conv_w, v_conv_b, v_ln_g, v_ln_b, v_w_out, v_g_norm2, v_w_up, v_ffn_conv_w, v_ffn_conv_b, v_w_down, v_g_final):
    w = dict(c_ctx=c_ctx, w_mod=w_mod, b_mod=b_mod, g_norm1=g_norm1, w_in=w_in, rpb=rpb, conv_w=conv_w, conv_b=conv_b,
             ln_g=ln_g, ln_b=ln_b, w_out=w_out, g_norm2=g_norm2, w_up=w_up, ffn_conv_w=ffn_conv_w, ffn_conv_b=ffn_conv_b,
             w_down=w_down, g_final=g_final)
    mom = dict(c_ctx=m_c_ctx, w_mod=m_w_mod, b_mod=m_b_mod, g_norm1=m_g_norm1, w_in=m_w_in, rpb=m_rpb, conv_w=m_conv_w,
               conv_b=m_conv_b, ln_g=m_ln_g, ln_b=m_ln_b, w_out=m_w_out, g_norm2=m_g_norm2, w_up=m_w_up,
               ffn_conv_w=m_ffn_conv_w, ffn_conv_b=m_ffn_conv_b, w_down=m_w_down, g_final=m_g_final)
    var = dict(c_ctx=v_c_ctx, w_mod=v_w_mod, b_mod=v_b_mod, g_norm1=v_g_norm1, w_in=v_w_in, rpb=v_rpb, conv_w=v_conv_w,
               conv_b=v_conv_b, ln_g=v_ln_g, ln_b=v_ln_b, w_out=v_w_out, g_norm2=v_g_norm2, w_up=v_w_up,
               ffn_conv_w=v_ffn_conv_w, ffn_conv_b=v_ffn_conv_b, w_down=v_w_down, g_final=v_g_final)
    xi, yi, ci = lax.axis_index("x"), lax.axis_index("y"), lax.axis_index("c")
    dev = 4 * xi + 2 * yi + ci
    chip = 2 * xi + yi
    core = ci.astype(jnp.int32).reshape(1)
    c_ctx2 = c_ctx.reshape(1, D)
    g_final2 = g_final.reshape(1, D)

    blk = jnp.concatenate([
        jnp.pad(c, ((0, 7), (0, FFW_COLS - D))),
        jnp.pad(ffn_conv_w[0], ((0, 5), (0, 0))),
        jnp.pad(conv_w[0], ((0, 1), (0, FFW_COLS - DC // N_CHIPS))),
    ], axis=0)
    got = _gather_small(blk, "gather_cond").reshape(8, GATHER_ROWS, FFW_COLS)
    cond = jnp.concatenate([got[:, 0, :D], c_ctx2, jnp.zeros((COND_ROWS - 9, D), F32)], axis=0)
    ffn_w_all = jnp.concatenate([got[2 * j, 8:11, :] for j in range(N_CHIPS)], axis=1)
    conv_w_all = jnp.concatenate([got[2 * j, 16:16 + CW, :DC // N_CHIPS] for j in range(N_CHIPS)], axis=1)

    b_sh = lax.dynamic_slice(b_mod, (0, chip * MOD_COLS), (1, MOD_COLS))
    mods = _gather_small(_mod_shard(cond, w_mod[0], b_sh), "gather_mod").reshape(8, COND_ROWS, MOD_COLS)
    mod_all = jnp.concatenate([mods[2 * j] for j in range(N_CHIPS)], axis=1)
    mod_me = lax.dynamic_slice(mod_all, (dev, 0), (1, 6 * D)).reshape(6, D)
    mod_c = mod_all[8:9, :2 * D].reshape(2, D)

    whole = _gather_weights([w[n][0].astype(MXU_DTYPE) for n in BIG_NAMES])

    rpb_rev = jnp.pad(rpb[0][:, :, ::-1], ((0, 0), (0, 1), (48, LANES - 48 - (4 * NA_ROWS - 1)))).reshape(NH * 16, LANES)
    vec = dict(g_norm1=g_norm1, g_norm2=g_norm2, g_final=g_final2, conv_w=conv_w_all, conv_b=conv_b, ln_g=ln_g, ln_b=ln_b,
               ffn_conv_w=ffn_w_all, ffn_conv_b=ffn_conv_b)
    loss_p, grad_x, d_in, d_out, d_up, d_down, small = _local_step(
        x[0], ctx[0], loss_target[0], mod_me, mod_c, vec, *whole, rpb_rev)

    big_grads = dict(zip(BIG_NAMES, _reduce_scatter([d_in, d_out, d_up, d_down], core)))

    parts = dict(dmod=small["dmod"], dmod_c=small["dmod_c"], g_norm1=[small["g_norm1"][0]], g_norm1_ctx=[small["g_norm1"][1]],
                 g_norm2=[small["g_norm2"]], g_final=[small["g_final"]], conv_b=[small["conv_b"]], ln_g=[small["ln_g"]],
                 ln_b=[small["ln_b"]], ffn_conv_b=small["ffn_conv_b"],
                 ffn_conv_w=[jnp.concatenate(small["ffn_conv_w"], axis=1).reshape(1, 3 * 2 * DFF)],
                 conv_w=[small["conv_w"].reshape(1, CW * DC)], rpb_rev=[small["rpb_rev"].reshape(1, NH * 16 * LANES)],
                 loss=[loss_p])
    flat = [p for n, _ in PACK for p in parts[n]]
    flat.append(jnp.zeros((1, PACK_N - _o), F32))
    pack = jnp.concatenate(flat, axis=1).reshape(8, PACK_N // 8)
    packs = _gather_small(pack, "gather_small_grads").reshape(8, PACK_N)
    o_dmod, o_dmc, o_g1, o_g1c = PACK_OFF["dmod"][0], PACK_OFF["dmod_c"][0], PACK_OFF["g_norm1"][0], PACK_OFF["g_norm1_ctx"][0]
    tot, g_b_mod, g_g1 = _sum_devices(packs, [(o_dmod, 6 * D, o_dmc, 2 * D), (o_g1, D, o_g1c, D)])

    dmc = jnp.pad(_seg(tot, "dmod_c"), ((0, 0), (0, 4 * D)))
    dmod_all = jnp.concatenate([packs[:, o_dmod:o_dmod + 6 * D], dmc, jnp.zeros((COND_ROWS - 9, 6 * D), F32)], axis=0)
    g_w_mod = _mod_weight_grad(cond, lax.dynamic_slice(dmod_all, (0, chip * MOD_COLS), (COND_ROWS, MOD_COLS)))
    dmc_sh = jnp.pad(lax.dynamic_slice(dmc, (0, chip * MOD_COLS), (1, MOD_COLS)), ((0, 7), (0, 0)))
    cparts = _gather_small(_cond_grad_partial(dmc_sh, w_mod[0]), "gather_cond_grad").reshape(8, 8, D)
    g_c_ctx = _cond_grad(jnp.concatenate([cparts[2 * j, 0:1] for j in range(N_CHIPS)], axis=0), c_ctx2)

    g_rpb = _seg(tot, "rpb_rev").reshape(NH, 16, LANES)[:, :2 * NA_ROWS - 1, 48:48 + 4 * NA_ROWS - 1][:, :, ::-1]
    g_conv_w = lax.dynamic_slice(_seg(tot, "conv_w").reshape(CW, DC), (0, chip * (DC // N_CHIPS)), (CW, DC // N_CHIPS))
    g_ffn_w = lax.dynamic_slice(_seg(tot, "ffn_conv_w").reshape(3, 2 * DFF), (0, chip * FFW_COLS), (3, FFW_COLS))
    loss = _seg(tot, "loss")[0, 0]

    grads = dict(
        c_ctx=g_c_ctx.reshape(D), w_mod=g_w_mod[None], b_mod=g_b_mod, g_norm1=g_g1, w_in=big_grads["w_in"][None],
        rpb=g_rpb[None], conv_w=g_conv_w[None], conv_b=_seg(tot, "conv_b"), ln_g=_seg(tot, "ln_g"), ln_b=_seg(tot, "ln_b"),
        w_out=big_grads["w_out"][None], g_norm2=_seg(tot, "g_norm2"), w_up=big_grads["w_up"][None], ffn_conv_w=g_ffn_w[None],
        ffn_conv_b=_seg(tot, "ffn_conv_b"), w_down=big_grads["w_down"][None], g_final=_seg(tot, "g_final").reshape(D),
    )

    delta, new_m, new_v = {}, {}, {}
    for n in ("w_mod", "w_in", "w_out", "w_up", "w_down", "conv_w", "ffn_conv_w"):
        shp = w[n].shape
        two = lambda a: a.reshape(shp[1], shp[2])
        d_, m_, v_ = _adamw(two(w[n]), two(grads[n]), two(mom[n]), two(var[n]), "adamw_" + n)
        delta[n], new_m[n], new_v[n] = d_.reshape(shp), m_.reshape(shp), v_.reshape(shp)

    def packed(tree):
        cols = []
        for n, width in SMALL:
            a = tree[n].reshape(1, -1)
            cols.append(jnp.pad(a, ((0, 0), (0, width - a.shape[1]))))
        return jnp.concatenate(cols, axis=1)

    d_, m_, v_ = _adamw(packed(w), packed(grads), packed(mom), packed(var), "adamw_small")
    off = 0
    for n, width in SMALL:
        size = int(np.prod(w[n].shape))
        for src, dst in ((d_, delta), (m_, new_m), (v_, new_v)):
            dst[n] = src[0, off:off + size].reshape(w[n].shape)
        off += width

    return (loss, grad_x[None], *[grads[n] for n in WEIGHTS], *[delta[n] for n in WEIGHTS],
            *[new_m[n] for n in WEIGHTS], *[new_v[n] for n in WEIGHTS])
```

```python
import functools

import jax
import jax.numpy as jnp
from jax import lax
from jax.experimental import pallas as pl
from jax.experimental.pallas import tpu as pltpu

F32 = jnp.float32
BF16 = jnp.bfloat16
MXU_DTYPE = jnp.bfloat16

D = 1024
CTX = 256
GW = 64
DA = 512
NH = 8
HD = 64
DC = 512
CW = 31
DFF = 2816
NIN = 3 * DA + 2 * DC
EPS = 1e-6
SCALE = HD ** -0.5
NEG = -1e30
NA_ROWS = 8
PAIR_ROWS = NA_ROWS + 1
TAB_BLOCKS = 17
LANES = 128
VMEM_LIMIT = 56 * 1024 * 1024

ADAM_LR = 0.001
ADAM_B1 = 0.9
ADAM_B2 = 0.999
ADAM_EPS = 1e-08
ADAM_WD = 0.01
ADAM_STEP = 10

MESH = pl.DeviceIdType.MESH


def _pallas(body, *, name, semantics=None, vmem=VMEM_LIMIT, prefetch=0, **kw):
    params = dict(vmem_limit_bytes=vmem)
    if semantics is not None:
        params["dimension_semantics"] = semantics
    if prefetch:
        kw["grid_spec"] = pltpu.PrefetchScalarGridSpec(
            num_scalar_prefetch=prefetch, grid=kw.pop("grid"), in_specs=kw.pop("in_specs"), out_specs=kw.pop("out_specs"),
            scratch_shapes=kw.pop("scratch_shapes", ()))
    return pl.pallas_call(body, name=name, compiler_params=pltpu.CompilerParams(**params), **kw)


def _sds(shape, dtype):
    return jax.ShapeDtypeStruct(shape, dtype)


def _vec_spec(n):
    return pl.BlockSpec((1, n), lambda *_: (0, 0))


def _colsum8(x):
    t, n = x.shape
    return jnp.sum(x.reshape(t // 8, 8, n), axis=0)


def _sigmoid(x):
    return 1.0 / (1.0 + jnp.exp(-x))


def _pieces(arrs, tile):
    lo, out = 0, []
    for a in arrs:
        nt = a.shape[1] // tile
        assert nt * tile == a.shape[1], (a.shape, tile)
        out.append((lo, nt))
        lo += nt
    return out


def _mm(a, b, *, mode, m, n, k, tm, tn, tk, out_dtype, name, a_off=(0, 0), b_off=(0, 0)):
    a_list = list(a) if isinstance(a, (list, tuple)) else [a]
    b_list = list(b) if isinstance(b, (list, tuple)) else [b]
    assert m % tm == 0 and n % tn == 0 and k % tk == 0, (name, m, n, k, tm, tn, tk)
    gi, gj, nk = m // tm, n // tn, k // tk
    a_tile = tm if mode == "tn" else tk
    a_pc = _pieces(a_list, a_tile) if len(a_list) > 1 else [(0, 1 << 30)]
    if mode == "nt":
        assert len(b_list) == 1
    b_pc = _pieces(b_list, tn) if len(b_list) > 1 else [(0, 1 << 30)]
    dims = {"nn": (((1,), (0,)), ((), ())), "nt": (((1,), (1,)), ((), ())), "tn": (((0,), (0,)), ((), ()))}[mode]

    def a_spec(lo, cnt):
        def loc(idx):
            return idx + a_off[1] if len(a_list) == 1 else jnp.clip(idx - lo, 0, cnt - 1)
        if mode == "tn":
            return pl.BlockSpec((tk, tm), lambda i, j, kk: (kk + a_off[0], loc(i)))
        return pl.BlockSpec((tm, tk), lambda i, j, kk: (i + a_off[0], loc(kk)))

    def b_spec(lo, cnt):
        def loc(idx):
            return idx + b_off[1] if len(b_list) == 1 else jnp.clip(idx - lo, 0, cnt - 1)
        if mode == "nt":
            return pl.BlockSpec((tn, tk), lambda i, j, kk: (j + b_off[0], kk + b_off[1]))
        return pl.BlockSpec((tk, tn), lambda i, j, kk: (kk + b_off[0], loc(j)))

    na, nb = len(a_list), len(b_list)

    def body(*refs):
        a_refs, b_refs, o_ref = refs[:na], refs[na:na + nb], refs[na + nb]
        acc = refs[na + nb + 1] if nk > 1 else None
        i, j, kk = pl.program_id(0), pl.program_id(1), pl.program_id(2)
        a_idx = i if mode == "tn" else kk

        def step(ar, br):
            p = lax.dot_general(ar[...].astype(MXU_DTYPE), br[...].astype(MXU_DTYPE), dims,
                                preferred_element_type=F32)
            if nk == 1:
                o_ref[...] = p.astype(out_dtype)
                return

            @pl.when(kk == 0)
            def _():
                acc[...] = p

            @pl.when(kk > 0)
            def _():
                acc[...] += p

            @pl.when(kk == nk - 1)
            def _():
                o_ref[...] = acc[...].astype(out_dtype)

        for pa, (alo, acnt) in enumerate(a_pc):
            for pb, (blo, bcnt) in enumerate(b_pc):
                if na == 1 and nb == 1:
                    step(a_refs[0], b_refs[0])
                else:
                    cond = (a_idx >= alo) & (a_idx < alo + acnt) & (j >= blo) & (j < blo + bcnt)
                    pl.when(cond)(functools.partial(step, a_refs[pa], b_refs[pb]))

    return _pallas(
        body, name=name, grid=(gi, gj, nk),
        in_specs=[a_spec(*p) for p in a_pc] + [b_spec(*p) for p in b_pc],
        out_specs=pl.BlockSpec((tm, tn), lambda i, j, kk: (i, j)),
        out_shape=_sds((m, n), out_dtype),
        scratch_shapes=[pltpu.VMEM((tm, tn), F32)] if nk > 1 else [],
        semantics=("parallel", "parallel", "arbitrary"),
    )(*a_list, *b_list)


ROW_TILE = 256


def _rmsmod_fwd(x, ctx, g, sc, sh, csc, csh):
    s = x.shape[0]
    nt = s // ROW_TILE
    assert ctx.shape[0] == ROW_TILE

    def body(x_ref, c_ref, g_ref, sc_ref, sh_ref, csc_ref, csh_ref, o_ref):
        is_ctx = pl.program_id(0) == nt
        xv = jnp.where(is_ctx, c_ref[...], x_ref[...])
        scv = jnp.where(is_ctx, csc_ref[...], sc_ref[...])
        shv = jnp.where(is_ctx, csh_ref[...], sh_ref[...])
        r = lax.rsqrt(jnp.mean(xv * xv, axis=-1, keepdims=True) + EPS)
        y = xv * r * g_ref[...]
        o_ref[...] = (y * (1.0 + scv) + shv).astype(o_ref.dtype)

    return _pallas(
        body, name="rmsmod1_fwd", grid=(nt + 1,),
        in_specs=[pl.BlockSpec((ROW_TILE, D), lambda i: (jnp.minimum(i, nt - 1), 0)),
                  pl.BlockSpec((ROW_TILE, D), lambda i: (0, 0))] + [_vec_spec(D)] * 5,
        out_specs=pl.BlockSpec((ROW_TILE, D), lambda i: (i, 0)),
        out_shape=_sds((s + CTX, D), MXU_DTYPE),
        semantics=("arbitrary",),
    )(x, ctx, g, sc, sh, csc, csh)


def _resid_rmsmod_fwd(x, y, gt, g, sc, sh):
    s = x.shape[0]

    def body(x_ref, y_ref, gt_ref, g_ref, sc_ref, sh_ref, x1_ref, h_ref):
        x1 = x_ref[...] + gt_ref[...] * y_ref[...]
        x1_ref[...] = x1
        r = lax.rsqrt(jnp.mean(x1 * x1, axis=-1, keepdims=True) + EPS)
        h_ref[...] = ((x1 * r * g_ref[...]) * (1.0 + sc_ref[...]) + sh_ref[...]).astype(h_ref.dtype)

    row = pl.BlockSpec((ROW_TILE, D), lambda i: (i, 0))
    return _pallas(
        body, name="resid_rmsmod2_fwd", grid=(s // ROW_TILE,),
        in_specs=[row, row] + [_vec_spec(D)] * 4,
        out_specs=[row, row],
        out_shape=[_sds((s, D), F32), _sds((s, D), MXU_DTYPE)],
        semantics=("parallel",),
    )(x, y, gt, g, sc, sh)


def _final_fwd_bwd(x1, z, gt2, gf, tgt):
    s = x1.shape[0]
    nt = s // ROW_TILE

    def body(x1_ref, z_ref, gt_ref, gf_ref, t_ref, dx2_ref, dz_ref, loss_ref, dgt_ref, dgf_ref, a_loss, a_gt, a_gf):
        i = pl.program_id(0)

        @pl.when(i == 0)
        def _():
            a_loss[...] = jnp.zeros_like(a_loss)
            a_gt[...] = jnp.zeros_like(a_gt)
            a_gf[...] = jnp.zeros_like(a_gf)

        zv = z_ref[...]
        gt = gt_ref[...]
        gf_ = gf_ref[...]
        x2 = x1_ref[...] + gt * zv
        r = lax.rsqrt(jnp.mean(x2 * x2, axis=-1, keepdims=True) + EPS)
        xn = x2 * r
        e = xn * gf_ - t_ref[...]
        a_loss[...] += _colsum8(e * e)
        dyo = e * (1.0 / D)
        a_gf[...] += _colsum8(dyo * xn)
        gdy = gf_ * dyo
        dx2 = r * gdy - xn * (r * r) * jnp.mean(x2 * gdy, axis=-1, keepdims=True)
        dx2_ref[...] = dx2
        dz_ref[...] = (gt * dx2).astype(dz_ref.dtype)
        a_gt[...] += _colsum8(dx2 * zv)

        @pl.when(i == nt - 1)
        def _():
            tot = jnp.sum(jnp.sum(a_loss[...], axis=0, keepdims=True), axis=1, keepdims=True) * (0.5 / D)
            loss_ref[...] = jnp.broadcast_to(tot, loss_ref.shape)
            dgt_ref[...] = jnp.sum(a_gt[...], axis=0, keepdims=True)
            dgf_ref[...] = jnp.sum(a_gf[...], axis=0, keepdims=True)

    row = pl.BlockSpec((ROW_TILE, D), lambda i: (i, 0))
    return _pallas(
        body, name="final_norm_loss", grid=(nt,),
        in_specs=[row, row, _vec_spec(D), _vec_spec(D), row],
        out_specs=[row, row, _vec_spec(LANES), _vec_spec(D), _vec_spec(D)],
        out_shape=[_sds((s, D), F32), _sds((s, D), MXU_DTYPE), _sds((1, LANES), F32), _sds((1, D), F32), _sds((1, D), F32)],
        scratch_shapes=[pltpu.VMEM((8, D), F32)] * 3,
        semantics=("arbitrary",),
    )(x1, z, gt2, gf, tgt)


def _rmsmod_bwd(xin, dh, g, sc, *, name, dh_row_off=0, add=None, resid=None):
    s = xin.shape[0]
    nt = s // ROW_TILE
    want_dx = add is not None
    assert resid is None or want_dx

    def body(*refs):
        it = iter(refs)
        x_ref, dh_ref, g_ref, sc_ref = next(it), next(it), next(it), next(it)
        add_ref = next(it) if want_dx else None
        gt_ref, y_ref = (next(it), next(it)) if resid is not None else (None, None)
        dsh_ref, dsc_ref, dg_ref = next(it), next(it), next(it)
        dx_ref = next(it) if want_dx else None
        dy_ref, dgt_ref = (next(it), next(it)) if resid is not None else (None, None)
        a_sh, a_sc, a_g = next(it), next(it), next(it)
        a_gt = next(it) if resid is not None else None
        i = pl.program_id(0)

        @pl.when(i == 0)
        def _():
            a_sh[...] = jnp.zeros_like(a_sh)
            a_sc[...] = jnp.zeros_like(a_sc)
            a_g[...] = jnp.zeros_like(a_g)
            if a_gt is not None:
                a_gt[...] = jnp.zeros_like(a_gt)

        xv = x_ref[...]
        dhv = dh_ref[...]
        gv = g_ref[...]
        r = lax.rsqrt(jnp.mean(xv * xv, axis=-1, keepdims=True) + EPS)
        xn = xv * r
        a_sh[...] += _colsum8(dhv)
        a_sc[...] += _colsum8(dhv * (xn * gv))
        dn = dhv * (1.0 + sc_ref[...])
        a_g[...] += _colsum8(dn * xn)
        if want_dx:
            gdn = gv * dn
            dx = add_ref[...] + r * gdn - xn * (r * r) * jnp.mean(xv * gdn, axis=-1, keepdims=True)
            dx_ref[...] = dx
            if resid is not None:
                dy_ref[...] = (gt_ref[...] * dx).astype(dy_ref.dtype)
                a_gt[...] += _colsum8(dx * y_ref[...])

        @pl.when(i == nt - 1)
        def _():
            dsh_ref[...] = jnp.sum(a_sh[...], axis=0, keepdims=True)
            dsc_ref[...] = jnp.sum(a_sc[...], axis=0, keepdims=True)
            dg_ref[...] = jnp.sum(a_g[...], axis=0, keepdims=True)
            if a_gt is not None:
                dgt_ref[...] = jnp.sum(a_gt[...], axis=0, keepdims=True)

    row = pl.BlockSpec((ROW_TILE, D), lambda i: (i, 0))
    ins = [xin, dh, g, sc]
    in_specs = [row, pl.BlockSpec((ROW_TILE, D), lambda i: (i + dh_row_off, 0)), _vec_spec(D), _vec_spec(D)]
    out_specs = [_vec_spec(D)] * 3
    out_shape = [_sds((1, D), F32)] * 3
    scratch = [pltpu.VMEM((8, D), F32)] * 3
    if want_dx:
        ins.append(add)
        in_specs.append(row)
        out_specs.append(row)
        out_shape.append(_sds((s, D), F32))
    if resid is not None:
        ins += [resid[0], resid[1]]
        in_specs += [_vec_spec(D), row]
        out_specs += [row, _vec_spec(D)]
        out_shape += [_sds((s, D), MXU_DTYPE), _sds((1, D), F32)]
        scratch.append(pltpu.VMEM((8, D), F32))
    return _pallas(body, name=name, grid=(nt,), in_specs=in_specs, out_specs=out_specs, out_shape=out_shape,
                   scratch_shapes=scratch, semantics=("arbitrary",))(*ins)


FF_TILE = 128
FF_CHUNK = 128
HALO = 8


def _shift3(pad_ref, r0, ch):
    win = pad_ref[pl.ds(r0, ch + 2 * HALO), :]
    prev = pltpu.roll(win, 1, 0)[HALO:HALO + ch]
    cur = win[HALO:HALO + ch]
    nxt = pltpu.roll(win, ch + 2 * HALO - 1, 0)[HALO:HALO + ch]
    return prev, cur, nxt


def _fill_padded(pad_ref, src_ref, s, ch, halo):
    zeros = jnp.zeros((halo, pad_ref.shape[1]), F32)
    pad_ref[0:halo, :] = zeros
    pad_ref[s + halo:s + 2 * halo, :] = zeros

    def cp(c, carry):
        r0 = pl.multiple_of(c * ch, ch)
        pad_ref[pl.ds(r0 + halo, ch), :] = src_ref[pl.ds(r0, ch), :].astype(F32)
        return carry

    lax.fori_loop(0, s // ch, cp, 0)


def _ffn_act_fwd(u, w, b):
    s = u.shape[0]
    nj = DFF // FF_TILE
    ch = FF_CHUNK

    def body(ug_ref, uv_ref, wg_ref, wv_ref, bg_ref, bv_ref, f_ref, gpad, vpad):
        _fill_padded(gpad, ug_ref, s, ch, HALO)
        _fill_padded(vpad, uv_ref, s, ch, HALO)

        def conv(pad, w_ref, b_ref, r0):
            prev, cur, nxt = _shift3(pad, r0, ch)
            return w_ref[0:1, :] * prev + w_ref[1:2, :] * cur + w_ref[2:3, :] * nxt + b_ref[...]

        def step(c, carry):
            r0 = pl.multiple_of(c * ch, ch)
            gc = conv(gpad, wg_ref, bg_ref, r0)
            vc = conv(vpad, wv_ref, bv_ref, r0)
            f_ref[pl.ds(r0, ch), :] = (gc * _sigmoid(gc) * vc).astype(f_ref.dtype)
            return carry

        lax.fori_loop(0, s // ch, step, 0)

    col = lambda off: pl.BlockSpec((s, FF_TILE), lambda j: (0, j + off))
    wsp = lambda off: pl.BlockSpec((3, FF_TILE), lambda j: (0, j + off))
    bsp = lambda off: pl.BlockSpec((1, FF_TILE), lambda j: (0, j + off))
    return _pallas(
        body, name="ffn_act_fwd", grid=(nj,),
        in_specs=[col(0), col(nj), wsp(0), wsp(nj), bsp(0), bsp(nj)],
        out_specs=col(0), out_shape=_sds((s, DFF), MXU_DTYPE),
        scratch_shapes=[pltpu.VMEM((s + 2 * HALO, FF_TILE), F32)] * 2,
        semantics=("parallel",),
    )(u, u, w, w, b, b)


def _ffn_act_bwd(u, df, w, b):
    s = u.shape[0]
    nj = DFF // FF_TILE
    ch = FF_CHUNK

    def body(ug_ref, uv_ref, df_ref, wg_ref, wv_ref, bg_ref, bv_ref,
             dug_ref, duv_ref, dwg_ref, dwv_ref, dbg_ref, dbv_ref, gpad, vpad, dgpad, dvpad, acc):
        _fill_padded(gpad, ug_ref, s, ch, HALO)
        _fill_padded(vpad, uv_ref, s, ch, HALO)
        zeros = jnp.zeros((HALO, FF_TILE), F32)
        for p in (dgpad, dvpad):
            p[0:HALO, :] = zeros
            p[s + HALO:s + 2 * HALO, :] = zeros
        acc[...] = jnp.zeros_like(acc)

        def step(c, carry):
            r0 = pl.multiple_of(c * ch, ch)
            gs = _shift3(gpad, r0, ch)
            vs = _shift3(vpad, r0, ch)
            gc = wg_ref[0:1, :] * gs[0] + wg_ref[1:2, :] * gs[1] + wg_ref[2:3, :] * gs[2] + bg_ref[...]
            vc = wv_ref[0:1, :] * vs[0] + wv_ref[1:2, :] * vs[1] + wv_ref[2:3, :] * vs[2] + bv_ref[...]
            sg = _sigmoid(gc)
            dfv = df_ref[pl.ds(r0, ch), :]
            dgc = dfv * vc * (sg * (1.0 + gc * (1.0 - sg)))
            dvc = dfv * (gc * sg)
            dgpad[pl.ds(r0 + HALO, ch), :] = dgc
            dvpad[pl.ds(r0 + HALO, ch), :] = dvc
            for t in range(3):
                acc[8 * t:8 * t + 8, :] += _colsum8(dgc * gs[t])
                acc[24 + 8 * t:32 + 8 * t, :] += _colsum8(dvc * vs[t])
            acc[48:56, :] += _colsum8(dgc)
            acc[56:64, :] += _colsum8(dvc)
            return carry

        lax.fori_loop(0, s // ch, step, 0)

        def step2(c, carry):
            r0 = pl.multiple_of(c * ch, ch)
            for pad, w_ref, o_ref in ((dgpad, wg_ref, dug_ref), (dvpad, wv_ref, duv_ref)):
                prev, cur, nxt = _shift3(pad, r0, ch)
                o_ref[pl.ds(r0, ch), :] = (w_ref[0:1, :] * nxt + w_ref[1:2, :] * cur + w_ref[2:3, :] * prev).astype(o_ref.dtype)
            return carry

        lax.fori_loop(0, s // ch, step2, 0)
        for t in range(3):
            dwg_ref[t:t + 1, :] = jnp.sum(acc[8 * t:8 * t + 8, :], axis=0, keepdims=True)
            dwv_ref[t:t + 1, :] = jnp.sum(acc[24 + 8 * t:32 + 8 * t, :], axis=0, keepdims=True)
        dbg_ref[...] = jnp.sum(acc[48:56, :], axis=0, keepdims=True)
        dbv_ref[...] = jnp.sum(acc[56:64, :], axis=0, keepdims=True)

    col = lambda off: pl.BlockSpec((s, FF_TILE), lambda j: (0, j + off))
    wsp = lambda off: pl.BlockSpec((3, FF_TILE), lambda j: (0, j + off))
    bsp = lambda off: pl.BlockSpec((1, FF_TILE), lambda j: (0, j + off))
    return _pallas(
        body, name="ffn_act_bwd", grid=(nj,),
        in_specs=[col(0), col(nj), col(0), wsp(0), wsp(nj), bsp(0), bsp(nj)],
        out_specs=[col(0), col(0), wsp(0), wsp(0), bsp(0), bsp(0)],
        out_shape=[_sds((s, DFF), MXU_DTYPE)] * 2 + [_sds((3, DFF), F32)] * 2 + [_sds((1, DFF), F32)] * 2,
        scratch_shapes=[pltpu.VMEM((s + 2 * HALO, FF_TILE), F32)] * 4 + [pltpu.VMEM((64, FF_TILE), F32)],
        semantics=("parallel",),
    )(u, u, df, w, w, b, b)


CONV_CHUNK = 64
CONV_HALO = 16
CONV_WIN = CONV_CHUNK + 2 * CONV_HALO


def _tap(win, k):
    off = CONV_HALO - CW // 2 + k
    return pltpu.roll(win, CONV_WIN - off, 0)[0:CONV_CHUNK]


def _glu_into(pad_ref, a_ref, g_ref, s):
    zeros = jnp.zeros((CONV_HALO, LANES), F32)
    pad_ref[0:CONV_HALO, :] = zeros
    pad_ref[s + CONV_HALO:s + 2 * CONV_HALO, :] = zeros

    def cp(c, carry):
        r0 = pl.multiple_of(c * ROW_TILE, ROW_TILE)
        pad_ref[pl.ds(r0 + CONV_HALO, ROW_TILE), :] = a_ref[pl.ds(r0, ROW_TILE), :] * _sigmoid(g_ref[pl.ds(r0, ROW_TILE), :])
        return carry

    lax.fori_loop(0, s // ROW_TILE, cp, 0)


def _conf_conv_fwd(ag, conv_w, conv_b):
    s = ag.shape[0]
    nc = DC // LANES

    def body(a_ref, g_ref, w_ref, b_ref, o_ref, upad):
        _glu_into(upad, a_ref, g_ref, s)

        def step(c, carry):
            r0 = pl.multiple_of(c * CONV_CHUNK, CONV_CHUNK)
            win = upad[pl.ds(r0, CONV_WIN), :]
            acc = jnp.broadcast_to(b_ref[...], (CONV_CHUNK, LANES))
            for k in range(CW):
                acc = acc + w_ref[k:k + 1, :] * _tap(win, k)
            o_ref[pl.ds(r0, CONV_CHUNK), :] = acc
            return carry

        lax.fori_loop(0, s // CONV_CHUNK, step, 0)

    col = lambda off: pl.BlockSpec((s, LANES), lambda c: (0, c + off))
    return _pallas(
        body, name="conf_conv_fwd", grid=(nc,),
        in_specs=[col(0), col(nc), pl.BlockSpec((CW, LANES), lambda c: (0, c)), pl.BlockSpec((1, LANES), lambda c: (0, c))],
        out_specs=col(0), out_shape=_sds((s, DC), F32),
        scratch_shapes=[pltpu.VMEM((s + 2 * CONV_HALO, LANES), F32)],
        semantics=("parallel",),
    )(ag, ag, conv_w, conv_b)


def _ln_stats(x):
    mu = jnp.mean(x, axis=-1, keepdims=True)
    xc = x - mu
    var = jnp.mean(xc * xc, axis=-1, keepdims=True)
    rstd = lax.rsqrt(var + EPS)
    return xc * rstd, rstd


def _conf_ln_fwd(u1, ln_g, ln_b, ycat):
    s = u1.shape[0]

    def body(u_ref, g_ref, b_ref, ycat_ref, o_ref):
        del ycat_ref
        xhat, _ = _ln_stats(u_ref[...])
        y = xhat * g_ref[...] + b_ref[...]
        o_ref[...] = (y * _sigmoid(y)).astype(o_ref.dtype)

    return _pallas(
        body, name="conf_ln_fwd", grid=(s // ROW_TILE,),
        in_specs=[pl.BlockSpec((ROW_TILE, DC), lambda i: (i, 0)), _vec_spec(DC), _vec_spec(DC),
                  pl.BlockSpec(memory_space=pl.ANY)],
        out_specs=pl.BlockSpec((ROW_TILE, DC), lambda i: (i, 1)),
        out_shape=_sds(ycat.shape, ycat.dtype),
        input_output_aliases={3: 0},
        semantics=("parallel",),
    )(u1, ln_g, ln_b, ycat)


def _conf_ln_bwd(dycat, u1, ln_g, ln_b):
    s = u1.shape[0]
    nt = s // ROW_TILE

    def body(dy_ref, u_ref, g_ref, b_ref, du_ref, dg_ref, db_ref, a_g, a_b):
        i = pl.program_id(0)

        @pl.when(i == 0)
        def _():
            a_g[...] = jnp.zeros_like(a_g)
            a_b[...] = jnp.zeros_like(a_b)

        xhat, rstd = _ln_stats(u_ref[...])
        gv = g_ref[...]
        y = xhat * gv + b_ref[...]
        sg = _sigmoid(y)
        dyl = dy_ref[...] * (sg * (1.0 + y * (1.0 - sg)))
        a_g[...] += _colsum8(dyl * xhat)
        a_b[...] += _colsum8(dyl)
        dxh = dyl * gv
        du_ref[...] = rstd * (dxh - jnp.mean(dxh, axis=-1, keepdims=True)
                              - xhat * jnp.mean(dxh * xhat, axis=-1, keepdims=True))

        @pl.when(i == nt - 1)
        def _():
            dg_ref[...] = jnp.sum(a_g[...], axis=0, keepdims=True)
            db_ref[...] = jnp.sum(a_b[...], axis=0, keepdims=True)

    return _pallas(
        body, name="conf_ln_bwd", grid=(nt,),
        in_specs=[pl.BlockSpec((ROW_TILE, DC), lambda i: (i, 1)), pl.BlockSpec((ROW_TILE, DC), lambda i: (i, 0)),
                  _vec_spec(DC), _vec_spec(DC)],
        out_specs=[pl.BlockSpec((ROW_TILE, DC), lambda i: (i, 0)), _vec_spec(DC), _vec_spec(DC)],
        out_shape=[_sds((s, DC), F32), _sds((1, DC), F32), _sds((1, DC), F32)],
        scratch_shapes=[pltpu.VMEM((8, DC), F32)] * 2,
        semantics=("arbitrary",),
    )(dycat, u1, ln_g, ln_b)


def _conf_conv_bwd(ag, du1, conv_w, rows_out):
    s = ag.shape[0]
    nc = DC // LANES

    def body(a_ref, g_ref, d_ref, w_ref, da_ref, dg_ref, dw_ref, db_ref, upad, dpad, acc):
        _glu_into(upad, a_ref, g_ref, s)
        _fill_padded(dpad, d_ref, s, ROW_TILE, CONV_HALO)
        acc[...] = jnp.zeros_like(acc)

        def step(c, carry):
            r0 = pl.multiple_of(c * CONV_CHUNK, CONV_CHUNK)
            uwin = upad[pl.ds(r0, CONV_WIN), :]
            dwin = dpad[pl.ds(r0, CONV_WIN), :]
            dcur = dwin[CONV_HALO:CONV_HALO + CONV_CHUNK]
            du0 = jnp.zeros((CONV_CHUNK, LANES), F32)
            for k in range(CW):
                du0 = du0 + w_ref[k:k + 1, :] * _tap(dwin, CW - 1 - k)
                acc[8 * k:8 * k + 8, :] += _colsum8(dcur * _tap(uwin, k))
            acc[8 * CW:8 * CW + 8, :] += _colsum8(dcur)
            av = a_ref[pl.ds(r0, CONV_CHUNK), :]
            sg = _sigmoid(g_ref[pl.ds(r0, CONV_CHUNK), :])
            da_ref[pl.ds(r0, CONV_CHUNK), :] = (du0 * sg).astype(da_ref.dtype)
            dg_ref[pl.ds(r0, CONV_CHUNK), :] = (du0 * av * (sg * (1.0 - sg))).astype(dg_ref.dtype)
            return carry

        lax.fori_loop(0, s // CONV_CHUNK, step, 0)
        if rows_out > s:
            zeros = jnp.zeros((rows_out - s, LANES), da_ref.dtype)
            da_ref[s:rows_out, :] = zeros
            dg_ref[s:rows_out, :] = zeros
        for k in range(CW):
            dw_ref[k:k + 1, :] = jnp.sum(acc[8 * k:8 * k + 8, :], axis=0, keepdims=True)
        db_ref[...] = jnp.sum(acc[8 * CW:8 * CW + 8, :], axis=0, keepdims=True)

    col = lambda off: pl.BlockSpec((s, LANES), lambda c: (0, c + off))
    ocol = pl.BlockSpec((rows_out, LANES), lambda c: (0, c))
    return _pallas(
        body, name="conf_conv_bwd", grid=(nc,),
        in_specs=[col(0), col(nc), col(0), pl.BlockSpec((CW, LANES), lambda c: (0, c))],
        out_specs=[ocol, ocol, pl.BlockSpec((CW, LANES), lambda c: (0, c)), pl.BlockSpec((1, LANES), lambda c: (0, c))],
        out_shape=[_sds((rows_out, DC), MXU_DTYPE)] * 2 + [_sds((CW, DC), F32), _sds((1, DC), F32)],
        scratch_shapes=[pltpu.VMEM((s + 2 * CONV_HALO, LANES), F32)] * 2 + [pltpu.VMEM((8 * (CW + 1), LANES), F32)],
        semantics=("parallel",),
    )(ag, ag, du1, conv_w)


Q_TILE = 2 * GW
K_WIN = PAIR_ROWS * GW


def _bias_table(rpb_rev):
    def body(p_ref, t_ref):
        kcol = lax.broadcasted_iota(jnp.int32, (GW, LANES), 0)
        lane = lax.broadcasted_iota(jnp.int32, (GW, LANES), 1)
        qcol = lane % GW
        cs = jnp.clip(qcol - NA_ROWS, 0, GW - 2 * NA_ROWS)
        colvalid = (kcol >= cs) & (kcol < cs + 2 * NA_ROWS)
        neg = jnp.full((GW, LANES), NEG, F32)

        def skew(h, ro, shift):
            if ro < 0 or ro >= 2 * NA_ROWS - 1:
                return neg
            row = jnp.broadcast_to(p_ref[h * 16 + ro:h * 16 + ro + 1, :], (GW, LANES))
            return pltpu.roll(row, shift, 1, stride=1, stride_axis=0)

        for h in range(NH):
            for b in range(TAB_BLOCKS):
                val = jnp.where(lane < GW, skew(h, b - 1, GW + 1), skew(h, b - 2, 1))
                t_ref[h, b * GW:(b + 1) * GW, :] = jnp.where(colvalid, val, neg)

    return _pallas(body, name="attn_bias_table", out_shape=_sds((NH, TAB_BLOCKS * GW, LANES), F32))(rpb_rev)


def _rpb_grad(tt):
    def body(t_ref, o_ref):
        lane = lax.broadcasted_iota(jnp.int32, (GW, LANES), 1)
        si = lax.broadcasted_iota(jnp.int32, (GW, GW), 0)
        ti = lax.broadcasted_iota(jnp.int32, (GW, GW), 1)
        flip = jnp.where(si + ti == GW - 1, 1.0, 0.0).astype(F32)
        o_ref[...] = jnp.zeros_like(o_ref)
        for h in range(NH):
            for ro in range(2 * NA_ROWS - 1):
                lo = t_ref[h, (ro + 1) * GW:(ro + 2) * GW, :]
                hi = t_ref[h, (ro + 2) * GW:(ro + 3) * GW, :]
                g = jnp.where(lane < GW, lo + pltpu.roll(hi, GW, 1), 0.0)
                gf = jnp.dot(flip, g, preferred_element_type=F32, precision=lax.Precision.HIGHEST)
                sk = pltpu.roll(gf, 0, 1, stride=1, stride_axis=0)
                o_ref[h * 16 + ro:h * 16 + ro + 1, :] = jnp.sum(sk, axis=0, keepdims=True)

    return _pallas(body, name="attn_rpb_grad", out_shape=_sds((NH * 16, LANES), F32))(tt)


def _attn_geometry(i, rows):
    wsp = jnp.clip(2 * i - NA_ROWS // 2, 0, rows - PAIR_ROWS)
    k0 = pl.multiple_of(wsp * GW, GW)
    t0 = pl.multiple_of((wsp - 2 * i + NA_ROWS) * GW, GW)
    jr = lax.broadcasted_iota(jnp.int32, (K_WIN, Q_TILE), 0) // GW
    rr = lax.broadcasted_iota(jnp.int32, (K_WIN, Q_TILE), 1) // GW
    kr = wsp + jr
    wsr = jnp.clip(2 * i + rr - NA_ROWS // 2, 0, rows - NA_ROWS)
    rowmask = jnp.where((kr >= wsr) & (kr < wsr + NA_ROWS), 0.0, NEG).astype(F32)
    return k0, t0, rowmask


def _two_heads_on_lanes(xt):
    feat = lax.broadcasted_iota(jnp.int32, xt.shape, 0)
    zero = jnp.zeros_like(xt)
    return jnp.concatenate([jnp.where(feat < HD, xt, zero), jnp.where(feat >= HD, xt, zero)], axis=1)


def _two_heads_on_rows(x):
    lane = lax.broadcasted_iota(jnp.int32, x.shape, 1)
    zero = jnp.zeros_like(x)
    return jnp.concatenate([jnp.where(lane < HD, x, zero), jnp.where(lane >= HD, x, zero)], axis=0)


def _pick_heads(x2):
    n = x2.shape[0] // 2
    lane = lax.broadcasted_iota(jnp.int32, (n, LANES), 1)
    return jnp.where(lane < HD, x2[:n], x2[n:])


_TN = (((0,), (0,)), ((), ()))


def _attn_fwd(qkv, tab, s):
    rows = s // GW
    npair = rows // 2

    def body(q_ref, kv_ref, tab_ref, o_ref, lse_ref):
        i = pl.program_id(0)
        k0, t0, rowmask = _attn_geometry(i, rows)
        for p in range(NH // 2):
            cq = slice(p * LANES, (p + 1) * LANES)
            ck = slice(DA + p * LANES, DA + (p + 1) * LANES)
            cv = slice(2 * DA + p * LANES, 2 * DA + (p + 1) * LANES)
            qm2 = _two_heads_on_lanes(q_ref[:, cq].T)
            s_loc = jnp.dot(kv_ref[pl.ds(k0, K_WIN), ck], qm2, preferred_element_type=F32) * SCALE
            s_ctx = jnp.dot(kv_ref[pl.ds(s, CTX), ck], qm2, preferred_element_type=F32) * SCALE
            p_loc, p_ctx = [], []
            for hh in range(2):
                h = 2 * p + hh
                ch = slice(hh * Q_TILE, (hh + 1) * Q_TILE)
                sl = s_loc[:, ch] + tab_ref[h, pl.ds(t0, K_WIN), :] + rowmask
                sc = s_ctx[:, ch]
                m = jnp.maximum(jnp.max(sl, axis=0, keepdims=True), jnp.max(sc, axis=0, keepdims=True))
                el = jnp.exp(sl - m)
                ec = jnp.exp(sc - m)
                l = jnp.sum(el, axis=0, keepdims=True) + jnp.sum(ec, axis=0, keepdims=True)
                inv = 1.0 / l
                lse_ref[h:h + 1, :] = m + jnp.log(l)
                p_loc.append((el * inv).astype(MXU_DTYPE))
                p_ctx.append((ec * inv).astype(MXU_DTYPE))
            o2 = (lax.dot_general(jnp.concatenate(p_loc, axis=1), kv_ref[pl.ds(k0, K_WIN), cv], _TN, preferred_element_type=F32)
                  + lax.dot_general(jnp.concatenate(p_ctx, axis=1), kv_ref[pl.ds(s, CTX), cv], _TN, preferred_element_type=F32))
            o_ref[:, cq] = _pick_heads(o2).astype(o_ref.dtype)

    return _pallas(
        body, name="attn_fwd", grid=(npair,),
        in_specs=[pl.BlockSpec((Q_TILE, DA), lambda i: (i, 0)), pl.BlockSpec(memory_space=pltpu.VMEM),
                  pl.BlockSpec(memory_space=pltpu.VMEM)],
        out_specs=[pl.BlockSpec((Q_TILE, DA), lambda i: (i, 0)), pl.BlockSpec((NH, Q_TILE), lambda i: (0, i))],
        out_shape=[_sds((s, D), MXU_DTYPE), _sds((NH, s), F32)],
        semantics=("arbitrary",),
    )(qkv, qkv, tab)


def _attn_bwd(qkv, tab, lse, dycat, s):
    rows = s // GW
    npair = rows // 2
    sa = s + CTX
    nzero = CTX // Q_TILE

    def body(q_ref, do_ref, lse_ref, kv_ref, tab_ref, dq_ref, dkv_ref, tt_ref, dk_acc, dv_acc):
        i = pl.program_id(0)

        @pl.when(i == 0)
        def _():
            dk_acc[...] = jnp.zeros_like(dk_acc)
            dv_acc[...] = jnp.zeros_like(dv_acc)
            tt_ref[...] = jnp.zeros_like(tt_ref)

        @pl.when(i >= npair)
        def _():
            dq_ref[...] = jnp.zeros_like(dq_ref)

        @pl.when(i < npair)
        def _():
            k0, t0, rowmask = _attn_geometry(i, rows)
            for p in range(NH // 2):
                cq = slice(p * LANES, (p + 1) * LANES)
                ck = slice(DA + p * LANES, DA + (p + 1) * LANES)
                cv = slice(2 * DA + p * LANES, 2 * DA + (p + 1) * LANES)
                qp = q_ref[:, cq]
                dop = do_ref[:, cq].astype(MXU_DTYPE)
                qm2 = _two_heads_on_lanes(qp.T)
                dom2 = _two_heads_on_lanes(dop.T)
                kw = kv_ref[pl.ds(k0, K_WIN), ck]
                kc = kv_ref[pl.ds(s, CTX), ck]
                vw = kv_ref[pl.ds(k0, K_WIN), cv]
                vc = kv_ref[pl.ds(s, CTX), cv]
                s_loc = jnp.dot(kw, qm2, preferred_element_type=F32) * SCALE
                s_ctx = jnp.dot(kc, qm2, preferred_element_type=F32) * SCALE
                dp_loc = jnp.dot(vw, dom2, preferred_element_type=F32)
                dp_ctx = jnp.dot(vc, dom2, preferred_element_type=F32)
                p_loc, p_ctx, ds_loc, ds_ctx = [], [], [], []
                for hh in range(2):
                    h = 2 * p + hh
                    ch = slice(hh * Q_TILE, (hh + 1) * Q_TILE)
                    lse_h = lse_ref[h:h + 1, :]
                    pl_ = jnp.exp(s_loc[:, ch] + tab_ref[h, pl.ds(t0, K_WIN), :] + rowmask - lse_h)
                    pc_ = jnp.exp(s_ctx[:, ch] - lse_h)
                    dpl = dp_loc[:, ch]
                    dpc = dp_ctx[:, ch]
                    delta = jnp.sum(pl_ * dpl, axis=0, keepdims=True) + jnp.sum(pc_ * dpc, axis=0, keepdims=True)
                    dsl = pl_ * (dpl - delta)
                    dsc = pc_ * (dpc - delta)
                    tt_ref[h, pl.ds(t0, K_WIN), :] += dsl
                    p_loc.append(pl_.astype(MXU_DTYPE))
                    p_ctx.append(pc_.astype(MXU_DTYPE))
                    ds_loc.append((dsl * SCALE).astype(MXU_DTYPE))
                    ds_ctx.append((dsc * SCALE).astype(MXU_DTYPE))
                p_loc, p_ctx = jnp.concatenate(p_loc, axis=1), jnp.concatenate(p_ctx, axis=1)
                ds_loc, ds_ctx = jnp.concatenate(ds_loc, axis=1), jnp.concatenate(ds_ctx, axis=1)
                do_rows = _two_heads_on_rows(dop)
                q_rows = _two_heads_on_rows(qp)
                dv_acc[pl.ds(k0, K_WIN), cq] += jnp.dot(p_loc, do_rows, preferred_element_type=F32)
                dv_acc[pl.ds(s, CTX), cq] += jnp.dot(p_ctx, do_rows, preferred_element_type=F32)
                dk_acc[pl.ds(k0, K_WIN), cq] += jnp.dot(ds_loc, q_rows, preferred_element_type=F32)
                dk_acc[pl.ds(s, CTX), cq] += jnp.dot(ds_ctx, q_rows, preferred_element_type=F32)
                dq2 = (lax.dot_general(ds_loc, kw, _TN, preferred_element_type=F32)
                       + lax.dot_general(ds_ctx, kc, _TN, preferred_element_type=F32))
                dq_ref[:, cq] = _pick_heads(dq2).astype(dq_ref.dtype)

        @pl.when(i == npair - 1)
        def _():
            def cp(c, carry):
                r0 = pl.multiple_of(c * ROW_TILE, ROW_TILE)
                dkv_ref[pl.ds(r0, ROW_TILE), 0:DA] = dk_acc[pl.ds(r0, ROW_TILE), :].astype(dkv_ref.dtype)
                dkv_ref[pl.ds(r0, ROW_TILE), DA:2 * DA] = dv_acc[pl.ds(r0, ROW_TILE), :].astype(dkv_ref.dtype)
                return carry

            lax.fori_loop(0, sa // ROW_TILE, cp, 0)

    qmap = lambda i: (jnp.minimum(i, npair - 1), 0)
    return _pallas(
        body, name="attn_bwd", grid=(npair + nzero,),
        in_specs=[pl.BlockSpec((Q_TILE, DA), qmap), pl.BlockSpec((Q_TILE, DA), qmap),
                  pl.BlockSpec((NH, Q_TILE), lambda i: (0, jnp.minimum(i, npair - 1))),
                  pl.BlockSpec(memory_space=pltpu.VMEM), pl.BlockSpec(memory_space=pltpu.VMEM)],
        out_specs=[pl.BlockSpec((Q_TILE, DA), lambda i: (i, 0)), pl.BlockSpec(memory_space=pltpu.VMEM),
                   pl.BlockSpec(memory_space=pltpu.VMEM)],
        out_shape=[_sds((sa, DA), MXU_DTYPE), _sds((sa, 2 * DA), MXU_DTYPE), _sds((NH, TAB_BLOCKS * GW, LANES), F32)],
        scratch_shapes=[pltpu.VMEM((sa, DA), F32)] * 2,
        semantics=("arbitrary",),
    )(qkv, dycat, lse, qkv, tab)


def _tile(n, prefs):
    for t in prefs:
        if n % t == 0:
            return t
    raise ValueError((n, prefs))


def _local_step(x, ctx, tgt, mod, mod_c, vec, w_in, w_out, w_up, w_down, rpb_rev):
    s = x.shape[0]
    sa = s + CTX
    ts = _tile(s, (1024, 512, 256))
    tsa = _tile(sa, (1088, 640, 256))
    sh1, sc1, gt1, sh2, sc2, gt2 = (mod[i:i + 1] for i in range(6))
    csh1, csc1 = mod_c[0:1], mod_c[1:2]

    tab = _bias_table(rpb_rev)
    h_all = _rmsmod_fwd(x, ctx, vec["g_norm1"], sc1, sh1, csc1, csh1)
    qkv = _mm(h_all, w_in, mode="nn", m=sa, n=3 * DA, k=D, tm=tsa, tn=512, tk=D, out_dtype=MXU_DTYPE, name="mm_qkv")
    ag = _mm(h_all, w_in, mode="nn", m=s, n=2 * DC, k=D, tm=ts, tn=512, tk=D, out_dtype=F32, name="mm_ag", b_off=(0, 3))
    ycat, lse = _attn_fwd(qkv, tab, s)
    u1 = _conf_conv_fwd(ag, vec["conv_w"], vec["conv_b"])
    ycat = _conf_ln_fwd(u1, vec["ln_g"], vec["ln_b"], ycat)
    y = _mm(ycat, w_out, mode="nn", m=s, n=D, k=D, tm=ts, tn=512, tk=D, out_dtype=F32, name="mm_out")
    x1, h2 = _resid_rmsmod_fwd(x, y, gt1, vec["g_norm2"], sc2, sh2)
    u = _mm(h2, w_up, mode="nn", m=s, n=2 * DFF, k=D, tm=ts, tn=512, tk=D, out_dtype=F32, name="mm_up")
    f = _ffn_act_fwd(u, vec["ffn_conv_w"], vec["ffn_conv_b"])
    z = _mm(f, w_down, mode="nn", m=s, n=D, k=DFF, tm=ts, tn=512, tk=DFF // 2, out_dtype=F32, name="mm_down")
    dx2, dz, loss, dgt2, dgf = _final_fwd_bwd(x1, z, gt2, vec["g_final"], tgt)

    df = _mm(dz, w_down, mode="nt", m=s, n=DFF, k=D, tm=ts, tn=DFF // 2, tk=D, out_dtype=F32, name="mm_down_dx")
    d_w_down = _mm(f, dz, mode="tn", m=DFF, n=D, k=s, tm=DFF // 2, tn=D, tk=ts, out_dtype=F32, name="mm_down_dw")
    dug, duv, dfw_g, dfw_v, dfb_g, dfb_v = _ffn_act_bwd(u, df, vec["ffn_conv_w"], vec["ffn_conv_b"])
    dh2 = _mm([dug, duv], w_up, mode="nt", m=s, n=D, k=2 * DFF, tm=ts, tn=D, tk=DFF // 2, out_dtype=F32, name="mm_up_dx")
    d_w_up = _mm(h2, [dug, duv], mode="tn", m=D, n=2 * DFF, k=s, tm=D, tn=DFF // 2, tk=ts, out_dtype=F32, name="mm_up_dw")
    dsh2, dsc2, dg2, dx1, dy, dgt1 = _rmsmod_bwd(x1, dh2, vec["g_norm2"], sc2, name="rmsmod2_bwd", add=dx2, resid=(gt1, y))
    dycat = _mm(dy, w_out, mode="nt", m=s, n=D, k=D, tm=ts, tn=512, tk=D, out_dtype=F32, name="mm_out_dx")
    d_w_out = _mm(ycat, dy, mode="tn", m=D, n=D, k=s, tm=512, tn=D, tk=ts, out_dtype=F32, name="mm_out_dw")
    du1, dln_g, dln_b = _conf_ln_bwd(dycat, u1, vec["ln_g"], vec["ln_b"])
    da, dg, dconv_w, dconv_b = _conf_conv_bwd(ag, du1, vec["conv_w"], sa)
    dq, dkv, tt = _attn_bwd(qkv, tab, lse, dycat, s)
    drpb_rev = _rpb_grad(tt)
    d_pieces = [dq, dkv, da, dg]
    dh = _mm(d_pieces, w_in, mode="nt", m=sa, n=D, k=NIN, tm=tsa, tn=D, tk=512, out_dtype=F32, name="mm_in_dx")
    d_w_in = _mm(h_all, d_pieces, mode="tn", m=D, n=NIN, k=sa, tm=D, tn=512, tk=tsa, out_dtype=F32, name="mm_in_dw")
    dsh1, dsc1, dg1, grad_x = _rmsmod_bwd(x, dh, vec["g_norm1"], sc1, name="rmsmod1_bwd", add=dx1)
    dcsh1, dcsc1, dg1c = _rmsmod_bwd(ctx, dh, vec["g_norm1"], csc1, name="rmsmod1_ctx_bwd", dh_row_off=s // ROW_TILE)

    small = dict(
        dmod=[dsh1, dsc1, dgt1, dsh2, dsc2, dgt2], dmod_c=[dcsh1, dcsc1],
        g_norm1=[dg1, dg1c], g_norm2=dg2, g_final=dgf, conv_b=dconv_b, ln_g=dln_g, ln_b=dln_b, conv_w=dconv_w,
        ffn_conv_w=[dfw_g, dfw_v], ffn_conv_b=[dfb_g, dfb_v], rpb_rev=drpb_rev,
    )
    return loss, grad_x, d_w_in, d_w_out, d_w_up, d_w_down, small


N_CHIPS = 4
HBM = pl.BlockSpec(memory_space=pl.ANY)
BIG = {"w_in": ("col", (D, NIN)), "w_out": ("row", (D, D)), "w_up": ("col", (D, 2 * DFF)), "w_down": ("row", (DFF, D))}
BIG_NAMES = tuple(BIG)


def _shard_shape(name):
    kind, (r, c) = BIG[name]
    return (r, c // N_CHIPS) if kind == "col" else (r // N_CHIPS, c)


def _half_rows(name):
    return _shard_shape(name)[0] // 2


def _place():
    x, y, c = lax.axis_index("x"), lax.axis_index("y"), lax.axis_index("c")
    others = [(1 - x, y), (x, 1 - y), (1 - x, 1 - y)]
    return x, y, c, 2 * x + y, (x, y, 1 - c), others


def _whole_region(ref, name, chip, half):
    kind, _ = BIG[name]
    r, c = _shard_shape(name)
    if kind == "col":
        return ref.at[pl.ds(half * (r // 2), r // 2), pl.ds(chip * c, c)]
    return ref.at[pl.ds(chip * r + half * (r // 2), r // 2), :]


def _remote(src, dst, send_sem, recv_sem, to):
    return pltpu.make_async_remote_copy(src_ref=src, dst_ref=dst, send_sem=send_sem, recv_sem=recv_sem,
                                        device_id=to, device_id_type=MESH)


def _gather_small(v, name):
    m_per, n = v.shape

    def body(x_ref, out_ref, send_sems, recv_sems, local_sem):
        x, y, c, _, sibling, others = _place()
        me = (x, y, c)

        def rows(px, py, pc):
            return out_ref.at[pl.ds((4 * px + 2 * py + pc) * m_per, m_per), :]

        def copy(k, block, to, src=None):
            return _remote(rows(*block) if src is None else src, rows(*block), send_sems.at[k], recv_sems.at[k], to)

        mine = pltpu.make_async_copy(x_ref, rows(*me), local_sem)
        mine.start()
        first = [copy(0, me, sibling, src=x_ref)]
        first += [copy(1 + j, me, (*chip, c), src=x_ref) for j, chip in enumerate(others)]
        for cp in first:
            cp.start()
        passed = [copy(4 + j, (*chip, c), sibling) for j, chip in enumerate(others)]
        for j, chip in enumerate(others):
            copy(1 + j, (*chip, c), me).wait_recv()
            passed[j].start()
        copy(0, sibling, me).wait_recv()
        for j, chip in enumerate(others):
            copy(4 + j, (*chip, 1 - c), me).wait_recv()
        for cp in first + passed:
            cp.wait_send()
        mine.wait()

    return pl.pallas_call(
        body, name=name, out_shape=_sds((8 * m_per, n), v.dtype),
        in_specs=[pl.BlockSpec(memory_space=pltpu.VMEM)], out_specs=pl.BlockSpec(memory_space=pltpu.VMEM),
        scratch_shapes=[pltpu.SemaphoreType.DMA((7,)), pltpu.SemaphoreType.DMA((7,)), pltpu.SemaphoreType.DMA],
    )(v)


def _cast_into_whole(name, shard, chip):
    kind, whole = BIG[name]
    r, c = shard.shape
    if kind == "col":
        tr = 256
        o_spec = pl.BlockSpec((tr, c), lambda i, ch: (i, ch[0]))
    else:
        tr = _tile(r, (128, 352))
        o_spec = pl.BlockSpec((tr, c), lambda i, ch: (ch[0] * (r // tr) + i, 0))

    def body(ch_ref, x_ref, o_ref):
        del ch_ref
        o_ref[...] = x_ref[...].astype(o_ref.dtype)

    return _pallas(body, name="cast_" + name, prefetch=1, grid=(r // tr,),
                   in_specs=[pl.BlockSpec((tr, c), lambda i, ch: (i, 0))], out_specs=o_spec,
                   out_shape=_sds(whole, MXU_DTYPE), semantics=("parallel",))(chip, shard)


def _gather_weights(wholes):
    nw = len(BIG_NAMES)

    def body(*refs):
        outs = refs[nw:2 * nw]
        send_sems, recv_sems = refs[2 * nw:]
        _, _, c, chip, sibling, others = _place()
        sends = []
        for w, name in enumerate(BIG_NAMES):
            mine = _whole_region(outs[w], name, chip, c)
            for t, (ox, oy) in enumerate(others):
                cp = _remote(mine, mine, send_sems.at[w, t], recv_sems.at[w, t], (ox, oy, c))
                cp.start()
                sends.append(cp)
        for w, name in enumerate(BIG_NAMES):
            for t, (ox, oy) in enumerate(others):
                got = _whole_region(outs[w], name, 2 * ox + oy, c)
                _remote(got, got, send_sems.at[w, t], recv_sems.at[w, t], (ox, oy, c)).wait_recv()
                cp = _remote(got, got, send_sems.at[w, 3 + t], recv_sems.at[w, 3 + t], sibling)
                cp.start()
                sends.append(cp)
        for w, name in enumerate(BIG_NAMES):
            for t, (ox, oy) in enumerate(others):
                got = _whole_region(outs[w], name, 2 * ox + oy, 1 - c)
                _remote(got, got, send_sems.at[w, 3 + t], recv_sems.at[w, 3 + t], sibling).wait_recv()
        for cp in sends:
            cp.wait_send()

    return pl.pallas_call(
        body, name="gather_weights",
        out_shape=[_sds(a.shape, a.dtype) for a in wholes],
        in_specs=[HBM] * nw, out_specs=[HBM] * nw,
        input_output_aliases={i: i for i in range(nw)},
        scratch_shapes=[pltpu.SemaphoreType.DMA((nw, 6)), pltpu.SemaphoreType.DMA((nw, 6))],
    )(*wholes)


def _compact_shape(name, dtype):
    kind, (r, c) = BIG[name]
    return _sds((r // 2, c), dtype)


def _swap_halves(grads):
    nw = len(BIG_NAMES)

    def body(*refs):
        ins, outs = refs[:nw], refs[nw:2 * nw]
        send_sems, recv_sems = refs[2 * nw:]
        _, _, c, _, sibling, _ = _place()
        copies = []
        for w, name in enumerate(BIG_NAMES):
            kind, (r, _) = BIG[name]
            half = _half_rows(name)
            if kind == "col":
                parts = [(ins[w].at[pl.ds((1 - c) * half, half), :], outs[w])]
            else:
                parts = [(ins[w].at[pl.ds(jj * 2 * half + (1 - c) * half, half), :], outs[w].at[pl.ds(jj * half, half), :])
                         for jj in range(N_CHIPS)]
            for t, (src, dst) in enumerate(parts):
                cp = _remote(src, dst, send_sems.at[w, t], recv_sems.at[w, t], sibling)
                cp.start()
                copies.append(cp)
        for cp in copies:
            cp.wait()

    return pl.pallas_call(
        body, name="grad_swap_halves",
        out_shape=[_compact_shape(n, F32) for n in BIG_NAMES],
        in_specs=[HBM] * nw, out_specs=[HBM] * nw,
        scratch_shapes=[pltpu.SemaphoreType.DMA((nw, N_CHIPS)), pltpu.SemaphoreType.DMA((nw, N_CHIPS))],
    )(*grads)


def _add_halves(name, grad, got, core):
    kind, (r, c) = BIG[name]
    half = _half_rows(name)
    if kind == "col":
        t = 128
        grid = (half // t,)
        g_spec = pl.BlockSpec((t, c), lambda i, cr: (cr[0] * (half // t) + i, 0))
        o_spec = pl.BlockSpec((t, c), lambda i, cr: (i, 0))
    else:
        t = half
        grid = (N_CHIPS,)
        g_spec = pl.BlockSpec((t, c), lambda i, cr: (2 * i + cr[0], 0))
        o_spec = pl.BlockSpec((t, c), lambda i, cr: (i, 0))

    def body(c_ref, g_ref, b_ref, o_ref):
        del c_ref
        o_ref[...] = (g_ref[...] + b_ref[...]).astype(o_ref.dtype)

    return pl.pallas_call(
        body, name="grad_add_" + name,
        grid_spec=pltpu.PrefetchScalarGridSpec(num_scalar_prefetch=1, grid=grid, in_specs=[g_spec, o_spec], out_specs=o_spec),
        out_shape=_compact_shape(name, BF16),
        compiler_params=pltpu.CompilerParams(dimension_semantics=("parallel",), vmem_limit_bytes=VMEM_LIMIT),
    )(core, grad, got)


def _exchange_shards(parts):
    nw = len(BIG_NAMES)

    def piece(ref, name, chip):
        kind, _ = BIG[name]
        r, c = _shard_shape(name)
        if kind == "col":
            return ref.at[:, pl.ds(chip * c, c)]
        return ref.at[pl.ds(chip * (r // 2), r // 2), :]

    def body(*refs):
        ins, outs = refs[:nw], refs[nw:2 * nw]
        send_sems, recv_sems = refs[2 * nw:]
        _, _, c, _, _, others = _place()
        sends = []
        for w, name in enumerate(BIG_NAMES):
            for t, (ox, oy) in enumerate(others):
                cp = _remote(piece(ins[w], name, 2 * ox + oy), outs[w].at[t], send_sems.at[w, t], recv_sems.at[w, t], (ox, oy, c))
                cp.start()
                sends.append(cp)
        for w, name in enumerate(BIG_NAMES):
            for t, (ox, oy) in enumerate(others):
                got = outs[w].at[t]
                _remote(got, got, send_sems.at[w, t], recv_sems.at[w, t], (ox, oy, c)).wait_recv()
        for cp in sends:
            cp.wait_send()

    def out_shape(name):
        r, c = _shard_shape(name)
        return _sds((N_CHIPS - 1, r // 2, c), BF16)

    return pl.pallas_call(
        body, name="grad_exchange_shards",
        out_shape=[out_shape(n) for n in BIG_NAMES],
        in_specs=[HBM] * nw, out_specs=[HBM] * nw,
        scratch_shapes=[pltpu.SemaphoreType.DMA((nw, 3)), pltpu.SemaphoreType.DMA((nw, 3))],
    )(*parts)


def _sum_chips(name, part, got, chip):
    kind, _ = BIG[name]
    _, r, c = got.shape
    t = _tile(r, (128, 352))
    if kind == "col":
        own = pl.BlockSpec((t, c), lambda i, ch: (i, ch[0]))
    else:
        own = pl.BlockSpec((t, c), lambda i, ch: (ch[0] * (r // t) + i, 0))

    def body(ch_ref, p_ref, g_ref, o_ref):
        del ch_ref
        acc = p_ref[...].astype(F32)
        for j in range(N_CHIPS - 1):
            acc = acc + g_ref[j].astype(F32)
        o_ref[...] = acc

    return _pallas(
        body, name="grad_sum_" + name, prefetch=1, grid=(r // t,),
        in_specs=[own, pl.BlockSpec((N_CHIPS - 1, t, c), lambda i, ch: (0, i, 0))],
        out_specs=pl.BlockSpec((t, c), lambda i, ch: (i, 0)),
        out_shape=_sds((r, c), F32), semantics=("parallel",),
    )(chip, part, got)


def _send_halves(sums):
    nw = len(BIG_NAMES)

    def body(*refs):
        ins, outs = refs[:nw], refs[nw:2 * nw]
        send_sems, recv_sems = refs[2 * nw:]
        _, _, _, _, sibling, _ = _place()
        copies = [_remote(ins[w], outs[w], send_sems.at[w], recv_sems.at[w], sibling) for w in range(nw)]
        for cp in copies:
            cp.start()
        for cp in copies:
            cp.wait()

    return pl.pallas_call(
        body, name="grad_send_halves",
        out_shape=[_sds(a.shape, a.dtype) for a in sums],
        in_specs=[HBM] * nw, out_specs=[HBM] * nw,
        scratch_shapes=[pltpu.SemaphoreType.DMA((nw,)), pltpu.SemaphoreType.DMA((nw,))],
    )(*sums)


def _reduce_scatter(grads, core, chip):
    got = _swap_halves(grads)
    parts = [_add_halves(n, grads[i], got[i], core) for i, n in enumerate(BIG_NAMES)]
    gathered = _exchange_shards(parts)
    sums = [_sum_chips(n, parts[i], gathered[i], chip) for i, n in enumerate(BIG_NAMES)]
    return sums, _send_halves(sums)


HI = lax.Precision.HIGHEST
MOD_COLS = 6 * D // N_CHIPS
COND_ROWS = 16


def _silu(v):
    return v * _sigmoid(v)


GATHER_ROWS = 48
FFW_COLS = 2 * DFF // N_CHIPS
CONV_COLS = DC // N_CHIPS


def _pack_cond(c, ffn_w, conv_w):
    def body(c_ref, f_ref, w_ref, o_ref):
        o_ref[...] = jnp.zeros_like(o_ref)
        o_ref[0:1, 0:D] = c_ref[...]
        o_ref[8:11, :] = f_ref[...]
        o_ref[16:16 + CW, 0:CONV_COLS] = w_ref[...]

    return _pallas(body, name="pack_cond", out_shape=_sds((GATHER_ROWS, FFW_COLS), F32))(c, ffn_w, conv_w)


def _unpack_cond(got, c_ctx):
    def body(g_ref, c_ref, cond_ref, f_ref, w_ref):
        cond_ref[...] = jnp.zeros_like(cond_ref)
        for d in range(8):
            cond_ref[d:d + 1, :] = g_ref[d * GATHER_ROWS:d * GATHER_ROWS + 1, 0:D]
        cond_ref[8:9, :] = c_ref[...]
        for j in range(N_CHIPS):
            r0 = 2 * j * GATHER_ROWS
            f_ref[:, j * FFW_COLS:(j + 1) * FFW_COLS] = g_ref[r0 + 8:r0 + 11, :]
            w_ref[:, j * CONV_COLS:(j + 1) * CONV_COLS] = g_ref[r0 + 16:r0 + 16 + CW, 0:CONV_COLS]

    return _pallas(body, name="unpack_cond",
                   out_shape=[_sds((COND_ROWS, D), F32), _sds((3, 2 * DFF), F32), _sds((CW, DC), F32)])(got, c_ctx)


def _chip_cols(rows, width):
    return pl.BlockSpec((rows, width), lambda i, ch: (0, ch[0]))


def _whole(shape):
    return pl.BlockSpec(shape, lambda i, ch: (0,) * len(shape))


def _mod_shard(cond, w_mod, b_mod, chip):
    def body(ch_ref, c_ref, w_ref, b_ref, o_ref):
        del ch_ref
        o_ref[...] = jnp.dot(_silu(c_ref[...]), w_ref[...], preferred_element_type=F32, precision=HI) + b_ref[...]

    return _pallas(body, name="mod_fwd", prefetch=1, grid=(1,),
                   in_specs=[_whole((COND_ROWS, D)), _whole((D, MOD_COLS)), _chip_cols(1, MOD_COLS)],
                   out_specs=_whole((COND_ROWS, MOD_COLS)),
                   out_shape=_sds((COND_ROWS, MOD_COLS), F32))(chip, cond, w_mod, b_mod)


def _unpack_mod(mods, dev):
    def body(dev_ref, m_ref, me_ref, c_ref):
        rowi = lax.broadcasted_iota(jnp.int32, (COND_ROWS, MOD_COLS), 0)
        mine, ctx = [], []
        for j in range(N_CHIPS):
            blk = m_ref[2 * j * COND_ROWS:(2 * j + 1) * COND_ROWS, :]
            mine.append(jnp.sum(jnp.where(rowi == dev_ref[0], blk, 0.0), axis=0, keepdims=True))
            ctx.append(blk[8:9, :])
        mine = jnp.concatenate(mine, axis=1)
        ctx = jnp.concatenate(ctx, axis=1)
        for k in range(6):
            me_ref[k:k + 1, :] = mine[:, k * D:(k + 1) * D]
        for k in range(2):
            c_ref[k:k + 1, :] = ctx[:, k * D:(k + 1) * D]

    return _pallas(body, name="unpack_mod", prefetch=1, grid=(1,),
                   in_specs=[_whole(mods.shape)], out_specs=[_whole((6, D)), _whole((2, D))],
                   out_shape=[_sds((6, D), F32), _sds((2, D), F32)])(dev, mods)


def _mod_weight_grad(cond, dmod_all, chip):
    def body(ch_ref, c_ref, d_ref, o_ref):
        del ch_ref
        o_ref[...] = lax.dot_general(_silu(c_ref[...]), d_ref[...], _TN, preferred_element_type=F32, precision=HI)

    return _pallas(body, name="mod_weight_grad", prefetch=1, grid=(1,),
                   in_specs=[_whole((COND_ROWS, D)), _chip_cols(COND_ROWS, MOD_COLS)], out_specs=_whole((D, MOD_COLS)),
                   out_shape=_sds((D, MOD_COLS), F32))(chip, cond, dmod_all)


def _cond_grad_partial(dmod_all, w_mod, chip):
    def body(ch_ref, d_ref, w_ref, o_ref):
        del ch_ref
        o_ref[...] = lax.dot_general(d_ref[...], w_ref[...], (((1,), (1,)), ((), ())), preferred_element_type=F32, precision=HI)

    return _pallas(body, name="cond_grad_partial", prefetch=1, grid=(1,),
                   in_specs=[pl.BlockSpec((8, MOD_COLS), lambda i, ch: (1, ch[0])), _whole((D, MOD_COLS))],
                   out_specs=_whole((8, D)), out_shape=_sds((8, D), F32))(chip, dmod_all, w_mod)


def _adam_math(w, g, m, v):
    nm = ADAM_B1 * m + (1.0 - ADAM_B1) * g
    nv = ADAM_B2 * v + (1.0 - ADAM_B2) * (g * g)
    c1 = 1.0 - ADAM_B1 ** ADAM_STEP
    c2 = 1.0 - ADAM_B2 ** ADAM_STEP
    return -ADAM_LR * ((nm / c1) / (jnp.sqrt(nv / c2) + ADAM_EPS) + ADAM_WD * w), nm, nv


def _cond_update(parts, c_ctx, m, v):
    def body(p_ref, c_ref, m_ref, v_ref, g_ref, d_ref, nm_ref, nv_ref):
        tot = p_ref[0:1, :]
        for j in range(1, N_CHIPS):
            tot = tot + p_ref[16 * j:16 * j + 1, :]
        cv = c_ref[...]
        sg = _sigmoid(cv)
        g = tot * (sg * (1.0 + cv * (1.0 - sg)))
        g_ref[...] = g
        d_ref[...], nm_ref[...], nv_ref[...] = _adam_math(cv, g, m_ref[...], v_ref[...])

    return _pallas(body, name="cond_update", out_shape=[_sds((1, D), F32)] * 4)(parts, c_ctx, m, v)


def _adamw(w, g, m, v, name):
    r, c = w.shape
    t = _tile(r, (128,)) if r % 128 == 0 and r > 128 else r

    def body(w_ref, g_ref, m_ref, v_ref, d_ref, nm_ref, nv_ref):
        d_ref[...], nm_ref[...], nv_ref[...] = _adam_math(w_ref[...], g_ref[...], m_ref[...], v_ref[...])

    blk = pl.BlockSpec((t, c), lambda i: (i, 0))
    return _pallas(body, name=name, grid=(r // t,), in_specs=[blk] * 4, out_specs=[blk] * 3,
                   out_shape=[_sds((r, c), F32)] * 3, semantics=("parallel",))(w, g, m, v)


def _adamw_cols(w, g_all, m, v, chip, name):
    r, c = w.shape

    def body(ch_ref, w_ref, g_ref, m_ref, v_ref, go_ref, d_ref, nm_ref, nv_ref):
        del ch_ref
        g = g_ref[...]
        go_ref[...] = g
        d_ref[...], nm_ref[...], nv_ref[...] = _adam_math(w_ref[...], g, m_ref[...], v_ref[...])

    return _pallas(body, name=name, prefetch=1, grid=(1,),
                   in_specs=[_whole((r, c)), _chip_cols(r, c), _whole((r, c)), _whole((r, c))],
                   out_specs=[_whole((r, c))] * 4, out_shape=[_sds((r, c), F32)] * 4)(chip, w, g_all, m, v)


def _adamw_halves(name, w, own, other, m, v, core):
    r, c = w.shape
    half = r // 2
    t = _tile(half, (128, 352))
    nh = half // t

    def pick(mine):
        def index(i, cr):
            first = cr[0] if mine else 1 - cr[0]
            return (jnp.clip(i - first * nh, 0, nh - 1), 0)
        return pl.BlockSpec((t, c), index)

    def body(c_ref, w_ref, own_ref, oth_ref, m_ref, v_ref, g_ref, d_ref, nm_ref, nv_ref):
        g = jnp.where(pl.program_id(0) // nh == c_ref[0], own_ref[...], oth_ref[...])
        g_ref[...] = g
        d_ref[...], nm_ref[...], nv_ref[...] = _adam_math(w_ref[...], g, m_ref[...], v_ref[...])

    blk = pl.BlockSpec((t, c), lambda i, cr: (i, 0))
    return _pallas(body, name="adamw_" + name, prefetch=1, grid=(2 * nh,),
                   in_specs=[blk, pick(True), pick(False), blk, blk], out_specs=[blk] * 4,
                   out_shape=[_sds((r, c), F32)] * 4, semantics=("parallel",))(core, w, own, other, m, v)


WEIGHTS = ("c_ctx", "w_mod", "b_mod", "g_norm1", "w_in", "rpb", "conv_w", "conv_b", "ln_g", "ln_b", "w_out", "g_norm2",
           "w_up", "ffn_conv_w", "ffn_conv_b", "w_down", "g_final")
PACK = (("dmod", 6 * D), ("dmod_c", 2 * D), ("g_norm1", D), ("g_norm1_ctx", D), ("g_norm2", D), ("g_final", D),
        ("conv_b", DC), ("ln_g", DC), ("ln_b", DC), ("ffn_conv_b", 2 * DFF), ("ffn_conv_w", 3 * 2 * DFF),
        ("conv_w", CW * DC), ("rpb_rev", NH * 16 * LANES), ("loss", LANES))
PACK_OFF = {}
_o = 0
for _n, _w in PACK:
    PACK_OFF[_n] = (_o, _w)
    _o += _w
PACK_N = -(-_o // (8 * LANES)) * (8 * LANES)
VECTORS = {"b_mod": (6 * D, ("dmod", "dmod_c")), "g_norm1": (D, ("g_norm1", "g_norm1_ctx")), "conv_b": (DC, ("conv_b",)),
           "ln_g": (DC, ("ln_g",)), "ln_b": (DC, ("ln_b",)), "g_norm2": (D, ("g_norm2",)),
           "ffn_conv_b": (2 * DFF, ("ffn_conv_b",)), "g_final": (D, ("g_final",))}
RPB_ROWS = NH * (2 * NA_ROWS - 1)
RPB_COLS = 4 * NA_ROWS - 1


def _pack_small(parts):
    arrs, places = [], []
    for name, _ in PACK:
        off, width = PACK_OFF[name]
        group = parts[name]
        rows = group[0].shape[0]
        row_w = sum(a.shape[1] for a in group)
        assert rows * row_w == width, (name, rows, row_w, width)
        col = 0
        for a in group:
            arrs.append(a)
            places.append([off + k * row_w + col for k in range(rows)])
            col += a.shape[1]

    def body(*refs):
        o_ref = refs[-1]
        o_ref[:, _o:PACK_N] = jnp.zeros((1, PACK_N - _o), F32)
        for ref, offs in zip(refs, places):
            n = ref.shape[1]
            for k, off in enumerate(offs):
                o_ref[:, off:off + n] = ref[k:k + 1, :]

    return _pallas(body, name="pack_small_grads", out_shape=_sds((1, PACK_N), F32))(*arrs)


def _small_update(packs, w, m, v):
    names = list(VECTORS)

    def body(*refs):
        it = iter(refs)
        p_ref = next(it)
        wmv = {n: (next(it), next(it), next(it)) for n in names}
        outs = {n: (next(it), next(it), next(it), next(it)) for n in names}
        dmod_ref, cw_ref, fw_ref, rpb_ref, loss_ref = next(it), next(it), next(it), next(it), next(it)

        def total(name):
            off, width = PACK_OFF[name]
            acc = p_ref[0:1, off:off + width]
            for d in range(1, 8):
                acc = acc + p_ref[d:d + 1, off:off + width]
            return acc

        for n in names:
            width, segs = VECTORS[n]
            g = total(segs[0])
            if len(segs) > 1:
                extra = total(segs[1])
                ew = extra.shape[1]
                g = g + extra if ew == width else jnp.concatenate([g[:, :ew] + extra, g[:, ew:]], axis=1)
            w_ref, m_ref, v_ref = wmv[n]
            g_ref, d_ref, nm_ref, nv_ref = outs[n]
            g_ref[...] = g
            d_ref[...], nm_ref[...], nv_ref[...] = _adam_math(w_ref[...], g, m_ref[...], v_ref[...])

        o_dmod = PACK_OFF["dmod"][0]
        dmod_ref[...] = jnp.zeros_like(dmod_ref)
        dmod_ref[0:8, :] = p_ref[:, o_dmod:o_dmod + 6 * D]
        dmod_ref[8:9, 0:2 * D] = total("dmod_c")
        for ref, name, rows in ((cw_ref, "conv_w", CW), (fw_ref, "ffn_conv_w", 3), (rpb_ref, "rpb_rev", NH * 16)):
            flat = total(name)
            n = ref.shape[1]
            for k in range(rows):
                ref[k:k + 1, :] = flat[:, k * n:(k + 1) * n]
        loss_ref[...] = total("loss")

    ins = [packs] + [a[n] for n in names for a in (w, m, v)]
    out_shape = [_sds((1, VECTORS[n][0]), F32) for n in names for _ in range(4)]
    out_shape += [_sds((COND_ROWS, 6 * D), F32), _sds((CW, DC), F32), _sds((3, 2 * DFF), F32), _sds((NH * 16, LANES), F32),
                  _sds((1, LANES), F32)]
    res = _pallas(body, name="small_update", out_shape=out_shape)(*ins)
    per = {n: tuple(res[4 * i:4 * i + 4]) for i, n in enumerate(names)}
    return (per, *res[4 * len(names):])


def _rpb_update(rev, w, m, v):
    def body(r_ref, w_ref, m_ref, v_ref, g_ref, d_ref, nm_ref, nv_ref):
        li = lax.broadcasted_iota(jnp.int32, (LANES, LANES), 0)
        co = lax.broadcasted_iota(jnp.int32, (LANES, LANES), 1)
        lane_of_co0 = GW - 1 + RPB_COLS // 2
        unflip = jnp.where((li == lane_of_co0 - co) & (co < RPB_COLS), 1.0, 0.0).astype(F32)
        g_all = jnp.dot(r_ref[...], unflip, preferred_element_type=F32, precision=HI)
        nr = 2 * NA_ROWS - 1
        for h in range(NH):
            rows = slice(h * nr, (h + 1) * nr)
            g = g_all[h * 16:h * 16 + nr, 0:RPB_COLS]
            g_ref[rows, :] = g
            d_ref[rows, :], nm_ref[rows, :], nv_ref[rows, :] = _adam_math(w_ref[rows, :], g, m_ref[rows, :], v_ref[rows, :])

    return _pallas(body, name="rpb_update", out_shape=[_sds((RPB_ROWS, RPB_COLS), F32)] * 4)(rev, w, m, v)


def kernel(x, c, ctx, c_ctx, w_mod, b_mod, g_norm1, w_in, rpb, conv_w, conv_b, ln_g, ln_b, w_out, g_norm2, w_up, ffn_conv_w, ffn_conv_b, w_down, g_final, loss_target, m_c_ctx, m_w_mod, m_b_mod, m_g_norm1, m_w_in, m_rpb, m_conv_w, m_conv_b, m_ln_g, m_ln_b, m_w_out, m_g_norm2, m_w_up, m_ffn_conv_w, m_ffn_conv_b, m_w_down, m_g_final, v_c_ctx, v_w_mod, v_b_mod, v_g_norm1, v_w_in, v_rpb, v_conv_w, v_conv_b, v_ln_g, v_ln_b, v_w_out, v_g_norm2, v_w_up, v_ffn_conv_w, v_ffn_conv_b, v_w_down, v_g_final):
    w = dict(c_ctx=c_ctx, w_mod=w_mod, b_mod=b_mod, g_norm1=g_norm1, w_in=w_in, rpb=rpb, conv_w=conv_w, conv_b=conv_b,
             ln_g=ln_g, ln_b=ln_b, w_out=w_out, g_norm2=g_norm2, w_up=w_up, ffn_conv_w=ffn_conv_w, ffn_conv_b=ffn_conv_b,
             w_down=w_down, g_final=g_final)
    mom = dict(c_ctx=m_c_ctx, w_mod=m_w_mod, b_mod=m_b_mod, g_norm1=m_g_norm1, w_in=m_w_in, rpb=m_rpb, conv_w=m_conv_w,
               conv_b=m_conv_b, ln_g=m_ln_g, ln_b=m_ln_b, w_out=m_w_out, g_norm2=m_g_norm2, w_up=m_w_up,
               ffn_conv_w=m_ffn_conv_w, ffn_conv_b=m_ffn_conv_b, w_down=m_w_down, g_final=m_g_final)
    var = dict(c_ctx=v_c_ctx, w_mod=v_w_mod, b_mod=v_b_mod, g_norm1=v_g_norm1, w_in=v_w_in, rpb=v_rpb, conv_w=v_conv_w,
               conv_b=v_conv_b, ln_g=v_ln_g, ln_b=v_ln_b, w_out=v_w_out, g_norm2=v_g_norm2, w_up=v_w_up,
               ffn_conv_w=v_ffn_conv_w, ffn_conv_b=v_ffn_conv_b, w_down=v_w_down, g_final=v_g_final)
    xi, yi, ci = lax.axis_index("x"), lax.axis_index("y"), lax.axis_index("c")
    dev = (4 * xi + 2 * yi + ci).astype(jnp.int32).reshape(1)
    chip = (2 * xi + yi).astype(jnp.int32).reshape(1)
    core = ci.astype(jnp.int32).reshape(1)
    c_ctx2 = c_ctx.reshape(1, D)
    g_final2 = g_final.reshape(1, D)
    mom["g_final"], var["g_final"] = m_g_final.reshape(1, D), v_g_final.reshape(1, D)

    got = _gather_small(_pack_cond(c, ffn_conv_w[0], conv_w[0]), "gather_cond")
    cond, ffn_w_all, conv_w_all = _unpack_cond(got, c_ctx2)

    mods = _gather_small(_mod_shard(cond, w_mod[0], b_mod, chip), "gather_mod")
    mod_me, mod_c = _unpack_mod(mods, dev)

    whole = _gather_weights([_cast_into_whole(n, w[n][0], chip) for n in BIG_NAMES])

    rpb_rev = jnp.pad(rpb[0][:, :, ::-1], ((0, 0), (0, 1), (48, LANES - 48 - RPB_COLS))).reshape(NH * 16, LANES)
    vec = dict(g_norm1=g_norm1, g_norm2=g_norm2, g_final=g_final2, conv_w=conv_w_all, conv_b=conv_b, ln_g=ln_g, ln_b=ln_b,
               ffn_conv_w=ffn_w_all, ffn_conv_b=ffn_conv_b)
    loss_p, grad_x, d_in, d_out, d_up, d_down, small = _local_step(
        x[0], ctx[0], loss_target[0], mod_me, mod_c, vec, *whole, rpb_rev)

    own, other = _reduce_scatter([d_in, d_out, d_up, d_down], core, chip)

    parts = dict(dmod=small["dmod"], dmod_c=small["dmod_c"], g_norm1=[small["g_norm1"][0]], g_norm1_ctx=[small["g_norm1"][1]],
                 g_norm2=[small["g_norm2"]], g_final=[small["g_final"]], conv_b=[small["conv_b"]], ln_g=[small["ln_g"]],
                 ln_b=[small["ln_b"]], ffn_conv_b=small["ffn_conv_b"], ffn_conv_w=small["ffn_conv_w"],
                 conv_w=[small["conv_w"]], rpb_rev=[small["rpb_rev"]], loss=[loss_p])
    pack = _pack_small(parts).reshape(8, PACK_N // 8)
    packs = _gather_small(pack, "gather_small_grads").reshape(8, PACK_N)
    w2 = dict(w, g_final=g_final2)
    per, dmod_all, g_conv_w_all, g_ffn_w_all, g_rpb_rev, loss_row = _small_update(packs, w2, mom, var)

    out = {}
    out.update(per)
    out["c_ctx"] = _cond_update(
        _gather_small(_cond_grad_partial(dmod_all, w_mod[0], chip), "gather_cond_grad"),
        c_ctx2, m_c_ctx.reshape(1, D), v_c_ctx.reshape(1, D))
    g_w_mod = _mod_weight_grad(cond, dmod_all, chip)
    out["w_mod"] = (g_w_mod, *_adamw(w_mod[0], g_w_mod, m_w_mod[0], v_w_mod[0], "adamw_w_mod"))
    for i, n in enumerate(BIG_NAMES):
        out[n] = _adamw_halves(n, w[n][0], own[i], other[i], mom[n][0], var[n][0], core)
    out["conv_w"] = _adamw_cols(conv_w[0], g_conv_w_all, m_conv_w[0], v_conv_w[0], chip, "adamw_conv_w")
    out["ffn_conv_w"] = _adamw_cols(ffn_conv_w[0], g_ffn_w_all, m_ffn_conv_w[0], v_ffn_conv_w[0], chip, "adamw_ffn_conv_w")
    flat = lambda a: a.reshape(RPB_ROWS, RPB_COLS)
    out["rpb"] = _rpb_update(g_rpb_rev, flat(rpb), flat(m_rpb), flat(v_rpb))

    res = [[out[n][k].reshape(w[n].shape) for n in WEIGHTS] for k in range(4)]
    return (loss_row[0, 0], grad_x[None], *res[0], *res[1], *res[2], *res[3])
```

```python
import functools

import jax
import jax.numpy as jnp
from jax import lax
from jax.experimental import pallas as pl
from jax.experimental.pallas import tpu as pltpu

F32 = jnp.float32
BF16 = jnp.bfloat16
MXU_DTYPE = jnp.bfloat16

D = 1024
CTX = 256
GW = 64
DA = 512
NH = 8
HD = 64
DC = 512
CW = 31
DFF = 2816
NIN = 3 * DA + 2 * DC
EPS = 1e-6
SCALE = HD ** -0.5
NEG = -1e30
NA_ROWS = 8
PAIR_ROWS = NA_ROWS + 1
TAB_BLOCKS = 17
LANES = 128
VMEM_LIMIT = 56 * 1024 * 1024

ADAM_LR = 0.001
ADAM_B1 = 0.9
ADAM_B2 = 0.999
ADAM_EPS = 1e-08
ADAM_WD = 0.01
ADAM_STEP = 10

MESH = pl.DeviceIdType.MESH


def _pallas(body, *, name, semantics=None, vmem=VMEM_LIMIT, prefetch=0, **kw):
    params = dict(vmem_limit_bytes=vmem)
    if semantics is not None:
        params["dimension_semantics"] = semantics
    if prefetch:
        kw["grid_spec"] = pltpu.PrefetchScalarGridSpec(
            num_scalar_prefetch=prefetch, grid=kw.pop("grid"), in_specs=kw.pop("in_specs"), out_specs=kw.pop("out_specs"),
            scratch_shapes=kw.pop("scratch_shapes", ()))
    return pl.pallas_call(body, name=name, compiler_params=pltpu.CompilerParams(**params), **kw)


def _sds(shape, dtype):
    return jax.ShapeDtypeStruct(shape, dtype)


def _vec_spec(n):
    return pl.BlockSpec((1, n), lambda *_: (0, 0))


def _colsum8(x):
    t, n = x.shape
    return jnp.sum(x.reshape(t // 8, 8, n), axis=0)


def _sigmoid(x):
    return 1.0 / (1.0 + jnp.exp(-x))


def _pieces(arrs, tile):
    lo, out = 0, []
    for a in arrs:
        nt = a.shape[1] // tile
        assert nt * tile == a.shape[1], (a.shape, tile)
        out.append((lo, nt))
        lo += nt
    return out


def _mm(a, b, *, mode, m, n, k, tm, tn, tk, out_dtype, name, a_off=(0, 0), b_off=(0, 0), k_outer=False):
    a_list = list(a) if isinstance(a, (list, tuple)) else [a]
    b_list = list(b) if isinstance(b, (list, tuple)) else [b]
    assert m % tm == 0 and n % tn == 0 and k % tk == 0, (name, m, n, k, tm, tn, tk)
    gi, gj, nk = m // tm, n // tn, k // tk
    a_tile = tm if mode == "tn" else tk
    a_pc = _pieces(a_list, a_tile) if len(a_list) > 1 else [(0, 1 << 30)]
    if mode == "nt":
        assert len(b_list) == 1
    b_pc = _pieces(b_list, tn) if len(b_list) > 1 else [(0, 1 << 30)]
    dims = {"nn": (((1,), (0,)), ((), ())), "nt": (((1,), (1,)), ((), ())), "tn": (((0,), (0,)), ((), ()))}[mode]
    k_outer = k_outer and nk > 1

    def ijk(fn):
        return (lambda i, kk, j: fn(i, j, kk)) if k_outer else fn

    def a_spec(lo, cnt):
        def loc(idx):
            return idx + a_off[1] if len(a_list) == 1 else jnp.clip(idx - lo, 0, cnt - 1)
        if mode == "tn":
            return pl.BlockSpec((tk, tm), ijk(lambda i, j, kk: (kk + a_off[0], loc(i))))
        return pl.BlockSpec((tm, tk), ijk(lambda i, j, kk: (i + a_off[0], loc(kk))))

    def b_spec(lo, cnt):
        def loc(idx):
            return idx + b_off[1] if len(b_list) == 1 else jnp.clip(idx - lo, 0, cnt - 1)
        if mode == "nt":
            return pl.BlockSpec((tn, tk), ijk(lambda i, j, kk: (j + b_off[0], kk + b_off[1])))
        return pl.BlockSpec((tk, tn), ijk(lambda i, j, kk: (kk + b_off[0], loc(j))))

    na, nb = len(a_list), len(b_list)
    in_place = nk > 1 and out_dtype == F32 and not k_outer

    def body(*refs):
        a_refs, b_refs, o_ref = refs[:na], refs[na:na + nb], refs[na + nb]
        if k_outer:
            i, kk, j = pl.program_id(0), pl.program_id(1), pl.program_id(2)
            acc = refs[na + nb + 1].at[j]
        else:
            i, j, kk = pl.program_id(0), pl.program_id(1), pl.program_id(2)
            acc = o_ref if in_place else (refs[na + nb + 1] if nk > 1 else None)
        a_idx = i if mode == "tn" else kk

        def step(ar, br):
            p = lax.dot_general(ar[...].astype(MXU_DTYPE), br[...].astype(MXU_DTYPE), dims,
                                preferred_element_type=F32)
            if nk == 1:
                o_ref[...] = p.astype(out_dtype)
                return

            @pl.when(kk == 0)
            def _():
                acc[...] = p

            @pl.when(kk > 0)
            def _():
                acc[...] += p

            if not in_place:
                @pl.when(kk == nk - 1)
                def _():
                    o_ref[...] = acc[...].astype(out_dtype)

        for pa, (alo, acnt) in enumerate(a_pc):
            for pb, (blo, bcnt) in enumerate(b_pc):
                if na == 1 and nb == 1:
                    step(a_refs[0], b_refs[0])
                else:
                    cond = (a_idx >= alo) & (a_idx < alo + acnt) & (j >= blo) & (j < blo + bcnt)
                    pl.when(cond)(functools.partial(step, a_refs[pa], b_refs[pb]))

    if k_outer:
        grid = (gi, nk, gj)
        o_spec = pl.BlockSpec((tm, tn), lambda i, kk, j: (i, jnp.where(kk == nk - 1, j, 0)))
        scratch = [pltpu.VMEM((gj, tm, tn), F32)]
        semantics = ("parallel", "arbitrary", "arbitrary")
    else:
        grid = (gi, gj, nk)
        o_spec = pl.BlockSpec((tm, tn), lambda i, j, kk: (i, j))
        scratch = [pltpu.VMEM((tm, tn), F32)] if nk > 1 and not in_place else []
        semantics = ("parallel", "parallel", "arbitrary")
    return _pallas(
        body, name=name, grid=grid,
        in_specs=[a_spec(*p) for p in a_pc] + [b_spec(*p) for p in b_pc],
        out_specs=o_spec, out_shape=_sds((m, n), out_dtype), scratch_shapes=scratch, semantics=semantics,
    )(*a_list, *b_list)


ROW_TILE = 256


def _rmsmod_fwd(x, ctx, g, sc, sh, csc, csh):
    s = x.shape[0]
    nt = s // ROW_TILE
    assert ctx.shape[0] == ROW_TILE

    def body(x_ref, c_ref, g_ref, sc_ref, sh_ref, csc_ref, csh_ref, o_ref):
        is_ctx = pl.program_id(0) == nt
        xv = jnp.where(is_ctx, c_ref[...], x_ref[...])
        scv = jnp.where(is_ctx, csc_ref[...], sc_ref[...])
        shv = jnp.where(is_ctx, csh_ref[...], sh_ref[...])
        r = lax.rsqrt(jnp.mean(xv * xv, axis=-1, keepdims=True) + EPS)
        y = xv * r * g_ref[...]
        o_ref[...] = (y * (1.0 + scv) + shv).astype(o_ref.dtype)

    return _pallas(
        body, name="rmsmod1_fwd", grid=(nt + 1,),
        in_specs=[pl.BlockSpec((ROW_TILE, D), lambda i: (jnp.minimum(i, nt - 1), 0)),
                  pl.BlockSpec((ROW_TILE, D), lambda i: (0, 0))] + [_vec_spec(D)] * 5,
        out_specs=pl.BlockSpec((ROW_TILE, D), lambda i: (i, 0)),
        out_shape=_sds((s + CTX, D), MXU_DTYPE),
        semantics=("arbitrary",),
    )(x, ctx, g, sc, sh, csc, csh)


def _resid_rmsmod_fwd(x, y, gt, g, sc, sh):
    s = x.shape[0]

    def body(x_ref, y_ref, gt_ref, g_ref, sc_ref, sh_ref, x1_ref, h_ref):
        x1 = x_ref[...] + gt_ref[...] * y_ref[...]
        x1_ref[...] = x1
        r = lax.rsqrt(jnp.mean(x1 * x1, axis=-1, keepdims=True) + EPS)
        h_ref[...] = ((x1 * r * g_ref[...]) * (1.0 + sc_ref[...]) + sh_ref[...]).astype(h_ref.dtype)

    row = pl.BlockSpec((ROW_TILE, D), lambda i: (i, 0))
    return _pallas(
        body, name="resid_rmsmod2_fwd", grid=(s // ROW_TILE,),
        in_specs=[row, row] + [_vec_spec(D)] * 4,
        out_specs=[row, row],
        out_shape=[_sds((s, D), F32), _sds((s, D), MXU_DTYPE)],
        semantics=("parallel",),
    )(x, y, gt, g, sc, sh)


def _final_fwd_bwd(x1, z, gt2, gf, tgt):
    s = x1.shape[0]
    nt = s // ROW_TILE

    def body(x1_ref, z_ref, gt_ref, gf_ref, t_ref, dx2_ref, dz_ref, loss_ref, dgt_ref, dgf_ref, a_loss, a_gt, a_gf):
        i = pl.program_id(0)

        @pl.when(i == 0)
        def _():
            a_loss[...] = jnp.zeros_like(a_loss)
            a_gt[...] = jnp.zeros_like(a_gt)
            a_gf[...] = jnp.zeros_like(a_gf)

        zv = z_ref[...]
        gt = gt_ref[...]
        gf_ = gf_ref[...]
        x2 = x1_ref[...] + gt * zv
        r = lax.rsqrt(jnp.mean(x2 * x2, axis=-1, keepdims=True) + EPS)
        xn = x2 * r
        e = xn * gf_ - t_ref[...]
        a_loss[...] += _colsum8(e * e)
        dyo = e * (1.0 / D)
        a_gf[...] += _colsum8(dyo * xn)
        gdy = gf_ * dyo
        dx2 = r * gdy - xn * (r * r) * jnp.mean(x2 * gdy, axis=-1, keepdims=True)
        dx2_ref[...] = dx2
        dz_ref[...] = (gt * dx2).astype(dz_ref.dtype)
        a_gt[...] += _colsum8(dx2 * zv)

        @pl.when(i == nt - 1)
        def _():
            tot = jnp.sum(jnp.sum(a_loss[...], axis=0, keepdims=True), axis=1, keepdims=True) * (0.5 / D)
            loss_ref[...] = jnp.broadcast_to(tot, loss_ref.shape)
            dgt_ref[...] = jnp.sum(a_gt[...], axis=0, keepdims=True)
            dgf_ref[...] = jnp.sum(a_gf[...], axis=0, keepdims=True)

    row = pl.BlockSpec((ROW_TILE, D), lambda i: (i, 0))
    return _pallas(
        body, name="final_norm_loss", grid=(nt,),
        in_specs=[row, row, _vec_spec(D), _vec_spec(D), row],
        out_specs=[row, row, _vec_spec(LANES), _vec_spec(D), _vec_spec(D)],
        out_shape=[_sds((s, D), F32), _sds((s, D), MXU_DTYPE), _sds((1, LANES), F32), _sds((1, D), F32), _sds((1, D), F32)],
        scratch_shapes=[pltpu.VMEM((8, D), F32)] * 3,
        semantics=("arbitrary",),
    )(x1, z, gt2, gf, tgt)


def _rmsmod_bwd(xin, dh, g, sc, *, name, dh_row_off=0, add=None, resid=None):
    s = xin.shape[0]
    nt = s // ROW_TILE
    want_dx = add is not None
    assert resid is None or want_dx

    def body(*refs):
        it = iter(refs)
        x_ref, dh_ref, g_ref, sc_ref = next(it), next(it), next(it), next(it)
        add_ref = next(it) if want_dx else None
        gt_ref, y_ref = (next(it), next(it)) if resid is not None else (None, None)
        dsh_ref, dsc_ref, dg_ref = next(it), next(it), next(it)
        dx_ref = next(it) if want_dx else None
        dy_ref, dgt_ref = (next(it), next(it)) if resid is not None else (None, None)
        a_sh, a_sc, a_g = next(it), next(it), next(it)
        a_gt = next(it) if resid is not None else None
        i = pl.program_id(0)

        @pl.when(i == 0)
        def _():
            a_sh[...] = jnp.zeros_like(a_sh)
            a_sc[...] = jnp.zeros_like(a_sc)
            a_g[...] = jnp.zeros_like(a_g)
            if a_gt is not None:
                a_gt[...] = jnp.zeros_like(a_gt)

        xv = x_ref[...]
        dhv = dh_ref[...]
        gv = g_ref[...]
        r = lax.rsqrt(jnp.mean(xv * xv, axis=-1, keepdims=True) + EPS)
        xn = xv * r
        a_sh[...] += _colsum8(dhv)
        a_sc[...] += _colsum8(dhv * (xn * gv))
        dn = dhv * (1.0 + sc_ref[...])
        a_g[...] += _colsum8(dn * xn)
        if want_dx:
            gdn = gv * dn
            dx = add_ref[...] + r * gdn - xn * (r * r) * jnp.mean(xv * gdn, axis=-1, keepdims=True)
            dx_ref[...] = dx
            if resid is not None:
                dy_ref[...] = (gt_ref[...] * dx).astype(dy_ref.dtype)
                a_gt[...] += _colsum8(dx * y_ref[...])

        @pl.when(i == nt - 1)
        def _():
            dsh_ref[...] = jnp.sum(a_sh[...], axis=0, keepdims=True)
            dsc_ref[...] = jnp.sum(a_sc[...], axis=0, keepdims=True)
            dg_ref[...] = jnp.sum(a_g[...], axis=0, keepdims=True)
            if a_gt is not None:
                dgt_ref[...] = jnp.sum(a_gt[...], axis=0, keepdims=True)

    row = pl.BlockSpec((ROW_TILE, D), lambda i: (i, 0))
    ins = [xin, dh, g, sc]
    in_specs = [row, pl.BlockSpec((ROW_TILE, D), lambda i: (i + dh_row_off, 0)), _vec_spec(D), _vec_spec(D)]
    out_specs = [_vec_spec(D)] * 3
    out_shape = [_sds((1, D), F32)] * 3
    scratch = [pltpu.VMEM((8, D), F32)] * 3
    if want_dx:
        ins.append(add)
        in_specs.append(row)
        out_specs.append(row)
        out_shape.append(_sds((s, D), F32))
    if resid is not None:
        ins += [resid[0], resid[1]]
        in_specs += [_vec_spec(D), row]
        out_specs += [row, _vec_spec(D)]
        out_shape += [_sds((s, D), MXU_DTYPE), _sds((1, D), F32)]
        scratch.append(pltpu.VMEM((8, D), F32))
    return _pallas(body, name=name, grid=(nt,), in_specs=in_specs, out_specs=out_specs, out_shape=out_shape,
                   scratch_shapes=scratch, semantics=("arbitrary",))(*ins)


FF_TILE = 128
FF_CHUNK = 128
HALO = 8


def _shift3(pad_ref, r0, ch):
    win = pad_ref[pl.ds(r0, ch + 2 * HALO), :]
    prev = pltpu.roll(win, 1, 0)[HALO:HALO + ch]
    cur = win[HALO:HALO + ch]
    nxt = pltpu.roll(win, ch + 2 * HALO - 1, 0)[HALO:HALO + ch]
    return prev, cur, nxt


def _fill_padded(pad_ref, src_ref, s, ch, halo):
    zeros = jnp.zeros((halo, pad_ref.shape[1]), F32)
    pad_ref[0:halo, :] = zeros
    pad_ref[s + halo:s + 2 * halo, :] = zeros

    def cp(c, carry):
        r0 = pl.multiple_of(c * ch, ch)
        pad_ref[pl.ds(r0 + halo, ch), :] = src_ref[pl.ds(r0, ch), :].astype(F32)
        return carry

    lax.fori_loop(0, s // ch, cp, 0)


def _ffn_act_fwd(u, w, b):
    s = u.shape[0]
    nj = DFF // FF_TILE
    ch = FF_CHUNK

    def body(ug_ref, uv_ref, wg_ref, wv_ref, bg_ref, bv_ref, f_ref, gpad, vpad):
        _fill_padded(gpad, ug_ref, s, ch, HALO)
        _fill_padded(vpad, uv_ref, s, ch, HALO)

        def conv(pad, w_ref, b_ref, r0):
            prev, cur, nxt = _shift3(pad, r0, ch)
            return w_ref[0:1, :] * prev + w_ref[1:2, :] * cur + w_ref[2:3, :] * nxt + b_ref[...]

        def step(c, carry):
            r0 = pl.multiple_of(c * ch, ch)
            gc = conv(gpad, wg_ref, bg_ref, r0)
            vc = conv(vpad, wv_ref, bv_ref, r0)
            f_ref[pl.ds(r0, ch), :] = (gc * _sigmoid(gc) * vc).astype(f_ref.dtype)
            return carry

        lax.fori_loop(0, s // ch, step, 0)

    col = lambda off: pl.BlockSpec((s, FF_TILE), lambda j: (0, j + off))
    wsp = lambda off: pl.BlockSpec((3, FF_TILE), lambda j: (0, j + off))
    bsp = lambda off: pl.BlockSpec((1, FF_TILE), lambda j: (0, j + off))
    return _pallas(
        body, name="ffn_act_fwd", grid=(nj,),
        in_specs=[col(0), col(nj), wsp(0), wsp(nj), bsp(0), bsp(nj)],
        out_specs=col(0), out_shape=_sds((s, DFF), MXU_DTYPE),
        scratch_shapes=[pltpu.VMEM((s + 2 * HALO, FF_TILE), F32)] * 2,
        semantics=("parallel",),
    )(u, u, w, w, b, b)


def _ffn_act_bwd(u, df, w, b):
    s = u.shape[0]
    nj = DFF // FF_TILE
    ch = FF_CHUNK

    def body(ug_ref, uv_ref, df_ref, wg_ref, wv_ref, bg_ref, bv_ref,
             dug_ref, duv_ref, dwg_ref, dwv_ref, dbg_ref, dbv_ref, gpad, vpad, dgpad, dvpad, acc):
        _fill_padded(gpad, ug_ref, s, ch, HALO)
        _fill_padded(vpad, uv_ref, s, ch, HALO)
        zeros = jnp.zeros((HALO, FF_TILE), F32)
        for p in (dgpad, dvpad):
            p[0:HALO, :] = zeros
            p[s + HALO:s + 2 * HALO, :] = zeros
        acc[...] = jnp.zeros_like(acc)

        def step(c, carry):
            r0 = pl.multiple_of(c * ch, ch)
            gs = _shift3(gpad, r0, ch)
            vs = _shift3(vpad, r0, ch)
            gc = wg_ref[0:1, :] * gs[0] + wg_ref[1:2, :] * gs[1] + wg_ref[2:3, :] * gs[2] + bg_ref[...]
            vc = wv_ref[0:1, :] * vs[0] + wv_ref[1:2, :] * vs[1] + wv_ref[2:3, :] * vs[2] + bv_ref[...]
            sg = _sigmoid(gc)
            dfv = df_ref[pl.ds(r0, ch), :].astype(F32)
            dgc = dfv * vc * (sg * (1.0 + gc * (1.0 - sg)))
            dvc = dfv * (gc * sg)
            dgpad[pl.ds(r0 + HALO, ch), :] = dgc
            dvpad[pl.ds(r0 + HALO, ch), :] = dvc
            for t in range(3):
                acc[8 * t:8 * t + 8, :] += _colsum8(dgc * gs[t])
                acc[24 + 8 * t:32 + 8 * t, :] += _colsum8(dvc * vs[t])
            acc[48:56, :] += _colsum8(dgc)
            acc[56:64, :] += _colsum8(dvc)
            return carry

        lax.fori_loop(0, s // ch, step, 0)

        def step2(c, carry):
            r0 = pl.multiple_of(c * ch, ch)
            for pad, w_ref, o_ref in ((dgpad, wg_ref, dug_ref), (dvpad, wv_ref, duv_ref)):
                prev, cur, nxt = _shift3(pad, r0, ch)
                o_ref[pl.ds(r0, ch), :] = (w_ref[0:1, :] * nxt + w_ref[1:2, :] * cur + w_ref[2:3, :] * prev).astype(o_ref.dtype)
            return carry

        lax.fori_loop(0, s // ch, step2, 0)
        for t in range(3):
            dwg_ref[t:t + 1, :] = jnp.sum(acc[8 * t:8 * t + 8, :], axis=0, keepdims=True)
            dwv_ref[t:t + 1, :] = jnp.sum(acc[24 + 8 * t:32 + 8 * t, :], axis=0, keepdims=True)
        dbg_ref[...] = jnp.sum(acc[48:56, :], axis=0, keepdims=True)
        dbv_ref[...] = jnp.sum(acc[56:64, :], axis=0, keepdims=True)

    col = lambda off: pl.BlockSpec((s, FF_TILE), lambda j: (0, j + off))
    wsp = lambda off: pl.BlockSpec((3, FF_TILE), lambda j: (0, j + off))
    bsp = lambda off: pl.BlockSpec((1, FF_TILE), lambda j: (0, j + off))
    return _pallas(
        body, name="ffn_act_bwd", grid=(nj,),
        in_specs=[col(0), col(nj), col(0), wsp(0), wsp(nj), bsp(0), bsp(nj)],
        out_specs=[col(0), col(0), wsp(0), wsp(0), bsp(0), bsp(0)],
        out_shape=[_sds((s, DFF), MXU_DTYPE)] * 2 + [_sds((3, DFF), F32)] * 2 + [_sds((1, DFF), F32)] * 2,
        scratch_shapes=[pltpu.VMEM((s + 2 * HALO, FF_TILE), F32)] * 4 + [pltpu.VMEM((64, FF_TILE), F32)],
        semantics=("parallel",),
    )(u, u, df, w, w, b, b)


CONV_CHUNK = 64
CONV_HALO = 16
CONV_WIN = CONV_CHUNK + 2 * CONV_HALO


def _tap(win, k):
    off = CONV_HALO - CW // 2 + k
    return pltpu.roll(win, CONV_WIN - off, 0)[0:CONV_CHUNK]


def _glu_into(pad_ref, a_ref, g_ref, s):
    zeros = jnp.zeros((CONV_HALO, LANES), F32)
    pad_ref[0:CONV_HALO, :] = zeros
    pad_ref[s + CONV_HALO:s + 2 * CONV_HALO, :] = zeros

    def cp(c, carry):
        r0 = pl.multiple_of(c * ROW_TILE, ROW_TILE)
        pad_ref[pl.ds(r0 + CONV_HALO, ROW_TILE), :] = a_ref[pl.ds(r0, ROW_TILE), :] * _sigmoid(g_ref[pl.ds(r0, ROW_TILE), :])
        return carry

    lax.fori_loop(0, s // ROW_TILE, cp, 0)


def _conf_conv_fwd(ag, conv_w, conv_b):
    s = ag.shape[0]
    nc = DC // LANES

    def body(a_ref, g_ref, w_ref, b_ref, o_ref, upad):
        _glu_into(upad, a_ref, g_ref, s)

        def step(c, carry):
            r0 = pl.multiple_of(c * CONV_CHUNK, CONV_CHUNK)
            win = upad[pl.ds(r0, CONV_WIN), :]
            acc = jnp.broadcast_to(b_ref[...], (CONV_CHUNK, LANES))
            for k in range(CW):
                acc = acc + w_ref[k:k + 1, :] * _tap(win, k)
            o_ref[pl.ds(r0, CONV_CHUNK), :] = acc
            return carry

        lax.fori_loop(0, s // CONV_CHUNK, step, 0)

    col = lambda off: pl.BlockSpec((s, LANES), lambda c: (0, c + off))
    return _pallas(
        body, name="conf_conv_fwd", grid=(nc,),
        in_specs=[col(0), col(nc), pl.BlockSpec((CW, LANES), lambda c: (0, c)), pl.BlockSpec((1, LANES), lambda c: (0, c))],
        out_specs=col(0), out_shape=_sds((s, DC), F32),
        scratch_shapes=[pltpu.VMEM((s + 2 * CONV_HALO, LANES), F32)],
        semantics=("parallel",),
    )(ag, ag, conv_w, conv_b)


def _ln_stats(x):
    mu = jnp.mean(x, axis=-1, keepdims=True)
    xc = x - mu
    var = jnp.mean(xc * xc, axis=-1, keepdims=True)
    rstd = lax.rsqrt(var + EPS)
    return xc * rstd, rstd


def _conf_ln_fwd(u1, ln_g, ln_b, ycat):
    s = u1.shape[0]

    def body(u_ref, g_ref, b_ref, ycat_ref, o_ref):
        del ycat_ref
        xhat, _ = _ln_stats(u_ref[...])
        y = xhat * g_ref[...] + b_ref[...]
        o_ref[...] = (y * _sigmoid(y)).astype(o_ref.dtype)

    return _pallas(
        body, name="conf_ln_fwd", grid=(s // ROW_TILE,),
        in_specs=[pl.BlockSpec((ROW_TILE, DC), lambda i: (i, 0)), _vec_spec(DC), _vec_spec(DC),
                  pl.BlockSpec(memory_space=pl.ANY)],
        out_specs=pl.BlockSpec((ROW_TILE, DC), lambda i: (i, 1)),
        out_shape=_sds(ycat.shape, ycat.dtype),
        input_output_aliases={3: 0},
        semantics=("parallel",),
    )(u1, ln_g, ln_b, ycat)


def _conf_ln_bwd(dycat, u1, ln_g, ln_b):
    s = u1.shape[0]
    nt = s // ROW_TILE

    def body(dy_ref, u_ref, g_ref, b_ref, du_ref, dg_ref, db_ref, a_g, a_b):
        i = pl.program_id(0)

        @pl.when(i == 0)
        def _():
            a_g[...] = jnp.zeros_like(a_g)
            a_b[...] = jnp.zeros_like(a_b)

        xhat, rstd = _ln_stats(u_ref[...])
        gv = g_ref[...]
        y = xhat * gv + b_ref[...]
        sg = _sigmoid(y)
        dyl = dy_ref[...] * (sg * (1.0 + y * (1.0 - sg)))
        a_g[...] += _colsum8(dyl * xhat)
        a_b[...] += _colsum8(dyl)
        dxh = dyl * gv
        du_ref[...] = rstd * (dxh - jnp.mean(dxh, axis=-1, keepdims=True)
                              - xhat * jnp.mean(dxh * xhat, axis=-1, keepdims=True))

        @pl.when(i == nt - 1)
        def _():
            dg_ref[...] = jnp.sum(a_g[...], axis=0, keepdims=True)
            db_ref[...] = jnp.sum(a_b[...], axis=0, keepdims=True)

    return _pallas(
        body, name="conf_ln_bwd", grid=(nt,),
        in_specs=[pl.BlockSpec((ROW_TILE, DC), lambda i: (i, 1)), pl.BlockSpec((ROW_TILE, DC), lambda i: (i, 0)),
                  _vec_spec(DC), _vec_spec(DC)],
        out_specs=[pl.BlockSpec((ROW_TILE, DC), lambda i: (i, 0)), _vec_spec(DC), _vec_spec(DC)],
        out_shape=[_sds((s, DC), F32), _sds((1, DC), F32), _sds((1, DC), F32)],
        scratch_shapes=[pltpu.VMEM((8, DC), F32)] * 2,
        semantics=("arbitrary",),
    )(dycat, u1, ln_g, ln_b)


def _conf_conv_bwd(ag, du1, conv_w, rows_out):
    s = ag.shape[0]
    nc = DC // LANES

    def body(a_ref, g_ref, d_ref, w_ref, da_ref, dg_ref, dw_ref, db_ref, upad, dpad, acc):
        _glu_into(upad, a_ref, g_ref, s)
        _fill_padded(dpad, d_ref, s, ROW_TILE, CONV_HALO)
        acc[...] = jnp.zeros_like(acc)

        def step(c, carry):
            r0 = pl.multiple_of(c * CONV_CHUNK, CONV_CHUNK)
            uwin = upad[pl.ds(r0, CONV_WIN), :]
            dwin = dpad[pl.ds(r0, CONV_WIN), :]
            dcur = dwin[CONV_HALO:CONV_HALO + CONV_CHUNK]
            du0 = jnp.zeros((CONV_CHUNK, LANES), F32)
            for k in range(CW):
                du0 = du0 + w_ref[k:k + 1, :] * _tap(dwin, CW - 1 - k)
                acc[8 * k:8 * k + 8, :] += _colsum8(dcur * _tap(uwin, k))
            acc[8 * CW:8 * CW + 8, :] += _colsum8(dcur)
            av = a_ref[pl.ds(r0, CONV_CHUNK), :]
            sg = _sigmoid(g_ref[pl.ds(r0, CONV_CHUNK), :])
            da_ref[pl.ds(r0, CONV_CHUNK), :] = (du0 * sg).astype(da_ref.dtype)
            dg_ref[pl.ds(r0, CONV_CHUNK), :] = (du0 * av * (sg * (1.0 - sg))).astype(dg_ref.dtype)
            return carry

        lax.fori_loop(0, s // CONV_CHUNK, step, 0)
        if rows_out > s:
            zeros = jnp.zeros((rows_out - s, LANES), da_ref.dtype)
            da_ref[s:rows_out, :] = zeros
            dg_ref[s:rows_out, :] = zeros
        for k in range(CW):
            dw_ref[k:k + 1, :] = jnp.sum(acc[8 * k:8 * k + 8, :], axis=0, keepdims=True)
        db_ref[...] = jnp.sum(acc[8 * CW:8 * CW + 8, :], axis=0, keepdims=True)

    col = lambda off: pl.BlockSpec((s, LANES), lambda c: (0, c + off))
    ocol = pl.BlockSpec((rows_out, LANES), lambda c: (0, c))
    return _pallas(
        body, name="conf_conv_bwd", grid=(nc,),
        in_specs=[col(0), col(nc), col(0), pl.BlockSpec((CW, LANES), lambda c: (0, c))],
        out_specs=[ocol, ocol, pl.BlockSpec((CW, LANES), lambda c: (0, c)), pl.BlockSpec((1, LANES), lambda c: (0, c))],
        out_shape=[_sds((rows_out, DC), MXU_DTYPE)] * 2 + [_sds((CW, DC), F32), _sds((1, DC), F32)],
        scratch_shapes=[pltpu.VMEM((s + 2 * CONV_HALO, LANES), F32)] * 2 + [pltpu.VMEM((8 * (CW + 1), LANES), F32)],
        semantics=("parallel",),
    )(ag, ag, du1, conv_w)


Q_TILE = 2 * GW
K_WIN = PAIR_ROWS * GW


def _bias_table(rpb_rev):
    def body(p_ref, t_ref):
        kcol = lax.broadcasted_iota(jnp.int32, (GW, LANES), 0)
        lane = lax.broadcasted_iota(jnp.int32, (GW, LANES), 1)
        qcol = lane % GW
        cs = jnp.clip(qcol - NA_ROWS, 0, GW - 2 * NA_ROWS)
        colvalid = (kcol >= cs) & (kcol < cs + 2 * NA_ROWS)
        neg = jnp.full((GW, LANES), NEG, F32)

        def skew(h, ro, shift):
            if ro < 0 or ro >= 2 * NA_ROWS - 1:
                return neg
            row = jnp.broadcast_to(p_ref[h * 16 + ro:h * 16 + ro + 1, :], (GW, LANES))
            return pltpu.roll(row, shift, 1, stride=1, stride_axis=0)

        for h in range(NH):
            for b in range(TAB_BLOCKS):
                val = jnp.where(lane < GW, skew(h, b - 1, GW + 1), skew(h, b - 2, 1))
                t_ref[h, b * GW:(b + 1) * GW, :] = jnp.where(colvalid, val, neg)

    return _pallas(body, name="attn_bias_table", out_shape=_sds((NH, TAB_BLOCKS * GW, LANES), F32))(rpb_rev)


def _rpb_grad(tt):
    def body(t_ref, o_ref):
        lane = lax.broadcasted_iota(jnp.int32, (GW, LANES), 1)
        si = lax.broadcasted_iota(jnp.int32, (GW, GW), 0)
        ti = lax.broadcasted_iota(jnp.int32, (GW, GW), 1)
        flip = jnp.where(si + ti == GW - 1, 1.0, 0.0).astype(F32)
        o_ref[...] = jnp.zeros_like(o_ref)
        for h in range(NH):
            for ro in range(2 * NA_ROWS - 1):
                lo = t_ref[h, (ro + 1) * GW:(ro + 2) * GW, :]
                hi = t_ref[h, (ro + 2) * GW:(ro + 3) * GW, :]
                g = jnp.where(lane < GW, lo + pltpu.roll(hi, GW, 1), 0.0)
                gf = jnp.dot(flip, g, preferred_element_type=F32, precision=lax.Precision.HIGHEST)
                sk = pltpu.roll(gf, 0, 1, stride=1, stride_axis=0)
                o_ref[h * 16 + ro:h * 16 + ro + 1, :] = jnp.sum(sk, axis=0, keepdims=True)

    return _pallas(body, name="attn_rpb_grad", out_shape=_sds((NH * 16, LANES), F32))(tt)


def _attn_geometry(i, rows):
    wsp = jnp.clip(2 * i - NA_ROWS // 2, 0, rows - PAIR_ROWS)
    k0 = pl.multiple_of(wsp * GW, GW)
    t0 = pl.multiple_of((wsp - 2 * i + NA_ROWS) * GW, GW)
    jr = lax.broadcasted_iota(jnp.int32, (K_WIN, Q_TILE), 0) // GW
    rr = lax.broadcasted_iota(jnp.int32, (K_WIN, Q_TILE), 1) // GW
    kr = wsp + jr
    wsr = jnp.clip(2 * i + rr - NA_ROWS // 2, 0, rows - NA_ROWS)
    rowmask = jnp.where((kr >= wsr) & (kr < wsr + NA_ROWS), 0.0, NEG).astype(F32)
    return k0, t0, rowmask


def _two_heads_on_lanes(xt):
    feat = lax.broadcasted_iota(jnp.int32, xt.shape, 0)
    zero = jnp.zeros_like(xt)
    return jnp.concatenate([jnp.where(feat < HD, xt, zero), jnp.where(feat >= HD, xt, zero)], axis=1)


def _two_heads_on_rows(x):
    lane = lax.broadcasted_iota(jnp.int32, x.shape, 1)
    zero = jnp.zeros_like(x)
    return jnp.concatenate([jnp.where(lane < HD, x, zero), jnp.where(lane >= HD, x, zero)], axis=0)


def _pick_heads(x2):
    n = x2.shape[0] // 2
    lane = lax.broadcasted_iota(jnp.int32, (n, LANES), 1)
    return jnp.where(lane < HD, x2[:n], x2[n:])


_TN = (((0,), (0,)), ((), ()))


def _attn_fwd(qkv, tab, s):
    rows = s // GW
    npair = rows // 2

    def body(q_ref, kv_ref, tab_ref, o_ref, lse_ref):
        i = pl.program_id(0)
        k0, t0, rowmask = _attn_geometry(i, rows)
        for p in range(NH // 2):
            cq = slice(p * LANES, (p + 1) * LANES)
            ck = slice(DA + p * LANES, DA + (p + 1) * LANES)
            cv = slice(2 * DA + p * LANES, 2 * DA + (p + 1) * LANES)
            qm2 = _two_heads_on_lanes(q_ref[:, cq].T)
            s_loc = jnp.dot(kv_ref[pl.ds(k0, K_WIN), ck], qm2, preferred_element_type=F32) * SCALE
            s_ctx = jnp.dot(kv_ref[pl.ds(s, CTX), ck], qm2, preferred_element_type=F32) * SCALE
            p_loc, p_ctx = [], []
            for hh in range(2):
                h = 2 * p + hh
                ch = slice(hh * Q_TILE, (hh + 1) * Q_TILE)
                sl = s_loc[:, ch] + tab_ref[h, pl.ds(t0, K_WIN), :] + rowmask
                sc = s_ctx[:, ch]
                m = jnp.maximum(jnp.max(sl, axis=0, keepdims=True), jnp.max(sc, axis=0, keepdims=True))
                el = jnp.exp(sl - m)
                ec = jnp.exp(sc - m)
                l = jnp.sum(el, axis=0, keepdims=True) + jnp.sum(ec, axis=0, keepdims=True)
                inv = 1.0 / l
                lse_ref[h:h + 1, :] = m + jnp.log(l)
                p_loc.append((el * inv).astype(MXU_DTYPE))
                p_ctx.append((ec * inv).astype(MXU_DTYPE))
            o2 = (lax.dot_general(jnp.concatenate(p_loc, axis=1), kv_ref[pl.ds(k0, K_WIN), cv], _TN, preferred_element_type=F32)
                  + lax.dot_general(jnp.concatenate(p_ctx, axis=1), kv_ref[pl.ds(s, CTX), cv], _TN, preferred_element_type=F32))
            o_ref[:, cq] = _pick_heads(o2).astype(o_ref.dtype)

    return _pallas(
        body, name="attn_fwd", grid=(npair,),
        in_specs=[pl.BlockSpec((Q_TILE, DA), lambda i: (i, 0)), pl.BlockSpec(memory_space=pltpu.VMEM),
                  pl.BlockSpec(memory_space=pltpu.VMEM)],
        out_specs=[pl.BlockSpec((Q_TILE, DA), lambda i: (i, 0)), pl.BlockSpec((NH, Q_TILE), lambda i: (0, i))],
        out_shape=[_sds((s, D), MXU_DTYPE), _sds((NH, s), F32)],
        semantics=("arbitrary",),
    )(qkv, qkv, tab)


def _attn_bwd(qkv, tab, lse, dycat, s):
    rows = s // GW
    npair = rows // 2
    sa = s + CTX
    nzero = CTX // Q_TILE

    def body(q_ref, do_ref, lse_ref, kv_ref, tab_ref, dq_ref, dkv_ref, tt_ref, dk_acc, dv_acc):
        i = pl.program_id(0)

        @pl.when(i == 0)
        def _():
            dk_acc[...] = jnp.zeros_like(dk_acc)
            dv_acc[...] = jnp.zeros_like(dv_acc)
            tt_ref[...] = jnp.zeros_like(tt_ref)

        @pl.when(i >= npair)
        def _():
            dq_ref[...] = jnp.zeros_like(dq_ref)

        @pl.when(i < npair)
        def _():
            k0, t0, rowmask = _attn_geometry(i, rows)
            for p in range(NH // 2):
                cq = slice(p * LANES, (p + 1) * LANES)
                ck = slice(DA + p * LANES, DA + (p + 1) * LANES)
                cv = slice(2 * DA + p * LANES, 2 * DA + (p + 1) * LANES)
                qp = q_ref[:, cq]
                dop = do_ref[:, cq].astype(MXU_DTYPE)
                qm2 = _two_heads_on_lanes(qp.T)
                dom2 = _two_heads_on_lanes(dop.T)
                kw = kv_ref[pl.ds(k0, K_WIN), ck]
                kc = kv_ref[pl.ds(s, CTX), ck]
                vw = kv_ref[pl.ds(k0, K_WIN), cv]
                vc = kv_ref[pl.ds(s, CTX), cv]
                s_loc = jnp.dot(kw, qm2, preferred_element_type=F32) * SCALE
                s_ctx = jnp.dot(kc, qm2, preferred_element_type=F32) * SCALE
                dp_loc = jnp.dot(vw, dom2, preferred_element_type=F32)
                dp_ctx = jnp.dot(vc, dom2, preferred_element_type=F32)
                p_loc, p_ctx, ds_loc, ds_ctx = [], [], [], []
                for hh in range(2):
                    h = 2 * p + hh
                    ch = slice(hh * Q_TILE, (hh + 1) * Q_TILE)
                    lse_h = lse_ref[h:h + 1, :]
                    pl_ = jnp.exp(s_loc[:, ch] + tab_ref[h, pl.ds(t0, K_WIN), :] + rowmask - lse_h)
                    pc_ = jnp.exp(s_ctx[:, ch] - lse_h)
                    dpl = dp_loc[:, ch]
                    dpc = dp_ctx[:, ch]
                    delta = jnp.sum(pl_ * dpl, axis=0, keepdims=True) + jnp.sum(pc_ * dpc, axis=0, keepdims=True)
                    dsl = pl_ * (dpl - delta)
                    dsc = pc_ * (dpc - delta)
                    tt_ref[h, pl.ds(t0, K_WIN), :] += dsl
                    p_loc.append(pl_.astype(MXU_DTYPE))
                    p_ctx.append(pc_.astype(MXU_DTYPE))
                    ds_loc.append((dsl * SCALE).astype(MXU_DTYPE))
                    ds_ctx.append((dsc * SCALE).astype(MXU_DTYPE))
                p_loc, p_ctx = jnp.concatenate(p_loc, axis=1), jnp.concatenate(p_ctx, axis=1)
                ds_loc, ds_ctx = jnp.concatenate(ds_loc, axis=1), jnp.concatenate(ds_ctx, axis=1)
                do_rows = _two_heads_on_rows(dop)
                q_rows = _two_heads_on_rows(qp)
                dv_acc[pl.ds(k0, K_WIN), cq] += jnp.dot(p_loc, do_rows, preferred_element_type=F32)
                dv_acc[pl.ds(s, CTX), cq] += jnp.dot(p_ctx, do_rows, preferred_element_type=F32)
                dk_acc[pl.ds(k0, K_WIN), cq] += jnp.dot(ds_loc, q_rows, preferred_element_type=F32)
                dk_acc[pl.ds(s, CTX), cq] += jnp.dot(ds_ctx, q_rows, preferred_element_type=F32)
                dq2 = (lax.dot_general(ds_loc, kw, _TN, preferred_element_type=F32)
                       + lax.dot_general(ds_ctx, kc, _TN, preferred_element_type=F32))
                dq_ref[:, cq] = _pick_heads(dq2).astype(dq_ref.dtype)

        @pl.when(i == npair - 1)
        def _():
            def cp(c, carry):
                r0 = pl.multiple_of(c * ROW_TILE, ROW_TILE)
                dkv_ref[pl.ds(r0, ROW_TILE), 0:DA] = dk_acc[pl.ds(r0, ROW_TILE), :].astype(dkv_ref.dtype)
                dkv_ref[pl.ds(r0, ROW_TILE), DA:2 * DA] = dv_acc[pl.ds(r0, ROW_TILE), :].astype(dkv_ref.dtype)
                return carry

            lax.fori_loop(0, sa // ROW_TILE, cp, 0)

    qmap = lambda i: (jnp.minimum(i, npair - 1), 0)
    return _pallas(
        body, name="attn_bwd", grid=(npair + nzero,),
        in_specs=[pl.BlockSpec((Q_TILE, DA), qmap), pl.BlockSpec((Q_TILE, DA), qmap),
                  pl.BlockSpec((NH, Q_TILE), lambda i: (0, jnp.minimum(i, npair - 1))),
                  pl.BlockSpec(memory_space=pltpu.VMEM), pl.BlockSpec(memory_space=pltpu.VMEM)],
        out_specs=[pl.BlockSpec((Q_TILE, DA), lambda i: (i, 0)), pl.BlockSpec(memory_space=pltpu.VMEM),
                   pl.BlockSpec(memory_space=pltpu.VMEM)],
        out_shape=[_sds((sa, DA), MXU_DTYPE), _sds((sa, 2 * DA), MXU_DTYPE), _sds((NH, TAB_BLOCKS * GW, LANES), F32)],
        scratch_shapes=[pltpu.VMEM((sa, DA), F32)] * 2,
        semantics=("arbitrary",),
    )(qkv, dycat, lse, qkv, tab)


def _tile(n, prefs):
    for t in prefs:
        if n % t == 0:
            return t
    raise ValueError((n, prefs))


def _local_step(x, ctx, tgt, mod, mod_c, vec, w_in, w_out, w_up, w_down, rpb_rev):
    s = x.shape[0]
    sa = s + CTX
    ts = _tile(s, (1024, 512, 256))
    ts2 = _tile(s, (2048, 1024, 512, 256))
    tsa = _tile(sa, (1088, 640, 256))
    tsa2 = _tile(sa, (2176, 640, 256))
    sh1, sc1, gt1, sh2, sc2, gt2 = (mod[i:i + 1] for i in range(6))
    csh1, csc1 = mod_c[0:1], mod_c[1:2]
    act = MXU_DTYPE

    tab = _bias_table(rpb_rev)
    h_all = _rmsmod_fwd(x, ctx, vec["g_norm1"], sc1, sh1, csc1, csh1)
    qkv = _mm(h_all, w_in, mode="nn", m=sa, n=3 * DA, k=D, tm=tsa2, tn=512, tk=D, out_dtype=MXU_DTYPE, name="mm_qkv")
    ag = _mm(h_all, w_in, mode="nn", m=s, n=2 * DC, k=D, tm=ts2, tn=512, tk=D, out_dtype=F32, name="mm_ag", b_off=(0, 3))
    ycat, lse = _attn_fwd(qkv, tab, s)
    u1 = _conf_conv_fwd(ag, vec["conv_w"], vec["conv_b"])
    ycat = _conf_ln_fwd(u1, vec["ln_g"], vec["ln_b"], ycat)
    y = _mm(ycat, w_out, mode="nn", m=s, n=D, k=D, tm=ts2, tn=512, tk=D, out_dtype=F32, name="mm_out")
    x1, h2 = _resid_rmsmod_fwd(x, y, gt1, vec["g_norm2"], sc2, sh2)
    u = _mm(h2, w_up, mode="nn", m=s, n=2 * DFF, k=D, tm=ts2, tn=512, tk=D, out_dtype=act, name="mm_up")
    f = _ffn_act_fwd(u, vec["ffn_conv_w"], vec["ffn_conv_b"])
    z = _mm(f, w_down, mode="nn", m=s, n=D, k=DFF, tm=ts2, tn=D, tk=DFF // 2, out_dtype=F32, name="mm_down")
    dx2, dz, loss, dgt2, dgf = _final_fwd_bwd(x1, z, gt2, vec["g_final"], tgt)

    df = _mm(dz, w_down, mode="nt", m=s, n=DFF, k=D, tm=ts2, tn=DFF // 2, tk=D, out_dtype=act, name="mm_down_dx")
    d_w_down = _mm(f, dz, mode="tn", m=DFF, n=D, k=s, tm=DFF, tn=D, tk=ts, out_dtype=F32, name="mm_down_dw")
    dug, duv, dfw_g, dfw_v, dfb_g, dfb_v = _ffn_act_bwd(u, df, vec["ffn_conv_w"], vec["ffn_conv_b"])
    dh2 = _mm([dug, duv], w_up, mode="nt", m=s, n=D, k=2 * DFF, tm=ts2, tn=D, tk=DFF // 2, out_dtype=F32, name="mm_up_dx")
    d_w_up = _mm(h2, [dug, duv], mode="tn", m=D, n=2 * DFF, k=s, tm=D, tn=DFF // 2, tk=512, out_dtype=F32, name="mm_up_dw",
                 k_outer=True)
    dsh2, dsc2, dg2, dx1, dy, dgt1 = _rmsmod_bwd(x1, dh2, vec["g_norm2"], sc2, name="rmsmod2_bwd", add=dx2, resid=(gt1, y))
    dycat = _mm(dy, w_out, mode="nt", m=s, n=D, k=D, tm=ts2, tn=512, tk=D, out_dtype=F32, name="mm_out_dx")
    d_w_out = _mm(ycat, dy, mode="tn", m=D, n=D, k=s, tm=D, tn=D, tk=ts, out_dtype=F32, name="mm_out_dw")
    du1, dln_g, dln_b = _conf_ln_bwd(dycat, u1, vec["ln_g"], vec["ln_b"])
    da, dg, dconv_w, dconv_b = _conf_conv_bwd(ag, du1, vec["conv_w"], sa)
    dq, dkv, tt = _attn_bwd(qkv, tab, lse, dycat, s)
    drpb_rev = _rpb_grad(tt)
    d_pieces = [dq, dkv, da, dg]
    dh = _mm(d_pieces, w_in, mode="nt", m=sa, n=D, k=NIN, tm=tsa2, tn=D, tk=512, out_dtype=F32, name="mm_in_dx")
    d_w_in = _mm(h_all, d_pieces, mode="tn", m=D, n=NIN, k=sa, tm=D, tn=512, tk=tsa, out_dtype=F32, name="mm_in_dw",
                 k_outer=True)
    dsh1, dsc1, dg1, grad_x = _rmsmod_bwd(x, dh, vec["g_norm1"], sc1, name="rmsmod1_bwd", add=dx1)
    dcsh1, dcsc1, dg1c = _rmsmod_bwd(ctx, dh, vec["g_norm1"], csc1, name="rmsmod1_ctx_bwd", dh_row_off=s // ROW_TILE)

    small = dict(
        dmod=[dsh1, dsc1, dgt1, dsh2, dsc2, dgt2], dmod_c=[dcsh1, dcsc1],
        g_norm1=[dg1, dg1c], g_norm2=dg2, g_final=dgf, conv_b=dconv_b, ln_g=dln_g, ln_b=dln_b, conv_w=dconv_w,
        ffn_conv_w=[dfw_g, dfw_v], ffn_conv_b=[dfb_g, dfb_v], rpb_rev=drpb_rev,
    )
    return loss, grad_x, d_w_in, d_w_out, d_w_up, d_w_down, small


N_CHIPS = 4
HBM = pl.BlockSpec(memory_space=pl.ANY)
BIG = {"w_in": ("col", (D, NIN)), "w_out": ("row", (D, D)), "w_up": ("col", (D, 2 * DFF)), "w_down": ("row", (DFF, D))}
BIG_NAMES = tuple(BIG)


def _shard_shape(name):
    kind, (r, c) = BIG[name]
    return (r, c // N_CHIPS) if kind == "col" else (r // N_CHIPS, c)


def _half_rows(name):
    return _shard_shape(name)[0] // 2


def _place():
    x, y, c = lax.axis_index("x"), lax.axis_index("y"), lax.axis_index("c")
    others = [(1 - x, y), (x, 1 - y), (1 - x, 1 - y)]
    return x, y, c, 2 * x + y, (x, y, 1 - c), others


def _whole_region(ref, name, chip, half):
    kind, _ = BIG[name]
    r, c = _shard_shape(name)
    if kind == "col":
        return ref.at[pl.ds(half * (r // 2), r // 2), pl.ds(chip * c, c)]
    return ref.at[pl.ds(chip * r + half * (r // 2), r // 2), :]


def _remote(src, dst, send_sem, recv_sem, to):
    return pltpu.make_async_remote_copy(src_ref=src, dst_ref=dst, send_sem=send_sem, recv_sem=recv_sem,
                                        device_id=to, device_id_type=MESH)


def _gather_small(v, name):
    m_per, n = v.shape

    def body(x_ref, out_ref, send_sems, recv_sems, local_sem):
        x, y, c, _, sibling, others = _place()
        me = (x, y, c)

        def rows(px, py, pc):
            return out_ref.at[pl.ds((4 * px + 2 * py + pc) * m_per, m_per), :]

        def copy(k, block, to, src=None):
            return _remote(rows(*block) if src is None else src, rows(*block), send_sems.at[k], recv_sems.at[k], to)

        mine = pltpu.make_async_copy(x_ref, rows(*me), local_sem)
        mine.start()
        first = [copy(0, me, sibling, src=x_ref)]
        first += [copy(1 + j, me, (*chip, c), src=x_ref) for j, chip in enumerate(others)]
        for cp in first:
            cp.start()
        passed = [copy(4 + j, (*chip, c), sibling) for j, chip in enumerate(others)]
        for j, chip in enumerate(others):
            copy(1 + j, (*chip, c), me).wait_recv()
            passed[j].start()
        copy(0, sibling, me).wait_recv()
        for j, chip in enumerate(others):
            copy(4 + j, (*chip, 1 - c), me).wait_recv()
        for cp in first + passed:
            cp.wait_send()
        mine.wait()

    return pl.pallas_call(
        body, name=name, out_shape=_sds((8 * m_per, n), v.dtype),
        in_specs=[pl.BlockSpec(memory_space=pltpu.VMEM)], out_specs=pl.BlockSpec(memory_space=pltpu.VMEM),
        scratch_shapes=[pltpu.SemaphoreType.DMA((7,)), pltpu.SemaphoreType.DMA((7,)), pltpu.SemaphoreType.DMA],
    )(v)


def _cast_into_whole(name, shard, chip):
    kind, whole = BIG[name]
    r, c = shard.shape
    if kind == "col":
        tr = 256
        o_spec = pl.BlockSpec((tr, c), lambda i, ch: (i, ch[0]))
    else:
        tr = _tile(r, (128, 352))
        o_spec = pl.BlockSpec((tr, c), lambda i, ch: (ch[0] * (r // tr) + i, 0))

    def body(ch_ref, x_ref, o_ref):
        del ch_ref
        o_ref[...] = x_ref[...].astype(o_ref.dtype)

    return _pallas(body, name="cast_" + name, prefetch=1, grid=(r // tr,),
                   in_specs=[pl.BlockSpec((tr, c), lambda i, ch: (i, 0))], out_specs=o_spec,
                   out_shape=_sds(whole, MXU_DTYPE), semantics=("parallel",))(chip, shard)


def _gather_weights(wholes):
    nw = len(BIG_NAMES)

    def body(*refs):
        outs = refs[nw:2 * nw]
        send_sems, recv_sems = refs[2 * nw:]
        _, _, c, chip, sibling, others = _place()
        sends = []
        for w, name in enumerate(BIG_NAMES):
            mine = _whole_region(outs[w], name, chip, c)
            for t, (ox, oy) in enumerate(others):
                cp = _remote(mine, mine, send_sems.at[w, t], recv_sems.at[w, t], (ox, oy, c))
                cp.start()
                sends.append(cp)
        for w, name in enumerate(BIG_NAMES):
            for t, (ox, oy) in enumerate(others):
                got = _whole_region(outs[w], name, 2 * ox + oy, c)
                _remote(got, got, send_sems.at[w, t], recv_sems.at[w, t], (ox, oy, c)).wait_recv()
                cp = _remote(got, got, send_sems.at[w, 3 + t], recv_sems.at[w, 3 + t], sibling)
                cp.start()
                sends.append(cp)
        for w, name in enumerate(BIG_NAMES):
            for t, (ox, oy) in enumerate(others):
                got = _whole_region(outs[w], name, 2 * ox + oy, 1 - c)
                _remote(got, got, send_sems.at[w, 3 + t], recv_sems.at[w, 3 + t], sibling).wait_recv()
        for cp in sends:
            cp.wait_send()

    return pl.pallas_call(
        body, name="gather_weights",
        out_shape=[_sds(a.shape, a.dtype) for a in wholes],
        in_specs=[HBM] * nw, out_specs=[HBM] * nw,
        input_output_aliases={i: i for i in range(nw)},
        scratch_shapes=[pltpu.SemaphoreType.DMA((nw, 6)), pltpu.SemaphoreType.DMA((nw, 6))],
    )(*wholes)


def _compact_shape(name, dtype):
    kind, (r, c) = BIG[name]
    return _sds((r // 2, c), dtype)


def _swap_halves(grads):
    nw = len(BIG_NAMES)

    def body(*refs):
        ins, outs = refs[:nw], refs[nw:2 * nw]
        send_sems, recv_sems = refs[2 * nw:]
        _, _, c, _, sibling, _ = _place()
        copies = []
        for w, name in enumerate(BIG_NAMES):
            kind, (r, _) = BIG[name]
            half = _half_rows(name)
            if kind == "col":
                parts = [(ins[w].at[pl.ds((1 - c) * half, half), :], outs[w])]
            else:
                parts = [(ins[w].at[pl.ds(jj * 2 * half + (1 - c) * half, half), :], outs[w].at[pl.ds(jj * half, half), :])
                         for jj in range(N_CHIPS)]
            for t, (src, dst) in enumerate(parts):
                cp = _remote(src, dst, send_sems.at[w, t], recv_sems.at[w, t], sibling)
                cp.start()
                copies.append(cp)
        for cp in copies:
            cp.wait()

    return pl.pallas_call(
        body, name="grad_swap_halves",
        out_shape=[_compact_shape(n, F32) for n in BIG_NAMES],
        in_specs=[HBM] * nw, out_specs=[HBM] * nw,
        scratch_shapes=[pltpu.SemaphoreType.DMA((nw, N_CHIPS)), pltpu.SemaphoreType.DMA((nw, N_CHIPS))],
    )(*grads)


def _add_halves(name, grad, got, core):
    kind, (r, c) = BIG[name]
    half = _half_rows(name)
    if kind == "col":
        t = 128
        grid = (half // t,)
        g_spec = pl.BlockSpec((t, c), lambda i, cr: (cr[0] * (half // t) + i, 0))
        o_spec = pl.BlockSpec((t, c), lambda i, cr: (i, 0))
    else:
        t = half
        grid = (N_CHIPS,)
        g_spec = pl.BlockSpec((t, c), lambda i, cr: (2 * i + cr[0], 0))
        o_spec = pl.BlockSpec((t, c), lambda i, cr: (i, 0))

    def body(c_ref, g_ref, b_ref, o_ref):
        del c_ref
        o_ref[...] = (g_ref[...] + b_ref[...]).astype(o_ref.dtype)

    return pl.pallas_call(
        body, name="grad_add_" + name,
        grid_spec=pltpu.PrefetchScalarGridSpec(num_scalar_prefetch=1, grid=grid, in_specs=[g_spec, o_spec], out_specs=o_spec),
        out_shape=_compact_shape(name, BF16),
        compiler_params=pltpu.CompilerParams(dimension_semantics=("parallel",), vmem_limit_bytes=VMEM_LIMIT),
    )(core, grad, got)


def _exchange_shards(parts):
    nw = len(BIG_NAMES)

    def piece(ref, name, chip):
        kind, _ = BIG[name]
        r, c = _shard_shape(name)
        if kind == "col":
            return ref.at[:, pl.ds(chip * c, c)]
        return ref.at[pl.ds(chip * (r // 2), r // 2), :]

    def body(*refs):
        ins, outs = refs[:nw], refs[nw:2 * nw]
        send_sems, recv_sems = refs[2 * nw:]
        _, _, c, _, _, others = _place()
        sends = []
        for w, name in enumerate(BIG_NAMES):
            for t, (ox, oy) in enumerate(others):
                cp = _remote(piece(ins[w], name, 2 * ox + oy), outs[w].at[t], send_sems.at[w, t], recv_sems.at[w, t], (ox, oy, c))
                cp.start()
                sends.append(cp)
        for w, name in enumerate(BIG_NAMES):
            for t, (ox, oy) in enumerate(others):
                got = outs[w].at[t]
                _remote(got, got, send_sems.at[w, t], recv_sems.at[w, t], (ox, oy, c)).wait_recv()
        for cp in sends:
            cp.wait_send()

    def out_shape(name):
        r, c = _shard_shape(name)
        return _sds((N_CHIPS - 1, r // 2, c), BF16)

    return pl.pallas_call(
        body, name="grad_exchange_shards",
        out_shape=[out_shape(n) for n in BIG_NAMES],
        in_specs=[HBM] * nw, out_specs=[HBM] * nw,
        scratch_shapes=[pltpu.SemaphoreType.DMA((nw, 3)), pltpu.SemaphoreType.DMA((nw, 3))],
    )(*parts)


def _sum_chips(name, part, got, chip):
    kind, _ = BIG[name]
    _, r, c = got.shape
    t = _tile(r, (128, 352))
    if kind == "col":
        own = pl.BlockSpec((t, c), lambda i, ch: (i, ch[0]))
    else:
        own = pl.BlockSpec((t, c), lambda i, ch: (ch[0] * (r // t) + i, 0))

    def body(ch_ref, p_ref, g_ref, o_ref):
        del ch_ref
        acc = p_ref[...].astype(F32)
        for j in range(N_CHIPS - 1):
            acc = acc + g_ref[j].astype(F32)
        o_ref[...] = acc

    return _pallas(
        body, name="grad_sum_" + name, prefetch=1, grid=(r // t,),
        in_specs=[own, pl.BlockSpec((N_CHIPS - 1, t, c), lambda i, ch: (0, i, 0))],
        out_specs=pl.BlockSpec((t, c), lambda i, ch: (i, 0)),
        out_shape=_sds((r, c), F32), semantics=("parallel",),
    )(chip, part, got)


def _send_halves(sums):
    nw = len(BIG_NAMES)

    def body(*refs):
        ins, outs = refs[:nw], refs[nw:2 * nw]
        send_sems, recv_sems = refs[2 * nw:]
        _, _, _, _, sibling, _ = _place()
        copies = [_remote(ins[w], outs[w], send_sems.at[w], recv_sems.at[w], sibling) for w in range(nw)]
        for cp in copies:
            cp.start()
        for cp in copies:
            cp.wait()

    return pl.pallas_call(
        body, name="grad_send_halves",
        out_shape=[_sds(a.shape, a.dtype) for a in sums],
        in_specs=[HBM] * nw, out_specs=[HBM] * nw,
        scratch_shapes=[pltpu.SemaphoreType.DMA((nw,)), pltpu.SemaphoreType.DMA((nw,))],
    )(*sums)


def _reduce_scatter(grads, core, chip):
    got = _swap_halves(grads)
    parts = [_add_halves(n, grads[i], got[i], core) for i, n in enumerate(BIG_NAMES)]
    gathered = _exchange_shards(parts)
    sums = [_sum_chips(n, parts[i], gathered[i], chip) for i, n in enumerate(BIG_NAMES)]
    return sums, _send_halves(sums)


HI = lax.Precision.HIGHEST
MOD_COLS = 6 * D // N_CHIPS
COND_ROWS = 16


def _silu(v):
    return v * _sigmoid(v)


GATHER_ROWS = 48
FFW_COLS = 2 * DFF // N_CHIPS
CONV_COLS = DC // N_CHIPS


def _pack_cond(c, ffn_w, conv_w):
    def body(c_ref, f_ref, w_ref, o_ref):
        o_ref[...] = jnp.zeros_like(o_ref)
        o_ref[0:1, 0:D] = c_ref[...]
        o_ref[8:11, :] = f_ref[...]
        o_ref[16:16 + CW, 0:CONV_COLS] = w_ref[...]

    return _pallas(body, name="pack_cond", out_shape=_sds((GATHER_ROWS, FFW_COLS), F32))(c, ffn_w, conv_w)


def _unpack_cond(got, c_ctx):
    def body(g_ref, c_ref, cond_ref, f_ref, w_ref):
        cond_ref[...] = jnp.zeros_like(cond_ref)
        for d in range(8):
            cond_ref[d:d + 1, :] = g_ref[d * GATHER_ROWS:d * GATHER_ROWS + 1, 0:D]
        cond_ref[8:9, :] = c_ref[...]
        for j in range(N_CHIPS):
            r0 = 2 * j * GATHER_ROWS
            f_ref[:, j * FFW_COLS:(j + 1) * FFW_COLS] = g_ref[r0 + 8:r0 + 11, :]
            w_ref[:, j * CONV_COLS:(j + 1) * CONV_COLS] = g_ref[r0 + 16:r0 + 16 + CW, 0:CONV_COLS]

    return _pallas(body, name="unpack_cond",
                   out_shape=[_sds((COND_ROWS, D), F32), _sds((3, 2 * DFF), F32), _sds((CW, DC), F32)])(got, c_ctx)


def _chip_cols(rows, width):
    return pl.BlockSpec((rows, width), lambda i, ch: (0, ch[0]))


def _whole(shape):
    return pl.BlockSpec(shape, lambda i, ch: (0,) * len(shape))


def _mod_shard(cond, w_mod, b_mod, chip):
    def body(ch_ref, c_ref, w_ref, b_ref, o_ref):
        del ch_ref
        o_ref[...] = jnp.dot(_silu(c_ref[...]), w_ref[...], preferred_element_type=F32, precision=HI) + b_ref[...]

    return _pallas(body, name="mod_fwd", prefetch=1, grid=(1,),
                   in_specs=[_whole((COND_ROWS, D)), _whole((D, MOD_COLS)), _chip_cols(1, MOD_COLS)],
                   out_specs=_whole((COND_ROWS, MOD_COLS)),
                   out_shape=_sds((COND_ROWS, MOD_COLS), F32))(chip, cond, w_mod, b_mod)


def _unpack_mod(mods, dev):
    def body(dev_ref, m_ref, me_ref, c_ref):
        rowi = lax.broadcasted_iota(jnp.int32, (COND_ROWS, MOD_COLS), 0)
        mine, ctx = [], []
        for j in range(N_CHIPS):
            blk = m_ref[2 * j * COND_ROWS:(2 * j + 1) * COND_ROWS, :]
            mine.append(jnp.sum(jnp.where(rowi == dev_ref[0], blk, 0.0), axis=0, keepdims=True))
            ctx.append(blk[8:9, :])
        mine = jnp.concatenate(mine, axis=1)
        ctx = jnp.concatenate(ctx, axis=1)
        for k in range(6):
            me_ref[k:k + 1, :] = mine[:, k * D:(k + 1) * D]
        for k in range(2):
            c_ref[k:k + 1, :] = ctx[:, k * D:(k + 1) * D]

    return _pallas(body, name="unpack_mod", prefetch=1, grid=(1,),
                   in_specs=[_whole(mods.shape)], out_specs=[_whole((6, D)), _whole((2, D))],
                   out_shape=[_sds((6, D), F32), _sds((2, D), F32)])(dev, mods)


def _mod_weight_grad(cond, dmod_all, chip):
    def body(ch_ref, c_ref, d_ref, o_ref):
        del ch_ref
        o_ref[...] = lax.dot_general(_silu(c_ref[...]), d_ref[...], _TN, preferred_element_type=F32, precision=HI)

    return _pallas(body, name="mod_weight_grad", prefetch=1, grid=(1,),
                   in_specs=[_whole((COND_ROWS, D)), _chip_cols(COND_ROWS, MOD_COLS)], out_specs=_whole((D, MOD_COLS)),
                   out_shape=_sds((D, MOD_COLS), F32))(chip, cond, dmod_all)


def _cond_grad_partial(dmod_all, w_mod, chip):
    def body(ch_ref, d_ref, w_ref, o_ref):
        del ch_ref
        o_ref[...] = lax.dot_general(d_ref[...], w_ref[...], (((1,), (1,)), ((), ())), preferred_element_type=F32, precision=HI)

    return _pallas(body, name="cond_grad_partial", prefetch=1, grid=(1,),
                   in_specs=[pl.BlockSpec((8, MOD_COLS), lambda i, ch: (1, ch[0])), _whole((D, MOD_COLS))],
                   out_specs=_whole((8, D)), out_shape=_sds((8, D), F32))(chip, dmod_all, w_mod)


def _adam_math(w, g, m, v):
    nm = ADAM_B1 * m + (1.0 - ADAM_B1) * g
    nv = ADAM_B2 * v + (1.0 - ADAM_B2) * (g * g)
    c1 = 1.0 - ADAM_B1 ** ADAM_STEP
    c2 = 1.0 - ADAM_B2 ** ADAM_STEP
    return -ADAM_LR * ((nm / c1) / (jnp.sqrt(nv / c2) + ADAM_EPS) + ADAM_WD * w), nm, nv


def _cond_update(parts, c_ctx, m, v):
    def body(p_ref, c_ref, m_ref, v_ref, g_ref, d_ref, nm_ref, nv_ref):
        tot = p_ref[0:1, :]
        for j in range(1, N_CHIPS):
            tot = tot + p_ref[16 * j:16 * j + 1, :]
        cv = c_ref[...]
        sg = _sigmoid(cv)
        g = tot * (sg * (1.0 + cv * (1.0 - sg)))
        g_ref[...] = g
        d_ref[...], nm_ref[...], nv_ref[...] = _adam_math(cv, g, m_ref[...], v_ref[...])

    return _pallas(body, name="cond_update", out_shape=[_sds((1, D), F32)] * 4)(parts, c_ctx, m, v)


def _adamw(w, g, m, v, name):
    r, c = w.shape
    t = _tile(r, (128,)) if r % 128 == 0 and r > 128 else r

    def body(w_ref, g_ref, m_ref, v_ref, d_ref, nm_ref, nv_ref):
        d_ref[...], nm_ref[...], nv_ref[...] = _adam_math(w_ref[...], g_ref[...], m_ref[...], v_ref[...])

    blk = pl.BlockSpec((t, c), lambda i: (i, 0))
    return _pallas(body, name=name, grid=(r // t,), in_specs=[blk] * 4, out_specs=[blk] * 3,
                   out_shape=[_sds((r, c), F32)] * 3, semantics=("parallel",))(w, g, m, v)


def _adamw_cols(w, g_all, m, v, chip, name):
    r, c = w.shape

    def body(ch_ref, w_ref, g_ref, m_ref, v_ref, go_ref, d_ref, nm_ref, nv_ref):
        del ch_ref
        g = g_ref[...]
        go_ref[...] = g
        d_ref[...], nm_ref[...], nv_ref[...] = _adam_math(w_ref[...], g, m_ref[...], v_ref[...])

    return _pallas(body, name=name, prefetch=1, grid=(1,),
                   in_specs=[_whole((r, c)), _chip_cols(r, c), _whole((r, c)), _whole((r, c))],
                   out_specs=[_whole((r, c))] * 4, out_shape=[_sds((r, c), F32)] * 4)(chip, w, g_all, m, v)


def _adamw_halves(name, w, own, other, m, v, core):
    r, c = w.shape
    half = r // 2
    t = _tile(half, (128, 352))
    nh = half // t

    def pick(mine):
        def index(i, cr):
            first = cr[0] if mine else 1 - cr[0]
            return (jnp.clip(i - first * nh, 0, nh - 1), 0)
        return pl.BlockSpec((t, c), index)

    def body(c_ref, w_ref, own_ref, oth_ref, m_ref, v_ref, g_ref, d_ref, nm_ref, nv_ref):
        g = jnp.where(pl.program_id(0) // nh == c_ref[0], own_ref[...], oth_ref[...])
        g_ref[...] = g
        d_ref[...], nm_ref[...], nv_ref[...] = _adam_math(w_ref[...], g, m_ref[...], v_ref[...])

    blk = pl.BlockSpec((t, c), lambda i, cr: (i, 0))
    return _pallas(body, name="adamw_" + name, prefetch=1, grid=(2 * nh,),
                   in_specs=[blk, pick(True), pick(False), blk, blk], out_specs=[blk] * 4,
                   out_shape=[_sds((r, c), F32)] * 4, semantics=("parallel",))(core, w, own, other, m, v)


WEIGHTS = ("c_ctx", "w_mod", "b_mod", "g_norm1", "w_in", "rpb", "conv_w", "conv_b", "ln_g", "ln_b", "w_out", "g_norm2",
           "w_up", "ffn_conv_w", "ffn_conv_b", "w_down", "g_final")
PACK = (("dmod", 6 * D), ("dmod_c", 2 * D), ("g_norm1", D), ("g_norm1_ctx", D), ("g_norm2", D), ("g_final", D),
        ("conv_b", DC), ("ln_g", DC), ("ln_b", DC), ("ffn_conv_b", 2 * DFF), ("ffn_conv_w", 3 * 2 * DFF),
        ("conv_w", CW * DC), ("rpb_rev", NH * 16 * LANES), ("loss", LANES))
PACK_OFF = {}
_o = 0
for _n, _w in PACK:
    PACK_OFF[_n] = (_o, _w)
    _o += _w
PACK_N = -(-_o // (8 * LANES)) * (8 * LANES)
VECTORS = {"b_mod": (6 * D, ("dmod", "dmod_c")), "g_norm1": (D, ("g_norm1", "g_norm1_ctx")), "conv_b": (DC, ("conv_b",)),
           "ln_g": (DC, ("ln_g",)), "ln_b": (DC, ("ln_b",)), "g_norm2": (D, ("g_norm2",)),
           "ffn_conv_b": (2 * DFF, ("ffn_conv_b",)), "g_final": (D, ("g_final",))}
RPB_ROWS = NH * (2 * NA_ROWS - 1)
RPB_COLS = 4 * NA_ROWS - 1


def _pack_small(parts):
    arrs, places = [], []
    for name, _ in PACK:
        off, width = PACK_OFF[name]
        group = parts[name]
        rows = group[0].shape[0]
        row_w = sum(a.shape[1] for a in group)
        assert rows * row_w == width, (name, rows, row_w, width)
        col = 0
        for a in group:
            arrs.append(a)
            places.append([off + k * row_w + col for k in range(rows)])
            col += a.shape[1]

    def body(*refs):
        o_ref = refs[-1]
        o_ref[:, _o:PACK_N] = jnp.zeros((1, PACK_N - _o), F32)
        for ref, offs in zip(refs, places):
            n = ref.shape[1]
            for k, off in enumerate(offs):
                o_ref[:, off:off + n] = ref[k:k + 1, :]

    return _pallas(body, name="pack_small_grads", out_shape=_sds((1, PACK_N), F32))(*arrs)


def _small_update(packs, w, m, v):
    names = list(VECTORS)

    def body(*refs):
        it = iter(refs)
        p_ref = next(it)
        wmv = {n: (next(it), next(it), next(it)) for n in names}
        outs = {n: (next(it), next(it), next(it), next(it)) for n in names}
        dmod_ref, cw_ref, fw_ref, rpb_ref, loss_ref = next(it), next(it), next(it), next(it), next(it)

        def total(name):
            off, width = PACK_OFF[name]
            acc = p_ref[0:1, off:off + width]
            for d in range(1, 8):
                acc = acc + p_ref[d:d + 1, off:off + width]
            return acc

        for n in names:
            width, segs = VECTORS[n]
            g = total(segs[0])
            if len(segs) > 1:
                extra = total(segs[1])
                ew = extra.shape[1]
                g = g + extra if ew == width else jnp.concatenate([g[:, :ew] + extra, g[:, ew:]], axis=1)
            w_ref, m_ref, v_ref = wmv[n]
            g_ref, d_ref, nm_ref, nv_ref = outs[n]
            g_ref[...] = g
            d_ref[...], nm_ref[...], nv_ref[...] = _adam_math(w_ref[...], g, m_ref[...], v_ref[...])

        o_dmod = PACK_OFF["dmod"][0]
        dmod_ref[...] = jnp.zeros_like(dmod_ref)
        dmod_ref[0:8, :] = p_ref[:, o_dmod:o_dmod + 6 * D]
        dmod_ref[8:9, 0:2 * D] = total("dmod_c")
        for ref, name, rows in ((cw_ref, "conv_w", CW), (fw_ref, "ffn_conv_w", 3), (rpb_ref, "rpb_rev", NH * 16)):
            flat = total(name)
            n = ref.shape[1]
            for k in range(rows):
                ref[k:k + 1, :] = flat[:, k * n:(k + 1) * n]
        loss_ref[...] = total("loss")

    ins = [packs] + [a[n] for n in names for a in (w, m, v)]
    out_shape = [_sds((1, VECTORS[n][0]), F32) for n in names for _ in range(4)]
    out_shape += [_sds((COND_ROWS, 6 * D), F32), _sds((CW, DC), F32), _sds((3, 2 * DFF), F32), _sds((NH * 16, LANES), F32),
                  _sds((1, LANES), F32)]
    res = _pallas(body, name="small_update", out_shape=out_shape)(*ins)
    per = {n: tuple(res[4 * i:4 * i + 4]) for i, n in enumerate(names)}
    return (per, *res[4 * len(names):])


def _rpb_update(rev, w, m, v):
    def body(r_ref, w_ref, m_ref, v_ref, g_ref, d_ref, nm_ref, nv_ref):
        li = lax.broadcasted_iota(jnp.int32, (LANES, LANES), 0)
        co = lax.broadcasted_iota(jnp.int32, (LANES, LANES), 1)
        lane_of_co0 = GW - 1 + RPB_COLS // 2
        unflip = jnp.where((li == lane_of_co0 - co) & (co < RPB_COLS), 1.0, 0.0).astype(F32)
        g_all = jnp.dot(r_ref[...], unflip, preferred_element_type=F32, precision=HI)
        nr = 2 * NA_ROWS - 1
        for h in range(NH):
            rows = slice(h * nr, (h + 1) * nr)
            g = g_all[h * 16:h * 16 + nr, 0:RPB_COLS]
            g_ref[rows, :] = g
            d_ref[rows, :], nm_ref[rows, :], nv_ref[rows, :] = _adam_math(w_ref[rows, :], g, m_ref[rows, :], v_ref[rows, :])

    return _pallas(body, name="rpb_update", out_shape=[_sds((RPB_ROWS, RPB_COLS), F32)] * 4)(rev, w, m, v)


def kernel(x, c, ctx, c_ctx, w_mod, b_mod, g_norm1, w_in, rpb, conv_w, conv_b, ln_g, ln_b, w_out, g_norm2, w_up, ffn_conv_w, ffn_conv_b, w_down, g_final, loss_target, m_c_ctx, m_w_mod, m_b_mod, m_g_norm1, m_w_in, m_rpb, m_conv_w, m_conv_b, m_ln_g, m_ln_b, m_w_out, m_g_norm2, m_w_up, m_ffn_conv_w, m_ffn_conv_b, m_w_down, m_g_final, v_c_ctx, v_w_mod, v_b_mod, v_g_norm1, v_w_in, v_rpb, v_conv_w, v_conv_b, v_ln_g, v_ln_b, v_w_out, v_g_norm2, v_w_up, v_ffn_conv_w, v_ffn_conv_b, v_w_down, v_g_final):
    w = dict(c_ctx=c_ctx, w_mod=w_mod, b_mod=b_mod, g_norm1=g_norm1, w_in=w_in, rpb=rpb, conv_w=conv_w, conv_b=conv_b,
             ln_g=ln_g, ln_b=ln_b, w_out=w_out, g_norm2=g_norm2, w_up=w_up, ffn_conv_w=ffn_conv_w, ffn_conv_b=ffn_conv_b,
             w_down=w_down, g_final=g_final)
    mom = dict(c_ctx=m_c_ctx, w_mod=m_w_mod, b_mod=m_b_mod, g_norm1=m_g_norm1, w_in=m_w_in, rpb=m_rpb, conv_w=m_conv_w,
               conv_b=m_conv_b, ln_g=m_ln_g, ln_b=m_ln_b, w_out=m_w_out, g_norm2=m_g_norm2, w_up=m_w_up,
               ffn_conv_w=m_ffn_conv_w, ffn_conv_b=m_ffn_conv_b, w_down=m_w_down, g_final=m_g_final)
    var = dict(c_ctx=v_c_ctx, w_mod=v_w_mod, b_mod=v_b_mod, g_norm1=v_g_norm1, w_in=v_w_in, rpb=v_rpb, conv_w=v_conv_w,
               conv_b=v_conv_b, ln_g=v_ln_g, ln_b=v_ln_b, w_out=v_w_out, g_norm2=v_g_norm2, w_up=v_w_up,
               ffn_conv_w=v_ffn_conv_w, ffn_conv_b=v_ffn_conv_b, w_down=v_w_down, g_final=v_g_final)
    xi, yi, ci = lax.axis_index("x"), lax.axis_index("y"), lax.axis_index("c")
    dev = (4 * xi + 2 * yi + ci).astype(jnp.int32).reshape(1)
    chip = (2 * xi + yi).astype(jnp.int32).reshape(1)
    core = ci.astype(jnp.int32).reshape(1)
    c_ctx2 = c_ctx.reshape(1, D)
    g_final2 = g_final.reshape(1, D)
    mom["g_final"], var["g_final"] = m_g_final.reshape(1, D), v_g_final.reshape(1, D)

    got = _gather_small(_pack_cond(c, ffn_conv_w[0], conv_w[0]), "gather_cond")
    cond, ffn_w_all, conv_w_all = _unpack_cond(got, c_ctx2)

    mods = _gather_small(_mod_shard(cond, w_mod[0], b_mod, chip), "gather_mod")
    mod_me, mod_c = _unpack_mod(mods, dev)

    whole = _gather_weights([_cast_into_whole(n, w[n][0], chip) for n in BIG_NAMES])

    rpb_rev = jnp.pad(rpb[0][:, :, ::-1], ((0, 0), (0, 1), (48, LANES - 48 - RPB_COLS))).reshape(NH * 16, LANES)
    vec = dict(g_norm1=g_norm1, g_norm2=g_norm2, g_final=g_final2, conv_w=conv_w_all, conv_b=conv_b, ln_g=ln_g, ln_b=ln_b,
               ffn_conv_w=ffn_w_all, ffn_conv_b=ffn_conv_b)
    loss_p, grad_x, d_in, d_out, d_up, d_down, small = _local_step(
        x[0], ctx[0], loss_target[0], mod_me, mod_c, vec, *whole, rpb_rev)

    own, other = _reduce_scatter([d_in, d_out, d_up, d_down], core, chip)

    parts = dict(dmod=small["dmod"], dmod_c=small["dmod_c"], g_norm1=[small["g_norm1"][0]], g_norm1_ctx=[small["g_norm1"][1]],
                 g_norm2=[small["g_norm2"]], g_final=[small["g_final"]], conv_b=[small["conv_b"]], ln_g=[small["ln_g"]],
                 ln_b=[small["ln_b"]], ffn_conv_b=small["ffn_conv_b"], ffn_conv_w=small["ffn_conv_w"],
                 conv_w=[small["conv_w"]], rpb_rev=[small["rpb_rev"]], loss=[loss_p])
    pack = _pack_small(parts).reshape(8, PACK_N // 8)
    packs = _gather_small(pack, "gather_small_grads").reshape(8, PACK_N)
    w2 = dict(w, g_final=g_final2)
    per, dmod_all, g_conv_w_all, g_ffn_w_all, g_rpb_rev, loss_row = _small_update(packs, w2, mom, var)

    out = {}
    out.update(per)
    out["c_ctx"] = _cond_update(
        _gather_small(_cond_grad_partial(dmod_all, w_mod[0], chip), "gather_cond_grad"),
        c_ctx2, m_c_ctx.reshape(1, D), v_c_ctx.reshape(1, D))
    g_w_mod = _mod_weight_grad(cond, dmod_all, chip)
    out["w_mod"] = (g_w_mod, *_adamw(w_mod[0], g_w_mod, m_w_mod[0], v_w_mod[0], "adamw_w_mod"))
    for i, n in enumerate(BIG_NAMES):
        out[n] = _adamw_halves(n, w[n][0], own[i], other[i], mom[n][0], var[n][0], core)
    out["conv_w"] = _adamw_cols(conv_w[0], g_conv_w_all, m_conv_w[0], v_conv_w[0], chip, "adamw_conv_w")
    out["ffn_conv_w"] = _adamw_cols(ffn_conv_w[0], g_ffn_w_all, m_ffn_conv_w[0], v_ffn_conv_w[0], chip, "adamw_ffn_conv_w")
    flat = lambda a: a.reshape(RPB_ROWS, RPB_COLS)
    out["rpb"] = _rpb_update(g_rpb_rev, flat(rpb), flat(m_rpb), flat(v_rpb))

    res = [[out[n][k].reshape(w[n].shape) for n in WEIGHTS] for k in range(4)]
    return (loss_row[0, 0], grad_x[None], *res[0], *res[1], *res[2], *res[3])
```

```python
import functools

import jax
import jax.numpy as jnp
from jax import lax
from jax.experimental import pallas as pl
from jax.experimental.pallas import tpu as pltpu

F32 = jnp.float32
BF16 = jnp.bfloat16
MXU_DTYPE = jnp.bfloat16

D = 1024
CTX = 256
GW = 64
DA = 512
NH = 8
HD = 64
DC = 512
CW = 31
DFF = 2816
NIN = 3 * DA + 2 * DC
EPS = 1e-6
SCALE = HD ** -0.5
NEG = -1e30
NA_ROWS = 8
PAIR_ROWS = NA_ROWS + 1
TAB_BLOCKS = 17
LANES = 128
VMEM_LIMIT = 56 * 1024 * 1024

ADAM_LR = 0.001
ADAM_B1 = 0.9
ADAM_B2 = 0.999
ADAM_EPS = 1e-08
ADAM_WD = 0.01
ADAM_STEP = 10

MESH = pl.DeviceIdType.MESH


def _pallas(body, *, name, semantics=None, vmem=VMEM_LIMIT, prefetch=0, **kw):
    params = dict(vmem_limit_bytes=vmem)
    if semantics is not None:
        params["dimension_semantics"] = semantics
    if prefetch:
        kw["grid_spec"] = pltpu.PrefetchScalarGridSpec(
            num_scalar_prefetch=prefetch, grid=kw.pop("grid"), in_specs=kw.pop("in_specs"), out_specs=kw.pop("out_specs"),
            scratch_shapes=kw.pop("scratch_shapes", ()))
    return pl.pallas_call(body, name=name, compiler_params=pltpu.CompilerParams(**params), **kw)


def _sds(shape, dtype):
    return jax.ShapeDtypeStruct(shape, dtype)


def _vec_spec(n):
    return pl.BlockSpec((1, n), lambda *_: (0, 0))


def _colsum8(x):
    t, n = x.shape
    return jnp.sum(x.reshape(t // 8, 8, n), axis=0)


def _sigmoid(x):
    return 0.5 * jnp.tanh(0.5 * x) + 0.5


def _pieces(arrs, tile):
    lo, out = 0, []
    for a in arrs:
        nt = a.shape[1] // tile
        assert nt * tile == a.shape[1], (a.shape, tile)
        out.append((lo, nt))
        lo += nt
    return out


def _mm(a, b, *, mode, m, n, k, tm, tn, tk, out_dtype, name, a_off=(0, 0), b_off=(0, 0), k_outer=False):
    a_list = list(a) if isinstance(a, (list, tuple)) else [a]
    b_list = list(b) if isinstance(b, (list, tuple)) else [b]
    assert m % tm == 0 and n % tn == 0 and k % tk == 0, (name, m, n, k, tm, tn, tk)
    gi, gj, nk = m // tm, n // tn, k // tk
    a_tile = tm if mode == "tn" else tk
    a_pc = _pieces(a_list, a_tile) if len(a_list) > 1 else [(0, 1 << 30)]
    if mode == "nt":
        assert len(b_list) == 1
    b_pc = _pieces(b_list, tn) if len(b_list) > 1 else [(0, 1 << 30)]
    dims = {"nn": (((1,), (0,)), ((), ())), "nt": (((1,), (1,)), ((), ())), "tn": (((0,), (0,)), ((), ()))}[mode]
    k_outer = k_outer and nk > 1

    def ijk(fn):
        return (lambda i, kk, j: fn(i, j, kk)) if k_outer else fn

    def a_spec(lo, cnt):
        def loc(idx):
            return idx + a_off[1] if len(a_list) == 1 else jnp.clip(idx - lo, 0, cnt - 1)
        if mode == "tn":
            return pl.BlockSpec((tk, tm), ijk(lambda i, j, kk: (kk + a_off[0], loc(i))))
        return pl.BlockSpec((tm, tk), ijk(lambda i, j, kk: (i + a_off[0], loc(kk))))

    def b_spec(lo, cnt):
        def loc(idx):
            return idx + b_off[1] if len(b_list) == 1 else jnp.clip(idx - lo, 0, cnt - 1)
        if mode == "nt":
            return pl.BlockSpec((tn, tk), ijk(lambda i, j, kk: (j + b_off[0], kk + b_off[1])))
        return pl.BlockSpec((tk, tn), ijk(lambda i, j, kk: (kk + b_off[0], loc(j))))

    na, nb = len(a_list), len(b_list)
    in_place = nk > 1 and out_dtype == F32 and not k_outer

    def body(*refs):
        a_refs, b_refs, o_ref = refs[:na], refs[na:na + nb], refs[na + nb]
        if k_outer:
            i, kk, j = pl.program_id(0), pl.program_id(1), pl.program_id(2)
            acc = refs[na + nb + 1].at[j]
        else:
            i, j, kk = pl.program_id(0), pl.program_id(1), pl.program_id(2)
            acc = o_ref if in_place else (refs[na + nb + 1] if nk > 1 else None)
        a_idx = i if mode == "tn" else kk

        def step(ar, br):
            p = lax.dot_general(ar[...].astype(MXU_DTYPE), br[...].astype(MXU_DTYPE), dims,
                                preferred_element_type=F32)
            if nk == 1:
                o_ref[...] = p.astype(out_dtype)
                return

            @pl.when(kk == 0)
            def _():
                acc[...] = p

            @pl.when(kk > 0)
            def _():
                acc[...] += p

            if not in_place:
                @pl.when(kk == nk - 1)
                def _():
                    o_ref[...] = acc[...].astype(out_dtype)

        for pa, (alo, acnt) in enumerate(a_pc):
            for pb, (blo, bcnt) in enumerate(b_pc):
                if na == 1 and nb == 1:
                    step(a_refs[0], b_refs[0])
                else:
                    cond = (a_idx >= alo) & (a_idx < alo + acnt) & (j >= blo) & (j < blo + bcnt)
                    pl.when(cond)(functools.partial(step, a_refs[pa], b_refs[pb]))

    if k_outer:
        grid = (gi, nk, gj)
        o_spec = pl.BlockSpec((tm, tn), lambda i, kk, j: (i, jnp.where(kk == nk - 1, j, 0)))
        scratch = [pltpu.VMEM((gj, tm, tn), F32)]
        semantics = ("parallel", "arbitrary", "arbitrary")
    else:
        grid = (gi, gj, nk)
        o_spec = pl.BlockSpec((tm, tn), lambda i, j, kk: (i, j))
        scratch = [pltpu.VMEM((tm, tn), F32)] if nk > 1 and not in_place else []
        semantics = ("parallel", "parallel", "arbitrary")
    return _pallas(
        body, name=name, grid=grid,
        in_specs=[a_spec(*p) for p in a_pc] + [b_spec(*p) for p in b_pc],
        out_specs=o_spec, out_shape=_sds((m, n), out_dtype), scratch_shapes=scratch, semantics=semantics,
    )(*a_list, *b_list)


ROW_TILE = 256


def _rmsmod_fwd(x, ctx, g, sc, sh, csc, csh):
    s = x.shape[0]
    nt = s // ROW_TILE
    assert ctx.shape[0] == ROW_TILE

    def body(x_ref, c_ref, g_ref, sc_ref, sh_ref, csc_ref, csh_ref, o_ref):
        is_ctx = pl.program_id(0) == nt
        xv = jnp.where(is_ctx, c_ref[...], x_ref[...])
        scv = jnp.where(is_ctx, csc_ref[...], sc_ref[...])
        shv = jnp.where(is_ctx, csh_ref[...], sh_ref[...])
        r = lax.rsqrt(jnp.mean(xv * xv, axis=-1, keepdims=True) + EPS)
        y = xv * r * g_ref[...]
        o_ref[...] = (y * (1.0 + scv) + shv).astype(o_ref.dtype)

    return _pallas(
        body, name="rmsmod1_fwd", grid=(nt + 1,),
        in_specs=[pl.BlockSpec((ROW_TILE, D), lambda i: (jnp.minimum(i, nt - 1), 0)),
                  pl.BlockSpec((ROW_TILE, D), lambda i: (0, 0))] + [_vec_spec(D)] * 5,
        out_specs=pl.BlockSpec((ROW_TILE, D), lambda i: (i, 0)),
        out_shape=_sds((s + CTX, D), MXU_DTYPE),
        semantics=("arbitrary",),
    )(x, ctx, g, sc, sh, csc, csh)


def _resid_rmsmod_fwd(x, y, gt, g, sc, sh):
    s = x.shape[0]

    def body(x_ref, y_ref, gt_ref, g_ref, sc_ref, sh_ref, x1_ref, h_ref):
        x1 = x_ref[...] + gt_ref[...] * y_ref[...]
        x1_ref[...] = x1
        r = lax.rsqrt(jnp.mean(x1 * x1, axis=-1, keepdims=True) + EPS)
        h_ref[...] = ((x1 * r * g_ref[...]) * (1.0 + sc_ref[...]) + sh_ref[...]).astype(h_ref.dtype)

    row = pl.BlockSpec((ROW_TILE, D), lambda i: (i, 0))
    return _pallas(
        body, name="resid_rmsmod2_fwd", grid=(s // ROW_TILE,),
        in_specs=[row, row] + [_vec_spec(D)] * 4,
        out_specs=[row, row],
        out_shape=[_sds((s, D), F32), _sds((s, D), MXU_DTYPE)],
        semantics=("parallel",),
    )(x, y, gt, g, sc, sh)


def _final_fwd_bwd(x1, z, gt2, gf, tgt):
    s = x1.shape[0]
    nt = s // ROW_TILE

    def body(x1_ref, z_ref, gt_ref, gf_ref, t_ref, dx2_ref, dz_ref, loss_ref, dgt_ref, dgf_ref, a_loss, a_gt, a_gf):
        i = pl.program_id(0)

        @pl.when(i == 0)
        def _():
            a_loss[...] = jnp.zeros_like(a_loss)
            a_gt[...] = jnp.zeros_like(a_gt)
            a_gf[...] = jnp.zeros_like(a_gf)

        zv = z_ref[...]
        gt = gt_ref[...]
        gf_ = gf_ref[...]
        x2 = x1_ref[...] + gt * zv
        r = lax.rsqrt(jnp.mean(x2 * x2, axis=-1, keepdims=True) + EPS)
        xn = x2 * r
        e = xn * gf_ - t_ref[...]
        a_loss[...] += _colsum8(e * e)
        dyo = e * (1.0 / D)
        a_gf[...] += _colsum8(dyo * xn)
        gdy = gf_ * dyo
        dx2 = r * gdy - xn * (r * r) * jnp.mean(x2 * gdy, axis=-1, keepdims=True)
        dx2_ref[...] = dx2
        dz_ref[...] = (gt * dx2).astype(dz_ref.dtype)
        a_gt[...] += _colsum8(dx2 * zv)

        @pl.when(i == nt - 1)
        def _():
            tot = jnp.sum(jnp.sum(a_loss[...], axis=0, keepdims=True), axis=1, keepdims=True) * (0.5 / D)
            loss_ref[...] = jnp.broadcast_to(tot, loss_ref.shape)
            dgt_ref[...] = jnp.sum(a_gt[...], axis=0, keepdims=True)
            dgf_ref[...] = jnp.sum(a_gf[...], axis=0, keepdims=True)

    row = pl.BlockSpec((ROW_TILE, D), lambda i: (i, 0))
    return _pallas(
        body, name="final_norm_loss", grid=(nt,),
        in_specs=[row, row, _vec_spec(D), _vec_spec(D), row],
        out_specs=[row, row, _vec_spec(LANES), _vec_spec(D), _vec_spec(D)],
        out_shape=[_sds((s, D), F32), _sds((s, D), MXU_DTYPE), _sds((1, LANES), F32), _sds((1, D), F32), _sds((1, D), F32)],
        scratch_shapes=[pltpu.VMEM((8, D), F32)] * 3,
        semantics=("arbitrary",),
    )(x1, z, gt2, gf, tgt)


def _rmsmod_bwd(xin, dh, g, sc, *, name, dh_row_off=0, add=None, resid=None):
    s = xin.shape[0]
    nt = s // ROW_TILE
    want_dx = add is not None
    assert resid is None or want_dx

    def body(*refs):
        it = iter(refs)
        x_ref, dh_ref, g_ref, sc_ref = next(it), next(it), next(it), next(it)
        add_ref = next(it) if want_dx else None
        gt_ref, y_ref = (next(it), next(it)) if resid is not None else (None, None)
        dsh_ref, dsc_ref, dg_ref = next(it), next(it), next(it)
        dx_ref = next(it) if want_dx else None
        dy_ref, dgt_ref = (next(it), next(it)) if resid is not None else (None, None)
        a_sh, a_sc, a_g = next(it), next(it), next(it)
        a_gt = next(it) if resid is not None else None
        i = pl.program_id(0)

        @pl.when(i == 0)
        def _():
            a_sh[...] = jnp.zeros_like(a_sh)
            a_sc[...] = jnp.zeros_like(a_sc)
            a_g[...] = jnp.zeros_like(a_g)
            if a_gt is not None:
                a_gt[...] = jnp.zeros_like(a_gt)

        xv = x_ref[...]
        dhv = dh_ref[...]
        gv = g_ref[...]
        r = lax.rsqrt(jnp.mean(xv * xv, axis=-1, keepdims=True) + EPS)
        xn = xv * r
        a_sh[...] += _colsum8(dhv)
        a_sc[...] += _colsum8(dhv * (xn * gv))
        dn = dhv * (1.0 + sc_ref[...])
        a_g[...] += _colsum8(dn * xn)
        if want_dx:
            gdn = gv * dn
            dx = add_ref[...] + r * gdn - xn * (r * r) * jnp.mean(xv * gdn, axis=-1, keepdims=True)
            dx_ref[...] = dx
            if resid is not None:
                dy_ref[...] = (gt_ref[...] * dx).astype(dy_ref.dtype)
                a_gt[...] += _colsum8(dx * y_ref[...])

        @pl.when(i == nt - 1)
        def _():
            dsh_ref[...] = jnp.sum(a_sh[...], axis=0, keepdims=True)
            dsc_ref[...] = jnp.sum(a_sc[...], axis=0, keepdims=True)
            dg_ref[...] = jnp.sum(a_g[...], axis=0, keepdims=True)
            if a_gt is not None:
                dgt_ref[...] = jnp.sum(a_gt[...], axis=0, keepdims=True)

    row = pl.BlockSpec((ROW_TILE, D), lambda i: (i, 0))
    ins = [xin, dh, g, sc]
    in_specs = [row, pl.BlockSpec((ROW_TILE, D), lambda i: (i + dh_row_off, 0)), _vec_spec(D), _vec_spec(D)]
    out_specs = [_vec_spec(D)] * 3
    out_shape = [_sds((1, D), F32)] * 3
    scratch = [pltpu.VMEM((8, D), F32)] * 3
    if want_dx:
        ins.append(add)
        in_specs.append(row)
        out_specs.append(row)
        out_shape.append(_sds((s, D), F32))
    if resid is not None:
        ins += [resid[0], resid[1]]
        in_specs += [_vec_spec(D), row]
        out_specs += [row, _vec_spec(D)]
        out_shape += [_sds((s, D), MXU_DTYPE), _sds((1, D), F32)]
        scratch.append(pltpu.VMEM((8, D), F32))
    return _pallas(body, name=name, grid=(nt,), in_specs=in_specs, out_specs=out_specs, out_shape=out_shape,
                   scratch_shapes=scratch, semantics=("arbitrary",))(*ins)


FF_TILE = 128
FF_CHUNK = 128
HALO = 8


def _shift3(pad_ref, r0, ch):
    return tuple(pad_ref[pl.ds(r0 + HALO + d, ch), :] for d in (-1, 0, 1))


def _fill_padded(pad_ref, src_ref, s, ch, halo):
    zeros = jnp.zeros((halo, pad_ref.shape[1]), F32)
    pad_ref[0:halo, :] = zeros
    pad_ref[s + halo:s + 2 * halo, :] = zeros

    def cp(c, carry):
        r0 = pl.multiple_of(c * ch, ch)
        pad_ref[pl.ds(r0 + halo, ch), :] = src_ref[pl.ds(r0, ch), :].astype(F32)
        return carry

    lax.fori_loop(0, s // ch, cp, 0)


def _ffn_act_fwd(u, w, b):
    s = u.shape[0]
    nj = DFF // FF_TILE
    ch = FF_CHUNK

    def body(ug_ref, uv_ref, wg_ref, wv_ref, bg_ref, bv_ref, f_ref, gpad, vpad):
        _fill_padded(gpad, ug_ref, s, ch, HALO)
        _fill_padded(vpad, uv_ref, s, ch, HALO)

        def conv(pad, w_ref, b_ref, r0):
            prev, cur, nxt = _shift3(pad, r0, ch)
            return w_ref[0:1, :] * prev + w_ref[1:2, :] * cur + w_ref[2:3, :] * nxt + b_ref[...]

        def step(c, carry):
            r0 = pl.multiple_of(c * ch, ch)
            gc = conv(gpad, wg_ref, bg_ref, r0)
            vc = conv(vpad, wv_ref, bv_ref, r0)
            f_ref[pl.ds(r0, ch), :] = (gc * _sigmoid(gc) * vc).astype(f_ref.dtype)
            return carry

        lax.fori_loop(0, s // ch, step, 0)

    col = lambda off: pl.BlockSpec((s, FF_TILE), lambda j: (0, j + off))
    wsp = lambda off: pl.BlockSpec((3, FF_TILE), lambda j: (0, j + off))
    bsp = lambda off: pl.BlockSpec((1, FF_TILE), lambda j: (0, j + off))
    return _pallas(
        body, name="ffn_act_fwd", grid=(nj,),
        in_specs=[col(0), col(nj), wsp(0), wsp(nj), bsp(0), bsp(nj)],
        out_specs=col(0), out_shape=_sds((s, DFF), MXU_DTYPE),
        scratch_shapes=[pltpu.VMEM((s + 2 * HALO, FF_TILE), F32)] * 2,
        semantics=("parallel",),
    )(u, u, w, w, b, b)


def _ffn_act_bwd(u, df, w, b):
    s = u.shape[0]
    nj = DFF // FF_TILE
    ch = FF_CHUNK

    def body(ug_ref, uv_ref, df_ref, wg_ref, wv_ref, bg_ref, bv_ref,
             dug_ref, duv_ref, dwg_ref, dwv_ref, dbg_ref, dbv_ref, gpad, vpad, dgpad, dvpad, acc):
        _fill_padded(gpad, ug_ref, s, ch, HALO)
        _fill_padded(vpad, uv_ref, s, ch, HALO)
        zeros = jnp.zeros((HALO, FF_TILE), F32)
        for p in (dgpad, dvpad):
            p[0:HALO, :] = zeros
            p[s + HALO:s + 2 * HALO, :] = zeros
        acc[...] = jnp.zeros_like(acc)

        def step(c, carry):
            r0 = pl.multiple_of(c * ch, ch)
            gs = _shift3(gpad, r0, ch)
            vs = _shift3(vpad, r0, ch)
            gc = wg_ref[0:1, :] * gs[0] + wg_ref[1:2, :] * gs[1] + wg_ref[2:3, :] * gs[2] + bg_ref[...]
            vc = wv_ref[0:1, :] * vs[0] + wv_ref[1:2, :] * vs[1] + wv_ref[2:3, :] * vs[2] + bv_ref[...]
            sg = _sigmoid(gc)
            dfv = df_ref[pl.ds(r0, ch), :].astype(F32)
            dgc = dfv * vc * (sg * (1.0 + gc * (1.0 - sg)))
            dvc = dfv * (gc * sg)
            dgpad[pl.ds(r0 + HALO, ch), :] = dgc
            dvpad[pl.ds(r0 + HALO, ch), :] = dvc
            for t in range(3):
                acc[8 * t:8 * t + 8, :] += _colsum8(dgc * gs[t])
                acc[24 + 8 * t:32 + 8 * t, :] += _colsum8(dvc * vs[t])
            acc[48:56, :] += _colsum8(dgc)
            acc[56:64, :] += _colsum8(dvc)
            return carry

        lax.fori_loop(0, s // ch, step, 0)

        def step2(c, carry):
            r0 = pl.multiple_of(c * ch, ch)
            for pad, w_ref, o_ref in ((dgpad, wg_ref, dug_ref), (dvpad, wv_ref, duv_ref)):
                prev, cur, nxt = _shift3(pad, r0, ch)
                o_ref[pl.ds(r0, ch), :] = (w_ref[0:1, :] * nxt + w_ref[1:2, :] * cur + w_ref[2:3, :] * prev).astype(o_ref.dtype)
            return carry

        lax.fori_loop(0, s // ch, step2, 0)
        for t in range(3):
            dwg_ref[t:t + 1, :] = jnp.sum(acc[8 * t:8 * t + 8, :], axis=0, keepdims=True)
            dwv_ref[t:t + 1, :] = jnp.sum(acc[24 + 8 * t:32 + 8 * t, :], axis=0, keepdims=True)
        dbg_ref[...] = jnp.sum(acc[48:56, :], axis=0, keepdims=True)
        dbv_ref[...] = jnp.sum(acc[56:64, :], axis=0, keepdims=True)

    col = lambda off: pl.BlockSpec((s, FF_TILE), lambda j: (0, j + off))
    wsp = lambda off: pl.BlockSpec((3, FF_TILE), lambda j: (0, j + off))
    bsp = lambda off: pl.BlockSpec((1, FF_TILE), lambda j: (0, j + off))
    return _pallas(
        body, name="ffn_act_bwd", grid=(nj,),
        in_specs=[col(0), col(nj), col(0), wsp(0), wsp(nj), bsp(0), bsp(nj)],
        out_specs=[col(0), col(0), wsp(0), wsp(0), bsp(0), bsp(0)],
        out_shape=[_sds((s, DFF), MXU_DTYPE)] * 2 + [_sds((3, DFF), F32)] * 2 + [_sds((1, DFF), F32)] * 2,
        scratch_shapes=[pltpu.VMEM((s + 2 * HALO, FF_TILE), F32)] * 4 + [pltpu.VMEM((64, FF_TILE), F32)],
        semantics=("parallel",),
    )(u, u, df, w, w, b, b)


CONV_CHUNK = 64
CONV_HALO = 16


def _tap(pad_ref, r0, k):
    return pad_ref[pl.ds(r0 + CONV_HALO - CW // 2 + k, CONV_CHUNK), :]


def _glu_into(pad_ref, a_ref, g_ref, s):
    zeros = jnp.zeros((CONV_HALO, LANES), F32)
    pad_ref[0:CONV_HALO, :] = zeros
    pad_ref[s + CONV_HALO:s + 2 * CONV_HALO, :] = zeros

    def cp(c, carry):
        r0 = pl.multiple_of(c * ROW_TILE, ROW_TILE)
        pad_ref[pl.ds(r0 + CONV_HALO, ROW_TILE), :] = a_ref[pl.ds(r0, ROW_TILE), :] * _sigmoid(g_ref[pl.ds(r0, ROW_TILE), :])
        return carry

    lax.fori_loop(0, s // ROW_TILE, cp, 0)


def _conf_conv_fwd(ag, conv_w, conv_b):
    s = ag.shape[0]
    nc = DC // LANES

    def body(a_ref, g_ref, w_ref, b_ref, o_ref, upad):
        _glu_into(upad, a_ref, g_ref, s)

        def step(c, carry):
            r0 = pl.multiple_of(c * CONV_CHUNK, CONV_CHUNK)
            acc = jnp.broadcast_to(b_ref[...], (CONV_CHUNK, LANES))
            for k in range(CW):
                acc = acc + w_ref[k:k + 1, :] * _tap(upad, r0, k)
            o_ref[pl.ds(r0, CONV_CHUNK), :] = acc
            return carry

        lax.fori_loop(0, s // CONV_CHUNK, step, 0)

    col = lambda off: pl.BlockSpec((s, LANES), lambda c: (0, c + off))
    return _pallas(
        body, name="conf_conv_fwd", grid=(nc,),
        in_specs=[col(0), col(nc), pl.BlockSpec((CW, LANES), lambda c: (0, c)), pl.BlockSpec((1, LANES), lambda c: (0, c))],
        out_specs=col(0), out_shape=_sds((s, DC), F32),
        scratch_shapes=[pltpu.VMEM((s + 2 * CONV_HALO, LANES), F32)],
        semantics=("parallel",),
    )(ag, ag, conv_w, conv_b)


def _ln_stats(x):
    mu = jnp.mean(x, axis=-1, keepdims=True)
    xc = x - mu
    var = jnp.mean(xc * xc, axis=-1, keepdims=True)
    rstd = lax.rsqrt(var + EPS)
    return xc * rstd, rstd


def _conf_ln_fwd(u1, ln_g, ln_b, ycat):
    s = u1.shape[0]

    def body(u_ref, g_ref, b_ref, ycat_ref, o_ref):
        del ycat_ref
        xhat, _ = _ln_stats(u_ref[...])
        y = xhat * g_ref[...] + b_ref[...]
        o_ref[...] = (y * _sigmoid(y)).astype(o_ref.dtype)

    return _pallas(
        body, name="conf_ln_fwd", grid=(s // ROW_TILE,),
        in_specs=[pl.BlockSpec((ROW_TILE, DC), lambda i: (i, 0)), _vec_spec(DC), _vec_spec(DC),
                  pl.BlockSpec(memory_space=pl.ANY)],
        out_specs=pl.BlockSpec((ROW_TILE, DC), lambda i: (i, 1)),
        out_shape=_sds(ycat.shape, ycat.dtype),
        input_output_aliases={3: 0},
        semantics=("parallel",),
    )(u1, ln_g, ln_b, ycat)


def _conf_ln_bwd(dycat, u1, ln_g, ln_b):
    s = u1.shape[0]
    nt = s // ROW_TILE

    def body(dy_ref, u_ref, g_ref, b_ref, du_ref, dg_ref, db_ref, a_g, a_b):
        i = pl.program_id(0)

        @pl.when(i == 0)
        def _():
            a_g[...] = jnp.zeros_like(a_g)
            a_b[...] = jnp.zeros_like(a_b)

        xhat, rstd = _ln_stats(u_ref[...])
        gv = g_ref[...]
        y = xhat * gv + b_ref[...]
        sg = _sigmoid(y)
        dyl = dy_ref[...] * (sg * (1.0 + y * (1.0 - sg)))
        a_g[...] += _colsum8(dyl * xhat)
        a_b[...] += _colsum8(dyl)
        dxh = dyl * gv
        du_ref[...] = rstd * (dxh - jnp.mean(dxh, axis=-1, keepdims=True)
                              - xhat * jnp.mean(dxh * xhat, axis=-1, keepdims=True))

        @pl.when(i == nt - 1)
        def _():
            dg_ref[...] = jnp.sum(a_g[...], axis=0, keepdims=True)
            db_ref[...] = jnp.sum(a_b[...], axis=0, keepdims=True)

    return _pallas(
        body, name="conf_ln_bwd", grid=(nt,),
        in_specs=[pl.BlockSpec((ROW_TILE, DC), lambda i: (i, 1)), pl.BlockSpec((ROW_TILE, DC), lambda i: (i, 0)),
                  _vec_spec(DC), _vec_spec(DC)],
        out_specs=[pl.BlockSpec((ROW_TILE, DC), lambda i: (i, 0)), _vec_spec(DC), _vec_spec(DC)],
        out_shape=[_sds((s, DC), F32), _sds((1, DC), F32), _sds((1, DC), F32)],
        scratch_shapes=[pltpu.VMEM((8, DC), F32)] * 2,
        semantics=("arbitrary",),
    )(dycat, u1, ln_g, ln_b)


def _conf_conv_bwd(ag, du1, conv_w, rows_out):
    s = ag.shape[0]
    nc = DC // LANES

    def body(a_ref, g_ref, d_ref, w_ref, da_ref, dg_ref, dw_ref, db_ref, upad, dpad, acc):
        _glu_into(upad, a_ref, g_ref, s)
        _fill_padded(dpad, d_ref, s, ROW_TILE, CONV_HALO)
        acc[...] = jnp.zeros_like(acc)

        def step(c, carry):
            r0 = pl.multiple_of(c * CONV_CHUNK, CONV_CHUNK)
            dcur = dpad[pl.ds(r0 + CONV_HALO, CONV_CHUNK), :]
            du0 = jnp.zeros((CONV_CHUNK, LANES), F32)
            for k in range(CW):
                du0 = du0 + w_ref[k:k + 1, :] * _tap(dpad, r0, CW - 1 - k)
                acc[8 * k:8 * k + 8, :] += _colsum8(dcur * _tap(upad, r0, k))
            acc[8 * CW:8 * CW + 8, :] += _colsum8(dcur)
            av = a_ref[pl.ds(r0, CONV_CHUNK), :]
            sg = _sigmoid(g_ref[pl.ds(r0, CONV_CHUNK), :])
            da_ref[pl.ds(r0, CONV_CHUNK), :] = (du0 * sg).astype(da_ref.dtype)
            dg_ref[pl.ds(r0, CONV_CHUNK), :] = (du0 * av * (sg * (1.0 - sg))).astype(dg_ref.dtype)
            return carry

        lax.fori_loop(0, s // CONV_CHUNK, step, 0)
        if rows_out > s:
            zeros = jnp.zeros((rows_out - s, LANES), da_ref.dtype)
            da_ref[s:rows_out, :] = zeros
            dg_ref[s:rows_out, :] = zeros
        for k in range(CW):
            dw_ref[k:k + 1, :] = jnp.sum(acc[8 * k:8 * k + 8, :], axis=0, keepdims=True)
        db_ref[...] = jnp.sum(acc[8 * CW:8 * CW + 8, :], axis=0, keepdims=True)

    col = lambda off: pl.BlockSpec((s, LANES), lambda c: (0, c + off))
    ocol = pl.BlockSpec((rows_out, LANES), lambda c: (0, c))
    return _pallas(
        body, name="conf_conv_bwd", grid=(nc,),
        in_specs=[col(0), col(nc), col(0), pl.BlockSpec((CW, LANES), lambda c: (0, c))],
        out_specs=[ocol, ocol, pl.BlockSpec((CW, LANES), lambda c: (0, c)), pl.BlockSpec((1, LANES), lambda c: (0, c))],
        out_shape=[_sds((rows_out, DC), MXU_DTYPE)] * 2 + [_sds((CW, DC), F32), _sds((1, DC), F32)],
        scratch_shapes=[pltpu.VMEM((s + 2 * CONV_HALO, LANES), F32)] * 2 + [pltpu.VMEM((8 * (CW + 1), LANES), F32)],
        semantics=("parallel",),
    )(ag, ag, du1, conv_w)


Q_TILE = 2 * GW
K_WIN = PAIR_ROWS * GW


def _bias_table(rpb_rev):
    def body(p_ref, t_ref):
        kcol = lax.broadcasted_iota(jnp.int32, (GW, LANES), 0)
        lane = lax.broadcasted_iota(jnp.int32, (GW, LANES), 1)
        qcol = lane % GW
        cs = jnp.clip(qcol - NA_ROWS, 0, GW - 2 * NA_ROWS)
        colvalid = (kcol >= cs) & (kcol < cs + 2 * NA_ROWS)
        neg = jnp.full((GW, LANES), NEG, F32)

        def skew(h, ro, shift):
            if ro < 0 or ro >= 2 * NA_ROWS - 1:
                return neg
            row = jnp.broadcast_to(p_ref[h * 16 + ro:h * 16 + ro + 1, :], (GW, LANES))
            return pltpu.roll(row, shift, 1, stride=1, stride_axis=0)

        for h in range(NH):
            for b in range(TAB_BLOCKS):
                val = jnp.where(lane < GW, skew(h, b - 1, GW + 1), skew(h, b - 2, 1))
                t_ref[h, b * GW:(b + 1) * GW, :] = jnp.where(colvalid, val, neg)

    return _pallas(body, name="attn_bias_table", out_shape=_sds((NH, TAB_BLOCKS * GW, LANES), F32))(rpb_rev)


def _rpb_grad(tt):
    def body(t_ref, o_ref):
        lane = lax.broadcasted_iota(jnp.int32, (GW, LANES), 1)
        si = lax.broadcasted_iota(jnp.int32, (GW, GW), 0)
        ti = lax.broadcasted_iota(jnp.int32, (GW, GW), 1)
        flip = jnp.where(si + ti == GW - 1, 1.0, 0.0).astype(F32)
        o_ref[...] = jnp.zeros_like(o_ref)
        for h in range(NH):
            for ro in range(2 * NA_ROWS - 1):
                lo = t_ref[h, (ro + 1) * GW:(ro + 2) * GW, :]
                hi = t_ref[h, (ro + 2) * GW:(ro + 3) * GW, :]
                g = jnp.where(lane < GW, lo + pltpu.roll(hi, GW, 1), 0.0)
                gf = jnp.dot(flip, g, preferred_element_type=F32, precision=lax.Precision.HIGHEST)
                sk = pltpu.roll(gf, 0, 1, stride=1, stride_axis=0)
                o_ref[h * 16 + ro:h * 16 + ro + 1, :] = jnp.sum(sk, axis=0, keepdims=True)

    return _pallas(body, name="attn_rpb_grad", out_shape=_sds((NH * 16, LANES), F32))(tt)


def _attn_geometry(i, rows):
    wsp = jnp.clip(2 * i - NA_ROWS // 2, 0, rows - PAIR_ROWS)
    k0 = pl.multiple_of(wsp * GW, GW)
    t0 = pl.multiple_of((wsp - 2 * i + NA_ROWS) * GW, GW)
    jr = lax.broadcasted_iota(jnp.int32, (K_WIN, Q_TILE), 0) // GW
    rr = lax.broadcasted_iota(jnp.int32, (K_WIN, Q_TILE), 1) // GW
    kr = wsp + jr
    wsr = jnp.clip(2 * i + rr - NA_ROWS // 2, 0, rows - NA_ROWS)
    rowmask = jnp.where((kr >= wsr) & (kr < wsr + NA_ROWS), 0.0, NEG).astype(F32)
    return k0, t0, rowmask


def _two_heads_on_lanes(xt):
    feat = lax.broadcasted_iota(jnp.int32, xt.shape, 0)
    zero = jnp.zeros_like(xt)
    return jnp.concatenate([jnp.where(feat < HD, xt, zero), jnp.where(feat >= HD, xt, zero)], axis=1)


def _two_heads_on_rows(x):
    lane = lax.broadcasted_iota(jnp.int32, x.shape, 1)
    zero = jnp.zeros_like(x)
    return jnp.concatenate([jnp.where(lane < HD, x, zero), jnp.where(lane >= HD, x, zero)], axis=0)


def _pick_heads(x2):
    n = x2.shape[0] // 2
    lane = lax.broadcasted_iota(jnp.int32, (n, LANES), 1)
    return jnp.where(lane < HD, x2[:n], x2[n:])


_TN = (((0,), (0,)), ((), ()))


def _attn_fwd(qkv, tab, s):
    rows = s // GW
    npair = rows // 2

    def body(q_ref, kv_ref, tab_ref, o_ref, lse_ref):
        i = pl.program_id(0)
        k0, t0, rowmask = _attn_geometry(i, rows)
        for p in range(NH // 2):
            cq = slice(p * LANES, (p + 1) * LANES)
            ck = slice(DA + p * LANES, DA + (p + 1) * LANES)
            cv = slice(2 * DA + p * LANES, 2 * DA + (p + 1) * LANES)
            qm2 = _two_heads_on_lanes(q_ref[:, cq].T)
            s_loc = jnp.dot(kv_ref[pl.ds(k0, K_WIN), ck], qm2, preferred_element_type=F32) * SCALE
            s_ctx = jnp.dot(kv_ref[pl.ds(s, CTX), ck], qm2, preferred_element_type=F32) * SCALE
            p_loc, p_ctx = [], []
            for hh in range(2):
                h = 2 * p + hh
                ch = slice(hh * Q_TILE, (hh + 1) * Q_TILE)
                sl = s_loc[:, ch] + tab_ref[h, pl.ds(t0, K_WIN), :] + rowmask
                sc = s_ctx[:, ch]
                m = jnp.maximum(jnp.max(sl, axis=0, keepdims=True), jnp.max(sc, axis=0, keepdims=True))
                el = jnp.exp(sl - m)
                ec = jnp.exp(sc - m)
                l = jnp.sum(el, axis=0, keepdims=True) + jnp.sum(ec, axis=0, keepdims=True)
                inv = 1.0 / l
                lse_ref[h:h + 1, :] = m + jnp.log(l)
                p_loc.append((el * inv).astype(MXU_DTYPE))
                p_ctx.append((ec * inv).astype(MXU_DTYPE))
            o2 = (lax.dot_general(jnp.concatenate(p_loc, axis=1), kv_ref[pl.ds(k0, K_WIN), cv], _TN, preferred_element_type=F32)
                  + lax.dot_general(jnp.concatenate(p_ctx, axis=1), kv_ref[pl.ds(s, CTX), cv], _TN, preferred_element_type=F32))
            o_ref[:, cq] = _pick_heads(o2).astype(o_ref.dtype)

    return _pallas(
        body, name="attn_fwd", grid=(npair,),
        in_specs=[pl.BlockSpec((Q_TILE, DA), lambda i: (i, 0)), pl.BlockSpec(memory_space=pltpu.VMEM),
                  pl.BlockSpec(memory_space=pltpu.VMEM)],
        out_specs=[pl.BlockSpec((Q_TILE, DA), lambda i: (i, 0)), pl.BlockSpec((NH, Q_TILE), lambda i: (0, i))],
        out_shape=[_sds((s, D), MXU_DTYPE), _sds((NH, s), F32)],
        semantics=("arbitrary",),
    )(qkv, qkv, tab)


def _attn_bwd(qkv, tab, lse, dycat, s):
    rows = s // GW
    npair = rows // 2
    sa = s + CTX
    nzero = CTX // Q_TILE

    def body(q_ref, do_ref, lse_ref, kv_ref, tab_ref, dq_ref, dkv_ref, tt_ref, dk_acc, dv_acc):
        i = pl.program_id(0)

        @pl.when(i == 0)
        def _():
            dk_acc[...] = jnp.zeros_like(dk_acc)
            dv_acc[...] = jnp.zeros_like(dv_acc)
            tt_ref[...] = jnp.zeros_like(tt_ref)

        @pl.when(i >= npair)
        def _():
            dq_ref[...] = jnp.zeros_like(dq_ref)

        @pl.when(i < npair)
        def _():
            k0, t0, rowmask = _attn_geometry(i, rows)
            for p in range(NH // 2):
                cq = slice(p * LANES, (p + 1) * LANES)
                ck = slice(DA + p * LANES, DA + (p + 1) * LANES)
                cv = slice(2 * DA + p * LANES, 2 * DA + (p + 1) * LANES)
                qp = q_ref[:, cq]
                dop = do_ref[:, cq].astype(MXU_DTYPE)
                qm2 = _two_heads_on_lanes(qp.T)
                dom2 = _two_heads_on_lanes(dop.T)
                kw = kv_ref[pl.ds(k0, K_WIN), ck]
                kc = kv_ref[pl.ds(s, CTX), ck]
                vw = kv_ref[pl.ds(k0, K_WIN), cv]
                vc = kv_ref[pl.ds(s, CTX), cv]
                s_loc = jnp.dot(kw, qm2, preferred_element_type=F32) * SCALE
                s_ctx = jnp.dot(kc, qm2, preferred_element_type=F32) * SCALE
                dp_loc = jnp.dot(vw, dom2, preferred_element_type=F32)
                dp_ctx = jnp.dot(vc, dom2, preferred_element_type=F32)
                p_loc, p_ctx, ds_loc, ds_ctx = [], [], [], []
                for hh in range(2):
                    h = 2 * p + hh
                    ch = slice(hh * Q_TILE, (hh + 1) * Q_TILE)
                    lse_h = lse_ref[h:h + 1, :]
                    pl_ = jnp.exp(s_loc[:, ch] + tab_ref[h, pl.ds(t0, K_WIN), :] + rowmask - lse_h)
                    pc_ = jnp.exp(s_ctx[:, ch] - lse_h)
                    dpl = dp_loc[:, ch]
                    dpc = dp_ctx[:, ch]
                    delta = jnp.sum(pl_ * dpl, axis=0, keepdims=True) + jnp.sum(pc_ * dpc, axis=0, keepdims=True)
                    dsl = pl_ * (dpl - delta)
                    dsc = pc_ * (dpc - delta)
                    tt_ref[h, pl.ds(t0, K_WIN), :] += dsl
                    p_loc.append(pl_.astype(MXU_DTYPE))
                    p_ctx.append(pc_.astype(MXU_DTYPE))
                    ds_loc.append((dsl * SCALE).astype(MXU_DTYPE))
                    ds_ctx.append((dsc * SCALE).astype(MXU_DTYPE))
                p_loc, p_ctx = jnp.concatenate(p_loc, axis=1), jnp.concatenate(p_ctx, axis=1)
                ds_loc, ds_ctx = jnp.concatenate(ds_loc, axis=1), jnp.concatenate(ds_ctx, axis=1)
                do_rows = _two_heads_on_rows(dop)
                q_rows = _two_heads_on_rows(qp)
                dv_acc[pl.ds(k0, K_WIN), cq] += jnp.dot(p_loc, do_rows, preferred_element_type=F32)
                dv_acc[pl.ds(s, CTX), cq] += jnp.dot(p_ctx, do_rows, preferred_element_type=F32)
                dk_acc[pl.ds(k0, K_WIN), cq] += jnp.dot(ds_loc, q_rows, preferred_element_type=F32)
                dk_acc[pl.ds(s, CTX), cq] += jnp.dot(ds_ctx, q_rows, preferred_element_type=F32)
                dq2 = (lax.dot_general(ds_loc, kw, _TN, preferred_element_type=F32)
                       + lax.dot_general(ds_ctx, kc, _TN, preferred_element_type=F32))
                dq_ref[:, cq] = _pick_heads(dq2).astype(dq_ref.dtype)

        @pl.when(i == npair - 1)
        def _():
            def cp(c, carry):
                r0 = pl.multiple_of(c * ROW_TILE, ROW_TILE)
                dkv_ref[pl.ds(r0, ROW_TILE), 0:DA] = dk_acc[pl.ds(r0, ROW_TILE), :].astype(dkv_ref.dtype)
                dkv_ref[pl.ds(r0, ROW_TILE), DA:2 * DA] = dv_acc[pl.ds(r0, ROW_TILE), :].astype(dkv_ref.dtype)
                return carry

            lax.fori_loop(0, sa // ROW_TILE, cp, 0)

    qmap = lambda i: (jnp.minimum(i, npair - 1), 0)
    return _pallas(
        body, name="attn_bwd", grid=(npair + nzero,),
        in_specs=[pl.BlockSpec((Q_TILE, DA), qmap), pl.BlockSpec((Q_TILE, DA), qmap),
                  pl.BlockSpec((NH, Q_TILE), lambda i: (0, jnp.minimum(i, npair - 1))),
                  pl.BlockSpec(memory_space=pltpu.VMEM), pl.BlockSpec(memory_space=pltpu.VMEM)],
        out_specs=[pl.BlockSpec((Q_TILE, DA), lambda i: (i, 0)), pl.BlockSpec(memory_space=pltpu.VMEM),
                   pl.BlockSpec(memory_space=pltpu.VMEM)],
        out_shape=[_sds((sa, DA), MXU_DTYPE), _sds((sa, 2 * DA), MXU_DTYPE), _sds((NH, TAB_BLOCKS * GW, LANES), F32)],
        scratch_shapes=[pltpu.VMEM((sa, DA), F32)] * 2,
        semantics=("arbitrary",),
    )(qkv, dycat, lse, qkv, tab)


def _tile(n, prefs):
    for t in prefs:
        if n % t == 0:
            return t
    raise ValueError((n, prefs))


def _local_step(x, ctx, tgt, mod, mod_c, vec, w_in, w_out, w_up, w_down, rpb_rev):
    s = x.shape[0]
    sa = s + CTX
    ts = _tile(s, (1024, 512, 256))
    ts2 = _tile(s, (2048, 1024, 512, 256))
    tsa = _tile(sa, (1088, 640, 256))
    tsa2 = _tile(sa, (2176, 640, 256))
    sh1, sc1, gt1, sh2, sc2, gt2 = (mod[i:i + 1] for i in range(6))
    csh1, csc1 = mod_c[0:1], mod_c[1:2]
    act = MXU_DTYPE

    tab = _bias_table(rpb_rev)
    h_all = _rmsmod_fwd(x, ctx, vec["g_norm1"], sc1, sh1, csc1, csh1)
    qkv = _mm(h_all, w_in, mode="nn", m=sa, n=3 * DA, k=D, tm=tsa2, tn=512, tk=D, out_dtype=MXU_DTYPE, name="mm_qkv")
    ag = _mm(h_all, w_in, mode="nn", m=s, n=2 * DC, k=D, tm=ts2, tn=512, tk=D, out_dtype=F32, name="mm_ag", b_off=(0, 3))
    ycat, lse = _attn_fwd(qkv, tab, s)
    u1 = _conf_conv_fwd(ag, vec["conv_w"], vec["conv_b"])
    ycat = _conf_ln_fwd(u1, vec["ln_g"], vec["ln_b"], ycat)
    y = _mm(ycat, w_out, mode="nn", m=s, n=D, k=D, tm=ts2, tn=512, tk=D, out_dtype=F32, name="mm_out")
    x1, h2 = _resid_rmsmod_fwd(x, y, gt1, vec["g_norm2"], sc2, sh2)
    u = _mm(h2, w_up, mode="nn", m=s, n=2 * DFF, k=D, tm=ts2, tn=512, tk=D, out_dtype=act, name="mm_up")
    f = _ffn_act_fwd(u, vec["ffn_conv_w"], vec["ffn_conv_b"])
    z = _mm(f, w_down, mode="nn", m=s, n=D, k=DFF, tm=ts, tn=D, tk=DFF, out_dtype=F32, name="mm_down")
    dx2, dz, loss, dgt2, dgf = _final_fwd_bwd(x1, z, gt2, vec["g_final"], tgt)

    df = _mm(dz, w_down, mode="nt", m=s, n=DFF, k=D, tm=ts, tn=DFF, tk=D, out_dtype=act, name="mm_down_dx")
    d_w_down = _mm(f, dz, mode="tn", m=DFF, n=D, k=s, tm=DFF, tn=D, tk=ts, out_dtype=F32, name="mm_down_dw")
    dug, duv, dfw_g, dfw_v, dfb_g, dfb_v = _ffn_act_bwd(u, df, vec["ffn_conv_w"], vec["ffn_conv_b"])
    dh2 = _mm([dug, duv], w_up, mode="nt", m=s, n=D, k=2 * DFF, tm=ts, tn=D, tk=DFF, out_dtype=F32, name="mm_up_dx")
    d_w_up = _mm(h2, [dug, duv], mode="tn", m=D, n=2 * DFF, k=s, tm=D, tn=DFF // 2, tk=ts, out_dtype=F32, name="mm_up_dw")
    dsh2, dsc2, dg2, dx1, dy, dgt1 = _rmsmod_bwd(x1, dh2, vec["g_norm2"], sc2, name="rmsmod2_bwd", add=dx2, resid=(gt1, y))
    dycat = _mm(dy, w_out, mode="nt", m=s, n=D, k=D, tm=ts2, tn=512, tk=D, out_dtype=F32, name="mm_out_dx")
    d_w_out = _mm(ycat, dy, mode="tn", m=D, n=D, k=s, tm=D, tn=D, tk=ts, out_dtype=F32, name="mm_out_dw")
    du1, dln_g, dln_b = _conf_ln_bwd(dycat, u1, vec["ln_g"], vec["ln_b"])
    da, dg, dconv_w, dconv_b = _conf_conv_bwd(ag, du1, vec["conv_w"], sa)
    dq, dkv, tt = _attn_bwd(qkv, tab, lse, dycat, s)
    drpb_rev = _rpb_grad(tt)
    d_pieces = [dq, dkv, da, dg]
    dh = _mm(d_pieces, w_in, mode="nt", m=sa, n=D, k=NIN, tm=tsa2, tn=D, tk=512, out_dtype=F32, name="mm_in_dx")
    d_w_in = _mm(h_all, d_pieces, mode="tn", m=D, n=NIN, k=sa, tm=D, tn=512, tk=tsa, out_dtype=F32, name="mm_in_dw",
                 k_outer=True)
    dsh1, dsc1, dg1, grad_x = _rmsmod_bwd(x, dh, vec["g_norm1"], sc1, name="rmsmod1_bwd", add=dx1)
    dcsh1, dcsc1, dg1c = _rmsmod_bwd(ctx, dh, vec["g_norm1"], csc1, name="rmsmod1_ctx_bwd", dh_row_off=s // ROW_TILE)

    small = dict(
        dmod=[dsh1, dsc1, dgt1, dsh2, dsc2, dgt2], dmod_c=[dcsh1, dcsc1],
        g_norm1=[dg1, dg1c], g_norm2=dg2, g_final=dgf, conv_b=dconv_b, ln_g=dln_g, ln_b=dln_b, conv_w=dconv_w,
        ffn_conv_w=[dfw_g, dfw_v], ffn_conv_b=[dfb_g, dfb_v], rpb_rev=drpb_rev,
    )
    return loss, grad_x, d_w_in, d_w_out, d_w_up, d_w_down, small


N_CHIPS = 4
HBM = pl.BlockSpec(memory_space=pl.ANY)
BIG = {"w_in": ("col", (D, NIN)), "w_out": ("row", (D, D)), "w_up": ("col", (D, 2 * DFF)), "w_down": ("row", (DFF, D))}
BIG_NAMES = tuple(BIG)


def _shard_shape(name):
    kind, (r, c) = BIG[name]
    return (r, c // N_CHIPS) if kind == "col" else (r // N_CHIPS, c)


def _half_rows(name):
    return _shard_shape(name)[0] // 2


def _place():
    x, y, c = lax.axis_index("x"), lax.axis_index("y"), lax.axis_index("c")
    others = [(1 - x, y), (x, 1 - y), (1 - x, 1 - y)]
    return x, y, c, 2 * x + y, (x, y, 1 - c), others


def _whole_region(ref, name, chip, half):
    kind, _ = BIG[name]
    r, c = _shard_shape(name)
    if kind == "col":
        return ref.at[pl.ds(half * (r // 2), r // 2), pl.ds(chip * c, c)]
    return ref.at[pl.ds(chip * r + half * (r // 2), r // 2), :]


def _remote(src, dst, send_sem, recv_sem, to):
    return pltpu.make_async_remote_copy(src_ref=src, dst_ref=dst, send_sem=send_sem, recv_sem=recv_sem,
                                        device_id=to, device_id_type=MESH)


def _gather_small(v, name):
    m_per, n = v.shape

    def body(x_ref, out_ref, send_sems, recv_sems, local_sem):
        x, y, c, _, sibling, others = _place()
        me = (x, y, c)

        def rows(px, py, pc):
            return out_ref.at[pl.ds((4 * px + 2 * py + pc) * m_per, m_per), :]

        def copy(k, block, to, src=None):
            return _remote(rows(*block) if src is None else src, rows(*block), send_sems.at[k], recv_sems.at[k], to)

        mine = pltpu.make_async_copy(x_ref, rows(*me), local_sem)
        mine.start()
        first = [copy(0, me, sibling, src=x_ref)]
        first += [copy(1 + j, me, (*chip, c), src=x_ref) for j, chip in enumerate(others)]
        for cp in first:
            cp.start()
        passed = [copy(4 + j, (*chip, c), sibling) for j, chip in enumerate(others)]
        for j, chip in enumerate(others):
            copy(1 + j, (*chip, c), me).wait_recv()
            passed[j].start()
        copy(0, sibling, me).wait_recv()
        for j, chip in enumerate(others):
            copy(4 + j, (*chip, 1 - c), me).wait_recv()
        for cp in first + passed:
            cp.wait_send()
        mine.wait()

    return pl.pallas_call(
        body, name=name, out_shape=_sds((8 * m_per, n), v.dtype),
        in_specs=[pl.BlockSpec(memory_space=pltpu.VMEM)], out_specs=pl.BlockSpec(memory_space=pltpu.VMEM),
        scratch_shapes=[pltpu.SemaphoreType.DMA((7,)), pltpu.SemaphoreType.DMA((7,)), pltpu.SemaphoreType.DMA],
    )(v)


def _cast_into_whole(name, shard, chip):
    kind, whole = BIG[name]
    r, c = shard.shape
    if kind == "col":
        tr = 256
        o_spec = pl.BlockSpec((tr, c), lambda i, ch: (i, ch[0]))
    else:
        tr = _tile(r, (128, 352))
        o_spec = pl.BlockSpec((tr, c), lambda i, ch: (ch[0] * (r // tr) + i, 0))

    def body(ch_ref, x_ref, o_ref):
        del ch_ref
        o_ref[...] = x_ref[...].astype(o_ref.dtype)

    return _pallas(body, name="cast_" + name, prefetch=1, grid=(r // tr,),
                   in_specs=[pl.BlockSpec((tr, c), lambda i, ch: (i, 0))], out_specs=o_spec,
                   out_shape=_sds(whole, MXU_DTYPE), semantics=("parallel",))(chip, shard)


def _gather_weights(wholes):
    nw = len(BIG_NAMES)

    def body(*refs):
        outs = refs[nw:2 * nw]
        send_sems, recv_sems = refs[2 * nw:]
        _, _, c, chip, sibling, others = _place()
        sends = []
        for w, name in enumerate(BIG_NAMES):
            mine = _whole_region(outs[w], name, chip, c)
            for t, (ox, oy) in enumerate(others):
                cp = _remote(mine, mine, send_sems.at[w, t], recv_sems.at[w, t], (ox, oy, c))
                cp.start()
                sends.append(cp)
        for w, name in enumerate(BIG_NAMES):
            for t, (ox, oy) in enumerate(others):
                got = _whole_region(outs[w], name, 2 * ox + oy, c)
                _remote(got, got, send_sems.at[w, t], recv_sems.at[w, t], (ox, oy, c)).wait_recv()
                cp = _remote(got, got, send_sems.at[w, 3 + t], recv_sems.at[w, 3 + t], sibling)
                cp.start()
                sends.append(cp)
        for w, name in enumerate(BIG_NAMES):
            for t, (ox, oy) in enumerate(others):
                got = _whole_region(outs[w], name, 2 * ox + oy, 1 - c)
                _remote(got, got, send_sems.at[w, 3 + t], recv_sems.at[w, 3 + t], sibling).wait_recv()
        for cp in sends:
            cp.wait_send()

    return pl.pallas_call(
        body, name="gather_weights",
        out_shape=[_sds(a.shape, a.dtype) for a in wholes],
        in_specs=[HBM] * nw, out_specs=[HBM] * nw,
        input_output_aliases={i: i for i in range(nw)},
        scratch_shapes=[pltpu.SemaphoreType.DMA((nw, 6)), pltpu.SemaphoreType.DMA((nw, 6))],
    )(*wholes)


def _compact_shape(name, dtype):
    kind, (r, c) = BIG[name]
    return _sds((r // 2, c), dtype)


def _swap_halves(grads):
    nw = len(BIG_NAMES)

    def body(*refs):
        ins, outs = refs[:nw], refs[nw:2 * nw]
        send_sems, recv_sems = refs[2 * nw:]
        _, _, c, _, sibling, _ = _place()
        copies = []
        for w, name in enumerate(BIG_NAMES):
            kind, (r, _) = BIG[name]
            half = _half_rows(name)
            if kind == "col":
                parts = [(ins[w].at[pl.ds((1 - c) * half, half), :], outs[w])]
            else:
                parts = [(ins[w].at[pl.ds(jj * 2 * half + (1 - c) * half, half), :], outs[w].at[pl.ds(jj * half, half), :])
                         for jj in range(N_CHIPS)]
            for t, (src, dst) in enumerate(parts):
                cp = _remote(src, dst, send_sems.at[w, t], recv_sems.at[w, t], sibling)
                cp.start()
                copies.append(cp)
        for cp in copies:
            cp.wait()

    return pl.pallas_call(
        body, name="grad_swap_halves",
        out_shape=[_compact_shape(n, F32) for n in BIG_NAMES],
        in_specs=[HBM] * nw, out_specs=[HBM] * nw,
        scratch_shapes=[pltpu.SemaphoreType.DMA((nw, N_CHIPS)), pltpu.SemaphoreType.DMA((nw, N_CHIPS))],
    )(*grads)


def _add_halves(name, grad, got, core):
    kind, (r, c) = BIG[name]
    half = _half_rows(name)
    if kind == "col":
        t = 128
        grid = (half // t,)
        g_spec = pl.BlockSpec((t, c), lambda i, cr: (cr[0] * (half // t) + i, 0))
        o_spec = pl.BlockSpec((t, c), lambda i, cr: (i, 0))
    else:
        t = half
        grid = (N_CHIPS,)
        g_spec = pl.BlockSpec((t, c), lambda i, cr: (2 * i + cr[0], 0))
        o_spec = pl.BlockSpec((t, c), lambda i, cr: (i, 0))

    def body(c_ref, g_ref, b_ref, o_ref):
        del c_ref
        o_ref[...] = (g_ref[...] + b_ref[...]).astype(o_ref.dtype)

    return pl.pallas_call(
        body, name="grad_add_" + name,
        grid_spec=pltpu.PrefetchScalarGridSpec(num_scalar_prefetch=1, grid=grid, in_specs=[g_spec, o_spec], out_specs=o_spec),
        out_shape=_compact_shape(name, BF16),
        compiler_params=pltpu.CompilerParams(dimension_semantics=("parallel",), vmem_limit_bytes=VMEM_LIMIT),
    )(core, grad, got)


def _exchange_shards(parts):
    nw = len(BIG_NAMES)

    def piece(ref, name, chip):
        kind, _ = BIG[name]
        r, c = _shard_shape(name)
        if kind == "col":
            return ref.at[:, pl.ds(chip * c, c)]
        return ref.at[pl.ds(chip * (r // 2), r // 2), :]

    def body(*refs):
        ins, outs = refs[:nw], refs[nw:2 * nw]
        send_sems, recv_sems = refs[2 * nw:]
        _, _, c, _, _, others = _place()
        sends = []
        for w, name in enumerate(BIG_NAMES):
            for t, (ox, oy) in enumerate(others):
                cp = _remote(piece(ins[w], name, 2 * ox + oy), outs[w].at[t], send_sems.at[w, t], recv_sems.at[w, t], (ox, oy, c))
                cp.start()
                sends.append(cp)
        for w, name in enumerate(BIG_NAMES):
            for t, (ox, oy) in enumerate(others):
                got = outs[w].at[t]
                _remote(got, got, send_sems.at[w, t], recv_sems.at[w, t], (ox, oy, c)).wait_recv()
        for cp in sends:
            cp.wait_send()

    def out_shape(name):
        r, c = _shard_shape(name)
        return _sds((N_CHIPS - 1, r // 2, c), BF16)

    return pl.pallas_call(
        body, name="grad_exchange_shards",
        out_shape=[out_shape(n) for n in BIG_NAMES],
        in_specs=[HBM] * nw, out_specs=[HBM] * nw,
        scratch_shapes=[pltpu.SemaphoreType.DMA((nw, 3)), pltpu.SemaphoreType.DMA((nw, 3))],
    )(*parts)


def _sum_chips(name, part, got, chip):
    kind, _ = BIG[name]
    _, r, c = got.shape
    t = _tile(r, (128, 352))
    if kind == "col":
        own = pl.BlockSpec((t, c), lambda i, ch: (i, ch[0]))
    else:
        own = pl.BlockSpec((t, c), lambda i, ch: (ch[0] * (r // t) + i, 0))

    def body(ch_ref, p_ref, g_ref, o_ref):
        del ch_ref
        acc = p_ref[...].astype(F32)
        for j in range(N_CHIPS - 1):
            acc = acc + g_ref[j].astype(F32)
        o_ref[...] = acc

    return _pallas(
        body, name="grad_sum_" + name, prefetch=1, grid=(r // t,),
        in_specs=[own, pl.BlockSpec((N_CHIPS - 1, t, c), lambda i, ch: (0, i, 0))],
        out_specs=pl.BlockSpec((t, c), lambda i, ch: (i, 0)),
        out_shape=_sds((r, c), F32), semantics=("parallel",),
    )(chip, part, got)


def _send_halves(sums):
    nw = len(BIG_NAMES)

    def body(*refs):
        ins, outs = refs[:nw], refs[nw:2 * nw]
        send_sems, recv_sems = refs[2 * nw:]
        _, _, _, _, sibling, _ = _place()
        copies = [_remote(ins[w], outs[w], send_sems.at[w], recv_sems.at[w], sibling) for w in range(nw)]
        for cp in copies:
            cp.start()
        for cp in copies:
            cp.wait()

    return pl.pallas_call(
        body, name="grad_send_halves",
        out_shape=[_sds(a.shape, a.dtype) for a in sums],
        in_specs=[HBM] * nw, out_specs=[HBM] * nw,
        scratch_shapes=[pltpu.SemaphoreType.DMA((nw,)), pltpu.SemaphoreType.DMA((nw,))],
    )(*sums)


def _reduce_scatter(grads, core, chip):
    got = _swap_halves(grads)
    parts = [_add_halves(n, grads[i], got[i], core) for i, n in enumerate(BIG_NAMES)]
    gathered = _exchange_shards(parts)
    sums = [_sum_chips(n, parts[i], gathered[i], chip) for i, n in enumerate(BIG_NAMES)]
    return sums, _send_halves(sums)


HI = lax.Precision.HIGHEST
MOD_COLS = 6 * D // N_CHIPS
COND_ROWS = 16


def _silu(v):
    return v * _sigmoid(v)


GATHER_ROWS = 48
FFW_COLS = 2 * DFF // N_CHIPS
CONV_COLS = DC // N_CHIPS


def _pack_cond(c, ffn_w, conv_w):
    def body(c_ref, f_ref, w_ref, o_ref):
        o_ref[...] = jnp.zeros_like(o_ref)
        o_ref[0:1, 0:D] = c_ref[...]
        o_ref[8:11, :] = f_ref[...]
        o_ref[16:16 + CW, 0:CONV_COLS] = w_ref[...]

    return _pallas(body, name="pack_cond", out_shape=_sds((GATHER_ROWS, FFW_COLS), F32))(c, ffn_w, conv_w)


def _unpack_cond(got, c_ctx):
    def body(g_ref, c_ref, cond_ref, f_ref, w_ref):
        cond_ref[...] = jnp.zeros_like(cond_ref)
        for d in range(8):
            cond_ref[d:d + 1, :] = g_ref[d * GATHER_ROWS:d * GATHER_ROWS + 1, 0:D]
        cond_ref[8:9, :] = c_ref[...]
        for j in range(N_CHIPS):
            r0 = 2 * j * GATHER_ROWS
            f_ref[:, j * FFW_COLS:(j + 1) * FFW_COLS] = g_ref[r0 + 8:r0 + 11, :]
            w_ref[:, j * CONV_COLS:(j + 1) * CONV_COLS] = g_ref[r0 + 16:r0 + 16 + CW, 0:CONV_COLS]

    return _pallas(body, name="unpack_cond",
                   out_shape=[_sds((COND_ROWS, D), F32), _sds((3, 2 * DFF), F32), _sds((CW, DC), F32)])(got, c_ctx)


def _chip_cols(rows, width):
    return pl.BlockSpec((rows, width), lambda i, ch: (0, ch[0]))


def _whole(shape):
    return pl.BlockSpec(shape, lambda i, ch: (0,) * len(shape))


def _mod_shard(cond, w_mod, b_mod, chip):
    def body(ch_ref, c_ref, w_ref, b_ref, o_ref):
        del ch_ref
        o_ref[...] = jnp.dot(_silu(c_ref[...]), w_ref[...], preferred_element_type=F32, precision=HI) + b_ref[...]

    return _pallas(body, name="mod_fwd", prefetch=1, grid=(1,),
                   in_specs=[_whole((COND_ROWS, D)), _whole((D, MOD_COLS)), _chip_cols(1, MOD_COLS)],
                   out_specs=_whole((COND_ROWS, MOD_COLS)),
                   out_shape=_sds((COND_ROWS, MOD_COLS), F32))(chip, cond, w_mod, b_mod)


def _unpack_mod(mods, dev):
    def body(dev_ref, m_ref, me_ref, c_ref):
        rowi = lax.broadcasted_iota(jnp.int32, (COND_ROWS, MOD_COLS), 0)
        mine, ctx = [], []
        for j in range(N_CHIPS):
            blk = m_ref[2 * j * COND_ROWS:(2 * j + 1) * COND_ROWS, :]
            mine.append(jnp.sum(jnp.where(rowi == dev_ref[0], blk, 0.0), axis=0, keepdims=True))
            ctx.append(blk[8:9, :])
        mine = jnp.concatenate(mine, axis=1)
        ctx = jnp.concatenate(ctx, axis=1)
        for k in range(6):
            me_ref[k:k + 1, :] = mine[:, k * D:(k + 1) * D]
        for k in range(2):
            c_ref[k:k + 1, :] = ctx[:, k * D:(k + 1) * D]

    return _pallas(body, name="unpack_mod", prefetch=1, grid=(1,),
                   in_specs=[_whole(mods.shape)], out_specs=[_whole((6, D)), _whole((2, D))],
                   out_shape=[_sds((6, D), F32), _sds((2, D), F32)])(dev, mods)


def _mod_weight_grad(cond, dmod_all, chip):
    def body(ch_ref, c_ref, d_ref, o_ref):
        del ch_ref
        o_ref[...] = lax.dot_general(_silu(c_ref[...]), d_ref[...], _TN, preferred_element_type=F32, precision=HI)

    return _pallas(body, name="mod_weight_grad", prefetch=1, grid=(1,),
                   in_specs=[_whole((COND_ROWS, D)), _chip_cols(COND_ROWS, MOD_COLS)], out_specs=_whole((D, MOD_COLS)),
                   out_shape=_sds((D, MOD_COLS), F32))(chip, cond, dmod_all)


def _cond_grad_partial(dmod_all, w_mod, chip):
    def body(ch_ref, d_ref, w_ref, o_ref):
        del ch_ref
        o_ref[...] = lax.dot_general(d_ref[...], w_ref[...], (((1,), (1,)), ((), ())), preferred_element_type=F32, precision=HI)

    return _pallas(body, name="cond_grad_partial", prefetch=1, grid=(1,),
                   in_specs=[pl.BlockSpec((8, MOD_COLS), lambda i, ch: (1, ch[0])), _whole((D, MOD_COLS))],
                   out_specs=_whole((8, D)), out_shape=_sds((8, D), F32))(chip, dmod_all, w_mod)


def _adam_math(w, g, m, v):
    nm = ADAM_B1 * m + (1.0 - ADAM_B1) * g
    nv = ADAM_B2 * v + (1.0 - ADAM_B2) * (g * g)
    c1 = 1.0 - ADAM_B1 ** ADAM_STEP
    c2 = 1.0 - ADAM_B2 ** ADAM_STEP
    return -ADAM_LR * ((nm / c1) / (jnp.sqrt(nv / c2) + ADAM_EPS) + ADAM_WD * w), nm, nv


def _cond_update(parts, c_ctx, m, v):
    def body(p_ref, c_ref, m_ref, v_ref, g_ref, d_ref, nm_ref, nv_ref):
        tot = p_ref[0:1, :]
        for j in range(1, N_CHIPS):
            tot = tot + p_ref[16 * j:16 * j + 1, :]
        cv = c_ref[...]
        sg = _sigmoid(cv)
        g = tot * (sg * (1.0 + cv * (1.0 - sg)))
        g_ref[...] = g
        d_ref[...], nm_ref[...], nv_ref[...] = _adam_math(cv, g, m_ref[...], v_ref[...])

    return _pallas(body, name="cond_update", out_shape=[_sds((1, D), F32)] * 4)(parts, c_ctx, m, v)


def _adamw(w, g, m, v, name):
    r, c = w.shape
    t = _tile(r, (128,)) if r % 128 == 0 and r > 128 else r

    def body(w_ref, g_ref, m_ref, v_ref, d_ref, nm_ref, nv_ref):
        d_ref[...], nm_ref[...], nv_ref[...] = _adam_math(w_ref[...], g_ref[...], m_ref[...], v_ref[...])

    blk = pl.BlockSpec((t, c), lambda i: (i, 0))
    return _pallas(body, name=name, grid=(r // t,), in_specs=[blk] * 4, out_specs=[blk] * 3,
                   out_shape=[_sds((r, c), F32)] * 3, semantics=("parallel",))(w, g, m, v)


def _adamw_cols(w, g_all, m, v, chip, name):
    r, c = w.shape

    def body(ch_ref, w_ref, g_ref, m_ref, v_ref, go_ref, d_ref, nm_ref, nv_ref):
        del ch_ref
        g = g_ref[...]
        go_ref[...] = g
        d_ref[...], nm_ref[...], nv_ref[...] = _adam_math(w_ref[...], g, m_ref[...], v_ref[...])

    return _pallas(body, name=name, prefetch=1, grid=(1,),
                   in_specs=[_whole((r, c)), _chip_cols(r, c), _whole((r, c)), _whole((r, c))],
                   out_specs=[_whole((r, c))] * 4, out_shape=[_sds((r, c), F32)] * 4)(chip, w, g_all, m, v)


def _adamw_halves(name, w, own, other, m, v, core):
    r, c = w.shape
    half = r // 2
    t = _tile(half, (128, 352))
    nh = half // t

    def pick(mine):
        def index(i, cr):
            first = cr[0] if mine else 1 - cr[0]
            return (jnp.clip(i - first * nh, 0, nh - 1), 0)
        return pl.BlockSpec((t, c), index)

    def body(c_ref, w_ref, own_ref, oth_ref, m_ref, v_ref, g_ref, d_ref, nm_ref, nv_ref):
        g = jnp.where(pl.program_id(0) // nh == c_ref[0], own_ref[...], oth_ref[...])
        g_ref[...] = g
        d_ref[...], nm_ref[...], nv_ref[...] = _adam_math(w_ref[...], g, m_ref[...], v_ref[...])

    blk = pl.BlockSpec((t, c), lambda i, cr: (i, 0))
    return _pallas(body, name="adamw_" + name, prefetch=1, grid=(2 * nh,),
                   in_specs=[blk, pick(True), pick(False), blk, blk], out_specs=[blk] * 4,
                   out_shape=[_sds((r, c), F32)] * 4, semantics=("parallel",))(core, w, own, other, m, v)


WEIGHTS = ("c_ctx", "w_mod", "b_mod", "g_norm1", "w_in", "rpb", "conv_w", "conv_b", "ln_g", "ln_b", "w_out", "g_norm2",
           "w_up", "ffn_conv_w", "ffn_conv_b", "w_down", "g_final")
PACK = (("dmod", 6 * D), ("dmod_c", 2 * D), ("g_norm1", D), ("g_norm1_ctx", D), ("g_norm2", D), ("g_final", D),
        ("conv_b", DC), ("ln_g", DC), ("ln_b", DC), ("ffn_conv_b", 2 * DFF), ("ffn_conv_w", 3 * 2 * DFF),
        ("conv_w", CW * DC), ("rpb_rev", NH * 16 * LANES), ("loss", LANES))
PACK_OFF = {}
_o = 0
for _n, _w in PACK:
    PACK_OFF[_n] = (_o, _w)
    _o += _w
PACK_N = -(-_o // (8 * LANES)) * (8 * LANES)
VECTORS = {"b_mod": (6 * D, ("dmod", "dmod_c")), "g_norm1": (D, ("g_norm1", "g_norm1_ctx")), "conv_b": (DC, ("conv_b",)),
           "ln_g": (DC, ("ln_g",)), "ln_b": (DC, ("ln_b",)), "g_norm2": (D, ("g_norm2",)),
           "ffn_conv_b": (2 * DFF, ("ffn_conv_b",)), "g_final": (D, ("g_final",))}
RPB_ROWS = NH * (2 * NA_ROWS - 1)
RPB_COLS = 4 * NA_ROWS - 1


def _pack_small(parts):
    arrs, places = [], []
    for name, _ in PACK:
        off, width = PACK_OFF[name]
        group = parts[name]
        rows = group[0].shape[0]
        row_w = sum(a.shape[1] for a in group)
        assert rows * row_w == width, (name, rows, row_w, width)
        col = 0
        for a in group:
            arrs.append(a)
            places.append([off + k * row_w + col for k in range(rows)])
            col += a.shape[1]

    def body(*refs):
        o_ref = refs[-1]
        o_ref[:, _o:PACK_N] = jnp.zeros((1, PACK_N - _o), F32)
        for ref, offs in zip(refs, places):
            n = ref.shape[1]
            for k, off in enumerate(offs):
                o_ref[:, off:off + n] = ref[k:k + 1, :]

    return _pallas(body, name="pack_small_grads", out_shape=_sds((1, PACK_N), F32))(*arrs)


def _small_update(packs, w, m, v):
    names = list(VECTORS)

    def body(*refs):
        it = iter(refs)
        p_ref = next(it)
        wmv = {n: (next(it), next(it), next(it)) for n in names}
        outs = {n: (next(it), next(it), next(it), next(it)) for n in names}
        dmod_ref, cw_ref, fw_ref, rpb_ref, loss_ref = next(it), next(it), next(it), next(it), next(it)

        def total(name):
            off, width = PACK_OFF[name]
            acc = p_ref[0:1, off:off + width]
            for d in range(1, 8):
                acc = acc + p_ref[d:d + 1, off:off + width]
            return acc

        for n in names:
            width, segs = VECTORS[n]
            g = total(segs[0])
            if len(segs) > 1:
                extra = total(segs[1])
                ew = extra.shape[1]
                g = g + extra if ew == width else jnp.concatenate([g[:, :ew] + extra, g[:, ew:]], axis=1)
            w_ref, m_ref, v_ref = wmv[n]
            g_ref, d_ref, nm_ref, nv_ref = outs[n]
            g_ref[...] = g
            d_ref[...], nm_ref[...], nv_ref[...] = _adam_math(w_ref[...], g, m_ref[...], v_ref[...])

        o_dmod = PACK_OFF["dmod"][0]
        dmod_ref[...] = jnp.zeros_like(dmod_ref)
        dmod_ref[0:8, :] = p_ref[:, o_dmod:o_dmod + 6 * D]
        dmod_ref[8:9, 0:2 * D] = total("dmod_c")
        for ref, name, rows in ((cw_ref, "conv_w", CW), (fw_ref, "ffn_conv_w", 3), (rpb_ref, "rpb_rev", NH * 16)):
            flat = total(name)
            n = ref.shape[1]
            for k in range(rows):
                ref[k:k + 1, :] = flat[:, k * n:(k + 1) * n]
        loss_ref[...] = total("loss")

    ins = [packs] + [a[n] for n in names for a in (w, m, v)]
    out_shape = [_sds((1, VECTORS[n][0]), F32) for n in names for _ in range(4)]
    out_shape += [_sds((COND_ROWS, 6 * D), F32), _sds((CW, DC), F32), _sds((3, 2 * DFF), F32), _sds((NH * 16, LANES), F32),
                  _sds((1, LANES), F32)]
    res = _pallas(body, name="small_update", out_shape=out_shape)(*ins)
    per = {n: tuple(res[4 * i:4 * i + 4]) for i, n in enumerate(names)}
    return (per, *res[4 * len(names):])


def _rpb_update(rev, w, m, v):
    def body(r_ref, w_ref, m_ref, v_ref, g_ref, d_ref, nm_ref, nv_ref):
        li = lax.broadcasted_iota(jnp.int32, (LANES, LANES), 0)
        co = lax.broadcasted_iota(jnp.int32, (LANES, LANES), 1)
        lane_of_co0 = GW - 1 + RPB_COLS // 2
        unflip = jnp.where((li == lane_of_co0 - co) & (co < RPB_COLS), 1.0, 0.0).astype(F32)
        g_all = jnp.dot(r_ref[...], unflip, preferred_element_type=F32, precision=HI)
        nr = 2 * NA_ROWS - 1
        for h in range(NH):
            rows = slice(h * nr, (h + 1) * nr)
            g = g_all[h * 16:h * 16 + nr, 0:RPB_COLS]
            g_ref[rows, :] = g
            d_ref[rows, :], nm_ref[rows, :], nv_ref[rows, :] = _adam_math(w_ref[rows, :], g, m_ref[rows, :], v_ref[rows, :])

    return _pallas(body, name="rpb_update", out_shape=[_sds((RPB_ROWS, RPB_COLS), F32)] * 4)(rev, w, m, v)


def kernel(x, c, ctx, c_ctx, w_mod, b_mod, g_norm1, w_in, rpb, conv_w, conv_b, ln_g, ln_b, w_out, g_norm2, w_up, ffn_conv_w, ffn_conv_b, w_down, g_final, loss_target, m_c_ctx, m_w_mod, m_b_mod, m_g_norm1, m_w_in, m_rpb, m_conv_w, m_conv_b, m_ln_g, m_ln_b, m_w_out, m_g_norm2, m_w_up, m_ffn_conv_w, m_ffn_conv_b, m_w_down, m_g_final, v_c_ctx, v_w_mod, v_b_mod, v_g_norm1, v_w_in, v_rpb, v_conv_w, v_conv_b, v_ln_g, v_ln_b, v_w_out, v_g_norm2, v_w_up, v_ffn_conv_w, v_ffn_conv_b, v_w_down, v_g_final):
    w = dict(c_ctx=c_ctx, w_mod=w_mod, b_mod=b_mod, g_norm1=g_norm1, w_in=w_in, rpb=rpb, conv_w=conv_w, conv_b=conv_b,
             ln_g=ln_g, ln_b=ln_b, w_out=w_out, g_norm2=g_norm2, w_up=w_up, ffn_conv_w=ffn_conv_w, ffn_conv_b=ffn_conv_b,
             w_down=w_down, g_final=g_final)
    mom = dict(c_ctx=m_c_ctx, w_mod=m_w_mod, b_mod=m_b_mod, g_norm1=m_g_norm1, w_in=m_w_in, rpb=m_rpb, conv_w=m_conv_w,
               conv_b=m_conv_b, ln_g=m_ln_g, ln_b=m_ln_b, w_out=m_w_out, g_norm2=m_g_norm2, w_up=m_w_up,
               ffn_conv_w=m_ffn_conv_w, ffn_conv_b=m_ffn_conv_b, w_down=m_w_down, g_final=m_g_final)
    var = dict(c_ctx=v_c_ctx, w_mod=v_w_mod, b_mod=v_b_mod, g_norm1=v_g_norm1, w_in=v_w_in, rpb=v_rpb, conv_w=v_conv_w,
               conv_b=v_conv_b, ln_g=v_ln_g, ln_b=v_ln_b, w_out=v_w_out, g_norm2=v_g_norm2, w_up=v_w_up,
               ffn_conv_w=v_ffn_conv_w, ffn_conv_b=v_ffn_conv_b, w_down=v_w_down, g_final=v_g_final)
    xi, yi, ci = lax.axis_index("x"), lax.axis_index("y"), lax.axis_index("c")
    dev = (4 * xi + 2 * yi + ci).astype(jnp.int32).reshape(1)
    chip = (2 * xi + yi).astype(jnp.int32).reshape(1)
    core = ci.astype(jnp.int32).reshape(1)
    c_ctx2 = c_ctx.reshape(1, D)
    g_final2 = g_final.reshape(1, D)
    mom["g_final"], var["g_final"] = m_g_final.reshape(1, D), v_g_final.reshape(1, D)

    got = _gather_small(_pack_cond(c, ffn_conv_w[0], conv_w[0]), "gather_cond")
    cond, ffn_w_all, conv_w_all = _unpack_cond(got, c_ctx2)

    mods = _gather_small(_mod_shard(cond, w_mod[0], b_mod, chip), "gather_mod")
    mod_me, mod_c = _unpack_mod(mods, dev)

    whole = _gather_weights([_cast_into_whole(n, w[n][0], chip) for n in BIG_NAMES])

    rpb_rev = jnp.pad(rpb[0][:, :, ::-1], ((0, 0), (0, 1), (48, LANES - 48 - RPB_COLS))).reshape(NH * 16, LANES)
    vec = dict(g_norm1=g_norm1, g_norm2=g_norm2, g_final=g_final2, conv_w=conv_w_all, conv_b=conv_b, ln_g=ln_g, ln_b=ln_b,
               ffn_conv_w=ffn_w_all, ffn_conv_b=ffn_conv_b)
    loss_p, grad_x, d_in, d_out, d_up, d_down, small = _local_step(
        x[0], ctx[0], loss_target[0], mod_me, mod_c, vec, *whole, rpb_rev)

    own, other = _reduce_scatter([d_in, d_out, d_up, d_down], core, chip)

    parts = dict(dmod=small["dmod"], dmod_c=small["dmod_c"], g_norm1=[small["g_norm1"][0]], g_norm1_ctx=[small["g_norm1"][1]],
                 g_norm2=[small["g_norm2"]], g_final=[small["g_final"]], conv_b=[small["conv_b"]], ln_g=[small["ln_g"]],
                 ln_b=[small["ln_b"]], ffn_conv_b=small["ffn_conv_b"], ffn_conv_w=small["ffn_conv_w"],
                 conv_w=[small["conv_w"]], rpb_rev=[small["rpb_rev"]], loss=[loss_p])
    pack = _pack_small(parts).reshape(8, PACK_N // 8)
    packs = _gather_small(pack, "gather_small_grads").reshape(8, PACK_N)
    w2 = dict(w, g_final=g_final2)
    per, dmod_all, g_conv_w_all, g_ffn_w_all, g_rpb_rev, loss_row = _small_update(packs, w2, mom, var)

    out = {}
    out.update(per)
    out["c_ctx"] = _cond_update(
        _gather_small(_cond_grad_partial(dmod_all, w_mod[0], chip), "gather_cond_grad"),
        c_ctx2, m_c_ctx.reshape(1, D), v_c_ctx.reshape(1, D))
    g_w_mod = _mod_weight_grad(cond, dmod_all, chip)
    out["w_mod"] = (g_w_mod, *_adamw(w_mod[0], g_w_mod, m_w_mod[0], v_w_mod[0], "adamw_w_mod"))
    for i, n in enumerate(BIG_NAMES):
        out[n] = _adamw_halves(n, w[n][0], own[i], other[i], mom[n][0], var[n][0], core)
    out["conv_w"] = _adamw_cols(conv_w[0], g_conv_w_all, m_conv_w[0], v_conv_w[0], chip, "adamw_conv_w")
    out["ffn_conv_w"] = _adamw_cols(ffn_conv_w[0], g_ffn_w_all, m_ffn_conv_w[0], v_ffn_conv_w[0], chip, "adamw_ffn_conv_w")
    flat = lambda a: a.reshape(RPB_ROWS, RPB_COLS)
    out["rpb"] = _rpb_update(g_rpb_rev, flat(rpb), flat(m_rpb), flat(v_rpb))

    res = [[out[n][k].reshape(w[n].shape) for n in WEIGHTS] for k in range(4)]
    return (loss_row[0, 0], grad_x[None], *res[0], *res[1], *res[2], *res[3])
```

```python
import functools

import jax
import jax.numpy as jnp
from jax import lax
from jax.experimental import pallas as pl
from jax.experimental.pallas import tpu as pltpu

F32 = jnp.float32
BF16 = jnp.bfloat16
MXU_DTYPE = jnp.bfloat16

D = 1024
CTX = 256
GW = 64
DA = 512
NH = 8
HD = 64
DC = 512
CW = 31
DFF = 2816
NIN = 3 * DA + 2 * DC
EPS = 1e-6
SCALE = HD ** -0.5
NEG = -1e30
NA_ROWS = 8
PAIR_ROWS = NA_ROWS + 1
TAB_BLOCKS = 17
LANES = 128
VMEM_LIMIT = 56 * 1024 * 1024

ADAM_LR = 0.001
ADAM_B1 = 0.9
ADAM_B2 = 0.999
ADAM_EPS = 1e-08
ADAM_WD = 0.01
ADAM_STEP = 10

MESH = pl.DeviceIdType.MESH


def _pallas(body, *, name, semantics=None, vmem=VMEM_LIMIT, prefetch=0, **kw):
    params = dict(vmem_limit_bytes=vmem)
    if semantics is not None:
        params["dimension_semantics"] = semantics
    if prefetch:
        kw["grid_spec"] = pltpu.PrefetchScalarGridSpec(
            num_scalar_prefetch=prefetch, grid=kw.pop("grid"), in_specs=kw.pop("in_specs"), out_specs=kw.pop("out_specs"),
            scratch_shapes=kw.pop("scratch_shapes", ()))
    return pl.pallas_call(body, name=name, compiler_params=pltpu.CompilerParams(**params), **kw)


def _sds(shape, dtype):
    return jax.ShapeDtypeStruct(shape, dtype)


def _vec_spec(n):
    return pl.BlockSpec((1, n), lambda *_: (0, 0))


def _colsum8(x):
    t, n = x.shape
    return jnp.sum(x.reshape(t // 8, 8, n), axis=0)


def _sigmoid(x):
    return 0.5 * jnp.tanh(0.5 * x) + 0.5


def _pieces(arrs, tile):
    lo, out = 0, []
    for a in arrs:
        nt = a.shape[1] // tile
        assert nt * tile == a.shape[1], (a.shape, tile)
        out.append((lo, nt))
        lo += nt
    return out


def _mm(a, b, *, mode, m, n, k, tm, tn, tk, out_dtype, name, a_off=(0, 0), b_off=(0, 0), k_outer=False):
    a_list = list(a) if isinstance(a, (list, tuple)) else [a]
    b_list = list(b) if isinstance(b, (list, tuple)) else [b]
    assert m % tm == 0 and n % tn == 0 and k % tk == 0, (name, m, n, k, tm, tn, tk)
    gi, gj, nk = m // tm, n // tn, k // tk
    a_tile = tm if mode == "tn" else tk
    a_pc = _pieces(a_list, a_tile) if len(a_list) > 1 else [(0, 1 << 30)]
    if mode == "nt":
        assert len(b_list) == 1
    b_pc = _pieces(b_list, tn) if len(b_list) > 1 else [(0, 1 << 30)]
    dims = {"nn": (((1,), (0,)), ((), ())), "nt": (((1,), (1,)), ((), ())), "tn": (((0,), (0,)), ((), ()))}[mode]
    k_outer = k_outer and nk > 1

    def ijk(fn):
        return (lambda i, kk, j: fn(i, j, kk)) if k_outer else fn

    def a_spec(lo, cnt):
        def loc(idx):
            return idx + a_off[1] if len(a_list) == 1 else jnp.clip(idx - lo, 0, cnt - 1)
        if mode == "tn":
            return pl.BlockSpec((tk, tm), ijk(lambda i, j, kk: (kk + a_off[0], loc(i))))
        return pl.BlockSpec((tm, tk), ijk(lambda i, j, kk: (i + a_off[0], loc(kk))))

    def b_spec(lo, cnt):
        def loc(idx):
            return idx + b_off[1] if len(b_list) == 1 else jnp.clip(idx - lo, 0, cnt - 1)
        if mode == "nt":
            return pl.BlockSpec((tn, tk), ijk(lambda i, j, kk: (j + b_off[0], kk + b_off[1])))
        return pl.BlockSpec((tk, tn), ijk(lambda i, j, kk: (kk + b_off[0], loc(j))))

    na, nb = len(a_list), len(b_list)
    in_place = nk > 1 and out_dtype == F32 and not k_outer

    def body(*refs):
        a_refs, b_refs, o_ref = refs[:na], refs[na:na + nb], refs[na + nb]
        if k_outer:
            i, kk, j = pl.program_id(0), pl.program_id(1), pl.program_id(2)
            acc = refs[na + nb + 1].at[j]
        else:
            i, j, kk = pl.program_id(0), pl.program_id(1), pl.program_id(2)
            acc = o_ref if in_place else (refs[na + nb + 1] if nk > 1 else None)
        a_idx = i if mode == "tn" else kk

        def step(ar, br):
            p = lax.dot_general(ar[...].astype(MXU_DTYPE), br[...].astype(MXU_DTYPE), dims,
                                preferred_element_type=F32)
            if nk == 1:
                o_ref[...] = p.astype(out_dtype)
                return

            @pl.when(kk == 0)
            def _():
                acc[...] = p

            @pl.when(kk > 0)
            def _():
                acc[...] += p

            if not in_place:
                @pl.when(kk == nk - 1)
                def _():
                    o_ref[...] = acc[...].astype(out_dtype)

        for pa, (alo, acnt) in enumerate(a_pc):
            for pb, (blo, bcnt) in enumerate(b_pc):
                if na == 1 and nb == 1:
                    step(a_refs[0], b_refs[0])
                else:
                    cond = (a_idx >= alo) & (a_idx < alo + acnt) & (j >= blo) & (j < blo + bcnt)
                    pl.when(cond)(functools.partial(step, a_refs[pa], b_refs[pb]))

    if k_outer:
        grid = (gi, nk, gj)
        o_spec = pl.BlockSpec((tm, tn), lambda i, kk, j: (i, jnp.where(kk == nk - 1, j, 0)))
        scratch = [pltpu.VMEM((gj, tm, tn), F32)]
        semantics = ("parallel", "arbitrary", "arbitrary")
    else:
        grid = (gi, gj, nk)
        o_spec = pl.BlockSpec((tm, tn), lambda i, j, kk: (i, j))
        scratch = [pltpu.VMEM((tm, tn), F32)] if nk > 1 and not in_place else []
        semantics = ("parallel", "parallel", "arbitrary")
    return _pallas(
        body, name=name, grid=grid,
        in_specs=[a_spec(*p) for p in a_pc] + [b_spec(*p) for p in b_pc],
        out_specs=o_spec, out_shape=_sds((m, n), out_dtype), scratch_shapes=scratch, semantics=semantics,
    )(*a_list, *b_list)


ROW_TILE = 256


def _rmsmod_fwd(x, ctx, g, sc, sh, csc, csh):
    s = x.shape[0]
    nt = s // ROW_TILE
    assert ctx.shape[0] == ROW_TILE

    def body(x_ref, c_ref, g_ref, sc_ref, sh_ref, csc_ref, csh_ref, o_ref):
        is_ctx = pl.program_id(0) == nt
        xv = jnp.where(is_ctx, c_ref[...], x_ref[...])
        scv = jnp.where(is_ctx, csc_ref[...], sc_ref[...])
        shv = jnp.where(is_ctx, csh_ref[...], sh_ref[...])
        r = lax.rsqrt(jnp.mean(xv * xv, axis=-1, keepdims=True) + EPS)
        y = xv * r * g_ref[...]
        o_ref[...] = (y * (1.0 + scv) + shv).astype(o_ref.dtype)

    return _pallas(
        body, name="rmsmod1_fwd", grid=(nt + 1,),
        in_specs=[pl.BlockSpec((ROW_TILE, D), lambda i: (jnp.minimum(i, nt - 1), 0)),
                  pl.BlockSpec((ROW_TILE, D), lambda i: (0, 0))] + [_vec_spec(D)] * 5,
        out_specs=pl.BlockSpec((ROW_TILE, D), lambda i: (i, 0)),
        out_shape=_sds((s + CTX, D), MXU_DTYPE),
        semantics=("arbitrary",),
    )(x, ctx, g, sc, sh, csc, csh)


def _resid_rmsmod_fwd(x, y, gt, g, sc, sh):
    s = x.shape[0]

    def body(x_ref, y_ref, gt_ref, g_ref, sc_ref, sh_ref, x1_ref, h_ref):
        x1 = x_ref[...] + gt_ref[...] * y_ref[...]
        x1_ref[...] = x1
        r = lax.rsqrt(jnp.mean(x1 * x1, axis=-1, keepdims=True) + EPS)
        h_ref[...] = ((x1 * r * g_ref[...]) * (1.0 + sc_ref[...]) + sh_ref[...]).astype(h_ref.dtype)

    row = pl.BlockSpec((ROW_TILE, D), lambda i: (i, 0))
    return _pallas(
        body, name="resid_rmsmod2_fwd", grid=(s // ROW_TILE,),
        in_specs=[row, row] + [_vec_spec(D)] * 4,
        out_specs=[row, row],
        out_shape=[_sds((s, D), F32), _sds((s, D), MXU_DTYPE)],
        semantics=("parallel",),
    )(x, y, gt, g, sc, sh)


def _final_fwd_bwd(x1, z, gt2, gf, tgt):
    s = x1.shape[0]
    nt = s // ROW_TILE

    def body(x1_ref, z_ref, gt_ref, gf_ref, t_ref, dx2_ref, dz_ref, loss_ref, dgt_ref, dgf_ref, a_loss, a_gt, a_gf):
        i = pl.program_id(0)

        @pl.when(i == 0)
        def _():
            a_loss[...] = jnp.zeros_like(a_loss)
            a_gt[...] = jnp.zeros_like(a_gt)
            a_gf[...] = jnp.zeros_like(a_gf)

        zv = z_ref[...]
        gt = gt_ref[...]
        gf_ = gf_ref[...]
        x2 = x1_ref[...] + gt * zv
        r = lax.rsqrt(jnp.mean(x2 * x2, axis=-1, keepdims=True) + EPS)
        xn = x2 * r
        e = xn * gf_ - t_ref[...]
        a_loss[...] += _colsum8(e * e)
        dyo = e * (1.0 / D)
        a_gf[...] += _colsum8(dyo * xn)
        gdy = gf_ * dyo
        dx2 = r * gdy - xn * (r * r) * jnp.mean(x2 * gdy, axis=-1, keepdims=True)
        dx2_ref[...] = dx2
        dz_ref[...] = (gt * dx2).astype(dz_ref.dtype)
        a_gt[...] += _colsum8(dx2 * zv)

        @pl.when(i == nt - 1)
        def _():
            tot = jnp.sum(jnp.sum(a_loss[...], axis=0, keepdims=True), axis=1, keepdims=True) * (0.5 / D)
            loss_ref[...] = jnp.broadcast_to(tot, loss_ref.shape)
            dgt_ref[...] = jnp.sum(a_gt[...], axis=0, keepdims=True)
            dgf_ref[...] = jnp.sum(a_gf[...], axis=0, keepdims=True)

    row = pl.BlockSpec((ROW_TILE, D), lambda i: (i, 0))
    return _pallas(
        body, name="final_norm_loss", grid=(nt,),
        in_specs=[row, row, _vec_spec(D), _vec_spec(D), row],
        out_specs=[row, row, _vec_spec(LANES), _vec_spec(D), _vec_spec(D)],
        out_shape=[_sds((s, D), F32), _sds((s, D), MXU_DTYPE), _sds((1, LANES), F32), _sds((1, D), F32), _sds((1, D), F32)],
        scratch_shapes=[pltpu.VMEM((8, D), F32)] * 3,
        semantics=("arbitrary",),
    )(x1, z, gt2, gf, tgt)


def _rmsmod_bwd(xin, dh, g, sc, *, name, dh_row_off=0, add=None, resid=None):
    s = xin.shape[0]
    nt = s // ROW_TILE
    want_dx = add is not None
    assert resid is None or want_dx

    def body(*refs):
        it = iter(refs)
        x_ref, dh_ref, g_ref, sc_ref = next(it), next(it), next(it), next(it)
        add_ref = next(it) if want_dx else None
        gt_ref, y_ref = (next(it), next(it)) if resid is not None else (None, None)
        dsh_ref, dsc_ref, dg_ref = next(it), next(it), next(it)
        dx_ref = next(it) if want_dx else None
        dy_ref, dgt_ref = (next(it), next(it)) if resid is not None else (None, None)
        a_sh, a_sc, a_g = next(it), next(it), next(it)
        a_gt = next(it) if resid is not None else None
        i = pl.program_id(0)

        @pl.when(i == 0)
        def _():
            a_sh[...] = jnp.zeros_like(a_sh)
            a_sc[...] = jnp.zeros_like(a_sc)
            a_g[...] = jnp.zeros_like(a_g)
            if a_gt is not None:
                a_gt[...] = jnp.zeros_like(a_gt)

        xv = x_ref[...]
        dhv = dh_ref[...]
        gv = g_ref[...]
        r = lax.rsqrt(jnp.mean(xv * xv, axis=-1, keepdims=True) + EPS)
        xn = xv * r
        a_sh[...] += _colsum8(dhv)
        a_sc[...] += _colsum8(dhv * (xn * gv))
        dn = dhv * (1.0 + sc_ref[...])
        a_g[...] += _colsum8(dn * xn)
        if want_dx:
            gdn = gv * dn
            dx = add_ref[...] + r * gdn - xn * (r * r) * jnp.mean(xv * gdn, axis=-1, keepdims=True)
            dx_ref[...] = dx
            if resid is not None:
                dy_ref[...] = (gt_ref[...] * dx).astype(dy_ref.dtype)
                a_gt[...] += _colsum8(dx * y_ref[...])

        @pl.when(i == nt - 1)
        def _():
            dsh_ref[...] = jnp.sum(a_sh[...], axis=0, keepdims=True)
            dsc_ref[...] = jnp.sum(a_sc[...], axis=0, keepdims=True)
            dg_ref[...] = jnp.sum(a_g[...], axis=0, keepdims=True)
            if a_gt is not None:
                dgt_ref[...] = jnp.sum(a_gt[...], axis=0, keepdims=True)

    row = pl.BlockSpec((ROW_TILE, D), lambda i: (i, 0))
    ins = [xin, dh, g, sc]
    in_specs = [row, pl.BlockSpec((ROW_TILE, D), lambda i: (i + dh_row_off, 0)), _vec_spec(D), _vec_spec(D)]
    out_specs = [_vec_spec(D)] * 3
    out_shape = [_sds((1, D), F32)] * 3
    scratch = [pltpu.VMEM((8, D), F32)] * 3
    if want_dx:
        ins.append(add)
        in_specs.append(row)
        out_specs.append(row)
        out_shape.append(_sds((s, D), F32))
    if resid is not None:
        ins += [resid[0], resid[1]]
        in_specs += [_vec_spec(D), row]
        out_specs += [row, _vec_spec(D)]
        out_shape += [_sds((s, D), MXU_DTYPE), _sds((1, D), F32)]
        scratch.append(pltpu.VMEM((8, D), F32))
    return _pallas(body, name=name, grid=(nt,), in_specs=in_specs, out_specs=out_specs, out_shape=out_shape,
                   scratch_shapes=scratch, semantics=("arbitrary",))(*ins)


FF_TILE = 128
FF_CHUNK = 128
HALO = 8


def _shift3(pad_ref, r0, ch):
    return tuple(pad_ref[pl.ds(r0 + HALO + d, ch), :] for d in (-1, 0, 1))


def _fill_padded(pad_ref, src_ref, s, ch, halo):
    zeros = jnp.zeros((halo, pad_ref.shape[1]), F32)
    pad_ref[0:halo, :] = zeros
    pad_ref[s + halo:s + 2 * halo, :] = zeros

    def cp(c, carry):
        r0 = pl.multiple_of(c * ch, ch)
        pad_ref[pl.ds(r0 + halo, ch), :] = src_ref[pl.ds(r0, ch), :].astype(F32)
        return carry

    lax.fori_loop(0, s // ch, cp, 0)


def _ffn_act_fwd(u, w, b):
    s = u.shape[0]
    nj = DFF // FF_TILE
    ch = FF_CHUNK

    def body(ug_ref, uv_ref, wg_ref, wv_ref, bg_ref, bv_ref, f_ref, gpad, vpad):
        _fill_padded(gpad, ug_ref, s, ch, HALO)
        _fill_padded(vpad, uv_ref, s, ch, HALO)

        def conv(pad, w_ref, b_ref, r0):
            prev, cur, nxt = _shift3(pad, r0, ch)
            return w_ref[0:1, :] * prev + w_ref[1:2, :] * cur + w_ref[2:3, :] * nxt + b_ref[...]

        def step(c, carry):
            r0 = pl.multiple_of(c * ch, ch)
            gc = conv(gpad, wg_ref, bg_ref, r0)
            vc = conv(vpad, wv_ref, bv_ref, r0)
            f_ref[pl.ds(r0, ch), :] = (gc * _sigmoid(gc) * vc).astype(f_ref.dtype)
            return carry

        lax.fori_loop(0, s // ch, step, 0)

    col = lambda off: pl.BlockSpec((s, FF_TILE), lambda j: (0, j + off))
    wsp = lambda off: pl.BlockSpec((3, FF_TILE), lambda j: (0, j + off))
    bsp = lambda off: pl.BlockSpec((1, FF_TILE), lambda j: (0, j + off))
    return _pallas(
        body, name="ffn_act_fwd", grid=(nj,),
        in_specs=[col(0), col(nj), wsp(0), wsp(nj), bsp(0), bsp(nj)],
        out_specs=col(0), out_shape=_sds((s, DFF), MXU_DTYPE),
        scratch_shapes=[pltpu.VMEM((s + 2 * HALO, FF_TILE), F32)] * 2,
        semantics=("parallel",),
    )(u, u, w, w, b, b)


def _ffn_act_bwd(u, df, w, b):
    s = u.shape[0]
    nj = DFF // FF_TILE
    ch = FF_CHUNK

    def body(ug_ref, uv_ref, df_ref, wg_ref, wv_ref, bg_ref, bv_ref,
             dug_ref, duv_ref, dwg_ref, dwv_ref, dbg_ref, dbv_ref, gpad, vpad, dgpad, dvpad, acc):
        _fill_padded(gpad, ug_ref, s, ch, HALO)
        _fill_padded(vpad, uv_ref, s, ch, HALO)
        zeros = jnp.zeros((HALO, FF_TILE), F32)
        for p in (dgpad, dvpad):
            p[0:HALO, :] = zeros
            p[s + HALO:s + 2 * HALO, :] = zeros
        acc[...] = jnp.zeros_like(acc)

        def step(c, carry):
            r0 = pl.multiple_of(c * ch, ch)
            gs = _shift3(gpad, r0, ch)
            vs = _shift3(vpad, r0, ch)
            gc = wg_ref[0:1, :] * gs[0] + wg_ref[1:2, :] * gs[1] + wg_ref[2:3, :] * gs[2] + bg_ref[...]
            vc = wv_ref[0:1, :] * vs[0] + wv_ref[1:2, :] * vs[1] + wv_ref[2:3, :] * vs[2] + bv_ref[...]
            sg = _sigmoid(gc)
            dfv = df_ref[pl.ds(r0, ch), :].astype(F32)
            dgc = dfv * vc * (sg * (1.0 + gc * (1.0 - sg)))
            dvc = dfv * (gc * sg)
            dgpad[pl.ds(r0 + HALO, ch), :] = dgc
            dvpad[pl.ds(r0 + HALO, ch), :] = dvc
            for t in range(3):
                acc[8 * t:8 * t + 8, :] += _colsum8(dgc * gs[t])
                acc[24 + 8 * t:32 + 8 * t, :] += _colsum8(dvc * vs[t])
            acc[48:56, :] += _colsum8(dgc)
            acc[56:64, :] += _colsum8(dvc)
            return carry

        lax.fori_loop(0, s // ch, step, 0)

        def step2(c, carry):
            r0 = pl.multiple_of(c * ch, ch)
            for pad, w_ref, o_ref in ((dgpad, wg_ref, dug_ref), (dvpad, wv_ref, duv_ref)):
                prev, cur, nxt = _shift3(pad, r0, ch)
                o_ref[pl.ds(r0, ch), :] = (w_ref[0:1, :] * nxt + w_ref[1:2, :] * cur + w_ref[2:3, :] * prev).astype(o_ref.dtype)
            return carry

        lax.fori_loop(0, s // ch, step2, 0)
        for t in range(3):
            dwg_ref[t:t + 1, :] = jnp.sum(acc[8 * t:8 * t + 8, :], axis=0, keepdims=True)
            dwv_ref[t:t + 1, :] = jnp.sum(acc[24 + 8 * t:32 + 8 * t, :], axis=0, keepdims=True)
        dbg_ref[...] = jnp.sum(acc[48:56, :], axis=0, keepdims=True)
        dbv_ref[...] = jnp.sum(acc[56:64, :], axis=0, keepdims=True)

    col = lambda off: pl.BlockSpec((s, FF_TILE), lambda j: (0, j + off))
    wsp = lambda off: pl.BlockSpec((3, FF_TILE), lambda j: (0, j + off))
    bsp = lambda off: pl.BlockSpec((1, FF_TILE), lambda j: (0, j + off))
    return _pallas(
        body, name="ffn_act_bwd", grid=(nj,),
        in_specs=[col(0), col(nj), col(0), wsp(0), wsp(nj), bsp(0), bsp(nj)],
        out_specs=[col(0), col(0), wsp(0), wsp(0), bsp(0), bsp(0)],
        out_shape=[_sds((s, DFF), MXU_DTYPE)] * 2 + [_sds((3, DFF), F32)] * 2 + [_sds((1, DFF), F32)] * 2,
        scratch_shapes=[pltpu.VMEM((s + 2 * HALO, FF_TILE), F32)] * 4 + [pltpu.VMEM((64, FF_TILE), F32)],
        semantics=("parallel",),
    )(u, u, df, w, w, b, b)


CONV_CHUNK = 64
CONV_HALO = 16


def _tap(pad_ref, r0, k):
    return pad_ref[pl.ds(r0 + CONV_HALO - CW // 2 + k, CONV_CHUNK), :]


def _glu_into(pad_ref, a_ref, g_ref, s):
    zeros = jnp.zeros((CONV_HALO, LANES), F32)
    pad_ref[0:CONV_HALO, :] = zeros
    pad_ref[s + CONV_HALO:s + 2 * CONV_HALO, :] = zeros

    def cp(c, carry):
        r0 = pl.multiple_of(c * ROW_TILE, ROW_TILE)
        pad_ref[pl.ds(r0 + CONV_HALO, ROW_TILE), :] = a_ref[pl.ds(r0, ROW_TILE), :] * _sigmoid(g_ref[pl.ds(r0, ROW_TILE), :])
        return carry

    lax.fori_loop(0, s // ROW_TILE, cp, 0)


def _conf_conv_fwd(ag, conv_w, conv_b):
    s = ag.shape[0]
    nc = DC // LANES

    def body(a_ref, g_ref, w_ref, b_ref, o_ref, upad):
        _glu_into(upad, a_ref, g_ref, s)

        def step(c, carry):
            r0 = pl.multiple_of(c * CONV_CHUNK, CONV_CHUNK)
            acc = jnp.broadcast_to(b_ref[...], (CONV_CHUNK, LANES))
            for k in range(CW):
                acc = acc + w_ref[k:k + 1, :] * _tap(upad, r0, k)
            o_ref[pl.ds(r0, CONV_CHUNK), :] = acc
            return carry

        lax.fori_loop(0, s // CONV_CHUNK, step, 0)

    col = lambda off: pl.BlockSpec((s, LANES), lambda c: (0, c + off))
    return _pallas(
        body, name="conf_conv_fwd", grid=(nc,),
        in_specs=[col(0), col(nc), pl.BlockSpec((CW, LANES), lambda c: (0, c)), pl.BlockSpec((1, LANES), lambda c: (0, c))],
        out_specs=col(0), out_shape=_sds((s, DC), F32),
        scratch_shapes=[pltpu.VMEM((s + 2 * CONV_HALO, LANES), F32)],
        semantics=("parallel",),
    )(ag, ag, conv_w, conv_b)


def _ln_stats(x):
    mu = jnp.mean(x, axis=-1, keepdims=True)
    xc = x - mu
    var = jnp.mean(xc * xc, axis=-1, keepdims=True)
    rstd = lax.rsqrt(var + EPS)
    return xc * rstd, rstd


def _conf_ln_fwd(u1, ln_g, ln_b, ycat):
    s = u1.shape[0]

    def body(u_ref, g_ref, b_ref, ycat_ref, o_ref):
        del ycat_ref
        xhat, _ = _ln_stats(u_ref[...])
        y = xhat * g_ref[...] + b_ref[...]
        o_ref[...] = (y * _sigmoid(y)).astype(o_ref.dtype)

    return _pallas(
        body, name="conf_ln_fwd", grid=(s // ROW_TILE,),
        in_specs=[pl.BlockSpec((ROW_TILE, DC), lambda i: (i, 0)), _vec_spec(DC), _vec_spec(DC),
                  pl.BlockSpec(memory_space=pl.ANY)],
        out_specs=pl.BlockSpec((ROW_TILE, DC), lambda i: (i, 1)),
        out_shape=_sds(ycat.shape, ycat.dtype),
        input_output_aliases={3: 0},
        semantics=("parallel",),
    )(u1, ln_g, ln_b, ycat)


def _conf_ln_bwd(dycat, u1, ln_g, ln_b):
    s = u1.shape[0]
    nt = s // ROW_TILE

    def body(dy_ref, u_ref, g_ref, b_ref, du_ref, dg_ref, db_ref, a_g, a_b):
        i = pl.program_id(0)

        @pl.when(i == 0)
        def _():
            a_g[...] = jnp.zeros_like(a_g)
            a_b[...] = jnp.zeros_like(a_b)

        xhat, rstd = _ln_stats(u_ref[...])
        gv = g_ref[...]
        y = xhat * gv + b_ref[...]
        sg = _sigmoid(y)
        dyl = dy_ref[...] * (sg * (1.0 + y * (1.0 - sg)))
        a_g[...] += _colsum8(dyl * xhat)
        a_b[...] += _colsum8(dyl)
        dxh = dyl * gv
        du_ref[...] = rstd * (dxh - jnp.mean(dxh, axis=-1, keepdims=True)
                              - xhat * jnp.mean(dxh * xhat, axis=-1, keepdims=True))

        @pl.when(i == nt - 1)
        def _():
            dg_ref[...] = jnp.sum(a_g[...], axis=0, keepdims=True)
            db_ref[...] = jnp.sum(a_b[...], axis=0, keepdims=True)

    return _pallas(
        body, name="conf_ln_bwd", grid=(nt,),
        in_specs=[pl.BlockSpec((ROW_TILE, DC), lambda i: (i, 1)), pl.BlockSpec((ROW_TILE, DC), lambda i: (i, 0)),
                  _vec_spec(DC), _vec_spec(DC)],
        out_specs=[pl.BlockSpec((ROW_TILE, DC), lambda i: (i, 0)), _vec_spec(DC), _vec_spec(DC)],
        out_shape=[_sds((s, DC), F32), _sds((1, DC), F32), _sds((1, DC), F32)],
        scratch_shapes=[pltpu.VMEM((8, DC), F32)] * 2,
        semantics=("arbitrary",),
    )(dycat, u1, ln_g, ln_b)


def _conf_conv_bwd(ag, du1, conv_w, rows_out):
    s = ag.shape[0]
    nc = DC // LANES

    def body(a_ref, g_ref, d_ref, w_ref, da_ref, dg_ref, dw_ref, db_ref, upad, dpad, acc):
        _glu_into(upad, a_ref, g_ref, s)
        _fill_padded(dpad, d_ref, s, ROW_TILE, CONV_HALO)
        acc[...] = jnp.zeros_like(acc)

        def step(c, carry):
            r0 = pl.multiple_of(c * CONV_CHUNK, CONV_CHUNK)
            dcur = dpad[pl.ds(r0 + CONV_HALO, CONV_CHUNK), :]
            du0 = jnp.zeros((CONV_CHUNK, LANES), F32)
            for k in range(CW):
                du0 = du0 + w_ref[k:k + 1, :] * _tap(dpad, r0, CW - 1 - k)
                acc[8 * k:8 * k + 8, :] += _colsum8(dcur * _tap(upad, r0, k))
            acc[8 * CW:8 * CW + 8, :] += _colsum8(dcur)
            av = a_ref[pl.ds(r0, CONV_CHUNK), :]
            sg = _sigmoid(g_ref[pl.ds(r0, CONV_CHUNK), :])
            da_ref[pl.ds(r0, CONV_CHUNK), :] = (du0 * sg).astype(da_ref.dtype)
            dg_ref[pl.ds(r0, CONV_CHUNK), :] = (du0 * av * (sg * (1.0 - sg))).astype(dg_ref.dtype)
            return carry

        lax.fori_loop(0, s // CONV_CHUNK, step, 0)
        if rows_out > s:
            zeros = jnp.zeros((rows_out - s, LANES), da_ref.dtype)
            da_ref[s:rows_out, :] = zeros
            dg_ref[s:rows_out, :] = zeros
        for k in range(CW):
            dw_ref[k:k + 1, :] = jnp.sum(acc[8 * k:8 * k + 8, :], axis=0, keepdims=True)
        db_ref[...] = jnp.sum(acc[8 * CW:8 * CW + 8, :], axis=0, keepdims=True)

    col = lambda off: pl.BlockSpec((s, LANES), lambda c: (0, c + off))
    ocol = pl.BlockSpec((rows_out, LANES), lambda c: (0, c))
    return _pallas(
        body, name="conf_conv_bwd", grid=(nc,),
        in_specs=[col(0), col(nc), col(0), pl.BlockSpec((CW, LANES), lambda c: (0, c))],
        out_specs=[ocol, ocol, pl.BlockSpec((CW, LANES), lambda c: (0, c)), pl.BlockSpec((1, LANES), lambda c: (0, c))],
        out_shape=[_sds((rows_out, DC), MXU_DTYPE)] * 2 + [_sds((CW, DC), F32), _sds((1, DC), F32)],
        scratch_shapes=[pltpu.VMEM((s + 2 * CONV_HALO, LANES), F32)] * 2 + [pltpu.VMEM((8 * (CW + 1), LANES), F32)],
        semantics=("parallel",),
    )(ag, ag, du1, conv_w)


Q_TILE = 2 * GW
K_WIN = PAIR_ROWS * GW


def _bias_table(rpb_rev):
    def body(p_ref, t_ref):
        kcol = lax.broadcasted_iota(jnp.int32, (GW, LANES), 0)
        lane = lax.broadcasted_iota(jnp.int32, (GW, LANES), 1)
        qcol = lane % GW
        cs = jnp.clip(qcol - NA_ROWS, 0, GW - 2 * NA_ROWS)
        colvalid = (kcol >= cs) & (kcol < cs + 2 * NA_ROWS)
        neg = jnp.full((GW, LANES), NEG, F32)

        def skew(h, ro, shift):
            if ro < 0 or ro >= 2 * NA_ROWS - 1:
                return neg
            row = jnp.broadcast_to(p_ref[h * 16 + ro:h * 16 + ro + 1, :], (GW, LANES))
            return pltpu.roll(row, shift, 1, stride=1, stride_axis=0)

        for h in range(NH):
            for b in range(TAB_BLOCKS):
                val = jnp.where(lane < GW, skew(h, b - 1, GW + 1), skew(h, b - 2, 1))
                t_ref[h, b * GW:(b + 1) * GW, :] = jnp.where(colvalid, val, neg)

    return _pallas(body, name="attn_bias_table", out_shape=_sds((NH, TAB_BLOCKS * GW, LANES), F32))(rpb_rev)


def _rpb_grad(tt):
    def body(t_ref, o_ref):
        lane = lax.broadcasted_iota(jnp.int32, (GW, LANES), 1)
        si = lax.broadcasted_iota(jnp.int32, (GW, GW), 0)
        ti = lax.broadcasted_iota(jnp.int32, (GW, GW), 1)
        flip = jnp.where(si + ti == GW - 1, 1.0, 0.0).astype(F32)
        o_ref[...] = jnp.zeros_like(o_ref)
        for h in range(NH):
            for ro in range(2 * NA_ROWS - 1):
                lo = t_ref[h, (ro + 1) * GW:(ro + 2) * GW, :]
                hi = t_ref[h, (ro + 2) * GW:(ro + 3) * GW, :]
                g = jnp.where(lane < GW, lo + pltpu.roll(hi, GW, 1), 0.0)
                gf = jnp.dot(flip, g, preferred_element_type=F32, precision=lax.Precision.HIGHEST)
                sk = pltpu.roll(gf, 0, 1, stride=1, stride_axis=0)
                o_ref[h * 16 + ro:h * 16 + ro + 1, :] = jnp.sum(sk, axis=0, keepdims=True)

    return _pallas(body, name="attn_rpb_grad", out_shape=_sds((NH * 16, LANES), F32))(tt)


def _attn_geometry(i, rows):
    wsp = jnp.clip(2 * i - NA_ROWS // 2, 0, rows - PAIR_ROWS)
    k0 = pl.multiple_of(wsp * GW, GW)
    t0 = pl.multiple_of((wsp - 2 * i + NA_ROWS) * GW, GW)
    jr = lax.broadcasted_iota(jnp.int32, (K_WIN, Q_TILE), 0) // GW
    rr = lax.broadcasted_iota(jnp.int32, (K_WIN, Q_TILE), 1) // GW
    kr = wsp + jr
    wsr = jnp.clip(2 * i + rr - NA_ROWS // 2, 0, rows - NA_ROWS)
    rowmask = jnp.where((kr >= wsr) & (kr < wsr + NA_ROWS), 0.0, NEG).astype(F32)
    return k0, t0, rowmask


def _two_heads_on_lanes(xt):
    feat = lax.broadcasted_iota(jnp.int32, xt.shape, 0)
    zero = jnp.zeros_like(xt)
    return jnp.concatenate([jnp.where(feat < HD, xt, zero), jnp.where(feat >= HD, xt, zero)], axis=1)


def _two_heads_on_rows(x):
    lane = lax.broadcasted_iota(jnp.int32, x.shape, 1)
    zero = jnp.zeros_like(x)
    return jnp.concatenate([jnp.where(lane < HD, x, zero), jnp.where(lane >= HD, x, zero)], axis=0)


def _pick_heads(x2):
    n = x2.shape[0] // 2
    lane = lax.broadcasted_iota(jnp.int32, (n, LANES), 1)
    return jnp.where(lane < HD, x2[:n], x2[n:])


_TN = (((0,), (0,)), ((), ()))


def _attn_fwd(qkv, tab, s):
    rows = s // GW
    npair = rows // 2

    def body(q_ref, kv_ref, tab_ref, o_ref, lse_ref):
        i = pl.program_id(0)
        k0, t0, rowmask = _attn_geometry(i, rows)
        for p in range(NH // 2):
            cq = slice(p * LANES, (p + 1) * LANES)
            ck = slice(DA + p * LANES, DA + (p + 1) * LANES)
            cv = slice(2 * DA + p * LANES, 2 * DA + (p + 1) * LANES)
            qm2 = _two_heads_on_lanes(q_ref[:, cq].T)
            s_loc = jnp.dot(kv_ref[pl.ds(k0, K_WIN), ck], qm2, preferred_element_type=F32) * SCALE
            s_ctx = jnp.dot(kv_ref[pl.ds(s, CTX), ck], qm2, preferred_element_type=F32) * SCALE
            p_loc, p_ctx = [], []
            for hh in range(2):
                h = 2 * p + hh
                ch = slice(hh * Q_TILE, (hh + 1) * Q_TILE)
                sl = s_loc[:, ch] + tab_ref[h, pl.ds(t0, K_WIN), :] + rowmask
                sc = s_ctx[:, ch]
                m = jnp.maximum(jnp.max(sl, axis=0, keepdims=True), jnp.max(sc, axis=0, keepdims=True))
                el = jnp.exp(sl - m)
                ec = jnp.exp(sc - m)
                l = jnp.sum(el, axis=0, keepdims=True) + jnp.sum(ec, axis=0, keepdims=True)
                inv = 1.0 / l
                lse_ref[h:h + 1, :] = m + jnp.log(l)
                p_loc.append((el * inv).astype(MXU_DTYPE))
                p_ctx.append((ec * inv).astype(MXU_DTYPE))
            o2 = (lax.dot_general(jnp.concatenate(p_loc, axis=1), kv_ref[pl.ds(k0, K_WIN), cv], _TN, preferred_element_type=F32)
                  + lax.dot_general(jnp.concatenate(p_ctx, axis=1), kv_ref[pl.ds(s, CTX), cv], _TN, preferred_element_type=F32))
            o_ref[:, cq] = _pick_heads(o2).astype(o_ref.dtype)

    return _pallas(
        body, name="attn_fwd", grid=(npair,),
        in_specs=[pl.BlockSpec((Q_TILE, DA), lambda i: (i, 0)), pl.BlockSpec(memory_space=pltpu.VMEM),
                  pl.BlockSpec(memory_space=pltpu.VMEM)],
        out_specs=[pl.BlockSpec((Q_TILE, DA), lambda i: (i, 0)), pl.BlockSpec((NH, Q_TILE), lambda i: (0, i))],
        out_shape=[_sds((s, D), MXU_DTYPE), _sds((NH, s), F32)],
        semantics=("arbitrary",),
    )(qkv, qkv, tab)


def _attn_bwd(qkv, tab, lse, dycat, s):
    rows = s // GW
    npair = rows // 2
    sa = s + CTX
    nzero = CTX // Q_TILE

    def body(q_ref, do_ref, lse_ref, kv_ref, tab_ref, dq_ref, dkv_ref, tt_ref, dk_acc, dv_acc):
        i = pl.program_id(0)

        @pl.when(i == 0)
        def _():
            dk_acc[...] = jnp.zeros_like(dk_acc)
            dv_acc[...] = jnp.zeros_like(dv_acc)
            tt_ref[...] = jnp.zeros_like(tt_ref)

        @pl.when(i >= npair)
        def _():
            dq_ref[...] = jnp.zeros_like(dq_ref)

        @pl.when(i < npair)
        def _():
            k0, t0, rowmask = _attn_geometry(i, rows)
            for p in range(NH // 2):
                cq = slice(p * LANES, (p + 1) * LANES)
                ck = slice(DA + p * LANES, DA + (p + 1) * LANES)
                cv = slice(2 * DA + p * LANES, 2 * DA + (p + 1) * LANES)
                qp = q_ref[:, cq]
                dop = do_ref[:, cq].astype(MXU_DTYPE)
                qm2 = _two_heads_on_lanes(qp.T)
                dom2 = _two_heads_on_lanes(dop.T)
                kw = kv_ref[pl.ds(k0, K_WIN), ck]
                kc = kv_ref[pl.ds(s, CTX), ck]
                vw = kv_ref[pl.ds(k0, K_WIN), cv]
                vc = kv_ref[pl.ds(s, CTX), cv]
                s_loc = jnp.dot(kw, qm2, preferred_element_type=F32) * SCALE
                s_ctx = jnp.dot(kc, qm2, preferred_element_type=F32) * SCALE
                dp_loc = jnp.dot(vw, dom2, preferred_element_type=F32)
                dp_ctx = jnp.dot(vc, dom2, preferred_element_type=F32)
                p_loc, p_ctx, ds_loc, ds_ctx = [], [], [], []
                for hh in range(2):
                    h = 2 * p + hh
                    ch = slice(hh * Q_TILE, (hh + 1) * Q_TILE)
                    lse_h = lse_ref[h:h + 1, :]
                    pl_ = jnp.exp(s_loc[:, ch] + tab_ref[h, pl.ds(t0, K_WIN), :] + rowmask - lse_h)
                    pc_ = jnp.exp(s_ctx[:, ch] - lse_h)
                    dpl = dp_loc[:, ch]
                    dpc = dp_ctx[:, ch]
                    delta = jnp.sum(pl_ * dpl, axis=0, keepdims=True) + jnp.sum(pc_ * dpc, axis=0, keepdims=True)
                    dsl = pl_ * (dpl - delta)
                    dsc = pc_ * (dpc - delta)
                    tt_ref[h, pl.ds(t0, K_WIN), :] += dsl
                    p_loc.append(pl_.astype(MXU_DTYPE))
                    p_ctx.append(pc_.astype(MXU_DTYPE))
                    ds_loc.append((dsl * SCALE).astype(MXU_DTYPE))
                    ds_ctx.append((dsc * SCALE).astype(MXU_DTYPE))
                p_loc, p_ctx = jnp.concatenate(p_loc, axis=1), jnp.concatenate(p_ctx, axis=1)
                ds_loc, ds_ctx = jnp.concatenate(ds_loc, axis=1), jnp.concatenate(ds_ctx, axis=1)
                do_rows = _two_heads_on_rows(dop)
                q_rows = _two_heads_on_rows(qp)
                dv_acc[pl.ds(k0, K_WIN), cq] += jnp.dot(p_loc, do_rows, preferred_element_type=F32)
                dv_acc[pl.ds(s, CTX), cq] += jnp.dot(p_ctx, do_rows, preferred_element_type=F32)
                dk_acc[pl.ds(k0, K_WIN), cq] += jnp.dot(ds_loc, q_rows, preferred_element_type=F32)
                dk_acc[pl.ds(s, CTX), cq] += jnp.dot(ds_ctx, q_rows, preferred_element_type=F32)
                dq2 = (lax.dot_general(ds_loc, kw, _TN, preferred_element_type=F32)
                       + lax.dot_general(ds_ctx, kc, _TN, preferred_element_type=F32))
                dq_ref[:, cq] = _pick_heads(dq2).astype(dq_ref.dtype)

        @pl.when(i == npair - 1)
        def _():
            def cp(c, carry):
                r0 = pl.multiple_of(c * ROW_TILE, ROW_TILE)
                dkv_ref[pl.ds(r0, ROW_TILE), 0:DA] = dk_acc[pl.ds(r0, ROW_TILE), :].astype(dkv_ref.dtype)
                dkv_ref[pl.ds(r0, ROW_TILE), DA:2 * DA] = dv_acc[pl.ds(r0, ROW_TILE), :].astype(dkv_ref.dtype)
                return carry

            lax.fori_loop(0, sa // ROW_TILE, cp, 0)

    qmap = lambda i: (jnp.minimum(i, npair - 1), 0)
    return _pallas(
        body, name="attn_bwd", grid=(npair + nzero,),
        in_specs=[pl.BlockSpec((Q_TILE, DA), qmap), pl.BlockSpec((Q_TILE, DA), qmap),
                  pl.BlockSpec((NH, Q_TILE), lambda i: (0, jnp.minimum(i, npair - 1))),
                  pl.BlockSpec(memory_space=pltpu.VMEM), pl.BlockSpec(memory_space=pltpu.VMEM)],
        out_specs=[pl.BlockSpec((Q_TILE, DA), lambda i: (i, 0)), pl.BlockSpec(memory_space=pltpu.VMEM),
                   pl.BlockSpec(memory_space=pltpu.VMEM)],
        out_shape=[_sds((sa, DA), MXU_DTYPE), _sds((sa, 2 * DA), MXU_DTYPE), _sds((NH, TAB_BLOCKS * GW, LANES), F32)],
        scratch_shapes=[pltpu.VMEM((sa, DA), F32)] * 2,
        semantics=("arbitrary",),
    )(qkv, dycat, lse, qkv, tab)


def _tile(n, prefs):
    for t in prefs:
        if n % t == 0:
            return t
    raise ValueError((n, prefs))


def _local_step(x, ctx, tgt, mod, mod_c, vec, w_in, late_weights, rpb_rev, early_grads=None):
    s = x.shape[0]
    sa = s + CTX
    ts = _tile(s, (1024, 512, 256))
    ts2 = _tile(s, (2048, 1024, 512, 256))
    tsa = _tile(sa, (1088, 640, 256))
    tsa2 = _tile(sa, (2176, 640, 256))
    sh1, sc1, gt1, sh2, sc2, gt2 = (mod[i:i + 1] for i in range(6))
    csh1, csc1 = mod_c[0:1], mod_c[1:2]
    act = MXU_DTYPE

    tab = _bias_table(rpb_rev)
    h_all = _rmsmod_fwd(x, ctx, vec["g_norm1"], sc1, sh1, csc1, csh1)
    qkv = _mm(h_all, w_in, mode="nn", m=sa, n=3 * DA, k=D, tm=tsa2, tn=512, tk=D, out_dtype=MXU_DTYPE, name="mm_qkv")
    ag = _mm(h_all, w_in, mode="nn", m=s, n=2 * DC, k=D, tm=ts2, tn=512, tk=D, out_dtype=F32, name="mm_ag", b_off=(0, 3))
    ycat, lse = _attn_fwd(qkv, tab, s)
    u1 = _conf_conv_fwd(ag, vec["conv_w"], vec["conv_b"])
    ycat = _conf_ln_fwd(u1, vec["ln_g"], vec["ln_b"], ycat)
    w_out, w_up, w_down = late_weights(ycat) if callable(late_weights) else late_weights
    y = _mm(ycat, w_out, mode="nn", m=s, n=D, k=D, tm=ts2, tn=512, tk=D, out_dtype=F32, name="mm_out")
    x1, h2 = _resid_rmsmod_fwd(x, y, gt1, vec["g_norm2"], sc2, sh2)
    u = _mm(h2, w_up, mode="nn", m=s, n=2 * DFF, k=D, tm=ts2, tn=512, tk=D, out_dtype=act, name="mm_up")
    f = _ffn_act_fwd(u, vec["ffn_conv_w"], vec["ffn_conv_b"])
    z = _mm(f, w_down, mode="nn", m=s, n=D, k=DFF, tm=ts, tn=D, tk=DFF, out_dtype=F32, name="mm_down")
    dx2, dz, loss, dgt2, dgf = _final_fwd_bwd(x1, z, gt2, vec["g_final"], tgt)

    df = _mm(dz, w_down, mode="nt", m=s, n=DFF, k=D, tm=ts, tn=DFF, tk=D, out_dtype=act, name="mm_down_dx")
    d_w_down = _mm(f, dz, mode="tn", m=DFF, n=D, k=s, tm=DFF, tn=D, tk=ts, out_dtype=F32, name="mm_down_dw")
    dug, duv, dfw_g, dfw_v, dfb_g, dfb_v = _ffn_act_bwd(u, df, vec["ffn_conv_w"], vec["ffn_conv_b"])
    dh2 = _mm([dug, duv], w_up, mode="nt", m=s, n=D, k=2 * DFF, tm=ts, tn=D, tk=DFF, out_dtype=F32, name="mm_up_dx")
    d_w_up = _mm(h2, [dug, duv], mode="tn", m=D, n=2 * DFF, k=s, tm=D, tn=DFF // 2, tk=ts, out_dtype=F32, name="mm_up_dw")
    sc2_b = sc2 if early_grads is None else sc2 + early_grads(d_w_up, d_w_down)
    dsh2, dsc2, dg2, dx1, dy, dgt1 = _rmsmod_bwd(x1, dh2, vec["g_norm2"], sc2_b, name="rmsmod2_bwd", add=dx2, resid=(gt1, y))
    dycat = _mm(dy, w_out, mode="nt", m=s, n=D, k=D, tm=ts2, tn=512, tk=D, out_dtype=F32, name="mm_out_dx")
    d_w_out = _mm(ycat, dy, mode="tn", m=D, n=D, k=s, tm=D, tn=D, tk=ts, out_dtype=F32, name="mm_out_dw")
    du1, dln_g, dln_b = _conf_ln_bwd(dycat, u1, vec["ln_g"], vec["ln_b"])
    da, dg, dconv_w, dconv_b = _conf_conv_bwd(ag, du1, vec["conv_w"], sa)
    dq, dkv, tt = _attn_bwd(qkv, tab, lse, dycat, s)
    drpb_rev = _rpb_grad(tt)
    d_pieces = [dq, dkv, da, dg]
    dh = _mm(d_pieces, w_in, mode="nt", m=sa, n=D, k=NIN, tm=tsa2, tn=D, tk=512, out_dtype=F32, name="mm_in_dx")
    d_w_in = _mm(h_all, d_pieces, mode="tn", m=D, n=NIN, k=sa, tm=D, tn=512, tk=tsa, out_dtype=F32, name="mm_in_dw",
                 k_outer=True)
    dsh1, dsc1, dg1, grad_x = _rmsmod_bwd(x, dh, vec["g_norm1"], sc1, name="rmsmod1_bwd", add=dx1)
    dcsh1, dcsc1, dg1c = _rmsmod_bwd(ctx, dh, vec["g_norm1"], csc1, name="rmsmod1_ctx_bwd", dh_row_off=s // ROW_TILE)

    small = dict(
        dmod=[dsh1, dsc1, dgt1, dsh2, dsc2, dgt2], dmod_c=[dcsh1, dcsc1],
        g_norm1=[dg1, dg1c], g_norm2=dg2, g_final=dgf, conv_b=dconv_b, ln_g=dln_g, ln_b=dln_b, conv_w=dconv_w,
        ffn_conv_w=[dfw_g, dfw_v], ffn_conv_b=[dfb_g, dfb_v], rpb_rev=drpb_rev,
    )
    return loss, grad_x, d_w_in, d_w_out, d_w_up, d_w_down, small


N_CHIPS = 4
HBM = pl.BlockSpec(memory_space=pl.ANY)
BIG = {"w_in": ("col", (D, NIN)), "w_out": ("row", (D, D)), "w_up": ("col", (D, 2 * DFF)), "w_down": ("row", (DFF, D))}
BIG_NAMES = tuple(BIG)
LATE_NAMES = ("w_out", "w_up", "w_down")


def _shard_shape(name):
    kind, (r, c) = BIG[name]
    return (r, c // N_CHIPS) if kind == "col" else (r // N_CHIPS, c)


def _half_rows(name):
    return _shard_shape(name)[0] // 2


def _place():
    x, y, c = lax.axis_index("x"), lax.axis_index("y"), lax.axis_index("c")
    others = [(1 - x, y), (x, 1 - y), (1 - x, 1 - y)]
    return x, y, c, 2 * x + y, (x, y, 1 - c), others


def _whole_region(ref, name, chip, half):
    kind, _ = BIG[name]
    r, c = _shard_shape(name)
    if kind == "col":
        return ref.at[pl.ds(half * (r // 2), r // 2), pl.ds(chip * c, c)]
    return ref.at[pl.ds(chip * r + half * (r // 2), r // 2), :]


def _remote(src, dst, send_sem, recv_sem, to):
    return pltpu.make_async_remote_copy(src_ref=src, dst_ref=dst, send_sem=send_sem, recv_sem=recv_sem,
                                        device_id=to, device_id_type=MESH)


def _gather_small(v, name):
    m_per, n = v.shape

    def body(x_ref, out_ref, send_sems, recv_sems, local_sem):
        x, y, c, _, sibling, others = _place()
        me = (x, y, c)

        def rows(px, py, pc):
            return out_ref.at[pl.ds((4 * px + 2 * py + pc) * m_per, m_per), :]

        def copy(k, block, to, src=None):
            return _remote(rows(*block) if src is None else src, rows(*block), send_sems.at[k], recv_sems.at[k], to)

        mine = pltpu.make_async_copy(x_ref, rows(*me), local_sem)
        mine.start()
        first = [copy(0, me, sibling, src=x_ref)]
        first += [copy(1 + j, me, (*chip, c), src=x_ref) for j, chip in enumerate(others)]
        for cp in first:
            cp.start()
        passed = [copy(4 + j, (*chip, c), sibling) for j, chip in enumerate(others)]
        for j, chip in enumerate(others):
            copy(1 + j, (*chip, c), me).wait_recv()
            passed[j].start()
        copy(0, sibling, me).wait_recv()
        for j, chip in enumerate(others):
            copy(4 + j, (*chip, 1 - c), me).wait_recv()
        for cp in first + passed:
            cp.wait_send()
        mine.wait()

    return pl.pallas_call(
        body, name=name, out_shape=_sds((8 * m_per, n), v.dtype),
        in_specs=[pl.BlockSpec(memory_space=pltpu.VMEM)], out_specs=pl.BlockSpec(memory_space=pltpu.VMEM),
        scratch_shapes=[pltpu.SemaphoreType.DMA((7,)), pltpu.SemaphoreType.DMA((7,)), pltpu.SemaphoreType.DMA],
    )(v)


def _cast_into_whole(name, shard, chip):
    kind, whole = BIG[name]
    r, c = shard.shape
    if kind == "col":
        tr = 256
        o_spec = pl.BlockSpec((tr, c), lambda i, ch: (i, ch[0]))
    else:
        tr = _tile(r, (128, 352))
        o_spec = pl.BlockSpec((tr, c), lambda i, ch: (ch[0] * (r // tr) + i, 0))

    def body(ch_ref, x_ref, o_ref):
        del ch_ref
        o_ref[...] = x_ref[...].astype(o_ref.dtype)

    return _pallas(body, name="cast_" + name, prefetch=1, grid=(r // tr,),
                   in_specs=[pl.BlockSpec((tr, c), lambda i, ch: (i, 0))], out_specs=o_spec,
                   out_shape=_sds(whole, MXU_DTYPE), semantics=("parallel",))(chip, shard)


def _gather_weights(wholes, names, label):
    nw = len(names)

    def body(*refs):
        outs = refs[nw:2 * nw]
        send_sems, recv_sems = refs[2 * nw:]
        _, _, c, chip, sibling, others = _place()
        sends = []
        for w, name in enumerate(names):
            mine = _whole_region(outs[w], name, chip, c)
            for t, (ox, oy) in enumerate(others):
                cp = _remote(mine, mine, send_sems.at[w, t], recv_sems.at[w, t], (ox, oy, c))
                cp.start()
                sends.append(cp)
        for w, name in enumerate(names):
            for t, (ox, oy) in enumerate(others):
                got = _whole_region(outs[w], name, 2 * ox + oy, c)
                _remote(got, got, send_sems.at[w, t], recv_sems.at[w, t], (ox, oy, c)).wait_recv()
                cp = _remote(got, got, send_sems.at[w, 3 + t], recv_sems.at[w, 3 + t], sibling)
                cp.start()
                sends.append(cp)
        for w, name in enumerate(names):
            for t, (ox, oy) in enumerate(others):
                got = _whole_region(outs[w], name, 2 * ox + oy, 1 - c)
                _remote(got, got, send_sems.at[w, 3 + t], recv_sems.at[w, 3 + t], sibling).wait_recv()
        for cp in sends:
            cp.wait_send()

    return pl.pallas_call(
        body, name=label,
        out_shape=[_sds(a.shape, a.dtype) for a in wholes],
        in_specs=[HBM] * nw, out_specs=[HBM] * nw,
        input_output_aliases={i: i for i in range(nw)},
        scratch_shapes=[pltpu.SemaphoreType.DMA((nw, 6)), pltpu.SemaphoreType.DMA((nw, 6))],
    )(*wholes)


SEM = pl.BlockSpec(memory_space=pltpu.SEMAPHORE)
IN_HBM = pl.BlockSpec(memory_space=pltpu.HBM)
DATAFLOW = pltpu.SideEffectType.DATAFLOW_SIDE_EFFECTING


def _keep_in_hbm(a):
    return pltpu.with_memory_space_constraint(a, pltpu.HBM)


def _gather_start(wholes, names):
    nw = len(names)
    ns = 2 * 3 * nw

    def body(*refs):
        ins = refs[:nw]
        sems = refs[nw:nw + ns]
        token = refs[2 * nw + ns]
        _, _, c, chip, _, others = _place()
        for w, name in enumerate(names):
            mine = _whole_region(ins[w], name, chip, c)
            for t, (ox, oy) in enumerate(others):
                k = 2 * (3 * w + t)
                _remote(mine, mine, sems[k], sems[k + 1], (ox, oy, c)).start()
        token[...] = jnp.zeros_like(token)

    res = pl.pallas_call(
        body, name="gather_late_start",
        out_shape=(*[pltpu.SemaphoreType.DMA(())] * ns, *[pltpu.HBM(a.shape, a.dtype) for a in wholes], _sds((8, LANES), F32)),
        in_specs=[IN_HBM] * nw, out_specs=(*[SEM] * ns, *[IN_HBM] * nw, pl.BlockSpec(memory_space=pltpu.VMEM)),
        input_output_aliases={i: ns + i for i in range(nw)},
        compiler_params=pltpu.CompilerParams(has_side_effects=DATAFLOW),
    )(*[_keep_in_hbm(a) for a in wholes])
    return list(res[:ns]), list(res[ns:ns + nw]), res[ns + nw]


def _gather_wait(sems, wholes, names, after):
    nw = len(names)
    ns = len(sems)

    def body(*refs):
        ins = refs[:nw]
        sem_refs = refs[nw:nw + ns]
        _, _, c, chip, _, others = _place()
        for w, name in enumerate(names):
            mine = _whole_region(ins[w], name, chip, c)
            for t, (ox, oy) in enumerate(others):
                got = _whole_region(ins[w], name, 2 * ox + oy, c)
                k = 2 * (3 * w + t)
                cp = _remote(mine, got, sem_refs[k], sem_refs[k + 1], (ox, oy, c))
                cp.wait_send()
                cp.wait_recv()

    return pl.pallas_call(
        body, name="gather_late_wait",
        out_shape=tuple(pltpu.HBM(a.shape, a.dtype) for a in wholes),
        in_specs=[IN_HBM] * nw + [SEM] * ns + [pl.BlockSpec(memory_space=pl.ANY)], out_specs=tuple([IN_HBM] * nw),
        input_output_aliases={i: i for i in range(nw)},
        compiler_params=pltpu.CompilerParams(has_side_effects=DATAFLOW),
    )(*wholes, *sems, after)


def _forward_halves(wholes, names):
    nw = len(names)

    def body(*refs):
        outs = refs[nw:2 * nw]
        send_sems, recv_sems = refs[2 * nw:]
        _, _, c, _, sibling, others = _place()
        sends = []
        for w, name in enumerate(names):
            for t, (ox, oy) in enumerate(others):
                got = _whole_region(outs[w], name, 2 * ox + oy, c)
                cp = _remote(got, got, send_sems.at[w, t], recv_sems.at[w, t], sibling)
                cp.start()
                sends.append(cp)
        for w, name in enumerate(names):
            for t, (ox, oy) in enumerate(others):
                got = _whole_region(outs[w], name, 2 * ox + oy, 1 - c)
                _remote(got, got, send_sems.at[w, t], recv_sems.at[w, t], sibling).wait_recv()
        for cp in sends:
            cp.wait_send()

    return pl.pallas_call(
        body, name="gather_late_forward",
        out_shape=[_sds(a.shape, a.dtype) for a in wholes],
        in_specs=[HBM] * nw, out_specs=[HBM] * nw,
        input_output_aliases={i: i for i in range(nw)},
        scratch_shapes=[pltpu.SemaphoreType.DMA((nw, 3)), pltpu.SemaphoreType.DMA((nw, 3))],
    )(*wholes)


def _compact_shape(name, dtype):
    kind, (r, c) = BIG[name]
    return _sds((r // 2, c), dtype)


def _swap_halves(grads, names, label):
    nw = len(names)

    def body(*refs):
        ins, outs = refs[:nw], refs[nw:2 * nw]
        send_sems, recv_sems = refs[2 * nw:]
        _, _, c, _, sibling, _ = _place()
        copies = []
        for w, name in enumerate(names):
            kind, (r, _) = BIG[name]
            half = _half_rows(name)
            if kind == "col":
                parts = [(ins[w].at[pl.ds((1 - c) * half, half), :], outs[w])]
            else:
                parts = [(ins[w].at[pl.ds(jj * 2 * half + (1 - c) * half, half), :], outs[w].at[pl.ds(jj * half, half), :])
                         for jj in range(N_CHIPS)]
            for t, (src, dst) in enumerate(parts):
                cp = _remote(src, dst, send_sems.at[w, t], recv_sems.at[w, t], sibling)
                cp.start()
                copies.append(cp)
        for cp in copies:
            cp.wait()

    return pl.pallas_call(
        body, name=label,
        out_shape=[_compact_shape(n, F32) for n in names],
        in_specs=[HBM] * nw, out_specs=[HBM] * nw,
        scratch_shapes=[pltpu.SemaphoreType.DMA((nw, N_CHIPS)), pltpu.SemaphoreType.DMA((nw, N_CHIPS))],
    )(*grads)


def _add_halves(name, grad, got, core):
    kind, (r, c) = BIG[name]
    half = _half_rows(name)
    if kind == "col":
        t = 128
        grid = (half // t,)
        g_spec = pl.BlockSpec((t, c), lambda i, cr: (cr[0] * (half // t) + i, 0))
        o_spec = pl.BlockSpec((t, c), lambda i, cr: (i, 0))
    else:
        t = half
        grid = (N_CHIPS,)
        g_spec = pl.BlockSpec((t, c), lambda i, cr: (2 * i + cr[0], 0))
        o_spec = pl.BlockSpec((t, c), lambda i, cr: (i, 0))

    def body(c_ref, g_ref, b_ref, o_ref):
        del c_ref
        o_ref[...] = (g_ref[...] + b_ref[...]).astype(o_ref.dtype)

    return pl.pallas_call(
        body, name="grad_add_" + name,
        grid_spec=pltpu.PrefetchScalarGridSpec(num_scalar_prefetch=1, grid=grid, in_specs=[g_spec, o_spec], out_specs=o_spec),
        out_shape=_compact_shape(name, BF16),
        compiler_params=pltpu.CompilerParams(dimension_semantics=("parallel",), vmem_limit_bytes=VMEM_LIMIT),
    )(core, grad, got)


def _piece(ref, name, chip):
    kind, _ = BIG[name]
    r, c = _shard_shape(name)
    if kind == "col":
        return ref.at[:, pl.ds(chip * c, c)]
    return ref.at[pl.ds(chip * (r // 2), r // 2), :]


def _landing_shape(name):
    r, c = _shard_shape(name)
    return (N_CHIPS - 1, r // 2, c)


def _exchange_shards(parts, names, label):
    nw = len(names)

    def body(*refs):
        ins, outs = refs[:nw], refs[nw:2 * nw]
        send_sems, recv_sems = refs[2 * nw:]
        _, _, c, _, _, others = _place()
        sends = []
        for w, name in enumerate(names):
            for t, (ox, oy) in enumerate(others):
                cp = _remote(_piece(ins[w], name, 2 * ox + oy), outs[w].at[t], send_sems.at[w, t], recv_sems.at[w, t], (ox, oy, c))
                cp.start()
                sends.append(cp)
        for w, name in enumerate(names):
            for t, (ox, oy) in enumerate(others):
                got = outs[w].at[t]
                _remote(got, got, send_sems.at[w, t], recv_sems.at[w, t], (ox, oy, c)).wait_recv()
        for cp in sends:
            cp.wait_send()

    return pl.pallas_call(
        body, name=label,
        out_shape=[_sds(_landing_shape(n), BF16) for n in names],
        in_specs=[HBM] * nw, out_specs=[HBM] * nw,
        scratch_shapes=[pltpu.SemaphoreType.DMA((nw, 3)), pltpu.SemaphoreType.DMA((nw, 3))],
    )(*parts)


def _exchange_start(parts, names):
    nw = len(names)
    ns = 2 * 3 * nw

    def body(*refs):
        ins, lands = refs[:nw], refs[nw:2 * nw]
        sems = refs[2 * nw:2 * nw + ns]
        token = refs[4 * nw + ns]
        _, _, c, _, _, others = _place()
        for w, name in enumerate(names):
            for t, (ox, oy) in enumerate(others):
                k = 2 * (3 * w + t)
                _remote(_piece(ins[w], name, 2 * ox + oy), lands[w].at[t], sems[k], sems[k + 1], (ox, oy, c)).start()
        token[...] = jnp.zeros_like(token)

    lands = [_keep_in_hbm(lax.empty(_landing_shape(n), BF16)) for n in names]
    res = pl.pallas_call(
        body, name="grad_exchange_start",
        out_shape=(*[pltpu.SemaphoreType.DMA(())] * ns, *[pltpu.HBM(a.shape, a.dtype) for a in parts],
                   *[pltpu.HBM(a.shape, a.dtype) for a in lands], _sds((8, LANES), F32)),
        in_specs=[IN_HBM] * (2 * nw),
        out_specs=(*[SEM] * ns, *[IN_HBM] * (2 * nw), pl.BlockSpec(memory_space=pltpu.VMEM)),
        input_output_aliases={i: ns + i for i in range(2 * nw)},
        compiler_params=pltpu.CompilerParams(has_side_effects=DATAFLOW),
    )(*[_keep_in_hbm(a) for a in parts], *lands)
    return list(res[:ns]), list(res[ns:ns + nw]), list(res[ns + nw:ns + 2 * nw]), res[ns + 2 * nw]


def _exchange_wait(sems, parts, lands, names, after):
    nw = len(names)
    ns = len(sems)

    def body(*refs):
        ins, land_refs = refs[:nw], refs[nw:2 * nw]
        sem_refs = refs[2 * nw:2 * nw + ns]
        _, _, c, _, _, others = _place()
        for w, name in enumerate(names):
            for t, (ox, oy) in enumerate(others):
                k = 2 * (3 * w + t)
                cp = _remote(_piece(ins[w], name, 2 * ox + oy), land_refs[w].at[t], sem_refs[k], sem_refs[k + 1], (ox, oy, c))
                cp.wait_send()
                cp.wait_recv()

    res = pl.pallas_call(
        body, name="grad_exchange_wait",
        out_shape=tuple(pltpu.HBM(a.shape, a.dtype) for a in (*parts, *lands)),
        in_specs=[IN_HBM] * (2 * nw) + [SEM] * ns + [pl.BlockSpec(memory_space=pl.ANY)],
        out_specs=tuple([IN_HBM] * (2 * nw)),
        input_output_aliases={i: i for i in range(2 * nw)},
        compiler_params=pltpu.CompilerParams(has_side_effects=DATAFLOW),
    )(*parts, *lands, *sems, after)
    return list(res[:nw]), list(res[nw:])


def _sum_chips(name, part, got, chip):
    kind, _ = BIG[name]
    _, r, c = got.shape
    t = _tile(r, (128, 352))
    if kind == "col":
        own = pl.BlockSpec((t, c), lambda i, ch: (i, ch[0]))
    else:
        own = pl.BlockSpec((t, c), lambda i, ch: (ch[0] * (r // t) + i, 0))

    def body(ch_ref, p_ref, g_ref, o_ref):
        del ch_ref
        acc = p_ref[...].astype(F32)
        for j in range(N_CHIPS - 1):
            acc = acc + g_ref[j].astype(F32)
        o_ref[...] = acc

    return _pallas(
        body, name="grad_sum_" + name, prefetch=1, grid=(r // t,),
        in_specs=[own, pl.BlockSpec((N_CHIPS - 1, t, c), lambda i, ch: (0, i, 0))],
        out_specs=pl.BlockSpec((t, c), lambda i, ch: (i, 0)),
        out_shape=_sds((r, c), F32), semantics=("parallel",),
    )(chip, part, got)


def _send_halves(sums):
    nw = len(sums)

    def body(*refs):
        ins, outs = refs[:nw], refs[nw:2 * nw]
        send_sems, recv_sems = refs[2 * nw:]
        _, _, _, _, sibling, _ = _place()
        copies = [_remote(ins[w], outs[w], send_sems.at[w], recv_sems.at[w], sibling) for w in range(nw)]
        for cp in copies:
            cp.start()
        for cp in copies:
            cp.wait()

    return pl.pallas_call(
        body, name="grad_send_halves",
        out_shape=[_sds(a.shape, a.dtype) for a in sums],
        in_specs=[HBM] * nw, out_specs=[HBM] * nw,
        scratch_shapes=[pltpu.SemaphoreType.DMA((nw,)), pltpu.SemaphoreType.DMA((nw,))],
    )(*sums)


EARLY_GRADS = ("w_up", "w_down")
LAST_GRADS = ("w_in", "w_out")


def _chip_partials(grads, names, core, label):
    got = _swap_halves(grads, names, label)
    return [_add_halves(n, grads[i], got[i], core) for i, n in enumerate(names)]


def _reduce_early_start(grads, core):
    parts = _chip_partials(grads, EARLY_GRADS, core, "grad_swap_early")
    return _exchange_start(parts, EARLY_GRADS)


def _reduce_early_finish(started, after, chip):
    sems, parts, lands, _ = started
    parts, lands = _exchange_wait(sems, parts, lands, EARLY_GRADS, after)
    return [_sum_chips(n, parts[i], lands[i], chip) for i, n in enumerate(EARLY_GRADS)]


def _reduce_last(grads, core, chip):
    parts = _chip_partials(grads, LAST_GRADS, core, "grad_swap_last")
    gathered = _exchange_shards(parts, LAST_GRADS, "grad_exchange_last")
    return [_sum_chips(n, parts[i], gathered[i], chip) for i, n in enumerate(LAST_GRADS)]


HI = lax.Precision.HIGHEST
MOD_COLS = 6 * D // N_CHIPS
COND_ROWS = 16


def _silu(v):
    return v * _sigmoid(v)


GATHER_ROWS = 48
FFW_COLS = 2 * DFF // N_CHIPS
CONV_COLS = DC // N_CHIPS


def _pack_cond(c, ffn_w, conv_w):
    def body(c_ref, f_ref, w_ref, o_ref):
        o_ref[...] = jnp.zeros_like(o_ref)
        o_ref[0:1, 0:D] = c_ref[...]
        o_ref[8:11, :] = f_ref[...]
        o_ref[16:16 + CW, 0:CONV_COLS] = w_ref[...]

    return _pallas(body, name="pack_cond", out_shape=_sds((GATHER_ROWS, FFW_COLS), F32))(c, ffn_w, conv_w)


def _unpack_cond(got, c_ctx):
    def body(g_ref, c_ref, cond_ref, f_ref, w_ref):
        cond_ref[...] = jnp.zeros_like(cond_ref)
        for d in range(8):
            cond_ref[d:d + 1, :] = g_ref[d * GATHER_ROWS:d * GATHER_ROWS + 1, 0:D]
        cond_ref[8:9, :] = c_ref[...]
        for j in range(N_CHIPS):
            r0 = 2 * j * GATHER_ROWS
            f_ref[:, j * FFW_COLS:(j + 1) * FFW_COLS] = g_ref[r0 + 8:r0 + 11, :]
            w_ref[:, j * CONV_COLS:(j + 1) * CONV_COLS] = g_ref[r0 + 16:r0 + 16 + CW, 0:CONV_COLS]

    return _pallas(body, name="unpack_cond",
                   out_shape=[_sds((COND_ROWS, D), F32), _sds((3, 2 * DFF), F32), _sds((CW, DC), F32)])(got, c_ctx)


def _chip_cols(rows, width):
    return pl.BlockSpec((rows, width), lambda i, ch: (0, ch[0]))


def _whole(shape):
    return pl.BlockSpec(shape, lambda i, ch: (0,) * len(shape))


def _mod_shard(cond, w_mod, b_mod, chip):
    def body(ch_ref, c_ref, w_ref, b_ref, o_ref):
        del ch_ref
        o_ref[...] = jnp.dot(_silu(c_ref[...]), w_ref[...], preferred_element_type=F32, precision=HI) + b_ref[...]

    return _pallas(body, name="mod_fwd", prefetch=1, grid=(1,),
                   in_specs=[_whole((COND_ROWS, D)), _whole((D, MOD_COLS)), _chip_cols(1, MOD_COLS)],
                   out_specs=_whole((COND_ROWS, MOD_COLS)),
                   out_shape=_sds((COND_ROWS, MOD_COLS), F32))(chip, cond, w_mod, b_mod)


def _unpack_mod(mods, dev):
    def body(dev_ref, m_ref, me_ref, c_ref):
        rowi = lax.broadcasted_iota(jnp.int32, (COND_ROWS, MOD_COLS), 0)
        mine, ctx = [], []
        for j in range(N_CHIPS):
            blk = m_ref[2 * j * COND_ROWS:(2 * j + 1) * COND_ROWS, :]
            mine.append(jnp.sum(jnp.where(rowi == dev_ref[0], blk, 0.0), axis=0, keepdims=True))
            ctx.append(blk[8:9, :])
        mine = jnp.concatenate(mine, axis=1)
        ctx = jnp.concatenate(ctx, axis=1)
        for k in range(6):
            me_ref[k:k + 1, :] = mine[:, k * D:(k + 1) * D]
        for k in range(2):
            c_ref[k:k + 1, :] = ctx[:, k * D:(k + 1) * D]

    return _pallas(body, name="unpack_mod", prefetch=1, grid=(1,),
                   in_specs=[_whole(mods.shape)], out_specs=[_whole((6, D)), _whole((2, D))],
                   out_shape=[_sds((6, D), F32), _sds((2, D), F32)])(dev, mods)


def _mod_weight_grad(cond, dmod_all, chip):
    def body(ch_ref, c_ref, d_ref, o_ref):
        del ch_ref
        o_ref[...] = lax.dot_general(_silu(c_ref[...]), d_ref[...], _TN, preferred_element_type=F32, precision=HI)

    return _pallas(body, name="mod_weight_grad", prefetch=1, grid=(1,),
                   in_specs=[_whole((COND_ROWS, D)), _chip_cols(COND_ROWS, MOD_COLS)], out_specs=_whole((D, MOD_COLS)),
                   out_shape=_sds((D, MOD_COLS), F32))(chip, cond, dmod_all)


def _cond_grad_partial(dmod_all, w_mod, chip):
    def body(ch_ref, d_ref, w_ref, o_ref):
        del ch_ref
        o_ref[...] = lax.dot_general(d_ref[...], w_ref[...], (((1,), (1,)), ((), ())), preferred_element_type=F32, precision=HI)

    return _pallas(body, name="cond_grad_partial", prefetch=1, grid=(1,),
                   in_specs=[pl.BlockSpec((8, MOD_COLS), lambda i, ch: (1, ch[0])), _whole((D, MOD_COLS))],
                   out_specs=_whole((8, D)), out_shape=_sds((8, D), F32))(chip, dmod_all, w_mod)


def _adam_math(w, g, m, v):
    nm = ADAM_B1 * m + (1.0 - ADAM_B1) * g
    nv = ADAM_B2 * v + (1.0 - ADAM_B2) * (g * g)
    c1 = 1.0 - ADAM_B1 ** ADAM_STEP
    c2 = 1.0 - ADAM_B2 ** ADAM_STEP
    return -ADAM_LR * ((nm / c1) / (jnp.sqrt(nv / c2) + ADAM_EPS) + ADAM_WD * w), nm, nv


def _cond_update(parts, c_ctx, m, v):
    def body(p_ref, c_ref, m_ref, v_ref, g_ref, d_ref, nm_ref, nv_ref):
        tot = p_ref[0:1, :]
        for j in range(1, N_CHIPS):
            tot = tot + p_ref[16 * j:16 * j + 1, :]
        cv = c_ref[...]
        sg = _sigmoid(cv)
        g = tot * (sg * (1.0 + cv * (1.0 - sg)))
        g_ref[...] = g
        d_ref[...], nm_ref[...], nv_ref[...] = _adam_math(cv, g, m_ref[...], v_ref[...])

    return _pallas(body, name="cond_update", out_shape=[_sds((1, D), F32)] * 4)(parts, c_ctx, m, v)


def _adamw(w, g, m, v, name):
    r, c = w.shape
    t = _tile(r, (128,)) if r % 128 == 0 and r > 128 else r

    def body(w_ref, g_ref, m_ref, v_ref, d_ref, nm_ref, nv_ref):
        d_ref[...], nm_ref[...], nv_ref[...] = _adam_math(w_ref[...], g_ref[...], m_ref[...], v_ref[...])

    blk = pl.BlockSpec((t, c), lambda i: (i, 0))
    return _pallas(body, name=name, grid=(r // t,), in_specs=[blk] * 4, out_specs=[blk] * 3,
                   out_shape=[_sds((r, c), F32)] * 3, semantics=("parallel",))(w, g, m, v)


def _adamw_cols(w, g_all, m, v, chip, name):
    r, c = w.shape

    def body(ch_ref, w_ref, g_ref, m_ref, v_ref, go_ref, d_ref, nm_ref, nv_ref):
        del ch_ref
        g = g_ref[...]
        go_ref[...] = g
        d_ref[...], nm_ref[...], nv_ref[...] = _adam_math(w_ref[...], g, m_ref[...], v_ref[...])

    return _pallas(body, name=name, prefetch=1, grid=(1,),
                   in_specs=[_whole((r, c)), _chip_cols(r, c), _whole((r, c)), _whole((r, c))],
                   out_specs=[_whole((r, c))] * 4, out_shape=[_sds((r, c), F32)] * 4)(chip, w, g_all, m, v)


def _adamw_halves(name, w, own, other, m, v, core):
    r, c = w.shape
    half = r // 2
    t = _tile(half, (128, 352))
    nh = half // t

    def pick(mine):
        def index(i, cr):
            first = cr[0] if mine else 1 - cr[0]
            return (jnp.clip(i - first * nh, 0, nh - 1), 0)
        return pl.BlockSpec((t, c), index)

    def body(c_ref, w_ref, own_ref, oth_ref, m_ref, v_ref, g_ref, d_ref, nm_ref, nv_ref):
        g = jnp.where(pl.program_id(0) // nh == c_ref[0], own_ref[...], oth_ref[...])
        g_ref[...] = g
        d_ref[...], nm_ref[...], nv_ref[...] = _adam_math(w_ref[...], g, m_ref[...], v_ref[...])

    blk = pl.BlockSpec((t, c), lambda i, cr: (i, 0))
    return _pallas(body, name="adamw_" + name, prefetch=1, grid=(2 * nh,),
                   in_specs=[blk, pick(True), pick(False), blk, blk], out_specs=[blk] * 4,
                   out_shape=[_sds((r, c), F32)] * 4, semantics=("parallel",))(core, w, own, other, m, v)


WEIGHTS = ("c_ctx", "w_mod", "b_mod", "g_norm1", "w_in", "rpb", "conv_w", "conv_b", "ln_g", "ln_b", "w_out", "g_norm2",
           "w_up", "ffn_conv_w", "ffn_conv_b", "w_down", "g_final")
PACK = (("dmod", 6 * D), ("dmod_c", 2 * D), ("g_norm1", D), ("g_norm1_ctx", D), ("g_norm2", D), ("g_final", D),
        ("conv_b", DC), ("ln_g", DC), ("ln_b", DC), ("ffn_conv_b", 2 * DFF), ("ffn_conv_w", 3 * 2 * DFF),
        ("conv_w", CW * DC), ("rpb_rev", NH * 16 * LANES), ("loss", LANES))
PACK_OFF = {}
_o = 0
for _n, _w in PACK:
    PACK_OFF[_n] = (_o, _w)
    _o += _w
PACK_N = -(-_o // (8 * LANES)) * (8 * LANES)
VECTORS = {"b_mod": (6 * D, ("dmod", "dmod_c")), "g_norm1": (D, ("g_norm1", "g_norm1_ctx")), "conv_b": (DC, ("conv_b",)),
           "ln_g": (DC, ("ln_g",)), "ln_b": (DC, ("ln_b",)), "g_norm2": (D, ("g_norm2",)),
           "ffn_conv_b": (2 * DFF, ("ffn_conv_b",)), "g_final": (D, ("g_final",))}
RPB_ROWS = NH * (2 * NA_ROWS - 1)
RPB_COLS = 4 * NA_ROWS - 1


def _pack_small(parts):
    arrs, places = [], []
    for name, _ in PACK:
        off, width = PACK_OFF[name]
        group = parts[name]
        rows = group[0].shape[0]
        row_w = sum(a.shape[1] for a in group)
        assert rows * row_w == width, (name, rows, row_w, width)
        col = 0
        for a in group:
            arrs.append(a)
            places.append([off + k * row_w + col for k in range(rows)])
            col += a.shape[1]

    def body(*refs):
        o_ref = refs[-1]
        o_ref[:, _o:PACK_N] = jnp.zeros((1, PACK_N - _o), F32)
        for ref, offs in zip(refs, places):
            n = ref.shape[1]
            for k, off in enumerate(offs):
                o_ref[:, off:off + n] = ref[k:k + 1, :]

    return _pallas(body, name="pack_small_grads", out_shape=_sds((1, PACK_N), F32))(*arrs)


def _small_update(packs, w, m, v):
    names = list(VECTORS)

    def body(*refs):
        it = iter(refs)
        p_ref = next(it)
        wmv = {n: (next(it), next(it), next(it)) for n in names}
        outs = {n: (next(it), next(it), next(it), next(it)) for n in names}
        dmod_ref, cw_ref, fw_ref, rpb_ref, loss_ref = next(it), next(it), next(it), next(it), next(it)

        def total(name):
            off, width = PACK_OFF[name]
            acc = p_ref[0:1, off:off + width]
            for d in range(1, 8):
                acc = acc + p_ref[d:d + 1, off:off + width]
            return acc

        for n in names:
            width, segs = VECTORS[n]
            g = total(segs[0])
            if len(segs) > 1:
                extra = total(segs[1])
                ew = extra.shape[1]
                g = g + extra if ew == width else jnp.concatenate([g[:, :ew] + extra, g[:, ew:]], axis=1)
            w_ref, m_ref, v_ref = wmv[n]
            g_ref, d_ref, nm_ref, nv_ref = outs[n]
            g_ref[...] = g
            d_ref[...], nm_ref[...], nv_ref[...] = _adam_math(w_ref[...], g, m_ref[...], v_ref[...])

        o_dmod = PACK_OFF["dmod"][0]
        dmod_ref[...] = jnp.zeros_like(dmod_ref)
        dmod_ref[0:8, :] = p_ref[:, o_dmod:o_dmod + 6 * D]
        dmod_ref[8:9, 0:2 * D] = total("dmod_c")
        for ref, name, rows in ((cw_ref, "conv_w", CW), (fw_ref, "ffn_conv_w", 3), (rpb_ref, "rpb_rev", NH * 16)):
            flat = total(name)
            n = ref.shape[1]
            for k in range(rows):
                ref[k:k + 1, :] = flat[:, k * n:(k + 1) * n]
        loss_ref[...] = total("loss")

    ins = [packs] + [a[n] for n in names for a in (w, m, v)]
    out_shape = [_sds((1, VECTORS[n][0]), F32) for n in names for _ in range(4)]
    out_shape += [_sds((COND_ROWS, 6 * D), F32), _sds((CW, DC), F32), _sds((3, 2 * DFF), F32), _sds((NH * 16, LANES), F32),
                  _sds((1, LANES), F32)]
    res = _pallas(body, name="small_update", out_shape=out_shape)(*ins)
    per = {n: tuple(res[4 * i:4 * i + 4]) for i, n in enumerate(names)}
    return (per, *res[4 * len(names):])


def _rpb_update(rev, w, m, v):
    def body(r_ref, w_ref, m_ref, v_ref, g_ref, d_ref, nm_ref, nv_ref):
        li = lax.broadcasted_iota(jnp.int32, (LANES, LANES), 0)
        co = lax.broadcasted_iota(jnp.int32, (LANES, LANES), 1)
        lane_of_co0 = GW - 1 + RPB_COLS // 2
        unflip = jnp.where((li == lane_of_co0 - co) & (co < RPB_COLS), 1.0, 0.0).astype(F32)
        g_all = jnp.dot(r_ref[...], unflip, preferred_element_type=F32, precision=HI)
        nr = 2 * NA_ROWS - 1
        for h in range(NH):
            rows = slice(h * nr, (h + 1) * nr)
            g = g_all[h * 16:h * 16 + nr, 0:RPB_COLS]
            g_ref[rows, :] = g
            d_ref[rows, :], nm_ref[rows, :], nv_ref[rows, :] = _adam_math(w_ref[rows, :], g, m_ref[rows, :], v_ref[rows, :])

    return _pallas(body, name="rpb_update", out_shape=[_sds((RPB_ROWS, RPB_COLS), F32)] * 4)(rev, w, m, v)


def kernel(x, c, ctx, c_ctx, w_mod, b_mod, g_norm1, w_in, rpb, conv_w, conv_b, ln_g, ln_b, w_out, g_norm2, w_up, ffn_conv_w, ffn_conv_b, w_down, g_final, loss_target, m_c_ctx, m_w_mod, m_b_mod, m_g_norm1, m_w_in, m_rpb, m_conv_w, m_conv_b, m_ln_g, m_ln_b, m_w_out, m_g_norm2, m_w_up, m_ffn_conv_w, m_ffn_conv_b, m_w_down, m_g_final, v_c_ctx, v_w_mod, v_b_mod, v_g_norm1, v_w_in, v_rpb, v_conv_w, v_conv_b, v_ln_g, v_ln_b, v_w_out, v_g_norm2, v_w_up, v_ffn_conv_w, v_ffn_conv_b, v_w_down, v_g_final):
    w = dict(c_ctx=c_ctx, w_mod=w_mod, b_mod=b_mod, g_norm1=g_norm1, w_in=w_in, rpb=rpb, conv_w=conv_w, conv_b=conv_b,
             ln_g=ln_g, ln_b=ln_b, w_out=w_out, g_norm2=g_norm2, w_up=w_up, ffn_conv_w=ffn_conv_w, ffn_conv_b=ffn_conv_b,
             w_down=w_down, g_final=g_final)
    mom = dict(c_ctx=m_c_ctx, w_mod=m_w_mod, b_mod=m_b_mod, g_norm1=m_g_norm1, w_in=m_w_in, rpb=m_rpb, conv_w=m_conv_w,
               conv_b=m_conv_b, ln_g=m_ln_g, ln_b=m_ln_b, w_out=m_w_out, g_norm2=m_g_norm2, w_up=m_w_up,
               ffn_conv_w=m_ffn_conv_w, ffn_conv_b=m_ffn_conv_b, w_down=m_w_down, g_final=m_g_final)
    var = dict(c_ctx=v_c_ctx, w_mod=v_w_mod, b_mod=v_b_mod, g_norm1=v_g_norm1, w_in=v_w_in, rpb=v_rpb, conv_w=v_conv_w,
               conv_b=v_conv_b, ln_g=v_ln_g, ln_b=v_ln_b, w_out=v_w_out, g_norm2=v_g_norm2, w_up=v_w_up,
               ffn_conv_w=v_ffn_conv_w, ffn_conv_b=v_ffn_conv_b, w_down=v_w_down, g_final=v_g_final)
    xi, yi, ci = lax.axis_index("x"), lax.axis_index("y"), lax.axis_index("c")
    dev = (4 * xi + 2 * yi + ci).astype(jnp.int32).reshape(1)
    chip = (2 * xi + yi).astype(jnp.int32).reshape(1)
    core = ci.astype(jnp.int32).reshape(1)
    c_ctx2 = c_ctx.reshape(1, D)
    g_final2 = g_final.reshape(1, D)
    mom["g_final"], var["g_final"] = m_g_final.reshape(1, D), v_g_final.reshape(1, D)

    got = _gather_small(_pack_cond(c, ffn_conv_w[0], conv_w[0]), "gather_cond")
    cond, ffn_w_all, conv_w_all = _unpack_cond(got, c_ctx2)

    mods = _gather_small(_mod_shard(cond, w_mod[0], b_mod, chip), "gather_mod")
    mod_me, mod_c = _unpack_mod(mods, dev)

    shards = {n: _cast_into_whole(n, w[n][0], chip) for n in BIG_NAMES}
    (w_in_all,) = _gather_weights([shards["w_in"]], ("w_in",), "gather_w_in")
    sems, late, token = _gather_start([shards[n] for n in LATE_NAMES], LATE_NAMES)
    mod_me = mod_me + token[0:1, 0:1]

    def late_weights(after):
        arrived = _gather_wait(sems, late, LATE_NAMES, after)
        return _forward_halves(list(arrived), LATE_NAMES)

    rpb_rev = jnp.pad(rpb[0][:, :, ::-1], ((0, 0), (0, 1), (48, LANES - 48 - RPB_COLS))).reshape(NH * 16, LANES)
    vec = dict(g_norm1=g_norm1, g_norm2=g_norm2, g_final=g_final2, conv_w=conv_w_all, conv_b=conv_b, ln_g=ln_g, ln_b=ln_b,
               ffn_conv_w=ffn_w_all, ffn_conv_b=ffn_conv_b)
    started = []

    def early_grads(d_up, d_down):
        started.append(_reduce_early_start([d_up, d_down], core))
        return started[0][3][0:1, 0:1]

    loss_p, grad_x, d_in, d_out, d_up, d_down, small = _local_step(
        x[0], ctx[0], loss_target[0], mod_me, mod_c, vec, w_in_all, late_weights, rpb_rev, early_grads)

    sums = dict(zip(EARLY_GRADS, _reduce_early_finish(started[0], grad_x, chip)))
    sums.update(zip(LAST_GRADS, _reduce_last([d_in, d_out], core, chip)))
    own = [sums[n] for n in BIG_NAMES]
    other = _send_halves(own)

    parts = dict(dmod=small["dmod"], dmod_c=small["dmod_c"], g_norm1=[small["g_norm1"][0]], g_norm1_ctx=[small["g_norm1"][1]],
                 g_norm2=[small["g_norm2"]], g_final=[small["g_final"]], conv_b=[small["conv_b"]], ln_g=[small["ln_g"]],
                 ln_b=[small["ln_b"]], ffn_conv_b=small["ffn_conv_b"], ffn_conv_w=small["ffn_conv_w"],
                 conv_w=[small["conv_w"]], rpb_rev=[small["rpb_rev"]], loss=[loss_p])
    pack = _pack_small(parts).reshape(8, PACK_N // 8)
    packs = _gather_small(pack, "gather_small_grads").reshape(8, PACK_N)
    w2 = dict(w, g_final=g_final2)
    per, dmod_all, g_conv_w_all, g_ffn_w_all, g_rpb_rev, loss_row = _small_update(packs, w2, mom, var)

    out = {}
    out.update(per)
    out["c_ctx"] = _cond_update(
        _gather_small(_cond_grad_partial(dmod_all, w_mod[0], chip), "gather_cond_grad"),
        c_ctx2, m_c_ctx.reshape(1, D), v_c_ctx.reshape(1, D))
    g_w_mod = _mod_weight_grad(cond, dmod_all, chip)
    out["w_mod"] = (g_w_mod, *_adamw(w_mod[0], g_w_mod, m_w_mod[0], v_w_mod[0], "adamw_w_mod"))
    for i, n in enumerate(BIG_NAMES):
        out[n] = _adamw_halves(n, w[n][0], own[i], other[i], mom[n][0], var[n][0], core)
    out["conv_w"] = _adamw_cols(conv_w[0], g_conv_w_all, m_conv_w[0], v_conv_w[0], chip, "adamw_conv_w")
    out["ffn_conv_w"] = _adamw_cols(ffn_conv_w[0], g_ffn_w_all, m_ffn_conv_w[0], v_ffn_conv_w[0], chip, "adamw_ffn_conv_w")
    flat = lambda a: a.reshape(RPB_ROWS, RPB_COLS)
    out["rpb"] = _rpb_update(g_rpb_rev, flat(rpb), flat(m_rpb), flat(v_rpb))

    res = [[out[n][k].reshape(w[n].shape) for n in WEIGHTS] for k in range(4)]
    return (loss_row[0, 0], grad_x[None], *res[0], *res[1], *res[2], *res[3])
```

```python
import functools

import jax
import jax.numpy as jnp
from jax import lax
from jax.experimental import pallas as pl
from jax.experimental.pallas import tpu as pltpu

F32 = jnp.float32
BF16 = jnp.bfloat16
MXU_DTYPE = jnp.bfloat16

D = 1024
CTX = 256
GW = 64
DA = 512
NH = 8
HD = 64
DC = 512
CW = 31
DFF = 2816
NIN = 3 * DA + 2 * DC
EPS = 1e-6
SCALE = HD ** -0.5
NEG = -1e30
NA_ROWS = 8
PAIR_ROWS = NA_ROWS + 1
TAB_BLOCKS = 17
LANES = 128
VMEM_LIMIT = 56 * 1024 * 1024

ADAM_LR = 0.001
ADAM_B1 = 0.9
ADAM_B2 = 0.999
ADAM_EPS = 1e-08
ADAM_WD = 0.01
ADAM_STEP = 10

MESH = pl.DeviceIdType.MESH


def _pallas(body, *, name, semantics=None, vmem=VMEM_LIMIT, prefetch=0, **kw):
    params = dict(vmem_limit_bytes=vmem)
    if semantics is not None:
        params["dimension_semantics"] = semantics
    if prefetch:
        kw["grid_spec"] = pltpu.PrefetchScalarGridSpec(
            num_scalar_prefetch=prefetch, grid=kw.pop("grid"), in_specs=kw.pop("in_specs"), out_specs=kw.pop("out_specs"),
            scratch_shapes=kw.pop("scratch_shapes", ()))
    return pl.pallas_call(body, name=name, compiler_params=pltpu.CompilerParams(**params), **kw)


def _sds(shape, dtype):
    return jax.ShapeDtypeStruct(shape, dtype)


def _vec_spec(n):
    return pl.BlockSpec((1, n), lambda *_: (0, 0))


def _colsum8(x):
    t, n = x.shape
    return jnp.sum(x.reshape(t // 8, 8, n), axis=0)


def _sigmoid(x):
    return 0.5 * jnp.tanh(0.5 * x) + 0.5


def _pieces(arrs, tile):
    lo, out = 0, []
    for a in arrs:
        nt = a.shape[1] // tile
        assert nt * tile == a.shape[1], (a.shape, tile)
        out.append((lo, nt))
        lo += nt
    return out


def _mm(a, b, *, mode, m, n, k, tm, tn, tk, out_dtype, name, a_off=(0, 0), b_off=(0, 0), k_outer=False):
    a_list = list(a) if isinstance(a, (list, tuple)) else [a]
    b_list = list(b) if isinstance(b, (list, tuple)) else [b]
    assert m % tm == 0 and n % tn == 0 and k % tk == 0, (name, m, n, k, tm, tn, tk)
    gi, gj, nk = m // tm, n // tn, k // tk
    a_tile = tm if mode == "tn" else tk
    a_pc = _pieces(a_list, a_tile) if len(a_list) > 1 else [(0, 1 << 30)]
    if mode == "nt":
        assert len(b_list) == 1
    b_pc = _pieces(b_list, tn) if len(b_list) > 1 else [(0, 1 << 30)]
    dims = {"nn": (((1,), (0,)), ((), ())), "nt": (((1,), (1,)), ((), ())), "tn": (((0,), (0,)), ((), ()))}[mode]
    k_outer = k_outer and nk > 1

    def ijk(fn):
        return (lambda i, kk, j: fn(i, j, kk)) if k_outer else fn

    def a_spec(lo, cnt):
        def loc(idx):
            return idx + a_off[1] if len(a_list) == 1 else jnp.clip(idx - lo, 0, cnt - 1)
        if mode == "tn":
            return pl.BlockSpec((tk, tm), ijk(lambda i, j, kk: (kk + a_off[0], loc(i))))
        return pl.BlockSpec((tm, tk), ijk(lambda i, j, kk: (i + a_off[0], loc(kk))))

    def b_spec(lo, cnt):
        def loc(idx):
            return idx + b_off[1] if len(b_list) == 1 else jnp.clip(idx - lo, 0, cnt - 1)
        if mode == "nt":
            return pl.BlockSpec((tn, tk), ijk(lambda i, j, kk: (j + b_off[0], kk + b_off[1])))
        return pl.BlockSpec((tk, tn), ijk(lambda i, j, kk: (kk + b_off[0], loc(j))))

    na, nb = len(a_list), len(b_list)
    in_place = nk > 1 and out_dtype == F32 and not k_outer

    def body(*refs):
        a_refs, b_refs, o_ref = refs[:na], refs[na:na + nb], refs[na + nb]
        if k_outer:
            i, kk, j = pl.program_id(0), pl.program_id(1), pl.program_id(2)
            acc = refs[na + nb + 1].at[j]
        else:
            i, j, kk = pl.program_id(0), pl.program_id(1), pl.program_id(2)
            acc = o_ref if in_place else (refs[na + nb + 1] if nk > 1 else None)
        a_idx = i if mode == "tn" else kk

        def step(ar, br):
            p = lax.dot_general(ar[...].astype(MXU_DTYPE), br[...].astype(MXU_DTYPE), dims,
                                preferred_element_type=F32)
            if nk == 1:
                o_ref[...] = p.astype(out_dtype)
                return

            @pl.when(kk == 0)
            def _():
                acc[...] = p

            @pl.when(kk > 0)
            def _():
                acc[...] += p

            if not in_place:
                @pl.when(kk == nk - 1)
                def _():
                    o_ref[...] = acc[...].astype(out_dtype)

        for pa, (alo, acnt) in enumerate(a_pc):
            for pb, (blo, bcnt) in enumerate(b_pc):
                if na == 1 and nb == 1:
                    step(a_refs[0], b_refs[0])
                else:
                    cond = (a_idx >= alo) & (a_idx < alo + acnt) & (j >= blo) & (j < blo + bcnt)
                    pl.when(cond)(functools.partial(step, a_refs[pa], b_refs[pb]))

    if k_outer:
        grid = (gi, nk, gj)
        o_spec = pl.BlockSpec((tm, tn), lambda i, kk, j: (i, jnp.where(kk == nk - 1, j, 0)))
        scratch = [pltpu.VMEM((gj, tm, tn), F32)]
        semantics = ("parallel", "arbitrary", "arbitrary")
    else:
        grid = (gi, gj, nk)
        o_spec = pl.BlockSpec((tm, tn), lambda i, j, kk: (i, j))
        scratch = [pltpu.VMEM((tm, tn), F32)] if nk > 1 and not in_place else []
        semantics = ("parallel", "parallel", "arbitrary")
    return _pallas(
        body, name=name, grid=grid,
        in_specs=[a_spec(*p) for p in a_pc] + [b_spec(*p) for p in b_pc],
        out_specs=o_spec, out_shape=_sds((m, n), out_dtype), scratch_shapes=scratch, semantics=semantics,
    )(*a_list, *b_list)


ROW_TILE = 256


def _rmsmod_fwd(x, ctx, g, sc, sh, csc, csh):
    s = x.shape[0]
    nt = s // ROW_TILE
    assert ctx.shape[0] == ROW_TILE

    def body(x_ref, c_ref, g_ref, sc_ref, sh_ref, csc_ref, csh_ref, o_ref):
        is_ctx = pl.program_id(0) == nt
        xv = jnp.where(is_ctx, c_ref[...], x_ref[...])
        scv = jnp.where(is_ctx, csc_ref[...], sc_ref[...])
        shv = jnp.where(is_ctx, csh_ref[...], sh_ref[...])
        r = lax.rsqrt(jnp.mean(xv * xv, axis=-1, keepdims=True) + EPS)
        y = xv * r * g_ref[...]
        o_ref[...] = (y * (1.0 + scv) + shv).astype(o_ref.dtype)

    return _pallas(
        body, name="rmsmod1_fwd", grid=(nt + 1,),
        in_specs=[pl.BlockSpec((ROW_TILE, D), lambda i: (jnp.minimum(i, nt - 1), 0)),
                  pl.BlockSpec((ROW_TILE, D), lambda i: (0, 0))] + [_vec_spec(D)] * 5,
        out_specs=pl.BlockSpec((ROW_TILE, D), lambda i: (i, 0)),
        out_shape=_sds((s + CTX, D), MXU_DTYPE),
        semantics=("arbitrary",),
    )(x, ctx, g, sc, sh, csc, csh)


def _resid_rmsmod_fwd(x, y, gt, g, sc, sh):
    s = x.shape[0]

    def body(x_ref, y_ref, gt_ref, g_ref, sc_ref, sh_ref, x1_ref, h_ref):
        x1 = x_ref[...] + gt_ref[...] * y_ref[...]
        x1_ref[...] = x1
        r = lax.rsqrt(jnp.mean(x1 * x1, axis=-1, keepdims=True) + EPS)
        h_ref[...] = ((x1 * r * g_ref[...]) * (1.0 + sc_ref[...]) + sh_ref[...]).astype(h_ref.dtype)

    row = pl.BlockSpec((ROW_TILE, D), lambda i: (i, 0))
    return _pallas(
        body, name="resid_rmsmod2_fwd", grid=(s // ROW_TILE,),
        in_specs=[row, row] + [_vec_spec(D)] * 4,
        out_specs=[row, row],
        out_shape=[_sds((s, D), F32), _sds((s, D), MXU_DTYPE)],
        semantics=("parallel",),
    )(x, y, gt, g, sc, sh)


def _final_fwd_bwd(x1, z, gt2, gf, tgt):
    s = x1.shape[0]
    nt = s // ROW_TILE

    def body(x1_ref, z_ref, gt_ref, gf_ref, t_ref, dx2_ref, dz_ref, loss_ref, dgt_ref, dgf_ref, a_loss, a_gt, a_gf):
        i = pl.program_id(0)

        @pl.when(i == 0)
        def _():
            a_loss[...] = jnp.zeros_like(a_loss)
            a_gt[...] = jnp.zeros_like(a_gt)
            a_gf[...] = jnp.zeros_like(a_gf)

        zv = z_ref[...]
        gt = gt_ref[...]
        gf_ = gf_ref[...]
        x2 = x1_ref[...] + gt * zv
        r = lax.rsqrt(jnp.mean(x2 * x2, axis=-1, keepdims=True) + EPS)
        xn = x2 * r
        e = xn * gf_ - t_ref[...]
        a_loss[...] += _colsum8(e * e)
        dyo = e * (1.0 / D)
        a_gf[...] += _colsum8(dyo * xn)
        gdy = gf_ * dyo
        dx2 = r * gdy - xn * (r * r) * jnp.mean(x2 * gdy, axis=-1, keepdims=True)
        dx2_ref[...] = dx2
        dz_ref[...] = (gt * dx2).astype(dz_ref.dtype)
        a_gt[...] += _colsum8(dx2 * zv)

        @pl.when(i == nt - 1)
        def _():
            tot = jnp.sum(jnp.sum(a_loss[...], axis=0, keepdims=True), axis=1, keepdims=True) * (0.5 / D)
            loss_ref[...] = jnp.broadcast_to(tot, loss_ref.shape)
            dgt_ref[...] = jnp.sum(a_gt[...], axis=0, keepdims=True)
            dgf_ref[...] = jnp.sum(a_gf[...], axis=0, keepdims=True)

    row = pl.BlockSpec((ROW_TILE, D), lambda i: (i, 0))
    return _pallas(
        body, name="final_norm_loss", grid=(nt,),
        in_specs=[row, row, _vec_spec(D), _vec_spec(D), row],
        out_specs=[row, row, _vec_spec(LANES), _vec_spec(D), _vec_spec(D)],
        out_shape=[_sds((s, D), F32), _sds((s, D), MXU_DTYPE), _sds((1, LANES), F32), _sds((1, D), F32), _sds((1, D), F32)],
        scratch_shapes=[pltpu.VMEM((8, D), F32)] * 3,
        semantics=("arbitrary",),
    )(x1, z, gt2, gf, tgt)


def _rmsmod_bwd(xin, dh, g, sc, *, name, dh_row_off=0, add=None, resid=None):
    s = xin.shape[0]
    nt = s // ROW_TILE
    want_dx = add is not None
    assert resid is None or want_dx

    def body(*refs):
        it = iter(refs)
        x_ref, dh_ref, g_ref, sc_ref = next(it), next(it), next(it), next(it)
        add_ref = next(it) if want_dx else None
        gt_ref, y_ref = (next(it), next(it)) if resid is not None else (None, None)
        dsh_ref, dsc_ref, dg_ref = next(it), next(it), next(it)
        dx_ref = next(it) if want_dx else None
        dy_ref, dgt_ref = (next(it), next(it)) if resid is not None else (None, None)
        a_sh, a_sc, a_g = next(it), next(it), next(it)
        a_gt = next(it) if resid is not None else None
        i = pl.program_id(0)

        @pl.when(i == 0)
        def _():
            a_sh[...] = jnp.zeros_like(a_sh)
            a_sc[...] = jnp.zeros_like(a_sc)
            a_g[...] = jnp.zeros_like(a_g)
            if a_gt is not None:
                a_gt[...] = jnp.zeros_like(a_gt)

        xv = x_ref[...]
        dhv = dh_ref[...]
        gv = g_ref[...]
        r = lax.rsqrt(jnp.mean(xv * xv, axis=-1, keepdims=True) + EPS)
        xn = xv * r
        a_sh[...] += _colsum8(dhv)
        a_sc[...] += _colsum8(dhv * (xn * gv))
        dn = dhv * (1.0 + sc_ref[...])
        a_g[...] += _colsum8(dn * xn)
        if want_dx:
            gdn = gv * dn
            dx = add_ref[...] + r * gdn - xn * (r * r) * jnp.mean(xv * gdn, axis=-1, keepdims=True)
            dx_ref[...] = dx
            if resid is not None:
                dy_ref[...] = (gt_ref[...] * dx).astype(dy_ref.dtype)
                a_gt[...] += _colsum8(dx * y_ref[...])

        @pl.when(i == nt - 1)
        def _():
            dsh_ref[...] = jnp.sum(a_sh[...], axis=0, keepdims=True)
            dsc_ref[...] = jnp.sum(a_sc[...], axis=0, keepdims=True)
            dg_ref[...] = jnp.sum(a_g[...], axis=0, keepdims=True)
            if a_gt is not None:
                dgt_ref[...] = jnp.sum(a_gt[...], axis=0, keepdims=True)

    row = pl.BlockSpec((ROW_TILE, D), lambda i: (i, 0))
    ins = [xin, dh, g, sc]
    in_specs = [row, pl.BlockSpec((ROW_TILE, D), lambda i: (i + dh_row_off, 0)), _vec_spec(D), _vec_spec(D)]
    out_specs = [_vec_spec(D)] * 3
    out_shape = [_sds((1, D), F32)] * 3
    scratch = [pltpu.VMEM((8, D), F32)] * 3
    if want_dx:
        ins.append(add)
        in_specs.append(row)
        out_specs.append(row)
        out_shape.append(_sds((s, D), F32))
    if resid is not None:
        ins += [resid[0], resid[1]]
        in_specs += [_vec_spec(D), row]
        out_specs += [row, _vec_spec(D)]
        out_shape += [_sds((s, D), MXU_DTYPE), _sds((1, D), F32)]
        scratch.append(pltpu.VMEM((8, D), F32))
    return _pallas(body, name=name, grid=(nt,), in_specs=in_specs, out_specs=out_specs, out_shape=out_shape,
                   scratch_shapes=scratch, semantics=("arbitrary",))(*ins)


FF_TILE = 128
FF_CHUNK = 128
HALO = 8


def _shift3(pad_ref, r0, ch):
    return tuple(pad_ref[pl.ds(r0 + HALO + d, ch), :] for d in (-1, 0, 1))


def _fill_padded(pad_ref, src_ref, s, ch, halo):
    zeros = jnp.zeros((halo, pad_ref.shape[1]), F32)
    pad_ref[0:halo, :] = zeros
    pad_ref[s + halo:s + 2 * halo, :] = zeros

    def cp(c, carry):
        r0 = pl.multiple_of(c * ch, ch)
        pad_ref[pl.ds(r0 + halo, ch), :] = src_ref[pl.ds(r0, ch), :].astype(F32)
        return carry

    lax.fori_loop(0, s // ch, cp, 0)


def _ffn_act_fwd(u, w, b):
    s = u.shape[0]
    nj = DFF // FF_TILE
    ch = FF_CHUNK

    def body(ug_ref, uv_ref, wg_ref, wv_ref, bg_ref, bv_ref, f_ref, gpad, vpad):
        _fill_padded(gpad, ug_ref, s, ch, HALO)
        _fill_padded(vpad, uv_ref, s, ch, HALO)

        def conv(pad, w_ref, b_ref, r0):
            prev, cur, nxt = _shift3(pad, r0, ch)
            return w_ref[0:1, :] * prev + w_ref[1:2, :] * cur + w_ref[2:3, :] * nxt + b_ref[...]

        def step(c, carry):
            r0 = pl.multiple_of(c * ch, ch)
            gc = conv(gpad, wg_ref, bg_ref, r0)
            vc = conv(vpad, wv_ref, bv_ref, r0)
            f_ref[pl.ds(r0, ch), :] = (gc * _sigmoid(gc) * vc).astype(f_ref.dtype)
            return carry

        lax.fori_loop(0, s // ch, step, 0)

    col = lambda off: pl.BlockSpec((s, FF_TILE), lambda j: (0, j + off))
    wsp = lambda off: pl.BlockSpec((3, FF_TILE), lambda j: (0, j + off))
    bsp = lambda off: pl.BlockSpec((1, FF_TILE), lambda j: (0, j + off))
    return _pallas(
        body, name="ffn_act_fwd", grid=(nj,),
        in_specs=[col(0), col(nj), wsp(0), wsp(nj), bsp(0), bsp(nj)],
        out_specs=col(0), out_shape=_sds((s, DFF), MXU_DTYPE),
        scratch_shapes=[pltpu.VMEM((s + 2 * HALO, FF_TILE), F32)] * 2,
        semantics=("parallel",),
    )(u, u, w, w, b, b)


def _ffn_act_bwd(u, df, w, b):
    s = u.shape[0]
    nj = DFF // FF_TILE
    ch = FF_CHUNK

    def body(ug_ref, uv_ref, df_ref, wg_ref, wv_ref, bg_ref, bv_ref,
             dug_ref, duv_ref, dwg_ref, dwv_ref, dbg_ref, dbv_ref, gpad, vpad, dgpad, dvpad, acc):
        _fill_padded(gpad, ug_ref, s, ch, HALO)
        _fill_padded(vpad, uv_ref, s, ch, HALO)
        zeros = jnp.zeros((HALO, FF_TILE), F32)
        for p in (dgpad, dvpad):
            p[0:HALO, :] = zeros
            p[s + HALO:s + 2 * HALO, :] = zeros
        acc[...] = jnp.zeros_like(acc)

        def step(c, carry):
            r0 = pl.multiple_of(c * ch, ch)
            gs = _shift3(gpad, r0, ch)
            vs = _shift3(vpad, r0, ch)
            gc = wg_ref[0:1, :] * gs[0] + wg_ref[1:2, :] * gs[1] + wg_ref[2:3, :] * gs[2] + bg_ref[...]
            vc = wv_ref[0:1, :] * vs[0] + wv_ref[1:2, :] * vs[1] + wv_ref[2:3, :] * vs[2] + bv_ref[...]
            sg = _sigmoid(gc)
            dfv = df_ref[pl.ds(r0, ch), :].astype(F32)
            dgc = dfv * vc * (sg * (1.0 + gc * (1.0 - sg)))
            dvc = dfv * (gc * sg)
            dgpad[pl.ds(r0 + HALO, ch), :] = dgc
            dvpad[pl.ds(r0 + HALO, ch), :] = dvc
            for t in range(3):
                acc[8 * t:8 * t + 8, :] += _colsum8(dgc * gs[t])
                acc[24 + 8 * t:32 + 8 * t, :] += _colsum8(dvc * vs[t])
            acc[48:56, :] += _colsum8(dgc)
            acc[56:64, :] += _colsum8(dvc)
            return carry

        lax.fori_loop(0, s // ch, step, 0)

        def step2(c, carry):
            r0 = pl.multiple_of(c * ch, ch)
            for pad, w_ref, o_ref in ((dgpad, wg_ref, dug_ref), (dvpad, wv_ref, duv_ref)):
                prev, cur, nxt = _shift3(pad, r0, ch)
                o_ref[pl.ds(r0, ch), :] = (w_ref[0:1, :] * nxt + w_ref[1:2, :] * cur + w_ref[2:3, :] * prev).astype(o_ref.dtype)
            return carry

        lax.fori_loop(0, s // ch, step2, 0)
        for t in range(3):
            dwg_ref[t:t + 1, :] = jnp.sum(acc[8 * t:8 * t + 8, :], axis=0, keepdims=True)
            dwv_ref[t:t + 1, :] = jnp.sum(acc[24 + 8 * t:32 + 8 * t, :], axis=0, keepdims=True)
        dbg_ref[...] = jnp.sum(acc[48:56, :], axis=0, keepdims=True)
        dbv_ref[...] = jnp.sum(acc[56:64, :], axis=0, keepdims=True)

    col = lambda off: pl.BlockSpec((s, FF_TILE), lambda j: (0, j + off))
    wsp = lambda off: pl.BlockSpec((3, FF_TILE), lambda j: (0, j + off))
    bsp = lambda off: pl.BlockSpec((1, FF_TILE), lambda j: (0, j + off))
    return _pallas(
        body, name="ffn_act_bwd", grid=(nj,),
        in_specs=[col(0), col(nj), col(0), wsp(0), wsp(nj), bsp(0), bsp(nj)],
        out_specs=[col(0), col(0), wsp(0), wsp(0), bsp(0), bsp(0)],
        out_shape=[_sds((s, DFF), MXU_DTYPE)] * 2 + [_sds((3, DFF), F32)] * 2 + [_sds((1, DFF), F32)] * 2,
        scratch_shapes=[pltpu.VMEM((s + 2 * HALO, FF_TILE), F32)] * 4 + [pltpu.VMEM((64, FF_TILE), F32)],
        semantics=("parallel",),
    )(u, u, df, w, w, b, b)


CONV_CHUNK = 64
CONV_HALO = 16


def _tap(pad_ref, r0, k):
    return pad_ref[pl.ds(r0 + CONV_HALO - CW // 2 + k, CONV_CHUNK), :]


def _glu_into(pad_ref, a_ref, g_ref, s):
    zeros = jnp.zeros((CONV_HALO, LANES), F32)
    pad_ref[0:CONV_HALO, :] = zeros
    pad_ref[s + CONV_HALO:s + 2 * CONV_HALO, :] = zeros

    def cp(c, carry):
        r0 = pl.multiple_of(c * ROW_TILE, ROW_TILE)
        pad_ref[pl.ds(r0 + CONV_HALO, ROW_TILE), :] = a_ref[pl.ds(r0, ROW_TILE), :] * _sigmoid(g_ref[pl.ds(r0, ROW_TILE), :])
        return carry

    lax.fori_loop(0, s // ROW_TILE, cp, 0)


def _conf_conv_fwd(ag, conv_w, conv_b):
    s = ag.shape[0]
    nc = DC // LANES

    def body(a_ref, g_ref, w_ref, b_ref, o_ref, upad):
        _glu_into(upad, a_ref, g_ref, s)

        def step(c, carry):
            r0 = pl.multiple_of(c * CONV_CHUNK, CONV_CHUNK)
            acc = jnp.broadcast_to(b_ref[...], (CONV_CHUNK, LANES))
            for k in range(CW):
                acc = acc + w_ref[k:k + 1, :] * _tap(upad, r0, k)
            o_ref[pl.ds(r0, CONV_CHUNK), :] = acc
            return carry

        lax.fori_loop(0, s // CONV_CHUNK, step, 0)

    col = lambda off: pl.BlockSpec((s, LANES), lambda c: (0, c + off))
    return _pallas(
        body, name="conf_conv_fwd", grid=(nc,),
        in_specs=[col(0), col(nc), pl.BlockSpec((CW, LANES), lambda c: (0, c)), pl.BlockSpec((1, LANES), lambda c: (0, c))],
        out_specs=col(0), out_shape=_sds((s, DC), F32),
        scratch_shapes=[pltpu.VMEM((s + 2 * CONV_HALO, LANES), F32)],
        semantics=("parallel",),
    )(ag, ag, conv_w, conv_b)


def _ln_stats(x):
    mu = jnp.mean(x, axis=-1, keepdims=True)
    xc = x - mu
    var = jnp.mean(xc * xc, axis=-1, keepdims=True)
    rstd = lax.rsqrt(var + EPS)
    return xc * rstd, rstd


def _conf_ln_fwd(u1, ln_g, ln_b, ycat):
    s = u1.shape[0]

    def body(u_ref, g_ref, b_ref, ycat_ref, o_ref):
        del ycat_ref
        xhat, _ = _ln_stats(u_ref[...])
        y = xhat * g_ref[...] + b_ref[...]
        o_ref[...] = (y * _sigmoid(y)).astype(o_ref.dtype)

    return _pallas(
        body, name="conf_ln_fwd", grid=(s // ROW_TILE,),
        in_specs=[pl.BlockSpec((ROW_TILE, DC), lambda i: (i, 0)), _vec_spec(DC), _vec_spec(DC),
                  pl.BlockSpec(memory_space=pl.ANY)],
        out_specs=pl.BlockSpec((ROW_TILE, DC), lambda i: (i, 1)),
        out_shape=_sds(ycat.shape, ycat.dtype),
        input_output_aliases={3: 0},
        semantics=("parallel",),
    )(u1, ln_g, ln_b, ycat)


def _conf_ln_bwd(dycat, u1, ln_g, ln_b):
    s = u1.shape[0]
    nt = s // ROW_TILE

    def body(dy_ref, u_ref, g_ref, b_ref, du_ref, dg_ref, db_ref, a_g, a_b):
        i = pl.program_id(0)

        @pl.when(i == 0)
        def _():
            a_g[...] = jnp.zeros_like(a_g)
            a_b[...] = jnp.zeros_like(a_b)

        xhat, rstd = _ln_stats(u_ref[...])
        gv = g_ref[...]
        y = xhat * gv + b_ref[...]
        sg = _sigmoid(y)
        dyl = dy_ref[...] * (sg * (1.0 + y * (1.0 - sg)))
        a_g[...] += _colsum8(dyl * xhat)
        a_b[...] += _colsum8(dyl)
        dxh = dyl * gv
        du_ref[...] = rstd * (dxh - jnp.mean(dxh, axis=-1, keepdims=True)
                              - xhat * jnp.mean(dxh * xhat, axis=-1, keepdims=True))

        @pl.when(i == nt - 1)
        def _():
            dg_ref[...] = jnp.sum(a_g[...], axis=0, keepdims=True)
            db_ref[...] = jnp.sum(a_b[...], axis=0, keepdims=True)

    return _pallas(
        body, name="conf_ln_bwd", grid=(nt,),
        in_specs=[pl.BlockSpec((ROW_TILE, DC), lambda i: (i, 1)), pl.BlockSpec((ROW_TILE, DC), lambda i: (i, 0)),
                  _vec_spec(DC), _vec_spec(DC)],
        out_specs=[pl.BlockSpec((ROW_TILE, DC), lambda i: (i, 0)), _vec_spec(DC), _vec_spec(DC)],
        out_shape=[_sds((s, DC), F32), _sds((1, DC), F32), _sds((1, DC), F32)],
        scratch_shapes=[pltpu.VMEM((8, DC), F32)] * 2,
        semantics=("arbitrary",),
    )(dycat, u1, ln_g, ln_b)


def _conf_conv_bwd(ag, du1, conv_w, rows_out):
    s = ag.shape[0]
    nc = DC // LANES

    def body(a_ref, g_ref, d_ref, w_ref, da_ref, dg_ref, dw_ref, db_ref, upad, dpad, acc):
        _glu_into(upad, a_ref, g_ref, s)
        _fill_padded(dpad, d_ref, s, ROW_TILE, CONV_HALO)
        acc[...] = jnp.zeros_like(acc)

        def step(c, carry):
            r0 = pl.multiple_of(c * CONV_CHUNK, CONV_CHUNK)
            dcur = dpad[pl.ds(r0 + CONV_HALO, CONV_CHUNK), :]
            du0 = jnp.zeros((CONV_CHUNK, LANES), F32)
            for k in range(CW):
                du0 = du0 + w_ref[k:k + 1, :] * _tap(dpad, r0, CW - 1 - k)
                acc[8 * k:8 * k + 8, :] += _colsum8(dcur * _tap(upad, r0, k))
            acc[8 * CW:8 * CW + 8, :] += _colsum8(dcur)
            av = a_ref[pl.ds(r0, CONV_CHUNK), :]
            sg = _sigmoid(g_ref[pl.ds(r0, CONV_CHUNK), :])
            da_ref[pl.ds(r0, CONV_CHUNK), :] = (du0 * sg).astype(da_ref.dtype)
            dg_ref[pl.ds(r0, CONV_CHUNK), :] = (du0 * av * (sg * (1.0 - sg))).astype(dg_ref.dtype)
            return carry

        lax.fori_loop(0, s // CONV_CHUNK, step, 0)
        if rows_out > s:
            zeros = jnp.zeros((rows_out - s, LANES), da_ref.dtype)
            da_ref[s:rows_out, :] = zeros
            dg_ref[s:rows_out, :] = zeros
        for k in range(CW):
            dw_ref[k:k + 1, :] = jnp.sum(acc[8 * k:8 * k + 8, :], axis=0, keepdims=True)
        db_ref[...] = jnp.sum(acc[8 * CW:8 * CW + 8, :], axis=0, keepdims=True)

    col = lambda off: pl.BlockSpec((s, LANES), lambda c: (0, c + off))
    ocol = pl.BlockSpec((rows_out, LANES), lambda c: (0, c))
    return _pallas(
        body, name="conf_conv_bwd", grid=(nc,),
        in_specs=[col(0), col(nc), col(0), pl.BlockSpec((CW, LANES), lambda c: (0, c))],
        out_specs=[ocol, ocol, pl.BlockSpec((CW, LANES), lambda c: (0, c)), pl.BlockSpec((1, LANES), lambda c: (0, c))],
        out_shape=[_sds((rows_out, DC), MXU_DTYPE)] * 2 + [_sds((CW, DC), F32), _sds((1, DC), F32)],
        scratch_shapes=[pltpu.VMEM((s + 2 * CONV_HALO, LANES), F32)] * 2 + [pltpu.VMEM((8 * (CW + 1), LANES), F32)],
        semantics=("parallel",),
    )(ag, ag, du1, conv_w)


Q_TILE = 2 * GW
K_WIN = PAIR_ROWS * GW


def _bias_table(rpb_rev):
    def body(p_ref, t_ref):
        kcol = lax.broadcasted_iota(jnp.int32, (GW, LANES), 0)
        lane = lax.broadcasted_iota(jnp.int32, (GW, LANES), 1)
        qcol = lane % GW
        cs = jnp.clip(qcol - NA_ROWS, 0, GW - 2 * NA_ROWS)
        colvalid = (kcol >= cs) & (kcol < cs + 2 * NA_ROWS)
        neg = jnp.full((GW, LANES), NEG, F32)

        def skew(h, ro, shift):
            if ro < 0 or ro >= 2 * NA_ROWS - 1:
                return neg
            row = jnp.broadcast_to(p_ref[h * 16 + ro:h * 16 + ro + 1, :], (GW, LANES))
            return pltpu.roll(row, shift, 1, stride=1, stride_axis=0)

        for h in range(NH):
            for b in range(TAB_BLOCKS):
                val = jnp.where(lane < GW, skew(h, b - 1, GW + 1), skew(h, b - 2, 1))
                t_ref[h, b * GW:(b + 1) * GW, :] = jnp.where(colvalid, val, neg)

    return _pallas(body, name="attn_bias_table", out_shape=_sds((NH, TAB_BLOCKS * GW, LANES), F32))(rpb_rev)


def _rpb_grad(tt):
    def body(t_ref, o_ref):
        lane = lax.broadcasted_iota(jnp.int32, (GW, LANES), 1)
        si = lax.broadcasted_iota(jnp.int32, (GW, GW), 0)
        ti = lax.broadcasted_iota(jnp.int32, (GW, GW), 1)
        flip = jnp.where(si + ti == GW - 1, 1.0, 0.0).astype(F32)
        o_ref[...] = jnp.zeros_like(o_ref)
        for h in range(NH):
            for ro in range(2 * NA_ROWS - 1):
                lo = t_ref[h, (ro + 1) * GW:(ro + 2) * GW, :]
                hi = t_ref[h, (ro + 2) * GW:(ro + 3) * GW, :]
                g = jnp.where(lane < GW, lo + pltpu.roll(hi, GW, 1), 0.0)
                gf = jnp.dot(flip, g, preferred_element_type=F32, precision=lax.Precision.HIGHEST)
                sk = pltpu.roll(gf, 0, 1, stride=1, stride_axis=0)
                o_ref[h * 16 + ro:h * 16 + ro + 1, :] = jnp.sum(sk, axis=0, keepdims=True)

    return _pallas(body, name="attn_rpb_grad", out_shape=_sds((NH * 16, LANES), F32))(tt)


def _attn_geometry(i, rows):
    wsp = jnp.clip(2 * i - NA_ROWS // 2, 0, rows - PAIR_ROWS)
    k0 = pl.multiple_of(wsp * GW, GW)
    t0 = pl.multiple_of((wsp - 2 * i + NA_ROWS) * GW, GW)
    jr = lax.broadcasted_iota(jnp.int32, (K_WIN, Q_TILE), 0) // GW
    rr = lax.broadcasted_iota(jnp.int32, (K_WIN, Q_TILE), 1) // GW
    kr = wsp + jr
    wsr = jnp.clip(2 * i + rr - NA_ROWS // 2, 0, rows - NA_ROWS)
    rowmask = jnp.where((kr >= wsr) & (kr < wsr + NA_ROWS), 0.0, NEG).astype(F32)
    return k0, t0, rowmask


def _two_heads_on_lanes(xt):
    feat = lax.broadcasted_iota(jnp.int32, xt.shape, 0)
    zero = jnp.zeros_like(xt)
    return jnp.concatenate([jnp.where(feat < HD, xt, zero), jnp.where(feat >= HD, xt, zero)], axis=1)


def _two_heads_on_rows(x):
    lane = lax.broadcasted_iota(jnp.int32, x.shape, 1)
    zero = jnp.zeros_like(x)
    return jnp.concatenate([jnp.where(lane < HD, x, zero), jnp.where(lane >= HD, x, zero)], axis=0)


def _pick_heads(x2):
    n = x2.shape[0] // 2
    lane = lax.broadcasted_iota(jnp.int32, (n, LANES), 1)
    return jnp.where(lane < HD, x2[:n], x2[n:])


_TN = (((0,), (0,)), ((), ()))


def _attn_fwd(qkv, tab, s):
    rows = s // GW
    npair = rows // 2

    def body(q_ref, kv_ref, tab_ref, o_ref, lse_ref):
        i = pl.program_id(0)
        k0, t0, rowmask = _attn_geometry(i, rows)
        for p in range(NH // 2):
            cq = slice(p * LANES, (p + 1) * LANES)
            ck = slice(DA + p * LANES, DA + (p + 1) * LANES)
            cv = slice(2 * DA + p * LANES, 2 * DA + (p + 1) * LANES)
            qm2 = _two_heads_on_lanes(q_ref[:, cq].T)
            s_loc = jnp.dot(kv_ref[pl.ds(k0, K_WIN), ck], qm2, preferred_element_type=F32) * SCALE
            s_ctx = jnp.dot(kv_ref[pl.ds(s, CTX), ck], qm2, preferred_element_type=F32) * SCALE
            p_loc, p_ctx = [], []
            for hh in range(2):
                h = 2 * p + hh
                ch = slice(hh * Q_TILE, (hh + 1) * Q_TILE)
                sl = s_loc[:, ch] + tab_ref[h, pl.ds(t0, K_WIN), :] + rowmask
                sc = s_ctx[:, ch]
                m = jnp.maximum(jnp.max(sl, axis=0, keepdims=True), jnp.max(sc, axis=0, keepdims=True))
                el = jnp.exp(sl - m)
                ec = jnp.exp(sc - m)
                l = jnp.sum(el, axis=0, keepdims=True) + jnp.sum(ec, axis=0, keepdims=True)
                inv = 1.0 / l
                lse_ref[h:h + 1, :] = m + jnp.log(l)
                p_loc.append((el * inv).astype(MXU_DTYPE))
                p_ctx.append((ec * inv).astype(MXU_DTYPE))
            o2 = (lax.dot_general(jnp.concatenate(p_loc, axis=1), kv_ref[pl.ds(k0, K_WIN), cv], _TN, preferred_element_type=F32)
                  + lax.dot_general(jnp.concatenate(p_ctx, axis=1), kv_ref[pl.ds(s, CTX), cv], _TN, preferred_element_type=F32))
            o_ref[:, cq] = _pick_heads(o2).astype(o_ref.dtype)

    return _pallas(
        body, name="attn_fwd", grid=(npair,),
        in_specs=[pl.BlockSpec((Q_TILE, DA), lambda i: (i, 0)), pl.BlockSpec(memory_space=pltpu.VMEM),
                  pl.BlockSpec(memory_space=pltpu.VMEM)],
        out_specs=[pl.BlockSpec((Q_TILE, DA), lambda i: (i, 0)), pl.BlockSpec((NH, Q_TILE), lambda i: (0, i))],
        out_shape=[_sds((s, D), MXU_DTYPE), _sds((NH, s), F32)],
        semantics=("arbitrary",),
    )(qkv, qkv, tab)


def _attn_bwd(qkv, tab, lse, dycat, s):
    rows = s // GW
    npair = rows // 2
    sa = s + CTX
    nzero = CTX // Q_TILE

    def body(q_ref, do_ref, lse_ref, kv_ref, tab_ref, dq_ref, dkv_ref, tt_ref, dk_acc, dv_acc):
        i = pl.program_id(0)

        @pl.when(i == 0)
        def _():
            dk_acc[...] = jnp.zeros_like(dk_acc)
            dv_acc[...] = jnp.zeros_like(dv_acc)
            tt_ref[...] = jnp.zeros_like(tt_ref)

        @pl.when(i >= npair)
        def _():
            dq_ref[...] = jnp.zeros_like(dq_ref)

        @pl.when(i < npair)
        def _():
            k0, t0, rowmask = _attn_geometry(i, rows)
            for p in range(NH // 2):
                cq = slice(p * LANES, (p + 1) * LANES)
                ck = slice(DA + p * LANES, DA + (p + 1) * LANES)
                cv = slice(2 * DA + p * LANES, 2 * DA + (p + 1) * LANES)
                qp = q_ref[:, cq]
                dop = do_ref[:, cq].astype(MXU_DTYPE)
                qm2 = _two_heads_on_lanes(qp.T)
                dom2 = _two_heads_on_lanes(dop.T)
                kw = kv_ref[pl.ds(k0, K_WIN), ck]
                kc = kv_ref[pl.ds(s, CTX), ck]
                vw = kv_ref[pl.ds(k0, K_WIN), cv]
                vc = kv_ref[pl.ds(s, CTX), cv]
                s_loc = jnp.dot(kw, qm2, preferred_element_type=F32) * SCALE
                s_ctx = jnp.dot(kc, qm2, preferred_element_type=F32) * SCALE
                dp_loc = jnp.dot(vw, dom2, preferred_element_type=F32)
                dp_ctx = jnp.dot(vc, dom2, preferred_element_type=F32)
                p_loc, p_ctx, ds_loc, ds_ctx = [], [], [], []
                for hh in range(2):
                    h = 2 * p + hh
                    ch = slice(hh * Q_TILE, (hh + 1) * Q_TILE)
                    lse_h = lse_ref[h:h + 1, :]
                    pl_ = jnp.exp(s_loc[:, ch] + tab_ref[h, pl.ds(t0, K_WIN), :] + rowmask - lse_h)
                    pc_ = jnp.exp(s_ctx[:, ch] - lse_h)
                    dpl = dp_loc[:, ch]
                    dpc = dp_ctx[:, ch]
                    delta = jnp.sum(pl_ * dpl, axis=0, keepdims=True) + jnp.sum(pc_ * dpc, axis=0, keepdims=True)
                    dsl = pl_ * (dpl - delta)
                    dsc = pc_ * (dpc - delta)
                    tt_ref[h, pl.ds(t0, K_WIN), :] += dsl
                    p_loc.append(pl_.astype(MXU_DTYPE))
                    p_ctx.append(pc_.astype(MXU_DTYPE))
                    ds_loc.append((dsl * SCALE).astype(MXU_DTYPE))
                    ds_ctx.append((dsc * SCALE).astype(MXU_DTYPE))
                p_loc, p_ctx = jnp.concatenate(p_loc, axis=1), jnp.concatenate(p_ctx, axis=1)
                ds_loc, ds_ctx = jnp.concatenate(ds_loc, axis=1), jnp.concatenate(ds_ctx, axis=1)
                do_rows = _two_heads_on_rows(dop)
                q_rows = _two_heads_on_rows(qp)
                dv_acc[pl.ds(k0, K_WIN), cq] += jnp.dot(p_loc, do_rows, preferred_element_type=F32)
                dv_acc[pl.ds(s, CTX), cq] += jnp.dot(p_ctx, do_rows, preferred_element_type=F32)
                dk_acc[pl.ds(k0, K_WIN), cq] += jnp.dot(ds_loc, q_rows, preferred_element_type=F32)
                dk_acc[pl.ds(s, CTX), cq] += jnp.dot(ds_ctx, q_rows, preferred_element_type=F32)
                dq2 = (lax.dot_general(ds_loc, kw, _TN, preferred_element_type=F32)
                       + lax.dot_general(ds_ctx, kc, _TN, preferred_element_type=F32))
                dq_ref[:, cq] = _pick_heads(dq2).astype(dq_ref.dtype)

        @pl.when(i == npair - 1)
        def _():
            def cp(c, carry):
                r0 = pl.multiple_of(c * ROW_TILE, ROW_TILE)
                dkv_ref[pl.ds(r0, ROW_TILE), 0:DA] = dk_acc[pl.ds(r0, ROW_TILE), :].astype(dkv_ref.dtype)
                dkv_ref[pl.ds(r0, ROW_TILE), DA:2 * DA] = dv_acc[pl.ds(r0, ROW_TILE), :].astype(dkv_ref.dtype)
                return carry

            lax.fori_loop(0, sa // ROW_TILE, cp, 0)

    qmap = lambda i: (jnp.minimum(i, npair - 1), 0)
    return _pallas(
        body, name="attn_bwd", grid=(npair + nzero,),
        in_specs=[pl.BlockSpec((Q_TILE, DA), qmap), pl.BlockSpec((Q_TILE, DA), qmap),
                  pl.BlockSpec((NH, Q_TILE), lambda i: (0, jnp.minimum(i, npair - 1))),
                  pl.BlockSpec(memory_space=pltpu.VMEM), pl.BlockSpec(memory_space=pltpu.VMEM)],
        out_specs=[pl.BlockSpec((Q_TILE, DA), lambda i: (i, 0)), pl.BlockSpec(memory_space=pltpu.VMEM),
                   pl.BlockSpec(memory_space=pltpu.VMEM)],
        out_shape=[_sds((sa, DA), MXU_DTYPE), _sds((sa, 2 * DA), MXU_DTYPE), _sds((NH, TAB_BLOCKS * GW, LANES), F32)],
        scratch_shapes=[pltpu.VMEM((sa, DA), F32)] * 2,
        semantics=("arbitrary",),
    )(qkv, dycat, lse, qkv, tab)


def _tile(n, prefs):
    for t in prefs:
        if n % t == 0:
            return t
    raise ValueError((n, prefs))


def _local_step(x, ctx, tgt, mod, mod_c, vec, w_in, late_weights, rpb_rev, early_grads=None):
    s = x.shape[0]
    sa = s + CTX
    ts = _tile(s, (1024, 512, 256))
    ts2 = _tile(s, (2048, 1024, 512, 256))
    tsa = _tile(sa, (1088, 640, 256))
    tsa2 = _tile(sa, (2176, 640, 256))
    sh1, sc1, gt1, sh2, sc2, gt2 = (mod[i:i + 1] for i in range(6))
    csh1, csc1 = mod_c[0:1], mod_c[1:2]
    act = MXU_DTYPE

    tab = _bias_table(rpb_rev)
    h_all = _rmsmod_fwd(x, ctx, vec["g_norm1"], sc1, sh1, csc1, csh1)
    qkv = _mm(h_all, w_in, mode="nn", m=sa, n=3 * DA, k=D, tm=tsa2, tn=512, tk=D, out_dtype=MXU_DTYPE, name="mm_qkv")
    ag = _mm(h_all, w_in, mode="nn", m=s, n=2 * DC, k=D, tm=ts2, tn=512, tk=D, out_dtype=F32, name="mm_ag", b_off=(0, 3))
    ycat, lse = _attn_fwd(qkv, tab, s)
    u1 = _conf_conv_fwd(ag, vec["conv_w"], vec["conv_b"])
    ycat = _conf_ln_fwd(u1, vec["ln_g"], vec["ln_b"], ycat)
    w_out, w_up, w_down = late_weights(ycat) if callable(late_weights) else late_weights
    y = _mm(ycat, w_out, mode="nn", m=s, n=D, k=D, tm=ts2, tn=512, tk=D, out_dtype=F32, name="mm_out")
    x1, h2 = _resid_rmsmod_fwd(x, y, gt1, vec["g_norm2"], sc2, sh2)
    u = _mm(h2, w_up, mode="nn", m=s, n=2 * DFF, k=D, tm=ts2, tn=512, tk=D, out_dtype=act, name="mm_up")
    f = _ffn_act_fwd(u, vec["ffn_conv_w"], vec["ffn_conv_b"])
    z = _mm(f, w_down, mode="nn", m=s, n=D, k=DFF, tm=ts, tn=D, tk=DFF, out_dtype=F32, name="mm_down")
    dx2, dz, loss, dgt2, dgf = _final_fwd_bwd(x1, z, gt2, vec["g_final"], tgt)

    df = _mm(dz, w_down, mode="nt", m=s, n=DFF, k=D, tm=ts, tn=DFF, tk=D, out_dtype=act, name="mm_down_dx")
    d_w_down = _mm(f, dz, mode="tn", m=DFF, n=D, k=s, tm=DFF, tn=D, tk=ts, out_dtype=F32, name="mm_down_dw")
    dug, duv, dfw_g, dfw_v, dfb_g, dfb_v = _ffn_act_bwd(u, df, vec["ffn_conv_w"], vec["ffn_conv_b"])
    dh2 = _mm([dug, duv], w_up, mode="nt", m=s, n=D, k=2 * DFF, tm=ts, tn=D, tk=DFF, out_dtype=F32, name="mm_up_dx")
    d_w_up = _mm(h2, [dug, duv], mode="tn", m=D, n=2 * DFF, k=s, tm=D, tn=DFF // 2, tk=ts, out_dtype=F32, name="mm_up_dw")
    sc2_b = sc2 if early_grads is None else sc2 + early_grads(d_w_up, d_w_down)
    dsh2, dsc2, dg2, dx1, dy, dgt1 = _rmsmod_bwd(x1, dh2, vec["g_norm2"], sc2_b, name="rmsmod2_bwd", add=dx2, resid=(gt1, y))
    dycat = _mm(dy, w_out, mode="nt", m=s, n=D, k=D, tm=ts2, tn=512, tk=D, out_dtype=F32, name="mm_out_dx")
    d_w_out = _mm(ycat, dy, mode="tn", m=D, n=D, k=s, tm=D, tn=D, tk=ts, out_dtype=F32, name="mm_out_dw")
    du1, dln_g, dln_b = _conf_ln_bwd(dycat, u1, vec["ln_g"], vec["ln_b"])
    da, dg, dconv_w, dconv_b = _conf_conv_bwd(ag, du1, vec["conv_w"], sa)
    dq, dkv, tt = _attn_bwd(qkv, tab, lse, dycat, s)
    drpb_rev = _rpb_grad(tt)
    d_pieces = [dq, dkv, da, dg]
    dh = _mm(d_pieces, w_in, mode="nt", m=sa, n=D, k=NIN, tm=tsa2, tn=D, tk=512, out_dtype=F32, name="mm_in_dx")
    d_w_in = _mm(h_all, d_pieces, mode="tn", m=D, n=NIN, k=sa, tm=D, tn=512, tk=tsa, out_dtype=F32, name="mm_in_dw",
                 k_outer=True)
    dsh1, dsc1, dg1, grad_x = _rmsmod_bwd(x, dh, vec["g_norm1"], sc1, name="rmsmod1_bwd", add=dx1)
    dcsh1, dcsc1, dg1c = _rmsmod_bwd(ctx, dh, vec["g_norm1"], csc1, name="rmsmod1_ctx_bwd", dh_row_off=s // ROW_TILE)

    small = dict(
        dmod=[dsh1, dsc1, dgt1, dsh2, dsc2, dgt2], dmod_c=[dcsh1, dcsc1],
        g_norm1=[dg1, dg1c], g_norm2=dg2, g_final=dgf, conv_b=dconv_b, ln_g=dln_g, ln_b=dln_b, conv_w=dconv_w,
        ffn_conv_w=[dfw_g, dfw_v], ffn_conv_b=[dfb_g, dfb_v], rpb_rev=drpb_rev,
    )
    return loss, grad_x, d_w_in, d_w_out, d_w_up, d_w_down, small


N_CHIPS = 4
HBM = pl.BlockSpec(memory_space=pl.ANY)
BIG = {"w_in": ("col", (D, NIN)), "w_out": ("row", (D, D)), "w_up": ("col", (D, 2 * DFF)), "w_down": ("row", (DFF, D))}
BIG_NAMES = tuple(BIG)
LATE_NAMES = ("w_out", "w_up", "w_down")


def _shard_shape(name):
    kind, (r, c) = BIG[name]
    return (r, c // N_CHIPS) if kind == "col" else (r // N_CHIPS, c)


def _half_rows(name):
    return _shard_shape(name)[0] // 2


def _place():
    x, y, c = lax.axis_index("x"), lax.axis_index("y"), lax.axis_index("c")
    others = [(1 - x, y), (x, 1 - y), (1 - x, 1 - y)]
    return x, y, c, 2 * x + y, (x, y, 1 - c), others


def _whole_region(ref, name, chip, half):
    kind, _ = BIG[name]
    r, c = _shard_shape(name)
    if kind == "col":
        return ref.at[pl.ds(half * (r // 2), r // 2), pl.ds(chip * c, c)]
    return ref.at[pl.ds(chip * r + half * (r // 2), r // 2), :]


def _remote(src, dst, send_sem, recv_sem, to):
    return pltpu.make_async_remote_copy(src_ref=src, dst_ref=dst, send_sem=send_sem, recv_sem=recv_sem,
                                        device_id=to, device_id_type=MESH)


def _gather_small(v, name):
    m_per, n = v.shape

    def body(x_ref, out_ref, send_sems, recv_sems, local_sem):
        x, y, c, _, sibling, others = _place()
        me = (x, y, c)

        def rows(px, py, pc):
            return out_ref.at[pl.ds((4 * px + 2 * py + pc) * m_per, m_per), :]

        def copy(k, block, to, src=None):
            return _remote(rows(*block) if src is None else src, rows(*block), send_sems.at[k], recv_sems.at[k], to)

        mine = pltpu.make_async_copy(x_ref, rows(*me), local_sem)
        mine.start()
        first = [copy(0, me, sibling, src=x_ref)]
        first += [copy(1 + j, me, (*chip, c), src=x_ref) for j, chip in enumerate(others)]
        for cp in first:
            cp.start()
        passed = [copy(4 + j, (*chip, c), sibling) for j, chip in enumerate(others)]
        for j, chip in enumerate(others):
            copy(1 + j, (*chip, c), me).wait_recv()
            passed[j].start()
        copy(0, sibling, me).wait_recv()
        for j, chip in enumerate(others):
            copy(4 + j, (*chip, 1 - c), me).wait_recv()
        for cp in first + passed:
            cp.wait_send()
        mine.wait()

    return pl.pallas_call(
        body, name=name, out_shape=_sds((8 * m_per, n), v.dtype),
        in_specs=[pl.BlockSpec(memory_space=pltpu.VMEM)], out_specs=pl.BlockSpec(memory_space=pltpu.VMEM),
        scratch_shapes=[pltpu.SemaphoreType.DMA((7,)), pltpu.SemaphoreType.DMA((7,)), pltpu.SemaphoreType.DMA],
    )(v)


def _cast_into_whole(name, shard, chip):
    kind, whole = BIG[name]
    r, c = shard.shape
    if kind == "col":
        tr = 256
        o_spec = pl.BlockSpec((tr, c), lambda i, ch: (i, ch[0]))
    else:
        tr = _tile(r, (128, 352))
        o_spec = pl.BlockSpec((tr, c), lambda i, ch: (ch[0] * (r // tr) + i, 0))

    def body(ch_ref, x_ref, o_ref):
        del ch_ref
        o_ref[...] = x_ref[...].astype(o_ref.dtype)

    return _pallas(body, name="cast_" + name, prefetch=1, grid=(r // tr,),
                   in_specs=[pl.BlockSpec((tr, c), lambda i, ch: (i, 0))], out_specs=o_spec,
                   out_shape=_sds(whole, MXU_DTYPE), semantics=("parallel",))(chip, shard)


def _gather_weights(wholes, names, label):
    nw = len(names)

    def body(*refs):
        outs = refs[nw:2 * nw]
        send_sems, recv_sems = refs[2 * nw:]
        _, _, c, chip, sibling, others = _place()
        sends = []
        for w, name in enumerate(names):
            mine = _whole_region(outs[w], name, chip, c)
            for t, (ox, oy) in enumerate(others):
                cp = _remote(mine, mine, send_sems.at[w, t], recv_sems.at[w, t], (ox, oy, c))
                cp.start()
                sends.append(cp)
        for w, name in enumerate(names):
            for t, (ox, oy) in enumerate(others):
                got = _whole_region(outs[w], name, 2 * ox + oy, c)
                _remote(got, got, send_sems.at[w, t], recv_sems.at[w, t], (ox, oy, c)).wait_recv()
                cp = _remote(got, got, send_sems.at[w, 3 + t], recv_sems.at[w, 3 + t], sibling)
                cp.start()
                sends.append(cp)
        for w, name in enumerate(names):
            for t, (ox, oy) in enumerate(others):
                got = _whole_region(outs[w], name, 2 * ox + oy, 1 - c)
                _remote(got, got, send_sems.at[w, 3 + t], recv_sems.at[w, 3 + t], sibling).wait_recv()
        for cp in sends:
            cp.wait_send()

    return pl.pallas_call(
        body, name=label,
        out_shape=[_sds(a.shape, a.dtype) for a in wholes],
        in_specs=[HBM] * nw, out_specs=[HBM] * nw,
        input_output_aliases={i: i for i in range(nw)},
        scratch_shapes=[pltpu.SemaphoreType.DMA((nw, 6)), pltpu.SemaphoreType.DMA((nw, 6))],
    )(*wholes)


SEM = pl.BlockSpec(memory_space=pltpu.SEMAPHORE)
IN_HBM = pl.BlockSpec(memory_space=pltpu.HBM)
DATAFLOW = pltpu.SideEffectType.DATAFLOW_SIDE_EFFECTING


def _keep_in_hbm(a):
    return pltpu.with_memory_space_constraint(a, pltpu.HBM)


def _gather_start(wholes, names, after):
    nw = len(names)
    ns = 2 * 3 * nw

    def body(*refs):
        ins = refs[:nw]
        sems = refs[nw + 1:nw + 1 + ns]
        token = refs[2 * nw + ns + 1]
        _, _, c, chip, _, others = _place()
        for w, name in enumerate(names):
            mine = _whole_region(ins[w], name, chip, c)
            for t, (ox, oy) in enumerate(others):
                k = 2 * (3 * w + t)
                _remote(mine, mine, sems[k], sems[k + 1], (ox, oy, c)).start()
        token[...] = jnp.zeros_like(token)

    res = pl.pallas_call(
        body, name="gather_late_start",
        out_shape=(*[pltpu.SemaphoreType.DMA(())] * ns, *[pltpu.HBM(a.shape, a.dtype) for a in wholes], _sds((8, LANES), F32)),
        in_specs=[IN_HBM] * nw + [pl.BlockSpec(memory_space=pl.ANY)],
        out_specs=(*[SEM] * ns, *[IN_HBM] * nw, pl.BlockSpec(memory_space=pltpu.VMEM)),
        input_output_aliases={i: ns + i for i in range(nw)},
        compiler_params=pltpu.CompilerParams(has_side_effects=DATAFLOW),
    )(*[_keep_in_hbm(a) for a in wholes], after)
    return list(res[:ns]), list(res[ns:ns + nw]), res[ns + nw]


def _gather_wait(sems, wholes, names, after):
    nw = len(names)
    ns = len(sems)

    def body(*refs):
        ins = refs[:nw]
        sem_refs = refs[nw:nw + ns]
        _, _, c, chip, _, others = _place()
        for w, name in enumerate(names):
            mine = _whole_region(ins[w], name, chip, c)
            for t, (ox, oy) in enumerate(others):
                got = _whole_region(ins[w], name, 2 * ox + oy, c)
                k = 2 * (3 * w + t)
                cp = _remote(mine, got, sem_refs[k], sem_refs[k + 1], (ox, oy, c))
                cp.wait_send()
                cp.wait_recv()

    return pl.pallas_call(
        body, name="gather_late_wait",
        out_shape=tuple(pltpu.HBM(a.shape, a.dtype) for a in wholes),
        in_specs=[IN_HBM] * nw + [SEM] * ns + [pl.BlockSpec(memory_space=pl.ANY)], out_specs=tuple([IN_HBM] * nw),
        input_output_aliases={i: i for i in range(nw)},
        compiler_params=pltpu.CompilerParams(has_side_effects=DATAFLOW),
    )(*wholes, *sems, after)


def _forward_halves(wholes, names):
    nw = len(names)

    def body(*refs):
        outs = refs[nw:2 * nw]
        send_sems, recv_sems = refs[2 * nw:]
        _, _, c, _, sibling, others = _place()
        sends = []
        for w, name in enumerate(names):
            for t, (ox, oy) in enumerate(others):
                got = _whole_region(outs[w], name, 2 * ox + oy, c)
                cp = _remote(got, got, send_sems.at[w, t], recv_sems.at[w, t], sibling)
                cp.start()
                sends.append(cp)
        for w, name in enumerate(names):
            for t, (ox, oy) in enumerate(others):
                got = _whole_region(outs[w], name, 2 * ox + oy, 1 - c)
                _remote(got, got, send_sems.at[w, t], recv_sems.at[w, t], sibling).wait_recv()
        for cp in sends:
            cp.wait_send()

    return pl.pallas_call(
        body, name="gather_late_forward",
        out_shape=[_sds(a.shape, a.dtype) for a in wholes],
        in_specs=[HBM] * nw, out_specs=[HBM] * nw,
        input_output_aliases={i: i for i in range(nw)},
        scratch_shapes=[pltpu.SemaphoreType.DMA((nw, 3)), pltpu.SemaphoreType.DMA((nw, 3))],
    )(*wholes)


def _compact_shape(name, dtype):
    kind, (r, c) = BIG[name]
    return _sds((r // 2, c), dtype)


def _swap_halves(grads, names, label):
    nw = len(names)

    def body(*refs):
        ins, outs = refs[:nw], refs[nw:2 * nw]
        send_sems, recv_sems = refs[2 * nw:]
        _, _, c, _, sibling, _ = _place()
        copies = []
        for w, name in enumerate(names):
            kind, (r, _) = BIG[name]
            half = _half_rows(name)
            if kind == "col":
                parts = [(ins[w].at[pl.ds((1 - c) * half, half), :], outs[w])]
            else:
                parts = [(ins[w].at[pl.ds(jj * 2 * half + (1 - c) * half, half), :], outs[w].at[pl.ds(jj * half, half), :])
                         for jj in range(N_CHIPS)]
            for t, (src, dst) in enumerate(parts):
                cp = _remote(src, dst, send_sems.at[w, t], recv_sems.at[w, t], sibling)
                cp.start()
                copies.append(cp)
        for cp in copies:
            cp.wait()

    return pl.pallas_call(
        body, name=label,
        out_shape=[_compact_shape(n, F32) for n in names],
        in_specs=[HBM] * nw, out_specs=[HBM] * nw,
        scratch_shapes=[pltpu.SemaphoreType.DMA((nw, N_CHIPS)), pltpu.SemaphoreType.DMA((nw, N_CHIPS))],
    )(*grads)


def _add_halves(name, grad, got, core):
    kind, (r, c) = BIG[name]
    half = _half_rows(name)
    if kind == "col":
        t = 128
        grid = (half // t,)
        g_spec = pl.BlockSpec((t, c), lambda i, cr: (cr[0] * (half // t) + i, 0))
        o_spec = pl.BlockSpec((t, c), lambda i, cr: (i, 0))
    else:
        t = half
        grid = (N_CHIPS,)
        g_spec = pl.BlockSpec((t, c), lambda i, cr: (2 * i + cr[0], 0))
        o_spec = pl.BlockSpec((t, c), lambda i, cr: (i, 0))

    def body(c_ref, g_ref, b_ref, o_ref):
        del c_ref
        o_ref[...] = (g_ref[...] + b_ref[...]).astype(o_ref.dtype)

    return pl.pallas_call(
        body, name="grad_add_" + name,
        grid_spec=pltpu.PrefetchScalarGridSpec(num_scalar_prefetch=1, grid=grid, in_specs=[g_spec, o_spec], out_specs=o_spec),
        out_shape=_compact_shape(name, BF16),
        compiler_params=pltpu.CompilerParams(dimension_semantics=("parallel",), vmem_limit_bytes=VMEM_LIMIT),
    )(core, grad, got)


def _piece(ref, name, chip):
    kind, _ = BIG[name]
    r, c = _shard_shape(name)
    if kind == "col":
        return ref.at[:, pl.ds(chip * c, c)]
    return ref.at[pl.ds(chip * (r // 2), r // 2), :]


def _landing_shape(name):
    r, c = _shard_shape(name)
    return (N_CHIPS - 1, r // 2, c)


def _exchange_start(parts, names, label):
    nw = len(names)
    ns = 2 * 3 * nw

    def body(*refs):
        ins, lands = refs[:nw], refs[nw:2 * nw]
        sems = refs[2 * nw:2 * nw + ns]
        token = refs[4 * nw + ns]
        _, _, c, _, _, others = _place()
        for w, name in enumerate(names):
            for t, (ox, oy) in enumerate(others):
                k = 2 * (3 * w + t)
                _remote(_piece(ins[w], name, 2 * ox + oy), lands[w].at[t], sems[k], sems[k + 1], (ox, oy, c)).start()
        token[...] = jnp.zeros_like(token)

    lands = [_keep_in_hbm(lax.empty(_landing_shape(n), BF16)) for n in names]
    res = pl.pallas_call(
        body, name=label,
        out_shape=(*[pltpu.SemaphoreType.DMA(())] * ns, *[pltpu.HBM(a.shape, a.dtype) for a in parts],
                   *[pltpu.HBM(a.shape, a.dtype) for a in lands], _sds((8, LANES), F32)),
        in_specs=[IN_HBM] * (2 * nw),
        out_specs=(*[SEM] * ns, *[IN_HBM] * (2 * nw), pl.BlockSpec(memory_space=pltpu.VMEM)),
        input_output_aliases={i: ns + i for i in range(2 * nw)},
        compiler_params=pltpu.CompilerParams(has_side_effects=DATAFLOW),
    )(*[_keep_in_hbm(a) for a in parts], *lands)
    return list(res[:ns]), list(res[ns:ns + nw]), list(res[ns + nw:ns + 2 * nw]), res[ns + 2 * nw]


def _exchange_wait(sems, parts, lands, names, after, label):
    nw = len(names)
    ns = len(sems)

    def body(*refs):
        ins, land_refs = refs[:nw], refs[nw:2 * nw]
        sem_refs = refs[2 * nw:2 * nw + ns]
        _, _, c, _, _, others = _place()
        for w, name in enumerate(names):
            for t, (ox, oy) in enumerate(others):
                k = 2 * (3 * w + t)
                cp = _remote(_piece(ins[w], name, 2 * ox + oy), land_refs[w].at[t], sem_refs[k], sem_refs[k + 1], (ox, oy, c))
                cp.wait_send()
                cp.wait_recv()

    res = pl.pallas_call(
        body, name=label,
        out_shape=tuple(pltpu.HBM(a.shape, a.dtype) for a in (*parts, *lands)),
        in_specs=[IN_HBM] * (2 * nw) + [SEM] * ns + [pl.BlockSpec(memory_space=pl.ANY)],
        out_specs=tuple([IN_HBM] * (2 * nw)),
        input_output_aliases={i: i for i in range(2 * nw)},
        compiler_params=pltpu.CompilerParams(has_side_effects=DATAFLOW),
    )(*parts, *lands, *sems, after)
    return list(res[:nw]), list(res[nw:])


def _sum_chips(name, part, got, chip):
    kind, _ = BIG[name]
    _, r, c = got.shape
    t = _tile(r, (128, 352))
    if kind == "col":
        own = pl.BlockSpec((t, c), lambda i, ch: (i, ch[0]))
    else:
        own = pl.BlockSpec((t, c), lambda i, ch: (ch[0] * (r // t) + i, 0))

    def body(ch_ref, p_ref, g_ref, o_ref):
        del ch_ref
        acc = p_ref[...].astype(F32)
        for j in range(N_CHIPS - 1):
            acc = acc + g_ref[j].astype(F32)
        o_ref[...] = acc

    return _pallas(
        body, name="grad_sum_" + name, prefetch=1, grid=(r // t,),
        in_specs=[own, pl.BlockSpec((N_CHIPS - 1, t, c), lambda i, ch: (0, i, 0))],
        out_specs=pl.BlockSpec((t, c), lambda i, ch: (i, 0)),
        out_shape=_sds((r, c), F32), semantics=("parallel",),
    )(chip, part, got)


def _send_halves(sums, label):
    nw = len(sums)

    def body(*refs):
        ins, outs = refs[:nw], refs[nw:2 * nw]
        send_sems, recv_sems = refs[2 * nw:]
        _, _, _, _, sibling, _ = _place()
        copies = [_remote(ins[w], outs[w], send_sems.at[w], recv_sems.at[w], sibling) for w in range(nw)]
        for cp in copies:
            cp.start()
        for cp in copies:
            cp.wait()

    return pl.pallas_call(
        body, name=label,
        out_shape=[_sds(a.shape, a.dtype) for a in sums],
        in_specs=[HBM] * nw, out_specs=[HBM] * nw,
        scratch_shapes=[pltpu.SemaphoreType.DMA((nw,)), pltpu.SemaphoreType.DMA((nw,))],
    )(*sums)


EARLY_GRADS = ("w_up", "w_down")
LAST_GRADS = ("w_in", "w_out")


def _chip_partials(grads, names, core, label):
    got = _swap_halves(grads, names, label)
    return [_add_halves(n, grads[i], got[i], core) for i, n in enumerate(names)]


def _reduce_start(grads, names, core, tag):
    parts = _chip_partials(grads, names, core, "grad_swap_" + tag)
    return _exchange_start(parts, names, "grad_exchange_start_" + tag)


def _reduce_finish(started, names, after, chip, tag):
    sems, parts, lands, _ = started
    parts, lands = _exchange_wait(sems, parts, lands, names, after, "grad_exchange_wait_" + tag)
    return [_sum_chips(n, parts[i], lands[i], chip) for i, n in enumerate(names)]


HI = lax.Precision.HIGHEST
MOD_COLS = 6 * D // N_CHIPS
COND_ROWS = 16


def _silu(v):
    return v * _sigmoid(v)


GATHER_ROWS = 48
FFW_COLS = 2 * DFF // N_CHIPS
CONV_COLS = DC // N_CHIPS


def _pack_cond(c, ffn_w, conv_w):
    def body(c_ref, f_ref, w_ref, o_ref):
        o_ref[...] = jnp.zeros_like(o_ref)
        o_ref[0:1, 0:D] = c_ref[...]
        o_ref[8:11, :] = f_ref[...]
        o_ref[16:16 + CW, 0:CONV_COLS] = w_ref[...]

    return _pallas(body, name="pack_cond", out_shape=_sds((GATHER_ROWS, FFW_COLS), F32))(c, ffn_w, conv_w)


def _unpack_cond(got, c_ctx):
    def body(g_ref, c_ref, cond_ref, f_ref, w_ref):
        cond_ref[...] = jnp.zeros_like(cond_ref)
        for d in range(8):
            cond_ref[d:d + 1, :] = g_ref[d * GATHER_ROWS:d * GATHER_ROWS + 1, 0:D]
        cond_ref[8:9, :] = c_ref[...]
        for j in range(N_CHIPS):
            r0 = 2 * j * GATHER_ROWS
            f_ref[:, j * FFW_COLS:(j + 1) * FFW_COLS] = g_ref[r0 + 8:r0 + 11, :]
            w_ref[:, j * CONV_COLS:(j + 1) * CONV_COLS] = g_ref[r0 + 16:r0 + 16 + CW, 0:CONV_COLS]

    return _pallas(body, name="unpack_cond",
                   out_shape=[_sds((COND_ROWS, D), F32), _sds((3, 2 * DFF), F32), _sds((CW, DC), F32)])(got, c_ctx)


def _chip_cols(rows, width):
    return pl.BlockSpec((rows, width), lambda i, ch: (0, ch[0]))


def _whole(shape):
    return pl.BlockSpec(shape, lambda i, ch: (0,) * len(shape))


def _mod_shard(cond, w_mod, b_mod, chip):
    def body(ch_ref, c_ref, w_ref, b_ref, o_ref):
        del ch_ref
        o_ref[...] = jnp.dot(_silu(c_ref[...]), w_ref[...], preferred_element_type=F32, precision=HI) + b_ref[...]

    return _pallas(body, name="mod_fwd", prefetch=1, grid=(1,),
                   in_specs=[_whole((COND_ROWS, D)), _whole((D, MOD_COLS)), _chip_cols(1, MOD_COLS)],
                   out_specs=_whole((COND_ROWS, MOD_COLS)),
                   out_shape=_sds((COND_ROWS, MOD_COLS), F32))(chip, cond, w_mod, b_mod)


def _unpack_mod(mods, dev):
    def body(dev_ref, m_ref, me_ref, c_ref):
        rowi = lax.broadcasted_iota(jnp.int32, (COND_ROWS, MOD_COLS), 0)
        mine, ctx = [], []
        for j in range(N_CHIPS):
            blk = m_ref[2 * j * COND_ROWS:(2 * j + 1) * COND_ROWS, :]
            mine.append(jnp.sum(jnp.where(rowi == dev_ref[0], blk, 0.0), axis=0, keepdims=True))
            ctx.append(blk[8:9, :])
        mine = jnp.concatenate(mine, axis=1)
        ctx = jnp.concatenate(ctx, axis=1)
        for k in range(6):
            me_ref[k:k + 1, :] = mine[:, k * D:(k + 1) * D]
        for k in range(2):
            c_ref[k:k + 1, :] = ctx[:, k * D:(k + 1) * D]

    return _pallas(body, name="unpack_mod", prefetch=1, grid=(1,),
                   in_specs=[_whole(mods.shape)], out_specs=[_whole((6, D)), _whole((2, D))],
                   out_shape=[_sds((6, D), F32), _sds((2, D), F32)])(dev, mods)


def _mod_weight_grad(cond, dmod_all, chip):
    def body(ch_ref, c_ref, d_ref, o_ref):
        del ch_ref
        o_ref[...] = lax.dot_general(_silu(c_ref[...]), d_ref[...], _TN, preferred_element_type=F32, precision=HI)

    return _pallas(body, name="mod_weight_grad", prefetch=1, grid=(1,),
                   in_specs=[_whole((COND_ROWS, D)), _chip_cols(COND_ROWS, MOD_COLS)], out_specs=_whole((D, MOD_COLS)),
                   out_shape=_sds((D, MOD_COLS), F32))(chip, cond, dmod_all)


def _cond_grad_partial(dmod_all, w_mod, chip):
    def body(ch_ref, d_ref, w_ref, o_ref):
        del ch_ref
        o_ref[...] = lax.dot_general(d_ref[...], w_ref[...], (((1,), (1,)), ((), ())), preferred_element_type=F32, precision=HI)

    return _pallas(body, name="cond_grad_partial", prefetch=1, grid=(1,),
                   in_specs=[pl.BlockSpec((8, MOD_COLS), lambda i, ch: (1, ch[0])), _whole((D, MOD_COLS))],
                   out_specs=_whole((8, D)), out_shape=_sds((8, D), F32))(chip, dmod_all, w_mod)


def _adam_math(w, g, m, v):
    nm = ADAM_B1 * m + (1.0 - ADAM_B1) * g
    nv = ADAM_B2 * v + (1.0 - ADAM_B2) * (g * g)
    c1 = 1.0 - ADAM_B1 ** ADAM_STEP
    c2 = 1.0 - ADAM_B2 ** ADAM_STEP
    return -ADAM_LR * ((nm / c1) / (jnp.sqrt(nv / c2) + ADAM_EPS) + ADAM_WD * w), nm, nv


def _cond_update(parts, c_ctx, m, v):
    def body(p_ref, c_ref, m_ref, v_ref, g_ref, d_ref, nm_ref, nv_ref):
        tot = p_ref[0:1, :]
        for j in range(1, N_CHIPS):
            tot = tot + p_ref[16 * j:16 * j + 1, :]
        cv = c_ref[...]
        sg = _sigmoid(cv)
        g = tot * (sg * (1.0 + cv * (1.0 - sg)))
        g_ref[...] = g
        d_ref[...], nm_ref[...], nv_ref[...] = _adam_math(cv, g, m_ref[...], v_ref[...])

    return _pallas(body, name="cond_update", out_shape=[_sds((1, D), F32)] * 4)(parts, c_ctx, m, v)


def _adamw(w, g, m, v, name):
    r, c = w.shape
    t = _tile(r, (128,)) if r % 128 == 0 and r > 128 else r

    def body(w_ref, g_ref, m_ref, v_ref, d_ref, nm_ref, nv_ref):
        d_ref[...], nm_ref[...], nv_ref[...] = _adam_math(w_ref[...], g_ref[...], m_ref[...], v_ref[...])

    blk = pl.BlockSpec((t, c), lambda i: (i, 0))
    return _pallas(body, name=name, grid=(r // t,), in_specs=[blk] * 4, out_specs=[blk] * 3,
                   out_shape=[_sds((r, c), F32)] * 3, semantics=("parallel",))(w, g, m, v)


def _adamw_cols(w, g_all, m, v, chip, name):
    r, c = w.shape

    def body(ch_ref, w_ref, g_ref, m_ref, v_ref, go_ref, d_ref, nm_ref, nv_ref):
        del ch_ref
        g = g_ref[...]
        go_ref[...] = g
        d_ref[...], nm_ref[...], nv_ref[...] = _adam_math(w_ref[...], g, m_ref[...], v_ref[...])

    return _pallas(body, name=name, prefetch=1, grid=(1,),
                   in_specs=[_whole((r, c)), _chip_cols(r, c), _whole((r, c)), _whole((r, c))],
                   out_specs=[_whole((r, c))] * 4, out_shape=[_sds((r, c), F32)] * 4)(chip, w, g_all, m, v)


def _adamw_halves(name, w, own, other, m, v, core, after):
    r, c = w.shape
    half = r // 2
    t = _tile(half, (128, 352))
    nh = half // t

    def pick(mine):
        def index(i, cr):
            first = cr[0] if mine else 1 - cr[0]
            return (jnp.clip(i - first * nh, 0, nh - 1), 0)
        return pl.BlockSpec((t, c), index)

    def body(c_ref, w_ref, own_ref, oth_ref, m_ref, v_ref, after_ref, g_ref, d_ref, nm_ref, nv_ref):
        del after_ref
        g = jnp.where(pl.program_id(0) // nh == c_ref[0], own_ref[...], oth_ref[...])
        g_ref[...] = g
        d_ref[...], nm_ref[...], nv_ref[...] = _adam_math(w_ref[...], g, m_ref[...], v_ref[...])

    blk = pl.BlockSpec((t, c), lambda i, cr: (i, 0))
    return _pallas(body, name="adamw_" + name, prefetch=1, grid=(2 * nh,),
                   in_specs=[blk, pick(True), pick(False), blk, blk, pl.BlockSpec(memory_space=pl.ANY)], out_specs=[blk] * 4,
                   out_shape=[_sds((r, c), F32)] * 4, semantics=("parallel",))(core, w, own, other, m, v, after)


WEIGHTS = ("c_ctx", "w_mod", "b_mod", "g_norm1", "w_in", "rpb", "conv_w", "conv_b", "ln_g", "ln_b", "w_out", "g_norm2",
           "w_up", "ffn_conv_w", "ffn_conv_b", "w_down", "g_final")
PACK = (("dmod", 6 * D), ("dmod_c", 2 * D), ("g_norm1", D), ("g_norm1_ctx", D), ("g_norm2", D), ("g_final", D),
        ("conv_b", DC), ("ln_g", DC), ("ln_b", DC), ("ffn_conv_b", 2 * DFF), ("ffn_conv_w", 3 * 2 * DFF),
        ("conv_w", CW * DC), ("rpb_rev", NH * 16 * LANES), ("loss", LANES))
PACK_OFF = {}
_o = 0
for _n, _w in PACK:
    PACK_OFF[_n] = (_o, _w)
    _o += _w
PACK_N = -(-_o // (8 * LANES)) * (8 * LANES)
VECTORS = {"b_mod": (6 * D, ("dmod", "dmod_c")), "g_norm1": (D, ("g_norm1", "g_norm1_ctx")), "conv_b": (DC, ("conv_b",)),
           "ln_g": (DC, ("ln_g",)), "ln_b": (DC, ("ln_b",)), "g_norm2": (D, ("g_norm2",)),
           "ffn_conv_b": (2 * DFF, ("ffn_conv_b",)), "g_final": (D, ("g_final",))}
RPB_ROWS = NH * (2 * NA_ROWS - 1)
RPB_COLS = 4 * NA_ROWS - 1


def _pack_small(parts):
    arrs, places = [], []
    for name, _ in PACK:
        off, width = PACK_OFF[name]
        group = parts[name]
        rows = group[0].shape[0]
        row_w = sum(a.shape[1] for a in group)
        assert rows * row_w == width, (name, rows, row_w, width)
        col = 0
        for a in group:
            arrs.append(a)
            places.append([off + k * row_w + col for k in range(rows)])
            col += a.shape[1]

    def body(*refs):
        o_ref = refs[-1]
        o_ref[:, _o:PACK_N] = jnp.zeros((1, PACK_N - _o), F32)
        for ref, offs in zip(refs, places):
            n = ref.shape[1]
            for k, off in enumerate(offs):
                o_ref[:, off:off + n] = ref[k:k + 1, :]

    return _pallas(body, name="pack_small_grads", out_shape=_sds((1, PACK_N), F32))(*arrs)


def _small_update(packs, w, m, v):
    names = list(VECTORS)

    def body(*refs):
        it = iter(refs)
        p_ref = next(it)
        wmv = {n: (next(it), next(it), next(it)) for n in names}
        outs = {n: (next(it), next(it), next(it), next(it)) for n in names}
        dmod_ref, cw_ref, fw_ref, rpb_ref, loss_ref = next(it), next(it), next(it), next(it), next(it)

        def total(name):
            off, width = PACK_OFF[name]
            acc = p_ref[0:1, off:off + width]
            for d in range(1, 8):
                acc = acc + p_ref[d:d + 1, off:off + width]
            return acc

        for n in names:
            width, segs = VECTORS[n]
            g = total(segs[0])
            if len(segs) > 1:
                extra = total(segs[1])
                ew = extra.shape[1]
                g = g + extra if ew == width else jnp.concatenate([g[:, :ew] + extra, g[:, ew:]], axis=1)
            w_ref, m_ref, v_ref = wmv[n]
            g_ref, d_ref, nm_ref, nv_ref = outs[n]
            g_ref[...] = g
            d_ref[...], nm_ref[...], nv_ref[...] = _adam_math(w_ref[...], g, m_ref[...], v_ref[...])

        o_dmod = PACK_OFF["dmod"][0]
        dmod_ref[...] = jnp.zeros_like(dmod_ref)
        dmod_ref[0:8, :] = p_ref[:, o_dmod:o_dmod + 6 * D]
        dmod_ref[8:9, 0:2 * D] = total("dmod_c")
        for ref, name, rows in ((cw_ref, "conv_w", CW), (fw_ref, "ffn_conv_w", 3), (rpb_ref, "rpb_rev", NH * 16)):
            flat = total(name)
            n = ref.shape[1]
            for k in range(rows):
                ref[k:k + 1, :] = flat[:, k * n:(k + 1) * n]
        loss_ref[...] = total("loss")

    ins = [packs] + [a[n] for n in names for a in (w, m, v)]
    out_shape = [_sds((1, VECTORS[n][0]), F32) for n in names for _ in range(4)]
    out_shape += [_sds((COND_ROWS, 6 * D), F32), _sds((CW, DC), F32), _sds((3, 2 * DFF), F32), _sds((NH * 16, LANES), F32),
                  _sds((1, LANES), F32)]
    res = _pallas(body, name="small_update", out_shape=out_shape)(*ins)
    per = {n: tuple(res[4 * i:4 * i + 4]) for i, n in enumerate(names)}
    return (per, *res[4 * len(names):])


def _rpb_update(rev, w, m, v):
    def body(r_ref, w_ref, m_ref, v_ref, g_ref, d_ref, nm_ref, nv_ref):
        li = lax.broadcasted_iota(jnp.int32, (LANES, LANES), 0)
        co = lax.broadcasted_iota(jnp.int32, (LANES, LANES), 1)
        lane_of_co0 = GW - 1 + RPB_COLS // 2
        unflip = jnp.where((li == lane_of_co0 - co) & (co < RPB_COLS), 1.0, 0.0).astype(F32)
        g_all = jnp.dot(r_ref[...], unflip, preferred_element_type=F32, precision=HI)
        nr = 2 * NA_ROWS - 1
        for h in range(NH):
            rows = slice(h * nr, (h + 1) * nr)
            g = g_all[h * 16:h * 16 + nr, 0:RPB_COLS]
            g_ref[rows, :] = g
            d_ref[rows, :], nm_ref[rows, :], nv_ref[rows, :] = _adam_math(w_ref[rows, :], g, m_ref[rows, :], v_ref[rows, :])

    return _pallas(body, name="rpb_update", out_shape=[_sds((RPB_ROWS, RPB_COLS), F32)] * 4)(rev, w, m, v)


def kernel(x, c, ctx, c_ctx, w_mod, b_mod, g_norm1, w_in, rpb, conv_w, conv_b, ln_g, ln_b, w_out, g_norm2, w_up, ffn_conv_w, ffn_conv_b, w_down, g_final, loss_target, m_c_ctx, m_w_mod, m_b_mod, m_g_norm1, m_w_in, m_rpb, m_conv_w, m_conv_b, m_ln_g, m_ln_b, m_w_out, m_g_norm2, m_w_up, m_ffn_conv_w, m_ffn_conv_b, m_w_down, m_g_final, v_c_ctx, v_w_mod, v_b_mod, v_g_norm1, v_w_in, v_rpb, v_conv_w, v_conv_b, v_ln_g, v_ln_b, v_w_out, v_g_norm2, v_w_up, v_ffn_conv_w, v_ffn_conv_b, v_w_down, v_g_final):
    w = dict(c_ctx=c_ctx, w_mod=w_mod, b_mod=b_mod, g_norm1=g_norm1, w_in=w_in, rpb=rpb, conv_w=conv_w, conv_b=conv_b,
             ln_g=ln_g, ln_b=ln_b, w_out=w_out, g_norm2=g_norm2, w_up=w_up, ffn_conv_w=ffn_conv_w, ffn_conv_b=ffn_conv_b,
             w_down=w_down, g_final=g_final)
    mom = dict(c_ctx=m_c_ctx, w_mod=m_w_mod, b_mod=m_b_mod, g_norm1=m_g_norm1, w_in=m_w_in, rpb=m_rpb, conv_w=m_conv_w,
               conv_b=m_conv_b, ln_g=m_ln_g, ln_b=m_ln_b, w_out=m_w_out, g_norm2=m_g_norm2, w_up=m_w_up,
               ffn_conv_w=m_ffn_conv_w, ffn_conv_b=m_ffn_conv_b, w_down=m_w_down, g_final=m_g_final)
    var = dict(c_ctx=v_c_ctx, w_mod=v_w_mod, b_mod=v_b_mod, g_norm1=v_g_norm1, w_in=v_w_in, rpb=v_rpb, conv_w=v_conv_w,
               conv_b=v_conv_b, ln_g=v_ln_g, ln_b=v_ln_b, w_out=v_w_out, g_norm2=v_g_norm2, w_up=v_w_up,
               ffn_conv_w=v_ffn_conv_w, ffn_conv_b=v_ffn_conv_b, w_down=v_w_down, g_final=v_g_final)
    xi, yi, ci = lax.axis_index("x"), lax.axis_index("y"), lax.axis_index("c")
    dev = (4 * xi + 2 * yi + ci).astype(jnp.int32).reshape(1)
    chip = (2 * xi + yi).astype(jnp.int32).reshape(1)
    core = ci.astype(jnp.int32).reshape(1)
    c_ctx2 = c_ctx.reshape(1, D)
    g_final2 = g_final.reshape(1, D)
    mom["g_final"], var["g_final"] = m_g_final.reshape(1, D), v_g_final.reshape(1, D)

    got = _gather_small(_pack_cond(c, ffn_conv_w[0], conv_w[0]), "gather_cond")
    cond, ffn_w_all, conv_w_all = _unpack_cond(got, c_ctx2)

    mods = _gather_small(_mod_shard(cond, w_mod[0], b_mod, chip), "gather_mod")
    mod_me, mod_c = _unpack_mod(mods, dev)

    shards = {n: _cast_into_whole(n, w[n][0], chip) for n in BIG_NAMES}
    (w_in_all,) = _gather_weights([shards["w_in"]], ("w_in",), "gather_w_in")
    sems, late, token = _gather_start([shards[n] for n in LATE_NAMES], LATE_NAMES, after=w_in_all)
    mod_me = mod_me + token[0:1, 0:1]

    def late_weights(after):
        arrived = _gather_wait(sems, late, LATE_NAMES, after)
        return _forward_halves(list(arrived), LATE_NAMES)

    rpb_rev = jnp.pad(rpb[0][:, :, ::-1], ((0, 0), (0, 1), (48, LANES - 48 - RPB_COLS))).reshape(NH * 16, LANES)
    vec = dict(g_norm1=g_norm1, g_norm2=g_norm2, g_final=g_final2, conv_w=conv_w_all, conv_b=conv_b, ln_g=ln_g, ln_b=ln_b,
               ffn_conv_w=ffn_w_all, ffn_conv_b=ffn_conv_b)
    started = []

    def early_grads(d_up, d_down):
        started.append(_reduce_start([d_up, d_down], EARLY_GRADS, core, "early"))
        return started[0][3][0:1, 0:1]

    loss_p, grad_x, d_in, d_out, d_up, d_down, small = _local_step(
        x[0], ctx[0], loss_target[0], mod_me, mod_c, vec, w_in_all, late_weights, rpb_rev, early_grads)

    out = {}
    early_own = _reduce_finish(started[0], EARLY_GRADS, grad_x, chip, "early")
    last_started = _reduce_start([d_in, d_out], LAST_GRADS, core, "last")
    early_other = _send_halves(early_own, "grad_send_early")
    for i, n in enumerate(EARLY_GRADS):
        out[n] = _adamw_halves(n, w[n][0], early_own[i], early_other[i], mom[n][0], var[n][0], core, last_started[3])

    parts = dict(dmod=small["dmod"], dmod_c=small["dmod_c"], g_norm1=[small["g_norm1"][0]], g_norm1_ctx=[small["g_norm1"][1]],
                 g_norm2=[small["g_norm2"]], g_final=[small["g_final"]], conv_b=[small["conv_b"]], ln_g=[small["ln_g"]],
                 ln_b=[small["ln_b"]], ffn_conv_b=small["ffn_conv_b"], ffn_conv_w=small["ffn_conv_w"],
                 conv_w=[small["conv_w"]], rpb_rev=[small["rpb_rev"]], loss=[loss_p])
    pack = _pack_small(parts).reshape(8, PACK_N // 8)
    packs = _gather_small(pack, "gather_small_grads").reshape(8, PACK_N)
    w2 = dict(w, g_final=g_final2)
    per, dmod_all, g_conv_w_all, g_ffn_w_all, g_rpb_rev, loss_row = _small_update(packs, w2, mom, var)

    out.update(per)
    out["c_ctx"] = _cond_update(
        _gather_small(_cond_grad_partial(dmod_all, w_mod[0], chip), "gather_cond_grad"),
        c_ctx2, m_c_ctx.reshape(1, D), v_c_ctx.reshape(1, D))
    g_w_mod = _mod_weight_grad(cond, dmod_all, chip)
    out["w_mod"] = (g_w_mod, *_adamw(w_mod[0], g_w_mod, m_w_mod[0], v_w_mod[0], "adamw_w_mod"))
    last_own = _reduce_finish(last_started, LAST_GRADS, out["w_mod"][1], chip, "last")
    last_other = _send_halves(last_own, "grad_send_last")
    for i, n in enumerate(LAST_GRADS):
        out[n] = _adamw_halves(n, w[n][0], last_own[i], last_other[i], mom[n][0], var[n][0], core, last_other[i])
    out["conv_w"] = _adamw_cols(conv_w[0], g_conv_w_all, m_conv_w[0], v_conv_w[0], chip, "adamw_conv_w")
    out["ffn_conv_w"] = _adamw_cols(ffn_conv_w[0], g_ffn_w_all, m_ffn_conv_w[0], v_ffn_conv_w[0], chip, "adamw_ffn_conv_w")
    flat = lambda a: a.reshape(RPB_ROWS, RPB_COLS)
    out["rpb"] = _rpb_update(g_rpb_rev, flat(rpb), flat(m_rpb), flat(v_rpb))

    res = [[out[n][k].reshape(w[n].shape) for n in WEIGHTS] for k in range(4)]
    return (loss_row[0, 0], grad_x[None], *res[0], *res[1], *res[2], *res[3])
```

```python
import functools

import jax
import jax.numpy as jnp
from jax import lax
from jax.experimental import pallas as pl
from jax.experimental.pallas import tpu as pltpu

F32 = jnp.float32
BF16 = jnp.bfloat16
MXU_DTYPE = jnp.bfloat16

D = 1024
CTX = 256
GW = 64
DA = 512
NH = 8
HD = 64
DC = 512
CW = 31
DFF = 2816
NIN = 3 * DA + 2 * DC
EPS = 1e-6
SCALE = HD ** -0.5
NEG = -1e30
NA_ROWS = 8
PAIR_ROWS = NA_ROWS + 1
TAB_BLOCKS = 17
LANES = 128
VMEM_LIMIT = 56 * 1024 * 1024

ADAM_LR = 0.001
ADAM_B1 = 0.9
ADAM_B2 = 0.999
ADAM_EPS = 1e-08
ADAM_WD = 0.01
ADAM_STEP = 10

MESH = pl.DeviceIdType.MESH


def _pallas(body, *, name, semantics=None, vmem=VMEM_LIMIT, prefetch=0, **kw):
    params = dict(vmem_limit_bytes=vmem)
    if semantics is not None:
        params["dimension_semantics"] = semantics
    if prefetch:
        kw["grid_spec"] = pltpu.PrefetchScalarGridSpec(
            num_scalar_prefetch=prefetch, grid=kw.pop("grid"), in_specs=kw.pop("in_specs"), out_specs=kw.pop("out_specs"),
            scratch_shapes=kw.pop("scratch_shapes", ()))
    return pl.pallas_call(body, name=name, compiler_params=pltpu.CompilerParams(**params), **kw)


def _sds(shape, dtype):
    return jax.ShapeDtypeStruct(shape, dtype)


def _vec_spec(n):
    return pl.BlockSpec((1, n), lambda *_: (0, 0))


def _colsum8(x):
    t, n = x.shape
    return jnp.sum(x.reshape(t // 8, 8, n), axis=0)


def _sigmoid(x):
    return 0.5 * jnp.tanh(0.5 * x) + 0.5


def _pieces(arrs, tile):
    lo, out = 0, []
    for a in arrs:
        nt = a.shape[1] // tile
        assert nt * tile == a.shape[1], (a.shape, tile)
        out.append((lo, nt))
        lo += nt
    return out


def _mm(a, b, *, mode, m, n, k, tm, tn, tk, out_dtype, name, a_off=(0, 0), b_off=(0, 0), k_outer=False,
        out_total=None, o_off=(0, 0), into=None):
    a_list = list(a) if isinstance(a, (list, tuple)) else [a]
    b_list = list(b) if isinstance(b, (list, tuple)) else [b]
    assert m % tm == 0 and n % tn == 0 and k % tk == 0, (name, m, n, k, tm, tn, tk)
    gi, gj, nk = m // tm, n // tn, k // tk
    a_tile = tm if mode == "tn" else tk
    a_pc = _pieces(a_list, a_tile) if len(a_list) > 1 else [(0, 1 << 30)]
    if mode == "nt":
        assert len(b_list) == 1
    b_pc = _pieces(b_list, tn) if len(b_list) > 1 else [(0, 1 << 30)]
    dims = {"nn": (((1,), (0,)), ((), ())), "nt": (((1,), (1,)), ((), ())), "tn": (((0,), (0,)), ((), ()))}[mode]
    k_outer = k_outer and nk > 1

    def ijk(fn):
        return (lambda i, kk, j: fn(i, j, kk)) if k_outer else fn

    def a_spec(lo, cnt):
        def loc(idx):
            return idx + a_off[1] if len(a_list) == 1 else jnp.clip(idx - lo, 0, cnt - 1)
        if mode == "tn":
            return pl.BlockSpec((tk, tm), ijk(lambda i, j, kk: (kk + a_off[0], loc(i))))
        return pl.BlockSpec((tm, tk), ijk(lambda i, j, kk: (i + a_off[0], loc(kk))))

    def b_spec(lo, cnt):
        def loc(idx):
            return idx + b_off[1] if len(b_list) == 1 else jnp.clip(idx - lo, 0, cnt - 1)
        if mode == "nt":
            return pl.BlockSpec((tn, tk), ijk(lambda i, j, kk: (j + b_off[0], kk + b_off[1])))
        return pl.BlockSpec((tk, tn), ijk(lambda i, j, kk: (kk + b_off[0], loc(j))))

    na, nb = len(a_list), len(b_list)
    in_place = nk > 1 and out_dtype == F32 and not k_outer

    n_in = na + nb + (into is not None)

    def body(*refs):
        a_refs, b_refs, o_ref = refs[:na], refs[na:na + nb], refs[n_in]
        if k_outer:
            i, kk, j = pl.program_id(0), pl.program_id(1), pl.program_id(2)
            acc = refs[n_in + 1].at[j]
        else:
            i, j, kk = pl.program_id(0), pl.program_id(1), pl.program_id(2)
            acc = o_ref if in_place else (refs[n_in + 1] if nk > 1 else None)
        a_idx = i if mode == "tn" else kk

        def step(ar, br):
            p = lax.dot_general(ar[...].astype(MXU_DTYPE), br[...].astype(MXU_DTYPE), dims,
                                preferred_element_type=F32)
            if nk == 1:
                o_ref[...] = p.astype(out_dtype)
                return

            @pl.when(kk == 0)
            def _():
                acc[...] = p

            @pl.when(kk > 0)
            def _():
                acc[...] += p

            if not in_place:
                @pl.when(kk == nk - 1)
                def _():
                    o_ref[...] = acc[...].astype(out_dtype)

        for pa, (alo, acnt) in enumerate(a_pc):
            for pb, (blo, bcnt) in enumerate(b_pc):
                if na == 1 and nb == 1:
                    step(a_refs[0], b_refs[0])
                else:
                    cond = (a_idx >= alo) & (a_idx < alo + acnt) & (j >= blo) & (j < blo + bcnt)
                    pl.when(cond)(functools.partial(step, a_refs[pa], b_refs[pb]))

    if k_outer:
        grid = (gi, nk, gj)
        o_spec = pl.BlockSpec((tm, tn), lambda i, kk, j: (i + o_off[0], jnp.where(kk == nk - 1, j, 0) + o_off[1]))
        scratch = [pltpu.VMEM((gj, tm, tn), F32)]
        semantics = ("parallel", "arbitrary", "arbitrary")
    else:
        grid = (gi, gj, nk)
        o_spec = pl.BlockSpec((tm, tn), lambda i, j, kk: (i + o_off[0], j + o_off[1]))
        scratch = [pltpu.VMEM((tm, tn), F32)] if nk > 1 and not in_place else []
        semantics = ("parallel", "parallel", "arbitrary")
    ins = [*a_list, *b_list]
    in_specs = [a_spec(*p) for p in a_pc] + [b_spec(*p) for p in b_pc]
    extra = {}
    if into is not None:
        extra["input_output_aliases"] = {len(ins): 0}
        ins.append(into)
        in_specs.append(pl.BlockSpec(memory_space=pl.ANY))
    return _pallas(
        body, name=name, grid=grid, in_specs=in_specs,
        out_specs=o_spec, out_shape=_sds(out_total or (m, n), out_dtype), scratch_shapes=scratch, semantics=semantics,
        **extra,
    )(*ins)


ROW_TILE = 256


def _rmsmod_fwd(x, ctx, g, sc, sh, csc, csh):
    s = x.shape[0]
    nt = s // ROW_TILE
    assert ctx.shape[0] == ROW_TILE

    def body(x_ref, c_ref, g_ref, sc_ref, sh_ref, csc_ref, csh_ref, o_ref):
        is_ctx = pl.program_id(0) == nt
        xv = jnp.where(is_ctx, c_ref[...], x_ref[...])
        scv = jnp.where(is_ctx, csc_ref[...], sc_ref[...])
        shv = jnp.where(is_ctx, csh_ref[...], sh_ref[...])
        r = lax.rsqrt(jnp.mean(xv * xv, axis=-1, keepdims=True) + EPS)
        y = xv * r * g_ref[...]
        o_ref[...] = (y * (1.0 + scv) + shv).astype(o_ref.dtype)

    return _pallas(
        body, name="rmsmod1_fwd", grid=(nt + 1,),
        in_specs=[pl.BlockSpec((ROW_TILE, D), lambda i: (jnp.minimum(i, nt - 1), 0)),
                  pl.BlockSpec((ROW_TILE, D), lambda i: (0, 0))] + [_vec_spec(D)] * 5,
        out_specs=pl.BlockSpec((ROW_TILE, D), lambda i: (i, 0)),
        out_shape=_sds((s + CTX, D), MXU_DTYPE),
        semantics=("arbitrary",),
    )(x, ctx, g, sc, sh, csc, csh)


def _resid_rmsmod_fwd(x, y, gt, g, sc, sh):
    s = x.shape[0]

    def body(x_ref, y_ref, gt_ref, g_ref, sc_ref, sh_ref, x1_ref, h_ref):
        x1 = x_ref[...] + gt_ref[...] * y_ref[...]
        x1_ref[...] = x1
        r = lax.rsqrt(jnp.mean(x1 * x1, axis=-1, keepdims=True) + EPS)
        h_ref[...] = ((x1 * r * g_ref[...]) * (1.0 + sc_ref[...]) + sh_ref[...]).astype(h_ref.dtype)

    row = pl.BlockSpec((ROW_TILE, D), lambda i: (i, 0))
    return _pallas(
        body, name="resid_rmsmod2_fwd", grid=(s // ROW_TILE,),
        in_specs=[row, row] + [_vec_spec(D)] * 4,
        out_specs=[row, row],
        out_shape=[_sds((s, D), F32), _sds((s, D), MXU_DTYPE)],
        semantics=("parallel",),
    )(x, y, gt, g, sc, sh)


def _final_fwd_bwd(x1, z, gt2, gf, tgt):
    s = x1.shape[0]
    nt = s // ROW_TILE

    def body(x1_ref, z_ref, gt_ref, gf_ref, t_ref, dx2_ref, dz_ref, loss_ref, dgt_ref, dgf_ref, a_loss, a_gt, a_gf):
        i = pl.program_id(0)

        @pl.when(i == 0)
        def _():
            a_loss[...] = jnp.zeros_like(a_loss)
            a_gt[...] = jnp.zeros_like(a_gt)
            a_gf[...] = jnp.zeros_like(a_gf)

        zv = z_ref[...]
        gt = gt_ref[...]
        gf_ = gf_ref[...]
        x2 = x1_ref[...] + gt * zv
        r = lax.rsqrt(jnp.mean(x2 * x2, axis=-1, keepdims=True) + EPS)
        xn = x2 * r
        e = xn * gf_ - t_ref[...]
        a_loss[...] += _colsum8(e * e)
        dyo = e * (1.0 / D)
        a_gf[...] += _colsum8(dyo * xn)
        gdy = gf_ * dyo
        dx2 = r * gdy - xn * (r * r) * jnp.mean(x2 * gdy, axis=-1, keepdims=True)
        dx2_ref[...] = dx2
        dz_ref[...] = (gt * dx2).astype(dz_ref.dtype)
        a_gt[...] += _colsum8(dx2 * zv)

        @pl.when(i == nt - 1)
        def _():
            tot = jnp.sum(jnp.sum(a_loss[...], axis=0, keepdims=True), axis=1, keepdims=True) * (0.5 / D)
            loss_ref[...] = jnp.broadcast_to(tot, loss_ref.shape)
            dgt_ref[...] = jnp.sum(a_gt[...], axis=0, keepdims=True)
            dgf_ref[...] = jnp.sum(a_gf[...], axis=0, keepdims=True)

    row = pl.BlockSpec((ROW_TILE, D), lambda i: (i, 0))
    return _pallas(
        body, name="final_norm_loss", grid=(nt,),
        in_specs=[row, row, _vec_spec(D), _vec_spec(D), row],
        out_specs=[row, row, _vec_spec(LANES), _vec_spec(D), _vec_spec(D)],
        out_shape=[_sds((s, D), F32), _sds((s, D), MXU_DTYPE), _sds((1, LANES), F32), _sds((1, D), F32), _sds((1, D), F32)],
        scratch_shapes=[pltpu.VMEM((8, D), F32)] * 3,
        semantics=("arbitrary",),
    )(x1, z, gt2, gf, tgt)


def _rmsmod_bwd(xin, dh, g, sc, *, name, dh_row_off=0, add=None, resid=None):
    s = xin.shape[0]
    nt = s // ROW_TILE
    want_dx = add is not None
    assert resid is None or want_dx

    def body(*refs):
        it = iter(refs)
        x_ref, dh_ref, g_ref, sc_ref = next(it), next(it), next(it), next(it)
        add_ref = next(it) if want_dx else None
        gt_ref, y_ref = (next(it), next(it)) if resid is not None else (None, None)
        dsh_ref, dsc_ref, dg_ref = next(it), next(it), next(it)
        dx_ref = next(it) if want_dx else None
        dy_ref, dgt_ref = (next(it), next(it)) if resid is not None else (None, None)
        a_sh, a_sc, a_g = next(it), next(it), next(it)
        a_gt = next(it) if resid is not None else None
        i = pl.program_id(0)

        @pl.when(i == 0)
        def _():
            a_sh[...] = jnp.zeros_like(a_sh)
            a_sc[...] = jnp.zeros_like(a_sc)
            a_g[...] = jnp.zeros_like(a_g)
            if a_gt is not None:
                a_gt[...] = jnp.zeros_like(a_gt)

        xv = x_ref[...]
        dhv = dh_ref[...]
        gv = g_ref[...]
        r = lax.rsqrt(jnp.mean(xv * xv, axis=-1, keepdims=True) + EPS)
        xn = xv * r
        a_sh[...] += _colsum8(dhv)
        a_sc[...] += _colsum8(dhv * (xn * gv))
        dn = dhv * (1.0 + sc_ref[...])
        a_g[...] += _colsum8(dn * xn)
        if want_dx:
            gdn = gv * dn
            dx = add_ref[...] + r * gdn - xn * (r * r) * jnp.mean(xv * gdn, axis=-1, keepdims=True)
            dx_ref[...] = dx
            if resid is not None:
                dy_ref[...] = (gt_ref[...] * dx).astype(dy_ref.dtype)
                a_gt[...] += _colsum8(dx * y_ref[...])

        @pl.when(i == nt - 1)
        def _():
            dsh_ref[...] = jnp.sum(a_sh[...], axis=0, keepdims=True)
            dsc_ref[...] = jnp.sum(a_sc[...], axis=0, keepdims=True)
            dg_ref[...] = jnp.sum(a_g[...], axis=0, keepdims=True)
            if a_gt is not None:
                dgt_ref[...] = jnp.sum(a_gt[...], axis=0, keepdims=True)

    row = pl.BlockSpec((ROW_TILE, D), lambda i: (i, 0))
    ins = [xin, dh, g, sc]
    in_specs = [row, pl.BlockSpec((ROW_TILE, D), lambda i: (i + dh_row_off, 0)), _vec_spec(D), _vec_spec(D)]
    out_specs = [_vec_spec(D)] * 3
    out_shape = [_sds((1, D), F32)] * 3
    scratch = [pltpu.VMEM((8, D), F32)] * 3
    if want_dx:
        ins.append(add)
        in_specs.append(row)
        out_specs.append(row)
        out_shape.append(_sds((s, D), F32))
    if resid is not None:
        ins += [resid[0], resid[1]]
        in_specs += [_vec_spec(D), row]
        out_specs += [row, _vec_spec(D)]
        out_shape += [_sds((s, D), MXU_DTYPE), _sds((1, D), F32)]
        scratch.append(pltpu.VMEM((8, D), F32))
    return _pallas(body, name=name, grid=(nt,), in_specs=in_specs, out_specs=out_specs, out_shape=out_shape,
                   scratch_shapes=scratch, semantics=("arbitrary",))(*ins)


FF_TILE = 128
FF_CHUNK = 128
HALO = 8


def _shift3(pad_ref, r0, ch):
    return tuple(pad_ref[pl.ds(r0 + HALO + d, ch), :] for d in (-1, 0, 1))


def _fill_padded(pad_ref, src_ref, s, ch, halo):
    zeros = jnp.zeros((halo, pad_ref.shape[1]), F32)
    pad_ref[0:halo, :] = zeros
    pad_ref[s + halo:s + 2 * halo, :] = zeros

    def cp(c, carry):
        r0 = pl.multiple_of(c * ch, ch)
        pad_ref[pl.ds(r0 + halo, ch), :] = src_ref[pl.ds(r0, ch), :].astype(F32)
        return carry

    lax.fori_loop(0, s // ch, cp, 0)


def _ffn_act_fwd(u, w, b):
    s = u.shape[0]
    nj = DFF // FF_TILE
    ch = FF_CHUNK

    def body(ug_ref, uv_ref, wg_ref, wv_ref, bg_ref, bv_ref, f_ref, gpad, vpad):
        _fill_padded(gpad, ug_ref, s, ch, HALO)
        _fill_padded(vpad, uv_ref, s, ch, HALO)

        def conv(pad, w_ref, b_ref, r0):
            prev, cur, nxt = _shift3(pad, r0, ch)
            return w_ref[0:1, :] * prev + w_ref[1:2, :] * cur + w_ref[2:3, :] * nxt + b_ref[...]

        def step(c, carry):
            r0 = pl.multiple_of(c * ch, ch)
            gc = conv(gpad, wg_ref, bg_ref, r0)
            vc = conv(vpad, wv_ref, bv_ref, r0)
            f_ref[pl.ds(r0, ch), :] = (gc * _sigmoid(gc) * vc).astype(f_ref.dtype)
            return carry

        lax.fori_loop(0, s // ch, step, 0)

    col = lambda off: pl.BlockSpec((s, FF_TILE), lambda j: (0, j + off))
    wsp = lambda off: pl.BlockSpec((3, FF_TILE), lambda j: (0, j + off))
    bsp = lambda off: pl.BlockSpec((1, FF_TILE), lambda j: (0, j + off))
    return _pallas(
        body, name="ffn_act_fwd", grid=(nj,),
        in_specs=[col(0), col(nj), wsp(0), wsp(nj), bsp(0), bsp(nj)],
        out_specs=col(0), out_shape=_sds((s, DFF), MXU_DTYPE),
        scratch_shapes=[pltpu.VMEM((s + 2 * HALO, FF_TILE), F32)] * 2,
        semantics=("parallel",),
    )(u, u, w, w, b, b)


def _ffn_act_bwd(u, df, w, b):
    s = u.shape[0]
    nj = DFF // FF_TILE
    ch = FF_CHUNK

    def body(ug_ref, uv_ref, df_ref, wg_ref, wv_ref, bg_ref, bv_ref,
             dug_ref, duv_ref, dwg_ref, dwv_ref, dbg_ref, dbv_ref, gpad, vpad, dgpad, dvpad, acc):
        _fill_padded(gpad, ug_ref, s, ch, HALO)
        _fill_padded(vpad, uv_ref, s, ch, HALO)
        zeros = jnp.zeros((HALO, FF_TILE), F32)
        for p in (dgpad, dvpad):
            p[0:HALO, :] = zeros
            p[s + HALO:s + 2 * HALO, :] = zeros
        acc[...] = jnp.zeros_like(acc)

        def step(c, carry):
            r0 = pl.multiple_of(c * ch, ch)
            gs = _shift3(gpad, r0, ch)
            vs = _shift3(vpad, r0, ch)
            gc = wg_ref[0:1, :] * gs[0] + wg_ref[1:2, :] * gs[1] + wg_ref[2:3, :] * gs[2] + bg_ref[...]
            vc = wv_ref[0:1, :] * vs[0] + wv_ref[1:2, :] * vs[1] + wv_ref[2:3, :] * vs[2] + bv_ref[...]
            sg = _sigmoid(gc)
            dfv = df_ref[pl.ds(r0, ch), :].astype(F32)
            dgc = dfv * vc * (sg * (1.0 + gc * (1.0 - sg)))
            dvc = dfv * (gc * sg)
            dgpad[pl.ds(r0 + HALO, ch), :] = dgc
            dvpad[pl.ds(r0 + HALO, ch), :] = dvc
            for t in range(3):
                acc[8 * t:8 * t + 8, :] += _colsum8(dgc * gs[t])
                acc[24 + 8 * t:32 + 8 * t, :] += _colsum8(dvc * vs[t])
            acc[48:56, :] += _colsum8(dgc)
            acc[56:64, :] += _colsum8(dvc)
            return carry

        lax.fori_loop(0, s // ch, step, 0)

        def step2(c, carry):
            r0 = pl.multiple_of(c * ch, ch)
            for pad, w_ref, o_ref in ((dgpad, wg_ref, dug_ref), (dvpad, wv_ref, duv_ref)):
                prev, cur, nxt = _shift3(pad, r0, ch)
                o_ref[pl.ds(r0, ch), :] = (w_ref[0:1, :] * nxt + w_ref[1:2, :] * cur + w_ref[2:3, :] * prev).astype(o_ref.dtype)
            return carry

        lax.fori_loop(0, s // ch, step2, 0)
        for t in range(3):
            dwg_ref[t:t + 1, :] = jnp.sum(acc[8 * t:8 * t + 8, :], axis=0, keepdims=True)
            dwv_ref[t:t + 1, :] = jnp.sum(acc[24 + 8 * t:32 + 8 * t, :], axis=0, keepdims=True)
        dbg_ref[...] = jnp.sum(acc[48:56, :], axis=0, keepdims=True)
        dbv_ref[...] = jnp.sum(acc[56:64, :], axis=0, keepdims=True)

    col = lambda off: pl.BlockSpec((s, FF_TILE), lambda j: (0, j + off))
    wsp = lambda off: pl.BlockSpec((3, FF_TILE), lambda j: (0, j + off))
    bsp = lambda off: pl.BlockSpec((1, FF_TILE), lambda j: (0, j + off))
    return _pallas(
        body, name="ffn_act_bwd", grid=(nj,),
        in_specs=[col(0), col(nj), col(0), wsp(0), wsp(nj), bsp(0), bsp(nj)],
        out_specs=[col(0), col(0), wsp(0), wsp(0), bsp(0), bsp(0)],
        out_shape=[_sds((s, DFF), MXU_DTYPE)] * 2 + [_sds((3, DFF), F32)] * 2 + [_sds((1, DFF), F32)] * 2,
        scratch_shapes=[pltpu.VMEM((s + 2 * HALO, FF_TILE), F32)] * 4 + [pltpu.VMEM((64, FF_TILE), F32)],
        semantics=("parallel",),
    )(u, u, df, w, w, b, b)


CONV_CHUNK = 64
CONV_HALO = 16


def _tap(pad_ref, r0, k):
    return pad_ref[pl.ds(r0 + CONV_HALO - CW // 2 + k, CONV_CHUNK), :]


def _glu_into(pad_ref, a_ref, g_ref, s):
    zeros = jnp.zeros((CONV_HALO, LANES), F32)
    pad_ref[0:CONV_HALO, :] = zeros
    pad_ref[s + CONV_HALO:s + 2 * CONV_HALO, :] = zeros

    def cp(c, carry):
        r0 = pl.multiple_of(c * ROW_TILE, ROW_TILE)
        pad_ref[pl.ds(r0 + CONV_HALO, ROW_TILE), :] = a_ref[pl.ds(r0, ROW_TILE), :] * _sigmoid(g_ref[pl.ds(r0, ROW_TILE), :])
        return carry

    lax.fori_loop(0, s // ROW_TILE, cp, 0)


def _conf_conv_fwd(ag, conv_w, conv_b):
    s = ag.shape[0]
    nc = DC // LANES

    def body(a_ref, g_ref, w_ref, b_ref, o_ref, upad):
        _glu_into(upad, a_ref, g_ref, s)

        def step(c, carry):
            r0 = pl.multiple_of(c * CONV_CHUNK, CONV_CHUNK)
            acc = jnp.broadcast_to(b_ref[...], (CONV_CHUNK, LANES))
            for k in range(CW):
                acc = acc + w_ref[k:k + 1, :] * _tap(upad, r0, k)
            o_ref[pl.ds(r0, CONV_CHUNK), :] = acc
            return carry

        lax.fori_loop(0, s // CONV_CHUNK, step, 0)

    col = lambda off: pl.BlockSpec((s, LANES), lambda c: (0, c + off))
    return _pallas(
        body, name="conf_conv_fwd", grid=(nc,),
        in_specs=[col(0), col(nc), pl.BlockSpec((CW, LANES), lambda c: (0, c)), pl.BlockSpec((1, LANES), lambda c: (0, c))],
        out_specs=col(0), out_shape=_sds((s, DC), F32),
        scratch_shapes=[pltpu.VMEM((s + 2 * CONV_HALO, LANES), F32)],
        semantics=("parallel",),
    )(ag, ag, conv_w, conv_b)


def _ln_stats(x):
    mu = jnp.mean(x, axis=-1, keepdims=True)
    xc = x - mu
    var = jnp.mean(xc * xc, axis=-1, keepdims=True)
    rstd = lax.rsqrt(var + EPS)
    return xc * rstd, rstd


def _conf_ln_fwd(u1, ln_g, ln_b, ycat):
    s = u1.shape[0]

    def body(u_ref, g_ref, b_ref, ycat_ref, o_ref):
        del ycat_ref
        xhat, _ = _ln_stats(u_ref[...])
        y = xhat * g_ref[...] + b_ref[...]
        o_ref[...] = (y * _sigmoid(y)).astype(o_ref.dtype)

    return _pallas(
        body, name="conf_ln_fwd", grid=(s // ROW_TILE,),
        in_specs=[pl.BlockSpec((ROW_TILE, DC), lambda i: (i, 0)), _vec_spec(DC), _vec_spec(DC),
                  pl.BlockSpec(memory_space=pl.ANY)],
        out_specs=pl.BlockSpec((ROW_TILE, DC), lambda i: (i, 1)),
        out_shape=_sds(ycat.shape, ycat.dtype),
        input_output_aliases={3: 0},
        semantics=("parallel",),
    )(u1, ln_g, ln_b, ycat)


def _conf_ln_bwd(dycat, u1, ln_g, ln_b):
    s = u1.shape[0]
    nt = s // ROW_TILE

    def body(dy_ref, u_ref, g_ref, b_ref, du_ref, dg_ref, db_ref, a_g, a_b):
        i = pl.program_id(0)

        @pl.when(i == 0)
        def _():
            a_g[...] = jnp.zeros_like(a_g)
            a_b[...] = jnp.zeros_like(a_b)

        xhat, rstd = _ln_stats(u_ref[...])
        gv = g_ref[...]
        y = xhat * gv + b_ref[...]
        sg = _sigmoid(y)
        dyl = dy_ref[...] * (sg * (1.0 + y * (1.0 - sg)))
        a_g[...] += _colsum8(dyl * xhat)
        a_b[...] += _colsum8(dyl)
        dxh = dyl * gv
        du_ref[...] = rstd * (dxh - jnp.mean(dxh, axis=-1, keepdims=True)
                              - xhat * jnp.mean(dxh * xhat, axis=-1, keepdims=True))

        @pl.when(i == nt - 1)
        def _():
            dg_ref[...] = jnp.sum(a_g[...], axis=0, keepdims=True)
            db_ref[...] = jnp.sum(a_b[...], axis=0, keepdims=True)

    return _pallas(
        body, name="conf_ln_bwd", grid=(nt,),
        in_specs=[pl.BlockSpec((ROW_TILE, DC), lambda i: (i, 1)), pl.BlockSpec((ROW_TILE, DC), lambda i: (i, 0)),
                  _vec_spec(DC), _vec_spec(DC)],
        out_specs=[pl.BlockSpec((ROW_TILE, DC), lambda i: (i, 0)), _vec_spec(DC), _vec_spec(DC)],
        out_shape=[_sds((s, DC), F32), _sds((1, DC), F32), _sds((1, DC), F32)],
        scratch_shapes=[pltpu.VMEM((8, DC), F32)] * 2,
        semantics=("arbitrary",),
    )(dycat, u1, ln_g, ln_b)


def _conf_conv_bwd(ag, du1, conv_w, rows_out):
    s = ag.shape[0]
    nc = DC // LANES

    def body(a_ref, g_ref, d_ref, w_ref, da_ref, dg_ref, dw_ref, db_ref, upad, dpad, acc):
        _glu_into(upad, a_ref, g_ref, s)
        _fill_padded(dpad, d_ref, s, ROW_TILE, CONV_HALO)
        acc[...] = jnp.zeros_like(acc)

        def step(c, carry):
            r0 = pl.multiple_of(c * CONV_CHUNK, CONV_CHUNK)
            dcur = dpad[pl.ds(r0 + CONV_HALO, CONV_CHUNK), :]
            du0 = jnp.zeros((CONV_CHUNK, LANES), F32)
            for k in range(CW):
                du0 = du0 + w_ref[k:k + 1, :] * _tap(dpad, r0, CW - 1 - k)
                acc[8 * k:8 * k + 8, :] += _colsum8(dcur * _tap(upad, r0, k))
            acc[8 * CW:8 * CW + 8, :] += _colsum8(dcur)
            av = a_ref[pl.ds(r0, CONV_CHUNK), :]
            sg = _sigmoid(g_ref[pl.ds(r0, CONV_CHUNK), :])
            da_ref[pl.ds(r0, CONV_CHUNK), :] = (du0 * sg).astype(da_ref.dtype)
            dg_ref[pl.ds(r0, CONV_CHUNK), :] = (du0 * av * (sg * (1.0 - sg))).astype(dg_ref.dtype)
            return carry

        lax.fori_loop(0, s // CONV_CHUNK, step, 0)
        if rows_out > s:
            zeros = jnp.zeros((rows_out - s, LANES), da_ref.dtype)
            da_ref[s:rows_out, :] = zeros
            dg_ref[s:rows_out, :] = zeros
        for k in range(CW):
            dw_ref[k:k + 1, :] = jnp.sum(acc[8 * k:8 * k + 8, :], axis=0, keepdims=True)
        db_ref[...] = jnp.sum(acc[8 * CW:8 * CW + 8, :], axis=0, keepdims=True)

    col = lambda off: pl.BlockSpec((s, LANES), lambda c: (0, c + off))
    ocol = pl.BlockSpec((rows_out, LANES), lambda c: (0, c))
    return _pallas(
        body, name="conf_conv_bwd", grid=(nc,),
        in_specs=[col(0), col(nc), col(0), pl.BlockSpec((CW, LANES), lambda c: (0, c))],
        out_specs=[ocol, ocol, pl.BlockSpec((CW, LANES), lambda c: (0, c)), pl.BlockSpec((1, LANES), lambda c: (0, c))],
        out_shape=[_sds((rows_out, DC), MXU_DTYPE)] * 2 + [_sds((CW, DC), F32), _sds((1, DC), F32)],
        scratch_shapes=[pltpu.VMEM((s + 2 * CONV_HALO, LANES), F32)] * 2 + [pltpu.VMEM((8 * (CW + 1), LANES), F32)],
        semantics=("parallel",),
    )(ag, ag, du1, conv_w)


Q_TILE = 2 * GW
K_WIN = PAIR_ROWS * GW


def _bias_table(rpb_rev):
    def body(p_ref, t_ref):
        kcol = lax.broadcasted_iota(jnp.int32, (GW, LANES), 0)
        lane = lax.broadcasted_iota(jnp.int32, (GW, LANES), 1)
        qcol = lane % GW
        cs = jnp.clip(qcol - NA_ROWS, 0, GW - 2 * NA_ROWS)
        colvalid = (kcol >= cs) & (kcol < cs + 2 * NA_ROWS)
        neg = jnp.full((GW, LANES), NEG, F32)

        def skew(h, ro, shift):
            if ro < 0 or ro >= 2 * NA_ROWS - 1:
                return neg
            row = jnp.broadcast_to(p_ref[h * 16 + ro:h * 16 + ro + 1, :], (GW, LANES))
            return pltpu.roll(row, shift, 1, stride=1, stride_axis=0)

        for h in range(NH):
            for b in range(TAB_BLOCKS):
                val = jnp.where(lane < GW, skew(h, b - 1, GW + 1), skew(h, b - 2, 1))
                t_ref[h, b * GW:(b + 1) * GW, :] = jnp.where(colvalid, val, neg)

    return _pallas(body, name="attn_bias_table", out_shape=_sds((NH, TAB_BLOCKS * GW, LANES), F32))(rpb_rev)


def _rpb_grad(tt):
    def body(t_ref, o_ref):
        lane = lax.broadcasted_iota(jnp.int32, (GW, LANES), 1)
        si = lax.broadcasted_iota(jnp.int32, (GW, GW), 0)
        ti = lax.broadcasted_iota(jnp.int32, (GW, GW), 1)
        flip = jnp.where(si + ti == GW - 1, 1.0, 0.0).astype(F32)
        o_ref[...] = jnp.zeros_like(o_ref)
        for h in range(NH):
            for ro in range(2 * NA_ROWS - 1):
                lo = t_ref[h, (ro + 1) * GW:(ro + 2) * GW, :]
                hi = t_ref[h, (ro + 2) * GW:(ro + 3) * GW, :]
                g = jnp.where(lane < GW, lo + pltpu.roll(hi, GW, 1), 0.0)
                gf = jnp.dot(flip, g, preferred_element_type=F32, precision=lax.Precision.HIGHEST)
                sk = pltpu.roll(gf, 0, 1, stride=1, stride_axis=0)
                o_ref[h * 16 + ro:h * 16 + ro + 1, :] = jnp.sum(sk, axis=0, keepdims=True)

    return _pallas(body, name="attn_rpb_grad", out_shape=_sds((NH * 16, LANES), F32))(tt)


def _attn_geometry(i, rows):
    wsp = jnp.clip(2 * i - NA_ROWS // 2, 0, rows - PAIR_ROWS)
    k0 = pl.multiple_of(wsp * GW, GW)
    t0 = pl.multiple_of((wsp - 2 * i + NA_ROWS) * GW, GW)
    jr = lax.broadcasted_iota(jnp.int32, (K_WIN, Q_TILE), 0) // GW
    rr = lax.broadcasted_iota(jnp.int32, (K_WIN, Q_TILE), 1) // GW
    kr = wsp + jr
    wsr = jnp.clip(2 * i + rr - NA_ROWS // 2, 0, rows - NA_ROWS)
    rowmask = jnp.where((kr >= wsr) & (kr < wsr + NA_ROWS), 0.0, NEG).astype(F32)
    return k0, t0, rowmask


def _two_heads_on_lanes(xt):
    feat = lax.broadcasted_iota(jnp.int32, xt.shape, 0)
    zero = jnp.zeros_like(xt)
    return jnp.concatenate([jnp.where(feat < HD, xt, zero), jnp.where(feat >= HD, xt, zero)], axis=1)


def _two_heads_on_rows(x):
    lane = lax.broadcasted_iota(jnp.int32, x.shape, 1)
    zero = jnp.zeros_like(x)
    return jnp.concatenate([jnp.where(lane < HD, x, zero), jnp.where(lane >= HD, x, zero)], axis=0)


def _pick_heads(x2):
    n = x2.shape[0] // 2
    lane = lax.broadcasted_iota(jnp.int32, (n, LANES), 1)
    return jnp.where(lane < HD, x2[:n], x2[n:])


_TN = (((0,), (0,)), ((), ()))


def _attn_fwd(qkv, tab, s):
    rows = s // GW
    npair = rows // 2

    def body(q_ref, kv_ref, tab_ref, o_ref, lse_ref):
        i = pl.program_id(0)
        k0, t0, rowmask = _attn_geometry(i, rows)
        for p in range(NH // 2):
            cq = slice(p * LANES, (p + 1) * LANES)
            ck = slice(DA + p * LANES, DA + (p + 1) * LANES)
            cv = slice(2 * DA + p * LANES, 2 * DA + (p + 1) * LANES)
            qm2 = _two_heads_on_lanes(q_ref[:, cq].T)
            s_loc = jnp.dot(kv_ref[pl.ds(k0, K_WIN), ck], qm2, preferred_element_type=F32) * SCALE
            s_ctx = jnp.dot(kv_ref[pl.ds(s, CTX), ck], qm2, preferred_element_type=F32) * SCALE
            p_loc, p_ctx = [], []
            for hh in range(2):
                h = 2 * p + hh
                ch = slice(hh * Q_TILE, (hh + 1) * Q_TILE)
                sl = s_loc[:, ch] + tab_ref[h, pl.ds(t0, K_WIN), :] + rowmask
                sc = s_ctx[:, ch]
                m = jnp.maximum(jnp.max(sl, axis=0, keepdims=True), jnp.max(sc, axis=0, keepdims=True))
                el = jnp.exp(sl - m)
                ec = jnp.exp(sc - m)
                l = jnp.sum(el, axis=0, keepdims=True) + jnp.sum(ec, axis=0, keepdims=True)
                inv = 1.0 / l
                lse_ref[h:h + 1, :] = m + jnp.log(l)
                p_loc.append((el * inv).astype(MXU_DTYPE))
                p_ctx.append((ec * inv).astype(MXU_DTYPE))
            o2 = (lax.dot_general(jnp.concatenate(p_loc, axis=1), kv_ref[pl.ds(k0, K_WIN), cv], _TN, preferred_element_type=F32)
                  + lax.dot_general(jnp.concatenate(p_ctx, axis=1), kv_ref[pl.ds(s, CTX), cv], _TN, preferred_element_type=F32))
            o_ref[:, cq] = _pick_heads(o2).astype(o_ref.dtype)

    return _pallas(
        body, name="attn_fwd", grid=(npair,),
        in_specs=[pl.BlockSpec((Q_TILE, DA), lambda i: (i, 0)), pl.BlockSpec(memory_space=pltpu.VMEM),
                  pl.BlockSpec(memory_space=pltpu.VMEM)],
        out_specs=[pl.BlockSpec((Q_TILE, DA), lambda i: (i, 0)), pl.BlockSpec((NH, Q_TILE), lambda i: (0, i))],
        out_shape=[_sds((s, D), MXU_DTYPE), _sds((NH, s), F32)],
        semantics=("arbitrary",),
    )(qkv, qkv, tab)


def _attn_bwd(qkv, tab, lse, dycat, s):
    rows = s // GW
    npair = rows // 2
    sa = s + CTX
    nzero = CTX // Q_TILE

    def body(q_ref, do_ref, lse_ref, kv_ref, tab_ref, dq_ref, dkv_ref, tt_ref, dk_acc, dv_acc):
        i = pl.program_id(0)

        @pl.when(i == 0)
        def _():
            dk_acc[...] = jnp.zeros_like(dk_acc)
            dv_acc[...] = jnp.zeros_like(dv_acc)
            tt_ref[...] = jnp.zeros_like(tt_ref)

        @pl.when(i >= npair)
        def _():
            dq_ref[...] = jnp.zeros_like(dq_ref)

        @pl.when(i < npair)
        def _():
            k0, t0, rowmask = _attn_geometry(i, rows)
            for p in range(NH // 2):
                cq = slice(p * LANES, (p + 1) * LANES)
                ck = slice(DA + p * LANES, DA + (p + 1) * LANES)
                cv = slice(2 * DA + p * LANES, 2 * DA + (p + 1) * LANES)
                qp = q_ref[:, cq]
                dop = do_ref[:, cq].astype(MXU_DTYPE)
                qm2 = _two_heads_on_lanes(qp.T)
                dom2 = _two_heads_on_lanes(dop.T)
                kw = kv_ref[pl.ds(k0, K_WIN), ck]
                kc = kv_ref[pl.ds(s, CTX), ck]
                vw = kv_ref[pl.ds(k0, K_WIN), cv]
                vc = kv_ref[pl.ds(s, CTX), cv]
                s_loc = jnp.dot(kw, qm2, preferred_element_type=F32) * SCALE
                s_ctx = jnp.dot(kc, qm2, preferred_element_type=F32) * SCALE
                dp_loc = jnp.dot(vw, dom2, preferred_element_type=F32)
                dp_ctx = jnp.dot(vc, dom2, preferred_element_type=F32)
                p_loc, p_ctx, ds_loc, ds_ctx = [], [], [], []
                for hh in range(2):
                    h = 2 * p + hh
                    ch = slice(hh * Q_TILE, (hh + 1) * Q_TILE)
                    lse_h = lse_ref[h:h + 1, :]
                    pl_ = jnp.exp(s_loc[:, ch] + tab_ref[h, pl.ds(t0, K_WIN), :] + rowmask - lse_h)
                    pc_ = jnp.exp(s_ctx[:, ch] - lse_h)
                    dpl = dp_loc[:, ch]
                    dpc = dp_ctx[:, ch]
                    delta = jnp.sum(pl_ * dpl, axis=0, keepdims=True) + jnp.sum(pc_ * dpc, axis=0, keepdims=True)
                    dsl = pl_ * (dpl - delta)
                    dsc = pc_ * (dpc - delta)
                    tt_ref[h, pl.ds(t0, K_WIN), :] += dsl
                    p_loc.append(pl_.astype(MXU_DTYPE))
                    p_ctx.append(pc_.astype(MXU_DTYPE))
                    ds_loc.append((dsl * SCALE).astype(MXU_DTYPE))
                    ds_ctx.append((dsc * SCALE).astype(MXU_DTYPE))
                p_loc, p_ctx = jnp.concatenate(p_loc, axis=1), jnp.concatenate(p_ctx, axis=1)
                ds_loc, ds_ctx = jnp.concatenate(ds_loc, axis=1), jnp.concatenate(ds_ctx, axis=1)
                do_rows = _two_heads_on_rows(dop)
                q_rows = _two_heads_on_rows(qp)
                dv_acc[pl.ds(k0, K_WIN), cq] += jnp.dot(p_loc, do_rows, preferred_element_type=F32)
                dv_acc[pl.ds(s, CTX), cq] += jnp.dot(p_ctx, do_rows, preferred_element_type=F32)
                dk_acc[pl.ds(k0, K_WIN), cq] += jnp.dot(ds_loc, q_rows, preferred_element_type=F32)
                dk_acc[pl.ds(s, CTX), cq] += jnp.dot(ds_ctx, q_rows, preferred_element_type=F32)
                dq2 = (lax.dot_general(ds_loc, kw, _TN, preferred_element_type=F32)
                       + lax.dot_general(ds_ctx, kc, _TN, preferred_element_type=F32))
                dq_ref[:, cq] = _pick_heads(dq2).astype(dq_ref.dtype)

        @pl.when(i == npair - 1)
        def _():
            def cp(c, carry):
                r0 = pl.multiple_of(c * ROW_TILE, ROW_TILE)
                dkv_ref[pl.ds(r0, ROW_TILE), 0:DA] = dk_acc[pl.ds(r0, ROW_TILE), :].astype(dkv_ref.dtype)
                dkv_ref[pl.ds(r0, ROW_TILE), DA:2 * DA] = dv_acc[pl.ds(r0, ROW_TILE), :].astype(dkv_ref.dtype)
                return carry

            lax.fori_loop(0, sa // ROW_TILE, cp, 0)

    qmap = lambda i: (jnp.minimum(i, npair - 1), 0)
    return _pallas(
        body, name="attn_bwd", grid=(npair + nzero,),
        in_specs=[pl.BlockSpec((Q_TILE, DA), qmap), pl.BlockSpec((Q_TILE, DA), qmap),
                  pl.BlockSpec((NH, Q_TILE), lambda i: (0, jnp.minimum(i, npair - 1))),
                  pl.BlockSpec(memory_space=pltpu.VMEM), pl.BlockSpec(memory_space=pltpu.VMEM)],
        out_specs=[pl.BlockSpec((Q_TILE, DA), lambda i: (i, 0)), pl.BlockSpec(memory_space=pltpu.VMEM),
                   pl.BlockSpec(memory_space=pltpu.VMEM)],
        out_shape=[_sds((sa, DA), MXU_DTYPE), _sds((sa, 2 * DA), MXU_DTYPE), _sds((NH, TAB_BLOCKS * GW, LANES), F32)],
        scratch_shapes=[pltpu.VMEM((sa, DA), F32)] * 2,
        semantics=("arbitrary",),
    )(qkv, dycat, lse, qkv, tab)


def _tile(n, prefs):
    for t in prefs:
        if n % t == 0:
            return t
    raise ValueError((n, prefs))


def _local_step(x, ctx, tgt, mod, mod_c, vec, w_in, late_weights, rpb_rev, early_grads=None):
    s = x.shape[0]
    sa = s + CTX
    ts = _tile(s, (1024, 512, 256))
    ts2 = _tile(s, (2048, 1024, 512, 256))
    tsa = _tile(sa, (1088, 640, 256))
    tsa2 = _tile(sa, (2176, 640, 256))
    sh1, sc1, gt1, sh2, sc2, gt2 = (mod[i:i + 1] for i in range(6))
    csh1, csc1 = mod_c[0:1], mod_c[1:2]
    act = MXU_DTYPE

    tab = _bias_table(rpb_rev)
    h_all = _rmsmod_fwd(x, ctx, vec["g_norm1"], sc1, sh1, csc1, csh1)
    qkv = _mm(h_all, w_in, mode="nn", m=sa, n=3 * DA, k=D, tm=tsa2, tn=512, tk=D, out_dtype=MXU_DTYPE, name="mm_qkv")
    ag = _mm(h_all, w_in, mode="nn", m=s, n=2 * DC, k=D, tm=ts2, tn=512, tk=D, out_dtype=F32, name="mm_ag", b_off=(0, 3))
    ycat, lse = _attn_fwd(qkv, tab, s)
    u1 = _conf_conv_fwd(ag, vec["conv_w"], vec["conv_b"])
    ycat = _conf_ln_fwd(u1, vec["ln_g"], vec["ln_b"], ycat)
    w_out, w_up, w_down = late_weights(ycat) if callable(late_weights) else late_weights
    y = _mm(ycat, w_out, mode="nn", m=s, n=D, k=D, tm=ts2, tn=512, tk=D, out_dtype=F32, name="mm_out")
    x1, h2 = _resid_rmsmod_fwd(x, y, gt1, vec["g_norm2"], sc2, sh2)
    u = _mm(h2, w_up, mode="nn", m=s, n=2 * DFF, k=D, tm=ts2, tn=512, tk=D, out_dtype=act, name="mm_up")
    f = _ffn_act_fwd(u, vec["ffn_conv_w"], vec["ffn_conv_b"])
    z = _mm(f, w_down, mode="nn", m=s, n=D, k=DFF, tm=ts, tn=D, tk=DFF, out_dtype=F32, name="mm_down")
    dx2, dz, loss, dgt2, dgf = _final_fwd_bwd(x1, z, gt2, vec["g_final"], tgt)

    df = _mm(dz, w_down, mode="nt", m=s, n=DFF, k=D, tm=ts, tn=DFF, tk=D, out_dtype=act, name="mm_down_dx")
    d_w_down = _mm(f, dz, mode="tn", m=DFF, n=D, k=s, tm=DFF, tn=D, tk=ts, out_dtype=F32, name="mm_down_dw")
    dug, duv, dfw_g, dfw_v, dfb_g, dfb_v = _ffn_act_bwd(u, df, vec["ffn_conv_w"], vec["ffn_conv_b"])
    dw_kw = dict(mode="tn", m=D, n=DFF, k=s, tm=D, tn=DFF, tk=ts, out_dtype=F32, out_total=(D, 2 * DFF))
    d_w_up = _mm(h2, dug, name="mm_up_dw_gate", **dw_kw)
    d_w_up = _mm(h2, duv, name="mm_up_dw_val", o_off=(0, 1), into=d_w_up, **dw_kw)
    if early_grads is not None:
        early_grads[0](d_w_up, d_w_down)
    dh2 = _mm([dug, duv], w_up, mode="nt", m=s, n=D, k=2 * DFF, tm=ts, tn=D, tk=DFF, out_dtype=F32, name="mm_up_dx")
    sc2_b = sc2 if early_grads is None else sc2 + early_grads[1](dh2)
    dsh2, dsc2, dg2, dx1, dy, dgt1 = _rmsmod_bwd(x1, dh2, vec["g_norm2"], sc2_b, name="rmsmod2_bwd", add=dx2, resid=(gt1, y))
    dycat = _mm(dy, w_out, mode="nt", m=s, n=D, k=D, tm=ts2, tn=512, tk=D, out_dtype=F32, name="mm_out_dx")
    d_w_out = _mm(ycat, dy, mode="tn", m=D, n=D, k=s, tm=D, tn=D, tk=ts, out_dtype=F32, name="mm_out_dw")
    du1, dln_g, dln_b = _conf_ln_bwd(dycat, u1, vec["ln_g"], vec["ln_b"])
    da, dg, dconv_w, dconv_b = _conf_conv_bwd(ag, du1, vec["conv_w"], sa)
    dq, dkv, tt = _attn_bwd(qkv, tab, lse, dycat, s)
    drpb_rev = _rpb_grad(tt)
    d_pieces = [dq, dkv, da, dg]
    dh = _mm(d_pieces, w_in, mode="nt", m=sa, n=D, k=NIN, tm=tsa2, tn=D, tk=512, out_dtype=F32, name="mm_in_dx")
    d_w_in = _mm(h_all, d_pieces, mode="tn", m=D, n=NIN, k=sa, tm=D, tn=512, tk=tsa, out_dtype=F32, name="mm_in_dw",
                 k_outer=True)
    dsh1, dsc1, dg1, grad_x = _rmsmod_bwd(x, dh, vec["g_norm1"], sc1, name="rmsmod1_bwd", add=dx1)
    dcsh1, dcsc1, dg1c = _rmsmod_bwd(ctx, dh, vec["g_norm1"], csc1, name="rmsmod1_ctx_bwd", dh_row_off=s // ROW_TILE)

    small = dict(
        dmod=[dsh1, dsc1, dgt1, dsh2, dsc2, dgt2], dmod_c=[dcsh1, dcsc1],
        g_norm1=[dg1, dg1c], g_norm2=dg2, g_final=dgf, conv_b=dconv_b, ln_g=dln_g, ln_b=dln_b, conv_w=dconv_w,
        ffn_conv_w=[dfw_g, dfw_v], ffn_conv_b=[dfb_g, dfb_v], rpb_rev=drpb_rev,
    )
    return loss, grad_x, d_w_in, d_w_out, d_w_up, d_w_down, small


N_CHIPS = 4
HBM = pl.BlockSpec(memory_space=pl.ANY)
BIG = {"w_in": ("col", (D, NIN)), "w_out": ("row", (D, D)), "w_up": ("col", (D, 2 * DFF)), "w_down": ("row", (DFF, D))}
BIG_NAMES = tuple(BIG)
LATE_NAMES = ("w_out", "w_up", "w_down")


def _shard_shape(name):
    kind, (r, c) = BIG[name]
    return (r, c // N_CHIPS) if kind == "col" else (r // N_CHIPS, c)


def _half_rows(name):
    return _shard_shape(name)[0] // 2


def _place():
    x, y, c = lax.axis_index("x"), lax.axis_index("y"), lax.axis_index("c")
    others = [(1 - x, y), (x, 1 - y), (1 - x, 1 - y)]
    return x, y, c, 2 * x + y, (x, y, 1 - c), others


def _whole_region(ref, name, chip, half):
    kind, _ = BIG[name]
    r, c = _shard_shape(name)
    if kind == "col":
        return ref.at[pl.ds(half * (r // 2), r // 2), pl.ds(chip * c, c)]
    return ref.at[pl.ds(chip * r + half * (r // 2), r // 2), :]


def _remote(src, dst, send_sem, recv_sem, to):
    return pltpu.make_async_remote_copy(src_ref=src, dst_ref=dst, send_sem=send_sem, recv_sem=recv_sem,
                                        device_id=to, device_id_type=MESH)


def _gather_small(v, name):
    m_per, n = v.shape

    def body(x_ref, out_ref, send_sems, recv_sems, local_sem):
        x, y, c, _, sibling, others = _place()
        me = (x, y, c)

        def rows(px, py, pc):
            return out_ref.at[pl.ds((4 * px + 2 * py + pc) * m_per, m_per), :]

        def copy(k, block, to, src=None):
            return _remote(rows(*block) if src is None else src, rows(*block), send_sems.at[k], recv_sems.at[k], to)

        mine = pltpu.make_async_copy(x_ref, rows(*me), local_sem)
        mine.start()
        first = [copy(0, me, sibling, src=x_ref)]
        first += [copy(1 + j, me, (*chip, c), src=x_ref) for j, chip in enumerate(others)]
        for cp in first:
            cp.start()
        passed = [copy(4 + j, (*chip, c), sibling) for j, chip in enumerate(others)]
        for j, chip in enumerate(others):
            copy(1 + j, (*chip, c), me).wait_recv()
            passed[j].start()
        copy(0, sibling, me).wait_recv()
        for j, chip in enumerate(others):
            copy(4 + j, (*chip, 1 - c), me).wait_recv()
        for cp in first + passed:
            cp.wait_send()
        mine.wait()

    return pl.pallas_call(
        body, name=name, out_shape=_sds((8 * m_per, n), v.dtype),
        in_specs=[pl.BlockSpec(memory_space=pltpu.VMEM)], out_specs=pl.BlockSpec(memory_space=pltpu.VMEM),
        scratch_shapes=[pltpu.SemaphoreType.DMA((7,)), pltpu.SemaphoreType.DMA((7,)), pltpu.SemaphoreType.DMA],
    )(v)


def _cast_into_whole(name, shard, chip):
    kind, whole = BIG[name]
    r, c = shard.shape
    if kind == "col":
        tr = 256
        o_spec = pl.BlockSpec((tr, c), lambda i, ch: (i, ch[0]))
    else:
        tr = _tile(r, (128, 352))
        o_spec = pl.BlockSpec((tr, c), lambda i, ch: (ch[0] * (r // tr) + i, 0))

    def body(ch_ref, x_ref, o_ref):
        del ch_ref
        o_ref[...] = x_ref[...].astype(o_ref.dtype)

    return _pallas(body, name="cast_" + name, prefetch=1, grid=(r // tr,),
                   in_specs=[pl.BlockSpec((tr, c), lambda i, ch: (i, 0))], out_specs=o_spec,
                   out_shape=_sds(whole, MXU_DTYPE), semantics=("parallel",))(chip, shard)


def _gather_weights(wholes, names, label, after):
    nw = len(names)

    def body(*refs):
        outs = refs[nw + 1:2 * nw + 1]
        send_sems, recv_sems = refs[2 * nw + 1:]
        _, _, c, chip, sibling, others = _place()
        sends = []
        for w, name in enumerate(names):
            mine = _whole_region(outs[w], name, chip, c)
            for t, (ox, oy) in enumerate(others):
                cp = _remote(mine, mine, send_sems.at[w, t], recv_sems.at[w, t], (ox, oy, c))
                cp.start()
                sends.append(cp)
        for w, name in enumerate(names):
            for t, (ox, oy) in enumerate(others):
                got = _whole_region(outs[w], name, 2 * ox + oy, c)
                _remote(got, got, send_sems.at[w, t], recv_sems.at[w, t], (ox, oy, c)).wait_recv()
                cp = _remote(got, got, send_sems.at[w, 3 + t], recv_sems.at[w, 3 + t], sibling)
                cp.start()
                sends.append(cp)
        for w, name in enumerate(names):
            for t, (ox, oy) in enumerate(others):
                got = _whole_region(outs[w], name, 2 * ox + oy, 1 - c)
                _remote(got, got, send_sems.at[w, 3 + t], recv_sems.at[w, 3 + t], sibling).wait_recv()
        for cp in sends:
            cp.wait_send()

    return pl.pallas_call(
        body, name=label,
        out_shape=[_sds(a.shape, a.dtype) for a in wholes],
        in_specs=[HBM] * (nw + 1), out_specs=[HBM] * nw,
        input_output_aliases={i: i for i in range(nw)},
        scratch_shapes=[pltpu.SemaphoreType.DMA((nw, 6)), pltpu.SemaphoreType.DMA((nw, 6))],
    )(*wholes, after)


SEM = pl.BlockSpec(memory_space=pltpu.SEMAPHORE)
IN_HBM = pl.BlockSpec(memory_space=pltpu.HBM)
DATAFLOW = pltpu.SideEffectType.DATAFLOW_SIDE_EFFECTING


def _keep_in_hbm(a):
    return pltpu.with_memory_space_constraint(a, pltpu.HBM)


def _gather_start(wholes, names, after):
    nw = len(names)
    ns = 2 * 3 * nw

    def body(*refs):
        ins = refs[:nw]
        sems = refs[nw + 1:nw + 1 + ns]
        token = refs[2 * nw + ns + 1]
        _, _, c, chip, _, others = _place()
        for w, name in enumerate(names):
            mine = _whole_region(ins[w], name, chip, c)
            for t, (ox, oy) in enumerate(others):
                k = 2 * (3 * w + t)
                _remote(mine, mine, sems[k], sems[k + 1], (ox, oy, c)).start()
        token[...] = jnp.zeros_like(token)

    res = pl.pallas_call(
        body, name="gather_late_start",
        out_shape=(*[pltpu.SemaphoreType.DMA(())] * ns, *[pltpu.HBM(a.shape, a.dtype) for a in wholes], _sds((8, LANES), F32)),
        in_specs=[IN_HBM] * nw + [pl.BlockSpec(memory_space=pl.ANY)],
        out_specs=(*[SEM] * ns, *[IN_HBM] * nw, pl.BlockSpec(memory_space=pltpu.VMEM)),
        input_output_aliases={i: ns + i for i in range(nw)},
        compiler_params=pltpu.CompilerParams(has_side_effects=DATAFLOW),
    )(*[_keep_in_hbm(a) for a in wholes], after)
    return list(res[:ns]), list(res[ns:ns + nw]), res[ns + nw]


def _gather_wait(sems, wholes, names, after):
    nw = len(names)
    ns = len(sems)

    def body(*refs):
        ins = refs[:nw]
        sem_refs = refs[nw:nw + ns]
        _, _, c, chip, _, others = _place()
        for w, name in enumerate(names):
            mine = _whole_region(ins[w], name, chip, c)
            for t, (ox, oy) in enumerate(others):
                got = _whole_region(ins[w], name, 2 * ox + oy, c)
                k = 2 * (3 * w + t)
                cp = _remote(mine, got, sem_refs[k], sem_refs[k + 1], (ox, oy, c))
                cp.wait_send()
                cp.wait_recv()

    return pl.pallas_call(
        body, name="gather_late_wait",
        out_shape=tuple(pltpu.HBM(a.shape, a.dtype) for a in wholes),
        in_specs=[IN_HBM] * nw + [SEM] * ns + [pl.BlockSpec(memory_space=pl.ANY)], out_specs=tuple([IN_HBM] * nw),
        input_output_aliases={i: i for i in range(nw)},
        compiler_params=pltpu.CompilerParams(has_side_effects=DATAFLOW),
    )(*wholes, *sems, after)


def _forward_halves(wholes, names):
    nw = len(names)

    def body(*refs):
        outs = refs[nw:2 * nw]
        send_sems, recv_sems = refs[2 * nw:]
        _, _, c, _, sibling, others = _place()
        sends = []
        for w, name in enumerate(names):
            for t, (ox, oy) in enumerate(others):
                got = _whole_region(outs[w], name, 2 * ox + oy, c)
                cp = _remote(got, got, send_sems.at[w, t], recv_sems.at[w, t], sibling)
                cp.start()
                sends.append(cp)
        for w, name in enumerate(names):
            for t, (ox, oy) in enumerate(others):
                got = _whole_region(outs[w], name, 2 * ox + oy, 1 - c)
                _remote(got, got, send_sems.at[w, t], recv_sems.at[w, t], sibling).wait_recv()
        for cp in sends:
            cp.wait_send()

    return pl.pallas_call(
        body, name="gather_late_forward",
        out_shape=[_sds(a.shape, a.dtype) for a in wholes],
        in_specs=[HBM] * nw, out_specs=[HBM] * nw,
        input_output_aliases={i: i for i in range(nw)},
        scratch_shapes=[pltpu.SemaphoreType.DMA((nw, 3)), pltpu.SemaphoreType.DMA((nw, 3))],
    )(*wholes)


def _compact_shape(name, dtype):
    kind, (r, c) = BIG[name]
    return _sds((r // 2, c), dtype)


def _swap_pairs(ins, outs, names, c):
    pairs = []
    for w, name in enumerate(names):
        kind, _ = BIG[name]
        half = _half_rows(name)
        if kind == "col":
            pairs.append((ins[w].at[pl.ds((1 - c) * half, half), :], outs[w]))
        else:
            pairs += [(ins[w].at[pl.ds(jj * 2 * half + (1 - c) * half, half), :], outs[w].at[pl.ds(jj * half, half), :])
                      for jj in range(N_CHIPS)]
    return pairs


def _n_swap_copies(names):
    return sum(1 if BIG[n][0] == "col" else N_CHIPS for n in names)


def _swap_halves(grads, names, label):
    nw = len(names)
    ncp = _n_swap_copies(names)

    def body(*refs):
        ins, outs = refs[:nw], refs[nw:2 * nw]
        send_sems, recv_sems = refs[2 * nw:]
        _, _, c, _, sibling, _ = _place()
        copies = [_remote(src, dst, send_sems.at[k], recv_sems.at[k], sibling)
                  for k, (src, dst) in enumerate(_swap_pairs(ins, outs, names, c))]
        for cp in copies:
            cp.start()
        for cp in copies:
            cp.wait()

    return pl.pallas_call(
        body, name=label,
        out_shape=[_compact_shape(n, F32) for n in names],
        in_specs=[HBM] * nw, out_specs=[HBM] * nw,
        scratch_shapes=[pltpu.SemaphoreType.DMA((ncp,)), pltpu.SemaphoreType.DMA((ncp,))],
    )(*grads)


def _swap_start(grads, names, label):
    nw = len(names)
    ns = 2 * _n_swap_copies(names)

    def body(*refs):
        ins, lands = refs[:nw], refs[nw:2 * nw]
        sems = refs[2 * nw:2 * nw + ns]
        token = refs[4 * nw + ns]
        _, _, c, _, sibling, _ = _place()
        for k, (src, dst) in enumerate(_swap_pairs(ins, lands, names, c)):
            _remote(src, dst, sems[2 * k], sems[2 * k + 1], sibling).start()
        token[...] = jnp.zeros_like(token)

    lands = [_keep_in_hbm(lax.empty(_compact_shape(n, F32).shape, F32)) for n in names]
    res = pl.pallas_call(
        body, name=label,
        out_shape=(*[pltpu.SemaphoreType.DMA(())] * ns, *[pltpu.HBM(a.shape, a.dtype) for a in grads],
                   *[pltpu.HBM(a.shape, a.dtype) for a in lands], _sds((8, LANES), F32)),
        in_specs=[IN_HBM] * (2 * nw),
        out_specs=(*[SEM] * ns, *[IN_HBM] * (2 * nw), pl.BlockSpec(memory_space=pltpu.VMEM)),
        input_output_aliases={i: ns + i for i in range(2 * nw)},
        compiler_params=pltpu.CompilerParams(has_side_effects=DATAFLOW),
    )(*[_keep_in_hbm(a) for a in grads], *lands)
    return list(res[:ns]), list(res[ns:ns + nw]), list(res[ns + nw:ns + 2 * nw]), res[ns + 2 * nw]


def _swap_wait(sems, grads, lands, names, after, label):
    nw = len(names)
    ns = len(sems)

    def body(*refs):
        ins, land_refs = refs[:nw], refs[nw:2 * nw]
        sem_refs = refs[2 * nw:2 * nw + ns]
        _, _, c, _, sibling, _ = _place()
        for k, (src, dst) in enumerate(_swap_pairs(ins, land_refs, names, c)):
            cp = _remote(src, dst, sem_refs[2 * k], sem_refs[2 * k + 1], sibling)
            cp.wait_send()
            cp.wait_recv()

    res = pl.pallas_call(
        body, name=label,
        out_shape=tuple(pltpu.HBM(a.shape, a.dtype) for a in (*grads, *lands)),
        in_specs=[IN_HBM] * (2 * nw) + [SEM] * ns + [pl.BlockSpec(memory_space=pl.ANY)],
        out_specs=tuple([IN_HBM] * (2 * nw)),
        input_output_aliases={i: i for i in range(2 * nw)},
        compiler_params=pltpu.CompilerParams(has_side_effects=DATAFLOW),
    )(*grads, *lands, *sems, after)
    return list(res[:nw]), list(res[nw:])


def _add_halves(name, grad, got, core):
    kind, (r, c) = BIG[name]
    half = _half_rows(name)
    if kind == "col":
        t = 128
        grid = (half // t,)
        g_spec = pl.BlockSpec((t, c), lambda i, cr: (cr[0] * (half // t) + i, 0))
        o_spec = pl.BlockSpec((t, c), lambda i, cr: (i, 0))
    else:
        t = half
        grid = (N_CHIPS,)
        g_spec = pl.BlockSpec((t, c), lambda i, cr: (2 * i + cr[0], 0))
        o_spec = pl.BlockSpec((t, c), lambda i, cr: (i, 0))

    def body(c_ref, g_ref, b_ref, o_ref):
        del c_ref
        o_ref[...] = (g_ref[...] + b_ref[...]).astype(o_ref.dtype)

    return pl.pallas_call(
        body, name="grad_add_" + name,
        grid_spec=pltpu.PrefetchScalarGridSpec(num_scalar_prefetch=1, grid=grid, in_specs=[g_spec, o_spec], out_specs=o_spec),
        out_shape=_compact_shape(name, BF16),
        compiler_params=pltpu.CompilerParams(dimension_semantics=("parallel",), vmem_limit_bytes=VMEM_LIMIT),
    )(core, grad, got)


def _piece(ref, name, chip):
    kind, _ = BIG[name]
    r, c = _shard_shape(name)
    if kind == "col":
        return ref.at[:, pl.ds(chip * c, c)]
    return ref.at[pl.ds(chip * (r // 2), r // 2), :]


def _landing_shape(name):
    r, c = _shard_shape(name)
    return (N_CHIPS - 1, r // 2, c)


def _exchange_start(parts, names, label):
    nw = len(names)
    ns = 2 * 3 * nw

    def body(*refs):
        ins, lands = refs[:nw], refs[nw:2 * nw]
        sems = refs[2 * nw:2 * nw + ns]
        token = refs[4 * nw + ns]
        _, _, c, _, _, others = _place()
        for w, name in enumerate(names):
            for t, (ox, oy) in enumerate(others):
                k = 2 * (3 * w + t)
                _remote(_piece(ins[w], name, 2 * ox + oy), lands[w].at[t], sems[k], sems[k + 1], (ox, oy, c)).start()
        token[...] = jnp.zeros_like(token)

    lands = [_keep_in_hbm(lax.empty(_landing_shape(n), BF16)) for n in names]
    res = pl.pallas_call(
        body, name=label,
        out_shape=(*[pltpu.SemaphoreType.DMA(())] * ns, *[pltpu.HBM(a.shape, a.dtype) for a in parts],
                   *[pltpu.HBM(a.shape, a.dtype) for a in lands], _sds((8, LANES), F32)),
        in_specs=[IN_HBM] * (2 * nw),
        out_specs=(*[SEM] * ns, *[IN_HBM] * (2 * nw), pl.BlockSpec(memory_space=pltpu.VMEM)),
        input_output_aliases={i: ns + i for i in range(2 * nw)},
        compiler_params=pltpu.CompilerParams(has_side_effects=DATAFLOW),
    )(*[_keep_in_hbm(a) for a in parts], *lands)
    return list(res[:ns]), list(res[ns:ns + nw]), list(res[ns + nw:ns + 2 * nw]), res[ns + 2 * nw]


def _exchange_wait(sems, parts, lands, names, after, label):
    nw = len(names)
    ns = len(sems)

    def body(*refs):
        ins, land_refs = refs[:nw], refs[nw:2 * nw]
        sem_refs = refs[2 * nw:2 * nw + ns]
        _, _, c, _, _, others = _place()
        for w, name in enumerate(names):
            for t, (ox, oy) in enumerate(others):
                k = 2 * (3 * w + t)
                cp = _remote(_piece(ins[w], name, 2 * ox + oy), land_refs[w].at[t], sem_refs[k], sem_refs[k + 1], (ox, oy, c))
                cp.wait_send()
                cp.wait_recv()

    res = pl.pallas_call(
        body, name=label,
        out_shape=tuple(pltpu.HBM(a.shape, a.dtype) for a in (*parts, *lands)),
        in_specs=[IN_HBM] * (2 * nw) + [SEM] * ns + [pl.BlockSpec(memory_space=pl.ANY)],
        out_specs=tuple([IN_HBM] * (2 * nw)),
        input_output_aliases={i: i for i in range(2 * nw)},
        compiler_params=pltpu.CompilerParams(has_side_effects=DATAFLOW),
    )(*parts, *lands, *sems, after)
    return list(res[:nw]), list(res[nw:])


def _sum_chips(name, part, got, chip):
    kind, _ = BIG[name]
    _, r, c = got.shape
    t = _tile(r, (128, 352))
    if kind == "col":
        own = pl.BlockSpec((t, c), lambda i, ch: (i, ch[0]))
    else:
        own = pl.BlockSpec((t, c), lambda i, ch: (ch[0] * (r // t) + i, 0))

    def body(ch_ref, p_ref, g_ref, o_ref):
        del ch_ref
        acc = p_ref[...].astype(F32)
        for j in range(N_CHIPS - 1):
            acc = acc + g_ref[j].astype(F32)
        o_ref[...] = acc

    return _pallas(
        body, name="grad_sum_" + name, prefetch=1, grid=(r // t,),
        in_specs=[own, pl.BlockSpec((N_CHIPS - 1, t, c), lambda i, ch: (0, i, 0))],
        out_specs=pl.BlockSpec((t, c), lambda i, ch: (i, 0)),
        out_shape=_sds((r, c), F32), semantics=("parallel",),
    )(chip, part, got)


def _send_halves(sums, label):
    nw = len(sums)

    def body(*refs):
        ins, outs = refs[:nw], refs[nw:2 * nw]
        send_sems, recv_sems = refs[2 * nw:]
        _, _, _, _, sibling, _ = _place()
        copies = [_remote(ins[w], outs[w], send_sems.at[w], recv_sems.at[w], sibling) for w in range(nw)]
        for cp in copies:
            cp.start()
        for cp in copies:
            cp.wait()

    return pl.pallas_call(
        body, name=label,
        out_shape=[_sds(a.shape, a.dtype) for a in sums],
        in_specs=[HBM] * nw, out_specs=[HBM] * nw,
        scratch_shapes=[pltpu.SemaphoreType.DMA((nw,)), pltpu.SemaphoreType.DMA((nw,))],
    )(*sums)


EARLY_GRADS = ("w_up", "w_down")
LAST_GRADS = ("w_in", "w_out")


def _chip_partials(grads, names, core, label):
    got = _swap_halves(grads, names, label)
    return [_add_halves(n, grads[i], got[i], core) for i, n in enumerate(names)]


def _reduce_start(grads, names, core, tag):
    parts = _chip_partials(grads, names, core, "grad_swap_" + tag)
    return _exchange_start(parts, names, "grad_exchange_start_" + tag)


def _reduce_finish(started, names, after, chip, tag):
    sems, parts, lands, _ = started
    parts, lands = _exchange_wait(sems, parts, lands, names, after, "grad_exchange_wait_" + tag)
    return [_sum_chips(n, parts[i], lands[i], chip) for i, n in enumerate(names)]


HI = lax.Precision.HIGHEST
MOD_COLS = 6 * D // N_CHIPS
COND_ROWS = 16


def _silu(v):
    return v * _sigmoid(v)


GATHER_ROWS = 48
FFW_COLS = 2 * DFF // N_CHIPS
CONV_COLS = DC // N_CHIPS


def _pack_cond(c, ffn_w, conv_w):
    def body(c_ref, f_ref, w_ref, o_ref):
        o_ref[...] = jnp.zeros_like(o_ref)
        o_ref[0:1, 0:D] = c_ref[...]
        o_ref[8:11, :] = f_ref[...]
        o_ref[16:16 + CW, 0:CONV_COLS] = w_ref[...]

    return _pallas(body, name="pack_cond", out_shape=_sds((GATHER_ROWS, FFW_COLS), F32))(c, ffn_w, conv_w)


def _unpack_cond(got, c_ctx):
    def body(g_ref, c_ref, cond_ref, f_ref, w_ref):
        cond_ref[...] = jnp.zeros_like(cond_ref)
        for d in range(8):
            cond_ref[d:d + 1, :] = g_ref[d * GATHER_ROWS:d * GATHER_ROWS + 1, 0:D]
        cond_ref[8:9, :] = c_ref[...]
        for j in range(N_CHIPS):
            r0 = 2 * j * GATHER_ROWS
            f_ref[:, j * FFW_COLS:(j + 1) * FFW_COLS] = g_ref[r0 + 8:r0 + 11, :]
            w_ref[:, j * CONV_COLS:(j + 1) * CONV_COLS] = g_ref[r0 + 16:r0 + 16 + CW, 0:CONV_COLS]

    return _pallas(body, name="unpack_cond",
                   out_shape=[_sds((COND_ROWS, D), F32), _sds((3, 2 * DFF), F32), _sds((CW, DC), F32)])(got, c_ctx)


def _chip_cols(rows, width):
    return pl.BlockSpec((rows, width), lambda i, ch: (0, ch[0]))


def _whole(shape):
    return pl.BlockSpec(shape, lambda i, ch: (0,) * len(shape))


def _mod_shard(cond, w_mod, b_mod, chip):
    def body(ch_ref, c_ref, w_ref, b_ref, o_ref):
        del ch_ref
        o_ref[...] = jnp.dot(_silu(c_ref[...]), w_ref[...], preferred_element_type=F32, precision=HI) + b_ref[...]

    return _pallas(body, name="mod_fwd", prefetch=1, grid=(1,),
                   in_specs=[_whole((COND_ROWS, D)), _whole((D, MOD_COLS)), _chip_cols(1, MOD_COLS)],
                   out_specs=_whole((COND_ROWS, MOD_COLS)),
                   out_shape=_sds((COND_ROWS, MOD_COLS), F32))(chip, cond, w_mod, b_mod)


def _unpack_mod(mods, dev):
    def body(dev_ref, m_ref, me_ref, c_ref):
        rowi = lax.broadcasted_iota(jnp.int32, (COND_ROWS, MOD_COLS), 0)
        mine, ctx = [], []
        for j in range(N_CHIPS):
            blk = m_ref[2 * j * COND_ROWS:(2 * j + 1) * COND_ROWS, :]
            mine.append(jnp.sum(jnp.where(rowi == dev_ref[0], blk, 0.0), axis=0, keepdims=True))
            ctx.append(blk[8:9, :])
        mine = jnp.concatenate(mine, axis=1)
        ctx = jnp.concatenate(ctx, axis=1)
        for k in range(6):
            me_ref[k:k + 1, :] = mine[:, k * D:(k + 1) * D]
        for k in range(2):
            c_ref[k:k + 1, :] = ctx[:, k * D:(k + 1) * D]

    return _pallas(body, name="unpack_mod", prefetch=1, grid=(1,),
                   in_specs=[_whole(mods.shape)], out_specs=[_whole((6, D)), _whole((2, D))],
                   out_shape=[_sds((6, D), F32), _sds((2, D), F32)])(dev, mods)


def _mod_weight_grad(cond, dmod_all, chip):
    def body(ch_ref, c_ref, d_ref, o_ref):
        del ch_ref
        o_ref[...] = lax.dot_general(_silu(c_ref[...]), d_ref[...], _TN, preferred_element_type=F32, precision=HI)

    return _pallas(body, name="mod_weight_grad", prefetch=1, grid=(1,),
                   in_specs=[_whole((COND_ROWS, D)), _chip_cols(COND_ROWS, MOD_COLS)], out_specs=_whole((D, MOD_COLS)),
                   out_shape=_sds((D, MOD_COLS), F32))(chip, cond, dmod_all)


def _cond_grad_partial(dmod_all, w_mod, chip):
    def body(ch_ref, d_ref, w_ref, o_ref):
        del ch_ref
        o_ref[...] = lax.dot_general(d_ref[...], w_ref[...], (((1,), (1,)), ((), ())), preferred_element_type=F32, precision=HI)

    return _pallas(body, name="cond_grad_partial", prefetch=1, grid=(1,),
                   in_specs=[pl.BlockSpec((8, MOD_COLS), lambda i, ch: (1, ch[0])), _whole((D, MOD_COLS))],
                   out_specs=_whole((8, D)), out_shape=_sds((8, D), F32))(chip, dmod_all, w_mod)


def _adam_math(w, g, m, v):
    nm = ADAM_B1 * m + (1.0 - ADAM_B1) * g
    nv = ADAM_B2 * v + (1.0 - ADAM_B2) * (g * g)
    c1 = 1.0 - ADAM_B1 ** ADAM_STEP
    c2 = 1.0 - ADAM_B2 ** ADAM_STEP
    return -ADAM_LR * ((nm / c1) / (jnp.sqrt(nv / c2) + ADAM_EPS) + ADAM_WD * w), nm, nv


def _cond_update(parts, c_ctx, m, v):
    def body(p_ref, c_ref, m_ref, v_ref, g_ref, d_ref, nm_ref, nv_ref):
        tot = p_ref[0:1, :]
        for j in range(1, N_CHIPS):
            tot = tot + p_ref[16 * j:16 * j + 1, :]
        cv = c_ref[...]
        sg = _sigmoid(cv)
        g = tot * (sg * (1.0 + cv * (1.0 - sg)))
        g_ref[...] = g
        d_ref[...], nm_ref[...], nv_ref[...] = _adam_math(cv, g, m_ref[...], v_ref[...])

    return _pallas(body, name="cond_update", out_shape=[_sds((1, D), F32)] * 4)(parts, c_ctx, m, v)


def _adamw(w, g, m, v, name):
    r, c = w.shape
    t = _tile(r, (128,)) if r % 128 == 0 and r > 128 else r

    def body(w_ref, g_ref, m_ref, v_ref, d_ref, nm_ref, nv_ref):
        d_ref[...], nm_ref[...], nv_ref[...] = _adam_math(w_ref[...], g_ref[...], m_ref[...], v_ref[...])

    blk = pl.BlockSpec((t, c), lambda i: (i, 0))
    return _pallas(body, name=name, grid=(r // t,), in_specs=[blk] * 4, out_specs=[blk] * 3,
                   out_shape=[_sds((r, c), F32)] * 3, semantics=("parallel",))(w, g, m, v)


def _adamw_cols(w, g_all, m, v, chip, name):
    r, c = w.shape

    def body(ch_ref, w_ref, g_ref, m_ref, v_ref, go_ref, d_ref, nm_ref, nv_ref):
        del ch_ref
        g = g_ref[...]
        go_ref[...] = g
        d_ref[...], nm_ref[...], nv_ref[...] = _adam_math(w_ref[...], g, m_ref[...], v_ref[...])

    return _pallas(body, name=name, prefetch=1, grid=(1,),
                   in_specs=[_whole((r, c)), _chip_cols(r, c), _whole((r, c)), _whole((r, c))],
                   out_specs=[_whole((r, c))] * 4, out_shape=[_sds((r, c), F32)] * 4)(chip, w, g_all, m, v)


def _adamw_halves(name, w, own, other, m, v, core, after):
    r, c = w.shape
    half = r // 2
    t = _tile(half, (128, 352))
    nh = half // t

    def pick(mine):
        def index(i, cr):
            first = cr[0] if mine else 1 - cr[0]
            return (jnp.clip(i - first * nh, 0, nh - 1), 0)
        return pl.BlockSpec((t, c), index)

    def body(c_ref, w_ref, own_ref, oth_ref, m_ref, v_ref, after_ref, g_ref, d_ref, nm_ref, nv_ref):
        del after_ref
        g = jnp.where(pl.program_id(0) // nh == c_ref[0], own_ref[...], oth_ref[...])
        g_ref[...] = g
        d_ref[...], nm_ref[...], nv_ref[...] = _adam_math(w_ref[...], g, m_ref[...], v_ref[...])

    blk = pl.BlockSpec((t, c), lambda i, cr: (i, 0))
    return _pallas(body, name="adamw_" + name, prefetch=1, grid=(2 * nh,),
                   in_specs=[blk, pick(True), pick(False), blk, blk, pl.BlockSpec(memory_space=pl.ANY)], out_specs=[blk] * 4,
                   out_shape=[_sds((r, c), F32)] * 4, semantics=("parallel",))(core, w, own, other, m, v, after)


WEIGHTS = ("c_ctx", "w_mod", "b_mod", "g_norm1", "w_in", "rpb", "conv_w", "conv_b", "ln_g", "ln_b", "w_out", "g_norm2",
           "w_up", "ffn_conv_w", "ffn_conv_b", "w_down", "g_final")
PACK = (("dmod", 6 * D), ("dmod_c", 2 * D), ("g_norm1", D), ("g_norm1_ctx", D), ("g_norm2", D), ("g_final", D),
        ("conv_b", DC), ("ln_g", DC), ("ln_b", DC), ("ffn_conv_b", 2 * DFF), ("ffn_conv_w", 3 * 2 * DFF),
        ("conv_w", CW * DC), ("rpb_rev", NH * 16 * LANES), ("loss", LANES))
PACK_OFF = {}
_o = 0
for _n, _w in PACK:
    PACK_OFF[_n] = (_o, _w)
    _o += _w
PACK_N = -(-_o // (8 * LANES)) * (8 * LANES)
VECTORS = {"b_mod": (6 * D, ("dmod", "dmod_c")), "g_norm1": (D, ("g_norm1", "g_norm1_ctx")), "conv_b": (DC, ("conv_b",)),
           "ln_g": (DC, ("ln_g",)), "ln_b": (DC, ("ln_b",)), "g_norm2": (D, ("g_norm2",)),
           "ffn_conv_b": (2 * DFF, ("ffn_conv_b",)), "g_final": (D, ("g_final",))}
RPB_ROWS = NH * (2 * NA_ROWS - 1)
RPB_COLS = 4 * NA_ROWS - 1


def _pack_small(parts):
    arrs, places = [], []
    for name, _ in PACK:
        off, width = PACK_OFF[name]
        group = parts[name]
        rows = group[0].shape[0]
        row_w = sum(a.shape[1] for a in group)
        assert rows * row_w == width, (name, rows, row_w, width)
        col = 0
        for a in group:
            arrs.append(a)
            places.append([off + k * row_w + col for k in range(rows)])
            col += a.shape[1]

    def body(*refs):
        o_ref = refs[-1]
        o_ref[:, _o:PACK_N] = jnp.zeros((1, PACK_N - _o), F32)
        for ref, offs in zip(refs, places):
            n = ref.shape[1]
            for k, off in enumerate(offs):
                o_ref[:, off:off + n] = ref[k:k + 1, :]

    return _pallas(body, name="pack_small_grads", out_shape=_sds((1, PACK_N), F32))(*arrs)


def _small_update(packs, w, m, v):
    names = list(VECTORS)

    def body(*refs):
        it = iter(refs)
        p_ref = next(it)
        wmv = {n: (next(it), next(it), next(it)) for n in names}
        outs = {n: (next(it), next(it), next(it), next(it)) for n in names}
        dmod_ref, cw_ref, fw_ref, rpb_ref, loss_ref = next(it), next(it), next(it), next(it), next(it)

        def total(name):
            off, width = PACK_OFF[name]
            acc = p_ref[0:1, off:off + width]
            for d in range(1, 8):
                acc = acc + p_ref[d:d + 1, off:off + width]
            return acc

        for n in names:
            width, segs = VECTORS[n]
            g = total(segs[0])
            if len(segs) > 1:
                extra = total(segs[1])
                ew = extra.shape[1]
                g = g + extra if ew == width else jnp.concatenate([g[:, :ew] + extra, g[:, ew:]], axis=1)
            w_ref, m_ref, v_ref = wmv[n]
            g_ref, d_ref, nm_ref, nv_ref = outs[n]
            g_ref[...] = g
            d_ref[...], nm_ref[...], nv_ref[...] = _adam_math(w_ref[...], g, m_ref[...], v_ref[...])

        o_dmod = PACK_OFF["dmod"][0]
        dmod_ref[...] = jnp.zeros_like(dmod_ref)
        dmod_ref[0:8, :] = p_ref[:, o_dmod:o_dmod + 6 * D]
        dmod_ref[8:9, 0:2 * D] = total("dmod_c")
        for ref, name, rows in ((cw_ref, "conv_w", CW), (fw_ref, "ffn_conv_w", 3), (rpb_ref, "rpb_rev", NH * 16)):
            flat = total(name)
            n = ref.shape[1]
            for k in range(rows):
                ref[k:k + 1, :] = flat[:, k * n:(k + 1) * n]
        loss_ref[...] = total("loss")

    ins = [packs] + [a[n] for n in names for a in (w, m, v)]
    out_shape = [_sds((1, VECTORS[n][0]), F32) for n in names for _ in range(4)]
    out_shape += [_sds((COND_ROWS, 6 * D), F32), _sds((CW, DC), F32), _sds((3, 2 * DFF), F32), _sds((NH * 16, LANES), F32),
                  _sds((1, LANES), F32)]
    res = _pallas(body, name="small_update", out_shape=out_shape)(*ins)
    per = {n: tuple(res[4 * i:4 * i + 4]) for i, n in enumerate(names)}
    return (per, *res[4 * len(names):])


def _rpb_update(rev, w, m, v):
    def body(r_ref, w_ref, m_ref, v_ref, g_ref, d_ref, nm_ref, nv_ref):
        li = lax.broadcasted_iota(jnp.int32, (LANES, LANES), 0)
        co = lax.broadcasted_iota(jnp.int32, (LANES, LANES), 1)
        lane_of_co0 = GW - 1 + RPB_COLS // 2
        unflip = jnp.where((li == lane_of_co0 - co) & (co < RPB_COLS), 1.0, 0.0).astype(F32)
        g_all = jnp.dot(r_ref[...], unflip, preferred_element_type=F32, precision=HI)
        nr = 2 * NA_ROWS - 1
        for h in range(NH):
            rows = slice(h * nr, (h + 1) * nr)
            g = g_all[h * 16:h * 16 + nr, 0:RPB_COLS]
            g_ref[rows, :] = g
            d_ref[rows, :], nm_ref[rows, :], nv_ref[rows, :] = _adam_math(w_ref[rows, :], g, m_ref[rows, :], v_ref[rows, :])

    return _pallas(body, name="rpb_update", out_shape=[_sds((RPB_ROWS, RPB_COLS), F32)] * 4)(rev, w, m, v)


def kernel(x, c, ctx, c_ctx, w_mod, b_mod, g_norm1, w_in, rpb, conv_w, conv_b, ln_g, ln_b, w_out, g_norm2, w_up, ffn_conv_w, ffn_conv_b, w_down, g_final, loss_target, m_c_ctx, m_w_mod, m_b_mod, m_g_norm1, m_w_in, m_rpb, m_conv_w, m_conv_b, m_ln_g, m_ln_b, m_w_out, m_g_norm2, m_w_up, m_ffn_conv_w, m_ffn_conv_b, m_w_down, m_g_final, v_c_ctx, v_w_mod, v_b_mod, v_g_norm1, v_w_in, v_rpb, v_conv_w, v_conv_b, v_ln_g, v_ln_b, v_w_out, v_g_norm2, v_w_up, v_ffn_conv_w, v_ffn_conv_b, v_w_down, v_g_final):
    w = dict(c_ctx=c_ctx, w_mod=w_mod, b_mod=b_mod, g_norm1=g_norm1, w_in=w_in, rpb=rpb, conv_w=conv_w, conv_b=conv_b,
             ln_g=ln_g, ln_b=ln_b, w_out=w_out, g_norm2=g_norm2, w_up=w_up, ffn_conv_w=ffn_conv_w, ffn_conv_b=ffn_conv_b,
             w_down=w_down, g_final=g_final)
    mom = dict(c_ctx=m_c_ctx, w_mod=m_w_mod, b_mod=m_b_mod, g_norm1=m_g_norm1, w_in=m_w_in, rpb=m_rpb, conv_w=m_conv_w,
               conv_b=m_conv_b, ln_g=m_ln_g, ln_b=m_ln_b, w_out=m_w_out, g_norm2=m_g_norm2, w_up=m_w_up,
               ffn_conv_w=m_ffn_conv_w, ffn_conv_b=m_ffn_conv_b, w_down=m_w_down, g_final=m_g_final)
    var = dict(c_ctx=v_c_ctx, w_mod=v_w_mod, b_mod=v_b_mod, g_norm1=v_g_norm1, w_in=v_w_in, rpb=v_rpb, conv_w=v_conv_w,
               conv_b=v_conv_b, ln_g=v_ln_g, ln_b=v_ln_b, w_out=v_w_out, g_norm2=v_g_norm2, w_up=v_w_up,
               ffn_conv_w=v_ffn_conv_w, ffn_conv_b=v_ffn_conv_b, w_down=v_w_down, g_final=v_g_final)
    xi, yi, ci = lax.axis_index("x"), lax.axis_index("y"), lax.axis_index("c")
    dev = (4 * xi + 2 * yi + ci).astype(jnp.int32).reshape(1)
    chip = (2 * xi + yi).astype(jnp.int32).reshape(1)
    core = ci.astype(jnp.int32).reshape(1)
    c_ctx2 = c_ctx.reshape(1, D)
    g_final2 = g_final.reshape(1, D)
    mom["g_final"], var["g_final"] = m_g_final.reshape(1, D), v_g_final.reshape(1, D)

    got = _gather_small(_pack_cond(c, ffn_conv_w[0], conv_w[0]), "gather_cond")
    cond, ffn_w_all, conv_w_all = _unpack_cond(got, c_ctx2)

    mods = _gather_small(_mod_shard(cond, w_mod[0], b_mod, chip), "gather_mod")
    mod_me, mod_c = _unpack_mod(mods, dev)

    shards = {n: _cast_into_whole(n, w[n][0], chip) for n in BIG_NAMES}
    (w_in_all,) = _gather_weights([shards["w_in"]], ("w_in",), "gather_w_in", after=mod_me)
    sems, late, token = _gather_start([shards[n] for n in LATE_NAMES], LATE_NAMES, after=w_in_all)
    mod_me = mod_me + token[0:1, 0:1]

    def late_weights(after):
        arrived = _gather_wait(sems, late, LATE_NAMES, after)
        return _forward_halves(list(arrived), LATE_NAMES)

    rpb_rev = jnp.pad(rpb[0][:, :, ::-1], ((0, 0), (0, 1), (48, LANES - 48 - RPB_COLS))).reshape(NH * 16, LANES)
    vec = dict(g_norm1=g_norm1, g_norm2=g_norm2, g_final=g_final2, conv_w=conv_w_all, conv_b=conv_b, ln_g=ln_g, ln_b=ln_b,
               ffn_conv_w=ffn_w_all, ffn_conv_b=ffn_conv_b)
    started = []

    def begin_early(d_up, d_down):
        started.append(_swap_start([d_up, d_down], EARLY_GRADS, "grad_swap_start_early"))

    def carry_on_early(after):
        sems_, grads_, lands_, _ = started.pop()
        grads_, lands_ = _swap_wait(sems_, grads_, lands_, EARLY_GRADS, after, "grad_swap_wait_early")
        parts_ = [_add_halves(n, grads_[i], lands_[i], core) for i, n in enumerate(EARLY_GRADS)]
        started.append(_exchange_start(parts_, EARLY_GRADS, "grad_exchange_start_early"))
        return started[0][3][0:1, 0:1]

    loss_p, grad_x, d_in, d_out, d_up, d_down, small = _local_step(
        x[0], ctx[0], loss_target[0], mod_me, mod_c, vec, w_in_all, late_weights, rpb_rev, (begin_early, carry_on_early))

    out = {}
    early_own = _reduce_finish(started[0], EARLY_GRADS, grad_x, chip, "early")
    last_started = _reduce_start([d_in, d_out], LAST_GRADS, core, "last")
    early_other = _send_halves(early_own, "grad_send_early")
    for i, n in enumerate(EARLY_GRADS):
        out[n] = _adamw_halves(n, w[n][0], early_own[i], early_other[i], mom[n][0], var[n][0], core, last_started[3])

    parts = dict(dmod=small["dmod"], dmod_c=small["dmod_c"], g_norm1=[small["g_norm1"][0]], g_norm1_ctx=[small["g_norm1"][1]],
                 g_norm2=[small["g_norm2"]], g_final=[small["g_final"]], conv_b=[small["conv_b"]], ln_g=[small["ln_g"]],
                 ln_b=[small["ln_b"]], ffn_conv_b=small["ffn_conv_b"], ffn_conv_w=small["ffn_conv_w"],
                 conv_w=[small["conv_w"]], rpb_rev=[small["rpb_rev"]], loss=[loss_p])
    pack = _pack_small(parts).reshape(8, PACK_N // 8)
    packs = _gather_small(pack, "gather_small_grads").reshape(8, PACK_N)
    w2 = dict(w, g_final=g_final2)
    per, dmod_all, g_conv_w_all, g_ffn_w_all, g_rpb_rev, loss_row = _small_update(packs, w2, mom, var)

    out.update(per)
    out["c_ctx"] = _cond_update(
        _gather_small(_cond_grad_partial(dmod_all, w_mod[0], chip), "gather_cond_grad"),
        c_ctx2, m_c_ctx.reshape(1, D), v_c_ctx.reshape(1, D))
    g_w_mod = _mod_weight_grad(cond, dmod_all, chip)
    out["w_mod"] = (g_w_mod, *_adamw(w_mod[0], g_w_mod, m_w_mod[0], v_w_mod[0], "adamw_w_mod"))
    behind = out["w_mod"][1][0:1, 0:1] + out["c_ctx"][1][0:1, 0:1]
    last_own = _reduce_finish(last_started, LAST_GRADS, behind, chip, "last")
    last_other = _send_halves(last_own, "grad_send_last")
    for i, n in enumerate(LAST_GRADS):
        out[n] = _adamw_halves(n, w[n][0], last_own[i], last_other[i], mom[n][0], var[n][0], core, last_other[i])
    out["conv_w"] = _adamw_cols(conv_w[0], g_conv_w_all, m_conv_w[0], v_conv_w[0], chip, "adamw_conv_w")
    out["ffn_conv_w"] = _adamw_cols(ffn_conv_w[0], g_ffn_w_all, m_ffn_conv_w[0], v_ffn_conv_w[0], chip, "adamw_ffn_conv_w")
    flat = lambda a: a.reshape(RPB_ROWS, RPB_COLS)
    out["rpb"] = _rpb_update(g_rpb_rev, flat(rpb), flat(m_rpb), flat(v_rpb))

    res = [[out[n][k].reshape(w[n].shape) for n in WEIGHTS] for k in range(4)]
    return (loss_row[0, 0], grad_x[None], *res[0], *res[1], *res[2], *res[3])
```

```python
import functools

import jax
import jax.numpy as jnp
from jax import lax
from jax.experimental import pallas as pl
from jax.experimental.pallas import tpu as pltpu

F32 = jnp.float32
BF16 = jnp.bfloat16
MXU_DTYPE = jnp.bfloat16

D = 1024
CTX = 256
GW = 64
DA = 512
NH = 8
HD = 64
DC = 512
CW = 31
DFF = 2816
NIN = 3 * DA + 2 * DC
EPS = 1e-6
SCALE = HD ** -0.5
NEG = -1e30
NA_ROWS = 8
PAIR_ROWS = NA_ROWS + 1
TAB_BLOCKS = 17
LANES = 128
VMEM_LIMIT = 56 * 1024 * 1024

ADAM_LR = 0.001
ADAM_B1 = 0.9
ADAM_B2 = 0.999
ADAM_EPS = 1e-08
ADAM_WD = 0.01
ADAM_STEP = 10

MESH = pl.DeviceIdType.MESH


def _pallas(body, *, name, semantics=None, vmem=VMEM_LIMIT, prefetch=0, **kw):
    params = dict(vmem_limit_bytes=vmem)
    if semantics is not None:
        params["dimension_semantics"] = semantics
    if prefetch:
        kw["grid_spec"] = pltpu.PrefetchScalarGridSpec(
            num_scalar_prefetch=prefetch, grid=kw.pop("grid"), in_specs=kw.pop("in_specs"), out_specs=kw.pop("out_specs"),
            scratch_shapes=kw.pop("scratch_shapes", ()))
    return pl.pallas_call(body, name=name, compiler_params=pltpu.CompilerParams(**params), **kw)


def _sds(shape, dtype):
    return jax.ShapeDtypeStruct(shape, dtype)


def _vec_spec(n):
    return pl.BlockSpec((1, n), lambda *_: (0, 0))


def _colsum8(x):
    t, n = x.shape
    return jnp.sum(x.reshape(t // 8, 8, n), axis=0)


def _sigmoid(x):
    return 0.5 * jnp.tanh(0.5 * x) + 0.5


def _pieces(arrs, tile):
    lo, out = 0, []
    for a in arrs:
        nt = a.shape[1] // tile
        assert nt * tile == a.shape[1], (a.shape, tile)
        out.append((lo, nt))
        lo += nt
    return out


def _mm(a, b, *, mode, m, n, k, tm, tn, tk, out_dtype, name, a_off=(0, 0), b_off=(0, 0), k_outer=False,
        out_total=None, o_off=(0, 0), into=None):
    a_list = list(a) if isinstance(a, (list, tuple)) else [a]
    b_list = list(b) if isinstance(b, (list, tuple)) else [b]
    assert m % tm == 0 and n % tn == 0 and k % tk == 0, (name, m, n, k, tm, tn, tk)
    gi, gj, nk = m // tm, n // tn, k // tk
    a_tile = tm if mode == "tn" else tk
    a_pc = _pieces(a_list, a_tile) if len(a_list) > 1 else [(0, 1 << 30)]
    if mode == "nt":
        assert len(b_list) == 1
    b_pc = _pieces(b_list, tn) if len(b_list) > 1 else [(0, 1 << 30)]
    dims = {"nn": (((1,), (0,)), ((), ())), "nt": (((1,), (1,)), ((), ())), "tn": (((0,), (0,)), ((), ()))}[mode]
    k_outer = k_outer and nk > 1

    def ijk(fn):
        return (lambda i, kk, j: fn(i, j, kk)) if k_outer else fn

    def a_spec(lo, cnt):
        def loc(idx):
            return idx + a_off[1] if len(a_list) == 1 else jnp.clip(idx - lo, 0, cnt - 1)
        if mode == "tn":
            return pl.BlockSpec((tk, tm), ijk(lambda i, j, kk: (kk + a_off[0], loc(i))))
        return pl.BlockSpec((tm, tk), ijk(lambda i, j, kk: (i + a_off[0], loc(kk))))

    def b_spec(lo, cnt):
        def loc(idx):
            return idx + b_off[1] if len(b_list) == 1 else jnp.clip(idx - lo, 0, cnt - 1)
        if mode == "nt":
            return pl.BlockSpec((tn, tk), ijk(lambda i, j, kk: (j + b_off[0], kk + b_off[1])))
        return pl.BlockSpec((tk, tn), ijk(lambda i, j, kk: (kk + b_off[0], loc(j))))

    na, nb = len(a_list), len(b_list)
    in_place = nk > 1 and out_dtype == F32 and not k_outer

    n_in = na + nb + (into is not None)

    def body(*refs):
        a_refs, b_refs, o_ref = refs[:na], refs[na:na + nb], refs[n_in]
        if k_outer:
            i, kk, j = pl.program_id(0), pl.program_id(1), pl.program_id(2)
            acc = refs[n_in + 1].at[j]
        else:
            i, j, kk = pl.program_id(0), pl.program_id(1), pl.program_id(2)
            acc = o_ref if in_place else (refs[n_in + 1] if nk > 1 else None)
        a_idx = i if mode == "tn" else kk

        def step(ar, br):
            p = lax.dot_general(ar[...].astype(MXU_DTYPE), br[...].astype(MXU_DTYPE), dims,
                                preferred_element_type=F32)
            if nk == 1:
                o_ref[...] = p.astype(out_dtype)
                return

            @pl.when(kk == 0)
            def _():
                acc[...] = p

            @pl.when(kk > 0)
            def _():
                acc[...] += p

            if not in_place:
                @pl.when(kk == nk - 1)
                def _():
                    o_ref[...] = acc[...].astype(out_dtype)

        for pa, (alo, acnt) in enumerate(a_pc):
            for pb, (blo, bcnt) in enumerate(b_pc):
                if na == 1 and nb == 1:
                    step(a_refs[0], b_refs[0])
                else:
                    cond = (a_idx >= alo) & (a_idx < alo + acnt) & (j >= blo) & (j < blo + bcnt)
                    pl.when(cond)(functools.partial(step, a_refs[pa], b_refs[pb]))

    if k_outer:
        grid = (gi, nk, gj)
        o_spec = pl.BlockSpec((tm, tn), lambda i, kk, j: (i + o_off[0], jnp.where(kk == nk - 1, j, 0) + o_off[1]))
        scratch = [pltpu.VMEM((gj, tm, tn), F32)]
        semantics = ("parallel", "arbitrary", "arbitrary")
    else:
        grid = (gi, gj, nk)
        o_spec = pl.BlockSpec((tm, tn), lambda i, j, kk: (i + o_off[0], j + o_off[1]))
        scratch = [pltpu.VMEM((tm, tn), F32)] if nk > 1 and not in_place else []
        semantics = ("parallel", "parallel", "arbitrary")
    ins = [*a_list, *b_list]
    in_specs = [a_spec(*p) for p in a_pc] + [b_spec(*p) for p in b_pc]
    extra = {}
    if into is not None:
        extra["input_output_aliases"] = {len(ins): 0}
        ins.append(into)
        in_specs.append(pl.BlockSpec(memory_space=pl.ANY))
    return _pallas(
        body, name=name, grid=grid, in_specs=in_specs,
        out_specs=o_spec, out_shape=_sds(out_total or (m, n), out_dtype), scratch_shapes=scratch, semantics=semantics,
        **extra,
    )(*ins)


ROW_TILE = 256


def _rmsmod_fwd(x, ctx, g, sc, sh, csc, csh):
    s = x.shape[0]
    nt = s // ROW_TILE
    assert ctx.shape[0] == ROW_TILE

    def body(x_ref, c_ref, g_ref, sc_ref, sh_ref, csc_ref, csh_ref, o_ref):
        is_ctx = pl.program_id(0) == nt
        xv = jnp.where(is_ctx, c_ref[...], x_ref[...])
        scv = jnp.where(is_ctx, csc_ref[...], sc_ref[...])
        shv = jnp.where(is_ctx, csh_ref[...], sh_ref[...])
        r = lax.rsqrt(jnp.mean(xv * xv, axis=-1, keepdims=True) + EPS)
        y = xv * r * g_ref[...]
        o_ref[...] = (y * (1.0 + scv) + shv).astype(o_ref.dtype)

    return _pallas(
        body, name="rmsmod1_fwd", grid=(nt + 1,),
        in_specs=[pl.BlockSpec((ROW_TILE, D), lambda i: (jnp.minimum(i, nt - 1), 0)),
                  pl.BlockSpec((ROW_TILE, D), lambda i: (0, 0))] + [_vec_spec(D)] * 5,
        out_specs=pl.BlockSpec((ROW_TILE, D), lambda i: (i, 0)),
        out_shape=_sds((s + CTX, D), MXU_DTYPE),
        semantics=("arbitrary",),
    )(x, ctx, g, sc, sh, csc, csh)


def _resid_rmsmod_fwd(x, y, gt, g, sc, sh):
    s = x.shape[0]

    def body(x_ref, y_ref, gt_ref, g_ref, sc_ref, sh_ref, x1_ref, h_ref):
        x1 = x_ref[...] + gt_ref[...] * y_ref[...]
        x1_ref[...] = x1
        r = lax.rsqrt(jnp.mean(x1 * x1, axis=-1, keepdims=True) + EPS)
        h_ref[...] = ((x1 * r * g_ref[...]) * (1.0 + sc_ref[...]) + sh_ref[...]).astype(h_ref.dtype)

    row = pl.BlockSpec((ROW_TILE, D), lambda i: (i, 0))
    return _pallas(
        body, name="resid_rmsmod2_fwd", grid=(s // ROW_TILE,),
        in_specs=[row, row] + [_vec_spec(D)] * 4,
        out_specs=[row, row],
        out_shape=[_sds((s, D), F32), _sds((s, D), MXU_DTYPE)],
        semantics=("parallel",),
    )(x, y, gt, g, sc, sh)


def _final_fwd_bwd(x1, z, gt2, gf, tgt):
    s = x1.shape[0]
    nt = s // ROW_TILE

    def body(x1_ref, z_ref, gt_ref, gf_ref, t_ref, dx2_ref, dz_ref, loss_ref, dgt_ref, dgf_ref, a_loss, a_gt, a_gf):
        i = pl.program_id(0)

        @pl.when(i == 0)
        def _():
            a_loss[...] = jnp.zeros_like(a_loss)
            a_gt[...] = jnp.zeros_like(a_gt)
            a_gf[...] = jnp.zeros_like(a_gf)

        zv = z_ref[...]
        gt = gt_ref[...]
        gf_ = gf_ref[...]
        x2 = x1_ref[...] + gt * zv
        r = lax.rsqrt(jnp.mean(x2 * x2, axis=-1, keepdims=True) + EPS)
        xn = x2 * r
        e = xn * gf_ - t_ref[...]
        a_loss[...] += _colsum8(e * e)
        dyo = e * (1.0 / D)
        a_gf[...] += _colsum8(dyo * xn)
        gdy = gf_ * dyo
        dx2 = r * gdy - xn * (r * r) * jnp.mean(x2 * gdy, axis=-1, keepdims=True)
        dx2_ref[...] = dx2
        dz_ref[...] = (gt * dx2).astype(dz_ref.dtype)
        a_gt[...] += _colsum8(dx2 * zv)

        @pl.when(i == nt - 1)
        def _():
            tot = jnp.sum(jnp.sum(a_loss[...], axis=0, keepdims=True), axis=1, keepdims=True) * (0.5 / D)
            loss_ref[...] = jnp.broadcast_to(tot, loss_ref.shape)
            dgt_ref[...] = jnp.sum(a_gt[...], axis=0, keepdims=True)
            dgf_ref[...] = jnp.sum(a_gf[...], axis=0, keepdims=True)

    row = pl.BlockSpec((ROW_TILE, D), lambda i: (i, 0))
    return _pallas(
        body, name="final_norm_loss", grid=(nt,),
        in_specs=[row, row, _vec_spec(D), _vec_spec(D), row],
        out_specs=[row, row, _vec_spec(LANES), _vec_spec(D), _vec_spec(D)],
        out_shape=[_sds((s, D), F32), _sds((s, D), MXU_DTYPE), _sds((1, LANES), F32), _sds((1, D), F32), _sds((1, D), F32)],
        scratch_shapes=[pltpu.VMEM((8, D), F32)] * 3,
        semantics=("arbitrary",),
    )(x1, z, gt2, gf, tgt)


def _rmsmod_bwd(xin, dh, g, sc, *, name, dh_row_off=0, add=None, resid=None):
    s = xin.shape[0]
    nt = s // ROW_TILE
    want_dx = add is not None
    assert resid is None or want_dx

    def body(*refs):
        it = iter(refs)
        x_ref, dh_ref, g_ref, sc_ref = next(it), next(it), next(it), next(it)
        add_ref = next(it) if want_dx else None
        gt_ref, y_ref = (next(it), next(it)) if resid is not None else (None, None)
        dsh_ref, dsc_ref, dg_ref = next(it), next(it), next(it)
        dx_ref = next(it) if want_dx else None
        dy_ref, dgt_ref = (next(it), next(it)) if resid is not None else (None, None)
        a_sh, a_sc, a_g = next(it), next(it), next(it)
        a_gt = next(it) if resid is not None else None
        i = pl.program_id(0)

        @pl.when(i == 0)
        def _():
            a_sh[...] = jnp.zeros_like(a_sh)
            a_sc[...] = jnp.zeros_like(a_sc)
            a_g[...] = jnp.zeros_like(a_g)
            if a_gt is not None:
                a_gt[...] = jnp.zeros_like(a_gt)

        xv = x_ref[...]
        dhv = dh_ref[...]
        gv = g_ref[...]
        r = lax.rsqrt(jnp.mean(xv * xv, axis=-1, keepdims=True) + EPS)
        xn = xv * r
        a_sh[...] += _colsum8(dhv)
        a_sc[...] += _colsum8(dhv * (xn * gv))
        dn = dhv * (1.0 + sc_ref[...])
        a_g[...] += _colsum8(dn * xn)
        if want_dx:
            gdn = gv * dn
            dx = add_ref[...] + r * gdn - xn * (r * r) * jnp.mean(xv * gdn, axis=-1, keepdims=True)
            dx_ref[...] = dx
            if resid is not None:
                dy_ref[...] = (gt_ref[...] * dx).astype(dy_ref.dtype)
                a_gt[...] += _colsum8(dx * y_ref[...])

        @pl.when(i == nt - 1)
        def _():
            dsh_ref[...] = jnp.sum(a_sh[...], axis=0, keepdims=True)
            dsc_ref[...] = jnp.sum(a_sc[...], axis=0, keepdims=True)
            dg_ref[...] = jnp.sum(a_g[...], axis=0, keepdims=True)
            if a_gt is not None:
                dgt_ref[...] = jnp.sum(a_gt[...], axis=0, keepdims=True)

    row = pl.BlockSpec((ROW_TILE, D), lambda i: (i, 0))
    ins = [xin, dh, g, sc]
    in_specs = [row, pl.BlockSpec((ROW_TILE, D), lambda i: (i + dh_row_off, 0)), _vec_spec(D), _vec_spec(D)]
    out_specs = [_vec_spec(D)] * 3
    out_shape = [_sds((1, D), F32)] * 3
    scratch = [pltpu.VMEM((8, D), F32)] * 3
    if want_dx:
        ins.append(add)
        in_specs.append(row)
        out_specs.append(row)
        out_shape.append(_sds((s, D), F32))
    if resid is not None:
        ins += [resid[0], resid[1]]
        in_specs += [_vec_spec(D), row]
        out_specs += [row, _vec_spec(D)]
        out_shape += [_sds((s, D), MXU_DTYPE), _sds((1, D), F32)]
        scratch.append(pltpu.VMEM((8, D), F32))
    return _pallas(body, name=name, grid=(nt,), in_specs=in_specs, out_specs=out_specs, out_shape=out_shape,
                   scratch_shapes=scratch, semantics=("arbitrary",))(*ins)


FF_TILE = 128
FF_CHUNK = 128
HALO = 8


def _shift3(pad_ref, r0, ch):
    return tuple(pad_ref[pl.ds(r0 + HALO + d, ch), :] for d in (-1, 0, 1))


def _fill_padded(pad_ref, src_ref, s, ch, halo):
    zeros = jnp.zeros((halo, pad_ref.shape[1]), F32)
    pad_ref[0:halo, :] = zeros
    pad_ref[s + halo:s + 2 * halo, :] = zeros

    def cp(c, carry):
        r0 = pl.multiple_of(c * ch, ch)
        pad_ref[pl.ds(r0 + halo, ch), :] = src_ref[pl.ds(r0, ch), :].astype(F32)
        return carry

    lax.fori_loop(0, s // ch, cp, 0)


def _ffn_act_fwd(u, w, b):
    s = u.shape[0]
    nj = DFF // FF_TILE
    ch = FF_CHUNK

    def body(ug_ref, uv_ref, wg_ref, wv_ref, bg_ref, bv_ref, f_ref, gpad, vpad):
        _fill_padded(gpad, ug_ref, s, ch, HALO)
        _fill_padded(vpad, uv_ref, s, ch, HALO)

        def conv(pad, w_ref, b_ref, r0):
            prev, cur, nxt = _shift3(pad, r0, ch)
            return w_ref[0:1, :] * prev + w_ref[1:2, :] * cur + w_ref[2:3, :] * nxt + b_ref[...]

        def step(c, carry):
            r0 = pl.multiple_of(c * ch, ch)
            gc = conv(gpad, wg_ref, bg_ref, r0)
            vc = conv(vpad, wv_ref, bv_ref, r0)
            f_ref[pl.ds(r0, ch), :] = (gc * _sigmoid(gc) * vc).astype(f_ref.dtype)
            return carry

        lax.fori_loop(0, s // ch, step, 0)

    col = lambda off: pl.BlockSpec((s, FF_TILE), lambda j: (0, j + off))
    wsp = lambda off: pl.BlockSpec((3, FF_TILE), lambda j: (0, j + off))
    bsp = lambda off: pl.BlockSpec((1, FF_TILE), lambda j: (0, j + off))
    return _pallas(
        body, name="ffn_act_fwd", grid=(nj,),
        in_specs=[col(0), col(nj), wsp(0), wsp(nj), bsp(0), bsp(nj)],
        out_specs=col(0), out_shape=_sds((s, DFF), MXU_DTYPE),
        scratch_shapes=[pltpu.VMEM((s + 2 * HALO, FF_TILE), F32)] * 2,
        semantics=("parallel",),
    )(u, u, w, w, b, b)


def _ffn_act_bwd(u, df, w, b):
    s = u.shape[0]
    nj = DFF // FF_TILE
    ch = FF_CHUNK

    def body(ug_ref, uv_ref, df_ref, wg_ref, wv_ref, bg_ref, bv_ref,
             dug_ref, duv_ref, dwg_ref, dwv_ref, dbg_ref, dbv_ref, gpad, vpad, dgpad, dvpad, acc):
        _fill_padded(gpad, ug_ref, s, ch, HALO)
        _fill_padded(vpad, uv_ref, s, ch, HALO)
        zeros = jnp.zeros((HALO, FF_TILE), F32)
        for p in (dgpad, dvpad):
            p[0:HALO, :] = zeros
            p[s + HALO:s + 2 * HALO, :] = zeros
        acc[...] = jnp.zeros_like(acc)

        def step(c, carry):
            r0 = pl.multiple_of(c * ch, ch)
            gs = _shift3(gpad, r0, ch)
            vs = _shift3(vpad, r0, ch)
            gc = wg_ref[0:1, :] * gs[0] + wg_ref[1:2, :] * gs[1] + wg_ref[2:3, :] * gs[2] + bg_ref[...]
            vc = wv_ref[0:1, :] * vs[0] + wv_ref[1:2, :] * vs[1] + wv_ref[2:3, :] * vs[2] + bv_ref[...]
            sg = _sigmoid(gc)
            dfv = df_ref[pl.ds(r0, ch), :].astype(F32)
            dgc = dfv * vc * (sg * (1.0 + gc * (1.0 - sg)))
            dvc = dfv * (gc * sg)
            dgpad[pl.ds(r0 + HALO, ch), :] = dgc
            dvpad[pl.ds(r0 + HALO, ch), :] = dvc
            for t in range(3):
                acc[8 * t:8 * t + 8, :] += _colsum8(dgc * gs[t])
                acc[24 + 8 * t:32 + 8 * t, :] += _colsum8(dvc * vs[t])
            acc[48:56, :] += _colsum8(dgc)
            acc[56:64, :] += _colsum8(dvc)
            return carry

        lax.fori_loop(0, s // ch, step, 0)

        def step2(c, carry):
            r0 = pl.multiple_of(c * ch, ch)
            for pad, w_ref, o_ref in ((dgpad, wg_ref, dug_ref), (dvpad, wv_ref, duv_ref)):
                prev, cur, nxt = _shift3(pad, r0, ch)
                o_ref[pl.ds(r0, ch), :] = (w_ref[0:1, :] * nxt + w_ref[1:2, :] * cur + w_ref[2:3, :] * prev).astype(o_ref.dtype)
            return carry

        lax.fori_loop(0, s // ch, step2, 0)
        for t in range(3):
            dwg_ref[t:t + 1, :] = jnp.sum(acc[8 * t:8 * t + 8, :], axis=0, keepdims=True)
            dwv_ref[t:t + 1, :] = jnp.sum(acc[24 + 8 * t:32 + 8 * t, :], axis=0, keepdims=True)
        dbg_ref[...] = jnp.sum(acc[48:56, :], axis=0, keepdims=True)
        dbv_ref[...] = jnp.sum(acc[56:64, :], axis=0, keepdims=True)

    col = lambda off: pl.BlockSpec((s, FF_TILE), lambda j: (0, j + off))
    wsp = lambda off: pl.BlockSpec((3, FF_TILE), lambda j: (0, j + off))
    bsp = lambda off: pl.BlockSpec((1, FF_TILE), lambda j: (0, j + off))
    return _pallas(
        body, name="ffn_act_bwd", grid=(nj,),
        in_specs=[col(0), col(nj), col(0), wsp(0), wsp(nj), bsp(0), bsp(nj)],
        out_specs=[col(0), col(0), wsp(0), wsp(0), bsp(0), bsp(0)],
        out_shape=[_sds((s, DFF), MXU_DTYPE)] * 2 + [_sds((3, DFF), F32)] * 2 + [_sds((1, DFF), F32)] * 2,
        scratch_shapes=[pltpu.VMEM((s + 2 * HALO, FF_TILE), F32)] * 4 + [pltpu.VMEM((64, FF_TILE), F32)],
        semantics=("parallel",),
    )(u, u, df, w, w, b, b)


CONV_CHUNK = 64
CONV_HALO = 16


def _tap(pad_ref, r0, k):
    return pad_ref[pl.ds(r0 + CONV_HALO - CW // 2 + k, CONV_CHUNK), :]


def _glu_into(pad_ref, a_ref, g_ref, s):
    zeros = jnp.zeros((CONV_HALO, LANES), F32)
    pad_ref[0:CONV_HALO, :] = zeros
    pad_ref[s + CONV_HALO:s + 2 * CONV_HALO, :] = zeros

    def cp(c, carry):
        r0 = pl.multiple_of(c * ROW_TILE, ROW_TILE)
        pad_ref[pl.ds(r0 + CONV_HALO, ROW_TILE), :] = a_ref[pl.ds(r0, ROW_TILE), :] * _sigmoid(g_ref[pl.ds(r0, ROW_TILE), :])
        return carry

    lax.fori_loop(0, s // ROW_TILE, cp, 0)


def _conf_conv_fwd(ag, conv_w, conv_b):
    s = ag.shape[0]
    nc = DC // LANES

    def body(a_ref, g_ref, w_ref, b_ref, o_ref, upad):
        _glu_into(upad, a_ref, g_ref, s)

        def step(c, carry):
            r0 = pl.multiple_of(c * CONV_CHUNK, CONV_CHUNK)
            acc = jnp.broadcast_to(b_ref[...], (CONV_CHUNK, LANES))
            for k in range(CW):
                acc = acc + w_ref[k:k + 1, :] * _tap(upad, r0, k)
            o_ref[pl.ds(r0, CONV_CHUNK), :] = acc
            return carry

        lax.fori_loop(0, s // CONV_CHUNK, step, 0)

    col = lambda off: pl.BlockSpec((s, LANES), lambda c: (0, c + off))
    return _pallas(
        body, name="conf_conv_fwd", grid=(nc,),
        in_specs=[col(0), col(nc), pl.BlockSpec((CW, LANES), lambda c: (0, c)), pl.BlockSpec((1, LANES), lambda c: (0, c))],
        out_specs=col(0), out_shape=_sds((s, DC), F32),
        scratch_shapes=[pltpu.VMEM((s + 2 * CONV_HALO, LANES), F32)],
        semantics=("parallel",),
    )(ag, ag, conv_w, conv_b)


def _ln_stats(x):
    mu = jnp.mean(x, axis=-1, keepdims=True)
    xc = x - mu
    var = jnp.mean(xc * xc, axis=-1, keepdims=True)
    rstd = lax.rsqrt(var + EPS)
    return xc * rstd, rstd


def _conf_ln_fwd(u1, ln_g, ln_b, ycat):
    s = u1.shape[0]

    def body(u_ref, g_ref, b_ref, ycat_ref, o_ref):
        del ycat_ref
        xhat, _ = _ln_stats(u_ref[...])
        y = xhat * g_ref[...] + b_ref[...]
        o_ref[...] = (y * _sigmoid(y)).astype(o_ref.dtype)

    return _pallas(
        body, name="conf_ln_fwd", grid=(s // ROW_TILE,),
        in_specs=[pl.BlockSpec((ROW_TILE, DC), lambda i: (i, 0)), _vec_spec(DC), _vec_spec(DC),
                  pl.BlockSpec(memory_space=pl.ANY)],
        out_specs=pl.BlockSpec((ROW_TILE, DC), lambda i: (i, 1)),
        out_shape=_sds(ycat.shape, ycat.dtype),
        input_output_aliases={3: 0},
        semantics=("parallel",),
    )(u1, ln_g, ln_b, ycat)


def _conf_ln_bwd(dycat, u1, ln_g, ln_b):
    s = u1.shape[0]
    nt = s // ROW_TILE

    def body(dy_ref, u_ref, g_ref, b_ref, du_ref, dg_ref, db_ref, a_g, a_b):
        i = pl.program_id(0)

        @pl.when(i == 0)
        def _():
            a_g[...] = jnp.zeros_like(a_g)
            a_b[...] = jnp.zeros_like(a_b)

        xhat, rstd = _ln_stats(u_ref[...])
        gv = g_ref[...]
        y = xhat * gv + b_ref[...]
        sg = _sigmoid(y)
        dyl = dy_ref[...] * (sg * (1.0 + y * (1.0 - sg)))
        a_g[...] += _colsum8(dyl * xhat)
        a_b[...] += _colsum8(dyl)
        dxh = dyl * gv
        du_ref[...] = rstd * (dxh - jnp.mean(dxh, axis=-1, keepdims=True)
                              - xhat * jnp.mean(dxh * xhat, axis=-1, keepdims=True))

        @pl.when(i == nt - 1)
        def _():
            dg_ref[...] = jnp.sum(a_g[...], axis=0, keepdims=True)
            db_ref[...] = jnp.sum(a_b[...], axis=0, keepdims=True)

    return _pallas(
        body, name="conf_ln_bwd", grid=(nt,),
        in_specs=[pl.BlockSpec((ROW_TILE, DC), lambda i: (i, 1)), pl.BlockSpec((ROW_TILE, DC), lambda i: (i, 0)),
                  _vec_spec(DC), _vec_spec(DC)],
        out_specs=[pl.BlockSpec((ROW_TILE, DC), lambda i: (i, 0)), _vec_spec(DC), _vec_spec(DC)],
        out_shape=[_sds((s, DC), F32), _sds((1, DC), F32), _sds((1, DC), F32)],
        scratch_shapes=[pltpu.VMEM((8, DC), F32)] * 2,
        semantics=("arbitrary",),
    )(dycat, u1, ln_g, ln_b)


def _conf_conv_bwd(ag, du1, conv_w, rows_out):
    s = ag.shape[0]
    nc = DC // LANES

    def body(a_ref, g_ref, d_ref, w_ref, da_ref, dg_ref, dw_ref, db_ref, upad, dpad, acc):
        _glu_into(upad, a_ref, g_ref, s)
        _fill_padded(dpad, d_ref, s, ROW_TILE, CONV_HALO)
        acc[...] = jnp.zeros_like(acc)

        def step(c, carry):
            r0 = pl.multiple_of(c * CONV_CHUNK, CONV_CHUNK)
            dcur = dpad[pl.ds(r0 + CONV_HALO, CONV_CHUNK), :]
            du0 = jnp.zeros((CONV_CHUNK, LANES), F32)
            for k in range(CW):
                du0 = du0 + w_ref[k:k + 1, :] * _tap(dpad, r0, CW - 1 - k)
                acc[8 * k:8 * k + 8, :] += _colsum8(dcur * _tap(upad, r0, k))
            acc[8 * CW:8 * CW + 8, :] += _colsum8(dcur)
            av = a_ref[pl.ds(r0, CONV_CHUNK), :]
            sg = _sigmoid(g_ref[pl.ds(r0, CONV_CHUNK), :])
            da_ref[pl.ds(r0, CONV_CHUNK), :] = (du0 * sg).astype(da_ref.dtype)
            dg_ref[pl.ds(r0, CONV_CHUNK), :] = (du0 * av * (sg * (1.0 - sg))).astype(dg_ref.dtype)
            return carry

        lax.fori_loop(0, s // CONV_CHUNK, step, 0)
        if rows_out > s:
            zeros = jnp.zeros((rows_out - s, LANES), da_ref.dtype)
            da_ref[s:rows_out, :] = zeros
            dg_ref[s:rows_out, :] = zeros
        for k in range(CW):
            dw_ref[k:k + 1, :] = jnp.sum(acc[8 * k:8 * k + 8, :], axis=0, keepdims=True)
        db_ref[...] = jnp.sum(acc[8 * CW:8 * CW + 8, :], axis=0, keepdims=True)

    col = lambda off: pl.BlockSpec((s, LANES), lambda c: (0, c + off))
    ocol = pl.BlockSpec((rows_out, LANES), lambda c: (0, c))
    return _pallas(
        body, name="conf_conv_bwd", grid=(nc,),
        in_specs=[col(0), col(nc), col(0), pl.BlockSpec((CW, LANES), lambda c: (0, c))],
        out_specs=[ocol, ocol, pl.BlockSpec((CW, LANES), lambda c: (0, c)), pl.BlockSpec((1, LANES), lambda c: (0, c))],
        out_shape=[_sds((rows_out, DC), MXU_DTYPE)] * 2 + [_sds((CW, DC), F32), _sds((1, DC), F32)],
        scratch_shapes=[pltpu.VMEM((s + 2 * CONV_HALO, LANES), F32)] * 2 + [pltpu.VMEM((8 * (CW + 1), LANES), F32)],
        semantics=("parallel",),
    )(ag, ag, du1, conv_w)


Q_TILE = 2 * GW
K_WIN = PAIR_ROWS * GW


def _bias_table(rpb_rev):
    def body(p_ref, t_ref):
        kcol = lax.broadcasted_iota(jnp.int32, (GW, LANES), 0)
        lane = lax.broadcasted_iota(jnp.int32, (GW, LANES), 1)
        qcol = lane % GW
        cs = jnp.clip(qcol - NA_ROWS, 0, GW - 2 * NA_ROWS)
        colvalid = (kcol >= cs) & (kcol < cs + 2 * NA_ROWS)
        neg = jnp.full((GW, LANES), NEG, F32)

        def skew(h, ro, shift):
            if ro < 0 or ro >= 2 * NA_ROWS - 1:
                return neg
            row = jnp.broadcast_to(p_ref[h * 16 + ro:h * 16 + ro + 1, :], (GW, LANES))
            return pltpu.roll(row, shift, 1, stride=1, stride_axis=0)

        for h in range(NH):
            for b in range(TAB_BLOCKS):
                val = jnp.where(lane < GW, skew(h, b - 1, GW + 1), skew(h, b - 2, 1))
                t_ref[h, b * GW:(b + 1) * GW, :] = jnp.where(colvalid, val, neg)

    return _pallas(body, name="attn_bias_table", out_shape=_sds((NH, TAB_BLOCKS * GW, LANES), F32))(rpb_rev)


def _rpb_grad(tt):
    def body(t_ref, o_ref):
        lane = lax.broadcasted_iota(jnp.int32, (GW, LANES), 1)
        si = lax.broadcasted_iota(jnp.int32, (GW, GW), 0)
        ti = lax.broadcasted_iota(jnp.int32, (GW, GW), 1)
        flip = jnp.where(si + ti == GW - 1, 1.0, 0.0).astype(F32)
        o_ref[...] = jnp.zeros_like(o_ref)
        for h in range(NH):
            for ro in range(2 * NA_ROWS - 1):
                lo = t_ref[h, (ro + 1) * GW:(ro + 2) * GW, :]
                hi = t_ref[h, (ro + 2) * GW:(ro + 3) * GW, :]
                g = jnp.where(lane < GW, lo + pltpu.roll(hi, GW, 1), 0.0)
                gf = jnp.dot(flip, g, preferred_element_type=F32, precision=lax.Precision.HIGHEST)
                sk = pltpu.roll(gf, 0, 1, stride=1, stride_axis=0)
                o_ref[h * 16 + ro:h * 16 + ro + 1, :] = jnp.sum(sk, axis=0, keepdims=True)

    return _pallas(body, name="attn_rpb_grad", out_shape=_sds((NH * 16, LANES), F32))(tt)


def _attn_geometry(i, rows):
    wsp = jnp.clip(2 * i - NA_ROWS // 2, 0, rows - PAIR_ROWS)
    k0 = pl.multiple_of(wsp * GW, GW)
    t0 = pl.multiple_of((wsp - 2 * i + NA_ROWS) * GW, GW)
    rr = lax.broadcasted_iota(jnp.int32, (GW, Q_TILE), 1) // GW
    wsr = jnp.clip(2 * i + rr - NA_ROWS // 2, 0, rows - NA_ROWS)
    edge_masks = tuple(jnp.where((kr >= wsr) & (kr < wsr + NA_ROWS), 0.0, NEG).astype(F32)
                       for kr in (wsp, wsp + PAIR_ROWS - 1))
    return k0, t0, edge_masks


def _biased(s_raw, bias, edge_masks):
    x = s_raw + bias
    return jnp.concatenate([x[:GW] + edge_masks[0], x[GW:K_WIN - GW], x[K_WIN - GW:] + edge_masks[1]], axis=0)


def _two_heads_on_lanes(xt):
    feat = lax.broadcasted_iota(jnp.int32, xt.shape, 0)
    zero = jnp.zeros_like(xt)
    return jnp.concatenate([jnp.where(feat < HD, xt, zero), jnp.where(feat >= HD, xt, zero)], axis=1)


def _two_heads_on_rows(x):
    lane = lax.broadcasted_iota(jnp.int32, x.shape, 1)
    zero = jnp.zeros_like(x)
    return jnp.concatenate([jnp.where(lane < HD, x, zero), jnp.where(lane >= HD, x, zero)], axis=0)


def _pick_heads(x2):
    n = x2.shape[0] // 2
    lane = lax.broadcasted_iota(jnp.int32, (n, LANES), 1)
    return jnp.where(lane < HD, x2[:n], x2[n:])


_TN = (((0,), (0,)), ((), ()))


def _attn_fwd(qkv, tab, s):
    rows = s // GW
    npair = rows // 2

    def body(q_ref, kv_ref, tab_ref, o_ref, lse_ref):
        i = pl.program_id(0)
        k0, t0, edge_masks = _attn_geometry(i, rows)
        for p in range(NH // 2):
            cq = slice(p * LANES, (p + 1) * LANES)
            ck = slice(DA + p * LANES, DA + (p + 1) * LANES)
            cv = slice(2 * DA + p * LANES, 2 * DA + (p + 1) * LANES)
            qm2 = _two_heads_on_lanes(q_ref[:, cq].T) * SCALE
            s_loc = jnp.dot(kv_ref[pl.ds(k0, K_WIN), ck], qm2, preferred_element_type=F32)
            s_ctx = jnp.dot(kv_ref[pl.ds(s, CTX), ck], qm2, preferred_element_type=F32)
            p_loc, p_ctx = [], []
            for hh in range(2):
                h = 2 * p + hh
                ch = slice(hh * Q_TILE, (hh + 1) * Q_TILE)
                sl = _biased(s_loc[:, ch], tab_ref[h, pl.ds(t0, K_WIN), :], edge_masks)
                sc = s_ctx[:, ch]
                m = jnp.maximum(jnp.max(sl, axis=0, keepdims=True), jnp.max(sc, axis=0, keepdims=True))
                el = jnp.exp(sl - m)
                ec = jnp.exp(sc - m)
                l = jnp.sum(el, axis=0, keepdims=True) + jnp.sum(ec, axis=0, keepdims=True)
                inv = 1.0 / l
                lse_ref[h:h + 1, :] = m + jnp.log(l)
                p_loc.append((el * inv).astype(MXU_DTYPE))
                p_ctx.append((ec * inv).astype(MXU_DTYPE))
            o2 = (lax.dot_general(jnp.concatenate(p_loc, axis=1), kv_ref[pl.ds(k0, K_WIN), cv], _TN, preferred_element_type=F32)
                  + lax.dot_general(jnp.concatenate(p_ctx, axis=1), kv_ref[pl.ds(s, CTX), cv], _TN, preferred_element_type=F32))
            o_ref[:, cq] = _pick_heads(o2).astype(o_ref.dtype)

    return _pallas(
        body, name="attn_fwd", grid=(npair,),
        in_specs=[pl.BlockSpec((Q_TILE, DA), lambda i: (i, 0)), pl.BlockSpec(memory_space=pltpu.VMEM),
                  pl.BlockSpec(memory_space=pltpu.VMEM)],
        out_specs=[pl.BlockSpec((Q_TILE, DA), lambda i: (i, 0)), pl.BlockSpec((NH, Q_TILE), lambda i: (0, i))],
        out_shape=[_sds((s, D), MXU_DTYPE), _sds((NH, s), F32)],
        semantics=("arbitrary",),
    )(qkv, qkv, tab)


def _attn_bwd(qkv, tab, lse, dycat, s):
    rows = s // GW
    npair = rows // 2
    sa = s + CTX
    nzero = CTX // Q_TILE

    def body(q_ref, do_ref, lse_ref, kv_ref, tab_ref, dq_ref, dkv_ref, tt_ref, dk_acc, dv_acc):
        i = pl.program_id(0)

        @pl.when(i == 0)
        def _():
            dk_acc[...] = jnp.zeros_like(dk_acc)
            dv_acc[...] = jnp.zeros_like(dv_acc)
            tt_ref[...] = jnp.zeros_like(tt_ref)

        @pl.when(i >= npair)
        def _():
            dq_ref[...] = jnp.zeros_like(dq_ref)

        @pl.when(i < npair)
        def _():
            k0, t0, edge_masks = _attn_geometry(i, rows)
            for p in range(NH // 2):
                cq = slice(p * LANES, (p + 1) * LANES)
                ck = slice(DA + p * LANES, DA + (p + 1) * LANES)
                cv = slice(2 * DA + p * LANES, 2 * DA + (p + 1) * LANES)
                qp = q_ref[:, cq] * SCALE
                dop = do_ref[:, cq].astype(MXU_DTYPE)
                qm2 = _two_heads_on_lanes(qp.T)
                dom2 = _two_heads_on_lanes(dop.T)
                kw = kv_ref[pl.ds(k0, K_WIN), ck]
                kc = kv_ref[pl.ds(s, CTX), ck]
                vw = kv_ref[pl.ds(k0, K_WIN), cv]
                vc = kv_ref[pl.ds(s, CTX), cv]
                s_loc = jnp.dot(kw, qm2, preferred_element_type=F32)
                s_ctx = jnp.dot(kc, qm2, preferred_element_type=F32)
                dp_loc = jnp.dot(vw, dom2, preferred_element_type=F32)
                dp_ctx = jnp.dot(vc, dom2, preferred_element_type=F32)
                p_loc, p_ctx, ds_loc, ds_ctx = [], [], [], []
                for hh in range(2):
                    h = 2 * p + hh
                    ch = slice(hh * Q_TILE, (hh + 1) * Q_TILE)
                    lse_h = lse_ref[h:h + 1, :]
                    pl_ = jnp.exp(_biased(s_loc[:, ch], tab_ref[h, pl.ds(t0, K_WIN), :], edge_masks) - lse_h)
                    pc_ = jnp.exp(s_ctx[:, ch] - lse_h)
                    dpl = dp_loc[:, ch]
                    dpc = dp_ctx[:, ch]
                    delta = jnp.sum(pl_ * dpl, axis=0, keepdims=True) + jnp.sum(pc_ * dpc, axis=0, keepdims=True)
                    dsl = pl_ * (dpl - delta)
                    dsc = pc_ * (dpc - delta)
                    tt_ref[h, pl.ds(t0, K_WIN), :] += dsl
                    p_loc.append(pl_.astype(MXU_DTYPE))
                    p_ctx.append(pc_.astype(MXU_DTYPE))
                    ds_loc.append(dsl.astype(MXU_DTYPE))
                    ds_ctx.append(dsc.astype(MXU_DTYPE))
                p_loc, p_ctx = jnp.concatenate(p_loc, axis=1), jnp.concatenate(p_ctx, axis=1)
                ds_loc, ds_ctx = jnp.concatenate(ds_loc, axis=1), jnp.concatenate(ds_ctx, axis=1)
                do_rows = _two_heads_on_rows(dop)
                q_rows = _two_heads_on_rows(qp)
                dv_acc[pl.ds(k0, K_WIN), cq] += jnp.dot(p_loc, do_rows, preferred_element_type=F32)
                dv_acc[pl.ds(s, CTX), cq] += jnp.dot(p_ctx, do_rows, preferred_element_type=F32)
                dk_acc[pl.ds(k0, K_WIN), cq] += jnp.dot(ds_loc, q_rows, preferred_element_type=F32)
                dk_acc[pl.ds(s, CTX), cq] += jnp.dot(ds_ctx, q_rows, preferred_element_type=F32)
                dq2 = (lax.dot_general(ds_loc, kw, _TN, preferred_element_type=F32)
                       + lax.dot_general(ds_ctx, kc, _TN, preferred_element_type=F32))
                dq_ref[:, cq] = (_pick_heads(dq2) * SCALE).astype(dq_ref.dtype)

        @pl.when(i == npair - 1)
        def _():
            def cp(c, carry):
                r0 = pl.multiple_of(c * ROW_TILE, ROW_TILE)
                dkv_ref[pl.ds(r0, ROW_TILE), 0:DA] = dk_acc[pl.ds(r0, ROW_TILE), :].astype(dkv_ref.dtype)
                dkv_ref[pl.ds(r0, ROW_TILE), DA:2 * DA] = dv_acc[pl.ds(r0, ROW_TILE), :].astype(dkv_ref.dtype)
                return carry

            lax.fori_loop(0, sa // ROW_TILE, cp, 0)

    qmap = lambda i: (jnp.minimum(i, npair - 1), 0)
    return _pallas(
        body, name="attn_bwd", grid=(npair + nzero,),
        in_specs=[pl.BlockSpec((Q_TILE, DA), qmap), pl.BlockSpec((Q_TILE, DA), qmap),
                  pl.BlockSpec((NH, Q_TILE), lambda i: (0, jnp.minimum(i, npair - 1))),
                  pl.BlockSpec(memory_space=pltpu.VMEM), pl.BlockSpec(memory_space=pltpu.VMEM)],
        out_specs=[pl.BlockSpec((Q_TILE, DA), lambda i: (i, 0)), pl.BlockSpec(memory_space=pltpu.VMEM),
                   pl.BlockSpec(memory_space=pltpu.VMEM)],
        out_shape=[_sds((sa, DA), MXU_DTYPE), _sds((sa, 2 * DA), MXU_DTYPE), _sds((NH, TAB_BLOCKS * GW, LANES), F32)],
        scratch_shapes=[pltpu.VMEM((sa, DA), F32)] * 2,
        semantics=("arbitrary",),
    )(qkv, dycat, lse, qkv, tab)


def _tile(n, prefs):
    for t in prefs:
        if n % t == 0:
            return t
    raise ValueError((n, prefs))


def _local_step(x, ctx, tgt, mod, mod_c, vec, w_in, late_weights, rpb_rev, early_grads=None):
    s = x.shape[0]
    sa = s + CTX
    ts = _tile(s, (1024, 512, 256))
    ts2 = _tile(s, (2048, 1024, 512, 256))
    tsa = _tile(sa, (1088, 640, 256))
    tsa2 = _tile(sa, (2176, 640, 256))
    sh1, sc1, gt1, sh2, sc2, gt2 = (mod[i:i + 1] for i in range(6))
    csh1, csc1 = mod_c[0:1], mod_c[1:2]
    act = MXU_DTYPE

    tab = _bias_table(rpb_rev)
    h_all = _rmsmod_fwd(x, ctx, vec["g_norm1"], sc1, sh1, csc1, csh1)
    w_in = w_in(h_all) if callable(w_in) else w_in
    qkv = _mm(h_all, w_in, mode="nn", m=sa, n=3 * DA, k=D, tm=tsa2, tn=512, tk=D, out_dtype=MXU_DTYPE, name="mm_qkv")
    ag = _mm(h_all, w_in, mode="nn", m=s, n=2 * DC, k=D, tm=ts2, tn=512, tk=D, out_dtype=F32, name="mm_ag", b_off=(0, 3))
    ycat, lse = _attn_fwd(qkv, tab, s)
    u1 = _conf_conv_fwd(ag, vec["conv_w"], vec["conv_b"])
    ycat = _conf_ln_fwd(u1, vec["ln_g"], vec["ln_b"], ycat)
    w_out, w_up, w_down = late_weights(ycat) if callable(late_weights) else late_weights
    y = _mm(ycat, w_out, mode="nn", m=s, n=D, k=D, tm=ts2, tn=512, tk=D, out_dtype=F32, name="mm_out")
    x1, h2 = _resid_rmsmod_fwd(x, y, gt1, vec["g_norm2"], sc2, sh2)
    u = _mm(h2, w_up, mode="nn", m=s, n=2 * DFF, k=D, tm=ts2, tn=512, tk=D, out_dtype=act, name="mm_up")
    f = _ffn_act_fwd(u, vec["ffn_conv_w"], vec["ffn_conv_b"])
    z = _mm(f, w_down, mode="nn", m=s, n=D, k=DFF, tm=ts, tn=D, tk=DFF, out_dtype=F32, name="mm_down")
    dx2, dz, loss, dgt2, dgf = _final_fwd_bwd(x1, z, gt2, vec["g_final"], tgt)

    df = _mm(dz, w_down, mode="nt", m=s, n=DFF, k=D, tm=ts, tn=DFF, tk=D, out_dtype=act, name="mm_down_dx")
    d_w_down = _mm(f, dz, mode="tn", m=DFF, n=D, k=s, tm=DFF, tn=D, tk=ts, out_dtype=F32, name="mm_down_dw")
    dug, duv, dfw_g, dfw_v, dfb_g, dfb_v = _ffn_act_bwd(u, df, vec["ffn_conv_w"], vec["ffn_conv_b"])
    dw_kw = dict(mode="tn", m=D, n=DFF, k=s, tm=D, tn=DFF, tk=ts, out_dtype=F32, out_total=(D, 2 * DFF))
    d_w_up = _mm(h2, dug, name="mm_up_dw_gate", **dw_kw)
    d_w_up = _mm(h2, duv, name="mm_up_dw_val", o_off=(0, 1), into=d_w_up, **dw_kw)
    if early_grads is not None:
        early_grads[0](d_w_up, d_w_down)
    dh2 = _mm([dug, duv], w_up, mode="nt", m=s, n=D, k=2 * DFF, tm=ts, tn=D, tk=DFF, out_dtype=F32, name="mm_up_dx")
    sc2_b = sc2 if early_grads is None else sc2 + early_grads[1](dh2)
    dsh2, dsc2, dg2, dx1, dy, dgt1 = _rmsmod_bwd(x1, dh2, vec["g_norm2"], sc2_b, name="rmsmod2_bwd", add=dx2, resid=(gt1, y))
    dycat = _mm(dy, w_out, mode="nt", m=s, n=D, k=D, tm=ts2, tn=512, tk=D, out_dtype=F32, name="mm_out_dx")
    d_w_out = _mm(ycat, dy, mode="tn", m=D, n=D, k=s, tm=D, tn=D, tk=ts, out_dtype=F32, name="mm_out_dw")
    du1, dln_g, dln_b = _conf_ln_bwd(dycat, u1, vec["ln_g"], vec["ln_b"])
    da, dg, dconv_w, dconv_b = _conf_conv_bwd(ag, du1, vec["conv_w"], sa)
    dq, dkv, tt = _attn_bwd(qkv, tab, lse, dycat, s)
    drpb_rev = _rpb_grad(tt)
    d_pieces = [dq, dkv, da, dg]
    dh = _mm(d_pieces, w_in, mode="nt", m=sa, n=D, k=NIN, tm=tsa2, tn=D, tk=512, out_dtype=F32, name="mm_in_dx")
    d_w_in = _mm(h_all, d_pieces, mode="tn", m=D, n=NIN, k=sa, tm=D, tn=512, tk=tsa, out_dtype=F32, name="mm_in_dw",
                 k_outer=True)
    dsh1, dsc1, dg1, grad_x = _rmsmod_bwd(x, dh, vec["g_norm1"], sc1, name="rmsmod1_bwd", add=dx1)
    dcsh1, dcsc1, dg1c = _rmsmod_bwd(ctx, dh, vec["g_norm1"], csc1, name="rmsmod1_ctx_bwd", dh_row_off=s // ROW_TILE)

    small = dict(
        dmod=[dsh1, dsc1, dgt1, dsh2, dsc2, dgt2], dmod_c=[dcsh1, dcsc1],
        g_norm1=[dg1, dg1c], g_norm2=dg2, g_final=dgf, conv_b=dconv_b, ln_g=dln_g, ln_b=dln_b, conv_w=dconv_w,
        ffn_conv_w=[dfw_g, dfw_v], ffn_conv_b=[dfb_g, dfb_v], rpb_rev=drpb_rev,
    )
    return loss, grad_x, d_w_in, d_w_out, d_w_up, d_w_down, small


N_CHIPS = 4
HBM = pl.BlockSpec(memory_space=pl.ANY)
BIG = {"w_in": ("col", (D, NIN)), "w_out": ("row", (D, D)), "w_up": ("col", (D, 2 * DFF)), "w_down": ("row", (DFF, D))}
BIG_NAMES = tuple(BIG)
LATE_NAMES = ("w_out", "w_up", "w_down")


def _shard_shape(name):
    kind, (r, c) = BIG[name]
    return (r, c // N_CHIPS) if kind == "col" else (r // N_CHIPS, c)


def _half_rows(name):
    return _shard_shape(name)[0] // 2


def _place():
    x, y, c = lax.axis_index("x"), lax.axis_index("y"), lax.axis_index("c")
    others = [(1 - x, y), (x, 1 - y), (1 - x, 1 - y)]
    return x, y, c, 2 * x + y, (x, y, 1 - c), others


def _whole_region(ref, name, chip, half):
    kind, _ = BIG[name]
    r, c = _shard_shape(name)
    if kind == "col":
        return ref.at[pl.ds(half * (r // 2), r // 2), pl.ds(chip * c, c)]
    return ref.at[pl.ds(chip * r + half * (r // 2), r // 2), :]


def _remote(src, dst, send_sem, recv_sem, to):
    return pltpu.make_async_remote_copy(src_ref=src, dst_ref=dst, send_sem=send_sem, recv_sem=recv_sem,
                                        device_id=to, device_id_type=MESH)


def _gather_small(v, name):
    m_per, n = v.shape

    def body(x_ref, out_ref, send_sems, recv_sems, local_sem):
        x, y, c, _, sibling, others = _place()
        me = (x, y, c)

        def rows(px, py, pc):
            return out_ref.at[pl.ds((4 * px + 2 * py + pc) * m_per, m_per), :]

        def copy(k, block, to, src=None):
            return _remote(rows(*block) if src is None else src, rows(*block), send_sems.at[k], recv_sems.at[k], to)

        mine = pltpu.make_async_copy(x_ref, rows(*me), local_sem)
        mine.start()
        first = [copy(0, me, sibling, src=x_ref)]
        first += [copy(1 + j, me, (*chip, c), src=x_ref) for j, chip in enumerate(others)]
        for cp in first:
            cp.start()
        passed = [copy(4 + j, (*chip, c), sibling) for j, chip in enumerate(others)]
        for j, chip in enumerate(others):
            copy(1 + j, (*chip, c), me).wait_recv()
            passed[j].start()
        copy(0, sibling, me).wait_recv()
        for j, chip in enumerate(others):
            copy(4 + j, (*chip, 1 - c), me).wait_recv()
        for cp in first + passed:
            cp.wait_send()
        mine.wait()

    return pl.pallas_call(
        body, name=name, out_shape=_sds((8 * m_per, n), v.dtype),
        in_specs=[pl.BlockSpec(memory_space=pltpu.VMEM)], out_specs=pl.BlockSpec(memory_space=pltpu.VMEM),
        scratch_shapes=[pltpu.SemaphoreType.DMA((7,)), pltpu.SemaphoreType.DMA((7,)), pltpu.SemaphoreType.DMA],
    )(v)


def _cast_into_whole(name, shard, chip):
    kind, whole = BIG[name]
    r, c = shard.shape
    if kind == "col":
        tr = 256
        o_spec = pl.BlockSpec((tr, c), lambda i, ch: (i, ch[0]))
    else:
        tr = _tile(r, (128, 352))
        o_spec = pl.BlockSpec((tr, c), lambda i, ch: (ch[0] * (r // tr) + i, 0))

    def body(ch_ref, x_ref, o_ref):
        del ch_ref
        o_ref[...] = x_ref[...].astype(o_ref.dtype)

    return _pallas(body, name="cast_" + name, prefetch=1, grid=(r // tr,),
                   in_specs=[pl.BlockSpec((tr, c), lambda i, ch: (i, 0))], out_specs=o_spec,
                   out_shape=_sds(whole, MXU_DTYPE), semantics=("parallel",))(chip, shard)


SEM = pl.BlockSpec(memory_space=pltpu.SEMAPHORE)
IN_HBM = pl.BlockSpec(memory_space=pltpu.HBM)
DATAFLOW = pltpu.SideEffectType.DATAFLOW_SIDE_EFFECTING


def _keep_in_hbm(a):
    return pltpu.with_memory_space_constraint(a, pltpu.HBM)


def _gather_start(wholes, names, after, tag):
    nw = len(names)
    ns = 2 * 3 * nw

    def body(*refs):
        ins = refs[:nw]
        sems = refs[nw + 1:nw + 1 + ns]
        token = refs[2 * nw + ns + 1]
        _, _, c, chip, _, others = _place()
        for w, name in enumerate(names):
            mine = _whole_region(ins[w], name, chip, c)
            for t, (ox, oy) in enumerate(others):
                k = 2 * (3 * w + t)
                _remote(mine, mine, sems[k], sems[k + 1], (ox, oy, c)).start()
        token[...] = jnp.zeros_like(token)

    res = pl.pallas_call(
        body, name="gather_" + tag + "_start",
        out_shape=(*[pltpu.SemaphoreType.DMA(())] * ns, *[pltpu.HBM(a.shape, a.dtype) for a in wholes], _sds((8, LANES), F32)),
        in_specs=[IN_HBM] * nw + [pl.BlockSpec(memory_space=pl.ANY)],
        out_specs=(*[SEM] * ns, *[IN_HBM] * nw, pl.BlockSpec(memory_space=pltpu.VMEM)),
        input_output_aliases={i: ns + i for i in range(nw)},
        compiler_params=pltpu.CompilerParams(has_side_effects=DATAFLOW),
    )(*[_keep_in_hbm(a) for a in wholes], after)
    return list(res[:ns]), list(res[ns:ns + nw]), res[ns + nw]


def _gather_wait(sems, wholes, names, after, tag):
    nw = len(names)
    ns = len(sems)

    def body(*refs):
        ins = refs[:nw]
        sem_refs = refs[nw:nw + ns]
        _, _, c, chip, _, others = _place()
        for w, name in enumerate(names):
            mine = _whole_region(ins[w], name, chip, c)
            for t, (ox, oy) in enumerate(others):
                got = _whole_region(ins[w], name, 2 * ox + oy, c)
                k = 2 * (3 * w + t)
                cp = _remote(mine, got, sem_refs[k], sem_refs[k + 1], (ox, oy, c))
                cp.wait_send()
                cp.wait_recv()

    return pl.pallas_call(
        body, name="gather_" + tag + "_wait",
        out_shape=tuple(pltpu.HBM(a.shape, a.dtype) for a in wholes),
        in_specs=[IN_HBM] * nw + [SEM] * ns + [pl.BlockSpec(memory_space=pl.ANY)], out_specs=tuple([IN_HBM] * nw),
        input_output_aliases={i: i for i in range(nw)},
        compiler_params=pltpu.CompilerParams(has_side_effects=DATAFLOW),
    )(*wholes, *sems, after)


def _forward_halves(wholes, names, tag):
    nw = len(names)

    def body(*refs):
        outs = refs[nw:2 * nw]
        send_sems, recv_sems = refs[2 * nw:]
        _, _, c, _, sibling, others = _place()
        sends = []
        for w, name in enumerate(names):
            for t, (ox, oy) in enumerate(others):
                got = _whole_region(outs[w], name, 2 * ox + oy, c)
                cp = _remote(got, got, send_sems.at[w, t], recv_sems.at[w, t], sibling)
                cp.start()
                sends.append(cp)
        for w, name in enumerate(names):
            for t, (ox, oy) in enumerate(others):
                got = _whole_region(outs[w], name, 2 * ox + oy, 1 - c)
                _remote(got, got, send_sems.at[w, t], recv_sems.at[w, t], sibling).wait_recv()
        for cp in sends:
            cp.wait_send()

    return pl.pallas_call(
        body, name="gather_" + tag + "_forward",
        out_shape=[_sds(a.shape, a.dtype) for a in wholes],
        in_specs=[HBM] * nw, out_specs=[HBM] * nw,
        input_output_aliases={i: i for i in range(nw)},
        scratch_shapes=[pltpu.SemaphoreType.DMA((nw, 3)), pltpu.SemaphoreType.DMA((nw, 3))],
    )(*wholes)


def _compact_shape(name, dtype):
    kind, (r, c) = BIG[name]
    return _sds((r // 2, c), dtype)


def _swap_pairs(ins, outs, names, c):
    pairs = []
    for w, name in enumerate(names):
        kind, _ = BIG[name]
        half = _half_rows(name)
        if kind == "col":
            pairs.append((ins[w].at[pl.ds((1 - c) * half, half), :], outs[w]))
        else:
            pairs += [(ins[w].at[pl.ds(jj * 2 * half + (1 - c) * half, half), :], outs[w].at[pl.ds(jj * half, half), :])
                      for jj in range(N_CHIPS)]
    return pairs


def _n_swap_copies(names):
    return sum(1 if BIG[n][0] == "col" else N_CHIPS for n in names)


def _swap_start(grads, names, label):
    nw = len(names)
    ns = 2 * _n_swap_copies(names)

    def body(*refs):
        ins, lands = refs[:nw], refs[nw:2 * nw]
        sems = refs[2 * nw:2 * nw + ns]
        token = refs[4 * nw + ns]
        _, _, c, _, sibling, _ = _place()
        for k, (src, dst) in enumerate(_swap_pairs(ins, lands, names, c)):
            _remote(src, dst, sems[2 * k], sems[2 * k + 1], sibling).start()
        token[...] = jnp.zeros_like(token)

    lands = [_keep_in_hbm(lax.empty(_compact_shape(n, F32).shape, F32)) for n in names]
    res = pl.pallas_call(
        body, name=label,
        out_shape=(*[pltpu.SemaphoreType.DMA(())] * ns, *[pltpu.HBM(a.shape, a.dtype) for a in grads],
                   *[pltpu.HBM(a.shape, a.dtype) for a in lands], _sds((8, LANES), F32)),
        in_specs=[IN_HBM] * (2 * nw),
        out_specs=(*[SEM] * ns, *[IN_HBM] * (2 * nw), pl.BlockSpec(memory_space=pltpu.VMEM)),
        input_output_aliases={i: ns + i for i in range(2 * nw)},
        compiler_params=pltpu.CompilerParams(has_side_effects=DATAFLOW),
    )(*[_keep_in_hbm(a) for a in grads], *lands)
    return list(res[:ns]), list(res[ns:ns + nw]), list(res[ns + nw:ns + 2 * nw]), res[ns + 2 * nw]


def _swap_wait(sems, grads, lands, names, after, label):
    nw = len(names)
    ns = len(sems)

    def body(*refs):
        ins, land_refs = refs[:nw], refs[nw:2 * nw]
        sem_refs = refs[2 * nw:2 * nw + ns]
        _, _, c, _, sibling, _ = _place()
        for k, (src, dst) in enumerate(_swap_pairs(ins, land_refs, names, c)):
            cp = _remote(src, dst, sem_refs[2 * k], sem_refs[2 * k + 1], sibling)
            cp.wait_send()
            cp.wait_recv()

    res = pl.pallas_call(
        body, name=label,
        out_shape=tuple(pltpu.HBM(a.shape, a.dtype) for a in (*grads, *lands)),
        in_specs=[IN_HBM] * (2 * nw) + [SEM] * ns + [pl.BlockSpec(memory_space=pl.ANY)],
        out_specs=tuple([IN_HBM] * (2 * nw)),
        input_output_aliases={i: i for i in range(2 * nw)},
        compiler_params=pltpu.CompilerParams(has_side_effects=DATAFLOW),
    )(*grads, *lands, *sems, after)
    return list(res[:nw]), list(res[nw:])


def _add_halves(name, grad, got, core):
    kind, (r, c) = BIG[name]
    half = _half_rows(name)
    if kind == "col":
        t = 128
        grid = (half // t,)
        g_spec = pl.BlockSpec((t, c), lambda i, cr: (cr[0] * (half // t) + i, 0))
        o_spec = pl.BlockSpec((t, c), lambda i, cr: (i, 0))
    else:
        t = half
        grid = (N_CHIPS,)
        g_spec = pl.BlockSpec((t, c), lambda i, cr: (2 * i + cr[0], 0))
        o_spec = pl.BlockSpec((t, c), lambda i, cr: (i, 0))

    def body(c_ref, g_ref, b_ref, o_ref):
        del c_ref
        o_ref[...] = (g_ref[...] + b_ref[...]).astype(o_ref.dtype)

    return pl.pallas_call(
        body, name="grad_add_" + name,
        grid_spec=pltpu.PrefetchScalarGridSpec(num_scalar_prefetch=1, grid=grid, in_specs=[g_spec, o_spec], out_specs=o_spec),
        out_shape=_compact_shape(name, BF16),
        compiler_params=pltpu.CompilerParams(dimension_semantics=("parallel",), vmem_limit_bytes=VMEM_LIMIT),
    )(core, grad, got)


def _piece(ref, name, chip):
    kind, _ = BIG[name]
    r, c = _shard_shape(name)
    if kind == "col":
        return ref.at[:, pl.ds(chip * c, c)]
    return ref.at[pl.ds(chip * (r // 2), r // 2), :]


def _landing_shape(name):
    r, c = _shard_shape(name)
    return (N_CHIPS - 1, r // 2, c)


def _exchange_start(parts, names, label):
    nw = len(names)
    ns = 2 * 3 * nw

    def body(*refs):
        ins, lands = refs[:nw], refs[nw:2 * nw]
        sems = refs[2 * nw:2 * nw + ns]
        token = refs[4 * nw + ns]
        _, _, c, _, _, others = _place()
        for w, name in enumerate(names):
            for t, (ox, oy) in enumerate(others):
                k = 2 * (3 * w + t)
                _remote(_piece(ins[w], name, 2 * ox + oy), lands[w].at[t], sems[k], sems[k + 1], (ox, oy, c)).start()
        token[...] = jnp.zeros_like(token)

    lands = [_keep_in_hbm(lax.empty(_landing_shape(n), BF16)) for n in names]
    res = pl.pallas_call(
        body, name=label,
        out_shape=(*[pltpu.SemaphoreType.DMA(())] * ns, *[pltpu.HBM(a.shape, a.dtype) for a in parts],
                   *[pltpu.HBM(a.shape, a.dtype) for a in lands], _sds((8, LANES), F32)),
        in_specs=[IN_HBM] * (2 * nw),
        out_specs=(*[SEM] * ns, *[IN_HBM] * (2 * nw), pl.BlockSpec(memory_space=pltpu.VMEM)),
        input_output_aliases={i: ns + i for i in range(2 * nw)},
        compiler_params=pltpu.CompilerParams(has_side_effects=DATAFLOW),
    )(*[_keep_in_hbm(a) for a in parts], *lands)
    return list(res[:ns]), list(res[ns:ns + nw]), list(res[ns + nw:ns + 2 * nw]), res[ns + 2 * nw]


def _exchange_wait(sems, parts, lands, names, after, label):
    nw = len(names)
    ns = len(sems)

    def body(*refs):
        ins, land_refs = refs[:nw], refs[nw:2 * nw]
        sem_refs = refs[2 * nw:2 * nw + ns]
        _, _, c, _, _, others = _place()
        for w, name in enumerate(names):
            for t, (ox, oy) in enumerate(others):
                k = 2 * (3 * w + t)
                cp = _remote(_piece(ins[w], name, 2 * ox + oy), land_refs[w].at[t], sem_refs[k], sem_refs[k + 1], (ox, oy, c))
                cp.wait_send()
                cp.wait_recv()

    res = pl.pallas_call(
        body, name=label,
        out_shape=tuple(pltpu.HBM(a.shape, a.dtype) for a in (*parts, *lands)),
        in_specs=[IN_HBM] * (2 * nw) + [SEM] * ns + [pl.BlockSpec(memory_space=pl.ANY)],
        out_specs=tuple([IN_HBM] * (2 * nw)),
        input_output_aliases={i: i for i in range(2 * nw)},
        compiler_params=pltpu.CompilerParams(has_side_effects=DATAFLOW),
    )(*parts, *lands, *sems, after)
    return list(res[:nw]), list(res[nw:])


def _sum_chips(name, part, got, chip):
    kind, _ = BIG[name]
    _, r, c = got.shape
    t = _tile(r, (128, 352))
    if kind == "col":
        own = pl.BlockSpec((t, c), lambda i, ch: (i, ch[0]))
    else:
        own = pl.BlockSpec((t, c), lambda i, ch: (ch[0] * (r // t) + i, 0))

    def body(ch_ref, p_ref, g_ref, o_ref):
        del ch_ref
        acc = p_ref[...].astype(F32)
        for j in range(N_CHIPS - 1):
            acc = acc + g_ref[j].astype(F32)
        o_ref[...] = acc

    return _pallas(
        body, name="grad_sum_" + name, prefetch=1, grid=(r // t,),
        in_specs=[own, pl.BlockSpec((N_CHIPS - 1, t, c), lambda i, ch: (0, i, 0))],
        out_specs=pl.BlockSpec((t, c), lambda i, ch: (i, 0)),
        out_shape=_sds((r, c), F32), semantics=("parallel",),
    )(chip, part, got)


def _send_halves(sums, label):
    nw = len(sums)

    def body(*refs):
        ins, outs = refs[:nw], refs[nw:2 * nw]
        send_sems, recv_sems = refs[2 * nw:]
        _, _, _, _, sibling, _ = _place()
        copies = [_remote(ins[w], outs[w], send_sems.at[w], recv_sems.at[w], sibling) for w in range(nw)]
        for cp in copies:
            cp.start()
        for cp in copies:
            cp.wait()

    return pl.pallas_call(
        body, name=label,
        out_shape=[_sds(a.shape, a.dtype) for a in sums],
        in_specs=[HBM] * nw, out_specs=[HBM] * nw,
        scratch_shapes=[pltpu.SemaphoreType.DMA((nw,)), pltpu.SemaphoreType.DMA((nw,))],
    )(*sums)


EARLY_GRADS = ("w_up", "w_down")
LAST_GRADS = ("w_in", "w_out")


def _reduce_finish(started, names, after, chip, tag):
    sems, parts, lands, _ = started
    parts, lands = _exchange_wait(sems, parts, lands, names, after, "grad_exchange_wait_" + tag)
    return [_sum_chips(n, parts[i], lands[i], chip) for i, n in enumerate(names)]


HI = lax.Precision.HIGHEST
MOD_COLS = 6 * D // N_CHIPS
COND_ROWS = 16


def _silu(v):
    return v * _sigmoid(v)


GATHER_ROWS = 48
FFW_COLS = 2 * DFF // N_CHIPS
CONV_COLS = DC // N_CHIPS


def _pack_cond(c, ffn_w, conv_w):
    def body(c_ref, f_ref, w_ref, o_ref):
        o_ref[...] = jnp.zeros_like(o_ref)
        o_ref[0:1, 0:D] = c_ref[...]
        o_ref[8:11, :] = f_ref[...]
        o_ref[16:16 + CW, 0:CONV_COLS] = w_ref[...]

    return _pallas(body, name="pack_cond", out_shape=_sds((GATHER_ROWS, FFW_COLS), F32))(c, ffn_w, conv_w)


def _unpack_cond(got, c_ctx):
    def body(g_ref, c_ref, cond_ref, f_ref, w_ref):
        cond_ref[...] = jnp.zeros_like(cond_ref)
        for d in range(8):
            cond_ref[d:d + 1, :] = g_ref[d * GATHER_ROWS:d * GATHER_ROWS + 1, 0:D]
        cond_ref[8:9, :] = c_ref[...]
        for j in range(N_CHIPS):
            r0 = 2 * j * GATHER_ROWS
            f_ref[:, j * FFW_COLS:(j + 1) * FFW_COLS] = g_ref[r0 + 8:r0 + 11, :]
            w_ref[:, j * CONV_COLS:(j + 1) * CONV_COLS] = g_ref[r0 + 16:r0 + 16 + CW, 0:CONV_COLS]

    return _pallas(body, name="unpack_cond",
                   out_shape=[_sds((COND_ROWS, D), F32), _sds((3, 2 * DFF), F32), _sds((CW, DC), F32)])(got, c_ctx)


def _chip_cols(rows, width):
    return pl.BlockSpec((rows, width), lambda i, ch: (0, ch[0]))


def _whole(shape):
    return pl.BlockSpec(shape, lambda i, ch: (0,) * len(shape))


def _mod_shard(cond, w_mod, b_mod, chip):
    def body(ch_ref, c_ref, w_ref, b_ref, o_ref):
        del ch_ref
        o_ref[...] = jnp.dot(_silu(c_ref[...]), w_ref[...], preferred_element_type=F32, precision=HI) + b_ref[...]

    return _pallas(body, name="mod_fwd", prefetch=1, grid=(1,),
                   in_specs=[_whole((COND_ROWS, D)), _whole((D, MOD_COLS)), _chip_cols(1, MOD_COLS)],
                   out_specs=_whole((COND_ROWS, MOD_COLS)),
                   out_shape=_sds((COND_ROWS, MOD_COLS), F32))(chip, cond, w_mod, b_mod)


def _unpack_mod(mods, dev):
    def body(dev_ref, m_ref, me_ref, c_ref):
        rowi = lax.broadcasted_iota(jnp.int32, (COND_ROWS, MOD_COLS), 0)
        mine, ctx = [], []
        for j in range(N_CHIPS):
            blk = m_ref[2 * j * COND_ROWS:(2 * j + 1) * COND_ROWS, :]
            mine.append(jnp.sum(jnp.where(rowi == dev_ref[0], blk, 0.0), axis=0, keepdims=True))
            ctx.append(blk[8:9, :])
        mine = jnp.concatenate(mine, axis=1)
        ctx = jnp.concatenate(ctx, axis=1)
        for k in range(6):
            me_ref[k:k + 1, :] = mine[:, k * D:(k + 1) * D]
        for k in range(2):
            c_ref[k:k + 1, :] = ctx[:, k * D:(k + 1) * D]

    return _pallas(body, name="unpack_mod", prefetch=1, grid=(1,),
                   in_specs=[_whole(mods.shape)], out_specs=[_whole((6, D)), _whole((2, D))],
                   out_shape=[_sds((6, D), F32), _sds((2, D), F32)])(dev, mods)


def _mod_weight_grad(cond, dmod_all, chip):
    def body(ch_ref, c_ref, d_ref, o_ref):
        del ch_ref
        o_ref[...] = lax.dot_general(_silu(c_ref[...]), d_ref[...], _TN, preferred_element_type=F32, precision=HI)

    return _pallas(body, name="mod_weight_grad", prefetch=1, grid=(1,),
                   in_specs=[_whole((COND_ROWS, D)), _chip_cols(COND_ROWS, MOD_COLS)], out_specs=_whole((D, MOD_COLS)),
                   out_shape=_sds((D, MOD_COLS), F32))(chip, cond, dmod_all)


def _cond_grad_partial(dmod_all, w_mod, chip):
    def body(ch_ref, d_ref, w_ref, o_ref):
        del ch_ref
        o_ref[...] = lax.dot_general(d_ref[...], w_ref[...], (((1,), (1,)), ((), ())), preferred_element_type=F32, precision=HI)

    return _pallas(body, name="cond_grad_partial", prefetch=1, grid=(1,),
                   in_specs=[pl.BlockSpec((8, MOD_COLS), lambda i, ch: (1, ch[0])), _whole((D, MOD_COLS))],
                   out_specs=_whole((8, D)), out_shape=_sds((8, D), F32))(chip, dmod_all, w_mod)


def _adam_math(w, g, m, v):
    nm = ADAM_B1 * m + (1.0 - ADAM_B1) * g
    nv = ADAM_B2 * v + (1.0 - ADAM_B2) * (g * g)
    c1 = 1.0 - ADAM_B1 ** ADAM_STEP
    c2 = 1.0 - ADAM_B2 ** ADAM_STEP
    return -ADAM_LR * ((nm / c1) / (jnp.sqrt(nv / c2) + ADAM_EPS) + ADAM_WD * w), nm, nv


def _cond_update(parts, c_ctx, m, v):
    def body(p_ref, c_ref, m_ref, v_ref, g_ref, d_ref, nm_ref, nv_ref):
        tot = p_ref[0:1, :]
        for j in range(1, N_CHIPS):
            tot = tot + p_ref[16 * j:16 * j + 1, :]
        cv = c_ref[...]
        sg = _sigmoid(cv)
        g = tot * (sg * (1.0 + cv * (1.0 - sg)))
        g_ref[...] = g
        d_ref[...], nm_ref[...], nv_ref[...] = _adam_math(cv, g, m_ref[...], v_ref[...])

    return _pallas(body, name="cond_update", out_shape=[_sds((1, D), F32)] * 4)(parts, c_ctx, m, v)


def _adamw(w, g, m, v, name):
    r, c = w.shape
    t = _tile(r, (128,)) if r % 128 == 0 and r > 128 else r

    def body(w_ref, g_ref, m_ref, v_ref, d_ref, nm_ref, nv_ref):
        d_ref[...], nm_ref[...], nv_ref[...] = _adam_math(w_ref[...], g_ref[...], m_ref[...], v_ref[...])

    blk = pl.BlockSpec((t, c), lambda i: (i, 0))
    return _pallas(body, name=name, grid=(r // t,), in_specs=[blk] * 4, out_specs=[blk] * 3,
                   out_shape=[_sds((r, c), F32)] * 3, semantics=("parallel",))(w, g, m, v)


def _adamw_cols(w, g_all, m, v, chip, name):
    r, c = w.shape

    def body(ch_ref, w_ref, g_ref, m_ref, v_ref, go_ref, d_ref, nm_ref, nv_ref):
        del ch_ref
        g = g_ref[...]
        go_ref[...] = g
        d_ref[...], nm_ref[...], nv_ref[...] = _adam_math(w_ref[...], g, m_ref[...], v_ref[...])

    return _pallas(body, name=name, prefetch=1, grid=(1,),
                   in_specs=[_whole((r, c)), _chip_cols(r, c), _whole((r, c)), _whole((r, c))],
                   out_specs=[_whole((r, c))] * 4, out_shape=[_sds((r, c), F32)] * 4)(chip, w, g_all, m, v)


def _adamw_halves(name, w, own, other, m, v, core, after):
    r, c = w.shape
    half = r // 2
    t = _tile(half, (128, 352))
    nh = half // t

    def pick(mine):
        def index(i, cr):
            first = cr[0] if mine else 1 - cr[0]
            return (jnp.clip(i - first * nh, 0, nh - 1), 0)
        return pl.BlockSpec((t, c), index)

    def body(c_ref, w_ref, own_ref, oth_ref, m_ref, v_ref, after_ref, g_ref, d_ref, nm_ref, nv_ref):
        del after_ref
        g = jnp.where(pl.program_id(0) // nh == c_ref[0], own_ref[...], oth_ref[...])
        g_ref[...] = g
        d_ref[...], nm_ref[...], nv_ref[...] = _adam_math(w_ref[...], g, m_ref[...], v_ref[...])

    blk = pl.BlockSpec((t, c), lambda i, cr: (i, 0))
    return _pallas(body, name="adamw_" + name, prefetch=1, grid=(2 * nh,),
                   in_specs=[blk, pick(True), pick(False), blk, blk, pl.BlockSpec(memory_space=pl.ANY)], out_specs=[blk] * 4,
                   out_shape=[_sds((r, c), F32)] * 4, semantics=("parallel",))(core, w, own, other, m, v, after)


WEIGHTS = ("c_ctx", "w_mod", "b_mod", "g_norm1", "w_in", "rpb", "conv_w", "conv_b", "ln_g", "ln_b", "w_out", "g_norm2",
           "w_up", "ffn_conv_w", "ffn_conv_b", "w_down", "g_final")
PACK = (("dmod", 6 * D), ("dmod_c", 2 * D), ("g_norm1", D), ("g_norm1_ctx", D), ("g_norm2", D), ("g_final", D),
        ("conv_b", DC), ("ln_g", DC), ("ln_b", DC), ("ffn_conv_b", 2 * DFF), ("ffn_conv_w", 3 * 2 * DFF),
        ("conv_w", CW * DC), ("rpb_rev", NH * 16 * LANES), ("loss", LANES))
PACK_OFF = {}
_o = 0
for _n, _w in PACK:
    PACK_OFF[_n] = (_o, _w)
    _o += _w
PACK_N = -(-_o // (8 * LANES)) * (8 * LANES)
VECTORS = {"b_mod": (6 * D, ("dmod", "dmod_c")), "g_norm1": (D, ("g_norm1", "g_norm1_ctx")), "conv_b": (DC, ("conv_b",)),
           "ln_g": (DC, ("ln_g",)), "ln_b": (DC, ("ln_b",)), "g_norm2": (D, ("g_norm2",)),
           "ffn_conv_b": (2 * DFF, ("ffn_conv_b",)), "g_final": (D, ("g_final",))}
RPB_ROWS = NH * (2 * NA_ROWS - 1)
RPB_COLS = 4 * NA_ROWS - 1


def _pack_small(parts):
    arrs, places = [], []
    for name, _ in PACK:
        off, width = PACK_OFF[name]
        group = parts[name]
        rows = group[0].shape[0]
        row_w = sum(a.shape[1] for a in group)
        assert rows * row_w == width, (name, rows, row_w, width)
        col = 0
        for a in group:
            arrs.append(a)
            places.append([off + k * row_w + col for k in range(rows)])
            col += a.shape[1]

    def body(*refs):
        o_ref = refs[-1]
        o_ref[:, _o:PACK_N] = jnp.zeros((1, PACK_N - _o), F32)
        for ref, offs in zip(refs, places):
            n = ref.shape[1]
            for k, off in enumerate(offs):
                o_ref[:, off:off + n] = ref[k:k + 1, :]

    return _pallas(body, name="pack_small_grads", out_shape=_sds((1, PACK_N), F32))(*arrs)


def _small_update(packs, w, m, v):
    names = list(VECTORS)

    def body(*refs):
        it = iter(refs)
        p_ref = next(it)
        wmv = {n: (next(it), next(it), next(it)) for n in names}
        outs = {n: (next(it), next(it), next(it), next(it)) for n in names}
        dmod_ref, cw_ref, fw_ref, rpb_ref, loss_ref = next(it), next(it), next(it), next(it), next(it)

        def total(name):
            off, width = PACK_OFF[name]
            acc = p_ref[0:1, off:off + width]
            for d in range(1, 8):
                acc = acc + p_ref[d:d + 1, off:off + width]
            return acc

        for n in names:
            width, segs = VECTORS[n]
            g = total(segs[0])
            if len(segs) > 1:
                extra = total(segs[1])
                ew = extra.shape[1]
                g = g + extra if ew == width else jnp.concatenate([g[:, :ew] + extra, g[:, ew:]], axis=1)
            w_ref, m_ref, v_ref = wmv[n]
            g_ref, d_ref, nm_ref, nv_ref = outs[n]
            g_ref[...] = g
            d_ref[...], nm_ref[...], nv_ref[...] = _adam_math(w_ref[...], g, m_ref[...], v_ref[...])

        o_dmod = PACK_OFF["dmod"][0]
        dmod_ref[...] = jnp.zeros_like(dmod_ref)
        dmod_ref[0:8, :] = p_ref[:, o_dmod:o_dmod + 6 * D]
        dmod_ref[8:9, 0:2 * D] = total("dmod_c")
        for ref, name, rows in ((cw_ref, "conv_w", CW), (fw_ref, "ffn_conv_w", 3), (rpb_ref, "rpb_rev", NH * 16)):
            flat = total(name)
            n = ref.shape[1]
            for k in range(rows):
                ref[k:k + 1, :] = flat[:, k * n:(k + 1) * n]
        loss_ref[...] = total("loss")

    ins = [packs] + [a[n] for n in names for a in (w, m, v)]
    out_shape = [_sds((1, VECTORS[n][0]), F32) for n in names for _ in range(4)]
    out_shape += [_sds((COND_ROWS, 6 * D), F32), _sds((CW, DC), F32), _sds((3, 2 * DFF), F32), _sds((NH * 16, LANES), F32),
                  _sds((1, LANES), F32)]
    res = _pallas(body, name="small_update", out_shape=out_shape)(*ins)
    per = {n: tuple(res[4 * i:4 * i + 4]) for i, n in enumerate(names)}
    return (per, *res[4 * len(names):])


def _rpb_update(rev, w, m, v):
    def body(r_ref, w_ref, m_ref, v_ref, g_ref, d_ref, nm_ref, nv_ref):
        li = lax.broadcasted_iota(jnp.int32, (LANES, LANES), 0)
        co = lax.broadcasted_iota(jnp.int32, (LANES, LANES), 1)
        lane_of_co0 = GW - 1 + RPB_COLS // 2
        unflip = jnp.where((li == lane_of_co0 - co) & (co < RPB_COLS), 1.0, 0.0).astype(F32)
        g_all = jnp.dot(r_ref[...], unflip, preferred_element_type=F32, precision=HI)
        nr = 2 * NA_ROWS - 1
        for h in range(NH):
            rows = slice(h * nr, (h + 1) * nr)
            g = g_all[h * 16:h * 16 + nr, 0:RPB_COLS]
            g_ref[rows, :] = g
            d_ref[rows, :], nm_ref[rows, :], nv_ref[rows, :] = _adam_math(w_ref[rows, :], g, m_ref[rows, :], v_ref[rows, :])

    return _pallas(body, name="rpb_update", out_shape=[_sds((RPB_ROWS, RPB_COLS), F32)] * 4)(rev, w, m, v)


def kernel(x, c, ctx, c_ctx, w_mod, b_mod, g_norm1, w_in, rpb, conv_w, conv_b, ln_g, ln_b, w_out, g_norm2, w_up, ffn_conv_w, ffn_conv_b, w_down, g_final, loss_target, m_c_ctx, m_w_mod, m_b_mod, m_g_norm1, m_w_in, m_rpb, m_conv_w, m_conv_b, m_ln_g, m_ln_b, m_w_out, m_g_norm2, m_w_up, m_ffn_conv_w, m_ffn_conv_b, m_w_down, m_g_final, v_c_ctx, v_w_mod, v_b_mod, v_g_norm1, v_w_in, v_rpb, v_conv_w, v_conv_b, v_ln_g, v_ln_b, v_w_out, v_g_norm2, v_w_up, v_ffn_conv_w, v_ffn_conv_b, v_w_down, v_g_final):
    w = dict(c_ctx=c_ctx, w_mod=w_mod, b_mod=b_mod, g_norm1=g_norm1, w_in=w_in, rpb=rpb, conv_w=conv_w, conv_b=conv_b,
             ln_g=ln_g, ln_b=ln_b, w_out=w_out, g_norm2=g_norm2, w_up=w_up, ffn_conv_w=ffn_conv_w, ffn_conv_b=ffn_conv_b,
             w_down=w_down, g_final=g_final)
    mom = dict(c_ctx=m_c_ctx, w_mod=m_w_mod, b_mod=m_b_mod, g_norm1=m_g_norm1, w_in=m_w_in, rpb=m_rpb, conv_w=m_conv_w,
               conv_b=m_conv_b, ln_g=m_ln_g, ln_b=m_ln_b, w_out=m_w_out, g_norm2=m_g_norm2, w_up=m_w_up,
               ffn_conv_w=m_ffn_conv_w, ffn_conv_b=m_ffn_conv_b, w_down=m_w_down, g_final=m_g_final)
    var = dict(c_ctx=v_c_ctx, w_mod=v_w_mod, b_mod=v_b_mod, g_norm1=v_g_norm1, w_in=v_w_in, rpb=v_rpb, conv_w=v_conv_w,
               conv_b=v_conv_b, ln_g=v_ln_g, ln_b=v_ln_b, w_out=v_w_out, g_norm2=v_g_norm2, w_up=v_w_up,
               ffn_conv_w=v_ffn_conv_w, ffn_conv_b=v_ffn_conv_b, w_down=v_w_down, g_final=v_g_final)
    xi, yi, ci = lax.axis_index("x"), lax.axis_index("y"), lax.axis_index("c")
    dev = (4 * xi + 2 * yi + ci).astype(jnp.int32).reshape(1)
    chip = (2 * xi + yi).astype(jnp.int32).reshape(1)
    core = ci.astype(jnp.int32).reshape(1)
    c_ctx2 = c_ctx.reshape(1, D)
    g_final2 = g_final.reshape(1, D)
    mom["g_final"], var["g_final"] = m_g_final.reshape(1, D), v_g_final.reshape(1, D)

    got = _gather_small(_pack_cond(c, ffn_conv_w[0], conv_w[0]), "gather_cond")
    cond, ffn_w_all, conv_w_all = _unpack_cond(got, c_ctx2)

    mods = _gather_small(_mod_shard(cond, w_mod[0], b_mod, chip), "gather_mod")
    mod_me, mod_c = _unpack_mod(mods, dev)

    shards = {n: _cast_into_whole(n, w[n][0], chip) for n in BIG_NAMES}
    sems_in, first, token_in = _gather_start([shards["w_in"]], ("w_in",), mod_me, "w_in")
    sems, late, token = _gather_start([shards[n] for n in LATE_NAMES], LATE_NAMES, token_in, "late")
    mod_me = mod_me + token[0:1, 0:1]

    def w_in_all(after):
        arrived = _gather_wait(sems_in, first, ("w_in",), after, "w_in")
        return _forward_halves(list(arrived), ("w_in",), "w_in")[0]

    def late_weights(after):
        arrived = _gather_wait(sems, late, LATE_NAMES, after, "late")
        return _forward_halves(list(arrived), LATE_NAMES, "late")

    rpb_rev = jnp.pad(rpb[0][:, :, ::-1], ((0, 0), (0, 1), (48, LANES - 48 - RPB_COLS))).reshape(NH * 16, LANES)
    vec = dict(g_norm1=g_norm1, g_norm2=g_norm2, g_final=g_final2, conv_w=conv_w_all, conv_b=conv_b, ln_g=ln_g, ln_b=ln_b,
               ffn_conv_w=ffn_w_all, ffn_conv_b=ffn_conv_b)
    started = []

    def begin_early(d_up, d_down):
        started.append(_swap_start([d_up, d_down], EARLY_GRADS, "grad_swap_start_early"))

    def carry_on_early(after):
        sems_, grads_, lands_, _ = started.pop()
        grads_, lands_ = _swap_wait(sems_, grads_, lands_, EARLY_GRADS, after, "grad_swap_wait_early")
        parts_ = [_add_halves(n, grads_[i], lands_[i], core) for i, n in enumerate(EARLY_GRADS)]
        started.append(_exchange_start(parts_, EARLY_GRADS, "grad_exchange_start_early"))
        return started[0][3][0:1, 0:1]

    loss_p, grad_x, d_in, d_out, d_up, d_down, small = _local_step(
        x[0], ctx[0], loss_target[0], mod_me, mod_c, vec, w_in_all, late_weights, rpb_rev, (begin_early, carry_on_early))

    out = {}
    early_own = _reduce_finish(started[0], EARLY_GRADS, grad_x, chip, "early")
    early_other = _send_halves(early_own, "grad_send_early")
    sems_, grads_, lands_, swap_token = _swap_start([d_in, d_out], LAST_GRADS, "grad_swap_start_last")
    for i, n in enumerate(EARLY_GRADS):
        out[n] = _adamw_halves(n, w[n][0], early_own[i], early_other[i], mom[n][0], var[n][0], core, swap_token)
    grads_, lands_ = _swap_wait(sems_, grads_, lands_, LAST_GRADS, out[EARLY_GRADS[-1]][1], "grad_swap_wait_last")
    parts_ = [_add_halves(n, grads_[i], lands_[i], core) for i, n in enumerate(LAST_GRADS)]
    last_started = _exchange_start(parts_, LAST_GRADS, "grad_exchange_start_last")
    loss_p = loss_p + last_started[3][0:1, :]

    parts = dict(dmod=small["dmod"], dmod_c=small["dmod_c"], g_norm1=[small["g_norm1"][0]], g_norm1_ctx=[small["g_norm1"][1]],
                 g_norm2=[small["g_norm2"]], g_final=[small["g_final"]], conv_b=[small["conv_b"]], ln_g=[small["ln_g"]],
                 ln_b=[small["ln_b"]], ffn_conv_b=small["ffn_conv_b"], ffn_conv_w=small["ffn_conv_w"],
                 conv_w=[small["conv_w"]], rpb_rev=[small["rpb_rev"]], loss=[loss_p])
    pack = _pack_small(parts).reshape(8, PACK_N // 8)
    packs = _gather_small(pack, "gather_small_grads").reshape(8, PACK_N)
    w2 = dict(w, g_final=g_final2)
    per, dmod_all, g_conv_w_all, g_ffn_w_all, g_rpb_rev, loss_row = _small_update(packs, w2, mom, var)

    out.update(per)
    out["c_ctx"] = _cond_update(
        _gather_small(_cond_grad_partial(dmod_all, w_mod[0], chip), "gather_cond_grad"),
        c_ctx2, m_c_ctx.reshape(1, D), v_c_ctx.reshape(1, D))
    g_w_mod = _mod_weight_grad(cond, dmod_all, chip)
    out["w_mod"] = (g_w_mod, *_adamw(w_mod[0], g_w_mod, m_w_mod[0], v_w_mod[0], "adamw_w_mod"))
    behind = out["w_mod"][1][0:1, 0:1] + out["c_ctx"][1][0:1, 0:1]
    last_own = _reduce_finish(last_started, LAST_GRADS, behind, chip, "last")
    last_other = _send_halves(last_own, "grad_send_last")
    for i, n in enumerate(LAST_GRADS):
        out[n] = _adamw_halves(n, w[n][0], last_own[i], last_other[i], mom[n][0], var[n][0], core, last_other[i])
    out["conv_w"] = _adamw_cols(conv_w[0], g_conv_w_all, m_conv_w[0], v_conv_w[0], chip, "adamw_conv_w")
    out["ffn_conv_w"] = _adamw_cols(ffn_conv_w[0], g_ffn_w_all, m_ffn_conv_w[0], v_ffn_conv_w[0], chip, "adamw_ffn_conv_w")
    flat = lambda a: a.reshape(RPB_ROWS, RPB_COLS)
    out["rpb"] = _rpb_update(g_rpb_rev, flat(rpb), flat(m_rpb), flat(v_rpb))

    res = [[out[n][k].reshape(w[n].shape) for n in WEIGHTS] for k in range(4)]
    return (loss_row[0, 0], grad_x[None], *res[0], *res[1], *res[2], *res[3])
```

```python
import functools

import jax
import jax.numpy as jnp
from jax import lax
from jax.experimental import pallas as pl
from jax.experimental.pallas import tpu as pltpu

F32 = jnp.float32
BF16 = jnp.bfloat16
MXU_DTYPE = jnp.bfloat16

D = 1024
CTX = 256
GW = 64
DA = 512
NH = 8
HD = 64
DC = 512
CW = 31
DFF = 2816
NIN = 3 * DA + 2 * DC
EPS = 1e-6
SCALE = HD ** -0.5
NEG = -1e30
NA_ROWS = 8
PAIR_ROWS = NA_ROWS + 1
TAB_BLOCKS = 17
LANES = 128
VMEM_LIMIT = 56 * 1024 * 1024

ADAM_LR = 0.001
ADAM_B1 = 0.9
ADAM_B2 = 0.999
ADAM_EPS = 1e-08
ADAM_WD = 0.01
ADAM_STEP = 10

MESH = pl.DeviceIdType.MESH


def _pallas(body, *, name, semantics=None, vmem=VMEM_LIMIT, prefetch=0, **kw):
    params = dict(vmem_limit_bytes=vmem)
    if semantics is not None:
        params["dimension_semantics"] = semantics
    if prefetch:
        kw["grid_spec"] = pltpu.PrefetchScalarGridSpec(
            num_scalar_prefetch=prefetch, grid=kw.pop("grid"), in_specs=kw.pop("in_specs"), out_specs=kw.pop("out_specs"),
            scratch_shapes=kw.pop("scratch_shapes", ()))
    return pl.pallas_call(body, name=name, compiler_params=pltpu.CompilerParams(**params), **kw)


def _sds(shape, dtype):
    return jax.ShapeDtypeStruct(shape, dtype)


def _vec_spec(n):
    return pl.BlockSpec((1, n), lambda *_: (0, 0))


def _colsum8(x):
    t, n = x.shape
    return jnp.sum(x.reshape(t // 8, 8, n), axis=0)


def _sigmoid(x):
    return 0.5 * jnp.tanh(0.5 * x) + 0.5


def _pieces(arrs, tile):
    lo, out = 0, []
    for a in arrs:
        nt = a.shape[1] // tile
        assert nt * tile == a.shape[1], (a.shape, tile)
        out.append((lo, nt))
        lo += nt
    return out


def _mm(a, b, *, mode, m, n, k, tm, tn, tk, out_dtype, name, a_off=(0, 0), b_off=(0, 0), k_outer=False,
        out_total=None, o_off=(0, 0), into=None):
    a_list = list(a) if isinstance(a, (list, tuple)) else [a]
    b_list = list(b) if isinstance(b, (list, tuple)) else [b]
    assert m % tm == 0 and n % tn == 0 and k % tk == 0, (name, m, n, k, tm, tn, tk)
    gi, gj, nk = m // tm, n // tn, k // tk
    a_tile = tm if mode == "tn" else tk
    a_pc = _pieces(a_list, a_tile) if len(a_list) > 1 else [(0, 1 << 30)]
    if mode == "nt":
        assert len(b_list) == 1
    b_pc = _pieces(b_list, tn) if len(b_list) > 1 else [(0, 1 << 30)]
    dims = {"nn": (((1,), (0,)), ((), ())), "nt": (((1,), (1,)), ((), ())), "tn": (((0,), (0,)), ((), ()))}[mode]
    k_outer = k_outer and nk > 1

    def ijk(fn):
        return (lambda i, kk, j: fn(i, j, kk)) if k_outer else fn

    def a_spec(lo, cnt):
        def loc(idx):
            return idx + a_off[1] if len(a_list) == 1 else jnp.clip(idx - lo, 0, cnt - 1)
        if mode == "tn":
            return pl.BlockSpec((tk, tm), ijk(lambda i, j, kk: (kk + a_off[0], loc(i))))
        return pl.BlockSpec((tm, tk), ijk(lambda i, j, kk: (i + a_off[0], loc(kk))))

    def b_spec(lo, cnt):
        def loc(idx):
            return idx + b_off[1] if len(b_list) == 1 else jnp.clip(idx - lo, 0, cnt - 1)
        if mode == "nt":
            return pl.BlockSpec((tn, tk), ijk(lambda i, j, kk: (j + b_off[0], kk + b_off[1])))
        return pl.BlockSpec((tk, tn), ijk(lambda i, j, kk: (kk + b_off[0], loc(j))))

    na, nb = len(a_list), len(b_list)
    in_place = nk > 1 and out_dtype == F32 and not k_outer

    n_in = na + nb + (into is not None)

    def body(*refs):
        a_refs, b_refs, o_ref = refs[:na], refs[na:na + nb], refs[n_in]
        if k_outer:
            i, kk, j = pl.program_id(0), pl.program_id(1), pl.program_id(2)
            acc = refs[n_in + 1].at[j]
        else:
            i, j, kk = pl.program_id(0), pl.program_id(1), pl.program_id(2)
            acc = o_ref if in_place else (refs[n_in + 1] if nk > 1 else None)
        a_idx = i if mode == "tn" else kk

        def step(ar, br):
            p = lax.dot_general(ar[...].astype(MXU_DTYPE), br[...].astype(MXU_DTYPE), dims,
                                preferred_element_type=F32)
            if nk == 1:
                o_ref[...] = p.astype(out_dtype)
                return

            @pl.when(kk == 0)
            def _():
                acc[...] = p

            @pl.when(kk > 0)
            def _():
                acc[...] += p

            if not in_place:
                @pl.when(kk == nk - 1)
                def _():
                    o_ref[...] = acc[...].astype(out_dtype)

        for pa, (alo, acnt) in enumerate(a_pc):
            for pb, (blo, bcnt) in enumerate(b_pc):
                if na == 1 and nb == 1:
                    step(a_refs[0], b_refs[0])
                else:
                    cond = (a_idx >= alo) & (a_idx < alo + acnt) & (j >= blo) & (j < blo + bcnt)
                    pl.when(cond)(functools.partial(step, a_refs[pa], b_refs[pb]))

    if k_outer:
        grid = (gi, nk, gj)
        o_spec = pl.BlockSpec((tm, tn), lambda i, kk, j: (i + o_off[0], jnp.where(kk == nk - 1, j, 0) + o_off[1]))
        scratch = [pltpu.VMEM((gj, tm, tn), F32)]
        semantics = ("parallel", "arbitrary", "arbitrary")
    else:
        grid = (gi, gj, nk)
        o_spec = pl.BlockSpec((tm, tn), lambda i, j, kk: (i + o_off[0], j + o_off[1]))
        scratch = [pltpu.VMEM((tm, tn), F32)] if nk > 1 and not in_place else []
        semantics = ("parallel", "parallel", "arbitrary")
    ins = [*a_list, *b_list]
    in_specs = [a_spec(*p) for p in a_pc] + [b_spec(*p) for p in b_pc]
    extra = {}
    if into is not None:
        extra["input_output_aliases"] = {len(ins): 0}
        ins.append(into)
        in_specs.append(pl.BlockSpec(memory_space=pl.ANY))
    return _pallas(
        body, name=name, grid=grid, in_specs=in_specs,
        out_specs=o_spec, out_shape=_sds(out_total or (m, n), out_dtype), scratch_shapes=scratch, semantics=semantics,
        **extra,
    )(*ins)


ROW_TILE = 256


def _rmsmod_fwd(x, ctx, g, sc, sh, csc, csh):
    s = x.shape[0]
    nt = s // ROW_TILE
    assert ctx.shape[0] == ROW_TILE

    def body(x_ref, c_ref, g_ref, sc_ref, sh_ref, csc_ref, csh_ref, o_ref):
        is_ctx = pl.program_id(0) == nt
        xv = jnp.where(is_ctx, c_ref[...], x_ref[...])
        scv = jnp.where(is_ctx, csc_ref[...], sc_ref[...])
        shv = jnp.where(is_ctx, csh_ref[...], sh_ref[...])
        r = lax.rsqrt(jnp.mean(xv * xv, axis=-1, keepdims=True) + EPS)
        y = xv * r * g_ref[...]
        o_ref[...] = (y * (1.0 + scv) + shv).astype(o_ref.dtype)

    return _pallas(
        body, name="rmsmod1_fwd", grid=(nt + 1,),
        in_specs=[pl.BlockSpec((ROW_TILE, D), lambda i: (jnp.minimum(i, nt - 1), 0)),
                  pl.BlockSpec((ROW_TILE, D), lambda i: (0, 0))] + [_vec_spec(D)] * 5,
        out_specs=pl.BlockSpec((ROW_TILE, D), lambda i: (i, 0)),
        out_shape=_sds((s + CTX, D), MXU_DTYPE),
        semantics=("arbitrary",),
    )(x, ctx, g, sc, sh, csc, csh)


def _resid_rmsmod_fwd(x, y, gt, g, sc, sh):
    s = x.shape[0]

    def body(x_ref, y_ref, gt_ref, g_ref, sc_ref, sh_ref, x1_ref, h_ref):
        x1 = x_ref[...] + gt_ref[...] * y_ref[...]
        x1_ref[...] = x1
        r = lax.rsqrt(jnp.mean(x1 * x1, axis=-1, keepdims=True) + EPS)
        h_ref[...] = ((x1 * r * g_ref[...]) * (1.0 + sc_ref[...]) + sh_ref[...]).astype(h_ref.dtype)

    row = pl.BlockSpec((ROW_TILE, D), lambda i: (i, 0))
    return _pallas(
        body, name="resid_rmsmod2_fwd", grid=(s // ROW_TILE,),
        in_specs=[row, row] + [_vec_spec(D)] * 4,
        out_specs=[row, row],
        out_shape=[_sds((s, D), F32), _sds((s, D), MXU_DTYPE)],
        semantics=("parallel",),
    )(x, y, gt, g, sc, sh)


def _final_fwd_bwd(x1, z, gt2, gf, tgt):
    s = x1.shape[0]
    nt = s // ROW_TILE

    def body(x1_ref, z_ref, gt_ref, gf_ref, t_ref, dx2_ref, dz_ref, loss_ref, dgt_ref, dgf_ref, a_loss, a_gt, a_gf):
        i = pl.program_id(0)

        @pl.when(i == 0)
        def _():
            a_loss[...] = jnp.zeros_like(a_loss)
            a_gt[...] = jnp.zeros_like(a_gt)
            a_gf[...] = jnp.zeros_like(a_gf)

        zv = z_ref[...]
        gt = gt_ref[...]
        gf_ = gf_ref[...]
        x2 = x1_ref[...] + gt * zv
        r = lax.rsqrt(jnp.mean(x2 * x2, axis=-1, keepdims=True) + EPS)
        xn = x2 * r
        e = xn * gf_ - t_ref[...]
        a_loss[...] += _colsum8(e * e)
        dyo = e * (1.0 / D)
        a_gf[...] += _colsum8(dyo * xn)
        gdy = gf_ * dyo
        dx2 = r * gdy - xn * (r * r) * jnp.mean(x2 * gdy, axis=-1, keepdims=True)
        dx2_ref[...] = dx2
        dz_ref[...] = (gt * dx2).astype(dz_ref.dtype)
        a_gt[...] += _colsum8(dx2 * zv)

        @pl.when(i == nt - 1)
        def _():
            tot = jnp.sum(jnp.sum(a_loss[...], axis=0, keepdims=True), axis=1, keepdims=True) * (0.5 / D)
            loss_ref[...] = jnp.broadcast_to(tot, loss_ref.shape)
            dgt_ref[...] = jnp.sum(a_gt[...], axis=0, keepdims=True)
            dgf_ref[...] = jnp.sum(a_gf[...], axis=0, keepdims=True)

    row = pl.BlockSpec((ROW_TILE, D), lambda i: (i, 0))
    return _pallas(
        body, name="final_norm_loss", grid=(nt,),
        in_specs=[row, row, _vec_spec(D), _vec_spec(D), row],
        out_specs=[row, row, _vec_spec(LANES), _vec_spec(D), _vec_spec(D)],
        out_shape=[_sds((s, D), F32), _sds((s, D), MXU_DTYPE), _sds((1, LANES), F32), _sds((1, D), F32), _sds((1, D), F32)],
        scratch_shapes=[pltpu.VMEM((8, D), F32)] * 3,
        semantics=("arbitrary",),
    )(x1, z, gt2, gf, tgt)


def _rmsmod_bwd(xin, dh, g, sc, *, name, dh_row_off=0, add=None, resid=None):
    s = xin.shape[0]
    nt = s // ROW_TILE
    want_dx = add is not None
    assert resid is None or want_dx

    def body(*refs):
        it = iter(refs)
        x_ref, dh_ref, g_ref, sc_ref = next(it), next(it), next(it), next(it)
        add_ref = next(it) if want_dx else None
        gt_ref, y_ref = (next(it), next(it)) if resid is not None else (None, None)
        dsh_ref, dsc_ref, dg_ref = next(it), next(it), next(it)
        dx_ref = next(it) if want_dx else None
        dy_ref, dgt_ref = (next(it), next(it)) if resid is not None else (None, None)
        a_sh, a_sc, a_g = next(it), next(it), next(it)
        a_gt = next(it) if resid is not None else None
        i = pl.program_id(0)

        @pl.when(i == 0)
        def _():
            a_sh[...] = jnp.zeros_like(a_sh)
            a_sc[...] = jnp.zeros_like(a_sc)
            a_g[...] = jnp.zeros_like(a_g)
            if a_gt is not None:
                a_gt[...] = jnp.zeros_like(a_gt)

        xv = x_ref[...]
        dhv = dh_ref[...]
        gv = g_ref[...]
        r = lax.rsqrt(jnp.mean(xv * xv, axis=-1, keepdims=True) + EPS)
        xn = xv * r
        a_sh[...] += _colsum8(dhv)
        a_sc[...] += _colsum8(dhv * (xn * gv))
        dn = dhv * (1.0 + sc_ref[...])
        a_g[...] += _colsum8(dn * xn)
        if want_dx:
            gdn = gv * dn
            dx = add_ref[...] + r * gdn - xn * (r * r) * jnp.mean(xv * gdn, axis=-1, keepdims=True)
            dx_ref[...] = dx
            if resid is not None:
                dy_ref[...] = (gt_ref[...] * dx).astype(dy_ref.dtype)
                a_gt[...] += _colsum8(dx * y_ref[...])

        @pl.when(i == nt - 1)
        def _():
            dsh_ref[...] = jnp.sum(a_sh[...], axis=0, keepdims=True)
            dsc_ref[...] = jnp.sum(a_sc[...], axis=0, keepdims=True)
            dg_ref[...] = jnp.sum(a_g[...], axis=0, keepdims=True)
            if a_gt is not None:
                dgt_ref[...] = jnp.sum(a_gt[...], axis=0, keepdims=True)

    row = pl.BlockSpec((ROW_TILE, D), lambda i: (i, 0))
    ins = [xin, dh, g, sc]
    in_specs = [row, pl.BlockSpec((ROW_TILE, D), lambda i: (i + dh_row_off, 0)), _vec_spec(D), _vec_spec(D)]
    out_specs = [_vec_spec(D)] * 3
    out_shape = [_sds((1, D), F32)] * 3
    scratch = [pltpu.VMEM((8, D), F32)] * 3
    if want_dx:
        ins.append(add)
        in_specs.append(row)
        out_specs.append(row)
        out_shape.append(_sds((s, D), F32))
    if resid is not None:
        ins += [resid[0], resid[1]]
        in_specs += [_vec_spec(D), row]
        out_specs += [row, _vec_spec(D)]
        out_shape += [_sds((s, D), MXU_DTYPE), _sds((1, D), F32)]
        scratch.append(pltpu.VMEM((8, D), F32))
    return _pallas(body, name=name, grid=(nt,), in_specs=in_specs, out_specs=out_specs, out_shape=out_shape,
                   scratch_shapes=scratch, semantics=("arbitrary",))(*ins)


FF_TILE = 128
FF_CHUNK = 128
HALO = 8


def _shift3(pad_ref, r0, ch):
    return tuple(pad_ref[pl.ds(r0 + HALO + d, ch), :] for d in (-1, 0, 1))


def _fill_padded(pad_ref, src_ref, s, ch, halo):
    zeros = jnp.zeros((halo, pad_ref.shape[1]), F32)
    pad_ref[0:halo, :] = zeros
    pad_ref[s + halo:s + 2 * halo, :] = zeros

    def cp(c, carry):
        r0 = pl.multiple_of(c * ch, ch)
        pad_ref[pl.ds(r0 + halo, ch), :] = src_ref[pl.ds(r0, ch), :].astype(F32)
        return carry

    lax.fori_loop(0, s // ch, cp, 0)


def _ffn_act_fwd(u, w, b):
    s = u.shape[0]
    nj = DFF // FF_TILE
    ch = FF_CHUNK

    def body(ug_ref, uv_ref, wg_ref, wv_ref, bg_ref, bv_ref, f_ref, gpad, vpad):
        _fill_padded(gpad, ug_ref, s, ch, HALO)
        _fill_padded(vpad, uv_ref, s, ch, HALO)

        def conv(pad, w_ref, b_ref, r0):
            prev, cur, nxt = _shift3(pad, r0, ch)
            return w_ref[0:1, :] * prev + w_ref[1:2, :] * cur + w_ref[2:3, :] * nxt + b_ref[...]

        def step(c, carry):
            r0 = pl.multiple_of(c * ch, ch)
            gc = conv(gpad, wg_ref, bg_ref, r0)
            vc = conv(vpad, wv_ref, bv_ref, r0)
            f_ref[pl.ds(r0, ch), :] = (gc * _sigmoid(gc) * vc).astype(f_ref.dtype)
            return carry

        lax.fori_loop(0, s // ch, step, 0)

    col = lambda off: pl.BlockSpec((s, FF_TILE), lambda j: (0, j + off))
    wsp = lambda off: pl.BlockSpec((3, FF_TILE), lambda j: (0, j + off))
    bsp = lambda off: pl.BlockSpec((1, FF_TILE), lambda j: (0, j + off))
    return _pallas(
        body, name="ffn_act_fwd", grid=(nj,),
        in_specs=[col(0), col(nj), wsp(0), wsp(nj), bsp(0), bsp(nj)],
        out_specs=col(0), out_shape=_sds((s, DFF), MXU_DTYPE),
        scratch_shapes=[pltpu.VMEM((s + 2 * HALO, FF_TILE), F32)] * 2,
        semantics=("parallel",),
    )(u, u, w, w, b, b)


def _ffn_act_bwd(u, df, w, b):
    s = u.shape[0]
    nj = DFF // FF_TILE
    ch = FF_CHUNK

    def body(ug_ref, uv_ref, df_ref, wg_ref, wv_ref, bg_ref, bv_ref,
             dug_ref, duv_ref, dwg_ref, dwv_ref, dbg_ref, dbv_ref, gpad, vpad, dgpad, dvpad, acc):
        _fill_padded(gpad, ug_ref, s, ch, HALO)
        _fill_padded(vpad, uv_ref, s, ch, HALO)
        zeros = jnp.zeros((HALO, FF_TILE), F32)
        for p in (dgpad, dvpad):
            p[0:HALO, :] = zeros
            p[s + HALO:s + 2 * HALO, :] = zeros
        acc[...] = jnp.zeros_like(acc)

        def step(c, carry):
            r0 = pl.multiple_of(c * ch, ch)
            gs = _shift3(gpad, r0, ch)
            vs = _shift3(vpad, r0, ch)
            gc = wg_ref[0:1, :] * gs[0] + wg_ref[1:2, :] * gs[1] + wg_ref[2:3, :] * gs[2] + bg_ref[...]
            vc = wv_ref[0:1, :] * vs[0] + wv_ref[1:2, :] * vs[1] + wv_ref[2:3, :] * vs[2] + bv_ref[...]
            sg = _sigmoid(gc)
            dfv = df_ref[pl.ds(r0, ch), :].astype(F32)
            dgc = dfv * vc * (sg * (1.0 + gc * (1.0 - sg)))
            dvc = dfv * (gc * sg)
            dgpad[pl.ds(r0 + HALO, ch), :] = dgc
            dvpad[pl.ds(r0 + HALO, ch), :] = dvc
            for t in range(3):
                acc[8 * t:8 * t + 8, :] += _colsum8(dgc * gs[t])
                acc[24 + 8 * t:32 + 8 * t, :] += _colsum8(dvc * vs[t])
            acc[48:56, :] += _colsum8(dgc)
            acc[56:64, :] += _colsum8(dvc)
            return carry

        lax.fori_loop(0, s // ch, step, 0)

        def step2(c, carry):
            r0 = pl.multiple_of(c * ch, ch)
            for pad, w_ref, o_ref in ((dgpad, wg_ref, dug_ref), (dvpad, wv_ref, duv_ref)):
                prev, cur, nxt = _shift3(pad, r0, ch)
                o_ref[pl.ds(r0, ch), :] = (w_ref[0:1, :] * nxt + w_ref[1:2, :] * cur + w_ref[2:3, :] * prev).astype(o_ref.dtype)
            return carry

        lax.fori_loop(0, s // ch, step2, 0)
        for t in range(3):
            dwg_ref[t:t + 1, :] = jnp.sum(acc[8 * t:8 * t + 8, :], axis=0, keepdims=True)
            dwv_ref[t:t + 1, :] = jnp.sum(acc[24 + 8 * t:32 + 8 * t, :], axis=0, keepdims=True)
        dbg_ref[...] = jnp.sum(acc[48:56, :], axis=0, keepdims=True)
        dbv_ref[...] = jnp.sum(acc[56:64, :], axis=0, keepdims=True)

    col = lambda off: pl.BlockSpec((s, FF_TILE), lambda j: (0, j + off))
    wsp = lambda off: pl.BlockSpec((3, FF_TILE), lambda j: (0, j + off))
    bsp = lambda off: pl.BlockSpec((1, FF_TILE), lambda j: (0, j + off))
    return _pallas(
        body, name="ffn_act_bwd", grid=(nj,),
        in_specs=[col(0), col(nj), col(0), wsp(0), wsp(nj), bsp(0), bsp(nj)],
        out_specs=[col(0), col(0), wsp(0), wsp(0), bsp(0), bsp(0)],
        out_shape=[_sds((s, DFF), MXU_DTYPE)] * 2 + [_sds((3, DFF), F32)] * 2 + [_sds((1, DFF), F32)] * 2,
        scratch_shapes=[pltpu.VMEM((s + 2 * HALO, FF_TILE), F32)] * 4 + [pltpu.VMEM((64, FF_TILE), F32)],
        semantics=("parallel",),
    )(u, u, df, w, w, b, b)


CONV_CHUNK = 64
CONV_HALO = 16


def _tap(pad_ref, r0, k):
    return pad_ref[pl.ds(r0 + CONV_HALO - CW // 2 + k, CONV_CHUNK), :]


def _glu_into(pad_ref, a_ref, g_ref, s):
    zeros = jnp.zeros((CONV_HALO, LANES), F32)
    pad_ref[0:CONV_HALO, :] = zeros
    pad_ref[s + CONV_HALO:s + 2 * CONV_HALO, :] = zeros

    def cp(c, carry):
        r0 = pl.multiple_of(c * ROW_TILE, ROW_TILE)
        pad_ref[pl.ds(r0 + CONV_HALO, ROW_TILE), :] = a_ref[pl.ds(r0, ROW_TILE), :] * _sigmoid(g_ref[pl.ds(r0, ROW_TILE), :])
        return carry

    lax.fori_loop(0, s // ROW_TILE, cp, 0)


def _conf_conv_fwd(ag, conv_w, conv_b):
    s = ag.shape[0]
    nc = DC // LANES

    def body(a_ref, g_ref, w_ref, b_ref, o_ref, upad):
        _glu_into(upad, a_ref, g_ref, s)

        def step(c, carry):
            r0 = pl.multiple_of(c * CONV_CHUNK, CONV_CHUNK)
            acc = jnp.broadcast_to(b_ref[...], (CONV_CHUNK, LANES))
            for k in range(CW):
                acc = acc + w_ref[k:k + 1, :] * _tap(upad, r0, k)
            o_ref[pl.ds(r0, CONV_CHUNK), :] = acc
            return carry

        lax.fori_loop(0, s // CONV_CHUNK, step, 0)

    col = lambda off: pl.BlockSpec((s, LANES), lambda c: (0, c + off))
    return _pallas(
        body, name="conf_conv_fwd", grid=(nc,),
        in_specs=[col(0), col(nc), pl.BlockSpec((CW, LANES), lambda c: (0, c)), pl.BlockSpec((1, LANES), lambda c: (0, c))],
        out_specs=col(0), out_shape=_sds((s, DC), F32),
        scratch_shapes=[pltpu.VMEM((s + 2 * CONV_HALO, LANES), F32)],
        semantics=("parallel",),
    )(ag, ag, conv_w, conv_b)


def _ln_stats(x):
    mu = jnp.mean(x, axis=-1, keepdims=True)
    xc = x - mu
    var = jnp.mean(xc * xc, axis=-1, keepdims=True)
    rstd = lax.rsqrt(var + EPS)
    return xc * rstd, rstd


def _conf_ln_fwd(u1, ln_g, ln_b, ycat):
    s = u1.shape[0]

    def body(u_ref, g_ref, b_ref, ycat_ref, o_ref):
        del ycat_ref
        xhat, _ = _ln_stats(u_ref[...])
        y = xhat * g_ref[...] + b_ref[...]
        o_ref[...] = (y * _sigmoid(y)).astype(o_ref.dtype)

    return _pallas(
        body, name="conf_ln_fwd", grid=(s // ROW_TILE,),
        in_specs=[pl.BlockSpec((ROW_TILE, DC), lambda i: (i, 0)), _vec_spec(DC), _vec_spec(DC),
                  pl.BlockSpec(memory_space=pl.ANY)],
        out_specs=pl.BlockSpec((ROW_TILE, DC), lambda i: (i, 1)),
        out_shape=_sds(ycat.shape, ycat.dtype),
        input_output_aliases={3: 0},
        semantics=("parallel",),
    )(u1, ln_g, ln_b, ycat)


def _conf_ln_bwd(dycat, u1, ln_g, ln_b):
    s = u1.shape[0]
    nt = s // ROW_TILE

    def body(dy_ref, u_ref, g_ref, b_ref, du_ref, dg_ref, db_ref, a_g, a_b):
        i = pl.program_id(0)

        @pl.when(i == 0)
        def _():
            a_g[...] = jnp.zeros_like(a_g)
            a_b[...] = jnp.zeros_like(a_b)

        xhat, rstd = _ln_stats(u_ref[...])
        gv = g_ref[...]
        y = xhat * gv + b_ref[...]
        sg = _sigmoid(y)
        dyl = dy_ref[...] * (sg * (1.0 + y * (1.0 - sg)))
        a_g[...] += _colsum8(dyl * xhat)
        a_b[...] += _colsum8(dyl)
        dxh = dyl * gv
        du_ref[...] = rstd * (dxh - jnp.mean(dxh, axis=-1, keepdims=True)
                              - xhat * jnp.mean(dxh * xhat, axis=-1, keepdims=True))

        @pl.when(i == nt - 1)
        def _():
            dg_ref[...] = jnp.sum(a_g[...], axis=0, keepdims=True)
            db_ref[...] = jnp.sum(a_b[...], axis=0, keepdims=True)

    return _pallas(
        body, name="conf_ln_bwd", grid=(nt,),
        in_specs=[pl.BlockSpec((ROW_TILE, DC), lambda i: (i, 1)), pl.BlockSpec((ROW_TILE, DC), lambda i: (i, 0)),
                  _vec_spec(DC), _vec_spec(DC)],
        out_specs=[pl.BlockSpec((ROW_TILE, DC), lambda i: (i, 0)), _vec_spec(DC), _vec_spec(DC)],
        out_shape=[_sds((s, DC), F32), _sds((1, DC), F32), _sds((1, DC), F32)],
        scratch_shapes=[pltpu.VMEM((8, DC), F32)] * 2,
        semantics=("arbitrary",),
    )(dycat, u1, ln_g, ln_b)


def _conf_conv_bwd(ag, du1, conv_w, rows_out):
    s = ag.shape[0]
    nc = DC // LANES

    def body(a_ref, g_ref, d_ref, w_ref, da_ref, dg_ref, dw_ref, db_ref, upad, dpad, acc):
        _glu_into(upad, a_ref, g_ref, s)
        _fill_padded(dpad, d_ref, s, ROW_TILE, CONV_HALO)
        acc[...] = jnp.zeros_like(acc)

        def step(c, carry):
            r0 = pl.multiple_of(c * CONV_CHUNK, CONV_CHUNK)
            dcur = dpad[pl.ds(r0 + CONV_HALO, CONV_CHUNK), :]
            du0 = jnp.zeros((CONV_CHUNK, LANES), F32)
            for k in range(CW):
                du0 = du0 + w_ref[k:k + 1, :] * _tap(dpad, r0, CW - 1 - k)
                acc[8 * k:8 * k + 8, :] += _colsum8(dcur * _tap(upad, r0, k))
            acc[8 * CW:8 * CW + 8, :] += _colsum8(dcur)
            av = a_ref[pl.ds(r0, CONV_CHUNK), :]
            sg = _sigmoid(g_ref[pl.ds(r0, CONV_CHUNK), :])
            da_ref[pl.ds(r0, CONV_CHUNK), :] = (du0 * sg).astype(da_ref.dtype)
            dg_ref[pl.ds(r0, CONV_CHUNK), :] = (du0 * av * (sg * (1.0 - sg))).astype(dg_ref.dtype)
            return carry

        lax.fori_loop(0, s // CONV_CHUNK, step, 0)
        if rows_out > s:
            zeros = jnp.zeros((rows_out - s, LANES), da_ref.dtype)
            da_ref[s:rows_out, :] = zeros
            dg_ref[s:rows_out, :] = zeros
        for k in range(CW):
            dw_ref[k:k + 1, :] = jnp.sum(acc[8 * k:8 * k + 8, :], axis=0, keepdims=True)
        db_ref[...] = jnp.sum(acc[8 * CW:8 * CW + 8, :], axis=0, keepdims=True)

    col = lambda off: pl.BlockSpec((s, LANES), lambda c: (0, c + off))
    ocol = pl.BlockSpec((rows_out, LANES), lambda c: (0, c))
    return _pallas(
        body, name="conf_conv_bwd", grid=(nc,),
        in_specs=[col(0), col(nc), col(0), pl.BlockSpec((CW, LANES), lambda c: (0, c))],
        out_specs=[ocol, ocol, pl.BlockSpec((CW, LANES), lambda c: (0, c)), pl.BlockSpec((1, LANES), lambda c: (0, c))],
        out_shape=[_sds((rows_out, DC), MXU_DTYPE)] * 2 + [_sds((CW, DC), F32), _sds((1, DC), F32)],
        scratch_shapes=[pltpu.VMEM((s + 2 * CONV_HALO, LANES), F32)] * 2 + [pltpu.VMEM((8 * (CW + 1), LANES), F32)],
        semantics=("parallel",),
    )(ag, ag, du1, conv_w)


Q_TILE = 2 * GW
K_WIN = PAIR_ROWS * GW


def _bias_table(rpb_rev):
    def body(p_ref, t_ref):
        kcol = lax.broadcasted_iota(jnp.int32, (GW, LANES), 0)
        lane = lax.broadcasted_iota(jnp.int32, (GW, LANES), 1)
        qcol = lane % GW
        cs = jnp.clip(qcol - NA_ROWS, 0, GW - 2 * NA_ROWS)
        colvalid = (kcol >= cs) & (kcol < cs + 2 * NA_ROWS)
        neg = jnp.full((GW, LANES), NEG, F32)

        def skew(h, ro, shift):
            if ro < 0 or ro >= 2 * NA_ROWS - 1:
                return neg
            row = jnp.broadcast_to(p_ref[h * 16 + ro:h * 16 + ro + 1, :], (GW, LANES))
            return pltpu.roll(row, shift, 1, stride=1, stride_axis=0)

        for h in range(NH):
            for b in range(TAB_BLOCKS):
                val = jnp.where(lane < GW, skew(h, b - 1, GW + 1), skew(h, b - 2, 1))
                t_ref[h, b * GW:(b + 1) * GW, :] = jnp.where(colvalid, val, neg)

    return _pallas(body, name="attn_bias_table", out_shape=_sds((NH, TAB_BLOCKS * GW, LANES), F32))(rpb_rev)


def _rpb_grad(tt):
    def body(t_ref, o_ref):
        lane = lax.broadcasted_iota(jnp.int32, (GW, LANES), 1)
        si = lax.broadcasted_iota(jnp.int32, (GW, GW), 0)
        ti = lax.broadcasted_iota(jnp.int32, (GW, GW), 1)
        flip = jnp.where(si + ti == GW - 1, 1.0, 0.0).astype(F32)
        o_ref[...] = jnp.zeros_like(o_ref)
        for h in range(NH):
            for ro in range(2 * NA_ROWS - 1):
                lo = t_ref[h, (ro + 1) * GW:(ro + 2) * GW, :]
                hi = t_ref[h, (ro + 2) * GW:(ro + 3) * GW, :]
                g = jnp.where(lane < GW, lo + pltpu.roll(hi, GW, 1), 0.0)
                gf = jnp.dot(flip, g, preferred_element_type=F32, precision=lax.Precision.HIGHEST)
                sk = pltpu.roll(gf, 0, 1, stride=1, stride_axis=0)
                o_ref[h * 16 + ro:h * 16 + ro + 1, :] = jnp.sum(sk, axis=0, keepdims=True)

    return _pallas(body, name="attn_rpb_grad", out_shape=_sds((NH * 16, LANES), F32))(tt)


def _attn_geometry(i, rows):
    wsp = jnp.clip(2 * i - NA_ROWS // 2, 0, rows - PAIR_ROWS)
    k0 = pl.multiple_of(wsp * GW, GW)
    t0 = pl.multiple_of((wsp - 2 * i + NA_ROWS) * GW, GW)
    rr = lax.broadcasted_iota(jnp.int32, (GW, Q_TILE), 1) // GW
    wsr = jnp.clip(2 * i + rr - NA_ROWS // 2, 0, rows - NA_ROWS)
    edge_masks = tuple(jnp.where((kr >= wsr) & (kr < wsr + NA_ROWS), 0.0, NEG).astype(F32)
                       for kr in (wsp, wsp + PAIR_ROWS - 1))
    return k0, t0, edge_masks


def _biased(s_raw, bias, edge_masks):
    x = s_raw + bias
    return jnp.concatenate([x[:GW] + edge_masks[0], x[GW:K_WIN - GW], x[K_WIN - GW:] + edge_masks[1]], axis=0)


def _two_heads_on_lanes(xt):
    feat = lax.broadcasted_iota(jnp.int32, xt.shape, 0)
    zero = jnp.zeros_like(xt)
    return jnp.concatenate([jnp.where(feat < HD, xt, zero), jnp.where(feat >= HD, xt, zero)], axis=1)


def _two_heads_on_rows(x):
    lane = lax.broadcasted_iota(jnp.int32, x.shape, 1)
    zero = jnp.zeros_like(x)
    return jnp.concatenate([jnp.where(lane < HD, x, zero), jnp.where(lane >= HD, x, zero)], axis=0)


def _pick_heads(x2):
    n = x2.shape[0] // 2
    lane = lax.broadcasted_iota(jnp.int32, (n, LANES), 1)
    return jnp.where(lane < HD, x2[:n], x2[n:])


_TN = (((0,), (0,)), ((), ()))


def _attn_fwd(qkv, tab, s):
    rows = s // GW
    npair = rows // 2

    def body(q_ref, kv_ref, tab_ref, o_ref, lse_ref):
        i = pl.program_id(0)
        k0, t0, edge_masks = _attn_geometry(i, rows)
        for p in range(NH // 2):
            cq = slice(p * LANES, (p + 1) * LANES)
            ck = slice(DA + p * LANES, DA + (p + 1) * LANES)
            cv = slice(2 * DA + p * LANES, 2 * DA + (p + 1) * LANES)
            qm2 = _two_heads_on_lanes(q_ref[:, cq].T) * SCALE
            s_loc = jnp.dot(kv_ref[pl.ds(k0, K_WIN), ck], qm2, preferred_element_type=F32)
            s_ctx = jnp.dot(kv_ref[pl.ds(s, CTX), ck], qm2, preferred_element_type=F32)
            p_loc, p_ctx = [], []
            for hh in range(2):
                h = 2 * p + hh
                ch = slice(hh * Q_TILE, (hh + 1) * Q_TILE)
                sl = _biased(s_loc[:, ch], tab_ref[h, pl.ds(t0, K_WIN), :], edge_masks)
                sc = s_ctx[:, ch]
                m = jnp.maximum(jnp.max(sl, axis=0, keepdims=True), jnp.max(sc, axis=0, keepdims=True))
                el = jnp.exp(sl - m)
                ec = jnp.exp(sc - m)
                l = jnp.sum(el, axis=0, keepdims=True) + jnp.sum(ec, axis=0, keepdims=True)
                inv = 1.0 / l
                lse_ref[h:h + 1, :] = m + jnp.log(l)
                p_loc.append((el * inv).astype(MXU_DTYPE))
                p_ctx.append((ec * inv).astype(MXU_DTYPE))
            o2 = (lax.dot_general(jnp.concatenate(p_loc, axis=1), kv_ref[pl.ds(k0, K_WIN), cv], _TN, preferred_element_type=F32)
                  + lax.dot_general(jnp.concatenate(p_ctx, axis=1), kv_ref[pl.ds(s, CTX), cv], _TN, preferred_element_type=F32))
            o_ref[:, cq] = _pick_heads(o2).astype(o_ref.dtype)

    return _pallas(
        body, name="attn_fwd", grid=(npair,),
        in_specs=[pl.BlockSpec((Q_TILE, DA), lambda i: (i, 0)), pl.BlockSpec(memory_space=pltpu.VMEM),
                  pl.BlockSpec(memory_space=pltpu.VMEM)],
        out_specs=[pl.BlockSpec((Q_TILE, DA), lambda i: (i, 0)), pl.BlockSpec((NH, Q_TILE), lambda i: (0, i))],
        out_shape=[_sds((s, D), MXU_DTYPE), _sds((NH, s), F32)],
        semantics=("arbitrary",),
    )(qkv, qkv, tab)


def _attn_bwd(qkv, tab, lse, dycat, s):
    rows = s // GW
    npair = rows // 2
    sa = s + CTX
    nzero = CTX // Q_TILE

    def body(q_ref, do_ref, lse_ref, kv_ref, tab_ref, dq_ref, dkv_ref, tt_ref, dk_acc, dv_acc):
        i = pl.program_id(0)

        @pl.when(i == 0)
        def _():
            dk_acc[...] = jnp.zeros_like(dk_acc)
            dv_acc[...] = jnp.zeros_like(dv_acc)
            tt_ref[...] = jnp.zeros_like(tt_ref)

        @pl.when(i >= npair)
        def _():
            dq_ref[...] = jnp.zeros_like(dq_ref)

        @pl.when(i < npair)
        def _():
            k0, t0, edge_masks = _attn_geometry(i, rows)
            for p in range(NH // 2):
                cq = slice(p * LANES, (p + 1) * LANES)
                ck = slice(DA + p * LANES, DA + (p + 1) * LANES)
                cv = slice(2 * DA + p * LANES, 2 * DA + (p + 1) * LANES)
                qp = q_ref[:, cq] * SCALE
                dop = do_ref[:, cq].astype(MXU_DTYPE)
                qm2 = _two_heads_on_lanes(qp.T)
                dom2 = _two_heads_on_lanes(dop.T)
                kw = kv_ref[pl.ds(k0, K_WIN), ck]
                kc = kv_ref[pl.ds(s, CTX), ck]
                vw = kv_ref[pl.ds(k0, K_WIN), cv]
                vc = kv_ref[pl.ds(s, CTX), cv]
                s_loc = jnp.dot(kw, qm2, preferred_element_type=F32)
                s_ctx = jnp.dot(kc, qm2, preferred_element_type=F32)
                dp_loc = jnp.dot(vw, dom2, preferred_element_type=F32)
                dp_ctx = jnp.dot(vc, dom2, preferred_element_type=F32)
                p_loc, p_ctx, ds_loc, ds_ctx = [], [], [], []
                for hh in range(2):
                    h = 2 * p + hh
                    ch = slice(hh * Q_TILE, (hh + 1) * Q_TILE)
                    lse_h = lse_ref[h:h + 1, :]
                    pl_ = jnp.exp(_biased(s_loc[:, ch], tab_ref[h, pl.ds(t0, K_WIN), :], edge_masks) - lse_h)
                    pc_ = jnp.exp(s_ctx[:, ch] - lse_h)
                    dpl = dp_loc[:, ch]
                    dpc = dp_ctx[:, ch]
                    delta = jnp.sum(pl_ * dpl, axis=0, keepdims=True) + jnp.sum(pc_ * dpc, axis=0, keepdims=True)
                    dsl = pl_ * (dpl - delta)
                    dsc = pc_ * (dpc - delta)
                    tt_ref[h, pl.ds(t0, K_WIN), :] += dsl
                    p_loc.append(pl_.astype(MXU_DTYPE))
                    p_ctx.append(pc_.astype(MXU_DTYPE))
                    ds_loc.append(dsl.astype(MXU_DTYPE))
                    ds_ctx.append(dsc.astype(MXU_DTYPE))
                p_loc, p_ctx = jnp.concatenate(p_loc, axis=1), jnp.concatenate(p_ctx, axis=1)
                ds_loc, ds_ctx = jnp.concatenate(ds_loc, axis=1), jnp.concatenate(ds_ctx, axis=1)
                do_rows = _two_heads_on_rows(dop)
                q_rows = _two_heads_on_rows(qp)
                dv_acc[pl.ds(k0, K_WIN), cq] += jnp.dot(p_loc, do_rows, preferred_element_type=F32)
                dv_acc[pl.ds(s, CTX), cq] += jnp.dot(p_ctx, do_rows, preferred_element_type=F32)
                dk_acc[pl.ds(k0, K_WIN), cq] += jnp.dot(ds_loc, q_rows, preferred_element_type=F32)
                dk_acc[pl.ds(s, CTX), cq] += jnp.dot(ds_ctx, q_rows, preferred_element_type=F32)
                dq2 = (lax.dot_general(ds_loc, kw, _TN, preferred_element_type=F32)
                       + lax.dot_general(ds_ctx, kc, _TN, preferred_element_type=F32))
                dq_ref[:, cq] = (_pick_heads(dq2) * SCALE).astype(dq_ref.dtype)

        @pl.when(i == npair - 1)
        def _():
            def cp(c, carry):
                r0 = pl.multiple_of(c * ROW_TILE, ROW_TILE)
                dkv_ref[pl.ds(r0, ROW_TILE), 0:DA] = dk_acc[pl.ds(r0, ROW_TILE), :].astype(dkv_ref.dtype)
                dkv_ref[pl.ds(r0, ROW_TILE), DA:2 * DA] = dv_acc[pl.ds(r0, ROW_TILE), :].astype(dkv_ref.dtype)
                return carry

            lax.fori_loop(0, sa // ROW_TILE, cp, 0)

    qmap = lambda i: (jnp.minimum(i, npair - 1), 0)
    return _pallas(
        body, name="attn_bwd", grid=(npair + nzero,),
        in_specs=[pl.BlockSpec((Q_TILE, DA), qmap), pl.BlockSpec((Q_TILE, DA), qmap),
                  pl.BlockSpec((NH, Q_TILE), lambda i: (0, jnp.minimum(i, npair - 1))),
                  pl.BlockSpec(memory_space=pltpu.VMEM), pl.BlockSpec(memory_space=pltpu.VMEM)],
        out_specs=[pl.BlockSpec((Q_TILE, DA), lambda i: (i, 0)), pl.BlockSpec(memory_space=pltpu.VMEM),
                   pl.BlockSpec(memory_space=pltpu.VMEM)],
        out_shape=[_sds((sa, DA), MXU_DTYPE), _sds((sa, 2 * DA), MXU_DTYPE), _sds((NH, TAB_BLOCKS * GW, LANES), F32)],
        scratch_shapes=[pltpu.VMEM((sa, DA), F32)] * 2,
        semantics=("arbitrary",),
    )(qkv, dycat, lse, qkv, tab)


def _tile(n, prefs):
    for t in prefs:
        if n % t == 0:
            return t
    raise ValueError((n, prefs))


def _local_step(x, ctx, tgt, mod, mod_c, vec, w_in, late_weights, rpb_rev, early_grads=None):
    s = x.shape[0]
    sa = s + CTX
    ts = _tile(s, (1024, 512, 256))
    ts2 = _tile(s, (2048, 1024, 512, 256))
    tsa = _tile(sa, (1088, 640, 256))
    tsa2 = _tile(sa, (2176, 640, 256))
    sh1, sc1, gt1, sh2, sc2, gt2 = (mod[i:i + 1] for i in range(6))
    csh1, csc1 = mod_c[0:1], mod_c[1:2]
    act = MXU_DTYPE

    tab = _bias_table(rpb_rev)
    h_all = _rmsmod_fwd(x, ctx, vec["g_norm1"], sc1, sh1, csc1, csh1)
    w_in = w_in(h_all) if callable(w_in) else w_in
    qkv = _mm(h_all, w_in, mode="nn", m=sa, n=3 * DA, k=D, tm=tsa2, tn=512, tk=D, out_dtype=MXU_DTYPE, name="mm_qkv")
    ag = _mm(h_all, w_in, mode="nn", m=s, n=2 * DC, k=D, tm=ts2, tn=512, tk=D, out_dtype=F32, name="mm_ag", b_off=(0, 3))
    ycat, lse = _attn_fwd(qkv, tab, s)
    u1 = _conf_conv_fwd(ag, vec["conv_w"], vec["conv_b"])
    ycat = _conf_ln_fwd(u1, vec["ln_g"], vec["ln_b"], ycat)
    w_out, w_up, w_down = late_weights(ycat) if callable(late_weights) else late_weights
    y = _mm(ycat, w_out, mode="nn", m=s, n=D, k=D, tm=ts2, tn=512, tk=D, out_dtype=F32, name="mm_out")
    x1, h2 = _resid_rmsmod_fwd(x, y, gt1, vec["g_norm2"], sc2, sh2)
    u = _mm(h2, w_up, mode="nn", m=s, n=2 * DFF, k=D, tm=ts2, tn=512, tk=D, out_dtype=act, name="mm_up")
    f = _ffn_act_fwd(u, vec["ffn_conv_w"], vec["ffn_conv_b"])
    z = _mm(f, w_down, mode="nn", m=s, n=D, k=DFF, tm=ts, tn=D, tk=DFF, out_dtype=F32, name="mm_down")
    dx2, dz, loss, dgt2, dgf = _final_fwd_bwd(x1, z, gt2, vec["g_final"], tgt)

    df = _mm(dz, w_down, mode="nt", m=s, n=DFF, k=D, tm=ts, tn=DFF, tk=D, out_dtype=act, name="mm_down_dx")
    d_w_down = _mm(f, dz, mode="tn", m=DFF, n=D, k=s, tm=DFF, tn=D, tk=ts, out_dtype=F32, name="mm_down_dw")
    dug, duv, dfw_g, dfw_v, dfb_g, dfb_v = _ffn_act_bwd(u, df, vec["ffn_conv_w"], vec["ffn_conv_b"])
    dw_kw = dict(mode="tn", m=D, n=DFF, k=s, tm=D, tn=DFF, tk=ts, out_dtype=F32, out_total=(D, 2 * DFF))
    d_w_up = _mm(h2, dug, name="mm_up_dw_gate", **dw_kw)
    d_w_up = _mm(h2, duv, name="mm_up_dw_val", o_off=(0, 1), into=d_w_up, **dw_kw)
    if early_grads is not None:
        early_grads[0](d_w_up, d_w_down)
    dh2 = _mm([dug, duv], w_up, mode="nt", m=s, n=D, k=2 * DFF, tm=ts, tn=D, tk=DFF, out_dtype=F32, name="mm_up_dx")
    sc2_b = sc2 if early_grads is None else sc2 + early_grads[1](dh2)
    dsh2, dsc2, dg2, dx1, dy, dgt1 = _rmsmod_bwd(x1, dh2, vec["g_norm2"], sc2_b, name="rmsmod2_bwd", add=dx2, resid=(gt1, y))
    dycat = _mm(dy, w_out, mode="nt", m=s, n=D, k=D, tm=ts2, tn=512, tk=D, out_dtype=F32, name="mm_out_dx")
    d_w_out = _mm(ycat, dy, mode="tn", m=D, n=D, k=s, tm=D, tn=D, tk=ts, out_dtype=F32, name="mm_out_dw")
    du1, dln_g, dln_b = _conf_ln_bwd(dycat, u1, vec["ln_g"], vec["ln_b"])
    da, dg, dconv_w, dconv_b = _conf_conv_bwd(ag, du1, vec["conv_w"], sa)
    dq, dkv, tt = _attn_bwd(qkv, tab, lse, dycat, s)
    drpb_rev = _rpb_grad(tt)
    d_pieces = [dq, dkv, da, dg]
    dh = _mm(d_pieces, w_in, mode="nt", m=sa, n=D, k=NIN, tm=tsa2, tn=D, tk=512, out_dtype=F32, name="mm_in_dx")
    d_w_in = _mm(h_all, d_pieces, mode="tn", m=D, n=NIN, k=sa, tm=D, tn=512, tk=tsa, out_dtype=F32, name="mm_in_dw",
                 k_outer=True)
    dsh1, dsc1, dg1, grad_x = _rmsmod_bwd(x, dh, vec["g_norm1"], sc1, name="rmsmod1_bwd", add=dx1)
    dcsh1, dcsc1, dg1c = _rmsmod_bwd(ctx, dh, vec["g_norm1"], csc1, name="rmsmod1_ctx_bwd", dh_row_off=s // ROW_TILE)

    small = dict(
        dmod=[dsh1, dsc1, dgt1, dsh2, dsc2, dgt2], dmod_c=[dcsh1, dcsc1],
        g_norm1=[dg1, dg1c], g_norm2=dg2, g_final=dgf, conv_b=dconv_b, ln_g=dln_g, ln_b=dln_b, conv_w=dconv_w,
        ffn_conv_w=[dfw_g, dfw_v], ffn_conv_b=[dfb_g, dfb_v], rpb_rev=drpb_rev,
    )
    return loss, grad_x, d_w_in, d_w_out, d_w_up, d_w_down, small


N_CHIPS = 4
HBM = pl.BlockSpec(memory_space=pl.ANY)
BIG = {"w_in": ("col", (D, NIN)), "w_out": ("row", (D, D)), "w_up": ("col", (D, 2 * DFF)), "w_down": ("row", (DFF, D))}
BIG_NAMES = tuple(BIG)
LATE_NAMES = ("w_out", "w_up", "w_down")


def _shard_shape(name):
    kind, (r, c) = BIG[name]
    return (r, c // N_CHIPS) if kind == "col" else (r // N_CHIPS, c)


def _half_rows(name):
    return _shard_shape(name)[0] // 2


def _place():
    x, y, c = lax.axis_index("x"), lax.axis_index("y"), lax.axis_index("c")
    others = [(1 - x, y), (x, 1 - y), (1 - x, 1 - y)]
    return x, y, c, 2 * x + y, (x, y, 1 - c), others


def _whole_region(ref, name, chip, half):
    kind, _ = BIG[name]
    r, c = _shard_shape(name)
    if kind == "col":
        return ref.at[pl.ds(half * (r // 2), r // 2), pl.ds(chip * c, c)]
    return ref.at[pl.ds(chip * r + half * (r // 2), r // 2), :]


def _remote(src, dst, send_sem, recv_sem, to):
    return pltpu.make_async_remote_copy(src_ref=src, dst_ref=dst, send_sem=send_sem, recv_sem=recv_sem,
                                        device_id=to, device_id_type=MESH)


def _gather_small(v, name):
    m_per, n = v.shape

    def body(x_ref, out_ref, send_sems, recv_sems, local_sem):
        x, y, c, _, sibling, others = _place()
        me = (x, y, c)

        def rows(px, py, pc):
            return out_ref.at[pl.ds((4 * px + 2 * py + pc) * m_per, m_per), :]

        def copy(k, block, to, src=None):
            return _remote(rows(*block) if src is None else src, rows(*block), send_sems.at[k], recv_sems.at[k], to)

        mine = pltpu.make_async_copy(x_ref, rows(*me), local_sem)
        mine.start()
        first = [copy(0, me, sibling, src=x_ref)]
        first += [copy(1 + j, me, (*chip, c), src=x_ref) for j, chip in enumerate(others)]
        for cp in first:
            cp.start()
        passed = [copy(4 + j, (*chip, c), sibling) for j, chip in enumerate(others)]
        for j, chip in enumerate(others):
            copy(1 + j, (*chip, c), me).wait_recv()
            passed[j].start()
        copy(0, sibling, me).wait_recv()
        for j, chip in enumerate(others):
            copy(4 + j, (*chip, 1 - c), me).wait_recv()
        for cp in first + passed:
            cp.wait_send()
        mine.wait()

    return pl.pallas_call(
        body, name=name, out_shape=_sds((8 * m_per, n), v.dtype),
        in_specs=[pl.BlockSpec(memory_space=pltpu.VMEM)], out_specs=pl.BlockSpec(memory_space=pltpu.VMEM),
        scratch_shapes=[pltpu.SemaphoreType.DMA((7,)), pltpu.SemaphoreType.DMA((7,)), pltpu.SemaphoreType.DMA],
    )(v)


def _cast_into_whole(name, shard, chip):
    kind, whole = BIG[name]
    r, c = shard.shape
    if kind == "col":
        tr = 256
        o_spec = pl.BlockSpec((tr, c), lambda i, ch: (i, ch[0]))
    else:
        tr = _tile(r, (128, 352))
        o_spec = pl.BlockSpec((tr, c), lambda i, ch: (ch[0] * (r // tr) + i, 0))

    def body(ch_ref, x_ref, o_ref):
        del ch_ref
        o_ref[...] = x_ref[...].astype(o_ref.dtype)

    return _pallas(body, name="cast_" + name, prefetch=1, grid=(r // tr,),
                   in_specs=[pl.BlockSpec((tr, c), lambda i, ch: (i, 0))], out_specs=o_spec,
                   out_shape=_sds(whole, MXU_DTYPE), semantics=("parallel",))(chip, shard)


SEM = pl.BlockSpec(memory_space=pltpu.SEMAPHORE)
IN_HBM = pl.BlockSpec(memory_space=pltpu.HBM)
DATAFLOW = pltpu.SideEffectType.DATAFLOW_SIDE_EFFECTING


def _keep_in_hbm(a):
    return pltpu.with_memory_space_constraint(a, pltpu.HBM)


def _gather_start(wholes, names, after, tag):
    nw = len(names)
    ns = 2 * 3 * nw

    def body(*refs):
        ins = refs[:nw]
        sems = refs[nw + 1:nw + 1 + ns]
        token = refs[2 * nw + ns + 1]
        _, _, c, chip, _, others = _place()
        for w, name in enumerate(names):
            mine = _whole_region(ins[w], name, chip, c)
            for t, (ox, oy) in enumerate(others):
                k = 2 * (3 * w + t)
                _remote(mine, mine, sems[k], sems[k + 1], (ox, oy, c)).start()
        token[...] = jnp.zeros_like(token)

    res = pl.pallas_call(
        body, name="gather_" + tag + "_start",
        out_shape=(*[pltpu.SemaphoreType.DMA(())] * ns, *[pltpu.HBM(a.shape, a.dtype) for a in wholes], _sds((8, LANES), F32)),
        in_specs=[IN_HBM] * nw + [pl.BlockSpec(memory_space=pl.ANY)],
        out_specs=(*[SEM] * ns, *[IN_HBM] * nw, pl.BlockSpec(memory_space=pltpu.VMEM)),
        input_output_aliases={i: ns + i for i in range(nw)},
        compiler_params=pltpu.CompilerParams(has_side_effects=DATAFLOW),
    )(*[_keep_in_hbm(a) for a in wholes], after)
    return list(res[:ns]), list(res[ns:ns + nw]), res[ns + nw]


def _gather_wait(sems, wholes, names, after, tag):
    nw = len(names)
    ns = len(sems)

    def body(*refs):
        ins = refs[:nw]
        sem_refs = refs[nw:nw + ns]
        _, _, c, chip, _, others = _place()
        for w, name in enumerate(names):
            mine = _whole_region(ins[w], name, chip, c)
            for t, (ox, oy) in enumerate(others):
                got = _whole_region(ins[w], name, 2 * ox + oy, c)
                k = 2 * (3 * w + t)
                cp = _remote(mine, got, sem_refs[k], sem_refs[k + 1], (ox, oy, c))
                cp.wait_send()
                cp.wait_recv()

    return pl.pallas_call(
        body, name="gather_" + tag + "_wait",
        out_shape=tuple(pltpu.HBM(a.shape, a.dtype) for a in wholes),
        in_specs=[IN_HBM] * nw + [SEM] * ns + [pl.BlockSpec(memory_space=pl.ANY)], out_specs=tuple([IN_HBM] * nw),
        input_output_aliases={i: i for i in range(nw)},
        compiler_params=pltpu.CompilerParams(has_side_effects=DATAFLOW),
    )(*wholes, *sems, after)


def _forward_halves(wholes, names, tag):
    nw = len(names)

    def body(*refs):
        outs = refs[nw:2 * nw]
        send_sems, recv_sems = refs[2 * nw:]
        _, _, c, _, sibling, others = _place()
        sends = []
        for w, name in enumerate(names):
            for t, (ox, oy) in enumerate(others):
                got = _whole_region(outs[w], name, 2 * ox + oy, c)
                cp = _remote(got, got, send_sems.at[w, t], recv_sems.at[w, t], sibling)
                cp.start()
                sends.append(cp)
        for w, name in enumerate(names):
            for t, (ox, oy) in enumerate(others):
                got = _whole_region(outs[w], name, 2 * ox + oy, 1 - c)
                _remote(got, got, send_sems.at[w, t], recv_sems.at[w, t], sibling).wait_recv()
        for cp in sends:
            cp.wait_send()

    return pl.pallas_call(
        body, name="gather_" + tag + "_forward",
        out_shape=[_sds(a.shape, a.dtype) for a in wholes],
        in_specs=[HBM] * nw, out_specs=[HBM] * nw,
        input_output_aliases={i: i for i in range(nw)},
        scratch_shapes=[pltpu.SemaphoreType.DMA((nw, 3)), pltpu.SemaphoreType.DMA((nw, 3))],
    )(*wholes)


def _compact_shape(name, dtype):
    kind, (r, c) = BIG[name]
    return _sds((r // 2, c), dtype)


def _swap_pairs(ins, outs, names, c):
    pairs = []
    for w, name in enumerate(names):
        kind, _ = BIG[name]
        half = _half_rows(name)
        if kind == "col":
            pairs.append((ins[w].at[pl.ds((1 - c) * half, half), :], outs[w]))
        else:
            pairs += [(ins[w].at[pl.ds(jj * 2 * half + (1 - c) * half, half), :], outs[w].at[pl.ds(jj * half, half), :])
                      for jj in range(N_CHIPS)]
    return pairs


def _n_swap_copies(names):
    return sum(1 if BIG[n][0] == "col" else N_CHIPS for n in names)


def _swap_start(grads, names, label):
    nw = len(names)
    ns = 2 * _n_swap_copies(names)

    def body(*refs):
        ins, lands = refs[:nw], refs[nw:2 * nw]
        sems = refs[2 * nw:2 * nw + ns]
        token = refs[4 * nw + ns]
        _, _, c, _, sibling, _ = _place()
        for k, (src, dst) in enumerate(_swap_pairs(ins, lands, names, c)):
            _remote(src, dst, sems[2 * k], sems[2 * k + 1], sibling).start()
        token[...] = jnp.zeros_like(token)

    lands = [_keep_in_hbm(lax.empty(_compact_shape(n, F32).shape, F32)) for n in names]
    res = pl.pallas_call(
        body, name=label,
        out_shape=(*[pltpu.SemaphoreType.DMA(())] * ns, *[pltpu.HBM(a.shape, a.dtype) for a in grads],
                   *[pltpu.HBM(a.shape, a.dtype) for a in lands], _sds((8, LANES), F32)),
        in_specs=[IN_HBM] * (2 * nw),
        out_specs=(*[SEM] * ns, *[IN_HBM] * (2 * nw), pl.BlockSpec(memory_space=pltpu.VMEM)),
        input_output_aliases={i: ns + i for i in range(2 * nw)},
        compiler_params=pltpu.CompilerParams(has_side_effects=DATAFLOW),
    )(*[_keep_in_hbm(a) for a in grads], *lands)
    return list(res[:ns]), list(res[ns:ns + nw]), list(res[ns + nw:ns + 2 * nw]), res[ns + 2 * nw]


def _swap_wait(sems, grads, lands, names, after, label):
    nw = len(names)
    ns = len(sems)

    def body(*refs):
        ins, land_refs = refs[:nw], refs[nw:2 * nw]
        sem_refs = refs[2 * nw:2 * nw + ns]
        _, _, c, _, sibling, _ = _place()
        for k, (src, dst) in enumerate(_swap_pairs(ins, land_refs, names, c)):
            cp = _remote(src, dst, sem_refs[2 * k], sem_refs[2 * k + 1], sibling)
            cp.wait_send()
            cp.wait_recv()

    res = pl.pallas_call(
        body, name=label,
        out_shape=tuple(pltpu.HBM(a.shape, a.dtype) for a in (*grads, *lands)),
        in_specs=[IN_HBM] * (2 * nw) + [SEM] * ns + [pl.BlockSpec(memory_space=pl.ANY)],
        out_specs=tuple([IN_HBM] * (2 * nw)),
        input_output_aliases={i: i for i in range(2 * nw)},
        compiler_params=pltpu.CompilerParams(has_side_effects=DATAFLOW),
    )(*grads, *lands, *sems, after)
    return list(res[:nw]), list(res[nw:])


def _add_halves(name, grad, got, core):
    kind, (r, c) = BIG[name]
    half = _half_rows(name)
    if kind == "col":
        t = 128
        grid = (half // t,)
        g_spec = pl.BlockSpec((t, c), lambda i, cr: (cr[0] * (half // t) + i, 0))
        o_spec = pl.BlockSpec((t, c), lambda i, cr: (i, 0))
    else:
        t = half
        grid = (N_CHIPS,)
        g_spec = pl.BlockSpec((t, c), lambda i, cr: (2 * i + cr[0], 0))
        o_spec = pl.BlockSpec((t, c), lambda i, cr: (i, 0))

    def body(c_ref, g_ref, b_ref, o_ref):
        del c_ref
        o_ref[...] = (g_ref[...] + b_ref[...]).astype(o_ref.dtype)

    return pl.pallas_call(
        body, name="grad_add_" + name,
        grid_spec=pltpu.PrefetchScalarGridSpec(num_scalar_prefetch=1, grid=grid, in_specs=[g_spec, o_spec], out_specs=o_spec),
        out_shape=_compact_shape(name, BF16),
        compiler_params=pltpu.CompilerParams(dimension_semantics=("parallel",), vmem_limit_bytes=VMEM_LIMIT),
    )(core, grad, got)


def _piece(ref, name, chip):
    kind, _ = BIG[name]
    r, c = _shard_shape(name)
    if kind == "col":
        return ref.at[:, pl.ds(chip * c, c)]
    return ref.at[pl.ds(chip * (r // 2), r // 2), :]


def _landing_shape(name):
    r, c = _shard_shape(name)
    return (N_CHIPS - 1, r // 2, c)


def _exchange_start(parts, names, label):
    nw = len(names)
    ns = 2 * 3 * nw

    def body(*refs):
        ins, lands = refs[:nw], refs[nw:2 * nw]
        sems = refs[2 * nw:2 * nw + ns]
        token = refs[4 * nw + ns]
        _, _, c, _, _, others = _place()
        for w, name in enumerate(names):
            for t, (ox, oy) in enumerate(others):
                k = 2 * (3 * w + t)
                _remote(_piece(ins[w], name, 2 * ox + oy), lands[w].at[t], sems[k], sems[k + 1], (ox, oy, c)).start()
        token[...] = jnp.zeros_like(token)

    lands = [_keep_in_hbm(lax.empty(_landing_shape(n), BF16)) for n in names]
    res = pl.pallas_call(
        body, name=label,
        out_shape=(*[pltpu.SemaphoreType.DMA(())] * ns, *[pltpu.HBM(a.shape, a.dtype) for a in parts],
                   *[pltpu.HBM(a.shape, a.dtype) for a in lands], _sds((8, LANES), F32)),
        in_specs=[IN_HBM] * (2 * nw),
        out_specs=(*[SEM] * ns, *[IN_HBM] * (2 * nw), pl.BlockSpec(memory_space=pltpu.VMEM)),
        input_output_aliases={i: ns + i for i in range(2 * nw)},
        compiler_params=pltpu.CompilerParams(has_side_effects=DATAFLOW),
    )(*[_keep_in_hbm(a) for a in parts], *lands)
    return list(res[:ns]), list(res[ns:ns + nw]), list(res[ns + nw:ns + 2 * nw]), res[ns + 2 * nw]


def _exchange_wait(sems, parts, lands, names, after, label):
    nw = len(names)
    ns = len(sems)

    def body(*refs):
        ins, land_refs = refs[:nw], refs[nw:2 * nw]
        sem_refs = refs[2 * nw:2 * nw + ns]
        _, _, c, _, _, others = _place()
        for w, name in enumerate(names):
            for t, (ox, oy) in enumerate(others):
                k = 2 * (3 * w + t)
                cp = _remote(_piece(ins[w], name, 2 * ox + oy), land_refs[w].at[t], sem_refs[k], sem_refs[k + 1], (ox, oy, c))
                cp.wait_send()
                cp.wait_recv()

    res = pl.pallas_call(
        body, name=label,
        out_shape=tuple(pltpu.HBM(a.shape, a.dtype) for a in (*parts, *lands)),
        in_specs=[IN_HBM] * (2 * nw) + [SEM] * ns + [pl.BlockSpec(memory_space=pl.ANY)],
        out_specs=tuple([IN_HBM] * (2 * nw)),
        input_output_aliases={i: i for i in range(2 * nw)},
        compiler_params=pltpu.CompilerParams(has_side_effects=DATAFLOW),
    )(*parts, *lands, *sems, after)
    return list(res[:nw]), list(res[nw:])


def _sum_chips(name, part, got, chip):
    kind, _ = BIG[name]
    _, r, c = got.shape
    t = _tile(r, (128, 352))
    if kind == "col":
        own = pl.BlockSpec((t, c), lambda i, ch: (i, ch[0]))
    else:
        own = pl.BlockSpec((t, c), lambda i, ch: (ch[0] * (r // t) + i, 0))

    def body(ch_ref, p_ref, g_ref, o_ref):
        del ch_ref
        acc = p_ref[...].astype(F32)
        for j in range(N_CHIPS - 1):
            acc = acc + g_ref[j].astype(F32)
        o_ref[...] = acc

    return _pallas(
        body, name="grad_sum_" + name, prefetch=1, grid=(r // t,),
        in_specs=[own, pl.BlockSpec((N_CHIPS - 1, t, c), lambda i, ch: (0, i, 0))],
        out_specs=pl.BlockSpec((t, c), lambda i, ch: (i, 0)),
        out_shape=_sds((r, c), F32), semantics=("parallel",),
    )(chip, part, got)


def _send_halves(sums, label, after):
    nw = len(sums)

    def body(*refs):
        ins, outs = refs[:nw], refs[nw + 1:2 * nw + 1]
        send_sems, recv_sems = refs[2 * nw + 1:]
        _, _, _, _, sibling, _ = _place()
        copies = [_remote(ins[w], outs[w], send_sems.at[w], recv_sems.at[w], sibling) for w in range(nw)]
        for cp in copies:
            cp.start()
        for cp in copies:
            cp.wait()

    return pl.pallas_call(
        body, name=label,
        out_shape=[_sds(a.shape, a.dtype) for a in sums],
        in_specs=[HBM] * (nw + 1), out_specs=[HBM] * nw,
        scratch_shapes=[pltpu.SemaphoreType.DMA((nw,)), pltpu.SemaphoreType.DMA((nw,))],
    )(*sums, after)


EARLY_GRADS = ("w_up", "w_down")
LAST_GRADS = ("w_in", "w_out")


def _reduce_finish(started, names, after, chip, tag):
    sems, parts, lands, _ = started
    parts, lands = _exchange_wait(sems, parts, lands, names, after, "grad_exchange_wait_" + tag)
    return [_sum_chips(n, parts[i], lands[i], chip) for i, n in enumerate(names)]


HI = lax.Precision.HIGHEST
MOD_COLS = 6 * D // N_CHIPS
COND_ROWS = 16


def _silu(v):
    return v * _sigmoid(v)


GATHER_ROWS = 48
FFW_COLS = 2 * DFF // N_CHIPS
CONV_COLS = DC // N_CHIPS


def _pack_cond(c, ffn_w, conv_w):
    def body(c_ref, f_ref, w_ref, o_ref):
        o_ref[...] = jnp.zeros_like(o_ref)
        o_ref[0:1, 0:D] = c_ref[...]
        o_ref[8:11, :] = f_ref[...]
        o_ref[16:16 + CW, 0:CONV_COLS] = w_ref[...]

    return _pallas(body, name="pack_cond", out_shape=_sds((GATHER_ROWS, FFW_COLS), F32))(c, ffn_w, conv_w)


def _unpack_cond(got, c_ctx):
    def body(g_ref, c_ref, cond_ref, f_ref, w_ref):
        cond_ref[...] = jnp.zeros_like(cond_ref)
        for d in range(8):
            cond_ref[d:d + 1, :] = g_ref[d * GATHER_ROWS:d * GATHER_ROWS + 1, 0:D]
        cond_ref[8:9, :] = c_ref[...]
        for j in range(N_CHIPS):
            r0 = 2 * j * GATHER_ROWS
            f_ref[:, j * FFW_COLS:(j + 1) * FFW_COLS] = g_ref[r0 + 8:r0 + 11, :]
            w_ref[:, j * CONV_COLS:(j + 1) * CONV_COLS] = g_ref[r0 + 16:r0 + 16 + CW, 0:CONV_COLS]

    return _pallas(body, name="unpack_cond",
                   out_shape=[_sds((COND_ROWS, D), F32), _sds((3, 2 * DFF), F32), _sds((CW, DC), F32)])(got, c_ctx)


def _chip_cols(rows, width):
    return pl.BlockSpec((rows, width), lambda i, ch: (0, ch[0]))


def _whole(shape):
    return pl.BlockSpec(shape, lambda i, ch: (0,) * len(shape))


def _mod_shard(cond, w_mod, b_mod, chip):
    def body(ch_ref, c_ref, w_ref, b_ref, o_ref):
        del ch_ref
        o_ref[...] = jnp.dot(_silu(c_ref[...]), w_ref[...], preferred_element_type=F32, precision=HI) + b_ref[...]

    return _pallas(body, name="mod_fwd", prefetch=1, grid=(1,),
                   in_specs=[_whole((COND_ROWS, D)), _whole((D, MOD_COLS)), _chip_cols(1, MOD_COLS)],
                   out_specs=_whole((COND_ROWS, MOD_COLS)),
                   out_shape=_sds((COND_ROWS, MOD_COLS), F32))(chip, cond, w_mod, b_mod)


def _unpack_mod(mods, dev):
    def body(dev_ref, m_ref, me_ref, c_ref):
        rowi = lax.broadcasted_iota(jnp.int32, (COND_ROWS, MOD_COLS), 0)
        mine, ctx = [], []
        for j in range(N_CHIPS):
            blk = m_ref[2 * j * COND_ROWS:(2 * j + 1) * COND_ROWS, :]
            mine.append(jnp.sum(jnp.where(rowi == dev_ref[0], blk, 0.0), axis=0, keepdims=True))
            ctx.append(blk[8:9, :])
        mine = jnp.concatenate(mine, axis=1)
        ctx = jnp.concatenate(ctx, axis=1)
        for k in range(6):
            me_ref[k:k + 1, :] = mine[:, k * D:(k + 1) * D]
        for k in range(2):
            c_ref[k:k + 1, :] = ctx[:, k * D:(k + 1) * D]

    return _pallas(body, name="unpack_mod", prefetch=1, grid=(1,),
                   in_specs=[_whole(mods.shape)], out_specs=[_whole((6, D)), _whole((2, D))],
                   out_shape=[_sds((6, D), F32), _sds((2, D), F32)])(dev, mods)


def _mod_weight_grad(cond, dmod_all, chip):
    def body(ch_ref, c_ref, d_ref, o_ref):
        del ch_ref
        o_ref[...] = lax.dot_general(_silu(c_ref[...]), d_ref[...], _TN, preferred_element_type=F32, precision=HI)

    return _pallas(body, name="mod_weight_grad", prefetch=1, grid=(1,),
                   in_specs=[_whole((COND_ROWS, D)), _chip_cols(COND_ROWS, MOD_COLS)], out_specs=_whole((D, MOD_COLS)),
                   out_shape=_sds((D, MOD_COLS), F32))(chip, cond, dmod_all)


def _cond_grad_partial(dmod_all, w_mod, chip):
    def body(ch_ref, d_ref, w_ref, o_ref):
        del ch_ref
        o_ref[...] = lax.dot_general(d_ref[...], w_ref[...], (((1,), (1,)), ((), ())), preferred_element_type=F32, precision=HI)

    return _pallas(body, name="cond_grad_partial", prefetch=1, grid=(1,),
                   in_specs=[pl.BlockSpec((8, MOD_COLS), lambda i, ch: (1, ch[0])), _whole((D, MOD_COLS))],
                   out_specs=_whole((8, D)), out_shape=_sds((8, D), F32))(chip, dmod_all, w_mod)


def _adam_math(w, g, m, v):
    nm = ADAM_B1 * m + (1.0 - ADAM_B1) * g
    nv = ADAM_B2 * v + (1.0 - ADAM_B2) * (g * g)
    c1 = 1.0 - ADAM_B1 ** ADAM_STEP
    c2 = 1.0 - ADAM_B2 ** ADAM_STEP
    return -ADAM_LR * ((nm / c1) / (jnp.sqrt(nv / c2) + ADAM_EPS) + ADAM_WD * w), nm, nv


def _cond_update(parts, c_ctx, m, v):
    def body(p_ref, c_ref, m_ref, v_ref, g_ref, d_ref, nm_ref, nv_ref):
        tot = p_ref[0:1, :]
        for j in range(1, N_CHIPS):
            tot = tot + p_ref[16 * j:16 * j + 1, :]
        cv = c_ref[...]
        sg = _sigmoid(cv)
        g = tot * (sg * (1.0 + cv * (1.0 - sg)))
        g_ref[...] = g
        d_ref[...], nm_ref[...], nv_ref[...] = _adam_math(cv, g, m_ref[...], v_ref[...])

    return _pallas(body, name="cond_update", out_shape=[_sds((1, D), F32)] * 4)(parts, c_ctx, m, v)


def _adamw(w, g, m, v, name):
    r, c = w.shape
    t = _tile(r, (128,)) if r % 128 == 0 and r > 128 else r

    def body(w_ref, g_ref, m_ref, v_ref, d_ref, nm_ref, nv_ref):
        d_ref[...], nm_ref[...], nv_ref[...] = _adam_math(w_ref[...], g_ref[...], m_ref[...], v_ref[...])

    blk = pl.BlockSpec((t, c), lambda i: (i, 0))
    return _pallas(body, name=name, grid=(r // t,), in_specs=[blk] * 4, out_specs=[blk] * 3,
                   out_shape=[_sds((r, c), F32)] * 3, semantics=("parallel",))(w, g, m, v)


def _adamw_cols(w, g_all, m, v, chip, name):
    r, c = w.shape

    def body(ch_ref, w_ref, g_ref, m_ref, v_ref, go_ref, d_ref, nm_ref, nv_ref):
        del ch_ref
        g = g_ref[...]
        go_ref[...] = g
        d_ref[...], nm_ref[...], nv_ref[...] = _adam_math(w_ref[...], g, m_ref[...], v_ref[...])

    return _pallas(body, name=name, prefetch=1, grid=(1,),
                   in_specs=[_whole((r, c)), _chip_cols(r, c), _whole((r, c)), _whole((r, c))],
                   out_specs=[_whole((r, c))] * 4, out_shape=[_sds((r, c), F32)] * 4)(chip, w, g_all, m, v)


def _adamw_halves(name, w, own, other, m, v, core, after):
    r, c = w.shape
    half = r // 2
    t = _tile(half, (128, 352))
    nh = half // t

    def pick(mine):
        def index(i, cr):
            first = cr[0] if mine else 1 - cr[0]
            return (jnp.clip(i - first * nh, 0, nh - 1), 0)
        return pl.BlockSpec((t, c), index)

    def body(c_ref, w_ref, own_ref, oth_ref, m_ref, v_ref, after_ref, g_ref, d_ref, nm_ref, nv_ref):
        del after_ref
        g = jnp.where(pl.program_id(0) // nh == c_ref[0], own_ref[...], oth_ref[...])
        g_ref[...] = g
        d_ref[...], nm_ref[...], nv_ref[...] = _adam_math(w_ref[...], g, m_ref[...], v_ref[...])

    blk = pl.BlockSpec((t, c), lambda i, cr: (i, 0))
    return _pallas(body, name="adamw_" + name, prefetch=1, grid=(2 * nh,),
                   in_specs=[blk, pick(True), pick(False), blk, blk, pl.BlockSpec(memory_space=pl.ANY)], out_specs=[blk] * 4,
                   out_shape=[_sds((r, c), F32)] * 4, semantics=("parallel",))(core, w, own, other, m, v, after)


WEIGHTS = ("c_ctx", "w_mod", "b_mod", "g_norm1", "w_in", "rpb", "conv_w", "conv_b", "ln_g", "ln_b", "w_out", "g_norm2",
           "w_up", "ffn_conv_w", "ffn_conv_b", "w_down", "g_final")
PACK = (("dmod", 6 * D), ("dmod_c", 2 * D), ("g_norm1", D), ("g_norm1_ctx", D), ("g_norm2", D), ("g_final", D),
        ("conv_b", DC), ("ln_g", DC), ("ln_b", DC), ("ffn_conv_b", 2 * DFF), ("ffn_conv_w", 3 * 2 * DFF),
        ("conv_w", CW * DC), ("rpb_rev", NH * 16 * LANES), ("loss", LANES))
PACK_OFF = {}
_o = 0
for _n, _w in PACK:
    PACK_OFF[_n] = (_o, _w)
    _o += _w
PACK_N = -(-_o // (8 * LANES)) * (8 * LANES)
VECTORS = {"b_mod": (6 * D, ("dmod", "dmod_c")), "g_norm1": (D, ("g_norm1", "g_norm1_ctx")), "conv_b": (DC, ("conv_b",)),
           "ln_g": (DC, ("ln_g",)), "ln_b": (DC, ("ln_b",)), "g_norm2": (D, ("g_norm2",)),
           "ffn_conv_b": (2 * DFF, ("ffn_conv_b",)), "g_final": (D, ("g_final",))}
RPB_ROWS = NH * (2 * NA_ROWS - 1)
RPB_COLS = 4 * NA_ROWS - 1


def _pack_small(parts, after):
    arrs, places = [], []
    for name, _ in PACK:
        off, width = PACK_OFF[name]
        group = parts[name]
        rows = group[0].shape[0]
        row_w = sum(a.shape[1] for a in group)
        assert rows * row_w == width, (name, rows, row_w, width)
        col = 0
        for a in group:
            arrs.append(a)
            places.append([off + k * row_w + col for k in range(rows)])
            col += a.shape[1]

    def body(*refs):
        o_ref = refs[-1]
        o_ref[:, _o:PACK_N] = jnp.zeros((1, PACK_N - _o), F32)
        for ref, offs in zip(refs, places):
            n = ref.shape[1]
            for k, off in enumerate(offs):
                o_ref[:, off:off + n] = ref[k:k + 1, :]

    vmem = pl.BlockSpec(memory_space=pltpu.VMEM)
    return _pallas(body, name="pack_small_grads", out_shape=_sds((1, PACK_N), F32),
                   in_specs=[vmem] * len(arrs) + [pl.BlockSpec(memory_space=pl.ANY)], out_specs=vmem)(*arrs, after)


def _small_update(packs, w, m, v):
    names = list(VECTORS)

    def body(*refs):
        it = iter(refs)
        p_ref = next(it)
        wmv = {n: (next(it), next(it), next(it)) for n in names}
        outs = {n: (next(it), next(it), next(it), next(it)) for n in names}
        dmod_ref, cw_ref, fw_ref, rpb_ref, loss_ref = next(it), next(it), next(it), next(it), next(it)

        def total(name):
            off, width = PACK_OFF[name]
            acc = p_ref[0:1, off:off + width]
            for d in range(1, 8):
                acc = acc + p_ref[d:d + 1, off:off + width]
            return acc

        for n in names:
            width, segs = VECTORS[n]
            g = total(segs[0])
            if len(segs) > 1:
                extra = total(segs[1])
                ew = extra.shape[1]
                g = g + extra if ew == width else jnp.concatenate([g[:, :ew] + extra, g[:, ew:]], axis=1)
            w_ref, m_ref, v_ref = wmv[n]
            g_ref, d_ref, nm_ref, nv_ref = outs[n]
            g_ref[...] = g
            d_ref[...], nm_ref[...], nv_ref[...] = _adam_math(w_ref[...], g, m_ref[...], v_ref[...])

        o_dmod = PACK_OFF["dmod"][0]
        dmod_ref[...] = jnp.zeros_like(dmod_ref)
        dmod_ref[0:8, :] = p_ref[:, o_dmod:o_dmod + 6 * D]
        dmod_ref[8:9, 0:2 * D] = total("dmod_c")
        for ref, name, rows in ((cw_ref, "conv_w", CW), (fw_ref, "ffn_conv_w", 3), (rpb_ref, "rpb_rev", NH * 16)):
            flat = total(name)
            n = ref.shape[1]
            for k in range(rows):
                ref[k:k + 1, :] = flat[:, k * n:(k + 1) * n]
        loss_ref[...] = total("loss")

    ins = [packs] + [a[n] for n in names for a in (w, m, v)]
    out_shape = [_sds((1, VECTORS[n][0]), F32) for n in names for _ in range(4)]
    out_shape += [_sds((COND_ROWS, 6 * D), F32), _sds((CW, DC), F32), _sds((3, 2 * DFF), F32), _sds((NH * 16, LANES), F32),
                  _sds((1, LANES), F32)]
    res = _pallas(body, name="small_update", out_shape=out_shape)(*ins)
    per = {n: tuple(res[4 * i:4 * i + 4]) for i, n in enumerate(names)}
    return (per, *res[4 * len(names):])


def _rpb_update(rev, w, m, v):
    def body(r_ref, w_ref, m_ref, v_ref, g_ref, d_ref, nm_ref, nv_ref):
        li = lax.broadcasted_iota(jnp.int32, (LANES, LANES), 0)
        co = lax.broadcasted_iota(jnp.int32, (LANES, LANES), 1)
        lane_of_co0 = GW - 1 + RPB_COLS // 2
        unflip = jnp.where((li == lane_of_co0 - co) & (co < RPB_COLS), 1.0, 0.0).astype(F32)
        g_all = jnp.dot(r_ref[...], unflip, preferred_element_type=F32, precision=HI)
        nr = 2 * NA_ROWS - 1
        for h in range(NH):
            rows = slice(h * nr, (h + 1) * nr)
            g = g_all[h * 16:h * 16 + nr, 0:RPB_COLS]
            g_ref[rows, :] = g
            d_ref[rows, :], nm_ref[rows, :], nv_ref[rows, :] = _adam_math(w_ref[rows, :], g, m_ref[rows, :], v_ref[rows, :])

    return _pallas(body, name="rpb_update", out_shape=[_sds((RPB_ROWS, RPB_COLS), F32)] * 4)(rev, w, m, v)


def kernel(x, c, ctx, c_ctx, w_mod, b_mod, g_norm1, w_in, rpb, conv_w, conv_b, ln_g, ln_b, w_out, g_norm2, w_up, ffn_conv_w, ffn_conv_b, w_down, g_final, loss_target, m_c_ctx, m_w_mod, m_b_mod, m_g_norm1, m_w_in, m_rpb, m_conv_w, m_conv_b, m_ln_g, m_ln_b, m_w_out, m_g_norm2, m_w_up, m_ffn_conv_w, m_ffn_conv_b, m_w_down, m_g_final, v_c_ctx, v_w_mod, v_b_mod, v_g_norm1, v_w_in, v_rpb, v_conv_w, v_conv_b, v_ln_g, v_ln_b, v_w_out, v_g_norm2, v_w_up, v_ffn_conv_w, v_ffn_conv_b, v_w_down, v_g_final):
    w = dict(c_ctx=c_ctx, w_mod=w_mod, b_mod=b_mod, g_norm1=g_norm1, w_in=w_in, rpb=rpb, conv_w=conv_w, conv_b=conv_b,
             ln_g=ln_g, ln_b=ln_b, w_out=w_out, g_norm2=g_norm2, w_up=w_up, ffn_conv_w=ffn_conv_w, ffn_conv_b=ffn_conv_b,
             w_down=w_down, g_final=g_final)
    mom = dict(c_ctx=m_c_ctx, w_mod=m_w_mod, b_mod=m_b_mod, g_norm1=m_g_norm1, w_in=m_w_in, rpb=m_rpb, conv_w=m_conv_w,
               conv_b=m_conv_b, ln_g=m_ln_g, ln_b=m_ln_b, w_out=m_w_out, g_norm2=m_g_norm2, w_up=m_w_up,
               ffn_conv_w=m_ffn_conv_w, ffn_conv_b=m_ffn_conv_b, w_down=m_w_down, g_final=m_g_final)
    var = dict(c_ctx=v_c_ctx, w_mod=v_w_mod, b_mod=v_b_mod, g_norm1=v_g_norm1, w_in=v_w_in, rpb=v_rpb, conv_w=v_conv_w,
               conv_b=v_conv_b, ln_g=v_ln_g, ln_b=v_ln_b, w_out=v_w_out, g_norm2=v_g_norm2, w_up=v_w_up,
               ffn_conv_w=v_ffn_conv_w, ffn_conv_b=v_ffn_conv_b, w_down=v_w_down, g_final=v_g_final)
    xi, yi, ci = lax.axis_index("x"), lax.axis_index("y"), lax.axis_index("c")
    dev = (4 * xi + 2 * yi + ci).astype(jnp.int32).reshape(1)
    chip = (2 * xi + yi).astype(jnp.int32).reshape(1)
    core = ci.astype(jnp.int32).reshape(1)
    c_ctx2 = c_ctx.reshape(1, D)
    g_final2 = g_final.reshape(1, D)
    mom["g_final"], var["g_final"] = m_g_final.reshape(1, D), v_g_final.reshape(1, D)

    got = _gather_small(_pack_cond(c, ffn_conv_w[0], conv_w[0]), "gather_cond")
    cond, ffn_w_all, conv_w_all = _unpack_cond(got, c_ctx2)

    mods = _gather_small(_mod_shard(cond, w_mod[0], b_mod, chip), "gather_mod")
    mod_me, mod_c = _unpack_mod(mods, dev)

    shards = {n: _cast_into_whole(n, w[n][0], chip) for n in BIG_NAMES}
    sems_in, first, token_in = _gather_start([shards["w_in"]], ("w_in",), mod_me, "w_in")
    sems, late, token = _gather_start([shards[n] for n in LATE_NAMES], LATE_NAMES, token_in, "late")
    mod_me = mod_me + token[0:1, 0:1]

    def w_in_all(after):
        arrived = _gather_wait(sems_in, first, ("w_in",), after, "w_in")
        return _forward_halves(list(arrived), ("w_in",), "w_in")[0]

    def late_weights(after):
        arrived = _gather_wait(sems, late, LATE_NAMES, after, "late")
        return _forward_halves(list(arrived), LATE_NAMES, "late")

    rpb_rev = jnp.pad(rpb[0][:, :, ::-1], ((0, 0), (0, 1), (48, LANES - 48 - RPB_COLS))).reshape(NH * 16, LANES)
    vec = dict(g_norm1=g_norm1, g_norm2=g_norm2, g_final=g_final2, conv_w=conv_w_all, conv_b=conv_b, ln_g=ln_g, ln_b=ln_b,
               ffn_conv_w=ffn_w_all, ffn_conv_b=ffn_conv_b)
    started = []

    def begin_early(d_up, d_down):
        started.append(_swap_start([d_up, d_down], EARLY_GRADS, "grad_swap_start_early"))

    def carry_on_early(after):
        sems_, grads_, lands_, _ = started.pop()
        grads_, lands_ = _swap_wait(sems_, grads_, lands_, EARLY_GRADS, after, "grad_swap_wait_early")
        parts_ = [_add_halves(n, grads_[i], lands_[i], core) for i, n in enumerate(EARLY_GRADS)]
        started.append(_exchange_start(parts_, EARLY_GRADS, "grad_exchange_start_early"))
        return started[0][3][0:1, 0:1]

    loss_p, grad_x, d_in, d_out, d_up, d_down, small = _local_step(
        x[0], ctx[0], loss_target[0], mod_me, mod_c, vec, w_in_all, late_weights, rpb_rev, (begin_early, carry_on_early))

    out = {}
    sems_, grads_, lands_, _ = _swap_start([d_in, d_out], LAST_GRADS, "grad_swap_start_last")
    early_own = _reduce_finish(started[0], EARLY_GRADS, grad_x, chip, "early")
    behind_swap = small["rpb_rev"][0:1, 0:1] + small["g_norm1"][1][0:1, 0:1] + early_own[0][0:1, 0:1]
    grads_, lands_ = _swap_wait(sems_, grads_, lands_, LAST_GRADS, behind_swap, "grad_swap_wait_last")
    parts_ = [_add_halves(n, grads_[i], lands_[i], core) for i, n in enumerate(LAST_GRADS)]
    last_started = _exchange_start(parts_, LAST_GRADS, "grad_exchange_start_last")
    early_other = _send_halves(early_own, "grad_send_early", after=last_started[3])
    for i, n in enumerate(EARLY_GRADS):
        out[n] = _adamw_halves(n, w[n][0], early_own[i], early_other[i], mom[n][0], var[n][0], core, early_other[i])
    behind_early = out[EARLY_GRADS[0]][1][0:1, 0:1] + out[EARLY_GRADS[1]][1][0:1, 0:1]

    parts = dict(dmod=small["dmod"], dmod_c=small["dmod_c"], g_norm1=[small["g_norm1"][0]], g_norm1_ctx=[small["g_norm1"][1]],
                 g_norm2=[small["g_norm2"]], g_final=[small["g_final"]], conv_b=[small["conv_b"]], ln_g=[small["ln_g"]],
                 ln_b=[small["ln_b"]], ffn_conv_b=small["ffn_conv_b"], ffn_conv_w=small["ffn_conv_w"],
                 conv_w=[small["conv_w"]], rpb_rev=[small["rpb_rev"]], loss=[loss_p])
    pack = _pack_small(parts, after=behind_early).reshape(8, PACK_N // 8)
    packs = _gather_small(pack, "gather_small_grads").reshape(8, PACK_N)
    w2 = dict(w, g_final=g_final2)
    per, dmod_all, g_conv_w_all, g_ffn_w_all, g_rpb_rev, loss_row = _small_update(packs, w2, mom, var)

    out.update(per)
    out["c_ctx"] = _cond_update(
        _gather_small(_cond_grad_partial(dmod_all, w_mod[0], chip), "gather_cond_grad"),
        c_ctx2, m_c_ctx.reshape(1, D), v_c_ctx.reshape(1, D))
    g_w_mod = _mod_weight_grad(cond, dmod_all, chip)
    out["w_mod"] = (g_w_mod, *_adamw(w_mod[0], g_w_mod, m_w_mod[0], v_w_mod[0], "adamw_w_mod"))
    behind = out["w_mod"][1][0:1, 0:1] + out["c_ctx"][1][0:1, 0:1]
    last_own = _reduce_finish(last_started, LAST_GRADS, behind, chip, "last")
    last_other = _send_halves(last_own, "grad_send_last", after=last_own[0])
    for i, n in enumerate(LAST_GRADS):
        out[n] = _adamw_halves(n, w[n][0], last_own[i], last_other[i], mom[n][0], var[n][0], core, last_other[i])
    out["conv_w"] = _adamw_cols(conv_w[0], g_conv_w_all, m_conv_w[0], v_conv_w[0], chip, "adamw_conv_w")
    out["ffn_conv_w"] = _adamw_cols(ffn_conv_w[0], g_ffn_w_all, m_ffn_conv_w[0], v_ffn_conv_w[0], chip, "adamw_ffn_conv_w")
    flat = lambda a: a.reshape(RPB_ROWS, RPB_COLS)
    out["rpb"] = _rpb_update(g_rpb_rev, flat(rpb), flat(m_rpb), flat(v_rpb))

    res = [[out[n][k].reshape(w[n].shape) for n in WEIGHTS] for k in range(4)]
    return (loss_row[0, 0], grad_x[None], *res[0], *res[1], *res[2], *res[3])
```

```python
import functools

import jax
import jax.numpy as jnp
from jax import lax
from jax.experimental import pallas as pl
from jax.experimental.pallas import tpu as pltpu

F32 = jnp.float32
BF16 = jnp.bfloat16
MXU_DTYPE = jnp.bfloat16

D = 1024
CTX = 256
GW = 64
DA = 512
NH = 8
HD = 64
DC = 512
CW = 31
DFF = 2816
NIN = 3 * DA + 2 * DC
EPS = 1e-6
SCALE = HD ** -0.5
NEG = -1e30
NA_ROWS = 8
PAIR_ROWS = NA_ROWS + 1
TAB_BLOCKS = 17
LANES = 128
VMEM_LIMIT = 56 * 1024 * 1024

ADAM_LR = 0.001
ADAM_B1 = 0.9
ADAM_B2 = 0.999
ADAM_EPS = 1e-08
ADAM_WD = 0.01
ADAM_STEP = 10

MESH = pl.DeviceIdType.MESH


def _pallas(body, *, name, semantics=None, vmem=VMEM_LIMIT, prefetch=0, **kw):
    params = dict(vmem_limit_bytes=vmem)
    if semantics is not None:
        params["dimension_semantics"] = semantics
    if prefetch:
        kw["grid_spec"] = pltpu.PrefetchScalarGridSpec(
            num_scalar_prefetch=prefetch, grid=kw.pop("grid"), in_specs=kw.pop("in_specs"), out_specs=kw.pop("out_specs"),
            scratch_shapes=kw.pop("scratch_shapes", ()))
    return pl.pallas_call(body, name=name, compiler_params=pltpu.CompilerParams(**params), **kw)


def _sds(shape, dtype):
    return jax.ShapeDtypeStruct(shape, dtype)


def _vec_spec(n):
    return pl.BlockSpec((1, n), lambda *_: (0, 0))


def _colsum8(x):
    t, n = x.shape
    return jnp.sum(x.reshape(t // 8, 8, n), axis=0)


def _sigmoid(x):
    return 0.5 * jnp.tanh(0.5 * x) + 0.5


def _pieces(arrs, tile):
    lo, out = 0, []
    for a in arrs:
        nt = a.shape[1] // tile
        assert nt * tile == a.shape[1], (a.shape, tile)
        out.append((lo, nt))
        lo += nt
    return out


def _mm(a, b, *, mode, m, n, k, tm, tn, tk, out_dtype, name, a_off=(0, 0), b_off=(0, 0), k_outer=False,
        out_total=None, o_off=(0, 0), into=None, join=False):
    a_list = list(a) if isinstance(a, (list, tuple)) else [a]
    b_list = list(b) if isinstance(b, (list, tuple)) else [b]
    assert m % tm == 0 and n % tn == 0 and k % tk == 0, (name, m, n, k, tm, tn, tk)
    gi, gj, nk = m // tm, n // tn, k // tk
    a_tile = tm if mode == "tn" else tk
    a_pc = _pieces(a_list, a_tile) if len(a_list) > 1 and not join else [(0, 1 << 30)] * len(a_list)
    if mode == "nt":
        assert len(b_list) == 1
    b_pc = _pieces(b_list, tn) if len(b_list) > 1 and not join else [(0, 1 << 30)] * len(b_list)
    dims = {"nn": (((1,), (0,)), ((), ())), "nt": (((1,), (1,)), ((), ())), "tn": (((0,), (0,)), ((), ()))}[mode]
    k_outer = k_outer and nk > 1

    def ijk(fn):
        return (lambda i, kk, j: fn(i, j, kk)) if k_outer else fn

    def a_spec(lo, cnt):
        def loc(idx):
            return idx + a_off[1] if len(a_list) == 1 else jnp.clip(idx - lo, 0, cnt - 1)
        if mode == "tn":
            return pl.BlockSpec((tk, tm), ijk(lambda i, j, kk: (kk + a_off[0], loc(i))))
        return pl.BlockSpec((tm, tk), ijk(lambda i, j, kk: (i + a_off[0], loc(kk))))

    def b_spec(lo, cnt):
        def loc(idx):
            return idx + b_off[1] if len(b_list) == 1 else jnp.clip(idx - lo, 0, cnt - 1)
        if mode == "nt":
            return pl.BlockSpec((tn, tk), ijk(lambda i, j, kk: (j + b_off[0], kk + b_off[1])))
        return pl.BlockSpec((tk, tn), ijk(lambda i, j, kk: (kk + b_off[0], loc(j))))

    a_specs = [a_spec(*p) for p in a_pc]
    b_specs = [b_spec(*p) for p in b_pc]
    if join:
        if len(a_list) > 1:
            assert mode != "tn" and nk == 1 and sum(x.shape[1] for x in a_list) == k
            a_specs = [pl.BlockSpec((tm, x.shape[1]), lambda i, j, kk: (i, 0)) for x in a_list]
        if len(b_list) > 1:
            assert mode == "tn" and gj == 1 and sum(x.shape[1] for x in b_list) == n
            b_specs = [pl.BlockSpec((tk, x.shape[1]), lambda i, j, kk: (kk, 0)) for x in b_list]

    na, nb = len(a_list), len(b_list)
    in_place = nk > 1 and out_dtype == F32 and not k_outer

    n_in = na + nb + (into is not None)

    def body(*refs):
        a_refs, b_refs, o_ref = refs[:na], refs[na:na + nb], refs[n_in]
        if k_outer:
            i, kk, j = pl.program_id(0), pl.program_id(1), pl.program_id(2)
            acc = refs[n_in + 1].at[j]
        else:
            i, j, kk = pl.program_id(0), pl.program_id(1), pl.program_id(2)
            acc = o_ref if in_place else (refs[n_in + 1] if nk > 1 else None)
        a_idx = i if mode == "tn" else kk

        def whole(piece_refs):
            vals = [r[...].astype(MXU_DTYPE) for r in piece_refs]
            return vals[0] if len(vals) == 1 else jnp.concatenate(vals, axis=1)

        def step(ar, br):
            av = whole(a_refs) if join else ar[...].astype(MXU_DTYPE)
            bv = whole(b_refs) if join else br[...].astype(MXU_DTYPE)
            p = lax.dot_general(av, bv, dims, preferred_element_type=F32)
            if nk == 1:
                o_ref[...] = p.astype(out_dtype)
                return

            @pl.when(kk == 0)
            def _():
                acc[...] = p

            @pl.when(kk > 0)
            def _():
                acc[...] += p

            if not in_place:
                @pl.when(kk == nk - 1)
                def _():
                    o_ref[...] = acc[...].astype(out_dtype)

        if join or (na == 1 and nb == 1):
            step(a_refs[0], b_refs[0])
        else:
            for pa, (alo, acnt) in enumerate(a_pc):
                for pb, (blo, bcnt) in enumerate(b_pc):
                    cond = (a_idx >= alo) & (a_idx < alo + acnt) & (j >= blo) & (j < blo + bcnt)
                    pl.when(cond)(functools.partial(step, a_refs[pa], b_refs[pb]))

    if k_outer:
        grid = (gi, nk, gj)
        o_spec = pl.BlockSpec((tm, tn), lambda i, kk, j: (i + o_off[0], jnp.where(kk == nk - 1, j, 0) + o_off[1]))
        scratch = [pltpu.VMEM((gj, tm, tn), F32)]
        semantics = ("parallel", "arbitrary", "arbitrary")
    else:
        grid = (gi, gj, nk)
        o_spec = pl.BlockSpec((tm, tn), lambda i, j, kk: (i + o_off[0], j + o_off[1]))
        scratch = [pltpu.VMEM((tm, tn), F32)] if nk > 1 and not in_place else []
        semantics = ("parallel", "parallel", "arbitrary")
    ins = [*a_list, *b_list]
    in_specs = a_specs + b_specs
    extra = {}
    if into is not None:
        extra["input_output_aliases"] = {len(ins): 0}
        ins.append(into)
        in_specs.append(pl.BlockSpec(memory_space=pl.ANY))
    return _pallas(
        body, name=name, grid=grid, in_specs=in_specs,
        out_specs=o_spec, out_shape=_sds(out_total or (m, n), out_dtype), scratch_shapes=scratch, semantics=semantics,
        **extra,
    )(*ins)


ROW_TILE = 256


def _rmsmod_fwd(x, ctx, g, sc, sh, csc, csh):
    s = x.shape[0]
    nt = s // ROW_TILE
    assert ctx.shape[0] == ROW_TILE

    def body(x_ref, c_ref, g_ref, sc_ref, sh_ref, csc_ref, csh_ref, o_ref):
        is_ctx = pl.program_id(0) == nt
        xv = jnp.where(is_ctx, c_ref[...], x_ref[...])
        scv = jnp.where(is_ctx, csc_ref[...], sc_ref[...])
        shv = jnp.where(is_ctx, csh_ref[...], sh_ref[...])
        r = lax.rsqrt(jnp.mean(xv * xv, axis=-1, keepdims=True) + EPS)
        y = xv * r * g_ref[...]
        o_ref[...] = (y * (1.0 + scv) + shv).astype(o_ref.dtype)

    return _pallas(
        body, name="rmsmod1_fwd", grid=(nt + 1,),
        in_specs=[pl.BlockSpec((ROW_TILE, D), lambda i: (jnp.minimum(i, nt - 1), 0)),
                  pl.BlockSpec((ROW_TILE, D), lambda i: (0, 0))] + [_vec_spec(D)] * 5,
        out_specs=pl.BlockSpec((ROW_TILE, D), lambda i: (i, 0)),
        out_shape=_sds((s + CTX, D), MXU_DTYPE),
        semantics=("arbitrary",),
    )(x, ctx, g, sc, sh, csc, csh)


def _resid_rmsmod_fwd(x, y, gt, g, sc, sh):
    s = x.shape[0]

    def body(x_ref, y_ref, gt_ref, g_ref, sc_ref, sh_ref, x1_ref, h_ref):
        x1 = x_ref[...] + gt_ref[...] * y_ref[...]
        x1_ref[...] = x1
        r = lax.rsqrt(jnp.mean(x1 * x1, axis=-1, keepdims=True) + EPS)
        h_ref[...] = ((x1 * r * g_ref[...]) * (1.0 + sc_ref[...]) + sh_ref[...]).astype(h_ref.dtype)

    row = pl.BlockSpec((ROW_TILE, D), lambda i: (i, 0))
    return _pallas(
        body, name="resid_rmsmod2_fwd", grid=(s // ROW_TILE,),
        in_specs=[row, row] + [_vec_spec(D)] * 4,
        out_specs=[row, row],
        out_shape=[_sds((s, D), F32), _sds((s, D), MXU_DTYPE)],
        semantics=("parallel",),
    )(x, y, gt, g, sc, sh)


def _final_fwd_bwd(x1, z, gt2, gf, tgt):
    s = x1.shape[0]
    nt = s // ROW_TILE

    def body(x1_ref, z_ref, gt_ref, gf_ref, t_ref, dx2_ref, dz_ref, loss_ref, dgt_ref, dgf_ref, a_loss, a_gt, a_gf):
        i = pl.program_id(0)

        @pl.when(i == 0)
        def _():
            a_loss[...] = jnp.zeros_like(a_loss)
            a_gt[...] = jnp.zeros_like(a_gt)
            a_gf[...] = jnp.zeros_like(a_gf)

        zv = z_ref[...]
        gt = gt_ref[...]
        gf_ = gf_ref[...]
        x2 = x1_ref[...] + gt * zv
        r = lax.rsqrt(jnp.mean(x2 * x2, axis=-1, keepdims=True) + EPS)
        xn = x2 * r
        e = xn * gf_ - t_ref[...]
        a_loss[...] += _colsum8(e * e)
        dyo = e * (1.0 / D)
        a_gf[...] += _colsum8(dyo * xn)
        gdy = gf_ * dyo
        dx2 = r * gdy - xn * (r * r) * jnp.mean(x2 * gdy, axis=-1, keepdims=True)
        dx2_ref[...] = dx2
        dz_ref[...] = (gt * dx2).astype(dz_ref.dtype)
        a_gt[...] += _colsum8(dx2 * zv)

        @pl.when(i == nt - 1)
        def _():
            tot = jnp.sum(jnp.sum(a_loss[...], axis=0, keepdims=True), axis=1, keepdims=True) * (0.5 / D)
            loss_ref[...] = jnp.broadcast_to(tot, loss_ref.shape)
            dgt_ref[...] = jnp.sum(a_gt[...], axis=0, keepdims=True)
            dgf_ref[...] = jnp.sum(a_gf[...], axis=0, keepdims=True)

    row = pl.BlockSpec((ROW_TILE, D), lambda i: (i, 0))
    return _pallas(
        body, name="final_norm_loss", grid=(nt,),
        in_specs=[row, row, _vec_spec(D), _vec_spec(D), row],
        out_specs=[row, row, _vec_spec(LANES), _vec_spec(D), _vec_spec(D)],
        out_shape=[_sds((s, D), F32), _sds((s, D), MXU_DTYPE), _sds((1, LANES), F32), _sds((1, D), F32), _sds((1, D), F32)],
        scratch_shapes=[pltpu.VMEM((8, D), F32)] * 3,
        semantics=("arbitrary",),
    )(x1, z, gt2, gf, tgt)


def _rmsmod_bwd(xin, dh, g, sc, *, name, dh_row_off=0, add=None, resid=None):
    s = xin.shape[0]
    nt = s // ROW_TILE
    want_dx = add is not None
    assert resid is None or want_dx

    def body(*refs):
        it = iter(refs)
        x_ref, dh_ref, g_ref, sc_ref = next(it), next(it), next(it), next(it)
        add_ref = next(it) if want_dx else None
        gt_ref, y_ref = (next(it), next(it)) if resid is not None else (None, None)
        dsh_ref, dsc_ref, dg_ref = next(it), next(it), next(it)
        dx_ref = next(it) if want_dx else None
        dy_ref, dgt_ref = (next(it), next(it)) if resid is not None else (None, None)
        a_sh, a_sc, a_g = next(it), next(it), next(it)
        a_gt = next(it) if resid is not None else None
        i = pl.program_id(0)

        @pl.when(i == 0)
        def _():
            a_sh[...] = jnp.zeros_like(a_sh)
            a_sc[...] = jnp.zeros_like(a_sc)
            a_g[...] = jnp.zeros_like(a_g)
            if a_gt is not None:
                a_gt[...] = jnp.zeros_like(a_gt)

        xv = x_ref[...]
        dhv = dh_ref[...]
        gv = g_ref[...]
        r = lax.rsqrt(jnp.mean(xv * xv, axis=-1, keepdims=True) + EPS)
        xn = xv * r
        a_sh[...] += _colsum8(dhv)
        a_sc[...] += _colsum8(dhv * (xn * gv))
        dn = dhv * (1.0 + sc_ref[...])
        a_g[...] += _colsum8(dn * xn)
        if want_dx:
            gdn = gv * dn
            dx = add_ref[...] + r * gdn - xn * (r * r) * jnp.mean(xv * gdn, axis=-1, keepdims=True)
            dx_ref[...] = dx
            if resid is not None:
                dy_ref[...] = (gt_ref[...] * dx).astype(dy_ref.dtype)
                a_gt[...] += _colsum8(dx * y_ref[...])

        @pl.when(i == nt - 1)
        def _():
            dsh_ref[...] = jnp.sum(a_sh[...], axis=0, keepdims=True)
            dsc_ref[...] = jnp.sum(a_sc[...], axis=0, keepdims=True)
            dg_ref[...] = jnp.sum(a_g[...], axis=0, keepdims=True)
            if a_gt is not None:
                dgt_ref[...] = jnp.sum(a_gt[...], axis=0, keepdims=True)

    row = pl.BlockSpec((ROW_TILE, D), lambda i: (i, 0))
    ins = [xin, dh, g, sc]
    in_specs = [row, pl.BlockSpec((ROW_TILE, D), lambda i: (i + dh_row_off, 0)), _vec_spec(D), _vec_spec(D)]
    out_specs = [_vec_spec(D)] * 3
    out_shape = [_sds((1, D), F32)] * 3
    scratch = [pltpu.VMEM((8, D), F32)] * 3
    if want_dx:
        ins.append(add)
        in_specs.append(row)
        out_specs.append(row)
        out_shape.append(_sds((s, D), F32))
    if resid is not None:
        ins += [resid[0], resid[1]]
        in_specs += [_vec_spec(D), row]
        out_specs += [row, _vec_spec(D)]
        out_shape += [_sds((s, D), MXU_DTYPE), _sds((1, D), F32)]
        scratch.append(pltpu.VMEM((8, D), F32))
    return _pallas(body, name=name, grid=(nt,), in_specs=in_specs, out_specs=out_specs, out_shape=out_shape,
                   scratch_shapes=scratch, semantics=("arbitrary",))(*ins)


FF_TILE = 128
FF_CHUNK = 128
HALO = 8


def _shift3(pad_ref, r0, ch):
    return tuple(pad_ref[pl.ds(r0 + HALO + d, ch), :] for d in (-1, 0, 1))


def _fill_padded(pad_ref, src_ref, s, ch, halo):
    zeros = jnp.zeros((halo, pad_ref.shape[1]), F32)
    pad_ref[0:halo, :] = zeros
    pad_ref[s + halo:s + 2 * halo, :] = zeros

    def cp(c, carry):
        r0 = pl.multiple_of(c * ch, ch)
        pad_ref[pl.ds(r0 + halo, ch), :] = src_ref[pl.ds(r0, ch), :].astype(F32)
        return carry

    lax.fori_loop(0, s // ch, cp, 0)


def _ffn_act_fwd(u, w, b):
    s = u.shape[0]
    nj = DFF // FF_TILE
    ch = FF_CHUNK

    def body(ug_ref, uv_ref, wg_ref, wv_ref, bg_ref, bv_ref, f_ref, gpad, vpad):
        _fill_padded(gpad, ug_ref, s, ch, HALO)
        _fill_padded(vpad, uv_ref, s, ch, HALO)

        def conv(pad, w_ref, b_ref, r0):
            prev, cur, nxt = _shift3(pad, r0, ch)
            return w_ref[0:1, :] * prev + w_ref[1:2, :] * cur + w_ref[2:3, :] * nxt + b_ref[...]

        def step(c, carry):
            r0 = pl.multiple_of(c * ch, ch)
            gc = conv(gpad, wg_ref, bg_ref, r0)
            vc = conv(vpad, wv_ref, bv_ref, r0)
            f_ref[pl.ds(r0, ch), :] = (gc * _sigmoid(gc) * vc).astype(f_ref.dtype)
            return carry

        lax.fori_loop(0, s // ch, step, 0)

    col = lambda off: pl.BlockSpec((s, FF_TILE), lambda j: (0, j + off))
    wsp = lambda off: pl.BlockSpec((3, FF_TILE), lambda j: (0, j + off))
    bsp = lambda off: pl.BlockSpec((1, FF_TILE), lambda j: (0, j + off))
    return _pallas(
        body, name="ffn_act_fwd", grid=(nj,),
        in_specs=[col(0), col(nj), wsp(0), wsp(nj), bsp(0), bsp(nj)],
        out_specs=col(0), out_shape=_sds((s, DFF), MXU_DTYPE),
        scratch_shapes=[pltpu.VMEM((s + 2 * HALO, FF_TILE), F32)] * 2,
        semantics=("parallel",),
    )(u, u, w, w, b, b)


def _ffn_act_bwd(u, df, w, b):
    s = u.shape[0]
    nj = DFF // FF_TILE
    ch = FF_CHUNK

    def body(ug_ref, uv_ref, df_ref, wg_ref, wv_ref, bg_ref, bv_ref,
             dug_ref, duv_ref, dwg_ref, dwv_ref, dbg_ref, dbv_ref, gpad, vpad, dgpad, dvpad, acc):
        _fill_padded(gpad, ug_ref, s, ch, HALO)
        _fill_padded(vpad, uv_ref, s, ch, HALO)
        zeros = jnp.zeros((HALO, FF_TILE), F32)
        for p in (dgpad, dvpad):
            p[0:HALO, :] = zeros
            p[s + HALO:s + 2 * HALO, :] = zeros
        acc[...] = jnp.zeros_like(acc)

        def step(c, carry):
            r0 = pl.multiple_of(c * ch, ch)
            gs = _shift3(gpad, r0, ch)
            vs = _shift3(vpad, r0, ch)
            gc = wg_ref[0:1, :] * gs[0] + wg_ref[1:2, :] * gs[1] + wg_ref[2:3, :] * gs[2] + bg_ref[...]
            vc = wv_ref[0:1, :] * vs[0] + wv_ref[1:2, :] * vs[1] + wv_ref[2:3, :] * vs[2] + bv_ref[...]
            sg = _sigmoid(gc)
            dfv = df_ref[pl.ds(r0, ch), :].astype(F32)
            dgc = dfv * vc * (sg * (1.0 + gc * (1.0 - sg)))
            dvc = dfv * (gc * sg)
            dgpad[pl.ds(r0 + HALO, ch), :] = dgc
            dvpad[pl.ds(r0 + HALO, ch), :] = dvc
            for t in range(3):
                acc[8 * t:8 * t + 8, :] += _colsum8(dgc * gs[t])
                acc[24 + 8 * t:32 + 8 * t, :] += _colsum8(dvc * vs[t])
            acc[48:56, :] += _colsum8(dgc)
            acc[56:64, :] += _colsum8(dvc)
            return carry

        lax.fori_loop(0, s // ch, step, 0)

        def step2(c, carry):
            r0 = pl.multiple_of(c * ch, ch)
            for pad, w_ref, o_ref in ((dgpad, wg_ref, dug_ref), (dvpad, wv_ref, duv_ref)):
                prev, cur, nxt = _shift3(pad, r0, ch)
                o_ref[pl.ds(r0, ch), :] = (w_ref[0:1, :] * nxt + w_ref[1:2, :] * cur + w_ref[2:3, :] * prev).astype(o_ref.dtype)
            return carry

        lax.fori_loop(0, s // ch, step2, 0)
        for t in range(3):
            dwg_ref[t:t + 1, :] = jnp.sum(acc[8 * t:8 * t + 8, :], axis=0, keepdims=True)
            dwv_ref[t:t + 1, :] = jnp.sum(acc[24 + 8 * t:32 + 8 * t, :], axis=0, keepdims=True)
        dbg_ref[...] = jnp.sum(acc[48:56, :], axis=0, keepdims=True)
        dbv_ref[...] = jnp.sum(acc[56:64, :], axis=0, keepdims=True)

    col = lambda off: pl.BlockSpec((s, FF_TILE), lambda j: (0, j + off))
    wsp = lambda off: pl.BlockSpec((3, FF_TILE), lambda j: (0, j + off))
    bsp = lambda off: pl.BlockSpec((1, FF_TILE), lambda j: (0, j + off))
    return _pallas(
        body, name="ffn_act_bwd", grid=(nj,),
        in_specs=[col(0), col(nj), col(0), wsp(0), wsp(nj), bsp(0), bsp(nj)],
        out_specs=[col(0), col(0), wsp(0), wsp(0), bsp(0), bsp(0)],
        out_shape=[_sds((s, DFF), MXU_DTYPE)] * 2 + [_sds((3, DFF), F32)] * 2 + [_sds((1, DFF), F32)] * 2,
        scratch_shapes=[pltpu.VMEM((s + 2 * HALO, FF_TILE), F32)] * 4 + [pltpu.VMEM((64, FF_TILE), F32)],
        semantics=("parallel",),
    )(u, u, df, w, w, b, b)


CONV_CHUNK = 64
CONV_HALO = 16


def _tap(pad_ref, r0, k):
    return pad_ref[pl.ds(r0 + CONV_HALO - CW // 2 + k, CONV_CHUNK), :]


def _glu_into(pad_ref, a_ref, g_ref, s):
    zeros = jnp.zeros((CONV_HALO, LANES), F32)
    pad_ref[0:CONV_HALO, :] = zeros
    pad_ref[s + CONV_HALO:s + 2 * CONV_HALO, :] = zeros

    def cp(c, carry):
        r0 = pl.multiple_of(c * ROW_TILE, ROW_TILE)
        pad_ref[pl.ds(r0 + CONV_HALO, ROW_TILE), :] = a_ref[pl.ds(r0, ROW_TILE), :] * _sigmoid(g_ref[pl.ds(r0, ROW_TILE), :])
        return carry

    lax.fori_loop(0, s // ROW_TILE, cp, 0)


def _conf_conv_fwd(ag, conv_w, conv_b):
    s = ag.shape[0]
    nc = DC // LANES

    def body(a_ref, g_ref, w_ref, b_ref, o_ref, upad):
        _glu_into(upad, a_ref, g_ref, s)

        def step(c, carry):
            r0 = pl.multiple_of(c * CONV_CHUNK, CONV_CHUNK)
            acc = jnp.broadcast_to(b_ref[...], (CONV_CHUNK, LANES))
            for k in range(CW):
                acc = acc + w_ref[k:k + 1, :] * _tap(upad, r0, k)
            o_ref[pl.ds(r0, CONV_CHUNK), :] = acc
            return carry

        lax.fori_loop(0, s // CONV_CHUNK, step, 0)

    col = lambda off: pl.BlockSpec((s, LANES), lambda c: (0, c + off))
    return _pallas(
        body, name="conf_conv_fwd", grid=(nc,),
        in_specs=[col(0), col(nc), pl.BlockSpec((CW, LANES), lambda c: (0, c)), pl.BlockSpec((1, LANES), lambda c: (0, c))],
        out_specs=col(0), out_shape=_sds((s, DC), F32),
        scratch_shapes=[pltpu.VMEM((s + 2 * CONV_HALO, LANES), F32)],
        semantics=("parallel",),
    )(ag, ag, conv_w, conv_b)


def _ln_stats(x):
    mu = jnp.mean(x, axis=-1, keepdims=True)
    xc = x - mu
    var = jnp.mean(xc * xc, axis=-1, keepdims=True)
    rstd = lax.rsqrt(var + EPS)
    return xc * rstd, rstd


def _conf_ln_fwd(u1, ln_g, ln_b, ycat):
    s = u1.shape[0]

    def body(u_ref, g_ref, b_ref, ycat_ref, o_ref):
        del ycat_ref
        xhat, _ = _ln_stats(u_ref[...])
        y = xhat * g_ref[...] + b_ref[...]
        o_ref[...] = (y * _sigmoid(y)).astype(o_ref.dtype)

    return _pallas(
        body, name="conf_ln_fwd", grid=(s // ROW_TILE,),
        in_specs=[pl.BlockSpec((ROW_TILE, DC), lambda i: (i, 0)), _vec_spec(DC), _vec_spec(DC),
                  pl.BlockSpec(memory_space=pl.ANY)],
        out_specs=pl.BlockSpec((ROW_TILE, DC), lambda i: (i, 1)),
        out_shape=_sds(ycat.shape, ycat.dtype),
        input_output_aliases={3: 0},
        semantics=("parallel",),
    )(u1, ln_g, ln_b, ycat)


def _conf_ln_bwd(dycat, u1, ln_g, ln_b):
    s = u1.shape[0]
    nt = s // ROW_TILE

    def body(dy_ref, u_ref, g_ref, b_ref, du_ref, dg_ref, db_ref, a_g, a_b):
        i = pl.program_id(0)

        @pl.when(i == 0)
        def _():
            a_g[...] = jnp.zeros_like(a_g)
            a_b[...] = jnp.zeros_like(a_b)

        xhat, rstd = _ln_stats(u_ref[...])
        gv = g_ref[...]
        y = xhat * gv + b_ref[...]
        sg = _sigmoid(y)
        dyl = dy_ref[...] * (sg * (1.0 + y * (1.0 - sg)))
        a_g[...] += _colsum8(dyl * xhat)
        a_b[...] += _colsum8(dyl)
        dxh = dyl * gv
        du_ref[...] = rstd * (dxh - jnp.mean(dxh, axis=-1, keepdims=True)
                              - xhat * jnp.mean(dxh * xhat, axis=-1, keepdims=True))

        @pl.when(i == nt - 1)
        def _():
            dg_ref[...] = jnp.sum(a_g[...], axis=0, keepdims=True)
            db_ref[...] = jnp.sum(a_b[...], axis=0, keepdims=True)

    return _pallas(
        body, name="conf_ln_bwd", grid=(nt,),
        in_specs=[pl.BlockSpec((ROW_TILE, DC), lambda i: (i, 1)), pl.BlockSpec((ROW_TILE, DC), lambda i: (i, 0)),
                  _vec_spec(DC), _vec_spec(DC)],
        out_specs=[pl.BlockSpec((ROW_TILE, DC), lambda i: (i, 0)), _vec_spec(DC), _vec_spec(DC)],
        out_shape=[_sds((s, DC), F32), _sds((1, DC), F32), _sds((1, DC), F32)],
        scratch_shapes=[pltpu.VMEM((8, DC), F32)] * 2,
        semantics=("arbitrary",),
    )(dycat, u1, ln_g, ln_b)


def _conf_conv_bwd(ag, du1, conv_w, rows_out):
    s = ag.shape[0]
    nc = DC // LANES

    def body(a_ref, g_ref, d_ref, w_ref, da_ref, dg_ref, dw_ref, db_ref, upad, dpad, acc):
        _glu_into(upad, a_ref, g_ref, s)
        _fill_padded(dpad, d_ref, s, ROW_TILE, CONV_HALO)
        acc[...] = jnp.zeros_like(acc)

        def step(c, carry):
            r0 = pl.multiple_of(c * CONV_CHUNK, CONV_CHUNK)
            dcur = dpad[pl.ds(r0 + CONV_HALO, CONV_CHUNK), :]
            du0 = jnp.zeros((CONV_CHUNK, LANES), F32)
            for k in range(CW):
                du0 = du0 + w_ref[k:k + 1, :] * _tap(dpad, r0, CW - 1 - k)
                acc[8 * k:8 * k + 8, :] += _colsum8(dcur * _tap(upad, r0, k))
            acc[8 * CW:8 * CW + 8, :] += _colsum8(dcur)
            av = a_ref[pl.ds(r0, CONV_CHUNK), :]
            sg = _sigmoid(g_ref[pl.ds(r0, CONV_CHUNK), :])
            da_ref[pl.ds(r0, CONV_CHUNK), :] = (du0 * sg).astype(da_ref.dtype)
            dg_ref[pl.ds(r0, CONV_CHUNK), :] = (du0 * av * (sg * (1.0 - sg))).astype(dg_ref.dtype)
            return carry

        lax.fori_loop(0, s // CONV_CHUNK, step, 0)
        if rows_out > s:
            zeros = jnp.zeros((rows_out - s, LANES), da_ref.dtype)
            da_ref[s:rows_out, :] = zeros
            dg_ref[s:rows_out, :] = zeros
        for k in range(CW):
            dw_ref[k:k + 1, :] = jnp.sum(acc[8 * k:8 * k + 8, :], axis=0, keepdims=True)
        db_ref[...] = jnp.sum(acc[8 * CW:8 * CW + 8, :], axis=0, keepdims=True)

    col = lambda off: pl.BlockSpec((s, LANES), lambda c: (0, c + off))
    ocol = pl.BlockSpec((rows_out, LANES), lambda c: (0, c))
    return _pallas(
        body, name="conf_conv_bwd", grid=(nc,),
        in_specs=[col(0), col(nc), col(0), pl.BlockSpec((CW, LANES), lambda c: (0, c))],
        out_specs=[ocol, ocol, pl.BlockSpec((CW, LANES), lambda c: (0, c)), pl.BlockSpec((1, LANES), lambda c: (0, c))],
        out_shape=[_sds((rows_out, DC), MXU_DTYPE)] * 2 + [_sds((CW, DC), F32), _sds((1, DC), F32)],
        scratch_shapes=[pltpu.VMEM((s + 2 * CONV_HALO, LANES), F32)] * 2 + [pltpu.VMEM((8 * (CW + 1), LANES), F32)],
        semantics=("parallel",),
    )(ag, ag, du1, conv_w)


Q_TILE = 2 * GW
K_WIN = PAIR_ROWS * GW


def _bias_table(rpb_rev):
    def body(p_ref, t_ref):
        kcol = lax.broadcasted_iota(jnp.int32, (GW, LANES), 0)
        lane = lax.broadcasted_iota(jnp.int32, (GW, LANES), 1)
        qcol = lane % GW
        cs = jnp.clip(qcol - NA_ROWS, 0, GW - 2 * NA_ROWS)
        colvalid = (kcol >= cs) & (kcol < cs + 2 * NA_ROWS)
        neg = jnp.full((GW, LANES), NEG, F32)

        def skew(h, ro, shift):
            if ro < 0 or ro >= 2 * NA_ROWS - 1:
                return neg
            row = jnp.broadcast_to(p_ref[h * 16 + ro:h * 16 + ro + 1, :], (GW, LANES))
            return pltpu.roll(row, shift, 1, stride=1, stride_axis=0)

        for h in range(NH):
            for b in range(TAB_BLOCKS):
                val = jnp.where(lane < GW, skew(h, b - 1, GW + 1), skew(h, b - 2, 1))
                t_ref[h, b * GW:(b + 1) * GW, :] = jnp.where(colvalid, val, neg)

    return _pallas(body, name="attn_bias_table", out_shape=_sds((NH, TAB_BLOCKS * GW, LANES), F32))(rpb_rev)


def _rpb_grad(tt):
    def body(t_ref, o_ref):
        lane = lax.broadcasted_iota(jnp.int32, (GW, LANES), 1)
        si = lax.broadcasted_iota(jnp.int32, (GW, GW), 0)
        ti = lax.broadcasted_iota(jnp.int32, (GW, GW), 1)
        flip = jnp.where(si + ti == GW - 1, 1.0, 0.0).astype(F32)
        o_ref[...] = jnp.zeros_like(o_ref)
        for h in range(NH):
            for ro in range(2 * NA_ROWS - 1):
                lo = t_ref[h, (ro + 1) * GW:(ro + 2) * GW, :]
                hi = t_ref[h, (ro + 2) * GW:(ro + 3) * GW, :]
                g = jnp.where(lane < GW, lo + pltpu.roll(hi, GW, 1), 0.0)
                gf = jnp.dot(flip, g, preferred_element_type=F32, precision=lax.Precision.HIGHEST)
                sk = pltpu.roll(gf, 0, 1, stride=1, stride_axis=0)
                o_ref[h * 16 + ro:h * 16 + ro + 1, :] = jnp.sum(sk, axis=0, keepdims=True)

    return _pallas(body, name="attn_rpb_grad", out_shape=_sds((NH * 16, LANES), F32))(tt)


def _attn_geometry(i, rows):
    wsp = jnp.clip(2 * i - NA_ROWS // 2, 0, rows - PAIR_ROWS)
    k0 = pl.multiple_of(wsp * GW, GW)
    t0 = pl.multiple_of((wsp - 2 * i + NA_ROWS) * GW, GW)
    rr = lax.broadcasted_iota(jnp.int32, (GW, Q_TILE), 1) // GW
    wsr = jnp.clip(2 * i + rr - NA_ROWS // 2, 0, rows - NA_ROWS)
    edge_masks = tuple(jnp.where((kr >= wsr) & (kr < wsr + NA_ROWS), 0.0, NEG).astype(F32)
                       for kr in (wsp, wsp + PAIR_ROWS - 1))
    return k0, t0, edge_masks


def _biased(s_raw, bias, edge_masks):
    x = s_raw + bias
    return jnp.concatenate([x[:GW] + edge_masks[0], x[GW:K_WIN - GW], x[K_WIN - GW:] + edge_masks[1]], axis=0)


def _two_heads_on_lanes(xt):
    feat = lax.broadcasted_iota(jnp.int32, xt.shape, 0)
    zero = jnp.zeros_like(xt)
    return jnp.concatenate([jnp.where(feat < HD, xt, zero), jnp.where(feat >= HD, xt, zero)], axis=1)


def _two_heads_on_rows(x):
    lane = lax.broadcasted_iota(jnp.int32, x.shape, 1)
    zero = jnp.zeros_like(x)
    return jnp.concatenate([jnp.where(lane < HD, x, zero), jnp.where(lane >= HD, x, zero)], axis=0)


def _pick_heads(x2):
    n = x2.shape[0] // 2
    lane = lax.broadcasted_iota(jnp.int32, (n, LANES), 1)
    return jnp.where(lane < HD, x2[:n], x2[n:])


_TN = (((0,), (0,)), ((), ()))


def _attn_fwd(qkv, tab, s):
    rows = s // GW
    npair = rows // 2

    def body(q_ref, kv_ref, tab_ref, o_ref, lse_ref):
        i = pl.program_id(0)
        k0, t0, edge_masks = _attn_geometry(i, rows)
        for p in range(NH // 2):
            cq = slice(p * LANES, (p + 1) * LANES)
            ck = slice(DA + p * LANES, DA + (p + 1) * LANES)
            cv = slice(2 * DA + p * LANES, 2 * DA + (p + 1) * LANES)
            qm2 = _two_heads_on_lanes(q_ref[:, cq].T) * SCALE
            s_loc = jnp.dot(kv_ref[pl.ds(k0, K_WIN), ck], qm2, preferred_element_type=F32)
            s_ctx = jnp.dot(kv_ref[pl.ds(s, CTX), ck], qm2, preferred_element_type=F32)
            p_loc, p_ctx = [], []
            for hh in range(2):
                h = 2 * p + hh
                ch = slice(hh * Q_TILE, (hh + 1) * Q_TILE)
                sl = _biased(s_loc[:, ch], tab_ref[h, pl.ds(t0, K_WIN), :], edge_masks)
                sc = s_ctx[:, ch]
                m = jnp.maximum(jnp.max(sl, axis=0, keepdims=True), jnp.max(sc, axis=0, keepdims=True))
                el = jnp.exp(sl - m)
                ec = jnp.exp(sc - m)
                l = jnp.sum(el, axis=0, keepdims=True) + jnp.sum(ec, axis=0, keepdims=True)
                inv = 1.0 / l
                lse_ref[h:h + 1, :] = m + jnp.log(l)
                p_loc.append((el * inv).astype(MXU_DTYPE))
                p_ctx.append((ec * inv).astype(MXU_DTYPE))
            o2 = (lax.dot_general(jnp.concatenate(p_loc, axis=1), kv_ref[pl.ds(k0, K_WIN), cv], _TN, preferred_element_type=F32)
                  + lax.dot_general(jnp.concatenate(p_ctx, axis=1), kv_ref[pl.ds(s, CTX), cv], _TN, preferred_element_type=F32))
            o_ref[:, cq] = _pick_heads(o2).astype(o_ref.dtype)

    return _pallas(
        body, name="attn_fwd", grid=(npair,),
        in_specs=[pl.BlockSpec((Q_TILE, DA), lambda i: (i, 0)), pl.BlockSpec(memory_space=pltpu.VMEM),
                  pl.BlockSpec(memory_space=pltpu.VMEM)],
        out_specs=[pl.BlockSpec((Q_TILE, DA), lambda i: (i, 0)), pl.BlockSpec((NH, Q_TILE), lambda i: (0, i))],
        out_shape=[_sds((s, D), MXU_DTYPE), _sds((NH, s), F32)],
        semantics=("arbitrary",),
    )(qkv, qkv, tab)


def _attn_bwd(qkv, tab, lse, dycat, s):
    rows = s // GW
    npair = rows // 2
    sa = s + CTX
    nzero = CTX // Q_TILE

    def body(q_ref, do_ref, lse_ref, kv_ref, tab_ref, dq_ref, dkv_ref, tt_ref, dk_acc, dv_acc):
        i = pl.program_id(0)

        @pl.when(i == 0)
        def _():
            dk_acc[...] = jnp.zeros_like(dk_acc)
            dv_acc[...] = jnp.zeros_like(dv_acc)
            tt_ref[...] = jnp.zeros_like(tt_ref)

        @pl.when(i >= npair)
        def _():
            dq_ref[...] = jnp.zeros_like(dq_ref)

        @pl.when(i < npair)
        def _():
            k0, t0, edge_masks = _attn_geometry(i, rows)
            for p in range(NH // 2):
                cq = slice(p * LANES, (p + 1) * LANES)
                ck = slice(DA + p * LANES, DA + (p + 1) * LANES)
                cv = slice(2 * DA + p * LANES, 2 * DA + (p + 1) * LANES)
                qp = q_ref[:, cq] * SCALE
                dop = do_ref[:, cq].astype(MXU_DTYPE)
                qm2 = _two_heads_on_lanes(qp.T)
                dom2 = _two_heads_on_lanes(dop.T)
                kw = kv_ref[pl.ds(k0, K_WIN), ck]
                kc = kv_ref[pl.ds(s, CTX), ck]
                vw = kv_ref[pl.ds(k0, K_WIN), cv]
                vc = kv_ref[pl.ds(s, CTX), cv]
                s_loc = jnp.dot(kw, qm2, preferred_element_type=F32)
                s_ctx = jnp.dot(kc, qm2, preferred_element_type=F32)
                dp_loc = jnp.dot(vw, dom2, preferred_element_type=F32)
                dp_ctx = jnp.dot(vc, dom2, preferred_element_type=F32)
                p_loc, p_ctx, ds_loc, ds_ctx = [], [], [], []
                for hh in range(2):
                    h = 2 * p + hh
                    ch = slice(hh * Q_TILE, (hh + 1) * Q_TILE)
                    lse_h = lse_ref[h:h + 1, :]
                    pl_ = jnp.exp(_biased(s_loc[:, ch], tab_ref[h, pl.ds(t0, K_WIN), :], edge_masks) - lse_h)
                    pc_ = jnp.exp(s_ctx[:, ch] - lse_h)
                    dpl = dp_loc[:, ch]
                    dpc = dp_ctx[:, ch]
                    delta = jnp.sum(pl_ * dpl, axis=0, keepdims=True) + jnp.sum(pc_ * dpc, axis=0, keepdims=True)
                    dsl = pl_ * (dpl - delta)
                    dsc = pc_ * (dpc - delta)
                    tt_ref[h, pl.ds(t0, K_WIN), :] += dsl
                    p_loc.append(pl_.astype(MXU_DTYPE))
                    p_ctx.append(pc_.astype(MXU_DTYPE))
                    ds_loc.append(dsl.astype(MXU_DTYPE))
                    ds_ctx.append(dsc.astype(MXU_DTYPE))
                p_loc, p_ctx = jnp.concatenate(p_loc, axis=1), jnp.concatenate(p_ctx, axis=1)
                ds_loc, ds_ctx = jnp.concatenate(ds_loc, axis=1), jnp.concatenate(ds_ctx, axis=1)
                do_rows = _two_heads_on_rows(dop)
                q_rows = _two_heads_on_rows(qp)
                dv_acc[pl.ds(k0, K_WIN), cq] += jnp.dot(p_loc, do_rows, preferred_element_type=F32)
                dv_acc[pl.ds(s, CTX), cq] += jnp.dot(p_ctx, do_rows, preferred_element_type=F32)
                dk_acc[pl.ds(k0, K_WIN), cq] += jnp.dot(ds_loc, q_rows, preferred_element_type=F32)
                dk_acc[pl.ds(s, CTX), cq] += jnp.dot(ds_ctx, q_rows, preferred_element_type=F32)
                dq2 = (lax.dot_general(ds_loc, kw, _TN, preferred_element_type=F32)
                       + lax.dot_general(ds_ctx, kc, _TN, preferred_element_type=F32))
                dq_ref[:, cq] = (_pick_heads(dq2) * SCALE).astype(dq_ref.dtype)

        @pl.when(i == npair - 1)
        def _():
            def cp(c, carry):
                r0 = pl.multiple_of(c * ROW_TILE, ROW_TILE)
                dkv_ref[pl.ds(r0, ROW_TILE), 0:DA] = dk_acc[pl.ds(r0, ROW_TILE), :].astype(dkv_ref.dtype)
                dkv_ref[pl.ds(r0, ROW_TILE), DA:2 * DA] = dv_acc[pl.ds(r0, ROW_TILE), :].astype(dkv_ref.dtype)
                return carry

            lax.fori_loop(0, sa // ROW_TILE, cp, 0)

    qmap = lambda i: (jnp.minimum(i, npair - 1), 0)
    return _pallas(
        body, name="attn_bwd", grid=(npair + nzero,),
        in_specs=[pl.BlockSpec((Q_TILE, DA), qmap), pl.BlockSpec((Q_TILE, DA), qmap),
                  pl.BlockSpec((NH, Q_TILE), lambda i: (0, jnp.minimum(i, npair - 1))),
                  pl.BlockSpec(memory_space=pltpu.VMEM), pl.BlockSpec(memory_space=pltpu.VMEM)],
        out_specs=[pl.BlockSpec((Q_TILE, DA), lambda i: (i, 0)), pl.BlockSpec(memory_space=pltpu.VMEM),
                   pl.BlockSpec(memory_space=pltpu.VMEM)],
        out_shape=[_sds((sa, DA), MXU_DTYPE), _sds((sa, 2 * DA), MXU_DTYPE), _sds((NH, TAB_BLOCKS * GW, LANES), F32)],
        scratch_shapes=[pltpu.VMEM((sa, DA), F32)] * 2,
        semantics=("arbitrary",),
    )(qkv, dycat, lse, qkv, tab)


def _tile(n, prefs):
    for t in prefs:
        if n % t == 0:
            return t
    raise ValueError((n, prefs))


def _local_step(x, ctx, tgt, mod, mod_c, vec, w_in, late_weights, rpb_rev, early_grads=None):
    s = x.shape[0]
    sa = s + CTX
    ts = _tile(s, (1024, 512, 256))
    ts2 = _tile(s, (2048, 1024, 512, 256))
    tsa = _tile(sa, (1088, 640, 256))
    tsa2 = _tile(sa, (2176, 640, 256))
    sh1, sc1, gt1, sh2, sc2, gt2 = (mod[i:i + 1] for i in range(6))
    csh1, csc1 = mod_c[0:1], mod_c[1:2]
    act = MXU_DTYPE

    tab = _bias_table(rpb_rev)
    h_all = _rmsmod_fwd(x, ctx, vec["g_norm1"], sc1, sh1, csc1, csh1)
    w_in = w_in(h_all) if callable(w_in) else w_in
    qkv = _mm(h_all, w_in, mode="nn", m=sa, n=3 * DA, k=D, tm=tsa2, tn=512, tk=D, out_dtype=MXU_DTYPE, name="mm_qkv")
    ag = _mm(h_all, w_in, mode="nn", m=s, n=2 * DC, k=D, tm=ts2, tn=512, tk=D, out_dtype=F32, name="mm_ag", b_off=(0, 3))
    ycat, lse = _attn_fwd(qkv, tab, s)
    u1 = _conf_conv_fwd(ag, vec["conv_w"], vec["conv_b"])
    ycat = _conf_ln_fwd(u1, vec["ln_g"], vec["ln_b"], ycat)
    w_out, w_up, w_down = late_weights(ycat) if callable(late_weights) else late_weights
    y = _mm(ycat, w_out, mode="nn", m=s, n=D, k=D, tm=ts2, tn=512, tk=D, out_dtype=F32, name="mm_out")
    x1, h2 = _resid_rmsmod_fwd(x, y, gt1, vec["g_norm2"], sc2, sh2)
    u = _mm(h2, w_up, mode="nn", m=s, n=2 * DFF, k=D, tm=ts2, tn=512, tk=D, out_dtype=act, name="mm_up")
    f = _ffn_act_fwd(u, vec["ffn_conv_w"], vec["ffn_conv_b"])
    z = _mm(f, w_down, mode="nn", m=s, n=D, k=DFF, tm=ts, tn=D, tk=DFF, out_dtype=F32, name="mm_down")
    dx2, dz, loss, dgt2, dgf = _final_fwd_bwd(x1, z, gt2, vec["g_final"], tgt)

    df = _mm(dz, w_down, mode="nt", m=s, n=DFF, k=D, tm=ts, tn=DFF, tk=D, out_dtype=act, name="mm_down_dx")
    d_w_down = _mm(f, dz, mode="tn", m=DFF, n=D, k=s, tm=DFF, tn=D, tk=ts, out_dtype=F32, name="mm_down_dw")
    dug, duv, dfw_g, dfw_v, dfb_g, dfb_v = _ffn_act_bwd(u, df, vec["ffn_conv_w"], vec["ffn_conv_b"])
    dw_kw = dict(mode="tn", m=D, n=DFF, k=s, tm=D, tn=DFF, tk=ts, out_dtype=F32, out_total=(D, 2 * DFF))
    d_w_up = _mm(h2, dug, name="mm_up_dw_gate", **dw_kw)
    d_w_up = _mm(h2, duv, name="mm_up_dw_val", o_off=(0, 1), into=d_w_up, **dw_kw)
    if early_grads is not None:
        early_grads[0](d_w_up, d_w_down)
    dh2 = _mm([dug, duv], w_up, mode="nt", m=s, n=D, k=2 * DFF, tm=ts, tn=D, tk=DFF, out_dtype=F32, name="mm_up_dx")
    sc2_b = sc2 if early_grads is None else sc2 + early_grads[1](dh2)
    dsh2, dsc2, dg2, dx1, dy, dgt1 = _rmsmod_bwd(x1, dh2, vec["g_norm2"], sc2_b, name="rmsmod2_bwd", add=dx2, resid=(gt1, y))
    dycat = _mm(dy, w_out, mode="nt", m=s, n=D, k=D, tm=ts2, tn=512, tk=D, out_dtype=F32, name="mm_out_dx")
    d_w_out = _mm(ycat, dy, mode="tn", m=D, n=D, k=s, tm=D, tn=D, tk=ts, out_dtype=F32, name="mm_out_dw")
    du1, dln_g, dln_b = _conf_ln_bwd(dycat, u1, vec["ln_g"], vec["ln_b"])
    da, dg, dconv_w, dconv_b = _conf_conv_bwd(ag, du1, vec["conv_w"], sa)
    dq, dkv, tt = _attn_bwd(qkv, tab, lse, dycat, s)
    drpb_rev = _rpb_grad(tt)
    d_pieces = [dq, dkv, da, dg]
    dh = _mm(d_pieces, w_in, mode="nt", m=sa, n=D, k=NIN, tm=tsa, tn=D, tk=NIN, out_dtype=F32, name="mm_in_dx", join=True)
    d_w_in = _mm(h_all, d_pieces, mode="tn", m=D, n=NIN, k=sa, tm=D, tn=NIN, tk=tsa, out_dtype=F32, name="mm_in_dw",
                 join=True)
    dsh1, dsc1, dg1, grad_x = _rmsmod_bwd(x, dh, vec["g_norm1"], sc1, name="rmsmod1_bwd", add=dx1)
    dcsh1, dcsc1, dg1c = _rmsmod_bwd(ctx, dh, vec["g_norm1"], csc1, name="rmsmod1_ctx_bwd", dh_row_off=s // ROW_TILE)

    small = dict(
        dmod=[dsh1, dsc1, dgt1, dsh2, dsc2, dgt2], dmod_c=[dcsh1, dcsc1],
        g_norm1=[dg1, dg1c], g_norm2=dg2, g_final=dgf, conv_b=dconv_b, ln_g=dln_g, ln_b=dln_b, conv_w=dconv_w,
        ffn_conv_w=[dfw_g, dfw_v], ffn_conv_b=[dfb_g, dfb_v], rpb_rev=drpb_rev,
    )
    return loss, grad_x, d_w_in, d_w_out, d_w_up, d_w_down, small


N_CHIPS = 4
HBM = pl.BlockSpec(memory_space=pl.ANY)
BIG = {"w_in": ("col", (D, NIN)), "w_out": ("row", (D, D)), "w_up": ("col", (D, 2 * DFF)), "w_down": ("row", (DFF, D))}
BIG_NAMES = tuple(BIG)
LATE_NAMES = ("w_out", "w_up", "w_down")


def _shard_shape(name):
    kind, (r, c) = BIG[name]
    return (r, c // N_CHIPS) if kind == "col" else (r // N_CHIPS, c)


def _half_rows(name):
    return _shard_shape(name)[0] // 2


def _place():
    x, y, c = lax.axis_index("x"), lax.axis_index("y"), lax.axis_index("c")
    others = [(1 - x, y), (x, 1 - y), (1 - x, 1 - y)]
    return x, y, c, 2 * x + y, (x, y, 1 - c), others


def _whole_region(ref, name, chip, half):
    kind, _ = BIG[name]
    r, c = _shard_shape(name)
    if kind == "col":
        return ref.at[pl.ds(half * (r // 2), r // 2), pl.ds(chip * c, c)]
    return ref.at[pl.ds(chip * r + half * (r // 2), r // 2), :]


def _remote(src, dst, send_sem, recv_sem, to):
    return pltpu.make_async_remote_copy(src_ref=src, dst_ref=dst, send_sem=send_sem, recv_sem=recv_sem,
                                        device_id=to, device_id_type=MESH)


def _gather_small(v, name):
    m_per, n = v.shape

    def body(x_ref, out_ref, send_sems, recv_sems, local_sem):
        x, y, c, _, sibling, others = _place()
        me = (x, y, c)

        def rows(px, py, pc):
            return out_ref.at[pl.ds((4 * px + 2 * py + pc) * m_per, m_per), :]

        def copy(k, block, to, src=None):
            return _remote(rows(*block) if src is None else src, rows(*block), send_sems.at[k], recv_sems.at[k], to)

        mine = pltpu.make_async_copy(x_ref, rows(*me), local_sem)
        mine.start()
        first = [copy(0, me, sibling, src=x_ref)]
        first += [copy(1 + j, me, (*chip, c), src=x_ref) for j, chip in enumerate(others)]
        for cp in first:
            cp.start()
        passed = [copy(4 + j, (*chip, c), sibling) for j, chip in enumerate(others)]
        for j, chip in enumerate(others):
            copy(1 + j, (*chip, c), me).wait_recv()
            passed[j].start()
        copy(0, sibling, me).wait_recv()
        for j, chip in enumerate(others):
            copy(4 + j, (*chip, 1 - c), me).wait_recv()
        for cp in first + passed:
            cp.wait_send()
        mine.wait()

    return pl.pallas_call(
        body, name=name, out_shape=_sds((8 * m_per, n), v.dtype),
        in_specs=[pl.BlockSpec(memory_space=pltpu.VMEM)], out_specs=pl.BlockSpec(memory_space=pltpu.VMEM),
        scratch_shapes=[pltpu.SemaphoreType.DMA((7,)), pltpu.SemaphoreType.DMA((7,)), pltpu.SemaphoreType.DMA],
    )(v)


def _cast_into_whole(name, shard, chip):
    kind, whole = BIG[name]
    r, c = shard.shape
    if kind == "col":
        tr = 256
        o_spec = pl.BlockSpec((tr, c), lambda i, ch: (i, ch[0]))
    else:
        tr = _tile(r, (128, 352))
        o_spec = pl.BlockSpec((tr, c), lambda i, ch: (ch[0] * (r // tr) + i, 0))

    def body(ch_ref, x_ref, o_ref):
        del ch_ref
        o_ref[...] = x_ref[...].astype(o_ref.dtype)

    return _pallas(body, name="cast_" + name, prefetch=1, grid=(r // tr,),
                   in_specs=[pl.BlockSpec((tr, c), lambda i, ch: (i, 0))], out_specs=o_spec,
                   out_shape=_sds(whole, MXU_DTYPE), semantics=("parallel",))(chip, shard)


SEM = pl.BlockSpec(memory_space=pltpu.SEMAPHORE)
IN_HBM = pl.BlockSpec(memory_space=pltpu.HBM)
DATAFLOW = pltpu.SideEffectType.DATAFLOW_SIDE_EFFECTING


def _keep_in_hbm(a):
    return pltpu.with_memory_space_constraint(a, pltpu.HBM)


def _gather_start(wholes, names, after, tag):
    nw = len(names)
    ns = 2 * 3 * nw

    def body(*refs):
        ins = refs[:nw]
        sems = refs[nw + 1:nw + 1 + ns]
        token = refs[2 * nw + ns + 1]
        _, _, c, chip, _, others = _place()
        for w, name in enumerate(names):
            mine = _whole_region(ins[w], name, chip, c)
            for t, (ox, oy) in enumerate(others):
                k = 2 * (3 * w + t)
                _remote(mine, mine, sems[k], sems[k + 1], (ox, oy, c)).start()
        token[...] = jnp.zeros_like(token)

    res = pl.pallas_call(
        body, name="gather_" + tag + "_start",
        out_shape=(*[pltpu.SemaphoreType.DMA(())] * ns, *[pltpu.HBM(a.shape, a.dtype) for a in wholes], _sds((8, LANES), F32)),
        in_specs=[IN_HBM] * nw + [pl.BlockSpec(memory_space=pl.ANY)],
        out_specs=(*[SEM] * ns, *[IN_HBM] * nw, pl.BlockSpec(memory_space=pltpu.VMEM)),
        input_output_aliases={i: ns + i for i in range(nw)},
        compiler_params=pltpu.CompilerParams(has_side_effects=DATAFLOW),
    )(*[_keep_in_hbm(a) for a in wholes], after)
    return list(res[:ns]), list(res[ns:ns + nw]), res[ns + nw]


def _gather_wait(sems, wholes, names, after, tag):
    nw = len(names)
    ns = len(sems)

    def body(*refs):
        ins = refs[:nw]
        sem_refs = refs[nw:nw + ns]
        _, _, c, chip, _, others = _place()
        for w, name in enumerate(names):
            mine = _whole_region(ins[w], name, chip, c)
            for t, (ox, oy) in enumerate(others):
                got = _whole_region(ins[w], name, 2 * ox + oy, c)
                k = 2 * (3 * w + t)
                cp = _remote(mine, got, sem_refs[k], sem_refs[k + 1], (ox, oy, c))
                cp.wait_send()
                cp.wait_recv()

    return pl.pallas_call(
        body, name="gather_" + tag + "_wait",
        out_shape=tuple(pltpu.HBM(a.shape, a.dtype) for a in wholes),
        in_specs=[IN_HBM] * nw + [SEM] * ns + [pl.BlockSpec(memory_space=pl.ANY)], out_specs=tuple([IN_HBM] * nw),
        input_output_aliases={i: i for i in range(nw)},
        compiler_params=pltpu.CompilerParams(has_side_effects=DATAFLOW),
    )(*wholes, *sems, after)


def _forward_halves(wholes, names, tag):
    nw = len(names)

    def body(*refs):
        outs = refs[nw:2 * nw]
        send_sems, recv_sems = refs[2 * nw:]
        _, _, c, _, sibling, others = _place()
        sends = []
        for w, name in enumerate(names):
            for t, (ox, oy) in enumerate(others):
                got = _whole_region(outs[w], name, 2 * ox + oy, c)
                cp = _remote(got, got, send_sems.at[w, t], recv_sems.at[w, t], sibling)
                cp.start()
                sends.append(cp)
        for w, name in enumerate(names):
            for t, (ox, oy) in enumerate(others):
                got = _whole_region(outs[w], name, 2 * ox + oy, 1 - c)
                _remote(got, got, send_sems.at[w, t], recv_sems.at[w, t], sibling).wait_recv()
        for cp in sends:
            cp.wait_send()

    return pl.pallas_call(
        body, name="gather_" + tag + "_forward",
        out_shape=[_sds(a.shape, a.dtype) for a in wholes],
        in_specs=[HBM] * nw, out_specs=[HBM] * nw,
        input_output_aliases={i: i for i in range(nw)},
        scratch_shapes=[pltpu.SemaphoreType.DMA((nw, 3)), pltpu.SemaphoreType.DMA((nw, 3))],
    )(*wholes)


def _compact_shape(name, dtype):
    kind, (r, c) = BIG[name]
    return _sds((r // 2, c), dtype)


def _swap_pairs(ins, outs, names, c):
    pairs = []
    for w, name in enumerate(names):
        kind, _ = BIG[name]
        half = _half_rows(name)
        if kind == "col":
            pairs.append((ins[w].at[pl.ds((1 - c) * half, half), :], outs[w]))
        else:
            pairs += [(ins[w].at[pl.ds(jj * 2 * half + (1 - c) * half, half), :], outs[w].at[pl.ds(jj * half, half), :])
                      for jj in range(N_CHIPS)]
    return pairs


def _n_swap_copies(names):
    return sum(1 if BIG[n][0] == "col" else N_CHIPS for n in names)


def _swap_start(grads, names, label):
    nw = len(names)
    ns = 2 * _n_swap_copies(names)

    def body(*refs):
        ins, lands = refs[:nw], refs[nw:2 * nw]
        sems = refs[2 * nw:2 * nw + ns]
        token = refs[4 * nw + ns]
        _, _, c, _, sibling, _ = _place()
        for k, (src, dst) in enumerate(_swap_pairs(ins, lands, names, c)):
            _remote(src, dst, sems[2 * k], sems[2 * k + 1], sibling).start()
        token[...] = jnp.zeros_like(token)

    lands = [_keep_in_hbm(lax.empty(_compact_shape(n, F32).shape, F32)) for n in names]
    res = pl.pallas_call(
        body, name=label,
        out_shape=(*[pltpu.SemaphoreType.DMA(())] * ns, *[pltpu.HBM(a.shape, a.dtype) for a in grads],
                   *[pltpu.HBM(a.shape, a.dtype) for a in lands], _sds((8, LANES), F32)),
        in_specs=[IN_HBM] * (2 * nw),
        out_specs=(*[SEM] * ns, *[IN_HBM] * (2 * nw), pl.BlockSpec(memory_space=pltpu.VMEM)),
        input_output_aliases={i: ns + i for i in range(2 * nw)},
        compiler_params=pltpu.CompilerParams(has_side_effects=DATAFLOW),
    )(*[_keep_in_hbm(a) for a in grads], *lands)
    return list(res[:ns]), list(res[ns:ns + nw]), list(res[ns + nw:ns + 2 * nw]), res[ns + 2 * nw]


def _swap_wait(sems, grads, lands, names, after, label):
    nw = len(names)
    ns = len(sems)

    def body(*refs):
        ins, land_refs = refs[:nw], refs[nw:2 * nw]
        sem_refs = refs[2 * nw:2 * nw + ns]
        _, _, c, _, sibling, _ = _place()
        for k, (src, dst) in enumerate(_swap_pairs(ins, land_refs, names, c)):
            cp = _remote(src, dst, sem_refs[2 * k], sem_refs[2 * k + 1], sibling)
            cp.wait_send()
            cp.wait_recv()

    res = pl.pallas_call(
        body, name=label,
        out_shape=tuple(pltpu.HBM(a.shape, a.dtype) for a in (*grads, *lands)),
        in_specs=[IN_HBM] * (2 * nw) + [SEM] * ns + [pl.BlockSpec(memory_space=pl.ANY)],
        out_specs=tuple([IN_HBM] * (2 * nw)),
        input_output_aliases={i: i for i in range(2 * nw)},
        compiler_params=pltpu.CompilerParams(has_side_effects=DATAFLOW),
    )(*grads, *lands, *sems, after)
    return list(res[:nw]), list(res[nw:])


def _add_halves(name, grad, got, core):
    kind, (r, c) = BIG[name]
    half = _half_rows(name)
    if kind == "col":
        t = 128
        grid = (half // t,)
        g_spec = pl.BlockSpec((t, c), lambda i, cr: (cr[0] * (half // t) + i, 0))
        o_spec = pl.BlockSpec((t, c), lambda i, cr: (i, 0))
    else:
        t = half
        grid = (N_CHIPS,)
        g_spec = pl.BlockSpec((t, c), lambda i, cr: (2 * i + cr[0], 0))
        o_spec = pl.BlockSpec((t, c), lambda i, cr: (i, 0))

    def body(c_ref, g_ref, b_ref, o_ref):
        del c_ref
        o_ref[...] = (g_ref[...] + b_ref[...]).astype(o_ref.dtype)

    return pl.pallas_call(
        body, name="grad_add_" + name,
        grid_spec=pltpu.PrefetchScalarGridSpec(num_scalar_prefetch=1, grid=grid, in_specs=[g_spec, o_spec], out_specs=o_spec),
        out_shape=_compact_shape(name, BF16),
        compiler_params=pltpu.CompilerParams(dimension_semantics=("parallel",), vmem_limit_bytes=VMEM_LIMIT),
    )(core, grad, got)


def _piece(ref, name, chip):
    kind, _ = BIG[name]
    r, c = _shard_shape(name)
    if kind == "col":
        return ref.at[:, pl.ds(chip * c, c)]
    return ref.at[pl.ds(chip * (r // 2), r // 2), :]


def _landing_shape(name):
    r, c = _shard_shape(name)
    return (N_CHIPS - 1, r // 2, c)


def _exchange_start(parts, names, label):
    nw = len(names)
    ns = 2 * 3 * nw

    def body(*refs):
        ins, lands = refs[:nw], refs[nw:2 * nw]
        sems = refs[2 * nw:2 * nw + ns]
        token = refs[4 * nw + ns]
        _, _, c, _, _, others = _place()
        for w, name in enumerate(names):
            for t, (ox, oy) in enumerate(others):
                k = 2 * (3 * w + t)
                _remote(_piece(ins[w], name, 2 * ox + oy), lands[w].at[t], sems[k], sems[k + 1], (ox, oy, c)).start()
        token[...] = jnp.zeros_like(token)

    lands = [_keep_in_hbm(lax.empty(_landing_shape(n), BF16)) for n in names]
    res = pl.pallas_call(
        body, name=label,
        out_shape=(*[pltpu.SemaphoreType.DMA(())] * ns, *[pltpu.HBM(a.shape, a.dtype) for a in parts],
                   *[pltpu.HBM(a.shape, a.dtype) for a in lands], _sds((8, LANES), F32)),
        in_specs=[IN_HBM] * (2 * nw),
        out_specs=(*[SEM] * ns, *[IN_HBM] * (2 * nw), pl.BlockSpec(memory_space=pltpu.VMEM)),
        input_output_aliases={i: ns + i for i in range(2 * nw)},
        compiler_params=pltpu.CompilerParams(has_side_effects=DATAFLOW),
    )(*[_keep_in_hbm(a) for a in parts], *lands)
    return list(res[:ns]), list(res[ns:ns + nw]), list(res[ns + nw:ns + 2 * nw]), res[ns + 2 * nw]


def _exchange_wait(sems, parts, lands, names, after, label):
    nw = len(names)
    ns = len(sems)

    def body(*refs):
        ins, land_refs = refs[:nw], refs[nw:2 * nw]
        sem_refs = refs[2 * nw:2 * nw + ns]
        _, _, c, _, _, others = _place()
        for w, name in enumerate(names):
            for t, (ox, oy) in enumerate(others):
                k = 2 * (3 * w + t)
                cp = _remote(_piece(ins[w], name, 2 * ox + oy), land_refs[w].at[t], sem_refs[k], sem_refs[k + 1], (ox, oy, c))
                cp.wait_send()
                cp.wait_recv()

    res = pl.pallas_call(
        body, name=label,
        out_shape=tuple(pltpu.HBM(a.shape, a.dtype) for a in (*parts, *lands)),
        in_specs=[IN_HBM] * (2 * nw) + [SEM] * ns + [pl.BlockSpec(memory_space=pl.ANY)],
        out_specs=tuple([IN_HBM] * (2 * nw)),
        input_output_aliases={i: i for i in range(2 * nw)},
        compiler_params=pltpu.CompilerParams(has_side_effects=DATAFLOW),
    )(*parts, *lands, *sems, after)
    return list(res[:nw]), list(res[nw:])


def _sum_chips(name, part, got, chip):
    kind, _ = BIG[name]
    _, r, c = got.shape
    t = _tile(r, (128, 352))
    if kind == "col":
        own = pl.BlockSpec((t, c), lambda i, ch: (i, ch[0]))
    else:
        own = pl.BlockSpec((t, c), lambda i, ch: (ch[0] * (r // t) + i, 0))

    def body(ch_ref, p_ref, g_ref, o_ref):
        del ch_ref
        acc = p_ref[...].astype(F32)
        for j in range(N_CHIPS - 1):
            acc = acc + g_ref[j].astype(F32)
        o_ref[...] = acc

    return _pallas(
        body, name="grad_sum_" + name, prefetch=1, grid=(r // t,),
        in_specs=[own, pl.BlockSpec((N_CHIPS - 1, t, c), lambda i, ch: (0, i, 0))],
        out_specs=pl.BlockSpec((t, c), lambda i, ch: (i, 0)),
        out_shape=_sds((r, c), F32), semantics=("parallel",),
    )(chip, part, got)


def _send_halves(sums, label, after):
    nw = len(sums)

    def body(*refs):
        ins, outs = refs[:nw], refs[nw + 1:2 * nw + 1]
        send_sems, recv_sems = refs[2 * nw + 1:]
        _, _, _, _, sibling, _ = _place()
        copies = [_remote(ins[w], outs[w], send_sems.at[w], recv_sems.at[w], sibling) for w in range(nw)]
        for cp in copies:
            cp.start()
        for cp in copies:
            cp.wait()

    return pl.pallas_call(
        body, name=label,
        out_shape=[_sds(a.shape, a.dtype) for a in sums],
        in_specs=[HBM] * (nw + 1), out_specs=[HBM] * nw,
        scratch_shapes=[pltpu.SemaphoreType.DMA((nw,)), pltpu.SemaphoreType.DMA((nw,))],
    )(*sums, after)


EARLY_GRADS = ("w_up", "w_down")
LAST_GRADS = ("w_in", "w_out")


def _reduce_finish(started, names, after, chip, tag):
    sems, parts, lands, _ = started
    parts, lands = _exchange_wait(sems, parts, lands, names, after, "grad_exchange_wait_" + tag)
    return [_sum_chips(n, parts[i], lands[i], chip) for i, n in enumerate(names)]


HI = lax.Precision.HIGHEST
MOD_COLS = 6 * D // N_CHIPS
COND_ROWS = 16


def _silu(v):
    return v * _sigmoid(v)


GATHER_ROWS = 48
FFW_COLS = 2 * DFF // N_CHIPS
CONV_COLS = DC // N_CHIPS


def _pack_cond(c, ffn_w, conv_w):
    def body(c_ref, f_ref, w_ref, o_ref):
        o_ref[...] = jnp.zeros_like(o_ref)
        o_ref[0:1, 0:D] = c_ref[...]
        o_ref[8:11, :] = f_ref[...]
        o_ref[16:16 + CW, 0:CONV_COLS] = w_ref[...]

    return _pallas(body, name="pack_cond", out_shape=_sds((GATHER_ROWS, FFW_COLS), F32))(c, ffn_w, conv_w)


def _unpack_cond(got, c_ctx):
    def body(g_ref, c_ref, cond_ref, f_ref, w_ref):
        cond_ref[...] = jnp.zeros_like(cond_ref)
        for d in range(8):
            cond_ref[d:d + 1, :] = g_ref[d * GATHER_ROWS:d * GATHER_ROWS + 1, 0:D]
        cond_ref[8:9, :] = c_ref[...]
        for j in range(N_CHIPS):
            r0 = 2 * j * GATHER_ROWS
            f_ref[:, j * FFW_COLS:(j + 1) * FFW_COLS] = g_ref[r0 + 8:r0 + 11, :]
            w_ref[:, j * CONV_COLS:(j + 1) * CONV_COLS] = g_ref[r0 + 16:r0 + 16 + CW, 0:CONV_COLS]

    return _pallas(body, name="unpack_cond",
                   out_shape=[_sds((COND_ROWS, D), F32), _sds((3, 2 * DFF), F32), _sds((CW, DC), F32)])(got, c_ctx)


def _chip_cols(rows, width):
    return pl.BlockSpec((rows, width), lambda i, ch: (0, ch[0]))


def _whole(shape):
    return pl.BlockSpec(shape, lambda i, ch: (0,) * len(shape))


def _mod_shard(cond, w_mod, b_mod, chip):
    def body(ch_ref, c_ref, w_ref, b_ref, o_ref):
        del ch_ref
        o_ref[...] = jnp.dot(_silu(c_ref[...]), w_ref[...], preferred_element_type=F32, precision=HI) + b_ref[...]

    return _pallas(body, name="mod_fwd", prefetch=1, grid=(1,),
                   in_specs=[_whole((COND_ROWS, D)), _whole((D, MOD_COLS)), _chip_cols(1, MOD_COLS)],
                   out_specs=_whole((COND_ROWS, MOD_COLS)),
                   out_shape=_sds((COND_ROWS, MOD_COLS), F32))(chip, cond, w_mod, b_mod)


def _unpack_mod(mods, dev):
    def body(dev_ref, m_ref, me_ref, c_ref):
        rowi = lax.broadcasted_iota(jnp.int32, (COND_ROWS, MOD_COLS), 0)
        mine, ctx = [], []
        for j in range(N_CHIPS):
            blk = m_ref[2 * j * COND_ROWS:(2 * j + 1) * COND_ROWS, :]
            mine.append(jnp.sum(jnp.where(rowi == dev_ref[0], blk, 0.0), axis=0, keepdims=True))
            ctx.append(blk[8:9, :])
        mine = jnp.concatenate(mine, axis=1)
        ctx = jnp.concatenate(ctx, axis=1)
        for k in range(6):
            me_ref[k:k + 1, :] = mine[:, k * D:(k + 1) * D]
        for k in range(2):
            c_ref[k:k + 1, :] = ctx[:, k * D:(k + 1) * D]

    return _pallas(body, name="unpack_mod", prefetch=1, grid=(1,),
                   in_specs=[_whole(mods.shape)], out_specs=[_whole((6, D)), _whole((2, D))],
                   out_shape=[_sds((6, D), F32), _sds((2, D), F32)])(dev, mods)


def _mod_weight_grad(cond, dmod_all, chip):
    def body(ch_ref, c_ref, d_ref, o_ref):
        del ch_ref
        o_ref[...] = lax.dot_general(_silu(c_ref[...]), d_ref[...], _TN, preferred_element_type=F32, precision=HI)

    return _pallas(body, name="mod_weight_grad", prefetch=1, grid=(1,),
                   in_specs=[_whole((COND_ROWS, D)), _chip_cols(COND_ROWS, MOD_COLS)], out_specs=_whole((D, MOD_COLS)),
                   out_shape=_sds((D, MOD_COLS), F32))(chip, cond, dmod_all)


def _cond_grad_partial(dmod_all, w_mod, chip):
    def body(ch_ref, d_ref, w_ref, o_ref):
        del ch_ref
        o_ref[...] = lax.dot_general(d_ref[...], w_ref[...], (((1,), (1,)), ((), ())), preferred_element_type=F32, precision=HI)

    return _pallas(body, name="cond_grad_partial", prefetch=1, grid=(1,),
                   in_specs=[pl.BlockSpec((8, MOD_COLS), lambda i, ch: (1, ch[0])), _whole((D, MOD_COLS))],
                   out_specs=_whole((8, D)), out_shape=_sds((8, D), F32))(chip, dmod_all, w_mod)


def _adam_math(w, g, m, v):
    nm = ADAM_B1 * m + (1.0 - ADAM_B1) * g
    nv = ADAM_B2 * v + (1.0 - ADAM_B2) * (g * g)
    c1 = 1.0 - ADAM_B1 ** ADAM_STEP
    c2 = 1.0 - ADAM_B2 ** ADAM_STEP
    return -ADAM_LR * ((nm / c1) / (jnp.sqrt(nv / c2) + ADAM_EPS) + ADAM_WD * w), nm, nv


def _cond_update(parts, c_ctx, m, v):
    def body(p_ref, c_ref, m_ref, v_ref, g_ref, d_ref, nm_ref, nv_ref):
        tot = p_ref[0:1, :]
        for j in range(1, N_CHIPS):
            tot = tot + p_ref[16 * j:16 * j + 1, :]
        cv = c_ref[...]
        sg = _sigmoid(cv)
        g = tot * (sg * (1.0 + cv * (1.0 - sg)))
        g_ref[...] = g
        d_ref[...], nm_ref[...], nv_ref[...] = _adam_math(cv, g, m_ref[...], v_ref[...])

    return _pallas(body, name="cond_update", out_shape=[_sds((1, D), F32)] * 4)(parts, c_ctx, m, v)


def _adamw(w, g, m, v, name):
    r, c = w.shape
    t = _tile(r, (128,)) if r % 128 == 0 and r > 128 else r

    def body(w_ref, g_ref, m_ref, v_ref, d_ref, nm_ref, nv_ref):
        d_ref[...], nm_ref[...], nv_ref[...] = _adam_math(w_ref[...], g_ref[...], m_ref[...], v_ref[...])

    blk = pl.BlockSpec((t, c), lambda i: (i, 0))
    return _pallas(body, name=name, grid=(r // t,), in_specs=[blk] * 4, out_specs=[blk] * 3,
                   out_shape=[_sds((r, c), F32)] * 3, semantics=("parallel",))(w, g, m, v)


def _adamw_cols(w, g_all, m, v, chip, name):
    r, c = w.shape

    def body(ch_ref, w_ref, g_ref, m_ref, v_ref, go_ref, d_ref, nm_ref, nv_ref):
        del ch_ref
        g = g_ref[...]
        go_ref[...] = g
        d_ref[...], nm_ref[...], nv_ref[...] = _adam_math(w_ref[...], g, m_ref[...], v_ref[...])

    return _pallas(body, name=name, prefetch=1, grid=(1,),
                   in_specs=[_whole((r, c)), _chip_cols(r, c), _whole((r, c)), _whole((r, c))],
                   out_specs=[_whole((r, c))] * 4, out_shape=[_sds((r, c), F32)] * 4)(chip, w, g_all, m, v)


def _adamw_halves(name, w, own, other, m, v, core, after):
    r, c = w.shape
    half = r // 2
    t = _tile(half, (128, 352))
    nh = half // t

    def pick(mine):
        def index(i, cr):
            first = cr[0] if mine else 1 - cr[0]
            return (jnp.clip(i - first * nh, 0, nh - 1), 0)
        return pl.BlockSpec((t, c), index)

    def body(c_ref, w_ref, own_ref, oth_ref, m_ref, v_ref, after_ref, g_ref, d_ref, nm_ref, nv_ref):
        del after_ref
        g = jnp.where(pl.program_id(0) // nh == c_ref[0], own_ref[...], oth_ref[...])
        g_ref[...] = g
        d_ref[...], nm_ref[...], nv_ref[...] = _adam_math(w_ref[...], g, m_ref[...], v_ref[...])

    blk = pl.BlockSpec((t, c), lambda i, cr: (i, 0))
    return _pallas(body, name="adamw_" + name, prefetch=1, grid=(2 * nh,),
                   in_specs=[blk, pick(True), pick(False), blk, blk, pl.BlockSpec(memory_space=pl.ANY)], out_specs=[blk] * 4,
                   out_shape=[_sds((r, c), F32)] * 4, semantics=("parallel",))(core, w, own, other, m, v, after)


WEIGHTS = ("c_ctx", "w_mod", "b_mod", "g_norm1", "w_in", "rpb", "conv_w", "conv_b", "ln_g", "ln_b", "w_out", "g_norm2",
           "w_up", "ffn_conv_w", "ffn_conv_b", "w_down", "g_final")
PACK = (("dmod", 6 * D), ("dmod_c", 2 * D), ("g_norm1", D), ("g_norm1_ctx", D), ("g_norm2", D), ("g_final", D),
        ("conv_b", DC), ("ln_g", DC), ("ln_b", DC), ("ffn_conv_b", 2 * DFF), ("ffn_conv_w", 3 * 2 * DFF),
        ("conv_w", CW * DC), ("rpb_rev", NH * 16 * LANES), ("loss", LANES))
PACK_OFF = {}
_o = 0
for _n, _w in PACK:
    PACK_OFF[_n] = (_o, _w)
    _o += _w
PACK_N = -(-_o // (8 * LANES)) * (8 * LANES)
VECTORS = {"b_mod": (6 * D, ("dmod", "dmod_c")), "g_norm1": (D, ("g_norm1", "g_norm1_ctx")), "conv_b": (DC, ("conv_b",)),
           "ln_g": (DC, ("ln_g",)), "ln_b": (DC, ("ln_b",)), "g_norm2": (D, ("g_norm2",)),
           "ffn_conv_b": (2 * DFF, ("ffn_conv_b",)), "g_final": (D, ("g_final",))}
RPB_ROWS = NH * (2 * NA_ROWS - 1)
RPB_COLS = 4 * NA_ROWS - 1


def _pack_small(parts, after):
    arrs, places = [], []
    for name, _ in PACK:
        off, width = PACK_OFF[name]
        group = parts[name]
        rows = group[0].shape[0]
        row_w = sum(a.shape[1] for a in group)
        assert rows * row_w == width, (name, rows, row_w, width)
        col = 0
        for a in group:
            arrs.append(a)
            places.append([off + k * row_w + col for k in range(rows)])
            col += a.shape[1]

    def body(*refs):
        o_ref = refs[-1]
        o_ref[:, _o:PACK_N] = jnp.zeros((1, PACK_N - _o), F32)
        for ref, offs in zip(refs, places):
            n = ref.shape[1]
            for k, off in enumerate(offs):
                o_ref[:, off:off + n] = ref[k:k + 1, :]

    vmem = pl.BlockSpec(memory_space=pltpu.VMEM)
    return _pallas(body, name="pack_small_grads", out_shape=_sds((1, PACK_N), F32),
                   in_specs=[vmem] * len(arrs) + [pl.BlockSpec(memory_space=pl.ANY)], out_specs=vmem)(*arrs, after)


def _small_update(packs, w, m, v):
    names = list(VECTORS)

    def body(*refs):
        it = iter(refs)
        p_ref = next(it)
        wmv = {n: (next(it), next(it), next(it)) for n in names}
        outs = {n: (next(it), next(it), next(it), next(it)) for n in names}
        dmod_ref, cw_ref, fw_ref, rpb_ref, loss_ref = next(it), next(it), next(it), next(it), next(it)

        def total(name):
            off, width = PACK_OFF[name]
            acc = p_ref[0:1, off:off + width]
            for d in range(1, 8):
                acc = acc + p_ref[d:d + 1, off:off + width]
            return acc

        for n in names:
            width, segs = VECTORS[n]
            g = total(segs[0])
            if len(segs) > 1:
                extra = total(segs[1])
                ew = extra.shape[1]
                g = g + extra if ew == width else jnp.concatenate([g[:, :ew] + extra, g[:, ew:]], axis=1)
            w_ref, m_ref, v_ref = wmv[n]
            g_ref, d_ref, nm_ref, nv_ref = outs[n]
            g_ref[...] = g
            d_ref[...], nm_ref[...], nv_ref[...] = _adam_math(w_ref[...], g, m_ref[...], v_ref[...])

        o_dmod = PACK_OFF["dmod"][0]
        dmod_ref[...] = jnp.zeros_like(dmod_ref)
        dmod_ref[0:8, :] = p_ref[:, o_dmod:o_dmod + 6 * D]
        dmod_ref[8:9, 0:2 * D] = total("dmod_c")
        for ref, name, rows in ((cw_ref, "conv_w", CW), (fw_ref, "ffn_conv_w", 3), (rpb_ref, "rpb_rev", NH * 16)):
            flat = total(name)
            n = ref.shape[1]
            for k in range(rows):
                ref[k:k + 1, :] = flat[:, k * n:(k + 1) * n]
        loss_ref[...] = total("loss")

    ins = [packs] + [a[n] for n in names for a in (w, m, v)]
    out_shape = [_sds((1, VECTORS[n][0]), F32) for n in names for _ in range(4)]
    out_shape += [_sds((COND_ROWS, 6 * D), F32), _sds((CW, DC), F32), _sds((3, 2 * DFF), F32), _sds((NH * 16, LANES), F32),
                  _sds((1, LANES), F32)]
    res = _pallas(body, name="small_update", out_shape=out_shape)(*ins)
    per = {n: tuple(res[4 * i:4 * i + 4]) for i, n in enumerate(names)}
    return (per, *res[4 * len(names):])


def _rpb_update(rev, w, m, v):
    def body(r_ref, w_ref, m_ref, v_ref, g_ref, d_ref, nm_ref, nv_ref):
        li = lax.broadcasted_iota(jnp.int32, (LANES, LANES), 0)
        co = lax.broadcasted_iota(jnp.int32, (LANES, LANES), 1)
        lane_of_co0 = GW - 1 + RPB_COLS // 2
        unflip = jnp.where((li == lane_of_co0 - co) & (co < RPB_COLS), 1.0, 0.0).astype(F32)
        g_all = jnp.dot(r_ref[...], unflip, preferred_element_type=F32, precision=HI)
        nr = 2 * NA_ROWS - 1
        for h in range(NH):
            rows = slice(h * nr, (h + 1) * nr)
            g = g_all[h * 16:h * 16 + nr, 0:RPB_COLS]
            g_ref[rows, :] = g
            d_ref[rows, :], nm_ref[rows, :], nv_ref[rows, :] = _adam_math(w_ref[rows, :], g, m_ref[rows, :], v_ref[rows, :])

    return _pallas(body, name="rpb_update", out_shape=[_sds((RPB_ROWS, RPB_COLS), F32)] * 4)(rev, w, m, v)


def kernel(x, c, ctx, c_ctx, w_mod, b_mod, g_norm1, w_in, rpb, conv_w, conv_b, ln_g, ln_b, w_out, g_norm2, w_up, ffn_conv_w, ffn_conv_b, w_down, g_final, loss_target, m_c_ctx, m_w_mod, m_b_mod, m_g_norm1, m_w_in, m_rpb, m_conv_w, m_conv_b, m_ln_g, m_ln_b, m_w_out, m_g_norm2, m_w_up, m_ffn_conv_w, m_ffn_conv_b, m_w_down, m_g_final, v_c_ctx, v_w_mod, v_b_mod, v_g_norm1, v_w_in, v_rpb, v_conv_w, v_conv_b, v_ln_g, v_ln_b, v_w_out, v_g_norm2, v_w_up, v_ffn_conv_w, v_ffn_conv_b, v_w_down, v_g_final):
    w = dict(c_ctx=c_ctx, w_mod=w_mod, b_mod=b_mod, g_norm1=g_norm1, w_in=w_in, rpb=rpb, conv_w=conv_w, conv_b=conv_b,
             ln_g=ln_g, ln_b=ln_b, w_out=w_out, g_norm2=g_norm2, w_up=w_up, ffn_conv_w=ffn_conv_w, ffn_conv_b=ffn_conv_b,
             w_down=w_down, g_final=g_final)
    mom = dict(c_ctx=m_c_ctx, w_mod=m_w_mod, b_mod=m_b_mod, g_norm1=m_g_norm1, w_in=m_w_in, rpb=m_rpb, conv_w=m_conv_w,
               conv_b=m_conv_b, ln_g=m_ln_g, ln_b=m_ln_b, w_out=m_w_out, g_norm2=m_g_norm2, w_up=m_w_up,
               ffn_conv_w=m_ffn_conv_w, ffn_conv_b=m_ffn_conv_b, w_down=m_w_down, g_final=m_g_final)
    var = dict(c_ctx=v_c_ctx, w_mod=v_w_mod, b_mod=v_b_mod, g_norm1=v_g_norm1, w_in=v_w_in, rpb=v_rpb, conv_w=v_conv_w,
               conv_b=v_conv_b, ln_g=v_ln_g, ln_b=v_ln_b, w_out=v_w_out, g_norm2=v_g_norm2, w_up=v_w_up,
               ffn_conv_w=v_ffn_conv_w, ffn_conv_b=v_ffn_conv_b, w_down=v_w_down, g_final=v_g_final)
    xi, yi, ci = lax.axis_index("x"), lax.axis_index("y"), lax.axis_index("c")
    dev = (4 * xi + 2 * yi + ci).astype(jnp.int32).reshape(1)
    chip = (2 * xi + yi).astype(jnp.int32).reshape(1)
    core = ci.astype(jnp.int32).reshape(1)
    c_ctx2 = c_ctx.reshape(1, D)
    g_final2 = g_final.reshape(1, D)
    mom["g_final"], var["g_final"] = m_g_final.reshape(1, D), v_g_final.reshape(1, D)

    got = _gather_small(_pack_cond(c, ffn_conv_w[0], conv_w[0]), "gather_cond")
    cond, ffn_w_all, conv_w_all = _unpack_cond(got, c_ctx2)

    mods = _gather_small(_mod_shard(cond, w_mod[0], b_mod, chip), "gather_mod")
    mod_me, mod_c = _unpack_mod(mods, dev)

    shards = {n: _cast_into_whole(n, w[n][0], chip) for n in BIG_NAMES}
    sems_in, first, token_in = _gather_start([shards["w_in"]], ("w_in",), mod_me, "w_in")
    sems, late, token = _gather_start([shards[n] for n in LATE_NAMES], LATE_NAMES, token_in, "late")
    mod_me = mod_me + token[0:1, 0:1]

    def w_in_all(after):
        arrived = _gather_wait(sems_in, first, ("w_in",), after, "w_in")
        return _forward_halves(list(arrived), ("w_in",), "w_in")[0]

    def late_weights(after):
        arrived = _gather_wait(sems, late, LATE_NAMES, after, "late")
        return _forward_halves(list(arrived), LATE_NAMES, "late")

    rpb_rev = jnp.pad(rpb[0][:, :, ::-1], ((0, 0), (0, 1), (48, LANES - 48 - RPB_COLS))).reshape(NH * 16, LANES)
    vec = dict(g_norm1=g_norm1, g_norm2=g_norm2, g_final=g_final2, conv_w=conv_w_all, conv_b=conv_b, ln_g=ln_g, ln_b=ln_b,
               ffn_conv_w=ffn_w_all, ffn_conv_b=ffn_conv_b)
    started = []

    def begin_early(d_up, d_down):
        started.append(_swap_start([d_up, d_down], EARLY_GRADS, "grad_swap_start_early"))

    def carry_on_early(after):
        sems_, grads_, lands_, _ = started.pop()
        grads_, lands_ = _swap_wait(sems_, grads_, lands_, EARLY_GRADS, after, "grad_swap_wait_early")
        parts_ = [_add_halves(n, grads_[i], lands_[i], core) for i, n in enumerate(EARLY_GRADS)]
        started.append(_exchange_start(parts_, EARLY_GRADS, "grad_exchange_start_early"))
        return started[0][3][0:1, 0:1]

    loss_p, grad_x, d_in, d_out, d_up, d_down, small = _local_step(
        x[0], ctx[0], loss_target[0], mod_me, mod_c, vec, w_in_all, late_weights, rpb_rev, (begin_early, carry_on_early))

    out = {}
    sems_, grads_, lands_, _ = _swap_start([d_in, d_out], LAST_GRADS, "grad_swap_start_last")
    early_own = _reduce_finish(started[0], EARLY_GRADS, grad_x, chip, "early")
    behind_swap = small["rpb_rev"][0:1, 0:1] + small["g_norm1"][1][0:1, 0:1] + early_own[0][0:1, 0:1]
    grads_, lands_ = _swap_wait(sems_, grads_, lands_, LAST_GRADS, behind_swap, "grad_swap_wait_last")
    parts_ = [_add_halves(n, grads_[i], lands_[i], core) for i, n in enumerate(LAST_GRADS)]
    last_started = _exchange_start(parts_, LAST_GRADS, "grad_exchange_start_last")
    early_other = _send_halves(early_own, "grad_send_early", after=last_started[3])
    for i, n in enumerate(EARLY_GRADS):
        out[n] = _adamw_halves(n, w[n][0], early_own[i], early_other[i], mom[n][0], var[n][0], core, early_other[i])
    behind_early = out[EARLY_GRADS[0]][1][0:1, 0:1] + out[EARLY_GRADS[1]][1][0:1, 0:1]

    parts = dict(dmod=small["dmod"], dmod_c=small["dmod_c"], g_norm1=[small["g_norm1"][0]], g_norm1_ctx=[small["g_norm1"][1]],
                 g_norm2=[small["g_norm2"]], g_final=[small["g_final"]], conv_b=[small["conv_b"]], ln_g=[small["ln_g"]],
                 ln_b=[small["ln_b"]], ffn_conv_b=small["ffn_conv_b"], ffn_conv_w=small["ffn_conv_w"],
                 conv_w=[small["conv_w"]], rpb_rev=[small["rpb_rev"]], loss=[loss_p])
    pack = _pack_small(parts, after=behind_early).reshape(8, PACK_N // 8)
    packs = _gather_small(pack, "gather_small_grads").reshape(8, PACK_N)
    w2 = dict(w, g_final=g_final2)
    per, dmod_all, g_conv_w_all, g_ffn_w_all, g_rpb_rev, loss_row = _small_update(packs, w2, mom, var)

    out.update(per)
    out["c_ctx"] = _cond_update(
        _gather_small(_cond_grad_partial(dmod_all, w_mod[0], chip), "gather_cond_grad"),
        c_ctx2, m_c_ctx.reshape(1, D), v_c_ctx.reshape(1, D))
    g_w_mod = _mod_weight_grad(cond, dmod_all, chip)
    out["w_mod"] = (g_w_mod, *_adamw(w_mod[0], g_w_mod, m_w_mod[0], v_w_mod[0], "adamw_w_mod"))
    behind = out["w_mod"][1][0:1, 0:1] + out["c_ctx"][1][0:1, 0:1]
    last_own = _reduce_finish(last_started, LAST_GRADS, behind, chip, "last")
    last_other = _send_halves(last_own, "grad_send_last", after=last_own[0])
    for i, n in enumerate(LAST_GRADS):
        out[n] = _adamw_halves(n, w[n][0], last_own[i], last_other[i], mom[n][0], var[n][0], core, last_other[i])
    out["conv_w"] = _adamw_cols(conv_w[0], g_conv_w_all, m_conv_w[0], v_conv_w[0], chip, "adamw_conv_w")
    out["ffn_conv_w"] = _adamw_cols(ffn_conv_w[0], g_ffn_w_all, m_ffn_conv_w[0], v_ffn_conv_w[0], chip, "adamw_ffn_conv_w")
    flat = lambda a: a.reshape(RPB_ROWS, RPB_COLS)
    out["rpb"] = _rpb_update(g_rpb_rev, flat(rpb), flat(m_rpb), flat(v_rpb))

    res = [[out[n][k].reshape(w[n].shape) for n in WEIGHTS] for k in range(4)]
    return (loss_row[0, 0], grad_x[None], *res[0], *res[1], *res[2], *res[3])
```

```python
import jax
import jax.numpy as jnp
from jax import lax
from jax.experimental import pallas as pl
from jax.experimental.pallas import tpu as pltpu

F32 = jnp.float32
BF16 = jnp.bfloat16
MXU_DTYPE = jnp.bfloat16

D = 1024
CTX = 256
GW = 64
DA = 512
NH = 8
HD = 64
DC = 512
CW = 31
DFF = 2816
NIN = 3 * DA + 2 * DC
EPS = 1e-6
SCALE = HD ** -0.5
NEG = -1e30
NA_ROWS = 8
PAIR_ROWS = NA_ROWS + 1
TAB_BLOCKS = 17
LANES = 128
VMEM_LIMIT = 56 * 1024 * 1024

ADAM_LR = 0.001
ADAM_B1 = 0.9
ADAM_B2 = 0.999
ADAM_EPS = 1e-08
ADAM_WD = 0.01
ADAM_STEP = 10

MESH = pl.DeviceIdType.MESH


def _pallas(body, *, name, semantics=None, vmem=VMEM_LIMIT, prefetch=0, **kw):
    params = dict(vmem_limit_bytes=vmem)
    if semantics is not None:
        params["dimension_semantics"] = semantics
    if prefetch:
        kw["grid_spec"] = pltpu.PrefetchScalarGridSpec(
            num_scalar_prefetch=prefetch, grid=kw.pop("grid"), in_specs=kw.pop("in_specs"), out_specs=kw.pop("out_specs"),
            scratch_shapes=kw.pop("scratch_shapes", ()))
    return pl.pallas_call(body, name=name, compiler_params=pltpu.CompilerParams(**params), **kw)


def _sds(shape, dtype):
    return jax.ShapeDtypeStruct(shape, dtype)


def _vec_spec(n):
    return pl.BlockSpec((1, n), lambda *_: (0, 0))


def _colsum8(x):
    t, n = x.shape
    return jnp.sum(x.reshape(t // 8, 8, n), axis=0)


def _sigmoid(x):
    return 0.5 * jnp.tanh(0.5 * x) + 0.5


def _mm(a, b, *, mode, m, n, k, tm, tn, tk, out_dtype, name, a_off=(0, 0), b_off=(0, 0),
        out_total=None, o_off=(0, 0), into=None):
    a_list = list(a) if isinstance(a, (list, tuple)) else [a]
    b_list = list(b) if isinstance(b, (list, tuple)) else [b]
    assert m % tm == 0 and n % tn == 0 and k % tk == 0, (name, m, n, k, tm, tn, tk)
    gi, gj, nk = m // tm, n // tn, k // tk
    dims = {"nn": (((1,), (0,)), ((), ())), "nt": (((1,), (1,)), ((), ())), "tn": (((0,), (0,)), ((), ()))}[mode]

    if len(a_list) > 1:
        assert mode != "tn" and nk == 1 and sum(x.shape[1] for x in a_list) == k
        a_specs = [pl.BlockSpec((tm, x.shape[1]), lambda i, j, kk: (i, 0)) for x in a_list]
    elif mode == "tn":
        a_specs = [pl.BlockSpec((tk, tm), lambda i, j, kk: (kk + a_off[0], i + a_off[1]))]
    else:
        a_specs = [pl.BlockSpec((tm, tk), lambda i, j, kk: (i + a_off[0], kk + a_off[1]))]
    if len(b_list) > 1:
        assert mode == "tn" and gj == 1 and sum(x.shape[1] for x in b_list) == n
        b_specs = [pl.BlockSpec((tk, x.shape[1]), lambda i, j, kk: (kk, 0)) for x in b_list]
    elif mode == "nt":
        b_specs = [pl.BlockSpec((tn, tk), lambda i, j, kk: (j + b_off[0], kk + b_off[1]))]
    else:
        b_specs = [pl.BlockSpec((tk, tn), lambda i, j, kk: (kk + b_off[0], j + b_off[1]))]

    na, nb = len(a_list), len(b_list)
    in_place = nk > 1 and out_dtype == F32
    n_in = na + nb + (into is not None)

    def body(*refs):
        a_refs, b_refs, o_ref = refs[:na], refs[na:na + nb], refs[n_in]
        acc = o_ref if in_place else (refs[n_in + 1] if nk > 1 else None)
        kk = pl.program_id(2)

        def whole(piece_refs):
            vals = [r[...].astype(MXU_DTYPE) for r in piece_refs]
            return vals[0] if len(vals) == 1 else jnp.concatenate(vals, axis=1)

        p = lax.dot_general(whole(a_refs), whole(b_refs), dims, preferred_element_type=F32)
        if nk == 1:
            o_ref[...] = p.astype(out_dtype)
            return

        @pl.when(kk == 0)
        def _():
            acc[...] = p

        @pl.when(kk > 0)
        def _():
            acc[...] += p

        if not in_place:
            @pl.when(kk == nk - 1)
            def _():
                o_ref[...] = acc[...].astype(out_dtype)

    ins = [*a_list, *b_list]
    in_specs = a_specs + b_specs
    extra = {}
    if into is not None:
        extra["input_output_aliases"] = {len(ins): 0}
        ins.append(into)
        in_specs.append(pl.BlockSpec(memory_space=pl.ANY))
    return _pallas(
        body, name=name, grid=(gi, gj, nk), in_specs=in_specs,
        out_specs=pl.BlockSpec((tm, tn), lambda i, j, kk: (i + o_off[0], j + o_off[1])),
        out_shape=_sds(out_total or (m, n), out_dtype),
        scratch_shapes=[pltpu.VMEM((tm, tn), F32)] if nk > 1 and not in_place else [],
        semantics=("parallel", "parallel", "arbitrary"), **extra,
    )(*ins)


ROW_TILE = 256


def _rmsmod_fwd(x, ctx, g, sc, sh, csc, csh):
    s = x.shape[0]
    nt = s // ROW_TILE
    assert ctx.shape[0] == ROW_TILE

    def body(x_ref, c_ref, g_ref, sc_ref, sh_ref, csc_ref, csh_ref, o_ref):
        is_ctx = pl.program_id(0) == nt
        xv = jnp.where(is_ctx, c_ref[...], x_ref[...])
        scv = jnp.where(is_ctx, csc_ref[...], sc_ref[...])
        shv = jnp.where(is_ctx, csh_ref[...], sh_ref[...])
        r = lax.rsqrt(jnp.mean(xv * xv, axis=-1, keepdims=True) + EPS)
        y = xv * r * g_ref[...]
        o_ref[...] = (y * (1.0 + scv) + shv).astype(o_ref.dtype)

    return _pallas(
        body, name="rmsmod1_fwd", grid=(nt + 1,),
        in_specs=[pl.BlockSpec((ROW_TILE, D), lambda i: (jnp.minimum(i, nt - 1), 0)),
                  pl.BlockSpec((ROW_TILE, D), lambda i: (0, 0))] + [_vec_spec(D)] * 5,
        out_specs=pl.BlockSpec((ROW_TILE, D), lambda i: (i, 0)),
        out_shape=_sds((s + CTX, D), MXU_DTYPE),
        semantics=("arbitrary",),
    )(x, ctx, g, sc, sh, csc, csh)


def _resid_rmsmod_fwd(x, y, gt, g, sc, sh):
    s = x.shape[0]

    def body(x_ref, y_ref, gt_ref, g_ref, sc_ref, sh_ref, x1_ref, h_ref):
        x1 = x_ref[...] + gt_ref[...] * y_ref[...]
        x1_ref[...] = x1
        r = lax.rsqrt(jnp.mean(x1 * x1, axis=-1, keepdims=True) + EPS)
        h_ref[...] = ((x1 * r * g_ref[...]) * (1.0 + sc_ref[...]) + sh_ref[...]).astype(h_ref.dtype)

    row = pl.BlockSpec((ROW_TILE, D), lambda i: (i, 0))
    return _pallas(
        body, name="resid_rmsmod2_fwd", grid=(s // ROW_TILE,),
        in_specs=[row, row] + [_vec_spec(D)] * 4,
        out_specs=[row, row],
        out_shape=[_sds((s, D), F32), _sds((s, D), MXU_DTYPE)],
        semantics=("parallel",),
    )(x, y, gt, g, sc, sh)


def _final_fwd_bwd(x1, z, gt2, gf, tgt):
    s = x1.shape[0]
    nt = s // ROW_TILE

    def body(x1_ref, z_ref, gt_ref, gf_ref, t_ref, dx2_ref, dz_ref, loss_ref, dgt_ref, dgf_ref, a_loss, a_gt, a_gf):
        i = pl.program_id(0)

        @pl.when(i == 0)
        def _():
            a_loss[...] = jnp.zeros_like(a_loss)
            a_gt[...] = jnp.zeros_like(a_gt)
            a_gf[...] = jnp.zeros_like(a_gf)

        zv = z_ref[...]
        gt = gt_ref[...]
        gf_ = gf_ref[...]
        x2 = x1_ref[...] + gt * zv
        r = lax.rsqrt(jnp.mean(x2 * x2, axis=-1, keepdims=True) + EPS)
        xn = x2 * r
        e = xn * gf_ - t_ref[...]
        a_loss[...] += _colsum8(e * e)
        dyo = e * (1.0 / D)
        a_gf[...] += _colsum8(dyo * xn)
        gdy = gf_ * dyo
        dx2 = r * gdy - xn * (r * r) * jnp.mean(x2 * gdy, axis=-1, keepdims=True)
        dx2_ref[...] = dx2
        dz_ref[...] = (gt * dx2).astype(dz_ref.dtype)
        a_gt[...] += _colsum8(dx2 * zv)

        @pl.when(i == nt - 1)
        def _():
            tot = jnp.sum(jnp.sum(a_loss[...], axis=0, keepdims=True), axis=1, keepdims=True) * (0.5 / D)
            loss_ref[...] = jnp.broadcast_to(tot, loss_ref.shape)
            dgt_ref[...] = jnp.sum(a_gt[...], axis=0, keepdims=True)
            dgf_ref[...] = jnp.sum(a_gf[...], axis=0, keepdims=True)

    row = pl.BlockSpec((ROW_TILE, D), lambda i: (i, 0))
    return _pallas(
        body, name="final_norm_loss", grid=(nt,),
        in_specs=[row, row, _vec_spec(D), _vec_spec(D), row],
        out_specs=[row, row, _vec_spec(LANES), _vec_spec(D), _vec_spec(D)],
        out_shape=[_sds((s, D), F32), _sds((s, D), MXU_DTYPE), _sds((1, LANES), F32), _sds((1, D), F32), _sds((1, D), F32)],
        scratch_shapes=[pltpu.VMEM((8, D), F32)] * 3,
        semantics=("arbitrary",),
    )(x1, z, gt2, gf, tgt)


def _rmsmod_bwd(xin, dh, g, sc, *, name, dh_row_off=0, add=None, resid=None):
    s = xin.shape[0]
    nt = s // ROW_TILE
    want_dx = add is not None
    assert resid is None or want_dx

    def body(*refs):
        it = iter(refs)
        x_ref, dh_ref, g_ref, sc_ref = next(it), next(it), next(it), next(it)
        add_ref = next(it) if want_dx else None
        gt_ref, y_ref = (next(it), next(it)) if resid is not None else (None, None)
        dsh_ref, dsc_ref, dg_ref = next(it), next(it), next(it)
        dx_ref = next(it) if want_dx else None
        dy_ref, dgt_ref = (next(it), next(it)) if resid is not None else (None, None)
        a_sh, a_sc, a_g = next(it), next(it), next(it)
        a_gt = next(it) if resid is not None else None
        i = pl.program_id(0)

        @pl.when(i == 0)
        def _():
            a_sh[...] = jnp.zeros_like(a_sh)
            a_sc[...] = jnp.zeros_like(a_sc)
            a_g[...] = jnp.zeros_like(a_g)
            if a_gt is not None:
                a_gt[...] = jnp.zeros_like(a_gt)

        xv = x_ref[...]
        dhv = dh_ref[...]
        gv = g_ref[...]
        r = lax.rsqrt(jnp.mean(xv * xv, axis=-1, keepdims=True) + EPS)
        xn = xv * r
        a_sh[...] += _colsum8(dhv)
        a_sc[...] += _colsum8(dhv * (xn * gv))
        dn = dhv * (1.0 + sc_ref[...])
        a_g[...] += _colsum8(dn * xn)
        if want_dx:
            gdn = gv * dn
            dx = add_ref[...] + r * gdn - xn * (r * r) * jnp.mean(xv * gdn, axis=-1, keepdims=True)
            dx_ref[...] = dx
            if resid is not None:
                dy_ref[...] = (gt_ref[...] * dx).astype(dy_ref.dtype)
                a_gt[...] += _colsum8(dx * y_ref[...])

        @pl.when(i == nt - 1)
        def _():
            dsh_ref[...] = jnp.sum(a_sh[...], axis=0, keepdims=True)
            dsc_ref[...] = jnp.sum(a_sc[...], axis=0, keepdims=True)
            dg_ref[...] = jnp.sum(a_g[...], axis=0, keepdims=True)
            if a_gt is not None:
                dgt_ref[...] = jnp.sum(a_gt[...], axis=0, keepdims=True)

    row = pl.BlockSpec((ROW_TILE, D), lambda i: (i, 0))
    ins = [xin, dh, g, sc]
    in_specs = [row, pl.BlockSpec((ROW_TILE, D), lambda i: (i + dh_row_off, 0)), _vec_spec(D), _vec_spec(D)]
    out_specs = [_vec_spec(D)] * 3
    out_shape = [_sds((1, D), F32)] * 3
    scratch = [pltpu.VMEM((8, D), F32)] * 3
    if want_dx:
        ins.append(add)
        in_specs.append(row)
        out_specs.append(row)
        out_shape.append(_sds((s, D), F32))
    if resid is not None:
        ins += [resid[0], resid[1]]
        in_specs += [_vec_spec(D), row]
        out_specs += [row, _vec_spec(D)]
        out_shape += [_sds((s, D), MXU_DTYPE), _sds((1, D), F32)]
        scratch.append(pltpu.VMEM((8, D), F32))
    return _pallas(body, name=name, grid=(nt,), in_specs=in_specs, out_specs=out_specs, out_shape=out_shape,
                   scratch_shapes=scratch, semantics=("arbitrary",))(*ins)


FF_TILE = 128
FF_CHUNK = 128
HALO = 8


def _shift3(pad_ref, r0, ch):
    return tuple(pad_ref[pl.ds(r0 + HALO + d, ch), :] for d in (-1, 0, 1))


def _fill_padded(pad_ref, src_ref, s, ch, halo):
    zeros = jnp.zeros((halo, pad_ref.shape[1]), F32)
    pad_ref[0:halo, :] = zeros
    pad_ref[s + halo:s + 2 * halo, :] = zeros

    def cp(c, carry):
        r0 = pl.multiple_of(c * ch, ch)
        pad_ref[pl.ds(r0 + halo, ch), :] = src_ref[pl.ds(r0, ch), :].astype(F32)
        return carry

    lax.fori_loop(0, s // ch, cp, 0)


def _ffn_act_fwd(u, w, b):
    s = u.shape[0]
    nj = DFF // FF_TILE
    ch = FF_CHUNK

    def body(ug_ref, uv_ref, wg_ref, wv_ref, bg_ref, bv_ref, f_ref, gpad, vpad):
        _fill_padded(gpad, ug_ref, s, ch, HALO)
        _fill_padded(vpad, uv_ref, s, ch, HALO)

        def conv(pad, w_ref, b_ref, r0):
            prev, cur, nxt = _shift3(pad, r0, ch)
            return w_ref[0:1, :] * prev + w_ref[1:2, :] * cur + w_ref[2:3, :] * nxt + b_ref[...]

        def step(c, carry):
            r0 = pl.multiple_of(c * ch, ch)
            gc = conv(gpad, wg_ref, bg_ref, r0)
            vc = conv(vpad, wv_ref, bv_ref, r0)
            f_ref[pl.ds(r0, ch), :] = (gc * _sigmoid(gc) * vc).astype(f_ref.dtype)
            return carry

        lax.fori_loop(0, s // ch, step, 0)

    col = lambda off: pl.BlockSpec((s, FF_TILE), lambda j: (0, j + off))
    wsp = lambda off: pl.BlockSpec((3, FF_TILE), lambda j: (0, j + off))
    bsp = lambda off: pl.BlockSpec((1, FF_TILE), lambda j: (0, j + off))
    return _pallas(
        body, name="ffn_act_fwd", grid=(nj,),
        in_specs=[col(0), col(nj), wsp(0), wsp(nj), bsp(0), bsp(nj)],
        out_specs=col(0), out_shape=_sds((s, DFF), MXU_DTYPE),
        scratch_shapes=[pltpu.VMEM((s + 2 * HALO, FF_TILE), F32)] * 2,
        semantics=("parallel",),
    )(u, u, w, w, b, b)


def _ffn_act_bwd(u, df, w, b):
    s = u.shape[0]
    nj = DFF // FF_TILE
    ch = FF_CHUNK

    def body(ug_ref, uv_ref, df_ref, wg_ref, wv_ref, bg_ref, bv_ref,
             dug_ref, duv_ref, dwg_ref, dwv_ref, dbg_ref, dbv_ref, gpad, vpad, dgpad, dvpad, acc):
        _fill_padded(gpad, ug_ref, s, ch, HALO)
        _fill_padded(vpad, uv_ref, s, ch, HALO)
        zeros = jnp.zeros((HALO, FF_TILE), F32)
        for p in (dgpad, dvpad):
            p[0:HALO, :] = zeros
            p[s + HALO:s + 2 * HALO, :] = zeros
        acc[...] = jnp.zeros_like(acc)

        def step(c, carry):
            r0 = pl.multiple_of(c * ch, ch)
            gs = _shift3(gpad, r0, ch)
            vs = _shift3(vpad, r0, ch)
            gc = wg_ref[0:1, :] * gs[0] + wg_ref[1:2, :] * gs[1] + wg_ref[2:3, :] * gs[2] + bg_ref[...]
            vc = wv_ref[0:1, :] * vs[0] + wv_ref[1:2, :] * vs[1] + wv_ref[2:3, :] * vs[2] + bv_ref[...]
            sg = _sigmoid(gc)
            dfv = df_ref[pl.ds(r0, ch), :].astype(F32)
            dgc = dfv * vc * (sg * (1.0 + gc * (1.0 - sg)))
            dvc = dfv * (gc * sg)
            dgpad[pl.ds(r0 + HALO, ch), :] = dgc
            dvpad[pl.ds(r0 + HALO, ch), :] = dvc
            for t in range(3):
                acc[8 * t:8 * t + 8, :] += _colsum8(dgc * gs[t])
                acc[24 + 8 * t:32 + 8 * t, :] += _colsum8(dvc * vs[t])
            acc[48:56, :] += _colsum8(dgc)
            acc[56:64, :] += _colsum8(dvc)
            return carry

        lax.fori_loop(0, s // ch, step, 0)

        def step2(c, carry):
            r0 = pl.multiple_of(c * ch, ch)
            for pad, w_ref, o_ref in ((dgpad, wg_ref, dug_ref), (dvpad, wv_ref, duv_ref)):
                prev, cur, nxt = _shift3(pad, r0, ch)
                o_ref[pl.ds(r0, ch), :] = (w_ref[0:1, :] * nxt + w_ref[1:2, :] * cur + w_ref[2:3, :] * prev).astype(o_ref.dtype)
            return carry

        lax.fori_loop(0, s // ch, step2, 0)
        for t in range(3):
            dwg_ref[t:t + 1, :] = jnp.sum(acc[8 * t:8 * t + 8, :], axis=0, keepdims=True)
            dwv_ref[t:t + 1, :] = jnp.sum(acc[24 + 8 * t:32 + 8 * t, :], axis=0, keepdims=True)
        dbg_ref[...] = jnp.sum(acc[48:56, :], axis=0, keepdims=True)
        dbv_ref[...] = jnp.sum(acc[56:64, :], axis=0, keepdims=True)

    col = lambda off: pl.BlockSpec((s, FF_TILE), lambda j: (0, j + off))
    wsp = lambda off: pl.BlockSpec((3, FF_TILE), lambda j: (0, j + off))
    bsp = lambda off: pl.BlockSpec((1, FF_TILE), lambda j: (0, j + off))
    return _pallas(
        body, name="ffn_act_bwd", grid=(nj,),
        in_specs=[col(0), col(nj), col(0), wsp(0), wsp(nj), bsp(0), bsp(nj)],
        out_specs=[col(0), col(0), wsp(0), wsp(0), bsp(0), bsp(0)],
        out_shape=[_sds((s, DFF), MXU_DTYPE)] * 2 + [_sds((3, DFF), F32)] * 2 + [_sds((1, DFF), F32)] * 2,
        scratch_shapes=[pltpu.VMEM((s + 2 * HALO, FF_TILE), F32)] * 4 + [pltpu.VMEM((64, FF_TILE), F32)],
        semantics=("parallel",),
    )(u, u, df, w, w, b, b)


CONV_CHUNK = 64
CONV_HALO = 16


def _tap(pad_ref, r0, k):
    return pad_ref[pl.ds(r0 + CONV_HALO - CW // 2 + k, CONV_CHUNK), :]


def _glu_into(pad_ref, a_ref, g_ref, s):
    zeros = jnp.zeros((CONV_HALO, LANES), F32)
    pad_ref[0:CONV_HALO, :] = zeros
    pad_ref[s + CONV_HALO:s + 2 * CONV_HALO, :] = zeros

    def cp(c, carry):
        r0 = pl.multiple_of(c * ROW_TILE, ROW_TILE)
        pad_ref[pl.ds(r0 + CONV_HALO, ROW_TILE), :] = a_ref[pl.ds(r0, ROW_TILE), :] * _sigmoid(g_ref[pl.ds(r0, ROW_TILE), :])
        return carry

    lax.fori_loop(0, s // ROW_TILE, cp, 0)


def _conf_conv_fwd(ag, conv_w, conv_b):
    s = ag.shape[0]
    nc = DC // LANES

    def body(a_ref, g_ref, w_ref, b_ref, o_ref, upad):
        _glu_into(upad, a_ref, g_ref, s)

        def step(c, carry):
            r0 = pl.multiple_of(c * CONV_CHUNK, CONV_CHUNK)
            acc = jnp.broadcast_to(b_ref[...], (CONV_CHUNK, LANES))
            for k in range(CW):
                acc = acc + w_ref[k:k + 1, :] * _tap(upad, r0, k)
            o_ref[pl.ds(r0, CONV_CHUNK), :] = acc
            return carry

        lax.fori_loop(0, s // CONV_CHUNK, step, 0)

    col = lambda off: pl.BlockSpec((s, LANES), lambda c: (0, c + off))
    return _pallas(
        body, name="conf_conv_fwd", grid=(nc,),
        in_specs=[col(0), col(nc), pl.BlockSpec((CW, LANES), lambda c: (0, c)), pl.BlockSpec((1, LANES), lambda c: (0, c))],
        out_specs=col(0), out_shape=_sds((s, DC), F32),
        scratch_shapes=[pltpu.VMEM((s + 2 * CONV_HALO, LANES), F32)],
        semantics=("parallel",),
    )(ag, ag, conv_w, conv_b)


def _ln_stats(x):
    mu = jnp.mean(x, axis=-1, keepdims=True)
    xc = x - mu
    var = jnp.mean(xc * xc, axis=-1, keepdims=True)
    rstd = lax.rsqrt(var + EPS)
    return xc * rstd, rstd


def _conf_ln_fwd(u1, ln_g, ln_b, ycat):
    s = u1.shape[0]

    def body(u_ref, g_ref, b_ref, ycat_ref, o_ref):
        del ycat_ref
        xhat, _ = _ln_stats(u_ref[...])
        y = xhat * g_ref[...] + b_ref[...]
        o_ref[...] = (y * _sigmoid(y)).astype(o_ref.dtype)

    return _pallas(
        body, name="conf_ln_fwd", grid=(s // ROW_TILE,),
        in_specs=[pl.BlockSpec((ROW_TILE, DC), lambda i: (i, 0)), _vec_spec(DC), _vec_spec(DC),
                  pl.BlockSpec(memory_space=pl.ANY)],
        out_specs=pl.BlockSpec((ROW_TILE, DC), lambda i: (i, 1)),
        out_shape=_sds(ycat.shape, ycat.dtype),
        input_output_aliases={3: 0},
        semantics=("parallel",),
    )(u1, ln_g, ln_b, ycat)


def _conf_ln_bwd(dycat, u1, ln_g, ln_b):
    s = u1.shape[0]
    nt = s // ROW_TILE

    def body(dy_ref, u_ref, g_ref, b_ref, du_ref, dg_ref, db_ref, a_g, a_b):
        i = pl.program_id(0)

        @pl.when(i == 0)
        def _():
            a_g[...] = jnp.zeros_like(a_g)
            a_b[...] = jnp.zeros_like(a_b)

        xhat, rstd = _ln_stats(u_ref[...])
        gv = g_ref[...]
        y = xhat * gv + b_ref[...]
        sg = _sigmoid(y)
        dyl = dy_ref[...] * (sg * (1.0 + y * (1.0 - sg)))
        a_g[...] += _colsum8(dyl * xhat)
        a_b[...] += _colsum8(dyl)
        dxh = dyl * gv
        du_ref[...] = rstd * (dxh - jnp.mean(dxh, axis=-1, keepdims=True)
                              - xhat * jnp.mean(dxh * xhat, axis=-1, keepdims=True))

        @pl.when(i == nt - 1)
        def _():
            dg_ref[...] = jnp.sum(a_g[...], axis=0, keepdims=True)
            db_ref[...] = jnp.sum(a_b[...], axis=0, keepdims=True)

    return _pallas(
        body, name="conf_ln_bwd", grid=(nt,),
        in_specs=[pl.BlockSpec((ROW_TILE, DC), lambda i: (i, 1)), pl.BlockSpec((ROW_TILE, DC), lambda i: (i, 0)),
                  _vec_spec(DC), _vec_spec(DC)],
        out_specs=[pl.BlockSpec((ROW_TILE, DC), lambda i: (i, 0)), _vec_spec(DC), _vec_spec(DC)],
        out_shape=[_sds((s, DC), F32), _sds((1, DC), F32), _sds((1, DC), F32)],
        scratch_shapes=[pltpu.VMEM((8, DC), F32)] * 2,
        semantics=("arbitrary",),
    )(dycat, u1, ln_g, ln_b)


def _conf_conv_bwd(ag, du1, conv_w, rows_out):
    s = ag.shape[0]
    nc = DC // LANES

    def body(a_ref, g_ref, d_ref, w_ref, da_ref, dg_ref, dw_ref, db_ref, upad, dpad, acc):
        _glu_into(upad, a_ref, g_ref, s)
        _fill_padded(dpad, d_ref, s, ROW_TILE, CONV_HALO)
        acc[...] = jnp.zeros_like(acc)

        def step(c, carry):
            r0 = pl.multiple_of(c * CONV_CHUNK, CONV_CHUNK)
            dcur = dpad[pl.ds(r0 + CONV_HALO, CONV_CHUNK), :]
            du0 = jnp.zeros((CONV_CHUNK, LANES), F32)
            for k in range(CW):
                du0 = du0 + w_ref[k:k + 1, :] * _tap(dpad, r0, CW - 1 - k)
                acc[8 * k:8 * k + 8, :] += _colsum8(dcur * _tap(upad, r0, k))
            acc[8 * CW:8 * CW + 8, :] += _colsum8(dcur)
            av = a_ref[pl.ds(r0, CONV_CHUNK), :]
            sg = _sigmoid(g_ref[pl.ds(r0, CONV_CHUNK), :])
            da_ref[pl.ds(r0, CONV_CHUNK), :] = (du0 * sg).astype(da_ref.dtype)
            dg_ref[pl.ds(r0, CONV_CHUNK), :] = (du0 * av * (sg * (1.0 - sg))).astype(dg_ref.dtype)
            return carry

        lax.fori_loop(0, s // CONV_CHUNK, step, 0)
        if rows_out > s:
            zeros = jnp.zeros((rows_out - s, LANES), da_ref.dtype)
            da_ref[s:rows_out, :] = zeros
            dg_ref[s:rows_out, :] = zeros
        for k in range(CW):
            dw_ref[k:k + 1, :] = jnp.sum(acc[8 * k:8 * k + 8, :], axis=0, keepdims=True)
        db_ref[...] = jnp.sum(acc[8 * CW:8 * CW + 8, :], axis=0, keepdims=True)

    col = lambda off: pl.BlockSpec((s, LANES), lambda c: (0, c + off))
    ocol = pl.BlockSpec((rows_out, LANES), lambda c: (0, c))
    return _pallas(
        body, name="conf_conv_bwd", grid=(nc,),
        in_specs=[col(0), col(nc), col(0), pl.BlockSpec((CW, LANES), lambda c: (0, c))],
        out_specs=[ocol, ocol, pl.BlockSpec((CW, LANES), lambda c: (0, c)), pl.BlockSpec((1, LANES), lambda c: (0, c))],
        out_shape=[_sds((rows_out, DC), MXU_DTYPE)] * 2 + [_sds((CW, DC), F32), _sds((1, DC), F32)],
        scratch_shapes=[pltpu.VMEM((s + 2 * CONV_HALO, LANES), F32)] * 2 + [pltpu.VMEM((8 * (CW + 1), LANES), F32)],
        semantics=("parallel",),
    )(ag, ag, du1, conv_w)


Q_TILE = 2 * GW
K_WIN = PAIR_ROWS * GW


def _bias_table(rpb_rev):
    def body(p_ref, t_ref):
        kcol = lax.broadcasted_iota(jnp.int32, (GW, LANES), 0)
        lane = lax.broadcasted_iota(jnp.int32, (GW, LANES), 1)
        qcol = lane % GW
        cs = jnp.clip(qcol - NA_ROWS, 0, GW - 2 * NA_ROWS)
        colvalid = (kcol >= cs) & (kcol < cs + 2 * NA_ROWS)
        neg = jnp.full((GW, LANES), NEG, F32)

        def skew(h, ro, shift):
            if ro < 0 or ro >= 2 * NA_ROWS - 1:
                return neg
            row = jnp.broadcast_to(p_ref[h * 16 + ro:h * 16 + ro + 1, :], (GW, LANES))
            return pltpu.roll(row, shift, 1, stride=1, stride_axis=0)

        for h in range(NH):
            for b in range(TAB_BLOCKS):
                val = jnp.where(lane < GW, skew(h, b - 1, GW + 1), skew(h, b - 2, 1))
                t_ref[h, b * GW:(b + 1) * GW, :] = jnp.where(colvalid, val, neg)

    return _pallas(body, name="attn_bias_table", out_shape=_sds((NH, TAB_BLOCKS * GW, LANES), F32))(rpb_rev)


def _rpb_grad(tt):
    def body(t_ref, o_ref):
        lane = lax.broadcasted_iota(jnp.int32, (GW, LANES), 1)
        si = lax.broadcasted_iota(jnp.int32, (GW, GW), 0)
        ti = lax.broadcasted_iota(jnp.int32, (GW, GW), 1)
        flip = jnp.where(si + ti == GW - 1, 1.0, 0.0).astype(F32)
        o_ref[...] = jnp.zeros_like(o_ref)
        for h in range(NH):
            for ro in range(2 * NA_ROWS - 1):
                lo = t_ref[h, (ro + 1) * GW:(ro + 2) * GW, :]
                hi = t_ref[h, (ro + 2) * GW:(ro + 3) * GW, :]
                g = jnp.where(lane < GW, lo + pltpu.roll(hi, GW, 1), 0.0)
                gf = jnp.dot(flip, g, preferred_element_type=F32, precision=lax.Precision.HIGHEST)
                sk = pltpu.roll(gf, 0, 1, stride=1, stride_axis=0)
                o_ref[h * 16 + ro:h * 16 + ro + 1, :] = jnp.sum(sk, axis=0, keepdims=True)

    return _pallas(body, name="attn_rpb_grad", out_shape=_sds((NH * 16, LANES), F32))(tt)


def _attn_geometry(i, rows):
    wsp = jnp.clip(2 * i - NA_ROWS // 2, 0, rows - PAIR_ROWS)
    k0 = pl.multiple_of(wsp * GW, GW)
    t0 = pl.multiple_of((wsp - 2 * i + NA_ROWS) * GW, GW)
    rr = lax.broadcasted_iota(jnp.int32, (GW, Q_TILE), 1) // GW
    wsr = jnp.clip(2 * i + rr - NA_ROWS // 2, 0, rows - NA_ROWS)
    edge_masks = tuple(jnp.where((kr >= wsr) & (kr < wsr + NA_ROWS), 0.0, NEG).astype(F32)
                       for kr in (wsp, wsp + PAIR_ROWS - 1))
    return k0, t0, edge_masks


def _biased(s_raw, bias, edge_masks):
    x = s_raw + bias
    return jnp.concatenate([x[:GW] + edge_masks[0], x[GW:K_WIN - GW], x[K_WIN - GW:] + edge_masks[1]], axis=0)


def _two_heads_on_lanes(xt):
    feat = lax.broadcasted_iota(jnp.int32, xt.shape, 0)
    zero = jnp.zeros_like(xt)
    return jnp.concatenate([jnp.where(feat < HD, xt, zero), jnp.where(feat >= HD, xt, zero)], axis=1)


def _two_heads_on_rows(x):
    lane = lax.broadcasted_iota(jnp.int32, x.shape, 1)
    zero = jnp.zeros_like(x)
    return jnp.concatenate([jnp.where(lane < HD, x, zero), jnp.where(lane >= HD, x, zero)], axis=0)


def _pick_heads(x2):
    n = x2.shape[0] // 2
    lane = lax.broadcasted_iota(jnp.int32, (n, LANES), 1)
    return jnp.where(lane < HD, x2[:n], x2[n:])


_TN = (((0,), (0,)), ((), ()))


def _attn_fwd(qkv, tab, s):
    rows = s // GW
    npair = rows // 2

    def body(q_ref, kv_ref, tab_ref, o_ref, lse_ref):
        i = pl.program_id(0)
        k0, t0, edge_masks = _attn_geometry(i, rows)
        for p in range(NH // 2):
            cq = slice(p * LANES, (p + 1) * LANES)
            ck = slice(DA + p * LANES, DA + (p + 1) * LANES)
            cv = slice(2 * DA + p * LANES, 2 * DA + (p + 1) * LANES)
            qm2 = _two_heads_on_lanes(q_ref[:, cq].T) * SCALE
            s_loc = jnp.dot(kv_ref[pl.ds(k0, K_WIN), ck], qm2, preferred_element_type=F32)
            s_ctx = jnp.dot(kv_ref[pl.ds(s, CTX), ck], qm2, preferred_element_type=F32)
            p_loc, p_ctx = [], []
            for hh in range(2):
                h = 2 * p + hh
                ch = slice(hh * Q_TILE, (hh + 1) * Q_TILE)
                sl = _biased(s_loc[:, ch], tab_ref[h, pl.ds(t0, K_WIN), :], edge_masks)
                sc = s_ctx[:, ch]
                m = jnp.maximum(jnp.max(sl, axis=0, keepdims=True), jnp.max(sc, axis=0, keepdims=True))
                el = jnp.exp(sl - m)
                ec = jnp.exp(sc - m)
                l = jnp.sum(el, axis=0, keepdims=True) + jnp.sum(ec, axis=0, keepdims=True)
                inv = 1.0 / l
                lse_ref[h:h + 1, :] = m + jnp.log(l)
                p_loc.append((el * inv).astype(MXU_DTYPE))
                p_ctx.append((ec * inv).astype(MXU_DTYPE))
            o2 = (lax.dot_general(jnp.concatenate(p_loc, axis=1), kv_ref[pl.ds(k0, K_WIN), cv], _TN, preferred_element_type=F32)
                  + lax.dot_general(jnp.concatenate(p_ctx, axis=1), kv_ref[pl.ds(s, CTX), cv], _TN, preferred_element_type=F32))
            o_ref[:, cq] = _pick_heads(o2).astype(o_ref.dtype)

    return _pallas(
        body, name="attn_fwd", grid=(npair,),
        in_specs=[pl.BlockSpec((Q_TILE, DA), lambda i: (i, 0)), pl.BlockSpec(memory_space=pltpu.VMEM),
                  pl.BlockSpec(memory_space=pltpu.VMEM)],
        out_specs=[pl.BlockSpec((Q_TILE, DA), lambda i: (i, 0)), pl.BlockSpec((NH, Q_TILE), lambda i: (0, i))],
        out_shape=[_sds((s, D), MXU_DTYPE), _sds((NH, s), F32)],
        semantics=("arbitrary",),
    )(qkv, qkv, tab)


def _attn_bwd(qkv, tab, lse, dycat, s):
    rows = s // GW
    npair = rows // 2
    sa = s + CTX
    nzero = CTX // Q_TILE

    def body(q_ref, do_ref, lse_ref, kv_ref, tab_ref, dq_ref, dkv_ref, tt_ref, dk_acc, dv_acc):
        i = pl.program_id(0)

        @pl.when(i == 0)
        def _():
            dk_acc[...] = jnp.zeros_like(dk_acc)
            dv_acc[...] = jnp.zeros_like(dv_acc)
            tt_ref[...] = jnp.zeros_like(tt_ref)

        @pl.when(i >= npair)
        def _():
            dq_ref[...] = jnp.zeros_like(dq_ref)

        @pl.when(i < npair)
        def _():
            k0, t0, edge_masks = _attn_geometry(i, rows)
            for p in range(NH // 2):
                cq = slice(p * LANES, (p + 1) * LANES)
                ck = slice(DA + p * LANES, DA + (p + 1) * LANES)
                cv = slice(2 * DA + p * LANES, 2 * DA + (p + 1) * LANES)
                qp = q_ref[:, cq] * SCALE
                dop = do_ref[:, cq].astype(MXU_DTYPE)
                qm2 = _two_heads_on_lanes(qp.T)
                dom2 = _two_heads_on_lanes(dop.T)
                kw = kv_ref[pl.ds(k0, K_WIN), ck]
                kc = kv_ref[pl.ds(s, CTX), ck]
                vw = kv_ref[pl.ds(k0, K_WIN), cv]
                vc = kv_ref[pl.ds(s, CTX), cv]
                s_loc = jnp.dot(kw, qm2, preferred_element_type=F32)
                s_ctx = jnp.dot(kc, qm2, preferred_element_type=F32)
                dp_loc = jnp.dot(vw, dom2, preferred_element_type=F32)
                dp_ctx = jnp.dot(vc, dom2, preferred_element_type=F32)
                p_loc, p_ctx, ds_loc, ds_ctx = [], [], [], []
                for hh in range(2):
                    h = 2 * p + hh
                    ch = slice(hh * Q_TILE, (hh + 1) * Q_TILE)
                    lse_h = lse_ref[h:h + 1, :]
                    pl_ = jnp.exp(_biased(s_loc[:, ch], tab_ref[h, pl.ds(t0, K_WIN), :], edge_masks) - lse_h)
                    pc_ = jnp.exp(s_ctx[:, ch] - lse_h)
                    dpl = dp_loc[:, ch]
                    dpc = dp_ctx[:, ch]
                    delta = jnp.sum(pl_ * dpl, axis=0, keepdims=True) + jnp.sum(pc_ * dpc, axis=0, keepdims=True)
                    dsl = pl_ * (dpl - delta)
                    dsc = pc_ * (dpc - delta)
                    tt_ref[h, pl.ds(t0, K_WIN), :] += dsl
                    p_loc.append(pl_.astype(MXU_DTYPE))
                    p_ctx.append(pc_.astype(MXU_DTYPE))
                    ds_loc.append(dsl.astype(MXU_DTYPE))
                    ds_ctx.append(dsc.astype(MXU_DTYPE))
                p_loc, p_ctx = jnp.concatenate(p_loc, axis=1), jnp.concatenate(p_ctx, axis=1)
                ds_loc, ds_ctx = jnp.concatenate(ds_loc, axis=1), jnp.concatenate(ds_ctx, axis=1)
                do_rows = _two_heads_on_rows(dop)
                q_rows = _two_heads_on_rows(qp)
                dv_acc[pl.ds(k0, K_WIN), cq] += jnp.dot(p_loc, do_rows, preferred_element_type=F32)
                dv_acc[pl.ds(s, CTX), cq] += jnp.dot(p_ctx, do_rows, preferred_element_type=F32)
                dk_acc[pl.ds(k0, K_WIN), cq] += jnp.dot(ds_loc, q_rows, preferred_element_type=F32)
                dk_acc[pl.ds(s, CTX), cq] += jnp.dot(ds_ctx, q_rows, preferred_element_type=F32)
                dq2 = (lax.dot_general(ds_loc, kw, _TN, preferred_element_type=F32)
                       + lax.dot_general(ds_ctx, kc, _TN, preferred_element_type=F32))
                dq_ref[:, cq] = (_pick_heads(dq2) * SCALE).astype(dq_ref.dtype)

        @pl.when(i == npair - 1)
        def _():
            def cp(c, carry):
                r0 = pl.multiple_of(c * ROW_TILE, ROW_TILE)
                dkv_ref[pl.ds(r0, ROW_TILE), 0:DA] = dk_acc[pl.ds(r0, ROW_TILE), :].astype(dkv_ref.dtype)
                dkv_ref[pl.ds(r0, ROW_TILE), DA:2 * DA] = dv_acc[pl.ds(r0, ROW_TILE), :].astype(dkv_ref.dtype)
                return carry

            lax.fori_loop(0, sa // ROW_TILE, cp, 0)

    qmap = lambda i: (jnp.minimum(i, npair - 1), 0)
    return _pallas(
        body, name="attn_bwd", grid=(npair + nzero,),
        in_specs=[pl.BlockSpec((Q_TILE, DA), qmap), pl.BlockSpec((Q_TILE, DA), qmap),
                  pl.BlockSpec((NH, Q_TILE), lambda i: (0, jnp.minimum(i, npair - 1))),
                  pl.BlockSpec(memory_space=pltpu.VMEM), pl.BlockSpec(memory_space=pltpu.VMEM)],
        out_specs=[pl.BlockSpec((Q_TILE, DA), lambda i: (i, 0)), pl.BlockSpec(memory_space=pltpu.VMEM),
                   pl.BlockSpec(memory_space=pltpu.VMEM)],
        out_shape=[_sds((sa, DA), MXU_DTYPE), _sds((sa, 2 * DA), MXU_DTYPE), _sds((NH, TAB_BLOCKS * GW, LANES), F32)],
        scratch_shapes=[pltpu.VMEM((sa, DA), F32)] * 2,
        semantics=("arbitrary",),
    )(qkv, dycat, lse, qkv, tab)


def _tile(n, prefs):
    for t in prefs:
        if n % t == 0:
            return t
    raise ValueError((n, prefs))


def _local_step(x, ctx, tgt, mod, mod_c, vec, w_in, late_weights, rpb_rev, early_grads=None):
    s = x.shape[0]
    sa = s + CTX
    ts = _tile(s, (1024, 512, 256))
    ts2 = _tile(s, (2048, 1024, 512, 256))
    tsa = _tile(sa, (1088, 640, 256))
    tsa2 = _tile(sa, (2176, 640, 256))
    sh1, sc1, gt1, sh2, sc2, gt2 = (mod[i:i + 1] for i in range(6))
    csh1, csc1 = mod_c[0:1], mod_c[1:2]
    act = MXU_DTYPE

    tab = _bias_table(rpb_rev)
    h_all = _rmsmod_fwd(x, ctx, vec["g_norm1"], sc1, sh1, csc1, csh1)
    w_in = w_in(h_all) if callable(w_in) else w_in
    qkv = _mm(h_all, w_in, mode="nn", m=sa, n=3 * DA, k=D, tm=tsa2, tn=512, tk=D, out_dtype=MXU_DTYPE, name="mm_qkv")
    ag = _mm(h_all, w_in, mode="nn", m=s, n=2 * DC, k=D, tm=ts2, tn=512, tk=D, out_dtype=F32, name="mm_ag", b_off=(0, 3))
    ycat, lse = _attn_fwd(qkv, tab, s)
    u1 = _conf_conv_fwd(ag, vec["conv_w"], vec["conv_b"])
    ycat = _conf_ln_fwd(u1, vec["ln_g"], vec["ln_b"], ycat)
    if callable(late_weights):
        w_out, ffn_weights = late_weights(ycat)
    else:
        w_out, ffn_weights = late_weights[0], late_weights[1:]
    y = _mm(ycat, w_out, mode="nn", m=s, n=D, k=D, tm=ts2, tn=512, tk=D, out_dtype=F32, name="mm_out")
    x1, h2 = _resid_rmsmod_fwd(x, y, gt1, vec["g_norm2"], sc2, sh2)
    w_up, w_down = ffn_weights(h2) if callable(ffn_weights) else ffn_weights
    u = _mm(h2, w_up, mode="nn", m=s, n=2 * DFF, k=D, tm=ts2, tn=512, tk=D, out_dtype=act, name="mm_up")
    f = _ffn_act_fwd(u, vec["ffn_conv_w"], vec["ffn_conv_b"])
    z = _mm(f, w_down, mode="nn", m=s, n=D, k=DFF, tm=ts, tn=D, tk=DFF, out_dtype=F32, name="mm_down")
    dx2, dz, loss, dgt2, dgf = _final_fwd_bwd(x1, z, gt2, vec["g_final"], tgt)

    df = _mm(dz, w_down, mode="nt", m=s, n=DFF, k=D, tm=ts, tn=DFF, tk=D, out_dtype=act, name="mm_down_dx")
    d_w_down = _mm(f, dz, mode="tn", m=DFF, n=D, k=s, tm=DFF, tn=D, tk=ts, out_dtype=F32, name="mm_down_dw")
    dug, duv, dfw_g, dfw_v, dfb_g, dfb_v = _ffn_act_bwd(u, df, vec["ffn_conv_w"], vec["ffn_conv_b"])
    dw_kw = dict(mode="tn", m=D, n=DFF, k=s, tm=D, tn=DFF, tk=ts, out_dtype=F32, out_total=(D, 2 * DFF))
    d_w_up = _mm(h2, dug, name="mm_up_dw_gate", **dw_kw)
    d_w_up = _mm(h2, duv, name="mm_up_dw_val", o_off=(0, 1), into=d_w_up, **dw_kw)
    if early_grads is not None:
        early_grads[0](d_w_up, d_w_down)
    dh2 = _mm([dug, duv], w_up, mode="nt", m=s, n=D, k=2 * DFF, tm=ts, tn=D, tk=2 * DFF, out_dtype=F32, name="mm_up_dx")
    sc2_b = sc2 if early_grads is None else sc2 + early_grads[1](dh2)
    dsh2, dsc2, dg2, dx1, dy, dgt1 = _rmsmod_bwd(x1, dh2, vec["g_norm2"], sc2_b, name="rmsmod2_bwd", add=dx2, resid=(gt1, y))
    dycat = _mm(dy, w_out, mode="nt", m=s, n=D, k=D, tm=ts2, tn=512, tk=D, out_dtype=F32, name="mm_out_dx")
    d_w_out = _mm(ycat, dy, mode="tn", m=D, n=D, k=s, tm=D, tn=D, tk=ts, out_dtype=F32, name="mm_out_dw")
    du1, dln_g, dln_b = _conf_ln_bwd(dycat, u1, vec["ln_g"], vec["ln_b"])
    da, dg, dconv_w, dconv_b = _conf_conv_bwd(ag, du1, vec["conv_w"], sa)
    dq, dkv, tt = _attn_bwd(qkv, tab, lse, dycat, s)
    drpb_rev = _rpb_grad(tt)
    d_pieces = [dq, dkv, da, dg]
    dh = _mm(d_pieces, w_in, mode="nt", m=sa, n=D, k=NIN, tm=tsa, tn=D, tk=NIN, out_dtype=F32, name="mm_in_dx")
    d_w_in = _mm(h_all, d_pieces, mode="tn", m=D, n=NIN, k=sa, tm=D, tn=NIN, tk=tsa, out_dtype=F32, name="mm_in_dw")
    dsh1, dsc1, dg1, grad_x = _rmsmod_bwd(x, dh, vec["g_norm1"], sc1, name="rmsmod1_bwd", add=dx1)
    dcsh1, dcsc1, dg1c = _rmsmod_bwd(ctx, dh, vec["g_norm1"], csc1, name="rmsmod1_ctx_bwd", dh_row_off=s // ROW_TILE)

    small = dict(
        dmod=[dsh1, dsc1, dgt1, dsh2, dsc2, dgt2], dmod_c=[dcsh1, dcsc1],
        g_norm1=[dg1, dg1c], g_norm2=dg2, g_final=dgf, conv_b=dconv_b, ln_g=dln_g, ln_b=dln_b, conv_w=dconv_w,
        ffn_conv_w=[dfw_g, dfw_v], ffn_conv_b=[dfb_g, dfb_v], rpb_rev=drpb_rev,
    )
    return loss, grad_x, d_w_in, d_w_out, d_w_up, d_w_down, small


N_CHIPS = 4
HBM = pl.BlockSpec(memory_space=pl.ANY)
BIG = {"w_in": ("col", (D, NIN)), "w_out": ("row", (D, D)), "w_up": ("col", (D, 2 * DFF)), "w_down": ("row", (DFF, D))}
BIG_NAMES = tuple(BIG)
LATE_NAMES = ("w_out", "w_up", "w_down")


def _shard_shape(name):
    kind, (r, c) = BIG[name]
    return (r, c // N_CHIPS) if kind == "col" else (r // N_CHIPS, c)


def _half_rows(name):
    return _shard_shape(name)[0] // 2


def _place():
    x, y, c = lax.axis_index("x"), lax.axis_index("y"), lax.axis_index("c")
    others = [(1 - x, y), (x, 1 - y), (1 - x, 1 - y)]
    return x, y, c, 2 * x + y, (x, y, 1 - c), others


def _whole_region(ref, name, chip, half):
    kind, _ = BIG[name]
    r, c = _shard_shape(name)
    if kind == "col":
        return ref.at[pl.ds(half * (r // 2), r // 2), pl.ds(chip * c, c)]
    return ref.at[pl.ds(chip * r + half * (r // 2), r // 2), :]


def _remote(src, dst, send_sem, recv_sem, to):
    return pltpu.make_async_remote_copy(src_ref=src, dst_ref=dst, send_sem=send_sem, recv_sem=recv_sem,
                                        device_id=to, device_id_type=MESH)


def _gather_small(v, name):
    m_per, n = v.shape

    def body(x_ref, out_ref, send_sems, recv_sems, local_sem):
        x, y, c, _, sibling, others = _place()
        me = (x, y, c)

        def rows(px, py, pc):
            return out_ref.at[pl.ds((4 * px + 2 * py + pc) * m_per, m_per), :]

        def copy(k, block, to, src=None):
            return _remote(rows(*block) if src is None else src, rows(*block), send_sems.at[k], recv_sems.at[k], to)

        mine = pltpu.make_async_copy(x_ref, rows(*me), local_sem)
        mine.start()
        first = [copy(0, me, sibling, src=x_ref)]
        first += [copy(1 + j, me, (*chip, c), src=x_ref) for j, chip in enumerate(others)]
        for cp in first:
            cp.start()
        passed = [copy(4 + j, (*chip, c), sibling) for j, chip in enumerate(others)]
        for j, chip in enumerate(others):
            copy(1 + j, (*chip, c), me).wait_recv()
            passed[j].start()
        copy(0, sibling, me).wait_recv()
        for j, chip in enumerate(others):
            copy(4 + j, (*chip, 1 - c), me).wait_recv()
        for cp in first + passed:
            cp.wait_send()
        mine.wait()

    return pl.pallas_call(
        body, name=name, out_shape=_sds((8 * m_per, n), v.dtype),
        in_specs=[pl.BlockSpec(memory_space=pltpu.VMEM)], out_specs=pl.BlockSpec(memory_space=pltpu.VMEM),
        scratch_shapes=[pltpu.SemaphoreType.DMA((7,)), pltpu.SemaphoreType.DMA((7,)), pltpu.SemaphoreType.DMA],
    )(v)


def _cast_into_whole(name, shard, chip):
    kind, whole = BIG[name]
    r, c = shard.shape
    if kind == "col":
        tr = 256
        o_spec = pl.BlockSpec((tr, c), lambda i, ch: (i, ch[0]))
    else:
        tr = _tile(r, (128, 352))
        o_spec = pl.BlockSpec((tr, c), lambda i, ch: (ch[0] * (r // tr) + i, 0))

    def body(ch_ref, x_ref, o_ref):
        del ch_ref
        o_ref[...] = x_ref[...].astype(o_ref.dtype)

    return _pallas(body, name="cast_" + name, prefetch=1, grid=(r // tr,),
                   in_specs=[pl.BlockSpec((tr, c), lambda i, ch: (i, 0))], out_specs=o_spec,
                   out_shape=_sds(whole, MXU_DTYPE), semantics=("parallel",))(chip, shard)


SEM = pl.BlockSpec(memory_space=pltpu.SEMAPHORE)
IN_HBM = pl.BlockSpec(memory_space=pltpu.HBM)
DATAFLOW = pltpu.SideEffectType.DATAFLOW_SIDE_EFFECTING


def _keep_in_hbm(a):
    return pltpu.with_memory_space_constraint(a, pltpu.HBM)


def _gather_start(wholes, names, after, tag):
    nw = len(names)
    ns = 2 * 3 * nw

    def body(*refs):
        ins = refs[:nw]
        sems = refs[nw + 1:nw + 1 + ns]
        token = refs[2 * nw + ns + 1]
        _, _, c, chip, _, others = _place()
        for w, name in enumerate(names):
            mine = _whole_region(ins[w], name, chip, c)
            for t, (ox, oy) in enumerate(others):
                k = 2 * (3 * w + t)
                _remote(mine, mine, sems[k], sems[k + 1], (ox, oy, c)).start()
        token[...] = jnp.zeros_like(token)

    res = pl.pallas_call(
        body, name="gather_" + tag + "_start",
        out_shape=(*[pltpu.SemaphoreType.DMA(())] * ns, *[pltpu.HBM(a.shape, a.dtype) for a in wholes], _sds((8, LANES), F32)),
        in_specs=[IN_HBM] * nw + [pl.BlockSpec(memory_space=pl.ANY)],
        out_specs=(*[SEM] * ns, *[IN_HBM] * nw, pl.BlockSpec(memory_space=pltpu.VMEM)),
        input_output_aliases={i: ns + i for i in range(nw)},
        compiler_params=pltpu.CompilerParams(has_side_effects=DATAFLOW),
    )(*[_keep_in_hbm(a) for a in wholes], after)
    return list(res[:ns]), list(res[ns:ns + nw]), res[ns + nw]


def _gather_wait(sems, wholes, names, after, tag):
    nw = len(names)
    ns = len(sems)

    def body(*refs):
        ins = refs[:nw]
        sem_refs = refs[nw:nw + ns]
        _, _, c, chip, _, others = _place()
        for w, name in enumerate(names):
            mine = _whole_region(ins[w], name, chip, c)
            for t, (ox, oy) in enumerate(others):
                got = _whole_region(ins[w], name, 2 * ox + oy, c)
                k = 2 * (3 * w + t)
                cp = _remote(mine, got, sem_refs[k], sem_refs[k + 1], (ox, oy, c))
                cp.wait_send()
                cp.wait_recv()

    return pl.pallas_call(
        body, name="gather_" + tag + "_wait",
        out_shape=tuple(pltpu.HBM(a.shape, a.dtype) for a in wholes),
        in_specs=[IN_HBM] * nw + [SEM] * ns + [pl.BlockSpec(memory_space=pl.ANY)], out_specs=tuple([IN_HBM] * nw),
        input_output_aliases={i: i for i in range(nw)},
        compiler_params=pltpu.CompilerParams(has_side_effects=DATAFLOW),
    )(*wholes, *sems, after)


def _forward_halves(wholes, names, tag):
    nw = len(names)

    def body(*refs):
        outs = refs[nw:2 * nw]
        send_sems, recv_sems = refs[2 * nw:]
        _, _, c, _, sibling, others = _place()
        sends = []
        for w, name in enumerate(names):
            for t, (ox, oy) in enumerate(others):
                got = _whole_region(outs[w], name, 2 * ox + oy, c)
                cp = _remote(got, got, send_sems.at[w, t], recv_sems.at[w, t], sibling)
                cp.start()
                sends.append(cp)
        for w, name in enumerate(names):
            for t, (ox, oy) in enumerate(others):
                got = _whole_region(outs[w], name, 2 * ox + oy, 1 - c)
                _remote(got, got, send_sems.at[w, t], recv_sems.at[w, t], sibling).wait_recv()
        for cp in sends:
            cp.wait_send()

    return pl.pallas_call(
        body, name="gather_" + tag + "_forward",
        out_shape=[_sds(a.shape, a.dtype) for a in wholes],
        in_specs=[HBM] * nw, out_specs=[HBM] * nw,
        input_output_aliases={i: i for i in range(nw)},
        scratch_shapes=[pltpu.SemaphoreType.DMA((nw, 3)), pltpu.SemaphoreType.DMA((nw, 3))],
    )(*wholes)


def _forward_start(wholes, names, tag):
    nw = len(names)
    ns = 2 * 3 * nw

    def body(*refs):
        ins = refs[:nw]
        sems = refs[nw:nw + ns]
        token = refs[2 * nw + ns]
        _, _, c, _, sibling, others = _place()
        for w, name in enumerate(names):
            for t, (ox, oy) in enumerate(others):
                got = _whole_region(ins[w], name, 2 * ox + oy, c)
                k = 2 * (3 * w + t)
                _remote(got, got, sems[k], sems[k + 1], sibling).start()
        token[...] = jnp.zeros_like(token)

    res = pl.pallas_call(
        body, name="gather_" + tag + "_forward_start",
        out_shape=(*[pltpu.SemaphoreType.DMA(())] * ns, *[pltpu.HBM(a.shape, a.dtype) for a in wholes], _sds((8, LANES), F32)),
        in_specs=[IN_HBM] * nw, out_specs=(*[SEM] * ns, *[IN_HBM] * nw, pl.BlockSpec(memory_space=pltpu.VMEM)),
        input_output_aliases={i: ns + i for i in range(nw)},
        compiler_params=pltpu.CompilerParams(has_side_effects=DATAFLOW),
    )(*[_keep_in_hbm(a) for a in wholes])
    return list(res[:ns]), list(res[ns:ns + nw]), res[ns + nw]


def _forward_wait(sems, wholes, names, after, tag):
    nw = len(names)
    ns = len(sems)

    def body(*refs):
        ins = refs[:nw]
        sem_refs = refs[nw:nw + ns]
        _, _, c, _, sibling, others = _place()
        for w, name in enumerate(names):
            for t, (ox, oy) in enumerate(others):
                k = 2 * (3 * w + t)
                cp = _remote(_whole_region(ins[w], name, 2 * ox + oy, c), _whole_region(ins[w], name, 2 * ox + oy, 1 - c),
                             sem_refs[k], sem_refs[k + 1], sibling)
                cp.wait_send()
                cp.wait_recv()

    return pl.pallas_call(
        body, name="gather_" + tag + "_forward_wait",
        out_shape=tuple(pltpu.HBM(a.shape, a.dtype) for a in wholes),
        in_specs=[IN_HBM] * nw + [SEM] * ns + [pl.BlockSpec(memory_space=pl.ANY)], out_specs=tuple([IN_HBM] * nw),
        input_output_aliases={i: i for i in range(nw)},
        compiler_params=pltpu.CompilerParams(has_side_effects=DATAFLOW),
    )(*wholes, *sems, after)


def _compact_shape(name, dtype):
    kind, (r, c) = BIG[name]
    return _sds((r // 2, c), dtype)


def _swap_pairs(ins, outs, names, c):
    pairs = []
    for w, name in enumerate(names):
        kind, _ = BIG[name]
        half = _half_rows(name)
        if kind == "col":
            pairs.append((ins[w].at[pl.ds((1 - c) * half, half), :], outs[w]))
        else:
            pairs += [(ins[w].at[pl.ds(jj * 2 * half + (1 - c) * half, half), :], outs[w].at[pl.ds(jj * half, half), :])
                      for jj in range(N_CHIPS)]
    return pairs


def _n_swap_copies(names):
    return sum(1 if BIG[n][0] == "col" else N_CHIPS for n in names)


def _swap_start(grads, names, label):
    nw = len(names)
    ns = 2 * _n_swap_copies(names)

    def body(*refs):
        ins, lands = refs[:nw], refs[nw:2 * nw]
        sems = refs[2 * nw:2 * nw + ns]
        token = refs[4 * nw + ns]
        _, _, c, _, sibling, _ = _place()
        for k, (src, dst) in enumerate(_swap_pairs(ins, lands, names, c)):
            _remote(src, dst, sems[2 * k], sems[2 * k + 1], sibling).start()
        token[...] = jnp.zeros_like(token)

    lands = [_keep_in_hbm(lax.empty(_compact_shape(n, F32).shape, F32)) for n in names]
    res = pl.pallas_call(
        body, name=label,
        out_shape=(*[pltpu.SemaphoreType.DMA(())] * ns, *[pltpu.HBM(a.shape, a.dtype) for a in grads],
                   *[pltpu.HBM(a.shape, a.dtype) for a in lands], _sds((8, LANES), F32)),
        in_specs=[IN_HBM] * (2 * nw),
        out_specs=(*[SEM] * ns, *[IN_HBM] * (2 * nw), pl.BlockSpec(memory_space=pltpu.VMEM)),
        input_output_aliases={i: ns + i for i in range(2 * nw)},
        compiler_params=pltpu.CompilerParams(has_side_effects=DATAFLOW),
    )(*[_keep_in_hbm(a) for a in grads], *lands)
    return list(res[:ns]), list(res[ns:ns + nw]), list(res[ns + nw:ns + 2 * nw]), res[ns + 2 * nw]


def _swap_wait(sems, grads, lands, names, after, label):
    nw = len(names)
    ns = len(sems)

    def body(*refs):
        ins, land_refs = refs[:nw], refs[nw:2 * nw]
        sem_refs = refs[2 * nw:2 * nw + ns]
        _, _, c, _, sibling, _ = _place()
        for k, (src, dst) in enumerate(_swap_pairs(ins, land_refs, names, c)):
            cp = _remote(src, dst, sem_refs[2 * k], sem_refs[2 * k + 1], sibling)
            cp.wait_send()
            cp.wait_recv()

    res = pl.pallas_call(
        body, name=label,
        out_shape=tuple(pltpu.HBM(a.shape, a.dtype) for a in (*grads, *lands)),
        in_specs=[IN_HBM] * (2 * nw) + [SEM] * ns + [pl.BlockSpec(memory_space=pl.ANY)],
        out_specs=tuple([IN_HBM] * (2 * nw)),
        input_output_aliases={i: i for i in range(2 * nw)},
        compiler_params=pltpu.CompilerParams(has_side_effects=DATAFLOW),
    )(*grads, *lands, *sems, after)
    return list(res[:nw]), list(res[nw:])


def _add_halves(name, grad, got, core):
    kind, (r, c) = BIG[name]
    half = _half_rows(name)
    if kind == "col":
        t = 128
        grid = (half // t,)
        g_spec = pl.BlockSpec((t, c), lambda i, cr: (cr[0] * (half // t) + i, 0))
        o_spec = pl.BlockSpec((t, c), lambda i, cr: (i, 0))
    else:
        t = half
        grid = (N_CHIPS,)
        g_spec = pl.BlockSpec((t, c), lambda i, cr: (2 * i + cr[0], 0))
        o_spec = pl.BlockSpec((t, c), lambda i, cr: (i, 0))

    def body(c_ref, g_ref, b_ref, o_ref):
        del c_ref
        o_ref[...] = (g_ref[...] + b_ref[...]).astype(o_ref.dtype)

    return pl.pallas_call(
        body, name="grad_add_" + name,
        grid_spec=pltpu.PrefetchScalarGridSpec(num_scalar_prefetch=1, grid=grid, in_specs=[g_spec, o_spec], out_specs=o_spec),
        out_shape=_compact_shape(name, BF16),
        compiler_params=pltpu.CompilerParams(dimension_semantics=("parallel",), vmem_limit_bytes=VMEM_LIMIT),
    )(core, grad, got)


def _piece(ref, name, chip):
    kind, _ = BIG[name]
    r, c = _shard_shape(name)
    if kind == "col":
        return ref.at[:, pl.ds(chip * c, c)]
    return ref.at[pl.ds(chip * (r // 2), r // 2), :]


def _landing_shape(name):
    r, c = _shard_shape(name)
    return (N_CHIPS - 1, r // 2, c)


def _exchange_start(parts, names, label):
    nw = len(names)
    ns = 2 * 3 * nw

    def body(*refs):
        ins, lands = refs[:nw], refs[nw:2 * nw]
        sems = refs[2 * nw:2 * nw + ns]
        token = refs[4 * nw + ns]
        _, _, c, _, _, others = _place()
        for w, name in enumerate(names):
            for t, (ox, oy) in enumerate(others):
                k = 2 * (3 * w + t)
                _remote(_piece(ins[w], name, 2 * ox + oy), lands[w].at[t], sems[k], sems[k + 1], (ox, oy, c)).start()
        token[...] = jnp.zeros_like(token)

    lands = [_keep_in_hbm(lax.empty(_landing_shape(n), BF16)) for n in names]
    res = pl.pallas_call(
        body, name=label,
        out_shape=(*[pltpu.SemaphoreType.DMA(())] * ns, *[pltpu.HBM(a.shape, a.dtype) for a in parts],
                   *[pltpu.HBM(a.shape, a.dtype) for a in lands], _sds((8, LANES), F32)),
        in_specs=[IN_HBM] * (2 * nw),
        out_specs=(*[SEM] * ns, *[IN_HBM] * (2 * nw), pl.BlockSpec(memory_space=pltpu.VMEM)),
        input_output_aliases={i: ns + i for i in range(2 * nw)},
        compiler_params=pltpu.CompilerParams(has_side_effects=DATAFLOW),
    )(*[_keep_in_hbm(a) for a in parts], *lands)
    return list(res[:ns]), list(res[ns:ns + nw]), list(res[ns + nw:ns + 2 * nw]), res[ns + 2 * nw]


def _exchange_wait(sems, parts, lands, names, after, label):
    nw = len(names)
    ns = len(sems)

    def body(*refs):
        ins, land_refs = refs[:nw], refs[nw:2 * nw]
        sem_refs = refs[2 * nw:2 * nw + ns]
        _, _, c, _, _, others = _place()
        for w, name in enumerate(names):
            for t, (ox, oy) in enumerate(others):
                k = 2 * (3 * w + t)
                cp = _remote(_piece(ins[w], name, 2 * ox + oy), land_refs[w].at[t], sem_refs[k], sem_refs[k + 1], (ox, oy, c))
                cp.wait_send()
                cp.wait_recv()

    res = pl.pallas_call(
        body, name=label,
        out_shape=tuple(pltpu.HBM(a.shape, a.dtype) for a in (*parts, *lands)),
        in_specs=[IN_HBM] * (2 * nw) + [SEM] * ns + [pl.BlockSpec(memory_space=pl.ANY)],
        out_specs=tuple([IN_HBM] * (2 * nw)),
        input_output_aliases={i: i for i in range(2 * nw)},
        compiler_params=pltpu.CompilerParams(has_side_effects=DATAFLOW),
    )(*parts, *lands, *sems, after)
    return list(res[:nw]), list(res[nw:])


def _sum_chips(name, part, got, chip):
    kind, _ = BIG[name]
    _, r, c = got.shape
    t = _tile(r, (128, 352))
    if kind == "col":
        own = pl.BlockSpec((t, c), lambda i, ch: (i, ch[0]))
    else:
        own = pl.BlockSpec((t, c), lambda i, ch: (ch[0] * (r // t) + i, 0))

    def body(ch_ref, p_ref, g_ref, o_ref):
        del ch_ref
        acc = p_ref[...].astype(F32)
        for j in range(N_CHIPS - 1):
            acc = acc + g_ref[j].astype(F32)
        o_ref[...] = acc

    return _pallas(
        body, name="grad_sum_" + name, prefetch=1, grid=(r // t,),
        in_specs=[own, pl.BlockSpec((N_CHIPS - 1, t, c), lambda i, ch: (0, i, 0))],
        out_specs=pl.BlockSpec((t, c), lambda i, ch: (i, 0)),
        out_shape=_sds((r, c), F32), semantics=("parallel",),
    )(chip, part, got)


def _send_halves(sums, label, after):
    nw = len(sums)

    def body(*refs):
        ins, outs = refs[:nw], refs[nw + 1:2 * nw + 1]
        send_sems, recv_sems = refs[2 * nw + 1:]
        _, _, _, _, sibling, _ = _place()
        copies = [_remote(ins[w], outs[w], send_sems.at[w], recv_sems.at[w], sibling) for w in range(nw)]
        for cp in copies:
            cp.start()
        for cp in copies:
            cp.wait()

    return pl.pallas_call(
        body, name=label,
        out_shape=[_sds(a.shape, a.dtype) for a in sums],
        in_specs=[HBM] * (nw + 1), out_specs=[HBM] * nw,
        scratch_shapes=[pltpu.SemaphoreType.DMA((nw,)), pltpu.SemaphoreType.DMA((nw,))],
    )(*sums, after)


EARLY_GRADS = ("w_up", "w_down")
LAST_GRADS = ("w_in", "w_out")


def _reduce_finish(started, names, after, chip, tag):
    sems, parts, lands, _ = started
    parts, lands = _exchange_wait(sems, parts, lands, names, after, "grad_exchange_wait_" + tag)
    return [_sum_chips(n, parts[i], lands[i], chip) for i, n in enumerate(names)]


HI = lax.Precision.HIGHEST
MOD_COLS = 6 * D // N_CHIPS
COND_ROWS = 16


def _silu(v):
    return v * _sigmoid(v)


GATHER_ROWS = 48
FFW_COLS = 2 * DFF // N_CHIPS
CONV_COLS = DC // N_CHIPS


def _pack_cond(c, ffn_w, conv_w):
    def body(c_ref, f_ref, w_ref, o_ref):
        o_ref[...] = jnp.zeros_like(o_ref)
        o_ref[0:1, 0:D] = c_ref[...]
        o_ref[8:11, :] = f_ref[...]
        o_ref[16:16 + CW, 0:CONV_COLS] = w_ref[...]

    return _pallas(body, name="pack_cond", out_shape=_sds((GATHER_ROWS, FFW_COLS), F32))(c, ffn_w, conv_w)


def _unpack_cond(got, c_ctx):
    def body(g_ref, c_ref, cond_ref, f_ref, w_ref):
        cond_ref[...] = jnp.zeros_like(cond_ref)
        for d in range(8):
            cond_ref[d:d + 1, :] = g_ref[d * GATHER_ROWS:d * GATHER_ROWS + 1, 0:D]
        cond_ref[8:9, :] = c_ref[...]
        for j in range(N_CHIPS):
            r0 = 2 * j * GATHER_ROWS
            f_ref[:, j * FFW_COLS:(j + 1) * FFW_COLS] = g_ref[r0 + 8:r0 + 11, :]
            w_ref[:, j * CONV_COLS:(j + 1) * CONV_COLS] = g_ref[r0 + 16:r0 + 16 + CW, 0:CONV_COLS]

    return _pallas(body, name="unpack_cond",
                   out_shape=[_sds((COND_ROWS, D), F32), _sds((3, 2 * DFF), F32), _sds((CW, DC), F32)])(got, c_ctx)


def _chip_cols(rows, width):
    return pl.BlockSpec((rows, width), lambda i, ch: (0, ch[0]))


def _whole(shape):
    return pl.BlockSpec(shape, lambda i, ch: (0,) * len(shape))


def _mod_shard(cond, w_mod, b_mod, chip):
    def body(ch_ref, c_ref, w_ref, b_ref, o_ref):
        del ch_ref
        o_ref[...] = jnp.dot(_silu(c_ref[...]), w_ref[...], preferred_element_type=F32, precision=HI) + b_ref[...]

    return _pallas(body, name="mod_fwd", prefetch=1, grid=(1,),
                   in_specs=[_whole((COND_ROWS, D)), _whole((D, MOD_COLS)), _chip_cols(1, MOD_COLS)],
                   out_specs=_whole((COND_ROWS, MOD_COLS)),
                   out_shape=_sds((COND_ROWS, MOD_COLS), F32))(chip, cond, w_mod, b_mod)


def _unpack_mod(mods, dev):
    def body(dev_ref, m_ref, me_ref, c_ref):
        rowi = lax.broadcasted_iota(jnp.int32, (COND_ROWS, MOD_COLS), 0)
        mine, ctx = [], []
        for j in range(N_CHIPS):
            blk = m_ref[2 * j * COND_ROWS:(2 * j + 1) * COND_ROWS, :]
            mine.append(jnp.sum(jnp.where(rowi == dev_ref[0], blk, 0.0), axis=0, keepdims=True))
            ctx.append(blk[8:9, :])
        mine = jnp.concatenate(mine, axis=1)
        ctx = jnp.concatenate(ctx, axis=1)
        for k in range(6):
            me_ref[k:k + 1, :] = mine[:, k * D:(k + 1) * D]
        for k in range(2):
            c_ref[k:k + 1, :] = ctx[:, k * D:(k + 1) * D]

    return _pallas(body, name="unpack_mod", prefetch=1, grid=(1,),
                   in_specs=[_whole(mods.shape)], out_specs=[_whole((6, D)), _whole((2, D))],
                   out_shape=[_sds((6, D), F32), _sds((2, D), F32)])(dev, mods)


def _mod_weight_grad(cond, dmod_all, chip):
    def body(ch_ref, c_ref, d_ref, o_ref):
        del ch_ref
        o_ref[...] = lax.dot_general(_silu(c_ref[...]), d_ref[...], _TN, preferred_element_type=F32, precision=HI)

    return _pallas(body, name="mod_weight_grad", prefetch=1, grid=(1,),
                   in_specs=[_whole((COND_ROWS, D)), _chip_cols(COND_ROWS, MOD_COLS)], out_specs=_whole((D, MOD_COLS)),
                   out_shape=_sds((D, MOD_COLS), F32))(chip, cond, dmod_all)


def _cond_grad_partial(dmod_all, w_mod, chip):
    def body(ch_ref, d_ref, w_ref, o_ref):
        del ch_ref
        o_ref[...] = lax.dot_general(d_ref[...], w_ref[...], (((1,), (1,)), ((), ())), preferred_element_type=F32, precision=HI)

    return _pallas(body, name="cond_grad_partial", prefetch=1, grid=(1,),
                   in_specs=[pl.BlockSpec((8, MOD_COLS), lambda i, ch: (1, ch[0])), _whole((D, MOD_COLS))],
                   out_specs=_whole((8, D)), out_shape=_sds((8, D), F32))(chip, dmod_all, w_mod)


def _adam_math(w, g, m, v):
    nm = ADAM_B1 * m + (1.0 - ADAM_B1) * g
    nv = ADAM_B2 * v + (1.0 - ADAM_B2) * (g * g)
    c1 = 1.0 - ADAM_B1 ** ADAM_STEP
    c2 = 1.0 - ADAM_B2 ** ADAM_STEP
    return -ADAM_LR * ((nm / c1) / (jnp.sqrt(nv / c2) + ADAM_EPS) + ADAM_WD * w), nm, nv


def _cond_update(parts, c_ctx, m, v):
    def body(p_ref, c_ref, m_ref, v_ref, g_ref, d_ref, nm_ref, nv_ref):
        tot = p_ref[0:1, :]
        for j in range(1, N_CHIPS):
            tot = tot + p_ref[16 * j:16 * j + 1, :]
        cv = c_ref[...]
        sg = _sigmoid(cv)
        g = tot * (sg * (1.0 + cv * (1.0 - sg)))
        g_ref[...] = g
        d_ref[...], nm_ref[...], nv_ref[...] = _adam_math(cv, g, m_ref[...], v_ref[...])

    return _pallas(body, name="cond_update", out_shape=[_sds((1, D), F32)] * 4)(parts, c_ctx, m, v)


def _adamw(w, g, m, v, name):
    r, c = w.shape
    t = _tile(r, (128,)) if r % 128 == 0 and r > 128 else r

    def body(w_ref, g_ref, m_ref, v_ref, d_ref, nm_ref, nv_ref):
        d_ref[...], nm_ref[...], nv_ref[...] = _adam_math(w_ref[...], g_ref[...], m_ref[...], v_ref[...])

    blk = pl.BlockSpec((t, c), lambda i: (i, 0))
    return _pallas(body, name=name, grid=(r // t,), in_specs=[blk] * 4, out_specs=[blk] * 3,
                   out_shape=[_sds((r, c), F32)] * 3, semantics=("parallel",))(w, g, m, v)


def _adamw_cols(w, g_all, m, v, chip, name):
    r, c = w.shape

    def body(ch_ref, w_ref, g_ref, m_ref, v_ref, go_ref, d_ref, nm_ref, nv_ref):
        del ch_ref
        g = g_ref[...]
        go_ref[...] = g
        d_ref[...], nm_ref[...], nv_ref[...] = _adam_math(w_ref[...], g, m_ref[...], v_ref[...])

    return _pallas(body, name=name, prefetch=1, grid=(1,),
                   in_specs=[_whole((r, c)), _chip_cols(r, c), _whole((r, c)), _whole((r, c))],
                   out_specs=[_whole((r, c))] * 4, out_shape=[_sds((r, c), F32)] * 4)(chip, w, g_all, m, v)


def _adamw_halves(name, w, own, other, m, v, core, after):
    r, c = w.shape
    half = r // 2
    t = _tile(half, (128, 352))
    nh = half // t

    def pick(mine):
        def index(i, cr):
            first = cr[0] if mine else 1 - cr[0]
            return (jnp.clip(i - first * nh, 0, nh - 1), 0)
        return pl.BlockSpec((t, c), index)

    def body(c_ref, w_ref, own_ref, oth_ref, m_ref, v_ref, after_ref, g_ref, d_ref, nm_ref, nv_ref):
        del after_ref
        g = jnp.where(pl.program_id(0) // nh == c_ref[0], own_ref[...], oth_ref[...])
        g_ref[...] = g
        d_ref[...], nm_ref[...], nv_ref[...] = _adam_math(w_ref[...], g, m_ref[...], v_ref[...])

    blk = pl.BlockSpec((t, c), lambda i, cr: (i, 0))
    return _pallas(body, name="adamw_" + name, prefetch=1, grid=(2 * nh,),
                   in_specs=[blk, pick(True), pick(False), blk, blk, pl.BlockSpec(memory_space=pl.ANY)], out_specs=[blk] * 4,
                   out_shape=[_sds((r, c), F32)] * 4, semantics=("parallel",))(core, w, own, other, m, v, after)


WEIGHTS = ("c_ctx", "w_mod", "b_mod", "g_norm1", "w_in", "rpb", "conv_w", "conv_b", "ln_g", "ln_b", "w_out", "g_norm2",
           "w_up", "ffn_conv_w", "ffn_conv_b", "w_down", "g_final")
PACK = (("dmod", 6 * D), ("dmod_c", 2 * D), ("g_norm1", D), ("g_norm1_ctx", D), ("g_norm2", D), ("g_final", D),
        ("conv_b", DC), ("ln_g", DC), ("ln_b", DC), ("ffn_conv_b", 2 * DFF), ("ffn_conv_w", 3 * 2 * DFF),
        ("conv_w", CW * DC), ("rpb_rev", NH * 16 * LANES), ("loss", LANES))
PACK_OFF = {}
_o = 0
for _n, _w in PACK:
    PACK_OFF[_n] = (_o, _w)
    _o += _w
PACK_N = -(-_o // (8 * LANES)) * (8 * LANES)
VECTORS = {"b_mod": (6 * D, ("dmod", "dmod_c")), "g_norm1": (D, ("g_norm1", "g_norm1_ctx")), "conv_b": (DC, ("conv_b",)),
           "ln_g": (DC, ("ln_g",)), "ln_b": (DC, ("ln_b",)), "g_norm2": (D, ("g_norm2",)),
           "ffn_conv_b": (2 * DFF, ("ffn_conv_b",)), "g_final": (D, ("g_final",))}
RPB_ROWS = NH * (2 * NA_ROWS - 1)
RPB_COLS = 4 * NA_ROWS - 1


def _pack_small(parts, after):
    arrs, places = [], []
    for name, _ in PACK:
        off, width = PACK_OFF[name]
        group = parts[name]
        rows = group[0].shape[0]
        row_w = sum(a.shape[1] for a in group)
        assert rows * row_w == width, (name, rows, row_w, width)
        col = 0
        for a in group:
            arrs.append(a)
            places.append([off + k * row_w + col for k in range(rows)])
            col += a.shape[1]

    def body(*refs):
        o_ref = refs[-1]
        o_ref[:, _o:PACK_N] = jnp.zeros((1, PACK_N - _o), F32)
        for ref, offs in zip(refs, places):
            n = ref.shape[1]
            for k, off in enumerate(offs):
                o_ref[:, off:off + n] = ref[k:k + 1, :]

    vmem = pl.BlockSpec(memory_space=pltpu.VMEM)
    return _pallas(body, name="pack_small_grads", out_shape=_sds((1, PACK_N), F32),
                   in_specs=[vmem] * len(arrs) + [pl.BlockSpec(memory_space=pl.ANY)], out_specs=vmem)(*arrs, after)


def _small_update(packs, w, m, v):
    names = list(VECTORS)

    def body(*refs):
        it = iter(refs)
        p_ref = next(it)
        wmv = {n: (next(it), next(it), next(it)) for n in names}
        outs = {n: (next(it), next(it), next(it), next(it)) for n in names}
        dmod_ref, cw_ref, fw_ref, rpb_ref, loss_ref = next(it), next(it), next(it), next(it), next(it)

        def total(name):
            off, width = PACK_OFF[name]
            acc = p_ref[0:1, off:off + width]
            for d in range(1, 8):
                acc = acc + p_ref[d:d + 1, off:off + width]
            return acc

        for n in names:
            width, segs = VECTORS[n]
            g = total(segs[0])
            if len(segs) > 1:
                extra = total(segs[1])
                ew = extra.shape[1]
                g = g + extra if ew == width else jnp.concatenate([g[:, :ew] + extra, g[:, ew:]], axis=1)
            w_ref, m_ref, v_ref = wmv[n]
            g_ref, d_ref, nm_ref, nv_ref = outs[n]
            g_ref[...] = g
            d_ref[...], nm_ref[...], nv_ref[...] = _adam_math(w_ref[...], g, m_ref[...], v_ref[...])

        o_dmod = PACK_OFF["dmod"][0]
        dmod_ref[...] = jnp.zeros_like(dmod_ref)
        dmod_ref[0:8, :] = p_ref[:, o_dmod:o_dmod + 6 * D]
        dmod_ref[8:9, 0:2 * D] = total("dmod_c")
        for ref, name, rows in ((cw_ref, "conv_w", CW), (fw_ref, "ffn_conv_w", 3), (rpb_ref, "rpb_rev", NH * 16)):
            flat = total(name)
            n = ref.shape[1]
            for k in range(rows):
                ref[k:k + 1, :] = flat[:, k * n:(k + 1) * n]
        loss_ref[...] = total("loss")

    ins = [packs] + [a[n] for n in names for a in (w, m, v)]
    out_shape = [_sds((1, VECTORS[n][0]), F32) for n in names for _ in range(4)]
    out_shape += [_sds((COND_ROWS, 6 * D), F32), _sds((CW, DC), F32), _sds((3, 2 * DFF), F32), _sds((NH * 16, LANES), F32),
                  _sds((1, LANES), F32)]
    res = _pallas(body, name="small_update", out_shape=out_shape)(*ins)
    per = {n: tuple(res[4 * i:4 * i + 4]) for i, n in enumerate(names)}
    return (per, *res[4 * len(names):])


def _rpb_update(rev, w, m, v):
    def body(r_ref, w_ref, m_ref, v_ref, g_ref, d_ref, nm_ref, nv_ref):
        li = lax.broadcasted_iota(jnp.int32, (LANES, LANES), 0)
        co = lax.broadcasted_iota(jnp.int32, (LANES, LANES), 1)
        lane_of_co0 = GW - 1 + RPB_COLS // 2
        unflip = jnp.where((li == lane_of_co0 - co) & (co < RPB_COLS), 1.0, 0.0).astype(F32)
        g_all = jnp.dot(r_ref[...], unflip, preferred_element_type=F32, precision=HI)
        nr = 2 * NA_ROWS - 1
        for h in range(NH):
            rows = slice(h * nr, (h + 1) * nr)
            g = g_all[h * 16:h * 16 + nr, 0:RPB_COLS]
            g_ref[rows, :] = g
            d_ref[rows, :], nm_ref[rows, :], nv_ref[rows, :] = _adam_math(w_ref[rows, :], g, m_ref[rows, :], v_ref[rows, :])

    return _pallas(body, name="rpb_update", out_shape=[_sds((RPB_ROWS, RPB_COLS), F32)] * 4)(rev, w, m, v)


def kernel(x, c, ctx, c_ctx, w_mod, b_mod, g_norm1, w_in, rpb, conv_w, conv_b, ln_g, ln_b, w_out, g_norm2, w_up, ffn_conv_w, ffn_conv_b, w_down, g_final, loss_target, m_c_ctx, m_w_mod, m_b_mod, m_g_norm1, m_w_in, m_rpb, m_conv_w, m_conv_b, m_ln_g, m_ln_b, m_w_out, m_g_norm2, m_w_up, m_ffn_conv_w, m_ffn_conv_b, m_w_down, m_g_final, v_c_ctx, v_w_mod, v_b_mod, v_g_norm1, v_w_in, v_rpb, v_conv_w, v_conv_b, v_ln_g, v_ln_b, v_w_out, v_g_norm2, v_w_up, v_ffn_conv_w, v_ffn_conv_b, v_w_down, v_g_final):
    w = dict(c_ctx=c_ctx, w_mod=w_mod, b_mod=b_mod, g_norm1=g_norm1, w_in=w_in, rpb=rpb, conv_w=conv_w, conv_b=conv_b,
             ln_g=ln_g, ln_b=ln_b, w_out=w_out, g_norm2=g_norm2, w_up=w_up, ffn_conv_w=ffn_conv_w, ffn_conv_b=ffn_conv_b,
             w_down=w_down, g_final=g_final)
    mom = dict(c_ctx=m_c_ctx, w_mod=m_w_mod, b_mod=m_b_mod, g_norm1=m_g_norm1, w_in=m_w_in, rpb=m_rpb, conv_w=m_conv_w,
               conv_b=m_conv_b, ln_g=m_ln_g, ln_b=m_ln_b, w_out=m_w_out, g_norm2=m_g_norm2, w_up=m_w_up,
               ffn_conv_w=m_ffn_conv_w, ffn_conv_b=m_ffn_conv_b, w_down=m_w_down, g_final=m_g_final)
    var = dict(c_ctx=v_c_ctx, w_mod=v_w_mod, b_mod=v_b_mod, g_norm1=v_g_norm1, w_in=v_w_in, rpb=v_rpb, conv_w=v_conv_w,
               conv_b=v_conv_b, ln_g=v_ln_g, ln_b=v_ln_b, w_out=v_w_out, g_norm2=v_g_norm2, w_up=v_w_up,
               ffn_conv_w=v_ffn_conv_w, ffn_conv_b=v_ffn_conv_b, w_down=v_w_down, g_final=v_g_final)
    xi, yi, ci = lax.axis_index("x"), lax.axis_index("y"), lax.axis_index("c")
    dev = (4 * xi + 2 * yi + ci).astype(jnp.int32).reshape(1)
    chip = (2 * xi + yi).astype(jnp.int32).reshape(1)
    core = ci.astype(jnp.int32).reshape(1)
    c_ctx2 = c_ctx.reshape(1, D)
    g_final2 = g_final.reshape(1, D)
    mom["g_final"], var["g_final"] = m_g_final.reshape(1, D), v_g_final.reshape(1, D)

    got = _gather_small(_pack_cond(c, ffn_conv_w[0], conv_w[0]), "gather_cond")
    cond, ffn_w_all, conv_w_all = _unpack_cond(got, c_ctx2)

    mods = _gather_small(_mod_shard(cond, w_mod[0], b_mod, chip), "gather_mod")
    mod_me, mod_c = _unpack_mod(mods, dev)

    shards = {n: _cast_into_whole(n, w[n][0], chip) for n in BIG_NAMES}
    sems_in, first, token_in = _gather_start([shards["w_in"]], ("w_in",), mod_me, "w_in")
    sems, late, token = _gather_start([shards[n] for n in LATE_NAMES], LATE_NAMES, token_in, "late")
    mod_me = mod_me + token[0:1, 0:1]

    def w_in_all(after):
        arrived = _gather_wait(sems_in, first, ("w_in",), after, "w_in")
        return _forward_halves(list(arrived), ("w_in",), "w_in")[0]

    def late_weights(after):
        arrived = list(_gather_wait(sems, late, LATE_NAMES, after, "late"))
        (w_out_all,) = _forward_halves(arrived[:1], LATE_NAMES[:1], "w_out")
        fsems, passing, _ = _forward_start(arrived[1:], LATE_NAMES[1:], "ffn")
        return w_out_all, lambda after2: _forward_wait(fsems, passing, LATE_NAMES[1:], after2, "ffn")

    rpb_rev = jnp.pad(rpb[0][:, :, ::-1], ((0, 0), (0, 1), (48, LANES - 48 - RPB_COLS))).reshape(NH * 16, LANES)
    vec = dict(g_norm1=g_norm1, g_norm2=g_norm2, g_final=g_final2, conv_w=conv_w_all, conv_b=conv_b, ln_g=ln_g, ln_b=ln_b,
               ffn_conv_w=ffn_w_all, ffn_conv_b=ffn_conv_b)
    started = []

    def begin_early(d_up, d_down):
        started.append(_swap_start([d_up, d_down], EARLY_GRADS, "grad_swap_start_early"))

    def carry_on_early(after):
        sems_, grads_, lands_, _ = started.pop()
        grads_, lands_ = _swap_wait(sems_, grads_, lands_, EARLY_GRADS, after, "grad_swap_wait_early")
        parts_ = [_add_halves(n, grads_[i], lands_[i], core) for i, n in enumerate(EARLY_GRADS)]
        started.append(_exchange_start(parts_, EARLY_GRADS, "grad_exchange_start_early"))
        return started[0][3][0:1, 0:1]

    loss_p, grad_x, d_in, d_out, d_up, d_down, small = _local_step(
        x[0], ctx[0], loss_target[0], mod_me, mod_c, vec, w_in_all, late_weights, rpb_rev, (begin_early, carry_on_early))

    out = {}
    sems_, grads_, lands_, _ = _swap_start([d_in, d_out], LAST_GRADS, "grad_swap_start_last")
    early_own = _reduce_finish(started[0], EARLY_GRADS, grad_x, chip, "early")
    behind_swap = small["rpb_rev"][0:1, 0:1] + small["g_norm1"][1][0:1, 0:1] + early_own[0][0:1, 0:1]
    grads_, lands_ = _swap_wait(sems_, grads_, lands_, LAST_GRADS, behind_swap, "grad_swap_wait_last")
    parts_ = [_add_halves(n, grads_[i], lands_[i], core) for i, n in enumerate(LAST_GRADS)]
    last_started = _exchange_start(parts_, LAST_GRADS, "grad_exchange_start_last")
    early_other = _send_halves(early_own, "grad_send_early", after=last_started[3])
    for i, n in enumerate(EARLY_GRADS):
        out[n] = _adamw_halves(n, w[n][0], early_own[i], early_other[i], mom[n][0], var[n][0], core, early_other[i])
    behind_early = out[EARLY_GRADS[0]][1][0:1, 0:1] + out[EARLY_GRADS[1]][1][0:1, 0:1]

    parts = dict(dmod=small["dmod"], dmod_c=small["dmod_c"], g_norm1=[small["g_norm1"][0]], g_norm1_ctx=[small["g_norm1"][1]],
                 g_norm2=[small["g_norm2"]], g_final=[small["g_final"]], conv_b=[small["conv_b"]], ln_g=[small["ln_g"]],
                 ln_b=[small["ln_b"]], ffn_conv_b=small["ffn_conv_b"], ffn_conv_w=small["ffn_conv_w"],
                 conv_w=[small["conv_w"]], rpb_rev=[small["rpb_rev"]], loss=[loss_p])
    pack = _pack_small(parts, after=behind_early).reshape(8, PACK_N // 8)
    packs = _gather_small(pack, "gather_small_grads").reshape(8, PACK_N)
    w2 = dict(w, g_final=g_final2)
    per, dmod_all, g_conv_w_all, g_ffn_w_all, g_rpb_rev, loss_row = _small_update(packs, w2, mom, var)

    out.update(per)
    out["c_ctx"] = _cond_update(
        _gather_small(_cond_grad_partial(dmod_all, w_mod[0], chip), "gather_cond_grad"),
        c_ctx2, m_c_ctx.reshape(1, D), v_c_ctx.reshape(1, D))
    g_w_mod = _mod_weight_grad(cond, dmod_all, chip)
    out["w_mod"] = (g_w_mod, *_adamw(w_mod[0], g_w_mod, m_w_mod[0], v_w_mod[0], "adamw_w_mod"))
    behind = out["w_mod"][1][0:1, 0:1] + out["c_ctx"][1][0:1, 0:1]
    last_own = _reduce_finish(last_started, LAST_GRADS, behind, chip, "last")
    last_other = _send_halves(last_own, "grad_send_last", after=last_own[0])
    for i, n in enumerate(LAST_GRADS):
        out[n] = _adamw_halves(n, w[n][0], last_own[i], last_other[i], mom[n][0], var[n][0], core, last_other[i])
    out["conv_w"] = _adamw_cols(conv_w[0], g_conv_w_all, m_conv_w[0], v_conv_w[0], chip, "adamw_conv_w")
    out["ffn_conv_w"] = _adamw_cols(ffn_conv_w[0], g_ffn_w_all, m_ffn_conv_w[0], v_ffn_conv_w[0], chip, "adamw_ffn_conv_w")
    flat = lambda a: a.reshape(RPB_ROWS, RPB_COLS)
    out["rpb"] = _rpb_update(g_rpb_rev, flat(rpb), flat(m_rpb), flat(v_rpb))

    res = [[out[n][k].reshape(w[n].shape) for n in WEIGHTS] for k in range(4)]
    return (loss_row[0, 0], grad_x[None], *res[0], *res[1], *res[2], *res[3])
```

```python
import jax
import jax.numpy as jnp
from jax import lax
from jax.experimental import pallas as pl
from jax.experimental.pallas import tpu as pltpu

F32 = jnp.float32
BF16 = jnp.bfloat16
MXU_DTYPE = jnp.bfloat16

D = 1024
CTX = 256
GW = 64
DA = 512
NH = 8
HD = 64
DC = 512
CW = 31
DFF = 2816
NIN = 3 * DA + 2 * DC
EPS = 1e-6
SCALE = HD ** -0.5
NEG = -1e30
NA_ROWS = 8
PAIR_ROWS = NA_ROWS + 1
TAB_BLOCKS = 17
LANES = 128
VMEM_LIMIT = 56 * 1024 * 1024

ADAM_LR = 0.001
ADAM_B1 = 0.9
ADAM_B2 = 0.999
ADAM_EPS = 1e-08
ADAM_WD = 0.01
ADAM_STEP = 10

MESH = pl.DeviceIdType.MESH


def _pallas(body, *, name, semantics=None, vmem=VMEM_LIMIT, prefetch=0, **kw):
    params = dict(vmem_limit_bytes=vmem)
    if semantics is not None:
        params["dimension_semantics"] = semantics
    if prefetch:
        kw["grid_spec"] = pltpu.PrefetchScalarGridSpec(
            num_scalar_prefetch=prefetch, grid=kw.pop("grid"), in_specs=kw.pop("in_specs"), out_specs=kw.pop("out_specs"),
            scratch_shapes=kw.pop("scratch_shapes", ()))
    return pl.pallas_call(body, name=name, compiler_params=pltpu.CompilerParams(**params), **kw)


def _sds(shape, dtype):
    return jax.ShapeDtypeStruct(shape, dtype)


def _vec_spec(n):
    return pl.BlockSpec((1, n), lambda *_: (0, 0))


def _colsum8(x):
    t, n = x.shape
    return jnp.sum(x.reshape(t // 8, 8, n), axis=0)


def _sigmoid(x):
    return 0.5 * jnp.tanh(0.5 * x) + 0.5


def _mm(a, b, *, mode, m, n, k, tm, tn, tk, out_dtype, name, a_off=(0, 0), b_off=(0, 0),
        out_total=None, o_off=(0, 0), into=None):
    a_list = list(a) if isinstance(a, (list, tuple)) else [a]
    b_list = list(b) if isinstance(b, (list, tuple)) else [b]
    assert m % tm == 0 and n % tn == 0 and k % tk == 0, (name, m, n, k, tm, tn, tk)
    gi, gj, nk = m // tm, n // tn, k // tk
    dims = {"nn": (((1,), (0,)), ((), ())), "nt": (((1,), (1,)), ((), ())), "tn": (((0,), (0,)), ((), ()))}[mode]

    if len(a_list) > 1:
        assert mode != "tn" and nk == 1 and sum(x.shape[1] for x in a_list) == k
        a_specs = [pl.BlockSpec((tm, x.shape[1]), lambda i, j, kk: (i, 0)) for x in a_list]
    elif mode == "tn":
        a_specs = [pl.BlockSpec((tk, tm), lambda i, j, kk: (kk + a_off[0], i + a_off[1]))]
    else:
        a_specs = [pl.BlockSpec((tm, tk), lambda i, j, kk: (i + a_off[0], kk + a_off[1]))]
    if len(b_list) > 1:
        assert mode == "tn" and gj == 1 and sum(x.shape[1] for x in b_list) == n
        b_specs = [pl.BlockSpec((tk, x.shape[1]), lambda i, j, kk: (kk, 0)) for x in b_list]
    elif mode == "nt":
        b_specs = [pl.BlockSpec((tn, tk), lambda i, j, kk: (j + b_off[0], kk + b_off[1]))]
    else:
        b_specs = [pl.BlockSpec((tk, tn), lambda i, j, kk: (kk + b_off[0], j + b_off[1]))]

    na, nb = len(a_list), len(b_list)
    in_place = nk > 1 and out_dtype == F32
    n_in = na + nb + (into is not None)

    def body(*refs):
        a_refs, b_refs, o_ref = refs[:na], refs[na:na + nb], refs[n_in]
        acc = o_ref if in_place else (refs[n_in + 1] if nk > 1 else None)
        kk = pl.program_id(2)

        def whole(piece_refs):
            vals = [r[...].astype(MXU_DTYPE) for r in piece_refs]
            return vals[0] if len(vals) == 1 else jnp.concatenate(vals, axis=1)

        p = lax.dot_general(whole(a_refs), whole(b_refs), dims, preferred_element_type=F32)
        if nk == 1:
            o_ref[...] = p.astype(out_dtype)
            return

        @pl.when(kk == 0)
        def _():
            acc[...] = p

        @pl.when(kk > 0)
        def _():
            acc[...] += p

        if not in_place:
            @pl.when(kk == nk - 1)
            def _():
                o_ref[...] = acc[...].astype(out_dtype)

    ins = [*a_list, *b_list]
    in_specs = a_specs + b_specs
    extra = {}
    if into is not None:
        extra["input_output_aliases"] = {len(ins): 0}
        ins.append(into)
        in_specs.append(pl.BlockSpec(memory_space=pl.ANY))
    return _pallas(
        body, name=name, grid=(gi, gj, nk), in_specs=in_specs,
        out_specs=pl.BlockSpec((tm, tn), lambda i, j, kk: (i + o_off[0], j + o_off[1])),
        out_shape=_sds(out_total or (m, n), out_dtype),
        scratch_shapes=[pltpu.VMEM((tm, tn), F32)] if nk > 1 and not in_place else [],
        semantics=("parallel", "parallel", "arbitrary"), **extra,
    )(*ins)


ROW_TILE = 256


def _rmsmod_fwd(x, ctx, g, sc, sh, csc, csh):
    s = x.shape[0]
    nt = s // ROW_TILE
    assert ctx.shape[0] == ROW_TILE

    def body(x_ref, c_ref, g_ref, sc_ref, sh_ref, csc_ref, csh_ref, o_ref):
        is_ctx = pl.program_id(0) == nt
        xv = jnp.where(is_ctx, c_ref[...], x_ref[...])
        scv = jnp.where(is_ctx, csc_ref[...], sc_ref[...])
        shv = jnp.where(is_ctx, csh_ref[...], sh_ref[...])
        r = lax.rsqrt(jnp.mean(xv * xv, axis=-1, keepdims=True) + EPS)
        y = xv * r * g_ref[...]
        o_ref[...] = (y * (1.0 + scv) + shv).astype(o_ref.dtype)

    return _pallas(
        body, name="rmsmod1_fwd", grid=(nt + 1,),
        in_specs=[pl.BlockSpec((ROW_TILE, D), lambda i: (jnp.minimum(i, nt - 1), 0)),
                  pl.BlockSpec((ROW_TILE, D), lambda i: (0, 0))] + [_vec_spec(D)] * 5,
        out_specs=pl.BlockSpec((ROW_TILE, D), lambda i: (i, 0)),
        out_shape=_sds((s + CTX, D), MXU_DTYPE),
        semantics=("arbitrary",),
    )(x, ctx, g, sc, sh, csc, csh)


def _resid_rmsmod_fwd(x, y, gt, g, sc, sh):
    s = x.shape[0]

    def body(x_ref, y_ref, gt_ref, g_ref, sc_ref, sh_ref, x1_ref, h_ref):
        x1 = x_ref[...] + gt_ref[...] * y_ref[...]
        x1_ref[...] = x1
        r = lax.rsqrt(jnp.mean(x1 * x1, axis=-1, keepdims=True) + EPS)
        h_ref[...] = ((x1 * r * g_ref[...]) * (1.0 + sc_ref[...]) + sh_ref[...]).astype(h_ref.dtype)

    row = pl.BlockSpec((ROW_TILE, D), lambda i: (i, 0))
    return _pallas(
        body, name="resid_rmsmod2_fwd", grid=(s // ROW_TILE,),
        in_specs=[row, row] + [_vec_spec(D)] * 4,
        out_specs=[row, row],
        out_shape=[_sds((s, D), F32), _sds((s, D), MXU_DTYPE)],
        semantics=("parallel",),
    )(x, y, gt, g, sc, sh)


def _final_fwd_bwd(x1, z, gt2, gf, tgt):
    s = x1.shape[0]
    nt = s // ROW_TILE

    def body(x1_ref, z_ref, gt_ref, gf_ref, t_ref, dx2_ref, dz_ref, loss_ref, dgt_ref, dgf_ref, a_loss, a_gt, a_gf):
        i = pl.program_id(0)

        @pl.when(i == 0)
        def _():
            a_loss[...] = jnp.zeros_like(a_loss)
            a_gt[...] = jnp.zeros_like(a_gt)
            a_gf[...] = jnp.zeros_like(a_gf)

        zv = z_ref[...]
        gt = gt_ref[...]
        gf_ = gf_ref[...]
        x2 = x1_ref[...] + gt * zv
        r = lax.rsqrt(jnp.mean(x2 * x2, axis=-1, keepdims=True) + EPS)
        xn = x2 * r
        e = xn * gf_ - t_ref[...]
        a_loss[...] += _colsum8(e * e)
        dyo = e * (1.0 / D)
        a_gf[...] += _colsum8(dyo * xn)
        gdy = gf_ * dyo
        dx2 = r * gdy - xn * (r * r) * jnp.mean(x2 * gdy, axis=-1, keepdims=True)
        dx2_ref[...] = dx2
        dz_ref[...] = (gt * dx2).astype(dz_ref.dtype)
        a_gt[...] += _colsum8(dx2 * zv)

        @pl.when(i == nt - 1)
        def _():
            tot = jnp.sum(jnp.sum(a_loss[...], axis=0, keepdims=True), axis=1, keepdims=True) * (0.5 / D)
            loss_ref[...] = jnp.broadcast_to(tot, loss_ref.shape)
            dgt_ref[...] = jnp.sum(a_gt[...], axis=0, keepdims=True)
            dgf_ref[...] = jnp.sum(a_gf[...], axis=0, keepdims=True)

    row = pl.BlockSpec((ROW_TILE, D), lambda i: (i, 0))
    return _pallas(
        body, name="final_norm_loss", grid=(nt,),
        in_specs=[row, row, _vec_spec(D), _vec_spec(D), row],
        out_specs=[row, row, _vec_spec(LANES), _vec_spec(D), _vec_spec(D)],
        out_shape=[_sds((s, D), F32), _sds((s, D), MXU_DTYPE), _sds((1, LANES), F32), _sds((1, D), F32), _sds((1, D), F32)],
        scratch_shapes=[pltpu.VMEM((8, D), F32)] * 3,
        semantics=("arbitrary",),
    )(x1, z, gt2, gf, tgt)


def _rmsmod_bwd(xin, dh, g, sc, *, name, dh_row_off=0, add=None, resid=None):
    s = xin.shape[0]
    nt = s // ROW_TILE
    want_dx = add is not None
    assert resid is None or want_dx

    def body(*refs):
        it = iter(refs)
        x_ref, dh_ref, g_ref, sc_ref = next(it), next(it), next(it), next(it)
        add_ref = next(it) if want_dx else None
        gt_ref, y_ref = (next(it), next(it)) if resid is not None else (None, None)
        dsh_ref, dsc_ref, dg_ref = next(it), next(it), next(it)
        dx_ref = next(it) if want_dx else None
        dy_ref, dgt_ref = (next(it), next(it)) if resid is not None else (None, None)
        a_sh, a_sc, a_g = next(it), next(it), next(it)
        a_gt = next(it) if resid is not None else None
        i = pl.program_id(0)

        @pl.when(i == 0)
        def _():
            a_sh[...] = jnp.zeros_like(a_sh)
            a_sc[...] = jnp.zeros_like(a_sc)
            a_g[...] = jnp.zeros_like(a_g)
            if a_gt is not None:
                a_gt[...] = jnp.zeros_like(a_gt)

        xv = x_ref[...]
        dhv = dh_ref[...]
        gv = g_ref[...]
        r = lax.rsqrt(jnp.mean(xv * xv, axis=-1, keepdims=True) + EPS)
        xn = xv * r
        a_sh[...] += _colsum8(dhv)
        a_sc[...] += _colsum8(dhv * (xn * gv))
        dn = dhv * (1.0 + sc_ref[...])
        a_g[...] += _colsum8(dn * xn)
        if want_dx:
            gdn = gv * dn
            dx = add_ref[...] + r * gdn - xn * (r * r) * jnp.mean(xv * gdn, axis=-1, keepdims=True)
            dx_ref[...] = dx
            if resid is not None:
                dy_ref[...] = (gt_ref[...] * dx).astype(dy_ref.dtype)
                a_gt[...] += _colsum8(dx * y_ref[...])

        @pl.when(i == nt - 1)
        def _():
            dsh_ref[...] = jnp.sum(a_sh[...], axis=0, keepdims=True)
            dsc_ref[...] = jnp.sum(a_sc[...], axis=0, keepdims=True)
            dg_ref[...] = jnp.sum(a_g[...], axis=0, keepdims=True)
            if a_gt is not None:
                dgt_ref[...] = jnp.sum(a_gt[...], axis=0, keepdims=True)

    row = pl.BlockSpec((ROW_TILE, D), lambda i: (i, 0))
    ins = [xin, dh, g, sc]
    in_specs = [row, pl.BlockSpec((ROW_TILE, D), lambda i: (i + dh_row_off, 0)), _vec_spec(D), _vec_spec(D)]
    out_specs = [_vec_spec(D)] * 3
    out_shape = [_sds((1, D), F32)] * 3
    scratch = [pltpu.VMEM((8, D), F32)] * 3
    if want_dx:
        ins.append(add)
        in_specs.append(row)
        out_specs.append(row)
        out_shape.append(_sds((s, D), F32))
    if resid is not None:
        ins += [resid[0], resid[1]]
        in_specs += [_vec_spec(D), row]
        out_specs += [row, _vec_spec(D)]
        out_shape += [_sds((s, D), MXU_DTYPE), _sds((1, D), F32)]
        scratch.append(pltpu.VMEM((8, D), F32))
    return _pallas(body, name=name, grid=(nt,), in_specs=in_specs, out_specs=out_specs, out_shape=out_shape,
                   scratch_shapes=scratch, semantics=("arbitrary",))(*ins)


FF_TILE = 128
FF_CHUNK = 128
HALO = 8


def _shift3(pad_ref, r0, ch):
    return tuple(pad_ref[pl.ds(r0 + HALO + d, ch), :] for d in (-1, 0, 1))


def _fill_padded(pad_ref, src_ref, s, ch, halo):
    zeros = jnp.zeros((halo, pad_ref.shape[1]), F32)
    pad_ref[0:halo, :] = zeros
    pad_ref[s + halo:s + 2 * halo, :] = zeros

    def cp(c, carry):
        r0 = pl.multiple_of(c * ch, ch)
        pad_ref[pl.ds(r0 + halo, ch), :] = src_ref[pl.ds(r0, ch), :].astype(F32)
        return carry

    lax.fori_loop(0, s // ch, cp, 0)


def _ffn_act_fwd(u, w, b):
    s = u.shape[0]
    nj = DFF // FF_TILE
    ch = FF_CHUNK

    def body(ug_ref, uv_ref, wg_ref, wv_ref, bg_ref, bv_ref, f_ref, gpad, vpad):
        _fill_padded(gpad, ug_ref, s, ch, HALO)
        _fill_padded(vpad, uv_ref, s, ch, HALO)

        def conv(pad, w_ref, b_ref, r0):
            prev, cur, nxt = _shift3(pad, r0, ch)
            return w_ref[0:1, :] * prev + w_ref[1:2, :] * cur + w_ref[2:3, :] * nxt + b_ref[...]

        def step(c, carry):
            r0 = pl.multiple_of(c * ch, ch)
            gc = conv(gpad, wg_ref, bg_ref, r0)
            vc = conv(vpad, wv_ref, bv_ref, r0)
            f_ref[pl.ds(r0, ch), :] = (gc * _sigmoid(gc) * vc).astype(f_ref.dtype)
            return carry

        lax.fori_loop(0, s // ch, step, 0)

    col = lambda off: pl.BlockSpec((s, FF_TILE), lambda j: (0, j + off))
    wsp = lambda off: pl.BlockSpec((3, FF_TILE), lambda j: (0, j + off))
    bsp = lambda off: pl.BlockSpec((1, FF_TILE), lambda j: (0, j + off))
    return _pallas(
        body, name="ffn_act_fwd", grid=(nj,),
        in_specs=[col(0), col(nj), wsp(0), wsp(nj), bsp(0), bsp(nj)],
        out_specs=col(0), out_shape=_sds((s, DFF), MXU_DTYPE),
        scratch_shapes=[pltpu.VMEM((s + 2 * HALO, FF_TILE), F32)] * 2,
        semantics=("parallel",),
    )(u, u, w, w, b, b)


def _ffn_act_bwd(u, df, w, b):
    s = u.shape[0]
    nj = DFF // FF_TILE
    ch = FF_CHUNK

    def body(ug_ref, uv_ref, df_ref, wg_ref, wv_ref, bg_ref, bv_ref,
             dug_ref, duv_ref, dwg_ref, dwv_ref, dbg_ref, dbv_ref, gpad, vpad, dgpad, dvpad, acc):
        _fill_padded(gpad, ug_ref, s, ch, HALO)
        _fill_padded(vpad, uv_ref, s, ch, HALO)
        zeros = jnp.zeros((HALO, FF_TILE), F32)
        for p in (dgpad, dvpad):
            p[0:HALO, :] = zeros
            p[s + HALO:s + 2 * HALO, :] = zeros
        acc[...] = jnp.zeros_like(acc)

        def step(c, carry):
            r0 = pl.multiple_of(c * ch, ch)
            gs = _shift3(gpad, r0, ch)
            vs = _shift3(vpad, r0, ch)
            gc = wg_ref[0:1, :] * gs[0] + wg_ref[1:2, :] * gs[1] + wg_ref[2:3, :] * gs[2] + bg_ref[...]
            vc = wv_ref[0:1, :] * vs[0] + wv_ref[1:2, :] * vs[1] + wv_ref[2:3, :] * vs[2] + bv_ref[...]
            sg = _sigmoid(gc)
            dfv = df_ref[pl.ds(r0, ch), :].astype(F32)
            dgc = dfv * vc * (sg * (1.0 + gc * (1.0 - sg)))
            dvc = dfv * (gc * sg)
            dgpad[pl.ds(r0 + HALO, ch), :] = dgc
            dvpad[pl.ds(r0 + HALO, ch), :] = dvc
            for t in range(3):
                acc[8 * t:8 * t + 8, :] += _colsum8(dgc * gs[t])
                acc[24 + 8 * t:32 + 8 * t, :] += _colsum8(dvc * vs[t])
            acc[48:56, :] += _colsum8(dgc)
            acc[56:64, :] += _colsum8(dvc)
            return carry

        lax.fori_loop(0, s // ch, step, 0)

        def step2(c, carry):
            r0 = pl.multiple_of(c * ch, ch)
            for pad, w_ref, o_ref in ((dgpad, wg_ref, dug_ref), (dvpad, wv_ref, duv_ref)):
                prev, cur, nxt = _shift3(pad, r0, ch)
                o_ref[pl.ds(r0, ch), :] = (w_ref[0:1, :] * nxt + w_ref[1:2, :] * cur + w_ref[2:3, :] * prev).astype(o_ref.dtype)
            return carry

        lax.fori_loop(0, s // ch, step2, 0)
        for t in range(3):
            dwg_ref[t:t + 1, :] = jnp.sum(acc[8 * t:8 * t + 8, :], axis=0, keepdims=True)
            dwv_ref[t:t + 1, :] = jnp.sum(acc[24 + 8 * t:32 + 8 * t, :], axis=0, keepdims=True)
        dbg_ref[...] = jnp.sum(acc[48:56, :], axis=0, keepdims=True)
        dbv_ref[...] = jnp.sum(acc[56:64, :], axis=0, keepdims=True)

    col = lambda off: pl.BlockSpec((s, FF_TILE), lambda j: (0, j + off))
    wsp = lambda off: pl.BlockSpec((3, FF_TILE), lambda j: (0, j + off))
    bsp = lambda off: pl.BlockSpec((1, FF_TILE), lambda j: (0, j + off))
    return _pallas(
        body, name="ffn_act_bwd", grid=(nj,),
        in_specs=[col(0), col(nj), col(0), wsp(0), wsp(nj), bsp(0), bsp(nj)],
        out_specs=[col(0), col(0), wsp(0), wsp(0), bsp(0), bsp(0)],
        out_shape=[_sds((s, DFF), MXU_DTYPE)] * 2 + [_sds((3, DFF), F32)] * 2 + [_sds((1, DFF), F32)] * 2,
        scratch_shapes=[pltpu.VMEM((s + 2 * HALO, FF_TILE), F32)] * 4 + [pltpu.VMEM((64, FF_TILE), F32)],
        semantics=("parallel",),
    )(u, u, df, w, w, b, b)


CONV_CHUNK = 64
CONV_HALO = 16


def _tap(pad_ref, r0, k):
    return pad_ref[pl.ds(r0 + CONV_HALO - CW // 2 + k, CONV_CHUNK), :]


def _glu_into(pad_ref, a_ref, g_ref, s):
    zeros = jnp.zeros((CONV_HALO, LANES), F32)
    pad_ref[0:CONV_HALO, :] = zeros
    pad_ref[s + CONV_HALO:s + 2 * CONV_HALO, :] = zeros

    def cp(c, carry):
        r0 = pl.multiple_of(c * ROW_TILE, ROW_TILE)
        pad_ref[pl.ds(r0 + CONV_HALO, ROW_TILE), :] = a_ref[pl.ds(r0, ROW_TILE), :] * _sigmoid(g_ref[pl.ds(r0, ROW_TILE), :])
        return carry

    lax.fori_loop(0, s // ROW_TILE, cp, 0)


def _conf_conv_fwd(ag, conv_w, conv_b):
    s = ag.shape[0]
    nc = DC // LANES

    def body(a_ref, g_ref, w_ref, b_ref, o_ref, upad):
        _glu_into(upad, a_ref, g_ref, s)

        def step(c, carry):
            r0 = pl.multiple_of(c * CONV_CHUNK, CONV_CHUNK)
            acc = jnp.broadcast_to(b_ref[...], (CONV_CHUNK, LANES))
            for k in range(CW):
                acc = acc + w_ref[k:k + 1, :] * _tap(upad, r0, k)
            o_ref[pl.ds(r0, CONV_CHUNK), :] = acc
            return carry

        lax.fori_loop(0, s // CONV_CHUNK, step, 0)

    col = lambda off: pl.BlockSpec((s, LANES), lambda c: (0, c + off))
    return _pallas(
        body, name="conf_conv_fwd", grid=(nc,),
        in_specs=[col(0), col(nc), pl.BlockSpec((CW, LANES), lambda c: (0, c)), pl.BlockSpec((1, LANES), lambda c: (0, c))],
        out_specs=col(0), out_shape=_sds((s, DC), F32),
        scratch_shapes=[pltpu.VMEM((s + 2 * CONV_HALO, LANES), F32)],
        semantics=("parallel",),
    )(ag, ag, conv_w, conv_b)


def _ln_stats(x):
    mu = jnp.mean(x, axis=-1, keepdims=True)
    xc = x - mu
    var = jnp.mean(xc * xc, axis=-1, keepdims=True)
    rstd = lax.rsqrt(var + EPS)
    return xc * rstd, rstd


def _conf_ln_fwd(u1, ln_g, ln_b, ycat):
    s = u1.shape[0]

    def body(u_ref, g_ref, b_ref, ycat_ref, o_ref):
        del ycat_ref
        xhat, _ = _ln_stats(u_ref[...])
        y = xhat * g_ref[...] + b_ref[...]
        o_ref[...] = (y * _sigmoid(y)).astype(o_ref.dtype)

    return _pallas(
        body, name="conf_ln_fwd", grid=(s // ROW_TILE,),
        in_specs=[pl.BlockSpec((ROW_TILE, DC), lambda i: (i, 0)), _vec_spec(DC), _vec_spec(DC),
                  pl.BlockSpec(memory_space=pl.ANY)],
        out_specs=pl.BlockSpec((ROW_TILE, DC), lambda i: (i, 1)),
        out_shape=_sds(ycat.shape, ycat.dtype),
        input_output_aliases={3: 0},
        semantics=("parallel",),
    )(u1, ln_g, ln_b, ycat)


def _conf_ln_bwd(dycat, u1, ln_g, ln_b):
    s = u1.shape[0]
    nt = s // ROW_TILE

    def body(dy_ref, u_ref, g_ref, b_ref, du_ref, dg_ref, db_ref, a_g, a_b):
        i = pl.program_id(0)

        @pl.when(i == 0)
        def _():
            a_g[...] = jnp.zeros_like(a_g)
            a_b[...] = jnp.zeros_like(a_b)

        xhat, rstd = _ln_stats(u_ref[...])
        gv = g_ref[...]
        y = xhat * gv + b_ref[...]
        sg = _sigmoid(y)
        dyl = dy_ref[...] * (sg * (1.0 + y * (1.0 - sg)))
        a_g[...] += _colsum8(dyl * xhat)
        a_b[...] += _colsum8(dyl)
        dxh = dyl * gv
        du_ref[...] = rstd * (dxh - jnp.mean(dxh, axis=-1, keepdims=True)
                              - xhat * jnp.mean(dxh * xhat, axis=-1, keepdims=True))

        @pl.when(i == nt - 1)
        def _():
            dg_ref[...] = jnp.sum(a_g[...], axis=0, keepdims=True)
            db_ref[...] = jnp.sum(a_b[...], axis=0, keepdims=True)

    return _pallas(
        body, name="conf_ln_bwd", grid=(nt,),
        in_specs=[pl.BlockSpec((ROW_TILE, DC), lambda i: (i, 1)), pl.BlockSpec((ROW_TILE, DC), lambda i: (i, 0)),
                  _vec_spec(DC), _vec_spec(DC)],
        out_specs=[pl.BlockSpec((ROW_TILE, DC), lambda i: (i, 0)), _vec_spec(DC), _vec_spec(DC)],
        out_shape=[_sds((s, DC), F32), _sds((1, DC), F32), _sds((1, DC), F32)],
        scratch_shapes=[pltpu.VMEM((8, DC), F32)] * 2,
        semantics=("arbitrary",),
    )(dycat, u1, ln_g, ln_b)


def _conf_conv_bwd(ag, du1, conv_w, rows_out):
    s = ag.shape[0]
    nc = DC // LANES

    def body(a_ref, g_ref, d_ref, w_ref, da_ref, dg_ref, dw_ref, db_ref, upad, dpad, acc):
        _glu_into(upad, a_ref, g_ref, s)
        _fill_padded(dpad, d_ref, s, ROW_TILE, CONV_HALO)
        acc[...] = jnp.zeros_like(acc)

        def step(c, carry):
            r0 = pl.multiple_of(c * CONV_CHUNK, CONV_CHUNK)
            dcur = dpad[pl.ds(r0 + CONV_HALO, CONV_CHUNK), :]
            du0 = jnp.zeros((CONV_CHUNK, LANES), F32)
            for k in range(CW):
                du0 = du0 + w_ref[k:k + 1, :] * _tap(dpad, r0, CW - 1 - k)
                acc[8 * k:8 * k + 8, :] += _colsum8(dcur * _tap(upad, r0, k))
            acc[8 * CW:8 * CW + 8, :] += _colsum8(dcur)
            av = a_ref[pl.ds(r0, CONV_CHUNK), :]
            sg = _sigmoid(g_ref[pl.ds(r0, CONV_CHUNK), :])
            da_ref[pl.ds(r0, CONV_CHUNK), :] = (du0 * sg).astype(da_ref.dtype)
            dg_ref[pl.ds(r0, CONV_CHUNK), :] = (du0 * av * (sg * (1.0 - sg))).astype(dg_ref.dtype)
            return carry

        lax.fori_loop(0, s // CONV_CHUNK, step, 0)
        if rows_out > s:
            zeros = jnp.zeros((rows_out - s, LANES), da_ref.dtype)
            da_ref[s:rows_out, :] = zeros
            dg_ref[s:rows_out, :] = zeros
        for k in range(CW):
            dw_ref[k:k + 1, :] = jnp.sum(acc[8 * k:8 * k + 8, :], axis=0, keepdims=True)
        db_ref[...] = jnp.sum(acc[8 * CW:8 * CW + 8, :], axis=0, keepdims=True)

    col = lambda off: pl.BlockSpec((s, LANES), lambda c: (0, c + off))
    ocol = pl.BlockSpec((rows_out, LANES), lambda c: (0, c))
    return _pallas(
        body, name="conf_conv_bwd", grid=(nc,),
        in_specs=[col(0), col(nc), col(0), pl.BlockSpec((CW, LANES), lambda c: (0, c))],
        out_specs=[ocol, ocol, pl.BlockSpec((CW, LANES), lambda c: (0, c)), pl.BlockSpec((1, LANES), lambda c: (0, c))],
        out_shape=[_sds((rows_out, DC), MXU_DTYPE)] * 2 + [_sds((CW, DC), F32), _sds((1, DC), F32)],
        scratch_shapes=[pltpu.VMEM((s + 2 * CONV_HALO, LANES), F32)] * 2 + [pltpu.VMEM((8 * (CW + 1), LANES), F32)],
        semantics=("parallel",),
    )(ag, ag, du1, conv_w)


Q_TILE = 2 * GW
K_WIN = PAIR_ROWS * GW


def _bias_table(rpb_rev):
    def body(p_ref, t_ref):
        kcol = lax.broadcasted_iota(jnp.int32, (GW, LANES), 0)
        lane = lax.broadcasted_iota(jnp.int32, (GW, LANES), 1)
        qcol = lane % GW
        cs = jnp.clip(qcol - NA_ROWS, 0, GW - 2 * NA_ROWS)
        colvalid = (kcol >= cs) & (kcol < cs + 2 * NA_ROWS)
        neg = jnp.full((GW, LANES), NEG, F32)

        def skew(h, ro, shift):
            if ro < 0 or ro >= 2 * NA_ROWS - 1:
                return neg
            row = jnp.broadcast_to(p_ref[h * 16 + ro:h * 16 + ro + 1, :], (GW, LANES))
            return pltpu.roll(row, shift, 1, stride=1, stride_axis=0)

        for h in range(NH):
            for b in range(TAB_BLOCKS):
                val = jnp.where(lane < GW, skew(h, b - 1, GW + 1), skew(h, b - 2, 1))
                t_ref[h, b * GW:(b + 1) * GW, :] = jnp.where(colvalid, val, neg)

    return _pallas(body, name="attn_bias_table", out_shape=_sds((NH, TAB_BLOCKS * GW, LANES), F32))(rpb_rev)


def _rpb_grad(tt):
    def body(t_ref, o_ref):
        lane = lax.broadcasted_iota(jnp.int32, (GW, LANES), 1)
        si = lax.broadcasted_iota(jnp.int32, (GW, GW), 0)
        ti = lax.broadcasted_iota(jnp.int32, (GW, GW), 1)
        flip = jnp.where(si + ti == GW - 1, 1.0, 0.0).astype(F32)
        o_ref[...] = jnp.zeros_like(o_ref)
        for h in range(NH):
            for ro in range(2 * NA_ROWS - 1):
                lo = t_ref[h, (ro + 1) * GW:(ro + 2) * GW, :]
                hi = t_ref[h, (ro + 2) * GW:(ro + 3) * GW, :]
                g = jnp.where(lane < GW, lo + pltpu.roll(hi, GW, 1), 0.0)
                gf = jnp.dot(flip, g, preferred_element_type=F32, precision=lax.Precision.HIGHEST)
                sk = pltpu.roll(gf, 0, 1, stride=1, stride_axis=0)
                o_ref[h * 16 + ro:h * 16 + ro + 1, :] = jnp.sum(sk, axis=0, keepdims=True)

    return _pallas(body, name="attn_rpb_grad", out_shape=_sds((NH * 16, LANES), F32))(tt)


def _attn_geometry(i, rows):
    wsp = jnp.clip(2 * i - NA_ROWS // 2, 0, rows - PAIR_ROWS)
    k0 = pl.multiple_of(wsp * GW, GW)
    t0 = pl.multiple_of((wsp - 2 * i + NA_ROWS) * GW, GW)
    rr = lax.broadcasted_iota(jnp.int32, (GW, Q_TILE), 1) // GW
    wsr = jnp.clip(2 * i + rr - NA_ROWS // 2, 0, rows - NA_ROWS)
    edge_masks = tuple(jnp.where((kr >= wsr) & (kr < wsr + NA_ROWS), 0.0, NEG).astype(F32)
                       for kr in (wsp, wsp + PAIR_ROWS - 1))
    return k0, t0, edge_masks


def _biased(s_raw, bias, edge_masks):
    x = s_raw + bias
    return jnp.concatenate([x[:GW] + edge_masks[0], x[GW:K_WIN - GW], x[K_WIN - GW:] + edge_masks[1]], axis=0)


def _two_heads_on_lanes(xt):
    feat = lax.broadcasted_iota(jnp.int32, xt.shape, 0)
    zero = jnp.zeros_like(xt)
    return jnp.concatenate([jnp.where(feat < HD, xt, zero), jnp.where(feat >= HD, xt, zero)], axis=1)


def _two_heads_on_rows(x):
    lane = lax.broadcasted_iota(jnp.int32, x.shape, 1)
    zero = jnp.zeros_like(x)
    return jnp.concatenate([jnp.where(lane < HD, x, zero), jnp.where(lane >= HD, x, zero)], axis=0)


def _pick_heads(x2):
    n = x2.shape[0] // 2
    lane = lax.broadcasted_iota(jnp.int32, (n, LANES), 1)
    return jnp.where(lane < HD, x2[:n], x2[n:])


_TN = (((0,), (0,)), ((), ()))


def _attn_fwd(qkv, tab, s):
    rows = s // GW
    npair = rows // 2

    def body(q_ref, kv_ref, tab_ref, o_ref, lse_ref):
        i = pl.program_id(0)
        k0, t0, edge_masks = _attn_geometry(i, rows)
        for p in range(NH // 2):
            cq = slice(p * LANES, (p + 1) * LANES)
            ck = slice(DA + p * LANES, DA + (p + 1) * LANES)
            cv = slice(2 * DA + p * LANES, 2 * DA + (p + 1) * LANES)
            qm2 = _two_heads_on_lanes(q_ref[:, cq].T) * SCALE
            s_loc = jnp.dot(kv_ref[pl.ds(k0, K_WIN), ck], qm2, preferred_element_type=F32)
            s_ctx = jnp.dot(kv_ref[pl.ds(s, CTX), ck], qm2, preferred_element_type=F32)
            p_loc, p_ctx = [], []
            for hh in range(2):
                h = 2 * p + hh
                ch = slice(hh * Q_TILE, (hh + 1) * Q_TILE)
                sl = _biased(s_loc[:, ch], tab_ref[h, pl.ds(t0, K_WIN), :], edge_masks)
                sc = s_ctx[:, ch]
                m = jnp.maximum(jnp.max(sl, axis=0, keepdims=True), jnp.max(sc, axis=0, keepdims=True))
                el = jnp.exp(sl - m)
                ec = jnp.exp(sc - m)
                l = jnp.sum(el, axis=0, keepdims=True) + jnp.sum(ec, axis=0, keepdims=True)
                inv = 1.0 / l
                lse_ref[h:h + 1, :] = m + jnp.log(l)
                p_loc.append((el * inv).astype(MXU_DTYPE))
                p_ctx.append((ec * inv).astype(MXU_DTYPE))
            o2 = (lax.dot_general(jnp.concatenate(p_loc, axis=1), kv_ref[pl.ds(k0, K_WIN), cv], _TN, preferred_element_type=F32)
                  + lax.dot_general(jnp.concatenate(p_ctx, axis=1), kv_ref[pl.ds(s, CTX), cv], _TN, preferred_element_type=F32))
            o_ref[:, cq] = _pick_heads(o2).astype(o_ref.dtype)

    return _pallas(
        body, name="attn_fwd", grid=(npair,),
        in_specs=[pl.BlockSpec((Q_TILE, DA), lambda i: (i, 0)), pl.BlockSpec(memory_space=pltpu.VMEM),
                  pl.BlockSpec(memory_space=pltpu.VMEM)],
        out_specs=[pl.BlockSpec((Q_TILE, DA), lambda i: (i, 0)), pl.BlockSpec((NH, Q_TILE), lambda i: (0, i))],
        out_shape=[_sds((s, D), MXU_DTYPE), _sds((NH, s), F32)],
        semantics=("arbitrary",),
    )(qkv, qkv, tab)


def _attn_bwd(qkv, tab, lse, dycat, s):
    rows = s // GW
    npair = rows // 2
    sa = s + CTX
    nzero = CTX // Q_TILE

    def body(q_ref, do_ref, lse_ref, kv_ref, tab_ref, dq_ref, dkv_ref, tt_ref, dk_acc, dv_acc):
        i = pl.program_id(0)

        @pl.when(i == 0)
        def _():
            dk_acc[...] = jnp.zeros_like(dk_acc)
            dv_acc[...] = jnp.zeros_like(dv_acc)
            tt_ref[...] = jnp.zeros_like(tt_ref)

        @pl.when(i >= npair)
        def _():
            dq_ref[...] = jnp.zeros_like(dq_ref)

        @pl.when(i < npair)
        def _():
            k0, t0, edge_masks = _attn_geometry(i, rows)
            for p in range(NH // 2):
                cq = slice(p * LANES, (p + 1) * LANES)
                ck = slice(DA + p * LANES, DA + (p + 1) * LANES)
                cv = slice(2 * DA + p * LANES, 2 * DA + (p + 1) * LANES)
                qp = q_ref[:, cq] * SCALE
                dop = do_ref[:, cq].astype(MXU_DTYPE)
                qm2 = _two_heads_on_lanes(qp.T)
                dom2 = _two_heads_on_lanes(dop.T)
                kw = kv_ref[pl.ds(k0, K_WIN), ck]
                kc = kv_ref[pl.ds(s, CTX), ck]
                vw = kv_ref[pl.ds(k0, K_WIN), cv]
                vc = kv_ref[pl.ds(s, CTX), cv]
                s_loc = jnp.dot(kw, qm2, preferred_element_type=F32)
                s_ctx = jnp.dot(kc, qm2, preferred_element_type=F32)
                dp_loc = jnp.dot(vw, dom2, preferred_element_type=F32)
                dp_ctx = jnp.dot(vc, dom2, preferred_element_type=F32)
                p_loc, p_ctx, ds_loc, ds_ctx = [], [], [], []
                for hh in range(2):
                    h = 2 * p + hh
                    ch = slice(hh * Q_TILE, (hh + 1) * Q_TILE)
                    lse_h = lse_ref[h:h + 1, :]
                    pl_ = jnp.exp(_biased(s_loc[:, ch], tab_ref[h, pl.ds(t0, K_WIN), :], edge_masks) - lse_h)
                    pc_ = jnp.exp(s_ctx[:, ch] - lse_h)
                    dpl = dp_loc[:, ch]
                    dpc = dp_ctx[:, ch]
                    delta = jnp.sum(pl_ * dpl, axis=0, keepdims=True) + jnp.sum(pc_ * dpc, axis=0, keepdims=True)
                    dsl = pl_ * (dpl - delta)
                    dsc = pc_ * (dpc - delta)
                    tt_ref[h, pl.ds(t0, K_WIN), :] += dsl
                    p_loc.append(pl_.astype(MXU_DTYPE))
                    p_ctx.append(pc_.astype(MXU_DTYPE))
                    ds_loc.append(dsl.astype(MXU_DTYPE))
                    ds_ctx.append(dsc.astype(MXU_DTYPE))
                p_loc, p_ctx = jnp.concatenate(p_loc, axis=1), jnp.concatenate(p_ctx, axis=1)
                ds_loc, ds_ctx = jnp.concatenate(ds_loc, axis=1), jnp.concatenate(ds_ctx, axis=1)
                do_rows = _two_heads_on_rows(dop)
                q_rows = _two_heads_on_rows(qp)
                dv_acc[pl.ds(k0, K_WIN), cq] += jnp.dot(p_loc, do_rows, preferred_element_type=F32)
                dv_acc[pl.ds(s, CTX), cq] += jnp.dot(p_ctx, do_rows, preferred_element_type=F32)
                dk_acc[pl.ds(k0, K_WIN), cq] += jnp.dot(ds_loc, q_rows, preferred_element_type=F32)
                dk_acc[pl.ds(s, CTX), cq] += jnp.dot(ds_ctx, q_rows, preferred_element_type=F32)
                dq2 = (lax.dot_general(ds_loc, kw, _TN, preferred_element_type=F32)
                       + lax.dot_general(ds_ctx, kc, _TN, preferred_element_type=F32))
                dq_ref[:, cq] = (_pick_heads(dq2) * SCALE).astype(dq_ref.dtype)

        @pl.when(i == npair - 1)
        def _():
            def cp(c, carry):
                r0 = pl.multiple_of(c * ROW_TILE, ROW_TILE)
                dkv_ref[pl.ds(r0, ROW_TILE), 0:DA] = dk_acc[pl.ds(r0, ROW_TILE), :].astype(dkv_ref.dtype)
                dkv_ref[pl.ds(r0, ROW_TILE), DA:2 * DA] = dv_acc[pl.ds(r0, ROW_TILE), :].astype(dkv_ref.dtype)
                return carry

            lax.fori_loop(0, sa // ROW_TILE, cp, 0)

    qmap = lambda i: (jnp.minimum(i, npair - 1), 0)
    return _pallas(
        body, name="attn_bwd", grid=(npair + nzero,),
        in_specs=[pl.BlockSpec((Q_TILE, DA), qmap), pl.BlockSpec((Q_TILE, DA), qmap),
                  pl.BlockSpec((NH, Q_TILE), lambda i: (0, jnp.minimum(i, npair - 1))),
                  pl.BlockSpec(memory_space=pltpu.VMEM), pl.BlockSpec(memory_space=pltpu.VMEM)],
        out_specs=[pl.BlockSpec((Q_TILE, DA), lambda i: (i, 0)), pl.BlockSpec(memory_space=pltpu.VMEM),
                   pl.BlockSpec(memory_space=pltpu.VMEM)],
        out_shape=[_sds((sa, DA), MXU_DTYPE), _sds((sa, 2 * DA), MXU_DTYPE), _sds((NH, TAB_BLOCKS * GW, LANES), F32)],
        scratch_shapes=[pltpu.VMEM((sa, DA), F32)] * 2,
        semantics=("arbitrary",),
    )(qkv, dycat, lse, qkv, tab)


def _tile(n, prefs):
    for t in prefs:
        if n % t == 0:
            return t
    raise ValueError((n, prefs))


def _local_step(x, ctx, tgt, mod, mod_c, vec, w_in, late_weights, rpb_rev, early_grads=None):
    s = x.shape[0]
    sa = s + CTX
    ts = _tile(s, (1024, 512, 256))
    ts2 = _tile(s, (2048, 1024, 512, 256))
    tsa = _tile(sa, (1088, 640, 256))
    tsa2 = _tile(sa, (2176, 640, 256))
    sh1, sc1, gt1, sh2, sc2, gt2 = (mod[i:i + 1] for i in range(6))
    csh1, csc1 = mod_c[0:1], mod_c[1:2]
    act = MXU_DTYPE

    tab = _bias_table(rpb_rev)
    h_all = _rmsmod_fwd(x, ctx, vec["g_norm1"], sc1, sh1, csc1, csh1)
    w_in = w_in(h_all) if callable(w_in) else w_in
    qkv = _mm(h_all, w_in, mode="nn", m=sa, n=3 * DA, k=D, tm=tsa2, tn=512, tk=D, out_dtype=MXU_DTYPE, name="mm_qkv")
    ag = _mm(h_all, w_in, mode="nn", m=s, n=2 * DC, k=D, tm=ts2, tn=512, tk=D, out_dtype=F32, name="mm_ag", b_off=(0, 3))
    ycat, lse = _attn_fwd(qkv, tab, s)
    u1 = _conf_conv_fwd(ag, vec["conv_w"], vec["conv_b"])
    ycat = _conf_ln_fwd(u1, vec["ln_g"], vec["ln_b"], ycat)
    if callable(late_weights):
        w_out, ffn_weights = late_weights(ycat)
    else:
        w_out, ffn_weights = late_weights[0], late_weights[1:]
    y = _mm(ycat, w_out, mode="nn", m=s, n=D, k=D, tm=ts2, tn=512, tk=D, out_dtype=F32, name="mm_out")
    x1, h2 = _resid_rmsmod_fwd(x, y, gt1, vec["g_norm2"], sc2, sh2)
    w_up, w_down = ffn_weights(h2) if callable(ffn_weights) else ffn_weights
    u = _mm(h2, w_up, mode="nn", m=s, n=2 * DFF, k=D, tm=ts2, tn=512, tk=D, out_dtype=act, name="mm_up")
    f = _ffn_act_fwd(u, vec["ffn_conv_w"], vec["ffn_conv_b"])
    z = _mm(f, w_down, mode="nn", m=s, n=D, k=DFF, tm=ts, tn=D, tk=DFF, out_dtype=F32, name="mm_down")
    dx2, dz, loss, dgt2, dgf = _final_fwd_bwd(x1, z, gt2, vec["g_final"], tgt)

    df = _mm(dz, w_down, mode="nt", m=s, n=DFF, k=D, tm=ts, tn=DFF, tk=D, out_dtype=act, name="mm_down_dx")
    d_w_down = _mm(f, dz, mode="tn", m=DFF, n=D, k=s, tm=DFF // 2, tn=D, tk=ts2, out_dtype=F32, name="mm_down_dw")
    dug, duv, dfw_g, dfw_v, dfb_g, dfb_v = _ffn_act_bwd(u, df, vec["ffn_conv_w"], vec["ffn_conv_b"])
    dw_kw = dict(mode="tn", m=D, n=DFF, k=s, tm=D, tn=DFF, tk=ts, out_dtype=F32, out_total=(D, 2 * DFF))
    d_w_up = _mm(h2, dug, name="mm_up_dw_gate", **dw_kw)
    d_w_up = _mm(h2, duv, name="mm_up_dw_val", o_off=(0, 1), into=d_w_up, **dw_kw)
    if early_grads is not None:
        early_grads[0](d_w_up, d_w_down)
    dh2 = _mm([dug, duv], w_up, mode="nt", m=s, n=D, k=2 * DFF, tm=ts, tn=D, tk=2 * DFF, out_dtype=F32, name="mm_up_dx")
    sc2_b = sc2 if early_grads is None else sc2 + early_grads[1](dh2)
    dsh2, dsc2, dg2, dx1, dy, dgt1 = _rmsmod_bwd(x1, dh2, vec["g_norm2"], sc2_b, name="rmsmod2_bwd", add=dx2, resid=(gt1, y))
    dycat = _mm(dy, w_out, mode="nt", m=s, n=D, k=D, tm=ts2, tn=512, tk=D, out_dtype=F32, name="mm_out_dx")
    d_w_out = _mm(ycat, dy, mode="tn", m=D, n=D, k=s, tm=D, tn=D, tk=ts, out_dtype=F32, name="mm_out_dw")
    du1, dln_g, dln_b = _conf_ln_bwd(dycat, u1, vec["ln_g"], vec["ln_b"])
    da, dg, dconv_w, dconv_b = _conf_conv_bwd(ag, du1, vec["conv_w"], sa)
    dq, dkv, tt = _attn_bwd(qkv, tab, lse, dycat, s)
    drpb_rev = _rpb_grad(tt)
    d_pieces = [dq, dkv, da, dg]
    dh = _mm(d_pieces, w_in, mode="nt", m=sa, n=D, k=NIN, tm=tsa, tn=D, tk=NIN, out_dtype=F32, name="mm_in_dx")
    d_w_in = _mm(h_all, d_pieces, mode="tn", m=D, n=NIN, k=sa, tm=D, tn=NIN, tk=tsa, out_dtype=F32, name="mm_in_dw")
    dsh1, dsc1, dg1, grad_x = _rmsmod_bwd(x, dh, vec["g_norm1"], sc1, name="rmsmod1_bwd", add=dx1)
    dcsh1, dcsc1, dg1c = _rmsmod_bwd(ctx, dh, vec["g_norm1"], csc1, name="rmsmod1_ctx_bwd", dh_row_off=s // ROW_TILE)

    small = dict(
        dmod=[dsh1, dsc1, dgt1, dsh2, dsc2, dgt2], dmod_c=[dcsh1, dcsc1],
        g_norm1=[dg1, dg1c], g_norm2=dg2, g_final=dgf, conv_b=dconv_b, ln_g=dln_g, ln_b=dln_b, conv_w=dconv_w,
        ffn_conv_w=[dfw_g, dfw_v], ffn_conv_b=[dfb_g, dfb_v], rpb_rev=drpb_rev,
    )
    return loss, grad_x, d_w_in, d_w_out, d_w_up, d_w_down, small


N_CHIPS = 4
HBM = pl.BlockSpec(memory_space=pl.ANY)
BIG = {"w_in": ("col", (D, NIN)), "w_out": ("row", (D, D)), "w_up": ("col", (D, 2 * DFF)), "w_down": ("row", (DFF, D))}
BIG_NAMES = tuple(BIG)
LATE_NAMES = ("w_out", "w_up", "w_down")


def _shard_shape(name):
    kind, (r, c) = BIG[name]
    return (r, c // N_CHIPS) if kind == "col" else (r // N_CHIPS, c)


def _half_rows(name):
    return _shard_shape(name)[0] // 2


def _place():
    x, y, c = lax.axis_index("x"), lax.axis_index("y"), lax.axis_index("c")
    others = [(1 - x, y), (x, 1 - y), (1 - x, 1 - y)]
    return x, y, c, 2 * x + y, (x, y, 1 - c), others


def _whole_region(ref, name, chip, half):
    kind, _ = BIG[name]
    r, c = _shard_shape(name)
    if kind == "col":
        return ref.at[pl.ds(half * (r // 2), r // 2), pl.ds(chip * c, c)]
    return ref.at[pl.ds(chip * r + half * (r // 2), r // 2), :]


def _remote(src, dst, send_sem, recv_sem, to):
    return pltpu.make_async_remote_copy(src_ref=src, dst_ref=dst, send_sem=send_sem, recv_sem=recv_sem,
                                        device_id=to, device_id_type=MESH)


def _gather_small(v, name):
    m_per, n = v.shape

    def body(x_ref, out_ref, send_sems, recv_sems, local_sem):
        x, y, c, _, sibling, others = _place()
        me = (x, y, c)

        def rows(px, py, pc):
            return out_ref.at[pl.ds((4 * px + 2 * py + pc) * m_per, m_per), :]

        def copy(k, block, to, src=None):
            return _remote(rows(*block) if src is None else src, rows(*block), send_sems.at[k], recv_sems.at[k], to)

        mine = pltpu.make_async_copy(x_ref, rows(*me), local_sem)
        mine.start()
        first = [copy(0, me, sibling, src=x_ref)]
        first += [copy(1 + j, me, (*chip, c), src=x_ref) for j, chip in enumerate(others)]
        for cp in first:
            cp.start()
        passed = [copy(4 + j, (*chip, c), sibling) for j, chip in enumerate(others)]
        for j, chip in enumerate(others):
            copy(1 + j, (*chip, c), me).wait_recv()
            passed[j].start()
        copy(0, sibling, me).wait_recv()
        for j, chip in enumerate(others):
            copy(4 + j, (*chip, 1 - c), me).wait_recv()
        for cp in first + passed:
            cp.wait_send()
        mine.wait()

    return pl.pallas_call(
        body, name=name, out_shape=_sds((8 * m_per, n), v.dtype),
        in_specs=[pl.BlockSpec(memory_space=pltpu.VMEM)], out_specs=pl.BlockSpec(memory_space=pltpu.VMEM),
        scratch_shapes=[pltpu.SemaphoreType.DMA((7,)), pltpu.SemaphoreType.DMA((7,)), pltpu.SemaphoreType.DMA],
    )(v)


def _cast_into_whole(name, shard, chip):
    kind, whole = BIG[name]
    r, c = shard.shape
    if kind == "col":
        tr = 256
        o_spec = pl.BlockSpec((tr, c), lambda i, ch: (i, ch[0]))
    else:
        tr = _tile(r, (128, 352))
        o_spec = pl.BlockSpec((tr, c), lambda i, ch: (ch[0] * (r // tr) + i, 0))

    def body(ch_ref, x_ref, o_ref):
        del ch_ref
        o_ref[...] = x_ref[...].astype(o_ref.dtype)

    return _pallas(body, name="cast_" + name, prefetch=1, grid=(r // tr,),
                   in_specs=[pl.BlockSpec((tr, c), lambda i, ch: (i, 0))], out_specs=o_spec,
                   out_shape=_sds(whole, MXU_DTYPE), semantics=("parallel",))(chip, shard)


SEM = pl.BlockSpec(memory_space=pltpu.SEMAPHORE)
IN_HBM = pl.BlockSpec(memory_space=pltpu.HBM)
DATAFLOW = pltpu.SideEffectType.DATAFLOW_SIDE_EFFECTING


def _keep_in_hbm(a):
    return pltpu.with_memory_space_constraint(a, pltpu.HBM)


def _gather_start(wholes, names, after, tag):
    nw = len(names)
    ns = 2 * 3 * nw

    def body(*refs):
        ins = refs[:nw]
        sems = refs[nw + 1:nw + 1 + ns]
        token = refs[2 * nw + ns + 1]
        _, _, c, chip, _, others = _place()
        for w, name in enumerate(names):
            mine = _whole_region(ins[w], name, chip, c)
            for t, (ox, oy) in enumerate(others):
                k = 2 * (3 * w + t)
                _remote(mine, mine, sems[k], sems[k + 1], (ox, oy, c)).start()
        token[...] = jnp.zeros_like(token)

    res = pl.pallas_call(
        body, name="gather_" + tag + "_start",
        out_shape=(*[pltpu.SemaphoreType.DMA(())] * ns, *[pltpu.HBM(a.shape, a.dtype) for a in wholes], _sds((8, LANES), F32)),
        in_specs=[IN_HBM] * nw + [pl.BlockSpec(memory_space=pl.ANY)],
        out_specs=(*[SEM] * ns, *[IN_HBM] * nw, pl.BlockSpec(memory_space=pltpu.VMEM)),
        input_output_aliases={i: ns + i for i in range(nw)},
        compiler_params=pltpu.CompilerParams(has_side_effects=DATAFLOW),
    )(*[_keep_in_hbm(a) for a in wholes], after)
    return list(res[:ns]), list(res[ns:ns + nw]), res[ns + nw]


def _gather_wait(sems, wholes, names, after, tag):
    nw = len(names)
    ns = len(sems)

    def body(*refs):
        ins = refs[:nw]
        sem_refs = refs[nw:nw + ns]
        _, _, c, chip, _, others = _place()
        for w, name in enumerate(names):
            mine = _whole_region(ins[w], name, chip, c)
            for t, (ox, oy) in enumerate(others):
                got = _whole_region(ins[w], name, 2 * ox + oy, c)
                k = 2 * (3 * w + t)
                cp = _remote(mine, got, sem_refs[k], sem_refs[k + 1], (ox, oy, c))
                cp.wait_send()
                cp.wait_recv()

    return pl.pallas_call(
        body, name="gather_" + tag + "_wait",
        out_shape=tuple(pltpu.HBM(a.shape, a.dtype) for a in wholes),
        in_specs=[IN_HBM] * nw + [SEM] * ns + [pl.BlockSpec(memory_space=pl.ANY)], out_specs=tuple([IN_HBM] * nw),
        input_output_aliases={i: i for i in range(nw)},
        compiler_params=pltpu.CompilerParams(has_side_effects=DATAFLOW),
    )(*wholes, *sems, after)


def _forward_halves(wholes, names, tag):
    nw = len(names)

    def body(*refs):
        outs = refs[nw:2 * nw]
        send_sems, recv_sems = refs[2 * nw:]
        _, _, c, _, sibling, others = _place()
        sends = []
        for w, name in enumerate(names):
            for t, (ox, oy) in enumerate(others):
                got = _whole_region(outs[w], name, 2 * ox + oy, c)
                cp = _remote(got, got, send_sems.at[w, t], recv_sems.at[w, t], sibling)
                cp.start()
                sends.append(cp)
        for w, name in enumerate(names):
            for t, (ox, oy) in enumerate(others):
                got = _whole_region(outs[w], name, 2 * ox + oy, 1 - c)
                _remote(got, got, send_sems.at[w, t], recv_sems.at[w, t], sibling).wait_recv()
        for cp in sends:
            cp.wait_send()

    return pl.pallas_call(
        body, name="gather_" + tag + "_forward",
        out_shape=[_sds(a.shape, a.dtype) for a in wholes],
        in_specs=[HBM] * nw, out_specs=[HBM] * nw,
        input_output_aliases={i: i for i in range(nw)},
        scratch_shapes=[pltpu.SemaphoreType.DMA((nw, 3)), pltpu.SemaphoreType.DMA((nw, 3))],
    )(*wholes)


def _forward_start(wholes, names, tag, after):
    nw = len(names)
    ns = 2 * 3 * nw

    def body(*refs):
        ins = refs[:nw]
        sems = refs[nw + 1:nw + 1 + ns]
        token = refs[2 * nw + ns + 1]
        _, _, c, _, sibling, others = _place()
        for w, name in enumerate(names):
            for t, (ox, oy) in enumerate(others):
                got = _whole_region(ins[w], name, 2 * ox + oy, c)
                k = 2 * (3 * w + t)
                _remote(got, got, sems[k], sems[k + 1], sibling).start()
        token[...] = jnp.zeros_like(token)

    res = pl.pallas_call(
        body, name="gather_" + tag + "_forward_start",
        out_shape=(*[pltpu.SemaphoreType.DMA(())] * ns, *[pltpu.HBM(a.shape, a.dtype) for a in wholes], _sds((8, LANES), F32)),
        in_specs=[IN_HBM] * nw + [pl.BlockSpec(memory_space=pl.ANY)],
        out_specs=(*[SEM] * ns, *[IN_HBM] * nw, pl.BlockSpec(memory_space=pltpu.VMEM)),
        input_output_aliases={i: ns + i for i in range(nw)},
        compiler_params=pltpu.CompilerParams(has_side_effects=DATAFLOW),
    )(*[_keep_in_hbm(a) for a in wholes], after)
    return list(res[:ns]), list(res[ns:ns + nw]), res[ns + nw]


def _forward_wait(sems, wholes, names, after, tag):
    nw = len(names)
    ns = len(sems)

    def body(*refs):
        ins = refs[:nw]
        sem_refs = refs[nw:nw + ns]
        _, _, c, _, sibling, others = _place()
        for w, name in enumerate(names):
            for t, (ox, oy) in enumerate(others):
                k = 2 * (3 * w + t)
                cp = _remote(_whole_region(ins[w], name, 2 * ox + oy, c), _whole_region(ins[w], name, 2 * ox + oy, 1 - c),
                             sem_refs[k], sem_refs[k + 1], sibling)
                cp.wait_send()
                cp.wait_recv()

    return pl.pallas_call(
        body, name="gather_" + tag + "_forward_wait",
        out_shape=tuple(pltpu.HBM(a.shape, a.dtype) for a in wholes),
        in_specs=[IN_HBM] * nw + [SEM] * ns + [pl.BlockSpec(memory_space=pl.ANY)], out_specs=tuple([IN_HBM] * nw),
        input_output_aliases={i: i for i in range(nw)},
        compiler_params=pltpu.CompilerParams(has_side_effects=DATAFLOW),
    )(*wholes, *sems, after)


def _compact_shape(name, dtype):
    kind, (r, c) = BIG[name]
    return _sds((r // 2, c), dtype)


def _swap_pairs(ins, outs, names, c):
    pairs = []
    for w, name in enumerate(names):
        kind, _ = BIG[name]
        half = _half_rows(name)
        if kind == "col":
            pairs.append((ins[w].at[pl.ds((1 - c) * half, half), :], outs[w]))
        else:
            pairs += [(ins[w].at[pl.ds(jj * 2 * half + (1 - c) * half, half), :], outs[w].at[pl.ds(jj * half, half), :])
                      for jj in range(N_CHIPS)]
    return pairs


def _n_swap_copies(names):
    return sum(1 if BIG[n][0] == "col" else N_CHIPS for n in names)


def _swap_start(grads, names, label):
    nw = len(names)
    ns = 2 * _n_swap_copies(names)

    def body(*refs):
        ins, lands = refs[:nw], refs[nw:2 * nw]
        sems = refs[2 * nw:2 * nw + ns]
        token = refs[4 * nw + ns]
        _, _, c, _, sibling, _ = _place()
        for k, (src, dst) in enumerate(_swap_pairs(ins, lands, names, c)):
            _remote(src, dst, sems[2 * k], sems[2 * k + 1], sibling).start()
        token[...] = jnp.zeros_like(token)

    lands = [_keep_in_hbm(lax.empty(_compact_shape(n, F32).shape, F32)) for n in names]
    res = pl.pallas_call(
        body, name=label,
        out_shape=(*[pltpu.SemaphoreType.DMA(())] * ns, *[pltpu.HBM(a.shape, a.dtype) for a in grads],
                   *[pltpu.HBM(a.shape, a.dtype) for a in lands], _sds((8, LANES), F32)),
        in_specs=[IN_HBM] * (2 * nw),
        out_specs=(*[SEM] * ns, *[IN_HBM] * (2 * nw), pl.BlockSpec(memory_space=pltpu.VMEM)),
        input_output_aliases={i: ns + i for i in range(2 * nw)},
        compiler_params=pltpu.CompilerParams(has_side_effects=DATAFLOW),
    )(*[_keep_in_hbm(a) for a in grads], *lands)
    return list(res[:ns]), list(res[ns:ns + nw]), list(res[ns + nw:ns + 2 * nw]), res[ns + 2 * nw]


def _swap_wait(sems, grads, lands, names, after, label):
    nw = len(names)
    ns = len(sems)

    def body(*refs):
        ins, land_refs = refs[:nw], refs[nw:2 * nw]
        sem_refs = refs[2 * nw:2 * nw + ns]
        _, _, c, _, sibling, _ = _place()
        for k, (src, dst) in enumerate(_swap_pairs(ins, land_refs, names, c)):
            cp = _remote(src, dst, sem_refs[2 * k], sem_refs[2 * k + 1], sibling)
            cp.wait_send()
            cp.wait_recv()

    res = pl.pallas_call(
        body, name=label,
        out_shape=tuple(pltpu.HBM(a.shape, a.dtype) for a in (*grads, *lands)),
        in_specs=[IN_HBM] * (2 * nw) + [SEM] * ns + [pl.BlockSpec(memory_space=pl.ANY)],
        out_specs=tuple([IN_HBM] * (2 * nw)),
        input_output_aliases={i: i for i in range(2 * nw)},
        compiler_params=pltpu.CompilerParams(has_side_effects=DATAFLOW),
    )(*grads, *lands, *sems, after)
    return list(res[:nw]), list(res[nw:])


def _add_halves(name, grad, got, core):
    kind, (r, c) = BIG[name]
    half = _half_rows(name)
    if kind == "col":
        t = 128
        grid = (half // t,)
        g_spec = pl.BlockSpec((t, c), lambda i, cr: (cr[0] * (half // t) + i, 0))
        o_spec = pl.BlockSpec((t, c), lambda i, cr: (i, 0))
    else:
        t = half
        grid = (N_CHIPS,)
        g_spec = pl.BlockSpec((t, c), lambda i, cr: (2 * i + cr[0], 0))
        o_spec = pl.BlockSpec((t, c), lambda i, cr: (i, 0))

    def body(c_ref, g_ref, b_ref, o_ref):
        del c_ref
        o_ref[...] = (g_ref[...] + b_ref[...]).astype(o_ref.dtype)

    return pl.pallas_call(
        body, name="grad_add_" + name,
        grid_spec=pltpu.PrefetchScalarGridSpec(num_scalar_prefetch=1, grid=grid, in_specs=[g_spec, o_spec], out_specs=o_spec),
        out_shape=_compact_shape(name, BF16),
        compiler_params=pltpu.CompilerParams(dimension_semantics=("parallel",), vmem_limit_bytes=VMEM_LIMIT),
    )(core, grad, got)


def _piece(ref, name, chip):
    kind, _ = BIG[name]
    r, c = _shard_shape(name)
    if kind == "col":
        return ref.at[:, pl.ds(chip * c, c)]
    return ref.at[pl.ds(chip * (r // 2), r // 2), :]


def _landing_shape(name):
    r, c = _shard_shape(name)
    return (N_CHIPS - 1, r // 2, c)


def _exchange_start(parts, names, label):
    nw = len(names)
    ns = 2 * 3 * nw

    def body(*refs):
        ins, lands = refs[:nw], refs[nw:2 * nw]
        sems = refs[2 * nw:2 * nw + ns]
        token = refs[4 * nw + ns]
        _, _, c, _, _, others = _place()
        for w, name in enumerate(names):
            for t, (ox, oy) in enumerate(others):
                k = 2 * (3 * w + t)
                _remote(_piece(ins[w], name, 2 * ox + oy), lands[w].at[t], sems[k], sems[k + 1], (ox, oy, c)).start()
        token[...] = jnp.zeros_like(token)

    lands = [_keep_in_hbm(lax.empty(_landing_shape(n), BF16)) for n in names]
    res = pl.pallas_call(
        body, name=label,
        out_shape=(*[pltpu.SemaphoreType.DMA(())] * ns, *[pltpu.HBM(a.shape, a.dtype) for a in parts],
                   *[pltpu.HBM(a.shape, a.dtype) for a in lands], _sds((8, LANES), F32)),
        in_specs=[IN_HBM] * (2 * nw),
        out_specs=(*[SEM] * ns, *[IN_HBM] * (2 * nw), pl.BlockSpec(memory_space=pltpu.VMEM)),
        input_output_aliases={i: ns + i for i in range(2 * nw)},
        compiler_params=pltpu.CompilerParams(has_side_effects=DATAFLOW),
    )(*[_keep_in_hbm(a) for a in parts], *lands)
    return list(res[:ns]), list(res[ns:ns + nw]), list(res[ns + nw:ns + 2 * nw]), res[ns + 2 * nw]


def _exchange_wait(sems, parts, lands, names, after, label):
    nw = len(names)
    ns = len(sems)

    def body(*refs):
        ins, land_refs = refs[:nw], refs[nw:2 * nw]
        sem_refs = refs[2 * nw:2 * nw + ns]
        _, _, c, _, _, others = _place()
        for w, name in enumerate(names):
            for t, (ox, oy) in enumerate(others):
                k = 2 * (3 * w + t)
                cp = _remote(_piece(ins[w], name, 2 * ox + oy), land_refs[w].at[t], sem_refs[k], sem_refs[k + 1], (ox, oy, c))
                cp.wait_send()
                cp.wait_recv()

    res = pl.pallas_call(
        body, name=label,
        out_shape=tuple(pltpu.HBM(a.shape, a.dtype) for a in (*parts, *lands)),
        in_specs=[IN_HBM] * (2 * nw) + [SEM] * ns + [pl.BlockSpec(memory_space=pl.ANY)],
        out_specs=tuple([IN_HBM] * (2 * nw)),
        input_output_aliases={i: i for i in range(2 * nw)},
        compiler_params=pltpu.CompilerParams(has_side_effects=DATAFLOW),
    )(*parts, *lands, *sems, after)
    return list(res[:nw]), list(res[nw:])


def _sum_chips(name, part, got, chip):
    kind, _ = BIG[name]
    _, r, c = got.shape
    t = _tile(r, (128, 352))
    if kind == "col":
        own = pl.BlockSpec((t, c), lambda i, ch: (i, ch[0]))
    else:
        own = pl.BlockSpec((t, c), lambda i, ch: (ch[0] * (r // t) + i, 0))

    def body(ch_ref, p_ref, g_ref, o_ref):
        del ch_ref
        acc = p_ref[...].astype(F32)
        for j in range(N_CHIPS - 1):
            acc = acc + g_ref[j].astype(F32)
        o_ref[...] = acc

    return _pallas(
        body, name="grad_sum_" + name, prefetch=1, grid=(r // t,),
        in_specs=[own, pl.BlockSpec((N_CHIPS - 1, t, c), lambda i, ch: (0, i, 0))],
        out_specs=pl.BlockSpec((t, c), lambda i, ch: (i, 0)),
        out_shape=_sds((r, c), F32), semantics=("parallel",),
    )(chip, part, got)


def _send_halves(sums, label, after):
    nw = len(sums)

    def body(*refs):
        ins, outs = refs[:nw], refs[nw + 1:2 * nw + 1]
        send_sems, recv_sems = refs[2 * nw + 1:]
        _, _, _, _, sibling, _ = _place()
        copies = [_remote(ins[w], outs[w], send_sems.at[w], recv_sems.at[w], sibling) for w in range(nw)]
        for cp in copies:
            cp.start()
        for cp in copies:
            cp.wait()

    return pl.pallas_call(
        body, name=label,
        out_shape=[_sds(a.shape, a.dtype) for a in sums],
        in_specs=[HBM] * (nw + 1), out_specs=[HBM] * nw,
        scratch_shapes=[pltpu.SemaphoreType.DMA((nw,)), pltpu.SemaphoreType.DMA((nw,))],
    )(*sums, after)


EARLY_GRADS = ("w_up", "w_down")
LAST_GRADS = ("w_in", "w_out")


def _reduce_finish(started, names, after, chip, tag):
    sems, parts, lands, _ = started
    parts, lands = _exchange_wait(sems, parts, lands, names, after, "grad_exchange_wait_" + tag)
    return [_sum_chips(n, parts[i], lands[i], chip) for i, n in enumerate(names)]


HI = lax.Precision.HIGHEST
MOD_COLS = 6 * D // N_CHIPS
COND_ROWS = 16


def _silu(v):
    return v * _sigmoid(v)


GATHER_ROWS = 48
FFW_COLS = 2 * DFF // N_CHIPS
CONV_COLS = DC // N_CHIPS


def _pack_cond(c, ffn_w, conv_w):
    def body(c_ref, f_ref, w_ref, o_ref):
        o_ref[...] = jnp.zeros_like(o_ref)
        o_ref[0:1, 0:D] = c_ref[...]
        o_ref[8:11, :] = f_ref[...]
        o_ref[16:16 + CW, 0:CONV_COLS] = w_ref[...]

    return _pallas(body, name="pack_cond", out_shape=_sds((GATHER_ROWS, FFW_COLS), F32))(c, ffn_w, conv_w)


def _unpack_cond(got, c_ctx):
    def body(g_ref, c_ref, cond_ref, f_ref, w_ref):
        cond_ref[...] = jnp.zeros_like(cond_ref)
        for d in range(8):
            cond_ref[d:d + 1, :] = g_ref[d * GATHER_ROWS:d * GATHER_ROWS + 1, 0:D]
        cond_ref[8:9, :] = c_ref[...]
        for j in range(N_CHIPS):
            r0 = 2 * j * GATHER_ROWS
            f_ref[:, j * FFW_COLS:(j + 1) * FFW_COLS] = g_ref[r0 + 8:r0 + 11, :]
            w_ref[:, j * CONV_COLS:(j + 1) * CONV_COLS] = g_ref[r0 + 16:r0 + 16 + CW, 0:CONV_COLS]

    return _pallas(body, name="unpack_cond",
                   out_shape=[_sds((COND_ROWS, D), F32), _sds((3, 2 * DFF), F32), _sds((CW, DC), F32)])(got, c_ctx)


def _chip_cols(rows, width):
    return pl.BlockSpec((rows, width), lambda i, ch: (0, ch[0]))


def _whole(shape):
    return pl.BlockSpec(shape, lambda i, ch: (0,) * len(shape))


def _mod_shard(cond, w_mod, b_mod, chip):
    def body(ch_ref, c_ref, w_ref, b_ref, o_ref):
        del ch_ref
        o_ref[...] = jnp.dot(_silu(c_ref[...]), w_ref[...], preferred_element_type=F32, precision=HI) + b_ref[...]

    return _pallas(body, name="mod_fwd", prefetch=1, grid=(1,),
                   in_specs=[_whole((COND_ROWS, D)), _whole((D, MOD_COLS)), _chip_cols(1, MOD_COLS)],
                   out_specs=_whole((COND_ROWS, MOD_COLS)),
                   out_shape=_sds((COND_ROWS, MOD_COLS), F32))(chip, cond, w_mod, b_mod)


def _unpack_mod(mods, dev):
    def body(dev_ref, m_ref, me_ref, c_ref):
        rowi = lax.broadcasted_iota(jnp.int32, (COND_ROWS, MOD_COLS), 0)
        mine, ctx = [], []
        for j in range(N_CHIPS):
            blk = m_ref[2 * j * COND_ROWS:(2 * j + 1) * COND_ROWS, :]
            mine.append(jnp.sum(jnp.where(rowi == dev_ref[0], blk, 0.0), axis=0, keepdims=True))
            ctx.append(blk[8:9, :])
        mine = jnp.concatenate(mine, axis=1)
        ctx = jnp.concatenate(ctx, axis=1)
        for k in range(6):
            me_ref[k:k + 1, :] = mine[:, k * D:(k + 1) * D]
        for k in range(2):
            c_ref[k:k + 1, :] = ctx[:, k * D:(k + 1) * D]

    return _pallas(body, name="unpack_mod", prefetch=1, grid=(1,),
                   in_specs=[_whole(mods.shape)], out_specs=[_whole((6, D)), _whole((2, D))],
                   out_shape=[_sds((6, D), F32), _sds((2, D), F32)])(dev, mods)


MOD_TILE = 512


def _mod_weight_update(cond, dmod_all, w, m, v, chip):
    nt = MOD_COLS // MOD_TILE

    def body(ch_ref, c_ref, d_ref, w_ref, m_ref, v_ref, g_ref, dl_ref, nm_ref, nv_ref):
        del ch_ref
        g = lax.dot_general(_silu(c_ref[...]), d_ref[...], _TN, preferred_element_type=F32, precision=HI)
        g_ref[...] = g
        dl_ref[...], nm_ref[...], nv_ref[...] = _adam_math(w_ref[...], g, m_ref[...], v_ref[...])

    blk = pl.BlockSpec((D, MOD_TILE), lambda j, ch: (0, j))
    return _pallas(body, name="mod_weight_update", prefetch=1, grid=(nt,),
                   in_specs=[_whole((COND_ROWS, D)), pl.BlockSpec((COND_ROWS, MOD_TILE), lambda j, ch: (0, ch[0] * nt + j)),
                             blk, blk, blk],
                   out_specs=[blk] * 4, out_shape=[_sds((D, MOD_COLS), F32)] * 4,
                   semantics=("parallel",))(chip, cond, dmod_all, w, m, v)


def _cond_grad_partial(dmod_all, w_mod, chip):
    def body(ch_ref, d_ref, w_ref, o_ref):
        del ch_ref
        o_ref[...] = lax.dot_general(d_ref[...], w_ref[...], (((1,), (1,)), ((), ())), preferred_element_type=F32, precision=HI)

    return _pallas(body, name="cond_grad_partial", prefetch=1, grid=(1,),
                   in_specs=[pl.BlockSpec((8, MOD_COLS), lambda i, ch: (1, ch[0])), _whole((D, MOD_COLS))],
                   out_specs=_whole((8, D)), out_shape=_sds((8, D), F32))(chip, dmod_all, w_mod)


def _adam_math(w, g, m, v):
    nm = ADAM_B1 * m + (1.0 - ADAM_B1) * g
    nv = ADAM_B2 * v + (1.0 - ADAM_B2) * (g * g)
    c1 = 1.0 - ADAM_B1 ** ADAM_STEP
    c2 = 1.0 - ADAM_B2 ** ADAM_STEP
    return -ADAM_LR * ((nm / c1) / (jnp.sqrt(nv / c2) + ADAM_EPS) + ADAM_WD * w), nm, nv


def _cond_update(parts, c_ctx, m, v):
    def body(p_ref, c_ref, m_ref, v_ref, g_ref, d_ref, nm_ref, nv_ref):
        tot = p_ref[0:1, :]
        for j in range(1, N_CHIPS):
            tot = tot + p_ref[16 * j:16 * j + 1, :]
        cv = c_ref[...]
        sg = _sigmoid(cv)
        g = tot * (sg * (1.0 + cv * (1.0 - sg)))
        g_ref[...] = g
        d_ref[...], nm_ref[...], nv_ref[...] = _adam_math(cv, g, m_ref[...], v_ref[...])

    return _pallas(body, name="cond_update", out_shape=[_sds((1, D), F32)] * 4)(parts, c_ctx, m, v)


def _adamw_cols(w, g_all, m, v, chip, name):
    r, c = w.shape

    def body(ch_ref, w_ref, g_ref, m_ref, v_ref, go_ref, d_ref, nm_ref, nv_ref):
        del ch_ref
        g = g_ref[...]
        go_ref[...] = g
        d_ref[...], nm_ref[...], nv_ref[...] = _adam_math(w_ref[...], g, m_ref[...], v_ref[...])

    return _pallas(body, name=name, prefetch=1, grid=(1,),
                   in_specs=[_whole((r, c)), _chip_cols(r, c), _whole((r, c)), _whole((r, c))],
                   out_specs=[_whole((r, c))] * 4, out_shape=[_sds((r, c), F32)] * 4)(chip, w, g_all, m, v)


def _adamw_halves(name, w, own, other, m, v, core, after):
    r, c = w.shape
    half = r // 2
    t = _tile(half, (128, 352))
    nh = half // t

    def pick(mine):
        def index(i, cr):
            first = cr[0] if mine else 1 - cr[0]
            return (jnp.clip(i - first * nh, 0, nh - 1), 0)
        return pl.BlockSpec((t, c), index)

    def body(c_ref, w_ref, own_ref, oth_ref, m_ref, v_ref, after_ref, g_ref, d_ref, nm_ref, nv_ref):
        del after_ref
        g = jnp.where(pl.program_id(0) // nh == c_ref[0], own_ref[...], oth_ref[...])
        g_ref[...] = g
        d_ref[...], nm_ref[...], nv_ref[...] = _adam_math(w_ref[...], g, m_ref[...], v_ref[...])

    blk = pl.BlockSpec((t, c), lambda i, cr: (i, 0))
    return _pallas(body, name="adamw_" + name, prefetch=1, grid=(2 * nh,),
                   in_specs=[blk, pick(True), pick(False), blk, blk, pl.BlockSpec(memory_space=pl.ANY)], out_specs=[blk] * 4,
                   out_shape=[_sds((r, c), F32)] * 4, semantics=("parallel",))(core, w, own, other, m, v, after)


WEIGHTS = ("c_ctx", "w_mod", "b_mod", "g_norm1", "w_in", "rpb", "conv_w", "conv_b", "ln_g", "ln_b", "w_out", "g_norm2",
           "w_up", "ffn_conv_w", "ffn_conv_b", "w_down", "g_final")
PACK = (("dmod", 6 * D), ("dmod_c", 2 * D), ("g_norm1", D), ("g_norm1_ctx", D), ("g_norm2", D), ("g_final", D),
        ("conv_b", DC), ("ln_g", DC), ("ln_b", DC), ("ffn_conv_b", 2 * DFF), ("ffn_conv_w", 3 * 2 * DFF),
        ("conv_w", CW * DC), ("rpb_rev", NH * 16 * LANES), ("loss", LANES))
PACK_OFF = {}
_o = 0
for _n, _w in PACK:
    PACK_OFF[_n] = (_o, _w)
    _o += _w
PACK_N = -(-_o // (8 * LANES)) * (8 * LANES)
VECTORS = {"b_mod": (6 * D, ("dmod", "dmod_c")), "g_norm1": (D, ("g_norm1", "g_norm1_ctx")), "conv_b": (DC, ("conv_b",)),
           "ln_g": (DC, ("ln_g",)), "ln_b": (DC, ("ln_b",)), "g_norm2": (D, ("g_norm2",)),
           "ffn_conv_b": (2 * DFF, ("ffn_conv_b",)), "g_final": (D, ("g_final",))}
RPB_ROWS = NH * (2 * NA_ROWS - 1)
RPB_COLS = 4 * NA_ROWS - 1


def _pack_small(parts, after):
    arrs, places = [], []
    for name, _ in PACK:
        off, width = PACK_OFF[name]
        group = parts[name]
        rows = group[0].shape[0]
        row_w = sum(a.shape[1] for a in group)
        assert rows * row_w == width, (name, rows, row_w, width)
        col = 0
        for a in group:
            arrs.append(a)
            places.append([off + k * row_w + col for k in range(rows)])
            col += a.shape[1]

    def body(*refs):
        o_ref = refs[-1]
        o_ref[:, _o:PACK_N] = jnp.zeros((1, PACK_N - _o), F32)
        for ref, offs in zip(refs, places):
            n = ref.shape[1]
            for k, off in enumerate(offs):
                o_ref[:, off:off + n] = ref[k:k + 1, :]

    vmem = pl.BlockSpec(memory_space=pltpu.VMEM)
    return _pallas(body, name="pack_small_grads", out_shape=_sds((1, PACK_N), F32),
                   in_specs=[vmem] * len(arrs) + [pl.BlockSpec(memory_space=pl.ANY)], out_specs=vmem)(*arrs, after)


def _small_update(packs, w, m, v):
    names = list(VECTORS)

    def body(*refs):
        it = iter(refs)
        p_ref = next(it)
        wmv = {n: (next(it), next(it), next(it)) for n in names}
        outs = {n: (next(it), next(it), next(it), next(it)) for n in names}
        dmod_ref, cw_ref, fw_ref, rpb_ref, loss_ref = next(it), next(it), next(it), next(it), next(it)

        def total(name):
            off, width = PACK_OFF[name]
            acc = p_ref[0:1, off:off + width]
            for d in range(1, 8):
                acc = acc + p_ref[d:d + 1, off:off + width]
            return acc

        for n in names:
            width, segs = VECTORS[n]
            g = total(segs[0])
            if len(segs) > 1:
                extra = total(segs[1])
                ew = extra.shape[1]
                g = g + extra if ew == width else jnp.concatenate([g[:, :ew] + extra, g[:, ew:]], axis=1)
            w_ref, m_ref, v_ref = wmv[n]
            g_ref, d_ref, nm_ref, nv_ref = outs[n]
            g_ref[...] = g
            d_ref[...], nm_ref[...], nv_ref[...] = _adam_math(w_ref[...], g, m_ref[...], v_ref[...])

        o_dmod = PACK_OFF["dmod"][0]
        dmod_ref[...] = jnp.zeros_like(dmod_ref)
        dmod_ref[0:8, :] = p_ref[:, o_dmod:o_dmod + 6 * D]
        dmod_ref[8:9, 0:2 * D] = total("dmod_c")
        for ref, name, rows in ((cw_ref, "conv_w", CW), (fw_ref, "ffn_conv_w", 3), (rpb_ref, "rpb_rev", NH * 16)):
            flat = total(name)
            n = ref.shape[1]
            for k in range(rows):
                ref[k:k + 1, :] = flat[:, k * n:(k + 1) * n]
        loss_ref[...] = total("loss")

    ins = [packs] + [a[n] for n in names for a in (w, m, v)]
    out_shape = [_sds((1, VECTORS[n][0]), F32) for n in names for _ in range(4)]
    out_shape += [_sds((COND_ROWS, 6 * D), F32), _sds((CW, DC), F32), _sds((3, 2 * DFF), F32), _sds((NH * 16, LANES), F32),
                  _sds((1, LANES), F32)]
    res = _pallas(body, name="small_update", out_shape=out_shape)(*ins)
    per = {n: tuple(res[4 * i:4 * i + 4]) for i, n in enumerate(names)}
    return (per, *res[4 * len(names):])


def _rpb_update(rev, w, m, v):
    def body(r_ref, w_ref, m_ref, v_ref, g_ref, d_ref, nm_ref, nv_ref):
        li = lax.broadcasted_iota(jnp.int32, (LANES, LANES), 0)
        co = lax.broadcasted_iota(jnp.int32, (LANES, LANES), 1)
        lane_of_co0 = GW - 1 + RPB_COLS // 2
        unflip = jnp.where((li == lane_of_co0 - co) & (co < RPB_COLS), 1.0, 0.0).astype(F32)
        g_all = jnp.dot(r_ref[...], unflip, preferred_element_type=F32, precision=HI)
        nr = 2 * NA_ROWS - 1
        for h in range(NH):
            rows = slice(h * nr, (h + 1) * nr)
            g = g_all[h * 16:h * 16 + nr, 0:RPB_COLS]
            g_ref[rows, :] = g
            d_ref[rows, :], nm_ref[rows, :], nv_ref[rows, :] = _adam_math(w_ref[rows, :], g, m_ref[rows, :], v_ref[rows, :])

    return _pallas(body, name="rpb_update", out_shape=[_sds((RPB_ROWS, RPB_COLS), F32)] * 4)(rev, w, m, v)


def kernel(x, c, ctx, c_ctx, w_mod, b_mod, g_norm1, w_in, rpb, conv_w, conv_b, ln_g, ln_b, w_out, g_norm2, w_up, ffn_conv_w, ffn_conv_b, w_down, g_final, loss_target, m_c_ctx, m_w_mod, m_b_mod, m_g_norm1, m_w_in, m_rpb, m_conv_w, m_conv_b, m_ln_g, m_ln_b, m_w_out, m_g_norm2, m_w_up, m_ffn_conv_w, m_ffn_conv_b, m_w_down, m_g_final, v_c_ctx, v_w_mod, v_b_mod, v_g_norm1, v_w_in, v_rpb, v_conv_w, v_conv_b, v_ln_g, v_ln_b, v_w_out, v_g_norm2, v_w_up, v_ffn_conv_w, v_ffn_conv_b, v_w_down, v_g_final):
    w = dict(c_ctx=c_ctx, w_mod=w_mod, b_mod=b_mod, g_norm1=g_norm1, w_in=w_in, rpb=rpb, conv_w=conv_w, conv_b=conv_b,
             ln_g=ln_g, ln_b=ln_b, w_out=w_out, g_norm2=g_norm2, w_up=w_up, ffn_conv_w=ffn_conv_w, ffn_conv_b=ffn_conv_b,
             w_down=w_down, g_final=g_final)
    mom = dict(c_ctx=m_c_ctx, w_mod=m_w_mod, b_mod=m_b_mod, g_norm1=m_g_norm1, w_in=m_w_in, rpb=m_rpb, conv_w=m_conv_w,
               conv_b=m_conv_b, ln_g=m_ln_g, ln_b=m_ln_b, w_out=m_w_out, g_norm2=m_g_norm2, w_up=m_w_up,
               ffn_conv_w=m_ffn_conv_w, ffn_conv_b=m_ffn_conv_b, w_down=m_w_down, g_final=m_g_final)
    var = dict(c_ctx=v_c_ctx, w_mod=v_w_mod, b_mod=v_b_mod, g_norm1=v_g_norm1, w_in=v_w_in, rpb=v_rpb, conv_w=v_conv_w,
               conv_b=v_conv_b, ln_g=v_ln_g, ln_b=v_ln_b, w_out=v_w_out, g_norm2=v_g_norm2, w_up=v_w_up,
               ffn_conv_w=v_ffn_conv_w, ffn_conv_b=v_ffn_conv_b, w_down=v_w_down, g_final=v_g_final)
    xi, yi, ci = lax.axis_index("x"), lax.axis_index("y"), lax.axis_index("c")
    dev = (4 * xi + 2 * yi + ci).astype(jnp.int32).reshape(1)
    chip = (2 * xi + yi).astype(jnp.int32).reshape(1)
    core = ci.astype(jnp.int32).reshape(1)
    c_ctx2 = c_ctx.reshape(1, D)
    g_final2 = g_final.reshape(1, D)
    mom["g_final"], var["g_final"] = m_g_final.reshape(1, D), v_g_final.reshape(1, D)

    got = _gather_small(_pack_cond(c, ffn_conv_w[0], conv_w[0]), "gather_cond")
    cond, ffn_w_all, conv_w_all = _unpack_cond(got, c_ctx2)

    mods = _gather_small(_mod_shard(cond, w_mod[0], b_mod, chip), "gather_mod")
    mod_me, mod_c = _unpack_mod(mods, dev)

    shards = {n: _cast_into_whole(n, w[n][0], chip) for n in BIG_NAMES}
    sems_in, first, token_in = _gather_start([shards["w_in"]], ("w_in",), mod_me, "w_in")
    sems, late, token = _gather_start([shards[n] for n in LATE_NAMES], LATE_NAMES, token_in, "late")
    mod_me = mod_me + token[0:1, 0:1]

    def w_in_all(after):
        arrived = _gather_wait(sems_in, first, ("w_in",), after, "w_in")
        return _forward_halves(list(arrived), ("w_in",), "w_in")[0]

    def late_weights(after):
        arrived = list(_gather_wait(sems, late, LATE_NAMES, after, "late"))
        (w_out_all,) = _forward_halves(arrived[:1], LATE_NAMES[:1], "w_out")
        fsems, passing, _ = _forward_start(arrived[1:], LATE_NAMES[1:], "ffn", after=w_out_all)
        return w_out_all, lambda after2: _forward_wait(fsems, passing, LATE_NAMES[1:], after2, "ffn")

    rpb_rev = jnp.pad(rpb[0][:, :, ::-1], ((0, 0), (0, 1), (48, LANES - 48 - RPB_COLS))).reshape(NH * 16, LANES)
    vec = dict(g_norm1=g_norm1, g_norm2=g_norm2, g_final=g_final2, conv_w=conv_w_all, conv_b=conv_b, ln_g=ln_g, ln_b=ln_b,
               ffn_conv_w=ffn_w_all, ffn_conv_b=ffn_conv_b)
    started = []

    def begin_early(d_up, d_down):
        started.append(_swap_start([d_up, d_down], EARLY_GRADS, "grad_swap_start_early"))

    def carry_on_early(after):
        sems_, grads_, lands_, _ = started.pop()
        grads_, lands_ = _swap_wait(sems_, grads_, lands_, EARLY_GRADS, after, "grad_swap_wait_early")
        parts_ = [_add_halves(n, grads_[i], lands_[i], core) for i, n in enumerate(EARLY_GRADS)]
        started.append(_exchange_start(parts_, EARLY_GRADS, "grad_exchange_start_early"))
        return started[0][3][0:1, 0:1]

    loss_p, grad_x, d_in, d_out, d_up, d_down, small = _local_step(
        x[0], ctx[0], loss_target[0], mod_me, mod_c, vec, w_in_all, late_weights, rpb_rev, (begin_early, carry_on_early))

    out = {}
    sems_, grads_, lands_, _ = _swap_start([d_in, d_out], LAST_GRADS, "grad_swap_start_last")
    early_own = _reduce_finish(started[0], EARLY_GRADS, grad_x, chip, "early")
    behind_swap = small["rpb_rev"][0:1, 0:1] + small["g_norm1"][1][0:1, 0:1] + early_own[0][0:1, 0:1]
    grads_, lands_ = _swap_wait(sems_, grads_, lands_, LAST_GRADS, behind_swap, "grad_swap_wait_last")
    parts_ = [_add_halves(n, grads_[i], lands_[i], core) for i, n in enumerate(LAST_GRADS)]
    last_started = _exchange_start(parts_, LAST_GRADS, "grad_exchange_start_last")
    early_other = _send_halves(early_own, "grad_send_early", after=last_started[3])
    for i, n in enumerate(EARLY_GRADS):
        out[n] = _adamw_halves(n, w[n][0], early_own[i], early_other[i], mom[n][0], var[n][0], core, early_other[i])
    behind_early = out[EARLY_GRADS[0]][1][0:1, 0:1] + out[EARLY_GRADS[1]][1][0:1, 0:1]

    parts = dict(dmod=small["dmod"], dmod_c=small["dmod_c"], g_norm1=[small["g_norm1"][0]], g_norm1_ctx=[small["g_norm1"][1]],
                 g_norm2=[small["g_norm2"]], g_final=[small["g_final"]], conv_b=[small["conv_b"]], ln_g=[small["ln_g"]],
                 ln_b=[small["ln_b"]], ffn_conv_b=small["ffn_conv_b"], ffn_conv_w=small["ffn_conv_w"],
                 conv_w=[small["conv_w"]], rpb_rev=[small["rpb_rev"]], loss=[loss_p])
    pack = _pack_small(parts, after=behind_early).reshape(8, PACK_N // 8)
    packs = _gather_small(pack, "gather_small_grads").reshape(8, PACK_N)
    w2 = dict(w, g_final=g_final2)
    per, dmod_all, g_conv_w_all, g_ffn_w_all, g_rpb_rev, loss_row = _small_update(packs, w2, mom, var)

    out.update(per)
    out["c_ctx"] = _cond_update(
        _gather_small(_cond_grad_partial(dmod_all, w_mod[0], chip), "gather_cond_grad"),
        c_ctx2, m_c_ctx.reshape(1, D), v_c_ctx.reshape(1, D))
    out["w_mod"] = _mod_weight_update(cond, dmod_all, w_mod[0], m_w_mod[0], v_w_mod[0], chip)
    behind = out["w_mod"][1][0:1, 0:1] + out["c_ctx"][1][0:1, 0:1]
    last_own = _reduce_finish(last_started, LAST_GRADS, behind, chip, "last")
    last_other = _send_halves(last_own, "grad_send_last", after=last_own[0])
    for i, n in enumerate(LAST_GRADS):
        out[n] = _adamw_halves(n, w[n][0], last_own[i], last_other[i], mom[n][0], var[n][0], core, last_other[i])
    out["conv_w"] = _adamw_cols(conv_w[0], g_conv_w_all, m_conv_w[0], v_conv_w[0], chip, "adamw_conv_w")
    out["ffn_conv_w"] = _adamw_cols(ffn_conv_w[0], g_ffn_w_all, m_ffn_conv_w[0], v_ffn_conv_w[0], chip, "adamw_ffn_conv_w")
    flat = lambda a: a.reshape(RPB_ROWS, RPB_COLS)
    out["rpb"] = _rpb_update(g_rpb_rev, flat(rpb), flat(m_rpb), flat(v_rpb))

    res = [[out[n][k].reshape(w[n].shape) for n in WEIGHTS] for k in range(4)]
    return (loss_row[0, 0], grad_x[None], *res[0], *res[1], *res[2], *res[3])
```

```python
import jax
import jax.numpy as jnp
from jax import lax
from jax.experimental import pallas as pl
from jax.experimental.pallas import tpu as pltpu

F32 = jnp.float32
BF16 = jnp.bfloat16
MXU_DTYPE = jnp.bfloat16

D = 1024
CTX = 256
GW = 64
DA = 512
NH = 8
HD = 64
DC = 512
CW = 31
DFF = 2816
NIN = 3 * DA + 2 * DC
EPS = 1e-6
SCALE = HD ** -0.5
NEG = -1e30
NA_ROWS = 8
PAIR_ROWS = NA_ROWS + 1
TAB_BLOCKS = 17
LANES = 128
VMEM_LIMIT = 56 * 1024 * 1024

ADAM_LR = 0.001
ADAM_B1 = 0.9
ADAM_B2 = 0.999
ADAM_EPS = 1e-08
ADAM_WD = 0.01
ADAM_STEP = 10

MESH = pl.DeviceIdType.MESH


def _pallas(body, *, name, semantics=None, vmem=VMEM_LIMIT, prefetch=0, **kw):
    params = dict(vmem_limit_bytes=vmem)
    if semantics is not None:
        params["dimension_semantics"] = semantics
    if prefetch:
        kw["grid_spec"] = pltpu.PrefetchScalarGridSpec(
            num_scalar_prefetch=prefetch, grid=kw.pop("grid"), in_specs=kw.pop("in_specs"), out_specs=kw.pop("out_specs"),
            scratch_shapes=kw.pop("scratch_shapes", ()))
    return pl.pallas_call(body, name=name, compiler_params=pltpu.CompilerParams(**params), **kw)


def _sds(shape, dtype):
    return jax.ShapeDtypeStruct(shape, dtype)


def _vec_spec(n):
    return pl.BlockSpec((1, n), lambda *_: (0, 0))


def _colsum8(x):
    t, n = x.shape
    return jnp.sum(x.reshape(t // 8, 8, n), axis=0)


def _sigmoid(x):
    return 0.5 * jnp.tanh(0.5 * x) + 0.5


def _mm(a, b, *, mode, m, n, k, tm, tn, tk, out_dtype, name, a_off=(0, 0), b_off=(0, 0),
        out_total=None, o_off=(0, 0), into=None):
    a_list = list(a) if isinstance(a, (list, tuple)) else [a]
    b_list = list(b) if isinstance(b, (list, tuple)) else [b]
    assert m % tm == 0 and n % tn == 0 and k % tk == 0, (name, m, n, k, tm, tn, tk)
    gi, gj, nk = m // tm, n // tn, k // tk
    dims = {"nn": (((1,), (0,)), ((), ())), "nt": (((1,), (1,)), ((), ())), "tn": (((0,), (0,)), ((), ()))}[mode]

    if len(a_list) > 1:
        assert mode != "tn" and nk == 1 and sum(x.shape[1] for x in a_list) == k
        a_specs = [pl.BlockSpec((tm, x.shape[1]), lambda i, j, kk: (i, 0)) for x in a_list]
    elif mode == "tn":
        a_specs = [pl.BlockSpec((tk, tm), lambda i, j, kk: (kk + a_off[0], i + a_off[1]))]
    else:
        a_specs = [pl.BlockSpec((tm, tk), lambda i, j, kk: (i + a_off[0], kk + a_off[1]))]
    if len(b_list) > 1:
        assert mode == "tn" and gj == 1 and sum(x.shape[1] for x in b_list) == n
        b_specs = [pl.BlockSpec((tk, x.shape[1]), lambda i, j, kk: (kk, 0)) for x in b_list]
    elif mode == "nt":
        b_specs = [pl.BlockSpec((tn, tk), lambda i, j, kk: (j + b_off[0], kk + b_off[1]))]
    else:
        b_specs = [pl.BlockSpec((tk, tn), lambda i, j, kk: (kk + b_off[0], j + b_off[1]))]

    na, nb = len(a_list), len(b_list)
    in_place = nk > 1 and out_dtype == F32
    n_in = na + nb + (into is not None)

    def body(*refs):
        a_refs, b_refs, o_ref = refs[:na], refs[na:na + nb], refs[n_in]
        acc = o_ref if in_place else (refs[n_in + 1] if nk > 1 else None)
        kk = pl.program_id(2)

        def whole(piece_refs):
            vals = [r[...].astype(MXU_DTYPE) for r in piece_refs]
            return vals[0] if len(vals) == 1 else jnp.concatenate(vals, axis=1)

        p = lax.dot_general(whole(a_refs), whole(b_refs), dims, preferred_element_type=F32)
        if nk == 1:
            o_ref[...] = p.astype(out_dtype)
            return

        @pl.when(kk == 0)
        def _():
            acc[...] = p

        @pl.when(kk > 0)
        def _():
            acc[...] += p

        if not in_place:
            @pl.when(kk == nk - 1)
            def _():
                o_ref[...] = acc[...].astype(out_dtype)

    ins = [*a_list, *b_list]
    in_specs = a_specs + b_specs
    extra = {}
    if into is not None:
        extra["input_output_aliases"] = {len(ins): 0}
        ins.append(into)
        in_specs.append(pl.BlockSpec(memory_space=pl.ANY))
    return _pallas(
        body, name=name, grid=(gi, gj, nk), in_specs=in_specs,
        out_specs=pl.BlockSpec((tm, tn), lambda i, j, kk: (i + o_off[0], j + o_off[1])),
        out_shape=_sds(out_total or (m, n), out_dtype),
        scratch_shapes=[pltpu.VMEM((tm, tn), F32)] if nk > 1 and not in_place else [],
        semantics=("parallel", "parallel", "arbitrary"), **extra,
    )(*ins)


ROW_TILE = 256


def _rmsmod_fwd(x, ctx, g, sc, sh, csc, csh):
    s = x.shape[0]
    nt = s // ROW_TILE
    assert ctx.shape[0] == ROW_TILE

    def body(x_ref, c_ref, g_ref, sc_ref, sh_ref, csc_ref, csh_ref, o_ref):
        is_ctx = pl.program_id(0) == nt
        xv = jnp.where(is_ctx, c_ref[...], x_ref[...])
        scv = jnp.where(is_ctx, csc_ref[...], sc_ref[...])
        shv = jnp.where(is_ctx, csh_ref[...], sh_ref[...])
        r = lax.rsqrt(jnp.mean(xv * xv, axis=-1, keepdims=True) + EPS)
        y = xv * r * g_ref[...]
        o_ref[...] = (y * (1.0 + scv) + shv).astype(o_ref.dtype)

    return _pallas(
        body, name="rmsmod1_fwd", grid=(nt + 1,),
        in_specs=[pl.BlockSpec((ROW_TILE, D), lambda i: (jnp.minimum(i, nt - 1), 0)),
                  pl.BlockSpec((ROW_TILE, D), lambda i: (0, 0))] + [_vec_spec(D)] * 5,
        out_specs=pl.BlockSpec((ROW_TILE, D), lambda i: (i, 0)),
        out_shape=_sds((s + CTX, D), MXU_DTYPE),
        semantics=("arbitrary",),
    )(x, ctx, g, sc, sh, csc, csh)


def _resid_rmsmod_fwd(x, y, gt, g, sc, sh):
    s = x.shape[0]

    def body(x_ref, y_ref, gt_ref, g_ref, sc_ref, sh_ref, x1_ref, h_ref):
        x1 = x_ref[...] + gt_ref[...] * y_ref[...]
        x1_ref[...] = x1
        r = lax.rsqrt(jnp.mean(x1 * x1, axis=-1, keepdims=True) + EPS)
        h_ref[...] = ((x1 * r * g_ref[...]) * (1.0 + sc_ref[...]) + sh_ref[...]).astype(h_ref.dtype)

    row = pl.BlockSpec((ROW_TILE, D), lambda i: (i, 0))
    return _pallas(
        body, name="resid_rmsmod2_fwd", grid=(s // ROW_TILE,),
        in_specs=[row, row] + [_vec_spec(D)] * 4,
        out_specs=[row, row],
        out_shape=[_sds((s, D), F32), _sds((s, D), MXU_DTYPE)],
        semantics=("parallel",),
    )(x, y, gt, g, sc, sh)


def _final_fwd_bwd(x1, z, gt2, gf, tgt):
    s = x1.shape[0]
    nt = s // ROW_TILE

    def body(x1_ref, z_ref, gt_ref, gf_ref, t_ref, dx2_ref, dz_ref, loss_ref, dgt_ref, dgf_ref, a_loss, a_gt, a_gf):
        i = pl.program_id(0)

        @pl.when(i == 0)
        def _():
            a_loss[...] = jnp.zeros_like(a_loss)
            a_gt[...] = jnp.zeros_like(a_gt)
            a_gf[...] = jnp.zeros_like(a_gf)

        zv = z_ref[...]
        gt = gt_ref[...]
        gf_ = gf_ref[...]
        x2 = x1_ref[...] + gt * zv
        r = lax.rsqrt(jnp.mean(x2 * x2, axis=-1, keepdims=True) + EPS)
        xn = x2 * r
        e = xn * gf_ - t_ref[...]
        a_loss[...] += _colsum8(e * e)
        dyo = e * (1.0 / D)
        a_gf[...] += _colsum8(dyo * xn)
        gdy = gf_ * dyo
        dx2 = r * gdy - xn * (r * r) * jnp.mean(x2 * gdy, axis=-1, keepdims=True)
        dx2_ref[...] = dx2
        dz_ref[...] = (gt * dx2).astype(dz_ref.dtype)
        a_gt[...] += _colsum8(dx2 * zv)

        @pl.when(i == nt - 1)
        def _():
            tot = jnp.sum(jnp.sum(a_loss[...], axis=0, keepdims=True), axis=1, keepdims=True) * (0.5 / D)
            loss_ref[...] = jnp.broadcast_to(tot, loss_ref.shape)
            dgt_ref[...] = jnp.sum(a_gt[...], axis=0, keepdims=True)
            dgf_ref[...] = jnp.sum(a_gf[...], axis=0, keepdims=True)

    row = pl.BlockSpec((ROW_TILE, D), lambda i: (i, 0))
    return _pallas(
        body, name="final_norm_loss", grid=(nt,),
        in_specs=[row, row, _vec_spec(D), _vec_spec(D), row],
        out_specs=[row, row, _vec_spec(LANES), _vec_spec(D), _vec_spec(D)],
        out_shape=[_sds((s, D), F32), _sds((s, D), MXU_DTYPE), _sds((1, LANES), F32), _sds((1, D), F32), _sds((1, D), F32)],
        scratch_shapes=[pltpu.VMEM((8, D), F32)] * 3,
        semantics=("arbitrary",),
    )(x1, z, gt2, gf, tgt)


def _rmsmod_bwd(xin, dh, g, sc, *, name, dh_row_off=0, add=None, resid=None):
    s = xin.shape[0]
    nt = s // ROW_TILE
    want_dx = add is not None
    assert resid is None or want_dx

    def body(*refs):
        it = iter(refs)
        x_ref, dh_ref, g_ref, sc_ref = next(it), next(it), next(it), next(it)
        add_ref = next(it) if want_dx else None
        gt_ref, y_ref = (next(it), next(it)) if resid is not None else (None, None)
        dsh_ref, dsc_ref, dg_ref = next(it), next(it), next(it)
        dx_ref = next(it) if want_dx else None
        dy_ref, dgt_ref = (next(it), next(it)) if resid is not None else (None, None)
        a_sh, a_sc, a_g = next(it), next(it), next(it)
        a_gt = next(it) if resid is not None else None
        i = pl.program_id(0)

        @pl.when(i == 0)
        def _():
            a_sh[...] = jnp.zeros_like(a_sh)
            a_sc[...] = jnp.zeros_like(a_sc)
            a_g[...] = jnp.zeros_like(a_g)
            if a_gt is not None:
                a_gt[...] = jnp.zeros_like(a_gt)

        xv = x_ref[...]
        dhv = dh_ref[...]
        gv = g_ref[...]
        r = lax.rsqrt(jnp.mean(xv * xv, axis=-1, keepdims=True) + EPS)
        xn = xv * r
        a_sh[...] += _colsum8(dhv)
        a_sc[...] += _colsum8(dhv * (xn * gv))
        dn = dhv * (1.0 + sc_ref[...])
        a_g[...] += _colsum8(dn * xn)
        if want_dx:
            gdn = gv * dn
            dx = add_ref[...] + r * gdn - xn * (r * r) * jnp.mean(xv * gdn, axis=-1, keepdims=True)
            dx_ref[...] = dx
            if resid is not None:
                dy_ref[...] = (gt_ref[...] * dx).astype(dy_ref.dtype)
                a_gt[...] += _colsum8(dx * y_ref[...])

        @pl.when(i == nt - 1)
        def _():
            dsh_ref[...] = jnp.sum(a_sh[...], axis=0, keepdims=True)
            dsc_ref[...] = jnp.sum(a_sc[...], axis=0, keepdims=True)
            dg_ref[...] = jnp.sum(a_g[...], axis=0, keepdims=True)
            if a_gt is not None:
                dgt_ref[...] = jnp.sum(a_gt[...], axis=0, keepdims=True)

    row = pl.BlockSpec((ROW_TILE, D), lambda i: (i, 0))
    ins = [xin, dh, g, sc]
    in_specs = [row, pl.BlockSpec((ROW_TILE, D), lambda i: (i + dh_row_off, 0)), _vec_spec(D), _vec_spec(D)]
    out_specs = [_vec_spec(D)] * 3
    out_shape = [_sds((1, D), F32)] * 3
    scratch = [pltpu.VMEM((8, D), F32)] * 3
    if want_dx:
        ins.append(add)
        in_specs.append(row)
        out_specs.append(row)
        out_shape.append(_sds((s, D), F32))
    if resid is not None:
        ins += [resid[0], resid[1]]
        in_specs += [_vec_spec(D), row]
        out_specs += [row, _vec_spec(D)]
        out_shape += [_sds((s, D), MXU_DTYPE), _sds((1, D), F32)]
        scratch.append(pltpu.VMEM((8, D), F32))
    return _pallas(body, name=name, grid=(nt,), in_specs=in_specs, out_specs=out_specs, out_shape=out_shape,
                   scratch_shapes=scratch, semantics=("arbitrary",))(*ins)


FF_TILE = 128
FF_CHUNK = 128
HALO = 8


def _shift3(pad_ref, r0, ch):
    return tuple(pad_ref[pl.ds(r0 + HALO + d, ch), :] for d in (-1, 0, 1))


def _fill_padded(pad_ref, src_ref, s, ch, halo):
    zeros = jnp.zeros((halo, pad_ref.shape[1]), F32)
    pad_ref[0:halo, :] = zeros
    pad_ref[s + halo:s + 2 * halo, :] = zeros

    def cp(c, carry):
        r0 = pl.multiple_of(c * ch, ch)
        pad_ref[pl.ds(r0 + halo, ch), :] = src_ref[pl.ds(r0, ch), :].astype(F32)
        return carry

    lax.fori_loop(0, s // ch, cp, 0)


def _ffn_act_fwd(u, w, b):
    s = u.shape[0]
    nj = DFF // FF_TILE
    ch = FF_CHUNK

    def body(ug_ref, uv_ref, wg_ref, wv_ref, bg_ref, bv_ref, f_ref, gpad, vpad):
        _fill_padded(gpad, ug_ref, s, ch, HALO)
        _fill_padded(vpad, uv_ref, s, ch, HALO)

        def conv(pad, w_ref, b_ref, r0):
            prev, cur, nxt = _shift3(pad, r0, ch)
            return w_ref[0:1, :] * prev + w_ref[1:2, :] * cur + w_ref[2:3, :] * nxt + b_ref[...]

        def step(c, carry):
            r0 = pl.multiple_of(c * ch, ch)
            gc = conv(gpad, wg_ref, bg_ref, r0)
            vc = conv(vpad, wv_ref, bv_ref, r0)
            f_ref[pl.ds(r0, ch), :] = (gc * _sigmoid(gc) * vc).astype(f_ref.dtype)
            return carry

        lax.fori_loop(0, s // ch, step, 0)

    col = lambda off: pl.BlockSpec((s, FF_TILE), lambda j: (0, j + off))
    wsp = lambda off: pl.BlockSpec((3, FF_TILE), lambda j: (0, j + off))
    bsp = lambda off: pl.BlockSpec((1, FF_TILE), lambda j: (0, j + off))
    return _pallas(
        body, name="ffn_act_fwd", grid=(nj,),
        in_specs=[col(0), col(nj), wsp(0), wsp(nj), bsp(0), bsp(nj)],
        out_specs=col(0), out_shape=_sds((s, DFF), MXU_DTYPE),
        scratch_shapes=[pltpu.VMEM((s + 2 * HALO, FF_TILE), F32)] * 2,
        semantics=("parallel",),
    )(u, u, w, w, b, b)


def _ffn_act_bwd(u, df, w, b):
    s = u.shape[0]
    nj = DFF // FF_TILE
    ch = FF_CHUNK

    def body(ug_ref, uv_ref, df_ref, wg_ref, wv_ref, bg_ref, bv_ref,
             dug_ref, duv_ref, dwg_ref, dwv_ref, dbg_ref, dbv_ref, gpad, vpad, dgpad, dvpad, acc):
        _fill_padded(gpad, ug_ref, s, ch, HALO)
        _fill_padded(vpad, uv_ref, s, ch, HALO)
        zeros = jnp.zeros((HALO, FF_TILE), F32)
        for p in (dgpad, dvpad):
            p[0:HALO, :] = zeros
            p[s + HALO:s + 2 * HALO, :] = zeros
        acc[...] = jnp.zeros_like(acc)

        def step(c, carry):
            r0 = pl.multiple_of(c * ch, ch)
            gs = _shift3(gpad, r0, ch)
            vs = _shift3(vpad, r0, ch)
            gc = wg_ref[0:1, :] * gs[0] + wg_ref[1:2, :] * gs[1] + wg_ref[2:3, :] * gs[2] + bg_ref[...]
            vc = wv_ref[0:1, :] * vs[0] + wv_ref[1:2, :] * vs[1] + wv_ref[2:3, :] * vs[2] + bv_ref[...]
            sg = _sigmoid(gc)
            dfv = df_ref[pl.ds(r0, ch), :].astype(F32)
            dgc = dfv * vc * (sg * (1.0 + gc * (1.0 - sg)))
            dvc = dfv * (gc * sg)
            dgpad[pl.ds(r0 + HALO, ch), :] = dgc
            dvpad[pl.ds(r0 + HALO, ch), :] = dvc
            for t in range(3):
                acc[8 * t:8 * t + 8, :] += _colsum8(dgc * gs[t])
                acc[24 + 8 * t:32 + 8 * t, :] += _colsum8(dvc * vs[t])
            acc[48:56, :] += _colsum8(dgc)
            acc[56:64, :] += _colsum8(dvc)
            return carry

        lax.fori_loop(0, s // ch, step, 0)

        def step2(c, carry):
            r0 = pl.multiple_of(c * ch, ch)
            for pad, w_ref, o_ref in ((dgpad, wg_ref, dug_ref), (dvpad, wv_ref, duv_ref)):
                prev, cur, nxt = _shift3(pad, r0, ch)
                o_ref[pl.ds(r0, ch), :] = (w_ref[0:1, :] * nxt + w_ref[1:2, :] * cur + w_ref[2:3, :] * prev).astype(o_ref.dtype)
            return carry

        lax.fori_loop(0, s // ch, step2, 0)
        for t in range(3):
            dwg_ref[t:t + 1, :] = jnp.sum(acc[8 * t:8 * t + 8, :], axis=0, keepdims=True)
            dwv_ref[t:t + 1, :] = jnp.sum(acc[24 + 8 * t:32 + 8 * t, :], axis=0, keepdims=True)
        dbg_ref[...] = jnp.sum(acc[48:56, :], axis=0, keepdims=True)
        dbv_ref[...] = jnp.sum(acc[56:64, :], axis=0, keepdims=True)

    col = lambda off: pl.BlockSpec((s, FF_TILE), lambda j: (0, j + off))
    wsp = lambda off: pl.BlockSpec((3, FF_TILE), lambda j: (0, j + off))
    bsp = lambda off: pl.BlockSpec((1, FF_TILE), lambda j: (0, j + off))
    return _pallas(
        body, name="ffn_act_bwd", grid=(nj,),
        in_specs=[col(0), col(nj), col(0), wsp(0), wsp(nj), bsp(0), bsp(nj)],
        out_specs=[col(0), col(0), wsp(0), wsp(0), bsp(0), bsp(0)],
        out_shape=[_sds((s, DFF), MXU_DTYPE)] * 2 + [_sds((3, DFF), F32)] * 2 + [_sds((1, DFF), F32)] * 2,
        scratch_shapes=[pltpu.VMEM((s + 2 * HALO, FF_TILE), F32)] * 4 + [pltpu.VMEM((64, FF_TILE), F32)],
        semantics=("parallel",),
    )(u, u, df, w, w, b, b)


CONV_CHUNK = 64
CONV_HALO = 16


def _tap(pad_ref, r0, k):
    return pad_ref[pl.ds(r0 + CONV_HALO - CW // 2 + k, CONV_CHUNK), :]


def _glu_into(pad_ref, a_ref, g_ref, s):
    zeros = jnp.zeros((CONV_HALO, LANES), F32)
    pad_ref[0:CONV_HALO, :] = zeros
    pad_ref[s + CONV_HALO:s + 2 * CONV_HALO, :] = zeros

    def cp(c, carry):
        r0 = pl.multiple_of(c * ROW_TILE, ROW_TILE)
        pad_ref[pl.ds(r0 + CONV_HALO, ROW_TILE), :] = a_ref[pl.ds(r0, ROW_TILE), :] * _sigmoid(g_ref[pl.ds(r0, ROW_TILE), :])
        return carry

    lax.fori_loop(0, s // ROW_TILE, cp, 0)


def _conf_conv_fwd(ag, conv_w, conv_b):
    s = ag.shape[0]
    nc = DC // LANES

    def body(a_ref, g_ref, w_ref, b_ref, o_ref, upad):
        _glu_into(upad, a_ref, g_ref, s)

        def step(c, carry):
            r0 = pl.multiple_of(c * CONV_CHUNK, CONV_CHUNK)
            acc = jnp.broadcast_to(b_ref[...], (CONV_CHUNK, LANES))
            for k in range(CW):
                acc = acc + w_ref[k:k + 1, :] * _tap(upad, r0, k)
            o_ref[pl.ds(r0, CONV_CHUNK), :] = acc
            return carry

        lax.fori_loop(0, s // CONV_CHUNK, step, 0)

    col = lambda off: pl.BlockSpec((s, LANES), lambda c: (0, c + off))
    return _pallas(
        body, name="conf_conv_fwd", grid=(nc,),
        in_specs=[col(0), col(nc), pl.BlockSpec((CW, LANES), lambda c: (0, c)), pl.BlockSpec((1, LANES), lambda c: (0, c))],
        out_specs=col(0), out_shape=_sds((s, DC), F32),
        scratch_shapes=[pltpu.VMEM((s + 2 * CONV_HALO, LANES), F32)],
        semantics=("parallel",),
    )(ag, ag, conv_w, conv_b)


def _ln_stats(x):
    mu = jnp.mean(x, axis=-1, keepdims=True)
    xc = x - mu
    var = jnp.mean(xc * xc, axis=-1, keepdims=True)
    rstd = lax.rsqrt(var + EPS)
    return xc * rstd, rstd


def _conf_ln_fwd(u1, ln_g, ln_b, ycat):
    s = u1.shape[0]

    def body(u_ref, g_ref, b_ref, ycat_ref, o_ref):
        del ycat_ref
        xhat, _ = _ln_stats(u_ref[...])
        y = xhat * g_ref[...] + b_ref[...]
        o_ref[...] = (y * _sigmoid(y)).astype(o_ref.dtype)

    return _pallas(
        body, name="conf_ln_fwd", grid=(s // ROW_TILE,),
        in_specs=[pl.BlockSpec((ROW_TILE, DC), lambda i: (i, 0)), _vec_spec(DC), _vec_spec(DC),
                  pl.BlockSpec(memory_space=pl.ANY)],
        out_specs=pl.BlockSpec((ROW_TILE, DC), lambda i: (i, 1)),
        out_shape=_sds(ycat.shape, ycat.dtype),
        input_output_aliases={3: 0},
        semantics=("parallel",),
    )(u1, ln_g, ln_b, ycat)


def _conf_ln_bwd(dycat, u1, ln_g, ln_b):
    s = u1.shape[0]
    nt = s // ROW_TILE

    def body(dy_ref, u_ref, g_ref, b_ref, du_ref, dg_ref, db_ref, a_g, a_b):
        i = pl.program_id(0)

        @pl.when(i == 0)
        def _():
            a_g[...] = jnp.zeros_like(a_g)
            a_b[...] = jnp.zeros_like(a_b)

        xhat, rstd = _ln_stats(u_ref[...])
        gv = g_ref[...]
        y = xhat * gv + b_ref[...]
        sg = _sigmoid(y)
        dyl = dy_ref[...] * (sg * (1.0 + y * (1.0 - sg)))
        a_g[...] += _colsum8(dyl * xhat)
        a_b[...] += _colsum8(dyl)
        dxh = dyl * gv
        du_ref[...] = rstd * (dxh - jnp.mean(dxh, axis=-1, keepdims=True)
                              - xhat * jnp.mean(dxh * xhat, axis=-1, keepdims=True))

        @pl.when(i == nt - 1)
        def _():
            dg_ref[...] = jnp.sum(a_g[...], axis=0, keepdims=True)
            db_ref[...] = jnp.sum(a_b[...], axis=0, keepdims=True)

    return _pallas(
        body, name="conf_ln_bwd", grid=(nt,),
        in_specs=[pl.BlockSpec((ROW_TILE, DC), lambda i: (i, 1)), pl.BlockSpec((ROW_TILE, DC), lambda i: (i, 0)),
                  _vec_spec(DC), _vec_spec(DC)],
        out_specs=[pl.BlockSpec((ROW_TILE, DC), lambda i: (i, 0)), _vec_spec(DC), _vec_spec(DC)],
        out_shape=[_sds((s, DC), F32), _sds((1, DC), F32), _sds((1, DC), F32)],
        scratch_shapes=[pltpu.VMEM((8, DC), F32)] * 2,
        semantics=("arbitrary",),
    )(dycat, u1, ln_g, ln_b)


def _conf_conv_bwd(ag, du1, conv_w, rows_out):
    s = ag.shape[0]
    nc = DC // LANES

    def body(a_ref, g_ref, d_ref, w_ref, da_ref, dg_ref, dw_ref, db_ref, upad, dpad, acc):
        _glu_into(upad, a_ref, g_ref, s)
        _fill_padded(dpad, d_ref, s, ROW_TILE, CONV_HALO)
        acc[...] = jnp.zeros_like(acc)

        def step(c, carry):
            r0 = pl.multiple_of(c * CONV_CHUNK, CONV_CHUNK)
            dcur = dpad[pl.ds(r0 + CONV_HALO, CONV_CHUNK), :]
            du0 = jnp.zeros((CONV_CHUNK, LANES), F32)
            for k in range(CW):
                du0 = du0 + w_ref[k:k + 1, :] * _tap(dpad, r0, CW - 1 - k)
                acc[8 * k:8 * k + 8, :] += _colsum8(dcur * _tap(upad, r0, k))
            acc[8 * CW:8 * CW + 8, :] += _colsum8(dcur)
            av = a_ref[pl.ds(r0, CONV_CHUNK), :]
            sg = _sigmoid(g_ref[pl.ds(r0, CONV_CHUNK), :])
            da_ref[pl.ds(r0, CONV_CHUNK), :] = (du0 * sg).astype(da_ref.dtype)
            dg_ref[pl.ds(r0, CONV_CHUNK), :] = (du0 * av * (sg * (1.0 - sg))).astype(dg_ref.dtype)
            return carry

        lax.fori_loop(0, s // CONV_CHUNK, step, 0)
        if rows_out > s:
            zeros = jnp.zeros((rows_out - s, LANES), da_ref.dtype)
            da_ref[s:rows_out, :] = zeros
            dg_ref[s:rows_out, :] = zeros
        for k in range(CW):
            dw_ref[k:k + 1, :] = jnp.sum(acc[8 * k:8 * k + 8, :], axis=0, keepdims=True)
        db_ref[...] = jnp.sum(acc[8 * CW:8 * CW + 8, :], axis=0, keepdims=True)

    col = lambda off: pl.BlockSpec((s, LANES), lambda c: (0, c + off))
    ocol = pl.BlockSpec((rows_out, LANES), lambda c: (0, c))
    return _pallas(
        body, name="conf_conv_bwd", grid=(nc,),
        in_specs=[col(0), col(nc), col(0), pl.BlockSpec((CW, LANES), lambda c: (0, c))],
        out_specs=[ocol, ocol, pl.BlockSpec((CW, LANES), lambda c: (0, c)), pl.BlockSpec((1, LANES), lambda c: (0, c))],
        out_shape=[_sds((rows_out, DC), MXU_DTYPE)] * 2 + [_sds((CW, DC), F32), _sds((1, DC), F32)],
        scratch_shapes=[pltpu.VMEM((s + 2 * CONV_HALO, LANES), F32)] * 2 + [pltpu.VMEM((8 * (CW + 1), LANES), F32)],
        semantics=("parallel",),
    )(ag, ag, du1, conv_w)


Q_TILE = 2 * GW
K_WIN = PAIR_ROWS * GW


def _bias_table(rpb_rev):
    def body(p_ref, t_ref):
        kcol = lax.broadcasted_iota(jnp.int32, (GW, LANES), 0)
        lane = lax.broadcasted_iota(jnp.int32, (GW, LANES), 1)
        qcol = lane % GW
        cs = jnp.clip(qcol - NA_ROWS, 0, GW - 2 * NA_ROWS)
        colvalid = (kcol >= cs) & (kcol < cs + 2 * NA_ROWS)
        neg = jnp.full((GW, LANES), NEG, F32)

        def skew(h, ro, shift):
            if ro < 0 or ro >= 2 * NA_ROWS - 1:
                return neg
            row = jnp.broadcast_to(p_ref[h * 16 + ro:h * 16 + ro + 1, :], (GW, LANES))
            return pltpu.roll(row, shift, 1, stride=1, stride_axis=0)

        for h in range(NH):
            for b in range(TAB_BLOCKS):
                val = jnp.where(lane < GW, skew(h, b - 1, GW + 1), skew(h, b - 2, 1))
                t_ref[h, b * GW:(b + 1) * GW, :] = jnp.where(colvalid, val, neg)

    return _pallas(body, name="attn_bias_table", out_shape=_sds((NH, TAB_BLOCKS * GW, LANES), F32))(rpb_rev)


def _rpb_grad(tt):
    def body(t_ref, o_ref):
        lane = lax.broadcasted_iota(jnp.int32, (GW, LANES), 1)
        si = lax.broadcasted_iota(jnp.int32, (GW, GW), 0)
        ti = lax.broadcasted_iota(jnp.int32, (GW, GW), 1)
        flip = jnp.where(si + ti == GW - 1, 1.0, 0.0).astype(F32)
        o_ref[...] = jnp.zeros_like(o_ref)
        for h in range(NH):
            for ro in range(2 * NA_ROWS - 1):
                lo = t_ref[h, (ro + 1) * GW:(ro + 2) * GW, :]
                hi = t_ref[h, (ro + 2) * GW:(ro + 3) * GW, :]
                g = jnp.where(lane < GW, lo + pltpu.roll(hi, GW, 1), 0.0)
                gf = jnp.dot(flip, g, preferred_element_type=F32, precision=lax.Precision.HIGHEST)
                sk = pltpu.roll(gf, 0, 1, stride=1, stride_axis=0)
                o_ref[h * 16 + ro:h * 16 + ro + 1, :] = jnp.sum(sk, axis=0, keepdims=True)

    return _pallas(body, name="attn_rpb_grad", out_shape=_sds((NH * 16, LANES), F32))(tt)


def _attn_geometry(i, rows):
    wsp = jnp.clip(2 * i - NA_ROWS // 2, 0, rows - PAIR_ROWS)
    k0 = pl.multiple_of(wsp * GW, GW)
    t0 = pl.multiple_of((wsp - 2 * i + NA_ROWS) * GW, GW)
    rr = lax.broadcasted_iota(jnp.int32, (GW, Q_TILE), 1) // GW
    wsr = jnp.clip(2 * i + rr - NA_ROWS // 2, 0, rows - NA_ROWS)
    edge_masks = tuple(jnp.where((kr >= wsr) & (kr < wsr + NA_ROWS), 0.0, NEG).astype(F32)
                       for kr in (wsp, wsp + PAIR_ROWS - 1))
    return k0, t0, edge_masks


def _biased(s_raw, bias, edge_masks):
    x = s_raw + bias
    return jnp.concatenate([x[:GW] + edge_masks[0], x[GW:K_WIN - GW], x[K_WIN - GW:] + edge_masks[1]], axis=0)


def _two_heads_on_lanes(xt):
    feat = lax.broadcasted_iota(jnp.int32, xt.shape, 0)
    zero = jnp.zeros_like(xt)
    return jnp.concatenate([jnp.where(feat < HD, xt, zero), jnp.where(feat >= HD, xt, zero)], axis=1)


def _two_heads_on_rows(x):
    lane = lax.broadcasted_iota(jnp.int32, x.shape, 1)
    zero = jnp.zeros_like(x)
    return jnp.concatenate([jnp.where(lane < HD, x, zero), jnp.where(lane >= HD, x, zero)], axis=0)


def _pick_heads(x2):
    n = x2.shape[0] // 2
    lane = lax.broadcasted_iota(jnp.int32, (n, LANES), 1)
    return jnp.where(lane < HD, x2[:n], x2[n:])


_TN = (((0,), (0,)), ((), ()))


def _attn_fwd(qkv, tab, s):
    rows = s // GW
    npair = rows // 2

    def body(q_ref, kv_ref, tab_ref, o_ref, lse_ref):
        i = pl.program_id(0)
        k0, t0, edge_masks = _attn_geometry(i, rows)
        for p in range(NH // 2):
            cq = slice(p * LANES, (p + 1) * LANES)
            ck = slice(DA + p * LANES, DA + (p + 1) * LANES)
            cv = slice(2 * DA + p * LANES, 2 * DA + (p + 1) * LANES)
            qm2 = _two_heads_on_lanes(q_ref[:, cq].T) * SCALE
            s_loc = jnp.dot(kv_ref[pl.ds(k0, K_WIN), ck], qm2, preferred_element_type=F32)
            s_ctx = jnp.dot(kv_ref[pl.ds(s, CTX), ck], qm2, preferred_element_type=F32)
            p_loc, p_ctx = [], []
            for hh in range(2):
                h = 2 * p + hh
                ch = slice(hh * Q_TILE, (hh + 1) * Q_TILE)
                sl = _biased(s_loc[:, ch], tab_ref[h, pl.ds(t0, K_WIN), :], edge_masks)
                sc = s_ctx[:, ch]
                m = jnp.maximum(jnp.max(sl, axis=0, keepdims=True), jnp.max(sc, axis=0, keepdims=True))
                el = jnp.exp(sl - m)
                ec = jnp.exp(sc - m)
                l = jnp.sum(el, axis=0, keepdims=True) + jnp.sum(ec, axis=0, keepdims=True)
                inv = 1.0 / l
                lse_ref[h:h + 1, :] = m + jnp.log(l)
                p_loc.append((el * inv).astype(MXU_DTYPE))
                p_ctx.append((ec * inv).astype(MXU_DTYPE))
            o2 = (lax.dot_general(jnp.concatenate(p_loc, axis=1), kv_ref[pl.ds(k0, K_WIN), cv], _TN, preferred_element_type=F32)
                  + lax.dot_general(jnp.concatenate(p_ctx, axis=1), kv_ref[pl.ds(s, CTX), cv], _TN, preferred_element_type=F32))
            o_ref[:, cq] = _pick_heads(o2).astype(o_ref.dtype)

    return _pallas(
        body, name="attn_fwd", grid=(npair,),
        in_specs=[pl.BlockSpec((Q_TILE, DA), lambda i: (i, 0)), pl.BlockSpec(memory_space=pltpu.VMEM),
                  pl.BlockSpec(memory_space=pltpu.VMEM)],
        out_specs=[pl.BlockSpec((Q_TILE, DA), lambda i: (i, 0)), pl.BlockSpec((NH, Q_TILE), lambda i: (0, i))],
        out_shape=[_sds((s, D), MXU_DTYPE), _sds((NH, s), F32)],
        semantics=("arbitrary",),
    )(qkv, qkv, tab)


def _attn_bwd(qkv, tab, lse, dycat, s):
    rows = s // GW
    npair = rows // 2
    sa = s + CTX
    nzero = CTX // Q_TILE

    def body(q_ref, do_ref, lse_ref, kv_ref, tab_ref, dq_ref, dkv_ref, tt_ref, dk_acc, dv_acc):
        i = pl.program_id(0)

        @pl.when(i == 0)
        def _():
            dk_acc[...] = jnp.zeros_like(dk_acc)
            dv_acc[...] = jnp.zeros_like(dv_acc)
            tt_ref[...] = jnp.zeros_like(tt_ref)

        @pl.when(i >= npair)
        def _():
            dq_ref[...] = jnp.zeros_like(dq_ref)

        @pl.when(i < npair)
        def _():
            k0, t0, edge_masks = _attn_geometry(i, rows)
            for p in range(NH // 2):
                cq = slice(p * LANES, (p + 1) * LANES)
                ck = slice(DA + p * LANES, DA + (p + 1) * LANES)
                cv = slice(2 * DA + p * LANES, 2 * DA + (p + 1) * LANES)
                qp = q_ref[:, cq] * SCALE
                dop = do_ref[:, cq].astype(MXU_DTYPE)
                qm2 = _two_heads_on_lanes(qp.T)
                dom2 = _two_heads_on_lanes(dop.T)
                kw = kv_ref[pl.ds(k0, K_WIN), ck]
                kc = kv_ref[pl.ds(s, CTX), ck]
                vw = kv_ref[pl.ds(k0, K_WIN), cv]
                vc = kv_ref[pl.ds(s, CTX), cv]
                s_loc = jnp.dot(kw, qm2, preferred_element_type=F32)
                s_ctx = jnp.dot(kc, qm2, preferred_element_type=F32)
                dp_loc = jnp.dot(vw, dom2, preferred_element_type=F32)
                dp_ctx = jnp.dot(vc, dom2, preferred_element_type=F32)
                p_loc, p_ctx, ds_loc, ds_ctx = [], [], [], []
                for hh in range(2):
                    h = 2 * p + hh
                    ch = slice(hh * Q_TILE, (hh + 1) * Q_TILE)
                    lse_h = lse_ref[h:h + 1, :]
                    pl_ = jnp.exp(_biased(s_loc[:, ch], tab_ref[h, pl.ds(t0, K_WIN), :], edge_masks) - lse_h)
                    pc_ = jnp.exp(s_ctx[:, ch] - lse_h)
                    dpl = dp_loc[:, ch]
                    dpc = dp_ctx[:, ch]
                    delta = jnp.sum(pl_ * dpl, axis=0, keepdims=True) + jnp.sum(pc_ * dpc, axis=0, keepdims=True)
                    dsl = pl_ * (dpl - delta)
                    dsc = pc_ * (dpc - delta)
                    tt_ref[h, pl.ds(t0, K_WIN), :] += dsl
                    p_loc.append(pl_.astype(MXU_DTYPE))
                    p_ctx.append(pc_.astype(MXU_DTYPE))
                    ds_loc.append(dsl.astype(MXU_DTYPE))
                    ds_ctx.append(dsc.astype(MXU_DTYPE))
                p_loc, p_ctx = jnp.concatenate(p_loc, axis=1), jnp.concatenate(p_ctx, axis=1)
                ds_loc, ds_ctx = jnp.concatenate(ds_loc, axis=1), jnp.concatenate(ds_ctx, axis=1)
                do_rows = _two_heads_on_rows(dop)
                q_rows = _two_heads_on_rows(qp)
                dv_acc[pl.ds(k0, K_WIN), cq] += jnp.dot(p_loc, do_rows, preferred_element_type=F32)
                dv_acc[pl.ds(s, CTX), cq] += jnp.dot(p_ctx, do_rows, preferred_element_type=F32)
                dk_acc[pl.ds(k0, K_WIN), cq] += jnp.dot(ds_loc, q_rows, preferred_element_type=F32)
                dk_acc[pl.ds(s, CTX), cq] += jnp.dot(ds_ctx, q_rows, preferred_element_type=F32)
                dq2 = (lax.dot_general(ds_loc, kw, _TN, preferred_element_type=F32)
                       + lax.dot_general(ds_ctx, kc, _TN, preferred_element_type=F32))
                dq_ref[:, cq] = (_pick_heads(dq2) * SCALE).astype(dq_ref.dtype)

        @pl.when(i == npair - 1)
        def _():
            def cp(c, carry):
                r0 = pl.multiple_of(c * ROW_TILE, ROW_TILE)
                dkv_ref[pl.ds(r0, ROW_TILE), 0:DA] = dk_acc[pl.ds(r0, ROW_TILE), :].astype(dkv_ref.dtype)
                dkv_ref[pl.ds(r0, ROW_TILE), DA:2 * DA] = dv_acc[pl.ds(r0, ROW_TILE), :].astype(dkv_ref.dtype)
                return carry

            lax.fori_loop(0, sa // ROW_TILE, cp, 0)

    qmap = lambda i: (jnp.minimum(i, npair - 1), 0)
    return _pallas(
        body, name="attn_bwd", grid=(npair + nzero,),
        in_specs=[pl.BlockSpec((Q_TILE, DA), qmap), pl.BlockSpec((Q_TILE, DA), qmap),
                  pl.BlockSpec((NH, Q_TILE), lambda i: (0, jnp.minimum(i, npair - 1))),
                  pl.BlockSpec(memory_space=pltpu.VMEM), pl.BlockSpec(memory_space=pltpu.VMEM)],
        out_specs=[pl.BlockSpec((Q_TILE, DA), lambda i: (i, 0)), pl.BlockSpec(memory_space=pltpu.VMEM),
                   pl.BlockSpec(memory_space=pltpu.VMEM)],
        out_shape=[_sds((sa, DA), MXU_DTYPE), _sds((sa, 2 * DA), MXU_DTYPE), _sds((NH, TAB_BLOCKS * GW, LANES), F32)],
        scratch_shapes=[pltpu.VMEM((sa, DA), F32)] * 2,
        semantics=("arbitrary",),
    )(qkv, dycat, lse, qkv, tab)


def _tile(n, prefs):
    for t in prefs:
        if n % t == 0:
            return t
    raise ValueError((n, prefs))


def _local_step(x, ctx, tgt, mod, mod_c, vec, w_in, late_weights, rpb_rev, early_grads=None):
    s = x.shape[0]
    sa = s + CTX
    ts = _tile(s, (1024, 512, 256))
    ts2 = _tile(s, (2048, 1024, 512, 256))
    tsa = _tile(sa, (1088, 640, 256))
    tsa2 = _tile(sa, (2176, 640, 256))
    sh1, sc1, gt1, sh2, sc2, gt2 = (mod[i:i + 1] for i in range(6))
    csh1, csc1 = mod_c[0:1], mod_c[1:2]
    act = MXU_DTYPE

    tab = _bias_table(rpb_rev)
    h_all = _rmsmod_fwd(x, ctx, vec["g_norm1"], sc1, sh1, csc1, csh1)
    w_in = w_in(h_all) if callable(w_in) else w_in
    qkv = _mm(h_all, w_in, mode="nn", m=sa, n=3 * DA, k=D, tm=tsa2, tn=512, tk=D, out_dtype=MXU_DTYPE, name="mm_qkv")
    ag = _mm(h_all, w_in, mode="nn", m=s, n=2 * DC, k=D, tm=ts2, tn=512, tk=D, out_dtype=F32, name="mm_ag", b_off=(0, 3))
    ycat, lse = _attn_fwd(qkv, tab, s)
    u1 = _conf_conv_fwd(ag, vec["conv_w"], vec["conv_b"])
    ycat = _conf_ln_fwd(u1, vec["ln_g"], vec["ln_b"], ycat)
    if callable(late_weights):
        w_out, ffn_weights = late_weights(ycat)
    else:
        w_out, ffn_weights = late_weights[0], late_weights[1:]
    y = _mm(ycat, w_out, mode="nn", m=s, n=D, k=D, tm=ts2, tn=512, tk=D, out_dtype=F32, name="mm_out")
    x1, h2 = _resid_rmsmod_fwd(x, y, gt1, vec["g_norm2"], sc2, sh2)
    w_up, w_down = ffn_weights(h2) if callable(ffn_weights) else ffn_weights
    u = _mm(h2, w_up, mode="nn", m=s, n=2 * DFF, k=D, tm=ts2, tn=512, tk=D, out_dtype=act, name="mm_up")
    f = _ffn_act_fwd(u, vec["ffn_conv_w"], vec["ffn_conv_b"])
    z = _mm(f, w_down, mode="nn", m=s, n=D, k=DFF, tm=ts, tn=D, tk=DFF, out_dtype=F32, name="mm_down")
    dx2, dz, loss, dgt2, dgf = _final_fwd_bwd(x1, z, gt2, vec["g_final"], tgt)

    df = _mm(dz, w_down, mode="nt", m=s, n=DFF, k=D, tm=ts, tn=DFF, tk=D, out_dtype=act, name="mm_down_dx")
    d_w_down = _mm(f, dz, mode="tn", m=DFF, n=D, k=s, tm=DFF // 2, tn=D, tk=ts2, out_dtype=F32, name="mm_down_dw")
    dug, duv, dfw_g, dfw_v, dfb_g, dfb_v = _ffn_act_bwd(u, df, vec["ffn_conv_w"], vec["ffn_conv_b"])
    dw_kw = dict(mode="tn", m=D, n=DFF, k=s, tm=D, tn=DFF, tk=ts, out_dtype=F32, out_total=(D, 2 * DFF))
    d_w_up = _mm(h2, dug, name="mm_up_dw_gate", **dw_kw)
    d_w_up = _mm(h2, duv, name="mm_up_dw_val", o_off=(0, 1), into=d_w_up, **dw_kw)
    if early_grads is not None:
        early_grads[0](d_w_up, d_w_down)
    dh2 = _mm([dug, duv], w_up, mode="nt", m=s, n=D, k=2 * DFF, tm=ts, tn=D, tk=2 * DFF, out_dtype=F32, name="mm_up_dx")
    sc2_b = sc2 if early_grads is None else sc2 + early_grads[1](dh2)
    dsh2, dsc2, dg2, dx1, dy, dgt1 = _rmsmod_bwd(x1, dh2, vec["g_norm2"], sc2_b, name="rmsmod2_bwd", add=dx2, resid=(gt1, y))
    dycat = _mm(dy, w_out, mode="nt", m=s, n=D, k=D, tm=ts2, tn=512, tk=D, out_dtype=F32, name="mm_out_dx")
    d_w_out = _mm(ycat, dy, mode="tn", m=D, n=D, k=s, tm=D, tn=D, tk=ts, out_dtype=F32, name="mm_out_dw")
    du1, dln_g, dln_b = _conf_ln_bwd(dycat, u1, vec["ln_g"], vec["ln_b"])
    da, dg, dconv_w, dconv_b = _conf_conv_bwd(ag, du1, vec["conv_w"], sa)
    dq, dkv, tt = _attn_bwd(qkv, tab, lse, dycat, s)
    drpb_rev = _rpb_grad(tt)
    d_pieces = [dq, dkv, da, dg]
    dh = _mm(d_pieces, w_in, mode="nt", m=sa, n=D, k=NIN, tm=tsa, tn=D, tk=NIN, out_dtype=F32, name="mm_in_dx")
    d_w_in = _mm(h_all, d_pieces, mode="tn", m=D, n=NIN, k=sa, tm=D, tn=NIN, tk=tsa, out_dtype=F32, name="mm_in_dw")
    dsh1, dsc1, dg1, grad_x = _rmsmod_bwd(x, dh, vec["g_norm1"], sc1, name="rmsmod1_bwd", add=dx1)
    dcsh1, dcsc1, dg1c = _rmsmod_bwd(ctx, dh, vec["g_norm1"], csc1, name="rmsmod1_ctx_bwd", dh_row_off=s // ROW_TILE)

    small = dict(
        dmod=[dsh1, dsc1, dgt1, dsh2, dsc2, dgt2], dmod_c=[dcsh1, dcsc1],
        g_norm1=[dg1, dg1c], g_norm2=dg2, g_final=dgf, conv_b=dconv_b, ln_g=dln_g, ln_b=dln_b, conv_w=dconv_w,
        ffn_conv_w=[dfw_g, dfw_v], ffn_conv_b=[dfb_g, dfb_v], rpb_rev=drpb_rev,
    )
    return loss, grad_x, d_w_in, d_w_out, d_w_up, d_w_down, small


N_CHIPS = 4
HBM = pl.BlockSpec(memory_space=pl.ANY)
BIG = {"w_in": ("col", (D, NIN)), "w_out": ("row", (D, D)), "w_up": ("col", (D, 2 * DFF)), "w_down": ("row", (DFF, D))}
BIG_NAMES = tuple(BIG)
LATE_NAMES = ("w_out", "w_up", "w_down")


def _shard_shape(name):
    kind, (r, c) = BIG[name]
    return (r, c // N_CHIPS) if kind == "col" else (r // N_CHIPS, c)


def _half_rows(name):
    return _shard_shape(name)[0] // 2


def _place():
    x, y, c = lax.axis_index("x"), lax.axis_index("y"), lax.axis_index("c")
    others = [(1 - x, y), (x, 1 - y), (1 - x, 1 - y)]
    return x, y, c, 2 * x + y, (x, y, 1 - c), others


def _whole_region(ref, name, chip, half):
    kind, _ = BIG[name]
    r, c = _shard_shape(name)
    if kind == "col":
        return ref.at[pl.ds(half * (r // 2), r // 2), pl.ds(chip * c, c)]
    return ref.at[pl.ds(chip * r + half * (r // 2), r // 2), :]


def _remote(src, dst, send_sem, recv_sem, to):
    return pltpu.make_async_remote_copy(src_ref=src, dst_ref=dst, send_sem=send_sem, recv_sem=recv_sem,
                                        device_id=to, device_id_type=MESH)


def _gather_small(v, name):
    m_per, n = v.shape

    def body(x_ref, out_ref, send_sems, recv_sems, local_sem):
        x, y, c, _, sibling, others = _place()
        me = (x, y, c)

        def rows(px, py, pc):
            return out_ref.at[pl.ds((4 * px + 2 * py + pc) * m_per, m_per), :]

        def copy(k, block, to, src=None):
            return _remote(rows(*block) if src is None else src, rows(*block), send_sems.at[k], recv_sems.at[k], to)

        mine = pltpu.make_async_copy(x_ref, rows(*me), local_sem)
        mine.start()
        first = [copy(0, me, sibling, src=x_ref)]
        first += [copy(1 + j, me, (*chip, c), src=x_ref) for j, chip in enumerate(others)]
        for cp in first:
            cp.start()
        passed = [copy(4 + j, (*chip, c), sibling) for j, chip in enumerate(others)]
        for j, chip in enumerate(others):
            copy(1 + j, (*chip, c), me).wait_recv()
            passed[j].start()
        copy(0, sibling, me).wait_recv()
        for j, chip in enumerate(others):
            copy(4 + j, (*chip, 1 - c), me).wait_recv()
        for cp in first + passed:
            cp.wait_send()
        mine.wait()

    return pl.pallas_call(
        body, name=name, out_shape=_sds((8 * m_per, n), v.dtype),
        in_specs=[pl.BlockSpec(memory_space=pltpu.VMEM)], out_specs=pl.BlockSpec(memory_space=pltpu.VMEM),
        scratch_shapes=[pltpu.SemaphoreType.DMA((7,)), pltpu.SemaphoreType.DMA((7,)), pltpu.SemaphoreType.DMA],
    )(v)


def _cast_into_whole(name, shard, chip):
    kind, whole = BIG[name]
    r, c = shard.shape
    if kind == "col":
        tr = 256
        o_spec = pl.BlockSpec((tr, c), lambda i, ch: (i, ch[0]))
    else:
        tr = _tile(r, (128, 352))
        o_spec = pl.BlockSpec((tr, c), lambda i, ch: (ch[0] * (r // tr) + i, 0))

    def body(ch_ref, x_ref, o_ref):
        del ch_ref
        o_ref[...] = x_ref[...].astype(o_ref.dtype)

    return _pallas(body, name="cast_" + name, prefetch=1, grid=(r // tr,),
                   in_specs=[pl.BlockSpec((tr, c), lambda i, ch: (i, 0))], out_specs=o_spec,
                   out_shape=_sds(whole, MXU_DTYPE), semantics=("parallel",))(chip, shard)


SEM = pl.BlockSpec(memory_space=pltpu.SEMAPHORE)
IN_HBM = pl.BlockSpec(memory_space=pltpu.HBM)
DATAFLOW = pltpu.SideEffectType.DATAFLOW_SIDE_EFFECTING


def _keep_in_hbm(a):
    return pltpu.with_memory_space_constraint(a, pltpu.HBM)


def _several(after):
    return list(after) if isinstance(after, (list, tuple)) else [after]


def _gather_start(wholes, names, after, tag):
    nw = len(names)
    ns = 2 * 3 * nw

    def body(*refs):
        ins = refs[:nw]
        sems = refs[nw + 1:nw + 1 + ns]
        token = refs[2 * nw + ns + 1]
        _, _, c, chip, _, others = _place()
        for w, name in enumerate(names):
            mine = _whole_region(ins[w], name, chip, c)
            for t, (ox, oy) in enumerate(others):
                k = 2 * (3 * w + t)
                _remote(mine, mine, sems[k], sems[k + 1], (ox, oy, c)).start()
        token[...] = jnp.zeros_like(token)

    res = pl.pallas_call(
        body, name="gather_" + tag + "_start",
        out_shape=(*[pltpu.SemaphoreType.DMA(())] * ns, *[pltpu.HBM(a.shape, a.dtype) for a in wholes], _sds((8, LANES), F32)),
        in_specs=[IN_HBM] * nw + [pl.BlockSpec(memory_space=pl.ANY)],
        out_specs=(*[SEM] * ns, *[IN_HBM] * nw, pl.BlockSpec(memory_space=pltpu.VMEM)),
        input_output_aliases={i: ns + i for i in range(nw)},
        compiler_params=pltpu.CompilerParams(has_side_effects=DATAFLOW),
    )(*[_keep_in_hbm(a) for a in wholes], after)
    return list(res[:ns]), list(res[ns:ns + nw]), res[ns + nw]


def _gather_wait(sems, wholes, names, after, tag):
    nw = len(names)
    ns = len(sems)

    def body(*refs):
        ins = refs[:nw]
        sem_refs = refs[nw:nw + ns]
        _, _, c, chip, _, others = _place()
        for w, name in enumerate(names):
            mine = _whole_region(ins[w], name, chip, c)
            for t, (ox, oy) in enumerate(others):
                got = _whole_region(ins[w], name, 2 * ox + oy, c)
                k = 2 * (3 * w + t)
                cp = _remote(mine, got, sem_refs[k], sem_refs[k + 1], (ox, oy, c))
                cp.wait_send()
                cp.wait_recv()

    return pl.pallas_call(
        body, name="gather_" + tag + "_wait",
        out_shape=tuple(pltpu.HBM(a.shape, a.dtype) for a in wholes),
        in_specs=[IN_HBM] * nw + [SEM] * ns + [pl.BlockSpec(memory_space=pl.ANY)], out_specs=tuple([IN_HBM] * nw),
        input_output_aliases={i: i for i in range(nw)},
        compiler_params=pltpu.CompilerParams(has_side_effects=DATAFLOW),
    )(*wholes, *sems, after)


def _forward_halves(wholes, names, tag):
    nw = len(names)

    def body(*refs):
        outs = refs[nw:2 * nw]
        send_sems, recv_sems = refs[2 * nw:]
        _, _, c, _, sibling, others = _place()
        sends = []
        for w, name in enumerate(names):
            for t, (ox, oy) in enumerate(others):
                got = _whole_region(outs[w], name, 2 * ox + oy, c)
                cp = _remote(got, got, send_sems.at[w, t], recv_sems.at[w, t], sibling)
                cp.start()
                sends.append(cp)
        for w, name in enumerate(names):
            for t, (ox, oy) in enumerate(others):
                got = _whole_region(outs[w], name, 2 * ox + oy, 1 - c)
                _remote(got, got, send_sems.at[w, t], recv_sems.at[w, t], sibling).wait_recv()
        for cp in sends:
            cp.wait_send()

    return pl.pallas_call(
        body, name="gather_" + tag + "_forward",
        out_shape=[_sds(a.shape, a.dtype) for a in wholes],
        in_specs=[HBM] * nw, out_specs=[HBM] * nw,
        input_output_aliases={i: i for i in range(nw)},
        scratch_shapes=[pltpu.SemaphoreType.DMA((nw, 3)), pltpu.SemaphoreType.DMA((nw, 3))],
    )(*wholes)


def _forward_start(wholes, names, tag, after):
    nw = len(names)
    ns = 2 * 3 * nw

    def body(*refs):
        ins = refs[:nw]
        sems = refs[nw + 1:nw + 1 + ns]
        token = refs[2 * nw + ns + 1]
        _, _, c, _, sibling, others = _place()
        for w, name in enumerate(names):
            for t, (ox, oy) in enumerate(others):
                got = _whole_region(ins[w], name, 2 * ox + oy, c)
                k = 2 * (3 * w + t)
                _remote(got, got, sems[k], sems[k + 1], sibling).start()
        token[...] = jnp.zeros_like(token)

    res = pl.pallas_call(
        body, name="gather_" + tag + "_forward_start",
        out_shape=(*[pltpu.SemaphoreType.DMA(())] * ns, *[pltpu.HBM(a.shape, a.dtype) for a in wholes], _sds((8, LANES), F32)),
        in_specs=[IN_HBM] * nw + [pl.BlockSpec(memory_space=pl.ANY)],
        out_specs=(*[SEM] * ns, *[IN_HBM] * nw, pl.BlockSpec(memory_space=pltpu.VMEM)),
        input_output_aliases={i: ns + i for i in range(nw)},
        compiler_params=pltpu.CompilerParams(has_side_effects=DATAFLOW),
    )(*[_keep_in_hbm(a) for a in wholes], after)
    return list(res[:ns]), list(res[ns:ns + nw]), res[ns + nw]


def _forward_wait(sems, wholes, names, after, tag):
    nw = len(names)
    ns = len(sems)

    def body(*refs):
        ins = refs[:nw]
        sem_refs = refs[nw:nw + ns]
        _, _, c, _, sibling, others = _place()
        for w, name in enumerate(names):
            for t, (ox, oy) in enumerate(others):
                k = 2 * (3 * w + t)
                cp = _remote(_whole_region(ins[w], name, 2 * ox + oy, c), _whole_region(ins[w], name, 2 * ox + oy, 1 - c),
                             sem_refs[k], sem_refs[k + 1], sibling)
                cp.wait_send()
                cp.wait_recv()

    return pl.pallas_call(
        body, name="gather_" + tag + "_forward_wait",
        out_shape=tuple(pltpu.HBM(a.shape, a.dtype) for a in wholes),
        in_specs=[IN_HBM] * nw + [SEM] * ns + [pl.BlockSpec(memory_space=pl.ANY)], out_specs=tuple([IN_HBM] * nw),
        input_output_aliases={i: i for i in range(nw)},
        compiler_params=pltpu.CompilerParams(has_side_effects=DATAFLOW),
    )(*wholes, *sems, after)


def _compact_shape(name, dtype):
    kind, (r, c) = BIG[name]
    return _sds((r // 2, c), dtype)


def _swap_pairs(ins, outs, names, c):
    pairs = []
    for w, name in enumerate(names):
        kind, _ = BIG[name]
        half = _half_rows(name)
        if kind == "col":
            pairs.append((ins[w].at[pl.ds((1 - c) * half, half), :], outs[w]))
        else:
            pairs += [(ins[w].at[pl.ds(jj * 2 * half + (1 - c) * half, half), :], outs[w].at[pl.ds(jj * half, half), :])
                      for jj in range(N_CHIPS)]
    return pairs


def _n_swap_copies(names):
    return sum(1 if BIG[n][0] == "col" else N_CHIPS for n in names)


def _swap_start(grads, names, label):
    nw = len(names)
    ns = 2 * _n_swap_copies(names)

    def body(*refs):
        ins, lands = refs[:nw], refs[nw:2 * nw]
        sems = refs[2 * nw:2 * nw + ns]
        token = refs[4 * nw + ns]
        _, _, c, _, sibling, _ = _place()
        for k, (src, dst) in enumerate(_swap_pairs(ins, lands, names, c)):
            _remote(src, dst, sems[2 * k], sems[2 * k + 1], sibling).start()
        token[...] = jnp.zeros_like(token)

    lands = [_keep_in_hbm(lax.empty(_compact_shape(n, F32).shape, F32)) for n in names]
    res = pl.pallas_call(
        body, name=label,
        out_shape=(*[pltpu.SemaphoreType.DMA(())] * ns, *[pltpu.HBM(a.shape, a.dtype) for a in grads],
                   *[pltpu.HBM(a.shape, a.dtype) for a in lands], _sds((8, LANES), F32)),
        in_specs=[IN_HBM] * (2 * nw),
        out_specs=(*[SEM] * ns, *[IN_HBM] * (2 * nw), pl.BlockSpec(memory_space=pltpu.VMEM)),
        input_output_aliases={i: ns + i for i in range(2 * nw)},
        compiler_params=pltpu.CompilerParams(has_side_effects=DATAFLOW),
    )(*[_keep_in_hbm(a) for a in grads], *lands)
    return list(res[:ns]), list(res[ns:ns + nw]), list(res[ns + nw:ns + 2 * nw]), res[ns + 2 * nw]


def _swap_wait(sems, grads, lands, names, after, label):
    nw = len(names)
    ns = len(sems)

    def body(*refs):
        ins, land_refs = refs[:nw], refs[nw:2 * nw]
        sem_refs = refs[2 * nw:2 * nw + ns]
        _, _, c, _, sibling, _ = _place()
        for k, (src, dst) in enumerate(_swap_pairs(ins, land_refs, names, c)):
            cp = _remote(src, dst, sem_refs[2 * k], sem_refs[2 * k + 1], sibling)
            cp.wait_send()
            cp.wait_recv()

    res = pl.pallas_call(
        body, name=label,
        out_shape=tuple(pltpu.HBM(a.shape, a.dtype) for a in (*grads, *lands)),
        in_specs=[IN_HBM] * (2 * nw) + [SEM] * ns + [pl.BlockSpec(memory_space=pl.ANY)] * len(_several(after)),
        out_specs=tuple([IN_HBM] * (2 * nw)),
        input_output_aliases={i: i for i in range(2 * nw)},
        compiler_params=pltpu.CompilerParams(has_side_effects=DATAFLOW),
    )(*grads, *lands, *sems, *_several(after))
    return list(res[:nw]), list(res[nw:])


def _add_halves(name, grad, got, core):
    kind, (r, c) = BIG[name]
    half = _half_rows(name)
    if kind == "col":
        t = 128
        grid = (half // t,)
        g_spec = pl.BlockSpec((t, c), lambda i, cr: (cr[0] * (half // t) + i, 0))
        o_spec = pl.BlockSpec((t, c), lambda i, cr: (i, 0))
    else:
        t = half
        grid = (N_CHIPS,)
        g_spec = pl.BlockSpec((t, c), lambda i, cr: (2 * i + cr[0], 0))
        o_spec = pl.BlockSpec((t, c), lambda i, cr: (i, 0))

    def body(c_ref, g_ref, b_ref, o_ref):
        del c_ref
        o_ref[...] = (g_ref[...] + b_ref[...]).astype(o_ref.dtype)

    return pl.pallas_call(
        body, name="grad_add_" + name,
        grid_spec=pltpu.PrefetchScalarGridSpec(num_scalar_prefetch=1, grid=grid, in_specs=[g_spec, o_spec], out_specs=o_spec),
        out_shape=_compact_shape(name, BF16),
        compiler_params=pltpu.CompilerParams(dimension_semantics=("parallel",), vmem_limit_bytes=VMEM_LIMIT),
    )(core, grad, got)


def _piece(ref, name, chip):
    kind, _ = BIG[name]
    r, c = _shard_shape(name)
    if kind == "col":
        return ref.at[:, pl.ds(chip * c, c)]
    return ref.at[pl.ds(chip * (r // 2), r // 2), :]


def _landing_shape(name):
    r, c = _shard_shape(name)
    return (N_CHIPS - 1, r // 2, c)


def _exchange_start(parts, names, label):
    nw = len(names)
    ns = 2 * 3 * nw

    def body(*refs):
        ins, lands = refs[:nw], refs[nw:2 * nw]
        sems = refs[2 * nw:2 * nw + ns]
        token = refs[4 * nw + ns]
        _, _, c, _, _, others = _place()
        for w, name in enumerate(names):
            for t, (ox, oy) in enumerate(others):
                k = 2 * (3 * w + t)
                _remote(_piece(ins[w], name, 2 * ox + oy), lands[w].at[t], sems[k], sems[k + 1], (ox, oy, c)).start()
        token[...] = jnp.zeros_like(token)

    lands = [_keep_in_hbm(lax.empty(_landing_shape(n), BF16)) for n in names]
    res = pl.pallas_call(
        body, name=label,
        out_shape=(*[pltpu.SemaphoreType.DMA(())] * ns, *[pltpu.HBM(a.shape, a.dtype) for a in parts],
                   *[pltpu.HBM(a.shape, a.dtype) for a in lands], _sds((8, LANES), F32)),
        in_specs=[IN_HBM] * (2 * nw),
        out_specs=(*[SEM] * ns, *[IN_HBM] * (2 * nw), pl.BlockSpec(memory_space=pltpu.VMEM)),
        input_output_aliases={i: ns + i for i in range(2 * nw)},
        compiler_params=pltpu.CompilerParams(has_side_effects=DATAFLOW),
    )(*[_keep_in_hbm(a) for a in parts], *lands)
    return list(res[:ns]), list(res[ns:ns + nw]), list(res[ns + nw:ns + 2 * nw]), res[ns + 2 * nw]


def _exchange_wait(sems, parts, lands, names, after, label):
    nw = len(names)
    ns = len(sems)

    def body(*refs):
        ins, land_refs = refs[:nw], refs[nw:2 * nw]
        sem_refs = refs[2 * nw:2 * nw + ns]
        _, _, c, _, _, others = _place()
        for w, name in enumerate(names):
            for t, (ox, oy) in enumerate(others):
                k = 2 * (3 * w + t)
                cp = _remote(_piece(ins[w], name, 2 * ox + oy), land_refs[w].at[t], sem_refs[k], sem_refs[k + 1], (ox, oy, c))
                cp.wait_send()
                cp.wait_recv()

    res = pl.pallas_call(
        body, name=label,
        out_shape=tuple(pltpu.HBM(a.shape, a.dtype) for a in (*parts, *lands)),
        in_specs=[IN_HBM] * (2 * nw) + [SEM] * ns + [pl.BlockSpec(memory_space=pl.ANY)] * len(_several(after)),
        out_specs=tuple([IN_HBM] * (2 * nw)),
        input_output_aliases={i: i for i in range(2 * nw)},
        compiler_params=pltpu.CompilerParams(has_side_effects=DATAFLOW),
    )(*parts, *lands, *sems, *_several(after))
    return list(res[:nw]), list(res[nw:])


def _sum_chips(name, part, got, chip):
    kind, _ = BIG[name]
    _, r, c = got.shape
    t = _tile(r, (128, 352))
    if kind == "col":
        own = pl.BlockSpec((t, c), lambda i, ch: (i, ch[0]))
    else:
        own = pl.BlockSpec((t, c), lambda i, ch: (ch[0] * (r // t) + i, 0))

    def body(ch_ref, p_ref, g_ref, o_ref):
        del ch_ref
        acc = p_ref[...].astype(F32)
        for j in range(N_CHIPS - 1):
            acc = acc + g_ref[j].astype(F32)
        o_ref[...] = acc

    return _pallas(
        body, name="grad_sum_" + name, prefetch=1, grid=(r // t,),
        in_specs=[own, pl.BlockSpec((N_CHIPS - 1, t, c), lambda i, ch: (0, i, 0))],
        out_specs=pl.BlockSpec((t, c), lambda i, ch: (i, 0)),
        out_shape=_sds((r, c), F32), semantics=("parallel",),
    )(chip, part, got)


def _send_halves(sums, label, after):
    nw = len(sums)

    def body(*refs):
        ins, outs = refs[:nw], refs[nw + 1:2 * nw + 1]
        send_sems, recv_sems = refs[2 * nw + 1:]
        _, _, _, _, sibling, _ = _place()
        copies = [_remote(ins[w], outs[w], send_sems.at[w], recv_sems.at[w], sibling) for w in range(nw)]
        for cp in copies:
            cp.start()
        for cp in copies:
            cp.wait()

    return pl.pallas_call(
        body, name=label,
        out_shape=[_sds(a.shape, a.dtype) for a in sums],
        in_specs=[HBM] * (nw + 1), out_specs=[HBM] * nw,
        scratch_shapes=[pltpu.SemaphoreType.DMA((nw,)), pltpu.SemaphoreType.DMA((nw,))],
    )(*sums, after)


EARLY_GRADS = ("w_up", "w_down")
LAST_GRADS = ("w_in", "w_out")


def _reduce_finish(started, names, after, chip, tag):
    sems, parts, lands, _ = started
    parts, lands = _exchange_wait(sems, parts, lands, names, after, "grad_exchange_wait_" + tag)
    return [_sum_chips(n, parts[i], lands[i], chip) for i, n in enumerate(names)]


HI = lax.Precision.HIGHEST
MOD_COLS = 6 * D // N_CHIPS
COND_ROWS = 16


def _silu(v):
    return v * _sigmoid(v)


GATHER_ROWS = 8
FFW_COLS = 2 * DFF // N_CHIPS
CONV_COLS = DC // N_CHIPS
TAPS_PER_ROW = FFW_COLS // CONV_COLS
assert 4 + -(-CW // TAPS_PER_ROW) <= GATHER_ROWS


def _conv_tap_place(k):
    return 4 + k // TAPS_PER_ROW, (k % TAPS_PER_ROW) * CONV_COLS


def _pack_cond(c, ffn_w, conv_w):
    def body(c_ref, f_ref, w_ref, o_ref):
        o_ref[...] = jnp.zeros_like(o_ref)
        o_ref[0:1, 0:D] = c_ref[...]
        o_ref[1:4, :] = f_ref[...]
        for k in range(CW):
            row, lane = _conv_tap_place(k)
            o_ref[row:row + 1, lane:lane + CONV_COLS] = w_ref[k:k + 1, :]

    return _pallas(body, name="pack_cond", out_shape=_sds((GATHER_ROWS, FFW_COLS), F32))(c, ffn_w, conv_w)


def _unpack_cond(got, c_ctx):
    def body(g_ref, c_ref, cond_ref, f_ref, w_ref):
        cond_ref[...] = jnp.zeros_like(cond_ref)
        for d in range(8):
            cond_ref[d:d + 1, :] = g_ref[d * GATHER_ROWS:d * GATHER_ROWS + 1, 0:D]
        cond_ref[8:9, :] = c_ref[...]
        for j in range(N_CHIPS):
            r0 = 2 * j * GATHER_ROWS
            f_ref[:, j * FFW_COLS:(j + 1) * FFW_COLS] = g_ref[r0 + 1:r0 + 4, :]
            for k in range(CW):
                row, lane = _conv_tap_place(k)
                w_ref[k:k + 1, j * CONV_COLS:(j + 1) * CONV_COLS] = g_ref[r0 + row:r0 + row + 1, lane:lane + CONV_COLS]

    return _pallas(body, name="unpack_cond",
                   out_shape=[_sds((COND_ROWS, D), F32), _sds((3, 2 * DFF), F32), _sds((CW, DC), F32)])(got, c_ctx)


def _chip_cols(rows, width):
    return pl.BlockSpec((rows, width), lambda i, ch: (0, ch[0]))


def _whole(shape):
    return pl.BlockSpec(shape, lambda i, ch: (0,) * len(shape))


def _mod_shard(cond, w_mod, b_mod, chip):
    def body(ch_ref, c_ref, w_ref, b_ref, o_ref):
        del ch_ref
        o_ref[...] = jnp.dot(_silu(c_ref[...]), w_ref[...], preferred_element_type=F32, precision=HI) + b_ref[...]

    return _pallas(body, name="mod_fwd", prefetch=1, grid=(1,),
                   in_specs=[_whole((COND_ROWS, D)), _whole((D, MOD_COLS)), _chip_cols(1, MOD_COLS)],
                   out_specs=_whole((COND_ROWS, MOD_COLS)),
                   out_shape=_sds((COND_ROWS, MOD_COLS), F32))(chip, cond, w_mod, b_mod)


def _unpack_mod(mods, dev):
    def body(dev_ref, m_ref, me_ref, c_ref):
        rowi = lax.broadcasted_iota(jnp.int32, (COND_ROWS, MOD_COLS), 0)
        mine, ctx = [], []
        for j in range(N_CHIPS):
            blk = m_ref[2 * j * COND_ROWS:(2 * j + 1) * COND_ROWS, :]
            mine.append(jnp.sum(jnp.where(rowi == dev_ref[0], blk, 0.0), axis=0, keepdims=True))
            ctx.append(blk[8:9, :])
        mine = jnp.concatenate(mine, axis=1)
        ctx = jnp.concatenate(ctx, axis=1)
        for k in range(6):
            me_ref[k:k + 1, :] = mine[:, k * D:(k + 1) * D]
        for k in range(2):
            c_ref[k:k + 1, :] = ctx[:, k * D:(k + 1) * D]

    return _pallas(body, name="unpack_mod", prefetch=1, grid=(1,),
                   in_specs=[_whole(mods.shape)], out_specs=[_whole((6, D)), _whole((2, D))],
                   out_shape=[_sds((6, D), F32), _sds((2, D), F32)])(dev, mods)


MOD_TILE = 512


def _mod_weight_update(cond, dmod_all, w, m, v, chip):
    nt = MOD_COLS // MOD_TILE

    def body(ch_ref, c_ref, d_ref, w_ref, m_ref, v_ref, g_ref, dl_ref, nm_ref, nv_ref):
        del ch_ref
        g = lax.dot_general(_silu(c_ref[...]), d_ref[...], _TN, preferred_element_type=F32, precision=HI)
        g_ref[...] = g
        dl_ref[...], nm_ref[...], nv_ref[...] = _adam_math(w_ref[...], g, m_ref[...], v_ref[...])

    blk = pl.BlockSpec((D, MOD_TILE), lambda j, ch: (0, j))
    return _pallas(body, name="mod_weight_update", prefetch=1, grid=(nt,),
                   in_specs=[_whole((COND_ROWS, D)), pl.BlockSpec((COND_ROWS, MOD_TILE), lambda j, ch: (0, ch[0] * nt + j)),
                             blk, blk, blk],
                   out_specs=[blk] * 4, out_shape=[_sds((D, MOD_COLS), F32)] * 4,
                   semantics=("parallel",))(chip, cond, dmod_all, w, m, v)


def _cond_grad_partial(dmod_all, w_mod, chip):
    def body(ch_ref, d_ref, w_ref, o_ref):
        del ch_ref
        o_ref[...] = lax.dot_general(d_ref[...], w_ref[...], (((1,), (1,)), ((), ())), preferred_element_type=F32, precision=HI)

    return _pallas(body, name="cond_grad_partial", prefetch=1, grid=(1,),
                   in_specs=[pl.BlockSpec((8, MOD_COLS), lambda i, ch: (1, ch[0])), _whole((D, MOD_COLS))],
                   out_specs=_whole((8, D)), out_shape=_sds((8, D), F32))(chip, dmod_all, w_mod)


def _adam_math(w, g, m, v):
    nm = ADAM_B1 * m + (1.0 - ADAM_B1) * g
    nv = ADAM_B2 * v + (1.0 - ADAM_B2) * (g * g)
    c1 = 1.0 - ADAM_B1 ** ADAM_STEP
    c2 = 1.0 - ADAM_B2 ** ADAM_STEP
    return -ADAM_LR * ((nm / c1) / (jnp.sqrt(nv / c2) + ADAM_EPS) + ADAM_WD * w), nm, nv


def _cond_update(parts, c_ctx, m, v):
    def body(p_ref, c_ref, m_ref, v_ref, g_ref, d_ref, nm_ref, nv_ref):
        tot = p_ref[0:1, :]
        for j in range(1, N_CHIPS):
            tot = tot + p_ref[16 * j:16 * j + 1, :]
        cv = c_ref[...]
        sg = _sigmoid(cv)
        g = tot * (sg * (1.0 + cv * (1.0 - sg)))
        g_ref[...] = g
        d_ref[...], nm_ref[...], nv_ref[...] = _adam_math(cv, g, m_ref[...], v_ref[...])

    return _pallas(body, name="cond_update", out_shape=[_sds((1, D), F32)] * 4)(parts, c_ctx, m, v)


def _adamw_cols(w, g_all, m, v, chip, name):
    r, c = w.shape

    def body(ch_ref, w_ref, g_ref, m_ref, v_ref, go_ref, d_ref, nm_ref, nv_ref):
        del ch_ref
        g = g_ref[...]
        go_ref[...] = g
        d_ref[...], nm_ref[...], nv_ref[...] = _adam_math(w_ref[...], g, m_ref[...], v_ref[...])

    return _pallas(body, name=name, prefetch=1, grid=(1,),
                   in_specs=[_whole((r, c)), _chip_cols(r, c), _whole((r, c)), _whole((r, c))],
                   out_specs=[_whole((r, c))] * 4, out_shape=[_sds((r, c), F32)] * 4)(chip, w, g_all, m, v)


def _adamw_halves(name, w, own, other, m, v, core, after):
    r, c = w.shape
    half = r // 2
    t = _tile(half, (128, 352))
    nh = half // t

    def pick(mine):
        def index(i, cr):
            first = cr[0] if mine else 1 - cr[0]
            return (jnp.clip(i - first * nh, 0, nh - 1), 0)
        return pl.BlockSpec((t, c), index)

    def body(c_ref, w_ref, own_ref, oth_ref, m_ref, v_ref, after_ref, g_ref, d_ref, nm_ref, nv_ref):
        del after_ref
        g = jnp.where(pl.program_id(0) // nh == c_ref[0], own_ref[...], oth_ref[...])
        g_ref[...] = g
        d_ref[...], nm_ref[...], nv_ref[...] = _adam_math(w_ref[...], g, m_ref[...], v_ref[...])

    blk = pl.BlockSpec((t, c), lambda i, cr: (i, 0))
    return _pallas(body, name="adamw_" + name, prefetch=1, grid=(2 * nh,),
                   in_specs=[blk, pick(True), pick(False), blk, blk, pl.BlockSpec(memory_space=pl.ANY)], out_specs=[blk] * 4,
                   out_shape=[_sds((r, c), F32)] * 4, semantics=("parallel",))(core, w, own, other, m, v, after)


WEIGHTS = ("c_ctx", "w_mod", "b_mod", "g_norm1", "w_in", "rpb", "conv_w", "conv_b", "ln_g", "ln_b", "w_out", "g_norm2",
           "w_up", "ffn_conv_w", "ffn_conv_b", "w_down", "g_final")
PACK = (("dmod", 6 * D), ("dmod_c", 2 * D), ("g_norm1", D), ("g_norm1_ctx", D), ("g_norm2", D), ("g_final", D),
        ("conv_b", DC), ("ln_g", DC), ("ln_b", DC), ("ffn_conv_b", 2 * DFF), ("ffn_conv_w", 3 * 2 * DFF),
        ("conv_w", CW * DC), ("rpb_rev", NH * 16 * LANES), ("loss", LANES))
PACK_OFF = {}
_o = 0
for _n, _w in PACK:
    PACK_OFF[_n] = (_o, _w)
    _o += _w
PACK_N = -(-_o // (8 * LANES)) * (8 * LANES)
VECTORS = {"b_mod": (6 * D, ("dmod", "dmod_c")), "g_norm1": (D, ("g_norm1", "g_norm1_ctx")), "conv_b": (DC, ("conv_b",)),
           "ln_g": (DC, ("ln_g",)), "ln_b": (DC, ("ln_b",)), "g_norm2": (D, ("g_norm2",)),
           "ffn_conv_b": (2 * DFF, ("ffn_conv_b",)), "g_final": (D, ("g_final",))}
RPB_COLS = 4 * NA_ROWS - 1


def _pack_small(parts, after):
    arrs, places = [], []
    for name, _ in PACK:
        off, width = PACK_OFF[name]
        group = parts[name]
        rows = group[0].shape[0]
        row_w = sum(a.shape[1] for a in group)
        assert rows * row_w == width, (name, rows, row_w, width)
        col = 0
        for a in group:
            arrs.append(a)
            places.append([off + k * row_w + col for k in range(rows)])
            col += a.shape[1]

    def body(*refs):
        o_ref = refs[-1]
        o_ref[:, _o:PACK_N] = jnp.zeros((1, PACK_N - _o), F32)
        for ref, offs in zip(refs, places):
            n = ref.shape[1]
            for k, off in enumerate(offs):
                o_ref[:, off:off + n] = ref[k:k + 1, :]

    vmem = pl.BlockSpec(memory_space=pltpu.VMEM)
    return _pallas(body, name="pack_small_grads", out_shape=_sds((1, PACK_N), F32),
                   in_specs=[vmem] * len(arrs) + [pl.BlockSpec(memory_space=pl.ANY)] * len(_several(after)),
                   out_specs=vmem)(*arrs, *_several(after))


def _small_update(packs, w, m, v):
    names = list(VECTORS)

    def body(*refs):
        it = iter(refs)
        p_ref = next(it)
        wmv = {n: (next(it), next(it), next(it)) for n in names}
        outs = {n: (next(it), next(it), next(it), next(it)) for n in names}
        dmod_ref, cw_ref, fw_ref, rpb_ref, loss_ref = next(it), next(it), next(it), next(it), next(it)

        def total(name):
            off, width = PACK_OFF[name]
            acc = p_ref[0:1, off:off + width]
            for d in range(1, 8):
                acc = acc + p_ref[d:d + 1, off:off + width]
            return acc

        for n in names:
            width, segs = VECTORS[n]
            g = total(segs[0])
            if len(segs) > 1:
                extra = total(segs[1])
                ew = extra.shape[1]
                g = g + extra if ew == width else jnp.concatenate([g[:, :ew] + extra, g[:, ew:]], axis=1)
            w_ref, m_ref, v_ref = wmv[n]
            g_ref, d_ref, nm_ref, nv_ref = outs[n]
            g_ref[...] = g
            d_ref[...], nm_ref[...], nv_ref[...] = _adam_math(w_ref[...], g, m_ref[...], v_ref[...])

        o_dmod = PACK_OFF["dmod"][0]
        dmod_ref[...] = jnp.zeros_like(dmod_ref)
        dmod_ref[0:8, :] = p_ref[:, o_dmod:o_dmod + 6 * D]
        dmod_ref[8:9, 0:2 * D] = total("dmod_c")
        for ref, name, rows in ((cw_ref, "conv_w", CW), (fw_ref, "ffn_conv_w", 3), (rpb_ref, "rpb_rev", NH * 16)):
            flat = total(name)
            n = ref.shape[1]
            for k in range(rows):
                ref[k:k + 1, :] = flat[:, k * n:(k + 1) * n]
        loss_ref[...] = total("loss")

    ins = [packs] + [a[n] for n in names for a in (w, m, v)]
    out_shape = [_sds((1, VECTORS[n][0]), F32) for n in names for _ in range(4)]
    out_shape += [_sds((COND_ROWS, 6 * D), F32), _sds((CW, DC), F32), _sds((3, 2 * DFF), F32), _sds((NH * 16, LANES), F32),
                  _sds((1, LANES), F32)]
    res = _pallas(body, name="small_update", out_shape=out_shape)(*ins)
    per = {n: tuple(res[4 * i:4 * i + 4]) for i, n in enumerate(names)}
    return (per, *res[4 * len(names):])


def _rpb_update(rev, w, m, v):
    def body(r_ref, w_ref, m_ref, v_ref, g_ref, d_ref, nm_ref, nv_ref):
        li = lax.broadcasted_iota(jnp.int32, (LANES, LANES), 0)
        co = lax.broadcasted_iota(jnp.int32, (LANES, LANES), 1)
        lane_of_co0 = GW - 1 + RPB_COLS // 2
        unflip = jnp.where((li == lane_of_co0 - co) & (co < RPB_COLS), 1.0, 0.0).astype(F32)
        g_all = jnp.dot(r_ref[...], unflip, preferred_element_type=F32, precision=HI)
        nr = 2 * NA_ROWS - 1
        for h in range(NH):
            g = g_all[h * 16:h * 16 + nr, 0:RPB_COLS]
            g_ref[0, h] = g
            d_ref[0, h], nm_ref[0, h], nv_ref[0, h] = _adam_math(w_ref[0, h], g, m_ref[0, h], v_ref[0, h])

    return _pallas(body, name="rpb_update", out_shape=[_sds(w.shape, F32)] * 4)(rev, w, m, v)


def kernel(x, c, ctx, c_ctx, w_mod, b_mod, g_norm1, w_in, rpb, conv_w, conv_b, ln_g, ln_b, w_out, g_norm2, w_up, ffn_conv_w, ffn_conv_b, w_down, g_final, loss_target, m_c_ctx, m_w_mod, m_b_mod, m_g_norm1, m_w_in, m_rpb, m_conv_w, m_conv_b, m_ln_g, m_ln_b, m_w_out, m_g_norm2, m_w_up, m_ffn_conv_w, m_ffn_conv_b, m_w_down, m_g_final, v_c_ctx, v_w_mod, v_b_mod, v_g_norm1, v_w_in, v_rpb, v_conv_w, v_conv_b, v_ln_g, v_ln_b, v_w_out, v_g_norm2, v_w_up, v_ffn_conv_w, v_ffn_conv_b, v_w_down, v_g_final):
    w = dict(c_ctx=c_ctx, w_mod=w_mod, b_mod=b_mod, g_norm1=g_norm1, w_in=w_in, rpb=rpb, conv_w=conv_w, conv_b=conv_b,
             ln_g=ln_g, ln_b=ln_b, w_out=w_out, g_norm2=g_norm2, w_up=w_up, ffn_conv_w=ffn_conv_w, ffn_conv_b=ffn_conv_b,
             w_down=w_down, g_final=g_final)
    mom = dict(c_ctx=m_c_ctx, w_mod=m_w_mod, b_mod=m_b_mod, g_norm1=m_g_norm1, w_in=m_w_in, rpb=m_rpb, conv_w=m_conv_w,
               conv_b=m_conv_b, ln_g=m_ln_g, ln_b=m_ln_b, w_out=m_w_out, g_norm2=m_g_norm2, w_up=m_w_up,
               ffn_conv_w=m_ffn_conv_w, ffn_conv_b=m_ffn_conv_b, w_down=m_w_down, g_final=m_g_final)
    var = dict(c_ctx=v_c_ctx, w_mod=v_w_mod, b_mod=v_b_mod, g_norm1=v_g_norm1, w_in=v_w_in, rpb=v_rpb, conv_w=v_conv_w,
               conv_b=v_conv_b, ln_g=v_ln_g, ln_b=v_ln_b, w_out=v_w_out, g_norm2=v_g_norm2, w_up=v_w_up,
               ffn_conv_w=v_ffn_conv_w, ffn_conv_b=v_ffn_conv_b, w_down=v_w_down, g_final=v_g_final)
    xi, yi, ci = lax.axis_index("x"), lax.axis_index("y"), lax.axis_index("c")
    dev = (4 * xi + 2 * yi + ci).astype(jnp.int32).reshape(1)
    chip = (2 * xi + yi).astype(jnp.int32).reshape(1)
    core = ci.astype(jnp.int32).reshape(1)
    c_ctx2 = c_ctx.reshape(1, D)
    g_final2 = g_final.reshape(1, D)
    mom["g_final"], var["g_final"] = m_g_final.reshape(1, D), v_g_final.reshape(1, D)

    got = _gather_small(_pack_cond(c, ffn_conv_w[0], conv_w[0]), "gather_cond")
    cond, ffn_w_all, conv_w_all = _unpack_cond(got, c_ctx2)

    mods = _gather_small(_mod_shard(cond, w_mod[0], b_mod, chip), "gather_mod")
    mod_me, mod_c = _unpack_mod(mods, dev)

    shards = {n: _cast_into_whole(n, w[n][0], chip) for n in BIG_NAMES}
    sems_in, first, token_in = _gather_start([shards["w_in"]], ("w_in",), mod_me, "w_in")
    sems, late, token = _gather_start([shards[n] for n in LATE_NAMES], LATE_NAMES, token_in, "late")
    mod_me = mod_me + token[0:1, 0:1]

    def w_in_all(after):
        arrived = _gather_wait(sems_in, first, ("w_in",), after, "w_in")
        return _forward_halves(list(arrived), ("w_in",), "w_in")[0]

    def late_weights(after):
        arrived = list(_gather_wait(sems, late, LATE_NAMES, after, "late"))
        (w_out_all,) = _forward_halves(arrived[:1], LATE_NAMES[:1], "w_out")
        fsems, passing, _ = _forward_start(arrived[1:], LATE_NAMES[1:], "ffn", after=w_out_all)
        return w_out_all, lambda after2: _forward_wait(fsems, passing, LATE_NAMES[1:], after2, "ffn")

    rpb_rev = jnp.pad(rpb[0][:, :, ::-1], ((0, 0), (0, 1), (48, LANES - 48 - RPB_COLS))).reshape(NH * 16, LANES)
    vec = dict(g_norm1=g_norm1, g_norm2=g_norm2, g_final=g_final2, conv_w=conv_w_all, conv_b=conv_b, ln_g=ln_g, ln_b=ln_b,
               ffn_conv_w=ffn_w_all, ffn_conv_b=ffn_conv_b)
    started = []

    def begin_early(d_up, d_down):
        started.append(_swap_start([d_up, d_down], EARLY_GRADS, "grad_swap_start_early"))

    def carry_on_early(after):
        sems_, grads_, lands_, _ = started.pop()
        grads_, lands_ = _swap_wait(sems_, grads_, lands_, EARLY_GRADS, after, "grad_swap_wait_early")
        parts_ = [_add_halves(n, grads_[i], lands_[i], core) for i, n in enumerate(EARLY_GRADS)]
        started.append(_exchange_start(parts_, EARLY_GRADS, "grad_exchange_start_early"))
        return started[0][3][0:1, 0:1]

    loss_p, grad_x, d_in, d_out, d_up, d_down, small = _local_step(
        x[0], ctx[0], loss_target[0], mod_me, mod_c, vec, w_in_all, late_weights, rpb_rev, (begin_early, carry_on_early))

    out = {}
    sems_, grads_, lands_, _ = _swap_start([d_in, d_out], LAST_GRADS, "grad_swap_start_last")
    early_own = _reduce_finish(started[0], EARLY_GRADS, grad_x, chip, "early")
    behind_swap = [small["rpb_rev"], small["g_norm1"][1], early_own[0]]
    grads_, lands_ = _swap_wait(sems_, grads_, lands_, LAST_GRADS, behind_swap, "grad_swap_wait_last")
    parts_ = [_add_halves(n, grads_[i], lands_[i], core) for i, n in enumerate(LAST_GRADS)]
    last_started = _exchange_start(parts_, LAST_GRADS, "grad_exchange_start_last")
    early_other = _send_halves(early_own, "grad_send_early", after=last_started[3])
    for i, n in enumerate(EARLY_GRADS):
        out[n] = _adamw_halves(n, w[n][0], early_own[i], early_other[i], mom[n][0], var[n][0], core, early_other[i])
    behind_early = [out[n][1] for n in EARLY_GRADS]

    parts = dict(dmod=small["dmod"], dmod_c=small["dmod_c"], g_norm1=[small["g_norm1"][0]], g_norm1_ctx=[small["g_norm1"][1]],
                 g_norm2=[small["g_norm2"]], g_final=[small["g_final"]], conv_b=[small["conv_b"]], ln_g=[small["ln_g"]],
                 ln_b=[small["ln_b"]], ffn_conv_b=small["ffn_conv_b"], ffn_conv_w=small["ffn_conv_w"],
                 conv_w=[small["conv_w"]], rpb_rev=[small["rpb_rev"]], loss=[loss_p])
    pack = _pack_small(parts, after=behind_early).reshape(8, PACK_N // 8)
    packs = _gather_small(pack, "gather_small_grads").reshape(8, PACK_N)
    w2 = dict(w, g_final=g_final2)
    per, dmod_all, g_conv_w_all, g_ffn_w_all, g_rpb_rev, loss_row = _small_update(packs, w2, mom, var)

    out.update(per)
    out["c_ctx"] = _cond_update(
        _gather_small(_cond_grad_partial(dmod_all, w_mod[0], chip), "gather_cond_grad"),
        c_ctx2, m_c_ctx.reshape(1, D), v_c_ctx.reshape(1, D))
    out["w_mod"] = _mod_weight_update(cond, dmod_all, w_mod[0], m_w_mod[0], v_w_mod[0], chip)
    behind = [out["w_mod"][1], out["c_ctx"][1]]
    last_own = _reduce_finish(last_started, LAST_GRADS, behind, chip, "last")
    last_other = _send_halves(last_own, "grad_send_last", after=last_own[0])
    for i, n in enumerate(LAST_GRADS):
        out[n] = _adamw_halves(n, w[n][0], last_own[i], last_other[i], mom[n][0], var[n][0], core, last_other[i])
    out["conv_w"] = _adamw_cols(conv_w[0], g_conv_w_all, m_conv_w[0], v_conv_w[0], chip, "adamw_conv_w")
    out["ffn_conv_w"] = _adamw_cols(ffn_conv_w[0], g_ffn_w_all, m_ffn_conv_w[0], v_ffn_conv_w[0], chip, "adamw_ffn_conv_w")
    out["rpb"] = _rpb_update(g_rpb_rev, rpb, m_rpb, v_rpb)

    res = [[out[n][k].reshape(w[n].shape) for n in WEIGHTS] for k in range(4)]
    return (loss_row[0, 0], grad_x[None], *res[0], *res[1], *res[2], *res[3])
```

```python
import jax
import jax.numpy as jnp
from jax import lax
from jax.experimental import pallas as pl
from jax.experimental.pallas import tpu as pltpu

F32 = jnp.float32
BF16 = jnp.bfloat16
MXU_DTYPE = jnp.bfloat16

D = 1024
CTX = 256
GW = 64
DA = 512
NH = 8
HD = 64
DC = 512
CW = 31
DFF = 2816
NIN = 3 * DA + 2 * DC
EPS = 1e-6
SCALE = HD ** -0.5
NEG = -1e30
NA_ROWS = 8
PAIR_ROWS = NA_ROWS + 1
TAB_BLOCKS = 17
LANES = 128
VMEM_LIMIT = 56 * 1024 * 1024

ADAM_LR = 0.001
ADAM_B1 = 0.9
ADAM_B2 = 0.999
ADAM_EPS = 1e-08
ADAM_WD = 0.01
ADAM_STEP = 10

MESH = pl.DeviceIdType.MESH


def _pallas(body, *, name, semantics=None, vmem=VMEM_LIMIT, prefetch=0, **kw):
    params = dict(vmem_limit_bytes=vmem)
    if semantics is not None:
        params["dimension_semantics"] = semantics
    if prefetch:
        kw["grid_spec"] = pltpu.PrefetchScalarGridSpec(
            num_scalar_prefetch=prefetch, grid=kw.pop("grid"), in_specs=kw.pop("in_specs"), out_specs=kw.pop("out_specs"),
            scratch_shapes=kw.pop("scratch_shapes", ()))
    return pl.pallas_call(body, name=name, compiler_params=pltpu.CompilerParams(**params), **kw)


def _sds(shape, dtype):
    return jax.ShapeDtypeStruct(shape, dtype)


def _vec_spec(n):
    return pl.BlockSpec((1, n), lambda *_: (0, 0))


def _colsum8(x):
    t, n = x.shape
    return jnp.sum(x.reshape(t // 8, 8, n), axis=0)


def _sigmoid(x):
    return 0.5 * jnp.tanh(0.5 * x) + 0.5


def _mm(a, b, *, mode, m, n, k, tm, tn, tk, out_dtype, name, a_off=(0, 0), b_off=(0, 0),
        out_total=None, o_off=(0, 0), into=None):
    a_list = list(a) if isinstance(a, (list, tuple)) else [a]
    b_list = list(b) if isinstance(b, (list, tuple)) else [b]
    assert m % tm == 0 and n % tn == 0 and k % tk == 0, (name, m, n, k, tm, tn, tk)
    gi, gj, nk = m // tm, n // tn, k // tk
    dims = {"nn": (((1,), (0,)), ((), ())), "nt": (((1,), (1,)), ((), ())), "tn": (((0,), (0,)), ((), ()))}[mode]

    if len(a_list) > 1:
        assert mode != "tn" and nk == 1 and sum(x.shape[1] for x in a_list) == k
        a_specs = [pl.BlockSpec((tm, x.shape[1]), lambda i, j, kk: (i, 0)) for x in a_list]
    elif mode == "tn":
        a_specs = [pl.BlockSpec((tk, tm), lambda i, j, kk: (kk + a_off[0], i + a_off[1]))]
    else:
        a_specs = [pl.BlockSpec((tm, tk), lambda i, j, kk: (i + a_off[0], kk + a_off[1]))]
    if len(b_list) > 1:
        assert mode == "tn" and gj == 1 and sum(x.shape[1] for x in b_list) == n
        b_specs = [pl.BlockSpec((tk, x.shape[1]), lambda i, j, kk: (kk, 0)) for x in b_list]
    elif mode == "nt":
        b_specs = [pl.BlockSpec((tn, tk), lambda i, j, kk: (j + b_off[0], kk + b_off[1]))]
    else:
        b_specs = [pl.BlockSpec((tk, tn), lambda i, j, kk: (kk + b_off[0], j + b_off[1]))]

    na, nb = len(a_list), len(b_list)
    in_place = nk > 1 and out_dtype == F32
    n_in = na + nb + (into is not None)

    def body(*refs):
        a_refs, b_refs, o_ref = refs[:na], refs[na:na + nb], refs[n_in]
        acc = o_ref if in_place else (refs[n_in + 1] if nk > 1 else None)
        kk = pl.program_id(2)

        def whole(piece_refs):
            vals = [r[...].astype(MXU_DTYPE) for r in piece_refs]
            return vals[0] if len(vals) == 1 else jnp.concatenate(vals, axis=1)

        p = lax.dot_general(whole(a_refs), whole(b_refs), dims, preferred_element_type=F32)
        if nk == 1:
            o_ref[...] = p.astype(out_dtype)
            return

        @pl.when(kk == 0)
        def _():
            acc[...] = p

        @pl.when(kk > 0)
        def _():
            acc[...] += p

        if not in_place:
            @pl.when(kk == nk - 1)
            def _():
                o_ref[...] = acc[...].astype(out_dtype)

    ins = [*a_list, *b_list]
    in_specs = a_specs + b_specs
    extra = {}
    if into is not None:
        extra["input_output_aliases"] = {len(ins): 0}
        ins.append(into)
        in_specs.append(pl.BlockSpec(memory_space=pl.ANY))
    return _pallas(
        body, name=name, grid=(gi, gj, nk), in_specs=in_specs,
        out_specs=pl.BlockSpec((tm, tn), lambda i, j, kk: (i + o_off[0], j + o_off[1])),
        out_shape=_sds(out_total or (m, n), out_dtype),
        scratch_shapes=[pltpu.VMEM((tm, tn), F32)] if nk > 1 and not in_place else [],
        semantics=("parallel", "parallel", "arbitrary"), **extra,
    )(*ins)


ROW_TILE = 256


def _rmsmod_fwd(x, ctx, g, sc, sh, csc, csh):
    s = x.shape[0]
    nt = s // ROW_TILE
    assert ctx.shape[0] == ROW_TILE

    def body(x_ref, c_ref, g_ref, sc_ref, sh_ref, csc_ref, csh_ref, o_ref):
        is_ctx = pl.program_id(0) == nt
        xv = jnp.where(is_ctx, c_ref[...], x_ref[...])
        scv = jnp.where(is_ctx, csc_ref[...], sc_ref[...])
        shv = jnp.where(is_ctx, csh_ref[...], sh_ref[...])
        r = lax.rsqrt(jnp.mean(xv * xv, axis=-1, keepdims=True) + EPS)
        y = xv * r * g_ref[...]
        o_ref[...] = (y * (1.0 + scv) + shv).astype(o_ref.dtype)

    return _pallas(
        body, name="rmsmod1_fwd", grid=(nt + 1,),
        in_specs=[pl.BlockSpec((ROW_TILE, D), lambda i: (jnp.minimum(i, nt - 1), 0)),
                  pl.BlockSpec((ROW_TILE, D), lambda i: (0, 0))] + [_vec_spec(D)] * 5,
        out_specs=pl.BlockSpec((ROW_TILE, D), lambda i: (i, 0)),
        out_shape=_sds((s + CTX, D), MXU_DTYPE),
        semantics=("arbitrary",),
    )(x, ctx, g, sc, sh, csc, csh)


def _resid_rmsmod_fwd(x, y, gt, g, sc, sh):
    s = x.shape[0]

    def body(x_ref, y_ref, gt_ref, g_ref, sc_ref, sh_ref, x1_ref, h_ref):
        x1 = x_ref[...] + gt_ref[...] * y_ref[...]
        x1_ref[...] = x1
        r = lax.rsqrt(jnp.mean(x1 * x1, axis=-1, keepdims=True) + EPS)
        h_ref[...] = ((x1 * r * g_ref[...]) * (1.0 + sc_ref[...]) + sh_ref[...]).astype(h_ref.dtype)

    row = pl.BlockSpec((ROW_TILE, D), lambda i: (i, 0))
    return _pallas(
        body, name="resid_rmsmod2_fwd", grid=(s // ROW_TILE,),
        in_specs=[row, row] + [_vec_spec(D)] * 4,
        out_specs=[row, row],
        out_shape=[_sds((s, D), F32), _sds((s, D), MXU_DTYPE)],
        semantics=("parallel",),
    )(x, y, gt, g, sc, sh)


def _final_fwd_bwd(x1, z, gt2, gf, tgt):
    s = x1.shape[0]
    nt = s // ROW_TILE

    def body(x1_ref, z_ref, gt_ref, gf_ref, t_ref, dx2_ref, dz_ref, loss_ref, dgt_ref, dgf_ref, a_loss, a_gt, a_gf):
        i = pl.program_id(0)

        @pl.when(i == 0)
        def _():
            a_loss[...] = jnp.zeros_like(a_loss)
            a_gt[...] = jnp.zeros_like(a_gt)
            a_gf[...] = jnp.zeros_like(a_gf)

        zv = z_ref[...]
        gt = gt_ref[...]
        gf_ = gf_ref[...]
        x2 = x1_ref[...] + gt * zv
        r = lax.rsqrt(jnp.mean(x2 * x2, axis=-1, keepdims=True) + EPS)
        xn = x2 * r
        e = xn * gf_ - t_ref[...]
        a_loss[...] += _colsum8(e * e)
        dyo = e * (1.0 / D)
        a_gf[...] += _colsum8(dyo * xn)
        gdy = gf_ * dyo
        dx2 = r * gdy - xn * (r * r) * jnp.mean(x2 * gdy, axis=-1, keepdims=True)
        dx2_ref[...] = dx2
        dz_ref[...] = (gt * dx2).astype(dz_ref.dtype)
        a_gt[...] += _colsum8(dx2 * zv)

        @pl.when(i == nt - 1)
        def _():
            tot = jnp.sum(jnp.sum(a_loss[...], axis=0, keepdims=True), axis=1, keepdims=True) * (0.5 / D)
            loss_ref[...] = jnp.broadcast_to(tot, loss_ref.shape)
            dgt_ref[...] = jnp.sum(a_gt[...], axis=0, keepdims=True)
            dgf_ref[...] = jnp.sum(a_gf[...], axis=0, keepdims=True)

    row = pl.BlockSpec((ROW_TILE, D), lambda i: (i, 0))
    return _pallas(
        body, name="final_norm_loss", grid=(nt,),
        in_specs=[row, row, _vec_spec(D), _vec_spec(D), row],
        out_specs=[row, row, _vec_spec(LANES), _vec_spec(D), _vec_spec(D)],
        out_shape=[_sds((s, D), F32), _sds((s, D), MXU_DTYPE), _sds((1, LANES), F32), _sds((1, D), F32), _sds((1, D), F32)],
        scratch_shapes=[pltpu.VMEM((8, D), F32)] * 3,
        semantics=("arbitrary",),
    )(x1, z, gt2, gf, tgt)


def _rmsmod_bwd(xin, dh, g, sc, *, name, dh_row_off=0, add=None, resid=None):
    s = xin.shape[0]
    nt = s // ROW_TILE
    want_dx = add is not None
    assert resid is None or want_dx

    def body(*refs):
        it = iter(refs)
        x_ref, dh_ref, g_ref, sc_ref = next(it), next(it), next(it), next(it)
        add_ref = next(it) if want_dx else None
        gt_ref, y_ref = (next(it), next(it)) if resid is not None else (None, None)
        dsh_ref, dsc_ref, dg_ref = next(it), next(it), next(it)
        dx_ref = next(it) if want_dx else None
        dy_ref, dgt_ref = (next(it), next(it)) if resid is not None else (None, None)
        a_sh, a_sc, a_g = next(it), next(it), next(it)
        a_gt = next(it) if resid is not None else None
        i = pl.program_id(0)

        @pl.when(i == 0)
        def _():
            a_sh[...] = jnp.zeros_like(a_sh)
            a_sc[...] = jnp.zeros_like(a_sc)
            a_g[...] = jnp.zeros_like(a_g)
            if a_gt is not None:
                a_gt[...] = jnp.zeros_like(a_gt)

        xv = x_ref[...]
        dhv = dh_ref[...]
        gv = g_ref[...]
        r = lax.rsqrt(jnp.mean(xv * xv, axis=-1, keepdims=True) + EPS)
        xn = xv * r
        a_sh[...] += _colsum8(dhv)
        a_sc[...] += _colsum8(dhv * (xn * gv))
        dn = dhv * (1.0 + sc_ref[...])
        a_g[...] += _colsum8(dn * xn)
        if want_dx:
            gdn = gv * dn
            dx = add_ref[...] + r * gdn - xn * (r * r) * jnp.mean(xv * gdn, axis=-1, keepdims=True)
            dx_ref[...] = dx
            if resid is not None:
                dy_ref[...] = (gt_ref[...] * dx).astype(dy_ref.dtype)
                a_gt[...] += _colsum8(dx * y_ref[...])

        @pl.when(i == nt - 1)
        def _():
            dsh_ref[...] = jnp.sum(a_sh[...], axis=0, keepdims=True)
            dsc_ref[...] = jnp.sum(a_sc[...], axis=0, keepdims=True)
            dg_ref[...] = jnp.sum(a_g[...], axis=0, keepdims=True)
            if a_gt is not None:
                dgt_ref[...] = jnp.sum(a_gt[...], axis=0, keepdims=True)

    row = pl.BlockSpec((ROW_TILE, D), lambda i: (i, 0))
    ins = [xin, dh, g, sc]
    in_specs = [row, pl.BlockSpec((ROW_TILE, D), lambda i: (i + dh_row_off, 0)), _vec_spec(D), _vec_spec(D)]
    out_specs = [_vec_spec(D)] * 3
    out_shape = [_sds((1, D), F32)] * 3
    scratch = [pltpu.VMEM((8, D), F32)] * 3
    if want_dx:
        ins.append(add)
        in_specs.append(row)
        out_specs.append(row)
        out_shape.append(_sds((s, D), F32))
    if resid is not None:
        ins += [resid[0], resid[1]]
        in_specs += [_vec_spec(D), row]
        out_specs += [row, _vec_spec(D)]
        out_shape += [_sds((s, D), MXU_DTYPE), _sds((1, D), F32)]
        scratch.append(pltpu.VMEM((8, D), F32))
    return _pallas(body, name=name, grid=(nt,), in_specs=in_specs, out_specs=out_specs, out_shape=out_shape,
                   scratch_shapes=scratch, semantics=("arbitrary",))(*ins)


FF_TILE = 128
FF_CHUNK = 128
HALO = 8


def _shift3(pad_ref, r0, ch):
    return tuple(pad_ref[pl.ds(r0 + HALO + d, ch), :] for d in (-1, 0, 1))


def _fill_padded(pad_ref, src_ref, s, ch, halo):
    zeros = jnp.zeros((halo, pad_ref.shape[1]), F32)
    pad_ref[0:halo, :] = zeros
    pad_ref[s + halo:s + 2 * halo, :] = zeros

    def cp(c, carry):
        r0 = pl.multiple_of(c * ch, ch)
        pad_ref[pl.ds(r0 + halo, ch), :] = src_ref[pl.ds(r0, ch), :].astype(F32)
        return carry

    lax.fori_loop(0, s // ch, cp, 0)


def _ffn_act_fwd(u, w, b):
    s = u.shape[0]
    nj = DFF // FF_TILE
    ch = FF_CHUNK

    def body(ug_ref, uv_ref, wg_ref, wv_ref, bg_ref, bv_ref, f_ref, gpad, vpad):
        _fill_padded(gpad, ug_ref, s, ch, HALO)
        _fill_padded(vpad, uv_ref, s, ch, HALO)

        def conv(pad, w_ref, b_ref, r0):
            prev, cur, nxt = _shift3(pad, r0, ch)
            return w_ref[0:1, :] * prev + w_ref[1:2, :] * cur + w_ref[2:3, :] * nxt + b_ref[...]

        def step(c, carry):
            r0 = pl.multiple_of(c * ch, ch)
            gc = conv(gpad, wg_ref, bg_ref, r0)
            vc = conv(vpad, wv_ref, bv_ref, r0)
            f_ref[pl.ds(r0, ch), :] = (gc * _sigmoid(gc) * vc).astype(f_ref.dtype)
            return carry

        lax.fori_loop(0, s // ch, step, 0)

    col = lambda off: pl.BlockSpec((s, FF_TILE), lambda j: (0, j + off))
    wsp = lambda off: pl.BlockSpec((3, FF_TILE), lambda j: (0, j + off))
    bsp = lambda off: pl.BlockSpec((1, FF_TILE), lambda j: (0, j + off))
    return _pallas(
        body, name="ffn_act_fwd", grid=(nj,),
        in_specs=[col(0), col(nj), wsp(0), wsp(nj), bsp(0), bsp(nj)],
        out_specs=col(0), out_shape=_sds((s, DFF), MXU_DTYPE),
        scratch_shapes=[pltpu.VMEM((s + 2 * HALO, FF_TILE), F32)] * 2,
        semantics=("parallel",),
    )(u, u, w, w, b, b)


def _ffn_act_bwd(u, df, w, b):
    s = u.shape[0]
    nj = DFF // FF_TILE
    ch = FF_CHUNK

    def body(ug_ref, uv_ref, df_ref, wg_ref, wv_ref, bg_ref, bv_ref,
             dug_ref, duv_ref, dwg_ref, dwv_ref, dbg_ref, dbv_ref, gpad, vpad, dgpad, dvpad, acc):
        _fill_padded(gpad, ug_ref, s, ch, HALO)
        _fill_padded(vpad, uv_ref, s, ch, HALO)
        zeros = jnp.zeros((HALO, FF_TILE), F32)
        for p in (dgpad, dvpad):
            p[0:HALO, :] = zeros
            p[s + HALO:s + 2 * HALO, :] = zeros
        acc[...] = jnp.zeros_like(acc)

        def step(c, carry):
            r0 = pl.multiple_of(c * ch, ch)
            gs = _shift3(gpad, r0, ch)
            vs = _shift3(vpad, r0, ch)
            gc = wg_ref[0:1, :] * gs[0] + wg_ref[1:2, :] * gs[1] + wg_ref[2:3, :] * gs[2] + bg_ref[...]
            vc = wv_ref[0:1, :] * vs[0] + wv_ref[1:2, :] * vs[1] + wv_ref[2:3, :] * vs[2] + bv_ref[...]
            sg = _sigmoid(gc)
            dfv = df_ref[pl.ds(r0, ch), :].astype(F32)
            dgc = dfv * vc * (sg * (1.0 + gc * (1.0 - sg)))
            dvc = dfv * (gc * sg)
            dgpad[pl.ds(r0 + HALO, ch), :] = dgc
            dvpad[pl.ds(r0 + HALO, ch), :] = dvc
            for t in range(3):
                acc[8 * t:8 * t + 8, :] += _colsum8(dgc * gs[t])
                acc[24 + 8 * t:32 + 8 * t, :] += _colsum8(dvc * vs[t])
            acc[48:56, :] += _colsum8(dgc)
            acc[56:64, :] += _colsum8(dvc)
            return carry

        lax.fori_loop(0, s // ch, step, 0)

        def step2(c, carry):
            r0 = pl.multiple_of(c * ch, ch)
            for pad, w_ref, o_ref in ((dgpad, wg_ref, dug_ref), (dvpad, wv_ref, duv_ref)):
                prev, cur, nxt = _shift3(pad, r0, ch)
                o_ref[pl.ds(r0, ch), :] = (w_ref[0:1, :] * nxt + w_ref[1:2, :] * cur + w_ref[2:3, :] * prev).astype(o_ref.dtype)
            return carry

        lax.fori_loop(0, s // ch, step2, 0)
        for t in range(3):
            dwg_ref[t:t + 1, :] = jnp.sum(acc[8 * t:8 * t + 8, :], axis=0, keepdims=True)
            dwv_ref[t:t + 1, :] = jnp.sum(acc[24 + 8 * t:32 + 8 * t, :], axis=0, keepdims=True)
        dbg_ref[...] = jnp.sum(acc[48:56, :], axis=0, keepdims=True)
        dbv_ref[...] = jnp.sum(acc[56:64, :], axis=0, keepdims=True)

    col = lambda off: pl.BlockSpec((s, FF_TILE), lambda j: (0, j + off))
    wsp = lambda off: pl.BlockSpec((3, FF_TILE), lambda j: (0, j + off))
    bsp = lambda off: pl.BlockSpec((1, FF_TILE), lambda j: (0, j + off))
    return _pallas(
        body, name="ffn_act_bwd", grid=(nj,),
        in_specs=[col(0), col(nj), col(0), wsp(0), wsp(nj), bsp(0), bsp(nj)],
        out_specs=[col(0), col(0), wsp(0), wsp(0), bsp(0), bsp(0)],
        out_shape=[_sds((s, DFF), MXU_DTYPE)] * 2 + [_sds((3, DFF), F32)] * 2 + [_sds((1, DFF), F32)] * 2,
        scratch_shapes=[pltpu.VMEM((s + 2 * HALO, FF_TILE), F32)] * 4 + [pltpu.VMEM((64, FF_TILE), F32)],
        semantics=("parallel",),
    )(u, u, df, w, w, b, b)


CONV_CHUNK = 64
CONV_HALO = 16


def _tap(pad_ref, r0, k):
    return pad_ref[pl.ds(r0 + CONV_HALO - CW // 2 + k, CONV_CHUNK), :]


def _glu_into(pad_ref, a_ref, g_ref, s):
    zeros = jnp.zeros((CONV_HALO, LANES), F32)
    pad_ref[0:CONV_HALO, :] = zeros
    pad_ref[s + CONV_HALO:s + 2 * CONV_HALO, :] = zeros

    def cp(c, carry):
        r0 = pl.multiple_of(c * ROW_TILE, ROW_TILE)
        pad_ref[pl.ds(r0 + CONV_HALO, ROW_TILE), :] = a_ref[pl.ds(r0, ROW_TILE), :] * _sigmoid(g_ref[pl.ds(r0, ROW_TILE), :])
        return carry

    lax.fori_loop(0, s // ROW_TILE, cp, 0)


def _conf_conv_fwd(ag, conv_w, conv_b):
    s = ag.shape[0]
    nc = DC // LANES

    def body(a_ref, g_ref, w_ref, b_ref, o_ref, upad):
        _glu_into(upad, a_ref, g_ref, s)

        def step(c, carry):
            r0 = pl.multiple_of(c * CONV_CHUNK, CONV_CHUNK)
            acc = jnp.broadcast_to(b_ref[...], (CONV_CHUNK, LANES))
            for k in range(CW):
                acc = acc + w_ref[k:k + 1, :] * _tap(upad, r0, k)
            o_ref[pl.ds(r0, CONV_CHUNK), :] = acc
            return carry

        lax.fori_loop(0, s // CONV_CHUNK, step, 0)

    col = lambda off: pl.BlockSpec((s, LANES), lambda c: (0, c + off))
    return _pallas(
        body, name="conf_conv_fwd", grid=(nc,),
        in_specs=[col(0), col(nc), pl.BlockSpec((CW, LANES), lambda c: (0, c)), pl.BlockSpec((1, LANES), lambda c: (0, c))],
        out_specs=col(0), out_shape=_sds((s, DC), F32),
        scratch_shapes=[pltpu.VMEM((s + 2 * CONV_HALO, LANES), F32)],
        semantics=("parallel",),
    )(ag, ag, conv_w, conv_b)


def _ln_stats(x):
    mu = jnp.mean(x, axis=-1, keepdims=True)
    xc = x - mu
    var = jnp.mean(xc * xc, axis=-1, keepdims=True)
    rstd = lax.rsqrt(var + EPS)
    return xc * rstd, rstd


def _conf_ln_fwd(u1, ln_g, ln_b, ycat):
    s = u1.shape[0]

    def body(u_ref, g_ref, b_ref, ycat_ref, o_ref):
        del ycat_ref
        xhat, _ = _ln_stats(u_ref[...])
        y = xhat * g_ref[...] + b_ref[...]
        o_ref[...] = (y * _sigmoid(y)).astype(o_ref.dtype)

    return _pallas(
        body, name="conf_ln_fwd", grid=(s // ROW_TILE,),
        in_specs=[pl.BlockSpec((ROW_TILE, DC), lambda i: (i, 0)), _vec_spec(DC), _vec_spec(DC),
                  pl.BlockSpec(memory_space=pl.ANY)],
        out_specs=pl.BlockSpec((ROW_TILE, DC), lambda i: (i, 1)),
        out_shape=_sds(ycat.shape, ycat.dtype),
        input_output_aliases={3: 0},
        semantics=("parallel",),
    )(u1, ln_g, ln_b, ycat)


def _conf_ln_bwd(dycat, u1, ln_g, ln_b):
    s = u1.shape[0]
    nt = s // ROW_TILE

    def body(dy_ref, u_ref, g_ref, b_ref, du_ref, dg_ref, db_ref, a_g, a_b):
        i = pl.program_id(0)

        @pl.when(i == 0)
        def _():
            a_g[...] = jnp.zeros_like(a_g)
            a_b[...] = jnp.zeros_like(a_b)

        xhat, rstd = _ln_stats(u_ref[...])
        gv = g_ref[...]
        y = xhat * gv + b_ref[...]
        sg = _sigmoid(y)
        dyl = dy_ref[...] * (sg * (1.0 + y * (1.0 - sg)))
        a_g[...] += _colsum8(dyl * xhat)
        a_b[...] += _colsum8(dyl)
        dxh = dyl * gv
        du_ref[...] = rstd * (dxh - jnp.mean(dxh, axis=-1, keepdims=True)
                              - xhat * jnp.mean(dxh * xhat, axis=-1, keepdims=True))

        @pl.when(i == nt - 1)
        def _():
            dg_ref[...] = jnp.sum(a_g[...], axis=0, keepdims=True)
            db_ref[...] = jnp.sum(a_b[...], axis=0, keepdims=True)

    return _pallas(
        body, name="conf_ln_bwd", grid=(nt,),
        in_specs=[pl.BlockSpec((ROW_TILE, DC), lambda i: (i, 1)), pl.BlockSpec((ROW_TILE, DC), lambda i: (i, 0)),
                  _vec_spec(DC), _vec_spec(DC)],
        out_specs=[pl.BlockSpec((ROW_TILE, DC), lambda i: (i, 0)), _vec_spec(DC), _vec_spec(DC)],
        out_shape=[_sds((s, DC), F32), _sds((1, DC), F32), _sds((1, DC), F32)],
        scratch_shapes=[pltpu.VMEM((8, DC), F32)] * 2,
        semantics=("arbitrary",),
    )(dycat, u1, ln_g, ln_b)


def _conf_conv_bwd(ag, du1, conv_w, rows_out):
    s = ag.shape[0]
    nc = DC // LANES

    def body(a_ref, g_ref, d_ref, w_ref, da_ref, dg_ref, dw_ref, db_ref, upad, dpad, acc):
        _glu_into(upad, a_ref, g_ref, s)
        _fill_padded(dpad, d_ref, s, ROW_TILE, CONV_HALO)
        acc[...] = jnp.zeros_like(acc)

        def step(c, carry):
            r0 = pl.multiple_of(c * CONV_CHUNK, CONV_CHUNK)
            dcur = dpad[pl.ds(r0 + CONV_HALO, CONV_CHUNK), :]
            du0 = jnp.zeros((CONV_CHUNK, LANES), F32)
            for k in range(CW):
                du0 = du0 + w_ref[k:k + 1, :] * _tap(dpad, r0, CW - 1 - k)
                acc[8 * k:8 * k + 8, :] += _colsum8(dcur * _tap(upad, r0, k))
            acc[8 * CW:8 * CW + 8, :] += _colsum8(dcur)
            av = a_ref[pl.ds(r0, CONV_CHUNK), :]
            sg = _sigmoid(g_ref[pl.ds(r0, CONV_CHUNK), :])
            da_ref[pl.ds(r0, CONV_CHUNK), :] = (du0 * sg).astype(da_ref.dtype)
            dg_ref[pl.ds(r0, CONV_CHUNK), :] = (du0 * av * (sg * (1.0 - sg))).astype(dg_ref.dtype)
            return carry

        lax.fori_loop(0, s // CONV_CHUNK, step, 0)
        if rows_out > s:
            zeros = jnp.zeros((rows_out - s, LANES), da_ref.dtype)
            da_ref[s:rows_out, :] = zeros
            dg_ref[s:rows_out, :] = zeros
        for k in range(CW):
            dw_ref[k:k + 1, :] = jnp.sum(acc[8 * k:8 * k + 8, :], axis=0, keepdims=True)
        db_ref[...] = jnp.sum(acc[8 * CW:8 * CW + 8, :], axis=0, keepdims=True)

    col = lambda off: pl.BlockSpec((s, LANES), lambda c: (0, c + off))
    ocol = pl.BlockSpec((rows_out, LANES), lambda c: (0, c))
    return _pallas(
        body, name="conf_conv_bwd", grid=(nc,),
        in_specs=[col(0), col(nc), col(0), pl.BlockSpec((CW, LANES), lambda c: (0, c))],
        out_specs=[ocol, ocol, pl.BlockSpec((CW, LANES), lambda c: (0, c)), pl.BlockSpec((1, LANES), lambda c: (0, c))],
        out_shape=[_sds((rows_out, DC), MXU_DTYPE)] * 2 + [_sds((CW, DC), F32), _sds((1, DC), F32)],
        scratch_shapes=[pltpu.VMEM((s + 2 * CONV_HALO, LANES), F32)] * 2 + [pltpu.VMEM((8 * (CW + 1), LANES), F32)],
        semantics=("parallel",),
    )(ag, ag, du1, conv_w)


Q_TILE = 2 * GW
K_WIN = PAIR_ROWS * GW


def _bias_table(rpb_rev):
    def body(p_ref, t_ref):
        kcol = lax.broadcasted_iota(jnp.int32, (GW, LANES), 0)
        lane = lax.broadcasted_iota(jnp.int32, (GW, LANES), 1)
        qcol = lane % GW
        cs = jnp.clip(qcol - NA_ROWS, 0, GW - 2 * NA_ROWS)
        colvalid = (kcol >= cs) & (kcol < cs + 2 * NA_ROWS)
        neg = jnp.full((GW, LANES), NEG, F32)

        def skew(h, ro, shift):
            if ro < 0 or ro >= 2 * NA_ROWS - 1:
                return neg
            row = jnp.broadcast_to(p_ref[h * 16 + ro:h * 16 + ro + 1, :], (GW, LANES))
            return pltpu.roll(row, shift, 1, stride=1, stride_axis=0)

        for h in range(NH):
            for b in range(TAB_BLOCKS):
                val = jnp.where(lane < GW, skew(h, b - 1, GW + 1), skew(h, b - 2, 1))
                t_ref[h, b * GW:(b + 1) * GW, :] = jnp.where(colvalid, val, neg)

    return _pallas(body, name="attn_bias_table", out_shape=_sds((NH, TAB_BLOCKS * GW, LANES), F32))(rpb_rev)


def _rpb_grad(tt):
    def body(t_ref, o_ref):
        lane = lax.broadcasted_iota(jnp.int32, (GW, LANES), 1)
        si = lax.broadcasted_iota(jnp.int32, (GW, GW), 0)
        ti = lax.broadcasted_iota(jnp.int32, (GW, GW), 1)
        flip = jnp.where(si + ti == GW - 1, 1.0, 0.0).astype(F32)
        o_ref[...] = jnp.zeros_like(o_ref)
        for h in range(NH):
            for ro in range(2 * NA_ROWS - 1):
                lo = t_ref[h, (ro + 1) * GW:(ro + 2) * GW, :]
                hi = t_ref[h, (ro + 2) * GW:(ro + 3) * GW, :]
                g = jnp.where(lane < GW, lo + pltpu.roll(hi, GW, 1), 0.0)
                gf = jnp.dot(flip, g, preferred_element_type=F32, precision=lax.Precision.HIGHEST)
                sk = pltpu.roll(gf, 0, 1, stride=1, stride_axis=0)
                o_ref[h * 16 + ro:h * 16 + ro + 1, :] = jnp.sum(sk, axis=0, keepdims=True)

    return _pallas(body, name="attn_rpb_grad", out_shape=_sds((NH * 16, LANES), F32))(tt)


def _attn_geometry(i, rows):
    wsp = jnp.clip(2 * i - NA_ROWS // 2, 0, rows - PAIR_ROWS)
    k0 = pl.multiple_of(wsp * GW, GW)
    t0 = pl.multiple_of((wsp - 2 * i + NA_ROWS) * GW, GW)
    rr = lax.broadcasted_iota(jnp.int32, (GW, Q_TILE), 1) // GW
    wsr = jnp.clip(2 * i + rr - NA_ROWS // 2, 0, rows - NA_ROWS)
    edge_masks = tuple(jnp.where((kr >= wsr) & (kr < wsr + NA_ROWS), 0.0, NEG).astype(F32)
                       for kr in (wsp, wsp + PAIR_ROWS - 1))
    return k0, t0, edge_masks


def _biased(s_raw, bias, edge_masks):
    x = s_raw + bias
    return jnp.concatenate([x[:GW] + edge_masks[0], x[GW:K_WIN - GW], x[K_WIN - GW:] + edge_masks[1]], axis=0)


def _two_heads_on_lanes(xt):
    feat = lax.broadcasted_iota(jnp.int32, xt.shape, 0)
    zero = jnp.zeros_like(xt)
    return jnp.concatenate([jnp.where(feat < HD, xt, zero), jnp.where(feat >= HD, xt, zero)], axis=1)


def _two_heads_on_rows(x):
    lane = lax.broadcasted_iota(jnp.int32, x.shape, 1)
    zero = jnp.zeros_like(x)
    return jnp.concatenate([jnp.where(lane < HD, x, zero), jnp.where(lane >= HD, x, zero)], axis=0)


def _pick_heads(x2):
    n = x2.shape[0] // 2
    lane = lax.broadcasted_iota(jnp.int32, (n, LANES), 1)
    return jnp.where(lane < HD, x2[:n], x2[n:])


_TN = (((0,), (0,)), ((), ()))


def _attn_fwd(qkv, tab, s):
    rows = s // GW
    npair = rows // 2

    def body(q_ref, kv_ref, tab_ref, o_ref, lse_ref):
        i = pl.program_id(0)
        k0, t0, edge_masks = _attn_geometry(i, rows)
        for p in range(NH // 2):
            cq = slice(p * LANES, (p + 1) * LANES)
            ck = slice(DA + p * LANES, DA + (p + 1) * LANES)
            cv = slice(2 * DA + p * LANES, 2 * DA + (p + 1) * LANES)
            qm2 = _two_heads_on_lanes(q_ref[:, cq].T) * SCALE
            s_loc = jnp.dot(kv_ref[pl.ds(k0, K_WIN), ck], qm2, preferred_element_type=F32)
            s_ctx = jnp.dot(kv_ref[pl.ds(s, CTX), ck], qm2, preferred_element_type=F32)
            p_loc, p_ctx = [], []
            for hh in range(2):
                h = 2 * p + hh
                ch = slice(hh * Q_TILE, (hh + 1) * Q_TILE)
                sl = _biased(s_loc[:, ch], tab_ref[h, pl.ds(t0, K_WIN), :], edge_masks)
                sc = s_ctx[:, ch]
                m = jnp.maximum(jnp.max(sl, axis=0, keepdims=True), jnp.max(sc, axis=0, keepdims=True))
                el = jnp.exp(sl - m)
                ec = jnp.exp(sc - m)
                l = jnp.sum(el, axis=0, keepdims=True) + jnp.sum(ec, axis=0, keepdims=True)
                inv = 1.0 / l
                lse_ref[h:h + 1, :] = m + jnp.log(l)
                p_loc.append((el * inv).astype(MXU_DTYPE))
                p_ctx.append((ec * inv).astype(MXU_DTYPE))
            o2 = (lax.dot_general(jnp.concatenate(p_loc, axis=1), kv_ref[pl.ds(k0, K_WIN), cv], _TN, preferred_element_type=F32)
                  + lax.dot_general(jnp.concatenate(p_ctx, axis=1), kv_ref[pl.ds(s, CTX), cv], _TN, preferred_element_type=F32))
            o_ref[:, cq] = _pick_heads(o2).astype(o_ref.dtype)

    return _pallas(
        body, name="attn_fwd", grid=(npair,),
        in_specs=[pl.BlockSpec((Q_TILE, DA), lambda i: (i, 0)), pl.BlockSpec(memory_space=pltpu.VMEM),
                  pl.BlockSpec(memory_space=pltpu.VMEM)],
        out_specs=[pl.BlockSpec((Q_TILE, DA), lambda i: (i, 0)), pl.BlockSpec((NH, Q_TILE), lambda i: (0, i))],
        out_shape=[_sds((s, D), MXU_DTYPE), _sds((NH, s), F32)],
        semantics=("arbitrary",),
    )(qkv, qkv, tab)


def _attn_bwd(qkv, tab, lse, dycat, s):
    rows = s // GW
    npair = rows // 2
    sa = s + CTX
    nzero = CTX // Q_TILE

    def body(q_ref, do_ref, lse_ref, kv_ref, tab_ref, dq_ref, dkv_ref, tt_ref, dk_acc, dv_acc):
        i = pl.program_id(0)

        @pl.when(i == 0)
        def _():
            dk_acc[...] = jnp.zeros_like(dk_acc)
            dv_acc[...] = jnp.zeros_like(dv_acc)
            tt_ref[...] = jnp.zeros_like(tt_ref)

        @pl.when(i >= npair)
        def _():
            dq_ref[...] = jnp.zeros_like(dq_ref)

        @pl.when(i < npair)
        def _():
            k0, t0, edge_masks = _attn_geometry(i, rows)
            for p in range(NH // 2):
                cq = slice(p * LANES, (p + 1) * LANES)
                ck = slice(DA + p * LANES, DA + (p + 1) * LANES)
                cv = slice(2 * DA + p * LANES, 2 * DA + (p + 1) * LANES)
                qp = q_ref[:, cq] * SCALE
                dop = do_ref[:, cq].astype(MXU_DTYPE)
                qm2 = _two_heads_on_lanes(qp.T)
                dom2 = _two_heads_on_lanes(dop.T)
                kw = kv_ref[pl.ds(k0, K_WIN), ck]
                kc = kv_ref[pl.ds(s, CTX), ck]
                vw = kv_ref[pl.ds(k0, K_WIN), cv]
                vc = kv_ref[pl.ds(s, CTX), cv]
                s_loc = jnp.dot(kw, qm2, preferred_element_type=F32)
                s_ctx = jnp.dot(kc, qm2, preferred_element_type=F32)
                dp_loc = jnp.dot(vw, dom2, preferred_element_type=F32)
                dp_ctx = jnp.dot(vc, dom2, preferred_element_type=F32)
                p_loc, p_ctx, ds_loc, ds_ctx = [], [], [], []
                for hh in range(2):
                    h = 2 * p + hh
                    ch = slice(hh * Q_TILE, (hh + 1) * Q_TILE)
                    lse_h = lse_ref[h:h + 1, :]
                    pl_ = jnp.exp(_biased(s_loc[:, ch], tab_ref[h, pl.ds(t0, K_WIN), :], edge_masks) - lse_h)
                    pc_ = jnp.exp(s_ctx[:, ch] - lse_h)
                    dpl = dp_loc[:, ch]
                    dpc = dp_ctx[:, ch]
                    delta = jnp.sum(pl_ * dpl, axis=0, keepdims=True) + jnp.sum(pc_ * dpc, axis=0, keepdims=True)
                    dsl = pl_ * (dpl - delta)
                    dsc = pc_ * (dpc - delta)
                    tt_ref[h, pl.ds(t0, K_WIN), :] += dsl
                    p_loc.append(pl_.astype(MXU_DTYPE))
                    p_ctx.append(pc_.astype(MXU_DTYPE))
                    ds_loc.append(dsl.astype(MXU_DTYPE))
                    ds_ctx.append(dsc.astype(MXU_DTYPE))
                p_loc, p_ctx = jnp.concatenate(p_loc, axis=1), jnp.concatenate(p_ctx, axis=1)
                ds_loc, ds_ctx = jnp.concatenate(ds_loc, axis=1), jnp.concatenate(ds_ctx, axis=1)
                do_rows = _two_heads_on_rows(dop)
                q_rows = _two_heads_on_rows(qp)
                dv_acc[pl.ds(k0, K_WIN), cq] += jnp.dot(p_loc, do_rows, preferred_element_type=F32)
                dv_acc[pl.ds(s, CTX), cq] += jnp.dot(p_ctx, do_rows, preferred_element_type=F32)
                dk_acc[pl.ds(k0, K_WIN), cq] += jnp.dot(ds_loc, q_rows, preferred_element_type=F32)
                dk_acc[pl.ds(s, CTX), cq] += jnp.dot(ds_ctx, q_rows, preferred_element_type=F32)
                dq2 = (lax.dot_general(ds_loc, kw, _TN, preferred_element_type=F32)
                       + lax.dot_general(ds_ctx, kc, _TN, preferred_element_type=F32))
                dq_ref[:, cq] = (_pick_heads(dq2) * SCALE).astype(dq_ref.dtype)

        @pl.when(i == npair - 1)
        def _():
            def cp(c, carry):
                r0 = pl.multiple_of(c * ROW_TILE, ROW_TILE)
                dkv_ref[pl.ds(r0, ROW_TILE), 0:DA] = dk_acc[pl.ds(r0, ROW_TILE), :].astype(dkv_ref.dtype)
                dkv_ref[pl.ds(r0, ROW_TILE), DA:2 * DA] = dv_acc[pl.ds(r0, ROW_TILE), :].astype(dkv_ref.dtype)
                return carry

            lax.fori_loop(0, sa // ROW_TILE, cp, 0)

    qmap = lambda i: (jnp.minimum(i, npair - 1), 0)
    return _pallas(
        body, name="attn_bwd", grid=(npair + nzero,),
        in_specs=[pl.BlockSpec((Q_TILE, DA), qmap), pl.BlockSpec((Q_TILE, DA), qmap),
                  pl.BlockSpec((NH, Q_TILE), lambda i: (0, jnp.minimum(i, npair - 1))),
                  pl.BlockSpec(memory_space=pltpu.VMEM), pl.BlockSpec(memory_space=pltpu.VMEM)],
        out_specs=[pl.BlockSpec((Q_TILE, DA), lambda i: (i, 0)), pl.BlockSpec(memory_space=pltpu.VMEM),
                   pl.BlockSpec(memory_space=pltpu.VMEM)],
        out_shape=[_sds((sa, DA), MXU_DTYPE), _sds((sa, 2 * DA), MXU_DTYPE), _sds((NH, TAB_BLOCKS * GW, LANES), F32)],
        scratch_shapes=[pltpu.VMEM((sa, DA), F32)] * 2,
        semantics=("arbitrary",),
    )(qkv, dycat, lse, qkv, tab)


def _tile(n, prefs):
    for t in prefs:
        if n % t == 0:
            return t
    raise ValueError((n, prefs))


def _local_step(x, ctx, tgt, mod, mod_c, vec, w_in, late_weights, rpb_rev, early_grads=None):
    s = x.shape[0]
    sa = s + CTX
    ts = _tile(s, (1024, 512, 256))
    ts2 = _tile(s, (2048, 1024, 512, 256))
    tsa = _tile(sa, (1088, 640, 256))
    tsa2 = _tile(sa, (2176, 640, 256))
    sh1, sc1, gt1, sh2, sc2, gt2 = (mod[i:i + 1] for i in range(6))
    csh1, csc1 = mod_c[0:1], mod_c[1:2]
    act = MXU_DTYPE

    tab = _bias_table(rpb_rev)
    h_all = _rmsmod_fwd(x, ctx, vec["g_norm1"], sc1, sh1, csc1, csh1)
    w_in = w_in(h_all) if callable(w_in) else w_in
    qkv = _mm(h_all, w_in, mode="nn", m=sa, n=3 * DA, k=D, tm=tsa2, tn=512, tk=D, out_dtype=MXU_DTYPE, name="mm_qkv")
    ag = _mm(h_all, w_in, mode="nn", m=s, n=2 * DC, k=D, tm=ts2, tn=512, tk=D, out_dtype=F32, name="mm_ag", b_off=(0, 3))
    ycat, lse = _attn_fwd(qkv, tab, s)
    u1 = _conf_conv_fwd(ag, vec["conv_w"], vec["conv_b"])
    ycat = _conf_ln_fwd(u1, vec["ln_g"], vec["ln_b"], ycat)
    if callable(late_weights):
        w_out, ffn_weights = late_weights(ycat)
    else:
        w_out, ffn_weights = late_weights[0], late_weights[1:]
    y = _mm(ycat, w_out, mode="nn", m=s, n=D, k=D, tm=ts2, tn=512, tk=D, out_dtype=F32, name="mm_out")
    x1, h2 = _resid_rmsmod_fwd(x, y, gt1, vec["g_norm2"], sc2, sh2)
    w_up, w_down = ffn_weights(h2) if callable(ffn_weights) else ffn_weights
    u = _mm(h2, w_up, mode="nn", m=s, n=2 * DFF, k=D, tm=ts2, tn=512, tk=D, out_dtype=act, name="mm_up")
    f = _ffn_act_fwd(u, vec["ffn_conv_w"], vec["ffn_conv_b"])
    z = _mm(f, w_down, mode="nn", m=s, n=D, k=DFF, tm=ts, tn=D, tk=DFF, out_dtype=F32, name="mm_down")
    dx2, dz, loss, dgt2, dgf = _final_fwd_bwd(x1, z, gt2, vec["g_final"], tgt)

    df = _mm(dz, w_down, mode="nt", m=s, n=DFF, k=D, tm=ts, tn=DFF, tk=D, out_dtype=act, name="mm_down_dx")
    d_w_down = _mm(f, dz, mode="tn", m=DFF, n=D, k=s, tm=DFF // 2, tn=D, tk=ts2, out_dtype=F32, name="mm_down_dw")
    dug, duv, dfw_g, dfw_v, dfb_g, dfb_v = _ffn_act_bwd(u, df, vec["ffn_conv_w"], vec["ffn_conv_b"])
    dw_kw = dict(mode="tn", m=D, n=DFF, k=s, tm=D, tn=DFF, tk=ts, out_dtype=F32, out_total=(D, 2 * DFF))
    d_w_up = _mm(h2, dug, name="mm_up_dw_gate", **dw_kw)
    d_w_up = _mm(h2, duv, name="mm_up_dw_val", o_off=(0, 1), into=d_w_up, **dw_kw)
    if early_grads is not None:
        early_grads[0](d_w_up, d_w_down)
    dh2 = _mm([dug, duv], w_up, mode="nt", m=s, n=D, k=2 * DFF, tm=ts, tn=D, tk=2 * DFF, out_dtype=F32, name="mm_up_dx")
    sc2_b = sc2 if early_grads is None else sc2 + early_grads[1](dh2)
    dsh2, dsc2, dg2, dx1, dy, dgt1 = _rmsmod_bwd(x1, dh2, vec["g_norm2"], sc2_b, name="rmsmod2_bwd", add=dx2, resid=(gt1, y))
    dycat = _mm(dy, w_out, mode="nt", m=s, n=D, k=D, tm=ts2, tn=512, tk=D, out_dtype=F32, name="mm_out_dx")
    d_w_out = _mm(ycat, dy, mode="tn", m=D, n=D, k=s, tm=D, tn=D, tk=ts, out_dtype=F32, name="mm_out_dw")
    du1, dln_g, dln_b = _conf_ln_bwd(dycat, u1, vec["ln_g"], vec["ln_b"])
    da, dg, dconv_w, dconv_b = _conf_conv_bwd(ag, du1, vec["conv_w"], sa)
    dq, dkv, tt = _attn_bwd(qkv, tab, lse, dycat, s)
    drpb_rev = _rpb_grad(tt)
    d_pieces = [dq, dkv, da, dg]
    dh = _mm(d_pieces, w_in, mode="nt", m=sa, n=D, k=NIN, tm=tsa, tn=D, tk=NIN, out_dtype=F32, name="mm_in_dx")
    d_w_in = _mm(h_all, d_pieces, mode="tn", m=D, n=NIN, k=sa, tm=D, tn=NIN, tk=tsa, out_dtype=F32, name="mm_in_dw")
    dsh1, dsc1, dg1, grad_x = _rmsmod_bwd(x, dh, vec["g_norm1"], sc1, name="rmsmod1_bwd", add=dx1)
    dcsh1, dcsc1, dg1c = _rmsmod_bwd(ctx, dh, vec["g_norm1"], csc1, name="rmsmod1_ctx_bwd", dh_row_off=s // ROW_TILE)

    small = dict(
        dmod=[dsh1, dsc1, dgt1, dsh2, dsc2, dgt2], dmod_c=[dcsh1, dcsc1],
        g_norm1=[dg1, dg1c], g_norm2=dg2, g_final=dgf, conv_b=dconv_b, ln_g=dln_g, ln_b=dln_b, conv_w=dconv_w,
        ffn_conv_w=[dfw_g, dfw_v], ffn_conv_b=[dfb_g, dfb_v], rpb_rev=drpb_rev,
    )
    return loss, grad_x, d_w_in, d_w_out, d_w_up, d_w_down, small


N_CHIPS = 4
HBM = pl.BlockSpec(memory_space=pl.ANY)
BIG = {"w_in": ("col", (D, NIN)), "w_out": ("row", (D, D)), "w_up": ("col", (D, 2 * DFF)), "w_down": ("row", (DFF, D))}
BIG_NAMES = tuple(BIG)
LATE_NAMES = ("w_out", "w_up", "w_down")


def _shard_shape(name):
    kind, (r, c) = BIG[name]
    return (r, c // N_CHIPS) if kind == "col" else (r // N_CHIPS, c)


def _half_rows(name):
    return _shard_shape(name)[0] // 2


def _place():
    x, y, c = lax.axis_index("x"), lax.axis_index("y"), lax.axis_index("c")
    others = [(1 - x, y), (x, 1 - y), (1 - x, 1 - y)]
    return x, y, c, 2 * x + y, (x, y, 1 - c), others


def _whole_region(ref, name, chip, half):
    kind, _ = BIG[name]
    r, c = _shard_shape(name)
    if kind == "col":
        return ref.at[pl.ds(half * (r // 2), r // 2), pl.ds(chip * c, c)]
    return ref.at[pl.ds(chip * r + half * (r // 2), r // 2), :]


def _remote(src, dst, send_sem, recv_sem, to):
    return pltpu.make_async_remote_copy(src_ref=src, dst_ref=dst, send_sem=send_sem, recv_sem=recv_sem,
                                        device_id=to, device_id_type=MESH)


def _gather_small(v, name):
    m_per, n = v.shape

    def body(x_ref, out_ref, send_sems, recv_sems, local_sem):
        x, y, c, _, sibling, others = _place()
        me = (x, y, c)

        def rows(px, py, pc):
            return out_ref.at[pl.ds((4 * px + 2 * py + pc) * m_per, m_per), :]

        def copy(k, block, to, src=None):
            return _remote(rows(*block) if src is None else src, rows(*block), send_sems.at[k], recv_sems.at[k], to)

        mine = pltpu.make_async_copy(x_ref, rows(*me), local_sem)
        mine.start()
        first = [copy(0, me, sibling, src=x_ref)]
        first += [copy(1 + j, me, (*chip, c), src=x_ref) for j, chip in enumerate(others)]
        for cp in first:
            cp.start()
        passed = [copy(4 + j, (*chip, c), sibling) for j, chip in enumerate(others)]
        for j, chip in enumerate(others):
            copy(1 + j, (*chip, c), me).wait_recv()
            passed[j].start()
        copy(0, sibling, me).wait_recv()
        for j, chip in enumerate(others):
            copy(4 + j, (*chip, 1 - c), me).wait_recv()
        for cp in first + passed:
            cp.wait_send()
        mine.wait()

    return pl.pallas_call(
        body, name=name, out_shape=_sds((8 * m_per, n), v.dtype),
        in_specs=[pl.BlockSpec(memory_space=pltpu.VMEM)], out_specs=pl.BlockSpec(memory_space=pltpu.VMEM),
        scratch_shapes=[pltpu.SemaphoreType.DMA((7,)), pltpu.SemaphoreType.DMA((7,)), pltpu.SemaphoreType.DMA],
    )(v)


def _cast_into_whole(name, shard, chip):
    kind, whole = BIG[name]
    r, c = shard.shape
    if kind == "col":
        tr = 256
        o_spec = pl.BlockSpec((tr, c), lambda i, ch: (i, ch[0]))
    else:
        tr = _tile(r, (128, 352))
        o_spec = pl.BlockSpec((tr, c), lambda i, ch: (ch[0] * (r // tr) + i, 0))

    def body(ch_ref, x_ref, o_ref):
        del ch_ref
        o_ref[...] = x_ref[...].astype(o_ref.dtype)

    return _pallas(body, name="cast_" + name, prefetch=1, grid=(r // tr,),
                   in_specs=[pl.BlockSpec((tr, c), lambda i, ch: (i, 0))], out_specs=o_spec,
                   out_shape=_sds(whole, MXU_DTYPE), semantics=("parallel",))(chip, shard)


SEM = pl.BlockSpec(memory_space=pltpu.SEMAPHORE)
IN_HBM = pl.BlockSpec(memory_space=pltpu.HBM)
DATAFLOW = pltpu.SideEffectType.DATAFLOW_SIDE_EFFECTING


def _keep_in_hbm(a):
    return pltpu.with_memory_space_constraint(a, pltpu.HBM)


def _several(after):
    return list(after) if isinstance(after, (list, tuple)) else [after]


FLIPS = [(dx, dy, dc) for dx in (0, 1) for dy in (0, 1) for dc in (0, 1)][1:]


def _flipped(flip):
    x, y, c = lax.axis_index("x"), lax.axis_index("y"), lax.axis_index("c")
    return tuple(1 - v if f else v for v, f in zip((x, y, c), flip))


def _share_start(v, tag, after):
    r, n = v.shape
    ns = 2 * len(FLIPS)

    def body(*refs):
        v_ref, land_ref = refs[0], refs[1]
        sems = refs[2 + len(_several(after)):2 + len(_several(after)) + ns]
        x, y, c = lax.axis_index("x"), lax.axis_index("y"), lax.axis_index("c")
        mine = land_ref.at[pl.ds((4 * x + 2 * y + c) * r, r), :]
        for k, flip in enumerate(FLIPS):
            _remote(v_ref, mine, sems[2 * k], sems[2 * k + 1], _flipped(flip)).start()

    res = pl.pallas_call(
        body, name="share_" + tag + "_start",
        out_shape=(*[pltpu.SemaphoreType.DMA(())] * ns, pltpu.HBM(v.shape, v.dtype), pltpu.HBM((8 * r, n), v.dtype)),
        in_specs=[IN_HBM] * 2 + [pl.BlockSpec(memory_space=pl.ANY)] * len(_several(after)),
        out_specs=(*[SEM] * ns, IN_HBM, IN_HBM),
        input_output_aliases={0: ns, 1: ns + 1},
        compiler_params=pltpu.CompilerParams(has_side_effects=DATAFLOW),
    )(_keep_in_hbm(v), _keep_in_hbm(jnp.tile(v, (8, 1))), *_several(after))
    return list(res[:ns]), res[ns], res[ns + 1]


def _share_wait(started, after, tag):
    sems, v, land = started
    r = v.shape[0]
    ns = len(sems)

    def body(*refs):
        v_ref, land_ref = refs[0], refs[1]
        sem_refs = refs[2:2 + ns]
        for k, flip in enumerate(FLIPS):
            px, py, pc = _flipped(flip)
            theirs = land_ref.at[pl.ds((4 * px + 2 * py + pc) * r, r), :]
            cp = _remote(v_ref, theirs, sem_refs[2 * k], sem_refs[2 * k + 1], (px, py, pc))
            cp.wait_send()
            cp.wait_recv()

    res = pl.pallas_call(
        body, name="share_" + tag + "_wait",
        out_shape=(pltpu.HBM(v.shape, v.dtype), pltpu.HBM(land.shape, land.dtype)),
        in_specs=[IN_HBM] * 2 + [SEM] * ns + [pl.BlockSpec(memory_space=pl.ANY)] * len(_several(after)),
        out_specs=(IN_HBM, IN_HBM),
        input_output_aliases={0: 0, 1: 1},
        compiler_params=pltpu.CompilerParams(has_side_effects=DATAFLOW),
    )(v, land, *sems, *_several(after))
    return res[1]


def _gather_start(wholes, names, after, tag):
    nw = len(names)
    ns = 2 * 3 * nw

    def body(*refs):
        ins = refs[:nw]
        sems = refs[nw + 1:nw + 1 + ns]
        token = refs[2 * nw + ns + 1]
        _, _, c, chip, _, others = _place()
        for w, name in enumerate(names):
            mine = _whole_region(ins[w], name, chip, c)
            for t, (ox, oy) in enumerate(others):
                k = 2 * (3 * w + t)
                _remote(mine, mine, sems[k], sems[k + 1], (ox, oy, c)).start()
        token[...] = jnp.zeros_like(token)

    res = pl.pallas_call(
        body, name="gather_" + tag + "_start",
        out_shape=(*[pltpu.SemaphoreType.DMA(())] * ns, *[pltpu.HBM(a.shape, a.dtype) for a in wholes], _sds((8, LANES), F32)),
        in_specs=[IN_HBM] * nw + [pl.BlockSpec(memory_space=pl.ANY)],
        out_specs=(*[SEM] * ns, *[IN_HBM] * nw, pl.BlockSpec(memory_space=pltpu.VMEM)),
        input_output_aliases={i: ns + i for i in range(nw)},
        compiler_params=pltpu.CompilerParams(has_side_effects=DATAFLOW),
    )(*[_keep_in_hbm(a) for a in wholes], after)
    return list(res[:ns]), list(res[ns:ns + nw]), res[ns + nw]


def _gather_wait(sems, wholes, names, after, tag):
    nw = len(names)
    ns = len(sems)

    def body(*refs):
        ins = refs[:nw]
        sem_refs = refs[nw:nw + ns]
        _, _, c, chip, _, others = _place()
        for w, name in enumerate(names):
            mine = _whole_region(ins[w], name, chip, c)
            for t, (ox, oy) in enumerate(others):
                got = _whole_region(ins[w], name, 2 * ox + oy, c)
                k = 2 * (3 * w + t)
                cp = _remote(mine, got, sem_refs[k], sem_refs[k + 1], (ox, oy, c))
                cp.wait_send()
                cp.wait_recv()

    return pl.pallas_call(
        body, name="gather_" + tag + "_wait",
        out_shape=tuple(pltpu.HBM(a.shape, a.dtype) for a in wholes),
        in_specs=[IN_HBM] * nw + [SEM] * ns + [pl.BlockSpec(memory_space=pl.ANY)], out_specs=tuple([IN_HBM] * nw),
        input_output_aliases={i: i for i in range(nw)},
        compiler_params=pltpu.CompilerParams(has_side_effects=DATAFLOW),
    )(*wholes, *sems, after)


def _forward_halves(wholes, names, tag):
    nw = len(names)

    def body(*refs):
        outs = refs[nw:2 * nw]
        send_sems, recv_sems = refs[2 * nw:]
        _, _, c, _, sibling, others = _place()
        sends = []
        for w, name in enumerate(names):
            for t, (ox, oy) in enumerate(others):
                got = _whole_region(outs[w], name, 2 * ox + oy, c)
                cp = _remote(got, got, send_sems.at[w, t], recv_sems.at[w, t], sibling)
                cp.start()
                sends.append(cp)
        for w, name in enumerate(names):
            for t, (ox, oy) in enumerate(others):
                got = _whole_region(outs[w], name, 2 * ox + oy, 1 - c)
                _remote(got, got, send_sems.at[w, t], recv_sems.at[w, t], sibling).wait_recv()
        for cp in sends:
            cp.wait_send()

    return pl.pallas_call(
        body, name="gather_" + tag + "_forward",
        out_shape=[_sds(a.shape, a.dtype) for a in wholes],
        in_specs=[HBM] * nw, out_specs=[HBM] * nw,
        input_output_aliases={i: i for i in range(nw)},
        scratch_shapes=[pltpu.SemaphoreType.DMA((nw, 3)), pltpu.SemaphoreType.DMA((nw, 3))],
    )(*wholes)


def _forward_start(wholes, names, tag, after):
    nw = len(names)
    ns = 2 * 3 * nw

    def body(*refs):
        ins = refs[:nw]
        sems = refs[nw + 1:nw + 1 + ns]
        token = refs[2 * nw + ns + 1]
        _, _, c, _, sibling, others = _place()
        for w, name in enumerate(names):
            for t, (ox, oy) in enumerate(others):
                got = _whole_region(ins[w], name, 2 * ox + oy, c)
                k = 2 * (3 * w + t)
                _remote(got, got, sems[k], sems[k + 1], sibling).start()
        token[...] = jnp.zeros_like(token)

    res = pl.pallas_call(
        body, name="gather_" + tag + "_forward_start",
        out_shape=(*[pltpu.SemaphoreType.DMA(())] * ns, *[pltpu.HBM(a.shape, a.dtype) for a in wholes], _sds((8, LANES), F32)),
        in_specs=[IN_HBM] * nw + [pl.BlockSpec(memory_space=pl.ANY)],
        out_specs=(*[SEM] * ns, *[IN_HBM] * nw, pl.BlockSpec(memory_space=pltpu.VMEM)),
        input_output_aliases={i: ns + i for i in range(nw)},
        compiler_params=pltpu.CompilerParams(has_side_effects=DATAFLOW),
    )(*[_keep_in_hbm(a) for a in wholes], after)
    return list(res[:ns]), list(res[ns:ns + nw]), res[ns + nw]


def _forward_wait(sems, wholes, names, after, tag):
    nw = len(names)
    ns = len(sems)

    def body(*refs):
        ins = refs[:nw]
        sem_refs = refs[nw:nw + ns]
        _, _, c, _, sibling, others = _place()
        for w, name in enumerate(names):
            for t, (ox, oy) in enumerate(others):
                k = 2 * (3 * w + t)
                cp = _remote(_whole_region(ins[w], name, 2 * ox + oy, c), _whole_region(ins[w], name, 2 * ox + oy, 1 - c),
                             sem_refs[k], sem_refs[k + 1], sibling)
                cp.wait_send()
                cp.wait_recv()

    return pl.pallas_call(
        body, name="gather_" + tag + "_forward_wait",
        out_shape=tuple(pltpu.HBM(a.shape, a.dtype) for a in wholes),
        in_specs=[IN_HBM] * nw + [SEM] * ns + [pl.BlockSpec(memory_space=pl.ANY)], out_specs=tuple([IN_HBM] * nw),
        input_output_aliases={i: i for i in range(nw)},
        compiler_params=pltpu.CompilerParams(has_side_effects=DATAFLOW),
    )(*wholes, *sems, after)


def _compact_shape(name, dtype):
    kind, (r, c) = BIG[name]
    return _sds((r // 2, c), dtype)


def _swap_pairs(ins, outs, names, c):
    pairs = []
    for w, name in enumerate(names):
        kind, _ = BIG[name]
        half = _half_rows(name)
        if kind == "col":
            pairs.append((ins[w].at[pl.ds((1 - c) * half, half), :], outs[w]))
        else:
            pairs += [(ins[w].at[pl.ds(jj * 2 * half + (1 - c) * half, half), :], outs[w].at[pl.ds(jj * half, half), :])
                      for jj in range(N_CHIPS)]
    return pairs


def _n_swap_copies(names):
    return sum(1 if BIG[n][0] == "col" else N_CHIPS for n in names)


def _swap_start(grads, names, label):
    nw = len(names)
    ns = 2 * _n_swap_copies(names)

    def body(*refs):
        ins, lands = refs[:nw], refs[nw:2 * nw]
        sems = refs[2 * nw:2 * nw + ns]
        token = refs[4 * nw + ns]
        _, _, c, _, sibling, _ = _place()
        for k, (src, dst) in enumerate(_swap_pairs(ins, lands, names, c)):
            _remote(src, dst, sems[2 * k], sems[2 * k + 1], sibling).start()
        token[...] = jnp.zeros_like(token)

    lands = [_keep_in_hbm(lax.empty(_compact_shape(n, F32).shape, F32)) for n in names]
    res = pl.pallas_call(
        body, name=label,
        out_shape=(*[pltpu.SemaphoreType.DMA(())] * ns, *[pltpu.HBM(a.shape, a.dtype) for a in grads],
                   *[pltpu.HBM(a.shape, a.dtype) for a in lands], _sds((8, LANES), F32)),
        in_specs=[IN_HBM] * (2 * nw),
        out_specs=(*[SEM] * ns, *[IN_HBM] * (2 * nw), pl.BlockSpec(memory_space=pltpu.VMEM)),
        input_output_aliases={i: ns + i for i in range(2 * nw)},
        compiler_params=pltpu.CompilerParams(has_side_effects=DATAFLOW),
    )(*[_keep_in_hbm(a) for a in grads], *lands)
    return list(res[:ns]), list(res[ns:ns + nw]), list(res[ns + nw:ns + 2 * nw]), res[ns + 2 * nw]


def _swap_wait(sems, grads, lands, names, after, label):
    nw = len(names)
    ns = len(sems)

    def body(*refs):
        ins, land_refs = refs[:nw], refs[nw:2 * nw]
        sem_refs = refs[2 * nw:2 * nw + ns]
        _, _, c, _, sibling, _ = _place()
        for k, (src, dst) in enumerate(_swap_pairs(ins, land_refs, names, c)):
            cp = _remote(src, dst, sem_refs[2 * k], sem_refs[2 * k + 1], sibling)
            cp.wait_send()
            cp.wait_recv()

    res = pl.pallas_call(
        body, name=label,
        out_shape=tuple(pltpu.HBM(a.shape, a.dtype) for a in (*grads, *lands)),
        in_specs=[IN_HBM] * (2 * nw) + [SEM] * ns + [pl.BlockSpec(memory_space=pl.ANY)] * len(_several(after)),
        out_specs=tuple([IN_HBM] * (2 * nw)),
        input_output_aliases={i: i for i in range(2 * nw)},
        compiler_params=pltpu.CompilerParams(has_side_effects=DATAFLOW),
    )(*grads, *lands, *sems, *_several(after))
    return list(res[:nw]), list(res[nw:])


def _add_halves(name, grad, got, core):
    kind, (r, c) = BIG[name]
    half = _half_rows(name)
    if kind == "col":
        t = 128
        grid = (half // t,)
        g_spec = pl.BlockSpec((t, c), lambda i, cr: (cr[0] * (half // t) + i, 0))
        o_spec = pl.BlockSpec((t, c), lambda i, cr: (i, 0))
    else:
        t = half
        grid = (N_CHIPS,)
        g_spec = pl.BlockSpec((t, c), lambda i, cr: (2 * i + cr[0], 0))
        o_spec = pl.BlockSpec((t, c), lambda i, cr: (i, 0))

    def body(c_ref, g_ref, b_ref, o_ref):
        del c_ref
        o_ref[...] = (g_ref[...] + b_ref[...]).astype(o_ref.dtype)

    return pl.pallas_call(
        body, name="grad_add_" + name,
        grid_spec=pltpu.PrefetchScalarGridSpec(num_scalar_prefetch=1, grid=grid, in_specs=[g_spec, o_spec], out_specs=o_spec),
        out_shape=_compact_shape(name, BF16),
        compiler_params=pltpu.CompilerParams(dimension_semantics=("parallel",), vmem_limit_bytes=VMEM_LIMIT),
    )(core, grad, got)


def _piece(ref, name, chip):
    kind, _ = BIG[name]
    r, c = _shard_shape(name)
    if kind == "col":
        return ref.at[:, pl.ds(chip * c, c)]
    return ref.at[pl.ds(chip * (r // 2), r // 2), :]


def _landing_shape(name):
    r, c = _shard_shape(name)
    return (N_CHIPS - 1, r // 2, c)


def _exchange_start(parts, names, label):
    nw = len(names)
    ns = 2 * 3 * nw

    def body(*refs):
        ins, lands = refs[:nw], refs[nw:2 * nw]
        sems = refs[2 * nw:2 * nw + ns]
        token = refs[4 * nw + ns]
        _, _, c, _, _, others = _place()
        for w, name in enumerate(names):
            for t, (ox, oy) in enumerate(others):
                k = 2 * (3 * w + t)
                _remote(_piece(ins[w], name, 2 * ox + oy), lands[w].at[t], sems[k], sems[k + 1], (ox, oy, c)).start()
        token[...] = jnp.zeros_like(token)

    lands = [_keep_in_hbm(lax.empty(_landing_shape(n), BF16)) for n in names]
    res = pl.pallas_call(
        body, name=label,
        out_shape=(*[pltpu.SemaphoreType.DMA(())] * ns, *[pltpu.HBM(a.shape, a.dtype) for a in parts],
                   *[pltpu.HBM(a.shape, a.dtype) for a in lands], _sds((8, LANES), F32)),
        in_specs=[IN_HBM] * (2 * nw),
        out_specs=(*[SEM] * ns, *[IN_HBM] * (2 * nw), pl.BlockSpec(memory_space=pltpu.VMEM)),
        input_output_aliases={i: ns + i for i in range(2 * nw)},
        compiler_params=pltpu.CompilerParams(has_side_effects=DATAFLOW),
    )(*[_keep_in_hbm(a) for a in parts], *lands)
    return list(res[:ns]), list(res[ns:ns + nw]), list(res[ns + nw:ns + 2 * nw]), res[ns + 2 * nw]


def _exchange_wait(sems, parts, lands, names, after, label):
    nw = len(names)
    ns = len(sems)

    def body(*refs):
        ins, land_refs = refs[:nw], refs[nw:2 * nw]
        sem_refs = refs[2 * nw:2 * nw + ns]
        _, _, c, _, _, others = _place()
        for w, name in enumerate(names):
            for t, (ox, oy) in enumerate(others):
                k = 2 * (3 * w + t)
                cp = _remote(_piece(ins[w], name, 2 * ox + oy), land_refs[w].at[t], sem_refs[k], sem_refs[k + 1], (ox, oy, c))
                cp.wait_send()
                cp.wait_recv()

    res = pl.pallas_call(
        body, name=label,
        out_shape=tuple(pltpu.HBM(a.shape, a.dtype) for a in (*parts, *lands)),
        in_specs=[IN_HBM] * (2 * nw) + [SEM] * ns + [pl.BlockSpec(memory_space=pl.ANY)] * len(_several(after)),
        out_specs=tuple([IN_HBM] * (2 * nw)),
        input_output_aliases={i: i for i in range(2 * nw)},
        compiler_params=pltpu.CompilerParams(has_side_effects=DATAFLOW),
    )(*parts, *lands, *sems, *_several(after))
    return list(res[:nw]), list(res[nw:])


def _sum_chips(name, part, got, chip):
    kind, _ = BIG[name]
    _, r, c = got.shape
    t = _tile(r, (128, 352))
    if kind == "col":
        own = pl.BlockSpec((t, c), lambda i, ch: (i, ch[0]))
    else:
        own = pl.BlockSpec((t, c), lambda i, ch: (ch[0] * (r // t) + i, 0))

    def body(ch_ref, p_ref, g_ref, o_ref):
        del ch_ref
        acc = p_ref[...].astype(F32)
        for j in range(N_CHIPS - 1):
            acc = acc + g_ref[j].astype(F32)
        o_ref[...] = acc

    return _pallas(
        body, name="grad_sum_" + name, prefetch=1, grid=(r // t,),
        in_specs=[own, pl.BlockSpec((N_CHIPS - 1, t, c), lambda i, ch: (0, i, 0))],
        out_specs=pl.BlockSpec((t, c), lambda i, ch: (i, 0)),
        out_shape=_sds((r, c), F32), semantics=("parallel",),
    )(chip, part, got)


def _send_halves(sums, label, after):
    nw = len(sums)

    def body(*refs):
        ins, outs = refs[:nw], refs[nw + 1:2 * nw + 1]
        send_sems, recv_sems = refs[2 * nw + 1:]
        _, _, _, _, sibling, _ = _place()
        copies = [_remote(ins[w], outs[w], send_sems.at[w], recv_sems.at[w], sibling) for w in range(nw)]
        for cp in copies:
            cp.start()
        for cp in copies:
            cp.wait()

    return pl.pallas_call(
        body, name=label,
        out_shape=[_sds(a.shape, a.dtype) for a in sums],
        in_specs=[HBM] * (nw + 1), out_specs=[HBM] * nw,
        scratch_shapes=[pltpu.SemaphoreType.DMA((nw,)), pltpu.SemaphoreType.DMA((nw,))],
    )(*sums, after)


EARLY_GRADS = ("w_up", "w_down")
LAST_GRADS = ("w_in", "w_out")


def _reduce_finish(started, names, after, chip, tag):
    sems, parts, lands, _ = started
    parts, lands = _exchange_wait(sems, parts, lands, names, after, "grad_exchange_wait_" + tag)
    return [_sum_chips(n, parts[i], lands[i], chip) for i, n in enumerate(names)]


HI = lax.Precision.HIGHEST
MOD_COLS = 6 * D // N_CHIPS
COND_ROWS = 16


def _silu(v):
    return v * _sigmoid(v)


GATHER_ROWS = 8
FFW_COLS = 2 * DFF // N_CHIPS
CONV_COLS = DC // N_CHIPS
TAPS_PER_ROW = FFW_COLS // CONV_COLS
assert 4 + -(-CW // TAPS_PER_ROW) <= GATHER_ROWS


def _conv_tap_place(k):
    return 4 + k // TAPS_PER_ROW, (k % TAPS_PER_ROW) * CONV_COLS


def _pack_cond(c, ffn_w, conv_w):
    def body(c_ref, f_ref, w_ref, o_ref):
        o_ref[...] = jnp.zeros_like(o_ref)
        o_ref[0:1, 0:D] = c_ref[...]
        o_ref[1:4, :] = f_ref[...]
        for k in range(CW):
            row, lane = _conv_tap_place(k)
            o_ref[row:row + 1, lane:lane + CONV_COLS] = w_ref[k:k + 1, :]

    return _pallas(body, name="pack_cond", out_shape=_sds((GATHER_ROWS, FFW_COLS), F32))(c, ffn_w, conv_w)


def _unpack_cond(got, c_ctx):
    def body(g_ref, c_ref, cond_ref, f_ref, w_ref):
        cond_ref[...] = jnp.zeros_like(cond_ref)
        for d in range(8):
            cond_ref[d:d + 1, :] = g_ref[d * GATHER_ROWS:d * GATHER_ROWS + 1, 0:D]
        cond_ref[8:9, :] = c_ref[...]
        for j in range(N_CHIPS):
            r0 = 2 * j * GATHER_ROWS
            f_ref[:, j * FFW_COLS:(j + 1) * FFW_COLS] = g_ref[r0 + 1:r0 + 4, :]
            for k in range(CW):
                row, lane = _conv_tap_place(k)
                w_ref[k:k + 1, j * CONV_COLS:(j + 1) * CONV_COLS] = g_ref[r0 + row:r0 + row + 1, lane:lane + CONV_COLS]

    return _pallas(body, name="unpack_cond",
                   out_shape=[_sds((COND_ROWS, D), F32), _sds((3, 2 * DFF), F32), _sds((CW, DC), F32)])(got, c_ctx)


def _chip_cols(rows, width):
    return pl.BlockSpec((rows, width), lambda i, ch: (0, ch[0]))


def _whole(shape):
    return pl.BlockSpec(shape, lambda i, ch: (0,) * len(shape))


def _mod_shard(cond, w_mod, b_mod, chip):
    def body(ch_ref, c_ref, w_ref, b_ref, o_ref):
        del ch_ref
        o_ref[...] = jnp.dot(_silu(c_ref[...]), w_ref[...], preferred_element_type=F32, precision=HI) + b_ref[...]

    return _pallas(body, name="mod_fwd", prefetch=1, grid=(1,),
                   in_specs=[_whole((COND_ROWS, D)), _whole((D, MOD_COLS)), _chip_cols(1, MOD_COLS)],
                   out_specs=_whole((COND_ROWS, MOD_COLS)),
                   out_shape=_sds((COND_ROWS, MOD_COLS), F32))(chip, cond, w_mod, b_mod)


def _unpack_mod(mods, dev):
    def body(dev_ref, m_ref, me_ref, c_ref):
        rowi = lax.broadcasted_iota(jnp.int32, (COND_ROWS, MOD_COLS), 0)
        mine, ctx = [], []
        for j in range(N_CHIPS):
            blk = m_ref[2 * j * COND_ROWS:(2 * j + 1) * COND_ROWS, :]
            mine.append(jnp.sum(jnp.where(rowi == dev_ref[0], blk, 0.0), axis=0, keepdims=True))
            ctx.append(blk[8:9, :])
        mine = jnp.concatenate(mine, axis=1)
        ctx = jnp.concatenate(ctx, axis=1)
        for k in range(6):
            me_ref[k:k + 1, :] = mine[:, k * D:(k + 1) * D]
        for k in range(2):
            c_ref[k:k + 1, :] = ctx[:, k * D:(k + 1) * D]

    return _pallas(body, name="unpack_mod", prefetch=1, grid=(1,),
                   in_specs=[_whole(mods.shape)], out_specs=[_whole((6, D)), _whole((2, D))],
                   out_shape=[_sds((6, D), F32), _sds((2, D), F32)])(dev, mods)


MOD_TILE = 512


def _mod_weight_update(cond, dmod_all, w, m, v, chip):
    nt = MOD_COLS // MOD_TILE

    def body(ch_ref, c_ref, d_ref, w_ref, m_ref, v_ref, g_ref, dl_ref, nm_ref, nv_ref):
        del ch_ref
        g = lax.dot_general(_silu(c_ref[...]), d_ref[...], _TN, preferred_element_type=F32, precision=HI)
        g_ref[...] = g
        dl_ref[...], nm_ref[...], nv_ref[...] = _adam_math(w_ref[...], g, m_ref[...], v_ref[...])

    blk = pl.BlockSpec((D, MOD_TILE), lambda j, ch: (0, j))
    return _pallas(body, name="mod_weight_update", prefetch=1, grid=(nt,),
                   in_specs=[_whole((COND_ROWS, D)), pl.BlockSpec((COND_ROWS, MOD_TILE), lambda j, ch: (0, ch[0] * nt + j)),
                             blk, blk, blk],
                   out_specs=[blk] * 4, out_shape=[_sds((D, MOD_COLS), F32)] * 4,
                   semantics=("parallel",))(chip, cond, dmod_all, w, m, v)


def _cond_grad_partial(dmod_all, w_mod, chip):
    def body(ch_ref, d_ref, w_ref, o_ref):
        del ch_ref
        o_ref[...] = lax.dot_general(d_ref[...], w_ref[...], (((1,), (1,)), ((), ())), preferred_element_type=F32, precision=HI)

    return _pallas(body, name="cond_grad_partial", prefetch=1, grid=(1,),
                   in_specs=[pl.BlockSpec((8, MOD_COLS), lambda i, ch: (1, ch[0])), _whole((D, MOD_COLS))],
                   out_specs=_whole((8, D)), out_shape=_sds((8, D), F32))(chip, dmod_all, w_mod)


def _adam_math(w, g, m, v):
    nm = ADAM_B1 * m + (1.0 - ADAM_B1) * g
    nv = ADAM_B2 * v + (1.0 - ADAM_B2) * (g * g)
    c1 = 1.0 - ADAM_B1 ** ADAM_STEP
    c2 = 1.0 - ADAM_B2 ** ADAM_STEP
    return -ADAM_LR * ((nm / c1) / (jnp.sqrt(nv / c2) + ADAM_EPS) + ADAM_WD * w), nm, nv


def _cond_update(parts, c_ctx, m, v):
    def body(p_ref, c_ref, m_ref, v_ref, g_ref, d_ref, nm_ref, nv_ref):
        tot = p_ref[0:1, :]
        for j in range(1, N_CHIPS):
            tot = tot + p_ref[16 * j:16 * j + 1, :]
        cv = c_ref[...]
        sg = _sigmoid(cv)
        g = tot * (sg * (1.0 + cv * (1.0 - sg)))
        g_ref[...] = g
        d_ref[...], nm_ref[...], nv_ref[...] = _adam_math(cv, g, m_ref[...], v_ref[...])

    return _pallas(body, name="cond_update", out_shape=[_sds((1, D), F32)] * 4)(parts, c_ctx, m, v)


def _adamw_cols(w, g_all, m, v, chip, name):
    r, c = w.shape

    def body(ch_ref, w_ref, g_ref, m_ref, v_ref, go_ref, d_ref, nm_ref, nv_ref):
        del ch_ref
        g = g_ref[...]
        go_ref[...] = g
        d_ref[...], nm_ref[...], nv_ref[...] = _adam_math(w_ref[...], g, m_ref[...], v_ref[...])

    return _pallas(body, name=name, prefetch=1, grid=(1,),
                   in_specs=[_whole((r, c)), _chip_cols(r, c), _whole((r, c)), _whole((r, c))],
                   out_specs=[_whole((r, c))] * 4, out_shape=[_sds((r, c), F32)] * 4)(chip, w, g_all, m, v)


def _adamw_halves(name, w, own, other, m, v, core, after):
    r, c = w.shape
    half = r // 2
    t = _tile(half, (128, 352))
    nh = half // t

    def pick(mine):
        def index(i, cr):
            first = cr[0] if mine else 1 - cr[0]
            return (jnp.clip(i - first * nh, 0, nh - 1), 0)
        return pl.BlockSpec((t, c), index)

    def body(c_ref, w_ref, own_ref, oth_ref, m_ref, v_ref, after_ref, g_ref, d_ref, nm_ref, nv_ref):
        del after_ref
        g = jnp.where(pl.program_id(0) // nh == c_ref[0], own_ref[...], oth_ref[...])
        g_ref[...] = g
        d_ref[...], nm_ref[...], nv_ref[...] = _adam_math(w_ref[...], g, m_ref[...], v_ref[...])

    blk = pl.BlockSpec((t, c), lambda i, cr: (i, 0))
    return _pallas(body, name="adamw_" + name, prefetch=1, grid=(2 * nh,),
                   in_specs=[blk, pick(True), pick(False), blk, blk, pl.BlockSpec(memory_space=pl.ANY)], out_specs=[blk] * 4,
                   out_shape=[_sds((r, c), F32)] * 4, semantics=("parallel",))(core, w, own, other, m, v, after)


WEIGHTS = ("c_ctx", "w_mod", "b_mod", "g_norm1", "w_in", "rpb", "conv_w", "conv_b", "ln_g", "ln_b", "w_out", "g_norm2",
           "w_up", "ffn_conv_w", "ffn_conv_b", "w_down", "g_final")
PACK = (("dmod", 6 * D), ("dmod_c", 2 * D), ("g_norm1", D), ("g_norm1_ctx", D), ("g_norm2", D), ("g_final", D),
        ("conv_b", DC), ("ln_g", DC), ("ln_b", DC), ("ffn_conv_b", 2 * DFF), ("ffn_conv_w", 3 * 2 * DFF),
        ("conv_w", CW * DC), ("rpb_rev", NH * 16 * LANES), ("loss", LANES))
PACK_OFF = {}
_o = 0
for _n, _w in PACK:
    PACK_OFF[_n] = (_o, _w)
    _o += _w
PACK_N = -(-_o // (8 * LANES)) * (8 * LANES)
VECTORS = {"b_mod": (6 * D, ("dmod", "dmod_c")), "g_norm1": (D, ("g_norm1", "g_norm1_ctx")), "conv_b": (DC, ("conv_b",)),
           "ln_g": (DC, ("ln_g",)), "ln_b": (DC, ("ln_b",)), "g_norm2": (D, ("g_norm2",)),
           "ffn_conv_b": (2 * DFF, ("ffn_conv_b",)), "g_final": (D, ("g_final",))}
RPB_COLS = 4 * NA_ROWS - 1


def _pack_small(parts, after):
    arrs, places = [], []
    for name, _ in PACK:
        off, width = PACK_OFF[name]
        group = parts[name]
        rows = group[0].shape[0]
        row_w = sum(a.shape[1] for a in group)
        assert rows * row_w == width, (name, rows, row_w, width)
        col = 0
        for a in group:
            arrs.append(a)
            places.append([off + k * row_w + col for k in range(rows)])
            col += a.shape[1]

    def body(*refs):
        o_ref = refs[-1]
        o_ref[:, _o:PACK_N] = jnp.zeros((1, PACK_N - _o), F32)
        for ref, offs in zip(refs, places):
            n = ref.shape[1]
            for k, off in enumerate(offs):
                o_ref[:, off:off + n] = ref[k:k + 1, :]

    vmem = pl.BlockSpec(memory_space=pltpu.VMEM)
    return _pallas(body, name="pack_small_grads", out_shape=_sds((1, PACK_N), F32),
                   in_specs=[vmem] * len(arrs) + [pl.BlockSpec(memory_space=pl.ANY)] * len(_several(after)),
                   out_specs=vmem)(*arrs, *_several(after))


def _small_update(packs, w, m, v):
    names = list(VECTORS)

    def body(*refs):
        it = iter(refs)
        p_ref = next(it)
        wmv = {n: (next(it), next(it), next(it)) for n in names}
        outs = {n: (next(it), next(it), next(it), next(it)) for n in names}
        dmod_ref, cw_ref, fw_ref, rpb_ref, loss_ref = next(it), next(it), next(it), next(it), next(it)

        def total(name):
            off, width = PACK_OFF[name]
            acc = p_ref[0:1, off:off + width]
            for d in range(1, 8):
                acc = acc + p_ref[d:d + 1, off:off + width]
            return acc

        for n in names:
            width, segs = VECTORS[n]
            g = total(segs[0])
            if len(segs) > 1:
                extra = total(segs[1])
                ew = extra.shape[1]
                g = g + extra if ew == width else jnp.concatenate([g[:, :ew] + extra, g[:, ew:]], axis=1)
            w_ref, m_ref, v_ref = wmv[n]
            g_ref, d_ref, nm_ref, nv_ref = outs[n]
            g_ref[...] = g
            d_ref[...], nm_ref[...], nv_ref[...] = _adam_math(w_ref[...], g, m_ref[...], v_ref[...])

        o_dmod = PACK_OFF["dmod"][0]
        dmod_ref[...] = jnp.zeros_like(dmod_ref)
        dmod_ref[0:8, :] = p_ref[:, o_dmod:o_dmod + 6 * D]
        dmod_ref[8:9, 0:2 * D] = total("dmod_c")
        for ref, name, rows in ((cw_ref, "conv_w", CW), (fw_ref, "ffn_conv_w", 3), (rpb_ref, "rpb_rev", NH * 16)):
            flat = total(name)
            n = ref.shape[1]
            for k in range(rows):
                ref[k:k + 1, :] = flat[:, k * n:(k + 1) * n]
        loss_ref[...] = total("loss")

    ins = [packs] + [a[n] for n in names for a in (w, m, v)]
    out_shape = [_sds((1, VECTORS[n][0]), F32) for n in names for _ in range(4)]
    out_shape += [_sds((COND_ROWS, 6 * D), F32), _sds((CW, DC), F32), _sds((3, 2 * DFF), F32), _sds((NH * 16, LANES), F32),
                  _sds((1, LANES), F32)]
    res = _pallas(body, name="small_update", out_shape=out_shape)(*ins)
    per = {n: tuple(res[4 * i:4 * i + 4]) for i, n in enumerate(names)}
    return (per, *res[4 * len(names):])


def _rpb_update(rev, w, m, v):
    def body(r_ref, w_ref, m_ref, v_ref, g_ref, d_ref, nm_ref, nv_ref):
        li = lax.broadcasted_iota(jnp.int32, (LANES, LANES), 0)
        co = lax.broadcasted_iota(jnp.int32, (LANES, LANES), 1)
        lane_of_co0 = GW - 1 + RPB_COLS // 2
        unflip = jnp.where((li == lane_of_co0 - co) & (co < RPB_COLS), 1.0, 0.0).astype(F32)
        g_all = jnp.dot(r_ref[...], unflip, preferred_element_type=F32, precision=HI)
        nr = 2 * NA_ROWS - 1
        for h in range(NH):
            g = g_all[h * 16:h * 16 + nr, 0:RPB_COLS]
            g_ref[0, h] = g
            d_ref[0, h], nm_ref[0, h], nv_ref[0, h] = _adam_math(w_ref[0, h], g, m_ref[0, h], v_ref[0, h])

    return _pallas(body, name="rpb_update", out_shape=[_sds(w.shape, F32)] * 4)(rev, w, m, v)


def kernel(x, c, ctx, c_ctx, w_mod, b_mod, g_norm1, w_in, rpb, conv_w, conv_b, ln_g, ln_b, w_out, g_norm2, w_up, ffn_conv_w, ffn_conv_b, w_down, g_final, loss_target, m_c_ctx, m_w_mod, m_b_mod, m_g_norm1, m_w_in, m_rpb, m_conv_w, m_conv_b, m_ln_g, m_ln_b, m_w_out, m_g_norm2, m_w_up, m_ffn_conv_w, m_ffn_conv_b, m_w_down, m_g_final, v_c_ctx, v_w_mod, v_b_mod, v_g_norm1, v_w_in, v_rpb, v_conv_w, v_conv_b, v_ln_g, v_ln_b, v_w_out, v_g_norm2, v_w_up, v_ffn_conv_w, v_ffn_conv_b, v_w_down, v_g_final):
    w = dict(c_ctx=c_ctx, w_mod=w_mod, b_mod=b_mod, g_norm1=g_norm1, w_in=w_in, rpb=rpb, conv_w=conv_w, conv_b=conv_b,
             ln_g=ln_g, ln_b=ln_b, w_out=w_out, g_norm2=g_norm2, w_up=w_up, ffn_conv_w=ffn_conv_w, ffn_conv_b=ffn_conv_b,
             w_down=w_down, g_final=g_final)
    mom = dict(c_ctx=m_c_ctx, w_mod=m_w_mod, b_mod=m_b_mod, g_norm1=m_g_norm1, w_in=m_w_in, rpb=m_rpb, conv_w=m_conv_w,
               conv_b=m_conv_b, ln_g=m_ln_g, ln_b=m_ln_b, w_out=m_w_out, g_norm2=m_g_norm2, w_up=m_w_up,
               ffn_conv_w=m_ffn_conv_w, ffn_conv_b=m_ffn_conv_b, w_down=m_w_down, g_final=m_g_final)
    var = dict(c_ctx=v_c_ctx, w_mod=v_w_mod, b_mod=v_b_mod, g_norm1=v_g_norm1, w_in=v_w_in, rpb=v_rpb, conv_w=v_conv_w,
               conv_b=v_conv_b, ln_g=v_ln_g, ln_b=v_ln_b, w_out=v_w_out, g_norm2=v_g_norm2, w_up=v_w_up,
               ffn_conv_w=v_ffn_conv_w, ffn_conv_b=v_ffn_conv_b, w_down=v_w_down, g_final=v_g_final)
    xi, yi, ci = lax.axis_index("x"), lax.axis_index("y"), lax.axis_index("c")
    dev = (4 * xi + 2 * yi + ci).astype(jnp.int32).reshape(1)
    chip = (2 * xi + yi).astype(jnp.int32).reshape(1)
    core = ci.astype(jnp.int32).reshape(1)
    c_ctx2 = c_ctx.reshape(1, D)
    g_final2 = g_final.reshape(1, D)
    mom["g_final"], var["g_final"] = m_g_final.reshape(1, D), v_g_final.reshape(1, D)

    got = _gather_small(_pack_cond(c, ffn_conv_w[0], conv_w[0]), "gather_cond")
    cond, ffn_w_all, conv_w_all = _unpack_cond(got, c_ctx2)

    mods = _gather_small(_mod_shard(cond, w_mod[0], b_mod, chip), "gather_mod")
    mod_me, mod_c = _unpack_mod(mods, dev)

    shards = {n: _cast_into_whole(n, w[n][0], chip) for n in BIG_NAMES}
    sems_in, first, token_in = _gather_start([shards["w_in"]], ("w_in",), mod_me, "w_in")
    sems, late, token = _gather_start([shards[n] for n in LATE_NAMES], LATE_NAMES, token_in, "late")
    mod_me = mod_me + token[0:1, 0:1]

    def w_in_all(after):
        arrived = _gather_wait(sems_in, first, ("w_in",), after, "w_in")
        return _forward_halves(list(arrived), ("w_in",), "w_in")[0]

    def late_weights(after):
        arrived = list(_gather_wait(sems, late, LATE_NAMES, after, "late"))
        (w_out_all,) = _forward_halves(arrived[:1], LATE_NAMES[:1], "w_out")
        fsems, passing, _ = _forward_start(arrived[1:], LATE_NAMES[1:], "ffn", after=w_out_all)
        return w_out_all, lambda after2: _forward_wait(fsems, passing, LATE_NAMES[1:], after2, "ffn")

    rpb_rev = jnp.pad(rpb[0][:, :, ::-1], ((0, 0), (0, 1), (48, LANES - 48 - RPB_COLS))).reshape(NH * 16, LANES)
    vec = dict(g_norm1=g_norm1, g_norm2=g_norm2, g_final=g_final2, conv_w=conv_w_all, conv_b=conv_b, ln_g=ln_g, ln_b=ln_b,
               ffn_conv_w=ffn_w_all, ffn_conv_b=ffn_conv_b)
    started = []

    def begin_early(d_up, d_down):
        started.append(_swap_start([d_up, d_down], EARLY_GRADS, "grad_swap_start_early"))

    def carry_on_early(after):
        sems_, grads_, lands_, _ = started.pop()
        grads_, lands_ = _swap_wait(sems_, grads_, lands_, EARLY_GRADS, after, "grad_swap_wait_early")
        parts_ = [_add_halves(n, grads_[i], lands_[i], core) for i, n in enumerate(EARLY_GRADS)]
        started.append(_exchange_start(parts_, EARLY_GRADS, "grad_exchange_start_early"))
        return started[0][3][0:1, 0:1]

    loss_p, grad_x, d_in, d_out, d_up, d_down, small = _local_step(
        x[0], ctx[0], loss_target[0], mod_me, mod_c, vec, w_in_all, late_weights, rpb_rev, (begin_early, carry_on_early))

    out = {}
    sems_, grads_, lands_, _ = _swap_start([d_in, d_out], LAST_GRADS, "grad_swap_start_last")
    early_own = _reduce_finish(started[0], EARLY_GRADS, grad_x, chip, "early")
    parts = dict(dmod=small["dmod"], dmod_c=small["dmod_c"], g_norm1=[small["g_norm1"][0]], g_norm1_ctx=[small["g_norm1"][1]],
                 g_norm2=[small["g_norm2"]], g_final=[small["g_final"]], conv_b=[small["conv_b"]], ln_g=[small["ln_g"]],
                 ln_b=[small["ln_b"]], ffn_conv_b=small["ffn_conv_b"], ffn_conv_w=small["ffn_conv_w"],
                 conv_w=[small["conv_w"]], rpb_rev=[small["rpb_rev"]], loss=[loss_p])
    pack = _pack_small(parts, after=early_own).reshape(8, PACK_N // 8)
    sharing = _share_start(pack, "small_grads", after=[])
    grads_, lands_ = _swap_wait(sems_, grads_, lands_, LAST_GRADS, sharing[2], "grad_swap_wait_last")
    parts_ = [_add_halves(n, grads_[i], lands_[i], core) for i, n in enumerate(LAST_GRADS)]
    last_started = _exchange_start(parts_, LAST_GRADS, "grad_exchange_start_last")
    early_other = _send_halves(early_own, "grad_send_early", after=last_started[3])
    for i, n in enumerate(EARLY_GRADS):
        out[n] = _adamw_halves(n, w[n][0], early_own[i], early_other[i], mom[n][0], var[n][0], core, early_other[i])

    packs = _share_wait(sharing, [out[n][1] for n in EARLY_GRADS], "small_grads").reshape(8, PACK_N)
    w2 = dict(w, g_final=g_final2)
    per, dmod_all, g_conv_w_all, g_ffn_w_all, g_rpb_rev, loss_row = _small_update(packs, w2, mom, var)
    out.update(per)
    out["w_mod"] = _mod_weight_update(cond, dmod_all, w_mod[0], m_w_mod[0], v_w_mod[0], chip)

    sharing_c = _share_start(_cond_grad_partial(dmod_all, w_mod[0], chip), "cond_grad", after=out["w_mod"][1])
    last_own = _reduce_finish(last_started, LAST_GRADS, sharing_c[2], chip, "last")
    last_other = _send_halves(last_own, "grad_send_last", after=last_own[0])
    for i, n in enumerate(LAST_GRADS):
        out[n] = _adamw_halves(n, w[n][0], last_own[i], last_other[i], mom[n][0], var[n][0], core, last_other[i])
    out["c_ctx"] = _cond_update(_share_wait(sharing_c, [out[n][1] for n in LAST_GRADS], "cond_grad"),
                                c_ctx2, m_c_ctx.reshape(1, D), v_c_ctx.reshape(1, D))
    out["conv_w"] = _adamw_cols(conv_w[0], g_conv_w_all, m_conv_w[0], v_conv_w[0], chip, "adamw_conv_w")
    out["ffn_conv_w"] = _adamw_cols(ffn_conv_w[0], g_ffn_w_all, m_ffn_conv_w[0], v_ffn_conv_w[0], chip, "adamw_ffn_conv_w")
    out["rpb"] = _rpb_update(g_rpb_rev, rpb, m_rpb, v_rpb)

    res = [[out[n][k].reshape(w[n].shape) for n in WEIGHTS] for k in range(4)]
    return (loss_row[0, 0], grad_x[None], *res[0], *res[1], *res[2], *res[3])
```

```python
import jax
import jax.numpy as jnp
from jax import lax
from jax.experimental import pallas as pl
from jax.experimental.pallas import tpu as pltpu

F32 = jnp.float32
BF16 = jnp.bfloat16
MXU_DTYPE = jnp.bfloat16

D = 1024
CTX = 256
GW = 64
DA = 512
NH = 8
HD = 64
DC = 512
CW = 31
DFF = 2816
NIN = 3 * DA + 2 * DC
EPS = 1e-6
SCALE = HD ** -0.5
NEG = -1e30
NA_ROWS = 8
PAIR_ROWS = NA_ROWS + 1
TAB_BLOCKS = 17
LANES = 128
VMEM_LIMIT = 56 * 1024 * 1024

ADAM_LR = 0.001
ADAM_B1 = 0.9
ADAM_B2 = 0.999
ADAM_EPS = 1e-08
ADAM_WD = 0.01
ADAM_STEP = 10

MESH = pl.DeviceIdType.MESH


def _pallas(body, *, name, semantics=None, vmem=VMEM_LIMIT, prefetch=0, **kw):
    params = dict(vmem_limit_bytes=vmem)
    if semantics is not None:
        params["dimension_semantics"] = semantics
    if prefetch:
        kw["grid_spec"] = pltpu.PrefetchScalarGridSpec(
            num_scalar_prefetch=prefetch, grid=kw.pop("grid"), in_specs=kw.pop("in_specs"), out_specs=kw.pop("out_specs"),
            scratch_shapes=kw.pop("scratch_shapes", ()))
    return pl.pallas_call(body, name=name, compiler_params=pltpu.CompilerParams(**params), **kw)


def _sds(shape, dtype):
    return jax.ShapeDtypeStruct(shape, dtype)


def _vec_spec(n):
    return pl.BlockSpec((1, n), lambda *_: (0, 0))


def _colsum8(x):
    t, n = x.shape
    return jnp.sum(x.reshape(t // 8, 8, n), axis=0)


def _sigmoid(x):
    return 0.5 * jnp.tanh(0.5 * x) + 0.5


def _mm(a, b, *, mode, m, n, k, tm, tn, tk, out_dtype, name, a_off=(0, 0), b_off=(0, 0),
        out_total=None, o_off=(0, 0), into=None):
    a_list = list(a) if isinstance(a, (list, tuple)) else [a]
    b_list = list(b) if isinstance(b, (list, tuple)) else [b]
    assert m % tm == 0 and n % tn == 0 and k % tk == 0, (name, m, n, k, tm, tn, tk)
    gi, gj, nk = m // tm, n // tn, k // tk
    dims = {"nn": (((1,), (0,)), ((), ())), "nt": (((1,), (1,)), ((), ())), "tn": (((0,), (0,)), ((), ()))}[mode]

    if len(a_list) > 1:
        assert mode != "tn" and nk == 1 and sum(x.shape[1] for x in a_list) == k
        a_specs = [pl.BlockSpec((tm, x.shape[1]), lambda i, j, kk: (i, 0)) for x in a_list]
    elif mode == "tn":
        a_specs = [pl.BlockSpec((tk, tm), lambda i, j, kk: (kk + a_off[0], i + a_off[1]))]
    else:
        a_specs = [pl.BlockSpec((tm, tk), lambda i, j, kk: (i + a_off[0], kk + a_off[1]))]
    if len(b_list) > 1:
        assert mode == "tn" and gj == 1 and sum(x.shape[1] for x in b_list) == n
        b_specs = [pl.BlockSpec((tk, x.shape[1]), lambda i, j, kk: (kk, 0)) for x in b_list]
    elif mode == "nt":
        b_specs = [pl.BlockSpec((tn, tk), lambda i, j, kk: (j + b_off[0], kk + b_off[1]))]
    else:
        b_specs = [pl.BlockSpec((tk, tn), lambda i, j, kk: (kk + b_off[0], j + b_off[1]))]

    na, nb = len(a_list), len(b_list)
    in_place = nk > 1 and out_dtype == F32
    n_in = na + nb + (into is not None)

    def body(*refs):
        a_refs, b_refs, o_ref = refs[:na], refs[na:na + nb], refs[n_in]
        acc = o_ref if in_place else (refs[n_in + 1] if nk > 1 else None)
        kk = pl.program_id(2)

        def whole(piece_refs):
            vals = [r[...].astype(MXU_DTYPE) for r in piece_refs]
            return vals[0] if len(vals) == 1 else jnp.concatenate(vals, axis=1)

        p = lax.dot_general(whole(a_refs), whole(b_refs), dims, preferred_element_type=F32)
        if nk == 1:
            o_ref[...] = p.astype(out_dtype)
            return

        @pl.when(kk == 0)
        def _():
            acc[...] = p

        @pl.when(kk > 0)
        def _():
            acc[...] += p

        if not in_place:
            @pl.when(kk == nk - 1)
            def _():
                o_ref[...] = acc[...].astype(out_dtype)

    ins = [*a_list, *b_list]
    in_specs = a_specs + b_specs
    extra = {}
    if into is not None:
        extra["input_output_aliases"] = {len(ins): 0}
        ins.append(into)
        in_specs.append(pl.BlockSpec(memory_space=pl.ANY))
    return _pallas(
        body, name=name, grid=(gi, gj, nk), in_specs=in_specs,
        out_specs=pl.BlockSpec((tm, tn), lambda i, j, kk: (i + o_off[0], j + o_off[1])),
        out_shape=_sds(out_total or (m, n), out_dtype),
        scratch_shapes=[pltpu.VMEM((tm, tn), F32)] if nk > 1 and not in_place else [],
        semantics=("parallel", "parallel", "arbitrary"), **extra,
    )(*ins)


ROW_TILE = 256


def _rmsmod_fwd(x, ctx, g, sc, sh, csc, csh):
    s = x.shape[0]
    nt = s // ROW_TILE
    assert ctx.shape[0] == ROW_TILE

    def body(x_ref, c_ref, g_ref, sc_ref, sh_ref, csc_ref, csh_ref, o_ref):
        is_ctx = pl.program_id(0) == nt
        xv = jnp.where(is_ctx, c_ref[...], x_ref[...])
        scv = jnp.where(is_ctx, csc_ref[...], sc_ref[...])
        shv = jnp.where(is_ctx, csh_ref[...], sh_ref[...])
        r = lax.rsqrt(jnp.mean(xv * xv, axis=-1, keepdims=True) + EPS)
        y = xv * r * g_ref[...]
        o_ref[...] = (y * (1.0 + scv) + shv).astype(o_ref.dtype)

    return _pallas(
        body, name="rmsmod1_fwd", grid=(nt + 1,),
        in_specs=[pl.BlockSpec((ROW_TILE, D), lambda i: (jnp.minimum(i, nt - 1), 0)),
                  pl.BlockSpec((ROW_TILE, D), lambda i: (0, 0))] + [_vec_spec(D)] * 5,
        out_specs=pl.BlockSpec((ROW_TILE, D), lambda i: (i, 0)),
        out_shape=_sds((s + CTX, D), MXU_DTYPE),
        semantics=("arbitrary",),
    )(x, ctx, g, sc, sh, csc, csh)


def _resid_rmsmod_fwd(x, y, gt, g, sc, sh):
    s = x.shape[0]

    def body(x_ref, y_ref, gt_ref, g_ref, sc_ref, sh_ref, x1_ref, h_ref):
        x1 = x_ref[...] + gt_ref[...] * y_ref[...]
        x1_ref[...] = x1
        r = lax.rsqrt(jnp.mean(x1 * x1, axis=-1, keepdims=True) + EPS)
        h_ref[...] = ((x1 * r * g_ref[...]) * (1.0 + sc_ref[...]) + sh_ref[...]).astype(h_ref.dtype)

    row = pl.BlockSpec((ROW_TILE, D), lambda i: (i, 0))
    return _pallas(
        body, name="resid_rmsmod2_fwd", grid=(s // ROW_TILE,),
        in_specs=[row, row] + [_vec_spec(D)] * 4,
        out_specs=[row, row],
        out_shape=[_sds((s, D), F32), _sds((s, D), MXU_DTYPE)],
        semantics=("parallel",),
    )(x, y, gt, g, sc, sh)


def _final_fwd_bwd(x1, z, gt2, gf, tgt):
    s = x1.shape[0]
    nt = s // ROW_TILE

    def body(x1_ref, z_ref, gt_ref, gf_ref, t_ref, dx2_ref, dz_ref, loss_ref, dgt_ref, dgf_ref, a_loss, a_gt, a_gf):
        i = pl.program_id(0)

        @pl.when(i == 0)
        def _():
            a_loss[...] = jnp.zeros_like(a_loss)
            a_gt[...] = jnp.zeros_like(a_gt)
            a_gf[...] = jnp.zeros_like(a_gf)

        zv = z_ref[...]
        gt = gt_ref[...]
        gf_ = gf_ref[...]
        x2 = x1_ref[...] + gt * zv
        r = lax.rsqrt(jnp.mean(x2 * x2, axis=-1, keepdims=True) + EPS)
        xn = x2 * r
        e = xn * gf_ - t_ref[...]
        a_loss[...] += _colsum8(e * e)
        dyo = e * (1.0 / D)
        a_gf[...] += _colsum8(dyo * xn)
        gdy = gf_ * dyo
        dx2 = r * gdy - xn * (r * r) * jnp.mean(x2 * gdy, axis=-1, keepdims=True)
        dx2_ref[...] = dx2
        dz_ref[...] = (gt * dx2).astype(dz_ref.dtype)
        a_gt[...] += _colsum8(dx2 * zv)

        @pl.when(i == nt - 1)
        def _():
            tot = jnp.sum(jnp.sum(a_loss[...], axis=0, keepdims=True), axis=1, keepdims=True) * (0.5 / D)
            loss_ref[...] = jnp.broadcast_to(tot, loss_ref.shape)
            dgt_ref[...] = jnp.sum(a_gt[...], axis=0, keepdims=True)
            dgf_ref[...] = jnp.sum(a_gf[...], axis=0, keepdims=True)

    row = pl.BlockSpec((ROW_TILE, D), lambda i: (i, 0))
    return _pallas(
        body, name="final_norm_loss", grid=(nt,),
        in_specs=[row, row, _vec_spec(D), _vec_spec(D), row],
        out_specs=[row, row, _vec_spec(LANES), _vec_spec(D), _vec_spec(D)],
        out_shape=[_sds((s, D), F32), _sds((s, D), MXU_DTYPE), _sds((1, LANES), F32), _sds((1, D), F32), _sds((1, D), F32)],
        scratch_shapes=[pltpu.VMEM((8, D), F32)] * 3,
        semantics=("arbitrary",),
    )(x1, z, gt2, gf, tgt)


def _rmsmod_bwd(xin, dh, g, sc, *, name, dh_row_off=0, add=None, resid=None):
    s = xin.shape[0]
    nt = s // ROW_TILE
    want_dx = add is not None
    assert resid is None or want_dx

    def body(*refs):
        it = iter(refs)
        x_ref, dh_ref, g_ref, sc_ref = next(it), next(it), next(it), next(it)
        add_ref = next(it) if want_dx else None
        gt_ref, y_ref = (next(it), next(it)) if resid is not None else (None, None)
        dsh_ref, dsc_ref, dg_ref = next(it), next(it), next(it)
        dx_ref = next(it) if want_dx else None
        dy_ref, dgt_ref = (next(it), next(it)) if resid is not None else (None, None)
        a_sh, a_sc, a_g = next(it), next(it), next(it)
        a_gt = next(it) if resid is not None else None
        i = pl.program_id(0)

        @pl.when(i == 0)
        def _():
            a_sh[...] = jnp.zeros_like(a_sh)
            a_sc[...] = jnp.zeros_like(a_sc)
            a_g[...] = jnp.zeros_like(a_g)
            if a_gt is not None:
                a_gt[...] = jnp.zeros_like(a_gt)

        xv = x_ref[...]
        dhv = dh_ref[...]
        gv = g_ref[...]
        r = lax.rsqrt(jnp.mean(xv * xv, axis=-1, keepdims=True) + EPS)
        xn = xv * r
        a_sh[...] += _colsum8(dhv)
        a_sc[...] += _colsum8(dhv * (xn * gv))
        dn = dhv * (1.0 + sc_ref[...])
        a_g[...] += _colsum8(dn * xn)
        if want_dx:
            gdn = gv * dn
            dx = add_ref[...] + r * gdn - xn * (r * r) * jnp.mean(xv * gdn, axis=-1, keepdims=True)
            dx_ref[...] = dx
            if resid is not None:
                dy_ref[...] = (gt_ref[...] * dx).astype(dy_ref.dtype)
                a_gt[...] += _colsum8(dx * y_ref[...])

        @pl.when(i == nt - 1)
        def _():
            dsh_ref[...] = jnp.sum(a_sh[...], axis=0, keepdims=True)
            dsc_ref[...] = jnp.sum(a_sc[...], axis=0, keepdims=True)
            dg_ref[...] = jnp.sum(a_g[...], axis=0, keepdims=True)
            if a_gt is not None:
                dgt_ref[...] = jnp.sum(a_gt[...], axis=0, keepdims=True)

    row = pl.BlockSpec((ROW_TILE, D), lambda i: (i, 0))
    ins = [xin, dh, g, sc]
    in_specs = [row, pl.BlockSpec((ROW_TILE, D), lambda i: (i + dh_row_off, 0)), _vec_spec(D), _vec_spec(D)]
    out_specs = [_vec_spec(D)] * 3
    out_shape = [_sds((1, D), F32)] * 3
    scratch = [pltpu.VMEM((8, D), F32)] * 3
    if want_dx:
        ins.append(add)
        in_specs.append(row)
        out_specs.append(row)
        out_shape.append(_sds((s, D), F32))
    if resid is not None:
        ins += [resid[0], resid[1]]
        in_specs += [_vec_spec(D), row]
        out_specs += [row, _vec_spec(D)]
        out_shape += [_sds((s, D), MXU_DTYPE), _sds((1, D), F32)]
        scratch.append(pltpu.VMEM((8, D), F32))
    return _pallas(body, name=name, grid=(nt,), in_specs=in_specs, out_specs=out_specs, out_shape=out_shape,
                   scratch_shapes=scratch, semantics=("arbitrary",))(*ins)


FF_TILE = 128
FF_CHUNK = 128
HALO = 8


def _shift3(pad_ref, r0, ch):
    return tuple(pad_ref[pl.ds(r0 + HALO + d, ch), :] for d in (-1, 0, 1))


def _fill_padded(pad_ref, src_ref, s, ch, halo):
    zeros = jnp.zeros((halo, pad_ref.shape[1]), F32)
    pad_ref[0:halo, :] = zeros
    pad_ref[s + halo:s + 2 * halo, :] = zeros

    def cp(c, carry):
        r0 = pl.multiple_of(c * ch, ch)
        pad_ref[pl.ds(r0 + halo, ch), :] = src_ref[pl.ds(r0, ch), :].astype(F32)
        return carry

    lax.fori_loop(0, s // ch, cp, 0)


def _ffn_act_fwd(u, w, b):
    s = u.shape[0]
    nj = DFF // FF_TILE
    ch = FF_CHUNK

    def body(ug_ref, uv_ref, wg_ref, wv_ref, bg_ref, bv_ref, f_ref, gpad, vpad):
        _fill_padded(gpad, ug_ref, s, ch, HALO)
        _fill_padded(vpad, uv_ref, s, ch, HALO)

        def conv(pad, w_ref, b_ref, r0):
            prev, cur, nxt = _shift3(pad, r0, ch)
            return w_ref[0:1, :] * prev + w_ref[1:2, :] * cur + w_ref[2:3, :] * nxt + b_ref[...]

        def step(c, carry):
            r0 = pl.multiple_of(c * ch, ch)
            gc = conv(gpad, wg_ref, bg_ref, r0)
            vc = conv(vpad, wv_ref, bv_ref, r0)
            f_ref[pl.ds(r0, ch), :] = (gc * _sigmoid(gc) * vc).astype(f_ref.dtype)
            return carry

        lax.fori_loop(0, s // ch, step, 0)

    col = lambda off: pl.BlockSpec((s, FF_TILE), lambda j: (0, j + off))
    wsp = lambda off: pl.BlockSpec((3, FF_TILE), lambda j: (0, j + off))
    bsp = lambda off: pl.BlockSpec((1, FF_TILE), lambda j: (0, j + off))
    return _pallas(
        body, name="ffn_act_fwd", grid=(nj,),
        in_specs=[col(0), col(nj), wsp(0), wsp(nj), bsp(0), bsp(nj)],
        out_specs=col(0), out_shape=_sds((s, DFF), MXU_DTYPE),
        scratch_shapes=[pltpu.VMEM((s + 2 * HALO, FF_TILE), F32)] * 2,
        semantics=("parallel",),
    )(u, u, w, w, b, b)


def _ffn_act_bwd(u, df, w, b):
    s = u.shape[0]
    nj = DFF // FF_TILE
    ch = FF_CHUNK

    def body(ug_ref, uv_ref, df_ref, wg_ref, wv_ref, bg_ref, bv_ref,
             dug_ref, duv_ref, dwg_ref, dwv_ref, dbg_ref, dbv_ref, gpad, vpad, dgpad, dvpad, acc):
        _fill_padded(gpad, ug_ref, s, ch, HALO)
        _fill_padded(vpad, uv_ref, s, ch, HALO)
        zeros = jnp.zeros((HALO, FF_TILE), F32)
        for p in (dgpad, dvpad):
            p[0:HALO, :] = zeros
            p[s + HALO:s + 2 * HALO, :] = zeros
        acc[...] = jnp.zeros_like(acc)

        def step(c, carry):
            r0 = pl.multiple_of(c * ch, ch)
            gs = _shift3(gpad, r0, ch)
            vs = _shift3(vpad, r0, ch)
            gc = wg_ref[0:1, :] * gs[0] + wg_ref[1:2, :] * gs[1] + wg_ref[2:3, :] * gs[2] + bg_ref[...]
            vc = wv_ref[0:1, :] * vs[0] + wv_ref[1:2, :] * vs[1] + wv_ref[2:3, :] * vs[2] + bv_ref[...]
            sg = _sigmoid(gc)
            dfv = df_ref[pl.ds(r0, ch), :].astype(F32)
            dgc = dfv * vc * (sg * (1.0 + gc * (1.0 - sg)))
            dvc = dfv * (gc * sg)
            dgpad[pl.ds(r0 + HALO, ch), :] = dgc
            dvpad[pl.ds(r0 + HALO, ch), :] = dvc
            for t in range(3):
                acc[8 * t:8 * t + 8, :] += _colsum8(dgc * gs[t])
                acc[24 + 8 * t:32 + 8 * t, :] += _colsum8(dvc * vs[t])
            acc[48:56, :] += _colsum8(dgc)
            acc[56:64, :] += _colsum8(dvc)
            return carry

        lax.fori_loop(0, s // ch, step, 0)

        def step2(c, carry):
            r0 = pl.multiple_of(c * ch, ch)
            for pad, w_ref, o_ref in ((dgpad, wg_ref, dug_ref), (dvpad, wv_ref, duv_ref)):
                prev, cur, nxt = _shift3(pad, r0, ch)
                o_ref[pl.ds(r0, ch), :] = (w_ref[0:1, :] * nxt + w_ref[1:2, :] * cur + w_ref[2:3, :] * prev).astype(o_ref.dtype)
            return carry

        lax.fori_loop(0, s // ch, step2, 0)
        for t in range(3):
            dwg_ref[t:t + 1, :] = jnp.sum(acc[8 * t:8 * t + 8, :], axis=0, keepdims=True)
            dwv_ref[t:t + 1, :] = jnp.sum(acc[24 + 8 * t:32 + 8 * t, :], axis=0, keepdims=True)
        dbg_ref[...] = jnp.sum(acc[48:56, :], axis=0, keepdims=True)
        dbv_ref[...] = jnp.sum(acc[56:64, :], axis=0, keepdims=True)

    col = lambda off: pl.BlockSpec((s, FF_TILE), lambda j: (0, j + off))
    wsp = lambda off: pl.BlockSpec((3, FF_TILE), lambda j: (0, j + off))
    bsp = lambda off: pl.BlockSpec((1, FF_TILE), lambda j: (0, j + off))
    return _pallas(
        body, name="ffn_act_bwd", grid=(nj,),
        in_specs=[col(0), col(nj), col(0), wsp(0), wsp(nj), bsp(0), bsp(nj)],
        out_specs=[col(0), col(0), wsp(0), wsp(0), bsp(0), bsp(0)],
        out_shape=[_sds((s, DFF), MXU_DTYPE)] * 2 + [_sds((3, DFF), F32)] * 2 + [_sds((1, DFF), F32)] * 2,
        scratch_shapes=[pltpu.VMEM((s + 2 * HALO, FF_TILE), F32)] * 4 + [pltpu.VMEM((64, FF_TILE), F32)],
        semantics=("parallel",),
    )(u, u, df, w, w, b, b)


CONV_CHUNK = 64
CONV_HALO = 16


def _tap(pad_ref, r0, k):
    return pad_ref[pl.ds(r0 + CONV_HALO - CW // 2 + k, CONV_CHUNK), :]


def _glu_into(pad_ref, a_ref, g_ref, s):
    zeros = jnp.zeros((CONV_HALO, LANES), F32)
    pad_ref[0:CONV_HALO, :] = zeros
    pad_ref[s + CONV_HALO:s + 2 * CONV_HALO, :] = zeros

    def cp(c, carry):
        r0 = pl.multiple_of(c * ROW_TILE, ROW_TILE)
        pad_ref[pl.ds(r0 + CONV_HALO, ROW_TILE), :] = a_ref[pl.ds(r0, ROW_TILE), :] * _sigmoid(g_ref[pl.ds(r0, ROW_TILE), :])
        return carry

    lax.fori_loop(0, s // ROW_TILE, cp, 0)


def _conf_conv_fwd(ag, conv_w, conv_b):
    s = ag.shape[0]
    nc = DC // LANES

    def body(a_ref, g_ref, w_ref, b_ref, o_ref, upad):
        _glu_into(upad, a_ref, g_ref, s)

        def step(c, carry):
            r0 = pl.multiple_of(c * CONV_CHUNK, CONV_CHUNK)
            acc = jnp.broadcast_to(b_ref[...], (CONV_CHUNK, LANES))
            for k in range(CW):
                acc = acc + w_ref[k:k + 1, :] * _tap(upad, r0, k)
            o_ref[pl.ds(r0, CONV_CHUNK), :] = acc
            return carry

        lax.fori_loop(0, s // CONV_CHUNK, step, 0)

    col = lambda off: pl.BlockSpec((s, LANES), lambda c: (0, c + off))
    return _pallas(
        body, name="conf_conv_fwd", grid=(nc,),
        in_specs=[col(0), col(nc), pl.BlockSpec((CW, LANES), lambda c: (0, c)), pl.BlockSpec((1, LANES), lambda c: (0, c))],
        out_specs=col(0), out_shape=_sds((s, DC), F32),
        scratch_shapes=[pltpu.VMEM((s + 2 * CONV_HALO, LANES), F32)],
        semantics=("parallel",),
    )(ag, ag, conv_w, conv_b)


def _ln_stats(x):
    mu = jnp.mean(x, axis=-1, keepdims=True)
    xc = x - mu
    var = jnp.mean(xc * xc, axis=-1, keepdims=True)
    rstd = lax.rsqrt(var + EPS)
    return xc * rstd, rstd


def _conf_ln_fwd(u1, ln_g, ln_b, ycat):
    s = u1.shape[0]

    def body(u_ref, g_ref, b_ref, ycat_ref, o_ref):
        del ycat_ref
        xhat, _ = _ln_stats(u_ref[...])
        y = xhat * g_ref[...] + b_ref[...]
        o_ref[...] = (y * _sigmoid(y)).astype(o_ref.dtype)

    return _pallas(
        body, name="conf_ln_fwd", grid=(s // ROW_TILE,),
        in_specs=[pl.BlockSpec((ROW_TILE, DC), lambda i: (i, 0)), _vec_spec(DC), _vec_spec(DC),
                  pl.BlockSpec(memory_space=pl.ANY)],
        out_specs=pl.BlockSpec((ROW_TILE, DC), lambda i: (i, 1)),
        out_shape=_sds(ycat.shape, ycat.dtype),
        input_output_aliases={3: 0},
        semantics=("parallel",),
    )(u1, ln_g, ln_b, ycat)


def _conf_ln_bwd(dycat, u1, ln_g, ln_b):
    s = u1.shape[0]
    nt = s // ROW_TILE

    def body(dy_ref, u_ref, g_ref, b_ref, du_ref, dg_ref, db_ref, a_g, a_b):
        i = pl.program_id(0)

        @pl.when(i == 0)
        def _():
            a_g[...] = jnp.zeros_like(a_g)
            a_b[...] = jnp.zeros_like(a_b)

        xhat, rstd = _ln_stats(u_ref[...])
        gv = g_ref[...]
        y = xhat * gv + b_ref[...]
        sg = _sigmoid(y)
        dyl = dy_ref[...] * (sg * (1.0 + y * (1.0 - sg)))
        a_g[...] += _colsum8(dyl * xhat)
        a_b[...] += _colsum8(dyl)
        dxh = dyl * gv
        du_ref[...] = rstd * (dxh - jnp.mean(dxh, axis=-1, keepdims=True)
                              - xhat * jnp.mean(dxh * xhat, axis=-1, keepdims=True))

        @pl.when(i == nt - 1)
        def _():
            dg_ref[...] = jnp.sum(a_g[...], axis=0, keepdims=True)
            db_ref[...] = jnp.sum(a_b[...], axis=0, keepdims=True)

    return _pallas(
        body, name="conf_ln_bwd", grid=(nt,),
        in_specs=[pl.BlockSpec((ROW_TILE, DC), lambda i: (i, 1)), pl.BlockSpec((ROW_TILE, DC), lambda i: (i, 0)),
                  _vec_spec(DC), _vec_spec(DC)],
        out_specs=[pl.BlockSpec((ROW_TILE, DC), lambda i: (i, 0)), _vec_spec(DC), _vec_spec(DC)],
        out_shape=[_sds((s, DC), F32), _sds((1, DC), F32), _sds((1, DC), F32)],
        scratch_shapes=[pltpu.VMEM((8, DC), F32)] * 2,
        semantics=("arbitrary",),
    )(dycat, u1, ln_g, ln_b)


def _conf_conv_bwd(ag, du1, conv_w, rows_out):
    s = ag.shape[0]
    nc = DC // LANES

    def body(a_ref, g_ref, d_ref, w_ref, da_ref, dg_ref, dw_ref, db_ref, upad, dpad, acc):
        _glu_into(upad, a_ref, g_ref, s)
        _fill_padded(dpad, d_ref, s, ROW_TILE, CONV_HALO)
        acc[...] = jnp.zeros_like(acc)

        def step(c, carry):
            r0 = pl.multiple_of(c * CONV_CHUNK, CONV_CHUNK)
            dcur = dpad[pl.ds(r0 + CONV_HALO, CONV_CHUNK), :]
            du0 = jnp.zeros((CONV_CHUNK, LANES), F32)
            for k in range(CW):
                du0 = du0 + w_ref[k:k + 1, :] * _tap(dpad, r0, CW - 1 - k)
                acc[8 * k:8 * k + 8, :] += _colsum8(dcur * _tap(upad, r0, k))
            acc[8 * CW:8 * CW + 8, :] += _colsum8(dcur)
            av = a_ref[pl.ds(r0, CONV_CHUNK), :]
            sg = _sigmoid(g_ref[pl.ds(r0, CONV_CHUNK), :])
            da_ref[pl.ds(r0, CONV_CHUNK), :] = (du0 * sg).astype(da_ref.dtype)
            dg_ref[pl.ds(r0, CONV_CHUNK), :] = (du0 * av * (sg * (1.0 - sg))).astype(dg_ref.dtype)
            return carry

        lax.fori_loop(0, s // CONV_CHUNK, step, 0)
        if rows_out > s:
            zeros = jnp.zeros((rows_out - s, LANES), da_ref.dtype)
            da_ref[s:rows_out, :] = zeros
            dg_ref[s:rows_out, :] = zeros
        for k in range(CW):
            dw_ref[k:k + 1, :] = jnp.sum(acc[8 * k:8 * k + 8, :], axis=0, keepdims=True)
        db_ref[...] = jnp.sum(acc[8 * CW:8 * CW + 8, :], axis=0, keepdims=True)

    col = lambda off: pl.BlockSpec((s, LANES), lambda c: (0, c + off))
    ocol = pl.BlockSpec((rows_out, LANES), lambda c: (0, c))
    return _pallas(
        body, name="conf_conv_bwd", grid=(nc,),
        in_specs=[col(0), col(nc), col(0), pl.BlockSpec((CW, LANES), lambda c: (0, c))],
        out_specs=[ocol, ocol, pl.BlockSpec((CW, LANES), lambda c: (0, c)), pl.BlockSpec((1, LANES), lambda c: (0, c))],
        out_shape=[_sds((rows_out, DC), MXU_DTYPE)] * 2 + [_sds((CW, DC), F32), _sds((1, DC), F32)],
        scratch_shapes=[pltpu.VMEM((s + 2 * CONV_HALO, LANES), F32)] * 2 + [pltpu.VMEM((8 * (CW + 1), LANES), F32)],
        semantics=("parallel",),
    )(ag, ag, du1, conv_w)


Q_TILE = 2 * GW
K_WIN = PAIR_ROWS * GW


def _bias_table(rpb_rev):
    def body(p_ref, t_ref):
        kcol = lax.broadcasted_iota(jnp.int32, (GW, LANES), 0)
        lane = lax.broadcasted_iota(jnp.int32, (GW, LANES), 1)
        qcol = lane % GW
        cs = jnp.clip(qcol - NA_ROWS, 0, GW - 2 * NA_ROWS)
        colvalid = (kcol >= cs) & (kcol < cs + 2 * NA_ROWS)
        neg = jnp.full((GW, LANES), NEG, F32)

        def skew(h, ro, shift):
            if ro < 0 or ro >= 2 * NA_ROWS - 1:
                return neg
            row = jnp.broadcast_to(p_ref[h * 16 + ro:h * 16 + ro + 1, :], (GW, LANES))
            return pltpu.roll(row, shift, 1, stride=1, stride_axis=0)

        for h in range(NH):
            for b in range(TAB_BLOCKS):
                val = jnp.where(lane < GW, skew(h, b - 1, GW + 1), skew(h, b - 2, 1))
                t_ref[h, b * GW:(b + 1) * GW, :] = jnp.where(colvalid, val, neg)

    return _pallas(body, name="attn_bias_table", out_shape=_sds((NH, TAB_BLOCKS * GW, LANES), F32))(rpb_rev)


def _rpb_grad(tt):
    def body(t_ref, o_ref):
        lane = lax.broadcasted_iota(jnp.int32, (GW, LANES), 1)
        si = lax.broadcasted_iota(jnp.int32, (GW, GW), 0)
        ti = lax.broadcasted_iota(jnp.int32, (GW, GW), 1)
        flip = jnp.where(si + ti == GW - 1, 1.0, 0.0).astype(F32)
        o_ref[...] = jnp.zeros_like(o_ref)
        for h in range(NH):
            for ro in range(2 * NA_ROWS - 1):
                lo = t_ref[h, (ro + 1) * GW:(ro + 2) * GW, :]
                hi = t_ref[h, (ro + 2) * GW:(ro + 3) * GW, :]
                g = jnp.where(lane < GW, lo + pltpu.roll(hi, GW, 1), 0.0)
                gf = jnp.dot(flip, g, preferred_element_type=F32, precision=lax.Precision.HIGHEST)
                sk = pltpu.roll(gf, 0, 1, stride=1, stride_axis=0)
                o_ref[h * 16 + ro:h * 16 + ro + 1, :] = jnp.sum(sk, axis=0, keepdims=True)

    return _pallas(body, name="attn_rpb_grad", out_shape=_sds((NH * 16, LANES), F32))(tt)


def _attn_geometry(i, rows):
    wsp = jnp.clip(2 * i - NA_ROWS // 2, 0, rows - PAIR_ROWS)
    k0 = pl.multiple_of(wsp * GW, GW)
    t0 = pl.multiple_of((wsp - 2 * i + NA_ROWS) * GW, GW)
    rr = lax.broadcasted_iota(jnp.int32, (GW, Q_TILE), 1) // GW
    wsr = jnp.clip(2 * i + rr - NA_ROWS // 2, 0, rows - NA_ROWS)
    edge_masks = tuple(jnp.where((kr >= wsr) & (kr < wsr + NA_ROWS), 0.0, NEG).astype(F32)
                       for kr in (wsp, wsp + PAIR_ROWS - 1))
    return k0, t0, edge_masks


def _biased(s_raw, bias, edge_masks):
    x = s_raw + bias
    return jnp.concatenate([x[:GW] + edge_masks[0], x[GW:K_WIN - GW], x[K_WIN - GW:] + edge_masks[1]], axis=0)


def _two_heads_on_lanes(xt):
    feat = lax.broadcasted_iota(jnp.int32, xt.shape, 0)
    zero = jnp.zeros_like(xt)
    return jnp.concatenate([jnp.where(feat < HD, xt, zero), jnp.where(feat >= HD, xt, zero)], axis=1)


def _two_heads_on_rows(x):
    lane = lax.broadcasted_iota(jnp.int32, x.shape, 1)
    zero = jnp.zeros_like(x)
    return jnp.concatenate([jnp.where(lane < HD, x, zero), jnp.where(lane >= HD, x, zero)], axis=0)


def _pick_heads(x2):
    n = x2.shape[0] // 2
    lane = lax.broadcasted_iota(jnp.int32, (n, LANES), 1)
    return jnp.where(lane < HD, x2[:n], x2[n:])


_TN = (((0,), (0,)), ((), ()))


def _attn_fwd(qkv, tab, s):
    rows = s // GW
    npair = rows // 2

    def body(q_ref, kv_ref, tab_ref, o_ref, lse_ref):
        i = pl.program_id(0)
        k0, t0, edge_masks = _attn_geometry(i, rows)
        for p in range(NH // 2):
            cq = slice(p * LANES, (p + 1) * LANES)
            ck = slice(DA + p * LANES, DA + (p + 1) * LANES)
            cv = slice(2 * DA + p * LANES, 2 * DA + (p + 1) * LANES)
            qm2 = _two_heads_on_lanes(q_ref[:, cq].T) * SCALE
            s_loc = jnp.dot(kv_ref[pl.ds(k0, K_WIN), ck], qm2, preferred_element_type=F32)
            s_ctx = jnp.dot(kv_ref[pl.ds(s, CTX), ck], qm2, preferred_element_type=F32)
            p_loc, p_ctx = [], []
            for hh in range(2):
                h = 2 * p + hh
                ch = slice(hh * Q_TILE, (hh + 1) * Q_TILE)
                sl = _biased(s_loc[:, ch], tab_ref[h, pl.ds(t0, K_WIN), :], edge_masks)
                sc = s_ctx[:, ch]
                m = jnp.maximum(jnp.max(sl, axis=0, keepdims=True), jnp.max(sc, axis=0, keepdims=True))
                el = jnp.exp(sl - m)
                ec = jnp.exp(sc - m)
                l = jnp.sum(el, axis=0, keepdims=True) + jnp.sum(ec, axis=0, keepdims=True)
                inv = 1.0 / l
                lse_ref[h:h + 1, :] = m + jnp.log(l)
                p_loc.append((el * inv).astype(MXU_DTYPE))
                p_ctx.append((ec * inv).astype(MXU_DTYPE))
            o2 = (lax.dot_general(jnp.concatenate(p_loc, axis=1), kv_ref[pl.ds(k0, K_WIN), cv], _TN, preferred_element_type=F32)
                  + lax.dot_general(jnp.concatenate(p_ctx, axis=1), kv_ref[pl.ds(s, CTX), cv], _TN, preferred_element_type=F32))
            o_ref[:, cq] = _pick_heads(o2).astype(o_ref.dtype)

    return _pallas(
        body, name="attn_fwd", grid=(npair,),
        in_specs=[pl.BlockSpec((Q_TILE, DA), lambda i: (i, 0)), pl.BlockSpec(memory_space=pltpu.VMEM),
                  pl.BlockSpec(memory_space=pltpu.VMEM)],
        out_specs=[pl.BlockSpec((Q_TILE, DA), lambda i: (i, 0)), pl.BlockSpec((NH, Q_TILE), lambda i: (0, i))],
        out_shape=[_sds((s, D), MXU_DTYPE), _sds((NH, s), F32)],
        semantics=("arbitrary",),
    )(qkv, qkv, tab)


def _attn_bwd(qkv, tab, lse, dycat, s):
    rows = s // GW
    npair = rows // 2
    sa = s + CTX
    nzero = CTX // Q_TILE

    def body(q_ref, do_ref, lse_ref, kv_ref, tab_ref, dq_ref, dkv_ref, tt_ref, dk_acc, dv_acc):
        i = pl.program_id(0)

        @pl.when(i == 0)
        def _():
            dk_acc[...] = jnp.zeros_like(dk_acc)
            dv_acc[...] = jnp.zeros_like(dv_acc)
            tt_ref[...] = jnp.zeros_like(tt_ref)

        @pl.when(i >= npair)
        def _():
            dq_ref[...] = jnp.zeros_like(dq_ref)

        @pl.when(i < npair)
        def _():
            k0, t0, edge_masks = _attn_geometry(i, rows)
            for p in range(NH // 2):
                cq = slice(p * LANES, (p + 1) * LANES)
                ck = slice(DA + p * LANES, DA + (p + 1) * LANES)
                cv = slice(2 * DA + p * LANES, 2 * DA + (p + 1) * LANES)
                qp = q_ref[:, cq] * SCALE
                dop = do_ref[:, cq].astype(MXU_DTYPE)
                qm2 = _two_heads_on_lanes(qp.T)
                dom2 = _two_heads_on_lanes(dop.T)
                kw = kv_ref[pl.ds(k0, K_WIN), ck]
                kc = kv_ref[pl.ds(s, CTX), ck]
                vw = kv_ref[pl.ds(k0, K_WIN), cv]
                vc = kv_ref[pl.ds(s, CTX), cv]
                s_loc = jnp.dot(kw, qm2, preferred_element_type=F32)
                s_ctx = jnp.dot(kc, qm2, preferred_element_type=F32)
                dp_loc = jnp.dot(vw, dom2, preferred_element_type=F32)
                dp_ctx = jnp.dot(vc, dom2, preferred_element_type=F32)
                p_loc, p_ctx, ds_loc, ds_ctx = [], [], [], []
                for hh in range(2):
                    h = 2 * p + hh
                    ch = slice(hh * Q_TILE, (hh + 1) * Q_TILE)
                    lse_h = lse_ref[h:h + 1, :]
                    pl_ = jnp.exp(_biased(s_loc[:, ch], tab_ref[h, pl.ds(t0, K_WIN), :], edge_masks) - lse_h)
                    pc_ = jnp.exp(s_ctx[:, ch] - lse_h)
                    dpl = dp_loc[:, ch]
                    dpc = dp_ctx[:, ch]
                    delta = jnp.sum(pl_ * dpl, axis=0, keepdims=True) + jnp.sum(pc_ * dpc, axis=0, keepdims=True)
                    dsl = pl_ * (dpl - delta)
                    dsc = pc_ * (dpc - delta)
                    tt_ref[h, pl.ds(t0, K_WIN), :] += dsl
                    p_loc.append(pl_.astype(MXU_DTYPE))
                    p_ctx.append(pc_.astype(MXU_DTYPE))
                    ds_loc.append(dsl.astype(MXU_DTYPE))
                    ds_ctx.append(dsc.astype(MXU_DTYPE))
                p_loc, p_ctx = jnp.concatenate(p_loc, axis=1), jnp.concatenate(p_ctx, axis=1)
                ds_loc, ds_ctx = jnp.concatenate(ds_loc, axis=1), jnp.concatenate(ds_ctx, axis=1)
                do_rows = _two_heads_on_rows(dop)
                q_rows = _two_heads_on_rows(qp)
                dv_acc[pl.ds(k0, K_WIN), cq] += jnp.dot(p_loc, do_rows, preferred_element_type=F32)
                dv_acc[pl.ds(s, CTX), cq] += jnp.dot(p_ctx, do_rows, preferred_element_type=F32)
                dk_acc[pl.ds(k0, K_WIN), cq] += jnp.dot(ds_loc, q_rows, preferred_element_type=F32)
                dk_acc[pl.ds(s, CTX), cq] += jnp.dot(ds_ctx, q_rows, preferred_element_type=F32)
                dq2 = (lax.dot_general(ds_loc, kw, _TN, preferred_element_type=F32)
                       + lax.dot_general(ds_ctx, kc, _TN, preferred_element_type=F32))
                dq_ref[:, cq] = (_pick_heads(dq2) * SCALE).astype(dq_ref.dtype)

        @pl.when(i == npair - 1)
        def _():
            def cp(c, carry):
                r0 = pl.multiple_of(c * ROW_TILE, ROW_TILE)
                dkv_ref[pl.ds(r0, ROW_TILE), 0:DA] = dk_acc[pl.ds(r0, ROW_TILE), :].astype(dkv_ref.dtype)
                dkv_ref[pl.ds(r0, ROW_TILE), DA:2 * DA] = dv_acc[pl.ds(r0, ROW_TILE), :].astype(dkv_ref.dtype)
                return carry

            lax.fori_loop(0, sa // ROW_TILE, cp, 0)

    qmap = lambda i: (jnp.minimum(i, npair - 1), 0)
    return _pallas(
        body, name="attn_bwd", grid=(npair + nzero,),
        in_specs=[pl.BlockSpec((Q_TILE, DA), qmap), pl.BlockSpec((Q_TILE, DA), qmap),
                  pl.BlockSpec((NH, Q_TILE), lambda i: (0, jnp.minimum(i, npair - 1))),
                  pl.BlockSpec(memory_space=pltpu.VMEM), pl.BlockSpec(memory_space=pltpu.VMEM)],
        out_specs=[pl.BlockSpec((Q_TILE, DA), lambda i: (i, 0)), pl.BlockSpec(memory_space=pltpu.VMEM),
                   pl.BlockSpec(memory_space=pltpu.VMEM)],
        out_shape=[_sds((sa, DA), MXU_DTYPE), _sds((sa, 2 * DA), MXU_DTYPE), _sds((NH, TAB_BLOCKS * GW, LANES), F32)],
        scratch_shapes=[pltpu.VMEM((sa, DA), F32)] * 2,
        semantics=("arbitrary",),
    )(qkv, dycat, lse, qkv, tab)


def _tile(n, prefs):
    for t in prefs:
        if n % t == 0:
            return t
    raise ValueError((n, prefs))


def _local_step(x, ctx, tgt, mod, mod_c, vec, w_in, late_weights, rpb_rev, early_grads=None):
    s = x.shape[0]
    sa = s + CTX
    ts = _tile(s, (1024, 512, 256))
    ts2 = _tile(s, (2048, 1024, 512, 256))
    tsa = _tile(sa, (1088, 640, 256))
    tsa2 = _tile(sa, (2176, 640, 256))
    sh1, sc1, gt1, sh2, sc2, gt2 = (mod[i:i + 1] for i in range(6))
    csh1, csc1 = mod_c[0:1], mod_c[1:2]
    act = MXU_DTYPE

    tab = _bias_table(rpb_rev)
    h_all = _rmsmod_fwd(x, ctx, vec["g_norm1"], sc1, sh1, csc1, csh1)
    w_in = w_in(h_all) if callable(w_in) else w_in
    qkv = _mm(h_all, w_in, mode="nn", m=sa, n=3 * DA, k=D, tm=tsa2, tn=512, tk=D, out_dtype=MXU_DTYPE, name="mm_qkv")
    ag = _mm(h_all, w_in, mode="nn", m=s, n=2 * DC, k=D, tm=ts2, tn=512, tk=D, out_dtype=F32, name="mm_ag", b_off=(0, 3))
    ycat, lse = _attn_fwd(qkv, tab, s)
    u1 = _conf_conv_fwd(ag, vec["conv_w"], vec["conv_b"])
    ycat = _conf_ln_fwd(u1, vec["ln_g"], vec["ln_b"], ycat)
    if callable(late_weights):
        w_out, ffn_weights = late_weights(ycat)
    else:
        w_out, ffn_weights = late_weights[0], late_weights[1:]
    y = _mm(ycat, w_out, mode="nn", m=s, n=D, k=D, tm=ts2, tn=512, tk=D, out_dtype=F32, name="mm_out")
    x1, h2 = _resid_rmsmod_fwd(x, y, gt1, vec["g_norm2"], sc2, sh2)
    w_up, w_down = ffn_weights(h2) if callable(ffn_weights) else ffn_weights
    u = _mm(h2, w_up, mode="nn", m=s, n=2 * DFF, k=D, tm=ts2, tn=512, tk=D, out_dtype=act, name="mm_up")
    f = _ffn_act_fwd(u, vec["ffn_conv_w"], vec["ffn_conv_b"])
    z = _mm(f, w_down, mode="nn", m=s, n=D, k=DFF, tm=ts, tn=D, tk=DFF, out_dtype=F32, name="mm_down")
    dx2, dz, loss, dgt2, dgf = _final_fwd_bwd(x1, z, gt2, vec["g_final"], tgt)

    df = _mm(dz, w_down, mode="nt", m=s, n=DFF, k=D, tm=ts, tn=DFF, tk=D, out_dtype=act, name="mm_down_dx")
    d_w_down = _mm(f, dz, mode="tn", m=DFF, n=D, k=s, tm=DFF // 2, tn=D, tk=ts2, out_dtype=F32, name="mm_down_dw")
    dug, duv, dfw_g, dfw_v, dfb_g, dfb_v = _ffn_act_bwd(u, df, vec["ffn_conv_w"], vec["ffn_conv_b"])
    dw_kw = dict(mode="tn", m=D, n=DFF, k=s, tm=D, tn=DFF, tk=ts, out_dtype=F32, out_total=(D, 2 * DFF))
    d_w_up = _mm(h2, dug, name="mm_up_dw_gate", **dw_kw)
    d_w_up = _mm(h2, duv, name="mm_up_dw_val", o_off=(0, 1), into=d_w_up, **dw_kw)
    if early_grads is not None:
        early_grads[0](d_w_up, d_w_down)
    dh2 = _mm([dug, duv], w_up, mode="nt", m=s, n=D, k=2 * DFF, tm=ts, tn=D, tk=2 * DFF, out_dtype=F32, name="mm_up_dx")
    sc2_b = sc2 if early_grads is None else sc2 + early_grads[1](dh2)
    dsh2, dsc2, dg2, dx1, dy, dgt1 = _rmsmod_bwd(x1, dh2, vec["g_norm2"], sc2_b, name="rmsmod2_bwd", add=dx2, resid=(gt1, y))
    dycat = _mm(dy, w_out, mode="nt", m=s, n=D, k=D, tm=ts2, tn=512, tk=D, out_dtype=F32, name="mm_out_dx")
    d_w_out = _mm(ycat, dy, mode="tn", m=D, n=D, k=s, tm=D, tn=D, tk=ts, out_dtype=F32, name="mm_out_dw")
    du1, dln_g, dln_b = _conf_ln_bwd(dycat, u1, vec["ln_g"], vec["ln_b"])
    da, dg, dconv_w, dconv_b = _conf_conv_bwd(ag, du1, vec["conv_w"], sa)
    dq, dkv, tt = _attn_bwd(qkv, tab, lse, dycat, s)
    drpb_rev = _rpb_grad(tt)
    d_pieces = [dq, dkv, da, dg]
    dh = _mm(d_pieces, w_in, mode="nt", m=sa, n=D, k=NIN, tm=tsa, tn=D, tk=NIN, out_dtype=F32, name="mm_in_dx")
    d_w_in = _mm(h_all, d_pieces, mode="tn", m=D, n=NIN, k=sa, tm=D, tn=NIN, tk=tsa, out_dtype=F32, name="mm_in_dw")
    dsh1, dsc1, dg1, grad_x = _rmsmod_bwd(x, dh, vec["g_norm1"], sc1, name="rmsmod1_bwd", add=dx1)
    dcsh1, dcsc1, dg1c = _rmsmod_bwd(ctx, dh, vec["g_norm1"], csc1, name="rmsmod1_ctx_bwd", dh_row_off=s // ROW_TILE)

    small = dict(
        dmod=[dsh1, dsc1, dgt1, dsh2, dsc2, dgt2], dmod_c=[dcsh1, dcsc1],
        g_norm1=[dg1, dg1c], g_norm2=dg2, g_final=dgf, conv_b=dconv_b, ln_g=dln_g, ln_b=dln_b, conv_w=dconv_w,
        ffn_conv_w=[dfw_g, dfw_v], ffn_conv_b=[dfb_g, dfb_v], rpb_rev=drpb_rev,
    )
    return loss, grad_x, d_w_in, d_w_out, d_w_up, d_w_down, small


N_CHIPS = 4
HBM = pl.BlockSpec(memory_space=pl.ANY)
BIG = {"w_in": ("col", (D, NIN)), "w_out": ("row", (D, D)), "w_up": ("col", (D, 2 * DFF)), "w_down": ("row", (DFF, D))}
BIG_NAMES = tuple(BIG)
LATE_NAMES = ("w_out", "w_up", "w_down")


def _shard_shape(name):
    kind, (r, c) = BIG[name]
    return (r, c // N_CHIPS) if kind == "col" else (r // N_CHIPS, c)


def _half_rows(name):
    return _shard_shape(name)[0] // 2


def _place():
    x, y, c = lax.axis_index("x"), lax.axis_index("y"), lax.axis_index("c")
    others = [(1 - x, y), (x, 1 - y), (1 - x, 1 - y)]
    return x, y, c, 2 * x + y, (x, y, 1 - c), others


def _whole_region(ref, name, chip, half):
    kind, _ = BIG[name]
    r, c = _shard_shape(name)
    if kind == "col":
        return ref.at[pl.ds(half * (r // 2), r // 2), pl.ds(chip * c, c)]
    return ref.at[pl.ds(chip * r + half * (r // 2), r // 2), :]


def _remote(src, dst, send_sem, recv_sem, to):
    return pltpu.make_async_remote_copy(src_ref=src, dst_ref=dst, send_sem=send_sem, recv_sem=recv_sem,
                                        device_id=to, device_id_type=MESH)


def _gather_small(v, name):
    m_per, n = v.shape

    def body(x_ref, out_ref, send_sems, recv_sems, local_sem):
        x, y, c, _, sibling, others = _place()
        me = (x, y, c)

        def rows(px, py, pc):
            return out_ref.at[pl.ds((4 * px + 2 * py + pc) * m_per, m_per), :]

        def copy(k, block, to, src=None):
            return _remote(rows(*block) if src is None else src, rows(*block), send_sems.at[k], recv_sems.at[k], to)

        mine = pltpu.make_async_copy(x_ref, rows(*me), local_sem)
        mine.start()
        first = [copy(0, me, sibling, src=x_ref)]
        first += [copy(1 + j, me, (*chip, c), src=x_ref) for j, chip in enumerate(others)]
        for cp in first:
            cp.start()
        passed = [copy(4 + j, (*chip, c), sibling) for j, chip in enumerate(others)]
        for j, chip in enumerate(others):
            copy(1 + j, (*chip, c), me).wait_recv()
            passed[j].start()
        copy(0, sibling, me).wait_recv()
        for j, chip in enumerate(others):
            copy(4 + j, (*chip, 1 - c), me).wait_recv()
        for cp in first + passed:
            cp.wait_send()
        mine.wait()

    return pl.pallas_call(
        body, name=name, out_shape=_sds((8 * m_per, n), v.dtype),
        in_specs=[pl.BlockSpec(memory_space=pltpu.VMEM)], out_specs=pl.BlockSpec(memory_space=pltpu.VMEM),
        scratch_shapes=[pltpu.SemaphoreType.DMA((7,)), pltpu.SemaphoreType.DMA((7,)), pltpu.SemaphoreType.DMA],
    )(v)


def _cast_into_whole(name, shard, chip):
    kind, whole = BIG[name]
    r, c = shard.shape
    if kind == "col":
        tr = 256
        o_spec = pl.BlockSpec((tr, c), lambda i, ch: (i, ch[0]))
    else:
        tr = _tile(r, (128, 352))
        o_spec = pl.BlockSpec((tr, c), lambda i, ch: (ch[0] * (r // tr) + i, 0))

    def body(ch_ref, x_ref, o_ref):
        del ch_ref
        o_ref[...] = x_ref[...].astype(o_ref.dtype)

    return _pallas(body, name="cast_" + name, prefetch=1, grid=(r // tr,),
                   in_specs=[pl.BlockSpec((tr, c), lambda i, ch: (i, 0))], out_specs=o_spec,
                   out_shape=_sds(whole, MXU_DTYPE), semantics=("parallel",))(chip, shard)


SEM = pl.BlockSpec(memory_space=pltpu.SEMAPHORE)
IN_HBM = pl.BlockSpec(memory_space=pltpu.HBM)
DATAFLOW = pltpu.SideEffectType.DATAFLOW_SIDE_EFFECTING


def _keep_in_hbm(a):
    return pltpu.with_memory_space_constraint(a, pltpu.HBM)


def _several(after):
    return list(after) if isinstance(after, (list, tuple)) else [after]


FLIPS = [(dx, dy, dc) for dx in (0, 1) for dy in (0, 1) for dc in (0, 1)][1:]


def _flipped(flip):
    x, y, c = lax.axis_index("x"), lax.axis_index("y"), lax.axis_index("c")
    return tuple(1 - v if f else v for v, f in zip((x, y, c), flip))


def _share_start(v, tag, after):
    r, n = v.shape
    ns = 2 * len(FLIPS)

    def body(*refs):
        v_ref, land_ref = refs[0], refs[1]
        sems = refs[2 + len(_several(after)):2 + len(_several(after)) + ns]
        x, y, c = lax.axis_index("x"), lax.axis_index("y"), lax.axis_index("c")
        mine = land_ref.at[pl.ds((4 * x + 2 * y + c) * r, r), :]
        for k, flip in enumerate(FLIPS):
            _remote(v_ref, mine, sems[2 * k], sems[2 * k + 1], _flipped(flip)).start()

    res = pl.pallas_call(
        body, name="share_" + tag + "_start",
        out_shape=(*[pltpu.SemaphoreType.DMA(())] * ns, pltpu.HBM(v.shape, v.dtype), pltpu.HBM((8 * r, n), v.dtype)),
        in_specs=[IN_HBM] * 2 + [pl.BlockSpec(memory_space=pl.ANY)] * len(_several(after)),
        out_specs=(*[SEM] * ns, IN_HBM, IN_HBM),
        input_output_aliases={0: ns, 1: ns + 1},
        compiler_params=pltpu.CompilerParams(has_side_effects=DATAFLOW),
    )(_keep_in_hbm(v), _keep_in_hbm(jnp.tile(v, (8, 1))), *_several(after))
    return list(res[:ns]), res[ns], res[ns + 1]


def _share_wait(started, after, tag):
    sems, v, land = started
    r = v.shape[0]
    ns = len(sems)

    def body(*refs):
        v_ref, land_ref = refs[0], refs[1]
        sem_refs = refs[2:2 + ns]
        for k, flip in enumerate(FLIPS):
            px, py, pc = _flipped(flip)
            theirs = land_ref.at[pl.ds((4 * px + 2 * py + pc) * r, r), :]
            cp = _remote(v_ref, theirs, sem_refs[2 * k], sem_refs[2 * k + 1], (px, py, pc))
            cp.wait_send()
            cp.wait_recv()

    res = pl.pallas_call(
        body, name="share_" + tag + "_wait",
        out_shape=(pltpu.HBM(v.shape, v.dtype), pltpu.HBM(land.shape, land.dtype)),
        in_specs=[IN_HBM] * 2 + [SEM] * ns + [pl.BlockSpec(memory_space=pl.ANY)] * len(_several(after)),
        out_specs=(IN_HBM, IN_HBM),
        input_output_aliases={0: 0, 1: 1},
        compiler_params=pltpu.CompilerParams(has_side_effects=DATAFLOW),
    )(v, land, *sems, *_several(after))
    return res[1]


def _gather_start(wholes, names, after, tag):
    nw = len(names)
    ns = 2 * 3 * nw

    def body(*refs):
        ins = refs[:nw]
        sems = refs[nw + 1:nw + 1 + ns]
        token = refs[2 * nw + ns + 1]
        _, _, c, chip, _, others = _place()
        for w, name in enumerate(names):
            mine = _whole_region(ins[w], name, chip, c)
            for t, (ox, oy) in enumerate(others):
                k = 2 * (3 * w + t)
                _remote(mine, mine, sems[k], sems[k + 1], (ox, oy, c)).start()
        token[...] = jnp.zeros_like(token)

    res = pl.pallas_call(
        body, name="gather_" + tag + "_start",
        out_shape=(*[pltpu.SemaphoreType.DMA(())] * ns, *[pltpu.HBM(a.shape, a.dtype) for a in wholes], _sds((8, LANES), F32)),
        in_specs=[IN_HBM] * nw + [pl.BlockSpec(memory_space=pl.ANY)],
        out_specs=(*[SEM] * ns, *[IN_HBM] * nw, pl.BlockSpec(memory_space=pltpu.VMEM)),
        input_output_aliases={i: ns + i for i in range(nw)},
        compiler_params=pltpu.CompilerParams(has_side_effects=DATAFLOW),
    )(*[_keep_in_hbm(a) for a in wholes], after)
    return list(res[:ns]), list(res[ns:ns + nw]), res[ns + nw]


def _gather_wait(sems, wholes, names, after, tag):
    nw = len(names)
    ns = len(sems)

    def body(*refs):
        ins = refs[:nw]
        sem_refs = refs[nw:nw + ns]
        _, _, c, chip, _, others = _place()
        for w, name in enumerate(names):
            mine = _whole_region(ins[w], name, chip, c)
            for t, (ox, oy) in enumerate(others):
                got = _whole_region(ins[w], name, 2 * ox + oy, c)
                k = 2 * (3 * w + t)
                cp = _remote(mine, got, sem_refs[k], sem_refs[k + 1], (ox, oy, c))
                cp.wait_send()
                cp.wait_recv()

    return pl.pallas_call(
        body, name="gather_" + tag + "_wait",
        out_shape=tuple(pltpu.HBM(a.shape, a.dtype) for a in wholes),
        in_specs=[IN_HBM] * nw + [SEM] * ns + [pl.BlockSpec(memory_space=pl.ANY)], out_specs=tuple([IN_HBM] * nw),
        input_output_aliases={i: i for i in range(nw)},
        compiler_params=pltpu.CompilerParams(has_side_effects=DATAFLOW),
    )(*wholes, *sems, after)


def _forward_halves(wholes, names, tag):
    nw = len(names)

    def body(*refs):
        outs = refs[nw:2 * nw]
        send_sems, recv_sems = refs[2 * nw:]
        _, _, c, _, sibling, others = _place()
        sends = []
        for w, name in enumerate(names):
            for t, (ox, oy) in enumerate(others):
                got = _whole_region(outs[w], name, 2 * ox + oy, c)
                cp = _remote(got, got, send_sems.at[w, t], recv_sems.at[w, t], sibling)
                cp.start()
                sends.append(cp)
        for w, name in enumerate(names):
            for t, (ox, oy) in enumerate(others):
                got = _whole_region(outs[w], name, 2 * ox + oy, 1 - c)
                _remote(got, got, send_sems.at[w, t], recv_sems.at[w, t], sibling).wait_recv()
        for cp in sends:
            cp.wait_send()

    return pl.pallas_call(
        body, name="gather_" + tag + "_forward",
        out_shape=[_sds(a.shape, a.dtype) for a in wholes],
        in_specs=[HBM] * nw, out_specs=[HBM] * nw,
        input_output_aliases={i: i for i in range(nw)},
        scratch_shapes=[pltpu.SemaphoreType.DMA((nw, 3)), pltpu.SemaphoreType.DMA((nw, 3))],
    )(*wholes)


def _forward_start(wholes, names, tag, after):
    nw = len(names)
    ns = 2 * 3 * nw

    def body(*refs):
        ins = refs[:nw]
        sems = refs[nw + 1:nw + 1 + ns]
        token = refs[2 * nw + ns + 1]
        _, _, c, _, sibling, others = _place()
        for w, name in enumerate(names):
            for t, (ox, oy) in enumerate(others):
                got = _whole_region(ins[w], name, 2 * ox + oy, c)
                k = 2 * (3 * w + t)
                _remote(got, got, sems[k], sems[k + 1], sibling).start()
        token[...] = jnp.zeros_like(token)

    res = pl.pallas_call(
        body, name="gather_" + tag + "_forward_start",
        out_shape=(*[pltpu.SemaphoreType.DMA(())] * ns, *[pltpu.HBM(a.shape, a.dtype) for a in wholes], _sds((8, LANES), F32)),
        in_specs=[IN_HBM] * nw + [pl.BlockSpec(memory_space=pl.ANY)],
        out_specs=(*[SEM] * ns, *[IN_HBM] * nw, pl.BlockSpec(memory_space=pltpu.VMEM)),
        input_output_aliases={i: ns + i for i in range(nw)},
        compiler_params=pltpu.CompilerParams(has_side_effects=DATAFLOW),
    )(*[_keep_in_hbm(a) for a in wholes], after)
    return list(res[:ns]), list(res[ns:ns + nw]), res[ns + nw]


def _forward_wait(sems, wholes, names, after, tag):
    nw = len(names)
    ns = len(sems)

    def body(*refs):
        ins = refs[:nw]
        sem_refs = refs[nw:nw + ns]
        _, _, c, _, sibling, others = _place()
        for w, name in enumerate(names):
            for t, (ox, oy) in enumerate(others):
                k = 2 * (3 * w + t)
                cp = _remote(_whole_region(ins[w], name, 2 * ox + oy, c), _whole_region(ins[w], name, 2 * ox + oy, 1 - c),
                             sem_refs[k], sem_refs[k + 1], sibling)
                cp.wait_send()
                cp.wait_recv()

    return pl.pallas_call(
        body, name="gather_" + tag + "_forward_wait",
        out_shape=tuple(pltpu.HBM(a.shape, a.dtype) for a in wholes),
        in_specs=[IN_HBM] * nw + [SEM] * ns + [pl.BlockSpec(memory_space=pl.ANY)], out_specs=tuple([IN_HBM] * nw),
        input_output_aliases={i: i for i in range(nw)},
        compiler_params=pltpu.CompilerParams(has_side_effects=DATAFLOW),
    )(*wholes, *sems, after)


def _compact_shape(name, dtype):
    kind, (r, c) = BIG[name]
    return _sds((r // 2, c), dtype)


def _swap_pairs(ins, outs, names, c):
    pairs = []
    for w, name in enumerate(names):
        kind, _ = BIG[name]
        half = _half_rows(name)
        if kind == "col":
            pairs.append((ins[w].at[pl.ds((1 - c) * half, half), :], outs[w]))
        else:
            pairs += [(ins[w].at[pl.ds(jj * 2 * half + (1 - c) * half, half), :], outs[w].at[pl.ds(jj * half, half), :])
                      for jj in range(N_CHIPS)]
    return pairs


def _n_swap_copies(names):
    return sum(1 if BIG[n][0] == "col" else N_CHIPS for n in names)


def _swap_start(grads, names, label):
    nw = len(names)
    ns = 2 * _n_swap_copies(names)

    def body(*refs):
        ins, lands = refs[:nw], refs[nw:2 * nw]
        sems = refs[2 * nw:2 * nw + ns]
        token = refs[4 * nw + ns]
        _, _, c, _, sibling, _ = _place()
        for k, (src, dst) in enumerate(_swap_pairs(ins, lands, names, c)):
            _remote(src, dst, sems[2 * k], sems[2 * k + 1], sibling).start()
        token[...] = jnp.zeros_like(token)

    lands = [_keep_in_hbm(lax.empty(_compact_shape(n, F32).shape, F32)) for n in names]
    res = pl.pallas_call(
        body, name=label,
        out_shape=(*[pltpu.SemaphoreType.DMA(())] * ns, *[pltpu.HBM(a.shape, a.dtype) for a in grads],
                   *[pltpu.HBM(a.shape, a.dtype) for a in lands], _sds((8, LANES), F32)),
        in_specs=[IN_HBM] * (2 * nw),
        out_specs=(*[SEM] * ns, *[IN_HBM] * (2 * nw), pl.BlockSpec(memory_space=pltpu.VMEM)),
        input_output_aliases={i: ns + i for i in range(2 * nw)},
        compiler_params=pltpu.CompilerParams(has_side_effects=DATAFLOW),
    )(*[_keep_in_hbm(a) for a in grads], *lands)
    return list(res[:ns]), list(res[ns:ns + nw]), list(res[ns + nw:ns + 2 * nw]), res[ns + 2 * nw]


def _swap_wait(sems, grads, lands, names, after, label):
    nw = len(names)
    ns = len(sems)

    def body(*refs):
        ins, land_refs = refs[:nw], refs[nw:2 * nw]
        sem_refs = refs[2 * nw:2 * nw + ns]
        _, _, c, _, sibling, _ = _place()
        for k, (src, dst) in enumerate(_swap_pairs(ins, land_refs, names, c)):
            cp = _remote(src, dst, sem_refs[2 * k], sem_refs[2 * k + 1], sibling)
            cp.wait_send()
            cp.wait_recv()

    res = pl.pallas_call(
        body, name=label,
        out_shape=tuple(pltpu.HBM(a.shape, a.dtype) for a in (*grads, *lands)),
        in_specs=[IN_HBM] * (2 * nw) + [SEM] * ns + [pl.BlockSpec(memory_space=pl.ANY)] * len(_several(after)),
        out_specs=tuple([IN_HBM] * (2 * nw)),
        input_output_aliases={i: i for i in range(2 * nw)},
        compiler_params=pltpu.CompilerParams(has_side_effects=DATAFLOW),
    )(*grads, *lands, *sems, *_several(after))
    return list(res[:nw]), list(res[nw:])


def _add_halves(name, grad, got, core):
    kind, (r, c) = BIG[name]
    half = _half_rows(name)
    if kind == "col":
        t = 128
        grid = (half // t,)
        g_spec = pl.BlockSpec((t, c), lambda i, cr: (cr[0] * (half // t) + i, 0))
        o_spec = pl.BlockSpec((t, c), lambda i, cr: (i, 0))
    else:
        t = half
        grid = (N_CHIPS,)
        g_spec = pl.BlockSpec((t, c), lambda i, cr: (2 * i + cr[0], 0))
        o_spec = pl.BlockSpec((t, c), lambda i, cr: (i, 0))

    def body(c_ref, g_ref, b_ref, o_ref):
        del c_ref
        o_ref[...] = (g_ref[...] + b_ref[...]).astype(o_ref.dtype)

    return pl.pallas_call(
        body, name="grad_add_" + name,
        grid_spec=pltpu.PrefetchScalarGridSpec(num_scalar_prefetch=1, grid=grid, in_specs=[g_spec, o_spec], out_specs=o_spec),
        out_shape=_compact_shape(name, BF16),
        compiler_params=pltpu.CompilerParams(dimension_semantics=("parallel",), vmem_limit_bytes=VMEM_LIMIT),
    )(core, grad, got)


def _piece(ref, name, chip):
    kind, _ = BIG[name]
    r, c = _shard_shape(name)
    if kind == "col":
        return ref.at[:, pl.ds(chip * c, c)]
    return ref.at[pl.ds(chip * (r // 2), r // 2), :]


def _landing_shape(name):
    r, c = _shard_shape(name)
    return (N_CHIPS - 1, r // 2, c)


def _exchange_start(parts, names, label):
    nw = len(names)
    ns = 2 * 3 * nw

    def body(*refs):
        ins, lands = refs[:nw], refs[nw:2 * nw]
        sems = refs[2 * nw:2 * nw + ns]
        token = refs[4 * nw + ns]
        _, _, c, _, _, others = _place()
        for w, name in enumerate(names):
            for t, (ox, oy) in enumerate(others):
                k = 2 * (3 * w + t)
                _remote(_piece(ins[w], name, 2 * ox + oy), lands[w].at[t], sems[k], sems[k + 1], (ox, oy, c)).start()
        token[...] = jnp.zeros_like(token)

    lands = [_keep_in_hbm(lax.empty(_landing_shape(n), BF16)) for n in names]
    res = pl.pallas_call(
        body, name=label,
        out_shape=(*[pltpu.SemaphoreType.DMA(())] * ns, *[pltpu.HBM(a.shape, a.dtype) for a in parts],
                   *[pltpu.HBM(a.shape, a.dtype) for a in lands], _sds((8, LANES), F32)),
        in_specs=[IN_HBM] * (2 * nw),
        out_specs=(*[SEM] * ns, *[IN_HBM] * (2 * nw), pl.BlockSpec(memory_space=pltpu.VMEM)),
        input_output_aliases={i: ns + i for i in range(2 * nw)},
        compiler_params=pltpu.CompilerParams(has_side_effects=DATAFLOW),
    )(*[_keep_in_hbm(a) for a in parts], *lands)
    return list(res[:ns]), list(res[ns:ns + nw]), list(res[ns + nw:ns + 2 * nw]), res[ns + 2 * nw]


def _exchange_wait(sems, parts, lands, names, after, label):
    nw = len(names)
    ns = len(sems)

    def body(*refs):
        ins, land_refs = refs[:nw], refs[nw:2 * nw]
        sem_refs = refs[2 * nw:2 * nw + ns]
        _, _, c, _, _, others = _place()
        for w, name in enumerate(names):
            for t, (ox, oy) in enumerate(others):
                k = 2 * (3 * w + t)
                cp = _remote(_piece(ins[w], name, 2 * ox + oy), land_refs[w].at[t], sem_refs[k], sem_refs[k + 1], (ox, oy, c))
                cp.wait_send()
                cp.wait_recv()

    res = pl.pallas_call(
        body, name=label,
        out_shape=tuple(pltpu.HBM(a.shape, a.dtype) for a in (*parts, *lands)),
        in_specs=[IN_HBM] * (2 * nw) + [SEM] * ns + [pl.BlockSpec(memory_space=pl.ANY)] * len(_several(after)),
        out_specs=tuple([IN_HBM] * (2 * nw)),
        input_output_aliases={i: i for i in range(2 * nw)},
        compiler_params=pltpu.CompilerParams(has_side_effects=DATAFLOW),
    )(*parts, *lands, *sems, *_several(after))
    return list(res[:nw]), list(res[nw:])


def _sum_chips(name, part, got, chip):
    kind, _ = BIG[name]
    _, r, c = got.shape
    t = _tile(r, (128, 352))
    if kind == "col":
        own = pl.BlockSpec((t, c), lambda i, ch: (i, ch[0]))
    else:
        own = pl.BlockSpec((t, c), lambda i, ch: (ch[0] * (r // t) + i, 0))

    def body(ch_ref, p_ref, g_ref, o_ref):
        del ch_ref
        acc = p_ref[...].astype(F32)
        for j in range(N_CHIPS - 1):
            acc = acc + g_ref[j].astype(F32)
        o_ref[...] = acc

    return _pallas(
        body, name="grad_sum_" + name, prefetch=1, grid=(r // t,),
        in_specs=[own, pl.BlockSpec((N_CHIPS - 1, t, c), lambda i, ch: (0, i, 0))],
        out_specs=pl.BlockSpec((t, c), lambda i, ch: (i, 0)),
        out_shape=_sds((r, c), F32), semantics=("parallel",),
    )(chip, part, got)


def _send_halves(sums, label, after):
    nw = len(sums)

    def body(*refs):
        ins, outs = refs[:nw], refs[nw + 1:2 * nw + 1]
        send_sems, recv_sems = refs[2 * nw + 1:]
        _, _, _, _, sibling, _ = _place()
        copies = [_remote(ins[w], outs[w], send_sems.at[w], recv_sems.at[w], sibling) for w in range(nw)]
        for cp in copies:
            cp.start()
        for cp in copies:
            cp.wait()

    return pl.pallas_call(
        body, name=label,
        out_shape=[_sds(a.shape, a.dtype) for a in sums],
        in_specs=[HBM] * (nw + 1), out_specs=[HBM] * nw,
        scratch_shapes=[pltpu.SemaphoreType.DMA((nw,)), pltpu.SemaphoreType.DMA((nw,))],
    )(*sums, after)


EARLY_GRADS = ("w_up", "w_down")
LAST_GRADS = ("w_in", "w_out")


def _reduce_finish(started, names, after, chip, tag):
    sems, parts, lands, _ = started
    parts, lands = _exchange_wait(sems, parts, lands, names, after, "grad_exchange_wait_" + tag)
    return [_sum_chips(n, parts[i], lands[i], chip) for i, n in enumerate(names)]


HI = lax.Precision.HIGHEST
MOD_COLS = 6 * D // N_CHIPS
COND_ROWS = 16


def _silu(v):
    return v * _sigmoid(v)


GATHER_ROWS = 8
FFW_COLS = 2 * DFF // N_CHIPS
CONV_COLS = DC // N_CHIPS
TAPS_PER_ROW = FFW_COLS // CONV_COLS
assert 4 + -(-CW // TAPS_PER_ROW) <= GATHER_ROWS


def _conv_tap_place(k):
    return 4 + k // TAPS_PER_ROW, (k % TAPS_PER_ROW) * CONV_COLS


def _pack_cond(c, ffn_w, conv_w):
    def body(c_ref, f_ref, w_ref, o_ref):
        o_ref[...] = jnp.zeros_like(o_ref)
        o_ref[0:1, 0:D] = c_ref[...]
        o_ref[1:4, :] = f_ref[...]
        for k in range(CW):
            row, lane = _conv_tap_place(k)
            o_ref[row:row + 1, lane:lane + CONV_COLS] = w_ref[k:k + 1, :]

    return _pallas(body, name="pack_cond", out_shape=_sds((GATHER_ROWS, FFW_COLS), F32))(c, ffn_w, conv_w)


def _unpack_cond(got, c_ctx):
    def body(g_ref, c_ref, cond_ref, f_ref, w_ref):
        cond_ref[...] = jnp.zeros_like(cond_ref)
        for d in range(8):
            cond_ref[d:d + 1, :] = g_ref[d * GATHER_ROWS:d * GATHER_ROWS + 1, 0:D]
        cond_ref[8:9, :] = c_ref[...]
        for j in range(N_CHIPS):
            r0 = 2 * j * GATHER_ROWS
            f_ref[:, j * FFW_COLS:(j + 1) * FFW_COLS] = g_ref[r0 + 1:r0 + 4, :]
            for k in range(CW):
                row, lane = _conv_tap_place(k)
                w_ref[k:k + 1, j * CONV_COLS:(j + 1) * CONV_COLS] = g_ref[r0 + row:r0 + row + 1, lane:lane + CONV_COLS]

    return _pallas(body, name="unpack_cond",
                   out_shape=[_sds((COND_ROWS, D), F32), _sds((3, 2 * DFF), F32), _sds((CW, DC), F32)])(got, c_ctx)


def _chip_cols(rows, width):
    return pl.BlockSpec((rows, width), lambda i, ch: (0, ch[0]))


def _whole(shape):
    return pl.BlockSpec(shape, lambda i, ch: (0,) * len(shape))


def _mod_shard(cond, w_mod, b_mod, chip):
    def body(ch_ref, c_ref, w_ref, b_ref, o_ref):
        del ch_ref
        o_ref[...] = jnp.dot(_silu(c_ref[...]), w_ref[...], preferred_element_type=F32, precision=HI) + b_ref[...]

    return _pallas(body, name="mod_fwd", prefetch=1, grid=(1,),
                   in_specs=[_whole((COND_ROWS, D)), _whole((D, MOD_COLS)), _chip_cols(1, MOD_COLS)],
                   out_specs=_whole((COND_ROWS, MOD_COLS)),
                   out_shape=_sds((COND_ROWS, MOD_COLS), F32))(chip, cond, w_mod, b_mod)


def _unpack_mod(mods, dev):
    def body(dev_ref, m_ref, me_ref, c_ref):
        rowi = lax.broadcasted_iota(jnp.int32, (COND_ROWS, MOD_COLS), 0)
        mine, ctx = [], []
        for j in range(N_CHIPS):
            blk = m_ref[2 * j * COND_ROWS:(2 * j + 1) * COND_ROWS, :]
            mine.append(jnp.sum(jnp.where(rowi == dev_ref[0], blk, 0.0), axis=0, keepdims=True))
            ctx.append(blk[8:9, :])
        mine = jnp.concatenate(mine, axis=1)
        ctx = jnp.concatenate(ctx, axis=1)
        for k in range(6):
            me_ref[k:k + 1, :] = mine[:, k * D:(k + 1) * D]
        for k in range(2):
            c_ref[k:k + 1, :] = ctx[:, k * D:(k + 1) * D]

    return _pallas(body, name="unpack_mod", prefetch=1, grid=(1,),
                   in_specs=[_whole(mods.shape)], out_specs=[_whole((6, D)), _whole((2, D))],
                   out_shape=[_sds((6, D), F32), _sds((2, D), F32)])(dev, mods)


MOD_TILE = 512


def _mod_weight_update(cond, dmod_all, w, m, v, chip):
    nt = MOD_COLS // MOD_TILE

    def body(ch_ref, c_ref, d_ref, w_ref, m_ref, v_ref, g_ref, dl_ref, nm_ref, nv_ref):
        del ch_ref
        g = lax.dot_general(_silu(c_ref[...]), d_ref[...], _TN, preferred_element_type=F32, precision=HI)
        g_ref[...] = g
        dl_ref[...], nm_ref[...], nv_ref[...] = _adam_math(w_ref[...], g, m_ref[...], v_ref[...])

    blk = pl.BlockSpec((D, MOD_TILE), lambda j, ch: (0, j))
    return _pallas(body, name="mod_weight_update", prefetch=1, grid=(nt,),
                   in_specs=[_whole((COND_ROWS, D)), pl.BlockSpec((COND_ROWS, MOD_TILE), lambda j, ch: (0, ch[0] * nt + j)),
                             blk, blk, blk],
                   out_specs=[blk] * 4, out_shape=[_sds((D, MOD_COLS), F32)] * 4,
                   semantics=("parallel",))(chip, cond, dmod_all, w, m, v)


def _cond_grad_partial(dmod_all, w_mod, chip):
    def body(ch_ref, d_ref, w_ref, o_ref):
        del ch_ref
        o_ref[...] = lax.dot_general(d_ref[...], w_ref[...], (((1,), (1,)), ((), ())), preferred_element_type=F32, precision=HI)

    return _pallas(body, name="cond_grad_partial", prefetch=1, grid=(1,),
                   in_specs=[pl.BlockSpec((8, MOD_COLS), lambda i, ch: (1, ch[0])), _whole((D, MOD_COLS))],
                   out_specs=_whole((8, D)), out_shape=_sds((8, D), F32))(chip, dmod_all, w_mod)


def _adam_math(w, g, m, v):
    nm = ADAM_B1 * m + (1.0 - ADAM_B1) * g
    nv = ADAM_B2 * v + (1.0 - ADAM_B2) * (g * g)
    c1 = 1.0 - ADAM_B1 ** ADAM_STEP
    c2 = 1.0 - ADAM_B2 ** ADAM_STEP
    return -ADAM_LR * ((nm / c1) / (jnp.sqrt(nv / c2) + ADAM_EPS) + ADAM_WD * w), nm, nv


def _cond_update(parts, c_ctx, m, v):
    def body(p_ref, c_ref, m_ref, v_ref, g_ref, d_ref, nm_ref, nv_ref):
        tot = p_ref[0:1, :]
        for j in range(1, N_CHIPS):
            tot = tot + p_ref[16 * j:16 * j + 1, :]
        cv = c_ref[...]
        sg = _sigmoid(cv)
        g = tot * (sg * (1.0 + cv * (1.0 - sg)))
        g_ref[...] = g
        d_ref[...], nm_ref[...], nv_ref[...] = _adam_math(cv, g, m_ref[...], v_ref[...])

    return _pallas(body, name="cond_update", out_shape=[_sds((1, D), F32)] * 4)(parts, c_ctx, m, v)


def _adamw_cols(w, g_all, m, v, chip, name):
    r, c = w.shape

    def body(ch_ref, w_ref, g_ref, m_ref, v_ref, go_ref, d_ref, nm_ref, nv_ref):
        del ch_ref
        g = g_ref[...]
        go_ref[...] = g
        d_ref[...], nm_ref[...], nv_ref[...] = _adam_math(w_ref[...], g, m_ref[...], v_ref[...])

    return _pallas(body, name=name, prefetch=1, grid=(1,),
                   in_specs=[_whole((r, c)), _chip_cols(r, c), _whole((r, c)), _whole((r, c))],
                   out_specs=[_whole((r, c))] * 4, out_shape=[_sds((r, c), F32)] * 4)(chip, w, g_all, m, v)


def _adamw_halves(name, w, own, other, m, v, core, after):
    r, c = w.shape
    half = r // 2
    t = _tile(half, (128, 352))
    nh = half // t

    def pick(mine):
        def index(i, cr):
            first = cr[0] if mine else 1 - cr[0]
            return (jnp.clip(i - first * nh, 0, nh - 1), 0)
        return pl.BlockSpec((t, c), index)

    def body(c_ref, w_ref, own_ref, oth_ref, m_ref, v_ref, after_ref, g_ref, d_ref, nm_ref, nv_ref):
        del after_ref
        g = jnp.where(pl.program_id(0) // nh == c_ref[0], own_ref[...], oth_ref[...])
        g_ref[...] = g
        d_ref[...], nm_ref[...], nv_ref[...] = _adam_math(w_ref[...], g, m_ref[...], v_ref[...])

    blk = pl.BlockSpec((t, c), lambda i, cr: (i, 0))
    return _pallas(body, name="adamw_" + name, prefetch=1, grid=(2 * nh,),
                   in_specs=[blk, pick(True), pick(False), blk, blk, pl.BlockSpec(memory_space=pl.ANY)], out_specs=[blk] * 4,
                   out_shape=[_sds((r, c), F32)] * 4, semantics=("parallel",))(core, w, own, other, m, v, after)


WEIGHTS = ("c_ctx", "w_mod", "b_mod", "g_norm1", "w_in", "rpb", "conv_w", "conv_b", "ln_g", "ln_b", "w_out", "g_norm2",
           "w_up", "ffn_conv_w", "ffn_conv_b", "w_down", "g_final")
PACK = (("dmod", 6 * D), ("dmod_c", 2 * D), ("g_norm1", D), ("g_norm1_ctx", D), ("g_norm2", D), ("g_final", D),
        ("conv_b", DC), ("ln_g", DC), ("ln_b", DC), ("ffn_conv_b", 2 * DFF), ("ffn_conv_w", 3 * 2 * DFF),
        ("conv_w", CW * DC), ("rpb_rev", NH * 16 * LANES), ("loss", LANES))
PACK_OFF = {}
_o = 0
for _n, _w in PACK:
    PACK_OFF[_n] = (_o, _w)
    _o += _w
PACK_N = -(-_o // (8 * LANES)) * (8 * LANES)
VECTORS = {"b_mod": (6 * D, ("dmod", "dmod_c")), "g_norm1": (D, ("g_norm1", "g_norm1_ctx")), "conv_b": (DC, ("conv_b",)),
           "ln_g": (DC, ("ln_g",)), "ln_b": (DC, ("ln_b",)), "g_norm2": (D, ("g_norm2",)),
           "ffn_conv_b": (2 * DFF, ("ffn_conv_b",)), "g_final": (D, ("g_final",))}
RPB_COLS = 4 * NA_ROWS - 1


def _pack_small(parts, after):
    arrs, places = [], []
    for name, _ in PACK:
        off, width = PACK_OFF[name]
        group = parts[name]
        rows = group[0].shape[0]
        row_w = sum(a.shape[1] for a in group)
        assert rows * row_w == width, (name, rows, row_w, width)
        col = 0
        for a in group:
            arrs.append(a)
            places.append([off + k * row_w + col for k in range(rows)])
            col += a.shape[1]

    def body(*refs):
        o_ref = refs[-1]
        o_ref[:, _o:PACK_N] = jnp.zeros((1, PACK_N - _o), F32)
        for ref, offs in zip(refs, places):
            n = ref.shape[1]
            for k, off in enumerate(offs):
                o_ref[:, off:off + n] = ref[k:k + 1, :]

    vmem = pl.BlockSpec(memory_space=pltpu.VMEM)
    return _pallas(body, name="pack_small_grads", out_shape=_sds((1, PACK_N), F32),
                   in_specs=[vmem] * len(arrs) + [pl.BlockSpec(memory_space=pl.ANY)] * len(_several(after)),
                   out_specs=vmem)(*arrs, *_several(after))


def _small_update(packs, w, m, v):
    names = list(VECTORS)

    def body(*refs):
        it = iter(refs)
        p_ref = next(it)
        wmv = {n: (next(it), next(it), next(it)) for n in names}
        outs = {n: (next(it), next(it), next(it), next(it)) for n in names}
        dmod_ref, cw_ref, fw_ref, rpb_ref, loss_ref = next(it), next(it), next(it), next(it), next(it)

        def total(name):
            off, width = PACK_OFF[name]
            acc = p_ref[0:1, off:off + width]
            for d in range(1, 8):
                acc = acc + p_ref[d:d + 1, off:off + width]
            return acc

        for n in names:
            width, segs = VECTORS[n]
            g = total(segs[0])
            if len(segs) > 1:
                extra = total(segs[1])
                ew = extra.shape[1]
                g = g + extra if ew == width else jnp.concatenate([g[:, :ew] + extra, g[:, ew:]], axis=1)
            w_ref, m_ref, v_ref = wmv[n]
            g_ref, d_ref, nm_ref, nv_ref = outs[n]
            g_ref[...] = g
            d_ref[...], nm_ref[...], nv_ref[...] = _adam_math(w_ref[...], g, m_ref[...], v_ref[...])

        o_dmod = PACK_OFF["dmod"][0]
        dmod_ref[...] = jnp.zeros_like(dmod_ref)
        dmod_ref[0:8, :] = p_ref[:, o_dmod:o_dmod + 6 * D]
        dmod_ref[8:9, 0:2 * D] = total("dmod_c")
        for ref, name, rows in ((cw_ref, "conv_w", CW), (fw_ref, "ffn_conv_w", 3), (rpb_ref, "rpb_rev", NH * 16)):
            flat = total(name)
            n = ref.shape[1]
            for k in range(rows):
                ref[k:k + 1, :] = flat[:, k * n:(k + 1) * n]
        loss_ref[...] = total("loss")

    ins = [packs] + [a[n] for n in names for a in (w, m, v)]
    out_shape = [_sds((1, VECTORS[n][0]), F32) for n in names for _ in range(4)]
    out_shape += [_sds((COND_ROWS, 6 * D), F32), _sds((CW, DC), F32), _sds((3, 2 * DFF), F32), _sds((NH * 16, LANES), F32),
                  _sds((1, LANES), F32)]
    res = _pallas(body, name="small_update", out_shape=out_shape)(*ins)
    per = {n: tuple(res[4 * i:4 * i + 4]) for i, n in enumerate(names)}
    return (per, *res[4 * len(names):])


def _rpb_update(rev, w, m, v):
    def body(r_ref, w_ref, m_ref, v_ref, g_ref, d_ref, nm_ref, nv_ref):
        li = lax.broadcasted_iota(jnp.int32, (LANES, LANES), 0)
        co = lax.broadcasted_iota(jnp.int32, (LANES, LANES), 1)
        lane_of_co0 = GW - 1 + RPB_COLS // 2
        unflip = jnp.where((li == lane_of_co0 - co) & (co < RPB_COLS), 1.0, 0.0).astype(F32)
        g_all = jnp.dot(r_ref[...], unflip, preferred_element_type=F32, precision=HI)
        nr = 2 * NA_ROWS - 1
        for h in range(NH):
            g = g_all[h * 16:h * 16 + nr, 0:RPB_COLS]
            g_ref[0, h] = g
            d_ref[0, h], nm_ref[0, h], nv_ref[0, h] = _adam_math(w_ref[0, h], g, m_ref[0, h], v_ref[0, h])

    return _pallas(body, name="rpb_update", out_shape=[_sds(w.shape, F32)] * 4)(rev, w, m, v)


def kernel(x, c, ctx, c_ctx, w_mod, b_mod, g_norm1, w_in, rpb, conv_w, conv_b, ln_g, ln_b, w_out, g_norm2, w_up, ffn_conv_w, ffn_conv_b, w_down, g_final, loss_target, m_c_ctx, m_w_mod, m_b_mod, m_g_norm1, m_w_in, m_rpb, m_conv_w, m_conv_b, m_ln_g, m_ln_b, m_w_out, m_g_norm2, m_w_up, m_ffn_conv_w, m_ffn_conv_b, m_w_down, m_g_final, v_c_ctx, v_w_mod, v_b_mod, v_g_norm1, v_w_in, v_rpb, v_conv_w, v_conv_b, v_ln_g, v_ln_b, v_w_out, v_g_norm2, v_w_up, v_ffn_conv_w, v_ffn_conv_b, v_w_down, v_g_final):
    w = dict(c_ctx=c_ctx, w_mod=w_mod, b_mod=b_mod, g_norm1=g_norm1, w_in=w_in, rpb=rpb, conv_w=conv_w, conv_b=conv_b,
             ln_g=ln_g, ln_b=ln_b, w_out=w_out, g_norm2=g_norm2, w_up=w_up, ffn_conv_w=ffn_conv_w, ffn_conv_b=ffn_conv_b,
             w_down=w_down, g_final=g_final)
    mom = dict(c_ctx=m_c_ctx, w_mod=m_w_mod, b_mod=m_b_mod, g_norm1=m_g_norm1, w_in=m_w_in, rpb=m_rpb, conv_w=m_conv_w,
               conv_b=m_conv_b, ln_g=m_ln_g, ln_b=m_ln_b, w_out=m_w_out, g_norm2=m_g_norm2, w_up=m_w_up,
               ffn_conv_w=m_ffn_conv_w, ffn_conv_b=m_ffn_conv_b, w_down=m_w_down, g_final=m_g_final)
    var = dict(c_ctx=v_c_ctx, w_mod=v_w_mod, b_mod=v_b_mod, g_norm1=v_g_norm1, w_in=v_w_in, rpb=v_rpb, conv_w=v_conv_w,
               conv_b=v_conv_b, ln_g=v_ln_g, ln_b=v_ln_b, w_out=v_w_out, g_norm2=v_g_norm2, w_up=v_w_up,
               ffn_conv_w=v_ffn_conv_w, ffn_conv_b=v_ffn_conv_b, w_down=v_w_down, g_final=v_g_final)
    xi, yi, ci = lax.axis_index("x"), lax.axis_index("y"), lax.axis_index("c")
    dev = (4 * xi + 2 * yi + ci).astype(jnp.int32).reshape(1)
    chip = (2 * xi + yi).astype(jnp.int32).reshape(1)
    core = ci.astype(jnp.int32).reshape(1)
    c_ctx2 = c_ctx.reshape(1, D)
    g_final2 = g_final.reshape(1, D)
    mom["g_final"], var["g_final"] = m_g_final.reshape(1, D), v_g_final.reshape(1, D)

    sharing_cond = _share_start(_pack_cond(c, ffn_conv_w[0], conv_w[0]), "cond", after=[])
    shards = {n: _cast_into_whole(n, w[n][0], chip) for n in BIG_NAMES}
    cond, ffn_w_all, conv_w_all = _unpack_cond(_share_wait(sharing_cond, list(shards.values()), "cond"), c_ctx2)

    mods = _gather_small(_mod_shard(cond, w_mod[0], b_mod, chip), "gather_mod")
    mod_me, mod_c = _unpack_mod(mods, dev)

    sems_in, first, token_in = _gather_start([shards["w_in"]], ("w_in",), mod_me, "w_in")
    sems, late, token = _gather_start([shards[n] for n in LATE_NAMES], LATE_NAMES, token_in, "late")
    mod_me = mod_me + token[0:1, 0:1]

    def w_in_all(after):
        arrived = _gather_wait(sems_in, first, ("w_in",), after, "w_in")
        return _forward_halves(list(arrived), ("w_in",), "w_in")[0]

    def late_weights(after):
        arrived = list(_gather_wait(sems, late, LATE_NAMES, after, "late"))
        (w_out_all,) = _forward_halves(arrived[:1], LATE_NAMES[:1], "w_out")
        fsems, passing, _ = _forward_start(arrived[1:], LATE_NAMES[1:], "ffn", after=w_out_all)
        return w_out_all, lambda after2: _forward_wait(fsems, passing, LATE_NAMES[1:], after2, "ffn")

    rpb_rev = jnp.pad(rpb[0][:, :, ::-1], ((0, 0), (0, 1), (48, LANES - 48 - RPB_COLS))).reshape(NH * 16, LANES)
    vec = dict(g_norm1=g_norm1, g_norm2=g_norm2, g_final=g_final2, conv_w=conv_w_all, conv_b=conv_b, ln_g=ln_g, ln_b=ln_b,
               ffn_conv_w=ffn_w_all, ffn_conv_b=ffn_conv_b)
    started = []

    def begin_early(d_up, d_down):
        started.append(_swap_start([d_up, d_down], EARLY_GRADS, "grad_swap_start_early"))

    def carry_on_early(after):
        sems_, grads_, lands_, _ = started.pop()
        grads_, lands_ = _swap_wait(sems_, grads_, lands_, EARLY_GRADS, after, "grad_swap_wait_early")
        parts_ = [_add_halves(n, grads_[i], lands_[i], core) for i, n in enumerate(EARLY_GRADS)]
        started.append(_exchange_start(parts_, EARLY_GRADS, "grad_exchange_start_early"))
        return started[0][3][0:1, 0:1]

    loss_p, grad_x, d_in, d_out, d_up, d_down, small = _local_step(
        x[0], ctx[0], loss_target[0], mod_me, mod_c, vec, w_in_all, late_weights, rpb_rev, (begin_early, carry_on_early))

    out = {}
    sems_, grads_, lands_, _ = _swap_start([d_in, d_out], LAST_GRADS, "grad_swap_start_last")
    early_own = _reduce_finish(started[0], EARLY_GRADS, grad_x, chip, "early")
    parts = dict(dmod=small["dmod"], dmod_c=small["dmod_c"], g_norm1=[small["g_norm1"][0]], g_norm1_ctx=[small["g_norm1"][1]],
                 g_norm2=[small["g_norm2"]], g_final=[small["g_final"]], conv_b=[small["conv_b"]], ln_g=[small["ln_g"]],
                 ln_b=[small["ln_b"]], ffn_conv_b=small["ffn_conv_b"], ffn_conv_w=small["ffn_conv_w"],
                 conv_w=[small["conv_w"]], rpb_rev=[small["rpb_rev"]], loss=[loss_p])
    pack = _pack_small(parts, after=early_own).reshape(8, PACK_N // 8)
    sharing = _share_start(pack, "small_grads", after=[])
    grads_, lands_ = _swap_wait(sems_, grads_, lands_, LAST_GRADS, sharing[2], "grad_swap_wait_last")
    parts_ = [_add_halves(n, grads_[i], lands_[i], core) for i, n in enumerate(LAST_GRADS)]
    last_started = _exchange_start(parts_, LAST_GRADS, "grad_exchange_start_last")
    early_other = _send_halves(early_own, "grad_send_early", after=last_started[3])
    for i, n in enumerate(EARLY_GRADS):
        out[n] = _adamw_halves(n, w[n][0], early_own[i], early_other[i], mom[n][0], var[n][0], core, early_other[i])

    packs = _share_wait(sharing, [out[n][1] for n in EARLY_GRADS], "small_grads").reshape(8, PACK_N)
    w2 = dict(w, g_final=g_final2)
    per, dmod_all, g_conv_w_all, g_ffn_w_all, g_rpb_rev, loss_row = _small_update(packs, w2, mom, var)
    out.update(per)
    out["w_mod"] = _mod_weight_update(cond, dmod_all, w_mod[0], m_w_mod[0], v_w_mod[0], chip)

    sharing_c = _share_start(_cond_grad_partial(dmod_all, w_mod[0], chip), "cond_grad", after=out["w_mod"][1])
    last_own = _reduce_finish(last_started, LAST_GRADS, sharing_c[2], chip, "last")
    last_other = _send_halves(last_own, "grad_send_last", after=last_own[0])
    for i, n in enumerate(LAST_GRADS):
        out[n] = _adamw_halves(n, w[n][0], last_own[i], last_other[i], mom[n][0], var[n][0], core, last_other[i])
    out["c_ctx"] = _cond_update(_share_wait(sharing_c, [out[n][1] for n in LAST_GRADS], "cond_grad"),
                                c_ctx2, m_c_ctx.reshape(1, D), v_c_ctx.reshape(1, D))
    out["conv_w"] = _adamw_cols(conv_w[0], g_conv_w_all, m_conv_w[0], v_conv_w[0], chip, "adamw_conv_w")
    out["ffn_conv_w"] = _adamw_cols(ffn_conv_w[0], g_ffn_w_all, m_ffn_conv_w[0], v_ffn_conv_w[0], chip, "adamw_ffn_conv_w")
    out["rpb"] = _rpb_update(g_rpb_rev, rpb, m_rpb, v_rpb)

    res = [[out[n][k].reshape(w[n].shape) for n in WEIGHTS] for k in range(4)]
    return (loss_row[0, 0], grad_x[None], *res[0], *res[1], *res[2], *res[3])
```

```python
import jax
import jax.numpy as jnp
from jax import lax
from jax.experimental import pallas as pl
from jax.experimental.pallas import tpu as pltpu

F32 = jnp.float32
BF16 = jnp.bfloat16
MXU_DTYPE = jnp.bfloat16

D = 1024
CTX = 256
GW = 64
DA = 512
NH = 8
HD = 64
DC = 512
CW = 31
DFF = 2816
NIN = 3 * DA + 2 * DC
EPS = 1e-6
SCALE = HD ** -0.5
NEG = -1e30
NA_ROWS = 8
PAIR_ROWS = NA_ROWS + 1
TAB_BLOCKS = 17
LANES = 128
VMEM_LIMIT = 56 * 1024 * 1024

ADAM_LR = 0.001
ADAM_B1 = 0.9
ADAM_B2 = 0.999
ADAM_EPS = 1e-08
ADAM_WD = 0.01
ADAM_STEP = 10

MESH = pl.DeviceIdType.MESH


def _pallas(body, *, name, semantics=None, vmem=VMEM_LIMIT, prefetch=0, **kw):
    params = dict(vmem_limit_bytes=vmem)
    if semantics is not None:
        params["dimension_semantics"] = semantics
    if prefetch:
        kw["grid_spec"] = pltpu.PrefetchScalarGridSpec(
            num_scalar_prefetch=prefetch, grid=kw.pop("grid"), in_specs=kw.pop("in_specs"), out_specs=kw.pop("out_specs"),
            scratch_shapes=kw.pop("scratch_shapes", ()))
    return pl.pallas_call(body, name=name, compiler_params=pltpu.CompilerParams(**params), **kw)


def _sds(shape, dtype):
    return jax.ShapeDtypeStruct(shape, dtype)


def _vec_spec(n):
    return pl.BlockSpec((1, n), lambda *_: (0, 0))


def _colsum8(x):
    t, n = x.shape
    return jnp.sum(x.reshape(t // 8, 8, n), axis=0)


def _sigmoid(x):
    return 0.5 * jnp.tanh(0.5 * x) + 0.5


def _mm(a, b, *, mode, m, n, k, tm, tn, tk, out_dtype, name, a_off=(0, 0), b_off=(0, 0),
        out_total=None, o_off=(0, 0), into=None):
    a_list = list(a) if isinstance(a, (list, tuple)) else [a]
    b_list = list(b) if isinstance(b, (list, tuple)) else [b]
    assert m % tm == 0 and n % tn == 0 and k % tk == 0, (name, m, n, k, tm, tn, tk)
    gi, gj, nk = m // tm, n // tn, k // tk
    dims = {"nn": (((1,), (0,)), ((), ())), "nt": (((1,), (1,)), ((), ())), "tn": (((0,), (0,)), ((), ()))}[mode]

    if len(a_list) > 1:
        assert mode != "tn" and nk == 1 and sum(x.shape[1] for x in a_list) == k
        a_specs = [pl.BlockSpec((tm, x.shape[1]), lambda i, j, kk: (i, 0)) for x in a_list]
    elif mode == "tn":
        a_specs = [pl.BlockSpec((tk, tm), lambda i, j, kk: (kk + a_off[0], i + a_off[1]))]
    else:
        a_specs = [pl.BlockSpec((tm, tk), lambda i, j, kk: (i + a_off[0], kk + a_off[1]))]
    if len(b_list) > 1:
        assert mode == "tn" and gj == 1 and sum(x.shape[1] for x in b_list) == n
        b_specs = [pl.BlockSpec((tk, x.shape[1]), lambda i, j, kk: (kk, 0)) for x in b_list]
    elif mode == "nt":
        b_specs = [pl.BlockSpec((tn, tk), lambda i, j, kk: (j + b_off[0], kk + b_off[1]))]
    else:
        b_specs = [pl.BlockSpec((tk, tn), lambda i, j, kk: (kk + b_off[0], j + b_off[1]))]

    na, nb = len(a_list), len(b_list)
    in_place = nk > 1 and out_dtype == F32
    n_in = na + nb + (into is not None)

    def body(*refs):
        a_refs, b_refs, o_ref = refs[:na], refs[na:na + nb], refs[n_in]
        acc = o_ref if in_place else (refs[n_in + 1] if nk > 1 else None)
        kk = pl.program_id(2)

        def whole(piece_refs):
            vals = [r[...].astype(MXU_DTYPE) for r in piece_refs]
            return vals[0] if len(vals) == 1 else jnp.concatenate(vals, axis=1)

        p = lax.dot_general(whole(a_refs), whole(b_refs), dims, preferred_element_type=F32)
        if nk == 1:
            o_ref[...] = p.astype(out_dtype)
            return

        @pl.when(kk == 0)
        def _():
            acc[...] = p

        @pl.when(kk > 0)
        def _():
            acc[...] += p

        if not in_place:
            @pl.when(kk == nk - 1)
            def _():
                o_ref[...] = acc[...].astype(out_dtype)

    ins = [*a_list, *b_list]
    in_specs = a_specs + b_specs
    extra = {}
    if into is not None:
        extra["input_output_aliases"] = {len(ins): 0}
        ins.append(into)
        in_specs.append(pl.BlockSpec(memory_space=pl.ANY))
    return _pallas(
        body, name=name, grid=(gi, gj, nk), in_specs=in_specs,
        out_specs=pl.BlockSpec((tm, tn), lambda i, j, kk: (i + o_off[0], j + o_off[1])),
        out_shape=_sds(out_total or (m, n), out_dtype),
        scratch_shapes=[pltpu.VMEM((tm, tn), F32)] if nk > 1 and not in_place else [],
        semantics=("parallel", "parallel", "arbitrary"), **extra,
    )(*ins)


ROW_TILE = 256


def _row_tile(s):
    return 2 * ROW_TILE if s % (2 * ROW_TILE) == 0 else ROW_TILE


def _rmsmod_fwd(x, ctx, g, sc, sh, csc, csh):
    s = x.shape[0]
    nt = s // ROW_TILE
    assert ctx.shape[0] == ROW_TILE

    def body(x_ref, c_ref, g_ref, sc_ref, sh_ref, csc_ref, csh_ref, o_ref):
        is_ctx = pl.program_id(0) == nt
        xv = jnp.where(is_ctx, c_ref[...], x_ref[...])
        scv = jnp.where(is_ctx, csc_ref[...], sc_ref[...])
        shv = jnp.where(is_ctx, csh_ref[...], sh_ref[...])
        r = lax.rsqrt(jnp.mean(xv * xv, axis=-1, keepdims=True) + EPS)
        y = xv * r * g_ref[...]
        o_ref[...] = (y * (1.0 + scv) + shv).astype(o_ref.dtype)

    return _pallas(
        body, name="rmsmod1_fwd", grid=(nt + 1,),
        in_specs=[pl.BlockSpec((ROW_TILE, D), lambda i: (jnp.minimum(i, nt - 1), 0)),
                  pl.BlockSpec((ROW_TILE, D), lambda i: (0, 0))] + [_vec_spec(D)] * 5,
        out_specs=pl.BlockSpec((ROW_TILE, D), lambda i: (i, 0)),
        out_shape=_sds((s + CTX, D), MXU_DTYPE),
        semantics=("arbitrary",),
    )(x, ctx, g, sc, sh, csc, csh)


def _resid_rmsmod_fwd(x, y, gt, g, sc, sh):
    s = x.shape[0]

    def body(x_ref, y_ref, gt_ref, g_ref, sc_ref, sh_ref, x1_ref, h_ref):
        x1 = x_ref[...] + gt_ref[...] * y_ref[...]
        x1_ref[...] = x1
        r = lax.rsqrt(jnp.mean(x1 * x1, axis=-1, keepdims=True) + EPS)
        h_ref[...] = ((x1 * r * g_ref[...]) * (1.0 + sc_ref[...]) + sh_ref[...]).astype(h_ref.dtype)

    t = _row_tile(s)
    row = pl.BlockSpec((t, D), lambda i: (i, 0))
    return _pallas(
        body, name="resid_rmsmod2_fwd", grid=(s // t,),
        in_specs=[row, row] + [_vec_spec(D)] * 4,
        out_specs=[row, row],
        out_shape=[_sds((s, D), F32), _sds((s, D), MXU_DTYPE)],
        semantics=("parallel",),
    )(x, y, gt, g, sc, sh)


def _final_fwd_bwd(x1, z, gt2, gf, tgt):
    s = x1.shape[0]
    tile = _row_tile(s)
    nt = s // tile

    def body(x1_ref, z_ref, gt_ref, gf_ref, t_ref, dx2_ref, dz_ref, loss_ref, dgt_ref, dgf_ref, a_loss, a_gt, a_gf):
        i = pl.program_id(0)

        @pl.when(i == 0)
        def _():
            a_loss[...] = jnp.zeros_like(a_loss)
            a_gt[...] = jnp.zeros_like(a_gt)
            a_gf[...] = jnp.zeros_like(a_gf)

        zv = z_ref[...]
        gt = gt_ref[...]
        gf_ = gf_ref[...]
        x2 = x1_ref[...] + gt * zv
        r = lax.rsqrt(jnp.mean(x2 * x2, axis=-1, keepdims=True) + EPS)
        xn = x2 * r
        e = xn * gf_ - t_ref[...]
        a_loss[...] += _colsum8(e * e)
        dyo = e * (1.0 / D)
        a_gf[...] += _colsum8(dyo * xn)
        gdy = gf_ * dyo
        dx2 = r * gdy - xn * (r * r) * jnp.mean(x2 * gdy, axis=-1, keepdims=True)
        dx2_ref[...] = dx2
        dz_ref[...] = (gt * dx2).astype(dz_ref.dtype)
        a_gt[...] += _colsum8(dx2 * zv)

        @pl.when(i == nt - 1)
        def _():
            tot = jnp.sum(jnp.sum(a_loss[...], axis=0, keepdims=True), axis=1, keepdims=True) * (0.5 / D)
            loss_ref[...] = jnp.broadcast_to(tot, loss_ref.shape)
            dgt_ref[...] = jnp.sum(a_gt[...], axis=0, keepdims=True)
            dgf_ref[...] = jnp.sum(a_gf[...], axis=0, keepdims=True)

    row = pl.BlockSpec((tile, D), lambda i: (i, 0))
    return _pallas(
        body, name="final_norm_loss", grid=(nt,),
        in_specs=[row, row, _vec_spec(D), _vec_spec(D), row],
        out_specs=[row, row, _vec_spec(LANES), _vec_spec(D), _vec_spec(D)],
        out_shape=[_sds((s, D), F32), _sds((s, D), MXU_DTYPE), _sds((1, LANES), F32), _sds((1, D), F32), _sds((1, D), F32)],
        scratch_shapes=[pltpu.VMEM((8, D), F32)] * 3,
        semantics=("arbitrary",),
    )(x1, z, gt2, gf, tgt)


def _rmsmod_bwd(xin, dh, g, sc, *, name, dh_row0=0, add=None, resid=None):
    s = xin.shape[0]
    tile = _row_tile(s)
    nt = s // tile
    want_dx = add is not None
    assert resid is None or want_dx

    def body(*refs):
        it = iter(refs)
        x_ref, dh_ref, g_ref, sc_ref = next(it), next(it), next(it), next(it)
        add_ref = next(it) if want_dx else None
        gt_ref, y_ref = (next(it), next(it)) if resid is not None else (None, None)
        dsh_ref, dsc_ref, dg_ref = next(it), next(it), next(it)
        dx_ref = next(it) if want_dx else None
        dy_ref, dgt_ref = (next(it), next(it)) if resid is not None else (None, None)
        a_sh, a_sc, a_g = next(it), next(it), next(it)
        a_gt = next(it) if resid is not None else None
        i = pl.program_id(0)

        @pl.when(i == 0)
        def _():
            a_sh[...] = jnp.zeros_like(a_sh)
            a_sc[...] = jnp.zeros_like(a_sc)
            a_g[...] = jnp.zeros_like(a_g)
            if a_gt is not None:
                a_gt[...] = jnp.zeros_like(a_gt)

        xv = x_ref[...]
        dhv = dh_ref[...]
        gv = g_ref[...]
        r = lax.rsqrt(jnp.mean(xv * xv, axis=-1, keepdims=True) + EPS)
        xn = xv * r
        a_sh[...] += _colsum8(dhv)
        a_sc[...] += _colsum8(dhv * (xn * gv))
        dn = dhv * (1.0 + sc_ref[...])
        a_g[...] += _colsum8(dn * xn)
        if want_dx:
            gdn = gv * dn
            dx = add_ref[...] + r * gdn - xn * (r * r) * jnp.mean(xv * gdn, axis=-1, keepdims=True)
            dx_ref[...] = dx
            if resid is not None:
                dy_ref[...] = (gt_ref[...] * dx).astype(dy_ref.dtype)
                a_gt[...] += _colsum8(dx * y_ref[...])

        @pl.when(i == nt - 1)
        def _():
            dsh_ref[...] = jnp.sum(a_sh[...], axis=0, keepdims=True)
            dsc_ref[...] = jnp.sum(a_sc[...], axis=0, keepdims=True)
            dg_ref[...] = jnp.sum(a_g[...], axis=0, keepdims=True)
            if a_gt is not None:
                dgt_ref[...] = jnp.sum(a_gt[...], axis=0, keepdims=True)

    row = pl.BlockSpec((tile, D), lambda i: (i, 0))
    ins = [xin, dh, g, sc]
    in_specs = [row, pl.BlockSpec((tile, D), lambda i: (i + dh_row0 // tile, 0)), _vec_spec(D), _vec_spec(D)]
    out_specs = [_vec_spec(D)] * 3
    out_shape = [_sds((1, D), F32)] * 3
    scratch = [pltpu.VMEM((8, D), F32)] * 3
    if want_dx:
        ins.append(add)
        in_specs.append(row)
        out_specs.append(row)
        out_shape.append(_sds((s, D), F32))
    if resid is not None:
        ins += [resid[0], resid[1]]
        in_specs += [_vec_spec(D), row]
        out_specs += [row, _vec_spec(D)]
        out_shape += [_sds((s, D), MXU_DTYPE), _sds((1, D), F32)]
        scratch.append(pltpu.VMEM((8, D), F32))
    return _pallas(body, name=name, grid=(nt,), in_specs=in_specs, out_specs=out_specs, out_shape=out_shape,
                   scratch_shapes=scratch, semantics=("arbitrary",))(*ins)


FF_TILE = 128
FF_CHUNK = 128
HALO = 8


def _shift3(pad_ref, r0, ch):
    return tuple(pad_ref[pl.ds(r0 + HALO + d, ch), :] for d in (-1, 0, 1))


def _fill_padded(pad_ref, src_ref, s, ch, halo):
    zeros = jnp.zeros((halo, pad_ref.shape[1]), F32)
    pad_ref[0:halo, :] = zeros
    pad_ref[s + halo:s + 2 * halo, :] = zeros

    def cp(c, carry):
        r0 = pl.multiple_of(c * ch, ch)
        pad_ref[pl.ds(r0 + halo, ch), :] = src_ref[pl.ds(r0, ch), :].astype(F32)
        return carry

    lax.fori_loop(0, s // ch, cp, 0)


def _ffn_act_fwd(u, w, b):
    s = u.shape[0]
    nj = DFF // FF_TILE
    ch = FF_CHUNK

    def body(ug_ref, uv_ref, wg_ref, wv_ref, bg_ref, bv_ref, f_ref, gpad, vpad):
        _fill_padded(gpad, ug_ref, s, ch, HALO)
        _fill_padded(vpad, uv_ref, s, ch, HALO)

        def conv(pad, w_ref, b_ref, r0):
            prev, cur, nxt = _shift3(pad, r0, ch)
            return w_ref[0:1, :] * prev + w_ref[1:2, :] * cur + w_ref[2:3, :] * nxt + b_ref[...]

        def step(c, carry):
            r0 = pl.multiple_of(c * ch, ch)
            gc = conv(gpad, wg_ref, bg_ref, r0)
            vc = conv(vpad, wv_ref, bv_ref, r0)
            f_ref[pl.ds(r0, ch), :] = (gc * _sigmoid(gc) * vc).astype(f_ref.dtype)
            return carry

        lax.fori_loop(0, s // ch, step, 0)

    col = lambda off: pl.BlockSpec((s, FF_TILE), lambda j: (0, j + off))
    wsp = lambda off: pl.BlockSpec((3, FF_TILE), lambda j: (0, j + off))
    bsp = lambda off: pl.BlockSpec((1, FF_TILE), lambda j: (0, j + off))
    return _pallas(
        body, name="ffn_act_fwd", grid=(nj,),
        in_specs=[col(0), col(nj), wsp(0), wsp(nj), bsp(0), bsp(nj)],
        out_specs=col(0), out_shape=_sds((s, DFF), MXU_DTYPE),
        scratch_shapes=[pltpu.VMEM((s + 2 * HALO, FF_TILE), F32)] * 2,
        semantics=("parallel",),
    )(u, u, w, w, b, b)


def _ffn_act_bwd(u, df, w, b):
    s = u.shape[0]
    nj = DFF // FF_TILE
    ch = FF_CHUNK

    def body(ug_ref, uv_ref, df_ref, wg_ref, wv_ref, bg_ref, bv_ref,
             dug_ref, duv_ref, dwg_ref, dwv_ref, dbg_ref, dbv_ref, gpad, vpad, dgpad, dvpad, acc):
        _fill_padded(gpad, ug_ref, s, ch, HALO)
        _fill_padded(vpad, uv_ref, s, ch, HALO)
        zeros = jnp.zeros((HALO, FF_TILE), F32)
        for p in (dgpad, dvpad):
            p[0:HALO, :] = zeros
            p[s + HALO:s + 2 * HALO, :] = zeros
        acc[...] = jnp.zeros_like(acc)

        def step(c, carry):
            r0 = pl.multiple_of(c * ch, ch)
            gs = _shift3(gpad, r0, ch)
            vs = _shift3(vpad, r0, ch)
            gc = wg_ref[0:1, :] * gs[0] + wg_ref[1:2, :] * gs[1] + wg_ref[2:3, :] * gs[2] + bg_ref[...]
            vc = wv_ref[0:1, :] * vs[0] + wv_ref[1:2, :] * vs[1] + wv_ref[2:3, :] * vs[2] + bv_ref[...]
            sg = _sigmoid(gc)
            dfv = df_ref[pl.ds(r0, ch), :].astype(F32)
            dgc = dfv * vc * (sg * (1.0 + gc * (1.0 - sg)))
            dvc = dfv * (gc * sg)
            dgpad[pl.ds(r0 + HALO, ch), :] = dgc
            dvpad[pl.ds(r0 + HALO, ch), :] = dvc
            for t in range(3):
                acc[8 * t:8 * t + 8, :] += _colsum8(dgc * gs[t])
                acc[24 + 8 * t:32 + 8 * t, :] += _colsum8(dvc * vs[t])
            acc[48:56, :] += _colsum8(dgc)
            acc[56:64, :] += _colsum8(dvc)
            return carry

        lax.fori_loop(0, s // ch, step, 0)

        def step2(c, carry):
            r0 = pl.multiple_of(c * ch, ch)
            for pad, w_ref, o_ref in ((dgpad, wg_ref, dug_ref), (dvpad, wv_ref, duv_ref)):
                prev, cur, nxt = _shift3(pad, r0, ch)
                o_ref[pl.ds(r0, ch), :] = (w_ref[0:1, :] * nxt + w_ref[1:2, :] * cur + w_ref[2:3, :] * prev).astype(o_ref.dtype)
            return carry

        lax.fori_loop(0, s // ch, step2, 0)
        for t in range(3):
            dwg_ref[t:t + 1, :] = jnp.sum(acc[8 * t:8 * t + 8, :], axis=0, keepdims=True)
            dwv_ref[t:t + 1, :] = jnp.sum(acc[24 + 8 * t:32 + 8 * t, :], axis=0, keepdims=True)
        dbg_ref[...] = jnp.sum(acc[48:56, :], axis=0, keepdims=True)
        dbv_ref[...] = jnp.sum(acc[56:64, :], axis=0, keepdims=True)

    col = lambda off: pl.BlockSpec((s, FF_TILE), lambda j: (0, j + off))
    wsp = lambda off: pl.BlockSpec((3, FF_TILE), lambda j: (0, j + off))
    bsp = lambda off: pl.BlockSpec((1, FF_TILE), lambda j: (0, j + off))
    return _pallas(
        body, name="ffn_act_bwd", grid=(nj,),
        in_specs=[col(0), col(nj), col(0), wsp(0), wsp(nj), bsp(0), bsp(nj)],
        out_specs=[col(0), col(0), wsp(0), wsp(0), bsp(0), bsp(0)],
        out_shape=[_sds((s, DFF), MXU_DTYPE)] * 2 + [_sds((3, DFF), F32)] * 2 + [_sds((1, DFF), F32)] * 2,
        scratch_shapes=[pltpu.VMEM((s + 2 * HALO, FF_TILE), F32)] * 4 + [pltpu.VMEM((64, FF_TILE), F32)],
        semantics=("parallel",),
    )(u, u, df, w, w, b, b)


CONV_CHUNK = 64
CONV_HALO = 16


def _tap(pad_ref, r0, k):
    return pad_ref[pl.ds(r0 + CONV_HALO - CW // 2 + k, CONV_CHUNK), :]


def _glu_into(pad_ref, a_ref, g_ref, s):
    zeros = jnp.zeros((CONV_HALO, LANES), F32)
    pad_ref[0:CONV_HALO, :] = zeros
    pad_ref[s + CONV_HALO:s + 2 * CONV_HALO, :] = zeros

    def cp(c, carry):
        r0 = pl.multiple_of(c * ROW_TILE, ROW_TILE)
        pad_ref[pl.ds(r0 + CONV_HALO, ROW_TILE), :] = a_ref[pl.ds(r0, ROW_TILE), :] * _sigmoid(g_ref[pl.ds(r0, ROW_TILE), :])
        return carry

    lax.fori_loop(0, s // ROW_TILE, cp, 0)


def _conf_conv_fwd(ag, conv_w, conv_b):
    s = ag.shape[0]
    nc = DC // LANES

    def body(a_ref, g_ref, w_ref, b_ref, o_ref, upad):
        _glu_into(upad, a_ref, g_ref, s)

        def step(c, carry):
            r0 = pl.multiple_of(c * CONV_CHUNK, CONV_CHUNK)
            acc = jnp.broadcast_to(b_ref[...], (CONV_CHUNK, LANES))
            for k in range(CW):
                acc = acc + w_ref[k:k + 1, :] * _tap(upad, r0, k)
            o_ref[pl.ds(r0, CONV_CHUNK), :] = acc
            return carry

        lax.fori_loop(0, s // CONV_CHUNK, step, 0)

    col = lambda off: pl.BlockSpec((s, LANES), lambda c: (0, c + off))
    return _pallas(
        body, name="conf_conv_fwd", grid=(nc,),
        in_specs=[col(0), col(nc), pl.BlockSpec((CW, LANES), lambda c: (0, c)), pl.BlockSpec((1, LANES), lambda c: (0, c))],
        out_specs=col(0), out_shape=_sds((s, DC), F32),
        scratch_shapes=[pltpu.VMEM((s + 2 * CONV_HALO, LANES), F32)],
        semantics=("parallel",),
    )(ag, ag, conv_w, conv_b)


def _ln_stats(x):
    mu = jnp.mean(x, axis=-1, keepdims=True)
    xc = x - mu
    var = jnp.mean(xc * xc, axis=-1, keepdims=True)
    rstd = lax.rsqrt(var + EPS)
    return xc * rstd, rstd


def _conf_ln_fwd(u1, ln_g, ln_b, ycat):
    s = u1.shape[0]

    def body(u_ref, g_ref, b_ref, ycat_ref, o_ref):
        del ycat_ref
        xhat, _ = _ln_stats(u_ref[...])
        y = xhat * g_ref[...] + b_ref[...]
        o_ref[...] = (y * _sigmoid(y)).astype(o_ref.dtype)

    t = _row_tile(s)
    return _pallas(
        body, name="conf_ln_fwd", grid=(s // t,),
        in_specs=[pl.BlockSpec((t, DC), lambda i: (i, 0)), _vec_spec(DC), _vec_spec(DC),
                  pl.BlockSpec(memory_space=pl.ANY)],
        out_specs=pl.BlockSpec((t, DC), lambda i: (i, 1)),
        out_shape=_sds(ycat.shape, ycat.dtype),
        input_output_aliases={3: 0},
        semantics=("parallel",),
    )(u1, ln_g, ln_b, ycat)


def _conf_ln_bwd(dycat, u1, ln_g, ln_b):
    s = u1.shape[0]
    t = _row_tile(s)
    nt = s // t

    def body(dy_ref, u_ref, g_ref, b_ref, du_ref, dg_ref, db_ref, a_g, a_b):
        i = pl.program_id(0)

        @pl.when(i == 0)
        def _():
            a_g[...] = jnp.zeros_like(a_g)
            a_b[...] = jnp.zeros_like(a_b)

        xhat, rstd = _ln_stats(u_ref[...])
        gv = g_ref[...]
        y = xhat * gv + b_ref[...]
        sg = _sigmoid(y)
        dyl = dy_ref[...] * (sg * (1.0 + y * (1.0 - sg)))
        a_g[...] += _colsum8(dyl * xhat)
        a_b[...] += _colsum8(dyl)
        dxh = dyl * gv
        du_ref[...] = rstd * (dxh - jnp.mean(dxh, axis=-1, keepdims=True)
                              - xhat * jnp.mean(dxh * xhat, axis=-1, keepdims=True))

        @pl.when(i == nt - 1)
        def _():
            dg_ref[...] = jnp.sum(a_g[...], axis=0, keepdims=True)
            db_ref[...] = jnp.sum(a_b[...], axis=0, keepdims=True)

    return _pallas(
        body, name="conf_ln_bwd", grid=(nt,),
        in_specs=[pl.BlockSpec((t, DC), lambda i: (i, 1)), pl.BlockSpec((t, DC), lambda i: (i, 0)),
                  _vec_spec(DC), _vec_spec(DC)],
        out_specs=[pl.BlockSpec((t, DC), lambda i: (i, 0)), _vec_spec(DC), _vec_spec(DC)],
        out_shape=[_sds((s, DC), F32), _sds((1, DC), F32), _sds((1, DC), F32)],
        scratch_shapes=[pltpu.VMEM((8, DC), F32)] * 2,
        semantics=("arbitrary",),
    )(dycat, u1, ln_g, ln_b)


def _conf_conv_bwd(ag, du1, conv_w, rows_out):
    s = ag.shape[0]
    nc = DC // LANES

    def body(a_ref, g_ref, d_ref, w_ref, da_ref, dg_ref, dw_ref, db_ref, upad, dpad, acc):
        _glu_into(upad, a_ref, g_ref, s)
        _fill_padded(dpad, d_ref, s, ROW_TILE, CONV_HALO)
        acc[...] = jnp.zeros_like(acc)

        def step(c, carry):
            r0 = pl.multiple_of(c * CONV_CHUNK, CONV_CHUNK)
            dcur = dpad[pl.ds(r0 + CONV_HALO, CONV_CHUNK), :]
            du0 = jnp.zeros((CONV_CHUNK, LANES), F32)
            for k in range(CW):
                du0 = du0 + w_ref[k:k + 1, :] * _tap(dpad, r0, CW - 1 - k)
                acc[8 * k:8 * k + 8, :] += _colsum8(dcur * _tap(upad, r0, k))
            acc[8 * CW:8 * CW + 8, :] += _colsum8(dcur)
            av = a_ref[pl.ds(r0, CONV_CHUNK), :]
            sg = _sigmoid(g_ref[pl.ds(r0, CONV_CHUNK), :])
            da_ref[pl.ds(r0, CONV_CHUNK), :] = (du0 * sg).astype(da_ref.dtype)
            dg_ref[pl.ds(r0, CONV_CHUNK), :] = (du0 * av * (sg * (1.0 - sg))).astype(dg_ref.dtype)
            return carry

        lax.fori_loop(0, s // CONV_CHUNK, step, 0)
        if rows_out > s:
            zeros = jnp.zeros((rows_out - s, LANES), da_ref.dtype)
            da_ref[s:rows_out, :] = zeros
            dg_ref[s:rows_out, :] = zeros
        for k in range(CW):
            dw_ref[k:k + 1, :] = jnp.sum(acc[8 * k:8 * k + 8, :], axis=0, keepdims=True)
        db_ref[...] = jnp.sum(acc[8 * CW:8 * CW + 8, :], axis=0, keepdims=True)

    col = lambda off: pl.BlockSpec((s, LANES), lambda c: (0, c + off))
    ocol = pl.BlockSpec((rows_out, LANES), lambda c: (0, c))
    return _pallas(
        body, name="conf_conv_bwd", grid=(nc,),
        in_specs=[col(0), col(nc), col(0), pl.BlockSpec((CW, LANES), lambda c: (0, c))],
        out_specs=[ocol, ocol, pl.BlockSpec((CW, LANES), lambda c: (0, c)), pl.BlockSpec((1, LANES), lambda c: (0, c))],
        out_shape=[_sds((rows_out, DC), MXU_DTYPE)] * 2 + [_sds((CW, DC), F32), _sds((1, DC), F32)],
        scratch_shapes=[pltpu.VMEM((s + 2 * CONV_HALO, LANES), F32)] * 2 + [pltpu.VMEM((8 * (CW + 1), LANES), F32)],
        semantics=("parallel",),
    )(ag, ag, du1, conv_w)


Q_TILE = 2 * GW
K_WIN = PAIR_ROWS * GW


def _bias_table(rpb_rev):
    def body(p_ref, t_ref):
        kcol = lax.broadcasted_iota(jnp.int32, (GW, LANES), 0)
        lane = lax.broadcasted_iota(jnp.int32, (GW, LANES), 1)
        qcol = lane % GW
        cs = jnp.clip(qcol - NA_ROWS, 0, GW - 2 * NA_ROWS)
        colvalid = (kcol >= cs) & (kcol < cs + 2 * NA_ROWS)
        neg = jnp.full((GW, LANES), NEG, F32)

        def skew(h, ro, shift):
            if ro < 0 or ro >= 2 * NA_ROWS - 1:
                return neg
            row = jnp.broadcast_to(p_ref[h * 16 + ro:h * 16 + ro + 1, :], (GW, LANES))
            return pltpu.roll(row, shift, 1, stride=1, stride_axis=0)

        for h in range(NH):
            for b in range(TAB_BLOCKS):
                val = jnp.where(lane < GW, skew(h, b - 1, GW + 1), skew(h, b - 2, 1))
                t_ref[h, b * GW:(b + 1) * GW, :] = jnp.where(colvalid, val, neg)

    return _pallas(body, name="attn_bias_table", out_shape=_sds((NH, TAB_BLOCKS * GW, LANES), F32))(rpb_rev)


def _rpb_grad(tt):
    def body(t_ref, o_ref):
        lane = lax.broadcasted_iota(jnp.int32, (GW, LANES), 1)
        si = lax.broadcasted_iota(jnp.int32, (GW, GW), 0)
        ti = lax.broadcasted_iota(jnp.int32, (GW, GW), 1)
        flip = jnp.where(si + ti == GW - 1, 1.0, 0.0).astype(F32)
        o_ref[...] = jnp.zeros_like(o_ref)
        for h in range(NH):
            for ro in range(2 * NA_ROWS - 1):
                lo = t_ref[h, (ro + 1) * GW:(ro + 2) * GW, :]
                hi = t_ref[h, (ro + 2) * GW:(ro + 3) * GW, :]
                g = jnp.where(lane < GW, lo + pltpu.roll(hi, GW, 1), 0.0)
                gf = jnp.dot(flip, g, preferred_element_type=F32, precision=lax.Precision.HIGHEST)
                sk = pltpu.roll(gf, 0, 1, stride=1, stride_axis=0)
                o_ref[h * 16 + ro:h * 16 + ro + 1, :] = jnp.sum(sk, axis=0, keepdims=True)

    return _pallas(body, name="attn_rpb_grad", out_shape=_sds((NH * 16, LANES), F32))(tt)


def _attn_geometry(i, rows):
    wsp = jnp.clip(2 * i - NA_ROWS // 2, 0, rows - PAIR_ROWS)
    k0 = pl.multiple_of(wsp * GW, GW)
    t0 = pl.multiple_of((wsp - 2 * i + NA_ROWS) * GW, GW)
    rr = lax.broadcasted_iota(jnp.int32, (GW, Q_TILE), 1) // GW
    wsr = jnp.clip(2 * i + rr - NA_ROWS // 2, 0, rows - NA_ROWS)
    edge_masks = tuple(jnp.where((kr >= wsr) & (kr < wsr + NA_ROWS), 0.0, NEG).astype(F32)
                       for kr in (wsp, wsp + PAIR_ROWS - 1))
    return k0, t0, edge_masks


def _biased(s_raw, bias, edge_masks):
    x = s_raw + bias
    return jnp.concatenate([x[:GW] + edge_masks[0], x[GW:K_WIN - GW], x[K_WIN - GW:] + edge_masks[1]], axis=0)


def _two_heads_on_lanes(xt):
    feat = lax.broadcasted_iota(jnp.int32, xt.shape, 0)
    zero = jnp.zeros_like(xt)
    return jnp.concatenate([jnp.where(feat < HD, xt, zero), jnp.where(feat >= HD, xt, zero)], axis=1)


def _two_heads_on_rows(x):
    lane = lax.broadcasted_iota(jnp.int32, x.shape, 1)
    zero = jnp.zeros_like(x)
    return jnp.concatenate([jnp.where(lane < HD, x, zero), jnp.where(lane >= HD, x, zero)], axis=0)


def _pick_heads(x2):
    n = x2.shape[0] // 2
    lane = lax.broadcasted_iota(jnp.int32, (n, LANES), 1)
    return jnp.where(lane < HD, x2[:n], x2[n:])


_TN = (((0,), (0,)), ((), ()))


def _attn_fwd(qkv, tab, s):
    rows = s // GW
    npair = rows // 2

    def body(q_ref, kv_ref, tab_ref, o_ref, lse_ref):
        i = pl.program_id(0)
        k0, t0, edge_masks = _attn_geometry(i, rows)
        for p in range(NH // 2):
            cq = slice(p * LANES, (p + 1) * LANES)
            ck = slice(DA + p * LANES, DA + (p + 1) * LANES)
            cv = slice(2 * DA + p * LANES, 2 * DA + (p + 1) * LANES)
            qm2 = _two_heads_on_lanes(q_ref[:, cq].T) * SCALE
            s_loc = jnp.dot(kv_ref[pl.ds(k0, K_WIN), ck], qm2, preferred_element_type=F32)
            s_ctx = jnp.dot(kv_ref[pl.ds(s, CTX), ck], qm2, preferred_element_type=F32)
            p_loc, p_ctx = [], []
            for hh in range(2):
                h = 2 * p + hh
                ch = slice(hh * Q_TILE, (hh + 1) * Q_TILE)
                sl = _biased(s_loc[:, ch], tab_ref[h, pl.ds(t0, K_WIN), :], edge_masks)
                sc = s_ctx[:, ch]
                m = jnp.maximum(jnp.max(sl, axis=0, keepdims=True), jnp.max(sc, axis=0, keepdims=True))
                el = jnp.exp(sl - m)
                ec = jnp.exp(sc - m)
                l = jnp.sum(el, axis=0, keepdims=True) + jnp.sum(ec, axis=0, keepdims=True)
                inv = 1.0 / l
                lse_ref[h:h + 1, :] = m + jnp.log(l)
                p_loc.append((el * inv).astype(MXU_DTYPE))
                p_ctx.append((ec * inv).astype(MXU_DTYPE))
            o2 = (lax.dot_general(jnp.concatenate(p_loc, axis=1), kv_ref[pl.ds(k0, K_WIN), cv], _TN, preferred_element_type=F32)
                  + lax.dot_general(jnp.concatenate(p_ctx, axis=1), kv_ref[pl.ds(s, CTX), cv], _TN, preferred_element_type=F32))
            o_ref[:, cq] = _pick_heads(o2).astype(o_ref.dtype)

    return _pallas(
        body, name="attn_fwd", grid=(npair,),
        in_specs=[pl.BlockSpec((Q_TILE, DA), lambda i: (i, 0)), pl.BlockSpec(memory_space=pltpu.VMEM),
                  pl.BlockSpec(memory_space=pltpu.VMEM)],
        out_specs=[pl.BlockSpec((Q_TILE, DA), lambda i: (i, 0)), pl.BlockSpec((NH, Q_TILE), lambda i: (0, i))],
        out_shape=[_sds((s, D), MXU_DTYPE), _sds((NH, s), F32)],
        semantics=("arbitrary",),
    )(qkv, qkv, tab)


def _attn_bwd(qkv, tab, lse, dycat, s):
    rows = s // GW
    npair = rows // 2
    sa = s + CTX
    nzero = CTX // Q_TILE

    def body(q_ref, do_ref, lse_ref, kv_ref, tab_ref, dq_ref, dkv_ref, tt_ref, dk_acc, dv_acc):
        i = pl.program_id(0)

        @pl.when(i == 0)
        def _():
            dk_acc[...] = jnp.zeros_like(dk_acc)
            dv_acc[...] = jnp.zeros_like(dv_acc)
            tt_ref[...] = jnp.zeros_like(tt_ref)

        @pl.when(i >= npair)
        def _():
            dq_ref[...] = jnp.zeros_like(dq_ref)

        @pl.when(i < npair)
        def _():
            k0, t0, edge_masks = _attn_geometry(i, rows)
            for p in range(NH // 2):
                cq = slice(p * LANES, (p + 1) * LANES)
                ck = slice(DA + p * LANES, DA + (p + 1) * LANES)
                cv = slice(2 * DA + p * LANES, 2 * DA + (p + 1) * LANES)
                qp = q_ref[:, cq] * SCALE
                dop = do_ref[:, cq].astype(MXU_DTYPE)
                qm2 = _two_heads_on_lanes(qp.T)
                dom2 = _two_heads_on_lanes(dop.T)
                kw = kv_ref[pl.ds(k0, K_WIN), ck]
                kc = kv_ref[pl.ds(s, CTX), ck]
                vw = kv_ref[pl.ds(k0, K_WIN), cv]
                vc = kv_ref[pl.ds(s, CTX), cv]
                s_loc = jnp.dot(kw, qm2, preferred_element_type=F32)
                s_ctx = jnp.dot(kc, qm2, preferred_element_type=F32)
                dp_loc = jnp.dot(vw, dom2, preferred_element_type=F32)
                dp_ctx = jnp.dot(vc, dom2, preferred_element_type=F32)
                p_loc, p_ctx, ds_loc, ds_ctx = [], [], [], []
                for hh in range(2):
                    h = 2 * p + hh
                    ch = slice(hh * Q_TILE, (hh + 1) * Q_TILE)
                    lse_h = lse_ref[h:h + 1, :]
                    pl_ = jnp.exp(_biased(s_loc[:, ch], tab_ref[h, pl.ds(t0, K_WIN), :], edge_masks) - lse_h)
                    pc_ = jnp.exp(s_ctx[:, ch] - lse_h)
                    dpl = dp_loc[:, ch]
                    dpc = dp_ctx[:, ch]
                    delta = jnp.sum(pl_ * dpl, axis=0, keepdims=True) + jnp.sum(pc_ * dpc, axis=0, keepdims=True)
                    dsl = pl_ * (dpl - delta)
                    dsc = pc_ * (dpc - delta)
                    tt_ref[h, pl.ds(t0, K_WIN), :] += dsl
                    p_loc.append(pl_.astype(MXU_DTYPE))
                    p_ctx.append(pc_.astype(MXU_DTYPE))
                    ds_loc.append(dsl.astype(MXU_DTYPE))
                    ds_ctx.append(dsc.astype(MXU_DTYPE))
                p_loc, p_ctx = jnp.concatenate(p_loc, axis=1), jnp.concatenate(p_ctx, axis=1)
                ds_loc, ds_ctx = jnp.concatenate(ds_loc, axis=1), jnp.concatenate(ds_ctx, axis=1)
                do_rows = _two_heads_on_rows(dop)
                q_rows = _two_heads_on_rows(qp)
                dv_acc[pl.ds(k0, K_WIN), cq] += jnp.dot(p_loc, do_rows, preferred_element_type=F32)
                dv_acc[pl.ds(s, CTX), cq] += jnp.dot(p_ctx, do_rows, preferred_element_type=F32)
                dk_acc[pl.ds(k0, K_WIN), cq] += jnp.dot(ds_loc, q_rows, preferred_element_type=F32)
                dk_acc[pl.ds(s, CTX), cq] += jnp.dot(ds_ctx, q_rows, preferred_element_type=F32)
                dq2 = (lax.dot_general(ds_loc, kw, _TN, preferred_element_type=F32)
                       + lax.dot_general(ds_ctx, kc, _TN, preferred_element_type=F32))
                dq_ref[:, cq] = (_pick_heads(dq2) * SCALE).astype(dq_ref.dtype)

        @pl.when(i == npair - 1)
        def _():
            def cp(c, carry):
                r0 = pl.multiple_of(c * ROW_TILE, ROW_TILE)
                dkv_ref[pl.ds(r0, ROW_TILE), 0:DA] = dk_acc[pl.ds(r0, ROW_TILE), :].astype(dkv_ref.dtype)
                dkv_ref[pl.ds(r0, ROW_TILE), DA:2 * DA] = dv_acc[pl.ds(r0, ROW_TILE), :].astype(dkv_ref.dtype)
                return carry

            lax.fori_loop(0, sa // ROW_TILE, cp, 0)

    qmap = lambda i: (jnp.minimum(i, npair - 1), 0)
    return _pallas(
        body, name="attn_bwd", grid=(npair + nzero,),
        in_specs=[pl.BlockSpec((Q_TILE, DA), qmap), pl.BlockSpec((Q_TILE, DA), qmap),
                  pl.BlockSpec((NH, Q_TILE), lambda i: (0, jnp.minimum(i, npair - 1))),
                  pl.BlockSpec(memory_space=pltpu.VMEM), pl.BlockSpec(memory_space=pltpu.VMEM)],
        out_specs=[pl.BlockSpec((Q_TILE, DA), lambda i: (i, 0)), pl.BlockSpec(memory_space=pltpu.VMEM),
                   pl.BlockSpec(memory_space=pltpu.VMEM)],
        out_shape=[_sds((sa, DA), MXU_DTYPE), _sds((sa, 2 * DA), MXU_DTYPE), _sds((NH, TAB_BLOCKS * GW, LANES), F32)],
        scratch_shapes=[pltpu.VMEM((sa, DA), F32)] * 2,
        semantics=("arbitrary",),
    )(qkv, dycat, lse, qkv, tab)


def _tile(n, prefs):
    for t in prefs:
        if n % t == 0:
            return t
    raise ValueError((n, prefs))


def _local_step(x, ctx, tgt, mod, mod_c, vec, w_in, late_weights, rpb_rev, early_grads=None):
    s = x.shape[0]
    sa = s + CTX
    ts = _tile(s, (1024, 512, 256))
    ts2 = _tile(s, (2048, 1024, 512, 256))
    tsa = _tile(sa, (1088, 640, 256))
    tsa2 = _tile(sa, (2176, 640, 256))
    sh1, sc1, gt1, sh2, sc2, gt2 = (mod[i:i + 1] for i in range(6))
    csh1, csc1 = mod_c[0:1], mod_c[1:2]
    act = MXU_DTYPE

    tab = _bias_table(rpb_rev)
    h_all = _rmsmod_fwd(x, ctx, vec["g_norm1"], sc1, sh1, csc1, csh1)
    w_in = w_in(h_all) if callable(w_in) else w_in
    qkv = _mm(h_all, w_in, mode="nn", m=sa, n=3 * DA, k=D, tm=tsa2, tn=512, tk=D, out_dtype=MXU_DTYPE, name="mm_qkv")
    ag = _mm(h_all, w_in, mode="nn", m=s, n=2 * DC, k=D, tm=ts2, tn=512, tk=D, out_dtype=F32, name="mm_ag", b_off=(0, 3))
    ycat, lse = _attn_fwd(qkv, tab, s)
    u1 = _conf_conv_fwd(ag, vec["conv_w"], vec["conv_b"])
    ycat = _conf_ln_fwd(u1, vec["ln_g"], vec["ln_b"], ycat)
    if callable(late_weights):
        w_out, ffn_weights = late_weights(ycat)
    else:
        w_out, ffn_weights = late_weights[0], late_weights[1:]
    y = _mm(ycat, w_out, mode="nn", m=s, n=D, k=D, tm=ts2, tn=512, tk=D, out_dtype=F32, name="mm_out")
    x1, h2 = _resid_rmsmod_fwd(x, y, gt1, vec["g_norm2"], sc2, sh2)
    w_up, w_down = ffn_weights(h2) if callable(ffn_weights) else ffn_weights
    u = _mm(h2, w_up, mode="nn", m=s, n=2 * DFF, k=D, tm=ts2, tn=512, tk=D, out_dtype=act, name="mm_up")
    f = _ffn_act_fwd(u, vec["ffn_conv_w"], vec["ffn_conv_b"])
    z = _mm(f, w_down, mode="nn", m=s, n=D, k=DFF, tm=ts, tn=D, tk=DFF, out_dtype=F32, name="mm_down")
    dx2, dz, loss, dgt2, dgf = _final_fwd_bwd(x1, z, gt2, vec["g_final"], tgt)

    df = _mm(dz, w_down, mode="nt", m=s, n=DFF, k=D, tm=ts, tn=DFF, tk=D, out_dtype=act, name="mm_down_dx")
    d_w_down = _mm(f, dz, mode="tn", m=DFF, n=D, k=s, tm=DFF // 2, tn=D, tk=ts2, out_dtype=F32, name="mm_down_dw")
    dug, duv, dfw_g, dfw_v, dfb_g, dfb_v = _ffn_act_bwd(u, df, vec["ffn_conv_w"], vec["ffn_conv_b"])
    dw_kw = dict(mode="tn", m=D, n=DFF, k=s, tm=D, tn=DFF, tk=ts, out_dtype=F32, out_total=(D, 2 * DFF))
    d_w_up = _mm(h2, dug, name="mm_up_dw_gate", **dw_kw)
    d_w_up = _mm(h2, duv, name="mm_up_dw_val", o_off=(0, 1), into=d_w_up, **dw_kw)
    if early_grads is not None:
        early_grads[0](d_w_up, d_w_down)
    dh2 = _mm([dug, duv], w_up, mode="nt", m=s, n=D, k=2 * DFF, tm=ts, tn=D, tk=2 * DFF, out_dtype=F32, name="mm_up_dx")
    sc2_b = sc2 if early_grads is None else sc2 + early_grads[1](dh2)
    dsh2, dsc2, dg2, dx1, dy, dgt1 = _rmsmod_bwd(x1, dh2, vec["g_norm2"], sc2_b, name="rmsmod2_bwd", add=dx2, resid=(gt1, y))
    dycat = _mm(dy, w_out, mode="nt", m=s, n=D, k=D, tm=ts2, tn=512, tk=D, out_dtype=F32, name="mm_out_dx")
    d_w_out = _mm(ycat, dy, mode="tn", m=D, n=D, k=s, tm=D, tn=D, tk=ts, out_dtype=F32, name="mm_out_dw")
    du1, dln_g, dln_b = _conf_ln_bwd(dycat, u1, vec["ln_g"], vec["ln_b"])
    da, dg, dconv_w, dconv_b = _conf_conv_bwd(ag, du1, vec["conv_w"], sa)
    dq, dkv, tt = _attn_bwd(qkv, tab, lse, dycat, s)
    drpb_rev = _rpb_grad(tt)
    d_pieces = [dq, dkv, da, dg]
    dh = _mm(d_pieces, w_in, mode="nt", m=sa, n=D, k=NIN, tm=tsa, tn=D, tk=NIN, out_dtype=F32, name="mm_in_dx")
    d_w_in = _mm(h_all, d_pieces, mode="tn", m=D, n=NIN, k=sa, tm=D, tn=NIN, tk=tsa, out_dtype=F32, name="mm_in_dw")
    dsh1, dsc1, dg1, grad_x = _rmsmod_bwd(x, dh, vec["g_norm1"], sc1, name="rmsmod1_bwd", add=dx1)
    dcsh1, dcsc1, dg1c = _rmsmod_bwd(ctx, dh, vec["g_norm1"], csc1, name="rmsmod1_ctx_bwd", dh_row0=s)

    small = dict(
        dmod=[dsh1, dsc1, dgt1, dsh2, dsc2, dgt2], dmod_c=[dcsh1, dcsc1],
        g_norm1=[dg1, dg1c], g_norm2=dg2, g_final=dgf, conv_b=dconv_b, ln_g=dln_g, ln_b=dln_b, conv_w=dconv_w,
        ffn_conv_w=[dfw_g, dfw_v], ffn_conv_b=[dfb_g, dfb_v], rpb_rev=drpb_rev,
    )
    return loss, grad_x, d_w_in, d_w_out, d_w_up, d_w_down, small


N_CHIPS = 4
HBM = pl.BlockSpec(memory_space=pl.ANY)
BIG = {"w_in": ("col", (D, NIN)), "w_out": ("row", (D, D)), "w_up": ("col", (D, 2 * DFF)), "w_down": ("row", (DFF, D))}
BIG_NAMES = tuple(BIG)
LATE_NAMES = ("w_out", "w_up", "w_down")


def _shard_shape(name):
    kind, (r, c) = BIG[name]
    return (r, c // N_CHIPS) if kind == "col" else (r // N_CHIPS, c)


def _half_rows(name):
    return _shard_shape(name)[0] // 2


def _place():
    x, y, c = lax.axis_index("x"), lax.axis_index("y"), lax.axis_index("c")
    others = [(1 - x, y), (x, 1 - y), (1 - x, 1 - y)]
    return x, y, c, 2 * x + y, (x, y, 1 - c), others


def _whole_region(ref, name, chip, half):
    kind, _ = BIG[name]
    r, c = _shard_shape(name)
    if kind == "col":
        return ref.at[pl.ds(half * (r // 2), r // 2), pl.ds(chip * c, c)]
    return ref.at[pl.ds(chip * r + half * (r // 2), r // 2), :]


def _remote(src, dst, send_sem, recv_sem, to):
    return pltpu.make_async_remote_copy(src_ref=src, dst_ref=dst, send_sem=send_sem, recv_sem=recv_sem,
                                        device_id=to, device_id_type=MESH)


def _cast_into_whole(name, shard, chip):
    kind, whole = BIG[name]
    r, c = shard.shape
    if kind == "col":
        tr = 256
        o_spec = pl.BlockSpec((tr, c), lambda i, ch: (i, ch[0]))
    else:
        tr = _tile(r, (128, 352))
        o_spec = pl.BlockSpec((tr, c), lambda i, ch: (ch[0] * (r // tr) + i, 0))

    def body(ch_ref, x_ref, o_ref):
        del ch_ref
        o_ref[...] = x_ref[...].astype(o_ref.dtype)

    return _pallas(body, name="cast_" + name, prefetch=1, grid=(r // tr,),
                   in_specs=[pl.BlockSpec((tr, c), lambda i, ch: (i, 0))], out_specs=o_spec,
                   out_shape=_sds(whole, MXU_DTYPE), semantics=("parallel",))(chip, shard)


SEM = pl.BlockSpec(memory_space=pltpu.SEMAPHORE)
IN_HBM = pl.BlockSpec(memory_space=pltpu.HBM)
DATAFLOW = pltpu.SideEffectType.DATAFLOW_SIDE_EFFECTING


def _keep_in_hbm(a):
    return pltpu.with_memory_space_constraint(a, pltpu.HBM)


def _several(after):
    return list(after) if isinstance(after, (list, tuple)) else [after]


FLIPS = [(dx, dy, dc) for dx in (0, 1) for dy in (0, 1) for dc in (0, 1)][1:]


def _flipped(flip):
    x, y, c = lax.axis_index("x"), lax.axis_index("y"), lax.axis_index("c")
    return tuple(1 - v if f else v for v, f in zip((x, y, c), flip))


def _share_start(v, tag, after):
    r, n = v.shape
    ns = 2 * len(FLIPS)

    def body(*refs):
        v_ref, land_ref = refs[0], refs[1]
        sems = refs[2 + len(_several(after)):2 + len(_several(after)) + ns]
        x, y, c = lax.axis_index("x"), lax.axis_index("y"), lax.axis_index("c")
        mine = land_ref.at[pl.ds((4 * x + 2 * y + c) * r, r), :]
        for k, flip in enumerate(FLIPS):
            _remote(v_ref, mine, sems[2 * k], sems[2 * k + 1], _flipped(flip)).start()

    res = pl.pallas_call(
        body, name="share_" + tag + "_start",
        out_shape=(*[pltpu.SemaphoreType.DMA(())] * ns, pltpu.HBM(v.shape, v.dtype), pltpu.HBM((8 * r, n), v.dtype)),
        in_specs=[IN_HBM] * 2 + [pl.BlockSpec(memory_space=pl.ANY)] * len(_several(after)),
        out_specs=(*[SEM] * ns, IN_HBM, IN_HBM),
        input_output_aliases={0: ns, 1: ns + 1},
        compiler_params=pltpu.CompilerParams(has_side_effects=DATAFLOW),
    )(_keep_in_hbm(v), _keep_in_hbm(jnp.tile(v, (8, 1))), *_several(after))
    return list(res[:ns]), res[ns], res[ns + 1]


def _share_wait(started, after, tag):
    sems, v, land = started
    r = v.shape[0]
    ns = len(sems)

    def body(*refs):
        v_ref, land_ref = refs[0], refs[1]
        sem_refs = refs[2:2 + ns]
        for k, flip in enumerate(FLIPS):
            px, py, pc = _flipped(flip)
            theirs = land_ref.at[pl.ds((4 * px + 2 * py + pc) * r, r), :]
            cp = _remote(v_ref, theirs, sem_refs[2 * k], sem_refs[2 * k + 1], (px, py, pc))
            cp.wait_send()
            cp.wait_recv()

    res = pl.pallas_call(
        body, name="share_" + tag + "_wait",
        out_shape=(pltpu.HBM(v.shape, v.dtype), pltpu.HBM(land.shape, land.dtype)),
        in_specs=[IN_HBM] * 2 + [SEM] * ns + [pl.BlockSpec(memory_space=pl.ANY)] * len(_several(after)),
        out_specs=(IN_HBM, IN_HBM),
        input_output_aliases={0: 0, 1: 1},
        compiler_params=pltpu.CompilerParams(has_side_effects=DATAFLOW),
    )(v, land, *sems, *_several(after))
    return res[1]


def _gather_start(wholes, names, after, tag):
    nw = len(names)
    ns = 2 * 3 * nw

    def body(*refs):
        ins = refs[:nw]
        sems = refs[nw + 1:nw + 1 + ns]
        token = refs[2 * nw + ns + 1]
        _, _, c, chip, _, others = _place()
        for w, name in enumerate(names):
            mine = _whole_region(ins[w], name, chip, c)
            for t, (ox, oy) in enumerate(others):
                k = 2 * (3 * w + t)
                _remote(mine, mine, sems[k], sems[k + 1], (ox, oy, c)).start()
        token[...] = jnp.zeros_like(token)

    res = pl.pallas_call(
        body, name="gather_" + tag + "_start",
        out_shape=(*[pltpu.SemaphoreType.DMA(())] * ns, *[pltpu.HBM(a.shape, a.dtype) for a in wholes], _sds((8, LANES), F32)),
        in_specs=[IN_HBM] * nw + [pl.BlockSpec(memory_space=pl.ANY)],
        out_specs=(*[SEM] * ns, *[IN_HBM] * nw, pl.BlockSpec(memory_space=pltpu.VMEM)),
        input_output_aliases={i: ns + i for i in range(nw)},
        compiler_params=pltpu.CompilerParams(has_side_effects=DATAFLOW),
    )(*[_keep_in_hbm(a) for a in wholes], after)
    return list(res[:ns]), list(res[ns:ns + nw]), res[ns + nw]


def _gather_wait(sems, wholes, names, after, tag):
    nw = len(names)
    ns = len(sems)

    def body(*refs):
        ins = refs[:nw]
        sem_refs = refs[nw:nw + ns]
        _, _, c, chip, _, others = _place()
        for w, name in enumerate(names):
            mine = _whole_region(ins[w], name, chip, c)
            for t, (ox, oy) in enumerate(others):
                got = _whole_region(ins[w], name, 2 * ox + oy, c)
                k = 2 * (3 * w + t)
                cp = _remote(mine, got, sem_refs[k], sem_refs[k + 1], (ox, oy, c))
                cp.wait_send()
                cp.wait_recv()

    return pl.pallas_call(
        body, name="gather_" + tag + "_wait",
        out_shape=tuple(pltpu.HBM(a.shape, a.dtype) for a in wholes),
        in_specs=[IN_HBM] * nw + [SEM] * ns + [pl.BlockSpec(memory_space=pl.ANY)], out_specs=tuple([IN_HBM] * nw),
        input_output_aliases={i: i for i in range(nw)},
        compiler_params=pltpu.CompilerParams(has_side_effects=DATAFLOW),
    )(*wholes, *sems, after)


def _forward_halves(wholes, names, tag):
    nw = len(names)

    def body(*refs):
        outs = refs[nw:2 * nw]
        send_sems, recv_sems = refs[2 * nw:]
        _, _, c, _, sibling, others = _place()
        sends = []
        for w, name in enumerate(names):
            for t, (ox, oy) in enumerate(others):
                got = _whole_region(outs[w], name, 2 * ox + oy, c)
                cp = _remote(got, got, send_sems.at[w, t], recv_sems.at[w, t], sibling)
                cp.start()
                sends.append(cp)
        for w, name in enumerate(names):
            for t, (ox, oy) in enumerate(others):
                got = _whole_region(outs[w], name, 2 * ox + oy, 1 - c)
                _remote(got, got, send_sems.at[w, t], recv_sems.at[w, t], sibling).wait_recv()
        for cp in sends:
            cp.wait_send()

    return pl.pallas_call(
        body, name="gather_" + tag + "_forward",
        out_shape=[_sds(a.shape, a.dtype) for a in wholes],
        in_specs=[HBM] * nw, out_specs=[HBM] * nw,
        input_output_aliases={i: i for i in range(nw)},
        scratch_shapes=[pltpu.SemaphoreType.DMA((nw, 3)), pltpu.SemaphoreType.DMA((nw, 3))],
    )(*wholes)


def _forward_start(wholes, names, tag, after):
    nw = len(names)
    ns = 2 * 3 * nw

    def body(*refs):
        ins = refs[:nw]
        sems = refs[nw + 1:nw + 1 + ns]
        token = refs[2 * nw + ns + 1]
        _, _, c, _, sibling, others = _place()
        for w, name in enumerate(names):
            for t, (ox, oy) in enumerate(others):
                got = _whole_region(ins[w], name, 2 * ox + oy, c)
                k = 2 * (3 * w + t)
                _remote(got, got, sems[k], sems[k + 1], sibling).start()
        token[...] = jnp.zeros_like(token)

    res = pl.pallas_call(
        body, name="gather_" + tag + "_forward_start",
        out_shape=(*[pltpu.SemaphoreType.DMA(())] * ns, *[pltpu.HBM(a.shape, a.dtype) for a in wholes], _sds((8, LANES), F32)),
        in_specs=[IN_HBM] * nw + [pl.BlockSpec(memory_space=pl.ANY)],
        out_specs=(*[SEM] * ns, *[IN_HBM] * nw, pl.BlockSpec(memory_space=pltpu.VMEM)),
        input_output_aliases={i: ns + i for i in range(nw)},
        compiler_params=pltpu.CompilerParams(has_side_effects=DATAFLOW),
    )(*[_keep_in_hbm(a) for a in wholes], after)
    return list(res[:ns]), list(res[ns:ns + nw]), res[ns + nw]


def _forward_wait(sems, wholes, names, after, tag):
    nw = len(names)
    ns = len(sems)

    def body(*refs):
        ins = refs[:nw]
        sem_refs = refs[nw:nw + ns]
        _, _, c, _, sibling, others = _place()
        for w, name in enumerate(names):
            for t, (ox, oy) in enumerate(others):
                k = 2 * (3 * w + t)
                cp = _remote(_whole_region(ins[w], name, 2 * ox + oy, c), _whole_region(ins[w], name, 2 * ox + oy, 1 - c),
                             sem_refs[k], sem_refs[k + 1], sibling)
                cp.wait_send()
                cp.wait_recv()

    return pl.pallas_call(
        body, name="gather_" + tag + "_forward_wait",
        out_shape=tuple(pltpu.HBM(a.shape, a.dtype) for a in wholes),
        in_specs=[IN_HBM] * nw + [SEM] * ns + [pl.BlockSpec(memory_space=pl.ANY)], out_specs=tuple([IN_HBM] * nw),
        input_output_aliases={i: i for i in range(nw)},
        compiler_params=pltpu.CompilerParams(has_side_effects=DATAFLOW),
    )(*wholes, *sems, after)


def _compact_shape(name, dtype):
    kind, (r, c) = BIG[name]
    return _sds((r // 2, c), dtype)


def _swap_pairs(ins, outs, names, c):
    pairs = []
    for w, name in enumerate(names):
        kind, _ = BIG[name]
        half = _half_rows(name)
        if kind == "col":
            pairs.append((ins[w].at[pl.ds((1 - c) * half, half), :], outs[w]))
        else:
            pairs += [(ins[w].at[pl.ds(jj * 2 * half + (1 - c) * half, half), :], outs[w].at[pl.ds(jj * half, half), :])
                      for jj in range(N_CHIPS)]
    return pairs


def _n_swap_copies(names):
    return sum(1 if BIG[n][0] == "col" else N_CHIPS for n in names)


def _swap_start(grads, names, label):
    nw = len(names)
    ns = 2 * _n_swap_copies(names)

    def body(*refs):
        ins, lands = refs[:nw], refs[nw:2 * nw]
        sems = refs[2 * nw:2 * nw + ns]
        token = refs[4 * nw + ns]
        _, _, c, _, sibling, _ = _place()
        for k, (src, dst) in enumerate(_swap_pairs(ins, lands, names, c)):
            _remote(src, dst, sems[2 * k], sems[2 * k + 1], sibling).start()
        token[...] = jnp.zeros_like(token)

    lands = [_keep_in_hbm(lax.empty(_compact_shape(n, F32).shape, F32)) for n in names]
    res = pl.pallas_call(
        body, name=label,
        out_shape=(*[pltpu.SemaphoreType.DMA(())] * ns, *[pltpu.HBM(a.shape, a.dtype) for a in grads],
                   *[pltpu.HBM(a.shape, a.dtype) for a in lands], _sds((8, LANES), F32)),
        in_specs=[IN_HBM] * (2 * nw),
        out_specs=(*[SEM] * ns, *[IN_HBM] * (2 * nw), pl.BlockSpec(memory_space=pltpu.VMEM)),
        input_output_aliases={i: ns + i for i in range(2 * nw)},
        compiler_params=pltpu.CompilerParams(has_side_effects=DATAFLOW),
    )(*[_keep_in_hbm(a) for a in grads], *lands)
    return list(res[:ns]), list(res[ns:ns + nw]), list(res[ns + nw:ns + 2 * nw]), res[ns + 2 * nw]


def _swap_wait(sems, grads, lands, names, after, label):
    nw = len(names)
    ns = len(sems)

    def body(*refs):
        ins, land_refs = refs[:nw], refs[nw:2 * nw]
        sem_refs = refs[2 * nw:2 * nw + ns]
        _, _, c, _, sibling, _ = _place()
        for k, (src, dst) in enumerate(_swap_pairs(ins, land_refs, names, c)):
            cp = _remote(src, dst, sem_refs[2 * k], sem_refs[2 * k + 1], sibling)
            cp.wait_send()
            cp.wait_recv()

    res = pl.pallas_call(
        body, name=label,
        out_shape=tuple(pltpu.HBM(a.shape, a.dtype) for a in (*grads, *lands)),
        in_specs=[IN_HBM] * (2 * nw) + [SEM] * ns + [pl.BlockSpec(memory_space=pl.ANY)] * len(_several(after)),
        out_specs=tuple([IN_HBM] * (2 * nw)),
        input_output_aliases={i: i for i in range(2 * nw)},
        compiler_params=pltpu.CompilerParams(has_side_effects=DATAFLOW),
    )(*grads, *lands, *sems, *_several(after))
    return list(res[:nw]), list(res[nw:])


def _add_halves(name, grad, got, core):
    kind, (r, c) = BIG[name]
    half = _half_rows(name)
    if kind == "col":
        t = 128
        grid = (half // t,)
        g_spec = pl.BlockSpec((t, c), lambda i, cr: (cr[0] * (half // t) + i, 0))
        o_spec = pl.BlockSpec((t, c), lambda i, cr: (i, 0))
    else:
        t = half
        grid = (N_CHIPS,)
        g_spec = pl.BlockSpec((t, c), lambda i, cr: (2 * i + cr[0], 0))
        o_spec = pl.BlockSpec((t, c), lambda i, cr: (i, 0))

    def body(c_ref, g_ref, b_ref, o_ref):
        del c_ref
        o_ref[...] = (g_ref[...] + b_ref[...]).astype(o_ref.dtype)

    return pl.pallas_call(
        body, name="grad_add_" + name,
        grid_spec=pltpu.PrefetchScalarGridSpec(num_scalar_prefetch=1, grid=grid, in_specs=[g_spec, o_spec], out_specs=o_spec),
        out_shape=_compact_shape(name, BF16),
        compiler_params=pltpu.CompilerParams(dimension_semantics=("parallel",), vmem_limit_bytes=VMEM_LIMIT),
    )(core, grad, got)


def _piece(ref, name, chip):
    kind, _ = BIG[name]
    r, c = _shard_shape(name)
    if kind == "col":
        return ref.at[:, pl.ds(chip * c, c)]
    return ref.at[pl.ds(chip * (r // 2), r // 2), :]


def _landing_shape(name):
    r, c = _shard_shape(name)
    return (N_CHIPS - 1, r // 2, c)


def _exchange_start(parts, names, label):
    nw = len(names)
    ns = 2 * 3 * nw

    def body(*refs):
        ins, lands = refs[:nw], refs[nw:2 * nw]
        sems = refs[2 * nw:2 * nw + ns]
        token = refs[4 * nw + ns]
        _, _, c, _, _, others = _place()
        for w, name in enumerate(names):
            for t, (ox, oy) in enumerate(others):
                k = 2 * (3 * w + t)
                _remote(_piece(ins[w], name, 2 * ox + oy), lands[w].at[t], sems[k], sems[k + 1], (ox, oy, c)).start()
        token[...] = jnp.zeros_like(token)

    lands = [_keep_in_hbm(lax.empty(_landing_shape(n), BF16)) for n in names]
    res = pl.pallas_call(
        body, name=label,
        out_shape=(*[pltpu.SemaphoreType.DMA(())] * ns, *[pltpu.HBM(a.shape, a.dtype) for a in parts],
                   *[pltpu.HBM(a.shape, a.dtype) for a in lands], _sds((8, LANES), F32)),
        in_specs=[IN_HBM] * (2 * nw),
        out_specs=(*[SEM] * ns, *[IN_HBM] * (2 * nw), pl.BlockSpec(memory_space=pltpu.VMEM)),
        input_output_aliases={i: ns + i for i in range(2 * nw)},
        compiler_params=pltpu.CompilerParams(has_side_effects=DATAFLOW),
    )(*[_keep_in_hbm(a) for a in parts], *lands)
    return list(res[:ns]), list(res[ns:ns + nw]), list(res[ns + nw:ns + 2 * nw]), res[ns + 2 * nw]


def _exchange_wait(sems, parts, lands, names, after, label):
    nw = len(names)
    ns = len(sems)

    def body(*refs):
        ins, land_refs = refs[:nw], refs[nw:2 * nw]
        sem_refs = refs[2 * nw:2 * nw + ns]
        _, _, c, _, _, others = _place()
        for w, name in enumerate(names):
            for t, (ox, oy) in enumerate(others):
                k = 2 * (3 * w + t)
                cp = _remote(_piece(ins[w], name, 2 * ox + oy), land_refs[w].at[t], sem_refs[k], sem_refs[k + 1], (ox, oy, c))
                cp.wait_send()
                cp.wait_recv()

    res = pl.pallas_call(
        body, name=label,
        out_shape=tuple(pltpu.HBM(a.shape, a.dtype) for a in (*parts, *lands)),
        in_specs=[IN_HBM] * (2 * nw) + [SEM] * ns + [pl.BlockSpec(memory_space=pl.ANY)] * len(_several(after)),
        out_specs=tuple([IN_HBM] * (2 * nw)),
        input_output_aliases={i: i for i in range(2 * nw)},
        compiler_params=pltpu.CompilerParams(has_side_effects=DATAFLOW),
    )(*parts, *lands, *sems, *_several(after))
    return list(res[:nw]), list(res[nw:])


def _sum_chips(name, part, got, chip):
    kind, _ = BIG[name]
    _, r, c = got.shape
    t = _tile(r, (128, 352))
    if kind == "col":
        own = pl.BlockSpec((t, c), lambda i, ch: (i, ch[0]))
    else:
        own = pl.BlockSpec((t, c), lambda i, ch: (ch[0] * (r // t) + i, 0))

    def body(ch_ref, p_ref, g_ref, o_ref):
        del ch_ref
        acc = p_ref[...].astype(F32)
        for j in range(N_CHIPS - 1):
            acc = acc + g_ref[j].astype(F32)
        o_ref[...] = acc

    return _pallas(
        body, name="grad_sum_" + name, prefetch=1, grid=(r // t,),
        in_specs=[own, pl.BlockSpec((N_CHIPS - 1, t, c), lambda i, ch: (0, i, 0))],
        out_specs=pl.BlockSpec((t, c), lambda i, ch: (i, 0)),
        out_shape=_sds((r, c), F32), semantics=("parallel",),
    )(chip, part, got)


def _send_halves(sums, label, after):
    nw = len(sums)

    def body(*refs):
        ins, outs = refs[:nw], refs[nw + 1:2 * nw + 1]
        send_sems, recv_sems = refs[2 * nw + 1:]
        _, _, _, _, sibling, _ = _place()
        copies = [_remote(ins[w], outs[w], send_sems.at[w], recv_sems.at[w], sibling) for w in range(nw)]
        for cp in copies:
            cp.start()
        for cp in copies:
            cp.wait()

    return pl.pallas_call(
        body, name=label,
        out_shape=[_sds(a.shape, a.dtype) for a in sums],
        in_specs=[HBM] * (nw + 1), out_specs=[HBM] * nw,
        scratch_shapes=[pltpu.SemaphoreType.DMA((nw,)), pltpu.SemaphoreType.DMA((nw,))],
    )(*sums, after)


EARLY_GRADS = ("w_up", "w_down")
LAST_GRADS = ("w_in", "w_out")


def _reduce_finish(started, names, after, chip, tag):
    sems, parts, lands, _ = started
    parts, lands = _exchange_wait(sems, parts, lands, names, after, "grad_exchange_wait_" + tag)
    return [_sum_chips(n, parts[i], lands[i], chip) for i, n in enumerate(names)]


HI = lax.Precision.HIGHEST
MOD_COLS = 6 * D // N_CHIPS
COND_ROWS = 16


def _silu(v):
    return v * _sigmoid(v)


GATHER_ROWS = 8
FFW_COLS = 2 * DFF // N_CHIPS
CONV_COLS = DC // N_CHIPS
TAPS_PER_ROW = FFW_COLS // CONV_COLS
assert 4 + -(-CW // TAPS_PER_ROW) <= GATHER_ROWS


def _conv_tap_place(k):
    return 4 + k // TAPS_PER_ROW, (k % TAPS_PER_ROW) * CONV_COLS


def _pack_cond(c, ffn_w, conv_w):
    def body(c_ref, f_ref, w_ref, o_ref):
        o_ref[...] = jnp.zeros_like(o_ref)
        o_ref[0:1, 0:D] = c_ref[...]
        o_ref[1:4, :] = f_ref[...]
        for k in range(CW):
            row, lane = _conv_tap_place(k)
            o_ref[row:row + 1, lane:lane + CONV_COLS] = w_ref[k:k + 1, :]

    return _pallas(body, name="pack_cond", out_shape=_sds((GATHER_ROWS, FFW_COLS), F32))(c, ffn_w, conv_w)


def _unpack_cond(got, c_ctx):
    def body(g_ref, c_ref, cond_ref, f_ref, w_ref):
        cond_ref[...] = jnp.zeros_like(cond_ref)
        for d in range(8):
            cond_ref[d:d + 1, :] = g_ref[d * GATHER_ROWS:d * GATHER_ROWS + 1, 0:D]
        cond_ref[8:9, :] = c_ref[...]
        for j in range(N_CHIPS):
            r0 = 2 * j * GATHER_ROWS
            f_ref[:, j * FFW_COLS:(j + 1) * FFW_COLS] = g_ref[r0 + 1:r0 + 4, :]
            for k in range(CW):
                row, lane = _conv_tap_place(k)
                w_ref[k:k + 1, j * CONV_COLS:(j + 1) * CONV_COLS] = g_ref[r0 + row:r0 + row + 1, lane:lane + CONV_COLS]

    return _pallas(body, name="unpack_cond",
                   out_shape=[_sds((COND_ROWS, D), F32), _sds((3, 2 * DFF), F32), _sds((CW, DC), F32)])(got, c_ctx)


def _chip_cols(rows, width):
    return pl.BlockSpec((rows, width), lambda i, ch: (0, ch[0]))


def _whole(shape):
    return pl.BlockSpec(shape, lambda i, ch: (0,) * len(shape))


def _mod_shard(cond, w_mod, b_mod, chip):
    def body(ch_ref, c_ref, w_ref, b_ref, o_ref):
        del ch_ref
        o_ref[...] = jnp.dot(_silu(c_ref[...]), w_ref[...], preferred_element_type=F32, precision=HI) + b_ref[...]

    return _pallas(body, name="mod_fwd", prefetch=1, grid=(1,),
                   in_specs=[_whole((COND_ROWS, D)), _whole((D, MOD_COLS)), _chip_cols(1, MOD_COLS)],
                   out_specs=_whole((COND_ROWS, MOD_COLS)),
                   out_shape=_sds((COND_ROWS, MOD_COLS), F32))(chip, cond, w_mod, b_mod)


def _unpack_mod(mods, dev):
    def body(dev_ref, m_ref, me_ref, c_ref):
        rowi = lax.broadcasted_iota(jnp.int32, (COND_ROWS, MOD_COLS), 0)
        mine, ctx = [], []
        for j in range(N_CHIPS):
            blk = m_ref[2 * j * COND_ROWS:(2 * j + 1) * COND_ROWS, :]
            mine.append(jnp.sum(jnp.where(rowi == dev_ref[0], blk, 0.0), axis=0, keepdims=True))
            ctx.append(blk[8:9, :])
        mine = jnp.concatenate(mine, axis=1)
        ctx = jnp.concatenate(ctx, axis=1)
        for k in range(6):
            me_ref[k:k + 1, :] = mine[:, k * D:(k + 1) * D]
        for k in range(2):
            c_ref[k:k + 1, :] = ctx[:, k * D:(k + 1) * D]

    return _pallas(body, name="unpack_mod", prefetch=1, grid=(1,),
                   in_specs=[_whole(mods.shape)], out_specs=[_whole((6, D)), _whole((2, D))],
                   out_shape=[_sds((6, D), F32), _sds((2, D), F32)])(dev, mods)


MOD_TILE = 512


def _mod_weight_update(cond, dmod_all, w, m, v, chip):
    nt = MOD_COLS // MOD_TILE

    def body(ch_ref, c_ref, d_ref, w_ref, m_ref, v_ref, g_ref, dl_ref, nm_ref, nv_ref):
        del ch_ref
        g = lax.dot_general(_silu(c_ref[...]), d_ref[...], _TN, preferred_element_type=F32, precision=HI)
        g_ref[...] = g
        dl_ref[...], nm_ref[...], nv_ref[...] = _adam_math(w_ref[...], g, m_ref[...], v_ref[...])

    blk = pl.BlockSpec((D, MOD_TILE), lambda j, ch: (0, j))
    return _pallas(body, name="mod_weight_update", prefetch=1, grid=(nt,),
                   in_specs=[_whole((COND_ROWS, D)), pl.BlockSpec((COND_ROWS, MOD_TILE), lambda j, ch: (0, ch[0] * nt + j)),
                             blk, blk, blk],
                   out_specs=[blk] * 4, out_shape=[_sds((D, MOD_COLS), F32)] * 4,
                   semantics=("parallel",))(chip, cond, dmod_all, w, m, v)


def _cond_grad_partial(dmod_all, w_mod, chip):
    def body(ch_ref, d_ref, w_ref, o_ref):
        del ch_ref
        o_ref[...] = lax.dot_general(d_ref[...], w_ref[...], (((1,), (1,)), ((), ())), preferred_element_type=F32, precision=HI)

    return _pallas(body, name="cond_grad_partial", prefetch=1, grid=(1,),
                   in_specs=[pl.BlockSpec((8, MOD_COLS), lambda i, ch: (1, ch[0])), _whole((D, MOD_COLS))],
                   out_specs=_whole((8, D)), out_shape=_sds((8, D), F32))(chip, dmod_all, w_mod)


def _adam_math(w, g, m, v):
    nm = ADAM_B1 * m + (1.0 - ADAM_B1) * g
    nv = ADAM_B2 * v + (1.0 - ADAM_B2) * (g * g)
    c1 = 1.0 - ADAM_B1 ** ADAM_STEP
    c2 = 1.0 - ADAM_B2 ** ADAM_STEP
    return -ADAM_LR * ((nm / c1) / (jnp.sqrt(nv / c2) + ADAM_EPS) + ADAM_WD * w), nm, nv


def _cond_update(parts, c_ctx, m, v):
    def body(p_ref, c_ref, m_ref, v_ref, g_ref, d_ref, nm_ref, nv_ref):
        tot = p_ref[0:1, :]
        for j in range(1, N_CHIPS):
            tot = tot + p_ref[16 * j:16 * j + 1, :]
        cv = c_ref[...]
        sg = _sigmoid(cv)
        g = tot * (sg * (1.0 + cv * (1.0 - sg)))
        g_ref[...] = g
        d_ref[...], nm_ref[...], nv_ref[...] = _adam_math(cv, g, m_ref[...], v_ref[...])

    return _pallas(body, name="cond_update", out_shape=[_sds((1, D), F32)] * 4)(parts, c_ctx, m, v)


def _adamw_cols(w, g_all, m, v, chip, name):
    r, c = w.shape

    def body(ch_ref, w_ref, g_ref, m_ref, v_ref, go_ref, d_ref, nm_ref, nv_ref):
        del ch_ref
        g = g_ref[...]
        go_ref[...] = g
        d_ref[...], nm_ref[...], nv_ref[...] = _adam_math(w_ref[...], g, m_ref[...], v_ref[...])

    return _pallas(body, name=name, prefetch=1, grid=(1,),
                   in_specs=[_whole((r, c)), _chip_cols(r, c), _whole((r, c)), _whole((r, c))],
                   out_specs=[_whole((r, c))] * 4, out_shape=[_sds((r, c), F32)] * 4)(chip, w, g_all, m, v)


def _adamw_halves(name, w, own, other, m, v, core, after):
    r, c = w.shape
    half = r // 2
    t = _tile(half, (128, 352))
    nh = half // t

    def pick(mine):
        def index(i, cr):
            first = cr[0] if mine else 1 - cr[0]
            return (jnp.clip(i - first * nh, 0, nh - 1), 0)
        return pl.BlockSpec((t, c), index)

    def body(c_ref, w_ref, own_ref, oth_ref, m_ref, v_ref, after_ref, g_ref, d_ref, nm_ref, nv_ref):
        del after_ref
        g = jnp.where(pl.program_id(0) // nh == c_ref[0], own_ref[...], oth_ref[...])
        g_ref[...] = g
        d_ref[...], nm_ref[...], nv_ref[...] = _adam_math(w_ref[...], g, m_ref[...], v_ref[...])

    blk = pl.BlockSpec((t, c), lambda i, cr: (i, 0))
    return _pallas(body, name="adamw_" + name, prefetch=1, grid=(2 * nh,),
                   in_specs=[blk, pick(True), pick(False), blk, blk, pl.BlockSpec(memory_space=pl.ANY)], out_specs=[blk] * 4,
                   out_shape=[_sds((r, c), F32)] * 4, semantics=("parallel",))(core, w, own, other, m, v, after)


WEIGHTS = ("c_ctx", "w_mod", "b_mod", "g_norm1", "w_in", "rpb", "conv_w", "conv_b", "ln_g", "ln_b", "w_out", "g_norm2",
           "w_up", "ffn_conv_w", "ffn_conv_b", "w_down", "g_final")
PACK = (("dmod", 6 * D), ("dmod_c", 2 * D), ("g_norm1", D), ("g_norm1_ctx", D), ("g_norm2", D), ("g_final", D),
        ("conv_b", DC), ("ln_g", DC), ("ln_b", DC), ("ffn_conv_b", 2 * DFF), ("ffn_conv_w", 3 * 2 * DFF),
        ("conv_w", CW * DC), ("rpb_rev", NH * 16 * LANES), ("loss", LANES))
PACK_OFF = {}
_o = 0
for _n, _w in PACK:
    PACK_OFF[_n] = (_o, _w)
    _o += _w
PACK_N = -(-_o // (8 * LANES)) * (8 * LANES)
VECTORS = {"b_mod": (6 * D, ("dmod", "dmod_c")), "g_norm1": (D, ("g_norm1", "g_norm1_ctx")), "conv_b": (DC, ("conv_b",)),
           "ln_g": (DC, ("ln_g",)), "ln_b": (DC, ("ln_b",)), "g_norm2": (D, ("g_norm2",)),
           "ffn_conv_b": (2 * DFF, ("ffn_conv_b",)), "g_final": (D, ("g_final",))}
RPB_COLS = 4 * NA_ROWS - 1


def _pack_small(parts, after):
    arrs, places = [], []
    for name, _ in PACK:
        off, width = PACK_OFF[name]
        group = parts[name]
        rows = group[0].shape[0]
        row_w = sum(a.shape[1] for a in group)
        assert rows * row_w == width, (name, rows, row_w, width)
        col = 0
        for a in group:
            arrs.append(a)
            places.append([off + k * row_w + col for k in range(rows)])
            col += a.shape[1]

    def body(*refs):
        o_ref = refs[-1]
        o_ref[:, _o:PACK_N] = jnp.zeros((1, PACK_N - _o), F32)
        for ref, offs in zip(refs, places):
            n = ref.shape[1]
            for k, off in enumerate(offs):
                o_ref[:, off:off + n] = ref[k:k + 1, :]

    vmem = pl.BlockSpec(memory_space=pltpu.VMEM)
    return _pallas(body, name="pack_small_grads", out_shape=_sds((1, PACK_N), F32),
                   in_specs=[vmem] * len(arrs) + [pl.BlockSpec(memory_space=pl.ANY)] * len(_several(after)),
                   out_specs=vmem)(*arrs, *_several(after))


def _small_update(packs, w, m, v):
    names = list(VECTORS)

    def body(*refs):
        it = iter(refs)
        p_ref = next(it)
        wmv = {n: (next(it), next(it), next(it)) for n in names}
        outs = {n: (next(it), next(it), next(it), next(it)) for n in names}
        dmod_ref, cw_ref, fw_ref, rpb_ref, loss_ref = next(it), next(it), next(it), next(it), next(it)

        def total(name):
            off, width = PACK_OFF[name]
            acc = p_ref[0:1, off:off + width]
            for d in range(1, 8):
                acc = acc + p_ref[d:d + 1, off:off + width]
            return acc

        for n in names:
            width, segs = VECTORS[n]
            g = total(segs[0])
            if len(segs) > 1:
                extra = total(segs[1])
                ew = extra.shape[1]
                g = g + extra if ew == width else jnp.concatenate([g[:, :ew] + extra, g[:, ew:]], axis=1)
            w_ref, m_ref, v_ref = wmv[n]
            g_ref, d_ref, nm_ref, nv_ref = outs[n]
            g_ref[...] = g
            d_ref[...], nm_ref[...], nv_ref[...] = _adam_math(w_ref[...], g, m_ref[...], v_ref[...])

        o_dmod = PACK_OFF["dmod"][0]
        dmod_ref[...] = jnp.zeros_like(dmod_ref)
        dmod_ref[0:8, :] = p_ref[:, o_dmod:o_dmod + 6 * D]
        dmod_ref[8:9, 0:2 * D] = total("dmod_c")
        for ref, name, rows in ((cw_ref, "conv_w", CW), (fw_ref, "ffn_conv_w", 3), (rpb_ref, "rpb_rev", NH * 16)):
            flat = total(name)
            n = ref.shape[1]
            for k in range(rows):
                ref[k:k + 1, :] = flat[:, k * n:(k + 1) * n]
        loss_ref[...] = total("loss")

    ins = [packs] + [a[n] for n in names for a in (w, m, v)]
    out_shape = [_sds((1, VECTORS[n][0]), F32) for n in names for _ in range(4)]
    out_shape += [_sds((COND_ROWS, 6 * D), F32), _sds((CW, DC), F32), _sds((3, 2 * DFF), F32), _sds((NH * 16, LANES), F32),
                  _sds((1, LANES), F32)]
    res = _pallas(body, name="small_update", out_shape=out_shape)(*ins)
    per = {n: tuple(res[4 * i:4 * i + 4]) for i, n in enumerate(names)}
    return (per, *res[4 * len(names):])


def _rpb_update(rev, w, m, v):
    def body(r_ref, w_ref, m_ref, v_ref, g_ref, d_ref, nm_ref, nv_ref):
        li = lax.broadcasted_iota(jnp.int32, (LANES, LANES), 0)
        co = lax.broadcasted_iota(jnp.int32, (LANES, LANES), 1)
        lane_of_co0 = GW - 1 + RPB_COLS // 2
        unflip = jnp.where((li == lane_of_co0 - co) & (co < RPB_COLS), 1.0, 0.0).astype(F32)
        g_all = jnp.dot(r_ref[...], unflip, preferred_element_type=F32, precision=HI)
        nr = 2 * NA_ROWS - 1
        for h in range(NH):
            g = g_all[h * 16:h * 16 + nr, 0:RPB_COLS]
            g_ref[0, h] = g
            d_ref[0, h], nm_ref[0, h], nv_ref[0, h] = _adam_math(w_ref[0, h], g, m_ref[0, h], v_ref[0, h])

    return _pallas(body, name="rpb_update", out_shape=[_sds(w.shape, F32)] * 4)(rev, w, m, v)


def kernel(x, c, ctx, c_ctx, w_mod, b_mod, g_norm1, w_in, rpb, conv_w, conv_b, ln_g, ln_b, w_out, g_norm2, w_up, ffn_conv_w, ffn_conv_b, w_down, g_final, loss_target, m_c_ctx, m_w_mod, m_b_mod, m_g_norm1, m_w_in, m_rpb, m_conv_w, m_conv_b, m_ln_g, m_ln_b, m_w_out, m_g_norm2, m_w_up, m_ffn_conv_w, m_ffn_conv_b, m_w_down, m_g_final, v_c_ctx, v_w_mod, v_b_mod, v_g_norm1, v_w_in, v_rpb, v_conv_w, v_conv_b, v_ln_g, v_ln_b, v_w_out, v_g_norm2, v_w_up, v_ffn_conv_w, v_ffn_conv_b, v_w_down, v_g_final):
    w = dict(c_ctx=c_ctx, w_mod=w_mod, b_mod=b_mod, g_norm1=g_norm1, w_in=w_in, rpb=rpb, conv_w=conv_w, conv_b=conv_b,
             ln_g=ln_g, ln_b=ln_b, w_out=w_out, g_norm2=g_norm2, w_up=w_up, ffn_conv_w=ffn_conv_w, ffn_conv_b=ffn_conv_b,
             w_down=w_down, g_final=g_final)
    mom = dict(c_ctx=m_c_ctx, w_mod=m_w_mod, b_mod=m_b_mod, g_norm1=m_g_norm1, w_in=m_w_in, rpb=m_rpb, conv_w=m_conv_w,
               conv_b=m_conv_b, ln_g=m_ln_g, ln_b=m_ln_b, w_out=m_w_out, g_norm2=m_g_norm2, w_up=m_w_up,
               ffn_conv_w=m_ffn_conv_w, ffn_conv_b=m_ffn_conv_b, w_down=m_w_down, g_final=m_g_final)
    var = dict(c_ctx=v_c_ctx, w_mod=v_w_mod, b_mod=v_b_mod, g_norm1=v_g_norm1, w_in=v_w_in, rpb=v_rpb, conv_w=v_conv_w,
               conv_b=v_conv_b, ln_g=v_ln_g, ln_b=v_ln_b, w_out=v_w_out, g_norm2=v_g_norm2, w_up=v_w_up,
               ffn_conv_w=v_ffn_conv_w, ffn_conv_b=v_ffn_conv_b, w_down=v_w_down, g_final=v_g_final)
    xi, yi, ci = lax.axis_index("x"), lax.axis_index("y"), lax.axis_index("c")
    dev = (4 * xi + 2 * yi + ci).astype(jnp.int32).reshape(1)
    chip = (2 * xi + yi).astype(jnp.int32).reshape(1)
    core = ci.astype(jnp.int32).reshape(1)
    c_ctx2 = c_ctx.reshape(1, D)
    g_final2 = g_final.reshape(1, D)
    mom["g_final"], var["g_final"] = m_g_final.reshape(1, D), v_g_final.reshape(1, D)

    sharing_cond = _share_start(_pack_cond(c, ffn_conv_w[0], conv_w[0]), "cond", after=[])
    shards = {n: _cast_into_whole(n, w[n][0], chip) for n in BIG_NAMES}
    cond, ffn_w_all, conv_w_all = _unpack_cond(_share_wait(sharing_cond, list(shards.values()), "cond"), c_ctx2)

    mods = _share_wait(_share_start(_mod_shard(cond, w_mod[0], b_mod, chip), "mod", after=[]), [], "mod")
    mod_me, mod_c = _unpack_mod(mods, dev)

    sems_in, first, token_in = _gather_start([shards["w_in"]], ("w_in",), mod_me, "w_in")
    sems, late, token = _gather_start([shards[n] for n in LATE_NAMES], LATE_NAMES, token_in, "late")
    mod_me = mod_me + token[0:1, 0:1]

    def w_in_all(after):
        arrived = _gather_wait(sems_in, first, ("w_in",), after, "w_in")
        return _forward_halves(list(arrived), ("w_in",), "w_in")[0]

    def late_weights(after):
        arrived = list(_gather_wait(sems, late, LATE_NAMES, after, "late"))
        (w_out_all,) = _forward_halves(arrived[:1], LATE_NAMES[:1], "w_out")
        fsems, passing, _ = _forward_start(arrived[1:], LATE_NAMES[1:], "ffn", after=w_out_all)
        return w_out_all, lambda after2: _forward_wait(fsems, passing, LATE_NAMES[1:], after2, "ffn")

    rpb_rev = jnp.pad(rpb[0][:, :, ::-1], ((0, 0), (0, 1), (48, LANES - 48 - RPB_COLS))).reshape(NH * 16, LANES)
    vec = dict(g_norm1=g_norm1, g_norm2=g_norm2, g_final=g_final2, conv_w=conv_w_all, conv_b=conv_b, ln_g=ln_g, ln_b=ln_b,
               ffn_conv_w=ffn_w_all, ffn_conv_b=ffn_conv_b)
    started = []

    def begin_early(d_up, d_down):
        started.append(_swap_start([d_up, d_down], EARLY_GRADS, "grad_swap_start_early"))

    def carry_on_early(after):
        sems_, grads_, lands_, _ = started.pop()
        grads_, lands_ = _swap_wait(sems_, grads_, lands_, EARLY_GRADS, after, "grad_swap_wait_early")
        parts_ = [_add_halves(n, grads_[i], lands_[i], core) for i, n in enumerate(EARLY_GRADS)]
        started.append(_exchange_start(parts_, EARLY_GRADS, "grad_exchange_start_early"))
        return started[0][3][0:1, 0:1]

    loss_p, grad_x, d_in, d_out, d_up, d_down, small = _local_step(
        x[0], ctx[0], loss_target[0], mod_me, mod_c, vec, w_in_all, late_weights, rpb_rev, (begin_early, carry_on_early))

    out = {}
    sems_, grads_, lands_, _ = _swap_start([d_in, d_out], LAST_GRADS, "grad_swap_start_last")
    early_own = _reduce_finish(started[0], EARLY_GRADS, grad_x, chip, "early")
    parts = dict(dmod=small["dmod"], dmod_c=small["dmod_c"], g_norm1=[small["g_norm1"][0]], g_norm1_ctx=[small["g_norm1"][1]],
                 g_norm2=[small["g_norm2"]], g_final=[small["g_final"]], conv_b=[small["conv_b"]], ln_g=[small["ln_g"]],
                 ln_b=[small["ln_b"]], ffn_conv_b=small["ffn_conv_b"], ffn_conv_w=small["ffn_conv_w"],
                 conv_w=[small["conv_w"]], rpb_rev=[small["rpb_rev"]], loss=[loss_p])
    pack = _pack_small(parts, after=early_own).reshape(8, PACK_N // 8)
    sharing = _share_start(pack, "small_grads", after=[])
    grads_, lands_ = _swap_wait(sems_, grads_, lands_, LAST_GRADS, sharing[2], "grad_swap_wait_last")
    parts_ = [_add_halves(n, grads_[i], lands_[i], core) for i, n in enumerate(LAST_GRADS)]
    last_started = _exchange_start(parts_, LAST_GRADS, "grad_exchange_start_last")
    early_other = _send_halves(early_own, "grad_send_early", after=last_started[3])
    for i, n in enumerate(EARLY_GRADS):
        out[n] = _adamw_halves(n, w[n][0], early_own[i], early_other[i], mom[n][0], var[n][0], core, early_other[i])

    packs = _share_wait(sharing, [out[n][1] for n in EARLY_GRADS], "small_grads").reshape(8, PACK_N)
    w2 = dict(w, g_final=g_final2)
    per, dmod_all, g_conv_w_all, g_ffn_w_all, g_rpb_rev, loss_row = _small_update(packs, w2, mom, var)
    out.update(per)
    out["w_mod"] = _mod_weight_update(cond, dmod_all, w_mod[0], m_w_mod[0], v_w_mod[0], chip)

    sharing_c = _share_start(_cond_grad_partial(dmod_all, w_mod[0], chip), "cond_grad", after=out["w_mod"][1])
    last_own = _reduce_finish(last_started, LAST_GRADS, sharing_c[2], chip, "last")
    last_other = _send_halves(last_own, "grad_send_last", after=last_own[0])
    for i, n in enumerate(LAST_GRADS):
        out[n] = _adamw_halves(n, w[n][0], last_own[i], last_other[i], mom[n][0], var[n][0], core, last_other[i])
    out["c_ctx"] = _cond_update(_share_wait(sharing_c, [out[n][1] for n in LAST_GRADS], "cond_grad"),
                                c_ctx2, m_c_ctx.reshape(1, D), v_c_ctx.reshape(1, D))
    out["conv_w"] = _adamw_cols(conv_w[0], g_conv_w_all, m_conv_w[0], v_conv_w[0], chip, "adamw_conv_w")
    out["ffn_conv_w"] = _adamw_cols(ffn_conv_w[0], g_ffn_w_all, m_ffn_conv_w[0], v_ffn_conv_w[0], chip, "adamw_ffn_conv_w")
    out["rpb"] = _rpb_update(g_rpb_rev, rpb, m_rpb, v_rpb)

    res = [[out[n][k].reshape(w[n].shape) for n in WEIGHTS] for k in range(4)]
    return (loss_row[0, 0], grad_x[None], *res[0], *res[1], *res[2], *res[3])
```

```python
import jax
import jax.numpy as jnp
from jax import lax
from jax.experimental import pallas as pl
from jax.experimental.pallas import tpu as pltpu

F32 = jnp.float32
BF16 = jnp.bfloat16
MXU_DTYPE = jnp.bfloat16

D = 1024
CTX = 256
GW = 64
DA = 512
NH = 8
HD = 64
DC = 512
CW = 31
DFF = 2816
NIN = 3 * DA + 2 * DC
EPS = 1e-6
SCALE = HD ** -0.5
NEG = -1e30
NA_ROWS = 8
PAIR_ROWS = NA_ROWS + 1
TAB_BLOCKS = 17
LANES = 128
VMEM_LIMIT = 56 * 1024 * 1024

ADAM_LR = 0.001
ADAM_B1 = 0.9
ADAM_B2 = 0.999
ADAM_EPS = 1e-08
ADAM_WD = 0.01
ADAM_STEP = 10

MESH = pl.DeviceIdType.MESH


def _pallas(body, *, name, semantics=None, vmem=VMEM_LIMIT, prefetch=0, **kw):
    params = dict(vmem_limit_bytes=vmem)
    if semantics is not None:
        params["dimension_semantics"] = semantics
    if prefetch:
        kw["grid_spec"] = pltpu.PrefetchScalarGridSpec(
            num_scalar_prefetch=prefetch, grid=kw.pop("grid"), in_specs=kw.pop("in_specs"), out_specs=kw.pop("out_specs"),
            scratch_shapes=kw.pop("scratch_shapes", ()))
    return pl.pallas_call(body, name=name, compiler_params=pltpu.CompilerParams(**params), **kw)


def _sds(shape, dtype):
    return jax.ShapeDtypeStruct(shape, dtype)


def _vec_spec(n):
    return pl.BlockSpec((1, n), lambda *_: (0, 0))


def _colsum8(x):
    t, n = x.shape
    return jnp.sum(x.reshape(t // 8, 8, n), axis=0)


def _sigmoid(x):
    return 0.5 * jnp.tanh(0.5 * x) + 0.5


def _mm(a, b, *, mode, m, n, k, tm, tn, tk, out_dtype, name, a_off=(0, 0), b_off=(0, 0),
        out_total=None, o_off=(0, 0), into=None):
    a_list = list(a) if isinstance(a, (list, tuple)) else [a]
    b_list = list(b) if isinstance(b, (list, tuple)) else [b]
    assert m % tm == 0 and n % tn == 0 and k % tk == 0, (name, m, n, k, tm, tn, tk)
    gi, gj, nk = m // tm, n // tn, k // tk
    dims = {"nn": (((1,), (0,)), ((), ())), "nt": (((1,), (1,)), ((), ())), "tn": (((0,), (0,)), ((), ()))}[mode]

    if len(a_list) > 1:
        assert mode != "tn" and nk == 1 and sum(x.shape[1] for x in a_list) == k
        a_specs = [pl.BlockSpec((tm, x.shape[1]), lambda i, j, kk: (i, 0)) for x in a_list]
    elif mode == "tn":
        a_specs = [pl.BlockSpec((tk, tm), lambda i, j, kk: (kk + a_off[0], i + a_off[1]))]
    else:
        a_specs = [pl.BlockSpec((tm, tk), lambda i, j, kk: (i + a_off[0], kk + a_off[1]))]
    if len(b_list) > 1:
        assert mode == "tn" and gj == 1 and sum(x.shape[1] for x in b_list) == n
        b_specs = [pl.BlockSpec((tk, x.shape[1]), lambda i, j, kk: (kk, 0)) for x in b_list]
    elif mode == "nt":
        b_specs = [pl.BlockSpec((tn, tk), lambda i, j, kk: (j + b_off[0], kk + b_off[1]))]
    else:
        b_specs = [pl.BlockSpec((tk, tn), lambda i, j, kk: (kk + b_off[0], j + b_off[1]))]

    na, nb = len(a_list), len(b_list)
    in_place = nk > 1 and out_dtype == F32
    n_in = na + nb + (into is not None)

    def body(*refs):
        a_refs, b_refs, o_ref = refs[:na], refs[na:na + nb], refs[n_in]
        acc = o_ref if in_place else (refs[n_in + 1] if nk > 1 else None)
        kk = pl.program_id(2)

        def whole(piece_refs):
            vals = [r[...].astype(MXU_DTYPE) for r in piece_refs]
            return vals[0] if len(vals) == 1 else jnp.concatenate(vals, axis=1)

        p = lax.dot_general(whole(a_refs), whole(b_refs), dims, preferred_element_type=F32)
        if nk == 1:
            o_ref[...] = p.astype(out_dtype)
            return

        @pl.when(kk == 0)
        def _():
            acc[...] = p

        @pl.when(kk > 0)
        def _():
            acc[...] += p

        if not in_place:
            @pl.when(kk == nk - 1)
            def _():
                o_ref[...] = acc[...].astype(out_dtype)

    ins = [*a_list, *b_list]
    in_specs = a_specs + b_specs
    extra = {}
    if into is not None:
        extra["input_output_aliases"] = {len(ins): 0}
        ins.append(into)
        in_specs.append(pl.BlockSpec(memory_space=pl.ANY))
    return _pallas(
        body, name=name, grid=(gi, gj, nk), in_specs=in_specs,
        out_specs=pl.BlockSpec((tm, tn), lambda i, j, kk: (i + o_off[0], j + o_off[1])),
        out_shape=_sds(out_total or (m, n), out_dtype),
        scratch_shapes=[pltpu.VMEM((tm, tn), F32)] if nk > 1 and not in_place else [],
        semantics=("parallel", "parallel", "arbitrary"), **extra,
    )(*ins)


ROW_TILE = 256


def _row_tile(s, most=2):
    for k in (4, 2):
        if k <= most and s % (k * ROW_TILE) == 0:
            return k * ROW_TILE
    return ROW_TILE


def _rmsmod_fwd(x, ctx, g, sc, sh, csc, csh):
    s = x.shape[0]
    nt = s // ROW_TILE
    assert ctx.shape[0] == ROW_TILE

    def body(x_ref, c_ref, g_ref, sc_ref, sh_ref, csc_ref, csh_ref, o_ref):
        is_ctx = pl.program_id(0) == nt
        xv = jnp.where(is_ctx, c_ref[...], x_ref[...])
        scv = jnp.where(is_ctx, csc_ref[...], sc_ref[...])
        shv = jnp.where(is_ctx, csh_ref[...], sh_ref[...])
        r = lax.rsqrt(jnp.mean(xv * xv, axis=-1, keepdims=True) + EPS)
        y = xv * r * g_ref[...]
        o_ref[...] = (y * (1.0 + scv) + shv).astype(o_ref.dtype)

    return _pallas(
        body, name="rmsmod1_fwd", grid=(nt + 1,),
        in_specs=[pl.BlockSpec((ROW_TILE, D), lambda i: (jnp.minimum(i, nt - 1), 0)),
                  pl.BlockSpec((ROW_TILE, D), lambda i: (0, 0))] + [_vec_spec(D)] * 5,
        out_specs=pl.BlockSpec((ROW_TILE, D), lambda i: (i, 0)),
        out_shape=_sds((s + CTX, D), MXU_DTYPE),
        semantics=("arbitrary",),
    )(x, ctx, g, sc, sh, csc, csh)


def _resid_rmsmod_fwd(x, y, gt, g, sc, sh):
    s = x.shape[0]

    def body(x_ref, y_ref, gt_ref, g_ref, sc_ref, sh_ref, x1_ref, h_ref):
        x1 = x_ref[...] + gt_ref[...] * y_ref[...]
        x1_ref[...] = x1
        r = lax.rsqrt(jnp.mean(x1 * x1, axis=-1, keepdims=True) + EPS)
        h_ref[...] = ((x1 * r * g_ref[...]) * (1.0 + sc_ref[...]) + sh_ref[...]).astype(h_ref.dtype)

    t = _row_tile(s)
    row = pl.BlockSpec((t, D), lambda i: (i, 0))
    return _pallas(
        body, name="resid_rmsmod2_fwd", grid=(s // t,),
        in_specs=[row, row] + [_vec_spec(D)] * 4,
        out_specs=[row, row],
        out_shape=[_sds((s, D), F32), _sds((s, D), MXU_DTYPE)],
        semantics=("parallel",),
    )(x, y, gt, g, sc, sh)


def _final_fwd_bwd(x1, z, gt2, gf, tgt):
    s = x1.shape[0]
    tile = _row_tile(s)
    nt = s // tile

    def body(x1_ref, z_ref, gt_ref, gf_ref, t_ref, dx2_ref, dz_ref, loss_ref, dgt_ref, dgf_ref, a_loss, a_gt, a_gf):
        i = pl.program_id(0)

        @pl.when(i == 0)
        def _():
            a_loss[...] = jnp.zeros_like(a_loss)
            a_gt[...] = jnp.zeros_like(a_gt)
            a_gf[...] = jnp.zeros_like(a_gf)

        zv = z_ref[...]
        gt = gt_ref[...]
        gf_ = gf_ref[...]
        x2 = x1_ref[...] + gt * zv
        r = lax.rsqrt(jnp.mean(x2 * x2, axis=-1, keepdims=True) + EPS)
        xn = x2 * r
        e = xn * gf_ - t_ref[...]
        a_loss[...] += _colsum8(e * e)
        dyo = e * (1.0 / D)
        a_gf[...] += _colsum8(dyo * xn)
        gdy = gf_ * dyo
        dx2 = r * gdy - xn * (r * r) * jnp.mean(x2 * gdy, axis=-1, keepdims=True)
        dx2_ref[...] = dx2
        dz_ref[...] = (gt * dx2).astype(dz_ref.dtype)
        a_gt[...] += _colsum8(dx2 * zv)

        @pl.when(i == nt - 1)
        def _():
            tot = jnp.sum(jnp.sum(a_loss[...], axis=0, keepdims=True), axis=1, keepdims=True) * (0.5 / D)
            loss_ref[...] = jnp.broadcast_to(tot, loss_ref.shape)
            dgt_ref[...] = jnp.sum(a_gt[...], axis=0, keepdims=True)
            dgf_ref[...] = jnp.sum(a_gf[...], axis=0, keepdims=True)

    row = pl.BlockSpec((tile, D), lambda i: (i, 0))
    return _pallas(
        body, name="final_norm_loss", grid=(nt,),
        in_specs=[row, row, _vec_spec(D), _vec_spec(D), row],
        out_specs=[row, row, _vec_spec(LANES), _vec_spec(D), _vec_spec(D)],
        out_shape=[_sds((s, D), F32), _sds((s, D), MXU_DTYPE), _sds((1, LANES), F32), _sds((1, D), F32), _sds((1, D), F32)],
        scratch_shapes=[pltpu.VMEM((8, D), F32)] * 3,
        semantics=("arbitrary",),
    )(x1, z, gt2, gf, tgt)


def _rmsmod_bwd(xin, dh, g, sc, *, name, dh_row0=0, add=None, resid=None):
    s = xin.shape[0]
    tile = _row_tile(s)
    nt = s // tile
    want_dx = add is not None
    assert resid is None or want_dx

    def body(*refs):
        it = iter(refs)
        x_ref, dh_ref, g_ref, sc_ref = next(it), next(it), next(it), next(it)
        add_ref = next(it) if want_dx else None
        gt_ref, y_ref = (next(it), next(it)) if resid is not None else (None, None)
        dsh_ref, dsc_ref, dg_ref = next(it), next(it), next(it)
        dx_ref = next(it) if want_dx else None
        dy_ref, dgt_ref = (next(it), next(it)) if resid is not None else (None, None)
        a_sh, a_sc, a_g = next(it), next(it), next(it)
        a_gt = next(it) if resid is not None else None
        i = pl.program_id(0)

        @pl.when(i == 0)
        def _():
            a_sh[...] = jnp.zeros_like(a_sh)
            a_sc[...] = jnp.zeros_like(a_sc)
            a_g[...] = jnp.zeros_like(a_g)
            if a_gt is not None:
                a_gt[...] = jnp.zeros_like(a_gt)

        xv = x_ref[...]
        dhv = dh_ref[...]
        gv = g_ref[...]
        r = lax.rsqrt(jnp.mean(xv * xv, axis=-1, keepdims=True) + EPS)
        xn = xv * r
        a_sh[...] += _colsum8(dhv)
        a_sc[...] += _colsum8(dhv * (xn * gv))
        dn = dhv * (1.0 + sc_ref[...])
        a_g[...] += _colsum8(dn * xn)
        if want_dx:
            gdn = gv * dn
            dx = add_ref[...] + r * gdn - xn * (r * r) * jnp.mean(xv * gdn, axis=-1, keepdims=True)
            dx_ref[...] = dx
            if resid is not None:
                dy_ref[...] = (gt_ref[...] * dx).astype(dy_ref.dtype)
                a_gt[...] += _colsum8(dx * y_ref[...])

        @pl.when(i == nt - 1)
        def _():
            dsh_ref[...] = jnp.sum(a_sh[...], axis=0, keepdims=True)
            dsc_ref[...] = jnp.sum(a_sc[...], axis=0, keepdims=True)
            dg_ref[...] = jnp.sum(a_g[...], axis=0, keepdims=True)
            if a_gt is not None:
                dgt_ref[...] = jnp.sum(a_gt[...], axis=0, keepdims=True)

    row = pl.BlockSpec((tile, D), lambda i: (i, 0))
    ins = [xin, dh, g, sc]
    in_specs = [row, pl.BlockSpec((tile, D), lambda i: (i + dh_row0 // tile, 0)), _vec_spec(D), _vec_spec(D)]
    out_specs = [_vec_spec(D)] * 3
    out_shape = [_sds((1, D), F32)] * 3
    scratch = [pltpu.VMEM((8, D), F32)] * 3
    if want_dx:
        ins.append(add)
        in_specs.append(row)
        out_specs.append(row)
        out_shape.append(_sds((s, D), F32))
    if resid is not None:
        ins += [resid[0], resid[1]]
        in_specs += [_vec_spec(D), row]
        out_specs += [row, _vec_spec(D)]
        out_shape += [_sds((s, D), MXU_DTYPE), _sds((1, D), F32)]
        scratch.append(pltpu.VMEM((8, D), F32))
    return _pallas(body, name=name, grid=(nt,), in_specs=in_specs, out_specs=out_specs, out_shape=out_shape,
                   scratch_shapes=scratch, semantics=("arbitrary",))(*ins)


FF_TILE = 128
FF_CHUNK = 128
HALO = 8


def _shift3(pad_ref, r0, ch):
    return tuple(pad_ref[pl.ds(r0 + HALO + d, ch), :] for d in (-1, 0, 1))


def _fill_padded(pad_ref, src_ref, s, ch, halo):
    zeros = jnp.zeros((halo, pad_ref.shape[1]), F32)
    pad_ref[0:halo, :] = zeros
    pad_ref[s + halo:s + 2 * halo, :] = zeros

    def cp(c, carry):
        r0 = pl.multiple_of(c * ch, ch)
        pad_ref[pl.ds(r0 + halo, ch), :] = src_ref[pl.ds(r0, ch), :].astype(F32)
        return carry

    lax.fori_loop(0, s // ch, cp, 0)


def _ffn_act_fwd(u, w, b):
    s = u.shape[0]
    tile, ch = FF_TILE, FF_CHUNK
    nj = DFF // tile

    def body(ug_ref, uv_ref, wg_ref, wv_ref, bg_ref, bv_ref, f_ref, gpad, vpad):
        _fill_padded(gpad, ug_ref, s, ch, HALO)
        _fill_padded(vpad, uv_ref, s, ch, HALO)

        def conv(pad, w_ref, b_ref, r0):
            prev, cur, nxt = _shift3(pad, r0, ch)
            return w_ref[0:1, :] * prev + w_ref[1:2, :] * cur + w_ref[2:3, :] * nxt + b_ref[...]

        def step(c, carry):
            r0 = pl.multiple_of(c * ch, ch)
            gc = conv(gpad, wg_ref, bg_ref, r0)
            vc = conv(vpad, wv_ref, bv_ref, r0)
            f_ref[pl.ds(r0, ch), :] = (gc * _sigmoid(gc) * vc).astype(f_ref.dtype)
            return carry

        lax.fori_loop(0, s // ch, step, 0)

    col = lambda off: pl.BlockSpec((s, tile), lambda j: (0, j + off))
    wsp = lambda off: pl.BlockSpec((3, tile), lambda j: (0, j + off))
    bsp = lambda off: pl.BlockSpec((1, tile), lambda j: (0, j + off))
    return _pallas(
        body, name="ffn_act_fwd", grid=(nj,),
        in_specs=[col(0), col(nj), wsp(0), wsp(nj), bsp(0), bsp(nj)],
        out_specs=col(0), out_shape=_sds((s, DFF), MXU_DTYPE),
        scratch_shapes=[pltpu.VMEM((s + 2 * HALO, tile), F32)] * 2,
        semantics=("parallel",),
    )(u, u, w, w, b, b)


def _ffn_act_bwd(u, df, w, b):
    s = u.shape[0]
    nj = DFF // FF_TILE
    ch = FF_CHUNK

    def body(ug_ref, uv_ref, df_ref, wg_ref, wv_ref, bg_ref, bv_ref,
             dug_ref, duv_ref, dwg_ref, dwv_ref, dbg_ref, dbv_ref, gpad, vpad, dgpad, dvpad, acc):
        _fill_padded(gpad, ug_ref, s, ch, HALO)
        _fill_padded(vpad, uv_ref, s, ch, HALO)
        zeros = jnp.zeros((HALO, FF_TILE), F32)
        for p in (dgpad, dvpad):
            p[0:HALO, :] = zeros
            p[s + HALO:s + 2 * HALO, :] = zeros
        acc[...] = jnp.zeros_like(acc)

        def step(c, carry):
            r0 = pl.multiple_of(c * ch, ch)
            gs = _shift3(gpad, r0, ch)
            vs = _shift3(vpad, r0, ch)
            gc = wg_ref[0:1, :] * gs[0] + wg_ref[1:2, :] * gs[1] + wg_ref[2:3, :] * gs[2] + bg_ref[...]
            vc = wv_ref[0:1, :] * vs[0] + wv_ref[1:2, :] * vs[1] + wv_ref[2:3, :] * vs[2] + bv_ref[...]
            sg = _sigmoid(gc)
            dfv = df_ref[pl.ds(r0, ch), :].astype(F32)
            dgc = dfv * vc * (sg * (1.0 + gc * (1.0 - sg)))
            dvc = dfv * (gc * sg)
            dgpad[pl.ds(r0 + HALO, ch), :] = dgc
            dvpad[pl.ds(r0 + HALO, ch), :] = dvc
            for t in range(3):
                acc[8 * t:8 * t + 8, :] += _colsum8(dgc * gs[t])
                acc[24 + 8 * t:32 + 8 * t, :] += _colsum8(dvc * vs[t])
            acc[48:56, :] += _colsum8(dgc)
            acc[56:64, :] += _colsum8(dvc)
            return carry

        lax.fori_loop(0, s // ch, step, 0)

        def step2(c, carry):
            r0 = pl.multiple_of(c * ch, ch)
            for pad, w_ref, o_ref in ((dgpad, wg_ref, dug_ref), (dvpad, wv_ref, duv_ref)):
                prev, cur, nxt = _shift3(pad, r0, ch)
                o_ref[pl.ds(r0, ch), :] = (w_ref[0:1, :] * nxt + w_ref[1:2, :] * cur + w_ref[2:3, :] * prev).astype(o_ref.dtype)
            return carry

        lax.fori_loop(0, s // ch, step2, 0)
        for t in range(3):
            dwg_ref[t:t + 1, :] = jnp.sum(acc[8 * t:8 * t + 8, :], axis=0, keepdims=True)
            dwv_ref[t:t + 1, :] = jnp.sum(acc[24 + 8 * t:32 + 8 * t, :], axis=0, keepdims=True)
        dbg_ref[...] = jnp.sum(acc[48:56, :], axis=0, keepdims=True)
        dbv_ref[...] = jnp.sum(acc[56:64, :], axis=0, keepdims=True)

    col = lambda off: pl.BlockSpec((s, FF_TILE), lambda j: (0, j + off))
    wsp = lambda off: pl.BlockSpec((3, FF_TILE), lambda j: (0, j + off))
    bsp = lambda off: pl.BlockSpec((1, FF_TILE), lambda j: (0, j + off))
    return _pallas(
        body, name="ffn_act_bwd", grid=(nj,),
        in_specs=[col(0), col(nj), col(0), wsp(0), wsp(nj), bsp(0), bsp(nj)],
        out_specs=[col(0), col(0), wsp(0), wsp(0), bsp(0), bsp(0)],
        out_shape=[_sds((s, DFF), MXU_DTYPE)] * 2 + [_sds((3, DFF), F32)] * 2 + [_sds((1, DFF), F32)] * 2,
        scratch_shapes=[pltpu.VMEM((s + 2 * HALO, FF_TILE), F32)] * 4 + [pltpu.VMEM((64, FF_TILE), F32)],
        semantics=("parallel",),
    )(u, u, df, w, w, b, b)


CONV_CHUNK = 64
CONV_HALO = 16


def _tap(pad_ref, r0, k):
    return pad_ref[pl.ds(r0 + CONV_HALO - CW // 2 + k, CONV_CHUNK), :]


def _glu_into(pad_ref, a_ref, g_ref, s):
    zeros = jnp.zeros((CONV_HALO, LANES), F32)
    pad_ref[0:CONV_HALO, :] = zeros
    pad_ref[s + CONV_HALO:s + 2 * CONV_HALO, :] = zeros

    def cp(c, carry):
        r0 = pl.multiple_of(c * ROW_TILE, ROW_TILE)
        pad_ref[pl.ds(r0 + CONV_HALO, ROW_TILE), :] = a_ref[pl.ds(r0, ROW_TILE), :] * _sigmoid(g_ref[pl.ds(r0, ROW_TILE), :])
        return carry

    lax.fori_loop(0, s // ROW_TILE, cp, 0)


def _conf_conv_fwd(ag, conv_w, conv_b):
    s = ag.shape[0]
    nc = DC // LANES

    def body(a_ref, g_ref, w_ref, b_ref, o_ref, upad):
        _glu_into(upad, a_ref, g_ref, s)

        def step(c, carry):
            r0 = pl.multiple_of(c * CONV_CHUNK, CONV_CHUNK)
            acc = jnp.broadcast_to(b_ref[...], (CONV_CHUNK, LANES))
            for k in range(CW):
                acc = acc + w_ref[k:k + 1, :] * _tap(upad, r0, k)
            o_ref[pl.ds(r0, CONV_CHUNK), :] = acc
            return carry

        lax.fori_loop(0, s // CONV_CHUNK, step, 0)

    col = lambda off: pl.BlockSpec((s, LANES), lambda c: (0, c + off))
    return _pallas(
        body, name="conf_conv_fwd", grid=(nc,),
        in_specs=[col(0), col(nc), pl.BlockSpec((CW, LANES), lambda c: (0, c)), pl.BlockSpec((1, LANES), lambda c: (0, c))],
        out_specs=col(0), out_shape=_sds((s, DC), F32),
        scratch_shapes=[pltpu.VMEM((s + 2 * CONV_HALO, LANES), F32)],
        semantics=("parallel",),
    )(ag, ag, conv_w, conv_b)


def _ln_stats(x):
    mu = jnp.mean(x, axis=-1, keepdims=True)
    xc = x - mu
    var = jnp.mean(xc * xc, axis=-1, keepdims=True)
    rstd = lax.rsqrt(var + EPS)
    return xc * rstd, rstd


def _conf_ln_fwd(u1, ln_g, ln_b, ycat):
    s = u1.shape[0]

    def body(u_ref, g_ref, b_ref, ycat_ref, o_ref):
        del ycat_ref
        xhat, _ = _ln_stats(u_ref[...])
        y = xhat * g_ref[...] + b_ref[...]
        o_ref[...] = (y * _sigmoid(y)).astype(o_ref.dtype)

    t = _row_tile(s, 4)
    return _pallas(
        body, name="conf_ln_fwd", grid=(s // t,),
        in_specs=[pl.BlockSpec((t, DC), lambda i: (i, 0)), _vec_spec(DC), _vec_spec(DC),
                  pl.BlockSpec(memory_space=pl.ANY)],
        out_specs=pl.BlockSpec((t, DC), lambda i: (i, 1)),
        out_shape=_sds(ycat.shape, ycat.dtype),
        input_output_aliases={3: 0},
        semantics=("parallel",),
    )(u1, ln_g, ln_b, ycat)


def _conf_ln_bwd(dycat, u1, ln_g, ln_b):
    s = u1.shape[0]
    t = _row_tile(s, 4)
    nt = s // t

    def body(dy_ref, u_ref, g_ref, b_ref, du_ref, dg_ref, db_ref, a_g, a_b):
        i = pl.program_id(0)

        @pl.when(i == 0)
        def _():
            a_g[...] = jnp.zeros_like(a_g)
            a_b[...] = jnp.zeros_like(a_b)

        xhat, rstd = _ln_stats(u_ref[...])
        gv = g_ref[...]
        y = xhat * gv + b_ref[...]
        sg = _sigmoid(y)
        dyl = dy_ref[...] * (sg * (1.0 + y * (1.0 - sg)))
        a_g[...] += _colsum8(dyl * xhat)
        a_b[...] += _colsum8(dyl)
        dxh = dyl * gv
        du_ref[...] = rstd * (dxh - jnp.mean(dxh, axis=-1, keepdims=True)
                              - xhat * jnp.mean(dxh * xhat, axis=-1, keepdims=True))

        @pl.when(i == nt - 1)
        def _():
            dg_ref[...] = jnp.sum(a_g[...], axis=0, keepdims=True)
            db_ref[...] = jnp.sum(a_b[...], axis=0, keepdims=True)

    return _pallas(
        body, name="conf_ln_bwd", grid=(nt,),
        in_specs=[pl.BlockSpec((t, DC), lambda i: (i, 1)), pl.BlockSpec((t, DC), lambda i: (i, 0)),
                  _vec_spec(DC), _vec_spec(DC)],
        out_specs=[pl.BlockSpec((t, DC), lambda i: (i, 0)), _vec_spec(DC), _vec_spec(DC)],
        out_shape=[_sds((s, DC), F32), _sds((1, DC), F32), _sds((1, DC), F32)],
        scratch_shapes=[pltpu.VMEM((8, DC), F32)] * 2,
        semantics=("arbitrary",),
    )(dycat, u1, ln_g, ln_b)


def _conf_conv_bwd(ag, du1, conv_w, rows_out):
    s = ag.shape[0]
    nc = DC // LANES

    def body(a_ref, g_ref, d_ref, w_ref, da_ref, dg_ref, dw_ref, db_ref, upad, dpad, acc):
        _glu_into(upad, a_ref, g_ref, s)
        _fill_padded(dpad, d_ref, s, ROW_TILE, CONV_HALO)
        acc[...] = jnp.zeros_like(acc)

        def step(c, carry):
            r0 = pl.multiple_of(c * CONV_CHUNK, CONV_CHUNK)
            dcur = dpad[pl.ds(r0 + CONV_HALO, CONV_CHUNK), :]
            du0 = jnp.zeros((CONV_CHUNK, LANES), F32)
            for k in range(CW):
                du0 = du0 + w_ref[k:k + 1, :] * _tap(dpad, r0, CW - 1 - k)
                acc[8 * k:8 * k + 8, :] += _colsum8(dcur * _tap(upad, r0, k))
            acc[8 * CW:8 * CW + 8, :] += _colsum8(dcur)
            av = a_ref[pl.ds(r0, CONV_CHUNK), :]
            sg = _sigmoid(g_ref[pl.ds(r0, CONV_CHUNK), :])
            da_ref[pl.ds(r0, CONV_CHUNK), :] = (du0 * sg).astype(da_ref.dtype)
            dg_ref[pl.ds(r0, CONV_CHUNK), :] = (du0 * av * (sg * (1.0 - sg))).astype(dg_ref.dtype)
            return carry

        lax.fori_loop(0, s // CONV_CHUNK, step, 0)
        if rows_out > s:
            zeros = jnp.zeros((rows_out - s, LANES), da_ref.dtype)
            da_ref[s:rows_out, :] = zeros
            dg_ref[s:rows_out, :] = zeros
        for k in range(CW):
            dw_ref[k:k + 1, :] = jnp.sum(acc[8 * k:8 * k + 8, :], axis=0, keepdims=True)
        db_ref[...] = jnp.sum(acc[8 * CW:8 * CW + 8, :], axis=0, keepdims=True)

    col = lambda off: pl.BlockSpec((s, LANES), lambda c: (0, c + off))
    ocol = pl.BlockSpec((rows_out, LANES), lambda c: (0, c))
    return _pallas(
        body, name="conf_conv_bwd", grid=(nc,),
        in_specs=[col(0), col(nc), col(0), pl.BlockSpec((CW, LANES), lambda c: (0, c))],
        out_specs=[ocol, ocol, pl.BlockSpec((CW, LANES), lambda c: (0, c)), pl.BlockSpec((1, LANES), lambda c: (0, c))],
        out_shape=[_sds((rows_out, DC), MXU_DTYPE)] * 2 + [_sds((CW, DC), F32), _sds((1, DC), F32)],
        scratch_shapes=[pltpu.VMEM((s + 2 * CONV_HALO, LANES), F32)] * 2 + [pltpu.VMEM((8 * (CW + 1), LANES), F32)],
        semantics=("parallel",),
    )(ag, ag, du1, conv_w)


Q_TILE = 2 * GW
K_WIN = PAIR_ROWS * GW


def _bias_table(rpb_rev):
    def body(p_ref, t_ref):
        kcol = lax.broadcasted_iota(jnp.int32, (GW, LANES), 0)
        lane = lax.broadcasted_iota(jnp.int32, (GW, LANES), 1)
        qcol = lane % GW
        cs = jnp.clip(qcol - NA_ROWS, 0, GW - 2 * NA_ROWS)
        colvalid = (kcol >= cs) & (kcol < cs + 2 * NA_ROWS)
        neg = jnp.full((GW, LANES), NEG, F32)

        def skew(h, ro, shift):
            if ro < 0 or ro >= 2 * NA_ROWS - 1:
                return neg
            row = jnp.broadcast_to(p_ref[h * 16 + ro:h * 16 + ro + 1, :], (GW, LANES))
            return pltpu.roll(row, shift, 1, stride=1, stride_axis=0)

        for h in range(NH):
            for b in range(TAB_BLOCKS):
                val = jnp.where(lane < GW, skew(h, b - 1, GW + 1), skew(h, b - 2, 1))
                t_ref[h, b * GW:(b + 1) * GW, :] = jnp.where(colvalid, val, neg)

    return _pallas(body, name="attn_bias_table", out_shape=_sds((NH, TAB_BLOCKS * GW, LANES), F32))(rpb_rev)


def _rpb_grad(tt):
    def body(t_ref, o_ref):
        lane = lax.broadcasted_iota(jnp.int32, (GW, LANES), 1)
        si = lax.broadcasted_iota(jnp.int32, (GW, GW), 0)
        ti = lax.broadcasted_iota(jnp.int32, (GW, GW), 1)
        flip = jnp.where(si + ti == GW - 1, 1.0, 0.0).astype(F32)
        o_ref[...] = jnp.zeros_like(o_ref)
        for h in range(NH):
            for ro in range(2 * NA_ROWS - 1):
                lo = t_ref[h, (ro + 1) * GW:(ro + 2) * GW, :]
                hi = t_ref[h, (ro + 2) * GW:(ro + 3) * GW, :]
                g = jnp.where(lane < GW, lo + pltpu.roll(hi, GW, 1), 0.0)
                gf = jnp.dot(flip, g, preferred_element_type=F32, precision=lax.Precision.HIGHEST)
                sk = pltpu.roll(gf, 0, 1, stride=1, stride_axis=0)
                o_ref[h * 16 + ro:h * 16 + ro + 1, :] = jnp.sum(sk, axis=0, keepdims=True)

    return _pallas(body, name="attn_rpb_grad", out_shape=_sds((NH * 16, LANES), F32))(tt)


def _attn_geometry(i, rows):
    wsp = jnp.clip(2 * i - NA_ROWS // 2, 0, rows - PAIR_ROWS)
    k0 = pl.multiple_of(wsp * GW, GW)
    t0 = pl.multiple_of((wsp - 2 * i + NA_ROWS) * GW, GW)
    rr = lax.broadcasted_iota(jnp.int32, (GW, Q_TILE), 1) // GW
    wsr = jnp.clip(2 * i + rr - NA_ROWS // 2, 0, rows - NA_ROWS)
    edge_masks = tuple(jnp.where((kr >= wsr) & (kr < wsr + NA_ROWS), 0.0, NEG).astype(F32)
                       for kr in (wsp, wsp + PAIR_ROWS - 1))
    return k0, t0, edge_masks


def _biased(s_raw, bias, edge_masks):
    x = s_raw + bias
    return jnp.concatenate([x[:GW] + edge_masks[0], x[GW:K_WIN - GW], x[K_WIN - GW:] + edge_masks[1]], axis=0)


def _two_heads_on_lanes(xt):
    feat = lax.broadcasted_iota(jnp.int32, xt.shape, 0)
    zero = jnp.zeros_like(xt)
    return jnp.concatenate([jnp.where(feat < HD, xt, zero), jnp.where(feat >= HD, xt, zero)], axis=1)


def _two_heads_on_rows(x):
    lane = lax.broadcasted_iota(jnp.int32, x.shape, 1)
    zero = jnp.zeros_like(x)
    return jnp.concatenate([jnp.where(lane < HD, x, zero), jnp.where(lane >= HD, x, zero)], axis=0)


def _pick_heads(x2):
    n = x2.shape[0] // 2
    lane = lax.broadcasted_iota(jnp.int32, (n, LANES), 1)
    return jnp.where(lane < HD, x2[:n], x2[n:])


_TN = (((0,), (0,)), ((), ()))


def _attn_fwd(qkv, tab, s):
    rows = s // GW
    npair = rows // 2

    def body(q_ref, kv_ref, tab_ref, o_ref, lse_ref):
        i = pl.program_id(0)
        k0, t0, edge_masks = _attn_geometry(i, rows)
        for p in range(NH // 2):
            cq = slice(p * LANES, (p + 1) * LANES)
            ck = slice(DA + p * LANES, DA + (p + 1) * LANES)
            cv = slice(2 * DA + p * LANES, 2 * DA + (p + 1) * LANES)
            qm2 = _two_heads_on_lanes(q_ref[:, cq].T) * SCALE
            s_loc = jnp.dot(kv_ref[pl.ds(k0, K_WIN), ck], qm2, preferred_element_type=F32)
            s_ctx = jnp.dot(kv_ref[pl.ds(s, CTX), ck], qm2, preferred_element_type=F32)
            p_loc, p_ctx = [], []
            for hh in range(2):
                h = 2 * p + hh
                ch = slice(hh * Q_TILE, (hh + 1) * Q_TILE)
                sl = _biased(s_loc[:, ch], tab_ref[h, pl.ds(t0, K_WIN), :], edge_masks)
                sc = s_ctx[:, ch]
                m = jnp.maximum(jnp.max(sl, axis=0, keepdims=True), jnp.max(sc, axis=0, keepdims=True))
                el = jnp.exp(sl - m)
                ec = jnp.exp(sc - m)
                l = jnp.sum(el, axis=0, keepdims=True) + jnp.sum(ec, axis=0, keepdims=True)
                inv = 1.0 / l
                lse_ref[h:h + 1, :] = m + jnp.log(l)
                p_loc.append((el * inv).astype(MXU_DTYPE))
                p_ctx.append((ec * inv).astype(MXU_DTYPE))
            o2 = (lax.dot_general(jnp.concatenate(p_loc, axis=1), kv_ref[pl.ds(k0, K_WIN), cv], _TN, preferred_element_type=F32)
                  + lax.dot_general(jnp.concatenate(p_ctx, axis=1), kv_ref[pl.ds(s, CTX), cv], _TN, preferred_element_type=F32))
            o_ref[:, cq] = _pick_heads(o2).astype(o_ref.dtype)

    return _pallas(
        body, name="attn_fwd", grid=(npair,),
        in_specs=[pl.BlockSpec((Q_TILE, DA), lambda i: (i, 0)), pl.BlockSpec(memory_space=pltpu.VMEM),
                  pl.BlockSpec(memory_space=pltpu.VMEM)],
        out_specs=[pl.BlockSpec((Q_TILE, DA), lambda i: (i, 0)), pl.BlockSpec((NH, Q_TILE), lambda i: (0, i))],
        out_shape=[_sds((s, D), MXU_DTYPE), _sds((NH, s), F32)],
        semantics=("arbitrary",),
    )(qkv, qkv, tab)


def _attn_bwd(qkv, tab, lse, dycat, s):
    rows = s // GW
    npair = rows // 2
    sa = s + CTX
    nzero = CTX // Q_TILE

    def body(q_ref, do_ref, lse_ref, kv_ref, tab_ref, dq_ref, dkv_ref, tt_ref, dk_acc, dv_acc):
        i = pl.program_id(0)

        @pl.when(i == 0)
        def _():
            dk_acc[...] = jnp.zeros_like(dk_acc)
            dv_acc[...] = jnp.zeros_like(dv_acc)
            tt_ref[...] = jnp.zeros_like(tt_ref)

        @pl.when(i >= npair)
        def _():
            dq_ref[...] = jnp.zeros_like(dq_ref)

        @pl.when(i < npair)
        def _():
            k0, t0, edge_masks = _attn_geometry(i, rows)
            for p in range(NH // 2):
                cq = slice(p * LANES, (p + 1) * LANES)
                ck = slice(DA + p * LANES, DA + (p + 1) * LANES)
                cv = slice(2 * DA + p * LANES, 2 * DA + (p + 1) * LANES)
                qp = q_ref[:, cq] * SCALE
                dop = do_ref[:, cq].astype(MXU_DTYPE)
                qm2 = _two_heads_on_lanes(qp.T)
                dom2 = _two_heads_on_lanes(dop.T)
                kw = kv_ref[pl.ds(k0, K_WIN), ck]
                kc = kv_ref[pl.ds(s, CTX), ck]
                vw = kv_ref[pl.ds(k0, K_WIN), cv]
                vc = kv_ref[pl.ds(s, CTX), cv]
                s_loc = jnp.dot(kw, qm2, preferred_element_type=F32)
                s_ctx = jnp.dot(kc, qm2, preferred_element_type=F32)
                dp_loc = jnp.dot(vw, dom2, preferred_element_type=F32)
                dp_ctx = jnp.dot(vc, dom2, preferred_element_type=F32)
                p_loc, p_ctx, ds_loc, ds_ctx = [], [], [], []
                for hh in range(2):
                    h = 2 * p + hh
                    ch = slice(hh * Q_TILE, (hh + 1) * Q_TILE)
                    lse_h = lse_ref[h:h + 1, :]
                    pl_ = jnp.exp(_biased(s_loc[:, ch], tab_ref[h, pl.ds(t0, K_WIN), :], edge_masks) - lse_h)
                    pc_ = jnp.exp(s_ctx[:, ch] - lse_h)
                    dpl = dp_loc[:, ch]
                    dpc = dp_ctx[:, ch]
                    delta = jnp.sum(pl_ * dpl, axis=0, keepdims=True) + jnp.sum(pc_ * dpc, axis=0, keepdims=True)
                    dsl = pl_ * (dpl - delta)
                    dsc = pc_ * (dpc - delta)
                    tt_ref[h, pl.ds(t0, K_WIN), :] += dsl
                    p_loc.append(pl_.astype(MXU_DTYPE))
                    p_ctx.append(pc_.astype(MXU_DTYPE))
                    ds_loc.append(dsl.astype(MXU_DTYPE))
                    ds_ctx.append(dsc.astype(MXU_DTYPE))
                p_loc, p_ctx = jnp.concatenate(p_loc, axis=1), jnp.concatenate(p_ctx, axis=1)
                ds_loc, ds_ctx = jnp.concatenate(ds_loc, axis=1), jnp.concatenate(ds_ctx, axis=1)
                do_rows = _two_heads_on_rows(dop)
                q_rows = _two_heads_on_rows(qp)
                dv_acc[pl.ds(k0, K_WIN), cq] += jnp.dot(p_loc, do_rows, preferred_element_type=F32)
                dv_acc[pl.ds(s, CTX), cq] += jnp.dot(p_ctx, do_rows, preferred_element_type=F32)
                dk_acc[pl.ds(k0, K_WIN), cq] += jnp.dot(ds_loc, q_rows, preferred_element_type=F32)
                dk_acc[pl.ds(s, CTX), cq] += jnp.dot(ds_ctx, q_rows, preferred_element_type=F32)
                dq2 = (lax.dot_general(ds_loc, kw, _TN, preferred_element_type=F32)
                       + lax.dot_general(ds_ctx, kc, _TN, preferred_element_type=F32))
                dq_ref[:, cq] = (_pick_heads(dq2) * SCALE).astype(dq_ref.dtype)

        @pl.when(i == npair - 1)
        def _():
            def cp(c, carry):
                r0 = pl.multiple_of(c * ROW_TILE, ROW_TILE)
                dkv_ref[pl.ds(r0, ROW_TILE), 0:DA] = dk_acc[pl.ds(r0, ROW_TILE), :].astype(dkv_ref.dtype)
                dkv_ref[pl.ds(r0, ROW_TILE), DA:2 * DA] = dv_acc[pl.ds(r0, ROW_TILE), :].astype(dkv_ref.dtype)
                return carry

            lax.fori_loop(0, sa // ROW_TILE, cp, 0)

    qmap = lambda i: (jnp.minimum(i, npair - 1), 0)
    return _pallas(
        body, name="attn_bwd", grid=(npair + nzero,),
        in_specs=[pl.BlockSpec((Q_TILE, DA), qmap), pl.BlockSpec((Q_TILE, DA), qmap),
                  pl.BlockSpec((NH, Q_TILE), lambda i: (0, jnp.minimum(i, npair - 1))),
                  pl.BlockSpec(memory_space=pltpu.VMEM), pl.BlockSpec(memory_space=pltpu.VMEM)],
        out_specs=[pl.BlockSpec((Q_TILE, DA), lambda i: (i, 0)), pl.BlockSpec(memory_space=pltpu.VMEM),
                   pl.BlockSpec(memory_space=pltpu.VMEM)],
        out_shape=[_sds((sa, DA), MXU_DTYPE), _sds((sa, 2 * DA), MXU_DTYPE), _sds((NH, TAB_BLOCKS * GW, LANES), F32)],
        scratch_shapes=[pltpu.VMEM((sa, DA), F32)] * 2,
        semantics=("arbitrary",),
    )(qkv, dycat, lse, qkv, tab)


def _tile(n, prefs):
    for t in prefs:
        if n % t == 0:
            return t
    raise ValueError((n, prefs))


def _local_step(x, ctx, tgt, mod, mod_c, vec, w_in, late_weights, rpb_rev, early_grads=None):
    s = x.shape[0]
    sa = s + CTX
    ts = _tile(s, (1024, 512, 256))
    ts2 = _tile(s, (2048, 1024, 512, 256))
    tsa = _tile(sa, (1088, 640, 256))
    tsa2 = _tile(sa, (2176, 640, 256))
    sh1, sc1, gt1, sh2, sc2, gt2 = (mod[i:i + 1] for i in range(6))
    csh1, csc1 = mod_c[0:1], mod_c[1:2]
    act = MXU_DTYPE

    tab = _bias_table(rpb_rev)
    h_all = _rmsmod_fwd(x, ctx, vec["g_norm1"], sc1, sh1, csc1, csh1)
    w_in = w_in(h_all) if callable(w_in) else w_in
    qkv = _mm(h_all, w_in, mode="nn", m=sa, n=3 * DA, k=D, tm=tsa2, tn=512, tk=D, out_dtype=MXU_DTYPE, name="mm_qkv")
    ag = _mm(h_all, w_in, mode="nn", m=s, n=2 * DC, k=D, tm=ts2, tn=512, tk=D, out_dtype=F32, name="mm_ag", b_off=(0, 3))
    ycat, lse = _attn_fwd(qkv, tab, s)
    u1 = _conf_conv_fwd(ag, vec["conv_w"], vec["conv_b"])
    ycat = _conf_ln_fwd(u1, vec["ln_g"], vec["ln_b"], ycat)
    if callable(late_weights):
        w_out, ffn_weights = late_weights(ycat)
    else:
        w_out, ffn_weights = late_weights[0], late_weights[1:]
    y = _mm(ycat, w_out, mode="nn", m=s, n=D, k=D, tm=ts2, tn=512, tk=D, out_dtype=F32, name="mm_out")
    x1, h2 = _resid_rmsmod_fwd(x, y, gt1, vec["g_norm2"], sc2, sh2)
    w_up, w_down = ffn_weights(h2) if callable(ffn_weights) else ffn_weights
    u = _mm(h2, w_up, mode="nn", m=s, n=2 * DFF, k=D, tm=ts2, tn=512, tk=D, out_dtype=act, name="mm_up")
    f = _ffn_act_fwd(u, vec["ffn_conv_w"], vec["ffn_conv_b"])
    z = _mm(f, w_down, mode="nn", m=s, n=D, k=DFF, tm=ts, tn=D, tk=DFF, out_dtype=F32, name="mm_down")
    dx2, dz, loss, dgt2, dgf = _final_fwd_bwd(x1, z, gt2, vec["g_final"], tgt)

    df = _mm(dz, w_down, mode="nt", m=s, n=DFF, k=D, tm=ts, tn=DFF, tk=D, out_dtype=act, name="mm_down_dx")
    d_w_down = _mm(f, dz, mode="tn", m=DFF, n=D, k=s, tm=DFF // 2, tn=D, tk=ts2, out_dtype=F32, name="mm_down_dw")
    dug, duv, dfw_g, dfw_v, dfb_g, dfb_v = _ffn_act_bwd(u, df, vec["ffn_conv_w"], vec["ffn_conv_b"])
    dw_kw = dict(mode="tn", m=D, n=DFF, k=s, tm=D, tn=DFF, tk=ts, out_dtype=F32, out_total=(D, 2 * DFF))
    d_w_up = _mm(h2, dug, name="mm_up_dw_gate", **dw_kw)
    d_w_up = _mm(h2, duv, name="mm_up_dw_val", o_off=(0, 1), into=d_w_up, **dw_kw)
    if early_grads is not None:
        early_grads[0](d_w_up, d_w_down)
    dh2 = _mm([dug, duv], w_up, mode="nt", m=s, n=D, k=2 * DFF, tm=ts, tn=D, tk=2 * DFF, out_dtype=F32, name="mm_up_dx")
    sc2_b = sc2 if early_grads is None else sc2 + early_grads[1](dh2)
    dsh2, dsc2, dg2, dx1, dy, dgt1 = _rmsmod_bwd(x1, dh2, vec["g_norm2"], sc2_b, name="rmsmod2_bwd", add=dx2, resid=(gt1, y))
    dycat = _mm(dy, w_out, mode="nt", m=s, n=D, k=D, tm=ts2, tn=512, tk=D, out_dtype=F32, name="mm_out_dx")
    d_w_out = _mm(ycat, dy, mode="tn", m=D, n=D, k=s, tm=D, tn=D, tk=ts, out_dtype=F32, name="mm_out_dw")
    du1, dln_g, dln_b = _conf_ln_bwd(dycat, u1, vec["ln_g"], vec["ln_b"])
    da, dg, dconv_w, dconv_b = _conf_conv_bwd(ag, du1, vec["conv_w"], sa)
    dq, dkv, tt = _attn_bwd(qkv, tab, lse, dycat, s)
    drpb_rev = _rpb_grad(tt)
    d_pieces = [dq, dkv, da, dg]
    dh = _mm(d_pieces, w_in, mode="nt", m=sa, n=D, k=NIN, tm=tsa, tn=D, tk=NIN, out_dtype=F32, name="mm_in_dx")
    d_w_in = _mm(h_all, d_pieces, mode="tn", m=D, n=NIN, k=sa, tm=D, tn=NIN, tk=tsa, out_dtype=F32, name="mm_in_dw")
    dsh1, dsc1, dg1, grad_x = _rmsmod_bwd(x, dh, vec["g_norm1"], sc1, name="rmsmod1_bwd", add=dx1)
    dcsh1, dcsc1, dg1c = _rmsmod_bwd(ctx, dh, vec["g_norm1"], csc1, name="rmsmod1_ctx_bwd", dh_row0=s)

    small = dict(
        dmod=[dsh1, dsc1, dgt1, dsh2, dsc2, dgt2], dmod_c=[dcsh1, dcsc1],
        g_norm1=[dg1, dg1c], g_norm2=dg2, g_final=dgf, conv_b=dconv_b, ln_g=dln_g, ln_b=dln_b, conv_w=dconv_w,
        ffn_conv_w=[dfw_g, dfw_v], ffn_conv_b=[dfb_g, dfb_v], rpb_rev=drpb_rev,
    )
    return loss, grad_x, d_w_in, d_w_out, d_w_up, d_w_down, small


N_CHIPS = 4
HBM = pl.BlockSpec(memory_space=pl.ANY)
BIG = {"w_in": ("col", (D, NIN)), "w_out": ("row", (D, D)), "w_up": ("col", (D, 2 * DFF)), "w_down": ("row", (DFF, D))}
BIG_NAMES = tuple(BIG)
LATE_NAMES = ("w_out", "w_up", "w_down")


def _shard_shape(name):
    kind, (r, c) = BIG[name]
    return (r, c // N_CHIPS) if kind == "col" else (r // N_CHIPS, c)


def _half_rows(name):
    return _shard_shape(name)[0] // 2


def _place():
    x, y, c = lax.axis_index("x"), lax.axis_index("y"), lax.axis_index("c")
    others = [(1 - x, y), (x, 1 - y), (1 - x, 1 - y)]
    return x, y, c, 2 * x + y, (x, y, 1 - c), others


def _whole_region(ref, name, chip, half):
    kind, _ = BIG[name]
    r, c = _shard_shape(name)
    if kind == "col":
        return ref.at[pl.ds(half * (r // 2), r // 2), pl.ds(chip * c, c)]
    return ref.at[pl.ds(chip * r + half * (r // 2), r // 2), :]


def _remote(src, dst, send_sem, recv_sem, to):
    return pltpu.make_async_remote_copy(src_ref=src, dst_ref=dst, send_sem=send_sem, recv_sem=recv_sem,
                                        device_id=to, device_id_type=MESH)


def _cast_into_whole(name, shard, chip):
    kind, whole = BIG[name]
    r, c = shard.shape
    if kind == "col":
        tr = 256
        o_spec = pl.BlockSpec((tr, c), lambda i, ch: (i, ch[0]))
    else:
        tr = _tile(r, (128, 352))
        o_spec = pl.BlockSpec((tr, c), lambda i, ch: (ch[0] * (r // tr) + i, 0))

    def body(ch_ref, x_ref, o_ref):
        del ch_ref
        o_ref[...] = x_ref[...].astype(o_ref.dtype)

    return _pallas(body, name="cast_" + name, prefetch=1, grid=(r // tr,),
                   in_specs=[pl.BlockSpec((tr, c), lambda i, ch: (i, 0))], out_specs=o_spec,
                   out_shape=_sds(whole, MXU_DTYPE), semantics=("parallel",))(chip, shard)


SEM = pl.BlockSpec(memory_space=pltpu.SEMAPHORE)
IN_HBM = pl.BlockSpec(memory_space=pltpu.HBM)
DATAFLOW = pltpu.SideEffectType.DATAFLOW_SIDE_EFFECTING


def _keep_in_hbm(a):
    return pltpu.with_memory_space_constraint(a, pltpu.HBM)


def _several(after):
    return list(after) if isinstance(after, (list, tuple)) else [after]


FLIPS = [(dx, dy, dc) for dx in (0, 1) for dy in (0, 1) for dc in (0, 1)][1:]
OTHER_CHIPS = [f for f in FLIPS if f[2] == 0]


def _flipped(flip):
    x, y, c = lax.axis_index("x"), lax.axis_index("y"), lax.axis_index("c")
    return tuple(1 - v if f else v for v, f in zip((x, y, c), flip))


def _share_start(v, tag, after, flips=FLIPS):
    r, n = v.shape
    ns = 2 * len(flips)

    def body(*refs):
        v_ref, land_ref = refs[0], refs[1]
        sems = refs[2 + len(_several(after)):2 + len(_several(after)) + ns]
        x, y, c = lax.axis_index("x"), lax.axis_index("y"), lax.axis_index("c")
        mine = land_ref.at[pl.ds((4 * x + 2 * y + c) * r, r), :]
        for k, flip in enumerate(flips):
            _remote(v_ref, mine, sems[2 * k], sems[2 * k + 1], _flipped(flip)).start()

    res = pl.pallas_call(
        body, name="share_" + tag + "_start",
        out_shape=(*[pltpu.SemaphoreType.DMA(())] * ns, pltpu.HBM(v.shape, v.dtype), pltpu.HBM((8 * r, n), v.dtype)),
        in_specs=[IN_HBM] * 2 + [pl.BlockSpec(memory_space=pl.ANY)] * len(_several(after)),
        out_specs=(*[SEM] * ns, IN_HBM, IN_HBM),
        input_output_aliases={0: ns, 1: ns + 1},
        compiler_params=pltpu.CompilerParams(has_side_effects=DATAFLOW),
    )(_keep_in_hbm(v), _keep_in_hbm(jnp.tile(v, (8, 1))), *_several(after))
    return list(res[:ns]), res[ns], res[ns + 1], flips


def _share_wait(started, after, tag):
    sems, v, land, flips = started
    r = v.shape[0]
    ns = len(sems)

    def body(*refs):
        v_ref, land_ref = refs[0], refs[1]
        sem_refs = refs[2:2 + ns]
        for k, flip in enumerate(flips):
            px, py, pc = _flipped(flip)
            theirs = land_ref.at[pl.ds((4 * px + 2 * py + pc) * r, r), :]
            cp = _remote(v_ref, theirs, sem_refs[2 * k], sem_refs[2 * k + 1], (px, py, pc))
            cp.wait_send()
            cp.wait_recv()

    res = pl.pallas_call(
        body, name="share_" + tag + "_wait",
        out_shape=(pltpu.HBM(v.shape, v.dtype), pltpu.HBM(land.shape, land.dtype)),
        in_specs=[IN_HBM] * 2 + [SEM] * ns + [pl.BlockSpec(memory_space=pl.ANY)] * len(_several(after)),
        out_specs=(IN_HBM, IN_HBM),
        input_output_aliases={0: 0, 1: 1},
        compiler_params=pltpu.CompilerParams(has_side_effects=DATAFLOW),
    )(v, land, *sems, *_several(after))
    return res[1]


def _gather_start(wholes, names, after, tag):
    nw = len(names)
    ns = 2 * 3 * nw

    def body(*refs):
        ins = refs[:nw]
        sems = refs[nw + 1:nw + 1 + ns]
        token = refs[2 * nw + ns + 1]
        _, _, c, chip, _, others = _place()
        for w, name in enumerate(names):
            mine = _whole_region(ins[w], name, chip, c)
            for t, (ox, oy) in enumerate(others):
                k = 2 * (3 * w + t)
                _remote(mine, mine, sems[k], sems[k + 1], (ox, oy, c)).start()
        token[...] = jnp.zeros_like(token)

    res = pl.pallas_call(
        body, name="gather_" + tag + "_start",
        out_shape=(*[pltpu.SemaphoreType.DMA(())] * ns, *[pltpu.HBM(a.shape, a.dtype) for a in wholes], _sds((8, LANES), F32)),
        in_specs=[IN_HBM] * nw + [pl.BlockSpec(memory_space=pl.ANY)],
        out_specs=(*[SEM] * ns, *[IN_HBM] * nw, pl.BlockSpec(memory_space=pltpu.VMEM)),
        input_output_aliases={i: ns + i for i in range(nw)},
        compiler_params=pltpu.CompilerParams(has_side_effects=DATAFLOW),
    )(*[_keep_in_hbm(a) for a in wholes], after)
    return list(res[:ns]), list(res[ns:ns + nw]), res[ns + nw]


def _gather_wait(sems, wholes, names, after, tag):
    nw = len(names)
    ns = len(sems)

    def body(*refs):
        ins = refs[:nw]
        sem_refs = refs[nw:nw + ns]
        _, _, c, chip, _, others = _place()
        for w, name in enumerate(names):
            mine = _whole_region(ins[w], name, chip, c)
            for t, (ox, oy) in enumerate(others):
                got = _whole_region(ins[w], name, 2 * ox + oy, c)
                k = 2 * (3 * w + t)
                cp = _remote(mine, got, sem_refs[k], sem_refs[k + 1], (ox, oy, c))
                cp.wait_send()
                cp.wait_recv()

    return pl.pallas_call(
        body, name="gather_" + tag + "_wait",
        out_shape=tuple(pltpu.HBM(a.shape, a.dtype) for a in wholes),
        in_specs=[IN_HBM] * nw + [SEM] * ns + [pl.BlockSpec(memory_space=pl.ANY)], out_specs=tuple([IN_HBM] * nw),
        input_output_aliases={i: i for i in range(nw)},
        compiler_params=pltpu.CompilerParams(has_side_effects=DATAFLOW),
    )(*wholes, *sems, after)


def _forward_halves(wholes, names, tag):
    nw = len(names)

    def body(*refs):
        outs = refs[nw:2 * nw]
        send_sems, recv_sems = refs[2 * nw:]
        _, _, c, _, sibling, others = _place()
        sends = []
        for w, name in enumerate(names):
            for t, (ox, oy) in enumerate(others):
                got = _whole_region(outs[w], name, 2 * ox + oy, c)
                cp = _remote(got, got, send_sems.at[w, t], recv_sems.at[w, t], sibling)
                cp.start()
                sends.append(cp)
        for w, name in enumerate(names):
            for t, (ox, oy) in enumerate(others):
                got = _whole_region(outs[w], name, 2 * ox + oy, 1 - c)
                _remote(got, got, send_sems.at[w, t], recv_sems.at[w, t], sibling).wait_recv()
        for cp in sends:
            cp.wait_send()

    return pl.pallas_call(
        body, name="gather_" + tag + "_forward",
        out_shape=[_sds(a.shape, a.dtype) for a in wholes],
        in_specs=[HBM] * nw, out_specs=[HBM] * nw,
        input_output_aliases={i: i for i in range(nw)},
        scratch_shapes=[pltpu.SemaphoreType.DMA((nw, 3)), pltpu.SemaphoreType.DMA((nw, 3))],
    )(*wholes)


def _forward_start(wholes, names, tag, after):
    nw = len(names)
    ns = 2 * 3 * nw

    def body(*refs):
        ins = refs[:nw]
        sems = refs[nw + 1:nw + 1 + ns]
        token = refs[2 * nw + ns + 1]
        _, _, c, _, sibling, others = _place()
        for w, name in enumerate(names):
            for t, (ox, oy) in enumerate(others):
                got = _whole_region(ins[w], name, 2 * ox + oy, c)
                k = 2 * (3 * w + t)
                _remote(got, got, sems[k], sems[k + 1], sibling).start()
        token[...] = jnp.zeros_like(token)

    res = pl.pallas_call(
        body, name="gather_" + tag + "_forward_start",
        out_shape=(*[pltpu.SemaphoreType.DMA(())] * ns, *[pltpu.HBM(a.shape, a.dtype) for a in wholes], _sds((8, LANES), F32)),
        in_specs=[IN_HBM] * nw + [pl.BlockSpec(memory_space=pl.ANY)],
        out_specs=(*[SEM] * ns, *[IN_HBM] * nw, pl.BlockSpec(memory_space=pltpu.VMEM)),
        input_output_aliases={i: ns + i for i in range(nw)},
        compiler_params=pltpu.CompilerParams(has_side_effects=DATAFLOW),
    )(*[_keep_in_hbm(a) for a in wholes], after)
    return list(res[:ns]), list(res[ns:ns + nw]), res[ns + nw]


def _forward_wait(sems, wholes, names, after, tag):
    nw = len(names)
    ns = len(sems)

    def body(*refs):
        ins = refs[:nw]
        sem_refs = refs[nw:nw + ns]
        _, _, c, _, sibling, others = _place()
        for w, name in enumerate(names):
            for t, (ox, oy) in enumerate(others):
                k = 2 * (3 * w + t)
                cp = _remote(_whole_region(ins[w], name, 2 * ox + oy, c), _whole_region(ins[w], name, 2 * ox + oy, 1 - c),
                             sem_refs[k], sem_refs[k + 1], sibling)
                cp.wait_send()
                cp.wait_recv()

    return pl.pallas_call(
        body, name="gather_" + tag + "_forward_wait",
        out_shape=tuple(pltpu.HBM(a.shape, a.dtype) for a in wholes),
        in_specs=[IN_HBM] * nw + [SEM] * ns + [pl.BlockSpec(memory_space=pl.ANY)], out_specs=tuple([IN_HBM] * nw),
        input_output_aliases={i: i for i in range(nw)},
        compiler_params=pltpu.CompilerParams(has_side_effects=DATAFLOW),
    )(*wholes, *sems, after)


def _compact_shape(name, dtype):
    kind, (r, c) = BIG[name]
    return _sds((r // 2, c), dtype)


def _swap_pairs(ins, outs, names, c):
    pairs = []
    for w, name in enumerate(names):
        kind, _ = BIG[name]
        half = _half_rows(name)
        if kind == "col":
            pairs.append((ins[w].at[pl.ds((1 - c) * half, half), :], outs[w]))
        else:
            pairs += [(ins[w].at[pl.ds(jj * 2 * half + (1 - c) * half, half), :], outs[w].at[pl.ds(jj * half, half), :])
                      for jj in range(N_CHIPS)]
    return pairs


def _n_swap_copies(names):
    return sum(1 if BIG[n][0] == "col" else N_CHIPS for n in names)


def _swap_start(grads, names, label):
    nw = len(names)
    ns = 2 * _n_swap_copies(names)

    def body(*refs):
        ins, lands = refs[:nw], refs[nw:2 * nw]
        sems = refs[2 * nw:2 * nw + ns]
        token = refs[4 * nw + ns]
        _, _, c, _, sibling, _ = _place()
        for k, (src, dst) in enumerate(_swap_pairs(ins, lands, names, c)):
            _remote(src, dst, sems[2 * k], sems[2 * k + 1], sibling).start()
        token[...] = jnp.zeros_like(token)

    lands = [_keep_in_hbm(lax.empty(_compact_shape(n, F32).shape, F32)) for n in names]
    res = pl.pallas_call(
        body, name=label,
        out_shape=(*[pltpu.SemaphoreType.DMA(())] * ns, *[pltpu.HBM(a.shape, a.dtype) for a in grads],
                   *[pltpu.HBM(a.shape, a.dtype) for a in lands], _sds((8, LANES), F32)),
        in_specs=[IN_HBM] * (2 * nw),
        out_specs=(*[SEM] * ns, *[IN_HBM] * (2 * nw), pl.BlockSpec(memory_space=pltpu.VMEM)),
        input_output_aliases={i: ns + i for i in range(2 * nw)},
        compiler_params=pltpu.CompilerParams(has_side_effects=DATAFLOW),
    )(*[_keep_in_hbm(a) for a in grads], *lands)
    return list(res[:ns]), list(res[ns:ns + nw]), list(res[ns + nw:ns + 2 * nw]), res[ns + 2 * nw]


def _swap_wait(sems, grads, lands, names, after, label):
    nw = len(names)
    ns = len(sems)

    def body(*refs):
        ins, land_refs = refs[:nw], refs[nw:2 * nw]
        sem_refs = refs[2 * nw:2 * nw + ns]
        _, _, c, _, sibling, _ = _place()
        for k, (src, dst) in enumerate(_swap_pairs(ins, land_refs, names, c)):
            cp = _remote(src, dst, sem_refs[2 * k], sem_refs[2 * k + 1], sibling)
            cp.wait_send()
            cp.wait_recv()

    res = pl.pallas_call(
        body, name=label,
        out_shape=tuple(pltpu.HBM(a.shape, a.dtype) for a in (*grads, *lands)),
        in_specs=[IN_HBM] * (2 * nw) + [SEM] * ns + [pl.BlockSpec(memory_space=pl.ANY)] * len(_several(after)),
        out_specs=tuple([IN_HBM] * (2 * nw)),
        input_output_aliases={i: i for i in range(2 * nw)},
        compiler_params=pltpu.CompilerParams(has_side_effects=DATAFLOW),
    )(*grads, *lands, *sems, *_several(after))
    return list(res[:nw]), list(res[nw:])


def _add_halves(name, grad, got, core):
    kind, (r, c) = BIG[name]
    half = _half_rows(name)
    if kind == "col":
        t = 128
        grid = (half // t,)
        g_spec = pl.BlockSpec((t, c), lambda i, cr: (cr[0] * (half // t) + i, 0))
        o_spec = pl.BlockSpec((t, c), lambda i, cr: (i, 0))
    else:
        t = half
        grid = (N_CHIPS,)
        g_spec = pl.BlockSpec((t, c), lambda i, cr: (2 * i + cr[0], 0))
        o_spec = pl.BlockSpec((t, c), lambda i, cr: (i, 0))

    def body(c_ref, g_ref, b_ref, o_ref):
        del c_ref
        o_ref[...] = (g_ref[...] + b_ref[...]).astype(o_ref.dtype)

    return pl.pallas_call(
        body, name="grad_add_" + name,
        grid_spec=pltpu.PrefetchScalarGridSpec(num_scalar_prefetch=1, grid=grid, in_specs=[g_spec, o_spec], out_specs=o_spec),
        out_shape=_compact_shape(name, BF16),
        compiler_params=pltpu.CompilerParams(dimension_semantics=("parallel",), vmem_limit_bytes=VMEM_LIMIT),
    )(core, grad, got)


def _piece(ref, name, chip):
    kind, _ = BIG[name]
    r, c = _shard_shape(name)
    if kind == "col":
        return ref.at[:, pl.ds(chip * c, c)]
    return ref.at[pl.ds(chip * (r // 2), r // 2), :]


def _landing_shape(name):
    r, c = _shard_shape(name)
    return (N_CHIPS - 1, r // 2, c)


def _exchange_start(parts, names, label):
    nw = len(names)
    ns = 2 * 3 * nw

    def body(*refs):
        ins, lands = refs[:nw], refs[nw:2 * nw]
        sems = refs[2 * nw:2 * nw + ns]
        token = refs[4 * nw + ns]
        _, _, c, _, _, others = _place()
        for w, name in enumerate(names):
            for t, (ox, oy) in enumerate(others):
                k = 2 * (3 * w + t)
                _remote(_piece(ins[w], name, 2 * ox + oy), lands[w].at[t], sems[k], sems[k + 1], (ox, oy, c)).start()
        token[...] = jnp.zeros_like(token)

    lands = [_keep_in_hbm(lax.empty(_landing_shape(n), BF16)) for n in names]
    res = pl.pallas_call(
        body, name=label,
        out_shape=(*[pltpu.SemaphoreType.DMA(())] * ns, *[pltpu.HBM(a.shape, a.dtype) for a in parts],
                   *[pltpu.HBM(a.shape, a.dtype) for a in lands], _sds((8, LANES), F32)),
        in_specs=[IN_HBM] * (2 * nw),
        out_specs=(*[SEM] * ns, *[IN_HBM] * (2 * nw), pl.BlockSpec(memory_space=pltpu.VMEM)),
        input_output_aliases={i: ns + i for i in range(2 * nw)},
        compiler_params=pltpu.CompilerParams(has_side_effects=DATAFLOW),
    )(*[_keep_in_hbm(a) for a in parts], *lands)
    return list(res[:ns]), list(res[ns:ns + nw]), list(res[ns + nw:ns + 2 * nw]), res[ns + 2 * nw]


def _exchange_wait(sems, parts, lands, names, after, label):
    nw = len(names)
    ns = len(sems)

    def body(*refs):
        ins, land_refs = refs[:nw], refs[nw:2 * nw]
        sem_refs = refs[2 * nw:2 * nw + ns]
        _, _, c, _, _, others = _place()
        for w, name in enumerate(names):
            for t, (ox, oy) in enumerate(others):
                k = 2 * (3 * w + t)
                cp = _remote(_piece(ins[w], name, 2 * ox + oy), land_refs[w].at[t], sem_refs[k], sem_refs[k + 1], (ox, oy, c))
                cp.wait_send()
                cp.wait_recv()

    res = pl.pallas_call(
        body, name=label,
        out_shape=tuple(pltpu.HBM(a.shape, a.dtype) for a in (*parts, *lands)),
        in_specs=[IN_HBM] * (2 * nw) + [SEM] * ns + [pl.BlockSpec(memory_space=pl.ANY)] * len(_several(after)),
        out_specs=tuple([IN_HBM] * (2 * nw)),
        input_output_aliases={i: i for i in range(2 * nw)},
        compiler_params=pltpu.CompilerParams(has_side_effects=DATAFLOW),
    )(*parts, *lands, *sems, *_several(after))
    return list(res[:nw]), list(res[nw:])


def _sum_chips(name, part, got, chip):
    kind, _ = BIG[name]
    _, r, c = got.shape
    t = _tile(r, (128, 352))
    if kind == "col":
        own = pl.BlockSpec((t, c), lambda i, ch: (i, ch[0]))
    else:
        own = pl.BlockSpec((t, c), lambda i, ch: (ch[0] * (r // t) + i, 0))

    def body(ch_ref, p_ref, g_ref, o_ref):
        del ch_ref
        acc = p_ref[...].astype(F32)
        for j in range(N_CHIPS - 1):
            acc = acc + g_ref[j].astype(F32)
        o_ref[...] = acc

    return _pallas(
        body, name="grad_sum_" + name, prefetch=1, grid=(r // t,),
        in_specs=[own, pl.BlockSpec((N_CHIPS - 1, t, c), lambda i, ch: (0, i, 0))],
        out_specs=pl.BlockSpec((t, c), lambda i, ch: (i, 0)),
        out_shape=_sds((r, c), F32), semantics=("parallel",),
    )(chip, part, got)


def _send_halves(sums, label, after):
    nw = len(sums)

    def body(*refs):
        ins, outs = refs[:nw], refs[nw + 1:2 * nw + 1]
        send_sems, recv_sems = refs[2 * nw + 1:]
        _, _, _, _, sibling, _ = _place()
        copies = [_remote(ins[w], outs[w], send_sems.at[w], recv_sems.at[w], sibling) for w in range(nw)]
        for cp in copies:
            cp.start()
        for cp in copies:
            cp.wait()

    return pl.pallas_call(
        body, name=label,
        out_shape=[_sds(a.shape, a.dtype) for a in sums],
        in_specs=[HBM] * (nw + 1), out_specs=[HBM] * nw,
        scratch_shapes=[pltpu.SemaphoreType.DMA((nw,)), pltpu.SemaphoreType.DMA((nw,))],
    )(*sums, after)


EARLY_GRADS = ("w_up", "w_down")
LAST_GRADS = ("w_in", "w_out")


def _reduce_finish(started, names, after, chip, tag):
    sems, parts, lands, _ = started
    parts, lands = _exchange_wait(sems, parts, lands, names, after, "grad_exchange_wait_" + tag)
    return [_sum_chips(n, parts[i], lands[i], chip) for i, n in enumerate(names)]


HI = lax.Precision.HIGHEST
MOD_COLS = 6 * D // N_CHIPS
COND_ROWS = 16


def _silu(v):
    return v * _sigmoid(v)


GATHER_ROWS = 8
FFW_COLS = 2 * DFF // N_CHIPS
CONV_COLS = DC // N_CHIPS
TAPS_PER_ROW = FFW_COLS // CONV_COLS
assert 4 + -(-CW // TAPS_PER_ROW) <= GATHER_ROWS


def _conv_tap_place(k):
    return 4 + k // TAPS_PER_ROW, (k % TAPS_PER_ROW) * CONV_COLS


def _pack_cond(c, ffn_w, conv_w):
    def body(c_ref, f_ref, w_ref, o_ref):
        o_ref[...] = jnp.zeros_like(o_ref)
        o_ref[0:1, 0:D] = c_ref[...]
        o_ref[1:4, :] = f_ref[...]
        for k in range(CW):
            row, lane = _conv_tap_place(k)
            o_ref[row:row + 1, lane:lane + CONV_COLS] = w_ref[k:k + 1, :]

    return _pallas(body, name="pack_cond", out_shape=_sds((GATHER_ROWS, FFW_COLS), F32))(c, ffn_w, conv_w)


def _unpack_cond(got, c_ctx):
    def body(g_ref, c_ref, cond_ref, f_ref, w_ref):
        cond_ref[...] = jnp.zeros_like(cond_ref)
        for d in range(8):
            cond_ref[d:d + 1, :] = g_ref[d * GATHER_ROWS:d * GATHER_ROWS + 1, 0:D]
        cond_ref[8:9, :] = c_ref[...]
        for j in range(N_CHIPS):
            r0 = 2 * j * GATHER_ROWS
            f_ref[:, j * FFW_COLS:(j + 1) * FFW_COLS] = g_ref[r0 + 1:r0 + 4, :]
            for k in range(CW):
                row, lane = _conv_tap_place(k)
                w_ref[k:k + 1, j * CONV_COLS:(j + 1) * CONV_COLS] = g_ref[r0 + row:r0 + row + 1, lane:lane + CONV_COLS]

    return _pallas(body, name="unpack_cond",
                   out_shape=[_sds((COND_ROWS, D), F32), _sds((3, 2 * DFF), F32), _sds((CW, DC), F32)])(got, c_ctx)


def _chip_cols(rows, width):
    return pl.BlockSpec((rows, width), lambda i, ch: (0, ch[0]))


def _whole(shape):
    return pl.BlockSpec(shape, lambda i, ch: (0,) * len(shape))


def _mod_shard(cond, w_mod, b_mod, chip):
    def body(ch_ref, c_ref, w_ref, b_ref, o_ref):
        del ch_ref
        o_ref[...] = jnp.dot(_silu(c_ref[...]), w_ref[...], preferred_element_type=F32, precision=HI) + b_ref[...]

    return _pallas(body, name="mod_fwd", prefetch=1, grid=(1,),
                   in_specs=[_whole((COND_ROWS, D)), _whole((D, MOD_COLS)), _chip_cols(1, MOD_COLS)],
                   out_specs=_whole((COND_ROWS, MOD_COLS)),
                   out_shape=_sds((COND_ROWS, MOD_COLS), F32))(chip, cond, w_mod, b_mod)


def _unpack_mod(mods, dev):
    def body(dev_ref, m_ref, me_ref, c_ref):
        rowi = lax.broadcasted_iota(jnp.int32, (COND_ROWS, MOD_COLS), 0)
        core = dev_ref[0] % 2
        mine, ctx = [], []
        for j in range(N_CHIPS):
            blk = m_ref[pl.ds(pl.multiple_of((2 * j + core) * COND_ROWS, COND_ROWS), COND_ROWS), :]
            mine.append(jnp.sum(jnp.where(rowi == dev_ref[0], blk, 0.0), axis=0, keepdims=True))
            ctx.append(blk[8:9, :])
        mine = jnp.concatenate(mine, axis=1)
        ctx = jnp.concatenate(ctx, axis=1)
        for k in range(6):
            me_ref[k:k + 1, :] = mine[:, k * D:(k + 1) * D]
        for k in range(2):
            c_ref[k:k + 1, :] = ctx[:, k * D:(k + 1) * D]

    return _pallas(body, name="unpack_mod", prefetch=1, grid=(1,),
                   in_specs=[_whole(mods.shape)], out_specs=[_whole((6, D)), _whole((2, D))],
                   out_shape=[_sds((6, D), F32), _sds((2, D), F32)])(dev, mods)


MOD_TILE = 512


def _mod_weight_update(cond, dmod_all, w, m, v, chip):
    nt = MOD_COLS // MOD_TILE

    def body(ch_ref, c_ref, d_ref, w_ref, m_ref, v_ref, g_ref, dl_ref, nm_ref, nv_ref):
        del ch_ref
        g = lax.dot_general(_silu(c_ref[...]), d_ref[...], _TN, preferred_element_type=F32, precision=HI)
        g_ref[...] = g
        dl_ref[...], nm_ref[...], nv_ref[...] = _adam_math(w_ref[...], g, m_ref[...], v_ref[...])

    blk = pl.BlockSpec((D, MOD_TILE), lambda j, ch: (0, j))
    return _pallas(body, name="mod_weight_update", prefetch=1, grid=(nt,),
                   in_specs=[_whole((COND_ROWS, D)), pl.BlockSpec((COND_ROWS, MOD_TILE), lambda j, ch: (0, ch[0] * nt + j)),
                             blk, blk, blk],
                   out_specs=[blk] * 4, out_shape=[_sds((D, MOD_COLS), F32)] * 4,
                   semantics=("parallel",))(chip, cond, dmod_all, w, m, v)


def _cond_grad_partial(dmod_all, w_mod, chip):
    def body(ch_ref, d_ref, w_ref, o_ref):
        del ch_ref
        o_ref[...] = lax.dot_general(d_ref[...], w_ref[...], (((1,), (1,)), ((), ())), preferred_element_type=F32, precision=HI)

    return _pallas(body, name="cond_grad_partial", prefetch=1, grid=(1,),
                   in_specs=[pl.BlockSpec((8, MOD_COLS), lambda i, ch: (1, ch[0])), _whole((D, MOD_COLS))],
                   out_specs=_whole((8, D)), out_shape=_sds((8, D), F32))(chip, dmod_all, w_mod)


def _adam_math(w, g, m, v):
    nm = ADAM_B1 * m + (1.0 - ADAM_B1) * g
    nv = ADAM_B2 * v + (1.0 - ADAM_B2) * (g * g)
    c1 = 1.0 - ADAM_B1 ** ADAM_STEP
    c2 = 1.0 - ADAM_B2 ** ADAM_STEP
    return -ADAM_LR * ((nm / c1) / (jnp.sqrt(nv / c2) + ADAM_EPS) + ADAM_WD * w), nm, nv


def _cond_update(parts, c_ctx, m, v):
    def body(p_ref, c_ref, m_ref, v_ref, g_ref, d_ref, nm_ref, nv_ref):
        tot = p_ref[0:1, :]
        for j in range(1, N_CHIPS):
            tot = tot + p_ref[16 * j:16 * j + 1, :]
        cv = c_ref[...]
        sg = _sigmoid(cv)
        g = tot * (sg * (1.0 + cv * (1.0 - sg)))
        g_ref[...] = g
        d_ref[...], nm_ref[...], nv_ref[...] = _adam_math(cv, g, m_ref[...], v_ref[...])

    return _pallas(body, name="cond_update", out_shape=[_sds((1, D), F32)] * 4)(parts, c_ctx, m, v)


def _adamw_cols(w, g_all, m, v, chip, name):
    r, c = w.shape

    def body(ch_ref, w_ref, g_ref, m_ref, v_ref, go_ref, d_ref, nm_ref, nv_ref):
        del ch_ref
        g = g_ref[...]
        go_ref[...] = g
        d_ref[...], nm_ref[...], nv_ref[...] = _adam_math(w_ref[...], g, m_ref[...], v_ref[...])

    return _pallas(body, name=name, prefetch=1, grid=(1,),
                   in_specs=[_whole((r, c)), _chip_cols(r, c), _whole((r, c)), _whole((r, c))],
                   out_specs=[_whole((r, c))] * 4, out_shape=[_sds((r, c), F32)] * 4)(chip, w, g_all, m, v)


def _adamw_halves(name, w, own, other, m, v, core, after):
    r, c = w.shape
    half = r // 2
    t = _tile(half, (128, 352))
    nh = half // t

    def pick(mine):
        def index(i, cr):
            first = cr[0] if mine else 1 - cr[0]
            return (jnp.clip(i - first * nh, 0, nh - 1), 0)
        return pl.BlockSpec((t, c), index)

    def body(c_ref, w_ref, own_ref, oth_ref, m_ref, v_ref, after_ref, g_ref, d_ref, nm_ref, nv_ref):
        del after_ref
        g = jnp.where(pl.program_id(0) // nh == c_ref[0], own_ref[...], oth_ref[...])
        g_ref[...] = g
        d_ref[...], nm_ref[...], nv_ref[...] = _adam_math(w_ref[...], g, m_ref[...], v_ref[...])

    blk = pl.BlockSpec((t, c), lambda i, cr: (i, 0))
    return _pallas(body, name="adamw_" + name, prefetch=1, grid=(2 * nh,),
                   in_specs=[blk, pick(True), pick(False), blk, blk, pl.BlockSpec(memory_space=pl.ANY)], out_specs=[blk] * 4,
                   out_shape=[_sds((r, c), F32)] * 4, semantics=("parallel",))(core, w, own, other, m, v, after)


WEIGHTS = ("c_ctx", "w_mod", "b_mod", "g_norm1", "w_in", "rpb", "conv_w", "conv_b", "ln_g", "ln_b", "w_out", "g_norm2",
           "w_up", "ffn_conv_w", "ffn_conv_b", "w_down", "g_final")
PACK = (("dmod", 6 * D), ("dmod_c", 2 * D), ("g_norm1", D), ("g_norm1_ctx", D), ("g_norm2", D), ("g_final", D),
        ("conv_b", DC), ("ln_g", DC), ("ln_b", DC), ("ffn_conv_b", 2 * DFF), ("ffn_conv_w", 3 * 2 * DFF),
        ("conv_w", CW * DC), ("rpb_rev", NH * 16 * LANES), ("loss", LANES))
PACK_OFF = {}
_o = 0
for _n, _w in PACK:
    PACK_OFF[_n] = (_o, _w)
    _o += _w
PACK_N = -(-_o // (8 * LANES)) * (8 * LANES)
VECTORS = {"b_mod": (6 * D, ("dmod", "dmod_c")), "g_norm1": (D, ("g_norm1", "g_norm1_ctx")), "conv_b": (DC, ("conv_b",)),
           "ln_g": (DC, ("ln_g",)), "ln_b": (DC, ("ln_b",)), "g_norm2": (D, ("g_norm2",)),
           "ffn_conv_b": (2 * DFF, ("ffn_conv_b",)), "g_final": (D, ("g_final",))}
RPB_COLS = 4 * NA_ROWS - 1


def _pack_small(parts, after):
    arrs, places = [], []
    for name, _ in PACK:
        off, width = PACK_OFF[name]
        group = parts[name]
        rows = group[0].shape[0]
        row_w = sum(a.shape[1] for a in group)
        assert rows * row_w == width, (name, rows, row_w, width)
        col = 0
        for a in group:
            arrs.append(a)
            places.append([off + k * row_w + col for k in range(rows)])
            col += a.shape[1]

    def body(*refs):
        o_ref = refs[-1]
        o_ref[:, _o:PACK_N] = jnp.zeros((1, PACK_N - _o), F32)
        for ref, offs in zip(refs, places):
            n = ref.shape[1]
            for k, off in enumerate(offs):
                o_ref[:, off:off + n] = ref[k:k + 1, :]

    vmem = pl.BlockSpec(memory_space=pltpu.VMEM)
    return _pallas(body, name="pack_small_grads", out_shape=_sds((1, PACK_N), F32),
                   in_specs=[vmem] * len(arrs) + [pl.BlockSpec(memory_space=pl.ANY)] * len(_several(after)),
                   out_specs=vmem)(*arrs, *_several(after))


def _small_update(packs, w, m, v):
    names = list(VECTORS)

    def body(*refs):
        it = iter(refs)
        p_ref = next(it)
        wmv = {n: (next(it), next(it), next(it)) for n in names}
        outs = {n: (next(it), next(it), next(it), next(it)) for n in names}
        dmod_ref, cw_ref, fw_ref, rpb_ref, loss_ref = next(it), next(it), next(it), next(it), next(it)

        def total(name):
            off, width = PACK_OFF[name]
            acc = p_ref[0:1, off:off + width]
            for d in range(1, 8):
                acc = acc + p_ref[d:d + 1, off:off + width]
            return acc

        for n in names:
            width, segs = VECTORS[n]
            g = total(segs[0])
            if len(segs) > 1:
                extra = total(segs[1])
                ew = extra.shape[1]
                g = g + extra if ew == width else jnp.concatenate([g[:, :ew] + extra, g[:, ew:]], axis=1)
            w_ref, m_ref, v_ref = wmv[n]
            g_ref, d_ref, nm_ref, nv_ref = outs[n]
            g_ref[...] = g
            d_ref[...], nm_ref[...], nv_ref[...] = _adam_math(w_ref[...], g, m_ref[...], v_ref[...])

        o_dmod = PACK_OFF["dmod"][0]
        dmod_ref[...] = jnp.zeros_like(dmod_ref)
        dmod_ref[0:8, :] = p_ref[:, o_dmod:o_dmod + 6 * D]
        dmod_ref[8:9, 0:2 * D] = total("dmod_c")
        for ref, name, rows in ((cw_ref, "conv_w", CW), (fw_ref, "ffn_conv_w", 3), (rpb_ref, "rpb_rev", NH * 16)):
            flat = total(name)
            n = ref.shape[1]
            for k in range(rows):
                ref[k:k + 1, :] = flat[:, k * n:(k + 1) * n]
        loss_ref[...] = total("loss")

    ins = [packs] + [a[n] for n in names for a in (w, m, v)]
    out_shape = [_sds((1, VECTORS[n][0]), F32) for n in names for _ in range(4)]
    out_shape += [_sds((COND_ROWS, 6 * D), F32), _sds((CW, DC), F32), _sds((3, 2 * DFF), F32), _sds((NH * 16, LANES), F32),
                  _sds((1, LANES), F32)]
    res = _pallas(body, name="small_update", out_shape=out_shape)(*ins)
    per = {n: tuple(res[4 * i:4 * i + 4]) for i, n in enumerate(names)}
    return (per, *res[4 * len(names):])


def _rpb_update(rev, w, m, v):
    def body(r_ref, w_ref, m_ref, v_ref, g_ref, d_ref, nm_ref, nv_ref):
        li = lax.broadcasted_iota(jnp.int32, (LANES, LANES), 0)
        co = lax.broadcasted_iota(jnp.int32, (LANES, LANES), 1)
        lane_of_co0 = GW - 1 + RPB_COLS // 2
        unflip = jnp.where((li == lane_of_co0 - co) & (co < RPB_COLS), 1.0, 0.0).astype(F32)
        g_all = jnp.dot(r_ref[...], unflip, preferred_element_type=F32, precision=HI)
        nr = 2 * NA_ROWS - 1
        for h in range(NH):
            g = g_all[h * 16:h * 16 + nr, 0:RPB_COLS]
            g_ref[0, h] = g
            d_ref[0, h], nm_ref[0, h], nv_ref[0, h] = _adam_math(w_ref[0, h], g, m_ref[0, h], v_ref[0, h])

    return _pallas(body, name="rpb_update", out_shape=[_sds(w.shape, F32)] * 4)(rev, w, m, v)


def kernel(x, c, ctx, c_ctx, w_mod, b_mod, g_norm1, w_in, rpb, conv_w, conv_b, ln_g, ln_b, w_out, g_norm2, w_up, ffn_conv_w, ffn_conv_b, w_down, g_final, loss_target, m_c_ctx, m_w_mod, m_b_mod, m_g_norm1, m_w_in, m_rpb, m_conv_w, m_conv_b, m_ln_g, m_ln_b, m_w_out, m_g_norm2, m_w_up, m_ffn_conv_w, m_ffn_conv_b, m_w_down, m_g_final, v_c_ctx, v_w_mod, v_b_mod, v_g_norm1, v_w_in, v_rpb, v_conv_w, v_conv_b, v_ln_g, v_ln_b, v_w_out, v_g_norm2, v_w_up, v_ffn_conv_w, v_ffn_conv_b, v_w_down, v_g_final):
    w = dict(c_ctx=c_ctx, w_mod=w_mod, b_mod=b_mod, g_norm1=g_norm1, w_in=w_in, rpb=rpb, conv_w=conv_w, conv_b=conv_b,
             ln_g=ln_g, ln_b=ln_b, w_out=w_out, g_norm2=g_norm2, w_up=w_up, ffn_conv_w=ffn_conv_w, ffn_conv_b=ffn_conv_b,
             w_down=w_down, g_final=g_final)
    mom = dict(c_ctx=m_c_ctx, w_mod=m_w_mod, b_mod=m_b_mod, g_norm1=m_g_norm1, w_in=m_w_in, rpb=m_rpb, conv_w=m_conv_w,
               conv_b=m_conv_b, ln_g=m_ln_g, ln_b=m_ln_b, w_out=m_w_out, g_norm2=m_g_norm2, w_up=m_w_up,
               ffn_conv_w=m_ffn_conv_w, ffn_conv_b=m_ffn_conv_b, w_down=m_w_down, g_final=m_g_final)
    var = dict(c_ctx=v_c_ctx, w_mod=v_w_mod, b_mod=v_b_mod, g_norm1=v_g_norm1, w_in=v_w_in, rpb=v_rpb, conv_w=v_conv_w,
               conv_b=v_conv_b, ln_g=v_ln_g, ln_b=v_ln_b, w_out=v_w_out, g_norm2=v_g_norm2, w_up=v_w_up,
               ffn_conv_w=v_ffn_conv_w, ffn_conv_b=v_ffn_conv_b, w_down=v_w_down, g_final=v_g_final)
    xi, yi, ci = lax.axis_index("x"), lax.axis_index("y"), lax.axis_index("c")
    dev = (4 * xi + 2 * yi + ci).astype(jnp.int32).reshape(1)
    chip = (2 * xi + yi).astype(jnp.int32).reshape(1)
    core = ci.astype(jnp.int32).reshape(1)
    c_ctx2 = c_ctx.reshape(1, D)
    g_final2 = g_final.reshape(1, D)
    mom["g_final"], var["g_final"] = m_g_final.reshape(1, D), v_g_final.reshape(1, D)

    sharing_cond = _share_start(_pack_cond(c, ffn_conv_w[0], conv_w[0]), "cond", after=[])
    shards = {n: _cast_into_whole(n, w[n][0], chip) for n in BIG_NAMES}
    cond, ffn_w_all, conv_w_all = _unpack_cond(_share_wait(sharing_cond, list(shards.values()), "cond"), c_ctx2)

    mods = _share_wait(_share_start(_mod_shard(cond, w_mod[0], b_mod, chip), "mod", after=[], flips=OTHER_CHIPS), [], "mod")
    mod_me, mod_c = _unpack_mod(mods, dev)

    sems_in, first, token_in = _gather_start([shards["w_in"]], ("w_in",), mod_me, "w_in")
    sems, late, token = _gather_start([shards[n] for n in LATE_NAMES], LATE_NAMES, token_in, "late")
    mod_me = mod_me + token[0:1, 0:1]

    def w_in_all(after):
        arrived = _gather_wait(sems_in, first, ("w_in",), after, "w_in")
        return _forward_halves(list(arrived), ("w_in",), "w_in")[0]

    def late_weights(after):
        arrived = list(_gather_wait(sems, late, LATE_NAMES, after, "late"))
        (w_out_all,) = _forward_halves(arrived[:1], LATE_NAMES[:1], "w_out")
        fsems, passing, _ = _forward_start(arrived[1:], LATE_NAMES[1:], "ffn", after=w_out_all)
        return w_out_all, lambda after2: _forward_wait(fsems, passing, LATE_NAMES[1:], after2, "ffn")

    rpb_rev = jnp.pad(rpb[0][:, :, ::-1], ((0, 0), (0, 1), (48, LANES - 48 - RPB_COLS))).reshape(NH * 16, LANES)
    vec = dict(g_norm1=g_norm1, g_norm2=g_norm2, g_final=g_final2, conv_w=conv_w_all, conv_b=conv_b, ln_g=ln_g, ln_b=ln_b,
               ffn_conv_w=ffn_w_all, ffn_conv_b=ffn_conv_b)
    started = []

    def begin_early(d_up, d_down):
        started.append(_swap_start([d_up, d_down], EARLY_GRADS, "grad_swap_start_early"))

    def carry_on_early(after):
        sems_, grads_, lands_, _ = started.pop()
        grads_, lands_ = _swap_wait(sems_, grads_, lands_, EARLY_GRADS, after, "grad_swap_wait_early")
        parts_ = [_add_halves(n, grads_[i], lands_[i], core) for i, n in enumerate(EARLY_GRADS)]
        started.append(_exchange_start(parts_, EARLY_GRADS, "grad_exchange_start_early"))
        return started[0][3][0:1, 0:1]

    loss_p, grad_x, d_in, d_out, d_up, d_down, small = _local_step(
        x[0], ctx[0], loss_target[0], mod_me, mod_c, vec, w_in_all, late_weights, rpb_rev, (begin_early, carry_on_early))

    out = {}
    sems_, grads_, lands_, _ = _swap_start([d_in, d_out], LAST_GRADS, "grad_swap_start_last")
    early_own = _reduce_finish(started[0], EARLY_GRADS, grad_x, chip, "early")
    parts = dict(dmod=small["dmod"], dmod_c=small["dmod_c"], g_norm1=[small["g_norm1"][0]], g_norm1_ctx=[small["g_norm1"][1]],
                 g_norm2=[small["g_norm2"]], g_final=[small["g_final"]], conv_b=[small["conv_b"]], ln_g=[small["ln_g"]],
                 ln_b=[small["ln_b"]], ffn_conv_b=small["ffn_conv_b"], ffn_conv_w=small["ffn_conv_w"],
                 conv_w=[small["conv_w"]], rpb_rev=[small["rpb_rev"]], loss=[loss_p])
    pack = _pack_small(parts, after=early_own).reshape(8, PACK_N // 8)
    sharing = _share_start(pack, "small_grads", after=[])
    grads_, lands_ = _swap_wait(sems_, grads_, lands_, LAST_GRADS, sharing[2], "grad_swap_wait_last")
    parts_ = [_add_halves(n, grads_[i], lands_[i], core) for i, n in enumerate(LAST_GRADS)]
    last_started = _exchange_start(parts_, LAST_GRADS, "grad_exchange_start_last")
    early_other = _send_halves(early_own, "grad_send_early", after=last_started[3])
    for i, n in enumerate(EARLY_GRADS):
        out[n] = _adamw_halves(n, w[n][0], early_own[i], early_other[i], mom[n][0], var[n][0], core, early_other[i])

    packs = _share_wait(sharing, [out[n][1] for n in EARLY_GRADS], "small_grads").reshape(8, PACK_N)
    w2 = dict(w, g_final=g_final2)
    per, dmod_all, g_conv_w_all, g_ffn_w_all, g_rpb_rev, loss_row = _small_update(packs, w2, mom, var)
    out.update(per)
    out["w_mod"] = _mod_weight_update(cond, dmod_all, w_mod[0], m_w_mod[0], v_w_mod[0], chip)

    sharing_c = _share_start(_cond_grad_partial(dmod_all, w_mod[0], chip), "cond_grad", after=out["w_mod"][1])
    last_own = _reduce_finish(last_started, LAST_GRADS, sharing_c[2], chip, "last")
    last_other = _send_halves(last_own, "grad_send_last", after=last_own[0])
    for i, n in enumerate(LAST_GRADS):
        out[n] = _adamw_halves(n, w[n][0], last_own[i], last_other[i], mom[n][0], var[n][0], core, last_other[i])
    out["c_ctx"] = _cond_update(_share_wait(sharing_c, [out[n][1] for n in LAST_GRADS], "cond_grad"),
                                c_ctx2, m_c_ctx.reshape(1, D), v_c_ctx.reshape(1, D))
    out["conv_w"] = _adamw_cols(conv_w[0], g_conv_w_all, m_conv_w[0], v_conv_w[0], chip, "adamw_conv_w")
    out["ffn_conv_w"] = _adamw_cols(ffn_conv_w[0], g_ffn_w_all, m_ffn_conv_w[0], v_ffn_conv_w[0], chip, "adamw_ffn_conv_w")
    out["rpb"] = _rpb_update(g_rpb_rev, rpb, m_rpb, v_rpb)

    res = [[out[n][k].reshape(w[n].shape) for n in WEIGHTS] for k in range(4)]
    return (loss_row[0, 0], grad_x[None], *res[0], *res[1], *res[2], *res[3])
```

```python
import jax
import jax.numpy as jnp
from jax import lax
from jax.experimental import pallas as pl
from jax.experimental.pallas import tpu as pltpu

F32 = jnp.float32
BF16 = jnp.bfloat16
MXU_DTYPE = jnp.bfloat16

D = 1024
CTX = 256
GW = 64
DA = 512
NH = 8
HD = 64
DC = 512
CW = 31
DFF = 2816
NIN = 3 * DA + 2 * DC
EPS = 1e-6
SCALE = HD ** -0.5
NEG = -1e30
NA_ROWS = 8
PAIR_ROWS = NA_ROWS + 1
TAB_BLOCKS = 17
LANES = 128
VMEM_LIMIT = 56 * 1024 * 1024

ADAM_LR = 0.001
ADAM_B1 = 0.9
ADAM_B2 = 0.999
ADAM_EPS = 1e-08
ADAM_WD = 0.01
ADAM_STEP = 10

MESH = pl.DeviceIdType.MESH


def _pallas(body, *, name, semantics=None, vmem=VMEM_LIMIT, prefetch=0, **kw):
    params = dict(vmem_limit_bytes=vmem)
    if semantics is not None:
        params["dimension_semantics"] = semantics
    if prefetch:
        kw["grid_spec"] = pltpu.PrefetchScalarGridSpec(
            num_scalar_prefetch=prefetch, grid=kw.pop("grid"), in_specs=kw.pop("in_specs"), out_specs=kw.pop("out_specs"),
            scratch_shapes=kw.pop("scratch_shapes", ()))
    return pl.pallas_call(body, name=name, compiler_params=pltpu.CompilerParams(**params), **kw)


def _sds(shape, dtype):
    return jax.ShapeDtypeStruct(shape, dtype)


def _vec_spec(n):
    return pl.BlockSpec((1, n), lambda *_: (0, 0))


def _colsum8(x):
    t, n = x.shape
    return jnp.sum(x.reshape(t // 8, 8, n), axis=0)


def _sigmoid(x):
    return 0.5 * jnp.tanh(0.5 * x) + 0.5


def _mm(a, b, *, mode, m, n, k, tm, tn, tk, out_dtype, name, a_off=(0, 0), b_off=(0, 0),
        out_total=None, o_off=(0, 0), into=None):
    a_list = list(a) if isinstance(a, (list, tuple)) else [a]
    b_list = list(b) if isinstance(b, (list, tuple)) else [b]
    assert m % tm == 0 and n % tn == 0 and k % tk == 0, (name, m, n, k, tm, tn, tk)
    gi, gj, nk = m // tm, n // tn, k // tk
    dims = {"nn": (((1,), (0,)), ((), ())), "nt": (((1,), (1,)), ((), ())), "tn": (((0,), (0,)), ((), ()))}[mode]

    if len(a_list) > 1:
        assert mode != "tn" and nk == 1 and sum(x.shape[1] for x in a_list) == k
        a_specs = [pl.BlockSpec((tm, x.shape[1]), lambda i, j, kk: (i, 0)) for x in a_list]
    elif mode == "tn":
        a_specs = [pl.BlockSpec((tk, tm), lambda i, j, kk: (kk + a_off[0], i + a_off[1]))]
    else:
        a_specs = [pl.BlockSpec((tm, tk), lambda i, j, kk: (i + a_off[0], kk + a_off[1]))]
    if len(b_list) > 1:
        assert mode == "tn" and gj == 1 and sum(x.shape[1] for x in b_list) == n
        b_specs = [pl.BlockSpec((tk, x.shape[1]), lambda i, j, kk: (kk, 0)) for x in b_list]
    elif mode == "nt":
        b_specs = [pl.BlockSpec((tn, tk), lambda i, j, kk: (j + b_off[0], kk + b_off[1]))]
    else:
        b_specs = [pl.BlockSpec((tk, tn), lambda i, j, kk: (kk + b_off[0], j + b_off[1]))]

    na, nb = len(a_list), len(b_list)
    in_place = nk > 1 and out_dtype == F32
    n_in = na + nb + (into is not None)

    def body(*refs):
        a_refs, b_refs, o_ref = refs[:na], refs[na:na + nb], refs[n_in]
        acc = o_ref if in_place else (refs[n_in + 1] if nk > 1 else None)
        kk = pl.program_id(2)

        def whole(piece_refs):
            vals = [r[...].astype(MXU_DTYPE) for r in piece_refs]
            return vals[0] if len(vals) == 1 else jnp.concatenate(vals, axis=1)

        p = lax.dot_general(whole(a_refs), whole(b_refs), dims, preferred_element_type=F32)
        if nk == 1:
            o_ref[...] = p.astype(out_dtype)
            return

        @pl.when(kk == 0)
        def _():
            acc[...] = p

        @pl.when(kk > 0)
        def _():
            acc[...] += p

        if not in_place:
            @pl.when(kk == nk - 1)
            def _():
                o_ref[...] = acc[...].astype(out_dtype)

    ins = [*a_list, *b_list]
    in_specs = a_specs + b_specs
    extra = {}
    if into is not None:
        extra["input_output_aliases"] = {len(ins): 0}
        ins.append(into)
        in_specs.append(pl.BlockSpec(memory_space=pl.ANY))
    return _pallas(
        body, name=name, grid=(gi, gj, nk), in_specs=in_specs,
        out_specs=pl.BlockSpec((tm, tn), lambda i, j, kk: (i + o_off[0], j + o_off[1])),
        out_shape=_sds(out_total or (m, n), out_dtype),
        scratch_shapes=[pltpu.VMEM((tm, tn), F32)] if nk > 1 and not in_place else [],
        semantics=("parallel", "parallel", "arbitrary"), **extra,
    )(*ins)


ROW_TILE = 256


def _row_tile(s, most=2):
    for k in (4, 2):
        if k <= most and s % (k * ROW_TILE) == 0:
            return k * ROW_TILE
    return ROW_TILE


def _rmsmod_fwd(x, ctx, g, sc, sh, csc, csh):
    s = x.shape[0]
    nt = s // ROW_TILE
    assert ctx.shape[0] == ROW_TILE

    def body(x_ref, c_ref, g_ref, sc_ref, sh_ref, csc_ref, csh_ref, o_ref):
        is_ctx = pl.program_id(0) == nt
        xv = jnp.where(is_ctx, c_ref[...], x_ref[...])
        scv = jnp.where(is_ctx, csc_ref[...], sc_ref[...])
        shv = jnp.where(is_ctx, csh_ref[...], sh_ref[...])
        r = lax.rsqrt(jnp.mean(xv * xv, axis=-1, keepdims=True) + EPS)
        y = xv * r * g_ref[...]
        o_ref[...] = (y * (1.0 + scv) + shv).astype(o_ref.dtype)

    return _pallas(
        body, name="rmsmod1_fwd", grid=(nt + 1,),
        in_specs=[pl.BlockSpec((ROW_TILE, D), lambda i: (jnp.minimum(i, nt - 1), 0)),
                  pl.BlockSpec((ROW_TILE, D), lambda i: (0, 0))] + [_vec_spec(D)] * 5,
        out_specs=pl.BlockSpec((ROW_TILE, D), lambda i: (i, 0)),
        out_shape=_sds((s + CTX, D), MXU_DTYPE),
        semantics=("arbitrary",),
    )(x, ctx, g, sc, sh, csc, csh)


def _resid_rmsmod_fwd(x, y, gt, g, sc, sh):
    s = x.shape[0]

    def body(x_ref, y_ref, gt_ref, g_ref, sc_ref, sh_ref, x1_ref, h_ref):
        x1 = x_ref[...] + gt_ref[...] * y_ref[...]
        x1_ref[...] = x1
        r = lax.rsqrt(jnp.mean(x1 * x1, axis=-1, keepdims=True) + EPS)
        h_ref[...] = ((x1 * r * g_ref[...]) * (1.0 + sc_ref[...]) + sh_ref[...]).astype(h_ref.dtype)

    t = _row_tile(s)
    row = pl.BlockSpec((t, D), lambda i: (i, 0))
    return _pallas(
        body, name="resid_rmsmod2_fwd", grid=(s // t,),
        in_specs=[row, row] + [_vec_spec(D)] * 4,
        out_specs=[row, row],
        out_shape=[_sds((s, D), F32), _sds((s, D), MXU_DTYPE)],
        semantics=("parallel",),
    )(x, y, gt, g, sc, sh)


def _final_fwd_bwd(x1, z, gt2, gf, tgt):
    s = x1.shape[0]
    tile = _row_tile(s)
    nt = s // tile

    def body(x1_ref, z_ref, gt_ref, gf_ref, t_ref, dx2_ref, dz_ref, loss_ref, dgt_ref, dgf_ref, a_loss, a_gt, a_gf):
        i = pl.program_id(0)

        @pl.when(i == 0)
        def _():
            a_loss[...] = jnp.zeros_like(a_loss)
            a_gt[...] = jnp.zeros_like(a_gt)
            a_gf[...] = jnp.zeros_like(a_gf)

        zv = z_ref[...]
        gt = gt_ref[...]
        gf_ = gf_ref[...]
        x2 = x1_ref[...] + gt * zv
        r = lax.rsqrt(jnp.mean(x2 * x2, axis=-1, keepdims=True) + EPS)
        xn = x2 * r
        e = xn * gf_ - t_ref[...]
        a_loss[...] += _colsum8(e * e)
        dyo = e * (1.0 / D)
        a_gf[...] += _colsum8(dyo * xn)
        gdy = gf_ * dyo
        dx2 = r * gdy - xn * (r * r) * jnp.mean(x2 * gdy, axis=-1, keepdims=True)
        dx2_ref[...] = dx2
        dz_ref[...] = (gt * dx2).astype(dz_ref.dtype)
        a_gt[...] += _colsum8(dx2 * zv)

        @pl.when(i == nt - 1)
        def _():
            tot = jnp.sum(jnp.sum(a_loss[...], axis=0, keepdims=True), axis=1, keepdims=True) * (0.5 / D)
            loss_ref[...] = jnp.broadcast_to(tot, loss_ref.shape)
            dgt_ref[...] = jnp.sum(a_gt[...], axis=0, keepdims=True)
            dgf_ref[...] = jnp.sum(a_gf[...], axis=0, keepdims=True)

    row = pl.BlockSpec((tile, D), lambda i: (i, 0))
    return _pallas(
        body, name="final_norm_loss", grid=(nt,),
        in_specs=[row, row, _vec_spec(D), _vec_spec(D), row],
        out_specs=[row, row, _vec_spec(LANES), _vec_spec(D), _vec_spec(D)],
        out_shape=[_sds((s, D), F32), _sds((s, D), MXU_DTYPE), _sds((1, LANES), F32), _sds((1, D), F32), _sds((1, D), F32)],
        scratch_shapes=[pltpu.VMEM((8, D), F32)] * 3,
        semantics=("arbitrary",),
    )(x1, z, gt2, gf, tgt)


def _rmsmod_bwd(xin, dh, g, sc, *, name, dh_row0=0, add=None, resid=None):
    s = xin.shape[0]
    tile = _row_tile(s)
    nt = s // tile
    want_dx = add is not None
    assert resid is None or want_dx

    def body(*refs):
        it = iter(refs)
        x_ref, dh_ref, g_ref, sc_ref = next(it), next(it), next(it), next(it)
        add_ref = next(it) if want_dx else None
        gt_ref, y_ref = (next(it), next(it)) if resid is not None else (None, None)
        dsh_ref, dsc_ref, dg_ref = next(it), next(it), next(it)
        dx_ref = next(it) if want_dx else None
        dy_ref, dgt_ref = (next(it), next(it)) if resid is not None else (None, None)
        a_sh, a_sc, a_g = next(it), next(it), next(it)
        a_gt = next(it) if resid is not None else None
        i = pl.program_id(0)

        @pl.when(i == 0)
        def _():
            a_sh[...] = jnp.zeros_like(a_sh)
            a_sc[...] = jnp.zeros_like(a_sc)
            a_g[...] = jnp.zeros_like(a_g)
            if a_gt is not None:
                a_gt[...] = jnp.zeros_like(a_gt)

        xv = x_ref[...]
        dhv = dh_ref[...]
        gv = g_ref[...]
        r = lax.rsqrt(jnp.mean(xv * xv, axis=-1, keepdims=True) + EPS)
        xn = xv * r
        a_sh[...] += _colsum8(dhv)
        a_sc[...] += _colsum8(dhv * (xn * gv))
        dn = dhv * (1.0 + sc_ref[...])
        a_g[...] += _colsum8(dn * xn)
        if want_dx:
            gdn = gv * dn
            dx = add_ref[...] + r * gdn - xn * (r * r) * jnp.mean(xv * gdn, axis=-1, keepdims=True)
            dx_ref[...] = dx
            if resid is not None:
                dy_ref[...] = (gt_ref[...] * dx).astype(dy_ref.dtype)
                a_gt[...] += _colsum8(dx * y_ref[...])

        @pl.when(i == nt - 1)
        def _():
            dsh_ref[...] = jnp.sum(a_sh[...], axis=0, keepdims=True)
            dsc_ref[...] = jnp.sum(a_sc[...], axis=0, keepdims=True)
            dg_ref[...] = jnp.sum(a_g[...], axis=0, keepdims=True)
            if a_gt is not None:
                dgt_ref[...] = jnp.sum(a_gt[...], axis=0, keepdims=True)

    row = pl.BlockSpec((tile, D), lambda i: (i, 0))
    ins = [xin, dh, g, sc]
    in_specs = [row, pl.BlockSpec((tile, D), lambda i: (i + dh_row0 // tile, 0)), _vec_spec(D), _vec_spec(D)]
    out_specs = [_vec_spec(D)] * 3
    out_shape = [_sds((1, D), F32)] * 3
    scratch = [pltpu.VMEM((8, D), F32)] * 3
    if want_dx:
        ins.append(add)
        in_specs.append(row)
        out_specs.append(row)
        out_shape.append(_sds((s, D), F32))
    if resid is not None:
        ins += [resid[0], resid[1]]
        in_specs += [_vec_spec(D), row]
        out_specs += [row, _vec_spec(D)]
        out_shape += [_sds((s, D), MXU_DTYPE), _sds((1, D), F32)]
        scratch.append(pltpu.VMEM((8, D), F32))
    return _pallas(body, name=name, grid=(nt,), in_specs=in_specs, out_specs=out_specs, out_shape=out_shape,
                   scratch_shapes=scratch, semantics=("arbitrary",))(*ins)


FF_TILE = 128
FF_CHUNK = 128
HALO = 8


def _shift3(pad_ref, r0, ch):
    return tuple(pad_ref[pl.ds(r0 + HALO + d, ch), :] for d in (-1, 0, 1))


def _fill_padded(pad_ref, src_ref, s, ch, halo):
    zeros = jnp.zeros((halo, pad_ref.shape[1]), F32)
    pad_ref[0:halo, :] = zeros
    pad_ref[s + halo:s + 2 * halo, :] = zeros

    def cp(c, carry):
        r0 = pl.multiple_of(c * ch, ch)
        pad_ref[pl.ds(r0 + halo, ch), :] = src_ref[pl.ds(r0, ch), :].astype(F32)
        return carry

    lax.fori_loop(0, s // ch, cp, 0)


def _ffn_act_fwd(u, w, b):
    s = u.shape[0]
    tile, ch = FF_TILE, FF_CHUNK
    nj = DFF // tile

    def body(ug_ref, uv_ref, wg_ref, wv_ref, bg_ref, bv_ref, f_ref, gpad, vpad):
        _fill_padded(gpad, ug_ref, s, ch, HALO)
        _fill_padded(vpad, uv_ref, s, ch, HALO)

        def conv(pad, w_ref, b_ref, r0):
            prev, cur, nxt = _shift3(pad, r0, ch)
            return w_ref[0:1, :] * prev + w_ref[1:2, :] * cur + w_ref[2:3, :] * nxt + b_ref[...]

        def step(c, carry):
            r0 = pl.multiple_of(c * ch, ch)
            gc = conv(gpad, wg_ref, bg_ref, r0)
            vc = conv(vpad, wv_ref, bv_ref, r0)
            f_ref[pl.ds(r0, ch), :] = (gc * _sigmoid(gc) * vc).astype(f_ref.dtype)
            return carry

        lax.fori_loop(0, s // ch, step, 0)

    col = lambda off: pl.BlockSpec((s, tile), lambda j: (0, j + off))
    wsp = lambda off: pl.BlockSpec((3, tile), lambda j: (0, j + off))
    bsp = lambda off: pl.BlockSpec((1, tile), lambda j: (0, j + off))
    return _pallas(
        body, name="ffn_act_fwd", grid=(nj,),
        in_specs=[col(0), col(nj), wsp(0), wsp(nj), bsp(0), bsp(nj)],
        out_specs=col(0), out_shape=_sds((s, DFF), MXU_DTYPE),
        scratch_shapes=[pltpu.VMEM((s + 2 * HALO, tile), F32)] * 2,
        semantics=("parallel",),
    )(u, u, w, w, b, b)


def _ffn_act_bwd(u, df, w, b):
    s = u.shape[0]
    nj = DFF // FF_TILE
    ch = FF_CHUNK

    def body(ug_ref, uv_ref, df_ref, wg_ref, wv_ref, bg_ref, bv_ref,
             dug_ref, duv_ref, dwg_ref, dwv_ref, dbg_ref, dbv_ref, gpad, vpad, dgpad, dvpad, acc):
        _fill_padded(gpad, ug_ref, s, ch, HALO)
        _fill_padded(vpad, uv_ref, s, ch, HALO)
        zeros = jnp.zeros((HALO, FF_TILE), F32)
        for p in (dgpad, dvpad):
            p[0:HALO, :] = zeros
            p[s + HALO:s + 2 * HALO, :] = zeros
        acc[...] = jnp.zeros_like(acc)

        def step(c, carry):
            r0 = pl.multiple_of(c * ch, ch)
            gs = _shift3(gpad, r0, ch)
            vs = _shift3(vpad, r0, ch)
            gc = wg_ref[0:1, :] * gs[0] + wg_ref[1:2, :] * gs[1] + wg_ref[2:3, :] * gs[2] + bg_ref[...]
            vc = wv_ref[0:1, :] * vs[0] + wv_ref[1:2, :] * vs[1] + wv_ref[2:3, :] * vs[2] + bv_ref[...]
            sg = _sigmoid(gc)
            dfv = df_ref[pl.ds(r0, ch), :].astype(F32)
            dgc = dfv * vc * (sg * (1.0 + gc * (1.0 - sg)))
            dvc = dfv * (gc * sg)
            dgpad[pl.ds(r0 + HALO, ch), :] = dgc
            dvpad[pl.ds(r0 + HALO, ch), :] = dvc
            for t in range(3):
                acc[8 * t:8 * t + 8, :] += _colsum8(dgc * gs[t])
                acc[24 + 8 * t:32 + 8 * t, :] += _colsum8(dvc * vs[t])
            acc[48:56, :] += _colsum8(dgc)
            acc[56:64, :] += _colsum8(dvc)
            return carry

        lax.fori_loop(0, s // ch, step, 0)

        def step2(c, carry):
            r0 = pl.multiple_of(c * ch, ch)
            for pad, w_ref, o_ref in ((dgpad, wg_ref, dug_ref), (dvpad, wv_ref, duv_ref)):
                prev, cur, nxt = _shift3(pad, r0, ch)
                o_ref[pl.ds(r0, ch), :] = (w_ref[0:1, :] * nxt + w_ref[1:2, :] * cur + w_ref[2:3, :] * prev).astype(o_ref.dtype)
            return carry

        lax.fori_loop(0, s // ch, step2, 0)
        for t in range(3):
            dwg_ref[t:t + 1, :] = jnp.sum(acc[8 * t:8 * t + 8, :], axis=0, keepdims=True)
            dwv_ref[t:t + 1, :] = jnp.sum(acc[24 + 8 * t:32 + 8 * t, :], axis=0, keepdims=True)
        dbg_ref[...] = jnp.sum(acc[48:56, :], axis=0, keepdims=True)
        dbv_ref[...] = jnp.sum(acc[56:64, :], axis=0, keepdims=True)

    col = lambda off: pl.BlockSpec((s, FF_TILE), lambda j: (0, j + off))
    wsp = lambda off: pl.BlockSpec((3, FF_TILE), lambda j: (0, j + off))
    bsp = lambda off: pl.BlockSpec((1, FF_TILE), lambda j: (0, j + off))
    return _pallas(
        body, name="ffn_act_bwd", grid=(nj,),
        in_specs=[col(0), col(nj), col(0), wsp(0), wsp(nj), bsp(0), bsp(nj)],
        out_specs=[col(0), col(0), wsp(0), wsp(0), bsp(0), bsp(0)],
        out_shape=[_sds((s, DFF), MXU_DTYPE)] * 2 + [_sds((3, DFF), F32)] * 2 + [_sds((1, DFF), F32)] * 2,
        scratch_shapes=[pltpu.VMEM((s + 2 * HALO, FF_TILE), F32)] * 4 + [pltpu.VMEM((64, FF_TILE), F32)],
        semantics=("parallel",),
    )(u, u, df, w, w, b, b)


CONV_CHUNK = 64
CONV_HALO = 16


def _tap(pad_ref, r0, k):
    return pad_ref[pl.ds(r0 + CONV_HALO - CW // 2 + k, CONV_CHUNK), :]


def _glu_into(pad_ref, a_ref, g_ref, s):
    zeros = jnp.zeros((CONV_HALO, LANES), F32)
    pad_ref[0:CONV_HALO, :] = zeros
    pad_ref[s + CONV_HALO:s + 2 * CONV_HALO, :] = zeros

    def cp(c, carry):
        r0 = pl.multiple_of(c * ROW_TILE, ROW_TILE)
        pad_ref[pl.ds(r0 + CONV_HALO, ROW_TILE), :] = a_ref[pl.ds(r0, ROW_TILE), :] * _sigmoid(g_ref[pl.ds(r0, ROW_TILE), :])
        return carry

    lax.fori_loop(0, s // ROW_TILE, cp, 0)


def _conf_conv_fwd(ag, conv_w, conv_b):
    s = ag.shape[0]
    nc = DC // LANES

    def body(a_ref, g_ref, w_ref, b_ref, o_ref, upad):
        _glu_into(upad, a_ref, g_ref, s)

        def step(c, carry):
            r0 = pl.multiple_of(c * CONV_CHUNK, CONV_CHUNK)
            acc = jnp.broadcast_to(b_ref[...], (CONV_CHUNK, LANES))
            for k in range(CW):
                acc = acc + w_ref[k:k + 1, :] * _tap(upad, r0, k)
            o_ref[pl.ds(r0, CONV_CHUNK), :] = acc
            return carry

        lax.fori_loop(0, s // CONV_CHUNK, step, 0)

    col = lambda off: pl.BlockSpec((s, LANES), lambda c: (0, c + off))
    return _pallas(
        body, name="conf_conv_fwd", grid=(nc,),
        in_specs=[col(0), col(nc), pl.BlockSpec((CW, LANES), lambda c: (0, c)), pl.BlockSpec((1, LANES), lambda c: (0, c))],
        out_specs=col(0), out_shape=_sds((s, DC), F32),
        scratch_shapes=[pltpu.VMEM((s + 2 * CONV_HALO, LANES), F32)],
        semantics=("parallel",),
    )(ag, ag, conv_w, conv_b)


def _ln_stats(x):
    mu = jnp.mean(x, axis=-1, keepdims=True)
    xc = x - mu
    var = jnp.mean(xc * xc, axis=-1, keepdims=True)
    rstd = lax.rsqrt(var + EPS)
    return xc * rstd, rstd


def _conf_ln_fwd(u1, ln_g, ln_b, ycat):
    s = u1.shape[0]

    def body(u_ref, g_ref, b_ref, ycat_ref, o_ref):
        del ycat_ref
        xhat, _ = _ln_stats(u_ref[...])
        y = xhat * g_ref[...] + b_ref[...]
        o_ref[...] = (y * _sigmoid(y)).astype(o_ref.dtype)

    t = _row_tile(s, 4)
    return _pallas(
        body, name="conf_ln_fwd", grid=(s // t,),
        in_specs=[pl.BlockSpec((t, DC), lambda i: (i, 0)), _vec_spec(DC), _vec_spec(DC),
                  pl.BlockSpec(memory_space=pl.ANY)],
        out_specs=pl.BlockSpec((t, DC), lambda i: (i, 1)),
        out_shape=_sds(ycat.shape, ycat.dtype),
        input_output_aliases={3: 0},
        semantics=("parallel",),
    )(u1, ln_g, ln_b, ycat)


def _conf_ln_bwd(dycat, u1, ln_g, ln_b):
    s = u1.shape[0]
    t = _row_tile(s, 4)
    nt = s // t

    def body(dy_ref, u_ref, g_ref, b_ref, du_ref, dg_ref, db_ref, a_g, a_b):
        i = pl.program_id(0)

        @pl.when(i == 0)
        def _():
            a_g[...] = jnp.zeros_like(a_g)
            a_b[...] = jnp.zeros_like(a_b)

        xhat, rstd = _ln_stats(u_ref[...])
        gv = g_ref[...]
        y = xhat * gv + b_ref[...]
        sg = _sigmoid(y)
        dyl = dy_ref[...] * (sg * (1.0 + y * (1.0 - sg)))
        a_g[...] += _colsum8(dyl * xhat)
        a_b[...] += _colsum8(dyl)
        dxh = dyl * gv
        du_ref[...] = rstd * (dxh - jnp.mean(dxh, axis=-1, keepdims=True)
                              - xhat * jnp.mean(dxh * xhat, axis=-1, keepdims=True))

        @pl.when(i == nt - 1)
        def _():
            dg_ref[...] = jnp.sum(a_g[...], axis=0, keepdims=True)
            db_ref[...] = jnp.sum(a_b[...], axis=0, keepdims=True)

    return _pallas(
        body, name="conf_ln_bwd", grid=(nt,),
        in_specs=[pl.BlockSpec((t, DC), lambda i: (i, 1)), pl.BlockSpec((t, DC), lambda i: (i, 0)),
                  _vec_spec(DC), _vec_spec(DC)],
        out_specs=[pl.BlockSpec((t, DC), lambda i: (i, 0)), _vec_spec(DC), _vec_spec(DC)],
        out_shape=[_sds((s, DC), F32), _sds((1, DC), F32), _sds((1, DC), F32)],
        scratch_shapes=[pltpu.VMEM((8, DC), F32)] * 2,
        semantics=("arbitrary",),
    )(dycat, u1, ln_g, ln_b)


def _conf_conv_bwd(ag, du1, conv_w, rows_out):
    s = ag.shape[0]
    nc = DC // LANES

    def body(a_ref, g_ref, d_ref, w_ref, da_ref, dg_ref, dw_ref, db_ref, upad, dpad, acc):
        _glu_into(upad, a_ref, g_ref, s)
        _fill_padded(dpad, d_ref, s, ROW_TILE, CONV_HALO)
        acc[...] = jnp.zeros_like(acc)

        def step(c, carry):
            r0 = pl.multiple_of(c * CONV_CHUNK, CONV_CHUNK)
            dcur = dpad[pl.ds(r0 + CONV_HALO, CONV_CHUNK), :]
            du0 = jnp.zeros((CONV_CHUNK, LANES), F32)
            for k in range(CW):
                du0 = du0 + w_ref[k:k + 1, :] * _tap(dpad, r0, CW - 1 - k)
                acc[8 * k:8 * k + 8, :] += _colsum8(dcur * _tap(upad, r0, k))
            acc[8 * CW:8 * CW + 8, :] += _colsum8(dcur)
            av = a_ref[pl.ds(r0, CONV_CHUNK), :]
            sg = _sigmoid(g_ref[pl.ds(r0, CONV_CHUNK), :])
            da_ref[pl.ds(r0, CONV_CHUNK), :] = (du0 * sg).astype(da_ref.dtype)
            dg_ref[pl.ds(r0, CONV_CHUNK), :] = (du0 * av * (sg * (1.0 - sg))).astype(dg_ref.dtype)
            return carry

        lax.fori_loop(0, s // CONV_CHUNK, step, 0)
        if rows_out > s:
            zeros = jnp.zeros((rows_out - s, LANES), da_ref.dtype)
            da_ref[s:rows_out, :] = zeros
            dg_ref[s:rows_out, :] = zeros
        for k in range(CW):
            dw_ref[k:k + 1, :] = jnp.sum(acc[8 * k:8 * k + 8, :], axis=0, keepdims=True)
        db_ref[...] = jnp.sum(acc[8 * CW:8 * CW + 8, :], axis=0, keepdims=True)

    col = lambda off: pl.BlockSpec((s, LANES), lambda c: (0, c + off))
    ocol = pl.BlockSpec((rows_out, LANES), lambda c: (0, c))
    return _pallas(
        body, name="conf_conv_bwd", grid=(nc,),
        in_specs=[col(0), col(nc), col(0), pl.BlockSpec((CW, LANES), lambda c: (0, c))],
        out_specs=[ocol, ocol, pl.BlockSpec((CW, LANES), lambda c: (0, c)), pl.BlockSpec((1, LANES), lambda c: (0, c))],
        out_shape=[_sds((rows_out, DC), MXU_DTYPE)] * 2 + [_sds((CW, DC), F32), _sds((1, DC), F32)],
        scratch_shapes=[pltpu.VMEM((s + 2 * CONV_HALO, LANES), F32)] * 2 + [pltpu.VMEM((8 * (CW + 1), LANES), F32)],
        semantics=("parallel",),
    )(ag, ag, du1, conv_w)


Q_TILE = 2 * GW
K_WIN = PAIR_ROWS * GW


def _bias_table(rpb_rev):
    def body(p_ref, t_ref):
        kcol = lax.broadcasted_iota(jnp.int32, (GW, LANES), 0)
        lane = lax.broadcasted_iota(jnp.int32, (GW, LANES), 1)
        qcol = lane % GW
        cs = jnp.clip(qcol - NA_ROWS, 0, GW - 2 * NA_ROWS)
        colvalid = (kcol >= cs) & (kcol < cs + 2 * NA_ROWS)
        neg = jnp.full((GW, LANES), NEG, F32)

        def skew(h, ro, shift):
            if ro < 0 or ro >= 2 * NA_ROWS - 1:
                return neg
            row = jnp.broadcast_to(p_ref[h * 16 + ro:h * 16 + ro + 1, :], (GW, LANES))
            return pltpu.roll(row, shift, 1, stride=1, stride_axis=0)

        for h in range(NH):
            for b in range(TAB_BLOCKS):
                val = jnp.where(lane < GW, skew(h, b - 1, GW + 1), skew(h, b - 2, 1))
                t_ref[h, b * GW:(b + 1) * GW, :] = jnp.where(colvalid, val, neg)

    return _pallas(body, name="attn_bias_table", out_shape=_sds((NH, TAB_BLOCKS * GW, LANES), F32))(rpb_rev)


def _rpb_grad(tt):
    def body(t_ref, o_ref):
        lane = lax.broadcasted_iota(jnp.int32, (GW, LANES), 1)
        si = lax.broadcasted_iota(jnp.int32, (GW, GW), 0)
        ti = lax.broadcasted_iota(jnp.int32, (GW, GW), 1)
        flip = jnp.where(si + ti == GW - 1, 1.0, 0.0).astype(F32)
        o_ref[...] = jnp.zeros_like(o_ref)
        for h in range(NH):
            for ro in range(2 * NA_ROWS - 1):
                lo = t_ref[h, (ro + 1) * GW:(ro + 2) * GW, :]
                hi = t_ref[h, (ro + 2) * GW:(ro + 3) * GW, :]
                g = jnp.where(lane < GW, lo + pltpu.roll(hi, GW, 1), 0.0)
                gf = jnp.dot(flip, g, preferred_element_type=F32, precision=lax.Precision.HIGHEST)
                sk = pltpu.roll(gf, 0, 1, stride=1, stride_axis=0)
                o_ref[h * 16 + ro:h * 16 + ro + 1, :] = jnp.sum(sk, axis=0, keepdims=True)

    return _pallas(body, name="attn_rpb_grad", out_shape=_sds((NH * 16, LANES), F32))(tt)


def _attn_geometry(i, rows):
    wsp = jnp.clip(2 * i - NA_ROWS // 2, 0, rows - PAIR_ROWS)
    k0 = pl.multiple_of(wsp * GW, GW)
    t0 = pl.multiple_of((wsp - 2 * i + NA_ROWS) * GW, GW)
    rr = lax.broadcasted_iota(jnp.int32, (GW, Q_TILE), 1) // GW
    wsr = jnp.clip(2 * i + rr - NA_ROWS // 2, 0, rows - NA_ROWS)
    edge_masks = tuple(jnp.where((kr >= wsr) & (kr < wsr + NA_ROWS), 0.0, NEG).astype(F32)
                       for kr in (wsp, wsp + PAIR_ROWS - 1))
    return k0, t0, edge_masks


def _biased(s_raw, bias, edge_masks):
    x = s_raw + bias
    return jnp.concatenate([x[:GW] + edge_masks[0], x[GW:K_WIN - GW], x[K_WIN - GW:] + edge_masks[1]], axis=0)


def _two_heads_on_lanes(xt):
    feat = lax.broadcasted_iota(jnp.int32, xt.shape, 0)
    zero = jnp.zeros_like(xt)
    return jnp.concatenate([jnp.where(feat < HD, xt, zero), jnp.where(feat >= HD, xt, zero)], axis=1)


def _two_heads_on_rows(x):
    lane = lax.broadcasted_iota(jnp.int32, x.shape, 1)
    zero = jnp.zeros_like(x)
    return jnp.concatenate([jnp.where(lane < HD, x, zero), jnp.where(lane >= HD, x, zero)], axis=0)


def _pick_heads(x2):
    n = x2.shape[0] // 2
    lane = lax.broadcasted_iota(jnp.int32, (n, LANES), 1)
    return jnp.where(lane < HD, x2[:n], x2[n:])


_TN = (((0,), (0,)), ((), ()))


def _attn_fwd(qkv, tab, s):
    rows = s // GW
    npair = rows // 2

    def body(q_ref, kv_ref, tab_ref, o_ref, lse_ref):
        i = pl.program_id(0)
        k0, t0, edge_masks = _attn_geometry(i, rows)
        for p in range(NH // 2):
            cq = slice(p * LANES, (p + 1) * LANES)
            ck = slice(DA + p * LANES, DA + (p + 1) * LANES)
            cv = slice(2 * DA + p * LANES, 2 * DA + (p + 1) * LANES)
            qm2 = _two_heads_on_lanes(q_ref[:, cq].T) * SCALE
            s_loc = jnp.dot(kv_ref[pl.ds(k0, K_WIN), ck], qm2, preferred_element_type=F32)
            s_ctx = jnp.dot(kv_ref[pl.ds(s, CTX), ck], qm2, preferred_element_type=F32)
            p_loc, p_ctx = [], []
            for hh in range(2):
                h = 2 * p + hh
                ch = slice(hh * Q_TILE, (hh + 1) * Q_TILE)
                sl = _biased(s_loc[:, ch], tab_ref[h, pl.ds(t0, K_WIN), :], edge_masks)
                sc = s_ctx[:, ch]
                m = jnp.maximum(jnp.max(sl, axis=0, keepdims=True), jnp.max(sc, axis=0, keepdims=True))
                el = jnp.exp(sl - m)
                ec = jnp.exp(sc - m)
                l = jnp.sum(el, axis=0, keepdims=True) + jnp.sum(ec, axis=0, keepdims=True)
                inv = 1.0 / l
                lse_ref[h:h + 1, :] = m + jnp.log(l)
                p_loc.append((el * inv).astype(MXU_DTYPE))
                p_ctx.append((ec * inv).astype(MXU_DTYPE))
            o2 = (lax.dot_general(jnp.concatenate(p_loc, axis=1), kv_ref[pl.ds(k0, K_WIN), cv], _TN, preferred_element_type=F32)
                  + lax.dot_general(jnp.concatenate(p_ctx, axis=1), kv_ref[pl.ds(s, CTX), cv], _TN, preferred_element_type=F32))
            o_ref[:, cq] = _pick_heads(o2).astype(o_ref.dtype)

    return _pallas(
        body, name="attn_fwd", grid=(npair,),
        in_specs=[pl.BlockSpec((Q_TILE, DA), lambda i: (i, 0)), pl.BlockSpec(memory_space=pltpu.VMEM),
                  pl.BlockSpec(memory_space=pltpu.VMEM)],
        out_specs=[pl.BlockSpec((Q_TILE, DA), lambda i: (i, 0)), pl.BlockSpec((NH, Q_TILE), lambda i: (0, i))],
        out_shape=[_sds((s, D), MXU_DTYPE), _sds((NH, s), F32)],
        semantics=("arbitrary",),
    )(qkv, qkv, tab)


def _attn_bwd(qkv, tab, lse, dycat, s):
    rows = s // GW
    npair = rows // 2
    sa = s + CTX
    nzero = CTX // Q_TILE

    def body(q_ref, do_ref, lse_ref, kv_ref, tab_ref, dq_ref, dkv_ref, tt_ref, dk_acc, dv_acc):
        i = pl.program_id(0)

        @pl.when(i == 0)
        def _():
            dk_acc[...] = jnp.zeros_like(dk_acc)
            dv_acc[...] = jnp.zeros_like(dv_acc)
            tt_ref[...] = jnp.zeros_like(tt_ref)

        @pl.when(i >= npair)
        def _():
            dq_ref[...] = jnp.zeros_like(dq_ref)

        @pl.when(i < npair)
        def _():
            k0, t0, edge_masks = _attn_geometry(i, rows)
            for p in range(NH // 2):
                cq = slice(p * LANES, (p + 1) * LANES)
                ck = slice(DA + p * LANES, DA + (p + 1) * LANES)
                cv = slice(2 * DA + p * LANES, 2 * DA + (p + 1) * LANES)
                qp = q_ref[:, cq] * SCALE
                dop = do_ref[:, cq].astype(MXU_DTYPE)
                qm2 = _two_heads_on_lanes(qp.T)
                dom2 = _two_heads_on_lanes(dop.T)
                kw = kv_ref[pl.ds(k0, K_WIN), ck]
                kc = kv_ref[pl.ds(s, CTX), ck]
                vw = kv_ref[pl.ds(k0, K_WIN), cv]
                vc = kv_ref[pl.ds(s, CTX), cv]
                s_loc = jnp.dot(kw, qm2, preferred_element_type=F32)
                s_ctx = jnp.dot(kc, qm2, preferred_element_type=F32)
                dp_loc = jnp.dot(vw, dom2, preferred_element_type=F32)
                dp_ctx = jnp.dot(vc, dom2, preferred_element_type=F32)
                p_loc, p_ctx, ds_loc, ds_ctx = [], [], [], []
                for hh in range(2):
                    h = 2 * p + hh
                    ch = slice(hh * Q_TILE, (hh + 1) * Q_TILE)
                    lse_h = lse_ref[h:h + 1, :]
                    pl_ = jnp.exp(_biased(s_loc[:, ch], tab_ref[h, pl.ds(t0, K_WIN), :], edge_masks) - lse_h)
                    pc_ = jnp.exp(s_ctx[:, ch] - lse_h)
                    dpl = dp_loc[:, ch]
                    dpc = dp_ctx[:, ch]
                    delta = jnp.sum(pl_ * dpl, axis=0, keepdims=True) + jnp.sum(pc_ * dpc, axis=0, keepdims=True)
                    dsl = pl_ * (dpl - delta)
                    dsc = pc_ * (dpc - delta)
                    tt_ref[h, pl.ds(t0, K_WIN), :] += dsl
                    p_loc.append(pl_.astype(MXU_DTYPE))
                    p_ctx.append(pc_.astype(MXU_DTYPE))
                    ds_loc.append(dsl.astype(MXU_DTYPE))
                    ds_ctx.append(dsc.astype(MXU_DTYPE))
                p_loc, p_ctx = jnp.concatenate(p_loc, axis=1), jnp.concatenate(p_ctx, axis=1)
                ds_loc, ds_ctx = jnp.concatenate(ds_loc, axis=1), jnp.concatenate(ds_ctx, axis=1)
                do_rows = _two_heads_on_rows(dop)
                q_rows = _two_heads_on_rows(qp)
                dv_acc[pl.ds(k0, K_WIN), cq] += jnp.dot(p_loc, do_rows, preferred_element_type=F32)
                dv_acc[pl.ds(s, CTX), cq] += jnp.dot(p_ctx, do_rows, preferred_element_type=F32)
                dk_acc[pl.ds(k0, K_WIN), cq] += jnp.dot(ds_loc, q_rows, preferred_element_type=F32)
                dk_acc[pl.ds(s, CTX), cq] += jnp.dot(ds_ctx, q_rows, preferred_element_type=F32)
                dq2 = (lax.dot_general(ds_loc, kw, _TN, preferred_element_type=F32)
                       + lax.dot_general(ds_ctx, kc, _TN, preferred_element_type=F32))
                dq_ref[:, cq] = (_pick_heads(dq2) * SCALE).astype(dq_ref.dtype)

        @pl.when(i == npair - 1)
        def _():
            def cp(c, carry):
                r0 = pl.multiple_of(c * ROW_TILE, ROW_TILE)
                dkv_ref[pl.ds(r0, ROW_TILE), 0:DA] = dk_acc[pl.ds(r0, ROW_TILE), :].astype(dkv_ref.dtype)
                dkv_ref[pl.ds(r0, ROW_TILE), DA:2 * DA] = dv_acc[pl.ds(r0, ROW_TILE), :].astype(dkv_ref.dtype)
                return carry

            lax.fori_loop(0, sa // ROW_TILE, cp, 0)

    qmap = lambda i: (jnp.minimum(i, npair - 1), 0)
    return _pallas(
        body, name="attn_bwd", grid=(npair + nzero,),
        in_specs=[pl.BlockSpec((Q_TILE, DA), qmap), pl.BlockSpec((Q_TILE, DA), qmap),
                  pl.BlockSpec((NH, Q_TILE), lambda i: (0, jnp.minimum(i, npair - 1))),
                  pl.BlockSpec(memory_space=pltpu.VMEM), pl.BlockSpec(memory_space=pltpu.VMEM)],
        out_specs=[pl.BlockSpec((Q_TILE, DA), lambda i: (i, 0)), pl.BlockSpec(memory_space=pltpu.VMEM),
                   pl.BlockSpec(memory_space=pltpu.VMEM)],
        out_shape=[_sds((sa, DA), MXU_DTYPE), _sds((sa, 2 * DA), MXU_DTYPE), _sds((NH, TAB_BLOCKS * GW, LANES), F32)],
        scratch_shapes=[pltpu.VMEM((sa, DA), F32)] * 2,
        semantics=("arbitrary",),
    )(qkv, dycat, lse, qkv, tab)


def _tile(n, prefs):
    for t in prefs:
        if n % t == 0:
            return t
    raise ValueError((n, prefs))


def _local_step(x, ctx, tgt, mod, mod_c, vec, w_in, late_weights, rpb_rev, early_grads=None):
    s = x.shape[0]
    sa = s + CTX
    ts = _tile(s, (1024, 512, 256))
    ts2 = _tile(s, (2048, 1024, 512, 256))
    tsa = _tile(sa, (1088, 640, 256))
    tsa2 = _tile(sa, (2176, 640, 256))
    sh1, sc1, gt1, sh2, sc2, gt2 = (mod[i:i + 1] for i in range(6))
    csh1, csc1 = mod_c[0:1], mod_c[1:2]
    act = MXU_DTYPE

    tab = _bias_table(rpb_rev)
    h_all = _rmsmod_fwd(x, ctx, vec["g_norm1"], sc1, sh1, csc1, csh1)
    w_in = w_in(h_all) if callable(w_in) else w_in
    qkv = _mm(h_all, w_in, mode="nn", m=sa, n=3 * DA, k=D, tm=tsa2, tn=512, tk=D, out_dtype=MXU_DTYPE, name="mm_qkv")
    ag = _mm(h_all, w_in, mode="nn", m=s, n=2 * DC, k=D, tm=ts2, tn=512, tk=D, out_dtype=F32, name="mm_ag", b_off=(0, 3))
    ycat, lse = _attn_fwd(qkv, tab, s)
    u1 = _conf_conv_fwd(ag, vec["conv_w"], vec["conv_b"])
    ycat = _conf_ln_fwd(u1, vec["ln_g"], vec["ln_b"], ycat)
    if callable(late_weights):
        w_out, ffn_weights = late_weights(ycat)
    else:
        w_out, ffn_weights = late_weights[0], late_weights[1:]
    y = _mm(ycat, w_out, mode="nn", m=s, n=D, k=D, tm=ts2, tn=512, tk=D, out_dtype=F32, name="mm_out")
    x1, h2 = _resid_rmsmod_fwd(x, y, gt1, vec["g_norm2"], sc2, sh2)
    w_up, w_down = ffn_weights(h2) if callable(ffn_weights) else ffn_weights
    u = _mm(h2, w_up, mode="nn", m=s, n=2 * DFF, k=D, tm=ts2, tn=512, tk=D, out_dtype=act, name="mm_up")
    f = _ffn_act_fwd(u, vec["ffn_conv_w"], vec["ffn_conv_b"])
    z = _mm(f, w_down, mode="nn", m=s, n=D, k=DFF, tm=ts, tn=D, tk=DFF, out_dtype=F32, name="mm_down")
    dx2, dz, loss, dgt2, dgf = _final_fwd_bwd(x1, z, gt2, vec["g_final"], tgt)

    df = _mm(dz, w_down, mode="nt", m=s, n=DFF, k=D, tm=ts, tn=DFF, tk=D, out_dtype=act, name="mm_down_dx")
    d_w_down = _mm(f, dz, mode="tn", m=DFF, n=D, k=s, tm=DFF // 2, tn=D, tk=ts2, out_dtype=F32, name="mm_down_dw")
    dug, duv, dfw_g, dfw_v, dfb_g, dfb_v = _ffn_act_bwd(u, df, vec["ffn_conv_w"], vec["ffn_conv_b"])
    dw_kw = dict(mode="tn", m=D, n=DFF, k=s, tm=D, tn=DFF, tk=ts, out_dtype=F32, out_total=(D, 2 * DFF))
    d_w_up = _mm(h2, dug, name="mm_up_dw_gate", **dw_kw)
    d_w_up = _mm(h2, duv, name="mm_up_dw_val", o_off=(0, 1), into=d_w_up, **dw_kw)
    if early_grads is not None:
        early_grads[0](d_w_up, d_w_down)
    dh2 = _mm([dug, duv], w_up, mode="nt", m=s, n=D, k=2 * DFF, tm=ts, tn=D, tk=2 * DFF, out_dtype=F32, name="mm_up_dx")
    sc2_b = sc2 if early_grads is None else sc2 + early_grads[1](dh2)
    dsh2, dsc2, dg2, dx1, dy, dgt1 = _rmsmod_bwd(x1, dh2, vec["g_norm2"], sc2_b, name="rmsmod2_bwd", add=dx2, resid=(gt1, y))
    dycat = _mm(dy, w_out, mode="nt", m=s, n=D, k=D, tm=ts2, tn=512, tk=D, out_dtype=F32, name="mm_out_dx")
    d_w_out = _mm(ycat, dy, mode="tn", m=D, n=D, k=s, tm=D, tn=D, tk=ts, out_dtype=F32, name="mm_out_dw")
    du1, dln_g, dln_b = _conf_ln_bwd(dycat, u1, vec["ln_g"], vec["ln_b"])
    da, dg, dconv_w, dconv_b = _conf_conv_bwd(ag, du1, vec["conv_w"], sa)
    dq, dkv, tt = _attn_bwd(qkv, tab, lse, dycat, s)
    drpb_rev = _rpb_grad(tt)
    d_pieces = [dq, dkv, da, dg]
    dh = _mm(d_pieces, w_in, mode="nt", m=sa, n=D, k=NIN, tm=tsa, tn=D, tk=NIN, out_dtype=F32, name="mm_in_dx")
    d_w_in = _mm(h_all, d_pieces, mode="tn", m=D, n=NIN, k=sa, tm=D, tn=NIN, tk=tsa, out_dtype=F32, name="mm_in_dw")
    dsh1, dsc1, dg1, grad_x = _rmsmod_bwd(x, dh, vec["g_norm1"], sc1, name="rmsmod1_bwd", add=dx1)
    dcsh1, dcsc1, dg1c = _rmsmod_bwd(ctx, dh, vec["g_norm1"], csc1, name="rmsmod1_ctx_bwd", dh_row0=s)

    small = dict(
        dmod=[dsh1, dsc1, dgt1, dsh2, dsc2, dgt2], dmod_c=[dcsh1, dcsc1],
        g_norm1=[dg1, dg1c], g_norm2=dg2, g_final=dgf, conv_b=dconv_b, ln_g=dln_g, ln_b=dln_b, conv_w=dconv_w,
        ffn_conv_w=[dfw_g, dfw_v], ffn_conv_b=[dfb_g, dfb_v], rpb_rev=drpb_rev,
    )
    return loss, grad_x, d_w_in, d_w_out, d_w_up, d_w_down, small


N_CHIPS = 4
HBM = pl.BlockSpec(memory_space=pl.ANY)
BIG = {"w_in": ("col", (D, NIN)), "w_out": ("row", (D, D)), "w_up": ("col", (D, 2 * DFF)), "w_down": ("row", (DFF, D))}
BIG_NAMES = tuple(BIG)
LATE_NAMES = ("w_out", "w_up", "w_down")


def _shard_shape(name):
    kind, (r, c) = BIG[name]
    return (r, c // N_CHIPS) if kind == "col" else (r // N_CHIPS, c)


def _half_rows(name):
    return _shard_shape(name)[0] // 2


def _place():
    x, y, c = lax.axis_index("x"), lax.axis_index("y"), lax.axis_index("c")
    others = [(1 - x, y), (x, 1 - y), (1 - x, 1 - y)]
    return x, y, c, 2 * x + y, (x, y, 1 - c), others


def _whole_region(ref, name, chip, half):
    kind, _ = BIG[name]
    r, c = _shard_shape(name)
    if kind == "col":
        return ref.at[pl.ds(half * (r // 2), r // 2), pl.ds(chip * c, c)]
    return ref.at[pl.ds(chip * r + half * (r // 2), r // 2), :]


def _remote(src, dst, send_sem, recv_sem, to):
    return pltpu.make_async_remote_copy(src_ref=src, dst_ref=dst, send_sem=send_sem, recv_sem=recv_sem,
                                        device_id=to, device_id_type=MESH)


def _cast_into_whole(name, shard, chip):
    kind, whole = BIG[name]
    r, c = shard.shape
    if kind == "col":
        tr = 256
        o_spec = pl.BlockSpec((tr, c), lambda i, ch: (i, ch[0]))
    else:
        tr = _tile(r, (128, 352))
        o_spec = pl.BlockSpec((tr, c), lambda i, ch: (ch[0] * (r // tr) + i, 0))

    def body(ch_ref, x_ref, o_ref):
        del ch_ref
        o_ref[...] = x_ref[...].astype(o_ref.dtype)

    return _pallas(body, name="cast_" + name, prefetch=1, grid=(r // tr,),
                   in_specs=[pl.BlockSpec((tr, c), lambda i, ch: (i, 0))], out_specs=o_spec,
                   out_shape=_sds(whole, MXU_DTYPE), semantics=("parallel",))(chip, shard)


SEM = pl.BlockSpec(memory_space=pltpu.SEMAPHORE)
IN_HBM = pl.BlockSpec(memory_space=pltpu.HBM)
DATAFLOW = pltpu.SideEffectType.DATAFLOW_SIDE_EFFECTING


def _keep_in_hbm(a):
    return pltpu.with_memory_space_constraint(a, pltpu.HBM)


def _several(after):
    return list(after) if isinstance(after, (list, tuple)) else [after]


FLIPS = [(dx, dy, dc) for dx in (0, 1) for dy in (0, 1) for dc in (0, 1)][1:]
OTHER_CHIPS = [f for f in FLIPS if f[2] == 0]


def _flipped(flip):
    x, y, c = lax.axis_index("x"), lax.axis_index("y"), lax.axis_index("c")
    return tuple(1 - v if f else v for v, f in zip((x, y, c), flip))


def _share_start(v, tag, after, flips=FLIPS):
    r, n = v.shape
    ns = 2 * len(flips)

    def body(*refs):
        v_ref, land_ref = refs[0], refs[1]
        sems = refs[2 + len(_several(after)):2 + len(_several(after)) + ns]
        x, y, c = lax.axis_index("x"), lax.axis_index("y"), lax.axis_index("c")
        mine = land_ref.at[pl.ds((4 * x + 2 * y + c) * r, r), :]
        for k, flip in enumerate(flips):
            _remote(v_ref, mine, sems[2 * k], sems[2 * k + 1], _flipped(flip)).start()

    res = pl.pallas_call(
        body, name="share_" + tag + "_start",
        out_shape=(*[pltpu.SemaphoreType.DMA(())] * ns, pltpu.HBM(v.shape, v.dtype), pltpu.HBM((8 * r, n), v.dtype)),
        in_specs=[IN_HBM] * 2 + [pl.BlockSpec(memory_space=pl.ANY)] * len(_several(after)),
        out_specs=(*[SEM] * ns, IN_HBM, IN_HBM),
        input_output_aliases={0: ns, 1: ns + 1},
        compiler_params=pltpu.CompilerParams(has_side_effects=DATAFLOW),
    )(_keep_in_hbm(v), _keep_in_hbm(jnp.tile(v, (8, 1))), *_several(after))
    return list(res[:ns]), res[ns], res[ns + 1], flips


def _share_wait(started, after, tag):
    sems, v, land, flips = started
    r = v.shape[0]
    ns = len(sems)

    def body(*refs):
        v_ref, land_ref = refs[0], refs[1]
        sem_refs = refs[2:2 + ns]
        for k, flip in enumerate(flips):
            px, py, pc = _flipped(flip)
            theirs = land_ref.at[pl.ds((4 * px + 2 * py + pc) * r, r), :]
            cp = _remote(v_ref, theirs, sem_refs[2 * k], sem_refs[2 * k + 1], (px, py, pc))
            cp.wait_send()
            cp.wait_recv()

    res = pl.pallas_call(
        body, name="share_" + tag + "_wait",
        out_shape=(pltpu.HBM(v.shape, v.dtype), pltpu.HBM(land.shape, land.dtype)),
        in_specs=[IN_HBM] * 2 + [SEM] * ns + [pl.BlockSpec(memory_space=pl.ANY)] * len(_several(after)),
        out_specs=(IN_HBM, IN_HBM),
        input_output_aliases={0: 0, 1: 1},
        compiler_params=pltpu.CompilerParams(has_side_effects=DATAFLOW),
    )(v, land, *sems, *_several(after))
    return res[1]


def _gather_start(wholes, names, after, tag):
    nw = len(names)
    ns = 2 * 3 * nw

    def body(*refs):
        ins = refs[:nw]
        sems = refs[nw + 1:nw + 1 + ns]
        token = refs[2 * nw + ns + 1]
        _, _, c, chip, _, others = _place()
        for w, name in enumerate(names):
            mine = _whole_region(ins[w], name, chip, c)
            for t, (ox, oy) in enumerate(others):
                k = 2 * (3 * w + t)
                _remote(mine, mine, sems[k], sems[k + 1], (ox, oy, c)).start()
        token[...] = jnp.zeros_like(token)

    res = pl.pallas_call(
        body, name="gather_" + tag + "_start",
        out_shape=(*[pltpu.SemaphoreType.DMA(())] * ns, *[pltpu.HBM(a.shape, a.dtype) for a in wholes], _sds((8, LANES), F32)),
        in_specs=[IN_HBM] * nw + [pl.BlockSpec(memory_space=pl.ANY)],
        out_specs=(*[SEM] * ns, *[IN_HBM] * nw, pl.BlockSpec(memory_space=pltpu.VMEM)),
        input_output_aliases={i: ns + i for i in range(nw)},
        compiler_params=pltpu.CompilerParams(has_side_effects=DATAFLOW),
    )(*[_keep_in_hbm(a) for a in wholes], after)
    return list(res[:ns]), list(res[ns:ns + nw]), res[ns + nw]


def _gather_wait(sems, wholes, names, after, tag):
    nw = len(names)
    ns = len(sems)

    def body(*refs):
        ins = refs[:nw]
        sem_refs = refs[nw:nw + ns]
        _, _, c, chip, _, others = _place()
        for w, name in enumerate(names):
            mine = _whole_region(ins[w], name, chip, c)
            for t, (ox, oy) in enumerate(others):
                got = _whole_region(ins[w], name, 2 * ox + oy, c)
                k = 2 * (3 * w + t)
                cp = _remote(mine, got, sem_refs[k], sem_refs[k + 1], (ox, oy, c))
                cp.wait_send()
                cp.wait_recv()

    return pl.pallas_call(
        body, name="gather_" + tag + "_wait",
        out_shape=tuple(pltpu.HBM(a.shape, a.dtype) for a in wholes),
        in_specs=[IN_HBM] * nw + [SEM] * ns + [pl.BlockSpec(memory_space=pl.ANY)], out_specs=tuple([IN_HBM] * nw),
        input_output_aliases={i: i for i in range(nw)},
        compiler_params=pltpu.CompilerParams(has_side_effects=DATAFLOW),
    )(*wholes, *sems, after)


def _forward_halves(wholes, names, tag):
    nw = len(names)

    def body(*refs):
        outs = refs[nw:2 * nw]
        send_sems, recv_sems = refs[2 * nw:]
        _, _, c, _, sibling, others = _place()
        sends = []
        for w, name in enumerate(names):
            for t, (ox, oy) in enumerate(others):
                got = _whole_region(outs[w], name, 2 * ox + oy, c)
                cp = _remote(got, got, send_sems.at[w, t], recv_sems.at[w, t], sibling)
                cp.start()
                sends.append(cp)
        for w, name in enumerate(names):
            for t, (ox, oy) in enumerate(others):
                got = _whole_region(outs[w], name, 2 * ox + oy, 1 - c)
                _remote(got, got, send_sems.at[w, t], recv_sems.at[w, t], sibling).wait_recv()
        for cp in sends:
            cp.wait_send()

    return pl.pallas_call(
        body, name="gather_" + tag + "_forward",
        out_shape=[_sds(a.shape, a.dtype) for a in wholes],
        in_specs=[HBM] * nw, out_specs=[HBM] * nw,
        input_output_aliases={i: i for i in range(nw)},
        scratch_shapes=[pltpu.SemaphoreType.DMA((nw, 3)), pltpu.SemaphoreType.DMA((nw, 3))],
    )(*wholes)


def _forward_start(wholes, names, tag, after):
    nw = len(names)
    ns = 2 * 3 * nw

    def body(*refs):
        ins = refs[:nw]
        sems = refs[nw + 1:nw + 1 + ns]
        token = refs[2 * nw + ns + 1]
        _, _, c, _, sibling, others = _place()
        for w, name in enumerate(names):
            for t, (ox, oy) in enumerate(others):
                got = _whole_region(ins[w], name, 2 * ox + oy, c)
                k = 2 * (3 * w + t)
                _remote(got, got, sems[k], sems[k + 1], sibling).start()
        token[...] = jnp.zeros_like(token)

    res = pl.pallas_call(
        body, name="gather_" + tag + "_forward_start",
        out_shape=(*[pltpu.SemaphoreType.DMA(())] * ns, *[pltpu.HBM(a.shape, a.dtype) for a in wholes], _sds((8, LANES), F32)),
        in_specs=[IN_HBM] * nw + [pl.BlockSpec(memory_space=pl.ANY)],
        out_specs=(*[SEM] * ns, *[IN_HBM] * nw, pl.BlockSpec(memory_space=pltpu.VMEM)),
        input_output_aliases={i: ns + i for i in range(nw)},
        compiler_params=pltpu.CompilerParams(has_side_effects=DATAFLOW),
    )(*[_keep_in_hbm(a) for a in wholes], after)
    return list(res[:ns]), list(res[ns:ns + nw]), res[ns + nw]


def _forward_wait(sems, wholes, names, after, tag):
    nw = len(names)
    ns = len(sems)

    def body(*refs):
        ins = refs[:nw]
        sem_refs = refs[nw:nw + ns]
        _, _, c, _, sibling, others = _place()
        for w, name in enumerate(names):
            for t, (ox, oy) in enumerate(others):
                k = 2 * (3 * w + t)
                cp = _remote(_whole_region(ins[w], name, 2 * ox + oy, c), _whole_region(ins[w], name, 2 * ox + oy, 1 - c),
                             sem_refs[k], sem_refs[k + 1], sibling)
                cp.wait_send()
                cp.wait_recv()

    return pl.pallas_call(
        body, name="gather_" + tag + "_forward_wait",
        out_shape=tuple(pltpu.HBM(a.shape, a.dtype) for a in wholes),
        in_specs=[IN_HBM] * nw + [SEM] * ns + [pl.BlockSpec(memory_space=pl.ANY)], out_specs=tuple([IN_HBM] * nw),
        input_output_aliases={i: i for i in range(nw)},
        compiler_params=pltpu.CompilerParams(has_side_effects=DATAFLOW),
    )(*wholes, *sems, after)


def _compact_shape(name, dtype):
    kind, (r, c) = BIG[name]
    return _sds((r // 2, c), dtype)


def _swap_pairs(ins, outs, names, c):
    pairs = []
    for w, name in enumerate(names):
        kind, _ = BIG[name]
        half = _half_rows(name)
        if kind == "col":
            pairs.append((ins[w].at[pl.ds((1 - c) * half, half), :], outs[w]))
        else:
            pairs += [(ins[w].at[pl.ds(jj * 2 * half + (1 - c) * half, half), :], outs[w].at[pl.ds(jj * half, half), :])
                      for jj in range(N_CHIPS)]
    return pairs


def _n_swap_copies(names):
    return sum(1 if BIG[n][0] == "col" else N_CHIPS for n in names)


def _swap_start(grads, names, label):
    nw = len(names)
    ns = 2 * _n_swap_copies(names)

    def body(*refs):
        ins, lands = refs[:nw], refs[nw:2 * nw]
        sems = refs[2 * nw:2 * nw + ns]
        token = refs[4 * nw + ns]
        _, _, c, _, sibling, _ = _place()
        for k, (src, dst) in enumerate(_swap_pairs(ins, lands, names, c)):
            _remote(src, dst, sems[2 * k], sems[2 * k + 1], sibling).start()
        token[...] = jnp.zeros_like(token)

    lands = [_keep_in_hbm(lax.empty(_compact_shape(n, F32).shape, F32)) for n in names]
    res = pl.pallas_call(
        body, name=label,
        out_shape=(*[pltpu.SemaphoreType.DMA(())] * ns, *[pltpu.HBM(a.shape, a.dtype) for a in grads],
                   *[pltpu.HBM(a.shape, a.dtype) for a in lands], _sds((8, LANES), F32)),
        in_specs=[IN_HBM] * (2 * nw),
        out_specs=(*[SEM] * ns, *[IN_HBM] * (2 * nw), pl.BlockSpec(memory_space=pltpu.VMEM)),
        input_output_aliases={i: ns + i for i in range(2 * nw)},
        compiler_params=pltpu.CompilerParams(has_side_effects=DATAFLOW),
    )(*[_keep_in_hbm(a) for a in grads], *lands)
    return list(res[:ns]), list(res[ns:ns + nw]), list(res[ns + nw:ns + 2 * nw]), res[ns + 2 * nw]


def _swap_wait(sems, grads, lands, names, after, label):
    nw = len(names)
    ns = len(sems)

    def body(*refs):
        ins, land_refs = refs[:nw], refs[nw:2 * nw]
        sem_refs = refs[2 * nw:2 * nw + ns]
        _, _, c, _, sibling, _ = _place()
        for k, (src, dst) in enumerate(_swap_pairs(ins, land_refs, names, c)):
            cp = _remote(src, dst, sem_refs[2 * k], sem_refs[2 * k + 1], sibling)
            cp.wait_send()
            cp.wait_recv()

    res = pl.pallas_call(
        body, name=label,
        out_shape=tuple(pltpu.HBM(a.shape, a.dtype) for a in (*grads, *lands)),
        in_specs=[IN_HBM] * (2 * nw) + [SEM] * ns + [pl.BlockSpec(memory_space=pl.ANY)] * len(_several(after)),
        out_specs=tuple([IN_HBM] * (2 * nw)),
        input_output_aliases={i: i for i in range(2 * nw)},
        compiler_params=pltpu.CompilerParams(has_side_effects=DATAFLOW),
    )(*grads, *lands, *sems, *_several(after))
    return list(res[:nw]), list(res[nw:])


def _add_halves(name, grad, got, core):
    kind, (r, c) = BIG[name]
    half = _half_rows(name)
    if kind == "col":
        t = 128
        grid = (half // t,)
        g_spec = pl.BlockSpec((t, c), lambda i, cr: (cr[0] * (half // t) + i, 0))
        o_spec = pl.BlockSpec((t, c), lambda i, cr: (i, 0))
    else:
        t = half
        grid = (N_CHIPS,)
        g_spec = pl.BlockSpec((t, c), lambda i, cr: (2 * i + cr[0], 0))
        o_spec = pl.BlockSpec((t, c), lambda i, cr: (i, 0))

    def body(c_ref, g_ref, b_ref, o_ref):
        del c_ref
        o_ref[...] = (g_ref[...] + b_ref[...]).astype(o_ref.dtype)

    return pl.pallas_call(
        body, name="grad_add_" + name,
        grid_spec=pltpu.PrefetchScalarGridSpec(num_scalar_prefetch=1, grid=grid, in_specs=[g_spec, o_spec], out_specs=o_spec),
        out_shape=_compact_shape(name, BF16),
        compiler_params=pltpu.CompilerParams(dimension_semantics=("parallel",), vmem_limit_bytes=VMEM_LIMIT),
    )(core, grad, got)


def _piece(ref, name, chip):
    kind, _ = BIG[name]
    r, c = _shard_shape(name)
    if kind == "col":
        return ref.at[:, pl.ds(chip * c, c)]
    return ref.at[pl.ds(chip * (r // 2), r // 2), :]


def _landing_shape(name):
    r, c = _shard_shape(name)
    return (N_CHIPS - 1, r // 2, c)


def _exchange_start(parts, names, label):
    nw = len(names)
    ns = 2 * 3 * nw

    def body(*refs):
        ins, lands = refs[:nw], refs[nw:2 * nw]
        sems = refs[2 * nw:2 * nw + ns]
        token = refs[4 * nw + ns]
        _, _, c, _, _, others = _place()
        for w, name in enumerate(names):
            for t, (ox, oy) in enumerate(others):
                k = 2 * (3 * w + t)
                _remote(_piece(ins[w], name, 2 * ox + oy), lands[w].at[t], sems[k], sems[k + 1], (ox, oy, c)).start()
        token[...] = jnp.zeros_like(token)

    lands = [_keep_in_hbm(lax.empty(_landing_shape(n), BF16)) for n in names]
    res = pl.pallas_call(
        body, name=label,
        out_shape=(*[pltpu.SemaphoreType.DMA(())] * ns, *[pltpu.HBM(a.shape, a.dtype) for a in parts],
                   *[pltpu.HBM(a.shape, a.dtype) for a in lands], _sds((8, LANES), F32)),
        in_specs=[IN_HBM] * (2 * nw),
        out_specs=(*[SEM] * ns, *[IN_HBM] * (2 * nw), pl.BlockSpec(memory_space=pltpu.VMEM)),
        input_output_aliases={i: ns + i for i in range(2 * nw)},
        compiler_params=pltpu.CompilerParams(has_side_effects=DATAFLOW),
    )(*[_keep_in_hbm(a) for a in parts], *lands)
    return list(res[:ns]), list(res[ns:ns + nw]), list(res[ns + nw:ns + 2 * nw]), res[ns + 2 * nw]


def _exchange_wait(sems, parts, lands, names, after, label):
    nw = len(names)
    ns = len(sems)

    def body(*refs):
        ins, land_refs = refs[:nw], refs[nw:2 * nw]
        sem_refs = refs[2 * nw:2 * nw + ns]
        _, _, c, _, _, others = _place()
        for w, name in enumerate(names):
            for t, (ox, oy) in enumerate(others):
                k = 2 * (3 * w + t)
                cp = _remote(_piece(ins[w], name, 2 * ox + oy), land_refs[w].at[t], sem_refs[k], sem_refs[k + 1], (ox, oy, c))
                cp.wait_send()
                cp.wait_recv()

    res = pl.pallas_call(
        body, name=label,
        out_shape=tuple(pltpu.HBM(a.shape, a.dtype) for a in (*parts, *lands)),
        in_specs=[IN_HBM] * (2 * nw) + [SEM] * ns + [pl.BlockSpec(memory_space=pl.ANY)] * len(_several(after)),
        out_specs=tuple([IN_HBM] * (2 * nw)),
        input_output_aliases={i: i for i in range(2 * nw)},
        compiler_params=pltpu.CompilerParams(has_side_effects=DATAFLOW),
    )(*parts, *lands, *sems, *_several(after))
    return list(res[:nw]), list(res[nw:])


def _sum_chips(name, part, got, chip):
    kind, _ = BIG[name]
    _, r, c = got.shape
    t = _tile(r, (128, 352))
    if kind == "col":
        own = pl.BlockSpec((t, c), lambda i, ch: (i, ch[0]))
    else:
        own = pl.BlockSpec((t, c), lambda i, ch: (ch[0] * (r // t) + i, 0))

    def body(ch_ref, p_ref, g_ref, o_ref):
        del ch_ref
        acc = p_ref[...].astype(F32)
        for j in range(N_CHIPS - 1):
            acc = acc + g_ref[j].astype(F32)
        o_ref[...] = acc

    return _pallas(
        body, name="grad_sum_" + name, prefetch=1, grid=(r // t,),
        in_specs=[own, pl.BlockSpec((N_CHIPS - 1, t, c), lambda i, ch: (0, i, 0))],
        out_specs=pl.BlockSpec((t, c), lambda i, ch: (i, 0)),
        out_shape=_sds((r, c), F32), semantics=("parallel",),
    )(chip, part, got)


def _send_halves(sums, label, after):
    nw = len(sums)

    def body(*refs):
        ins, outs = refs[:nw], refs[nw + 1:2 * nw + 1]
        send_sems, recv_sems = refs[2 * nw + 1:]
        _, _, _, _, sibling, _ = _place()
        copies = [_remote(ins[w], outs[w], send_sems.at[w], recv_sems.at[w], sibling) for w in range(nw)]
        for cp in copies:
            cp.start()
        for cp in copies:
            cp.wait()

    return pl.pallas_call(
        body, name=label,
        out_shape=[_sds(a.shape, a.dtype) for a in sums],
        in_specs=[HBM] * (nw + 1), out_specs=[HBM] * nw,
        scratch_shapes=[pltpu.SemaphoreType.DMA((nw,)), pltpu.SemaphoreType.DMA((nw,))],
    )(*sums, after)


EARLY_GRADS = ("w_up", "w_down")
LAST_GRADS = ("w_in", "w_out")


def _reduce_finish(started, names, after, chip, tag):
    sems, parts, lands, _ = started
    parts, lands = _exchange_wait(sems, parts, lands, names, after, "grad_exchange_wait_" + tag)
    return [_sum_chips(n, parts[i], lands[i], chip) for i, n in enumerate(names)]


HI = lax.Precision.HIGHEST
MOD_COLS = 6 * D // N_CHIPS
COND_ROWS = 16


def _silu(v):
    return v * _sigmoid(v)


GATHER_ROWS = 8
FFW_COLS = 2 * DFF // N_CHIPS
CONV_COLS = DC // N_CHIPS
TAPS_PER_ROW = FFW_COLS // CONV_COLS
assert 4 + -(-CW // TAPS_PER_ROW) <= GATHER_ROWS


def _conv_tap_place(k):
    return 4 + k // TAPS_PER_ROW, (k % TAPS_PER_ROW) * CONV_COLS


def _pack_cond(c, ffn_w, conv_w):
    def body(c_ref, f_ref, w_ref, o_ref):
        o_ref[...] = jnp.zeros_like(o_ref)
        o_ref[0:1, 0:D] = c_ref[...]
        o_ref[1:4, :] = f_ref[...]
        for k in range(CW):
            row, lane = _conv_tap_place(k)
            o_ref[row:row + 1, lane:lane + CONV_COLS] = w_ref[k:k + 1, :]

    return _pallas(body, name="pack_cond", out_shape=_sds((GATHER_ROWS, FFW_COLS), F32))(c, ffn_w, conv_w)


def _unpack_cond(got, c_ctx):
    def body(g_ref, c_ref, cond_ref, f_ref, w_ref):
        cond_ref[...] = jnp.zeros_like(cond_ref)
        for d in range(8):
            cond_ref[d:d + 1, :] = g_ref[d * GATHER_ROWS:d * GATHER_ROWS + 1, 0:D]
        cond_ref[8:9, :] = c_ref[...]
        for j in range(N_CHIPS):
            r0 = 2 * j * GATHER_ROWS
            f_ref[:, j * FFW_COLS:(j + 1) * FFW_COLS] = g_ref[r0 + 1:r0 + 4, :]
            for k in range(CW):
                row, lane = _conv_tap_place(k)
                w_ref[k:k + 1, j * CONV_COLS:(j + 1) * CONV_COLS] = g_ref[r0 + row:r0 + row + 1, lane:lane + CONV_COLS]

    return _pallas(body, name="unpack_cond",
                   out_shape=[_sds((COND_ROWS, D), F32), _sds((3, 2 * DFF), F32), _sds((CW, DC), F32)])(got, c_ctx)


def _chip_cols(rows, width):
    return pl.BlockSpec((rows, width), lambda i, ch: (0, ch[0]))


def _whole(shape):
    return pl.BlockSpec(shape, lambda i, ch: (0,) * len(shape))


def _mod_shard(cond, w_mod, b_mod, chip):
    def body(ch_ref, c_ref, w_ref, b_ref, o_ref):
        del ch_ref
        o_ref[...] = jnp.dot(_silu(c_ref[...]), w_ref[...], preferred_element_type=F32, precision=HI) + b_ref[...]

    return _pallas(body, name="mod_fwd", prefetch=1, grid=(1,),
                   in_specs=[_whole((COND_ROWS, D)), _whole((D, MOD_COLS)), _chip_cols(1, MOD_COLS)],
                   out_specs=_whole((COND_ROWS, MOD_COLS)),
                   out_shape=_sds((COND_ROWS, MOD_COLS), F32))(chip, cond, w_mod, b_mod)


def _unpack_mod(mods, dev):
    def body(dev_ref, m_ref, me_ref, c_ref):
        rowi = lax.broadcasted_iota(jnp.int32, (COND_ROWS, MOD_COLS), 0)
        core = dev_ref[0] % 2
        mine, ctx = [], []
        for j in range(N_CHIPS):
            blk = m_ref[pl.ds(pl.multiple_of((2 * j + core) * COND_ROWS, COND_ROWS), COND_ROWS), :]
            mine.append(jnp.sum(jnp.where(rowi == dev_ref[0], blk, 0.0), axis=0, keepdims=True))
            ctx.append(blk[8:9, :])
        mine = jnp.concatenate(mine, axis=1)
        ctx = jnp.concatenate(ctx, axis=1)
        for k in range(6):
            me_ref[k:k + 1, :] = mine[:, k * D:(k + 1) * D]
        for k in range(2):
            c_ref[k:k + 1, :] = ctx[:, k * D:(k + 1) * D]

    return _pallas(body, name="unpack_mod", prefetch=1, grid=(1,),
                   in_specs=[_whole(mods.shape)], out_specs=[_whole((6, D)), _whole((2, D))],
                   out_shape=[_sds((6, D), F32), _sds((2, D), F32)])(dev, mods)


MOD_TILE = 512


def _mod_weight_update(cond, dmod_all, w, m, v, chip):
    nt = MOD_COLS // MOD_TILE

    def body(ch_ref, c_ref, d_ref, w_ref, m_ref, v_ref, g_ref, dl_ref, nm_ref, nv_ref):
        del ch_ref
        g = lax.dot_general(_silu(c_ref[...]), d_ref[...], _TN, preferred_element_type=F32, precision=HI)
        g_ref[...] = g
        dl_ref[...], nm_ref[...], nv_ref[...] = _adam_math(w_ref[...], g, m_ref[...], v_ref[...])

    blk = pl.BlockSpec((D, MOD_TILE), lambda j, ch: (0, j))
    return _pallas(body, name="mod_weight_update", prefetch=1, grid=(nt,),
                   in_specs=[_whole((COND_ROWS, D)), pl.BlockSpec((COND_ROWS, MOD_TILE), lambda j, ch: (0, ch[0] * nt + j)),
                             blk, blk, blk],
                   out_specs=[blk] * 4, out_shape=[_sds((D, MOD_COLS), F32)] * 4,
                   semantics=("parallel",))(chip, cond, dmod_all, w, m, v)


def _cond_grad_partial(dmod_all, w_mod, chip):
    def body(ch_ref, d_ref, w_ref, o_ref):
        del ch_ref
        o_ref[...] = lax.dot_general(d_ref[...], w_ref[...], (((1,), (1,)), ((), ())), preferred_element_type=F32, precision=HI)

    return _pallas(body, name="cond_grad_partial", prefetch=1, grid=(1,),
                   in_specs=[pl.BlockSpec((8, MOD_COLS), lambda i, ch: (1, ch[0])), _whole((D, MOD_COLS))],
                   out_specs=_whole((8, D)), out_shape=_sds((8, D), F32))(chip, dmod_all, w_mod)


def _adam_math(w, g, m, v):
    nm = ADAM_B1 * m + (1.0 - ADAM_B1) * g
    nv = ADAM_B2 * v + (1.0 - ADAM_B2) * (g * g)
    c1 = 1.0 - ADAM_B1 ** ADAM_STEP
    c2 = 1.0 - ADAM_B2 ** ADAM_STEP
    return -ADAM_LR * ((nm / c1) / (jnp.sqrt(nv / c2) + ADAM_EPS) + ADAM_WD * w), nm, nv


def _cond_update(parts, c_ctx, m, v):
    def body(p_ref, c_ref, m_ref, v_ref, g_ref, d_ref, nm_ref, nv_ref):
        tot = p_ref[0:1, :]
        for j in range(1, N_CHIPS):
            tot = tot + p_ref[16 * j:16 * j + 1, :]
        cv = c_ref[...]
        sg = _sigmoid(cv)
        g = tot * (sg * (1.0 + cv * (1.0 - sg)))
        g_ref[...] = g
        d_ref[...], nm_ref[...], nv_ref[...] = _adam_math(cv, g, m_ref[...], v_ref[...])

    return _pallas(body, name="cond_update", out_shape=[_sds((1, D), F32)] * 4)(parts, c_ctx, m, v)


def _adamw_cols(w, g_all, m, v, chip, name):
    r, c = w.shape

    def body(ch_ref, w_ref, g_ref, m_ref, v_ref, go_ref, d_ref, nm_ref, nv_ref):
        del ch_ref
        g = g_ref[...]
        go_ref[...] = g
        d_ref[...], nm_ref[...], nv_ref[...] = _adam_math(w_ref[...], g, m_ref[...], v_ref[...])

    return _pallas(body, name=name, prefetch=1, grid=(1,),
                   in_specs=[_whole((r, c)), _chip_cols(r, c), _whole((r, c)), _whole((r, c))],
                   out_specs=[_whole((r, c))] * 4, out_shape=[_sds((r, c), F32)] * 4)(chip, w, g_all, m, v)


def _adamw_halves(name, w, own, other, m, v, core, after):
    r, c = w.shape
    half = r // 2
    t = _tile(half, (128, 352))
    nh = half // t

    def pick(mine):
        def index(i, cr):
            first = cr[0] if mine else 1 - cr[0]
            return (jnp.clip(i - first * nh, 0, nh - 1), 0)
        return pl.BlockSpec((t, c), index)

    def body(c_ref, w_ref, own_ref, oth_ref, m_ref, v_ref, after_ref, g_ref, d_ref, nm_ref, nv_ref):
        del after_ref
        g = jnp.where(pl.program_id(0) // nh == c_ref[0], own_ref[...], oth_ref[...])
        g_ref[...] = g
        d_ref[...], nm_ref[...], nv_ref[...] = _adam_math(w_ref[...], g, m_ref[...], v_ref[...])

    blk = pl.BlockSpec((t, c), lambda i, cr: (i, 0))
    return _pallas(body, name="adamw_" + name, prefetch=1, grid=(2 * nh,),
                   in_specs=[blk, pick(True), pick(False), blk, blk, pl.BlockSpec(memory_space=pl.ANY)], out_specs=[blk] * 4,
                   out_shape=[_sds((r, c), F32)] * 4, semantics=("parallel",))(core, w, own, other, m, v, after)


WEIGHTS = ("c_ctx", "w_mod", "b_mod", "g_norm1", "w_in", "rpb", "conv_w", "conv_b", "ln_g", "ln_b", "w_out", "g_norm2",
           "w_up", "ffn_conv_w", "ffn_conv_b", "w_down", "g_final")
PACK = (("dmod", 6 * D), ("dmod_c", 2 * D), ("g_norm1", D), ("g_norm1_ctx", D), ("g_norm2", D), ("g_final", D),
        ("conv_b", DC), ("ln_g", DC), ("ln_b", DC), ("ffn_conv_b", 2 * DFF), ("ffn_conv_w", 3 * 2 * DFF),
        ("conv_w", CW * DC), ("rpb_rev", NH * 16 * LANES), ("loss", LANES))
PACK_OFF = {}
_o = 0
for _n, _w in PACK:
    PACK_OFF[_n] = (_o, _w)
    _o += _w
PACK_N = -(-_o // (8 * LANES)) * (8 * LANES)
VECTORS = {"b_mod": (6 * D, ("dmod", "dmod_c")), "g_norm1": (D, ("g_norm1", "g_norm1_ctx")), "conv_b": (DC, ("conv_b",)),
           "ln_g": (DC, ("ln_g",)), "ln_b": (DC, ("ln_b",)), "g_norm2": (D, ("g_norm2",)),
           "ffn_conv_b": (2 * DFF, ("ffn_conv_b",)), "g_final": (D, ("g_final",))}
RPB_COLS = 4 * NA_ROWS - 1


def _pack_small(parts, after):
    arrs, places = [], []
    for name, _ in PACK:
        off, width = PACK_OFF[name]
        group = parts[name]
        rows = group[0].shape[0]
        row_w = sum(a.shape[1] for a in group)
        assert rows * row_w == width, (name, rows, row_w, width)
        col = 0
        for a in group:
            arrs.append(a)
            places.append([off + k * row_w + col for k in range(rows)])
            col += a.shape[1]

    def body(*refs):
        o_ref = refs[-1]
        o_ref[:, _o:PACK_N] = jnp.zeros((1, PACK_N - _o), F32)
        for ref, offs in zip(refs, places):
            n = ref.shape[1]
            for k, off in enumerate(offs):
                o_ref[:, off:off + n] = ref[k:k + 1, :]

    vmem = pl.BlockSpec(memory_space=pltpu.VMEM)
    return _pallas(body, name="pack_small_grads", out_shape=_sds((1, PACK_N), F32),
                   in_specs=[vmem] * len(arrs) + [pl.BlockSpec(memory_space=pl.ANY)] * len(_several(after)),
                   out_specs=vmem)(*arrs, *_several(after))


def _small_update(packs, w, m, v):
    names = list(VECTORS)

    def body(*refs):
        it = iter(refs)
        p_ref = next(it)
        wmv = {n: (next(it), next(it), next(it)) for n in names}
        outs = {n: (next(it), next(it), next(it), next(it)) for n in names}
        dmod_ref, cw_ref, fw_ref, rpb_ref, loss_ref = next(it), next(it), next(it), next(it), next(it)

        def total(name):
            off, width = PACK_OFF[name]
            acc = p_ref[0:1, off:off + width]
            for d in range(1, 8):
                acc = acc + p_ref[d:d + 1, off:off + width]
            return acc

        for n in names:
            width, segs = VECTORS[n]
            g = total(segs[0])
            if len(segs) > 1:
                extra = total(segs[1])
                ew = extra.shape[1]
                g = g + extra if ew == width else jnp.concatenate([g[:, :ew] + extra, g[:, ew:]], axis=1)
            w_ref, m_ref, v_ref = wmv[n]
            g_ref, d_ref, nm_ref, nv_ref = outs[n]
            g_ref[...] = g
            d_ref[...], nm_ref[...], nv_ref[...] = _adam_math(w_ref[...], g, m_ref[...], v_ref[...])

        o_dmod = PACK_OFF["dmod"][0]
        dmod_ref[...] = jnp.zeros_like(dmod_ref)
        dmod_ref[0:8, :] = p_ref[:, o_dmod:o_dmod + 6 * D]
        dmod_ref[8:9, 0:2 * D] = total("dmod_c")
        for ref, name, rows in ((cw_ref, "conv_w", CW), (fw_ref, "ffn_conv_w", 3), (rpb_ref, "rpb_rev", NH * 16)):
            flat = total(name)
            n = ref.shape[1]
            for k in range(rows):
                ref[k:k + 1, :] = flat[:, k * n:(k + 1) * n]
        loss_ref[...] = total("loss")

    ins = [packs] + [a[n] for n in names for a in (w, m, v)]
    out_shape = [_sds((1, VECTORS[n][0]), F32) for n in names for _ in range(4)]
    out_shape += [_sds((COND_ROWS, 6 * D), F32), _sds((CW, DC), F32), _sds((3, 2 * DFF), F32), _sds((NH * 16, LANES), F32),
                  _sds((1, LANES), F32)]
    res = _pallas(body, name="small_update", out_shape=out_shape)(*ins)
    per = {n: tuple(res[4 * i:4 * i + 4]) for i, n in enumerate(names)}
    return (per, *res[4 * len(names):])


def _rpb_update(rev, w, m, v):
    def body(r_ref, w_ref, m_ref, v_ref, g_ref, d_ref, nm_ref, nv_ref):
        li = lax.broadcasted_iota(jnp.int32, (LANES, LANES), 0)
        co = lax.broadcasted_iota(jnp.int32, (LANES, LANES), 1)
        lane_of_co0 = GW - 1 + RPB_COLS // 2
        unflip = jnp.where((li == lane_of_co0 - co) & (co < RPB_COLS), 1.0, 0.0).astype(F32)
        g_all = jnp.dot(r_ref[...], unflip, preferred_element_type=F32, precision=HI)
        nr = 2 * NA_ROWS - 1
        for h in range(NH):
            g = g_all[h * 16:h * 16 + nr, 0:RPB_COLS]
            g_ref[0, h] = g
            d_ref[0, h], nm_ref[0, h], nv_ref[0, h] = _adam_math(w_ref[0, h], g, m_ref[0, h], v_ref[0, h])

    return _pallas(body, name="rpb_update", out_shape=[_sds(w.shape, F32)] * 4)(rev, w, m, v)


def kernel(x, c, ctx, c_ctx, w_mod, b_mod, g_norm1, w_in, rpb, conv_w, conv_b, ln_g, ln_b, w_out, g_norm2, w_up, ffn_conv_w, ffn_conv_b, w_down, g_final, loss_target, m_c_ctx, m_w_mod, m_b_mod, m_g_norm1, m_w_in, m_rpb, m_conv_w, m_conv_b, m_ln_g, m_ln_b, m_w_out, m_g_norm2, m_w_up, m_ffn_conv_w, m_ffn_conv_b, m_w_down, m_g_final, v_c_ctx, v_w_mod, v_b_mod, v_g_norm1, v_w_in, v_rpb, v_conv_w, v_conv_b, v_ln_g, v_ln_b, v_w_out, v_g_norm2, v_w_up, v_ffn_conv_w, v_ffn_conv_b, v_w_down, v_g_final):
    w = dict(c_ctx=c_ctx, w_mod=w_mod, b_mod=b_mod, g_norm1=g_norm1, w_in=w_in, rpb=rpb, conv_w=conv_w, conv_b=conv_b,
             ln_g=ln_g, ln_b=ln_b, w_out=w_out, g_norm2=g_norm2, w_up=w_up, ffn_conv_w=ffn_conv_w, ffn_conv_b=ffn_conv_b,
             w_down=w_down, g_final=g_final)
    mom = dict(c_ctx=m_c_ctx, w_mod=m_w_mod, b_mod=m_b_mod, g_norm1=m_g_norm1, w_in=m_w_in, rpb=m_rpb, conv_w=m_conv_w,
               conv_b=m_conv_b, ln_g=m_ln_g, ln_b=m_ln_b, w_out=m_w_out, g_norm2=m_g_norm2, w_up=m_w_up,
               ffn_conv_w=m_ffn_conv_w, ffn_conv_b=m_ffn_conv_b, w_down=m_w_down, g_final=m_g_final)
    var = dict(c_ctx=v_c_ctx, w_mod=v_w_mod, b_mod=v_b_mod, g_norm1=v_g_norm1, w_in=v_w_in, rpb=v_rpb, conv_w=v_conv_w,
               conv_b=v_conv_b, ln_g=v_ln_g, ln_b=v_ln_b, w_out=v_w_out, g_norm2=v_g_norm2, w_up=v_w_up,
               ffn_conv_w=v_ffn_conv_w, ffn_conv_b=v_ffn_conv_b, w_down=v_w_down, g_final=v_g_final)
    xi, yi, ci = lax.axis_index("x"), lax.axis_index("y"), lax.axis_index("c")
    dev = (4 * xi + 2 * yi + ci).astype(jnp.int32).reshape(1)
    chip = (2 * xi + yi).astype(jnp.int32).reshape(1)
    core = ci.astype(jnp.int32).reshape(1)
    c_ctx2 = c_ctx.reshape(1, D)
    g_final2 = g_final.reshape(1, D)
    mom["g_final"], var["g_final"] = m_g_final.reshape(1, D), v_g_final.reshape(1, D)

    sharing_cond = _share_start(_pack_cond(c, ffn_conv_w[0], conv_w[0]), "cond", after=[])
    shards = {n: _cast_into_whole(n, w[n][0], chip) for n in BIG_NAMES}
    cond, ffn_w_all, conv_w_all = _unpack_cond(_share_wait(sharing_cond, list(shards.values()), "cond"), c_ctx2)

    sharing_mod = _share_start(_mod_shard(cond, w_mod[0], b_mod, chip), "mod", after=[], flips=OTHER_CHIPS)

    sems_in, first, token_in = _gather_start([shards["w_in"]], ("w_in",), sharing_mod[2], "w_in")
    sems, late, token = _gather_start([shards[n] for n in LATE_NAMES], LATE_NAMES, token_in, "late")
    mod_me, mod_c = _unpack_mod(_share_wait(sharing_mod, [], "mod"), dev)
    mod_me = mod_me + token[0:1, 0:1]

    def w_in_all(after):
        arrived = _gather_wait(sems_in, first, ("w_in",), after, "w_in")
        return _forward_halves(list(arrived), ("w_in",), "w_in")[0]

    def late_weights(after):
        arrived = list(_gather_wait(sems, late, LATE_NAMES, after, "late"))
        (w_out_all,) = _forward_halves(arrived[:1], LATE_NAMES[:1], "w_out")
        fsems, passing, _ = _forward_start(arrived[1:], LATE_NAMES[1:], "ffn", after=w_out_all)
        return w_out_all, lambda after2: _forward_wait(fsems, passing, LATE_NAMES[1:], after2, "ffn")

    rpb_rev = jnp.pad(rpb[0][:, :, ::-1], ((0, 0), (0, 1), (48, LANES - 48 - RPB_COLS))).reshape(NH * 16, LANES)
    vec = dict(g_norm1=g_norm1, g_norm2=g_norm2, g_final=g_final2, conv_w=conv_w_all, conv_b=conv_b, ln_g=ln_g, ln_b=ln_b,
               ffn_conv_w=ffn_w_all, ffn_conv_b=ffn_conv_b)
    started = []

    def begin_early(d_up, d_down):
        started.append(_swap_start([d_up, d_down], EARLY_GRADS, "grad_swap_start_early"))

    def carry_on_early(after):
        sems_, grads_, lands_, _ = started.pop()
        grads_, lands_ = _swap_wait(sems_, grads_, lands_, EARLY_GRADS, after, "grad_swap_wait_early")
        parts_ = [_add_halves(n, grads_[i], lands_[i], core) for i, n in enumerate(EARLY_GRADS)]
        started.append(_exchange_start(parts_, EARLY_GRADS, "grad_exchange_start_early"))
        return started[0][3][0:1, 0:1]

    loss_p, grad_x, d_in, d_out, d_up, d_down, small = _local_step(
        x[0], ctx[0], loss_target[0], mod_me, mod_c, vec, w_in_all, late_weights, rpb_rev, (begin_early, carry_on_early))

    out = {}
    sems_, grads_, lands_, _ = _swap_start([d_in, d_out], LAST_GRADS, "grad_swap_start_last")
    early_own = _reduce_finish(started[0], EARLY_GRADS, grad_x, chip, "early")
    parts = dict(dmod=small["dmod"], dmod_c=small["dmod_c"], g_norm1=[small["g_norm1"][0]], g_norm1_ctx=[small["g_norm1"][1]],
                 g_norm2=[small["g_norm2"]], g_final=[small["g_final"]], conv_b=[small["conv_b"]], ln_g=[small["ln_g"]],
                 ln_b=[small["ln_b"]], ffn_conv_b=small["ffn_conv_b"], ffn_conv_w=small["ffn_conv_w"],
                 conv_w=[small["conv_w"]], rpb_rev=[small["rpb_rev"]], loss=[loss_p])
    pack = _pack_small(parts, after=early_own).reshape(8, PACK_N // 8)
    sharing = _share_start(pack, "small_grads", after=[])
    grads_, lands_ = _swap_wait(sems_, grads_, lands_, LAST_GRADS, sharing[2], "grad_swap_wait_last")
    parts_ = [_add_halves(n, grads_[i], lands_[i], core) for i, n in enumerate(LAST_GRADS)]
    last_started = _exchange_start(parts_, LAST_GRADS, "grad_exchange_start_last")
    early_other = _send_halves(early_own, "grad_send_early", after=last_started[3])
    for i, n in enumerate(EARLY_GRADS):
        out[n] = _adamw_halves(n, w[n][0], early_own[i], early_other[i], mom[n][0], var[n][0], core, early_other[i])

    packs = _share_wait(sharing, [out[n][1] for n in EARLY_GRADS], "small_grads").reshape(8, PACK_N)
    w2 = dict(w, g_final=g_final2)
    per, dmod_all, g_conv_w_all, g_ffn_w_all, g_rpb_rev, loss_row = _small_update(packs, w2, mom, var)
    out.update(per)
    out["w_mod"] = _mod_weight_update(cond, dmod_all, w_mod[0], m_w_mod[0], v_w_mod[0], chip)

    sharing_c = _share_start(_cond_grad_partial(dmod_all, w_mod[0], chip), "cond_grad", after=out["w_mod"][1])
    last_own = _reduce_finish(last_started, LAST_GRADS, sharing_c[2], chip, "last")
    last_other = _send_halves(last_own, "grad_send_last", after=last_own[0])
    for i, n in enumerate(LAST_GRADS):
        out[n] = _adamw_halves(n, w[n][0], last_own[i], last_other[i], mom[n][0], var[n][0], core, last_other[i])
    out["c_ctx"] = _cond_update(_share_wait(sharing_c, [out[n][1] for n in LAST_GRADS], "cond_grad"),
                                c_ctx2, m_c_ctx.reshape(1, D), v_c_ctx.reshape(1, D))
    out["conv_w"] = _adamw_cols(conv_w[0], g_conv_w_all, m_conv_w[0], v_conv_w[0], chip, "adamw_conv_w")
    out["ffn_conv_w"] = _adamw_cols(ffn_conv_w[0], g_ffn_w_all, m_ffn_conv_w[0], v_ffn_conv_w[0], chip, "adamw_ffn_conv_w")
    out["rpb"] = _rpb_update(g_rpb_rev, rpb, m_rpb, v_rpb)

    res = [[out[n][k].reshape(w[n].shape) for n in WEIGHTS] for k in range(4)]
    return (loss_row[0, 0], grad_x[None], *res[0], *res[1], *res[2], *res[3])
```

```python
import jax
import jax.numpy as jnp
from jax import lax
from jax.experimental import pallas as pl
from jax.experimental.pallas import tpu as pltpu

F32 = jnp.float32
BF16 = jnp.bfloat16
MXU_DTYPE = jnp.bfloat16

D = 1024
CTX = 256
GW = 64
DA = 512
NH = 8
HD = 64
DC = 512
CW = 31
DFF = 2816
NIN = 3 * DA + 2 * DC
EPS = 1e-6
SCALE = HD ** -0.5
NEG = -1e30
NA_ROWS = 8
PAIR_ROWS = NA_ROWS + 1
TAB_BLOCKS = 17
LANES = 128
VMEM_LIMIT = 56 * 1024 * 1024

ADAM_LR = 0.001
ADAM_B1 = 0.9
ADAM_B2 = 0.999
ADAM_EPS = 1e-08
ADAM_WD = 0.01
ADAM_STEP = 10

MESH = pl.DeviceIdType.MESH


def _pallas(body, *, name, semantics=None, vmem=VMEM_LIMIT, prefetch=0, **kw):
    params = dict(vmem_limit_bytes=vmem)
    if semantics is not None:
        params["dimension_semantics"] = semantics
    if prefetch:
        kw["grid_spec"] = pltpu.PrefetchScalarGridSpec(
            num_scalar_prefetch=prefetch, grid=kw.pop("grid"), in_specs=kw.pop("in_specs"), out_specs=kw.pop("out_specs"),
            scratch_shapes=kw.pop("scratch_shapes", ()))
    return pl.pallas_call(body, name=name, compiler_params=pltpu.CompilerParams(**params), **kw)


def _sds(shape, dtype):
    return jax.ShapeDtypeStruct(shape, dtype)


def _vec_spec(n):
    return pl.BlockSpec((1, n), lambda *_: (0, 0))


def _colsum8(x):
    t, n = x.shape
    return jnp.sum(x.reshape(t // 8, 8, n), axis=0)


def _sigmoid(x):
    return 0.5 * jnp.tanh(0.5 * x) + 0.5


def _mm(a, b, *, mode, m, n, k, tm, tn, tk, out_dtype, name, a_off=(0, 0), b_off=(0, 0),
        out_total=None, o_off=(0, 0), into=None):
    a_list = list(a) if isinstance(a, (list, tuple)) else [a]
    b_list = list(b) if isinstance(b, (list, tuple)) else [b]
    assert m % tm == 0 and n % tn == 0 and k % tk == 0, (name, m, n, k, tm, tn, tk)
    gi, gj, nk = m // tm, n // tn, k // tk
    dims = {"nn": (((1,), (0,)), ((), ())), "nt": (((1,), (1,)), ((), ())), "tn": (((0,), (0,)), ((), ()))}[mode]

    if len(a_list) > 1:
        assert mode != "tn" and nk == 1 and sum(x.shape[1] for x in a_list) == k
        a_specs = [pl.BlockSpec((tm, x.shape[1]), lambda i, j, kk: (i, 0)) for x in a_list]
    elif mode == "tn":
        a_specs = [pl.BlockSpec((tk, tm), lambda i, j, kk: (kk + a_off[0], i + a_off[1]))]
    else:
        a_specs = [pl.BlockSpec((tm, tk), lambda i, j, kk: (i + a_off[0], kk + a_off[1]))]
    if len(b_list) > 1:
        assert mode == "tn" and gj == 1 and sum(x.shape[1] for x in b_list) == n
        b_specs = [pl.BlockSpec((tk, x.shape[1]), lambda i, j, kk: (kk, 0)) for x in b_list]
    elif mode == "nt":
        b_specs = [pl.BlockSpec((tn, tk), lambda i, j, kk: (j + b_off[0], kk + b_off[1]))]
    else:
        b_specs = [pl.BlockSpec((tk, tn), lambda i, j, kk: (kk + b_off[0], j + b_off[1]))]

    na, nb = len(a_list), len(b_list)
    in_place = nk > 1 and out_dtype == F32
    n_in = na + nb + (into is not None)

    def body(*refs):
        a_refs, b_refs, o_ref = refs[:na], refs[na:na + nb], refs[n_in]
        acc = o_ref if in_place else (refs[n_in + 1] if nk > 1 else None)
        kk = pl.program_id(2)

        def whole(piece_refs):
            vals = [r[...].astype(MXU_DTYPE) for r in piece_refs]
            return vals[0] if len(vals) == 1 else jnp.concatenate(vals, axis=1)

        p = lax.dot_general(whole(a_refs), whole(b_refs), dims, preferred_element_type=F32)
        if nk == 1:
            o_ref[...] = p.astype(out_dtype)
            return

        @pl.when(kk == 0)
        def _():
            acc[...] = p

        @pl.when(kk > 0)
        def _():
            acc[...] += p

        if not in_place:
            @pl.when(kk == nk - 1)
            def _():
                o_ref[...] = acc[...].astype(out_dtype)

    ins = [*a_list, *b_list]
    in_specs = a_specs + b_specs
    extra = {}
    if into is not None:
        extra["input_output_aliases"] = {len(ins): 0}
        ins.append(into)
        in_specs.append(pl.BlockSpec(memory_space=pl.ANY))
    return _pallas(
        body, name=name, grid=(gi, gj, nk), in_specs=in_specs,
        out_specs=pl.BlockSpec((tm, tn), lambda i, j, kk: (i + o_off[0], j + o_off[1])),
        out_shape=_sds(out_total or (m, n), out_dtype),
        scratch_shapes=[pltpu.VMEM((tm, tn), F32)] if nk > 1 and not in_place else [],
        semantics=("parallel", "parallel", "arbitrary"), **extra,
    )(*ins)


ROW_TILE = 256


def _row_tile(s, most=2):
    for k in (4, 2):
        if k <= most and s % (k * ROW_TILE) == 0:
            return k * ROW_TILE
    return ROW_TILE


def _rmsmod_fwd(x, ctx, g, sc, sh, csc, csh):
    s = x.shape[0]
    nt = s // ROW_TILE
    assert ctx.shape[0] == ROW_TILE

    def body(x_ref, c_ref, g_ref, sc_ref, sh_ref, csc_ref, csh_ref, o_ref):
        is_ctx = pl.program_id(0) == nt
        xv = jnp.where(is_ctx, c_ref[...], x_ref[...])
        scv = jnp.where(is_ctx, csc_ref[...], sc_ref[...])
        shv = jnp.where(is_ctx, csh_ref[...], sh_ref[...])
        r = lax.rsqrt(jnp.mean(xv * xv, axis=-1, keepdims=True) + EPS)
        y = xv * r * g_ref[...]
        o_ref[...] = (y * (1.0 + scv) + shv).astype(o_ref.dtype)

    return _pallas(
        body, name="rmsmod1_fwd", grid=(nt + 1,),
        in_specs=[pl.BlockSpec((ROW_TILE, D), lambda i: (jnp.minimum(i, nt - 1), 0)),
                  pl.BlockSpec((ROW_TILE, D), lambda i: (0, 0))] + [_vec_spec(D)] * 5,
        out_specs=pl.BlockSpec((ROW_TILE, D), lambda i: (i, 0)),
        out_shape=_sds((s + CTX, D), MXU_DTYPE),
        semantics=("arbitrary",),
    )(x, ctx, g, sc, sh, csc, csh)


def _resid_rmsmod_fwd(x, y, gt, g, sc, sh):
    s = x.shape[0]

    def body(x_ref, y_ref, gt_ref, g_ref, sc_ref, sh_ref, x1_ref, h_ref):
        x1 = x_ref[...] + gt_ref[...] * y_ref[...]
        x1_ref[...] = x1
        r = lax.rsqrt(jnp.mean(x1 * x1, axis=-1, keepdims=True) + EPS)
        h_ref[...] = ((x1 * r * g_ref[...]) * (1.0 + sc_ref[...]) + sh_ref[...]).astype(h_ref.dtype)

    t = _row_tile(s)
    row = pl.BlockSpec((t, D), lambda i: (i, 0))
    return _pallas(
        body, name="resid_rmsmod2_fwd", grid=(s // t,),
        in_specs=[row, row] + [_vec_spec(D)] * 4,
        out_specs=[row, row],
        out_shape=[_sds((s, D), F32), _sds((s, D), MXU_DTYPE)],
        semantics=("parallel",),
    )(x, y, gt, g, sc, sh)


def _final_fwd_bwd(x1, z, gt2, gf, tgt):
    s = x1.shape[0]
    tile = _row_tile(s)
    nt = s // tile

    def body(x1_ref, z_ref, gt_ref, gf_ref, t_ref, dx2_ref, dz_ref, loss_ref, dgt_ref, dgf_ref, a_loss, a_gt, a_gf):
        i = pl.program_id(0)

        @pl.when(i == 0)
        def _():
            a_loss[...] = jnp.zeros_like(a_loss)
            a_gt[...] = jnp.zeros_like(a_gt)
            a_gf[...] = jnp.zeros_like(a_gf)

        zv = z_ref[...]
        gt = gt_ref[...]
        gf_ = gf_ref[...]
        x2 = x1_ref[...] + gt * zv
        r = lax.rsqrt(jnp.mean(x2 * x2, axis=-1, keepdims=True) + EPS)
        xn = x2 * r
        e = xn * gf_ - t_ref[...]
        a_loss[...] += _colsum8(e * e)
        dyo = e * (1.0 / D)
        a_gf[...] += _colsum8(dyo * xn)
        gdy = gf_ * dyo
        dx2 = r * gdy - xn * (r * r) * jnp.mean(x2 * gdy, axis=-1, keepdims=True)
        dx2_ref[...] = dx2
        dz_ref[...] = (gt * dx2).astype(dz_ref.dtype)
        a_gt[...] += _colsum8(dx2 * zv)

        @pl.when(i == nt - 1)
        def _():
            tot = jnp.sum(jnp.sum(a_loss[...], axis=0, keepdims=True), axis=1, keepdims=True) * (0.5 / D)
            loss_ref[...] = jnp.broadcast_to(tot, loss_ref.shape)
            dgt_ref[...] = jnp.sum(a_gt[...], axis=0, keepdims=True)
            dgf_ref[...] = jnp.sum(a_gf[...], axis=0, keepdims=True)

    row = pl.BlockSpec((tile, D), lambda i: (i, 0))
    return _pallas(
        body, name="final_norm_loss", grid=(nt,),
        in_specs=[row, row, _vec_spec(D), _vec_spec(D), row],
        out_specs=[row, row, _vec_spec(LANES), _vec_spec(D), _vec_spec(D)],
        out_shape=[_sds((s, D), F32), _sds((s, D), MXU_DTYPE), _sds((1, LANES), F32), _sds((1, D), F32), _sds((1, D), F32)],
        scratch_shapes=[pltpu.VMEM((8, D), F32)] * 3,
        semantics=("arbitrary",),
    )(x1, z, gt2, gf, tgt)


def _rmsmod_bwd(xin, dh, g, sc, *, name, dh_row0=0, add=None, resid=None):
    s = xin.shape[0]
    tile = _row_tile(s)
    nt = s // tile
    want_dx = add is not None
    assert resid is None or want_dx

    def body(*refs):
        it = iter(refs)
        x_ref, dh_ref, g_ref, sc_ref = next(it), next(it), next(it), next(it)
        add_ref = next(it) if want_dx else None
        gt_ref, y_ref = (next(it), next(it)) if resid is not None else (None, None)
        dsh_ref, dsc_ref, dg_ref = next(it), next(it), next(it)
        dx_ref = next(it) if want_dx else None
        dy_ref, dgt_ref = (next(it), next(it)) if resid is not None else (None, None)
        a_sh, a_sc, a_g = next(it), next(it), next(it)
        a_gt = next(it) if resid is not None else None
        i = pl.program_id(0)

        @pl.when(i == 0)
        def _():
            a_sh[...] = jnp.zeros_like(a_sh)
            a_sc[...] = jnp.zeros_like(a_sc)
            a_g[...] = jnp.zeros_like(a_g)
            if a_gt is not None:
                a_gt[...] = jnp.zeros_like(a_gt)

        xv = x_ref[...]
        dhv = dh_ref[...]
        gv = g_ref[...]
        r = lax.rsqrt(jnp.mean(xv * xv, axis=-1, keepdims=True) + EPS)
        xn = xv * r
        a_sh[...] += _colsum8(dhv)
        a_sc[...] += _colsum8(dhv * (xn * gv))
        dn = dhv * (1.0 + sc_ref[...])
        a_g[...] += _colsum8(dn * xn)
        if want_dx:
            gdn = gv * dn
            dx = add_ref[...] + r * gdn - xn * (r * r) * jnp.mean(xv * gdn, axis=-1, keepdims=True)
            dx_ref[...] = dx
            if resid is not None:
                dy_ref[...] = (gt_ref[...] * dx).astype(dy_ref.dtype)
                a_gt[...] += _colsum8(dx * y_ref[...])

        @pl.when(i == nt - 1)
        def _():
            dsh_ref[...] = jnp.sum(a_sh[...], axis=0, keepdims=True)
            dsc_ref[...] = jnp.sum(a_sc[...], axis=0, keepdims=True)
            dg_ref[...] = jnp.sum(a_g[...], axis=0, keepdims=True)
            if a_gt is not None:
                dgt_ref[...] = jnp.sum(a_gt[...], axis=0, keepdims=True)

    row = pl.BlockSpec((tile, D), lambda i: (i, 0))
    ins = [xin, dh, g, sc]
    in_specs = [row, pl.BlockSpec((tile, D), lambda i: (i + dh_row0 // tile, 0)), _vec_spec(D), _vec_spec(D)]
    out_specs = [_vec_spec(D)] * 3
    out_shape = [_sds((1, D), F32)] * 3
    scratch = [pltpu.VMEM((8, D), F32)] * 3
    if want_dx:
        ins.append(add)
        in_specs.append(row)
        out_specs.append(row)
        out_shape.append(_sds((s, D), F32))
    if resid is not None:
        ins += [resid[0], resid[1]]
        in_specs += [_vec_spec(D), row]
        out_specs += [row, _vec_spec(D)]
        out_shape += [_sds((s, D), MXU_DTYPE), _sds((1, D), F32)]
        scratch.append(pltpu.VMEM((8, D), F32))
    return _pallas(body, name=name, grid=(nt,), in_specs=in_specs, out_specs=out_specs, out_shape=out_shape,
                   scratch_shapes=scratch, semantics=("arbitrary",))(*ins)


FF_TILE = 128
FF_CHUNK = 128
HALO = 8


def _shift3(pad_ref, r0, ch):
    return tuple(pad_ref[pl.ds(r0 + HALO + d, ch), :] for d in (-1, 0, 1))


def _fill_padded(pad_ref, src_ref, s, ch, halo):
    zeros = jnp.zeros((halo, pad_ref.shape[1]), F32)
    pad_ref[0:halo, :] = zeros
    pad_ref[s + halo:s + 2 * halo, :] = zeros

    def cp(c, carry):
        r0 = pl.multiple_of(c * ch, ch)
        pad_ref[pl.ds(r0 + halo, ch), :] = src_ref[pl.ds(r0, ch), :].astype(F32)
        return carry

    lax.fori_loop(0, s // ch, cp, 0)


def _ffn_act_fwd(u, w, b):
    s = u.shape[0]
    tile, ch = FF_TILE, FF_CHUNK
    nj = DFF // tile

    def body(ug_ref, uv_ref, wg_ref, wv_ref, bg_ref, bv_ref, f_ref, gpad, vpad):
        _fill_padded(gpad, ug_ref, s, ch, HALO)
        _fill_padded(vpad, uv_ref, s, ch, HALO)

        def conv(pad, w_ref, b_ref, r0):
            prev, cur, nxt = _shift3(pad, r0, ch)
            return w_ref[0:1, :] * prev + w_ref[1:2, :] * cur + w_ref[2:3, :] * nxt + b_ref[...]

        def step(c, carry):
            r0 = pl.multiple_of(c * ch, ch)
            gc = conv(gpad, wg_ref, bg_ref, r0)
            vc = conv(vpad, wv_ref, bv_ref, r0)
            f_ref[pl.ds(r0, ch), :] = (gc * _sigmoid(gc) * vc).astype(f_ref.dtype)
            return carry

        lax.fori_loop(0, s // ch, step, 0)

    col = lambda off: pl.BlockSpec((s, tile), lambda j: (0, j + off))
    wsp = lambda off: pl.BlockSpec((3, tile), lambda j: (0, j + off))
    bsp = lambda off: pl.BlockSpec((1, tile), lambda j: (0, j + off))
    return _pallas(
        body, name="ffn_act_fwd", grid=(nj,),
        in_specs=[col(0), col(nj), wsp(0), wsp(nj), bsp(0), bsp(nj)],
        out_specs=col(0), out_shape=_sds((s, DFF), MXU_DTYPE),
        scratch_shapes=[pltpu.VMEM((s + 2 * HALO, tile), F32)] * 2,
        semantics=("parallel",),
    )(u, u, w, w, b, b)


def _ffn_act_bwd(u, df, w, b):
    s = u.shape[0]
    nj = DFF // FF_TILE
    ch = FF_CHUNK

    def body(ug_ref, uv_ref, df_ref, wg_ref, wv_ref, bg_ref, bv_ref,
             dug_ref, duv_ref, dwg_ref, dwv_ref, dbg_ref, dbv_ref, gpad, vpad, dgpad, dvpad, acc):
        _fill_padded(gpad, ug_ref, s, ch, HALO)
        _fill_padded(vpad, uv_ref, s, ch, HALO)
        zeros = jnp.zeros((HALO, FF_TILE), F32)
        for p in (dgpad, dvpad):
            p[0:HALO, :] = zeros
            p[s + HALO:s + 2 * HALO, :] = zeros
        acc[...] = jnp.zeros_like(acc)

        def step(c, carry):
            r0 = pl.multiple_of(c * ch, ch)
            gs = _shift3(gpad, r0, ch)
            vs = _shift3(vpad, r0, ch)
            gc = wg_ref[0:1, :] * gs[0] + wg_ref[1:2, :] * gs[1] + wg_ref[2:3, :] * gs[2] + bg_ref[...]
            vc = wv_ref[0:1, :] * vs[0] + wv_ref[1:2, :] * vs[1] + wv_ref[2:3, :] * vs[2] + bv_ref[...]
            sg = _sigmoid(gc)
            dfv = df_ref[pl.ds(r0, ch), :].astype(F32)
            dgc = dfv * vc * (sg * (1.0 + gc * (1.0 - sg)))
            dvc = dfv * (gc * sg)
            dgpad[pl.ds(r0 + HALO, ch), :] = dgc
            dvpad[pl.ds(r0 + HALO, ch), :] = dvc
            for t in range(3):
                acc[8 * t:8 * t + 8, :] += _colsum8(dgc * gs[t])
                acc[24 + 8 * t:32 + 8 * t, :] += _colsum8(dvc * vs[t])
            acc[48:56, :] += _colsum8(dgc)
            acc[56:64, :] += _colsum8(dvc)
            return carry

        lax.fori_loop(0, s // ch, step, 0)

        def step2(c, carry):
            r0 = pl.multiple_of(c * ch, ch)
            for pad, w_ref, o_ref in ((dgpad, wg_ref, dug_ref), (dvpad, wv_ref, duv_ref)):
                prev, cur, nxt = _shift3(pad, r0, ch)
                o_ref[pl.ds(r0, ch), :] = (w_ref[0:1, :] * nxt + w_ref[1:2, :] * cur + w_ref[2:3, :] * prev).astype(o_ref.dtype)
            return carry

        lax.fori_loop(0, s // ch, step2, 0)
        for t in range(3):
            dwg_ref[t:t + 1, :] = jnp.sum(acc[8 * t:8 * t + 8, :], axis=0, keepdims=True)
            dwv_ref[t:t + 1, :] = jnp.sum(acc[24 + 8 * t:32 + 8 * t, :], axis=0, keepdims=True)
        dbg_ref[...] = jnp.sum(acc[48:56, :], axis=0, keepdims=True)
        dbv_ref[...] = jnp.sum(acc[56:64, :], axis=0, keepdims=True)

    col = lambda off: pl.BlockSpec((s, FF_TILE), lambda j: (0, j + off))
    wsp = lambda off: pl.BlockSpec((3, FF_TILE), lambda j: (0, j + off))
    bsp = lambda off: pl.BlockSpec((1, FF_TILE), lambda j: (0, j + off))
    return _pallas(
        body, name="ffn_act_bwd", grid=(nj,),
        in_specs=[col(0), col(nj), col(0), wsp(0), wsp(nj), bsp(0), bsp(nj)],
        out_specs=[col(0), col(0), wsp(0), wsp(0), bsp(0), bsp(0)],
        out_shape=[_sds((s, DFF), MXU_DTYPE)] * 2 + [_sds((3, DFF), F32)] * 2 + [_sds((1, DFF), F32)] * 2,
        scratch_shapes=[pltpu.VMEM((s + 2 * HALO, FF_TILE), F32)] * 4 + [pltpu.VMEM((64, FF_TILE), F32)],
        semantics=("parallel",),
    )(u, u, df, w, w, b, b)


CONV_CHUNK = 64
CONV_HALO = 16


def _tap(pad_ref, r0, k):
    return pad_ref[pl.ds(r0 + CONV_HALO - CW // 2 + k, CONV_CHUNK), :]


def _glu_into(pad_ref, a_ref, g_ref, s):
    zeros = jnp.zeros((CONV_HALO, LANES), F32)
    pad_ref[0:CONV_HALO, :] = zeros
    pad_ref[s + CONV_HALO:s + 2 * CONV_HALO, :] = zeros

    def cp(c, carry):
        r0 = pl.multiple_of(c * ROW_TILE, ROW_TILE)
        pad_ref[pl.ds(r0 + CONV_HALO, ROW_TILE), :] = a_ref[pl.ds(r0, ROW_TILE), :] * _sigmoid(g_ref[pl.ds(r0, ROW_TILE), :])
        return carry

    lax.fori_loop(0, s // ROW_TILE, cp, 0)


def _conf_conv_fwd(ag, conv_w, conv_b):
    s = ag.shape[0]
    nc = DC // LANES

    def body(a_ref, g_ref, w_ref, b_ref, o_ref, upad):
        _glu_into(upad, a_ref, g_ref, s)

        def step(c, carry):
            r0 = pl.multiple_of(c * CONV_CHUNK, CONV_CHUNK)
            acc = jnp.broadcast_to(b_ref[...], (CONV_CHUNK, LANES))
            for k in range(CW):
                acc = acc + w_ref[k:k + 1, :] * _tap(upad, r0, k)
            o_ref[pl.ds(r0, CONV_CHUNK), :] = acc
            return carry

        lax.fori_loop(0, s // CONV_CHUNK, step, 0)

    col = lambda off: pl.BlockSpec((s, LANES), lambda c: (0, c + off))
    return _pallas(
        body, name="conf_conv_fwd", grid=(nc,),
        in_specs=[col(0), col(nc), pl.BlockSpec((CW, LANES), lambda c: (0, c)), pl.BlockSpec((1, LANES), lambda c: (0, c))],
        out_specs=col(0), out_shape=_sds((s, DC), F32),
        scratch_shapes=[pltpu.VMEM((s + 2 * CONV_HALO, LANES), F32)],
        semantics=("parallel",),
    )(ag, ag, conv_w, conv_b)


def _ln_stats(x):
    mu = jnp.mean(x, axis=-1, keepdims=True)
    xc = x - mu
    var = jnp.mean(xc * xc, axis=-1, keepdims=True)
    rstd = lax.rsqrt(var + EPS)
    return xc * rstd, rstd


def _conf_ln_fwd(u1, ln_g, ln_b, ycat):
    s = u1.shape[0]

    def body(u_ref, g_ref, b_ref, ycat_ref, o_ref):
        del ycat_ref
        xhat, _ = _ln_stats(u_ref[...])
        y = xhat * g_ref[...] + b_ref[...]
        o_ref[...] = (y * _sigmoid(y)).astype(o_ref.dtype)

    t = _row_tile(s, 4)
    return _pallas(
        body, name="conf_ln_fwd", grid=(s // t,),
        in_specs=[pl.BlockSpec((t, DC), lambda i: (i, 0)), _vec_spec(DC), _vec_spec(DC),
                  pl.BlockSpec(memory_space=pl.ANY)],
        out_specs=pl.BlockSpec((t, DC), lambda i: (i, 1)),
        out_shape=_sds(ycat.shape, ycat.dtype),
        input_output_aliases={3: 0},
        semantics=("parallel",),
    )(u1, ln_g, ln_b, ycat)


def _conf_ln_bwd(dycat, u1, ln_g, ln_b):
    s = u1.shape[0]
    t = _row_tile(s, 4)
    nt = s // t

    def body(dy_ref, u_ref, g_ref, b_ref, du_ref, dg_ref, db_ref, a_g, a_b):
        i = pl.program_id(0)

        @pl.when(i == 0)
        def _():
            a_g[...] = jnp.zeros_like(a_g)
            a_b[...] = jnp.zeros_like(a_b)

        xhat, rstd = _ln_stats(u_ref[...])
        gv = g_ref[...]
        y = xhat * gv + b_ref[...]
        sg = _sigmoid(y)
        dyl = dy_ref[...] * (sg * (1.0 + y * (1.0 - sg)))
        a_g[...] += _colsum8(dyl * xhat)
        a_b[...] += _colsum8(dyl)
        dxh = dyl * gv
        du_ref[...] = rstd * (dxh - jnp.mean(dxh, axis=-1, keepdims=True)
                              - xhat * jnp.mean(dxh * xhat, axis=-1, keepdims=True))

        @pl.when(i == nt - 1)
        def _():
            dg_ref[...] = jnp.sum(a_g[...], axis=0, keepdims=True)
            db_ref[...] = jnp.sum(a_b[...], axis=0, keepdims=True)

    return _pallas(
        body, name="conf_ln_bwd", grid=(nt,),
        in_specs=[pl.BlockSpec((t, DC), lambda i: (i, 1)), pl.BlockSpec((t, DC), lambda i: (i, 0)),
                  _vec_spec(DC), _vec_spec(DC)],
        out_specs=[pl.BlockSpec((t, DC), lambda i: (i, 0)), _vec_spec(DC), _vec_spec(DC)],
        out_shape=[_sds((s, DC), F32), _sds((1, DC), F32), _sds((1, DC), F32)],
        scratch_shapes=[pltpu.VMEM((8, DC), F32)] * 2,
        semantics=("arbitrary",),
    )(dycat, u1, ln_g, ln_b)


def _conf_conv_bwd(ag, du1, conv_w, rows_out):
    s = ag.shape[0]
    nc = DC // LANES

    def body(a_ref, g_ref, d_ref, w_ref, da_ref, dg_ref, dw_ref, db_ref, upad, dpad, acc):
        _glu_into(upad, a_ref, g_ref, s)
        _fill_padded(dpad, d_ref, s, ROW_TILE, CONV_HALO)
        acc[...] = jnp.zeros_like(acc)

        def step(c, carry):
            r0 = pl.multiple_of(c * CONV_CHUNK, CONV_CHUNK)
            dcur = dpad[pl.ds(r0 + CONV_HALO, CONV_CHUNK), :]
            du0 = jnp.zeros((CONV_CHUNK, LANES), F32)
            for k in range(CW):
                du0 = du0 + w_ref[k:k + 1, :] * _tap(dpad, r0, CW - 1 - k)
                acc[8 * k:8 * k + 8, :] += _colsum8(dcur * _tap(upad, r0, k))
            acc[8 * CW:8 * CW + 8, :] += _colsum8(dcur)
            av = a_ref[pl.ds(r0, CONV_CHUNK), :]
            sg = _sigmoid(g_ref[pl.ds(r0, CONV_CHUNK), :])
            da_ref[pl.ds(r0, CONV_CHUNK), :] = (du0 * sg).astype(da_ref.dtype)
            dg_ref[pl.ds(r0, CONV_CHUNK), :] = (du0 * av * (sg * (1.0 - sg))).astype(dg_ref.dtype)
            return carry

        lax.fori_loop(0, s // CONV_CHUNK, step, 0)
        if rows_out > s:
            zeros = jnp.zeros((rows_out - s, LANES), da_ref.dtype)
            da_ref[s:rows_out, :] = zeros
            dg_ref[s:rows_out, :] = zeros
        for k in range(CW):
            dw_ref[k:k + 1, :] = jnp.sum(acc[8 * k:8 * k + 8, :], axis=0, keepdims=True)
        db_ref[...] = jnp.sum(acc[8 * CW:8 * CW + 8, :], axis=0, keepdims=True)

    col = lambda off: pl.BlockSpec((s, LANES), lambda c: (0, c + off))
    ocol = pl.BlockSpec((rows_out, LANES), lambda c: (0, c))
    return _pallas(
        body, name="conf_conv_bwd", grid=(nc,),
        in_specs=[col(0), col(nc), col(0), pl.BlockSpec((CW, LANES), lambda c: (0, c))],
        out_specs=[ocol, ocol, pl.BlockSpec((CW, LANES), lambda c: (0, c)), pl.BlockSpec((1, LANES), lambda c: (0, c))],
        out_shape=[_sds((rows_out, DC), MXU_DTYPE)] * 2 + [_sds((CW, DC), F32), _sds((1, DC), F32)],
        scratch_shapes=[pltpu.VMEM((s + 2 * CONV_HALO, LANES), F32)] * 2 + [pltpu.VMEM((8 * (CW + 1), LANES), F32)],
        semantics=("parallel",),
    )(ag, ag, du1, conv_w)


Q_TILE = 2 * GW
K_WIN = PAIR_ROWS * GW


def _bias_table(rpb_rev):
    def body(p_ref, t_ref):
        kcol = lax.broadcasted_iota(jnp.int32, (GW, LANES), 0)
        lane = lax.broadcasted_iota(jnp.int32, (GW, LANES), 1)
        qcol = lane % GW
        cs = jnp.clip(qcol - NA_ROWS, 0, GW - 2 * NA_ROWS)
        colvalid = (kcol >= cs) & (kcol < cs + 2 * NA_ROWS)
        neg = jnp.full((GW, LANES), NEG, F32)

        def skew(h, ro, shift):
            if ro < 0 or ro >= 2 * NA_ROWS - 1:
                return neg
            row = jnp.broadcast_to(p_ref[h * 16 + ro:h * 16 + ro + 1, :], (GW, LANES))
            return pltpu.roll(row, shift, 1, stride=1, stride_axis=0)

        for h in range(NH):
            for b in range(TAB_BLOCKS):
                val = jnp.where(lane < GW, skew(h, b - 1, GW + 1), skew(h, b - 2, 1))
                t_ref[h, b * GW:(b + 1) * GW, :] = jnp.where(colvalid, val, neg)

    return _pallas(body, name="attn_bias_table", out_shape=_sds((NH, TAB_BLOCKS * GW, LANES), F32))(rpb_rev)


def _rpb_grad(tt):
    def body(t_ref, o_ref):
        lane = lax.broadcasted_iota(jnp.int32, (GW, LANES), 1)
        si = lax.broadcasted_iota(jnp.int32, (GW, GW), 0)
        ti = lax.broadcasted_iota(jnp.int32, (GW, GW), 1)
        flip = jnp.where(si + ti == GW - 1, 1.0, 0.0).astype(F32)
        o_ref[...] = jnp.zeros_like(o_ref)
        for h in range(NH):
            for ro in range(2 * NA_ROWS - 1):
                lo = t_ref[h, (ro + 1) * GW:(ro + 2) * GW, :]
                hi = t_ref[h, (ro + 2) * GW:(ro + 3) * GW, :]
                g = jnp.where(lane < GW, lo + pltpu.roll(hi, GW, 1), 0.0)
                gf = jnp.dot(flip, g, preferred_element_type=F32, precision=lax.Precision.HIGHEST)
                sk = pltpu.roll(gf, 0, 1, stride=1, stride_axis=0)
                o_ref[h * 16 + ro:h * 16 + ro + 1, :] = jnp.sum(sk, axis=0, keepdims=True)

    return _pallas(body, name="attn_rpb_grad", out_shape=_sds((NH * 16, LANES), F32))(tt)


def _attn_geometry(i, rows):
    wsp = jnp.clip(2 * i - NA_ROWS // 2, 0, rows - PAIR_ROWS)
    k0 = pl.multiple_of(wsp * GW, GW)
    t0 = pl.multiple_of((wsp - 2 * i + NA_ROWS) * GW, GW)
    rr = lax.broadcasted_iota(jnp.int32, (GW, Q_TILE), 1) // GW
    wsr = jnp.clip(2 * i + rr - NA_ROWS // 2, 0, rows - NA_ROWS)
    edge_masks = tuple(jnp.where((kr >= wsr) & (kr < wsr + NA_ROWS), 0.0, NEG).astype(F32)
                       for kr in (wsp, wsp + PAIR_ROWS - 1))
    return k0, t0, edge_masks


def _biased(s_raw, bias, edge_masks):
    x = s_raw + bias
    return jnp.concatenate([x[:GW] + edge_masks[0], x[GW:K_WIN - GW], x[K_WIN - GW:] + edge_masks[1]], axis=0)


def _two_heads_on_lanes(xt):
    feat = lax.broadcasted_iota(jnp.int32, xt.shape, 0)
    zero = jnp.zeros_like(xt)
    return jnp.concatenate([jnp.where(feat < HD, xt, zero), jnp.where(feat >= HD, xt, zero)], axis=1)


def _two_heads_on_rows(x):
    lane = lax.broadcasted_iota(jnp.int32, x.shape, 1)
    zero = jnp.zeros_like(x)
    return jnp.concatenate([jnp.where(lane < HD, x, zero), jnp.where(lane >= HD, x, zero)], axis=0)


def _pick_heads(x2):
    n = x2.shape[0] // 2
    lane = lax.broadcasted_iota(jnp.int32, (n, LANES), 1)
    return jnp.where(lane < HD, x2[:n], x2[n:])


_TN = (((0,), (0,)), ((), ()))


def _attn_fwd(qkv, tab, s):
    rows = s // GW
    npair = rows // 2

    def body(q_ref, kv_ref, tab_ref, o_ref, lse_ref):
        i = pl.program_id(0)
        k0, t0, edge_masks = _attn_geometry(i, rows)
        for p in range(NH // 2):
            cq = slice(p * LANES, (p + 1) * LANES)
            ck = slice(DA + p * LANES, DA + (p + 1) * LANES)
            cv = slice(2 * DA + p * LANES, 2 * DA + (p + 1) * LANES)
            qm2 = _two_heads_on_lanes(q_ref[:, cq].T) * SCALE
            s_loc = jnp.dot(kv_ref[pl.ds(k0, K_WIN), ck], qm2, preferred_element_type=F32)
            s_ctx = jnp.dot(kv_ref[pl.ds(s, CTX), ck], qm2, preferred_element_type=F32)
            p_loc, p_ctx = [], []
            for hh in range(2):
                h = 2 * p + hh
                ch = slice(hh * Q_TILE, (hh + 1) * Q_TILE)
                sl = _biased(s_loc[:, ch], tab_ref[h, pl.ds(t0, K_WIN), :], edge_masks)
                sc = s_ctx[:, ch]
                m = jnp.maximum(jnp.max(sl, axis=0, keepdims=True), jnp.max(sc, axis=0, keepdims=True))
                el = jnp.exp(sl - m)
                ec = jnp.exp(sc - m)
                l = jnp.sum(el, axis=0, keepdims=True) + jnp.sum(ec, axis=0, keepdims=True)
                inv = 1.0 / l
                lse_ref[h:h + 1, :] = m + jnp.log(l)
                p_loc.append((el * inv).astype(MXU_DTYPE))
                p_ctx.append((ec * inv).astype(MXU_DTYPE))
            o2 = (lax.dot_general(jnp.concatenate(p_loc, axis=1), kv_ref[pl.ds(k0, K_WIN), cv], _TN, preferred_element_type=F32)
                  + lax.dot_general(jnp.concatenate(p_ctx, axis=1), kv_ref[pl.ds(s, CTX), cv], _TN, preferred_element_type=F32))
            o_ref[:, cq] = _pick_heads(o2).astype(o_ref.dtype)

    return _pallas(
        body, name="attn_fwd", grid=(npair,),
        in_specs=[pl.BlockSpec((Q_TILE, DA), lambda i: (i, 0)), pl.BlockSpec(memory_space=pltpu.VMEM),
                  pl.BlockSpec(memory_space=pltpu.VMEM)],
        out_specs=[pl.BlockSpec((Q_TILE, DA), lambda i: (i, 0)), pl.BlockSpec((NH, Q_TILE), lambda i: (0, i))],
        out_shape=[_sds((s, D), MXU_DTYPE), _sds((NH, s), F32)],
        semantics=("arbitrary",),
    )(qkv, qkv, tab)


def _attn_bwd(qkv, tab, lse, dycat, s):
    rows = s // GW
    npair = rows // 2
    sa = s + CTX
    nzero = CTX // Q_TILE

    def body(q_ref, do_ref, lse_ref, kv_ref, tab_ref, dq_ref, dkv_ref, tt_ref, dk_acc, dv_acc):
        i = pl.program_id(0)

        @pl.when(i == 0)
        def _():
            dk_acc[...] = jnp.zeros_like(dk_acc)
            dv_acc[...] = jnp.zeros_like(dv_acc)
            tt_ref[...] = jnp.zeros_like(tt_ref)

        @pl.when(i >= npair)
        def _():
            dq_ref[...] = jnp.zeros_like(dq_ref)

        @pl.when(i < npair)
        def _():
            k0, t0, edge_masks = _attn_geometry(i, rows)
            for p in range(NH // 2):
                cq = slice(p * LANES, (p + 1) * LANES)
                ck = slice(DA + p * LANES, DA + (p + 1) * LANES)
                cv = slice(2 * DA + p * LANES, 2 * DA + (p + 1) * LANES)
                qp = q_ref[:, cq] * SCALE
                dop = do_ref[:, cq].astype(MXU_DTYPE)
                qm2 = _two_heads_on_lanes(qp.T)
                dom2 = _two_heads_on_lanes(dop.T)
                kw = kv_ref[pl.ds(k0, K_WIN), ck]
                kc = kv_ref[pl.ds(s, CTX), ck]
                vw = kv_ref[pl.ds(k0, K_WIN), cv]
                vc = kv_ref[pl.ds(s, CTX), cv]
                s_loc = jnp.dot(kw, qm2, preferred_element_type=F32)
                s_ctx = jnp.dot(kc, qm2, preferred_element_type=F32)
                dp_loc = jnp.dot(vw, dom2, preferred_element_type=F32)
                dp_ctx = jnp.dot(vc, dom2, preferred_element_type=F32)
                p_loc, p_ctx, ds_loc, ds_ctx = [], [], [], []
                for hh in range(2):
                    h = 2 * p + hh
                    ch = slice(hh * Q_TILE, (hh + 1) * Q_TILE)
                    lse_h = lse_ref[h:h + 1, :]
                    pl_ = jnp.exp(_biased(s_loc[:, ch], tab_ref[h, pl.ds(t0, K_WIN), :], edge_masks) - lse_h)
                    pc_ = jnp.exp(s_ctx[:, ch] - lse_h)
                    dpl = dp_loc[:, ch]
                    dpc = dp_ctx[:, ch]
                    delta = jnp.sum(pl_ * dpl, axis=0, keepdims=True) + jnp.sum(pc_ * dpc, axis=0, keepdims=True)
                    dsl = pl_ * (dpl - delta)
                    dsc = pc_ * (dpc - delta)
                    tt_ref[h, pl.ds(t0, K_WIN), :] += dsl
                    p_loc.append(pl_.astype(MXU_DTYPE))
                    p_ctx.append(pc_.astype(MXU_DTYPE))
                    ds_loc.append(dsl.astype(MXU_DTYPE))
                    ds_ctx.append(dsc.astype(MXU_DTYPE))
                p_loc, p_ctx = jnp.concatenate(p_loc, axis=1), jnp.concatenate(p_ctx, axis=1)
                ds_loc, ds_ctx = jnp.concatenate(ds_loc, axis=1), jnp.concatenate(ds_ctx, axis=1)
                do_rows = _two_heads_on_rows(dop)
                q_rows = _two_heads_on_rows(qp)
                dv_acc[pl.ds(k0, K_WIN), cq] += jnp.dot(p_loc, do_rows, preferred_element_type=F32)
                dv_acc[pl.ds(s, CTX), cq] += jnp.dot(p_ctx, do_rows, preferred_element_type=F32)
                dk_acc[pl.ds(k0, K_WIN), cq] += jnp.dot(ds_loc, q_rows, preferred_element_type=F32)
                dk_acc[pl.ds(s, CTX), cq] += jnp.dot(ds_ctx, q_rows, preferred_element_type=F32)
                dq2 = (lax.dot_general(ds_loc, kw, _TN, preferred_element_type=F32)
                       + lax.dot_general(ds_ctx, kc, _TN, preferred_element_type=F32))
                dq_ref[:, cq] = (_pick_heads(dq2) * SCALE).astype(dq_ref.dtype)

        @pl.when(i == npair - 1)
        def _():
            def cp(c, carry):
                r0 = pl.multiple_of(c * ROW_TILE, ROW_TILE)
                dkv_ref[pl.ds(r0, ROW_TILE), 0:DA] = dk_acc[pl.ds(r0, ROW_TILE), :].astype(dkv_ref.dtype)
                dkv_ref[pl.ds(r0, ROW_TILE), DA:2 * DA] = dv_acc[pl.ds(r0, ROW_TILE), :].astype(dkv_ref.dtype)
                return carry

            lax.fori_loop(0, sa // ROW_TILE, cp, 0)

    qmap = lambda i: (jnp.minimum(i, npair - 1), 0)
    return _pallas(
        body, name="attn_bwd", grid=(npair + nzero,),
        in_specs=[pl.BlockSpec((Q_TILE, DA), qmap), pl.BlockSpec((Q_TILE, DA), qmap),
                  pl.BlockSpec((NH, Q_TILE), lambda i: (0, jnp.minimum(i, npair - 1))),
                  pl.BlockSpec(memory_space=pltpu.VMEM), pl.BlockSpec(memory_space=pltpu.VMEM)],
        out_specs=[pl.BlockSpec((Q_TILE, DA), lambda i: (i, 0)), pl.BlockSpec(memory_space=pltpu.VMEM),
                   pl.BlockSpec(memory_space=pltpu.VMEM)],
        out_shape=[_sds((sa, DA), MXU_DTYPE), _sds((sa, 2 * DA), MXU_DTYPE), _sds((NH, TAB_BLOCKS * GW, LANES), F32)],
        scratch_shapes=[pltpu.VMEM((sa, DA), F32)] * 2,
        semantics=("arbitrary",),
    )(qkv, dycat, lse, qkv, tab)


def _tile(n, prefs):
    for t in prefs:
        if n % t == 0:
            return t
    raise ValueError((n, prefs))


def _local_step(x, ctx, tgt, mod, mod_c, vec, w_in, late_weights, rpb_rev, early_grads=None):
    s = x.shape[0]
    sa = s + CTX
    ts = _tile(s, (1024, 512, 256))
    ts2 = _tile(s, (2048, 1024, 512, 256))
    tsa = _tile(sa, (1088, 640, 256))
    tsa2 = _tile(sa, (2176, 640, 256))
    sh1, sc1, gt1, sh2, sc2, gt2 = (mod[i:i + 1] for i in range(6))
    csh1, csc1 = mod_c[0:1], mod_c[1:2]
    act = MXU_DTYPE

    tab = _bias_table(rpb_rev)
    h_all = _rmsmod_fwd(x, ctx, vec["g_norm1"], sc1, sh1, csc1, csh1)
    w_in = w_in(h_all) if callable(w_in) else w_in
    qkv = _mm(h_all, w_in, mode="nn", m=sa, n=3 * DA, k=D, tm=tsa2, tn=512, tk=D, out_dtype=MXU_DTYPE, name="mm_qkv")
    ag = _mm(h_all, w_in, mode="nn", m=s, n=2 * DC, k=D, tm=ts2, tn=512, tk=D, out_dtype=F32, name="mm_ag", b_off=(0, 3))
    ycat, lse = _attn_fwd(qkv, tab, s)
    u1 = _conf_conv_fwd(ag, vec["conv_w"], vec["conv_b"])
    ycat = _conf_ln_fwd(u1, vec["ln_g"], vec["ln_b"], ycat)
    if callable(late_weights):
        w_out, ffn_weights = late_weights(ycat)
    else:
        w_out, ffn_weights = late_weights[0], late_weights[1:]
    y = _mm(ycat, w_out, mode="nn", m=s, n=D, k=D, tm=ts2, tn=512, tk=D, out_dtype=F32, name="mm_out")
    x1, h2 = _resid_rmsmod_fwd(x, y, gt1, vec["g_norm2"], sc2, sh2)
    w_up, w_down = ffn_weights(h2) if callable(ffn_weights) else ffn_weights
    u = _mm(h2, w_up, mode="nn", m=s, n=2 * DFF, k=D, tm=ts2, tn=512, tk=D, out_dtype=act, name="mm_up")
    f = _ffn_act_fwd(u, vec["ffn_conv_w"], vec["ffn_conv_b"])
    z = _mm(f, w_down, mode="nn", m=s, n=D, k=DFF, tm=ts, tn=D, tk=DFF, out_dtype=F32, name="mm_down")
    dx2, dz, loss, dgt2, dgf = _final_fwd_bwd(x1, z, gt2, vec["g_final"], tgt)

    df = _mm(dz, w_down, mode="nt", m=s, n=DFF, k=D, tm=ts, tn=DFF, tk=D, out_dtype=act, name="mm_down_dx")
    d_w_down = _mm(f, dz, mode="tn", m=DFF, n=D, k=s, tm=DFF // 2, tn=D, tk=ts2, out_dtype=F32, name="mm_down_dw")
    dug, duv, dfw_g, dfw_v, dfb_g, dfb_v = _ffn_act_bwd(u, df, vec["ffn_conv_w"], vec["ffn_conv_b"])
    dw_kw = dict(mode="tn", m=D, n=DFF, k=s, tm=D, tn=DFF, tk=ts, out_dtype=F32, out_total=(D, 2 * DFF))
    d_w_up = _mm(h2, dug, name="mm_up_dw_gate", **dw_kw)
    d_w_up = _mm(h2, duv, name="mm_up_dw_val", o_off=(0, 1), into=d_w_up, **dw_kw)
    if early_grads is not None:
        early_grads[0](d_w_up, d_w_down)
    dh2 = _mm([dug, duv], w_up, mode="nt", m=s, n=D, k=2 * DFF, tm=ts, tn=D, tk=2 * DFF, out_dtype=F32, name="mm_up_dx")
    sc2_b = sc2 if early_grads is None else sc2 + early_grads[1](dh2)
    dsh2, dsc2, dg2, dx1, dy, dgt1 = _rmsmod_bwd(x1, dh2, vec["g_norm2"], sc2_b, name="rmsmod2_bwd", add=dx2, resid=(gt1, y))
    dycat = _mm(dy, w_out, mode="nt", m=s, n=D, k=D, tm=ts2, tn=512, tk=D, out_dtype=F32, name="mm_out_dx")
    d_w_out = _mm(ycat, dy, mode="tn", m=D, n=D, k=s, tm=D, tn=D, tk=ts, out_dtype=F32, name="mm_out_dw")
    du1, dln_g, dln_b = _conf_ln_bwd(dycat, u1, vec["ln_g"], vec["ln_b"])
    da, dg, dconv_w, dconv_b = _conf_conv_bwd(ag, du1, vec["conv_w"], sa)
    dq, dkv, tt = _attn_bwd(qkv, tab, lse, dycat, s)
    drpb_rev = _rpb_grad(tt)
    d_pieces = [dq, dkv, da, dg]
    dh = _mm(d_pieces, w_in, mode="nt", m=sa, n=D, k=NIN, tm=tsa, tn=D, tk=NIN, out_dtype=F32, name="mm_in_dx")
    d_w_in = _mm(h_all, d_pieces, mode="tn", m=D, n=NIN, k=sa, tm=D, tn=NIN, tk=tsa, out_dtype=F32, name="mm_in_dw")
    dsh1, dsc1, dg1, grad_x = _rmsmod_bwd(x, dh, vec["g_norm1"], sc1, name="rmsmod1_bwd", add=dx1)
    dcsh1, dcsc1, dg1c = _rmsmod_bwd(ctx, dh, vec["g_norm1"], csc1, name="rmsmod1_ctx_bwd", dh_row0=s)

    small = dict(
        dmod=[dsh1, dsc1, dgt1, dsh2, dsc2, dgt2], dmod_c=[dcsh1, dcsc1],
        g_norm1=[dg1, dg1c], g_norm2=dg2, g_final=dgf, conv_b=dconv_b, ln_g=dln_g, ln_b=dln_b, conv_w=dconv_w,
        ffn_conv_w=[dfw_g, dfw_v], ffn_conv_b=[dfb_g, dfb_v], rpb_rev=drpb_rev,
    )
    return loss, grad_x, d_w_in, d_w_out, d_w_up, d_w_down, small


N_CHIPS = 4
HBM = pl.BlockSpec(memory_space=pl.ANY)
BIG = {"w_in": ("col", (D, NIN)), "w_out": ("row", (D, D)), "w_up": ("col", (D, 2 * DFF)), "w_down": ("row", (DFF, D))}
BIG_NAMES = tuple(BIG)
LATE_NAMES = ("w_out", "w_up", "w_down")


def _shard_shape(name):
    kind, (r, c) = BIG[name]
    return (r, c // N_CHIPS) if kind == "col" else (r // N_CHIPS, c)


def _half_rows(name):
    return _shard_shape(name)[0] // 2


def _place():
    x, y, c = lax.axis_index("x"), lax.axis_index("y"), lax.axis_index("c")
    others = [(1 - x, y), (x, 1 - y), (1 - x, 1 - y)]
    return x, y, c, 2 * x + y, (x, y, 1 - c), others


def _whole_region(ref, name, chip, half):
    kind, _ = BIG[name]
    r, c = _shard_shape(name)
    if kind == "col":
        return ref.at[pl.ds(half * (r // 2), r // 2), pl.ds(chip * c, c)]
    return ref.at[pl.ds(chip * r + half * (r // 2), r // 2), :]


def _remote(src, dst, send_sem, recv_sem, to):
    return pltpu.make_async_remote_copy(src_ref=src, dst_ref=dst, send_sem=send_sem, recv_sem=recv_sem,
                                        device_id=to, device_id_type=MESH)


def _cast_into_whole(name, shard, chip):
    kind, whole = BIG[name]
    r, c = shard.shape
    if kind == "col":
        tr = 256
        o_spec = pl.BlockSpec((tr, c), lambda i, ch: (i, ch[0]))
    else:
        tr = _tile(r, (128, 352))
        o_spec = pl.BlockSpec((tr, c), lambda i, ch: (ch[0] * (r // tr) + i, 0))

    def body(ch_ref, x_ref, o_ref):
        del ch_ref
        o_ref[...] = x_ref[...].astype(o_ref.dtype)

    return _pallas(body, name="cast_" + name, prefetch=1, grid=(r // tr,),
                   in_specs=[pl.BlockSpec((tr, c), lambda i, ch: (i, 0))], out_specs=o_spec,
                   out_shape=_sds(whole, MXU_DTYPE), semantics=("parallel",))(chip, shard)


SEM = pl.BlockSpec(memory_space=pltpu.SEMAPHORE)
IN_HBM = pl.BlockSpec(memory_space=pltpu.HBM)
DATAFLOW = pltpu.SideEffectType.DATAFLOW_SIDE_EFFECTING


def _keep_in_hbm(a):
    return pltpu.with_memory_space_constraint(a, pltpu.HBM)


def _several(after):
    return list(after) if isinstance(after, (list, tuple)) else [after]


FLIPS = [(dx, dy, dc) for dx in (0, 1) for dy in (0, 1) for dc in (0, 1)][1:]
OTHER_CHIPS = [f for f in FLIPS if f[2] == 0]


def _flipped(flip):
    x, y, c = lax.axis_index("x"), lax.axis_index("y"), lax.axis_index("c")
    return tuple(1 - v if f else v for v, f in zip((x, y, c), flip))


def _share_start(v, tag, after, flips=FLIPS):
    r, n = v.shape
    ns = 2 * len(flips)

    def body(*refs):
        v_ref, land_ref = refs[0], refs[1]
        sems = refs[2 + len(_several(after)):2 + len(_several(after)) + ns]
        x, y, c = lax.axis_index("x"), lax.axis_index("y"), lax.axis_index("c")
        mine = land_ref.at[pl.ds((4 * x + 2 * y + c) * r, r), :]
        for k, flip in enumerate(flips):
            _remote(v_ref, mine, sems[2 * k], sems[2 * k + 1], _flipped(flip)).start()

    res = pl.pallas_call(
        body, name="share_" + tag + "_start",
        out_shape=(*[pltpu.SemaphoreType.DMA(())] * ns, pltpu.HBM(v.shape, v.dtype), pltpu.HBM((8 * r, n), v.dtype)),
        in_specs=[IN_HBM] * 2 + [pl.BlockSpec(memory_space=pl.ANY)] * len(_several(after)),
        out_specs=(*[SEM] * ns, IN_HBM, IN_HBM),
        input_output_aliases={0: ns, 1: ns + 1},
        compiler_params=pltpu.CompilerParams(has_side_effects=DATAFLOW),
    )(_keep_in_hbm(v), _keep_in_hbm(jnp.tile(v, (8, 1))), *_several(after))
    return list(res[:ns]), res[ns], res[ns + 1], flips


def _share_wait(started, after, tag):
    sems, v, land, flips = started
    r = v.shape[0]
    ns = len(sems)

    def body(*refs):
        v_ref, land_ref = refs[0], refs[1]
        sem_refs = refs[2:2 + ns]
        for k, flip in enumerate(flips):
            px, py, pc = _flipped(flip)
            theirs = land_ref.at[pl.ds((4 * px + 2 * py + pc) * r, r), :]
            cp = _remote(v_ref, theirs, sem_refs[2 * k], sem_refs[2 * k + 1], (px, py, pc))
            cp.wait_send()
            cp.wait_recv()

    res = pl.pallas_call(
        body, name="share_" + tag + "_wait",
        out_shape=(pltpu.HBM(v.shape, v.dtype), pltpu.HBM(land.shape, land.dtype)),
        in_specs=[IN_HBM] * 2 + [SEM] * ns + [pl.BlockSpec(memory_space=pl.ANY)] * len(_several(after)),
        out_specs=(IN_HBM, IN_HBM),
        input_output_aliases={0: 0, 1: 1},
        compiler_params=pltpu.CompilerParams(has_side_effects=DATAFLOW),
    )(v, land, *sems, *_several(after))
    return res[1]


def _gather_start(wholes, names, after, tag):
    nw = len(names)
    ns = 2 * 3 * nw

    def body(*refs):
        ins = refs[:nw]
        sems = refs[nw + 1:nw + 1 + ns]
        token = refs[2 * nw + ns + 1]
        _, _, c, chip, _, others = _place()
        for w, name in enumerate(names):
            mine = _whole_region(ins[w], name, chip, c)
            for t, (ox, oy) in enumerate(others):
                k = 2 * (3 * w + t)
                _remote(mine, mine, sems[k], sems[k + 1], (ox, oy, c)).start()
        token[...] = jnp.zeros_like(token)

    res = pl.pallas_call(
        body, name="gather_" + tag + "_start",
        out_shape=(*[pltpu.SemaphoreType.DMA(())] * ns, *[pltpu.HBM(a.shape, a.dtype) for a in wholes], _sds((8, LANES), F32)),
        in_specs=[IN_HBM] * nw + [pl.BlockSpec(memory_space=pl.ANY)],
        out_specs=(*[SEM] * ns, *[IN_HBM] * nw, pl.BlockSpec(memory_space=pltpu.VMEM)),
        input_output_aliases={i: ns + i for i in range(nw)},
        compiler_params=pltpu.CompilerParams(has_side_effects=DATAFLOW),
    )(*[_keep_in_hbm(a) for a in wholes], after)
    return list(res[:ns]), list(res[ns:ns + nw]), res[ns + nw]


def _gather_wait(sems, wholes, names, after, tag):
    nw = len(names)
    ns = len(sems)

    def body(*refs):
        ins = refs[:nw]
        sem_refs = refs[nw:nw + ns]
        _, _, c, chip, _, others = _place()
        for w, name in enumerate(names):
            mine = _whole_region(ins[w], name, chip, c)
            for t, (ox, oy) in enumerate(others):
                got = _whole_region(ins[w], name, 2 * ox + oy, c)
                k = 2 * (3 * w + t)
                cp = _remote(mine, got, sem_refs[k], sem_refs[k + 1], (ox, oy, c))
                cp.wait_send()
                cp.wait_recv()

    return pl.pallas_call(
        body, name="gather_" + tag + "_wait",
        out_shape=tuple(pltpu.HBM(a.shape, a.dtype) for a in wholes),
        in_specs=[IN_HBM] * nw + [SEM] * ns + [pl.BlockSpec(memory_space=pl.ANY)], out_specs=tuple([IN_HBM] * nw),
        input_output_aliases={i: i for i in range(nw)},
        compiler_params=pltpu.CompilerParams(has_side_effects=DATAFLOW),
    )(*wholes, *sems, after)


def _forward_halves(wholes, names, tag):
    nw = len(names)

    def body(*refs):
        outs = refs[nw:2 * nw]
        send_sems, recv_sems = refs[2 * nw:]
        _, _, c, _, sibling, others = _place()
        sends = []
        for w, name in enumerate(names):
            for t, (ox, oy) in enumerate(others):
                got = _whole_region(outs[w], name, 2 * ox + oy, c)
                cp = _remote(got, got, send_sems.at[w, t], recv_sems.at[w, t], sibling)
                cp.start()
                sends.append(cp)
        for w, name in enumerate(names):
            for t, (ox, oy) in enumerate(others):
                got = _whole_region(outs[w], name, 2 * ox + oy, 1 - c)
                _remote(got, got, send_sems.at[w, t], recv_sems.at[w, t], sibling).wait_recv()
        for cp in sends:
            cp.wait_send()

    return pl.pallas_call(
        body, name="gather_" + tag + "_forward",
        out_shape=[_sds(a.shape, a.dtype) for a in wholes],
        in_specs=[HBM] * nw, out_specs=[HBM] * nw,
        input_output_aliases={i: i for i in range(nw)},
        scratch_shapes=[pltpu.SemaphoreType.DMA((nw, 3)), pltpu.SemaphoreType.DMA((nw, 3))],
    )(*wholes)


def _forward_start(wholes, names, tag, after):
    nw = len(names)
    ns = 2 * 3 * nw

    def body(*refs):
        ins = refs[:nw]
        sems = refs[nw + 1:nw + 1 + ns]
        token = refs[2 * nw + ns + 1]
        _, _, c, _, sibling, others = _place()
        for w, name in enumerate(names):
            for t, (ox, oy) in enumerate(others):
                got = _whole_region(ins[w], name, 2 * ox + oy, c)
                k = 2 * (3 * w + t)
                _remote(got, got, sems[k], sems[k + 1], sibling).start()
        token[...] = jnp.zeros_like(token)

    res = pl.pallas_call(
        body, name="gather_" + tag + "_forward_start",
        out_shape=(*[pltpu.SemaphoreType.DMA(())] * ns, *[pltpu.HBM(a.shape, a.dtype) for a in wholes], _sds((8, LANES), F32)),
        in_specs=[IN_HBM] * nw + [pl.BlockSpec(memory_space=pl.ANY)],
        out_specs=(*[SEM] * ns, *[IN_HBM] * nw, pl.BlockSpec(memory_space=pltpu.VMEM)),
        input_output_aliases={i: ns + i for i in range(nw)},
        compiler_params=pltpu.CompilerParams(has_side_effects=DATAFLOW),
    )(*[_keep_in_hbm(a) for a in wholes], after)
    return list(res[:ns]), list(res[ns:ns + nw]), res[ns + nw]


def _forward_wait(sems, wholes, names, after, tag):
    nw = len(names)
    ns = len(sems)

    def body(*refs):
        ins = refs[:nw]
        sem_refs = refs[nw:nw + ns]
        _, _, c, _, sibling, others = _place()
        for w, name in enumerate(names):
            for t, (ox, oy) in enumerate(others):
                k = 2 * (3 * w + t)
                cp = _remote(_whole_region(ins[w], name, 2 * ox + oy, c), _whole_region(ins[w], name, 2 * ox + oy, 1 - c),
                             sem_refs[k], sem_refs[k + 1], sibling)
                cp.wait_send()
                cp.wait_recv()

    return pl.pallas_call(
        body, name="gather_" + tag + "_forward_wait",
        out_shape=tuple(pltpu.HBM(a.shape, a.dtype) for a in wholes),
        in_specs=[IN_HBM] * nw + [SEM] * ns + [pl.BlockSpec(memory_space=pl.ANY)], out_specs=tuple([IN_HBM] * nw),
        input_output_aliases={i: i for i in range(nw)},
        compiler_params=pltpu.CompilerParams(has_side_effects=DATAFLOW),
    )(*wholes, *sems, after)


def _compact_shape(name, dtype):
    kind, (r, c) = BIG[name]
    return _sds((r // 2, c), dtype)


def _swap_pairs(ins, outs, names, c):
    pairs = []
    for w, name in enumerate(names):
        kind, _ = BIG[name]
        half = _half_rows(name)
        if kind == "col":
            pairs.append((ins[w].at[pl.ds((1 - c) * half, half), :], outs[w]))
        else:
            pairs += [(ins[w].at[pl.ds(jj * 2 * half + (1 - c) * half, half), :], outs[w].at[pl.ds(jj * half, half), :])
                      for jj in range(N_CHIPS)]
    return pairs


def _n_swap_copies(names):
    return sum(1 if BIG[n][0] == "col" else N_CHIPS for n in names)


def _swap_start(grads, names, label):
    nw = len(names)
    ns = 2 * _n_swap_copies(names)

    def body(*refs):
        ins, lands = refs[:nw], refs[nw:2 * nw]
        sems = refs[2 * nw:2 * nw + ns]
        token = refs[4 * nw + ns]
        _, _, c, _, sibling, _ = _place()
        for k, (src, dst) in enumerate(_swap_pairs(ins, lands, names, c)):
            _remote(src, dst, sems[2 * k], sems[2 * k + 1], sibling).start()
        token[...] = jnp.zeros_like(token)

    lands = [_keep_in_hbm(lax.empty(_compact_shape(n, F32).shape, F32)) for n in names]
    res = pl.pallas_call(
        body, name=label,
        out_shape=(*[pltpu.SemaphoreType.DMA(())] * ns, *[pltpu.HBM(a.shape, a.dtype) for a in grads],
                   *[pltpu.HBM(a.shape, a.dtype) for a in lands], _sds((8, LANES), F32)),
        in_specs=[IN_HBM] * (2 * nw),
        out_specs=(*[SEM] * ns, *[IN_HBM] * (2 * nw), pl.BlockSpec(memory_space=pltpu.VMEM)),
        input_output_aliases={i: ns + i for i in range(2 * nw)},
        compiler_params=pltpu.CompilerParams(has_side_effects=DATAFLOW),
    )(*[_keep_in_hbm(a) for a in grads], *lands)
    return list(res[:ns]), list(res[ns:ns + nw]), list(res[ns + nw:ns + 2 * nw]), res[ns + 2 * nw]


def _swap_wait(sems, grads, lands, names, after, label):
    nw = len(names)
    ns = len(sems)

    def body(*refs):
        ins, land_refs = refs[:nw], refs[nw:2 * nw]
        sem_refs = refs[2 * nw:2 * nw + ns]
        _, _, c, _, sibling, _ = _place()
        for k, (src, dst) in enumerate(_swap_pairs(ins, land_refs, names, c)):
            cp = _remote(src, dst, sem_refs[2 * k], sem_refs[2 * k + 1], sibling)
            cp.wait_send()
            cp.wait_recv()

    res = pl.pallas_call(
        body, name=label,
        out_shape=tuple(pltpu.HBM(a.shape, a.dtype) for a in (*grads, *lands)),
        in_specs=[IN_HBM] * (2 * nw) + [SEM] * ns + [pl.BlockSpec(memory_space=pl.ANY)] * len(_several(after)),
        out_specs=tuple([IN_HBM] * (2 * nw)),
        input_output_aliases={i: i for i in range(2 * nw)},
        compiler_params=pltpu.CompilerParams(has_side_effects=DATAFLOW),
    )(*grads, *lands, *sems, *_several(after))
    return list(res[:nw]), list(res[nw:])


def _add_halves(name, grad, got, core):
    kind, (r, c) = BIG[name]
    half = _half_rows(name)
    if kind == "col":
        t = 128
        grid = (half // t,)
        g_spec = pl.BlockSpec((t, c), lambda i, cr: (cr[0] * (half // t) + i, 0))
        o_spec = pl.BlockSpec((t, c), lambda i, cr: (i, 0))
    else:
        t = half
        grid = (N_CHIPS,)
        g_spec = pl.BlockSpec((t, c), lambda i, cr: (2 * i + cr[0], 0))
        o_spec = pl.BlockSpec((t, c), lambda i, cr: (i, 0))

    def body(c_ref, g_ref, b_ref, o_ref):
        del c_ref
        o_ref[...] = (g_ref[...] + b_ref[...]).astype(o_ref.dtype)

    return pl.pallas_call(
        body, name="grad_add_" + name,
        grid_spec=pltpu.PrefetchScalarGridSpec(num_scalar_prefetch=1, grid=grid, in_specs=[g_spec, o_spec], out_specs=o_spec),
        out_shape=_compact_shape(name, BF16),
        compiler_params=pltpu.CompilerParams(dimension_semantics=("parallel",), vmem_limit_bytes=VMEM_LIMIT),
    )(core, grad, got)


def _piece(ref, name, chip):
    kind, _ = BIG[name]
    r, c = _shard_shape(name)
    if kind == "col":
        return ref.at[:, pl.ds(chip * c, c)]
    return ref.at[pl.ds(chip * (r // 2), r // 2), :]


def _landing_shape(name):
    r, c = _shard_shape(name)
    return (N_CHIPS - 1, r // 2, c)


def _exchange_start(parts, names, label):
    nw = len(names)
    ns = 2 * 3 * nw

    def body(*refs):
        ins, lands = refs[:nw], refs[nw:2 * nw]
        sems = refs[2 * nw:2 * nw + ns]
        token = refs[4 * nw + ns]
        _, _, c, _, _, others = _place()
        for w, name in enumerate(names):
            for t, (ox, oy) in enumerate(others):
                k = 2 * (3 * w + t)
                _remote(_piece(ins[w], name, 2 * ox + oy), lands[w].at[t], sems[k], sems[k + 1], (ox, oy, c)).start()
        token[...] = jnp.zeros_like(token)

    lands = [_keep_in_hbm(lax.empty(_landing_shape(n), BF16)) for n in names]
    res = pl.pallas_call(
        body, name=label,
        out_shape=(*[pltpu.SemaphoreType.DMA(())] * ns, *[pltpu.HBM(a.shape, a.dtype) for a in parts],
                   *[pltpu.HBM(a.shape, a.dtype) for a in lands], _sds((8, LANES), F32)),
        in_specs=[IN_HBM] * (2 * nw),
        out_specs=(*[SEM] * ns, *[IN_HBM] * (2 * nw), pl.BlockSpec(memory_space=pltpu.VMEM)),
        input_output_aliases={i: ns + i for i in range(2 * nw)},
        compiler_params=pltpu.CompilerParams(has_side_effects=DATAFLOW),
    )(*[_keep_in_hbm(a) for a in parts], *lands)
    return list(res[:ns]), list(res[ns:ns + nw]), list(res[ns + nw:ns + 2 * nw]), res[ns + 2 * nw]


def _exchange_wait(sems, parts, lands, names, after, label):
    nw = len(names)
    ns = len(sems)

    def body(*refs):
        ins, land_refs = refs[:nw], refs[nw:2 * nw]
        sem_refs = refs[2 * nw:2 * nw + ns]
        _, _, c, _, _, others = _place()
        for w, name in enumerate(names):
            for t, (ox, oy) in enumerate(others):
                k = 2 * (3 * w + t)
                cp = _remote(_piece(ins[w], name, 2 * ox + oy), land_refs[w].at[t], sem_refs[k], sem_refs[k + 1], (ox, oy, c))
                cp.wait_send()
                cp.wait_recv()

    res = pl.pallas_call(
        body, name=label,
        out_shape=tuple(pltpu.HBM(a.shape, a.dtype) for a in (*parts, *lands)),
        in_specs=[IN_HBM] * (2 * nw) + [SEM] * ns + [pl.BlockSpec(memory_space=pl.ANY)] * len(_several(after)),
        out_specs=tuple([IN_HBM] * (2 * nw)),
        input_output_aliases={i: i for i in range(2 * nw)},
        compiler_params=pltpu.CompilerParams(has_side_effects=DATAFLOW),
    )(*parts, *lands, *sems, *_several(after))
    return list(res[:nw]), list(res[nw:])


def _sum_chips(name, part, got, chip):
    kind, _ = BIG[name]
    _, r, c = got.shape
    t = _tile(r, (128, 352))
    if kind == "col":
        own = pl.BlockSpec((t, c), lambda i, ch: (i, ch[0]))
    else:
        own = pl.BlockSpec((t, c), lambda i, ch: (ch[0] * (r // t) + i, 0))

    def body(ch_ref, p_ref, g_ref, o_ref):
        del ch_ref
        acc = p_ref[...].astype(F32)
        for j in range(N_CHIPS - 1):
            acc = acc + g_ref[j].astype(F32)
        o_ref[...] = acc

    return _pallas(
        body, name="grad_sum_" + name, prefetch=1, grid=(r // t,),
        in_specs=[own, pl.BlockSpec((N_CHIPS - 1, t, c), lambda i, ch: (0, i, 0))],
        out_specs=pl.BlockSpec((t, c), lambda i, ch: (i, 0)),
        out_shape=_sds((r, c), F32), semantics=("parallel",),
    )(chip, part, got)


def _send_halves(sums, label, after):
    nw = len(sums)

    def body(*refs):
        ins, outs = refs[:nw], refs[nw + 1:2 * nw + 1]
        send_sems, recv_sems = refs[2 * nw + 1:]
        _, _, _, _, sibling, _ = _place()
        copies = [_remote(ins[w], outs[w], send_sems.at[w], recv_sems.at[w], sibling) for w in range(nw)]
        for cp in copies:
            cp.start()
        for cp in copies:
            cp.wait()

    return pl.pallas_call(
        body, name=label,
        out_shape=[_sds(a.shape, a.dtype) for a in sums],
        in_specs=[HBM] * (nw + 1), out_specs=[HBM] * nw,
        scratch_shapes=[pltpu.SemaphoreType.DMA((nw,)), pltpu.SemaphoreType.DMA((nw,))],
    )(*sums, after)


EARLY_GRADS = ("w_up", "w_down")
LAST_GRADS = ("w_in", "w_out")


def _reduce_finish(started, names, after, chip, tag):
    sems, parts, lands, _ = started
    parts, lands = _exchange_wait(sems, parts, lands, names, after, "grad_exchange_wait_" + tag)
    return [_sum_chips(n, parts[i], lands[i], chip) for i, n in enumerate(names)]


HI = lax.Precision.HIGHEST
MOD_COLS = 6 * D // N_CHIPS
COND_ROWS = 16


def _silu(v):
    return v * _sigmoid(v)


GATHER_ROWS = 8
FFW_COLS = 2 * DFF // N_CHIPS
CONV_COLS = DC // N_CHIPS
TAPS_PER_ROW = FFW_COLS // CONV_COLS
assert 4 + -(-CW // TAPS_PER_ROW) <= GATHER_ROWS


def _conv_tap_place(k):
    return 4 + k // TAPS_PER_ROW, (k % TAPS_PER_ROW) * CONV_COLS


def _taps_first(a):
    return jnp.transpose(a, (1, 0, 2))


def _pack_cond(c, ffn_w, conv_w):
    def body(c_ref, f_ref, w_ref, o_ref):
        o_ref[...] = jnp.zeros_like(o_ref)
        o_ref[0:1, 0:D] = c_ref[...]
        for k in range(3):
            o_ref[1 + k:2 + k, :] = f_ref[k]
        for k in range(CW):
            row, lane = _conv_tap_place(k)
            o_ref[row:row + 1, lane:lane + CONV_COLS] = w_ref[k]

    return _pallas(body, name="pack_cond", out_shape=_sds((GATHER_ROWS, FFW_COLS), F32))(c, ffn_w, conv_w)


def _unpack_cond(got, c_ctx):
    def body(g_ref, c_ref, cond_ref, f_ref, w_ref):
        cond_ref[...] = jnp.zeros_like(cond_ref)
        for d in range(8):
            cond_ref[d:d + 1, :] = g_ref[d * GATHER_ROWS:d * GATHER_ROWS + 1, 0:D]
        cond_ref[8:9, :] = c_ref[...]
        for j in range(N_CHIPS):
            r0 = 2 * j * GATHER_ROWS
            f_ref[:, j * FFW_COLS:(j + 1) * FFW_COLS] = g_ref[r0 + 1:r0 + 4, :]
            for k in range(CW):
                row, lane = _conv_tap_place(k)
                w_ref[k:k + 1, j * CONV_COLS:(j + 1) * CONV_COLS] = g_ref[r0 + row:r0 + row + 1, lane:lane + CONV_COLS]

    return _pallas(body, name="unpack_cond",
                   out_shape=[_sds((COND_ROWS, D), F32), _sds((3, 2 * DFF), F32), _sds((CW, DC), F32)])(got, c_ctx)


def _chip_cols(rows, width):
    return pl.BlockSpec((rows, width), lambda i, ch: (0, ch[0]))


def _whole(shape):
    return pl.BlockSpec(shape, lambda i, ch: (0,) * len(shape))


def _mod_shard(cond, w_mod, b_mod, chip):
    def body(ch_ref, c_ref, w_ref, b_ref, o_ref):
        del ch_ref
        o_ref[...] = jnp.dot(_silu(c_ref[...]), w_ref[...], preferred_element_type=F32, precision=HI) + b_ref[...]

    return _pallas(body, name="mod_fwd", prefetch=1, grid=(1,),
                   in_specs=[_whole((COND_ROWS, D)), _whole((D, MOD_COLS)), _chip_cols(1, MOD_COLS)],
                   out_specs=_whole((COND_ROWS, MOD_COLS)),
                   out_shape=_sds((COND_ROWS, MOD_COLS), F32))(chip, cond, w_mod, b_mod)


def _unpack_mod(mods, dev):
    def body(dev_ref, m_ref, me_ref, c_ref):
        rowi = lax.broadcasted_iota(jnp.int32, (COND_ROWS, MOD_COLS), 0)
        core = dev_ref[0] % 2
        mine, ctx = [], []
        for j in range(N_CHIPS):
            blk = m_ref[pl.ds(pl.multiple_of((2 * j + core) * COND_ROWS, COND_ROWS), COND_ROWS), :]
            mine.append(jnp.sum(jnp.where(rowi == dev_ref[0], blk, 0.0), axis=0, keepdims=True))
            ctx.append(blk[8:9, :])
        mine = jnp.concatenate(mine, axis=1)
        ctx = jnp.concatenate(ctx, axis=1)
        for k in range(6):
            me_ref[k:k + 1, :] = mine[:, k * D:(k + 1) * D]
        for k in range(2):
            c_ref[k:k + 1, :] = ctx[:, k * D:(k + 1) * D]

    return _pallas(body, name="unpack_mod", prefetch=1, grid=(1,),
                   in_specs=[_whole(mods.shape)], out_specs=[_whole((6, D)), _whole((2, D))],
                   out_shape=[_sds((6, D), F32), _sds((2, D), F32)])(dev, mods)


MOD_TILE = 512


def _mod_weight_update(cond, dmod_all, w, m, v, chip):
    nt = MOD_COLS // MOD_TILE

    def body(ch_ref, c_ref, d_ref, w_ref, m_ref, v_ref, g_ref, dl_ref, nm_ref, nv_ref):
        del ch_ref
        g = lax.dot_general(_silu(c_ref[...]), d_ref[...], _TN, preferred_element_type=F32, precision=HI)
        g_ref[...] = g
        dl_ref[...], nm_ref[...], nv_ref[...] = _adam_math(w_ref[...], g, m_ref[...], v_ref[...])

    blk = pl.BlockSpec((D, MOD_TILE), lambda j, ch: (0, j))
    return _pallas(body, name="mod_weight_update", prefetch=1, grid=(nt,),
                   in_specs=[_whole((COND_ROWS, D)), pl.BlockSpec((COND_ROWS, MOD_TILE), lambda j, ch: (0, ch[0] * nt + j)),
                             blk, blk, blk],
                   out_specs=[blk] * 4, out_shape=[_sds((D, MOD_COLS), F32)] * 4,
                   semantics=("parallel",))(chip, cond, dmod_all, w, m, v)


def _cond_grad_partial(dmod_all, w_mod, chip):
    def body(ch_ref, d_ref, w_ref, o_ref):
        del ch_ref
        o_ref[...] = lax.dot_general(d_ref[...], w_ref[...], (((1,), (1,)), ((), ())), preferred_element_type=F32, precision=HI)

    return _pallas(body, name="cond_grad_partial", prefetch=1, grid=(1,),
                   in_specs=[pl.BlockSpec((8, MOD_COLS), lambda i, ch: (1, ch[0])), _whole((D, MOD_COLS))],
                   out_specs=_whole((8, D)), out_shape=_sds((8, D), F32))(chip, dmod_all, w_mod)


def _adam_math(w, g, m, v):
    nm = ADAM_B1 * m + (1.0 - ADAM_B1) * g
    nv = ADAM_B2 * v + (1.0 - ADAM_B2) * (g * g)
    c1 = 1.0 - ADAM_B1 ** ADAM_STEP
    c2 = 1.0 - ADAM_B2 ** ADAM_STEP
    return -ADAM_LR * ((nm / c1) / (jnp.sqrt(nv / c2) + ADAM_EPS) + ADAM_WD * w), nm, nv


def _cond_update(parts, c_ctx, m, v):
    def body(p_ref, c_ref, m_ref, v_ref, g_ref, d_ref, nm_ref, nv_ref):
        tot = p_ref[0:1, :]
        for j in range(1, N_CHIPS):
            tot = tot + p_ref[16 * j:16 * j + 1, :]
        cv = c_ref[...]
        sg = _sigmoid(cv)
        g = tot * (sg * (1.0 + cv * (1.0 - sg)))
        g_ref[...] = g
        d_ref[...], nm_ref[...], nv_ref[...] = _adam_math(cv, g, m_ref[...], v_ref[...])

    return _pallas(body, name="cond_update", out_shape=[_sds((1, D), F32)] * 4)(parts, c_ctx, m, v)


def _adamw_cols(w, g_all, m, v, chip, name):
    _, r, c = w.shape

    def body(ch_ref, w_ref, g_ref, m_ref, v_ref, go_ref, d_ref, nm_ref, nv_ref):
        del ch_ref
        for k in range(r):
            g = g_ref[k:k + 1, :]
            go_ref[k] = g
            d_ref[k], nm_ref[k], nv_ref[k] = _adam_math(w_ref[k], g, m_ref[k], v_ref[k])

    res = _pallas(body, name=name, prefetch=1, grid=(1,),
                  in_specs=[_whole((r, 1, c)), _chip_cols(r, c), _whole((r, 1, c)), _whole((r, 1, c))],
                  out_specs=[_whole((r, 1, c))] * 4, out_shape=[_sds((r, 1, c), F32)] * 4)(
                      chip, _taps_first(w), g_all, _taps_first(m), _taps_first(v))
    return tuple(jnp.transpose(a, (1, 0, 2)) for a in res)


def _adamw_halves(name, w, own, other, m, v, core, after):
    r, c = w.shape
    half = r // 2
    t = _tile(half, (128, 352))
    nh = half // t

    def pick(mine):
        def index(i, cr):
            first = cr[0] if mine else 1 - cr[0]
            return (jnp.clip(i - first * nh, 0, nh - 1), 0)
        return pl.BlockSpec((t, c), index)

    def body(c_ref, w_ref, own_ref, oth_ref, m_ref, v_ref, after_ref, g_ref, d_ref, nm_ref, nv_ref):
        del after_ref
        g = jnp.where(pl.program_id(0) // nh == c_ref[0], own_ref[...], oth_ref[...])
        g_ref[...] = g
        d_ref[...], nm_ref[...], nv_ref[...] = _adam_math(w_ref[...], g, m_ref[...], v_ref[...])

    blk = pl.BlockSpec((t, c), lambda i, cr: (i, 0))
    return _pallas(body, name="adamw_" + name, prefetch=1, grid=(2 * nh,),
                   in_specs=[blk, pick(True), pick(False), blk, blk, pl.BlockSpec(memory_space=pl.ANY)], out_specs=[blk] * 4,
                   out_shape=[_sds((r, c), F32)] * 4, semantics=("parallel",))(core, w, own, other, m, v, after)


WEIGHTS = ("c_ctx", "w_mod", "b_mod", "g_norm1", "w_in", "rpb", "conv_w", "conv_b", "ln_g", "ln_b", "w_out", "g_norm2",
           "w_up", "ffn_conv_w", "ffn_conv_b", "w_down", "g_final")
PACK = (("dmod", 6 * D), ("dmod_c", 2 * D), ("g_norm1", D), ("g_norm1_ctx", D), ("g_norm2", D), ("g_final", D),
        ("conv_b", DC), ("ln_g", DC), ("ln_b", DC), ("ffn_conv_b", 2 * DFF), ("ffn_conv_w", 3 * 2 * DFF),
        ("conv_w", CW * DC), ("rpb_rev", NH * 16 * LANES), ("loss", LANES))
PACK_OFF = {}
_o = 0
for _n, _w in PACK:
    PACK_OFF[_n] = (_o, _w)
    _o += _w
PACK_N = -(-_o // (8 * LANES)) * (8 * LANES)
VECTORS = {"b_mod": (6 * D, ("dmod", "dmod_c")), "g_norm1": (D, ("g_norm1", "g_norm1_ctx")), "conv_b": (DC, ("conv_b",)),
           "ln_g": (DC, ("ln_g",)), "ln_b": (DC, ("ln_b",)), "g_norm2": (D, ("g_norm2",)),
           "ffn_conv_b": (2 * DFF, ("ffn_conv_b",)), "g_final": (D, ("g_final",))}
RPB_COLS = 4 * NA_ROWS - 1


def _pack_small(parts, after):
    arrs, places = [], []
    for name, _ in PACK:
        off, width = PACK_OFF[name]
        group = parts[name]
        rows = group[0].shape[0]
        row_w = sum(a.shape[1] for a in group)
        assert rows * row_w == width, (name, rows, row_w, width)
        col = 0
        for a in group:
            arrs.append(a)
            places.append([off + k * row_w + col for k in range(rows)])
            col += a.shape[1]

    def body(*refs):
        o_ref = refs[-1]
        o_ref[:, _o:PACK_N] = jnp.zeros((1, PACK_N - _o), F32)
        for ref, offs in zip(refs, places):
            n = ref.shape[1]
            for k, off in enumerate(offs):
                o_ref[:, off:off + n] = ref[k:k + 1, :]

    vmem = pl.BlockSpec(memory_space=pltpu.VMEM)
    return _pallas(body, name="pack_small_grads", out_shape=_sds((1, PACK_N), F32),
                   in_specs=[vmem] * len(arrs) + [pl.BlockSpec(memory_space=pl.ANY)] * len(_several(after)),
                   out_specs=vmem)(*arrs, *_several(after))


def _small_update(packs, w, m, v):
    names = list(VECTORS)

    def body(*refs):
        it = iter(refs)
        p_ref = next(it)
        wmv = {n: (next(it), next(it), next(it)) for n in names}
        outs = {n: (next(it), next(it), next(it), next(it)) for n in names}
        dmod_ref, cw_ref, fw_ref, rpb_ref, loss_ref = next(it), next(it), next(it), next(it), next(it)

        def total(name):
            off, width = PACK_OFF[name]
            acc = p_ref[0:1, off:off + width]
            for d in range(1, 8):
                acc = acc + p_ref[d:d + 1, off:off + width]
            return acc

        for n in names:
            width, segs = VECTORS[n]
            g = total(segs[0])
            if len(segs) > 1:
                extra = total(segs[1])
                ew = extra.shape[1]
                g = g + extra if ew == width else jnp.concatenate([g[:, :ew] + extra, g[:, ew:]], axis=1)
            w_ref, m_ref, v_ref = wmv[n]
            g_ref, d_ref, nm_ref, nv_ref = outs[n]
            g_ref[...] = g
            d_ref[...], nm_ref[...], nv_ref[...] = _adam_math(w_ref[...], g, m_ref[...], v_ref[...])

        o_dmod = PACK_OFF["dmod"][0]
        dmod_ref[...] = jnp.zeros_like(dmod_ref)
        dmod_ref[0:8, :] = p_ref[:, o_dmod:o_dmod + 6 * D]
        dmod_ref[8:9, 0:2 * D] = total("dmod_c")
        for ref, name, rows in ((cw_ref, "conv_w", CW), (fw_ref, "ffn_conv_w", 3), (rpb_ref, "rpb_rev", NH * 16)):
            flat = total(name)
            n = ref.shape[1]
            for k in range(rows):
                ref[k:k + 1, :] = flat[:, k * n:(k + 1) * n]
        loss_ref[...] = total("loss")

    ins = [packs] + [a[n] for n in names for a in (w, m, v)]
    out_shape = [_sds((1, VECTORS[n][0]), F32) for n in names for _ in range(4)]
    out_shape += [_sds((COND_ROWS, 6 * D), F32), _sds((CW, DC), F32), _sds((3, 2 * DFF), F32), _sds((NH * 16, LANES), F32),
                  _sds((1, LANES), F32)]
    res = _pallas(body, name="small_update", out_shape=out_shape)(*ins)
    per = {n: tuple(res[4 * i:4 * i + 4]) for i, n in enumerate(names)}
    return (per, *res[4 * len(names):])


def _rpb_update(rev, w, m, v):
    nr = 2 * NA_ROWS - 1
    heads_inside = lambda a: jnp.transpose(a, (0, 2, 1, 3))

    def body(r_ref, w_ref, m_ref, v_ref, g_ref, d_ref, nm_ref, nv_ref):
        li = lax.broadcasted_iota(jnp.int32, (LANES, LANES), 0)
        co = lax.broadcasted_iota(jnp.int32, (LANES, LANES), 1)
        lane_of_co0 = GW - 1 + RPB_COLS // 2
        unflip = jnp.where((li == lane_of_co0 - co) & (co < RPB_COLS), 1.0, 0.0).astype(F32)
        assert NH & (NH - 1) == 0
        regroup = jnp.where((co == (li & (NH - 1)) * 16 + (li >> (NH.bit_length() - 1))) & (li < nr * NH),
                            1.0, 0.0).astype(F32)
        g_all = jnp.dot(jnp.dot(regroup, r_ref[...], preferred_element_type=F32, precision=HI), unflip,
                        preferred_element_type=F32, precision=HI)
        for ro in range(nr):
            g = g_all[ro * NH:(ro + 1) * NH, 0:RPB_COLS]
            g_ref[0, ro] = g
            d_ref[0, ro], nm_ref[0, ro], nv_ref[0, ro] = _adam_math(w_ref[0, ro], g, m_ref[0, ro], v_ref[0, ro])

    res = _pallas(body, name="rpb_update", out_shape=[_sds((1, nr, NH, RPB_COLS), F32)] * 4)(
        rev, heads_inside(w), heads_inside(m), heads_inside(v))
    return tuple(heads_inside(a) for a in res)


def kernel(x, c, ctx, c_ctx, w_mod, b_mod, g_norm1, w_in, rpb, conv_w, conv_b, ln_g, ln_b, w_out, g_norm2, w_up, ffn_conv_w, ffn_conv_b, w_down, g_final, loss_target, m_c_ctx, m_w_mod, m_b_mod, m_g_norm1, m_w_in, m_rpb, m_conv_w, m_conv_b, m_ln_g, m_ln_b, m_w_out, m_g_norm2, m_w_up, m_ffn_conv_w, m_ffn_conv_b, m_w_down, m_g_final, v_c_ctx, v_w_mod, v_b_mod, v_g_norm1, v_w_in, v_rpb, v_conv_w, v_conv_b, v_ln_g, v_ln_b, v_w_out, v_g_norm2, v_w_up, v_ffn_conv_w, v_ffn_conv_b, v_w_down, v_g_final):
    w = dict(c_ctx=c_ctx, w_mod=w_mod, b_mod=b_mod, g_norm1=g_norm1, w_in=w_in, rpb=rpb, conv_w=conv_w, conv_b=conv_b,
             ln_g=ln_g, ln_b=ln_b, w_out=w_out, g_norm2=g_norm2, w_up=w_up, ffn_conv_w=ffn_conv_w, ffn_conv_b=ffn_conv_b,
             w_down=w_down, g_final=g_final)
    mom = dict(c_ctx=m_c_ctx, w_mod=m_w_mod, b_mod=m_b_mod, g_norm1=m_g_norm1, w_in=m_w_in, rpb=m_rpb, conv_w=m_conv_w,
               conv_b=m_conv_b, ln_g=m_ln_g, ln_b=m_ln_b, w_out=m_w_out, g_norm2=m_g_norm2, w_up=m_w_up,
               ffn_conv_w=m_ffn_conv_w, ffn_conv_b=m_ffn_conv_b, w_down=m_w_down, g_final=m_g_final)
    var = dict(c_ctx=v_c_ctx, w_mod=v_w_mod, b_mod=v_b_mod, g_norm1=v_g_norm1, w_in=v_w_in, rpb=v_rpb, conv_w=v_conv_w,
               conv_b=v_conv_b, ln_g=v_ln_g, ln_b=v_ln_b, w_out=v_w_out, g_norm2=v_g_norm2, w_up=v_w_up,
               ffn_conv_w=v_ffn_conv_w, ffn_conv_b=v_ffn_conv_b, w_down=v_w_down, g_final=v_g_final)
    xi, yi, ci = lax.axis_index("x"), lax.axis_index("y"), lax.axis_index("c")
    dev = (4 * xi + 2 * yi + ci).astype(jnp.int32).reshape(1)
    chip = (2 * xi + yi).astype(jnp.int32).reshape(1)
    core = ci.astype(jnp.int32).reshape(1)
    c_ctx2 = c_ctx.reshape(1, D)
    g_final2 = g_final.reshape(1, D)
    mom["g_final"], var["g_final"] = m_g_final.reshape(1, D), v_g_final.reshape(1, D)

    sharing_cond = _share_start(_pack_cond(c, _taps_first(ffn_conv_w), _taps_first(conv_w)), "cond", after=[])
    shards = {n: _cast_into_whole(n, w[n][0], chip) for n in BIG_NAMES}
    cond, ffn_w_all, conv_w_all = _unpack_cond(_share_wait(sharing_cond, list(shards.values()), "cond"), c_ctx2)

    sharing_mod = _share_start(_mod_shard(cond, w_mod[0], b_mod, chip), "mod", after=[], flips=OTHER_CHIPS)

    sems_in, first, _ = _gather_start([shards["w_in"]], ("w_in",), sharing_mod[2], "w_in")
    mod_me, mod_c = _unpack_mod(_share_wait(sharing_mod, [], "mod"), dev)
    late_started = []

    def w_in_all(after):
        arrived = _gather_wait(sems_in, first, ("w_in",), after, "w_in")
        late_started.append(_gather_start([shards[n] for n in LATE_NAMES], LATE_NAMES, arrived[0], "late"))
        return _forward_halves(list(arrived), ("w_in",), "w_in")[0]

    def late_weights(after):
        sems, late, _ = late_started.pop()
        arrived = list(_gather_wait(sems, late, LATE_NAMES, after, "late"))
        (w_out_all,) = _forward_halves(arrived[:1], LATE_NAMES[:1], "w_out")
        fsems, passing, _ = _forward_start(arrived[1:], LATE_NAMES[1:], "ffn", after=w_out_all)
        return w_out_all, lambda after2: _forward_wait(fsems, passing, LATE_NAMES[1:], after2, "ffn")

    rpb_rev = jnp.pad(rpb[0][:, :, ::-1], ((0, 0), (0, 1), (48, LANES - 48 - RPB_COLS))).reshape(NH * 16, LANES)
    vec = dict(g_norm1=g_norm1, g_norm2=g_norm2, g_final=g_final2, conv_w=conv_w_all, conv_b=conv_b, ln_g=ln_g, ln_b=ln_b,
               ffn_conv_w=ffn_w_all, ffn_conv_b=ffn_conv_b)
    started = []

    def begin_early(d_up, d_down):
        started.append(_swap_start([d_up, d_down], EARLY_GRADS, "grad_swap_start_early"))

    def carry_on_early(after):
        sems_, grads_, lands_, _ = started.pop()
        grads_, lands_ = _swap_wait(sems_, grads_, lands_, EARLY_GRADS, after, "grad_swap_wait_early")
        parts_ = [_add_halves(n, grads_[i], lands_[i], core) for i, n in enumerate(EARLY_GRADS)]
        started.append(_exchange_start(parts_, EARLY_GRADS, "grad_exchange_start_early"))
        return started[0][3][0:1, 0:1]

    loss_p, grad_x, d_in, d_out, d_up, d_down, small = _local_step(
        x[0], ctx[0], loss_target[0], mod_me, mod_c, vec, w_in_all, late_weights, rpb_rev, (begin_early, carry_on_early))

    out = {}
    sems_, grads_, lands_, _ = _swap_start([d_in, d_out], LAST_GRADS, "grad_swap_start_last")
    early_own = _reduce_finish(started[0], EARLY_GRADS, grad_x, chip, "early")
    parts = dict(dmod=small["dmod"], dmod_c=small["dmod_c"], g_norm1=[small["g_norm1"][0]], g_norm1_ctx=[small["g_norm1"][1]],
                 g_norm2=[small["g_norm2"]], g_final=[small["g_final"]], conv_b=[small["conv_b"]], ln_g=[small["ln_g"]],
                 ln_b=[small["ln_b"]], ffn_conv_b=small["ffn_conv_b"], ffn_conv_w=small["ffn_conv_w"],
                 conv_w=[small["conv_w"]], rpb_rev=[small["rpb_rev"]], loss=[loss_p])
    pack = _pack_small(parts, after=early_own).reshape(8, PACK_N // 8)
    sharing = _share_start(pack, "small_grads", after=[])
    grads_, lands_ = _swap_wait(sems_, grads_, lands_, LAST_GRADS, sharing[2], "grad_swap_wait_last")
    parts_ = [_add_halves(n, grads_[i], lands_[i], core) for i, n in enumerate(LAST_GRADS)]
    last_started = _exchange_start(parts_, LAST_GRADS, "grad_exchange_start_last")
    early_other = _send_halves(early_own, "grad_send_early", after=last_started[3])
    for i, n in enumerate(EARLY_GRADS):
        out[n] = _adamw_halves(n, w[n][0], early_own[i], early_other[i], mom[n][0], var[n][0], core, early_other[i])

    packs = _share_wait(sharing, [out[n][1] for n in EARLY_GRADS], "small_grads").reshape(8, PACK_N)
    w2 = dict(w, g_final=g_final2)
    per, dmod_all, g_conv_w_all, g_ffn_w_all, g_rpb_rev, loss_row = _small_update(packs, w2, mom, var)
    out.update(per)
    out["w_mod"] = _mod_weight_update(cond, dmod_all, w_mod[0], m_w_mod[0], v_w_mod[0], chip)

    sharing_c = _share_start(_cond_grad_partial(dmod_all, w_mod[0], chip), "cond_grad", after=out["w_mod"][1])
    last_own = _reduce_finish(last_started, LAST_GRADS, sharing_c[2], chip, "last")
    last_other = _send_halves(last_own, "grad_send_last", after=last_own[0])
    for i, n in enumerate(LAST_GRADS):
        out[n] = _adamw_halves(n, w[n][0], last_own[i], last_other[i], mom[n][0], var[n][0], core, last_other[i])
    out["c_ctx"] = _cond_update(_share_wait(sharing_c, [out[n][1] for n in LAST_GRADS], "cond_grad"),
                                c_ctx2, m_c_ctx.reshape(1, D), v_c_ctx.reshape(1, D))
    out["conv_w"] = _adamw_cols(conv_w, g_conv_w_all, m_conv_w, v_conv_w, chip, "adamw_conv_w")
    out["ffn_conv_w"] = _adamw_cols(ffn_conv_w, g_ffn_w_all, m_ffn_conv_w, v_ffn_conv_w, chip, "adamw_ffn_conv_w")
    out["rpb"] = _rpb_update(g_rpb_rev, rpb, m_rpb, v_rpb)

    res = [[out[n][k].reshape(w[n].shape) for n in WEIGHTS] for k in range(4)]
    return (loss_row[0, 0], grad_x[None], *res[0], *res[1], *res[2], *res[3])
```

```python
import jax
import jax.numpy as jnp
from jax import lax
from jax.experimental import pallas as pl
from jax.experimental.pallas import tpu as pltpu

F32 = jnp.float32
BF16 = jnp.bfloat16
MXU_DTYPE = jnp.bfloat16

D = 1024
CTX = 256
GW = 64
DA = 512
NH = 8
HD = 64
DC = 512
CW = 31
DFF = 2816
NIN = 3 * DA + 2 * DC
EPS = 1e-6
SCALE = HD ** -0.5
NEG = -1e30
NA_ROWS = 8
PAIR_ROWS = NA_ROWS + 1
TAB_BLOCKS = 17
LANES = 128
VMEM_LIMIT = 56 * 1024 * 1024

ADAM_LR = 0.001
ADAM_B1 = 0.9
ADAM_B2 = 0.999
ADAM_EPS = 1e-08
ADAM_WD = 0.01
ADAM_STEP = 10

MESH = pl.DeviceIdType.MESH


def _pallas(body, *, name, semantics=None, vmem=VMEM_LIMIT, prefetch=0, **kw):
    params = dict(vmem_limit_bytes=vmem)
    if semantics is not None:
        params["dimension_semantics"] = semantics
    if prefetch:
        kw["grid_spec"] = pltpu.PrefetchScalarGridSpec(
            num_scalar_prefetch=prefetch, grid=kw.pop("grid"), in_specs=kw.pop("in_specs"), out_specs=kw.pop("out_specs"),
            scratch_shapes=kw.pop("scratch_shapes", ()))
    return pl.pallas_call(body, name=name, compiler_params=pltpu.CompilerParams(**params), **kw)


def _sds(shape, dtype):
    return jax.ShapeDtypeStruct(shape, dtype)


def _vec_spec(n):
    return pl.BlockSpec((1, n), lambda *_: (0, 0))


def _colsum8(x):
    t, n = x.shape
    return jnp.sum(x.reshape(t // 8, 8, n), axis=0)


def _sigmoid(x):
    return 0.5 * jnp.tanh(0.5 * x) + 0.5


def _mm(a, b, *, mode, m, n, k, tm, tn, tk, out_dtype, name, a_off=(0, 0), b_off=(0, 0),
        out_total=None, o_off=(0, 0), into=None):
    a_list = list(a) if isinstance(a, (list, tuple)) else [a]
    b_list = list(b) if isinstance(b, (list, tuple)) else [b]
    assert m % tm == 0 and n % tn == 0 and k % tk == 0, (name, m, n, k, tm, tn, tk)
    gi, gj, nk = m // tm, n // tn, k // tk
    dims = {"nn": (((1,), (0,)), ((), ())), "nt": (((1,), (1,)), ((), ())), "tn": (((0,), (0,)), ((), ()))}[mode]

    if len(a_list) > 1:
        assert mode != "tn" and nk == 1 and sum(x.shape[1] for x in a_list) == k
        a_specs = [pl.BlockSpec((tm, x.shape[1]), lambda i, j, kk: (i, 0)) for x in a_list]
    elif mode == "tn":
        a_specs = [pl.BlockSpec((tk, tm), lambda i, j, kk: (kk + a_off[0], i + a_off[1]))]
    else:
        a_specs = [pl.BlockSpec((tm, tk), lambda i, j, kk: (i + a_off[0], kk + a_off[1]))]
    if len(b_list) > 1:
        assert mode == "tn" and gj == 1 and sum(x.shape[1] for x in b_list) == n
        b_specs = [pl.BlockSpec((tk, x.shape[1]), lambda i, j, kk: (kk, 0)) for x in b_list]
    elif mode == "nt":
        b_specs = [pl.BlockSpec((tn, tk), lambda i, j, kk: (j + b_off[0], kk + b_off[1]))]
    else:
        b_specs = [pl.BlockSpec((tk, tn), lambda i, j, kk: (kk + b_off[0], j + b_off[1]))]

    na, nb = len(a_list), len(b_list)
    in_place = nk > 1 and out_dtype == F32
    n_in = na + nb + (into is not None)

    def body(*refs):
        a_refs, b_refs, o_ref = refs[:na], refs[na:na + nb], refs[n_in]
        acc = o_ref if in_place else (refs[n_in + 1] if nk > 1 else None)
        kk = pl.program_id(2)

        def whole(piece_refs):
            vals = [r[...].astype(MXU_DTYPE) for r in piece_refs]
            return vals[0] if len(vals) == 1 else jnp.concatenate(vals, axis=1)

        p = lax.dot_general(whole(a_refs), whole(b_refs), dims, preferred_element_type=F32)
        if nk == 1:
            o_ref[...] = p.astype(out_dtype)
            return

        @pl.when(kk == 0)
        def _():
            acc[...] = p

        @pl.when(kk > 0)
        def _():
            acc[...] += p

        if not in_place:
            @pl.when(kk == nk - 1)
            def _():
                o_ref[...] = acc[...].astype(out_dtype)

    ins = [*a_list, *b_list]
    in_specs = a_specs + b_specs
    extra = {}
    if into is not None:
        extra["input_output_aliases"] = {len(ins): 0}
        ins.append(into)
        in_specs.append(pl.BlockSpec(memory_space=pl.ANY))
    return _pallas(
        body, name=name, grid=(gi, gj, nk), in_specs=in_specs,
        out_specs=pl.BlockSpec((tm, tn), lambda i, j, kk: (i + o_off[0], j + o_off[1])),
        out_shape=_sds(out_total or (m, n), out_dtype),
        scratch_shapes=[pltpu.VMEM((tm, tn), F32)] if nk > 1 and not in_place else [],
        semantics=("parallel", "parallel", "arbitrary"), **extra,
    )(*ins)


ROW_TILE = 256


def _row_tile(s, most=2):
    for k in (4, 2):
        if k <= most and s % (k * ROW_TILE) == 0:
            return k * ROW_TILE
    return ROW_TILE


def _rmsmod_fwd(x, ctx, g, sc, sh, csc, csh):
    s = x.shape[0]
    nt = s // ROW_TILE
    assert ctx.shape[0] == ROW_TILE

    def body(x_ref, c_ref, g_ref, sc_ref, sh_ref, csc_ref, csh_ref, o_ref):
        is_ctx = pl.program_id(0) == nt
        xv = jnp.where(is_ctx, c_ref[...], x_ref[...])
        scv = jnp.where(is_ctx, csc_ref[...], sc_ref[...])
        shv = jnp.where(is_ctx, csh_ref[...], sh_ref[...])
        r = lax.rsqrt(jnp.mean(xv * xv, axis=-1, keepdims=True) + EPS)
        y = xv * r * g_ref[...]
        o_ref[...] = (y * (1.0 + scv) + shv).astype(o_ref.dtype)

    return _pallas(
        body, name="rmsmod1_fwd", grid=(nt + 1,),
        in_specs=[pl.BlockSpec((ROW_TILE, D), lambda i: (jnp.minimum(i, nt - 1), 0)),
                  pl.BlockSpec((ROW_TILE, D), lambda i: (0, 0))] + [_vec_spec(D)] * 5,
        out_specs=pl.BlockSpec((ROW_TILE, D), lambda i: (i, 0)),
        out_shape=_sds((s + CTX, D), MXU_DTYPE),
        semantics=("arbitrary",),
    )(x, ctx, g, sc, sh, csc, csh)


def _resid_rmsmod_fwd(x, y, gt, g, sc, sh):
    s = x.shape[0]

    def body(x_ref, y_ref, gt_ref, g_ref, sc_ref, sh_ref, x1_ref, h_ref):
        x1 = x_ref[...] + gt_ref[...] * y_ref[...]
        x1_ref[...] = x1
        r = lax.rsqrt(jnp.mean(x1 * x1, axis=-1, keepdims=True) + EPS)
        h_ref[...] = ((x1 * r * g_ref[...]) * (1.0 + sc_ref[...]) + sh_ref[...]).astype(h_ref.dtype)

    t = _row_tile(s)
    row = pl.BlockSpec((t, D), lambda i: (i, 0))
    return _pallas(
        body, name="resid_rmsmod2_fwd", grid=(s // t,),
        in_specs=[row, row] + [_vec_spec(D)] * 4,
        out_specs=[row, row],
        out_shape=[_sds((s, D), F32), _sds((s, D), MXU_DTYPE)],
        semantics=("parallel",),
    )(x, y, gt, g, sc, sh)


def _final_fwd_bwd(x1, z, gt2, gf, tgt):
    s = x1.shape[0]
    tile = _row_tile(s)
    nt = s // tile

    def body(x1_ref, z_ref, gt_ref, gf_ref, t_ref, dx2_ref, dz_ref, loss_ref, dgt_ref, dgf_ref, a_loss, a_gt, a_gf):
        i = pl.program_id(0)

        @pl.when(i == 0)
        def _():
            a_loss[...] = jnp.zeros_like(a_loss)
            a_gt[...] = jnp.zeros_like(a_gt)
            a_gf[...] = jnp.zeros_like(a_gf)

        zv = z_ref[...]
        gt = gt_ref[...]
        gf_ = gf_ref[...]
        x2 = x1_ref[...] + gt * zv
        r = lax.rsqrt(jnp.mean(x2 * x2, axis=-1, keepdims=True) + EPS)
        xn = x2 * r
        e = xn * gf_ - t_ref[...]
        a_loss[...] += _colsum8(e * e)
        dyo = e * (1.0 / D)
        a_gf[...] += _colsum8(dyo * xn)
        gdy = gf_ * dyo
        dx2 = r * gdy - xn * (r * r) * jnp.mean(x2 * gdy, axis=-1, keepdims=True)
        dx2_ref[...] = dx2
        dz_ref[...] = (gt * dx2).astype(dz_ref.dtype)
        a_gt[...] += _colsum8(dx2 * zv)

        @pl.when(i == nt - 1)
        def _():
            tot = jnp.sum(jnp.sum(a_loss[...], axis=0, keepdims=True), axis=1, keepdims=True) * (0.5 / D)
            loss_ref[...] = jnp.broadcast_to(tot, loss_ref.shape)
            dgt_ref[...] = jnp.sum(a_gt[...], axis=0, keepdims=True)
            dgf_ref[...] = jnp.sum(a_gf[...], axis=0, keepdims=True)

    row = pl.BlockSpec((tile, D), lambda i: (i, 0))
    return _pallas(
        body, name="final_norm_loss", grid=(nt,),
        in_specs=[row, row, _vec_spec(D), _vec_spec(D), row],
        out_specs=[row, row, _vec_spec(LANES), _vec_spec(D), _vec_spec(D)],
        out_shape=[_sds((s, D), F32), _sds((s, D), MXU_DTYPE), _sds((1, LANES), F32), _sds((1, D), F32), _sds((1, D), F32)],
        scratch_shapes=[pltpu.VMEM((8, D), F32)] * 3,
        semantics=("arbitrary",),
    )(x1, z, gt2, gf, tgt)


def _rmsmod_bwd(xin, dh, g, sc, *, name, dh_row0=0, add=None, resid=None):
    s = xin.shape[0]
    tile = _row_tile(s)
    nt = s // tile
    want_dx = add is not None
    assert resid is None or want_dx

    def body(*refs):
        it = iter(refs)
        x_ref, dh_ref, g_ref, sc_ref = next(it), next(it), next(it), next(it)
        add_ref = next(it) if want_dx else None
        gt_ref, y_ref = (next(it), next(it)) if resid is not None else (None, None)
        dsh_ref, dsc_ref, dg_ref = next(it), next(it), next(it)
        dx_ref = next(it) if want_dx else None
        dy_ref, dgt_ref = (next(it), next(it)) if resid is not None else (None, None)
        a_sh, a_sc, a_g = next(it), next(it), next(it)
        a_gt = next(it) if resid is not None else None
        i = pl.program_id(0)

        @pl.when(i == 0)
        def _():
            a_sh[...] = jnp.zeros_like(a_sh)
            a_sc[...] = jnp.zeros_like(a_sc)
            a_g[...] = jnp.zeros_like(a_g)
            if a_gt is not None:
                a_gt[...] = jnp.zeros_like(a_gt)

        xv = x_ref[...]
        dhv = dh_ref[...]
        gv = g_ref[...]
        r = lax.rsqrt(jnp.mean(xv * xv, axis=-1, keepdims=True) + EPS)
        xn = xv * r
        a_sh[...] += _colsum8(dhv)
        a_sc[...] += _colsum8(dhv * (xn * gv))
        dn = dhv * (1.0 + sc_ref[...])
        a_g[...] += _colsum8(dn * xn)
        if want_dx:
            gdn = gv * dn
            dx = add_ref[...] + r * gdn - xn * (r * r) * jnp.mean(xv * gdn, axis=-1, keepdims=True)
            dx_ref[...] = dx
            if resid is not None:
                dy_ref[...] = (gt_ref[...] * dx).astype(dy_ref.dtype)
                a_gt[...] += _colsum8(dx * y_ref[...])

        @pl.when(i == nt - 1)
        def _():
            dsh_ref[...] = jnp.sum(a_sh[...], axis=0, keepdims=True)
            dsc_ref[...] = jnp.sum(a_sc[...], axis=0, keepdims=True)
            dg_ref[...] = jnp.sum(a_g[...], axis=0, keepdims=True)
            if a_gt is not None:
                dgt_ref[...] = jnp.sum(a_gt[...], axis=0, keepdims=True)

    row = pl.BlockSpec((tile, D), lambda i: (i, 0))
    ins = [xin, dh, g, sc]
    in_specs = [row, pl.BlockSpec((tile, D), lambda i: (i + dh_row0 // tile, 0)), _vec_spec(D), _vec_spec(D)]
    out_specs = [_vec_spec(D)] * 3
    out_shape = [_sds((1, D), F32)] * 3
    scratch = [pltpu.VMEM((8, D), F32)] * 3
    if want_dx:
        ins.append(add)
        in_specs.append(row)
        out_specs.append(row)
        out_shape.append(_sds((s, D), F32))
    if resid is not None:
        ins += [resid[0], resid[1]]
        in_specs += [_vec_spec(D), row]
        out_specs += [row, _vec_spec(D)]
        out_shape += [_sds((s, D), MXU_DTYPE), _sds((1, D), F32)]
        scratch.append(pltpu.VMEM((8, D), F32))
    return _pallas(body, name=name, grid=(nt,), in_specs=in_specs, out_specs=out_specs, out_shape=out_shape,
                   scratch_shapes=scratch, semantics=("arbitrary",))(*ins)


FF_TILE = 128
FF_CHUNK = 128
HALO = 8


def _shift3(pad_ref, r0, ch):
    return tuple(pad_ref[pl.ds(r0 + HALO + d, ch), :] for d in (-1, 0, 1))


def _fill_padded(pad_ref, src_ref, s, ch, halo):
    zeros = jnp.zeros((halo, pad_ref.shape[1]), F32)
    pad_ref[0:halo, :] = zeros
    pad_ref[s + halo:s + 2 * halo, :] = zeros

    def cp(c, carry):
        r0 = pl.multiple_of(c * ch, ch)
        pad_ref[pl.ds(r0 + halo, ch), :] = src_ref[pl.ds(r0, ch), :].astype(F32)
        return carry

    lax.fori_loop(0, s // ch, cp, 0)


def _ffn_act_fwd(u, w, b):
    s = u.shape[0]
    tile, ch = FF_TILE, FF_CHUNK
    nj = DFF // tile

    def body(ug_ref, uv_ref, wg_ref, wv_ref, bg_ref, bv_ref, f_ref, gpad, vpad):
        _fill_padded(gpad, ug_ref, s, ch, HALO)
        _fill_padded(vpad, uv_ref, s, ch, HALO)

        def conv(pad, w_ref, b_ref, r0):
            prev, cur, nxt = _shift3(pad, r0, ch)
            return w_ref[0:1, :] * prev + w_ref[1:2, :] * cur + w_ref[2:3, :] * nxt + b_ref[...]

        def step(c, carry):
            r0 = pl.multiple_of(c * ch, ch)
            gc = conv(gpad, wg_ref, bg_ref, r0)
            vc = conv(vpad, wv_ref, bv_ref, r0)
            f_ref[pl.ds(r0, ch), :] = (gc * _sigmoid(gc) * vc).astype(f_ref.dtype)
            return carry

        lax.fori_loop(0, s // ch, step, 0)

    col = lambda off: pl.BlockSpec((s, tile), lambda j: (0, j + off))
    wsp = lambda off: pl.BlockSpec((3, tile), lambda j: (0, j + off))
    bsp = lambda off: pl.BlockSpec((1, tile), lambda j: (0, j + off))
    return _pallas(
        body, name="ffn_act_fwd", grid=(nj,),
        in_specs=[col(0), col(nj), wsp(0), wsp(nj), bsp(0), bsp(nj)],
        out_specs=col(0), out_shape=_sds((s, DFF), MXU_DTYPE),
        scratch_shapes=[pltpu.VMEM((s + 2 * HALO, tile), F32)] * 2,
        semantics=("parallel",),
    )(u, u, w, w, b, b)


def _ffn_act_bwd(u, df, w, b):
    s = u.shape[0]
    nj = DFF // FF_TILE
    ch = FF_CHUNK

    def body(ug_ref, uv_ref, df_ref, wg_ref, wv_ref, bg_ref, bv_ref,
             dug_ref, duv_ref, dwg_ref, dwv_ref, dbg_ref, dbv_ref, gpad, vpad, dgpad, dvpad, acc):
        _fill_padded(gpad, ug_ref, s, ch, HALO)
        _fill_padded(vpad, uv_ref, s, ch, HALO)
        zeros = jnp.zeros((HALO, FF_TILE), F32)
        for p in (dgpad, dvpad):
            p[0:HALO, :] = zeros
            p[s + HALO:s + 2 * HALO, :] = zeros
        acc[...] = jnp.zeros_like(acc)

        def step(c, carry):
            r0 = pl.multiple_of(c * ch, ch)
            gs = _shift3(gpad, r0, ch)
            vs = _shift3(vpad, r0, ch)
            gc = wg_ref[0:1, :] * gs[0] + wg_ref[1:2, :] * gs[1] + wg_ref[2:3, :] * gs[2] + bg_ref[...]
            vc = wv_ref[0:1, :] * vs[0] + wv_ref[1:2, :] * vs[1] + wv_ref[2:3, :] * vs[2] + bv_ref[...]
            sg = _sigmoid(gc)
            dfv = df_ref[pl.ds(r0, ch), :].astype(F32)
            dgc = dfv * vc * (sg * (1.0 + gc * (1.0 - sg)))
            dvc = dfv * (gc * sg)
            dgpad[pl.ds(r0 + HALO, ch), :] = dgc
            dvpad[pl.ds(r0 + HALO, ch), :] = dvc
            for t in range(3):
                acc[8 * t:8 * t + 8, :] += _colsum8(dgc * gs[t])
                acc[24 + 8 * t:32 + 8 * t, :] += _colsum8(dvc * vs[t])
            acc[48:56, :] += _colsum8(dgc)
            acc[56:64, :] += _colsum8(dvc)
            return carry

        lax.fori_loop(0, s // ch, step, 0)

        def step2(c, carry):
            r0 = pl.multiple_of(c * ch, ch)
            for pad, w_ref, o_ref in ((dgpad, wg_ref, dug_ref), (dvpad, wv_ref, duv_ref)):
                prev, cur, nxt = _shift3(pad, r0, ch)
                o_ref[pl.ds(r0, ch), :] = (w_ref[0:1, :] * nxt + w_ref[1:2, :] * cur + w_ref[2:3, :] * prev).astype(o_ref.dtype)
            return carry

        lax.fori_loop(0, s // ch, step2, 0)
        for t in range(3):
            dwg_ref[t:t + 1, :] = jnp.sum(acc[8 * t:8 * t + 8, :], axis=0, keepdims=True)
            dwv_ref[t:t + 1, :] = jnp.sum(acc[24 + 8 * t:32 + 8 * t, :], axis=0, keepdims=True)
        dbg_ref[...] = jnp.sum(acc[48:56, :], axis=0, keepdims=True)
        dbv_ref[...] = jnp.sum(acc[56:64, :], axis=0, keepdims=True)

    col = lambda off: pl.BlockSpec((s, FF_TILE), lambda j: (0, j + off))
    wsp = lambda off: pl.BlockSpec((3, FF_TILE), lambda j: (0, j + off))
    bsp = lambda off: pl.BlockSpec((1, FF_TILE), lambda j: (0, j + off))
    return _pallas(
        body, name="ffn_act_bwd", grid=(nj,),
        in_specs=[col(0), col(nj), col(0), wsp(0), wsp(nj), bsp(0), bsp(nj)],
        out_specs=[col(0), col(0), wsp(0), wsp(0), bsp(0), bsp(0)],
        out_shape=[_sds((s, DFF), MXU_DTYPE)] * 2 + [_sds((3, DFF), F32)] * 2 + [_sds((1, DFF), F32)] * 2,
        scratch_shapes=[pltpu.VMEM((s + 2 * HALO, FF_TILE), F32)] * 4 + [pltpu.VMEM((64, FF_TILE), F32)],
        semantics=("parallel",),
    )(u, u, df, w, w, b, b)


CONV_CHUNK = 64
CONV_HALO = 16


def _tap(pad_ref, r0, k):
    return pad_ref[pl.ds(r0 + CONV_HALO - CW // 2 + k, CONV_CHUNK), :]


def _glu_into(pad_ref, a_ref, g_ref, s):
    zeros = jnp.zeros((CONV_HALO, LANES), F32)
    pad_ref[0:CONV_HALO, :] = zeros
    pad_ref[s + CONV_HALO:s + 2 * CONV_HALO, :] = zeros

    def cp(c, carry):
        r0 = pl.multiple_of(c * ROW_TILE, ROW_TILE)
        pad_ref[pl.ds(r0 + CONV_HALO, ROW_TILE), :] = a_ref[pl.ds(r0, ROW_TILE), :] * _sigmoid(g_ref[pl.ds(r0, ROW_TILE), :])
        return carry

    lax.fori_loop(0, s // ROW_TILE, cp, 0)


def _conf_conv_fwd(ag, conv_w, conv_b):
    s = ag.shape[0]
    nc = DC // LANES

    def body(a_ref, g_ref, w_ref, b_ref, o_ref, upad):
        _glu_into(upad, a_ref, g_ref, s)

        def step(c, carry):
            r0 = pl.multiple_of(c * CONV_CHUNK, CONV_CHUNK)
            acc = jnp.broadcast_to(b_ref[...], (CONV_CHUNK, LANES))
            for k in range(CW):
                acc = acc + w_ref[k:k + 1, :] * _tap(upad, r0, k)
            o_ref[pl.ds(r0, CONV_CHUNK), :] = acc
            return carry

        lax.fori_loop(0, s // CONV_CHUNK, step, 0)

    col = lambda off: pl.BlockSpec((s, LANES), lambda c: (0, c + off))
    return _pallas(
        body, name="conf_conv_fwd", grid=(nc,),
        in_specs=[col(0), col(nc), pl.BlockSpec((CW, LANES), lambda c: (0, c)), pl.BlockSpec((1, LANES), lambda c: (0, c))],
        out_specs=col(0), out_shape=_sds((s, DC), F32),
        scratch_shapes=[pltpu.VMEM((s + 2 * CONV_HALO, LANES), F32)],
        semantics=("parallel",),
    )(ag, ag, conv_w, conv_b)


def _ln_stats(x):
    mu = jnp.mean(x, axis=-1, keepdims=True)
    xc = x - mu
    var = jnp.mean(xc * xc, axis=-1, keepdims=True)
    rstd = lax.rsqrt(var + EPS)
    return xc * rstd, rstd


def _conf_ln_fwd(u1, ln_g, ln_b, ycat):
    s = u1.shape[0]

    def body(u_ref, g_ref, b_ref, ycat_ref, o_ref):
        del ycat_ref
        xhat, _ = _ln_stats(u_ref[...])
        y = xhat * g_ref[...] + b_ref[...]
        o_ref[...] = (y * _sigmoid(y)).astype(o_ref.dtype)

    t = _row_tile(s, 4)
    return _pallas(
        body, name="conf_ln_fwd", grid=(s // t,),
        in_specs=[pl.BlockSpec((t, DC), lambda i: (i, 0)), _vec_spec(DC), _vec_spec(DC),
                  pl.BlockSpec(memory_space=pl.ANY)],
        out_specs=pl.BlockSpec((t, DC), lambda i: (i, 1)),
        out_shape=_sds(ycat.shape, ycat.dtype),
        input_output_aliases={3: 0},
        semantics=("parallel",),
    )(u1, ln_g, ln_b, ycat)


def _conf_ln_bwd(dycat, u1, ln_g, ln_b):
    s = u1.shape[0]
    t = _row_tile(s, 4)
    nt = s // t

    def body(dy_ref, u_ref, g_ref, b_ref, du_ref, dg_ref, db_ref, a_g, a_b):
        i = pl.program_id(0)

        @pl.when(i == 0)
        def _():
            a_g[...] = jnp.zeros_like(a_g)
            a_b[...] = jnp.zeros_like(a_b)

        xhat, rstd = _ln_stats(u_ref[...])
        gv = g_ref[...]
        y = xhat * gv + b_ref[...]
        sg = _sigmoid(y)
        dyl = dy_ref[...] * (sg * (1.0 + y * (1.0 - sg)))
        a_g[...] += _colsum8(dyl * xhat)
        a_b[...] += _colsum8(dyl)
        dxh = dyl * gv
        du_ref[...] = rstd * (dxh - jnp.mean(dxh, axis=-1, keepdims=True)
                              - xhat * jnp.mean(dxh * xhat, axis=-1, keepdims=True))

        @pl.when(i == nt - 1)
        def _():
            dg_ref[...] = jnp.sum(a_g[...], axis=0, keepdims=True)
            db_ref[...] = jnp.sum(a_b[...], axis=0, keepdims=True)

    return _pallas(
        body, name="conf_ln_bwd", grid=(nt,),
        in_specs=[pl.BlockSpec((t, DC), lambda i: (i, 1)), pl.BlockSpec((t, DC), lambda i: (i, 0)),
                  _vec_spec(DC), _vec_spec(DC)],
        out_specs=[pl.BlockSpec((t, DC), lambda i: (i, 0)), _vec_spec(DC), _vec_spec(DC)],
        out_shape=[_sds((s, DC), F32), _sds((1, DC), F32), _sds((1, DC), F32)],
        scratch_shapes=[pltpu.VMEM((8, DC), F32)] * 2,
        semantics=("arbitrary",),
    )(dycat, u1, ln_g, ln_b)


def _conf_conv_bwd(ag, du1, conv_w, rows_out):
    s = ag.shape[0]
    nc = DC // LANES

    def body(a_ref, g_ref, d_ref, w_ref, da_ref, dg_ref, dw_ref, db_ref, upad, dpad, acc):
        _glu_into(upad, a_ref, g_ref, s)
        _fill_padded(dpad, d_ref, s, ROW_TILE, CONV_HALO)
        acc[...] = jnp.zeros_like(acc)

        def step(c, carry):
            r0 = pl.multiple_of(c * CONV_CHUNK, CONV_CHUNK)
            dcur = dpad[pl.ds(r0 + CONV_HALO, CONV_CHUNK), :]
            du0 = jnp.zeros((CONV_CHUNK, LANES), F32)
            for k in range(CW):
                du0 = du0 + w_ref[k:k + 1, :] * _tap(dpad, r0, CW - 1 - k)
                acc[8 * k:8 * k + 8, :] += _colsum8(dcur * _tap(upad, r0, k))
            acc[8 * CW:8 * CW + 8, :] += _colsum8(dcur)
            av = a_ref[pl.ds(r0, CONV_CHUNK), :]
            sg = _sigmoid(g_ref[pl.ds(r0, CONV_CHUNK), :])
            da_ref[pl.ds(r0, CONV_CHUNK), :] = (du0 * sg).astype(da_ref.dtype)
            dg_ref[pl.ds(r0, CONV_CHUNK), :] = (du0 * av * (sg * (1.0 - sg))).astype(dg_ref.dtype)
            return carry

        lax.fori_loop(0, s // CONV_CHUNK, step, 0)
        if rows_out > s:
            zeros = jnp.zeros((rows_out - s, LANES), da_ref.dtype)
            da_ref[s:rows_out, :] = zeros
            dg_ref[s:rows_out, :] = zeros
        for k in range(CW):
            dw_ref[k:k + 1, :] = jnp.sum(acc[8 * k:8 * k + 8, :], axis=0, keepdims=True)
        db_ref[...] = jnp.sum(acc[8 * CW:8 * CW + 8, :], axis=0, keepdims=True)

    col = lambda off: pl.BlockSpec((s, LANES), lambda c: (0, c + off))
    ocol = pl.BlockSpec((rows_out, LANES), lambda c: (0, c))
    return _pallas(
        body, name="conf_conv_bwd", grid=(nc,),
        in_specs=[col(0), col(nc), col(0), pl.BlockSpec((CW, LANES), lambda c: (0, c))],
        out_specs=[ocol, ocol, pl.BlockSpec((CW, LANES), lambda c: (0, c)), pl.BlockSpec((1, LANES), lambda c: (0, c))],
        out_shape=[_sds((rows_out, DC), MXU_DTYPE)] * 2 + [_sds((CW, DC), F32), _sds((1, DC), F32)],
        scratch_shapes=[pltpu.VMEM((s + 2 * CONV_HALO, LANES), F32)] * 2 + [pltpu.VMEM((8 * (CW + 1), LANES), F32)],
        semantics=("parallel",),
    )(ag, ag, du1, conv_w)


Q_TILE = 2 * GW
K_WIN = PAIR_ROWS * GW


def _bias_table(rpb_rev):
    def body(p_ref, t_ref):
        kcol = lax.broadcasted_iota(jnp.int32, (GW, LANES), 0)
        lane = lax.broadcasted_iota(jnp.int32, (GW, LANES), 1)
        qcol = lane % GW
        cs = jnp.clip(qcol - NA_ROWS, 0, GW - 2 * NA_ROWS)
        colvalid = (kcol >= cs) & (kcol < cs + 2 * NA_ROWS)
        neg = jnp.full((GW, LANES), NEG, F32)

        def skew(h, ro, shift):
            if ro < 0 or ro >= 2 * NA_ROWS - 1:
                return neg
            row = jnp.broadcast_to(p_ref[h * 16 + ro:h * 16 + ro + 1, :], (GW, LANES))
            return pltpu.roll(row, shift, 1, stride=1, stride_axis=0)

        for h in range(NH):
            for b in range(TAB_BLOCKS):
                val = jnp.where(lane < GW, skew(h, b - 1, GW + 1), skew(h, b - 2, 1))
                t_ref[h, b * GW:(b + 1) * GW, :] = jnp.where(colvalid, val, neg)

    return _pallas(body, name="attn_bias_table", out_shape=_sds((NH, TAB_BLOCKS * GW, LANES), F32))(rpb_rev)


def _rpb_grad(tt):
    def body(t_ref, o_ref):
        lane = lax.broadcasted_iota(jnp.int32, (GW, LANES), 1)
        si = lax.broadcasted_iota(jnp.int32, (GW, GW), 0)
        ti = lax.broadcasted_iota(jnp.int32, (GW, GW), 1)
        flip = jnp.where(si + ti == GW - 1, 1.0, 0.0).astype(F32)
        o_ref[...] = jnp.zeros_like(o_ref)
        for h in range(NH):
            for ro in range(2 * NA_ROWS - 1):
                lo = t_ref[h, (ro + 1) * GW:(ro + 2) * GW, :]
                hi = t_ref[h, (ro + 2) * GW:(ro + 3) * GW, :]
                g = jnp.where(lane < GW, lo + pltpu.roll(hi, GW, 1), 0.0)
                gf = jnp.dot(flip, g, preferred_element_type=F32, precision=lax.Precision.HIGHEST)
                sk = pltpu.roll(gf, 0, 1, stride=1, stride_axis=0)
                o_ref[h * 16 + ro:h * 16 + ro + 1, :] = jnp.sum(sk, axis=0, keepdims=True)

    return _pallas(body, name="attn_rpb_grad", out_shape=_sds((NH * 16, LANES), F32))(tt)


def _attn_geometry(i, rows):
    wsp = jnp.clip(2 * i - NA_ROWS // 2, 0, rows - PAIR_ROWS)
    k0 = pl.multiple_of(wsp * GW, GW)
    t0 = pl.multiple_of((wsp - 2 * i + NA_ROWS) * GW, GW)
    rr = lax.broadcasted_iota(jnp.int32, (GW, Q_TILE), 1) // GW
    wsr = jnp.clip(2 * i + rr - NA_ROWS // 2, 0, rows - NA_ROWS)
    edge_masks = tuple(jnp.where((kr >= wsr) & (kr < wsr + NA_ROWS), 0.0, NEG).astype(F32)
                       for kr in (wsp, wsp + PAIR_ROWS - 1))
    return k0, t0, edge_masks


def _biased(s_raw, bias, edge_masks):
    x = s_raw + bias
    return jnp.concatenate([x[:GW] + edge_masks[0], x[GW:K_WIN - GW], x[K_WIN - GW:] + edge_masks[1]], axis=0)


def _two_heads_on_lanes(xt):
    feat = lax.broadcasted_iota(jnp.int32, xt.shape, 0)
    zero = jnp.zeros_like(xt)
    return jnp.concatenate([jnp.where(feat < HD, xt, zero), jnp.where(feat >= HD, xt, zero)], axis=1)


def _two_heads_on_rows(x):
    lane = lax.broadcasted_iota(jnp.int32, x.shape, 1)
    zero = jnp.zeros_like(x)
    return jnp.concatenate([jnp.where(lane < HD, x, zero), jnp.where(lane >= HD, x, zero)], axis=0)


def _pick_heads(x2):
    n = x2.shape[0] // 2
    lane = lax.broadcasted_iota(jnp.int32, (n, LANES), 1)
    return jnp.where(lane < HD, x2[:n], x2[n:])


_TN = (((0,), (0,)), ((), ()))


def _attn_fwd(qkv, tab, s):
    rows = s // GW
    npair = rows // 2

    def body(q_ref, kv_ref, tab_ref, o_ref, lse_ref):
        i = pl.program_id(0)
        k0, t0, edge_masks = _attn_geometry(i, rows)
        for p in range(NH // 2):
            cq = slice(p * LANES, (p + 1) * LANES)
            ck = slice(DA + p * LANES, DA + (p + 1) * LANES)
            cv = slice(2 * DA + p * LANES, 2 * DA + (p + 1) * LANES)
            qm2 = _two_heads_on_lanes(q_ref[:, cq].T) * SCALE
            s_loc = jnp.dot(kv_ref[pl.ds(k0, K_WIN), ck], qm2, preferred_element_type=F32)
            s_ctx = jnp.dot(kv_ref[pl.ds(s, CTX), ck], qm2, preferred_element_type=F32)
            p_loc, p_ctx = [], []
            for hh in range(2):
                h = 2 * p + hh
                ch = slice(hh * Q_TILE, (hh + 1) * Q_TILE)
                sl = _biased(s_loc[:, ch], tab_ref[h, pl.ds(t0, K_WIN), :], edge_masks)
                sc = s_ctx[:, ch]
                m = jnp.maximum(jnp.max(sl, axis=0, keepdims=True), jnp.max(sc, axis=0, keepdims=True))
                el = jnp.exp(sl - m)
                ec = jnp.exp(sc - m)
                l = jnp.sum(el, axis=0, keepdims=True) + jnp.sum(ec, axis=0, keepdims=True)
                inv = 1.0 / l
                lse_ref[h:h + 1, :] = m + jnp.log(l)
                p_loc.append((el * inv).astype(MXU_DTYPE))
                p_ctx.append((ec * inv).astype(MXU_DTYPE))
            o2 = (lax.dot_general(jnp.concatenate(p_loc, axis=1), kv_ref[pl.ds(k0, K_WIN), cv], _TN, preferred_element_type=F32)
                  + lax.dot_general(jnp.concatenate(p_ctx, axis=1), kv_ref[pl.ds(s, CTX), cv], _TN, preferred_element_type=F32))
            o_ref[:, cq] = _pick_heads(o2).astype(o_ref.dtype)

    return _pallas(
        body, name="attn_fwd", grid=(npair,),
        in_specs=[pl.BlockSpec((Q_TILE, DA), lambda i: (i, 0)), pl.BlockSpec(memory_space=pltpu.VMEM),
                  pl.BlockSpec(memory_space=pltpu.VMEM)],
        out_specs=[pl.BlockSpec((Q_TILE, DA), lambda i: (i, 0)), pl.BlockSpec((NH, Q_TILE), lambda i: (0, i))],
        out_shape=[_sds((s, D), MXU_DTYPE), _sds((NH, s), F32)],
        semantics=("arbitrary",),
    )(qkv, qkv, tab)


def _attn_bwd(qkv, tab, lse, dycat, s):
    rows = s // GW
    npair = rows // 2
    sa = s + CTX
    nzero = CTX // Q_TILE

    def body(q_ref, do_ref, lse_ref, kv_ref, tab_ref, dq_ref, dkv_ref, tt_ref, dk_acc, dv_acc):
        i = pl.program_id(0)

        @pl.when(i == 0)
        def _():
            dk_acc[...] = jnp.zeros_like(dk_acc)
            dv_acc[...] = jnp.zeros_like(dv_acc)
            tt_ref[...] = jnp.zeros_like(tt_ref)

        @pl.when(i >= npair)
        def _():
            dq_ref[...] = jnp.zeros_like(dq_ref)

        @pl.when(i < npair)
        def _():
            k0, t0, edge_masks = _attn_geometry(i, rows)
            for p in range(NH // 2):
                cq = slice(p * LANES, (p + 1) * LANES)
                ck = slice(DA + p * LANES, DA + (p + 1) * LANES)
                cv = slice(2 * DA + p * LANES, 2 * DA + (p + 1) * LANES)
                qp = q_ref[:, cq] * SCALE
                dop = do_ref[:, cq].astype(MXU_DTYPE)
                qm2 = _two_heads_on_lanes(qp.T)
                dom2 = _two_heads_on_lanes(dop.T)
                kw = kv_ref[pl.ds(k0, K_WIN), ck]
                kc = kv_ref[pl.ds(s, CTX), ck]
                vw = kv_ref[pl.ds(k0, K_WIN), cv]
                vc = kv_ref[pl.ds(s, CTX), cv]
                s_loc = jnp.dot(kw, qm2, preferred_element_type=F32)
                s_ctx = jnp.dot(kc, qm2, preferred_element_type=F32)
                dp_loc = jnp.dot(vw, dom2, preferred_element_type=F32)
                dp_ctx = jnp.dot(vc, dom2, preferred_element_type=F32)
                p_loc, p_ctx, ds_loc, ds_ctx = [], [], [], []
                for hh in range(2):
                    h = 2 * p + hh
                    ch = slice(hh * Q_TILE, (hh + 1) * Q_TILE)
                    lse_h = lse_ref[h:h + 1, :]
                    pl_ = jnp.exp(_biased(s_loc[:, ch], tab_ref[h, pl.ds(t0, K_WIN), :], edge_masks) - lse_h)
                    pc_ = jnp.exp(s_ctx[:, ch] - lse_h)
                    dpl = dp_loc[:, ch]
                    dpc = dp_ctx[:, ch]
                    delta = jnp.sum(pl_ * dpl, axis=0, keepdims=True) + jnp.sum(pc_ * dpc, axis=0, keepdims=True)
                    dsl = pl_ * (dpl - delta)
                    dsc = pc_ * (dpc - delta)
                    tt_ref[h, pl.ds(t0, K_WIN), :] += dsl
                    p_loc.append(pl_.astype(MXU_DTYPE))
                    p_ctx.append(pc_.astype(MXU_DTYPE))
                    ds_loc.append(dsl.astype(MXU_DTYPE))
                    ds_ctx.append(dsc.astype(MXU_DTYPE))
                p_loc, p_ctx = jnp.concatenate(p_loc, axis=1), jnp.concatenate(p_ctx, axis=1)
                ds_loc, ds_ctx = jnp.concatenate(ds_loc, axis=1), jnp.concatenate(ds_ctx, axis=1)
                do_rows = _two_heads_on_rows(dop)
                q_rows = _two_heads_on_rows(qp)
                dv_acc[pl.ds(k0, K_WIN), cq] += jnp.dot(p_loc, do_rows, preferred_element_type=F32)
                dv_acc[pl.ds(s, CTX), cq] += jnp.dot(p_ctx, do_rows, preferred_element_type=F32)
                dk_acc[pl.ds(k0, K_WIN), cq] += jnp.dot(ds_loc, q_rows, preferred_element_type=F32)
                dk_acc[pl.ds(s, CTX), cq] += jnp.dot(ds_ctx, q_rows, preferred_element_type=F32)
                dq2 = (lax.dot_general(ds_loc, kw, _TN, preferred_element_type=F32)
                       + lax.dot_general(ds_ctx, kc, _TN, preferred_element_type=F32))
                dq_ref[:, cq] = (_pick_heads(dq2) * SCALE).astype(dq_ref.dtype)

        @pl.when(i == npair - 1)
        def _():
            def cp(c, carry):
                r0 = pl.multiple_of(c * ROW_TILE, ROW_TILE)
                dkv_ref[pl.ds(r0, ROW_TILE), 0:DA] = dk_acc[pl.ds(r0, ROW_TILE), :].astype(dkv_ref.dtype)
                dkv_ref[pl.ds(r0, ROW_TILE), DA:2 * DA] = dv_acc[pl.ds(r0, ROW_TILE), :].astype(dkv_ref.dtype)
                return carry

            lax.fori_loop(0, sa // ROW_TILE, cp, 0)

    qmap = lambda i: (jnp.minimum(i, npair - 1), 0)
    return _pallas(
        body, name="attn_bwd", grid=(npair + nzero,),
        in_specs=[pl.BlockSpec((Q_TILE, DA), qmap), pl.BlockSpec((Q_TILE, DA), qmap),
                  pl.BlockSpec((NH, Q_TILE), lambda i: (0, jnp.minimum(i, npair - 1))),
                  pl.BlockSpec(memory_space=pltpu.VMEM), pl.BlockSpec(memory_space=pltpu.VMEM)],
        out_specs=[pl.BlockSpec((Q_TILE, DA), lambda i: (i, 0)), pl.BlockSpec(memory_space=pltpu.VMEM),
                   pl.BlockSpec(memory_space=pltpu.VMEM)],
        out_shape=[_sds((sa, DA), MXU_DTYPE), _sds((sa, 2 * DA), MXU_DTYPE), _sds((NH, TAB_BLOCKS * GW, LANES), F32)],
        scratch_shapes=[pltpu.VMEM((sa, DA), F32)] * 2,
        semantics=("arbitrary",),
    )(qkv, dycat, lse, qkv, tab)


def _tile(n, prefs):
    for t in prefs:
        if n % t == 0:
            return t
    raise ValueError((n, prefs))


def _local_step(x, ctx, tgt, mod, mod_c, vec, w_in, late_weights, rpb_rev, early_grads=None):
    s = x.shape[0]
    sa = s + CTX
    ts = _tile(s, (1024, 512, 256))
    ts2 = _tile(s, (2048, 1024, 512, 256))
    tsa = _tile(sa, (1088, 640, 256))
    tsa2 = _tile(sa, (2176, 640, 256))
    sh1, sc1, gt1, sh2, sc2, gt2 = (mod[i:i + 1] for i in range(6))
    csh1, csc1 = mod_c[0:1], mod_c[1:2]
    act = MXU_DTYPE

    tab = _bias_table(rpb_rev)
    h_all = _rmsmod_fwd(x, ctx, vec["g_norm1"], sc1, sh1, csc1, csh1)
    w_in = w_in(h_all) if callable(w_in) else w_in
    qkv = _mm(h_all, w_in, mode="nn", m=sa, n=3 * DA, k=D, tm=tsa2, tn=512, tk=D, out_dtype=MXU_DTYPE, name="mm_qkv")
    ag = _mm(h_all, w_in, mode="nn", m=s, n=2 * DC, k=D, tm=ts2, tn=512, tk=D, out_dtype=F32, name="mm_ag", b_off=(0, 3))
    ycat, lse = _attn_fwd(qkv, tab, s)
    u1 = _conf_conv_fwd(ag, vec["conv_w"], vec["conv_b"])
    ycat = _conf_ln_fwd(u1, vec["ln_g"], vec["ln_b"], ycat)
    if callable(late_weights):
        w_out, ffn_weights = late_weights(ycat)
    else:
        w_out, ffn_weights = late_weights[0], late_weights[1:]
    y = _mm(ycat, w_out, mode="nn", m=s, n=D, k=D, tm=ts2, tn=512, tk=D, out_dtype=F32, name="mm_out")
    x1, h2 = _resid_rmsmod_fwd(x, y, gt1, vec["g_norm2"], sc2, sh2)
    w_up, w_down = ffn_weights(h2) if callable(ffn_weights) else ffn_weights
    u = _mm(h2, w_up, mode="nn", m=s, n=2 * DFF, k=D, tm=ts2, tn=512, tk=D, out_dtype=act, name="mm_up")
    f = _ffn_act_fwd(u, vec["ffn_conv_w"], vec["ffn_conv_b"])
    z = _mm(f, w_down, mode="nn", m=s, n=D, k=DFF, tm=ts, tn=D, tk=DFF, out_dtype=F32, name="mm_down")
    dx2, dz, loss, dgt2, dgf = _final_fwd_bwd(x1, z, gt2, vec["g_final"], tgt)

    df = _mm(dz, w_down, mode="nt", m=s, n=DFF, k=D, tm=ts, tn=DFF, tk=D, out_dtype=act, name="mm_down_dx")
    d_w_down = _mm(f, dz, mode="tn", m=DFF, n=D, k=s, tm=DFF // 2, tn=D, tk=ts2, out_dtype=F32, name="mm_down_dw")
    dug, duv, dfw_g, dfw_v, dfb_g, dfb_v = _ffn_act_bwd(u, df, vec["ffn_conv_w"], vec["ffn_conv_b"])
    dw_kw = dict(mode="tn", m=D, n=DFF, k=s, tm=D, tn=DFF, tk=ts, out_dtype=F32, out_total=(D, 2 * DFF))
    d_w_up = _mm(h2, dug, name="mm_up_dw_gate", **dw_kw)
    d_w_up = _mm(h2, duv, name="mm_up_dw_val", o_off=(0, 1), into=d_w_up, **dw_kw)
    if early_grads is not None:
        early_grads[0](d_w_up, d_w_down)
    dh2 = _mm([dug, duv], w_up, mode="nt", m=s, n=D, k=2 * DFF, tm=ts, tn=D, tk=2 * DFF, out_dtype=F32, name="mm_up_dx")
    sc2_b = sc2 if early_grads is None else sc2 + early_grads[1](dh2)
    dsh2, dsc2, dg2, dx1, dy, dgt1 = _rmsmod_bwd(x1, dh2, vec["g_norm2"], sc2_b, name="rmsmod2_bwd", add=dx2, resid=(gt1, y))
    dycat = _mm(dy, w_out, mode="nt", m=s, n=D, k=D, tm=ts2, tn=512, tk=D, out_dtype=F32, name="mm_out_dx")
    d_w_out = _mm(ycat, dy, mode="tn", m=D, n=D, k=s, tm=D, tn=D, tk=ts, out_dtype=F32, name="mm_out_dw")
    du1, dln_g, dln_b = _conf_ln_bwd(dycat, u1, vec["ln_g"], vec["ln_b"])
    da, dg, dconv_w, dconv_b = _conf_conv_bwd(ag, du1, vec["conv_w"], sa)
    dq, dkv, tt = _attn_bwd(qkv, tab, lse, dycat, s)
    drpb_rev = _rpb_grad(tt)
    d_pieces = [dq, dkv, da, dg]
    dh = _mm(d_pieces, w_in, mode="nt", m=sa, n=D, k=NIN, tm=tsa, tn=D, tk=NIN, out_dtype=F32, name="mm_in_dx")
    d_w_in = _mm(h_all, d_pieces, mode="tn", m=D, n=NIN, k=sa, tm=D, tn=NIN, tk=tsa, out_dtype=F32, name="mm_in_dw")
    dsh1, dsc1, dg1, grad_x = _rmsmod_bwd(x, dh, vec["g_norm1"], sc1, name="rmsmod1_bwd", add=dx1)
    dcsh1, dcsc1, dg1c = _rmsmod_bwd(ctx, dh, vec["g_norm1"], csc1, name="rmsmod1_ctx_bwd", dh_row0=s)

    small = dict(
        dmod=[dsh1, dsc1, dgt1, dsh2, dsc2, dgt2], dmod_c=[dcsh1, dcsc1],
        g_norm1=[dg1, dg1c], g_norm2=dg2, g_final=dgf, conv_b=dconv_b, ln_g=dln_g, ln_b=dln_b, conv_w=dconv_w,
        ffn_conv_w=[dfw_g, dfw_v], ffn_conv_b=[dfb_g, dfb_v], rpb_rev=drpb_rev,
    )
    return loss, grad_x, d_w_in, d_w_out, d_w_up, d_w_down, small


N_CHIPS = 4
HBM = pl.BlockSpec(memory_space=pl.ANY)
BIG = {"w_in": ("col", (D, NIN)), "w_out": ("row", (D, D)), "w_up": ("col", (D, 2 * DFF)), "w_down": ("row", (DFF, D))}
BIG_NAMES = tuple(BIG)
LATE_NAMES = ("w_out", "w_up", "w_down")


def _shard_shape(name):
    kind, (r, c) = BIG[name]
    return (r, c // N_CHIPS) if kind == "col" else (r // N_CHIPS, c)


def _half_rows(name):
    return _shard_shape(name)[0] // 2


def _place():
    x, y, c = lax.axis_index("x"), lax.axis_index("y"), lax.axis_index("c")
    others = [(1 - x, y), (x, 1 - y), (1 - x, 1 - y)]
    return x, y, c, 2 * x + y, (x, y, 1 - c), others


def _whole_region(ref, name, chip, half):
    kind, _ = BIG[name]
    r, c = _shard_shape(name)
    if kind == "col":
        return ref.at[pl.ds(half * (r // 2), r // 2), pl.ds(chip * c, c)]
    return ref.at[pl.ds(chip * r + half * (r // 2), r // 2), :]


def _remote(src, dst, send_sem, recv_sem, to):
    return pltpu.make_async_remote_copy(src_ref=src, dst_ref=dst, send_sem=send_sem, recv_sem=recv_sem,
                                        device_id=to, device_id_type=MESH)


def _cast_into_whole(name, shard, chip):
    kind, whole = BIG[name]
    r, c = shard.shape
    if kind == "col":
        tr = 256
        o_spec = pl.BlockSpec((tr, c), lambda i, ch: (i, ch[0]))
    else:
        tr = _tile(r, (128, 352))
        o_spec = pl.BlockSpec((tr, c), lambda i, ch: (ch[0] * (r // tr) + i, 0))

    def body(ch_ref, x_ref, o_ref):
        del ch_ref
        o_ref[...] = x_ref[...].astype(o_ref.dtype)

    return _pallas(body, name="cast_" + name, prefetch=1, grid=(r // tr,),
                   in_specs=[pl.BlockSpec((tr, c), lambda i, ch: (i, 0))], out_specs=o_spec,
                   out_shape=_sds(whole, MXU_DTYPE), semantics=("parallel",))(chip, shard)


SEM = pl.BlockSpec(memory_space=pltpu.SEMAPHORE)
IN_HBM = pl.BlockSpec(memory_space=pltpu.HBM)
DATAFLOW = pltpu.SideEffectType.DATAFLOW_SIDE_EFFECTING


def _keep_in_hbm(a):
    return pltpu.with_memory_space_constraint(a, pltpu.HBM)


def _several(after):
    return list(after) if isinstance(after, (list, tuple)) else [after]


FLIPS = [(dx, dy, dc) for dx in (0, 1) for dy in (0, 1) for dc in (0, 1)][1:]
OTHER_CHIPS = [f for f in FLIPS if f[2] == 0]


def _flipped(flip):
    x, y, c = lax.axis_index("x"), lax.axis_index("y"), lax.axis_index("c")
    return tuple(1 - v if f else v for v, f in zip((x, y, c), flip))


def _eight_copies(o_ref, val):
    r = val.shape[0]
    for d in range(8):
        o_ref[d * r:(d + 1) * r, :] = val


def _share_start(block, tag, after, flips=FLIPS):
    v, land = block
    r, n = v.shape
    assert land.shape == (8 * r, n)
    ns = 2 * len(flips)

    def body(*refs):
        v_ref, land_ref = refs[0], refs[1]
        sems = refs[2 + len(_several(after)):2 + len(_several(after)) + ns]
        x, y, c = lax.axis_index("x"), lax.axis_index("y"), lax.axis_index("c")
        mine = land_ref.at[pl.ds((4 * x + 2 * y + c) * r, r), :]
        for k, flip in enumerate(flips):
            _remote(v_ref, mine, sems[2 * k], sems[2 * k + 1], _flipped(flip)).start()

    res = pl.pallas_call(
        body, name="share_" + tag + "_start",
        out_shape=(*[pltpu.SemaphoreType.DMA(())] * ns, pltpu.HBM(v.shape, v.dtype), pltpu.HBM((8 * r, n), v.dtype)),
        in_specs=[IN_HBM] * 2 + [pl.BlockSpec(memory_space=pl.ANY)] * len(_several(after)),
        out_specs=(*[SEM] * ns, IN_HBM, IN_HBM),
        input_output_aliases={0: ns, 1: ns + 1},
        compiler_params=pltpu.CompilerParams(has_side_effects=DATAFLOW),
    )(_keep_in_hbm(v), _keep_in_hbm(land), *_several(after))
    return list(res[:ns]), res[ns], res[ns + 1], flips


def _share_wait(started, after, tag):
    sems, v, land, flips = started
    r = v.shape[0]
    ns = len(sems)

    def body(*refs):
        v_ref, land_ref = refs[0], refs[1]
        sem_refs = refs[2:2 + ns]
        for k, flip in enumerate(flips):
            px, py, pc = _flipped(flip)
            theirs = land_ref.at[pl.ds((4 * px + 2 * py + pc) * r, r), :]
            cp = _remote(v_ref, theirs, sem_refs[2 * k], sem_refs[2 * k + 1], (px, py, pc))
            cp.wait_send()
            cp.wait_recv()

    res = pl.pallas_call(
        body, name="share_" + tag + "_wait",
        out_shape=(pltpu.HBM(v.shape, v.dtype), pltpu.HBM(land.shape, land.dtype)),
        in_specs=[IN_HBM] * 2 + [SEM] * ns + [pl.BlockSpec(memory_space=pl.ANY)] * len(_several(after)),
        out_specs=(IN_HBM, IN_HBM),
        input_output_aliases={0: 0, 1: 1},
        compiler_params=pltpu.CompilerParams(has_side_effects=DATAFLOW),
    )(v, land, *sems, *_several(after))
    return res[1]


def _gather_start(wholes, names, after, tag):
    nw = len(names)
    ns = 2 * 3 * nw

    def body(*refs):
        ins = refs[:nw]
        sems = refs[nw + 1:nw + 1 + ns]
        token = refs[2 * nw + ns + 1]
        _, _, c, chip, _, others = _place()
        for w, name in enumerate(names):
            mine = _whole_region(ins[w], name, chip, c)
            for t, (ox, oy) in enumerate(others):
                k = 2 * (3 * w + t)
                _remote(mine, mine, sems[k], sems[k + 1], (ox, oy, c)).start()
        token[...] = jnp.zeros_like(token)

    res = pl.pallas_call(
        body, name="gather_" + tag + "_start",
        out_shape=(*[pltpu.SemaphoreType.DMA(())] * ns, *[pltpu.HBM(a.shape, a.dtype) for a in wholes], _sds((8, LANES), F32)),
        in_specs=[IN_HBM] * nw + [pl.BlockSpec(memory_space=pl.ANY)],
        out_specs=(*[SEM] * ns, *[IN_HBM] * nw, pl.BlockSpec(memory_space=pltpu.VMEM)),
        input_output_aliases={i: ns + i for i in range(nw)},
        compiler_params=pltpu.CompilerParams(has_side_effects=DATAFLOW),
    )(*[_keep_in_hbm(a) for a in wholes], after)
    return list(res[:ns]), list(res[ns:ns + nw]), res[ns + nw]


def _gather_wait(sems, wholes, names, after, tag):
    nw = len(names)
    ns = len(sems)

    def body(*refs):
        ins = refs[:nw]
        sem_refs = refs[nw:nw + ns]
        _, _, c, chip, _, others = _place()
        for w, name in enumerate(names):
            mine = _whole_region(ins[w], name, chip, c)
            for t, (ox, oy) in enumerate(others):
                got = _whole_region(ins[w], name, 2 * ox + oy, c)
                k = 2 * (3 * w + t)
                cp = _remote(mine, got, sem_refs[k], sem_refs[k + 1], (ox, oy, c))
                cp.wait_send()
                cp.wait_recv()

    return pl.pallas_call(
        body, name="gather_" + tag + "_wait",
        out_shape=tuple(pltpu.HBM(a.shape, a.dtype) for a in wholes),
        in_specs=[IN_HBM] * nw + [SEM] * ns + [pl.BlockSpec(memory_space=pl.ANY)], out_specs=tuple([IN_HBM] * nw),
        input_output_aliases={i: i for i in range(nw)},
        compiler_params=pltpu.CompilerParams(has_side_effects=DATAFLOW),
    )(*wholes, *sems, after)


def _forward_halves(wholes, names, tag):
    nw = len(names)

    def body(*refs):
        outs = refs[nw:2 * nw]
        send_sems, recv_sems = refs[2 * nw:]
        _, _, c, _, sibling, others = _place()
        sends = []
        for w, name in enumerate(names):
            for t, (ox, oy) in enumerate(others):
                got = _whole_region(outs[w], name, 2 * ox + oy, c)
                cp = _remote(got, got, send_sems.at[w, t], recv_sems.at[w, t], sibling)
                cp.start()
                sends.append(cp)
        for w, name in enumerate(names):
            for t, (ox, oy) in enumerate(others):
                got = _whole_region(outs[w], name, 2 * ox + oy, 1 - c)
                _remote(got, got, send_sems.at[w, t], recv_sems.at[w, t], sibling).wait_recv()
        for cp in sends:
            cp.wait_send()

    return pl.pallas_call(
        body, name="gather_" + tag + "_forward",
        out_shape=[_sds(a.shape, a.dtype) for a in wholes],
        in_specs=[HBM] * nw, out_specs=[HBM] * nw,
        input_output_aliases={i: i for i in range(nw)},
        scratch_shapes=[pltpu.SemaphoreType.DMA((nw, 3)), pltpu.SemaphoreType.DMA((nw, 3))],
    )(*wholes)


def _forward_start(wholes, names, tag, after):
    nw = len(names)
    ns = 2 * 3 * nw

    def body(*refs):
        ins = refs[:nw]
        sems = refs[nw + 1:nw + 1 + ns]
        token = refs[2 * nw + ns + 1]
        _, _, c, _, sibling, others = _place()
        for w, name in enumerate(names):
            for t, (ox, oy) in enumerate(others):
                got = _whole_region(ins[w], name, 2 * ox + oy, c)
                k = 2 * (3 * w + t)
                _remote(got, got, sems[k], sems[k + 1], sibling).start()
        token[...] = jnp.zeros_like(token)

    res = pl.pallas_call(
        body, name="gather_" + tag + "_forward_start",
        out_shape=(*[pltpu.SemaphoreType.DMA(())] * ns, *[pltpu.HBM(a.shape, a.dtype) for a in wholes], _sds((8, LANES), F32)),
        in_specs=[IN_HBM] * nw + [pl.BlockSpec(memory_space=pl.ANY)],
        out_specs=(*[SEM] * ns, *[IN_HBM] * nw, pl.BlockSpec(memory_space=pltpu.VMEM)),
        input_output_aliases={i: ns + i for i in range(nw)},
        compiler_params=pltpu.CompilerParams(has_side_effects=DATAFLOW),
    )(*[_keep_in_hbm(a) for a in wholes], after)
    return list(res[:ns]), list(res[ns:ns + nw]), res[ns + nw]


def _forward_wait(sems, wholes, names, after, tag):
    nw = len(names)
    ns = len(sems)

    def body(*refs):
        ins = refs[:nw]
        sem_refs = refs[nw:nw + ns]
        _, _, c, _, sibling, others = _place()
        for w, name in enumerate(names):
            for t, (ox, oy) in enumerate(others):
                k = 2 * (3 * w + t)
                cp = _remote(_whole_region(ins[w], name, 2 * ox + oy, c), _whole_region(ins[w], name, 2 * ox + oy, 1 - c),
                             sem_refs[k], sem_refs[k + 1], sibling)
                cp.wait_send()
                cp.wait_recv()

    return pl.pallas_call(
        body, name="gather_" + tag + "_forward_wait",
        out_shape=tuple(pltpu.HBM(a.shape, a.dtype) for a in wholes),
        in_specs=[IN_HBM] * nw + [SEM] * ns + [pl.BlockSpec(memory_space=pl.ANY)], out_specs=tuple([IN_HBM] * nw),
        input_output_aliases={i: i for i in range(nw)},
        compiler_params=pltpu.CompilerParams(has_side_effects=DATAFLOW),
    )(*wholes, *sems, after)


def _compact_shape(name, dtype):
    kind, (r, c) = BIG[name]
    return _sds((r // 2, c), dtype)


def _swap_pairs(ins, outs, names, c):
    pairs = []
    for w, name in enumerate(names):
        kind, _ = BIG[name]
        half = _half_rows(name)
        if kind == "col":
            pairs.append((ins[w].at[pl.ds((1 - c) * half, half), :], outs[w]))
        else:
            pairs += [(ins[w].at[pl.ds(jj * 2 * half + (1 - c) * half, half), :], outs[w].at[pl.ds(jj * half, half), :])
                      for jj in range(N_CHIPS)]
    return pairs


def _n_swap_copies(names):
    return sum(1 if BIG[n][0] == "col" else N_CHIPS for n in names)


def _swap_start(grads, names, label):
    nw = len(names)
    ns = 2 * _n_swap_copies(names)

    def body(*refs):
        ins, lands = refs[:nw], refs[nw:2 * nw]
        sems = refs[2 * nw:2 * nw + ns]
        token = refs[4 * nw + ns]
        _, _, c, _, sibling, _ = _place()
        for k, (src, dst) in enumerate(_swap_pairs(ins, lands, names, c)):
            _remote(src, dst, sems[2 * k], sems[2 * k + 1], sibling).start()
        token[...] = jnp.zeros_like(token)

    lands = [_keep_in_hbm(lax.empty(_compact_shape(n, F32).shape, F32)) for n in names]
    res = pl.pallas_call(
        body, name=label,
        out_shape=(*[pltpu.SemaphoreType.DMA(())] * ns, *[pltpu.HBM(a.shape, a.dtype) for a in grads],
                   *[pltpu.HBM(a.shape, a.dtype) for a in lands], _sds((8, LANES), F32)),
        in_specs=[IN_HBM] * (2 * nw),
        out_specs=(*[SEM] * ns, *[IN_HBM] * (2 * nw), pl.BlockSpec(memory_space=pltpu.VMEM)),
        input_output_aliases={i: ns + i for i in range(2 * nw)},
        compiler_params=pltpu.CompilerParams(has_side_effects=DATAFLOW),
    )(*[_keep_in_hbm(a) for a in grads], *lands)
    return list(res[:ns]), list(res[ns:ns + nw]), list(res[ns + nw:ns + 2 * nw]), res[ns + 2 * nw]


def _swap_wait(sems, grads, lands, names, after, label):
    nw = len(names)
    ns = len(sems)

    def body(*refs):
        ins, land_refs = refs[:nw], refs[nw:2 * nw]
        sem_refs = refs[2 * nw:2 * nw + ns]
        _, _, c, _, sibling, _ = _place()
        for k, (src, dst) in enumerate(_swap_pairs(ins, land_refs, names, c)):
            cp = _remote(src, dst, sem_refs[2 * k], sem_refs[2 * k + 1], sibling)
            cp.wait_send()
            cp.wait_recv()

    res = pl.pallas_call(
        body, name=label,
        out_shape=tuple(pltpu.HBM(a.shape, a.dtype) for a in (*grads, *lands)),
        in_specs=[IN_HBM] * (2 * nw) + [SEM] * ns + [pl.BlockSpec(memory_space=pl.ANY)] * len(_several(after)),
        out_specs=tuple([IN_HBM] * (2 * nw)),
        input_output_aliases={i: i for i in range(2 * nw)},
        compiler_params=pltpu.CompilerParams(has_side_effects=DATAFLOW),
    )(*grads, *lands, *sems, *_several(after))
    return list(res[:nw]), list(res[nw:])


def _add_halves(name, grad, got, core):
    kind, (r, c) = BIG[name]
    half = _half_rows(name)
    if kind == "col":
        t = 128
        grid = (half // t,)
        g_spec = pl.BlockSpec((t, c), lambda i, cr: (cr[0] * (half // t) + i, 0))
        o_spec = pl.BlockSpec((t, c), lambda i, cr: (i, 0))
    else:
        t = half
        grid = (N_CHIPS,)
        g_spec = pl.BlockSpec((t, c), lambda i, cr: (2 * i + cr[0], 0))
        o_spec = pl.BlockSpec((t, c), lambda i, cr: (i, 0))

    def body(c_ref, g_ref, b_ref, o_ref):
        del c_ref
        o_ref[...] = (g_ref[...] + b_ref[...]).astype(o_ref.dtype)

    return pl.pallas_call(
        body, name="grad_add_" + name,
        grid_spec=pltpu.PrefetchScalarGridSpec(num_scalar_prefetch=1, grid=grid, in_specs=[g_spec, o_spec], out_specs=o_spec),
        out_shape=_compact_shape(name, BF16),
        compiler_params=pltpu.CompilerParams(dimension_semantics=("parallel",), vmem_limit_bytes=VMEM_LIMIT),
    )(core, grad, got)


def _piece(ref, name, chip):
    kind, _ = BIG[name]
    r, c = _shard_shape(name)
    if kind == "col":
        return ref.at[:, pl.ds(chip * c, c)]
    return ref.at[pl.ds(chip * (r // 2), r // 2), :]


def _landing_shape(name):
    r, c = _shard_shape(name)
    return (N_CHIPS - 1, r // 2, c)


def _exchange_start(parts, names, label):
    nw = len(names)
    ns = 2 * 3 * nw

    def body(*refs):
        ins, lands = refs[:nw], refs[nw:2 * nw]
        sems = refs[2 * nw:2 * nw + ns]
        token = refs[4 * nw + ns]
        _, _, c, _, _, others = _place()
        for w, name in enumerate(names):
            for t, (ox, oy) in enumerate(others):
                k = 2 * (3 * w + t)
                _remote(_piece(ins[w], name, 2 * ox + oy), lands[w].at[t], sems[k], sems[k + 1], (ox, oy, c)).start()
        token[...] = jnp.zeros_like(token)

    lands = [_keep_in_hbm(lax.empty(_landing_shape(n), BF16)) for n in names]
    res = pl.pallas_call(
        body, name=label,
        out_shape=(*[pltpu.SemaphoreType.DMA(())] * ns, *[pltpu.HBM(a.shape, a.dtype) for a in parts],
                   *[pltpu.HBM(a.shape, a.dtype) for a in lands], _sds((8, LANES), F32)),
        in_specs=[IN_HBM] * (2 * nw),
        out_specs=(*[SEM] * ns, *[IN_HBM] * (2 * nw), pl.BlockSpec(memory_space=pltpu.VMEM)),
        input_output_aliases={i: ns + i for i in range(2 * nw)},
        compiler_params=pltpu.CompilerParams(has_side_effects=DATAFLOW),
    )(*[_keep_in_hbm(a) for a in parts], *lands)
    return list(res[:ns]), list(res[ns:ns + nw]), list(res[ns + nw:ns + 2 * nw]), res[ns + 2 * nw]


def _exchange_wait(sems, parts, lands, names, after, label):
    nw = len(names)
    ns = len(sems)

    def body(*refs):
        ins, land_refs = refs[:nw], refs[nw:2 * nw]
        sem_refs = refs[2 * nw:2 * nw + ns]
        _, _, c, _, _, others = _place()
        for w, name in enumerate(names):
            for t, (ox, oy) in enumerate(others):
                k = 2 * (3 * w + t)
                cp = _remote(_piece(ins[w], name, 2 * ox + oy), land_refs[w].at[t], sem_refs[k], sem_refs[k + 1], (ox, oy, c))
                cp.wait_send()
                cp.wait_recv()

    res = pl.pallas_call(
        body, name=label,
        out_shape=tuple(pltpu.HBM(a.shape, a.dtype) for a in (*parts, *lands)),
        in_specs=[IN_HBM] * (2 * nw) + [SEM] * ns + [pl.BlockSpec(memory_space=pl.ANY)] * len(_several(after)),
        out_specs=tuple([IN_HBM] * (2 * nw)),
        input_output_aliases={i: i for i in range(2 * nw)},
        compiler_params=pltpu.CompilerParams(has_side_effects=DATAFLOW),
    )(*parts, *lands, *sems, *_several(after))
    return list(res[:nw]), list(res[nw:])


def _sum_chips(name, part, got, chip):
    kind, _ = BIG[name]
    _, r, c = got.shape
    t = _tile(r, (128, 352))
    if kind == "col":
        own = pl.BlockSpec((t, c), lambda i, ch: (i, ch[0]))
    else:
        own = pl.BlockSpec((t, c), lambda i, ch: (ch[0] * (r // t) + i, 0))

    def body(ch_ref, p_ref, g_ref, o_ref):
        del ch_ref
        acc = p_ref[...].astype(F32)
        for j in range(N_CHIPS - 1):
            acc = acc + g_ref[j].astype(F32)
        o_ref[...] = acc

    return _pallas(
        body, name="grad_sum_" + name, prefetch=1, grid=(r // t,),
        in_specs=[own, pl.BlockSpec((N_CHIPS - 1, t, c), lambda i, ch: (0, i, 0))],
        out_specs=pl.BlockSpec((t, c), lambda i, ch: (i, 0)),
        out_shape=_sds((r, c), F32), semantics=("parallel",),
    )(chip, part, got)


def _send_halves(sums, label, after):
    nw = len(sums)

    def body(*refs):
        ins, outs = refs[:nw], refs[nw + 1:2 * nw + 1]
        send_sems, recv_sems = refs[2 * nw + 1:]
        _, _, _, _, sibling, _ = _place()
        copies = [_remote(ins[w], outs[w], send_sems.at[w], recv_sems.at[w], sibling) for w in range(nw)]
        for cp in copies:
            cp.start()
        for cp in copies:
            cp.wait()

    return pl.pallas_call(
        body, name=label,
        out_shape=[_sds(a.shape, a.dtype) for a in sums],
        in_specs=[HBM] * (nw + 1), out_specs=[HBM] * nw,
        scratch_shapes=[pltpu.SemaphoreType.DMA((nw,)), pltpu.SemaphoreType.DMA((nw,))],
    )(*sums, after)


EARLY_GRADS = ("w_up", "w_down")
LAST_GRADS = ("w_in", "w_out")


def _reduce_finish(started, names, after, chip, tag):
    sems, parts, lands, _ = started
    parts, lands = _exchange_wait(sems, parts, lands, names, after, "grad_exchange_wait_" + tag)
    return [_sum_chips(n, parts[i], lands[i], chip) for i, n in enumerate(names)]


HI = lax.Precision.HIGHEST
MOD_COLS = 6 * D // N_CHIPS
COND_ROWS = 16


def _silu(v):
    return v * _sigmoid(v)


GATHER_ROWS = 8
FFW_COLS = 2 * DFF // N_CHIPS
CONV_COLS = DC // N_CHIPS
TAPS_PER_ROW = FFW_COLS // CONV_COLS
assert 4 + -(-CW // TAPS_PER_ROW) <= GATHER_ROWS


def _conv_tap_place(k):
    return 4 + k // TAPS_PER_ROW, (k % TAPS_PER_ROW) * CONV_COLS


def _taps_first(a):
    return jnp.transpose(a, (1, 0, 2))


def _pack_cond(c, ffn_w, conv_w):
    def body(c_ref, f_ref, w_ref, o_ref, o8_ref):
        o_ref[...] = jnp.zeros_like(o_ref)
        o_ref[0:1, 0:D] = c_ref[...]
        for k in range(3):
            o_ref[1 + k:2 + k, :] = f_ref[k]
        for k in range(CW):
            row, lane = _conv_tap_place(k)
            o_ref[row:row + 1, lane:lane + CONV_COLS] = w_ref[k]
        _eight_copies(o8_ref, o_ref[...])

    return _pallas(body, name="pack_cond", out_shape=[_sds((GATHER_ROWS, FFW_COLS), F32),
                                                      _sds((8 * GATHER_ROWS, FFW_COLS), F32)])(c, ffn_w, conv_w)


def _unpack_cond(got, c_ctx):
    def body(g_ref, c_ref, cond_ref, f_ref, w_ref):
        cond_ref[...] = jnp.zeros_like(cond_ref)
        for d in range(8):
            cond_ref[d:d + 1, :] = g_ref[d * GATHER_ROWS:d * GATHER_ROWS + 1, 0:D]
        cond_ref[8:9, :] = c_ref[...]
        for j in range(N_CHIPS):
            r0 = 2 * j * GATHER_ROWS
            f_ref[:, j * FFW_COLS:(j + 1) * FFW_COLS] = g_ref[r0 + 1:r0 + 4, :]
            for k in range(CW):
                row, lane = _conv_tap_place(k)
                w_ref[k:k + 1, j * CONV_COLS:(j + 1) * CONV_COLS] = g_ref[r0 + row:r0 + row + 1, lane:lane + CONV_COLS]

    return _pallas(body, name="unpack_cond",
                   out_shape=[_sds((COND_ROWS, D), F32), _sds((3, 2 * DFF), F32), _sds((CW, DC), F32)])(got, c_ctx)


def _chip_cols(rows, width):
    return pl.BlockSpec((rows, width), lambda i, ch: (0, ch[0]))


def _whole(shape):
    return pl.BlockSpec(shape, lambda i, ch: (0,) * len(shape))


def _mod_shard(cond, w_mod, b_mod, chip):
    def body(ch_ref, c_ref, w_ref, b_ref, o_ref, o8_ref):
        del ch_ref
        o_ref[...] = jnp.dot(_silu(c_ref[...]), w_ref[...], preferred_element_type=F32, precision=HI) + b_ref[...]
        _eight_copies(o8_ref, o_ref[...])

    return _pallas(body, name="mod_fwd", prefetch=1, grid=(1,),
                   in_specs=[_whole((COND_ROWS, D)), _whole((D, MOD_COLS)), _chip_cols(1, MOD_COLS)],
                   out_specs=[_whole((COND_ROWS, MOD_COLS)), _whole((8 * COND_ROWS, MOD_COLS))],
                   out_shape=[_sds((COND_ROWS, MOD_COLS), F32), _sds((8 * COND_ROWS, MOD_COLS), F32)])(
                       chip, cond, w_mod, b_mod)


def _unpack_mod(mods, dev):
    def body(dev_ref, m_ref, me_ref, c_ref):
        rowi = lax.broadcasted_iota(jnp.int32, (COND_ROWS, MOD_COLS), 0)
        core = dev_ref[0] % 2
        mine, ctx = [], []
        for j in range(N_CHIPS):
            blk = m_ref[pl.ds(pl.multiple_of((2 * j + core) * COND_ROWS, COND_ROWS), COND_ROWS), :]
            mine.append(jnp.sum(jnp.where(rowi == dev_ref[0], blk, 0.0), axis=0, keepdims=True))
            ctx.append(blk[8:9, :])
        mine = jnp.concatenate(mine, axis=1)
        ctx = jnp.concatenate(ctx, axis=1)
        for k in range(6):
            me_ref[k:k + 1, :] = mine[:, k * D:(k + 1) * D]
        for k in range(2):
            c_ref[k:k + 1, :] = ctx[:, k * D:(k + 1) * D]

    return _pallas(body, name="unpack_mod", prefetch=1, grid=(1,),
                   in_specs=[_whole(mods.shape)], out_specs=[_whole((6, D)), _whole((2, D))],
                   out_shape=[_sds((6, D), F32), _sds((2, D), F32)])(dev, mods)


MOD_TILE = 512


def _mod_weight_update(cond, dmod_all, w, m, v, chip):
    nt = MOD_COLS // MOD_TILE

    def body(ch_ref, c_ref, d_ref, w_ref, m_ref, v_ref, g_ref, dl_ref, nm_ref, nv_ref):
        del ch_ref
        g = lax.dot_general(_silu(c_ref[...]), d_ref[...], _TN, preferred_element_type=F32, precision=HI)
        g_ref[...] = g
        dl_ref[...], nm_ref[...], nv_ref[...] = _adam_math(w_ref[...], g, m_ref[...], v_ref[...])

    blk = pl.BlockSpec((D, MOD_TILE), lambda j, ch: (0, j))
    return _pallas(body, name="mod_weight_update", prefetch=1, grid=(nt,),
                   in_specs=[_whole((COND_ROWS, D)), pl.BlockSpec((COND_ROWS, MOD_TILE), lambda j, ch: (0, ch[0] * nt + j)),
                             blk, blk, blk],
                   out_specs=[blk] * 4, out_shape=[_sds((D, MOD_COLS), F32)] * 4,
                   semantics=("parallel",))(chip, cond, dmod_all, w, m, v)


def _cond_grad_partial(dmod_all, w_mod, chip):
    def body(ch_ref, d_ref, w_ref, o_ref, o8_ref):
        del ch_ref
        o_ref[...] = lax.dot_general(d_ref[...], w_ref[...], (((1,), (1,)), ((), ())), preferred_element_type=F32, precision=HI)
        _eight_copies(o8_ref, o_ref[...])

    return _pallas(body, name="cond_grad_partial", prefetch=1, grid=(1,),
                   in_specs=[pl.BlockSpec((8, MOD_COLS), lambda i, ch: (1, ch[0])), _whole((D, MOD_COLS))],
                   out_specs=[_whole((8, D)), _whole((64, D))],
                   out_shape=[_sds((8, D), F32), _sds((64, D), F32)])(chip, dmod_all, w_mod)


def _adam_math(w, g, m, v):
    nm = ADAM_B1 * m + (1.0 - ADAM_B1) * g
    nv = ADAM_B2 * v + (1.0 - ADAM_B2) * (g * g)
    c1 = 1.0 - ADAM_B1 ** ADAM_STEP
    c2 = 1.0 - ADAM_B2 ** ADAM_STEP
    return -ADAM_LR * ((nm / c1) / (jnp.sqrt(nv / c2) + ADAM_EPS) + ADAM_WD * w), nm, nv


def _cond_update(parts, c_ctx, m, v):
    def body(p_ref, c_ref, m_ref, v_ref, g_ref, d_ref, nm_ref, nv_ref):
        tot = p_ref[0:1, :]
        for j in range(1, N_CHIPS):
            tot = tot + p_ref[16 * j:16 * j + 1, :]
        cv = c_ref[...]
        sg = _sigmoid(cv)
        g = tot * (sg * (1.0 + cv * (1.0 - sg)))
        g_ref[...] = g
        d_ref[...], nm_ref[...], nv_ref[...] = _adam_math(cv, g, m_ref[...], v_ref[...])

    return _pallas(body, name="cond_update", out_shape=[_sds((1, D), F32)] * 4)(parts, c_ctx, m, v)


def _adamw_cols(w, g_all, m, v, chip, name):
    _, r, c = w.shape

    def body(ch_ref, w_ref, g_ref, m_ref, v_ref, go_ref, d_ref, nm_ref, nv_ref):
        del ch_ref
        for k in range(r):
            g = g_ref[k:k + 1, :]
            go_ref[k] = g
            d_ref[k], nm_ref[k], nv_ref[k] = _adam_math(w_ref[k], g, m_ref[k], v_ref[k])

    res = _pallas(body, name=name, prefetch=1, grid=(1,),
                  in_specs=[_whole((r, 1, c)), _chip_cols(r, c), _whole((r, 1, c)), _whole((r, 1, c))],
                  out_specs=[_whole((r, 1, c))] * 4, out_shape=[_sds((r, 1, c), F32)] * 4)(
                      chip, _taps_first(w), g_all, _taps_first(m), _taps_first(v))
    return tuple(jnp.transpose(a, (1, 0, 2)) for a in res)


def _adamw_halves(name, w, own, other, m, v, core, after):
    r, c = w.shape
    half = r // 2
    t = _tile(half, (128, 352))
    nh = half // t

    def pick(mine):
        def index(i, cr):
            first = cr[0] if mine else 1 - cr[0]
            return (jnp.clip(i - first * nh, 0, nh - 1), 0)
        return pl.BlockSpec((t, c), index)

    def body(c_ref, w_ref, own_ref, oth_ref, m_ref, v_ref, after_ref, g_ref, d_ref, nm_ref, nv_ref):
        del after_ref
        g = jnp.where(pl.program_id(0) // nh == c_ref[0], own_ref[...], oth_ref[...])
        g_ref[...] = g
        d_ref[...], nm_ref[...], nv_ref[...] = _adam_math(w_ref[...], g, m_ref[...], v_ref[...])

    blk = pl.BlockSpec((t, c), lambda i, cr: (i, 0))
    return _pallas(body, name="adamw_" + name, prefetch=1, grid=(2 * nh,),
                   in_specs=[blk, pick(True), pick(False), blk, blk, pl.BlockSpec(memory_space=pl.ANY)], out_specs=[blk] * 4,
                   out_shape=[_sds((r, c), F32)] * 4, semantics=("parallel",))(core, w, own, other, m, v, after)


WEIGHTS = ("c_ctx", "w_mod", "b_mod", "g_norm1", "w_in", "rpb", "conv_w", "conv_b", "ln_g", "ln_b", "w_out", "g_norm2",
           "w_up", "ffn_conv_w", "ffn_conv_b", "w_down", "g_final")
PACK = (("dmod", 6 * D), ("dmod_c", 2 * D), ("g_norm1", D), ("g_norm1_ctx", D), ("g_norm2", D), ("g_final", D),
        ("conv_b", DC), ("ln_g", DC), ("ln_b", DC), ("ffn_conv_b", 2 * DFF), ("ffn_conv_w", 3 * 2 * DFF),
        ("conv_w", CW * DC), ("rpb_rev", NH * 16 * LANES), ("loss", LANES))
PACK_OFF = {}
_o = 0
for _n, _w in PACK:
    PACK_OFF[_n] = (_o, _w)
    _o += _w
PACK_N = -(-_o // (8 * LANES)) * (8 * LANES)
VECTORS = {"b_mod": (6 * D, ("dmod", "dmod_c")), "g_norm1": (D, ("g_norm1", "g_norm1_ctx")), "conv_b": (DC, ("conv_b",)),
           "ln_g": (DC, ("ln_g",)), "ln_b": (DC, ("ln_b",)), "g_norm2": (D, ("g_norm2",)),
           "ffn_conv_b": (2 * DFF, ("ffn_conv_b",)), "g_final": (D, ("g_final",))}
RPB_COLS = 4 * NA_ROWS - 1


PACK_ROW = PACK_N // 8
assert PACK_ROW % LANES == 0 and all(w_ % LANES == 0 for _, w_ in PACK)


def _pack_pieces(off, n):
    pieces, s = [], 0
    while s < n:
        row, col = divmod(off + s, PACK_ROW)
        take = min(n - s, PACK_ROW - col)
        pieces.append((row, col, s, take))
        s += take
    return pieces


def _pack_small(parts, after):
    arrs, places = [], []
    for name, _ in PACK:
        off, width = PACK_OFF[name]
        group = parts[name]
        rows = group[0].shape[0]
        row_w = sum(a.shape[1] for a in group)
        assert rows * row_w == width, (name, rows, row_w, width)
        col = 0
        for a in group:
            arrs.append(a)
            places.append([off + k * row_w + col for k in range(rows)])
            col += a.shape[1]

    def body(*refs):
        o_ref, o8_ref = refs[-2], refs[-1]
        for row, col, _, take in _pack_pieces(_o, PACK_N - _o):
            o_ref[row:row + 1, col:col + take] = jnp.zeros((1, take), F32)
        for ref, offs in zip(refs, places):
            for k, off in enumerate(offs):
                for row, col, s, take in _pack_pieces(off, ref.shape[1]):
                    o_ref[row:row + 1, col:col + take] = ref[k:k + 1, s:s + take]
        _eight_copies(o8_ref, o_ref[...])

    vmem = pl.BlockSpec(memory_space=pltpu.VMEM)
    return _pallas(body, name="pack_small_grads", out_shape=[_sds((8, PACK_ROW), F32), _sds((64, PACK_ROW), F32)],
                   in_specs=[vmem] * len(arrs) + [pl.BlockSpec(memory_space=pl.ANY)] * len(_several(after)),
                   out_specs=[vmem, vmem])(*arrs, *_several(after))


def _small_update(packs, w, m, v):
    names = list(VECTORS)

    def body(*refs):
        it = iter(refs)
        p_ref = next(it)
        wmv = {n: (next(it), next(it), next(it)) for n in names}
        outs = {n: (next(it), next(it), next(it), next(it)) for n in names}
        dmod_ref, cw_ref, fw_ref, rpb_ref, loss_ref = next(it), next(it), next(it), next(it), next(it)

        def segment(d, name):
            pieces = [p_ref[8 * d + row:8 * d + row + 1, col:col + take] for row, col, _, take in _pack_pieces(*PACK_OFF[name])]
            return pieces[0] if len(pieces) == 1 else jnp.concatenate(pieces, axis=1)

        def total(name):
            acc = segment(0, name)
            for d in range(1, 8):
                acc = acc + segment(d, name)
            return acc

        for n in names:
            width, segs = VECTORS[n]
            g = total(segs[0])
            if len(segs) > 1:
                extra = total(segs[1])
                ew = extra.shape[1]
                g = g + extra if ew == width else jnp.concatenate([g[:, :ew] + extra, g[:, ew:]], axis=1)
            w_ref, m_ref, v_ref = wmv[n]
            g_ref, d_ref, nm_ref, nv_ref = outs[n]
            g_ref[...] = g
            d_ref[...], nm_ref[...], nv_ref[...] = _adam_math(w_ref[...], g, m_ref[...], v_ref[...])

        dmod_ref[...] = jnp.zeros_like(dmod_ref)
        for d in range(8):
            dmod_ref[d:d + 1, :] = segment(d, "dmod")
        dmod_ref[8:9, 0:2 * D] = total("dmod_c")
        for ref, name, rows in ((cw_ref, "conv_w", CW), (fw_ref, "ffn_conv_w", 3), (rpb_ref, "rpb_rev", NH * 16)):
            flat = total(name)
            n = ref.shape[1]
            for k in range(rows):
                ref[k:k + 1, :] = flat[:, k * n:(k + 1) * n]
        loss_ref[...] = total("loss")

    ins = [packs] + [a[n] for n in names for a in (w, m, v)]
    out_shape = [_sds((1, VECTORS[n][0]), F32) for n in names for _ in range(4)]
    out_shape += [_sds((COND_ROWS, 6 * D), F32), _sds((CW, DC), F32), _sds((3, 2 * DFF), F32), _sds((NH * 16, LANES), F32),
                  _sds((1, LANES), F32)]
    res = _pallas(body, name="small_update", out_shape=out_shape)(*ins)
    per = {n: tuple(res[4 * i:4 * i + 4]) for i, n in enumerate(names)}
    return (per, *res[4 * len(names):])


def _rpb_update(rev, w, m, v):
    nr = 2 * NA_ROWS - 1
    heads_inside = lambda a: jnp.transpose(a, (0, 2, 1, 3))

    def body(r_ref, w_ref, m_ref, v_ref, g_ref, d_ref, nm_ref, nv_ref):
        li = lax.broadcasted_iota(jnp.int32, (LANES, LANES), 0)
        co = lax.broadcasted_iota(jnp.int32, (LANES, LANES), 1)
        lane_of_co0 = GW - 1 + RPB_COLS // 2
        unflip = jnp.where((li == lane_of_co0 - co) & (co < RPB_COLS), 1.0, 0.0).astype(F32)
        assert NH & (NH - 1) == 0
        regroup = jnp.where((co == (li & (NH - 1)) * 16 + (li >> (NH.bit_length() - 1))) & (li < nr * NH),
                            1.0, 0.0).astype(F32)
        g_all = jnp.dot(jnp.dot(regroup, r_ref[...], preferred_element_type=F32, precision=HI), unflip,
                        preferred_element_type=F32, precision=HI)
        for ro in range(nr):
            g = g_all[ro * NH:(ro + 1) * NH, 0:RPB_COLS]
            g_ref[0, ro] = g
            d_ref[0, ro], nm_ref[0, ro], nv_ref[0, ro] = _adam_math(w_ref[0, ro], g, m_ref[0, ro], v_ref[0, ro])

    res = _pallas(body, name="rpb_update", out_shape=[_sds((1, nr, NH, RPB_COLS), F32)] * 4)(
        rev, heads_inside(w), heads_inside(m), heads_inside(v))
    return tuple(heads_inside(a) for a in res)


def kernel(x, c, ctx, c_ctx, w_mod, b_mod, g_norm1, w_in, rpb, conv_w, conv_b, ln_g, ln_b, w_out, g_norm2, w_up, ffn_conv_w, ffn_conv_b, w_down, g_final, loss_target, m_c_ctx, m_w_mod, m_b_mod, m_g_norm1, m_w_in, m_rpb, m_conv_w, m_conv_b, m_ln_g, m_ln_b, m_w_out, m_g_norm2, m_w_up, m_ffn_conv_w, m_ffn_conv_b, m_w_down, m_g_final, v_c_ctx, v_w_mod, v_b_mod, v_g_norm1, v_w_in, v_rpb, v_conv_w, v_conv_b, v_ln_g, v_ln_b, v_w_out, v_g_norm2, v_w_up, v_ffn_conv_w, v_ffn_conv_b, v_w_down, v_g_final):
    w = dict(c_ctx=c_ctx, w_mod=w_mod, b_mod=b_mod, g_norm1=g_norm1, w_in=w_in, rpb=rpb, conv_w=conv_w, conv_b=conv_b,
             ln_g=ln_g, ln_b=ln_b, w_out=w_out, g_norm2=g_norm2, w_up=w_up, ffn_conv_w=ffn_conv_w, ffn_conv_b=ffn_conv_b,
             w_down=w_down, g_final=g_final)
    mom = dict(c_ctx=m_c_ctx, w_mod=m_w_mod, b_mod=m_b_mod, g_norm1=m_g_norm1, w_in=m_w_in, rpb=m_rpb, conv_w=m_conv_w,
               conv_b=m_conv_b, ln_g=m_ln_g, ln_b=m_ln_b, w_out=m_w_out, g_norm2=m_g_norm2, w_up=m_w_up,
               ffn_conv_w=m_ffn_conv_w, ffn_conv_b=m_ffn_conv_b, w_down=m_w_down, g_final=m_g_final)
    var = dict(c_ctx=v_c_ctx, w_mod=v_w_mod, b_mod=v_b_mod, g_norm1=v_g_norm1, w_in=v_w_in, rpb=v_rpb, conv_w=v_conv_w,
               conv_b=v_conv_b, ln_g=v_ln_g, ln_b=v_ln_b, w_out=v_w_out, g_norm2=v_g_norm2, w_up=v_w_up,
               ffn_conv_w=v_ffn_conv_w, ffn_conv_b=v_ffn_conv_b, w_down=v_w_down, g_final=v_g_final)
    xi, yi, ci = lax.axis_index("x"), lax.axis_index("y"), lax.axis_index("c")
    dev = (4 * xi + 2 * yi + ci).astype(jnp.int32).reshape(1)
    chip = (2 * xi + yi).astype(jnp.int32).reshape(1)
    core = ci.astype(jnp.int32).reshape(1)
    c_ctx2 = c_ctx.reshape(1, D)
    g_final2 = g_final.reshape(1, D)
    mom["g_final"], var["g_final"] = m_g_final.reshape(1, D), v_g_final.reshape(1, D)

    sharing_cond = _share_start(_pack_cond(c, _taps_first(ffn_conv_w), _taps_first(conv_w)), "cond", after=[])
    shards = {n: _cast_into_whole(n, w[n][0], chip) for n in BIG_NAMES}
    cond, ffn_w_all, conv_w_all = _unpack_cond(_share_wait(sharing_cond, list(shards.values()), "cond"), c_ctx2)

    sharing_mod = _share_start(_mod_shard(cond, w_mod[0], b_mod, chip), "mod", after=[], flips=OTHER_CHIPS)

    sems_in, first, _ = _gather_start([shards["w_in"]], ("w_in",), sharing_mod[2], "w_in")
    mod_me, mod_c = _unpack_mod(_share_wait(sharing_mod, [], "mod"), dev)
    late_started = []

    def w_in_all(after):
        arrived = _gather_wait(sems_in, first, ("w_in",), after, "w_in")
        late_started.append(_gather_start([shards[n] for n in LATE_NAMES], LATE_NAMES, arrived[0], "late"))
        return _forward_halves(list(arrived), ("w_in",), "w_in")[0]

    def late_weights(after):
        sems, late, _ = late_started.pop()
        arrived = list(_gather_wait(sems, late, LATE_NAMES, after, "late"))
        (w_out_all,) = _forward_halves(arrived[:1], LATE_NAMES[:1], "w_out")
        fsems, passing, _ = _forward_start(arrived[1:], LATE_NAMES[1:], "ffn", after=w_out_all)
        return w_out_all, lambda after2: _forward_wait(fsems, passing, LATE_NAMES[1:], after2, "ffn")

    rpb_rev = jnp.pad(rpb[0][:, :, ::-1], ((0, 0), (0, 1), (48, LANES - 48 - RPB_COLS))).reshape(NH * 16, LANES)
    vec = dict(g_norm1=g_norm1, g_norm2=g_norm2, g_final=g_final2, conv_w=conv_w_all, conv_b=conv_b, ln_g=ln_g, ln_b=ln_b,
               ffn_conv_w=ffn_w_all, ffn_conv_b=ffn_conv_b)
    started = []

    def begin_early(d_up, d_down):
        started.append(_swap_start([d_up, d_down], EARLY_GRADS, "grad_swap_start_early"))

    def carry_on_early(after):
        sems_, grads_, lands_, _ = started.pop()
        grads_, lands_ = _swap_wait(sems_, grads_, lands_, EARLY_GRADS, after, "grad_swap_wait_early")
        parts_ = [_add_halves(n, grads_[i], lands_[i], core) for i, n in enumerate(EARLY_GRADS)]
        started.append(_exchange_start(parts_, EARLY_GRADS, "grad_exchange_start_early"))
        return started[0][3][0:1, 0:1]

    loss_p, grad_x, d_in, d_out, d_up, d_down, small = _local_step(
        x[0], ctx[0], loss_target[0], mod_me, mod_c, vec, w_in_all, late_weights, rpb_rev, (begin_early, carry_on_early))

    out = {}
    sems_, grads_, lands_, _ = _swap_start([d_in, d_out], LAST_GRADS, "grad_swap_start_last")
    early_own = _reduce_finish(started[0], EARLY_GRADS, grad_x, chip, "early")
    parts = dict(dmod=small["dmod"], dmod_c=small["dmod_c"], g_norm1=[small["g_norm1"][0]], g_norm1_ctx=[small["g_norm1"][1]],
                 g_norm2=[small["g_norm2"]], g_final=[small["g_final"]], conv_b=[small["conv_b"]], ln_g=[small["ln_g"]],
                 ln_b=[small["ln_b"]], ffn_conv_b=small["ffn_conv_b"], ffn_conv_w=small["ffn_conv_w"],
                 conv_w=[small["conv_w"]], rpb_rev=[small["rpb_rev"]], loss=[loss_p])
    sharing = _share_start(_pack_small(parts, after=early_own), "small_grads", after=[])
    grads_, lands_ = _swap_wait(sems_, grads_, lands_, LAST_GRADS, sharing[2], "grad_swap_wait_last")
    parts_ = [_add_halves(n, grads_[i], lands_[i], core) for i, n in enumerate(LAST_GRADS)]
    last_started = _exchange_start(parts_, LAST_GRADS, "grad_exchange_start_last")
    early_other = _send_halves(early_own, "grad_send_early", after=last_started[3])
    for i, n in enumerate(EARLY_GRADS):
        out[n] = _adamw_halves(n, w[n][0], early_own[i], early_other[i], mom[n][0], var[n][0], core, early_other[i])

    packs = _share_wait(sharing, [out[n][1] for n in EARLY_GRADS], "small_grads")
    w2 = dict(w, g_final=g_final2)
    per, dmod_all, g_conv_w_all, g_ffn_w_all, g_rpb_rev, loss_row = _small_update(packs, w2, mom, var)
    out.update(per)
    out["w_mod"] = _mod_weight_update(cond, dmod_all, w_mod[0], m_w_mod[0], v_w_mod[0], chip)

    sharing_c = _share_start(_cond_grad_partial(dmod_all, w_mod[0], chip), "cond_grad", after=out["w_mod"][1])
    last_own = _reduce_finish(last_started, LAST_GRADS, sharing_c[2], chip, "last")
    last_other = _send_halves(last_own, "grad_send_last", after=last_own[0])
    for i, n in enumerate(LAST_GRADS):
        out[n] = _adamw_halves(n, w[n][0], last_own[i], last_other[i], mom[n][0], var[n][0], core, last_other[i])
    out["c_ctx"] = _cond_update(_share_wait(sharing_c, [out[n][1] for n in LAST_GRADS], "cond_grad"),
                                c_ctx2, m_c_ctx.reshape(1, D), v_c_ctx.reshape(1, D))
    out["conv_w"] = _adamw_cols(conv_w, g_conv_w_all, m_conv_w, v_conv_w, chip, "adamw_conv_w")
    out["ffn_conv_w"] = _adamw_cols(ffn_conv_w, g_ffn_w_all, m_ffn_conv_w, v_ffn_conv_w, chip, "adamw_ffn_conv_w")
    out["rpb"] = _rpb_update(g_rpb_rev, rpb, m_rpb, v_rpb)

    res = [[out[n][k].reshape(w[n].shape) for n in WEIGHTS] for k in range(4)]
    return (loss_row[0, 0], grad_x[None], *res[0], *res[1], *res[2], *res[3])
```

```python
import jax
import jax.numpy as jnp
from jax import lax
from jax.experimental import pallas as pl
from jax.experimental.pallas import tpu as pltpu

F32 = jnp.float32
BF16 = jnp.bfloat16
MXU_DTYPE = jnp.bfloat16

D = 1024
CTX = 256
GW = 64
DA = 512
NH = 8
HD = 64
DC = 512
CW = 31
DFF = 2816
NIN = 3 * DA + 2 * DC
EPS = 1e-6
SCALE = HD ** -0.5
NEG = -1e30
NA_ROWS = 8
PAIR_ROWS = NA_ROWS + 1
TAB_BLOCKS = 17
LANES = 128
VMEM_LIMIT = 56 * 1024 * 1024

ADAM_LR = 0.001
ADAM_B1 = 0.9
ADAM_B2 = 0.999
ADAM_EPS = 1e-08
ADAM_WD = 0.01
ADAM_STEP = 10

MESH = pl.DeviceIdType.MESH


def _pallas(body, *, name, semantics=None, vmem=VMEM_LIMIT, prefetch=0, **kw):
    params = dict(vmem_limit_bytes=vmem)
    if semantics is not None:
        params["dimension_semantics"] = semantics
    if prefetch:
        kw["grid_spec"] = pltpu.PrefetchScalarGridSpec(
            num_scalar_prefetch=prefetch, grid=kw.pop("grid"), in_specs=kw.pop("in_specs"), out_specs=kw.pop("out_specs"),
            scratch_shapes=kw.pop("scratch_shapes", ()))
    return pl.pallas_call(body, name=name, compiler_params=pltpu.CompilerParams(**params), **kw)


def _sds(shape, dtype):
    return jax.ShapeDtypeStruct(shape, dtype)


def _vec_spec(n):
    return pl.BlockSpec((1, n), lambda *_: (0, 0))


def _colsum8(x):
    t, n = x.shape
    return jnp.sum(x.reshape(t // 8, 8, n), axis=0)


def _sigmoid(x):
    return 0.5 * jnp.tanh(0.5 * x) + 0.5


def _mm(a, b, *, mode, m, n, k, tm, tn, tk, out_dtype, name, a_off=(0, 0), b_off=(0, 0),
        out_total=None, o_off=(0, 0), into=None):
    a_list = list(a) if isinstance(a, (list, tuple)) else [a]
    b_list = list(b) if isinstance(b, (list, tuple)) else [b]
    assert m % tm == 0 and n % tn == 0 and k % tk == 0, (name, m, n, k, tm, tn, tk)
    gi, gj, nk = m // tm, n // tn, k // tk
    dims = {"nn": (((1,), (0,)), ((), ())), "nt": (((1,), (1,)), ((), ())), "tn": (((0,), (0,)), ((), ()))}[mode]

    if len(a_list) > 1:
        assert mode != "tn" and nk == 1 and sum(x.shape[1] for x in a_list) == k
        a_specs = [pl.BlockSpec((tm, x.shape[1]), lambda i, j, kk: (i, 0)) for x in a_list]
    elif mode == "tn":
        a_specs = [pl.BlockSpec((tk, tm), lambda i, j, kk: (kk + a_off[0], i + a_off[1]))]
    else:
        a_specs = [pl.BlockSpec((tm, tk), lambda i, j, kk: (i + a_off[0], kk + a_off[1]))]
    if len(b_list) > 1:
        assert mode == "tn" and gj == 1 and sum(x.shape[1] for x in b_list) == n
        b_specs = [pl.BlockSpec((tk, x.shape[1]), lambda i, j, kk: (kk, 0)) for x in b_list]
    elif mode == "nt":
        b_specs = [pl.BlockSpec((tn, tk), lambda i, j, kk: (j + b_off[0], kk + b_off[1]))]
    else:
        b_specs = [pl.BlockSpec((tk, tn), lambda i, j, kk: (kk + b_off[0], j + b_off[1]))]

    na, nb = len(a_list), len(b_list)
    in_place = nk > 1 and out_dtype == F32
    n_in = na + nb + (into is not None)

    def body(*refs):
        a_refs, b_refs, o_ref = refs[:na], refs[na:na + nb], refs[n_in]
        acc = o_ref if in_place else (refs[n_in + 1] if nk > 1 else None)
        kk = pl.program_id(2)

        def whole(piece_refs):
            vals = [r[...].astype(MXU_DTYPE) for r in piece_refs]
            return vals[0] if len(vals) == 1 else jnp.concatenate(vals, axis=1)

        p = lax.dot_general(whole(a_refs), whole(b_refs), dims, preferred_element_type=F32)
        if nk == 1:
            o_ref[...] = p.astype(out_dtype)
            return

        @pl.when(kk == 0)
        def _():
            acc[...] = p

        @pl.when(kk > 0)
        def _():
            acc[...] += p

        if not in_place:
            @pl.when(kk == nk - 1)
            def _():
                o_ref[...] = acc[...].astype(out_dtype)

    ins = [*a_list, *b_list]
    in_specs = a_specs + b_specs
    extra = {}
    if into is not None:
        extra["input_output_aliases"] = {len(ins): 0}
        ins.append(into)
        in_specs.append(pl.BlockSpec(memory_space=pl.ANY))
    return _pallas(
        body, name=name, grid=(gi, gj, nk), in_specs=in_specs,
        out_specs=pl.BlockSpec((tm, tn), lambda i, j, kk: (i + o_off[0], j + o_off[1])),
        out_shape=_sds(out_total or (m, n), out_dtype),
        scratch_shapes=[pltpu.VMEM((tm, tn), F32)] if nk > 1 and not in_place else [],
        semantics=("parallel", "parallel", "arbitrary"), **extra,
    )(*ins)


ROW_TILE = 256


def _row_tile(s, most=2):
    for k in (4, 2):
        if k <= most and s % (k * ROW_TILE) == 0:
            return k * ROW_TILE
    return ROW_TILE


def _rmsmod_fwd(x, ctx, g, sc, sh, csc, csh):
    s = x.shape[0]
    nt = s // ROW_TILE
    assert ctx.shape[0] == ROW_TILE

    def body(x_ref, c_ref, g_ref, sc_ref, sh_ref, csc_ref, csh_ref, o_ref):
        is_ctx = pl.program_id(0) == nt
        xv = jnp.where(is_ctx, c_ref[...], x_ref[...])
        scv = jnp.where(is_ctx, csc_ref[...], sc_ref[...])
        shv = jnp.where(is_ctx, csh_ref[...], sh_ref[...])
        r = lax.rsqrt(jnp.mean(xv * xv, axis=-1, keepdims=True) + EPS)
        y = xv * r * g_ref[...]
        o_ref[...] = (y * (1.0 + scv) + shv).astype(o_ref.dtype)

    return _pallas(
        body, name="rmsmod1_fwd", grid=(nt + 1,),
        in_specs=[pl.BlockSpec((ROW_TILE, D), lambda i: (jnp.minimum(i, nt - 1), 0)),
                  pl.BlockSpec((ROW_TILE, D), lambda i: (0, 0))] + [_vec_spec(D)] * 5,
        out_specs=pl.BlockSpec((ROW_TILE, D), lambda i: (i, 0)),
        out_shape=_sds((s + CTX, D), MXU_DTYPE),
        semantics=("arbitrary",),
    )(x, ctx, g, sc, sh, csc, csh)


def _resid_rmsmod_fwd(x, y, gt, g, sc, sh):
    s = x.shape[0]

    def body(x_ref, y_ref, gt_ref, g_ref, sc_ref, sh_ref, x1_ref, h_ref):
        x1 = x_ref[...] + gt_ref[...] * y_ref[...]
        x1_ref[...] = x1
        r = lax.rsqrt(jnp.mean(x1 * x1, axis=-1, keepdims=True) + EPS)
        h_ref[...] = ((x1 * r * g_ref[...]) * (1.0 + sc_ref[...]) + sh_ref[...]).astype(h_ref.dtype)

    t = _row_tile(s)
    row = pl.BlockSpec((t, D), lambda i: (i, 0))
    return _pallas(
        body, name="resid_rmsmod2_fwd", grid=(s // t,),
        in_specs=[row, row] + [_vec_spec(D)] * 4,
        out_specs=[row, row],
        out_shape=[_sds((s, D), F32), _sds((s, D), MXU_DTYPE)],
        semantics=("parallel",),
    )(x, y, gt, g, sc, sh)


def _final_fwd_bwd(x1, z, gt2, gf, tgt):
    s = x1.shape[0]
    tile = _row_tile(s)
    nt = s // tile

    def body(x1_ref, z_ref, gt_ref, gf_ref, t_ref, dx2_ref, dz_ref, loss_ref, dgt_ref, dgf_ref, a_loss, a_gt, a_gf):
        i = pl.program_id(0)

        @pl.when(i == 0)
        def _():
            a_loss[...] = jnp.zeros_like(a_loss)
            a_gt[...] = jnp.zeros_like(a_gt)
            a_gf[...] = jnp.zeros_like(a_gf)

        zv = z_ref[...]
        gt = gt_ref[...]
        gf_ = gf_ref[...]
        x2 = x1_ref[...] + gt * zv
        r = lax.rsqrt(jnp.mean(x2 * x2, axis=-1, keepdims=True) + EPS)
        xn = x2 * r
        e = xn * gf_ - t_ref[...]
        a_loss[...] += _colsum8(e * e)
        dyo = e * (1.0 / D)
        a_gf[...] += _colsum8(dyo * xn)
        gdy = gf_ * dyo
        dx2 = r * gdy - xn * (r * r) * jnp.mean(x2 * gdy, axis=-1, keepdims=True)
        dx2_ref[...] = dx2
        dz_ref[...] = (gt * dx2).astype(dz_ref.dtype)
        a_gt[...] += _colsum8(dx2 * zv)

        @pl.when(i == nt - 1)
        def _():
            tot = jnp.sum(jnp.sum(a_loss[...], axis=0, keepdims=True), axis=1, keepdims=True) * (0.5 / D)
            loss_ref[...] = jnp.broadcast_to(tot, loss_ref.shape)
            dgt_ref[...] = jnp.sum(a_gt[...], axis=0, keepdims=True)
            dgf_ref[...] = jnp.sum(a_gf[...], axis=0, keepdims=True)

    row = pl.BlockSpec((tile, D), lambda i: (i, 0))
    return _pallas(
        body, name="final_norm_loss", grid=(nt,),
        in_specs=[row, row, _vec_spec(D), _vec_spec(D), row],
        out_specs=[row, row, _vec_spec(LANES), _vec_spec(D), _vec_spec(D)],
        out_shape=[_sds((s, D), F32), _sds((s, D), MXU_DTYPE), _sds((1, LANES), F32), _sds((1, D), F32), _sds((1, D), F32)],
        scratch_shapes=[pltpu.VMEM((8, D), F32)] * 3,
        semantics=("arbitrary",),
    )(x1, z, gt2, gf, tgt)


def _rmsmod_bwd(xin, dh, g, sc, *, name, dh_row0=0, add=None, resid=None):
    s = xin.shape[0]
    tile = _row_tile(s)
    nt = s // tile
    want_dx = add is not None
    assert resid is None or want_dx

    def body(*refs):
        it = iter(refs)
        x_ref, dh_ref, g_ref, sc_ref = next(it), next(it), next(it), next(it)
        add_ref = next(it) if want_dx else None
        gt_ref, y_ref = (next(it), next(it)) if resid is not None else (None, None)
        dsh_ref, dsc_ref, dg_ref = next(it), next(it), next(it)
        dx_ref = next(it) if want_dx else None
        dy_ref, dgt_ref = (next(it), next(it)) if resid is not None else (None, None)
        a_sh, a_sc, a_g = next(it), next(it), next(it)
        a_gt = next(it) if resid is not None else None
        i = pl.program_id(0)

        @pl.when(i == 0)
        def _():
            a_sh[...] = jnp.zeros_like(a_sh)
            a_sc[...] = jnp.zeros_like(a_sc)
            a_g[...] = jnp.zeros_like(a_g)
            if a_gt is not None:
                a_gt[...] = jnp.zeros_like(a_gt)

        xv = x_ref[...]
        dhv = dh_ref[...]
        gv = g_ref[...]
        r = lax.rsqrt(jnp.mean(xv * xv, axis=-1, keepdims=True) + EPS)
        xn = xv * r
        a_sh[...] += _colsum8(dhv)
        a_sc[...] += _colsum8(dhv * (xn * gv))
        dn = dhv * (1.0 + sc_ref[...])
        a_g[...] += _colsum8(dn * xn)
        if want_dx:
            gdn = gv * dn
            dx = add_ref[...] + r * gdn - xn * (r * r) * jnp.mean(xv * gdn, axis=-1, keepdims=True)
            dx_ref[...] = dx
            if resid is not None:
                dy_ref[...] = (gt_ref[...] * dx).astype(dy_ref.dtype)
                a_gt[...] += _colsum8(dx * y_ref[...])

        @pl.when(i == nt - 1)
        def _():
            dsh_ref[...] = jnp.sum(a_sh[...], axis=0, keepdims=True)
            dsc_ref[...] = jnp.sum(a_sc[...], axis=0, keepdims=True)
            dg_ref[...] = jnp.sum(a_g[...], axis=0, keepdims=True)
            if a_gt is not None:
                dgt_ref[...] = jnp.sum(a_gt[...], axis=0, keepdims=True)

    row = pl.BlockSpec((tile, D), lambda i: (i, 0))
    ins = [xin, dh, g, sc]
    in_specs = [row, pl.BlockSpec((tile, D), lambda i: (i + dh_row0 // tile, 0)), _vec_spec(D), _vec_spec(D)]
    out_specs = [_vec_spec(D)] * 3
    out_shape = [_sds((1, D), F32)] * 3
    scratch = [pltpu.VMEM((8, D), F32)] * 3
    if want_dx:
        ins.append(add)
        in_specs.append(row)
        out_specs.append(row)
        out_shape.append(_sds((s, D), F32))
    if resid is not None:
        ins += [resid[0], resid[1]]
        in_specs += [_vec_spec(D), row]
        out_specs += [row, _vec_spec(D)]
        out_shape += [_sds((s, D), MXU_DTYPE), _sds((1, D), F32)]
        scratch.append(pltpu.VMEM((8, D), F32))
    return _pallas(body, name=name, grid=(nt,), in_specs=in_specs, out_specs=out_specs, out_shape=out_shape,
                   scratch_shapes=scratch, semantics=("arbitrary",))(*ins)


FF_TILE = 128
FF_CHUNK = 128
HALO = 8


def _shift3(pad_ref, r0, ch):
    return tuple(pad_ref[pl.ds(r0 + HALO + d, ch), :] for d in (-1, 0, 1))


def _fill_padded(pad_ref, src_ref, s, ch, halo):
    zeros = jnp.zeros((halo, pad_ref.shape[1]), F32)
    pad_ref[0:halo, :] = zeros
    pad_ref[s + halo:s + 2 * halo, :] = zeros

    def cp(c, carry):
        r0 = pl.multiple_of(c * ch, ch)
        pad_ref[pl.ds(r0 + halo, ch), :] = src_ref[pl.ds(r0, ch), :].astype(F32)
        return carry

    lax.fori_loop(0, s // ch, cp, 0)


def _ffn_act_fwd(u, w, b):
    s = u.shape[0]
    tile, ch = FF_TILE, FF_CHUNK
    nj = DFF // tile

    def body(ug_ref, uv_ref, wg_ref, wv_ref, bg_ref, bv_ref, f_ref, gpad, vpad):
        _fill_padded(gpad, ug_ref, s, ch, HALO)
        _fill_padded(vpad, uv_ref, s, ch, HALO)

        def conv(pad, w_ref, b_ref, r0):
            prev, cur, nxt = _shift3(pad, r0, ch)
            return w_ref[0:1, :] * prev + w_ref[1:2, :] * cur + w_ref[2:3, :] * nxt + b_ref[...]

        def step(c, carry):
            r0 = pl.multiple_of(c * ch, ch)
            gc = conv(gpad, wg_ref, bg_ref, r0)
            vc = conv(vpad, wv_ref, bv_ref, r0)
            f_ref[pl.ds(r0, ch), :] = (gc * _sigmoid(gc) * vc).astype(f_ref.dtype)
            return carry

        lax.fori_loop(0, s // ch, step, 0)

    col = lambda off: pl.BlockSpec((s, tile), lambda j: (0, j + off))
    wsp = lambda off: pl.BlockSpec((3, tile), lambda j: (0, j + off))
    bsp = lambda off: pl.BlockSpec((1, tile), lambda j: (0, j + off))
    return _pallas(
        body, name="ffn_act_fwd", grid=(nj,),
        in_specs=[col(0), col(nj), wsp(0), wsp(nj), bsp(0), bsp(nj)],
        out_specs=col(0), out_shape=_sds((s, DFF), MXU_DTYPE),
        scratch_shapes=[pltpu.VMEM((s + 2 * HALO, tile), F32)] * 2,
        semantics=("parallel",),
    )(u, u, w, w, b, b)


def _ffn_act_bwd(u, df, w, b):
    s = u.shape[0]
    nj = DFF // FF_TILE
    ch = FF_CHUNK

    def body(ug_ref, uv_ref, df_ref, wg_ref, wv_ref, bg_ref, bv_ref,
             dug_ref, duv_ref, dwg_ref, dwv_ref, dbg_ref, dbv_ref, gpad, vpad, dgpad, dvpad, acc):
        _fill_padded(gpad, ug_ref, s, ch, HALO)
        _fill_padded(vpad, uv_ref, s, ch, HALO)
        zeros = jnp.zeros((HALO, FF_TILE), F32)
        for p in (dgpad, dvpad):
            p[0:HALO, :] = zeros
            p[s + HALO:s + 2 * HALO, :] = zeros
        acc[...] = jnp.zeros_like(acc)

        def step(c, carry):
            r0 = pl.multiple_of(c * ch, ch)
            gs = _shift3(gpad, r0, ch)
            vs = _shift3(vpad, r0, ch)
            gc = wg_ref[0:1, :] * gs[0] + wg_ref[1:2, :] * gs[1] + wg_ref[2:3, :] * gs[2] + bg_ref[...]
            vc = wv_ref[0:1, :] * vs[0] + wv_ref[1:2, :] * vs[1] + wv_ref[2:3, :] * vs[2] + bv_ref[...]
            sg = _sigmoid(gc)
            dfv = df_ref[pl.ds(r0, ch), :].astype(F32)
            dgc = dfv * vc * (sg * (1.0 + gc * (1.0 - sg)))
            dvc = dfv * (gc * sg)
            dgpad[pl.ds(r0 + HALO, ch), :] = dgc
            dvpad[pl.ds(r0 + HALO, ch), :] = dvc
            for t in range(3):
                acc[8 * t:8 * t + 8, :] += _colsum8(dgc * gs[t])
                acc[24 + 8 * t:32 + 8 * t, :] += _colsum8(dvc * vs[t])
            acc[48:56, :] += _colsum8(dgc)
            acc[56:64, :] += _colsum8(dvc)
            return carry

        lax.fori_loop(0, s // ch, step, 0)

        def step2(c, carry):
            r0 = pl.multiple_of(c * ch, ch)
            for pad, w_ref, o_ref in ((dgpad, wg_ref, dug_ref), (dvpad, wv_ref, duv_ref)):
                prev, cur, nxt = _shift3(pad, r0, ch)
                o_ref[pl.ds(r0, ch), :] = (w_ref[0:1, :] * nxt + w_ref[1:2, :] * cur + w_ref[2:3, :] * prev).astype(o_ref.dtype)
            return carry

        lax.fori_loop(0, s // ch, step2, 0)
        for t in range(3):
            dwg_ref[t:t + 1, :] = jnp.sum(acc[8 * t:8 * t + 8, :], axis=0, keepdims=True)
            dwv_ref[t:t + 1, :] = jnp.sum(acc[24 + 8 * t:32 + 8 * t, :], axis=0, keepdims=True)
        dbg_ref[...] = jnp.sum(acc[48:56, :], axis=0, keepdims=True)
        dbv_ref[...] = jnp.sum(acc[56:64, :], axis=0, keepdims=True)

    col = lambda off: pl.BlockSpec((s, FF_TILE), lambda j: (0, j + off))
    wsp = lambda off: pl.BlockSpec((3, FF_TILE), lambda j: (0, j + off))
    bsp = lambda off: pl.BlockSpec((1, FF_TILE), lambda j: (0, j + off))
    return _pallas(
        body, name="ffn_act_bwd", grid=(nj,),
        in_specs=[col(0), col(nj), col(0), wsp(0), wsp(nj), bsp(0), bsp(nj)],
        out_specs=[col(0), col(0), wsp(0), wsp(0), bsp(0), bsp(0)],
        out_shape=[_sds((s, DFF), MXU_DTYPE)] * 2 + [_sds((3, DFF), F32)] * 2 + [_sds((1, DFF), F32)] * 2,
        scratch_shapes=[pltpu.VMEM((s + 2 * HALO, FF_TILE), F32)] * 4 + [pltpu.VMEM((64, FF_TILE), F32)],
        semantics=("parallel",),
    )(u, u, df, w, w, b, b)


CONV_CHUNK = 64
CONV_HALO = 16


def _tap(pad_ref, r0, k):
    return pad_ref[pl.ds(r0 + CONV_HALO - CW // 2 + k, CONV_CHUNK), :]


def _glu_into(pad_ref, a_ref, g_ref, s):
    zeros = jnp.zeros((CONV_HALO, LANES), F32)
    pad_ref[0:CONV_HALO, :] = zeros
    pad_ref[s + CONV_HALO:s + 2 * CONV_HALO, :] = zeros

    def cp(c, carry):
        r0 = pl.multiple_of(c * ROW_TILE, ROW_TILE)
        pad_ref[pl.ds(r0 + CONV_HALO, ROW_TILE), :] = a_ref[pl.ds(r0, ROW_TILE), :] * _sigmoid(g_ref[pl.ds(r0, ROW_TILE), :])
        return carry

    lax.fori_loop(0, s // ROW_TILE, cp, 0)


def _conf_conv_fwd(ag, conv_w, conv_b):
    s = ag.shape[0]
    nc = DC // LANES

    def body(a_ref, g_ref, w_ref, b_ref, o_ref, upad):
        _glu_into(upad, a_ref, g_ref, s)

        def step(c, carry):
            r0 = pl.multiple_of(c * CONV_CHUNK, CONV_CHUNK)
            acc = jnp.broadcast_to(b_ref[...], (CONV_CHUNK, LANES))
            for k in range(CW):
                acc = acc + w_ref[k:k + 1, :] * _tap(upad, r0, k)
            o_ref[pl.ds(r0, CONV_CHUNK), :] = acc
            return carry

        lax.fori_loop(0, s // CONV_CHUNK, step, 0)

    col = lambda off: pl.BlockSpec((s, LANES), lambda c: (0, c + off))
    return _pallas(
        body, name="conf_conv_fwd", grid=(nc,),
        in_specs=[col(0), col(nc), pl.BlockSpec((CW, LANES), lambda c: (0, c)), pl.BlockSpec((1, LANES), lambda c: (0, c))],
        out_specs=col(0), out_shape=_sds((s, DC), F32),
        scratch_shapes=[pltpu.VMEM((s + 2 * CONV_HALO, LANES), F32)],
        semantics=("parallel",),
    )(ag, ag, conv_w, conv_b)


def _ln_stats(x):
    mu = jnp.mean(x, axis=-1, keepdims=True)
    xc = x - mu
    var = jnp.mean(xc * xc, axis=-1, keepdims=True)
    rstd = lax.rsqrt(var + EPS)
    return xc * rstd, rstd


def _conf_ln_fwd(u1, ln_g, ln_b, ycat):
    s = u1.shape[0]

    def body(u_ref, g_ref, b_ref, ycat_ref, o_ref):
        del ycat_ref
        xhat, _ = _ln_stats(u_ref[...])
        y = xhat * g_ref[...] + b_ref[...]
        o_ref[...] = (y * _sigmoid(y)).astype(o_ref.dtype)

    t = _row_tile(s, 4)
    return _pallas(
        body, name="conf_ln_fwd", grid=(s // t,),
        in_specs=[pl.BlockSpec((t, DC), lambda i: (i, 0)), _vec_spec(DC), _vec_spec(DC),
                  pl.BlockSpec(memory_space=pl.ANY)],
        out_specs=pl.BlockSpec((t, DC), lambda i: (i, 1)),
        out_shape=_sds(ycat.shape, ycat.dtype),
        input_output_aliases={3: 0},
        semantics=("parallel",),
    )(u1, ln_g, ln_b, ycat)


def _conf_ln_bwd(dycat, u1, ln_g, ln_b):
    s = u1.shape[0]
    t = _row_tile(s, 4)
    nt = s // t

    def body(dy_ref, u_ref, g_ref, b_ref, du_ref, dg_ref, db_ref, a_g, a_b):
        i = pl.program_id(0)

        @pl.when(i == 0)
        def _():
            a_g[...] = jnp.zeros_like(a_g)
            a_b[...] = jnp.zeros_like(a_b)

        xhat, rstd = _ln_stats(u_ref[...])
        gv = g_ref[...]
        y = xhat * gv + b_ref[...]
        sg = _sigmoid(y)
        dyl = dy_ref[...] * (sg * (1.0 + y * (1.0 - sg)))
        a_g[...] += _colsum8(dyl * xhat)
        a_b[...] += _colsum8(dyl)
        dxh = dyl * gv
        du_ref[...] = rstd * (dxh - jnp.mean(dxh, axis=-1, keepdims=True)
                              - xhat * jnp.mean(dxh * xhat, axis=-1, keepdims=True))

        @pl.when(i == nt - 1)
        def _():
            dg_ref[...] = jnp.sum(a_g[...], axis=0, keepdims=True)
            db_ref[...] = jnp.sum(a_b[...], axis=0, keepdims=True)

    return _pallas(
        body, name="conf_ln_bwd", grid=(nt,),
        in_specs=[pl.BlockSpec((t, DC), lambda i: (i, 1)), pl.BlockSpec((t, DC), lambda i: (i, 0)),
                  _vec_spec(DC), _vec_spec(DC)],
        out_specs=[pl.BlockSpec((t, DC), lambda i: (i, 0)), _vec_spec(DC), _vec_spec(DC)],
        out_shape=[_sds((s, DC), F32), _sds((1, DC), F32), _sds((1, DC), F32)],
        scratch_shapes=[pltpu.VMEM((8, DC), F32)] * 2,
        semantics=("arbitrary",),
    )(dycat, u1, ln_g, ln_b)


def _conf_conv_bwd(ag, du1, conv_w, rows_out):
    s = ag.shape[0]
    nc = DC // LANES

    def body(a_ref, g_ref, d_ref, w_ref, da_ref, dg_ref, dw_ref, db_ref, upad, dpad, acc):
        _glu_into(upad, a_ref, g_ref, s)
        _fill_padded(dpad, d_ref, s, ROW_TILE, CONV_HALO)
        acc[...] = jnp.zeros_like(acc)

        def step(c, carry):
            r0 = pl.multiple_of(c * CONV_CHUNK, CONV_CHUNK)
            dcur = dpad[pl.ds(r0 + CONV_HALO, CONV_CHUNK), :]
            du0 = jnp.zeros((CONV_CHUNK, LANES), F32)
            for k in range(CW):
                du0 = du0 + w_ref[k:k + 1, :] * _tap(dpad, r0, CW - 1 - k)
                acc[8 * k:8 * k + 8, :] += _colsum8(dcur * _tap(upad, r0, k))
            acc[8 * CW:8 * CW + 8, :] += _colsum8(dcur)
            av = a_ref[pl.ds(r0, CONV_CHUNK), :]
            sg = _sigmoid(g_ref[pl.ds(r0, CONV_CHUNK), :])
            da_ref[pl.ds(r0, CONV_CHUNK), :] = (du0 * sg).astype(da_ref.dtype)
            dg_ref[pl.ds(r0, CONV_CHUNK), :] = (du0 * av * (sg * (1.0 - sg))).astype(dg_ref.dtype)
            return carry

        lax.fori_loop(0, s // CONV_CHUNK, step, 0)
        if rows_out > s:
            zeros = jnp.zeros((rows_out - s, LANES), da_ref.dtype)
            da_ref[s:rows_out, :] = zeros
            dg_ref[s:rows_out, :] = zeros
        for k in range(CW):
            dw_ref[k:k + 1, :] = jnp.sum(acc[8 * k:8 * k + 8, :], axis=0, keepdims=True)
        db_ref[...] = jnp.sum(acc[8 * CW:8 * CW + 8, :], axis=0, keepdims=True)

    col = lambda off: pl.BlockSpec((s, LANES), lambda c: (0, c + off))
    ocol = pl.BlockSpec((rows_out, LANES), lambda c: (0, c))
    return _pallas(
        body, name="conf_conv_bwd", grid=(nc,),
        in_specs=[col(0), col(nc), col(0), pl.BlockSpec((CW, LANES), lambda c: (0, c))],
        out_specs=[ocol, ocol, pl.BlockSpec((CW, LANES), lambda c: (0, c)), pl.BlockSpec((1, LANES), lambda c: (0, c))],
        out_shape=[_sds((rows_out, DC), MXU_DTYPE)] * 2 + [_sds((CW, DC), F32), _sds((1, DC), F32)],
        scratch_shapes=[pltpu.VMEM((s + 2 * CONV_HALO, LANES), F32)] * 2 + [pltpu.VMEM((8 * (CW + 1), LANES), F32)],
        semantics=("parallel",),
    )(ag, ag, du1, conv_w)


Q_TILE = 2 * GW
K_WIN = PAIR_ROWS * GW


def _bias_table(rpb_rev, after=()):
    def body(p_ref, *refs):
        t_ref = refs[-1]
        kcol = lax.broadcasted_iota(jnp.int32, (GW, LANES), 0)
        lane = lax.broadcasted_iota(jnp.int32, (GW, LANES), 1)
        qcol = lane % GW
        cs = jnp.clip(qcol - NA_ROWS, 0, GW - 2 * NA_ROWS)
        colvalid = (kcol >= cs) & (kcol < cs + 2 * NA_ROWS)
        neg = jnp.full((GW, LANES), NEG, F32)

        def skew(h, ro, shift):
            if ro < 0 or ro >= 2 * NA_ROWS - 1:
                return neg
            row = jnp.broadcast_to(p_ref[h * 16 + ro:h * 16 + ro + 1, :], (GW, LANES))
            return pltpu.roll(row, shift, 1, stride=1, stride_axis=0)

        for h in range(NH):
            for b in range(TAB_BLOCKS):
                val = jnp.where(lane < GW, skew(h, b - 1, GW + 1), skew(h, b - 2, 1))
                t_ref[h, b * GW:(b + 1) * GW, :] = jnp.where(colvalid, val, neg)

    return _pallas(body, name="attn_bias_table", out_shape=_sds((NH, TAB_BLOCKS * GW, LANES), F32),
                   in_specs=[pl.BlockSpec(memory_space=pltpu.VMEM)] + [pl.BlockSpec(memory_space=pl.ANY)] * len(_several(after)),
                   out_specs=pl.BlockSpec(memory_space=pltpu.VMEM))(rpb_rev, *_several(after))


def _rpb_grad(tt):
    def body(t_ref, o_ref):
        lane = lax.broadcasted_iota(jnp.int32, (GW, LANES), 1)
        si = lax.broadcasted_iota(jnp.int32, (GW, GW), 0)
        ti = lax.broadcasted_iota(jnp.int32, (GW, GW), 1)
        flip = jnp.where(si + ti == GW - 1, 1.0, 0.0).astype(F32)
        o_ref[...] = jnp.zeros_like(o_ref)
        for h in range(NH):
            for ro in range(2 * NA_ROWS - 1):
                lo = t_ref[h, (ro + 1) * GW:(ro + 2) * GW, :]
                hi = t_ref[h, (ro + 2) * GW:(ro + 3) * GW, :]
                g = jnp.where(lane < GW, lo + pltpu.roll(hi, GW, 1), 0.0)
                gf = jnp.dot(flip, g, preferred_element_type=F32, precision=lax.Precision.HIGHEST)
                sk = pltpu.roll(gf, 0, 1, stride=1, stride_axis=0)
                o_ref[h * 16 + ro:h * 16 + ro + 1, :] = jnp.sum(sk, axis=0, keepdims=True)

    return _pallas(body, name="attn_rpb_grad", out_shape=_sds((NH * 16, LANES), F32))(tt)


def _attn_geometry(i, rows):
    wsp = jnp.clip(2 * i - NA_ROWS // 2, 0, rows - PAIR_ROWS)
    k0 = pl.multiple_of(wsp * GW, GW)
    t0 = pl.multiple_of((wsp - 2 * i + NA_ROWS) * GW, GW)
    rr = lax.broadcasted_iota(jnp.int32, (GW, Q_TILE), 1) // GW
    wsr = jnp.clip(2 * i + rr - NA_ROWS // 2, 0, rows - NA_ROWS)
    edge_masks = tuple(jnp.where((kr >= wsr) & (kr < wsr + NA_ROWS), 0.0, NEG).astype(F32)
                       for kr in (wsp, wsp + PAIR_ROWS - 1))
    return k0, t0, edge_masks


def _biased(s_raw, bias, edge_masks):
    x = s_raw + bias
    return jnp.concatenate([x[:GW] + edge_masks[0], x[GW:K_WIN - GW], x[K_WIN - GW:] + edge_masks[1]], axis=0)


def _two_heads_on_lanes(xt):
    feat = lax.broadcasted_iota(jnp.int32, xt.shape, 0)
    zero = jnp.zeros_like(xt)
    return jnp.concatenate([jnp.where(feat < HD, xt, zero), jnp.where(feat >= HD, xt, zero)], axis=1)


def _two_heads_on_rows(x):
    lane = lax.broadcasted_iota(jnp.int32, x.shape, 1)
    zero = jnp.zeros_like(x)
    return jnp.concatenate([jnp.where(lane < HD, x, zero), jnp.where(lane >= HD, x, zero)], axis=0)


def _pick_heads(x2):
    n = x2.shape[0] // 2
    lane = lax.broadcasted_iota(jnp.int32, (n, LANES), 1)
    return jnp.where(lane < HD, x2[:n], x2[n:])


_TN = (((0,), (0,)), ((), ()))


def _attn_fwd(qkv, tab, s):
    rows = s // GW
    npair = rows // 2

    def body(q_ref, kv_ref, tab_ref, o_ref, lse_ref):
        i = pl.program_id(0)
        k0, t0, edge_masks = _attn_geometry(i, rows)
        for p in range(NH // 2):
            cq = slice(p * LANES, (p + 1) * LANES)
            ck = slice(DA + p * LANES, DA + (p + 1) * LANES)
            cv = slice(2 * DA + p * LANES, 2 * DA + (p + 1) * LANES)
            qm2 = _two_heads_on_lanes(q_ref[:, cq].T) * SCALE
            s_loc = jnp.dot(kv_ref[pl.ds(k0, K_WIN), ck], qm2, preferred_element_type=F32)
            s_ctx = jnp.dot(kv_ref[pl.ds(s, CTX), ck], qm2, preferred_element_type=F32)
            p_loc, p_ctx = [], []
            for hh in range(2):
                h = 2 * p + hh
                ch = slice(hh * Q_TILE, (hh + 1) * Q_TILE)
                sl = _biased(s_loc[:, ch], tab_ref[h, pl.ds(t0, K_WIN), :], edge_masks)
                sc = s_ctx[:, ch]
                m = jnp.maximum(jnp.max(sl, axis=0, keepdims=True), jnp.max(sc, axis=0, keepdims=True))
                el = jnp.exp(sl - m)
                ec = jnp.exp(sc - m)
                l = jnp.sum(el, axis=0, keepdims=True) + jnp.sum(ec, axis=0, keepdims=True)
                inv = 1.0 / l
                lse_ref[h:h + 1, :] = m + jnp.log(l)
                p_loc.append((el * inv).astype(MXU_DTYPE))
                p_ctx.append((ec * inv).astype(MXU_DTYPE))
            o2 = (lax.dot_general(jnp.concatenate(p_loc, axis=1), kv_ref[pl.ds(k0, K_WIN), cv], _TN, preferred_element_type=F32)
                  + lax.dot_general(jnp.concatenate(p_ctx, axis=1), kv_ref[pl.ds(s, CTX), cv], _TN, preferred_element_type=F32))
            o_ref[:, cq] = _pick_heads(o2).astype(o_ref.dtype)

    return _pallas(
        body, name="attn_fwd", grid=(npair,),
        in_specs=[pl.BlockSpec((Q_TILE, DA), lambda i: (i, 0)), pl.BlockSpec(memory_space=pltpu.VMEM),
                  pl.BlockSpec(memory_space=pltpu.VMEM)],
        out_specs=[pl.BlockSpec((Q_TILE, DA), lambda i: (i, 0)), pl.BlockSpec((NH, Q_TILE), lambda i: (0, i))],
        out_shape=[_sds((s, D), MXU_DTYPE), _sds((NH, s), F32)],
        semantics=("arbitrary",),
    )(qkv, qkv, tab)


def _attn_bwd(qkv, tab, lse, dycat, s):
    rows = s // GW
    npair = rows // 2
    sa = s + CTX
    nzero = CTX // Q_TILE

    def body(q_ref, do_ref, lse_ref, kv_ref, tab_ref, dq_ref, dkv_ref, tt_ref, dk_acc, dv_acc):
        i = pl.program_id(0)

        @pl.when(i == 0)
        def _():
            dk_acc[...] = jnp.zeros_like(dk_acc)
            dv_acc[...] = jnp.zeros_like(dv_acc)
            tt_ref[...] = jnp.zeros_like(tt_ref)

        @pl.when(i >= npair)
        def _():
            dq_ref[...] = jnp.zeros_like(dq_ref)

        @pl.when(i < npair)
        def _():
            k0, t0, edge_masks = _attn_geometry(i, rows)
            for p in range(NH // 2):
                cq = slice(p * LANES, (p + 1) * LANES)
                ck = slice(DA + p * LANES, DA + (p + 1) * LANES)
                cv = slice(2 * DA + p * LANES, 2 * DA + (p + 1) * LANES)
                qp = q_ref[:, cq] * SCALE
                dop = do_ref[:, cq].astype(MXU_DTYPE)
                qm2 = _two_heads_on_lanes(qp.T)
                dom2 = _two_heads_on_lanes(dop.T)
                kw = kv_ref[pl.ds(k0, K_WIN), ck]
                kc = kv_ref[pl.ds(s, CTX), ck]
                vw = kv_ref[pl.ds(k0, K_WIN), cv]
                vc = kv_ref[pl.ds(s, CTX), cv]
                s_loc = jnp.dot(kw, qm2, preferred_element_type=F32)
                s_ctx = jnp.dot(kc, qm2, preferred_element_type=F32)
                dp_loc = jnp.dot(vw, dom2, preferred_element_type=F32)
                dp_ctx = jnp.dot(vc, dom2, preferred_element_type=F32)
                p_loc, p_ctx, ds_loc, ds_ctx = [], [], [], []
                for hh in range(2):
                    h = 2 * p + hh
                    ch = slice(hh * Q_TILE, (hh + 1) * Q_TILE)
                    lse_h = lse_ref[h:h + 1, :]
                    pl_ = jnp.exp(_biased(s_loc[:, ch], tab_ref[h, pl.ds(t0, K_WIN), :], edge_masks) - lse_h)
                    pc_ = jnp.exp(s_ctx[:, ch] - lse_h)
                    dpl = dp_loc[:, ch]
                    dpc = dp_ctx[:, ch]
                    delta = jnp.sum(pl_ * dpl, axis=0, keepdims=True) + jnp.sum(pc_ * dpc, axis=0, keepdims=True)
                    dsl = pl_ * (dpl - delta)
                    dsc = pc_ * (dpc - delta)
                    tt_ref[h, pl.ds(t0, K_WIN), :] += dsl
                    p_loc.append(pl_.astype(MXU_DTYPE))
                    p_ctx.append(pc_.astype(MXU_DTYPE))
                    ds_loc.append(dsl.astype(MXU_DTYPE))
                    ds_ctx.append(dsc.astype(MXU_DTYPE))
                p_loc, p_ctx = jnp.concatenate(p_loc, axis=1), jnp.concatenate(p_ctx, axis=1)
                ds_loc, ds_ctx = jnp.concatenate(ds_loc, axis=1), jnp.concatenate(ds_ctx, axis=1)
                do_rows = _two_heads_on_rows(dop)
                q_rows = _two_heads_on_rows(qp)
                dv_acc[pl.ds(k0, K_WIN), cq] += jnp.dot(p_loc, do_rows, preferred_element_type=F32)
                dv_acc[pl.ds(s, CTX), cq] += jnp.dot(p_ctx, do_rows, preferred_element_type=F32)
                dk_acc[pl.ds(k0, K_WIN), cq] += jnp.dot(ds_loc, q_rows, preferred_element_type=F32)
                dk_acc[pl.ds(s, CTX), cq] += jnp.dot(ds_ctx, q_rows, preferred_element_type=F32)
                dq2 = (lax.dot_general(ds_loc, kw, _TN, preferred_element_type=F32)
                       + lax.dot_general(ds_ctx, kc, _TN, preferred_element_type=F32))
                dq_ref[:, cq] = (_pick_heads(dq2) * SCALE).astype(dq_ref.dtype)

        @pl.when(i == npair - 1)
        def _():
            def cp(c, carry):
                r0 = pl.multiple_of(c * ROW_TILE, ROW_TILE)
                dkv_ref[pl.ds(r0, ROW_TILE), 0:DA] = dk_acc[pl.ds(r0, ROW_TILE), :].astype(dkv_ref.dtype)
                dkv_ref[pl.ds(r0, ROW_TILE), DA:2 * DA] = dv_acc[pl.ds(r0, ROW_TILE), :].astype(dkv_ref.dtype)
                return carry

            lax.fori_loop(0, sa // ROW_TILE, cp, 0)

    qmap = lambda i: (jnp.minimum(i, npair - 1), 0)
    return _pallas(
        body, name="attn_bwd", grid=(npair + nzero,),
        in_specs=[pl.BlockSpec((Q_TILE, DA), qmap), pl.BlockSpec((Q_TILE, DA), qmap),
                  pl.BlockSpec((NH, Q_TILE), lambda i: (0, jnp.minimum(i, npair - 1))),
                  pl.BlockSpec(memory_space=pltpu.VMEM), pl.BlockSpec(memory_space=pltpu.VMEM)],
        out_specs=[pl.BlockSpec((Q_TILE, DA), lambda i: (i, 0)), pl.BlockSpec(memory_space=pltpu.VMEM),
                   pl.BlockSpec(memory_space=pltpu.VMEM)],
        out_shape=[_sds((sa, DA), MXU_DTYPE), _sds((sa, 2 * DA), MXU_DTYPE), _sds((NH, TAB_BLOCKS * GW, LANES), F32)],
        scratch_shapes=[pltpu.VMEM((sa, DA), F32)] * 2,
        semantics=("arbitrary",),
    )(qkv, dycat, lse, qkv, tab)


def _tile(n, prefs):
    for t in prefs:
        if n % t == 0:
            return t
    raise ValueError((n, prefs))


def _local_step(x, ctx, tgt, mod, mod_c, vec, w_in, late_weights, tab, early_grads=None):
    s = x.shape[0]
    sa = s + CTX
    ts = _tile(s, (1024, 512, 256))
    ts2 = _tile(s, (2048, 1024, 512, 256))
    tsa = _tile(sa, (1088, 640, 256))
    tsa2 = _tile(sa, (2176, 640, 256))
    sh1, sc1, gt1, sh2, sc2, gt2 = (mod[i:i + 1] for i in range(6))
    csh1, csc1 = mod_c[0:1], mod_c[1:2]
    act = MXU_DTYPE

    h_all = _rmsmod_fwd(x, ctx, vec["g_norm1"], sc1, sh1, csc1, csh1)
    w_in = w_in(h_all) if callable(w_in) else w_in
    qkv = _mm(h_all, w_in, mode="nn", m=sa, n=3 * DA, k=D, tm=tsa2, tn=512, tk=D, out_dtype=MXU_DTYPE, name="mm_qkv")
    ag = _mm(h_all, w_in, mode="nn", m=s, n=2 * DC, k=D, tm=ts2, tn=512, tk=D, out_dtype=F32, name="mm_ag", b_off=(0, 3))
    ycat, lse = _attn_fwd(qkv, tab, s)
    u1 = _conf_conv_fwd(ag, vec["conv_w"], vec["conv_b"])
    ycat = _conf_ln_fwd(u1, vec["ln_g"], vec["ln_b"], ycat)
    if callable(late_weights):
        w_out, ffn_weights = late_weights(ycat)
    else:
        w_out, ffn_weights = late_weights[0], late_weights[1:]
    y = _mm(ycat, w_out, mode="nn", m=s, n=D, k=D, tm=ts2, tn=512, tk=D, out_dtype=F32, name="mm_out")
    x1, h2 = _resid_rmsmod_fwd(x, y, gt1, vec["g_norm2"], sc2, sh2)
    w_up, w_down = ffn_weights(h2) if callable(ffn_weights) else ffn_weights
    u = _mm(h2, w_up, mode="nn", m=s, n=2 * DFF, k=D, tm=ts2, tn=512, tk=D, out_dtype=act, name="mm_up")
    f = _ffn_act_fwd(u, vec["ffn_conv_w"], vec["ffn_conv_b"])
    z = _mm(f, w_down, mode="nn", m=s, n=D, k=DFF, tm=ts, tn=D, tk=DFF, out_dtype=F32, name="mm_down")
    dx2, dz, loss, dgt2, dgf = _final_fwd_bwd(x1, z, gt2, vec["g_final"], tgt)

    df = _mm(dz, w_down, mode="nt", m=s, n=DFF, k=D, tm=ts, tn=DFF, tk=D, out_dtype=act, name="mm_down_dx")
    d_w_down = _mm(f, dz, mode="tn", m=DFF, n=D, k=s, tm=DFF // 2, tn=D, tk=ts2, out_dtype=F32, name="mm_down_dw")
    dug, duv, dfw_g, dfw_v, dfb_g, dfb_v = _ffn_act_bwd(u, df, vec["ffn_conv_w"], vec["ffn_conv_b"])
    dw_kw = dict(mode="tn", m=D, n=DFF, k=s, tm=D, tn=DFF, tk=ts, out_dtype=F32, out_total=(D, 2 * DFF))
    d_w_up = _mm(h2, dug, name="mm_up_dw_gate", **dw_kw)
    d_w_up = _mm(h2, duv, name="mm_up_dw_val", o_off=(0, 1), into=d_w_up, **dw_kw)
    if early_grads is not None:
        early_grads[0](d_w_up, d_w_down)
    dh2 = _mm([dug, duv], w_up, mode="nt", m=s, n=D, k=2 * DFF, tm=ts, tn=D, tk=2 * DFF, out_dtype=F32, name="mm_up_dx")
    sc2_b = sc2 if early_grads is None else sc2 + early_grads[1](dh2)
    dsh2, dsc2, dg2, dx1, dy, dgt1 = _rmsmod_bwd(x1, dh2, vec["g_norm2"], sc2_b, name="rmsmod2_bwd", add=dx2, resid=(gt1, y))
    dycat = _mm(dy, w_out, mode="nt", m=s, n=D, k=D, tm=ts2, tn=512, tk=D, out_dtype=F32, name="mm_out_dx")
    d_w_out = _mm(ycat, dy, mode="tn", m=D, n=D, k=s, tm=D, tn=D, tk=ts, out_dtype=F32, name="mm_out_dw")
    du1, dln_g, dln_b = _conf_ln_bwd(dycat, u1, vec["ln_g"], vec["ln_b"])
    da, dg, dconv_w, dconv_b = _conf_conv_bwd(ag, du1, vec["conv_w"], sa)
    dq, dkv, tt = _attn_bwd(qkv, tab, lse, dycat, s)
    drpb_rev = _rpb_grad(tt)
    d_pieces = [dq, dkv, da, dg]
    dh = _mm(d_pieces, w_in, mode="nt", m=sa, n=D, k=NIN, tm=tsa, tn=D, tk=NIN, out_dtype=F32, name="mm_in_dx")
    d_w_in = _mm(h_all, d_pieces, mode="tn", m=D, n=NIN, k=sa, tm=D, tn=NIN, tk=tsa, out_dtype=F32, name="mm_in_dw")
    dsh1, dsc1, dg1, grad_x = _rmsmod_bwd(x, dh, vec["g_norm1"], sc1, name="rmsmod1_bwd", add=dx1)
    dcsh1, dcsc1, dg1c = _rmsmod_bwd(ctx, dh, vec["g_norm1"], csc1, name="rmsmod1_ctx_bwd", dh_row0=s)

    small = dict(
        dmod=[dsh1, dsc1, dgt1, dsh2, dsc2, dgt2], dmod_c=[dcsh1, dcsc1],
        g_norm1=[dg1, dg1c], g_norm2=dg2, g_final=dgf, conv_b=dconv_b, ln_g=dln_g, ln_b=dln_b, conv_w=dconv_w,
        ffn_conv_w=[dfw_g, dfw_v], ffn_conv_b=[dfb_g, dfb_v], rpb_rev=drpb_rev,
    )
    return loss, grad_x, d_w_in, d_w_out, d_w_up, d_w_down, small


N_CHIPS = 4
HBM = pl.BlockSpec(memory_space=pl.ANY)
BIG = {"w_in": ("col", (D, NIN)), "w_out": ("row", (D, D)), "w_up": ("col", (D, 2 * DFF)), "w_down": ("row", (DFF, D))}
BIG_NAMES = tuple(BIG)
LATE_NAMES = ("w_out", "w_up", "w_down")


def _shard_shape(name):
    kind, (r, c) = BIG[name]
    return (r, c // N_CHIPS) if kind == "col" else (r // N_CHIPS, c)


def _half_rows(name):
    return _shard_shape(name)[0] // 2


def _place():
    x, y, c = lax.axis_index("x"), lax.axis_index("y"), lax.axis_index("c")
    others = [(1 - x, y), (x, 1 - y), (1 - x, 1 - y)]
    return x, y, c, 2 * x + y, (x, y, 1 - c), others


def _whole_region(ref, name, chip, half):
    kind, _ = BIG[name]
    r, c = _shard_shape(name)
    if kind == "col":
        return ref.at[pl.ds(half * (r // 2), r // 2), pl.ds(chip * c, c)]
    return ref.at[pl.ds(chip * r + half * (r // 2), r // 2), :]


def _remote(src, dst, send_sem, recv_sem, to):
    return pltpu.make_async_remote_copy(src_ref=src, dst_ref=dst, send_sem=send_sem, recv_sem=recv_sem,
                                        device_id=to, device_id_type=MESH)


def _cast_into_whole(name, shard, chip):
    kind, whole = BIG[name]
    r, c = shard.shape
    if kind == "col":
        tr = 256
        o_spec = pl.BlockSpec((tr, c), lambda i, ch: (i, ch[0]))
    else:
        tr = _tile(r, (128, 352))
        o_spec = pl.BlockSpec((tr, c), lambda i, ch: (ch[0] * (r // tr) + i, 0))

    def body(ch_ref, x_ref, o_ref):
        del ch_ref
        o_ref[...] = x_ref[...].astype(o_ref.dtype)

    return _pallas(body, name="cast_" + name, prefetch=1, grid=(r // tr,),
                   in_specs=[pl.BlockSpec((tr, c), lambda i, ch: (i, 0))], out_specs=o_spec,
                   out_shape=_sds(whole, MXU_DTYPE), semantics=("parallel",))(chip, shard)


SEM = pl.BlockSpec(memory_space=pltpu.SEMAPHORE)
IN_HBM = pl.BlockSpec(memory_space=pltpu.HBM)
DATAFLOW = pltpu.SideEffectType.DATAFLOW_SIDE_EFFECTING


def _keep_in_hbm(a):
    return pltpu.with_memory_space_constraint(a, pltpu.HBM)


def _several(after):
    return list(after) if isinstance(after, (list, tuple)) else [after]


FLIPS = [(dx, dy, dc) for dx in (0, 1) for dy in (0, 1) for dc in (0, 1)][1:]
OTHER_CHIPS = [f for f in FLIPS if f[2] == 0]


def _flipped(flip):
    x, y, c = lax.axis_index("x"), lax.axis_index("y"), lax.axis_index("c")
    return tuple(1 - v if f else v for v, f in zip((x, y, c), flip))


def _eight_copies(o_ref, val):
    r = val.shape[0]
    for d in range(8):
        o_ref[d * r:(d + 1) * r, :] = val


def _share_start(block, tag, after, flips=FLIPS):
    v, land = block
    r, n = v.shape
    assert land.shape == (8 * r, n)
    ns = 2 * len(flips)

    def body(*refs):
        v_ref, land_ref = refs[0], refs[1]
        sems = refs[2 + len(_several(after)):2 + len(_several(after)) + ns]
        x, y, c = lax.axis_index("x"), lax.axis_index("y"), lax.axis_index("c")
        mine = land_ref.at[pl.ds((4 * x + 2 * y + c) * r, r), :]
        for k, flip in enumerate(flips):
            _remote(v_ref, mine, sems[2 * k], sems[2 * k + 1], _flipped(flip)).start()

    res = pl.pallas_call(
        body, name="share_" + tag + "_start",
        out_shape=(*[pltpu.SemaphoreType.DMA(())] * ns, pltpu.HBM(v.shape, v.dtype), pltpu.HBM((8 * r, n), v.dtype)),
        in_specs=[IN_HBM] * 2 + [pl.BlockSpec(memory_space=pl.ANY)] * len(_several(after)),
        out_specs=(*[SEM] * ns, IN_HBM, IN_HBM),
        input_output_aliases={0: ns, 1: ns + 1},
        compiler_params=pltpu.CompilerParams(has_side_effects=DATAFLOW),
    )(_keep_in_hbm(v), _keep_in_hbm(land), *_several(after))
    return list(res[:ns]), res[ns], res[ns + 1], flips


def _share_wait(started, after, tag):
    sems, v, land, flips = started
    r = v.shape[0]
    ns = len(sems)

    def body(*refs):
        v_ref, land_ref = refs[0], refs[1]
        sem_refs = refs[2:2 + ns]
        for k, flip in enumerate(flips):
            px, py, pc = _flipped(flip)
            theirs = land_ref.at[pl.ds((4 * px + 2 * py + pc) * r, r), :]
            cp = _remote(v_ref, theirs, sem_refs[2 * k], sem_refs[2 * k + 1], (px, py, pc))
            cp.wait_send()
            cp.wait_recv()

    res = pl.pallas_call(
        body, name="share_" + tag + "_wait",
        out_shape=(pltpu.HBM(v.shape, v.dtype), pltpu.HBM(land.shape, land.dtype)),
        in_specs=[IN_HBM] * 2 + [SEM] * ns + [pl.BlockSpec(memory_space=pl.ANY)] * len(_several(after)),
        out_specs=(IN_HBM, IN_HBM),
        input_output_aliases={0: 0, 1: 1},
        compiler_params=pltpu.CompilerParams(has_side_effects=DATAFLOW),
    )(v, land, *sems, *_several(after))
    return res[1]


def _gather_start(wholes, names, after, tag):
    nw = len(names)
    ns = 2 * 3 * nw

    def body(*refs):
        ins = refs[:nw]
        sems = refs[nw + 1:nw + 1 + ns]
        token = refs[2 * nw + ns + 1]
        _, _, c, chip, _, others = _place()
        for w, name in enumerate(names):
            mine = _whole_region(ins[w], name, chip, c)
            for t, (ox, oy) in enumerate(others):
                k = 2 * (3 * w + t)
                _remote(mine, mine, sems[k], sems[k + 1], (ox, oy, c)).start()
        token[...] = jnp.zeros_like(token)

    res = pl.pallas_call(
        body, name="gather_" + tag + "_start",
        out_shape=(*[pltpu.SemaphoreType.DMA(())] * ns, *[pltpu.HBM(a.shape, a.dtype) for a in wholes], _sds((8, LANES), F32)),
        in_specs=[IN_HBM] * nw + [pl.BlockSpec(memory_space=pl.ANY)],
        out_specs=(*[SEM] * ns, *[IN_HBM] * nw, pl.BlockSpec(memory_space=pltpu.VMEM)),
        input_output_aliases={i: ns + i for i in range(nw)},
        compiler_params=pltpu.CompilerParams(has_side_effects=DATAFLOW),
    )(*[_keep_in_hbm(a) for a in wholes], after)
    return list(res[:ns]), list(res[ns:ns + nw]), res[ns + nw]


def _gather_wait(sems, wholes, names, after, tag):
    nw = len(names)
    ns = len(sems)

    def body(*refs):
        ins = refs[:nw]
        sem_refs = refs[nw:nw + ns]
        _, _, c, chip, _, others = _place()
        for w, name in enumerate(names):
            mine = _whole_region(ins[w], name, chip, c)
            for t, (ox, oy) in enumerate(others):
                got = _whole_region(ins[w], name, 2 * ox + oy, c)
                k = 2 * (3 * w + t)
                cp = _remote(mine, got, sem_refs[k], sem_refs[k + 1], (ox, oy, c))
                cp.wait_send()
                cp.wait_recv()

    return pl.pallas_call(
        body, name="gather_" + tag + "_wait",
        out_shape=tuple(pltpu.HBM(a.shape, a.dtype) for a in wholes),
        in_specs=[IN_HBM] * nw + [SEM] * ns + [pl.BlockSpec(memory_space=pl.ANY)], out_specs=tuple([IN_HBM] * nw),
        input_output_aliases={i: i for i in range(nw)},
        compiler_params=pltpu.CompilerParams(has_side_effects=DATAFLOW),
    )(*wholes, *sems, after)


def _forward_halves(wholes, names, tag):
    nw = len(names)

    def body(*refs):
        outs = refs[nw:2 * nw]
        send_sems, recv_sems = refs[2 * nw:]
        _, _, c, _, sibling, others = _place()
        sends = []
        for w, name in enumerate(names):
            for t, (ox, oy) in enumerate(others):
                got = _whole_region(outs[w], name, 2 * ox + oy, c)
                cp = _remote(got, got, send_sems.at[w, t], recv_sems.at[w, t], sibling)
                cp.start()
                sends.append(cp)
        for w, name in enumerate(names):
            for t, (ox, oy) in enumerate(others):
                got = _whole_region(outs[w], name, 2 * ox + oy, 1 - c)
                _remote(got, got, send_sems.at[w, t], recv_sems.at[w, t], sibling).wait_recv()
        for cp in sends:
            cp.wait_send()

    return pl.pallas_call(
        body, name="gather_" + tag + "_forward",
        out_shape=[_sds(a.shape, a.dtype) for a in wholes],
        in_specs=[HBM] * nw, out_specs=[HBM] * nw,
        input_output_aliases={i: i for i in range(nw)},
        scratch_shapes=[pltpu.SemaphoreType.DMA((nw, 3)), pltpu.SemaphoreType.DMA((nw, 3))],
    )(*wholes)


def _forward_start(wholes, names, tag, after):
    nw = len(names)
    ns = 2 * 3 * nw

    def body(*refs):
        ins = refs[:nw]
        sems = refs[nw + 1:nw + 1 + ns]
        token = refs[2 * nw + ns + 1]
        _, _, c, _, sibling, others = _place()
        for w, name in enumerate(names):
            for t, (ox, oy) in enumerate(others):
                got = _whole_region(ins[w], name, 2 * ox + oy, c)
                k = 2 * (3 * w + t)
                _remote(got, got, sems[k], sems[k + 1], sibling).start()
        token[...] = jnp.zeros_like(token)

    res = pl.pallas_call(
        body, name="gather_" + tag + "_forward_start",
        out_shape=(*[pltpu.SemaphoreType.DMA(())] * ns, *[pltpu.HBM(a.shape, a.dtype) for a in wholes], _sds((8, LANES), F32)),
        in_specs=[IN_HBM] * nw + [pl.BlockSpec(memory_space=pl.ANY)],
        out_specs=(*[SEM] * ns, *[IN_HBM] * nw, pl.BlockSpec(memory_space=pltpu.VMEM)),
        input_output_aliases={i: ns + i for i in range(nw)},
        compiler_params=pltpu.CompilerParams(has_side_effects=DATAFLOW),
    )(*[_keep_in_hbm(a) for a in wholes], after)
    return list(res[:ns]), list(res[ns:ns + nw]), res[ns + nw]


def _forward_wait(sems, wholes, names, after, tag):
    nw = len(names)
    ns = len(sems)

    def body(*refs):
        ins = refs[:nw]
        sem_refs = refs[nw:nw + ns]
        _, _, c, _, sibling, others = _place()
        for w, name in enumerate(names):
            for t, (ox, oy) in enumerate(others):
                k = 2 * (3 * w + t)
                cp = _remote(_whole_region(ins[w], name, 2 * ox + oy, c), _whole_region(ins[w], name, 2 * ox + oy, 1 - c),
                             sem_refs[k], sem_refs[k + 1], sibling)
                cp.wait_send()
                cp.wait_recv()

    return pl.pallas_call(
        body, name="gather_" + tag + "_forward_wait",
        out_shape=tuple(pltpu.HBM(a.shape, a.dtype) for a in wholes),
        in_specs=[IN_HBM] * nw + [SEM] * ns + [pl.BlockSpec(memory_space=pl.ANY)], out_specs=tuple([IN_HBM] * nw),
        input_output_aliases={i: i for i in range(nw)},
        compiler_params=pltpu.CompilerParams(has_side_effects=DATAFLOW),
    )(*wholes, *sems, after)


def _compact_shape(name, dtype):
    kind, (r, c) = BIG[name]
    return _sds((r // 2, c), dtype)


def _swap_pairs(ins, outs, names, c):
    pairs = []
    for w, name in enumerate(names):
        kind, _ = BIG[name]
        half = _half_rows(name)
        if kind == "col":
            pairs.append((ins[w].at[pl.ds((1 - c) * half, half), :], outs[w]))
        else:
            pairs += [(ins[w].at[pl.ds(jj * 2 * half + (1 - c) * half, half), :], outs[w].at[pl.ds(jj * half, half), :])
                      for jj in range(N_CHIPS)]
    return pairs


def _n_swap_copies(names):
    return sum(1 if BIG[n][0] == "col" else N_CHIPS for n in names)


def _swap_start(grads, names, label):
    nw = len(names)
    ns = 2 * _n_swap_copies(names)

    def body(*refs):
        ins, lands = refs[:nw], refs[nw:2 * nw]
        sems = refs[2 * nw:2 * nw + ns]
        token = refs[4 * nw + ns]
        _, _, c, _, sibling, _ = _place()
        for k, (src, dst) in enumerate(_swap_pairs(ins, lands, names, c)):
            _remote(src, dst, sems[2 * k], sems[2 * k + 1], sibling).start()
        token[...] = jnp.zeros_like(token)

    lands = [_keep_in_hbm(lax.empty(_compact_shape(n, F32).shape, F32)) for n in names]
    res = pl.pallas_call(
        body, name=label,
        out_shape=(*[pltpu.SemaphoreType.DMA(())] * ns, *[pltpu.HBM(a.shape, a.dtype) for a in grads],
                   *[pltpu.HBM(a.shape, a.dtype) for a in lands], _sds((8, LANES), F32)),
        in_specs=[IN_HBM] * (2 * nw),
        out_specs=(*[SEM] * ns, *[IN_HBM] * (2 * nw), pl.BlockSpec(memory_space=pltpu.VMEM)),
        input_output_aliases={i: ns + i for i in range(2 * nw)},
        compiler_params=pltpu.CompilerParams(has_side_effects=DATAFLOW),
    )(*[_keep_in_hbm(a) for a in grads], *lands)
    return list(res[:ns]), list(res[ns:ns + nw]), list(res[ns + nw:ns + 2 * nw]), res[ns + 2 * nw]


def _swap_wait(sems, grads, lands, names, after, label):
    nw = len(names)
    ns = len(sems)

    def body(*refs):
        ins, land_refs = refs[:nw], refs[nw:2 * nw]
        sem_refs = refs[2 * nw:2 * nw + ns]
        _, _, c, _, sibling, _ = _place()
        for k, (src, dst) in enumerate(_swap_pairs(ins, land_refs, names, c)):
            cp = _remote(src, dst, sem_refs[2 * k], sem_refs[2 * k + 1], sibling)
            cp.wait_send()
            cp.wait_recv()

    res = pl.pallas_call(
        body, name=label,
        out_shape=tuple(pltpu.HBM(a.shape, a.dtype) for a in (*grads, *lands)),
        in_specs=[IN_HBM] * (2 * nw) + [SEM] * ns + [pl.BlockSpec(memory_space=pl.ANY)] * len(_several(after)),
        out_specs=tuple([IN_HBM] * (2 * nw)),
        input_output_aliases={i: i for i in range(2 * nw)},
        compiler_params=pltpu.CompilerParams(has_side_effects=DATAFLOW),
    )(*grads, *lands, *sems, *_several(after))
    return list(res[:nw]), list(res[nw:])


def _add_halves(name, grad, got, core):
    kind, (r, c) = BIG[name]
    half = _half_rows(name)
    if kind == "col":
        t = 128
        grid = (half // t,)
        g_spec = pl.BlockSpec((t, c), lambda i, cr: (cr[0] * (half // t) + i, 0))
        o_spec = pl.BlockSpec((t, c), lambda i, cr: (i, 0))
    else:
        t = half
        grid = (N_CHIPS,)
        g_spec = pl.BlockSpec((t, c), lambda i, cr: (2 * i + cr[0], 0))
        o_spec = pl.BlockSpec((t, c), lambda i, cr: (i, 0))

    def body(c_ref, g_ref, b_ref, o_ref):
        del c_ref
        o_ref[...] = (g_ref[...] + b_ref[...]).astype(o_ref.dtype)

    return pl.pallas_call(
        body, name="grad_add_" + name,
        grid_spec=pltpu.PrefetchScalarGridSpec(num_scalar_prefetch=1, grid=grid, in_specs=[g_spec, o_spec], out_specs=o_spec),
        out_shape=_compact_shape(name, BF16),
        compiler_params=pltpu.CompilerParams(dimension_semantics=("parallel",), vmem_limit_bytes=VMEM_LIMIT),
    )(core, grad, got)


def _piece(ref, name, chip):
    kind, _ = BIG[name]
    r, c = _shard_shape(name)
    if kind == "col":
        return ref.at[:, pl.ds(chip * c, c)]
    return ref.at[pl.ds(chip * (r // 2), r // 2), :]


def _landing_shape(name):
    r, c = _shard_shape(name)
    return (N_CHIPS - 1, r // 2, c)


def _exchange_start(parts, names, label):
    nw = len(names)
    ns = 2 * 3 * nw

    def body(*refs):
        ins, lands = refs[:nw], refs[nw:2 * nw]
        sems = refs[2 * nw:2 * nw + ns]
        token = refs[4 * nw + ns]
        _, _, c, _, _, others = _place()
        for w, name in enumerate(names):
            for t, (ox, oy) in enumerate(others):
                k = 2 * (3 * w + t)
                _remote(_piece(ins[w], name, 2 * ox + oy), lands[w].at[t], sems[k], sems[k + 1], (ox, oy, c)).start()
        token[...] = jnp.zeros_like(token)

    lands = [_keep_in_hbm(lax.empty(_landing_shape(n), BF16)) for n in names]
    res = pl.pallas_call(
        body, name=label,
        out_shape=(*[pltpu.SemaphoreType.DMA(())] * ns, *[pltpu.HBM(a.shape, a.dtype) for a in parts],
                   *[pltpu.HBM(a.shape, a.dtype) for a in lands], _sds((8, LANES), F32)),
        in_specs=[IN_HBM] * (2 * nw),
        out_specs=(*[SEM] * ns, *[IN_HBM] * (2 * nw), pl.BlockSpec(memory_space=pltpu.VMEM)),
        input_output_aliases={i: ns + i for i in range(2 * nw)},
        compiler_params=pltpu.CompilerParams(has_side_effects=DATAFLOW),
    )(*[_keep_in_hbm(a) for a in parts], *lands)
    return list(res[:ns]), list(res[ns:ns + nw]), list(res[ns + nw:ns + 2 * nw]), res[ns + 2 * nw]


def _exchange_wait(sems, parts, lands, names, after, label):
    nw = len(names)
    ns = len(sems)

    def body(*refs):
        ins, land_refs = refs[:nw], refs[nw:2 * nw]
        sem_refs = refs[2 * nw:2 * nw + ns]
        _, _, c, _, _, others = _place()
        for w, name in enumerate(names):
            for t, (ox, oy) in enumerate(others):
                k = 2 * (3 * w + t)
                cp = _remote(_piece(ins[w], name, 2 * ox + oy), land_refs[w].at[t], sem_refs[k], sem_refs[k + 1], (ox, oy, c))
                cp.wait_send()
                cp.wait_recv()

    res = pl.pallas_call(
        body, name=label,
        out_shape=tuple(pltpu.HBM(a.shape, a.dtype) for a in (*parts, *lands)),
        in_specs=[IN_HBM] * (2 * nw) + [SEM] * ns + [pl.BlockSpec(memory_space=pl.ANY)] * len(_several(after)),
        out_specs=tuple([IN_HBM] * (2 * nw)),
        input_output_aliases={i: i for i in range(2 * nw)},
        compiler_params=pltpu.CompilerParams(has_side_effects=DATAFLOW),
    )(*parts, *lands, *sems, *_several(after))
    return list(res[:nw]), list(res[nw:])


def _sum_chips(name, part, got, chip):
    kind, _ = BIG[name]
    _, r, c = got.shape
    t = _tile(r, (128, 352))
    if kind == "col":
        own = pl.BlockSpec((t, c), lambda i, ch: (i, ch[0]))
    else:
        own = pl.BlockSpec((t, c), lambda i, ch: (ch[0] * (r // t) + i, 0))

    def body(ch_ref, p_ref, g_ref, o_ref):
        del ch_ref
        acc = p_ref[...].astype(F32)
        for j in range(N_CHIPS - 1):
            acc = acc + g_ref[j].astype(F32)
        o_ref[...] = acc

    return _pallas(
        body, name="grad_sum_" + name, prefetch=1, grid=(r // t,),
        in_specs=[own, pl.BlockSpec((N_CHIPS - 1, t, c), lambda i, ch: (0, i, 0))],
        out_specs=pl.BlockSpec((t, c), lambda i, ch: (i, 0)),
        out_shape=_sds((r, c), F32), semantics=("parallel",),
    )(chip, part, got)


def _send_halves(sums, label, after):
    nw = len(sums)

    def body(*refs):
        ins, outs = refs[:nw], refs[nw + 1:2 * nw + 1]
        send_sems, recv_sems = refs[2 * nw + 1:]
        _, _, _, _, sibling, _ = _place()
        copies = [_remote(ins[w], outs[w], send_sems.at[w], recv_sems.at[w], sibling) for w in range(nw)]
        for cp in copies:
            cp.start()
        for cp in copies:
            cp.wait()

    return pl.pallas_call(
        body, name=label,
        out_shape=[_sds(a.shape, a.dtype) for a in sums],
        in_specs=[HBM] * (nw + 1), out_specs=[HBM] * nw,
        scratch_shapes=[pltpu.SemaphoreType.DMA((nw,)), pltpu.SemaphoreType.DMA((nw,))],
    )(*sums, after)


EARLY_GRADS = ("w_up", "w_down")
LAST_GRADS = ("w_in", "w_out")


def _reduce_finish(started, names, after, chip, tag):
    sems, parts, lands, _ = started
    parts, lands = _exchange_wait(sems, parts, lands, names, after, "grad_exchange_wait_" + tag)
    return [_sum_chips(n, parts[i], lands[i], chip) for i, n in enumerate(names)]


HI = lax.Precision.HIGHEST
MOD_COLS = 6 * D // N_CHIPS
COND_ROWS = 16


def _silu(v):
    return v * _sigmoid(v)


GATHER_ROWS = 8
FFW_COLS = 2 * DFF // N_CHIPS
CONV_COLS = DC // N_CHIPS
TAPS_PER_ROW = FFW_COLS // CONV_COLS
assert 4 + -(-CW // TAPS_PER_ROW) <= GATHER_ROWS


def _conv_tap_place(k):
    return 4 + k // TAPS_PER_ROW, (k % TAPS_PER_ROW) * CONV_COLS


def _taps_first(a):
    return jnp.transpose(a, (1, 0, 2))


def _pack_cond(c, ffn_w, conv_w):
    def body(c_ref, f_ref, w_ref, o_ref, o8_ref):
        o_ref[...] = jnp.zeros_like(o_ref)
        o_ref[0:1, 0:D] = c_ref[...]
        for k in range(3):
            o_ref[1 + k:2 + k, :] = f_ref[k]
        for k in range(CW):
            row, lane = _conv_tap_place(k)
            o_ref[row:row + 1, lane:lane + CONV_COLS] = w_ref[k]
        _eight_copies(o8_ref, o_ref[...])

    return _pallas(body, name="pack_cond", out_shape=[_sds((GATHER_ROWS, FFW_COLS), F32),
                                                      _sds((8 * GATHER_ROWS, FFW_COLS), F32)])(c, ffn_w, conv_w)


def _unpack_cond(got, c_ctx):
    def body(g_ref, c_ref, cond_ref, f_ref, w_ref):
        cond_ref[...] = jnp.zeros_like(cond_ref)
        for d in range(8):
            cond_ref[d:d + 1, :] = g_ref[d * GATHER_ROWS:d * GATHER_ROWS + 1, 0:D]
        cond_ref[8:9, :] = c_ref[...]
        for j in range(N_CHIPS):
            r0 = 2 * j * GATHER_ROWS
            f_ref[:, j * FFW_COLS:(j + 1) * FFW_COLS] = g_ref[r0 + 1:r0 + 4, :]
            for k in range(CW):
                row, lane = _conv_tap_place(k)
                w_ref[k:k + 1, j * CONV_COLS:(j + 1) * CONV_COLS] = g_ref[r0 + row:r0 + row + 1, lane:lane + CONV_COLS]

    return _pallas(body, name="unpack_cond",
                   out_shape=[_sds((COND_ROWS, D), F32), _sds((3, 2 * DFF), F32), _sds((CW, DC), F32)])(got, c_ctx)


def _chip_cols(rows, width):
    return pl.BlockSpec((rows, width), lambda i, ch: (0, ch[0]))


def _whole(shape):
    return pl.BlockSpec(shape, lambda i, ch: (0,) * len(shape))


def _mod_shard(cond, w_mod, b_mod, chip):
    def body(ch_ref, c_ref, w_ref, b_ref, o_ref, o8_ref):
        del ch_ref
        o_ref[...] = jnp.dot(_silu(c_ref[...]), w_ref[...], preferred_element_type=F32, precision=HI) + b_ref[...]
        _eight_copies(o8_ref, o_ref[...])

    return _pallas(body, name="mod_fwd", prefetch=1, grid=(1,),
                   in_specs=[_whole((COND_ROWS, D)), _whole((D, MOD_COLS)), _chip_cols(1, MOD_COLS)],
                   out_specs=[_whole((COND_ROWS, MOD_COLS)), _whole((8 * COND_ROWS, MOD_COLS))],
                   out_shape=[_sds((COND_ROWS, MOD_COLS), F32), _sds((8 * COND_ROWS, MOD_COLS), F32)])(
                       chip, cond, w_mod, b_mod)


def _unpack_mod(mods, dev):
    def body(dev_ref, m_ref, me_ref, c_ref):
        rowi = lax.broadcasted_iota(jnp.int32, (COND_ROWS, MOD_COLS), 0)
        core = dev_ref[0] % 2
        mine, ctx = [], []
        for j in range(N_CHIPS):
            blk = m_ref[pl.ds(pl.multiple_of((2 * j + core) * COND_ROWS, COND_ROWS), COND_ROWS), :]
            mine.append(jnp.sum(jnp.where(rowi == dev_ref[0], blk, 0.0), axis=0, keepdims=True))
            ctx.append(blk[8:9, :])
        mine = jnp.concatenate(mine, axis=1)
        ctx = jnp.concatenate(ctx, axis=1)
        for k in range(6):
            me_ref[k:k + 1, :] = mine[:, k * D:(k + 1) * D]
        for k in range(2):
            c_ref[k:k + 1, :] = ctx[:, k * D:(k + 1) * D]

    return _pallas(body, name="unpack_mod", prefetch=1, grid=(1,),
                   in_specs=[_whole(mods.shape)], out_specs=[_whole((6, D)), _whole((2, D))],
                   out_shape=[_sds((6, D), F32), _sds((2, D), F32)])(dev, mods)


MOD_TILE = 512


def _mod_weight_update(cond, dmod_all, w, m, v, chip):
    nt = MOD_COLS // MOD_TILE

    def body(ch_ref, c_ref, d_ref, w_ref, m_ref, v_ref, g_ref, dl_ref, nm_ref, nv_ref):
        del ch_ref
        g = lax.dot_general(_silu(c_ref[...]), d_ref[...], _TN, preferred_element_type=F32, precision=HI)
        g_ref[...] = g
        dl_ref[...], nm_ref[...], nv_ref[...] = _adam_math(w_ref[...], g, m_ref[...], v_ref[...])

    blk = pl.BlockSpec((D, MOD_TILE), lambda j, ch: (0, j))
    return _pallas(body, name="mod_weight_update", prefetch=1, grid=(nt,),
                   in_specs=[_whole((COND_ROWS, D)), pl.BlockSpec((COND_ROWS, MOD_TILE), lambda j, ch: (0, ch[0] * nt + j)),
                             blk, blk, blk],
                   out_specs=[blk] * 4, out_shape=[_sds((D, MOD_COLS), F32)] * 4,
                   semantics=("parallel",))(chip, cond, dmod_all, w, m, v)


def _cond_grad_partial(dmod_all, w_mod, chip):
    def body(ch_ref, d_ref, w_ref, o_ref, o8_ref):
        del ch_ref
        o_ref[...] = lax.dot_general(d_ref[...], w_ref[...], (((1,), (1,)), ((), ())), preferred_element_type=F32, precision=HI)
        _eight_copies(o8_ref, o_ref[...])

    return _pallas(body, name="cond_grad_partial", prefetch=1, grid=(1,),
                   in_specs=[pl.BlockSpec((8, MOD_COLS), lambda i, ch: (1, ch[0])), _whole((D, MOD_COLS))],
                   out_specs=[_whole((8, D)), _whole((64, D))],
                   out_shape=[_sds((8, D), F32), _sds((64, D), F32)])(chip, dmod_all, w_mod)


def _adam_math(w, g, m, v):
    nm = ADAM_B1 * m + (1.0 - ADAM_B1) * g
    nv = ADAM_B2 * v + (1.0 - ADAM_B2) * (g * g)
    c1 = 1.0 - ADAM_B1 ** ADAM_STEP
    c2 = 1.0 - ADAM_B2 ** ADAM_STEP
    return -ADAM_LR * ((nm / c1) / (jnp.sqrt(nv / c2) + ADAM_EPS) + ADAM_WD * w), nm, nv


def _cond_update(parts, c_ctx, m, v):
    def body(p_ref, c_ref, m_ref, v_ref, g_ref, d_ref, nm_ref, nv_ref):
        tot = p_ref[0:1, :]
        for j in range(1, N_CHIPS):
            tot = tot + p_ref[16 * j:16 * j + 1, :]
        cv = c_ref[...]
        sg = _sigmoid(cv)
        g = tot * (sg * (1.0 + cv * (1.0 - sg)))
        g_ref[...] = g
        d_ref[...], nm_ref[...], nv_ref[...] = _adam_math(cv, g, m_ref[...], v_ref[...])

    return _pallas(body, name="cond_update", out_shape=[_sds((1, D), F32)] * 4)(parts, c_ctx, m, v)


def _adamw_cols(w, g_all, m, v, chip, name):
    _, r, c = w.shape

    def body(ch_ref, w_ref, g_ref, m_ref, v_ref, go_ref, d_ref, nm_ref, nv_ref):
        del ch_ref
        for k in range(r):
            g = g_ref[k:k + 1, :]
            go_ref[k] = g
            d_ref[k], nm_ref[k], nv_ref[k] = _adam_math(w_ref[k], g, m_ref[k], v_ref[k])

    res = _pallas(body, name=name, prefetch=1, grid=(1,),
                  in_specs=[_whole((r, 1, c)), _chip_cols(r, c), _whole((r, 1, c)), _whole((r, 1, c))],
                  out_specs=[_whole((r, 1, c))] * 4, out_shape=[_sds((r, 1, c), F32)] * 4)(
                      chip, _taps_first(w), g_all, _taps_first(m), _taps_first(v))
    return tuple(jnp.transpose(a, (1, 0, 2)) for a in res)


def _adamw_halves(name, w, own, other, m, v, core, after):
    r, c = w.shape
    half = r // 2
    t = _tile(half, (128, 352))
    nh = half // t

    def pick(mine):
        def index(i, cr):
            first = cr[0] if mine else 1 - cr[0]
            return (jnp.clip(i - first * nh, 0, nh - 1), 0)
        return pl.BlockSpec((t, c), index)

    def body(c_ref, w_ref, own_ref, oth_ref, m_ref, v_ref, after_ref, g_ref, d_ref, nm_ref, nv_ref):
        del after_ref
        g = jnp.where(pl.program_id(0) // nh == c_ref[0], own_ref[...], oth_ref[...])
        g_ref[...] = g
        d_ref[...], nm_ref[...], nv_ref[...] = _adam_math(w_ref[...], g, m_ref[...], v_ref[...])

    blk = pl.BlockSpec((t, c), lambda i, cr: (i, 0))
    return _pallas(body, name="adamw_" + name, prefetch=1, grid=(2 * nh,),
                   in_specs=[blk, pick(True), pick(False), blk, blk, pl.BlockSpec(memory_space=pl.ANY)], out_specs=[blk] * 4,
                   out_shape=[_sds((r, c), F32)] * 4, semantics=("parallel",))(core, w, own, other, m, v, after)


WEIGHTS = ("c_ctx", "w_mod", "b_mod", "g_norm1", "w_in", "rpb", "conv_w", "conv_b", "ln_g", "ln_b", "w_out", "g_norm2",
           "w_up", "ffn_conv_w", "ffn_conv_b", "w_down", "g_final")
PACK = (("dmod", 6 * D), ("dmod_c", 2 * D), ("g_norm1", D), ("g_norm1_ctx", D), ("g_norm2", D), ("g_final", D),
        ("conv_b", DC), ("ln_g", DC), ("ln_b", DC), ("ffn_conv_b", 2 * DFF), ("ffn_conv_w", 3 * 2 * DFF),
        ("conv_w", CW * DC), ("rpb_rev", NH * 16 * LANES), ("loss", LANES))
PACK_OFF = {}
_o = 0
for _n, _w in PACK:
    PACK_OFF[_n] = (_o, _w)
    _o += _w
PACK_N = -(-_o // (8 * LANES)) * (8 * LANES)
VECTORS = {"b_mod": (6 * D, ("dmod", "dmod_c")), "g_norm1": (D, ("g_norm1", "g_norm1_ctx")), "conv_b": (DC, ("conv_b",)),
           "ln_g": (DC, ("ln_g",)), "ln_b": (DC, ("ln_b",)), "g_norm2": (D, ("g_norm2",)),
           "ffn_conv_b": (2 * DFF, ("ffn_conv_b",)), "g_final": (D, ("g_final",))}
RPB_COLS = 4 * NA_ROWS - 1


PACK_ROW = PACK_N // 8
assert PACK_ROW % LANES == 0 and all(w_ % LANES == 0 for _, w_ in PACK)


def _pack_pieces(off, n):
    pieces, s = [], 0
    while s < n:
        row, col = divmod(off + s, PACK_ROW)
        take = min(n - s, PACK_ROW - col)
        pieces.append((row, col, s, take))
        s += take
    return pieces


def _pack_small(parts, after):
    arrs, places = [], []
    for name, _ in PACK:
        off, width = PACK_OFF[name]
        group = parts[name]
        rows = group[0].shape[0]
        row_w = sum(a.shape[1] for a in group)
        assert rows * row_w == width, (name, rows, row_w, width)
        col = 0
        for a in group:
            arrs.append(a)
            places.append([off + k * row_w + col for k in range(rows)])
            col += a.shape[1]

    def body(*refs):
        o_ref, o8_ref = refs[-2], refs[-1]
        for row, col, _, take in _pack_pieces(_o, PACK_N - _o):
            o_ref[row:row + 1, col:col + take] = jnp.zeros((1, take), F32)
        for ref, offs in zip(refs, places):
            for k, off in enumerate(offs):
                for row, col, s, take in _pack_pieces(off, ref.shape[1]):
                    o_ref[row:row + 1, col:col + take] = ref[k:k + 1, s:s + take]
        _eight_copies(o8_ref, o_ref[...])

    vmem = pl.BlockSpec(memory_space=pltpu.VMEM)
    return _pallas(body, name="pack_small_grads", out_shape=[_sds((8, PACK_ROW), F32), _sds((64, PACK_ROW), F32)],
                   in_specs=[vmem] * len(arrs) + [pl.BlockSpec(memory_space=pl.ANY)] * len(_several(after)),
                   out_specs=[vmem, vmem])(*arrs, *_several(after))


def _small_update(packs, w, m, v):
    names = list(VECTORS)

    def body(*refs):
        it = iter(refs)
        p_ref = next(it)
        wmv = {n: (next(it), next(it), next(it)) for n in names}
        outs = {n: (next(it), next(it), next(it), next(it)) for n in names}
        dmod_ref, cw_ref, fw_ref, rpb_ref, loss_ref = next(it), next(it), next(it), next(it), next(it)

        def segment(d, name):
            pieces = [p_ref[8 * d + row:8 * d + row + 1, col:col + take] for row, col, _, take in _pack_pieces(*PACK_OFF[name])]
            return pieces[0] if len(pieces) == 1 else jnp.concatenate(pieces, axis=1)

        def total(name):
            acc = segment(0, name)
            for d in range(1, 8):
                acc = acc + segment(d, name)
            return acc

        for n in names:
            width, segs = VECTORS[n]
            g = total(segs[0])
            if len(segs) > 1:
                extra = total(segs[1])
                ew = extra.shape[1]
                g = g + extra if ew == width else jnp.concatenate([g[:, :ew] + extra, g[:, ew:]], axis=1)
            w_ref, m_ref, v_ref = wmv[n]
            g_ref, d_ref, nm_ref, nv_ref = outs[n]
            g_ref[...] = g
            d_ref[...], nm_ref[...], nv_ref[...] = _adam_math(w_ref[...], g, m_ref[...], v_ref[...])

        dmod_ref[...] = jnp.zeros_like(dmod_ref)
        for d in range(8):
            dmod_ref[d:d + 1, :] = segment(d, "dmod")
        dmod_ref[8:9, 0:2 * D] = total("dmod_c")
        for ref, name, rows in ((cw_ref, "conv_w", CW), (fw_ref, "ffn_conv_w", 3), (rpb_ref, "rpb_rev", NH * 16)):
            flat = total(name)
            n = ref.shape[1]
            for k in range(rows):
                ref[k:k + 1, :] = flat[:, k * n:(k + 1) * n]
        loss_ref[...] = total("loss")

    ins = [packs] + [a[n] for n in names for a in (w, m, v)]
    out_shape = [_sds((1, VECTORS[n][0]), F32) for n in names for _ in range(4)]
    out_shape += [_sds((COND_ROWS, 6 * D), F32), _sds((CW, DC), F32), _sds((3, 2 * DFF), F32), _sds((NH * 16, LANES), F32),
                  _sds((1, LANES), F32)]
    res = _pallas(body, name="small_update", out_shape=out_shape)(*ins)
    per = {n: tuple(res[4 * i:4 * i + 4]) for i, n in enumerate(names)}
    return (per, *res[4 * len(names):])


def _rpb_update(rev, w, m, v):
    nr = 2 * NA_ROWS - 1
    heads_inside = lambda a: jnp.transpose(a, (0, 2, 1, 3))

    def body(r_ref, w_ref, m_ref, v_ref, g_ref, d_ref, nm_ref, nv_ref):
        li = lax.broadcasted_iota(jnp.int32, (LANES, LANES), 0)
        co = lax.broadcasted_iota(jnp.int32, (LANES, LANES), 1)
        lane_of_co0 = GW - 1 + RPB_COLS // 2
        unflip = jnp.where((li == lane_of_co0 - co) & (co < RPB_COLS), 1.0, 0.0).astype(F32)
        assert NH & (NH - 1) == 0
        regroup = jnp.where((co == (li & (NH - 1)) * 16 + (li >> (NH.bit_length() - 1))) & (li < nr * NH),
                            1.0, 0.0).astype(F32)
        g_all = jnp.dot(jnp.dot(regroup, r_ref[...], preferred_element_type=F32, precision=HI), unflip,
                        preferred_element_type=F32, precision=HI)
        for ro in range(nr):
            g = g_all[ro * NH:(ro + 1) * NH, 0:RPB_COLS]
            g_ref[0, ro] = g
            d_ref[0, ro], nm_ref[0, ro], nv_ref[0, ro] = _adam_math(w_ref[0, ro], g, m_ref[0, ro], v_ref[0, ro])

    res = _pallas(body, name="rpb_update", out_shape=[_sds((1, nr, NH, RPB_COLS), F32)] * 4)(
        rev, heads_inside(w), heads_inside(m), heads_inside(v))
    return tuple(heads_inside(a) for a in res)


def kernel(x, c, ctx, c_ctx, w_mod, b_mod, g_norm1, w_in, rpb, conv_w, conv_b, ln_g, ln_b, w_out, g_norm2, w_up, ffn_conv_w, ffn_conv_b, w_down, g_final, loss_target, m_c_ctx, m_w_mod, m_b_mod, m_g_norm1, m_w_in, m_rpb, m_conv_w, m_conv_b, m_ln_g, m_ln_b, m_w_out, m_g_norm2, m_w_up, m_ffn_conv_w, m_ffn_conv_b, m_w_down, m_g_final, v_c_ctx, v_w_mod, v_b_mod, v_g_norm1, v_w_in, v_rpb, v_conv_w, v_conv_b, v_ln_g, v_ln_b, v_w_out, v_g_norm2, v_w_up, v_ffn_conv_w, v_ffn_conv_b, v_w_down, v_g_final):
    w = dict(c_ctx=c_ctx, w_mod=w_mod, b_mod=b_mod, g_norm1=g_norm1, w_in=w_in, rpb=rpb, conv_w=conv_w, conv_b=conv_b,
             ln_g=ln_g, ln_b=ln_b, w_out=w_out, g_norm2=g_norm2, w_up=w_up, ffn_conv_w=ffn_conv_w, ffn_conv_b=ffn_conv_b,
             w_down=w_down, g_final=g_final)
    mom = dict(c_ctx=m_c_ctx, w_mod=m_w_mod, b_mod=m_b_mod, g_norm1=m_g_norm1, w_in=m_w_in, rpb=m_rpb, conv_w=m_conv_w,
               conv_b=m_conv_b, ln_g=m_ln_g, ln_b=m_ln_b, w_out=m_w_out, g_norm2=m_g_norm2, w_up=m_w_up,
               ffn_conv_w=m_ffn_conv_w, ffn_conv_b=m_ffn_conv_b, w_down=m_w_down, g_final=m_g_final)
    var = dict(c_ctx=v_c_ctx, w_mod=v_w_mod, b_mod=v_b_mod, g_norm1=v_g_norm1, w_in=v_w_in, rpb=v_rpb, conv_w=v_conv_w,
               conv_b=v_conv_b, ln_g=v_ln_g, ln_b=v_ln_b, w_out=v_w_out, g_norm2=v_g_norm2, w_up=v_w_up,
               ffn_conv_w=v_ffn_conv_w, ffn_conv_b=v_ffn_conv_b, w_down=v_w_down, g_final=v_g_final)
    xi, yi, ci = lax.axis_index("x"), lax.axis_index("y"), lax.axis_index("c")
    dev = (4 * xi + 2 * yi + ci).astype(jnp.int32).reshape(1)
    chip = (2 * xi + yi).astype(jnp.int32).reshape(1)
    core = ci.astype(jnp.int32).reshape(1)
    c_ctx2 = c_ctx.reshape(1, D)
    g_final2 = g_final.reshape(1, D)
    mom["g_final"], var["g_final"] = m_g_final.reshape(1, D), v_g_final.reshape(1, D)

    sharing_cond = _share_start(_pack_cond(c, _taps_first(ffn_conv_w), _taps_first(conv_w)), "cond", after=[])
    shards = {n: _cast_into_whole(n, w[n][0], chip) for n in BIG_NAMES}
    cond, ffn_w_all, conv_w_all = _unpack_cond(_share_wait(sharing_cond, list(shards.values()), "cond"), c_ctx2)

    sharing_mod = _share_start(_mod_shard(cond, w_mod[0], b_mod, chip), "mod", after=[], flips=OTHER_CHIPS)

    sems_in, first, token_in = _gather_start([shards["w_in"]], ("w_in",), sharing_mod[2], "w_in")
    rpb_rev = jnp.pad(rpb[0][:, :, ::-1], ((0, 0), (0, 1), (48, LANES - 48 - RPB_COLS))).reshape(NH * 16, LANES)
    tab = _bias_table(rpb_rev, after=token_in)
    mod_me, mod_c = _unpack_mod(_share_wait(sharing_mod, [tab], "mod"), dev)
    late_started = []

    def w_in_all(after):
        arrived = _gather_wait(sems_in, first, ("w_in",), after, "w_in")
        late_started.append(_gather_start([shards[n] for n in LATE_NAMES], LATE_NAMES, arrived[0], "late"))
        return _forward_halves(list(arrived), ("w_in",), "w_in")[0]

    def late_weights(after):
        sems, late, _ = late_started.pop()
        arrived = list(_gather_wait(sems, late, LATE_NAMES, after, "late"))
        (w_out_all,) = _forward_halves(arrived[:1], LATE_NAMES[:1], "w_out")
        fsems, passing, _ = _forward_start(arrived[1:], LATE_NAMES[1:], "ffn", after=w_out_all)
        return w_out_all, lambda after2: _forward_wait(fsems, passing, LATE_NAMES[1:], after2, "ffn")

    vec = dict(g_norm1=g_norm1, g_norm2=g_norm2, g_final=g_final2, conv_w=conv_w_all, conv_b=conv_b, ln_g=ln_g, ln_b=ln_b,
               ffn_conv_w=ffn_w_all, ffn_conv_b=ffn_conv_b)
    started = []

    def begin_early(d_up, d_down):
        started.append(_swap_start([d_up, d_down], EARLY_GRADS, "grad_swap_start_early"))

    def carry_on_early(after):
        sems_, grads_, lands_, _ = started.pop()
        grads_, lands_ = _swap_wait(sems_, grads_, lands_, EARLY_GRADS, after, "grad_swap_wait_early")
        parts_ = [_add_halves(n, grads_[i], lands_[i], core) for i, n in enumerate(EARLY_GRADS)]
        started.append(_exchange_start(parts_, EARLY_GRADS, "grad_exchange_start_early"))
        return started[0][3][0:1, 0:1]

    loss_p, grad_x, d_in, d_out, d_up, d_down, small = _local_step(
        x[0], ctx[0], loss_target[0], mod_me, mod_c, vec, w_in_all, late_weights, tab, (begin_early, carry_on_early))

    out = {}
    sems_, grads_, lands_, _ = _swap_start([d_in, d_out], LAST_GRADS, "grad_swap_start_last")
    early_own = _reduce_finish(started[0], EARLY_GRADS, grad_x, chip, "early")
    parts = dict(dmod=small["dmod"], dmod_c=small["dmod_c"], g_norm1=[small["g_norm1"][0]], g_norm1_ctx=[small["g_norm1"][1]],
                 g_norm2=[small["g_norm2"]], g_final=[small["g_final"]], conv_b=[small["conv_b"]], ln_g=[small["ln_g"]],
                 ln_b=[small["ln_b"]], ffn_conv_b=small["ffn_conv_b"], ffn_conv_w=small["ffn_conv_w"],
                 conv_w=[small["conv_w"]], rpb_rev=[small["rpb_rev"]], loss=[loss_p])
    sharing = _share_start(_pack_small(parts, after=early_own), "small_grads", after=[])
    grads_, lands_ = _swap_wait(sems_, grads_, lands_, LAST_GRADS, sharing[2], "grad_swap_wait_last")
    parts_ = [_add_halves(n, grads_[i], lands_[i], core) for i, n in enumerate(LAST_GRADS)]
    last_started = _exchange_start(parts_, LAST_GRADS, "grad_exchange_start_last")
    early_other = _send_halves(early_own, "grad_send_early", after=last_started[3])
    for i, n in enumerate(EARLY_GRADS):
        out[n] = _adamw_halves(n, w[n][0], early_own[i], early_other[i], mom[n][0], var[n][0], core, early_other[i])

    packs = _share_wait(sharing, [out[n][1] for n in EARLY_GRADS], "small_grads")
    w2 = dict(w, g_final=g_final2)
    per, dmod_all, g_conv_w_all, g_ffn_w_all, g_rpb_rev, loss_row = _small_update(packs, w2, mom, var)
    out.update(per)
    out["w_mod"] = _mod_weight_update(cond, dmod_all, w_mod[0], m_w_mod[0], v_w_mod[0], chip)

    sharing_c = _share_start(_cond_grad_partial(dmod_all, w_mod[0], chip), "cond_grad", after=out["w_mod"][1])
    last_own = _reduce_finish(last_started, LAST_GRADS, sharing_c[2], chip, "last")
    last_other = _send_halves(last_own, "grad_send_last", after=last_own[0])
    for i, n in enumerate(LAST_GRADS):
        out[n] = _adamw_halves(n, w[n][0], last_own[i], last_other[i], mom[n][0], var[n][0], core, last_other[i])
    out["c_ctx"] = _cond_update(_share_wait(sharing_c, [out[n][1] for n in LAST_GRADS], "cond_grad"),
                                c_ctx2, m_c_ctx.reshape(1, D), v_c_ctx.reshape(1, D))
    out["conv_w"] = _adamw_cols(conv_w, g_conv_w_all, m_conv_w, v_conv_w, chip, "adamw_conv_w")
    out["ffn_conv_w"] = _adamw_cols(ffn_conv_w, g_ffn_w_all, m_ffn_conv_w, v_ffn_conv_w, chip, "adamw_ffn_conv_w")
    out["rpb"] = _rpb_update(g_rpb_rev, rpb, m_rpb, v_rpb)

    res = [[out[n][k].reshape(w[n].shape) for n in WEIGHTS] for k in range(4)]
    return (loss_row[0, 0], grad_x[None], *res[0], *res[1], *res[2], *res[3])
```

```python
import jax
import jax.numpy as jnp
from jax import lax
from jax.experimental import pallas as pl
from jax.experimental.pallas import tpu as pltpu

F32 = jnp.float32
BF16 = jnp.bfloat16
MXU_DTYPE = jnp.bfloat16

D = 1024
CTX = 256
GW = 64
DA = 512
NH = 8
HD = 64
DC = 512
CW = 31
DFF = 2816
NIN = 3 * DA + 2 * DC
EPS = 1e-6
SCALE = HD ** -0.5
NEG = -1e30
NA_ROWS = 8
PAIR_ROWS = NA_ROWS + 1
TAB_BLOCKS = 17
LANES = 128
VMEM_LIMIT = 56 * 1024 * 1024

ADAM_LR = 0.001
ADAM_B1 = 0.9
ADAM_B2 = 0.999
ADAM_EPS = 1e-08
ADAM_WD = 0.01
ADAM_STEP = 10

MESH = pl.DeviceIdType.MESH


def _pallas(body, *, name, semantics=None, vmem=VMEM_LIMIT, prefetch=0, **kw):
    params = dict(vmem_limit_bytes=vmem)
    if semantics is not None:
        params["dimension_semantics"] = semantics
    if prefetch:
        kw["grid_spec"] = pltpu.PrefetchScalarGridSpec(
            num_scalar_prefetch=prefetch, grid=kw.pop("grid"), in_specs=kw.pop("in_specs"), out_specs=kw.pop("out_specs"),
            scratch_shapes=kw.pop("scratch_shapes", ()))
    return pl.pallas_call(body, name=name, compiler_params=pltpu.CompilerParams(**params), **kw)


def _sds(shape, dtype):
    return jax.ShapeDtypeStruct(shape, dtype)


def _vec_spec(n):
    return pl.BlockSpec((1, n), lambda *_: (0, 0))


def _colsum8(x):
    t, n = x.shape
    return jnp.sum(x.reshape(t // 8, 8, n), axis=0)


def _sigmoid(x):
    return 0.5 * jnp.tanh(0.5 * x) + 0.5


def _mm(a, b, *, mode, m, n, k, tm, tn, tk, out_dtype, name, a_off=(0, 0), b_off=(0, 0),
        out_total=None, o_off=(0, 0), into=None):
    a_list = list(a) if isinstance(a, (list, tuple)) else [a]
    b_list = list(b) if isinstance(b, (list, tuple)) else [b]
    assert m % tm == 0 and n % tn == 0 and k % tk == 0, (name, m, n, k, tm, tn, tk)
    gi, gj, nk = m // tm, n // tn, k // tk
    dims = {"nn": (((1,), (0,)), ((), ())), "nt": (((1,), (1,)), ((), ())), "tn": (((0,), (0,)), ((), ()))}[mode]

    if len(a_list) > 1:
        assert mode != "tn" and nk == 1 and sum(x.shape[1] for x in a_list) == k
        a_specs = [pl.BlockSpec((tm, x.shape[1]), lambda i, j, kk: (i, 0)) for x in a_list]
    elif mode == "tn":
        a_specs = [pl.BlockSpec((tk, tm), lambda i, j, kk: (kk + a_off[0], i + a_off[1]))]
    else:
        a_specs = [pl.BlockSpec((tm, tk), lambda i, j, kk: (i + a_off[0], kk + a_off[1]))]
    if len(b_list) > 1:
        assert mode == "tn" and gj == 1 and sum(x.shape[1] for x in b_list) == n
        b_specs = [pl.BlockSpec((tk, x.shape[1]), lambda i, j, kk: (kk, 0)) for x in b_list]
    elif mode == "nt":
        b_specs = [pl.BlockSpec((tn, tk), lambda i, j, kk: (j + b_off[0], kk + b_off[1]))]
    else:
        b_specs = [pl.BlockSpec((tk, tn), lambda i, j, kk: (kk + b_off[0], j + b_off[1]))]

    na, nb = len(a_list), len(b_list)
    in_place = nk > 1 and out_dtype == F32
    n_in = na + nb + (into is not None)

    def body(*refs):
        a_refs, b_refs, o_ref = refs[:na], refs[na:na + nb], refs[n_in]
        acc = o_ref if in_place else (refs[n_in + 1] if nk > 1 else None)
        kk = pl.program_id(2)

        def whole(piece_refs):
            vals = [r[...].astype(MXU_DTYPE) for r in piece_refs]
            return vals[0] if len(vals) == 1 else jnp.concatenate(vals, axis=1)

        p = lax.dot_general(whole(a_refs), whole(b_refs), dims, preferred_element_type=F32)
        if nk == 1:
            o_ref[...] = p.astype(out_dtype)
            return

        @pl.when(kk == 0)
        def _():
            acc[...] = p

        @pl.when(kk > 0)
        def _():
            acc[...] += p

        if not in_place:
            @pl.when(kk == nk - 1)
            def _():
                o_ref[...] = acc[...].astype(out_dtype)

    ins = [*a_list, *b_list]
    in_specs = a_specs + b_specs
    extra = {}
    if into is not None:
        extra["input_output_aliases"] = {len(ins): 0}
        ins.append(into)
        in_specs.append(pl.BlockSpec(memory_space=pl.ANY))
    return _pallas(
        body, name=name, grid=(gi, gj, nk), in_specs=in_specs,
        out_specs=pl.BlockSpec((tm, tn), lambda i, j, kk: (i + o_off[0], j + o_off[1])),
        out_shape=_sds(out_total or (m, n), out_dtype),
        scratch_shapes=[pltpu.VMEM((tm, tn), F32)] if nk > 1 and not in_place else [],
        semantics=("parallel", "parallel", "arbitrary"), **extra,
    )(*ins)


ROW_TILE = 256


def _row_tile(s, most=2):
    for k in (4, 2):
        if k <= most and s % (k * ROW_TILE) == 0:
            return k * ROW_TILE
    return ROW_TILE


def _row_in(t, index_map=lambda i: (i, 0)):
    return pl.BlockSpec((t, D), index_map)


def _rmsmod_fwd(x, ctx, g, sc, sh, csc, csh):
    s = x.shape[0]
    t = ROW_TILE
    nt = s // t
    nb = 3
    assert ctx.shape[0] == t and nt >= nb

    def body(x_hbm, c_ref, g_ref, sc_ref, sh_ref, csc_ref, csh_ref, o_hbm, ibuf, obuf, isem, osem):
        def read(i):
            return pltpu.make_async_copy(x_hbm.at[pl.ds(i * t, t), :], ibuf.at[i % nb], isem.at[i % nb])

        def write(i):
            return pltpu.make_async_copy(obuf.at[i % 2], o_hbm.at[pl.ds(i * t, t), :], osem.at[i % 2])

        def normed(xv, scv, shv):
            r = lax.rsqrt(jnp.mean(xv * xv, axis=-1, keepdims=True) + EPS)
            return ((xv * r * g_ref[...]) * (1.0 + scv) + shv).astype(obuf.dtype)

        for i in range(nb):
            read(i).start()

        def step(i, carry):
            read(i).wait()
            y = normed(ibuf[i % nb], sc_ref[...], sh_ref[...])

            @pl.when(i >= 2)
            def _():
                write(i - 2).wait()

            obuf[i % 2] = y
            write(i).start()

            @pl.when(i + nb < nt)
            def _():
                read(i + nb).start()

            return carry

        lax.fori_loop(0, nt, step, 0)
        y = normed(c_ref[...], csc_ref[...], csh_ref[...])
        write(nt - 2).wait()
        obuf[nt % 2] = y
        write(nt).start()
        write(nt - 1).wait()
        write(nt).wait()

    vmem = pl.BlockSpec(memory_space=pltpu.VMEM)
    hbm = pl.BlockSpec(memory_space=pl.ANY)
    return _pallas(
        body, name="rmsmod1_fwd", in_specs=[hbm] + [vmem] * 6, out_specs=hbm,
        out_shape=_sds((s + CTX, D), MXU_DTYPE),
        scratch_shapes=[pltpu.VMEM((nb, t, D), F32), pltpu.VMEM((2, t, D), MXU_DTYPE),
                        pltpu.SemaphoreType.DMA((nb,)), pltpu.SemaphoreType.DMA((2,))],
    )(x, ctx, g, sc, sh, csc, csh)


def _resid_rmsmod_fwd(x, y, gt, g, sc, sh):
    s = x.shape[0]

    def body(x_ref, y_ref, gt_ref, g_ref, sc_ref, sh_ref, x1_ref, h_ref):
        x1 = x_ref[...] + gt_ref[...] * y_ref[...]
        x1_ref[...] = x1
        r = lax.rsqrt(jnp.mean(x1 * x1, axis=-1, keepdims=True) + EPS)
        h_ref[...] = ((x1 * r * g_ref[...]) * (1.0 + sc_ref[...]) + sh_ref[...]).astype(h_ref.dtype)

    t = _row_tile(s)
    row = pl.BlockSpec((t, D), lambda i: (i, 0))
    return _pallas(
        body, name="resid_rmsmod2_fwd", grid=(s // t,),
        in_specs=[_row_in(t), _row_in(t)] + [_vec_spec(D)] * 4,
        out_specs=[row, row],
        out_shape=[_sds((s, D), F32), _sds((s, D), MXU_DTYPE)],
        semantics=("parallel",),
    )(x, y, gt, g, sc, sh)


def _final_fwd_bwd(x1, z, gt2, gf, tgt):
    s = x1.shape[0]
    tile = _row_tile(s)
    nt = s // tile

    def body(x1_ref, z_ref, gt_ref, gf_ref, t_ref, dx2_ref, dz_ref, loss_ref, dgt_ref, dgf_ref, a_loss, a_gt, a_gf):
        i = pl.program_id(0)

        @pl.when(i == 0)
        def _():
            a_loss[...] = jnp.zeros_like(a_loss)
            a_gt[...] = jnp.zeros_like(a_gt)
            a_gf[...] = jnp.zeros_like(a_gf)

        zv = z_ref[...]
        gt = gt_ref[...]
        gf_ = gf_ref[...]
        x2 = x1_ref[...] + gt * zv
        r = lax.rsqrt(jnp.mean(x2 * x2, axis=-1, keepdims=True) + EPS)
        xn = x2 * r
        e = xn * gf_ - t_ref[...]
        a_loss[...] += _colsum8(e * e)
        dyo = e * (1.0 / D)
        a_gf[...] += _colsum8(dyo * xn)
        gdy = gf_ * dyo
        dx2 = r * gdy - xn * (r * r) * jnp.mean(x2 * gdy, axis=-1, keepdims=True)
        dx2_ref[...] = dx2
        dz_ref[...] = (gt * dx2).astype(dz_ref.dtype)
        a_gt[...] += _colsum8(dx2 * zv)

        @pl.when(i == nt - 1)
        def _():
            tot = jnp.sum(jnp.sum(a_loss[...], axis=0, keepdims=True), axis=1, keepdims=True) * (0.5 / D)
            loss_ref[...] = jnp.broadcast_to(tot, loss_ref.shape)
            dgt_ref[...] = jnp.sum(a_gt[...], axis=0, keepdims=True)
            dgf_ref[...] = jnp.sum(a_gf[...], axis=0, keepdims=True)

    row = pl.BlockSpec((tile, D), lambda i: (i, 0))
    return _pallas(
        body, name="final_norm_loss", grid=(nt,),
        in_specs=[_row_in(tile), _row_in(tile), _vec_spec(D), _vec_spec(D), _row_in(tile)],
        out_specs=[row, row, _vec_spec(LANES), _vec_spec(D), _vec_spec(D)],
        out_shape=[_sds((s, D), F32), _sds((s, D), MXU_DTYPE), _sds((1, LANES), F32), _sds((1, D), F32), _sds((1, D), F32)],
        scratch_shapes=[pltpu.VMEM((8, D), F32)] * 3,
        semantics=("arbitrary",),
    )(x1, z, gt2, gf, tgt)


def _rmsmod_bwd(xin, dh, g, sc, *, name, dh_row0=0, add=None, resid=None):
    s = xin.shape[0]
    tile = _row_tile(s)
    nt = s // tile
    want_dx = add is not None
    assert resid is None or want_dx

    def body(*refs):
        it = iter(refs)
        x_ref, dh_ref, g_ref, sc_ref = next(it), next(it), next(it), next(it)
        add_ref = next(it) if want_dx else None
        gt_ref, y_ref = (next(it), next(it)) if resid is not None else (None, None)
        dsh_ref, dsc_ref, dg_ref = next(it), next(it), next(it)
        dx_ref = next(it) if want_dx else None
        dy_ref, dgt_ref = (next(it), next(it)) if resid is not None else (None, None)
        a_sh, a_sc, a_g = next(it), next(it), next(it)
        a_gt = next(it) if resid is not None else None
        i = pl.program_id(0)

        @pl.when(i == 0)
        def _():
            a_sh[...] = jnp.zeros_like(a_sh)
            a_sc[...] = jnp.zeros_like(a_sc)
            a_g[...] = jnp.zeros_like(a_g)
            if a_gt is not None:
                a_gt[...] = jnp.zeros_like(a_gt)

        xv = x_ref[...]
        dhv = dh_ref[...]
        gv = g_ref[...]
        r = lax.rsqrt(jnp.mean(xv * xv, axis=-1, keepdims=True) + EPS)
        xn = xv * r
        a_sh[...] += _colsum8(dhv)
        a_sc[...] += _colsum8(dhv * (xn * gv))
        dn = dhv * (1.0 + sc_ref[...])
        a_g[...] += _colsum8(dn * xn)
        if want_dx:
            gdn = gv * dn
            dx = add_ref[...] + r * gdn - xn * (r * r) * jnp.mean(xv * gdn, axis=-1, keepdims=True)
            dx_ref[...] = dx
            if resid is not None:
                dy_ref[...] = (gt_ref[...] * dx).astype(dy_ref.dtype)
                a_gt[...] += _colsum8(dx * y_ref[...])

        @pl.when(i == nt - 1)
        def _():
            dsh_ref[...] = jnp.sum(a_sh[...], axis=0, keepdims=True)
            dsc_ref[...] = jnp.sum(a_sc[...], axis=0, keepdims=True)
            dg_ref[...] = jnp.sum(a_g[...], axis=0, keepdims=True)
            if a_gt is not None:
                dgt_ref[...] = jnp.sum(a_gt[...], axis=0, keepdims=True)

    row = pl.BlockSpec((tile, D), lambda i: (i, 0))
    ins = [xin, dh, g, sc]
    in_specs = [_row_in(tile), _row_in(tile, lambda i: (i + dh_row0 // tile, 0)), _vec_spec(D), _vec_spec(D)]
    out_specs = [_vec_spec(D)] * 3
    out_shape = [_sds((1, D), F32)] * 3
    scratch = [pltpu.VMEM((8, D), F32)] * 3
    if want_dx:
        ins.append(add)
        in_specs.append(_row_in(tile))
        out_specs.append(row)
        out_shape.append(_sds((s, D), F32))
    if resid is not None:
        ins += [resid[0], resid[1]]
        in_specs += [_vec_spec(D), _row_in(tile)]
        out_specs += [row, _vec_spec(D)]
        out_shape += [_sds((s, D), MXU_DTYPE), _sds((1, D), F32)]
        scratch.append(pltpu.VMEM((8, D), F32))
    return _pallas(body, name=name, grid=(nt,), in_specs=in_specs, out_specs=out_specs, out_shape=out_shape,
                   scratch_shapes=scratch, semantics=("arbitrary",))(*ins)


FF_TILE = 128
FF_CHUNK = 128
HALO = 8


def _shift3(pad_ref, r0, ch):
    return tuple(pad_ref[pl.ds(r0 + HALO + d, ch), :] for d in (-1, 0, 1))


def _fill_padded(pad_ref, src_ref, s, ch, halo):
    zeros = jnp.zeros((halo, pad_ref.shape[1]), F32)
    pad_ref[0:halo, :] = zeros
    pad_ref[s + halo:s + 2 * halo, :] = zeros

    def cp(c, carry):
        r0 = pl.multiple_of(c * ch, ch)
        pad_ref[pl.ds(r0 + halo, ch), :] = src_ref[pl.ds(r0, ch), :].astype(F32)
        return carry

    lax.fori_loop(0, s // ch, cp, 0)


def _ffn_act_fwd(u, w, b):
    s = u.shape[0]
    tile, ch = FF_TILE, FF_CHUNK
    nj = DFF // tile

    def body(ug_ref, uv_ref, wg_ref, wv_ref, bg_ref, bv_ref, f_ref, gpad, vpad):
        _fill_padded(gpad, ug_ref, s, ch, HALO)
        _fill_padded(vpad, uv_ref, s, ch, HALO)

        def conv(pad, w_ref, b_ref, r0):
            prev, cur, nxt = _shift3(pad, r0, ch)
            return w_ref[0:1, :] * prev + w_ref[1:2, :] * cur + w_ref[2:3, :] * nxt + b_ref[...]

        def step(c, carry):
            r0 = pl.multiple_of(c * ch, ch)
            gc = conv(gpad, wg_ref, bg_ref, r0)
            vc = conv(vpad, wv_ref, bv_ref, r0)
            f_ref[pl.ds(r0, ch), :] = (gc * _sigmoid(gc) * vc).astype(f_ref.dtype)
            return carry

        lax.fori_loop(0, s // ch, step, 0)

    col = lambda off: pl.BlockSpec((s, tile), lambda j: (0, j + off))
    wsp = lambda off: pl.BlockSpec((3, tile), lambda j: (0, j + off))
    bsp = lambda off: pl.BlockSpec((1, tile), lambda j: (0, j + off))
    return _pallas(
        body, name="ffn_act_fwd", grid=(nj,),
        in_specs=[col(0), col(nj), wsp(0), wsp(nj), bsp(0), bsp(nj)],
        out_specs=col(0), out_shape=_sds((s, DFF), MXU_DTYPE),
        scratch_shapes=[pltpu.VMEM((s + 2 * HALO, tile), F32)] * 2,
        semantics=("parallel",),
    )(u, u, w, w, b, b)


def _ffn_act_bwd(u, df, w, b):
    s = u.shape[0]
    nj = DFF // FF_TILE
    ch = FF_CHUNK

    def body(ug_ref, uv_ref, df_ref, wg_ref, wv_ref, bg_ref, bv_ref,
             dug_ref, duv_ref, dwg_ref, dwv_ref, dbg_ref, dbv_ref, gpad, vpad, dgpad, dvpad, acc):
        _fill_padded(gpad, ug_ref, s, ch, HALO)
        _fill_padded(vpad, uv_ref, s, ch, HALO)
        zeros = jnp.zeros((HALO, FF_TILE), F32)
        for p in (dgpad, dvpad):
            p[0:HALO, :] = zeros
            p[s + HALO:s + 2 * HALO, :] = zeros
        acc[...] = jnp.zeros_like(acc)

        def step(c, carry):
            r0 = pl.multiple_of(c * ch, ch)
            gs = _shift3(gpad, r0, ch)
            vs = _shift3(vpad, r0, ch)
            gc = wg_ref[0:1, :] * gs[0] + wg_ref[1:2, :] * gs[1] + wg_ref[2:3, :] * gs[2] + bg_ref[...]
            vc = wv_ref[0:1, :] * vs[0] + wv_ref[1:2, :] * vs[1] + wv_ref[2:3, :] * vs[2] + bv_ref[...]
            sg = _sigmoid(gc)
            dfv = df_ref[pl.ds(r0, ch), :].astype(F32)
            dgc = dfv * vc * (sg * (1.0 + gc * (1.0 - sg)))
            dvc = dfv * (gc * sg)
            dgpad[pl.ds(r0 + HALO, ch), :] = dgc
            dvpad[pl.ds(r0 + HALO, ch), :] = dvc
            for t in range(3):
                acc[8 * t:8 * t + 8, :] += _colsum8(dgc * gs[t])
                acc[24 + 8 * t:32 + 8 * t, :] += _colsum8(dvc * vs[t])
            acc[48:56, :] += _colsum8(dgc)
            acc[56:64, :] += _colsum8(dvc)
            return carry

        lax.fori_loop(0, s // ch, step, 0)

        def step2(c, carry):
            r0 = pl.multiple_of(c * ch, ch)
            for pad, w_ref, o_ref in ((dgpad, wg_ref, dug_ref), (dvpad, wv_ref, duv_ref)):
                prev, cur, nxt = _shift3(pad, r0, ch)
                o_ref[pl.ds(r0, ch), :] = (w_ref[0:1, :] * nxt + w_ref[1:2, :] * cur + w_ref[2:3, :] * prev).astype(o_ref.dtype)
            return carry

        lax.fori_loop(0, s // ch, step2, 0)
        for t in range(3):
            dwg_ref[t:t + 1, :] = jnp.sum(acc[8 * t:8 * t + 8, :], axis=0, keepdims=True)
            dwv_ref[t:t + 1, :] = jnp.sum(acc[24 + 8 * t:32 + 8 * t, :], axis=0, keepdims=True)
        dbg_ref[...] = jnp.sum(acc[48:56, :], axis=0, keepdims=True)
        dbv_ref[...] = jnp.sum(acc[56:64, :], axis=0, keepdims=True)

    col = lambda off: pl.BlockSpec((s, FF_TILE), lambda j: (0, j + off))
    wsp = lambda off: pl.BlockSpec((3, FF_TILE), lambda j: (0, j + off))
    bsp = lambda off: pl.BlockSpec((1, FF_TILE), lambda j: (0, j + off))
    return _pallas(
        body, name="ffn_act_bwd", grid=(nj,),
        in_specs=[col(0), col(nj), col(0), wsp(0), wsp(nj), bsp(0), bsp(nj)],
        out_specs=[col(0), col(0), wsp(0), wsp(0), bsp(0), bsp(0)],
        out_shape=[_sds((s, DFF), MXU_DTYPE)] * 2 + [_sds((3, DFF), F32)] * 2 + [_sds((1, DFF), F32)] * 2,
        scratch_shapes=[pltpu.VMEM((s + 2 * HALO, FF_TILE), F32)] * 4 + [pltpu.VMEM((64, FF_TILE), F32)],
        semantics=("parallel",),
    )(u, u, df, w, w, b, b)


CONV_CHUNK = 64
CONV_HALO = 16


def _tap(pad_ref, r0, k):
    return pad_ref[pl.ds(r0 + CONV_HALO - CW // 2 + k, CONV_CHUNK), :]


def _glu_into(pad_ref, a_ref, g_ref, s):
    zeros = jnp.zeros((CONV_HALO, LANES), F32)
    pad_ref[0:CONV_HALO, :] = zeros
    pad_ref[s + CONV_HALO:s + 2 * CONV_HALO, :] = zeros

    def cp(c, carry):
        r0 = pl.multiple_of(c * ROW_TILE, ROW_TILE)
        pad_ref[pl.ds(r0 + CONV_HALO, ROW_TILE), :] = a_ref[pl.ds(r0, ROW_TILE), :] * _sigmoid(g_ref[pl.ds(r0, ROW_TILE), :])
        return carry

    lax.fori_loop(0, s // ROW_TILE, cp, 0)


def _conf_conv_fwd(ag, conv_w, conv_b):
    s = ag.shape[0]
    nc = DC // LANES

    def body(a_ref, g_ref, w_ref, b_ref, o_ref, upad):
        _glu_into(upad, a_ref, g_ref, s)

        def step(c, carry):
            r0 = pl.multiple_of(c * CONV_CHUNK, CONV_CHUNK)
            acc = jnp.broadcast_to(b_ref[...], (CONV_CHUNK, LANES))
            for k in range(CW):
                acc = acc + w_ref[k:k + 1, :] * _tap(upad, r0, k)
            o_ref[pl.ds(r0, CONV_CHUNK), :] = acc
            return carry

        lax.fori_loop(0, s // CONV_CHUNK, step, 0)

    col = lambda off: pl.BlockSpec((s, LANES), lambda c: (0, c + off))
    return _pallas(
        body, name="conf_conv_fwd", grid=(nc,),
        in_specs=[col(0), col(nc), pl.BlockSpec((CW, LANES), lambda c: (0, c)), pl.BlockSpec((1, LANES), lambda c: (0, c))],
        out_specs=col(0), out_shape=_sds((s, DC), F32),
        scratch_shapes=[pltpu.VMEM((s + 2 * CONV_HALO, LANES), F32)],
        semantics=("parallel",),
    )(ag, ag, conv_w, conv_b)


def _ln_stats(x):
    mu = jnp.mean(x, axis=-1, keepdims=True)
    xc = x - mu
    var = jnp.mean(xc * xc, axis=-1, keepdims=True)
    rstd = lax.rsqrt(var + EPS)
    return xc * rstd, rstd


def _conf_ln_fwd(u1, ln_g, ln_b, ycat):
    s = u1.shape[0]

    def body(u_ref, g_ref, b_ref, ycat_ref, o_ref):
        del ycat_ref
        xhat, _ = _ln_stats(u_ref[...])
        y = xhat * g_ref[...] + b_ref[...]
        o_ref[...] = (y * _sigmoid(y)).astype(o_ref.dtype)

    t = _row_tile(s, 4)
    return _pallas(
        body, name="conf_ln_fwd", grid=(s // t,),
        in_specs=[pl.BlockSpec((t, DC), lambda i: (i, 0)), _vec_spec(DC), _vec_spec(DC),
                  pl.BlockSpec(memory_space=pl.ANY)],
        out_specs=pl.BlockSpec((t, DC), lambda i: (i, 1)),
        out_shape=_sds(ycat.shape, ycat.dtype),
        input_output_aliases={3: 0},
        semantics=("parallel",),
    )(u1, ln_g, ln_b, ycat)


def _conf_ln_bwd(dycat, u1, ln_g, ln_b):
    s = u1.shape[0]
    t = _row_tile(s, 4)
    nt = s // t

    def body(dy_ref, u_ref, g_ref, b_ref, du_ref, dg_ref, db_ref, a_g, a_b):
        i = pl.program_id(0)

        @pl.when(i == 0)
        def _():
            a_g[...] = jnp.zeros_like(a_g)
            a_b[...] = jnp.zeros_like(a_b)

        xhat, rstd = _ln_stats(u_ref[...])
        gv = g_ref[...]
        y = xhat * gv + b_ref[...]
        sg = _sigmoid(y)
        dyl = dy_ref[...] * (sg * (1.0 + y * (1.0 - sg)))
        a_g[...] += _colsum8(dyl * xhat)
        a_b[...] += _colsum8(dyl)
        dxh = dyl * gv
        du_ref[...] = rstd * (dxh - jnp.mean(dxh, axis=-1, keepdims=True)
                              - xhat * jnp.mean(dxh * xhat, axis=-1, keepdims=True))

        @pl.when(i == nt - 1)
        def _():
            dg_ref[...] = jnp.sum(a_g[...], axis=0, keepdims=True)
            db_ref[...] = jnp.sum(a_b[...], axis=0, keepdims=True)

    return _pallas(
        body, name="conf_ln_bwd", grid=(nt,),
        in_specs=[pl.BlockSpec((t, DC), lambda i: (i, 1)), pl.BlockSpec((t, DC), lambda i: (i, 0)),
                  _vec_spec(DC), _vec_spec(DC)],
        out_specs=[pl.BlockSpec((t, DC), lambda i: (i, 0)), _vec_spec(DC), _vec_spec(DC)],
        out_shape=[_sds((s, DC), F32), _sds((1, DC), F32), _sds((1, DC), F32)],
        scratch_shapes=[pltpu.VMEM((8, DC), F32)] * 2,
        semantics=("arbitrary",),
    )(dycat, u1, ln_g, ln_b)


def _conf_conv_bwd(ag, du1, conv_w, rows_out):
    s = ag.shape[0]
    nc = DC // LANES

    def body(a_ref, g_ref, d_ref, w_ref, da_ref, dg_ref, dw_ref, db_ref, upad, dpad, acc):
        _glu_into(upad, a_ref, g_ref, s)
        _fill_padded(dpad, d_ref, s, ROW_TILE, CONV_HALO)
        acc[...] = jnp.zeros_like(acc)

        def step(c, carry):
            r0 = pl.multiple_of(c * CONV_CHUNK, CONV_CHUNK)
            dcur = dpad[pl.ds(r0 + CONV_HALO, CONV_CHUNK), :]
            du0 = jnp.zeros((CONV_CHUNK, LANES), F32)
            for k in range(CW):
                du0 = du0 + w_ref[k:k + 1, :] * _tap(dpad, r0, CW - 1 - k)
                acc[8 * k:8 * k + 8, :] += _colsum8(dcur * _tap(upad, r0, k))
            acc[8 * CW:8 * CW + 8, :] += _colsum8(dcur)
            av = a_ref[pl.ds(r0, CONV_CHUNK), :]
            sg = _sigmoid(g_ref[pl.ds(r0, CONV_CHUNK), :])
            da_ref[pl.ds(r0, CONV_CHUNK), :] = (du0 * sg).astype(da_ref.dtype)
            dg_ref[pl.ds(r0, CONV_CHUNK), :] = (du0 * av * (sg * (1.0 - sg))).astype(dg_ref.dtype)
            return carry

        lax.fori_loop(0, s // CONV_CHUNK, step, 0)
        if rows_out > s:
            zeros = jnp.zeros((rows_out - s, LANES), da_ref.dtype)
            da_ref[s:rows_out, :] = zeros
            dg_ref[s:rows_out, :] = zeros
        for k in range(CW):
            dw_ref[k:k + 1, :] = jnp.sum(acc[8 * k:8 * k + 8, :], axis=0, keepdims=True)
        db_ref[...] = jnp.sum(acc[8 * CW:8 * CW + 8, :], axis=0, keepdims=True)

    col = lambda off: pl.BlockSpec((s, LANES), lambda c: (0, c + off))
    ocol = pl.BlockSpec((rows_out, LANES), lambda c: (0, c))
    return _pallas(
        body, name="conf_conv_bwd", grid=(nc,),
        in_specs=[col(0), col(nc), col(0), pl.BlockSpec((CW, LANES), lambda c: (0, c))],
        out_specs=[ocol, ocol, pl.BlockSpec((CW, LANES), lambda c: (0, c)), pl.BlockSpec((1, LANES), lambda c: (0, c))],
        out_shape=[_sds((rows_out, DC), MXU_DTYPE)] * 2 + [_sds((CW, DC), F32), _sds((1, DC), F32)],
        scratch_shapes=[pltpu.VMEM((s + 2 * CONV_HALO, LANES), F32)] * 2 + [pltpu.VMEM((8 * (CW + 1), LANES), F32)],
        semantics=("parallel",),
    )(ag, ag, du1, conv_w)


Q_TILE = 2 * GW
K_WIN = PAIR_ROWS * GW


def _bias_table(rpb_rev, after=()):
    def body(p_ref, *refs):
        t_ref = refs[-1]
        kcol = lax.broadcasted_iota(jnp.int32, (GW, LANES), 0)
        lane = lax.broadcasted_iota(jnp.int32, (GW, LANES), 1)
        qcol = lane % GW
        cs = jnp.clip(qcol - NA_ROWS, 0, GW - 2 * NA_ROWS)
        colvalid = (kcol >= cs) & (kcol < cs + 2 * NA_ROWS)
        neg = jnp.full((GW, LANES), NEG, F32)

        def skew(h, ro, shift):
            if ro < 0 or ro >= 2 * NA_ROWS - 1:
                return neg
            row = jnp.broadcast_to(p_ref[h * 16 + ro:h * 16 + ro + 1, :], (GW, LANES))
            return pltpu.roll(row, shift, 1, stride=1, stride_axis=0)

        for h in range(NH):
            for b in range(TAB_BLOCKS):
                val = jnp.where(lane < GW, skew(h, b - 1, GW + 1), skew(h, b - 2, 1))
                t_ref[h, b * GW:(b + 1) * GW, :] = jnp.where(colvalid, val, neg)

    return _pallas(body, name="attn_bias_table", out_shape=_sds((NH, TAB_BLOCKS * GW, LANES), F32),
                   in_specs=[pl.BlockSpec(memory_space=pltpu.VMEM)] + [pl.BlockSpec(memory_space=pl.ANY)] * len(_several(after)),
                   out_specs=pl.BlockSpec(memory_space=pltpu.VMEM))(rpb_rev, *_several(after))


def _rpb_grad(tt):
    def body(t_ref, o_ref):
        lane = lax.broadcasted_iota(jnp.int32, (GW, LANES), 1)
        si = lax.broadcasted_iota(jnp.int32, (GW, GW), 0)
        ti = lax.broadcasted_iota(jnp.int32, (GW, GW), 1)
        flip = jnp.where(si + ti == GW - 1, 1.0, 0.0).astype(F32)
        o_ref[...] = jnp.zeros_like(o_ref)
        for h in range(NH):
            for ro in range(2 * NA_ROWS - 1):
                lo = t_ref[h, (ro + 1) * GW:(ro + 2) * GW, :]
                hi = t_ref[h, (ro + 2) * GW:(ro + 3) * GW, :]
                g = jnp.where(lane < GW, lo + pltpu.roll(hi, GW, 1), 0.0)
                gf = jnp.dot(flip, g, preferred_element_type=F32, precision=lax.Precision.HIGHEST)
                sk = pltpu.roll(gf, 0, 1, stride=1, stride_axis=0)
                o_ref[h * 16 + ro:h * 16 + ro + 1, :] = jnp.sum(sk, axis=0, keepdims=True)

    return _pallas(body, name="attn_rpb_grad", out_shape=_sds((NH * 16, LANES), F32))(tt)


def _attn_geometry(i, rows):
    wsp = jnp.clip(2 * i - NA_ROWS // 2, 0, rows - PAIR_ROWS)
    k0 = pl.multiple_of(wsp * GW, GW)
    t0 = pl.multiple_of((wsp - 2 * i + NA_ROWS) * GW, GW)
    rr = lax.broadcasted_iota(jnp.int32, (GW, Q_TILE), 1) // GW
    wsr = jnp.clip(2 * i + rr - NA_ROWS // 2, 0, rows - NA_ROWS)
    edge_masks = tuple(jnp.where((kr >= wsr) & (kr < wsr + NA_ROWS), 0.0, NEG).astype(F32)
                       for kr in (wsp, wsp + PAIR_ROWS - 1))
    return k0, t0, edge_masks


def _biased(s_raw, bias, edge_masks):
    x = s_raw + bias
    return jnp.concatenate([x[:GW] + edge_masks[0], x[GW:K_WIN - GW], x[K_WIN - GW:] + edge_masks[1]], axis=0)


def _two_heads_on_lanes(xt):
    feat = lax.broadcasted_iota(jnp.int32, xt.shape, 0)
    zero = jnp.zeros_like(xt)
    return jnp.concatenate([jnp.where(feat < HD, xt, zero), jnp.where(feat >= HD, xt, zero)], axis=1)


def _two_heads_on_rows(x):
    lane = lax.broadcasted_iota(jnp.int32, x.shape, 1)
    zero = jnp.zeros_like(x)
    return jnp.concatenate([jnp.where(lane < HD, x, zero), jnp.where(lane >= HD, x, zero)], axis=0)


def _pick_heads(x2):
    n = x2.shape[0] // 2
    lane = lax.broadcasted_iota(jnp.int32, (n, LANES), 1)
    return jnp.where(lane < HD, x2[:n], x2[n:])


_TN = (((0,), (0,)), ((), ()))


def _attn_fwd(qkv, tab, s):
    rows = s // GW
    npair = rows // 2

    def body(q_ref, kv_ref, tab_ref, o_ref, lse_ref):
        i = pl.program_id(0)
        k0, t0, edge_masks = _attn_geometry(i, rows)
        for p in range(NH // 2):
            cq = slice(p * LANES, (p + 1) * LANES)
            ck = slice(DA + p * LANES, DA + (p + 1) * LANES)
            cv = slice(2 * DA + p * LANES, 2 * DA + (p + 1) * LANES)
            qm2 = _two_heads_on_lanes(q_ref[:, cq].T) * SCALE
            s_loc = jnp.dot(kv_ref[pl.ds(k0, K_WIN), ck], qm2, preferred_element_type=F32)
            s_ctx = jnp.dot(kv_ref[pl.ds(s, CTX), ck], qm2, preferred_element_type=F32)
            p_loc, p_ctx = [], []
            for hh in range(2):
                h = 2 * p + hh
                ch = slice(hh * Q_TILE, (hh + 1) * Q_TILE)
                sl = _biased(s_loc[:, ch], tab_ref[h, pl.ds(t0, K_WIN), :], edge_masks)
                sc = s_ctx[:, ch]
                m = jnp.maximum(jnp.max(sl, axis=0, keepdims=True), jnp.max(sc, axis=0, keepdims=True))
                el = jnp.exp(sl - m)
                ec = jnp.exp(sc - m)
                l = jnp.sum(el, axis=0, keepdims=True) + jnp.sum(ec, axis=0, keepdims=True)
                inv = 1.0 / l
                lse_ref[h:h + 1, :] = m + jnp.log(l)
                p_loc.append((el * inv).astype(MXU_DTYPE))
                p_ctx.append((ec * inv).astype(MXU_DTYPE))
            o2 = (lax.dot_general(jnp.concatenate(p_loc, axis=1), kv_ref[pl.ds(k0, K_WIN), cv], _TN, preferred_element_type=F32)
                  + lax.dot_general(jnp.concatenate(p_ctx, axis=1), kv_ref[pl.ds(s, CTX), cv], _TN, preferred_element_type=F32))
            o_ref[:, cq] = _pick_heads(o2).astype(o_ref.dtype)

    return _pallas(
        body, name="attn_fwd", grid=(npair,),
        in_specs=[pl.BlockSpec((Q_TILE, DA), lambda i: (i, 0)), pl.BlockSpec(memory_space=pltpu.VMEM),
                  pl.BlockSpec(memory_space=pltpu.VMEM)],
        out_specs=[pl.BlockSpec((Q_TILE, DA), lambda i: (i, 0)), pl.BlockSpec((NH, Q_TILE), lambda i: (0, i))],
        out_shape=[_sds((s, D), MXU_DTYPE), _sds((NH, s), F32)],
        semantics=("arbitrary",),
    )(qkv, qkv, tab)


def _attn_bwd(qkv, tab, lse, dycat, s):
    rows = s // GW
    npair = rows // 2
    sa = s + CTX
    nzero = CTX // Q_TILE

    def body(q_ref, do_ref, lse_ref, kv_ref, tab_ref, dq_ref, dkv_ref, tt_ref, dk_acc, dv_acc):
        i = pl.program_id(0)

        @pl.when(i == 0)
        def _():
            dk_acc[...] = jnp.zeros_like(dk_acc)
            dv_acc[...] = jnp.zeros_like(dv_acc)
            tt_ref[...] = jnp.zeros_like(tt_ref)

        @pl.when(i >= npair)
        def _():
            dq_ref[...] = jnp.zeros_like(dq_ref)

        @pl.when(i < npair)
        def _():
            k0, t0, edge_masks = _attn_geometry(i, rows)
            for p in range(NH // 2):
                cq = slice(p * LANES, (p + 1) * LANES)
                ck = slice(DA + p * LANES, DA + (p + 1) * LANES)
                cv = slice(2 * DA + p * LANES, 2 * DA + (p + 1) * LANES)
                qp = q_ref[:, cq] * SCALE
                dop = do_ref[:, cq].astype(MXU_DTYPE)
                qm2 = _two_heads_on_lanes(qp.T)
                dom2 = _two_heads_on_lanes(dop.T)
                kw = kv_ref[pl.ds(k0, K_WIN), ck]
                kc = kv_ref[pl.ds(s, CTX), ck]
                vw = kv_ref[pl.ds(k0, K_WIN), cv]
                vc = kv_ref[pl.ds(s, CTX), cv]
                s_loc = jnp.dot(kw, qm2, preferred_element_type=F32)
                s_ctx = jnp.dot(kc, qm2, preferred_element_type=F32)
                dp_loc = jnp.dot(vw, dom2, preferred_element_type=F32)
                dp_ctx = jnp.dot(vc, dom2, preferred_element_type=F32)
                p_loc, p_ctx, ds_loc, ds_ctx = [], [], [], []
                for hh in range(2):
                    h = 2 * p + hh
                    ch = slice(hh * Q_TILE, (hh + 1) * Q_TILE)
                    lse_h = lse_ref[h:h + 1, :]
                    pl_ = jnp.exp(_biased(s_loc[:, ch], tab_ref[h, pl.ds(t0, K_WIN), :], edge_masks) - lse_h)
                    pc_ = jnp.exp(s_ctx[:, ch] - lse_h)
                    dpl = dp_loc[:, ch]
                    dpc = dp_ctx[:, ch]
                    delta = jnp.sum(pl_ * dpl, axis=0, keepdims=True) + jnp.sum(pc_ * dpc, axis=0, keepdims=True)
                    dsl = pl_ * (dpl - delta)
                    dsc = pc_ * (dpc - delta)
                    tt_ref[h, pl.ds(t0, K_WIN), :] += dsl
                    p_loc.append(pl_.astype(MXU_DTYPE))
                    p_ctx.append(pc_.astype(MXU_DTYPE))
                    ds_loc.append(dsl.astype(MXU_DTYPE))
                    ds_ctx.append(dsc.astype(MXU_DTYPE))
                p_loc, p_ctx = jnp.concatenate(p_loc, axis=1), jnp.concatenate(p_ctx, axis=1)
                ds_loc, ds_ctx = jnp.concatenate(ds_loc, axis=1), jnp.concatenate(ds_ctx, axis=1)
                do_rows = _two_heads_on_rows(dop)
                q_rows = _two_heads_on_rows(qp)
                dv_acc[pl.ds(k0, K_WIN), cq] += jnp.dot(p_loc, do_rows, preferred_element_type=F32)
                dv_acc[pl.ds(s, CTX), cq] += jnp.dot(p_ctx, do_rows, preferred_element_type=F32)
                dk_acc[pl.ds(k0, K_WIN), cq] += jnp.dot(ds_loc, q_rows, preferred_element_type=F32)
                dk_acc[pl.ds(s, CTX), cq] += jnp.dot(ds_ctx, q_rows, preferred_element_type=F32)
                dq2 = (lax.dot_general(ds_loc, kw, _TN, preferred_element_type=F32)
                       + lax.dot_general(ds_ctx, kc, _TN, preferred_element_type=F32))
                dq_ref[:, cq] = (_pick_heads(dq2) * SCALE).astype(dq_ref.dtype)

        @pl.when(i == npair - 1)
        def _():
            def cp(c, carry):
                r0 = pl.multiple_of(c * ROW_TILE, ROW_TILE)
                dkv_ref[pl.ds(r0, ROW_TILE), 0:DA] = dk_acc[pl.ds(r0, ROW_TILE), :].astype(dkv_ref.dtype)
                dkv_ref[pl.ds(r0, ROW_TILE), DA:2 * DA] = dv_acc[pl.ds(r0, ROW_TILE), :].astype(dkv_ref.dtype)
                return carry

            lax.fori_loop(0, sa // ROW_TILE, cp, 0)

    qmap = lambda i: (jnp.minimum(i, npair - 1), 0)
    return _pallas(
        body, name="attn_bwd", grid=(npair + nzero,),
        in_specs=[pl.BlockSpec((Q_TILE, DA), qmap), pl.BlockSpec((Q_TILE, DA), qmap),
                  pl.BlockSpec((NH, Q_TILE), lambda i: (0, jnp.minimum(i, npair - 1))),
                  pl.BlockSpec(memory_space=pltpu.VMEM), pl.BlockSpec(memory_space=pltpu.VMEM)],
        out_specs=[pl.BlockSpec((Q_TILE, DA), lambda i: (i, 0)), pl.BlockSpec(memory_space=pltpu.VMEM),
                   pl.BlockSpec(memory_space=pltpu.VMEM)],
        out_shape=[_sds((sa, DA), MXU_DTYPE), _sds((sa, 2 * DA), MXU_DTYPE), _sds((NH, TAB_BLOCKS * GW, LANES), F32)],
        scratch_shapes=[pltpu.VMEM((sa, DA), F32)] * 2,
        semantics=("arbitrary",),
    )(qkv, dycat, lse, qkv, tab)


def _tile(n, prefs):
    for t in prefs:
        if n % t == 0:
            return t
    raise ValueError((n, prefs))


def _local_step(x, ctx, tgt, mod, mod_c, vec, w_in, late_weights, tab, early_grads=None):
    s = x.shape[0]
    sa = s + CTX
    ts = _tile(s, (1024, 512, 256))
    ts2 = _tile(s, (2048, 1024, 512, 256))
    tsa = _tile(sa, (1088, 640, 256))
    tsa2 = _tile(sa, (2176, 640, 256))
    sh1, sc1, gt1, sh2, sc2, gt2 = (mod[i:i + 1] for i in range(6))
    csh1, csc1 = mod_c[0:1], mod_c[1:2]
    act = MXU_DTYPE

    h_all = _rmsmod_fwd(x, ctx, vec["g_norm1"], sc1, sh1, csc1, csh1)
    w_in = w_in(h_all) if callable(w_in) else w_in
    qkv = _mm(h_all, w_in, mode="nn", m=sa, n=3 * DA, k=D, tm=tsa2, tn=512, tk=D, out_dtype=MXU_DTYPE, name="mm_qkv")
    ag = _mm(h_all, w_in, mode="nn", m=s, n=2 * DC, k=D, tm=ts2, tn=512, tk=D, out_dtype=F32, name="mm_ag", b_off=(0, 3))
    ycat, lse = _attn_fwd(qkv, tab, s)
    u1 = _conf_conv_fwd(ag, vec["conv_w"], vec["conv_b"])
    ycat = _conf_ln_fwd(u1, vec["ln_g"], vec["ln_b"], ycat)
    if callable(late_weights):
        w_out, ffn_weights = late_weights(ycat)
    else:
        w_out, ffn_weights = late_weights[0], late_weights[1:]
    y = _mm(ycat, w_out, mode="nn", m=s, n=D, k=D, tm=ts2, tn=512, tk=D, out_dtype=F32, name="mm_out")
    x1, h2 = _resid_rmsmod_fwd(x, y, gt1, vec["g_norm2"], sc2, sh2)
    w_up, w_down = ffn_weights(h2) if callable(ffn_weights) else ffn_weights
    u = _mm(h2, w_up, mode="nn", m=s, n=2 * DFF, k=D, tm=ts2, tn=512, tk=D, out_dtype=act, name="mm_up")
    f = _ffn_act_fwd(u, vec["ffn_conv_w"], vec["ffn_conv_b"])
    z = _mm(f, w_down, mode="nn", m=s, n=D, k=DFF, tm=ts, tn=D, tk=DFF, out_dtype=F32, name="mm_down")
    dx2, dz, loss, dgt2, dgf = _final_fwd_bwd(x1, z, gt2, vec["g_final"], tgt)

    df = _mm(dz, w_down, mode="nt", m=s, n=DFF, k=D, tm=ts, tn=DFF, tk=D, out_dtype=act, name="mm_down_dx")
    d_w_down = _mm(f, dz, mode="tn", m=DFF, n=D, k=s, tm=DFF // 2, tn=D, tk=ts2, out_dtype=F32, name="mm_down_dw")
    dug, duv, dfw_g, dfw_v, dfb_g, dfb_v = _ffn_act_bwd(u, df, vec["ffn_conv_w"], vec["ffn_conv_b"])
    dw_kw = dict(mode="tn", m=D, n=DFF, k=s, tm=D, tn=DFF, tk=ts, out_dtype=F32, out_total=(D, 2 * DFF))
    d_w_up = _mm(h2, dug, name="mm_up_dw_gate", **dw_kw)
    d_w_up = _mm(h2, duv, name="mm_up_dw_val", o_off=(0, 1), into=d_w_up, **dw_kw)
    if early_grads is not None:
        early_grads[0](d_w_up, d_w_down)
    dh2 = _mm([dug, duv], w_up, mode="nt", m=s, n=D, k=2 * DFF, tm=ts, tn=D, tk=2 * DFF, out_dtype=F32, name="mm_up_dx")
    sc2_b = sc2 if early_grads is None else sc2 + early_grads[1](dh2)
    dsh2, dsc2, dg2, dx1, dy, dgt1 = _rmsmod_bwd(x1, dh2, vec["g_norm2"], sc2_b, name="rmsmod2_bwd", add=dx2, resid=(gt1, y))
    dycat = _mm(dy, w_out, mode="nt", m=s, n=D, k=D, tm=ts2, tn=512, tk=D, out_dtype=F32, name="mm_out_dx")
    d_w_out = _mm(ycat, dy, mode="tn", m=D, n=D, k=s, tm=D, tn=D, tk=ts, out_dtype=F32, name="mm_out_dw")
    du1, dln_g, dln_b = _conf_ln_bwd(dycat, u1, vec["ln_g"], vec["ln_b"])
    da, dg, dconv_w, dconv_b = _conf_conv_bwd(ag, du1, vec["conv_w"], sa)
    dq, dkv, tt = _attn_bwd(qkv, tab, lse, dycat, s)
    drpb_rev = _rpb_grad(tt)
    d_pieces = [dq, dkv, da, dg]
    dh = _mm(d_pieces, w_in, mode="nt", m=sa, n=D, k=NIN, tm=tsa, tn=D, tk=NIN, out_dtype=F32, name="mm_in_dx")
    d_w_in = _mm(h_all, d_pieces, mode="tn", m=D, n=NIN, k=sa, tm=D, tn=NIN, tk=tsa, out_dtype=F32, name="mm_in_dw")
    dsh1, dsc1, dg1, grad_x = _rmsmod_bwd(x, dh, vec["g_norm1"], sc1, name="rmsmod1_bwd", add=dx1)
    dcsh1, dcsc1, dg1c = _rmsmod_bwd(ctx, dh, vec["g_norm1"], csc1, name="rmsmod1_ctx_bwd", dh_row0=s)

    small = dict(
        dmod=[dsh1, dsc1, dgt1, dsh2, dsc2, dgt2], dmod_c=[dcsh1, dcsc1],
        g_norm1=[dg1, dg1c], g_norm2=dg2, g_final=dgf, conv_b=dconv_b, ln_g=dln_g, ln_b=dln_b, conv_w=dconv_w,
        ffn_conv_w=[dfw_g, dfw_v], ffn_conv_b=[dfb_g, dfb_v], rpb_rev=drpb_rev,
    )
    return loss, grad_x, d_w_in, d_w_out, d_w_up, d_w_down, small


N_CHIPS = 4
HBM = pl.BlockSpec(memory_space=pl.ANY)
BIG = {"w_in": ("col", (D, NIN)), "w_out": ("row", (D, D)), "w_up": ("col", (D, 2 * DFF)), "w_down": ("row", (DFF, D))}
BIG_NAMES = tuple(BIG)
LATE_NAMES = ("w_out", "w_up", "w_down")


def _shard_shape(name):
    kind, (r, c) = BIG[name]
    return (r, c // N_CHIPS) if kind == "col" else (r // N_CHIPS, c)


def _half_rows(name):
    return _shard_shape(name)[0] // 2


def _place():
    x, y, c = lax.axis_index("x"), lax.axis_index("y"), lax.axis_index("c")
    others = [(1 - x, y), (x, 1 - y), (1 - x, 1 - y)]
    return x, y, c, 2 * x + y, (x, y, 1 - c), others


def _whole_region(ref, name, chip, half):
    kind, _ = BIG[name]
    r, c = _shard_shape(name)
    if kind == "col":
        return ref.at[pl.ds(half * (r // 2), r // 2), pl.ds(chip * c, c)]
    return ref.at[pl.ds(chip * r + half * (r // 2), r // 2), :]


def _remote(src, dst, send_sem, recv_sem, to):
    return pltpu.make_async_remote_copy(src_ref=src, dst_ref=dst, send_sem=send_sem, recv_sem=recv_sem,
                                        device_id=to, device_id_type=MESH)


def _cast_into_whole(name, shard, chip):
    kind, whole = BIG[name]
    r, c = shard.shape
    if kind == "col":
        tr = 256
        o_spec = pl.BlockSpec((tr, c), lambda i, ch: (i, ch[0]))
    else:
        tr = _tile(r, (128, 352))
        o_spec = pl.BlockSpec((tr, c), lambda i, ch: (ch[0] * (r // tr) + i, 0))

    def body(ch_ref, x_ref, o_ref):
        del ch_ref
        o_ref[...] = x_ref[...].astype(o_ref.dtype)

    return _pallas(body, name="cast_" + name, prefetch=1, grid=(r // tr,),
                   in_specs=[pl.BlockSpec((tr, c), lambda i, ch: (i, 0))], out_specs=o_spec,
                   out_shape=_sds(whole, MXU_DTYPE), semantics=("parallel",))(chip, shard)


SEM = pl.BlockSpec(memory_space=pltpu.SEMAPHORE)
IN_HBM = pl.BlockSpec(memory_space=pltpu.HBM)
DATAFLOW = pltpu.SideEffectType.DATAFLOW_SIDE_EFFECTING


def _keep_in_hbm(a):
    return pltpu.with_memory_space_constraint(a, pltpu.HBM)


def _several(after):
    return list(after) if isinstance(after, (list, tuple)) else [after]


FLIPS = [(dx, dy, dc) for dx in (0, 1) for dy in (0, 1) for dc in (0, 1)][1:]
OTHER_CHIPS = [f for f in FLIPS if f[2] == 0]


def _flipped(flip):
    x, y, c = lax.axis_index("x"), lax.axis_index("y"), lax.axis_index("c")
    return tuple(1 - v if f else v for v, f in zip((x, y, c), flip))


def _eight_copies(o_ref, val):
    r = val.shape[0]
    for d in range(8):
        o_ref[d * r:(d + 1) * r, :] = val


def _share_start(block, tag, after, flips=FLIPS):
    v, land = block
    r, n = v.shape
    assert land.shape == (8 * r, n)
    ns = 2 * len(flips)

    def body(*refs):
        v_ref, land_ref = refs[0], refs[1]
        sems = refs[2 + len(_several(after)):2 + len(_several(after)) + ns]
        x, y, c = lax.axis_index("x"), lax.axis_index("y"), lax.axis_index("c")
        mine = land_ref.at[pl.ds((4 * x + 2 * y + c) * r, r), :]
        for k, flip in enumerate(flips):
            _remote(v_ref, mine, sems[2 * k], sems[2 * k + 1], _flipped(flip)).start()

    res = pl.pallas_call(
        body, name="share_" + tag + "_start",
        out_shape=(*[pltpu.SemaphoreType.DMA(())] * ns, pltpu.HBM(v.shape, v.dtype), pltpu.HBM((8 * r, n), v.dtype)),
        in_specs=[IN_HBM] * 2 + [pl.BlockSpec(memory_space=pl.ANY)] * len(_several(after)),
        out_specs=(*[SEM] * ns, IN_HBM, IN_HBM),
        input_output_aliases={0: ns, 1: ns + 1},
        compiler_params=pltpu.CompilerParams(has_side_effects=DATAFLOW),
    )(_keep_in_hbm(v), _keep_in_hbm(land), *_several(after))
    return list(res[:ns]), res[ns], res[ns + 1], flips


def _share_wait(started, after, tag):
    sems, v, land, flips = started
    r = v.shape[0]
    ns = len(sems)

    def body(*refs):
        v_ref, land_ref = refs[0], refs[1]
        sem_refs = refs[2:2 + ns]
        for k, flip in enumerate(flips):
            px, py, pc = _flipped(flip)
            theirs = land_ref.at[pl.ds((4 * px + 2 * py + pc) * r, r), :]
            cp = _remote(v_ref, theirs, sem_refs[2 * k], sem_refs[2 * k + 1], (px, py, pc))
            cp.wait_send()
            cp.wait_recv()

    res = pl.pallas_call(
        body, name="share_" + tag + "_wait",
        out_shape=(pltpu.HBM(v.shape, v.dtype), pltpu.HBM(land.shape, land.dtype)),
        in_specs=[IN_HBM] * 2 + [SEM] * ns + [pl.BlockSpec(memory_space=pl.ANY)] * len(_several(after)),
        out_specs=(IN_HBM, IN_HBM),
        input_output_aliases={0: 0, 1: 1},
        compiler_params=pltpu.CompilerParams(has_side_effects=DATAFLOW),
    )(v, land, *sems, *_several(after))
    return res[1]


def _gather_start(wholes, names, after, tag):
    nw = len(names)
    ns = 2 * 3 * nw

    def body(*refs):
        ins = refs[:nw]
        sems = refs[nw + 1:nw + 1 + ns]
        token = refs[2 * nw + ns + 1]
        _, _, c, chip, _, others = _place()
        for w, name in enumerate(names):
            mine = _whole_region(ins[w], name, chip, c)
            for t, (ox, oy) in enumerate(others):
                k = 2 * (3 * w + t)
                _remote(mine, mine, sems[k], sems[k + 1], (ox, oy, c)).start()
        token[...] = jnp.zeros_like(token)

    res = pl.pallas_call(
        body, name="gather_" + tag + "_start",
        out_shape=(*[pltpu.SemaphoreType.DMA(())] * ns, *[pltpu.HBM(a.shape, a.dtype) for a in wholes], _sds((8, LANES), F32)),
        in_specs=[IN_HBM] * nw + [pl.BlockSpec(memory_space=pl.ANY)],
        out_specs=(*[SEM] * ns, *[IN_HBM] * nw, pl.BlockSpec(memory_space=pltpu.VMEM)),
        input_output_aliases={i: ns + i for i in range(nw)},
        compiler_params=pltpu.CompilerParams(has_side_effects=DATAFLOW),
    )(*[_keep_in_hbm(a) for a in wholes], after)
    return list(res[:ns]), list(res[ns:ns + nw]), res[ns + nw]


def _gather_wait(sems, wholes, names, after, tag):
    nw = len(names)
    ns = len(sems)

    def body(*refs):
        ins = refs[:nw]
        sem_refs = refs[nw:nw + ns]
        _, _, c, chip, _, others = _place()
        for w, name in enumerate(names):
            mine = _whole_region(ins[w], name, chip, c)
            for t, (ox, oy) in enumerate(others):
                got = _whole_region(ins[w], name, 2 * ox + oy, c)
                k = 2 * (3 * w + t)
                cp = _remote(mine, got, sem_refs[k], sem_refs[k + 1], (ox, oy, c))
                cp.wait_send()
                cp.wait_recv()

    return pl.pallas_call(
        body, name="gather_" + tag + "_wait",
        out_shape=tuple(pltpu.HBM(a.shape, a.dtype) for a in wholes),
        in_specs=[IN_HBM] * nw + [SEM] * ns + [pl.BlockSpec(memory_space=pl.ANY)], out_specs=tuple([IN_HBM] * nw),
        input_output_aliases={i: i for i in range(nw)},
        compiler_params=pltpu.CompilerParams(has_side_effects=DATAFLOW),
    )(*wholes, *sems, after)


def _forward_halves(wholes, names, tag):
    nw = len(names)

    def body(*refs):
        outs = refs[nw:2 * nw]
        send_sems, recv_sems = refs[2 * nw:]
        _, _, c, _, sibling, others = _place()
        sends = []
        for w, name in enumerate(names):
            for t, (ox, oy) in enumerate(others):
                got = _whole_region(outs[w], name, 2 * ox + oy, c)
                cp = _remote(got, got, send_sems.at[w, t], recv_sems.at[w, t], sibling)
                cp.start()
                sends.append(cp)
        for w, name in enumerate(names):
            for t, (ox, oy) in enumerate(others):
                got = _whole_region(outs[w], name, 2 * ox + oy, 1 - c)
                _remote(got, got, send_sems.at[w, t], recv_sems.at[w, t], sibling).wait_recv()
        for cp in sends:
            cp.wait_send()

    return pl.pallas_call(
        body, name="gather_" + tag + "_forward",
        out_shape=[_sds(a.shape, a.dtype) for a in wholes],
        in_specs=[HBM] * nw, out_specs=[HBM] * nw,
        input_output_aliases={i: i for i in range(nw)},
        scratch_shapes=[pltpu.SemaphoreType.DMA((nw, 3)), pltpu.SemaphoreType.DMA((nw, 3))],
    )(*wholes)


def _forward_start(wholes, names, tag, after):
    nw = len(names)
    ns = 2 * 3 * nw

    def body(*refs):
        ins = refs[:nw]
        sems = refs[nw + 1:nw + 1 + ns]
        token = refs[2 * nw + ns + 1]
        _, _, c, _, sibling, others = _place()
        for w, name in enumerate(names):
            for t, (ox, oy) in enumerate(others):
                got = _whole_region(ins[w], name, 2 * ox + oy, c)
                k = 2 * (3 * w + t)
                _remote(got, got, sems[k], sems[k + 1], sibling).start()
        token[...] = jnp.zeros_like(token)

    res = pl.pallas_call(
        body, name="gather_" + tag + "_forward_start",
        out_shape=(*[pltpu.SemaphoreType.DMA(())] * ns, *[pltpu.HBM(a.shape, a.dtype) for a in wholes], _sds((8, LANES), F32)),
        in_specs=[IN_HBM] * nw + [pl.BlockSpec(memory_space=pl.ANY)],
        out_specs=(*[SEM] * ns, *[IN_HBM] * nw, pl.BlockSpec(memory_space=pltpu.VMEM)),
        input_output_aliases={i: ns + i for i in range(nw)},
        compiler_params=pltpu.CompilerParams(has_side_effects=DATAFLOW),
    )(*[_keep_in_hbm(a) for a in wholes], after)
    return list(res[:ns]), list(res[ns:ns + nw]), res[ns + nw]


def _forward_wait(sems, wholes, names, after, tag):
    nw = len(names)
    ns = len(sems)

    def body(*refs):
        ins = refs[:nw]
        sem_refs = refs[nw:nw + ns]
        _, _, c, _, sibling, others = _place()
        for w, name in enumerate(names):
            for t, (ox, oy) in enumerate(others):
                k = 2 * (3 * w + t)
                cp = _remote(_whole_region(ins[w], name, 2 * ox + oy, c), _whole_region(ins[w], name, 2 * ox + oy, 1 - c),
                             sem_refs[k], sem_refs[k + 1], sibling)
                cp.wait_send()
                cp.wait_recv()

    return pl.pallas_call(
        body, name="gather_" + tag + "_forward_wait",
        out_shape=tuple(pltpu.HBM(a.shape, a.dtype) for a in wholes),
        in_specs=[IN_HBM] * nw + [SEM] * ns + [pl.BlockSpec(memory_space=pl.ANY)], out_specs=tuple([IN_HBM] * nw),
        input_output_aliases={i: i for i in range(nw)},
        compiler_params=pltpu.CompilerParams(has_side_effects=DATAFLOW),
    )(*wholes, *sems, after)


def _compact_shape(name, dtype):
    kind, (r, c) = BIG[name]
    return _sds((r // 2, c), dtype)


def _swap_pairs(ins, outs, names, c):
    pairs = []
    for w, name in enumerate(names):
        kind, _ = BIG[name]
        half = _half_rows(name)
        if kind == "col":
            pairs.append((ins[w].at[pl.ds((1 - c) * half, half), :], outs[w]))
        else:
            pairs += [(ins[w].at[pl.ds(jj * 2 * half + (1 - c) * half, half), :], outs[w].at[pl.ds(jj * half, half), :])
                      for jj in range(N_CHIPS)]
    return pairs


def _n_swap_copies(names):
    return sum(1 if BIG[n][0] == "col" else N_CHIPS for n in names)


def _swap_start(grads, names, label):
    nw = len(names)
    ns = 2 * _n_swap_copies(names)

    def body(*refs):
        ins, lands = refs[:nw], refs[nw:2 * nw]
        sems = refs[2 * nw:2 * nw + ns]
        token = refs[4 * nw + ns]
        _, _, c, _, sibling, _ = _place()
        for k, (src, dst) in enumerate(_swap_pairs(ins, lands, names, c)):
            _remote(src, dst, sems[2 * k], sems[2 * k + 1], sibling).start()
        token[...] = jnp.zeros_like(token)

    lands = [_keep_in_hbm(lax.empty(_compact_shape(n, F32).shape, F32)) for n in names]
    res = pl.pallas_call(
        body, name=label,
        out_shape=(*[pltpu.SemaphoreType.DMA(())] * ns, *[pltpu.HBM(a.shape, a.dtype) for a in grads],
                   *[pltpu.HBM(a.shape, a.dtype) for a in lands], _sds((8, LANES), F32)),
        in_specs=[IN_HBM] * (2 * nw),
        out_specs=(*[SEM] * ns, *[IN_HBM] * (2 * nw), pl.BlockSpec(memory_space=pltpu.VMEM)),
        input_output_aliases={i: ns + i for i in range(2 * nw)},
        compiler_params=pltpu.CompilerParams(has_side_effects=DATAFLOW),
    )(*[_keep_in_hbm(a) for a in grads], *lands)
    return list(res[:ns]), list(res[ns:ns + nw]), list(res[ns + nw:ns + 2 * nw]), res[ns + 2 * nw]


def _swap_wait(sems, grads, lands, names, after, label):
    nw = len(names)
    ns = len(sems)

    def body(*refs):
        ins, land_refs = refs[:nw], refs[nw:2 * nw]
        sem_refs = refs[2 * nw:2 * nw + ns]
        _, _, c, _, sibling, _ = _place()
        for k, (src, dst) in enumerate(_swap_pairs(ins, land_refs, names, c)):
            cp = _remote(src, dst, sem_refs[2 * k], sem_refs[2 * k + 1], sibling)
            cp.wait_send()
            cp.wait_recv()

    res = pl.pallas_call(
        body, name=label,
        out_shape=tuple(pltpu.HBM(a.shape, a.dtype) for a in (*grads, *lands)),
        in_specs=[IN_HBM] * (2 * nw) + [SEM] * ns + [pl.BlockSpec(memory_space=pl.ANY)] * len(_several(after)),
        out_specs=tuple([IN_HBM] * (2 * nw)),
        input_output_aliases={i: i for i in range(2 * nw)},
        compiler_params=pltpu.CompilerParams(has_side_effects=DATAFLOW),
    )(*grads, *lands, *sems, *_several(after))
    return list(res[:nw]), list(res[nw:])


def _add_halves(name, grad, got, core):
    kind, (r, c) = BIG[name]
    half = _half_rows(name)
    if kind == "col":
        t = 128
        grid = (half // t,)
        g_spec = pl.BlockSpec((t, c), lambda i, cr: (cr[0] * (half // t) + i, 0))
        o_spec = pl.BlockSpec((t, c), lambda i, cr: (i, 0))
    else:
        t = half
        grid = (N_CHIPS,)
        g_spec = pl.BlockSpec((t, c), lambda i, cr: (2 * i + cr[0], 0))
        o_spec = pl.BlockSpec((t, c), lambda i, cr: (i, 0))

    def body(c_ref, g_ref, b_ref, o_ref):
        del c_ref
        o_ref[...] = (g_ref[...] + b_ref[...]).astype(o_ref.dtype)

    return pl.pallas_call(
        body, name="grad_add_" + name,
        grid_spec=pltpu.PrefetchScalarGridSpec(num_scalar_prefetch=1, grid=grid, in_specs=[g_spec, o_spec], out_specs=o_spec),
        out_shape=_compact_shape(name, BF16),
        compiler_params=pltpu.CompilerParams(dimension_semantics=("parallel",), vmem_limit_bytes=VMEM_LIMIT),
    )(core, grad, got)


def _piece(ref, name, chip):
    kind, _ = BIG[name]
    r, c = _shard_shape(name)
    if kind == "col":
        return ref.at[:, pl.ds(chip * c, c)]
    return ref.at[pl.ds(chip * (r // 2), r // 2), :]


def _landing_shape(name):
    r, c = _shard_shape(name)
    return (N_CHIPS - 1, r // 2, c)


def _exchange_start(parts, names, label):
    nw = len(names)
    ns = 2 * 3 * nw

    def body(*refs):
        ins, lands = refs[:nw], refs[nw:2 * nw]
        sems = refs[2 * nw:2 * nw + ns]
        token = refs[4 * nw + ns]
        _, _, c, _, _, others = _place()
        for w, name in enumerate(names):
            for t, (ox, oy) in enumerate(others):
                k = 2 * (3 * w + t)
                _remote(_piece(ins[w], name, 2 * ox + oy), lands[w].at[t], sems[k], sems[k + 1], (ox, oy, c)).start()
        token[...] = jnp.zeros_like(token)

    lands = [_keep_in_hbm(lax.empty(_landing_shape(n), BF16)) for n in names]
    res = pl.pallas_call(
        body, name=label,
        out_shape=(*[pltpu.SemaphoreType.DMA(())] * ns, *[pltpu.HBM(a.shape, a.dtype) for a in parts],
                   *[pltpu.HBM(a.shape, a.dtype) for a in lands], _sds((8, LANES), F32)),
        in_specs=[IN_HBM] * (2 * nw),
        out_specs=(*[SEM] * ns, *[IN_HBM] * (2 * nw), pl.BlockSpec(memory_space=pltpu.VMEM)),
        input_output_aliases={i: ns + i for i in range(2 * nw)},
        compiler_params=pltpu.CompilerParams(has_side_effects=DATAFLOW),
    )(*[_keep_in_hbm(a) for a in parts], *lands)
    return list(res[:ns]), list(res[ns:ns + nw]), list(res[ns + nw:ns + 2 * nw]), res[ns + 2 * nw]


def _exchange_wait(sems, parts, lands, names, after, label):
    nw = len(names)
    ns = len(sems)

    def body(*refs):
        ins, land_refs = refs[:nw], refs[nw:2 * nw]
        sem_refs = refs[2 * nw:2 * nw + ns]
        _, _, c, _, _, others = _place()
        for w, name in enumerate(names):
            for t, (ox, oy) in enumerate(others):
                k = 2 * (3 * w + t)
                cp = _remote(_piece(ins[w], name, 2 * ox + oy), land_refs[w].at[t], sem_refs[k], sem_refs[k + 1], (ox, oy, c))
                cp.wait_send()
                cp.wait_recv()

    res = pl.pallas_call(
        body, name=label,
        out_shape=tuple(pltpu.HBM(a.shape, a.dtype) for a in (*parts, *lands)),
        in_specs=[IN_HBM] * (2 * nw) + [SEM] * ns + [pl.BlockSpec(memory_space=pl.ANY)] * len(_several(after)),
        out_specs=tuple([IN_HBM] * (2 * nw)),
        input_output_aliases={i: i for i in range(2 * nw)},
        compiler_params=pltpu.CompilerParams(has_side_effects=DATAFLOW),
    )(*parts, *lands, *sems, *_several(after))
    return list(res[:nw]), list(res[nw:])


def _sum_chips(name, part, got, chip):
    kind, _ = BIG[name]
    _, r, c = got.shape
    t = _tile(r, (128, 352))
    if kind == "col":
        own = pl.BlockSpec((t, c), lambda i, ch: (i, ch[0]))
    else:
        own = pl.BlockSpec((t, c), lambda i, ch: (ch[0] * (r // t) + i, 0))

    def body(ch_ref, p_ref, g_ref, o_ref):
        del ch_ref
        acc = p_ref[...].astype(F32)
        for j in range(N_CHIPS - 1):
            acc = acc + g_ref[j].astype(F32)
        o_ref[...] = acc

    return _pallas(
        body, name="grad_sum_" + name, prefetch=1, grid=(r // t,),
        in_specs=[own, pl.BlockSpec((N_CHIPS - 1, t, c), lambda i, ch: (0, i, 0))],
        out_specs=pl.BlockSpec((t, c), lambda i, ch: (i, 0)),
        out_shape=_sds((r, c), F32), semantics=("parallel",),
    )(chip, part, got)


def _send_halves(sums, label, after):
    nw = len(sums)

    def body(*refs):
        ins, outs = refs[:nw], refs[nw + 1:2 * nw + 1]
        send_sems, recv_sems = refs[2 * nw + 1:]
        _, _, _, _, sibling, _ = _place()
        copies = [_remote(ins[w], outs[w], send_sems.at[w], recv_sems.at[w], sibling) for w in range(nw)]
        for cp in copies:
            cp.start()
        for cp in copies:
            cp.wait()

    return pl.pallas_call(
        body, name=label,
        out_shape=[_sds(a.shape, a.dtype) for a in sums],
        in_specs=[HBM] * (nw + 1), out_specs=[HBM] * nw,
        scratch_shapes=[pltpu.SemaphoreType.DMA((nw,)), pltpu.SemaphoreType.DMA((nw,))],
    )(*sums, after)


EARLY_GRADS = ("w_up", "w_down")
LAST_GRADS = ("w_in", "w_out")


def _reduce_finish(started, names, after, chip, tag):
    sems, parts, lands, _ = started
    parts, lands = _exchange_wait(sems, parts, lands, names, after, "grad_exchange_wait_" + tag)
    return [_sum_chips(n, parts[i], lands[i], chip) for i, n in enumerate(names)]


HI = lax.Precision.HIGHEST
MOD_COLS = 6 * D // N_CHIPS
COND_ROWS = 16


def _silu(v):
    return v * _sigmoid(v)


GATHER_ROWS = 8
FFW_COLS = 2 * DFF // N_CHIPS
CONV_COLS = DC // N_CHIPS
TAPS_PER_ROW = FFW_COLS // CONV_COLS
assert 4 + -(-CW // TAPS_PER_ROW) <= GATHER_ROWS


def _conv_tap_place(k):
    return 4 + k // TAPS_PER_ROW, (k % TAPS_PER_ROW) * CONV_COLS


def _taps_first(a):
    return jnp.transpose(a, (1, 0, 2))


def _pack_cond(c, ffn_w, conv_w):
    def body(c_ref, f_ref, w_ref, o_ref, o8_ref):
        o_ref[...] = jnp.zeros_like(o_ref)
        o_ref[0:1, 0:D] = c_ref[...]
        for k in range(3):
            o_ref[1 + k:2 + k, :] = f_ref[k]
        for k in range(CW):
            row, lane = _conv_tap_place(k)
            o_ref[row:row + 1, lane:lane + CONV_COLS] = w_ref[k]
        _eight_copies(o8_ref, o_ref[...])

    return _pallas(body, name="pack_cond", out_shape=[_sds((GATHER_ROWS, FFW_COLS), F32),
                                                      _sds((8 * GATHER_ROWS, FFW_COLS), F32)])(c, ffn_w, conv_w)


def _unpack_cond(got, c_ctx):
    def body(g_ref, c_ref, cond_ref, f_ref, w_ref):
        cond_ref[...] = jnp.zeros_like(cond_ref)
        for d in range(8):
            cond_ref[d:d + 1, :] = g_ref[d * GATHER_ROWS:d * GATHER_ROWS + 1, 0:D]
        cond_ref[8:9, :] = c_ref[...]
        for j in range(N_CHIPS):
            r0 = 2 * j * GATHER_ROWS
            f_ref[:, j * FFW_COLS:(j + 1) * FFW_COLS] = g_ref[r0 + 1:r0 + 4, :]
            for k in range(CW):
                row, lane = _conv_tap_place(k)
                w_ref[k:k + 1, j * CONV_COLS:(j + 1) * CONV_COLS] = g_ref[r0 + row:r0 + row + 1, lane:lane + CONV_COLS]

    return _pallas(body, name="unpack_cond",
                   out_shape=[_sds((COND_ROWS, D), F32), _sds((3, 2 * DFF), F32), _sds((CW, DC), F32)])(got, c_ctx)


def _chip_cols(rows, width):
    return pl.BlockSpec((rows, width), lambda i, ch: (0, ch[0]))


def _whole(shape):
    return pl.BlockSpec(shape, lambda i, ch: (0,) * len(shape))


def _mod_shard(cond, w_mod, b_mod, chip):
    def body(ch_ref, c_ref, w_ref, b_ref, o_ref, o8_ref):
        del ch_ref
        o_ref[...] = jnp.dot(_silu(c_ref[...]), w_ref[...], preferred_element_type=F32, precision=HI) + b_ref[...]
        _eight_copies(o8_ref, o_ref[...])

    return _pallas(body, name="mod_fwd", prefetch=1, grid=(1,),
                   in_specs=[_whole((COND_ROWS, D)), _whole((D, MOD_COLS)), _chip_cols(1, MOD_COLS)],
                   out_specs=[_whole((COND_ROWS, MOD_COLS)), _whole((8 * COND_ROWS, MOD_COLS))],
                   out_shape=[_sds((COND_ROWS, MOD_COLS), F32), _sds((8 * COND_ROWS, MOD_COLS), F32)])(
                       chip, cond, w_mod, b_mod)


def _unpack_mod(mods, dev):
    def body(dev_ref, m_ref, me_ref, c_ref):
        rowi = lax.broadcasted_iota(jnp.int32, (COND_ROWS, MOD_COLS), 0)
        core = dev_ref[0] % 2
        mine, ctx = [], []
        for j in range(N_CHIPS):
            blk = m_ref[pl.ds(pl.multiple_of((2 * j + core) * COND_ROWS, COND_ROWS), COND_ROWS), :]
            mine.append(jnp.sum(jnp.where(rowi == dev_ref[0], blk, 0.0), axis=0, keepdims=True))
            ctx.append(blk[8:9, :])
        mine = jnp.concatenate(mine, axis=1)
        ctx = jnp.concatenate(ctx, axis=1)
        for k in range(6):
            me_ref[k:k + 1, :] = mine[:, k * D:(k + 1) * D]
        for k in range(2):
            c_ref[k:k + 1, :] = ctx[:, k * D:(k + 1) * D]

    return _pallas(body, name="unpack_mod", prefetch=1, grid=(1,),
                   in_specs=[_whole(mods.shape)], out_specs=[_whole((6, D)), _whole((2, D))],
                   out_shape=[_sds((6, D), F32), _sds((2, D), F32)])(dev, mods)


MOD_TILE = 512


def _mod_weight_update(cond, dmod_all, w, m, v, chip):
    nt = MOD_COLS // MOD_TILE

    def body(ch_ref, c_ref, d_ref, w_ref, m_ref, v_ref, g_ref, dl_ref, nm_ref, nv_ref):
        del ch_ref
        g = lax.dot_general(_silu(c_ref[...]), d_ref[...], _TN, preferred_element_type=F32, precision=HI)
        g_ref[...] = g
        dl_ref[...], nm_ref[...], nv_ref[...] = _adam_math(w_ref[...], g, m_ref[...], v_ref[...])

    blk = pl.BlockSpec((D, MOD_TILE), lambda j, ch: (0, j))
    return _pallas(body, name="mod_weight_update", prefetch=1, grid=(nt,),
                   in_specs=[_whole((COND_ROWS, D)), pl.BlockSpec((COND_ROWS, MOD_TILE), lambda j, ch: (0, ch[0] * nt + j)),
                             blk, blk, blk],
                   out_specs=[blk] * 4, out_shape=[_sds((D, MOD_COLS), F32)] * 4,
                   semantics=("parallel",))(chip, cond, dmod_all, w, m, v)


def _cond_grad_partial(dmod_all, w_mod, chip):
    def body(ch_ref, d_ref, w_ref, o_ref, o8_ref):
        del ch_ref
        o_ref[...] = lax.dot_general(d_ref[...], w_ref[...], (((1,), (1,)), ((), ())), preferred_element_type=F32, precision=HI)
        _eight_copies(o8_ref, o_ref[...])

    return _pallas(body, name="cond_grad_partial", prefetch=1, grid=(1,),
                   in_specs=[pl.BlockSpec((8, MOD_COLS), lambda i, ch: (1, ch[0])), _whole((D, MOD_COLS))],
                   out_specs=[_whole((8, D)), _whole((64, D))],
                   out_shape=[_sds((8, D), F32), _sds((64, D), F32)])(chip, dmod_all, w_mod)


def _adam_math(w, g, m, v):
    nm = ADAM_B1 * m + (1.0 - ADAM_B1) * g
    nv = ADAM_B2 * v + (1.0 - ADAM_B2) * (g * g)
    c1 = 1.0 - ADAM_B1 ** ADAM_STEP
    c2 = 1.0 - ADAM_B2 ** ADAM_STEP
    return -ADAM_LR * ((nm / c1) / (jnp.sqrt(nv / c2) + ADAM_EPS) + ADAM_WD * w), nm, nv


def _cond_update(parts, c_ctx, m, v):
    def body(p_ref, c_ref, m_ref, v_ref, g_ref, d_ref, nm_ref, nv_ref):
        tot = p_ref[0:1, :]
        for j in range(1, N_CHIPS):
            tot = tot + p_ref[16 * j:16 * j + 1, :]
        cv = c_ref[...]
        sg = _sigmoid(cv)
        g = tot * (sg * (1.0 + cv * (1.0 - sg)))
        g_ref[...] = g
        d_ref[...], nm_ref[...], nv_ref[...] = _adam_math(cv, g, m_ref[...], v_ref[...])

    return _pallas(body, name="cond_update", out_shape=[_sds((1, D), F32)] * 4)(parts, c_ctx, m, v)


def _adamw_cols(w, g_all, m, v, chip, name):
    _, r, c = w.shape

    def body(ch_ref, w_ref, g_ref, m_ref, v_ref, go_ref, d_ref, nm_ref, nv_ref):
        del ch_ref
        for k in range(r):
            g = g_ref[k:k + 1, :]
            go_ref[k] = g
            d_ref[k], nm_ref[k], nv_ref[k] = _adam_math(w_ref[k], g, m_ref[k], v_ref[k])

    res = _pallas(body, name=name, prefetch=1, grid=(1,),
                  in_specs=[_whole((r, 1, c)), _chip_cols(r, c), _whole((r, 1, c)), _whole((r, 1, c))],
                  out_specs=[_whole((r, 1, c))] * 4, out_shape=[_sds((r, 1, c), F32)] * 4)(
                      chip, _taps_first(w), g_all, _taps_first(m), _taps_first(v))
    return tuple(jnp.transpose(a, (1, 0, 2)) for a in res)


def _adamw_halves(name, w, own, other, m, v, core, after):
    r, c = w.shape
    half = r // 2
    t = _tile(half, (128, 352))
    nh = half // t

    def pick(mine):
        def index(i, cr):
            first = cr[0] if mine else 1 - cr[0]
            return (jnp.clip(i - first * nh, 0, nh - 1), 0)
        return pl.BlockSpec((t, c), index)

    def body(c_ref, w_ref, own_ref, oth_ref, m_ref, v_ref, after_ref, g_ref, d_ref, nm_ref, nv_ref):
        del after_ref
        g = jnp.where(pl.program_id(0) // nh == c_ref[0], own_ref[...], oth_ref[...])
        g_ref[...] = g
        d_ref[...], nm_ref[...], nv_ref[...] = _adam_math(w_ref[...], g, m_ref[...], v_ref[...])

    blk = pl.BlockSpec((t, c), lambda i, cr: (i, 0))
    return _pallas(body, name="adamw_" + name, prefetch=1, grid=(2 * nh,),
                   in_specs=[blk, pick(True), pick(False), blk, blk, pl.BlockSpec(memory_space=pl.ANY)], out_specs=[blk] * 4,
                   out_shape=[_sds((r, c), F32)] * 4, semantics=("parallel",))(core, w, own, other, m, v, after)


WEIGHTS = ("c_ctx", "w_mod", "b_mod", "g_norm1", "w_in", "rpb", "conv_w", "conv_b", "ln_g", "ln_b", "w_out", "g_norm2",
           "w_up", "ffn_conv_w", "ffn_conv_b", "w_down", "g_final")
PACK = (("dmod", 6 * D), ("dmod_c", 2 * D), ("g_norm1", D), ("g_norm1_ctx", D), ("g_norm2", D), ("g_final", D),
        ("conv_b", DC), ("ln_g", DC), ("ln_b", DC), ("ffn_conv_b", 2 * DFF), ("ffn_conv_w", 3 * 2 * DFF),
        ("conv_w", CW * DC), ("rpb_rev", NH * 16 * LANES), ("loss", LANES))
PACK_OFF = {}
_o = 0
for _n, _w in PACK:
    PACK_OFF[_n] = (_o, _w)
    _o += _w
PACK_N = -(-_o // (8 * LANES)) * (8 * LANES)
VECTORS = {"b_mod": (6 * D, ("dmod", "dmod_c")), "g_norm1": (D, ("g_norm1", "g_norm1_ctx")), "conv_b": (DC, ("conv_b",)),
           "ln_g": (DC, ("ln_g",)), "ln_b": (DC, ("ln_b",)), "g_norm2": (D, ("g_norm2",)),
           "ffn_conv_b": (2 * DFF, ("ffn_conv_b",)), "g_final": (D, ("g_final",))}
RPB_COLS = 4 * NA_ROWS - 1


PACK_ROW = PACK_N // 8
assert PACK_ROW % LANES == 0 and all(w_ % LANES == 0 for _, w_ in PACK)


def _pack_pieces(off, n):
    pieces, s = [], 0
    while s < n:
        row, col = divmod(off + s, PACK_ROW)
        take = min(n - s, PACK_ROW - col)
        pieces.append((row, col, s, take))
        s += take
    return pieces


def _pack_small(parts, after):
    arrs, places = [], []
    for name, _ in PACK:
        off, width = PACK_OFF[name]
        group = parts[name]
        rows = group[0].shape[0]
        row_w = sum(a.shape[1] for a in group)
        assert rows * row_w == width, (name, rows, row_w, width)
        col = 0
        for a in group:
            arrs.append(a)
            places.append([off + k * row_w + col for k in range(rows)])
            col += a.shape[1]

    def body(*refs):
        o_ref, o8_ref = refs[-2], refs[-1]
        for row, col, _, take in _pack_pieces(_o, PACK_N - _o):
            o_ref[row:row + 1, col:col + take] = jnp.zeros((1, take), F32)
        for ref, offs in zip(refs, places):
            for k, off in enumerate(offs):
                for row, col, s, take in _pack_pieces(off, ref.shape[1]):
                    o_ref[row:row + 1, col:col + take] = ref[k:k + 1, s:s + take]
        _eight_copies(o8_ref, o_ref[...])

    vmem = pl.BlockSpec(memory_space=pltpu.VMEM)
    return _pallas(body, name="pack_small_grads", out_shape=[_sds((8, PACK_ROW), F32), _sds((64, PACK_ROW), F32)],
                   in_specs=[vmem] * len(arrs) + [pl.BlockSpec(memory_space=pl.ANY)] * len(_several(after)),
                   out_specs=[vmem, vmem])(*arrs, *_several(after))


def _small_update(packs, w, m, v):
    names = list(VECTORS)

    def body(*refs):
        it = iter(refs)
        p_ref = next(it)
        wmv = {n: (next(it), next(it), next(it)) for n in names}
        outs = {n: (next(it), next(it), next(it), next(it)) for n in names}
        dmod_ref, cw_ref, fw_ref, rpb_ref, loss_ref = next(it), next(it), next(it), next(it), next(it)

        def segment(d, name):
            pieces = [p_ref[8 * d + row:8 * d + row + 1, col:col + take] for row, col, _, take in _pack_pieces(*PACK_OFF[name])]
            return pieces[0] if len(pieces) == 1 else jnp.concatenate(pieces, axis=1)

        def total(name):
            acc = segment(0, name)
            for d in range(1, 8):
                acc = acc + segment(d, name)
            return acc

        for n in names:
            width, segs = VECTORS[n]
            g = total(segs[0])
            if len(segs) > 1:
                extra = total(segs[1])
                ew = extra.shape[1]
                g = g + extra if ew == width else jnp.concatenate([g[:, :ew] + extra, g[:, ew:]], axis=1)
            w_ref, m_ref, v_ref = wmv[n]
            g_ref, d_ref, nm_ref, nv_ref = outs[n]
            g_ref[...] = g
            d_ref[...], nm_ref[...], nv_ref[...] = _adam_math(w_ref[...], g, m_ref[...], v_ref[...])

        dmod_ref[...] = jnp.zeros_like(dmod_ref)
        for d in range(8):
            dmod_ref[d:d + 1, :] = segment(d, "dmod")
        dmod_ref[8:9, 0:2 * D] = total("dmod_c")
        for ref, name, rows in ((cw_ref, "conv_w", CW), (fw_ref, "ffn_conv_w", 3), (rpb_ref, "rpb_rev", NH * 16)):
            flat = total(name)
            n = ref.shape[1]
            for k in range(rows):
                ref[k:k + 1, :] = flat[:, k * n:(k + 1) * n]
        loss_ref[...] = total("loss")

    ins = [packs] + [a[n] for n in names for a in (w, m, v)]
    out_shape = [_sds((1, VECTORS[n][0]), F32) for n in names for _ in range(4)]
    out_shape += [_sds((COND_ROWS, 6 * D), F32), _sds((CW, DC), F32), _sds((3, 2 * DFF), F32), _sds((NH * 16, LANES), F32),
                  _sds((1, LANES), F32)]
    res = _pallas(body, name="small_update", out_shape=out_shape)(*ins)
    per = {n: tuple(res[4 * i:4 * i + 4]) for i, n in enumerate(names)}
    return (per, *res[4 * len(names):])


def _rpb_update(rev, w, m, v):
    nr = 2 * NA_ROWS - 1
    heads_inside = lambda a: jnp.transpose(a, (0, 2, 1, 3))

    def body(r_ref, w_ref, m_ref, v_ref, g_ref, d_ref, nm_ref, nv_ref):
        li = lax.broadcasted_iota(jnp.int32, (LANES, LANES), 0)
        co = lax.broadcasted_iota(jnp.int32, (LANES, LANES), 1)
        lane_of_co0 = GW - 1 + RPB_COLS // 2
        unflip = jnp.where((li == lane_of_co0 - co) & (co < RPB_COLS), 1.0, 0.0).astype(F32)
        assert NH & (NH - 1) == 0
        regroup = jnp.where((co == (li & (NH - 1)) * 16 + (li >> (NH.bit_length() - 1))) & (li < nr * NH),
                            1.0, 0.0).astype(F32)
        g_all = jnp.dot(jnp.dot(regroup, r_ref[...], preferred_element_type=F32, precision=HI), unflip,
                        preferred_element_type=F32, precision=HI)
        for ro in range(nr):
            g = g_all[ro * NH:(ro + 1) * NH, 0:RPB_COLS]
            g_ref[0, ro] = g
            d_ref[0, ro], nm_ref[0, ro], nv_ref[0, ro] = _adam_math(w_ref[0, ro], g, m_ref[0, ro], v_ref[0, ro])

    res = _pallas(body, name="rpb_update", out_shape=[_sds((1, nr, NH, RPB_COLS), F32)] * 4)(
        rev, heads_inside(w), heads_inside(m), heads_inside(v))
    return tuple(heads_inside(a) for a in res)


def kernel(x, c, ctx, c_ctx, w_mod, b_mod, g_norm1, w_in, rpb, conv_w, conv_b, ln_g, ln_b, w_out, g_norm2, w_up, ffn_conv_w, ffn_conv_b, w_down, g_final, loss_target, m_c_ctx, m_w_mod, m_b_mod, m_g_norm1, m_w_in, m_rpb, m_conv_w, m_conv_b, m_ln_g, m_ln_b, m_w_out, m_g_norm2, m_w_up, m_ffn_conv_w, m_ffn_conv_b, m_w_down, m_g_final, v_c_ctx, v_w_mod, v_b_mod, v_g_norm1, v_w_in, v_rpb, v_conv_w, v_conv_b, v_ln_g, v_ln_b, v_w_out, v_g_norm2, v_w_up, v_ffn_conv_w, v_ffn_conv_b, v_w_down, v_g_final):
    w = dict(c_ctx=c_ctx, w_mod=w_mod, b_mod=b_mod, g_norm1=g_norm1, w_in=w_in, rpb=rpb, conv_w=conv_w, conv_b=conv_b,
             ln_g=ln_g, ln_b=ln_b, w_out=w_out, g_norm2=g_norm2, w_up=w_up, ffn_conv_w=ffn_conv_w, ffn_conv_b=ffn_conv_b,
             w_down=w_down, g_final=g_final)
    mom = dict(c_ctx=m_c_ctx, w_mod=m_w_mod, b_mod=m_b_mod, g_norm1=m_g_norm1, w_in=m_w_in, rpb=m_rpb, conv_w=m_conv_w,
               conv_b=m_conv_b, ln_g=m_ln_g, ln_b=m_ln_b, w_out=m_w_out, g_norm2=m_g_norm2, w_up=m_w_up,
               ffn_conv_w=m_ffn_conv_w, ffn_conv_b=m_ffn_conv_b, w_down=m_w_down, g_final=m_g_final)
    var = dict(c_ctx=v_c_ctx, w_mod=v_w_mod, b_mod=v_b_mod, g_norm1=v_g_norm1, w_in=v_w_in, rpb=v_rpb, conv_w=v_conv_w,
               conv_b=v_conv_b, ln_g=v_ln_g, ln_b=v_ln_b, w_out=v_w_out, g_norm2=v_g_norm2, w_up=v_w_up,
               ffn_conv_w=v_ffn_conv_w, ffn_conv_b=v_ffn_conv_b, w_down=v_w_down, g_final=v_g_final)
    xi, yi, ci = lax.axis_index("x"), lax.axis_index("y"), lax.axis_index("c")
    dev = (4 * xi + 2 * yi + ci).astype(jnp.int32).reshape(1)
    chip = (2 * xi + yi).astype(jnp.int32).reshape(1)
    core = ci.astype(jnp.int32).reshape(1)
    c_ctx2 = c_ctx.reshape(1, D)
    g_final2 = g_final.reshape(1, D)
    mom["g_final"], var["g_final"] = m_g_final.reshape(1, D), v_g_final.reshape(1, D)

    sharing_cond = _share_start(_pack_cond(c, _taps_first(ffn_conv_w), _taps_first(conv_w)), "cond", after=[])
    shards = {n: _cast_into_whole(n, w[n][0], chip) for n in BIG_NAMES}
    cond, ffn_w_all, conv_w_all = _unpack_cond(_share_wait(sharing_cond, list(shards.values()), "cond"), c_ctx2)

    sharing_mod = _share_start(_mod_shard(cond, w_mod[0], b_mod, chip), "mod", after=[], flips=OTHER_CHIPS)

    sems_in, first, token_in = _gather_start([shards["w_in"]], ("w_in",), sharing_mod[2], "w_in")
    rpb_rev = jnp.pad(rpb[0][:, :, ::-1], ((0, 0), (0, 1), (48, LANES - 48 - RPB_COLS))).reshape(NH * 16, LANES)
    tab = _bias_table(rpb_rev, after=token_in)
    mod_me, mod_c = _unpack_mod(_share_wait(sharing_mod, [tab], "mod"), dev)
    late_started = []

    def w_in_all(after):
        arrived = _gather_wait(sems_in, first, ("w_in",), after, "w_in")
        late_started.append(_gather_start([shards[n] for n in LATE_NAMES], LATE_NAMES, arrived[0], "late"))
        return _forward_halves(list(arrived), ("w_in",), "w_in")[0]

    def late_weights(after):
        sems, late, _ = late_started.pop()
        arrived = list(_gather_wait(sems, late, LATE_NAMES, after, "late"))
        (w_out_all,) = _forward_halves(arrived[:1], LATE_NAMES[:1], "w_out")
        fsems, passing, _ = _forward_start(arrived[1:], LATE_NAMES[1:], "ffn", after=w_out_all)
        return w_out_all, lambda after2: _forward_wait(fsems, passing, LATE_NAMES[1:], after2, "ffn")

    vec = dict(g_norm1=g_norm1, g_norm2=g_norm2, g_final=g_final2, conv_w=conv_w_all, conv_b=conv_b, ln_g=ln_g, ln_b=ln_b,
               ffn_conv_w=ffn_w_all, ffn_conv_b=ffn_conv_b)
    started = []

    def begin_early(d_up, d_down):
        started.append(_swap_start([d_up, d_down], EARLY_GRADS, "grad_swap_start_early"))

    def carry_on_early(after):
        sems_, grads_, lands_, _ = started.pop()
        grads_, lands_ = _swap_wait(sems_, grads_, lands_, EARLY_GRADS, after, "grad_swap_wait_early")
        parts_ = [_add_halves(n, grads_[i], lands_[i], core) for i, n in enumerate(EARLY_GRADS)]
        started.append(_exchange_start(parts_, EARLY_GRADS, "grad_exchange_start_early"))
        return started[0][3][0:1, 0:1]

    loss_p, grad_x, d_in, d_out, d_up, d_down, small = _local_step(
        x[0], ctx[0], loss_target[0], mod_me, mod_c, vec, w_in_all, late_weights, tab, (begin_early, carry_on_early))

    out = {}
    sems_, grads_, lands_, _ = _swap_start([d_in, d_out], LAST_GRADS, "grad_swap_start_last")
    early_own = _reduce_finish(started[0], EARLY_GRADS, grad_x, chip, "early")
    parts = dict(dmod=small["dmod"], dmod_c=small["dmod_c"], g_norm1=[small["g_norm1"][0]], g_norm1_ctx=[small["g_norm1"][1]],
                 g_norm2=[small["g_norm2"]], g_final=[small["g_final"]], conv_b=[small["conv_b"]], ln_g=[small["ln_g"]],
                 ln_b=[small["ln_b"]], ffn_conv_b=small["ffn_conv_b"], ffn_conv_w=small["ffn_conv_w"],
                 conv_w=[small["conv_w"]], rpb_rev=[small["rpb_rev"]], loss=[loss_p])
    sharing = _share_start(_pack_small(parts, after=early_own), "small_grads", after=[])
    grads_, lands_ = _swap_wait(sems_, grads_, lands_, LAST_GRADS, sharing[2], "grad_swap_wait_last")
    parts_ = [_add_halves(n, grads_[i], lands_[i], core) for i, n in enumerate(LAST_GRADS)]
    last_started = _exchange_start(parts_, LAST_GRADS, "grad_exchange_start_last")
    early_other = _send_halves(early_own, "grad_send_early", after=last_started[3])
    for i, n in enumerate(EARLY_GRADS):
        out[n] = _adamw_halves(n, w[n][0], early_own[i], early_other[i], mom[n][0], var[n][0], core, early_other[i])

    packs = _share_wait(sharing, [out[n][1] for n in EARLY_GRADS], "small_grads")
    w2 = dict(w, g_final=g_final2)
    per, dmod_all, g_conv_w_all, g_ffn_w_all, g_rpb_rev, loss_row = _small_update(packs, w2, mom, var)
    out.update(per)
    out["w_mod"] = _mod_weight_update(cond, dmod_all, w_mod[0], m_w_mod[0], v_w_mod[0], chip)

    sharing_c = _share_start(_cond_grad_partial(dmod_all, w_mod[0], chip), "cond_grad", after=out["w_mod"][1])
    last_own = _reduce_finish(last_started, LAST_GRADS, sharing_c[2], chip, "last")
    last_other = _send_halves(last_own, "grad_send_last", after=last_own[0])
    for i, n in enumerate(LAST_GRADS):
        out[n] = _adamw_halves(n, w[n][0], last_own[i], last_other[i], mom[n][0], var[n][0], core, last_other[i])
    out["c_ctx"] = _cond_update(_share_wait(sharing_c, [out[n][1] for n in LAST_GRADS], "cond_grad"),
                                c_ctx2, m_c_ctx.reshape(1, D), v_c_ctx.reshape(1, D))
    out["conv_w"] = _adamw_cols(conv_w, g_conv_w_all, m_conv_w, v_conv_w, chip, "adamw_conv_w")
    out["ffn_conv_w"] = _adamw_cols(ffn_conv_w, g_ffn_w_all, m_ffn_conv_w, v_ffn_conv_w, chip, "adamw_ffn_conv_w")
    out["rpb"] = _rpb_update(g_rpb_rev, rpb, m_rpb, v_rpb)

    res = [[out[n][k].reshape(w[n].shape) for n in WEIGHTS] for k in range(4)]
    return (loss_row[0, 0], grad_x[None], *res[0], *res[1], *res[2], *res[3])
```
